```python
import math
import jax, jax.numpy as jnp
from jax import lax
import numpy as np

D_MODEL = 1024
BATCH = 8
SEQ = 2048
DEPTH = 1

CONV_WIDTH = D_MODEL // 2
CONV_K = 3
HEAD_DIM = 64
N_HEADS = (D_MODEL - CONV_WIDTH) // HEAD_DIM
N_KV_HEADS = 2
GQA_GROUP = N_HEADS // N_KV_HEADS
ATTN_WIDTH = N_HEADS * HEAD_DIM
KV_WIDTH = N_KV_HEADS * HEAD_DIM
WINDOW = 128
BLK = 128
NUM_BUCKETS = 32
MAX_DISTANCE = 128
MAX_EXACT = NUM_BUCKETS // 2
D_FF = 2816
FFN_K = 3
EPS = 1e-6
NEG_INF = -1e30
IN_WIDTH = 3 * CONV_WIDTH + ATTN_WIDTH + 2 * KV_WIDTH

kernel_name = "hybrid_shortconv_swa_sink_convffn"


def rms_norm(x, g):
    xf = x.astype(jnp.float32)
    y = xf * lax.rsqrt(jnp.mean(xf * xf, axis=-1, keepdims=True) + EPS)
    return (y * g.astype(jnp.float32)).astype(x.dtype)


def causal_dwconv(x, w):
    K = w.shape[0]
    S = x.shape[1]
    xp = jnp.pad(x, ((0, 0), (K - 1, 0), (0, 0)))
    y = xp[:, 0:S] * w[0]
    for k in range(1, K):
        y = y + xp[:, k:k + S] * w[k]
    return y


def band_offsets():
    q = jnp.arange(BLK, dtype=jnp.int32)[:, None]
    j = jnp.arange(2 * BLK, dtype=jnp.int32)[None, :]
    return q + BLK - j


def t5_band_bias(rel_table, d):
    n = jnp.maximum(d, 0)
    nf = jnp.maximum(n, 1).astype(jnp.float32)
    large = MAX_EXACT + (jnp.log(nf / MAX_EXACT) / math.log(MAX_DISTANCE / MAX_EXACT)
                         * (NUM_BUCKETS - MAX_EXACT)).astype(jnp.int32)
    large = jnp.minimum(large, NUM_BUCKETS - 1)
    bucket = jnp.where(n < MAX_EXACT, n, large)
    bias = rel_table[bucket].astype(jnp.float32)
    return bias.transpose(2, 0, 1).reshape(N_KV_HEADS, GQA_GROUP, BLK, 2 * BLK)


def band_blocks(t):
    Bn, S = t.shape[0], t.shape[1]
    nb = S // BLK
    tb = t.reshape(Bn, nb, BLK, N_KV_HEADS, HEAD_DIM)
    prev = jnp.pad(tb, ((0, 0), (1, 0), (0, 0), (0, 0), (0, 0)))[:, :-1]
    return jnp.concatenate([prev, tb], axis=2)


def sliding_window_attention(q, k, v, sinks, bias, d):
    Bn, S = q.shape[0], q.shape[1]
    nb = S // BLK
    qb = q.reshape(Bn, nb, BLK, N_KV_HEADS, GQA_GROUP, HEAD_DIM)
    kband = band_blocks(k)
    vband = band_blocks(v)
    scale = HEAD_DIM ** -0.5
    logits = jnp.einsum('bnqhgd,bnkhd->bnhgqk', qb, kband).astype(jnp.float32) * scale + bias
    within = (d >= 0) & (d < WINDOW)
    key_pos = (jnp.arange(nb, dtype=jnp.int32)[:, None, None] - 1) * BLK \
        + jnp.arange(2 * BLK, dtype=jnp.int32)[None, None, :]
    valid = within[None] & (key_pos >= 0)
    logits = jnp.where(valid[None, :, None, None], logits, NEG_INF)
    sink = sinks.astype(jnp.float32).reshape(1, 1, N_KV_HEADS, GQA_GROUP, 1, 1)
    m = jnp.maximum(jnp.max(logits, axis=-1, keepdims=True), sink)
    p = jnp.exp(logits - m)
    denom = jnp.sum(p, axis=-1, keepdims=True) + jnp.exp(sink - m)
    probs = (p / denom).astype(v.dtype)
    out = jnp.einsum('bnhgqk,bnkhd->bnqhgd', probs, vband)
    return out.reshape(Bn, S, ATTN_WIDTH)


def _fwd_setup_inputs(seed: int = 0) -> dict:
    key = jax.random.key(seed)
    ks = jax.random.split(key, 17)
    f32 = jnp.float32

    def nrm(k, shape, scale):
        return jax.random.normal(k, shape, f32) * scale

    def gain(k, shape):
        return 1.0 + 0.02 * jax.random.normal(k, shape, f32)

    return {
        "x": jax.random.normal(ks[0], (BATCH, SEQ, D_MODEL), f32),
        "norm_mix_g": gain(ks[1], (DEPTH, D_MODEL)),
        "w_in": nrm(ks[2], (DEPTH, D_MODEL, IN_WIDTH), D_MODEL ** -0.5),
        "conv_w": nrm(ks[3], (DEPTH, CONV_K, CONV_WIDTH), CONV_K ** -0.5),
        "q_norm_g": gain(ks[4], (DEPTH, HEAD_DIM)),
        "k_norm_g": gain(ks[5], (DEPTH, HEAD_DIM)),
        "rel_bias_table": nrm(ks[6], (NUM_BUCKETS, N_HEADS), 0.5),
        "sinks": nrm(ks[7], (DEPTH, N_HEADS), 1.0),
        "out_norm_conv_g": gain(ks[8], (DEPTH, CONV_WIDTH)),
        "out_norm_attn_g": gain(ks[9], (DEPTH, ATTN_WIDTH)),
        "w_out": nrm(ks[10], (DEPTH, CONV_WIDTH + ATTN_WIDTH, D_MODEL), (CONV_WIDTH + ATTN_WIDTH) ** -0.5),
        "norm_ffn_g": gain(ks[11], (DEPTH, D_MODEL)),
        "w_up": nrm(ks[12], (DEPTH, D_MODEL, 2 * D_FF), D_MODEL ** -0.5),
        "ffn_conv_w": nrm(ks[13], (DEPTH, FFN_K, 2 * D_FF), FFN_K ** -0.5),
        "ffn_conv_b": nrm(ks[14], (DEPTH, 2 * D_FF), 0.02),
        "w_down": nrm(ks[15], (DEPTH, D_FF, D_MODEL), D_FF ** -0.5),
    }


def _fwd_reference(x, norm_mix_g, w_in, conv_w, q_norm_g, k_norm_g, rel_bias_table, sinks,
              out_norm_conv_g, out_norm_attn_g, w_out, norm_ffn_g, w_up, ffn_conv_w,
              ffn_conv_b, w_down):
    Bn, S, _ = x.shape
    d = band_offsets()
    bias = t5_band_bias(rel_bias_table, d)
    h = x
    for l in range(DEPTH):
        u = rms_norm(h, norm_mix_g[l])
        proj = jnp.einsum('bsd,de->bse', u, w_in[l])
        c0 = 0
        gate_b = proj[..., c0:c0 + CONV_WIDTH]; c0 += CONV_WIDTH
        gate_c = proj[..., c0:c0 + CONV_WIDTH]; c0 += CONV_WIDTH
        hc = proj[..., c0:c0 + CONV_WIDTH]; c0 += CONV_WIDTH
        q = proj[..., c0:c0 + ATTN_WIDTH]; c0 += ATTN_WIDTH
        k = proj[..., c0:c0 + KV_WIDTH]; c0 += KV_WIDTH
        v = proj[..., c0:c0 + KV_WIDTH]

        y_conv = gate_b * causal_dwconv(gate_c * hc, conv_w[l])

        q = rms_norm(q.reshape(Bn, S, N_HEADS, HEAD_DIM), q_norm_g[l])
        k = rms_norm(k.reshape(Bn, S, N_KV_HEADS, HEAD_DIM), k_norm_g[l])
        v = v.reshape(Bn, S, N_KV_HEADS, HEAD_DIM)
        y_attn = sliding_window_attention(q, k, v, sinks[l], bias, d)

        y = jnp.concatenate([rms_norm(y_conv, out_norm_conv_g[l]),
                             rms_norm(y_attn, out_norm_attn_g[l])], axis=-1)
        h = h + jnp.einsum('bse,ed->bsd', y, w_out[l])

        u = rms_norm(h, norm_ffn_g[l])
        up = jnp.einsum('bsd,df->bsf', u, w_up[l])
        up = causal_dwconv(up, ffn_conv_w[l]) + ffn_conv_b[l]
        g, val = up[..., :D_FF], up[..., D_FF:]
        h = h + jnp.einsum('bsf,fd->bsd', jax.nn.silu(g) * val, w_down[l])
    return h


import jax as _jax
import jax.numpy as _jnp

TWIN_FORMAT = 'train_step'
FWD_PARAMS = ['x', 'norm_mix_g', 'w_in', 'conv_w', 'q_norm_g', 'k_norm_g', 'rel_bias_table', 'sinks', 'out_norm_conv_g', 'out_norm_attn_g', 'w_out', 'norm_ffn_g', 'w_up', 'ffn_conv_w', 'ffn_conv_b', 'w_down']
TWIN_WEIGHTS = ['norm_mix_g', 'w_in', 'conv_w', 'q_norm_g', 'k_norm_g', 'rel_bias_table', 'sinks', 'out_norm_conv_g', 'out_norm_attn_g', 'w_out', 'norm_ffn_g', 'w_up', 'ffn_conv_w', 'ffn_conv_b', 'w_down']
TWIN_DIFF_INPUT = 'x'
TWIN_INPUTS = ['x', 'norm_mix_g', 'w_in', 'conv_w', 'q_norm_g', 'k_norm_g', 'rel_bias_table', 'sinks', 'out_norm_conv_g', 'out_norm_attn_g', 'w_out', 'norm_ffn_g', 'w_up', 'ffn_conv_w', 'ffn_conv_b', 'w_down', 'loss_target', 'm_norm_mix_g', 'm_w_in', 'm_conv_w', 'm_q_norm_g', 'm_k_norm_g', 'm_rel_bias_table', 'm_sinks', 'm_out_norm_conv_g', 'm_out_norm_attn_g', 'm_w_out', 'm_norm_ffn_g', 'm_w_up', 'm_ffn_conv_w', 'm_ffn_conv_b', 'm_w_down', 'v_norm_mix_g', 'v_w_in', 'v_conv_w', 'v_q_norm_g', 'v_k_norm_g', 'v_rel_bias_table', 'v_sinks', 'v_out_norm_conv_g', 'v_out_norm_attn_g', 'v_w_out', 'v_norm_ffn_g', 'v_w_up', 'v_ffn_conv_w', 'v_ffn_conv_b', 'v_w_down']
TWIN_OUTPUTS = ['loss', 'grad_x', 'grad_norm_mix_g', 'grad_w_in', 'grad_conv_w', 'grad_q_norm_g', 'grad_k_norm_g', 'grad_rel_bias_table', 'grad_sinks', 'grad_out_norm_conv_g', 'grad_out_norm_attn_g', 'grad_w_out', 'grad_norm_ffn_g', 'grad_w_up', 'grad_ffn_conv_w', 'grad_ffn_conv_b', 'grad_w_down', 'delta_norm_mix_g', 'delta_w_in', 'delta_conv_w', 'delta_q_norm_g', 'delta_k_norm_g', 'delta_rel_bias_table', 'delta_sinks', 'delta_out_norm_conv_g', 'delta_out_norm_attn_g', 'delta_w_out', 'delta_norm_ffn_g', 'delta_w_up', 'delta_ffn_conv_w', 'delta_ffn_conv_b', 'delta_w_down', 'new_m_norm_mix_g', 'new_m_w_in', 'new_m_conv_w', 'new_m_q_norm_g', 'new_m_k_norm_g', 'new_m_rel_bias_table', 'new_m_sinks', 'new_m_out_norm_conv_g', 'new_m_out_norm_attn_g', 'new_m_w_out', 'new_m_norm_ffn_g', 'new_m_w_up', 'new_m_ffn_conv_w', 'new_m_ffn_conv_b', 'new_m_w_down', 'new_v_norm_mix_g', 'new_v_w_in', 'new_v_conv_w', 'new_v_q_norm_g', 'new_v_k_norm_g', 'new_v_rel_bias_table', 'new_v_sinks', 'new_v_out_norm_conv_g', 'new_v_out_norm_attn_g', 'new_v_w_out', 'new_v_norm_ffn_g', 'new_v_w_up', 'new_v_ffn_conv_w', 'new_v_ffn_conv_b', 'new_v_w_down']
TWIN_LEAF_KINDS = {'loss': 'loss', 'grad_x': 'grad_x', 'grad_norm_mix_g': 'grad_w', 'grad_w_in': 'grad_w', 'grad_conv_w': 'grad_w', 'grad_q_norm_g': 'grad_w', 'grad_k_norm_g': 'grad_w', 'grad_rel_bias_table': 'grad_w', 'grad_sinks': 'grad_w', 'grad_out_norm_conv_g': 'grad_w', 'grad_out_norm_attn_g': 'grad_w', 'grad_w_out': 'grad_w', 'grad_norm_ffn_g': 'grad_w', 'grad_w_up': 'grad_w', 'grad_ffn_conv_w': 'grad_w', 'grad_ffn_conv_b': 'grad_w', 'grad_w_down': 'grad_w', 'delta_norm_mix_g': 'delta_w', 'delta_w_in': 'delta_w', 'delta_conv_w': 'delta_w', 'delta_q_norm_g': 'delta_w', 'delta_k_norm_g': 'delta_w', 'delta_rel_bias_table': 'delta_w', 'delta_sinks': 'delta_w', 'delta_out_norm_conv_g': 'delta_w', 'delta_out_norm_attn_g': 'delta_w', 'delta_w_out': 'delta_w', 'delta_norm_ffn_g': 'delta_w', 'delta_w_up': 'delta_w', 'delta_ffn_conv_w': 'delta_w', 'delta_ffn_conv_b': 'delta_w', 'delta_w_down': 'delta_w', 'new_m_norm_mix_g': 'new_m', 'new_m_w_in': 'new_m', 'new_m_conv_w': 'new_m', 'new_m_q_norm_g': 'new_m', 'new_m_k_norm_g': 'new_m', 'new_m_rel_bias_table': 'new_m', 'new_m_sinks': 'new_m', 'new_m_out_norm_conv_g': 'new_m', 'new_m_out_norm_attn_g': 'new_m', 'new_m_w_out': 'new_m', 'new_m_norm_ffn_g': 'new_m', 'new_m_w_up': 'new_m', 'new_m_ffn_conv_w': 'new_m', 'new_m_ffn_conv_b': 'new_m', 'new_m_w_down': 'new_m', 'new_v_norm_mix_g': 'new_v', 'new_v_w_in': 'new_v', 'new_v_conv_w': 'new_v', 'new_v_q_norm_g': 'new_v', 'new_v_k_norm_g': 'new_v', 'new_v_rel_bias_table': 'new_v', 'new_v_sinks': 'new_v', 'new_v_out_norm_conv_g': 'new_v', 'new_v_out_norm_attn_g': 'new_v', 'new_v_w_out': 'new_v', 'new_v_norm_ffn_g': 'new_v', 'new_v_w_up': 'new_v', 'new_v_ffn_conv_w': 'new_v', 'new_v_ffn_conv_b': 'new_v', 'new_v_w_down': 'new_v'}


def _forward(args):
    return _fwd_reference(*[args[k] for k in FWD_PARAMS])


def _output_shape():
    out = _jax.eval_shape(lambda: _forward(_fwd_setup_inputs(0)))
    return out.shape, out.dtype

N_MICROBATCH = 1
ADAM_LR = 0.001
ADAM_B1 = 0.9
ADAM_B2 = 0.999
ADAM_EPS = 1e-08
ADAM_WD = 0.01
ADAM_STEP = 10
PER_EXAMPLE_BATCH_AXIS = {'x': 0, 'loss_target': 0}
SHARED_INPUTS = []
_WEIGHT_DTYPES = {'norm_mix_g': _jnp.float32, 'w_in': _jnp.float32, 'conv_w': _jnp.float32, 'q_norm_g': _jnp.float32, 'k_norm_g': _jnp.float32, 'rel_bias_table': _jnp.float32, 'sinks': _jnp.float32, 'out_norm_conv_g': _jnp.float32, 'out_norm_attn_g': _jnp.float32, 'w_out': _jnp.float32, 'norm_ffn_g': _jnp.float32, 'w_up': _jnp.float32, 'ffn_conv_w': _jnp.float32, 'ffn_conv_b': _jnp.float32, 'w_down': _jnp.float32}
MOMENT_SCALE = {'norm_mix_g': 6.228202e-01, 'w_in': 4.165603e-01, 'conv_w': 5.126106e-01, 'q_norm_g': 8.002899e-01, 'k_norm_g': 8.157190e-01, 'rel_bias_table': 5.013274e-01, 'sinks': 8.685081e-02, 'out_norm_conv_g': 2.040471e+01, 'out_norm_attn_g': 1.588724e+01, 'w_out': 6.255424e-01, 'norm_ffn_g': 1.351268e+01, 'w_up': 1.923542e-01, 'ffn_conv_w': 1.859803e+00, 'ffn_conv_b': 1.610288e+00, 'w_down': 2.304258e-01}


def _to_microbatches(a, axis):
    t = _jnp.moveaxis(a, axis, 0)
    t = t.reshape((N_MICROBATCH, t.shape[0] // N_MICROBATCH) + t.shape[1:])
    return _jnp.moveaxis(t, 1, axis + 1)


def setup_inputs(seed: int = 0) -> dict:
    inp = _fwd_setup_inputs(seed)
    key = _jax.random.fold_in(_jax.random.key(seed), 7919)
    shape, _ = _output_shape()
    out = dict(inp)
    out["loss_target"] = _jax.random.normal(_jax.random.fold_in(key, 0), shape, _jnp.float32)
    for i, name in enumerate(TWIN_WEIGHTS):
        w = inp[name].astype(_jnp.float32)
        if MOMENT_SCALE is None:
            s = _jnp.sqrt(_jnp.mean(_jnp.square(w)) + 1e-30)
        else:
            s = MOMENT_SCALE[name]
        km, kv = _jax.random.split(_jax.random.fold_in(key, i + 1))
        out[name] = w
        out["m_" + name] = s * _jax.random.normal(km, w.shape, _jnp.float32)
        out["v_" + name] = (s * s) * _jax.random.uniform(kv, w.shape, _jnp.float32, 0.5, 1.5)
    if N_MICROBATCH > 1:
        for name, axis in PER_EXAMPLE_BATCH_AXIS.items():
            out[name] = _to_microbatches(out[name], axis)
    return {'x': out['x'], 'norm_mix_g': out['norm_mix_g'], 'w_in': out['w_in'], 'conv_w': out['conv_w'], 'q_norm_g': out['q_norm_g'], 'k_norm_g': out['k_norm_g'], 'rel_bias_table': out['rel_bias_table'], 'sinks': out['sinks'], 'out_norm_conv_g': out['out_norm_conv_g'], 'out_norm_attn_g': out['out_norm_attn_g'], 'w_out': out['w_out'], 'norm_ffn_g': out['norm_ffn_g'], 'w_up': out['w_up'], 'ffn_conv_w': out['ffn_conv_w'], 'ffn_conv_b': out['ffn_conv_b'], 'w_down': out['w_down'], 'loss_target': out['loss_target'], 'm_norm_mix_g': out['m_norm_mix_g'], 'm_w_in': out['m_w_in'], 'm_conv_w': out['m_conv_w'], 'm_q_norm_g': out['m_q_norm_g'], 'm_k_norm_g': out['m_k_norm_g'], 'm_rel_bias_table': out['m_rel_bias_table'], 'm_sinks': out['m_sinks'], 'm_out_norm_conv_g': out['m_out_norm_conv_g'], 'm_out_norm_attn_g': out['m_out_norm_attn_g'], 'm_w_out': out['m_w_out'], 'm_norm_ffn_g': out['m_norm_ffn_g'], 'm_w_up': out['m_w_up'], 'm_ffn_conv_w': out['m_ffn_conv_w'], 'm_ffn_conv_b': out['m_ffn_conv_b'], 'm_w_down': out['m_w_down'], 'v_norm_mix_g': out['v_norm_mix_g'], 'v_w_in': out['v_w_in'], 'v_conv_w': out['v_conv_w'], 'v_q_norm_g': out['v_q_norm_g'], 'v_k_norm_g': out['v_k_norm_g'], 'v_rel_bias_table': out['v_rel_bias_table'], 'v_sinks': out['v_sinks'], 'v_out_norm_conv_g': out['v_out_norm_conv_g'], 'v_out_norm_attn_g': out['v_out_norm_attn_g'], 'v_w_out': out['v_w_out'], 'v_norm_ffn_g': out['v_norm_ffn_g'], 'v_w_up': out['v_w_up'], 'v_ffn_conv_w': out['v_ffn_conv_w'], 'v_ffn_conv_b': out['v_ffn_conv_b'], 'v_w_down': out['v_w_down']}


def _loss(weights, diff, rest, loss_target):
    with _jax.named_scope("forward"):
        args = {**rest, TWIN_DIFF_INPUT: diff, **{k: w.astype(_WEIGHT_DTYPES[k]) for k, w in weights.items()}}
        y = _forward(args)
    with _jax.named_scope("loss_head"):
        err = _jnp.square(y.astype(_jnp.float32) - loss_target)
        return 0.5 * _jnp.sum(_jnp.mean(err, axis=-1)) if err.ndim else 0.5 * err


def _adamw(w, g, m, v):
    m = ADAM_B1 * m + (1.0 - ADAM_B1) * g
    v = ADAM_B2 * v + (1.0 - ADAM_B2) * _jnp.square(g)
    m_hat = m / (1.0 - ADAM_B1 ** ADAM_STEP)
    v_hat = v / (1.0 - ADAM_B2 ** ADAM_STEP)
    delta = -ADAM_LR * (m_hat / (_jnp.sqrt(v_hat) + ADAM_EPS) + ADAM_WD * w)
    return delta, m, v


def reference(x, norm_mix_g, w_in, conv_w, q_norm_g, k_norm_g, rel_bias_table, sinks, out_norm_conv_g, out_norm_attn_g, w_out, norm_ffn_g, w_up, ffn_conv_w, ffn_conv_b, w_down, loss_target, m_norm_mix_g, m_w_in, m_conv_w, m_q_norm_g, m_k_norm_g, m_rel_bias_table, m_sinks, m_out_norm_conv_g, m_out_norm_attn_g, m_w_out, m_norm_ffn_g, m_w_up, m_ffn_conv_w, m_ffn_conv_b, m_w_down, v_norm_mix_g, v_w_in, v_conv_w, v_q_norm_g, v_k_norm_g, v_rel_bias_table, v_sinks, v_out_norm_conv_g, v_out_norm_attn_g, v_w_out, v_norm_ffn_g, v_w_up, v_ffn_conv_w, v_ffn_conv_b, v_w_down):
    given = dict(x=x, norm_mix_g=norm_mix_g, w_in=w_in, conv_w=conv_w, q_norm_g=q_norm_g, k_norm_g=k_norm_g, rel_bias_table=rel_bias_table, sinks=sinks, out_norm_conv_g=out_norm_conv_g, out_norm_attn_g=out_norm_attn_g, w_out=w_out, norm_ffn_g=norm_ffn_g, w_up=w_up, ffn_conv_w=ffn_conv_w, ffn_conv_b=ffn_conv_b, w_down=w_down, loss_target=loss_target, m_norm_mix_g=m_norm_mix_g, m_w_in=m_w_in, m_conv_w=m_conv_w, m_q_norm_g=m_q_norm_g, m_k_norm_g=m_k_norm_g, m_rel_bias_table=m_rel_bias_table, m_sinks=m_sinks, m_out_norm_conv_g=m_out_norm_conv_g, m_out_norm_attn_g=m_out_norm_attn_g, m_w_out=m_w_out, m_norm_ffn_g=m_norm_ffn_g, m_w_up=m_w_up, m_ffn_conv_w=m_ffn_conv_w, m_ffn_conv_b=m_ffn_conv_b, m_w_down=m_w_down, v_norm_mix_g=v_norm_mix_g, v_w_in=v_w_in, v_conv_w=v_conv_w, v_q_norm_g=v_q_norm_g, v_k_norm_g=v_k_norm_g, v_rel_bias_table=v_rel_bias_table, v_sinks=v_sinks, v_out_norm_conv_g=v_out_norm_conv_g, v_out_norm_attn_g=v_out_norm_attn_g, v_w_out=v_w_out, v_norm_ffn_g=v_norm_ffn_g, v_w_up=v_w_up, v_ffn_conv_w=v_ffn_conv_w, v_ffn_conv_b=v_ffn_conv_b, v_w_down=v_w_down)
    weights = {n: given[n] for n in TWIN_WEIGHTS}
    shared = {n: given[n] for n in SHARED_INPUTS}
    per_example = {n: given[n] for n in ['x']}
    grad_fn = _jax.value_and_grad(_loss, argnums=(0, 1))

    def one_microbatch(ex, loss_target):
        ex = dict(ex)
        diff = ex.pop(TWIN_DIFF_INPUT)
        return grad_fn(weights, diff, {**shared, **ex}, loss_target)

    if N_MICROBATCH == 1:
        loss, (grad_w, grad_x) = one_microbatch(per_example, given["loss_target"])
    else:
        def body(carry, xs):
            loss_sum, grad_sum = carry
            l_k, (gw_k, gx_k) = one_microbatch(xs[0], xs[1])
            with _jax.named_scope("update"):
                return (loss_sum + l_k, _jax.tree.map(_jnp.add, grad_sum, gw_k)), gx_k

        init = (_jnp.zeros((), _jnp.float32), _jax.tree.map(_jnp.zeros_like, weights))
        (loss, grad_w), grad_x = _jax.lax.scan(body, init, (per_example, given["loss_target"]))
    with _jax.named_scope("update"):
        delta_w, new_m, new_v = {}, {}, {}
        for n in TWIN_WEIGHTS:
            delta_w[n], new_m[n], new_v[n] = _adamw(weights[n], grad_w[n], given["m_" + n], given["v_" + n])
    return (loss, grad_x, *[grad_w[n] for n in TWIN_WEIGHTS], *[delta_w[n] for n in TWIN_WEIGHTS],
            *[new_m[n] for n in TWIN_WEIGHTS], *[new_v[n] for n in TWIN_WEIGHTS])
```

```python
import functools
import math

import numpy as np
import jax
import jax.numpy as jnp
from jax import lax
from jax.experimental import pallas as pl
from jax.experimental.pallas import tpu as pltpu

F32 = jnp.float32
BF16 = jnp.bfloat16

SEQ = 2048
D_MODEL = 1024
CONV_WIDTH = 512
ATTN_WIDTH = 512
KV_WIDTH = 128
HEAD_DIM = 64
N_HEADS = 8
GQA_GROUP = 4
IN_WIDTH = 2304
D_FF = 2816
BLK = 128
NUM_BUCKETS = 32
EPS = 1e-6
NEG_INF = -1e30
ADAM_LR = 0.001
ADAM_B1 = 0.9
ADAM_B2 = 0.999
ADAM_EPS = 1e-08
ADAM_WD = 0.01
ADAM_STEP = 10

N_DEV = 8
FFN_BLK = 2 * D_FF // N_DEV
N_FFN_BLK = D_FF // FFN_BLK
SUBLANES = 8
VMEM_LIMIT = 56 * 1024 * 1024
SMALL_ROWS, SMALL_COLS = 16, 1024

_MESH = pl.DeviceIdType.MESH
_ANY = pl.BlockSpec(memory_space=pl.ANY)


def _params(sem):
    return pltpu.CompilerParams(dimension_semantics=sem, vmem_limit_bytes=VMEM_LIMIT)


def _dot(a, b):
    return jnp.dot(a, b, preferred_element_type=F32)


def _dot_nt(a, b):
    return lax.dot_general(a, b, (((1,), (1,)), ((), ())), preferred_element_type=F32)


def _dot_tn(a, b):
    return lax.dot_general(a, b, (((0,), (0,)), ((), ())), preferred_element_type=F32)


def _shift_down(x, s, halo):
    r = pltpu.roll(x, s, axis=0)
    hr = pltpu.roll(halo, s, axis=0)
    row = lax.broadcasted_iota(jnp.int32, halo.shape, 0)
    top = jnp.where(row < s, hr, r[:SUBLANES])
    return jnp.concatenate([top, r[SUBLANES:]], axis=0)


def _shift_up(x, s, halo):
    n = x.shape[0]
    r = pltpu.roll(x, n - s, axis=0)
    hr = pltpu.roll(halo, SUBLANES - s, axis=0)
    row = lax.broadcasted_iota(jnp.int32, halo.shape, 0)
    bot = jnp.where(row >= SUBLANES - s, hr, r[n - SUBLANES:])
    return jnp.concatenate([r[:n - SUBLANES], bot], axis=0)


def _conv3(x, w, halo):
    x2 = _shift_down(x, 2, halo)
    x1 = _shift_down(x, 1, halo)
    return x2 * w[0:1] + x1 * w[1:2] + x * w[2:3], x2, x1


def _conv3_bwd_input(dy, w, halo_next):
    return dy * w[2:3] + _shift_up(dy, 1, halo_next) * w[1:2] + _shift_up(dy, 2, halo_next) * w[0:1]


def _rstd(x):
    return lax.rsqrt(jnp.mean(x * x, axis=-1, keepdims=True) + EPS)


def _rms_bwd(x, g, dy):
    r = _rstd(x)
    n = x * r
    dn = dy * g
    dx = r * (dn - n * jnp.mean(dn * n, axis=-1, keepdims=True))
    return dx, jnp.sum(dy * n, axis=0, keepdims=True)


def _head_norm(x, g, heads):
    parts = []
    for h in range(heads):
        xh = x[:, HEAD_DIM * h:HEAD_DIM * (h + 1)]
        parts.append(xh * _rstd(xh) * g)
    return jnp.concatenate(parts, axis=-1)


def _head_norm_bwd(x, g, dy, heads):
    dxs, dg = [], jnp.zeros((1, HEAD_DIM), F32)
    for h in range(heads):
        sl = slice(HEAD_DIM * h, HEAD_DIM * (h + 1))
        dxh, dgh = _rms_bwd(x[:, sl], g, dy[:, sl])
        dxs.append(dxh)
        dg = dg + dgh
    return jnp.concatenate(dxs, axis=-1), dg


def _bucket_map():
    q = np.arange(BLK)[:, None]
    j = np.arange(2 * BLK)[None, :]
    n = np.maximum(q + BLK - j, 0)
    nf = np.maximum(n, 1).astype(np.float32)
    max_exact = NUM_BUCKETS // 2
    large = max_exact + (np.log(nf / max_exact) / math.log(BLK / max_exact) * (NUM_BUCKETS - max_exact)).astype(np.int32)
    large = np.minimum(large, NUM_BUCKETS - 1)
    return np.where(n < max_exact, n, large).astype(np.int32)


def _coords():
    return lax.axis_index("x"), lax.axis_index("y"), lax.axis_index("c")


def _lin(px, py, pc):
    return 4 * px + 2 * py + pc


def _all_gather(arrs, name):
    n = len(arrs)

    def body(*refs):
        ins, outs = refs[:n], refs[n:2 * n]
        send_sems, recv_sems, loc_sems = refs[2 * n:]
        x, y, c = _coords()
        me = _lin(x, y, c)
        sib = (x, y, 1 - c)
        chips = [(1 - x, y), (x, 1 - y), (1 - x, 1 - y)]

        def copy(k, a, slot, to, src=None):
            dst = outs[a].at[slot]
            return pltpu.make_async_remote_copy(
                src_ref=dst if src is None else src, dst_ref=dst, send_sem=send_sems.at[k, a],
                recv_sem=recv_sems.at[k, a], device_id=to, device_id_type=_MESH)

        mine = [pltpu.make_async_copy(ins[a], outs[a].at[me], loc_sems.at[a]) for a in range(n)]
        for m in mine:
            m.start()
        first = []
        for a in range(n):
            first.append(copy(0, a, me, sib, src=ins[a]))
            for j, chip in enumerate(chips):
                first.append(copy(1 + j, a, me, (*chip, c), src=ins[a]))
        for cp in first:
            cp.start()
        passed = []
        for j, chip in enumerate(chips):
            slot = _lin(*chip, c)
            for a in range(n):
                copy(1 + j, a, slot, sib).wait_recv()
                p = copy(4 + j, a, slot, sib)
                p.start()
                passed.append(p)
        for a in range(n):
            copy(0, a, _lin(x, y, 1 - c), sib).wait_recv()
        for j, chip in enumerate(chips):
            for a in range(n):
                copy(4 + j, a, _lin(*chip, 1 - c), sib).wait_recv()
        for cp in first + passed:
            cp.wait_send()
        for m in mine:
            m.wait()

    return pl.pallas_call(
        body, name=name,
        out_shape=[jax.ShapeDtypeStruct((N_DEV,) + a.shape, a.dtype) for a in arrs],
        in_specs=[_ANY] * n, out_specs=[_ANY] * n,
        scratch_shapes=[pltpu.SemaphoreType.DMA((7, n)), pltpu.SemaphoreType.DMA((7, n)), pltpu.SemaphoreType.DMA((n,))],
    )(*arrs)


def _exchange_direct(arrs, name, scatter):
    n = len(arrs)

    def body(*refs):
        ins, outs = refs[:n], refs[n:2 * n]
        send_sems, recv_sems, loc_sems = refs[2 * n:]
        x, y, c = _coords()
        me = _lin(x, y, c)
        local, sends, recvs = [], [], []
        for a in range(n):
            local.append(pltpu.make_async_copy(ins[a].at[me] if scatter else ins[a], outs[a].at[me], loc_sems.at[a]))
            for r in range(1, N_DEV):
                px = 1 - x if r & 4 else x
                py = 1 - y if r & 2 else y
                pc = 1 - c if r & 1 else c
                peer = _lin(px, py, pc)
                src = ins[a].at[peer] if scatter else ins[a]
                sends.append(pltpu.make_async_remote_copy(
                    src_ref=src, dst_ref=outs[a].at[me], send_sem=send_sems.at[r - 1, a],
                    recv_sem=recv_sems.at[r - 1, a], device_id=(px, py, pc), device_id_type=_MESH))
                recvs.append(pltpu.make_async_remote_copy(
                    src_ref=src, dst_ref=outs[a].at[peer], send_sem=send_sems.at[r - 1, a],
                    recv_sem=recv_sems.at[r - 1, a], device_id=(px, py, pc), device_id_type=_MESH))
        for cp in local + sends:
            cp.start()
        for cp in recvs:
            cp.wait_recv()
        for cp in sends:
            cp.wait_send()
        for cp in local:
            cp.wait()

    shapes = [a.shape if scatter else (N_DEV,) + a.shape for a in arrs]
    return pl.pallas_call(
        body, name=name,
        out_shape=[jax.ShapeDtypeStruct(s, a.dtype) for s, a in zip(shapes, arrs)],
        in_specs=[_ANY] * n, out_specs=[_ANY] * n,
        scratch_shapes=[pltpu.SemaphoreType.DMA((7, n)), pltpu.SemaphoreType.DMA((7, n)), pltpu.SemaphoreType.DMA((n,))],
    )(*arrs)


def _cast_shards(w_in, w_out, w_up, w_down):
    def body(wi_ref, wo_ref, wu_ref, wd_ref, wi_o, wo_o, wu_o, wd_o):
        wi_o[...] = wi_ref[...].T.astype(BF16)
        wo_o[...] = wo_ref[...].astype(BF16)
        wu_o[...] = wu_ref[...].astype(BF16)
        wd_o[...] = wd_ref[...].astype(BF16)

    return pl.pallas_call(
        body, name="cast_shards",
        out_shape=[jax.ShapeDtypeStruct(w_in.shape[::-1], BF16), jax.ShapeDtypeStruct(w_out.shape, BF16),
                   jax.ShapeDtypeStruct(w_up.shape, BF16), jax.ShapeDtypeStruct(w_down.shape, BF16)],
        compiler_params=pltpu.CompilerParams(vmem_limit_bytes=VMEM_LIMIT),
    )(w_in, w_out, w_up, w_down)


def _mix_in_fwd(x, g1, w_in_t, conv_w, gq, gk, gconv):
    tm = 256
    n_t = SEQ // tm

    def body(x_ref, g1_ref, w_ref, cw_ref, gq_ref, gk_ref, gc_ref,
             proj_ref, ycn_ref, qn_ref, kn_ref, v_ref, halo_ref):
        @pl.when(pl.program_id(0) == 0)
        def _():
            halo_ref[...] = jnp.zeros_like(halo_ref)

        xv = x_ref[...]
        u = (xv * _rstd(xv) * g1_ref[...]).astype(BF16)
        proj = _dot_nt(u, w_ref[...])
        proj_ref[...] = proj
        gate_b = proj[:, 0:CONV_WIDTH]
        a = proj[:, CONV_WIDTH:2 * CONV_WIDTH] * proj[:, 2 * CONV_WIDTH:3 * CONV_WIDTH]
        cv, _, _ = _conv3(a, cw_ref[...], halo_ref[...])
        halo_ref[...] = a[tm - SUBLANES:]
        yc = gate_b * cv
        ycn_ref[...] = (yc * _rstd(yc) * gc_ref[...]).astype(BF16)
        q0 = 3 * CONV_WIDTH
        qn_ref[...] = _head_norm(proj[:, q0:q0 + ATTN_WIDTH], gq_ref[...], N_HEADS).astype(BF16)
        k0 = q0 + ATTN_WIDTH
        kn_ref[...] = _head_norm(proj[:, k0:k0 + KV_WIDTH], gk_ref[...], 2).astype(BF16)
        v_ref[...] = proj[:, k0 + KV_WIDTH:k0 + 2 * KV_WIDTH].astype(BF16)

    const = lambda shape: pl.BlockSpec(shape, lambda i: (0,) * len(shape))
    rows = lambda w: pl.BlockSpec((tm, w), lambda i: (i, 0))
    return pl.pallas_call(
        body, name="mix_in_fwd", grid=(n_t,),
        in_specs=[rows(D_MODEL), const((1, D_MODEL)), const((IN_WIDTH, D_MODEL)), const((3, CONV_WIDTH)),
                  const((1, HEAD_DIM)), const((1, HEAD_DIM)), const((1, CONV_WIDTH))],
        out_specs=[rows(IN_WIDTH), rows(CONV_WIDTH), rows(ATTN_WIDTH), rows(KV_WIDTH), rows(KV_WIDTH)],
        out_shape=[jax.ShapeDtypeStruct((SEQ, IN_WIDTH), F32), jax.ShapeDtypeStruct((SEQ, CONV_WIDTH), BF16),
                   jax.ShapeDtypeStruct((SEQ, ATTN_WIDTH), BF16), jax.ShapeDtypeStruct((SEQ, KV_WIDTH), BF16),
                   jax.ShapeDtypeStruct((SEQ, KV_WIDTH), BF16)],
        scratch_shapes=[pltpu.VMEM((SUBLANES, CONV_WIDTH), F32)],
        compiler_params=_params(("arbitrary",)),
    )(x, g1, w_in_t, conv_w, gq, gk, gconv)


def _band_bias(tbl_ref, bkt, bias_ref):
    for h in range(N_HEADS):
        acc = jnp.zeros(bkt.shape, F32)
        for b in range(NUM_BUCKETS):
            acc = jnp.where(bkt == b, tbl_ref[b, h], acc)
        bias_ref[h] = acc


def _band_valid(i):
    qi = lax.broadcasted_iota(jnp.int32, (BLK, 2 * BLK), 0)
    ji = lax.broadcasted_iota(jnp.int32, (BLK, 2 * BLK), 1)
    dist = qi + BLK - ji
    return (dist >= 0) & (dist < BLK) & ((ji >= BLK) | (i > 0))


def _band_rows(ref, i):
    prev = pl.multiple_of(jnp.maximum(i - 1, 0) * BLK, BLK)
    cur = pl.multiple_of(i * BLK, BLK)
    return jnp.concatenate([ref[pl.ds(prev, BLK), :], ref[pl.ds(cur, BLK), :]], axis=0), prev, cur


def _head_probs(qh, kh, bias, valid, sink):
    logits = _dot_nt(qh, kh) * (HEAD_DIM ** -0.5) + bias
    logits = jnp.where(valid, logits, NEG_INF)
    m = jnp.maximum(jnp.max(logits, axis=-1, keepdims=True), sink)
    p = jnp.exp(logits - m)
    es = jnp.exp(sink - m)
    den = jnp.sum(p, axis=-1, keepdims=True) + es
    return p / den, es / den


def _attn_fwd(qn, kn, v, tbl, sinks, bkt, gattn):
    n_b = SEQ // BLK

    def body(q_ref, k_ref, v_ref, tbl_ref, sink_ref, bkt_ref, ga_ref, y_ref, yn_ref, bias_ref):
        i = pl.program_id(0)

        @pl.when(i == 0)
        def _():
            _band_bias(tbl_ref, bkt_ref[...], bias_ref)

        kb, _, _ = _band_rows(k_ref, i)
        vb, _, _ = _band_rows(v_ref, i)
        valid = _band_valid(i)
        q = q_ref[...]
        outs = []
        for h in range(N_HEADS):
            hk = h // GQA_GROUP
            kv = slice(HEAD_DIM * hk, HEAD_DIM * (hk + 1))
            probs, _ = _head_probs(q[:, HEAD_DIM * h:HEAD_DIM * (h + 1)], kb[:, kv], bias_ref[h], valid, sink_ref[0, h])
            outs.append(_dot(probs.astype(BF16), vb[:, kv]))
        y = jnp.concatenate(outs, axis=-1)
        y_ref[...] = y
        yn_ref[...] = (y * _rstd(y) * ga_ref[...]).astype(BF16)

    const = lambda shape: pl.BlockSpec(shape, lambda i: (0,) * len(shape))
    rows = lambda w: pl.BlockSpec((BLK, w), lambda i: (i, 0))
    smem = pl.BlockSpec(memory_space=pltpu.SMEM)
    return pl.pallas_call(
        body, name="attn_fwd", grid=(n_b,),
        in_specs=[rows(ATTN_WIDTH), const((SEQ, KV_WIDTH)), const((SEQ, KV_WIDTH)), smem, smem,
                  const((BLK, 2 * BLK)), const((1, ATTN_WIDTH))],
        out_specs=[rows(ATTN_WIDTH), rows(ATTN_WIDTH)],
        out_shape=[jax.ShapeDtypeStruct((SEQ, ATTN_WIDTH), F32), jax.ShapeDtypeStruct((SEQ, ATTN_WIDTH), BF16)],
        scratch_shapes=[pltpu.VMEM((N_HEADS, BLK, 2 * BLK), F32)],
        compiler_params=_params(("arbitrary",)),
    )(qn, kn, v, tbl, sinks, bkt, gattn)


def _ffn_fwd(x, ycn, yan, w_out, g2, w_up, fcw, fcb, w_down, tgt):
    tm = 256
    n_t = SEQ // tm

    def body(x_ref, ycn_ref, yan_ref, wo_ref, g2_ref, wu_ref, cw_ref, b_ref, wd_ref, tgt_ref,
             h1_ref, u2_ref, up_ref, dh2_ref, dh2b_ref, loss_ref, acc_ref, halo_ref):
        i, j = pl.program_id(0), pl.program_id(1)

        @pl.when((i == 0) & (j == 0))
        def _():
            loss_ref[...] = jnp.zeros_like(loss_ref)

        @pl.when(j == 0)
        def _():
            h1 = x_ref[...] + _dot(ycn_ref[...], wo_ref[0:CONV_WIDTH, :]) + _dot(yan_ref[...], wo_ref[CONV_WIDTH:, :])
            h1_ref[...] = h1
            u2_ref[...] = (h1 * _rstd(h1) * g2_ref[...]).astype(BF16)
            acc_ref[...] = jnp.zeros_like(acc_ref)

        u2 = u2_ref[...]
        pre = []
        for s in range(2):
            up = _dot(u2, wu_ref[s])
            up_ref[s] = up
            halo = jnp.where(i == 0, 0.0, halo_ref[s, j])
            pre.append(_conv3(up, cw_ref[s], halo)[0] + b_ref[s])
            halo_ref[s, j] = up[tm - SUBLANES:]
        g, val = pre
        act = (g * jax.nn.sigmoid(g) * val).astype(BF16)
        acc_ref[...] += _dot(act, wd_ref[...])

        @pl.when(j == N_FFN_BLK - 1)
        def _():
            err = h1_ref[...] + acc_ref[...] - tgt_ref[...]
            loss_ref[...] += 0.5 * jnp.sum(err * err) / D_MODEL
            dh2 = err / D_MODEL
            dh2_ref[...] = dh2
            dh2b_ref[...] = dh2.astype(BF16)

    rows = lambda w: pl.BlockSpec((tm, w), lambda i, j: (i, 0))
    const = lambda shape: pl.BlockSpec(shape, lambda i, j: (0,) * len(shape))
    pair = lambda r, c: pl.BlockSpec((2, None, r, c), lambda i, j: (0, j, 0, 0))
    upb = pl.BlockSpec((2, None, tm, FFN_BLK), lambda i, j: (0, j, i, 0))
    return pl.pallas_call(
        body, name="ffn_fwd", grid=(n_t, N_FFN_BLK),
        in_specs=[rows(D_MODEL), rows(CONV_WIDTH), rows(ATTN_WIDTH), const((D_MODEL, D_MODEL)), const((1, D_MODEL)),
                  pair(D_MODEL, FFN_BLK), pair(3, FFN_BLK), pair(1, FFN_BLK),
                  pl.BlockSpec((None, FFN_BLK, D_MODEL), lambda i, j: (j, 0, 0)), rows(D_MODEL)],
        out_specs=[rows(D_MODEL), rows(D_MODEL), upb, rows(D_MODEL), rows(D_MODEL), const((SUBLANES, 128))],
        out_shape=[jax.ShapeDtypeStruct((SEQ, D_MODEL), F32), jax.ShapeDtypeStruct((SEQ, D_MODEL), BF16),
                   jax.ShapeDtypeStruct((2, N_FFN_BLK, SEQ, FFN_BLK), F32),
                   jax.ShapeDtypeStruct((SEQ, D_MODEL), F32), jax.ShapeDtypeStruct((SEQ, D_MODEL), BF16),
                   jax.ShapeDtypeStruct((SUBLANES, 128), F32)],
        scratch_shapes=[pltpu.VMEM((tm, D_MODEL), F32), pltpu.VMEM((2, N_FFN_BLK, SUBLANES, FFN_BLK), F32)],
        compiler_params=_params(("arbitrary", "arbitrary")),
    )(x, ycn, yan, w_out, g2, w_up, fcw, fcb, w_down, tgt)


def _ffn_bwd(dh2, dh2b, h1, g2, up, w_up, fcw, fcb, w_down):
    tm = 256
    n_t = SEQ // tm
    halo_blocks = tm // SUBLANES

    def body(dh2_ref, dh2b_ref, h1_ref, g2_ref, up_ref, uph_ref, wu_ref, cw_ref, b_ref, wd_ref,
             act_ref, dup_ref, dh1_ref, dh1b_ref, dfb_ref, dfcw_ref, dg2_ref, acc_ref, next_ref):
        i, j = pl.program_id(0), pl.program_id(1)
        first_tile = i == n_t - 1

        @pl.when((i == 0) & (j == 0))
        def _():
            dfb_ref[...] = jnp.zeros_like(dfb_ref)
            dfcw_ref[...] = jnp.zeros_like(dfcw_ref)
            dg2_ref[...] = jnp.zeros_like(dg2_ref)

        @pl.when(j == 0)
        def _():
            acc_ref[...] = jnp.zeros_like(acc_ref)

        ups, pre = [], []
        for s in range(2):
            up = up_ref[s]
            halo = jnp.where(first_tile, 0.0, uph_ref[s])
            p, up2, up1 = _conv3(up, cw_ref[s], halo)
            pre.append(p + b_ref[s])
            ups.append((up, up1, up2))
        g, val = pre
        sg = jax.nn.sigmoid(g)
        silu = g * sg
        act_ref[...] = (silu * val).astype(BF16)
        dact = _dot_nt(dh2b_ref[...], wd_ref[...])
        dpre = (dact * val * (sg * (1.0 + g * (1.0 - sg))), dact * silu)
        for s in range(2):
            d = dpre[s]
            u, u1, u2 = ups[s]
            dfb_ref[s, j] += jnp.sum(d, axis=0, keepdims=True)
            dfcw_ref[s, j, 0:1, :] += jnp.sum(d * u2, axis=0, keepdims=True)
            dfcw_ref[s, j, 1:2, :] += jnp.sum(d * u1, axis=0, keepdims=True)
            dfcw_ref[s, j, 2:3, :] += jnp.sum(d * u, axis=0, keepdims=True)
            nxt = jnp.where(i == 0, 0.0, next_ref[s, j])
            dup = _conv3_bwd_input(d, cw_ref[s], nxt).astype(BF16)
            next_ref[s, j] = d[:SUBLANES]
            dup_ref[s] = dup
            acc_ref[...] += _dot_nt(dup, wu_ref[s])

        @pl.when(j == N_FFN_BLK - 1)
        def _():
            dn, dgain = _rms_bwd(h1_ref[...], g2_ref[...], acc_ref[...])
            dh1 = dh2_ref[...] + dn
            dh1_ref[...] = dh1
            dh1b_ref[...] = dh1.astype(BF16)
            dg2_ref[...] += dgain

    rev = lambda i: n_t - 1 - i
    rows = lambda w: pl.BlockSpec((tm, w), lambda i, j: (rev(i), 0))
    const = lambda shape: pl.BlockSpec(shape, lambda i, j: (0,) * len(shape))
    pair = lambda r, c: pl.BlockSpec((2, None, r, c), lambda i, j: (0, j, 0, 0))
    upb = pl.BlockSpec((2, None, tm, FFN_BLK), lambda i, j: (0, j, rev(i), 0))
    halo = pl.BlockSpec((2, None, SUBLANES, FFN_BLK), lambda i, j: (0, j, jnp.maximum(rev(i) * halo_blocks - 1, 0), 0))
    return pl.pallas_call(
        body, name="ffn_bwd", grid=(n_t, N_FFN_BLK),
        in_specs=[rows(D_MODEL), rows(D_MODEL), rows(D_MODEL), const((1, D_MODEL)), upb, halo,
                  pair(D_MODEL, FFN_BLK), pair(3, FFN_BLK), pair(1, FFN_BLK),
                  pl.BlockSpec((None, FFN_BLK, D_MODEL), lambda i, j: (j, 0, 0))],
        out_specs=[pl.BlockSpec((None, tm, FFN_BLK), lambda i, j: (j, rev(i), 0)), upb, rows(D_MODEL), rows(D_MODEL),
                   const((2, N_FFN_BLK, 1, FFN_BLK)), const((2, N_FFN_BLK, 3, FFN_BLK)), const((1, D_MODEL))],
        out_shape=[jax.ShapeDtypeStruct((N_FFN_BLK, SEQ, FFN_BLK), BF16),
                   jax.ShapeDtypeStruct((2, N_FFN_BLK, SEQ, FFN_BLK), BF16), jax.ShapeDtypeStruct((SEQ, D_MODEL), F32),
                   jax.ShapeDtypeStruct((SEQ, D_MODEL), BF16), jax.ShapeDtypeStruct((2, N_FFN_BLK, 1, FFN_BLK), F32),
                   jax.ShapeDtypeStruct((2, N_FFN_BLK, 3, FFN_BLK), F32), jax.ShapeDtypeStruct((1, D_MODEL), F32)],
        scratch_shapes=[pltpu.VMEM((tm, D_MODEL), F32), pltpu.VMEM((2, N_FFN_BLK, SUBLANES, FFN_BLK), F32)],
        compiler_params=_params(("arbitrary", "arbitrary")),
    )(dh2, dh2b, h1, g2, up, up, w_up, fcw, fcb, w_down)


def _grad_tn(a_list, b, out_rows, name):
    n = len(a_list)
    ncol = b.shape[1]

    def body(*refs):
        a_refs, b_ref, o_ref = refs[:n], refs[n], refs[n + 1]
        j = pl.program_id(0)
        for k in range(n):
            @pl.when(j == k)
            def _(k=k):
                o_ref[...] = _dot_tn(a_refs[k][...], b_ref[...]).astype(BF16)

    full = lambda shape: pl.BlockSpec(shape, lambda j: (0,) * len(shape))
    return pl.pallas_call(
        body, name=name, grid=(n,),
        in_specs=[full((SEQ, out_rows))] * n + [full((SEQ, ncol))],
        out_specs=pl.BlockSpec((None, out_rows, ncol), lambda j: (j, 0, 0)),
        out_shape=jax.ShapeDtypeStruct((n, out_rows, ncol), BF16),
        compiler_params=_params(("arbitrary",)),
    )(*a_list, b)


def _grad_tn_blocked(a, b, name, a_is_blocked):
    nb = a.shape[0] if a_is_blocked else b.shape[0]
    a_w, b_w = a.shape[-1], b.shape[-1]

    def body(a_ref, b_ref, o_ref):
        o_ref[...] = _dot_tn(a_ref[...], b_ref[...]).astype(BF16)

    blocked = lambda w: pl.BlockSpec((None, SEQ, w), lambda k: (k, 0, 0))
    full = lambda w: pl.BlockSpec((SEQ, w), lambda k: (0, 0))
    return pl.pallas_call(
        body, name=name, grid=(nb,),
        in_specs=[blocked(a_w) if a_is_blocked else full(a_w), full(b_w) if a_is_blocked else blocked(b_w)],
        out_specs=pl.BlockSpec((None, a_w, b_w), lambda k: (k, 0, 0)),
        out_shape=jax.ShapeDtypeStruct((nb, a_w, b_w), BF16),
        compiler_params=_params(("arbitrary",)),
    )(a, b)


def _out_bwd(dh1b, w_out, y_attn, gattn):
    tm = 256
    n_t = SEQ // tm

    def body(dh_ref, wo_ref, y_ref, ga_ref, dycn_ref, dy_ref, dga_ref):
        @pl.when(pl.program_id(0) == 0)
        def _():
            dga_ref[...] = jnp.zeros_like(dga_ref)

        dycat = _dot_nt(dh_ref[...], wo_ref[...])
        dycn_ref[...] = dycat[:, :CONV_WIDTH]
        dy, dga = _rms_bwd(y_ref[...], ga_ref[...], dycat[:, CONV_WIDTH:])
        dy_ref[...] = dy
        dga_ref[...] += dga

    rows = lambda w: pl.BlockSpec((tm, w), lambda i: (i, 0))
    const = lambda shape: pl.BlockSpec(shape, lambda i: (0,) * len(shape))
    return pl.pallas_call(
        body, name="out_bwd", grid=(n_t,),
        in_specs=[rows(D_MODEL), const((D_MODEL, D_MODEL)), rows(ATTN_WIDTH), const((1, ATTN_WIDTH))],
        out_specs=[rows(CONV_WIDTH), rows(ATTN_WIDTH), const((1, ATTN_WIDTH))],
        out_shape=[jax.ShapeDtypeStruct((SEQ, CONV_WIDTH), F32), jax.ShapeDtypeStruct((SEQ, ATTN_WIDTH), F32),
                   jax.ShapeDtypeStruct((1, ATTN_WIDTH), F32)],
        compiler_params=_params(("arbitrary",)),
    )(dh1b, w_out, y_attn, gattn)


def _attn_bwd(qn, kn, v, dy, tbl, sinks, bkt):
    n_b = SEQ // BLK

    def body(q_ref, k_ref, v_ref, dy_ref, tbl_ref, sink_ref, bkt_ref,
             dq_ref, dk_ref, dv_ref, dtbl_ref, dsink_ref, bias_ref, dbias_ref, dsacc_ref):
        i = pl.program_id(0)

        @pl.when(i == 0)
        def _():
            _band_bias(tbl_ref, bkt_ref[...], bias_ref)
            dbias_ref[...] = jnp.zeros_like(dbias_ref)
            dsacc_ref[...] = jnp.zeros_like(dsacc_ref)
            dk_ref[...] = jnp.zeros_like(dk_ref)
            dv_ref[...] = jnp.zeros_like(dv_ref)

        kb, prev, cur = _band_rows(k_ref, i)
        vb, _, _ = _band_rows(v_ref, i)
        valid = _band_valid(i)
        q = q_ref[...]
        dy = dy_ref[...]
        lane = lax.broadcasted_iota(jnp.int32, (BLK, 128), 1)
        dqs, dks, dvs = [], [], []
        dsink = jnp.zeros((BLK, 128), F32)
        for h in range(N_HEADS):
            hk = h // GQA_GROUP
            kv = slice(HEAD_DIM * hk, HEAD_DIM * (hk + 1))
            qh = q[:, HEAD_DIM * h:HEAD_DIM * (h + 1)]
            probs, psink = _head_probs(qh, kb[:, kv], bias_ref[h], valid, sink_ref[0, h])
            doh = dy[:, HEAD_DIM * h:HEAD_DIM * (h + 1)].astype(BF16)
            dprobs = _dot_nt(doh, vb[:, kv])
            dvh = _dot_tn(probs.astype(BF16), doh)
            dsum = jnp.sum(probs * dprobs, axis=-1, keepdims=True)
            dlogits = probs * (dprobs - dsum)
            dsink = jnp.where(lane == h, -psink * dsum, dsink)
            dbias_ref[h] += dlogits
            ds = (dlogits * (HEAD_DIM ** -0.5)).astype(BF16)
            dqs.append(_dot(ds, kb[:, kv]))
            dkh = _dot_tn(ds, qh)
            if h % GQA_GROUP == 0:
                dks.append(dkh)
                dvs.append(dvh)
            else:
                dks[hk] = dks[hk] + dkh
                dvs[hk] = dvs[hk] + dvh
        dq_ref[...] = jnp.concatenate(dqs, axis=-1)
        dsacc_ref[...] += dsink
        dkb = jnp.concatenate(dks, axis=-1)
        dvb = jnp.concatenate(dvs, axis=-1)
        dk_ref[pl.ds(prev, BLK), :] += dkb[:BLK]
        dk_ref[pl.ds(cur, BLK), :] += dkb[BLK:]
        dv_ref[pl.ds(prev, BLK), :] += dvb[:BLK]
        dv_ref[pl.ds(cur, BLK), :] += dvb[BLK:]

        @pl.when(i == n_b - 1)
        def _():
            dsink_ref[...] = jnp.sum(dsacc_ref[...], axis=0, keepdims=True)
            bkt = bkt_ref[...]
            row8 = lax.broadcasted_iota(jnp.int32, (N_HEADS, 128), 0)
            lane8 = lax.broadcasted_iota(jnp.int32, (N_HEADS, 128), 1)
            acc = jnp.zeros((N_HEADS, 128), F32)
            for h in range(N_HEADS):
                dbh = dbias_ref[h]
                for b in range(NUM_BUCKETS):
                    acc = jnp.where((row8 == h) & (lane8 == b), jnp.sum(jnp.where(bkt == b, dbh, 0.0)), acc)
            dtbl_ref[...] = acc

    const = lambda shape: pl.BlockSpec(shape, lambda i: (0,) * len(shape))
    rows = lambda w: pl.BlockSpec((BLK, w), lambda i: (i, 0))
    smem = pl.BlockSpec(memory_space=pltpu.SMEM)
    return pl.pallas_call(
        body, name="attn_bwd", grid=(n_b,),
        in_specs=[rows(ATTN_WIDTH), const((SEQ, KV_WIDTH)), const((SEQ, KV_WIDTH)), rows(ATTN_WIDTH), smem, smem,
                  const((BLK, 2 * BLK))],
        out_specs=[rows(ATTN_WIDTH), const((SEQ, KV_WIDTH)), const((SEQ, KV_WIDTH)), const((N_HEADS, 128)), const((1, 128))],
        out_shape=[jax.ShapeDtypeStruct((SEQ, ATTN_WIDTH), F32), jax.ShapeDtypeStruct((SEQ, KV_WIDTH), F32),
                   jax.ShapeDtypeStruct((SEQ, KV_WIDTH), F32), jax.ShapeDtypeStruct((N_HEADS, 128), F32),
                   jax.ShapeDtypeStruct((1, 128), F32)],
        scratch_shapes=[pltpu.VMEM((N_HEADS, BLK, 2 * BLK), F32), pltpu.VMEM((N_HEADS, BLK, 2 * BLK), F32),
                        pltpu.VMEM((BLK, 128), F32)],
        compiler_params=_params(("arbitrary",)),
    )(qn, kn, v, dy, tbl, sinks, bkt)


def _mix_in_bwd(x, dh1, proj, dycn, dqn, dkn, dv, w_in_t, conv_w, g1, gq, gk, gconv):
    tm = 256
    n_t = SEQ // tm
    halo_blocks = tm // SUBLANES

    def body(x_ref, dh1_ref, proj_ref, halo_ref, dycn_ref, dqn_ref, dkn_ref, dv_ref, w_ref, cw_ref,
             g1_ref, gq_ref, gk_ref, gc_ref,
             dx_ref, dproj_ref, u1_ref, dcw_ref, dgc_ref, dgq_ref, dgk_ref, dg1_ref, next_ref):
        i = pl.program_id(0)
        first_tile = i == n_t - 1

        @pl.when(i == 0)
        def _():
            for r in (dcw_ref, dgc_ref, dgq_ref, dgk_ref, dg1_ref, next_ref):
                r[...] = jnp.zeros_like(r)

        proj = proj_ref[...]
        hp = halo_ref[...]
        gate_b = proj[:, 0:CONV_WIDTH]
        gate_c = proj[:, CONV_WIDTH:2 * CONV_WIDTH]
        hc = proj[:, 2 * CONV_WIDTH:3 * CONV_WIDTH]
        a = gate_c * hc
        a_halo = jnp.where(first_tile, 0.0, hp[:, CONV_WIDTH:2 * CONV_WIDTH] * hp[:, 2 * CONV_WIDTH:3 * CONV_WIDTH])
        cw = cw_ref[...]
        cv, a2, a1 = _conv3(a, cw, a_halo)
        dyc, dgc = _rms_bwd(gate_b * cv, gc_ref[...], dycn_ref[...])
        dgc_ref[...] += dgc
        dcv = dyc * gate_b
        dcw_ref[...] += jnp.concatenate(
            [jnp.sum(dcv * a2, axis=0, keepdims=True), jnp.sum(dcv * a1, axis=0, keepdims=True),
             jnp.sum(dcv * a, axis=0, keepdims=True)], axis=0)
        da = _conv3_bwd_input(dcv, cw, next_ref[...])
        next_ref[...] = dcv[:SUBLANES]
        q0 = 3 * CONV_WIDTH
        k0 = q0 + ATTN_WIDTH
        dq, dgq = _head_norm_bwd(proj[:, q0:k0], gq_ref[...], dqn_ref[...], N_HEADS)
        dk, dgk = _head_norm_bwd(proj[:, k0:k0 + KV_WIDTH], gk_ref[...], dkn_ref[...], 2)
        dgq_ref[...] += dgq
        dgk_ref[...] += dgk
        dproj = jnp.concatenate([dyc * cv, da * hc, da * gate_c, dq, dk, dv_ref[...]], axis=-1).astype(BF16)
        dproj_ref[...] = dproj
        du1 = _dot(dproj, w_ref[...])
        xv = x_ref[...]
        dn, dg1 = _rms_bwd(xv, g1_ref[...], du1)
        dx_ref[...] = dh1_ref[...] + dn
        dg1_ref[...] += dg1
        u1_ref[...] = (xv * _rstd(xv) * g1_ref[...]).astype(BF16)

    rev = lambda i: n_t - 1 - i
    rows = lambda w: pl.BlockSpec((tm, w), lambda i: (rev(i), 0))
    const = lambda shape: pl.BlockSpec(shape, lambda i: (0,) * len(shape))
    halo = pl.BlockSpec((SUBLANES, IN_WIDTH), lambda i: (jnp.maximum(rev(i) * halo_blocks - 1, 0), 0))
    return pl.pallas_call(
        body, name="mix_in_bwd", grid=(n_t,),
        in_specs=[rows(D_MODEL), rows(D_MODEL), rows(IN_WIDTH), halo, rows(CONV_WIDTH), rows(ATTN_WIDTH), rows(KV_WIDTH),
                  rows(KV_WIDTH), const((IN_WIDTH, D_MODEL)), const((3, CONV_WIDTH)), const((1, D_MODEL)),
                  const((1, HEAD_DIM)), const((1, HEAD_DIM)), const((1, CONV_WIDTH))],
        out_specs=[rows(D_MODEL), rows(IN_WIDTH), rows(D_MODEL), const((3, CONV_WIDTH)), const((1, CONV_WIDTH)),
                   const((1, HEAD_DIM)), const((1, HEAD_DIM)), const((1, D_MODEL))],
        out_shape=[jax.ShapeDtypeStruct((SEQ, D_MODEL), F32), jax.ShapeDtypeStruct((SEQ, IN_WIDTH), BF16),
                   jax.ShapeDtypeStruct((SEQ, D_MODEL), BF16), jax.ShapeDtypeStruct((3, CONV_WIDTH), F32),
                   jax.ShapeDtypeStruct((1, CONV_WIDTH), F32), jax.ShapeDtypeStruct((1, HEAD_DIM), F32),
                   jax.ShapeDtypeStruct((1, HEAD_DIM), F32), jax.ShapeDtypeStruct((1, D_MODEL), F32)],
        scratch_shapes=[pltpu.VMEM((SUBLANES, CONV_WIDTH), F32)],
        compiler_params=_params(("arbitrary",)),
    )(x, dh1, proj, proj, dycn, dqn, dkn, dv, w_in_t, conv_w, g1, gq, gk, gconv)


def _grad_w_in(dproj, u1):
    bw = 768

    def body(a_ref, b_ref, o_ref):
        o_ref[...] = _dot_tn(a_ref[...], b_ref[...]).astype(BF16)

    return pl.pallas_call(
        body, name="grad_w_in", grid=(IN_WIDTH // bw,),
        in_specs=[pl.BlockSpec((SEQ, bw), lambda k: (0, k)), pl.BlockSpec((SEQ, D_MODEL), lambda k: (0, 0))],
        out_specs=pl.BlockSpec((bw, D_MODEL), lambda k: (k, 0)),
        out_shape=jax.ShapeDtypeStruct((IN_WIDTH, D_MODEL), BF16),
        compiler_params=_params(("arbitrary",)),
    )(dproj, u1)


def _adamw_math(w, g, m, v):
    m = ADAM_B1 * m + (1.0 - ADAM_B1) * g
    v = ADAM_B2 * v + (1.0 - ADAM_B2) * (g * g)
    m_hat = m / (1.0 - ADAM_B1 ** ADAM_STEP)
    v_hat = v / (1.0 - ADAM_B2 ** ADAM_STEP)
    return -ADAM_LR * (m_hat / (jnp.sqrt(v_hat) + ADAM_EPS) + ADAM_WD * w), m, v


def _adamw(w, m, v, recv, name, row_blocks=1, transpose=False):
    rows, cols = w.shape
    rb = rows // row_blocks

    def body(w_ref, m_ref, v_ref, r_ref, g_o, d_o, m_o, v_o):
        g = r_ref[0].astype(F32)
        for s in range(1, N_DEV):
            g = g + r_ref[s].astype(F32)
        if transpose:
            g = g.T
        g_o[...] = g
        d_o[...], m_o[...], v_o[...] = _adamw_math(w_ref[...], g, m_ref[...], v_ref[...])

    blk = pl.BlockSpec((rb, cols), lambda i: (i, 0))
    rblk = pl.BlockSpec((N_DEV,) + recv.shape[1:], lambda i: (0, 0, 0)) if transpose else \
        pl.BlockSpec((N_DEV, rb, cols), lambda i: (0, i, 0))
    return pl.pallas_call(
        body, name=name, grid=(row_blocks,),
        in_specs=[blk, blk, blk, rblk], out_specs=[blk] * 4,
        out_shape=[jax.ShapeDtypeStruct((rows, cols), F32)] * 4,
        compiler_params=_params(("arbitrary",)),
    )(w, m, v, recv)


def _local_step(x, tgt, w_in_t, w_out, w_up, w_down, conv_w, fcw, fcb, g1, g2, gq, gk, gconv, gattn, tbl, sinks):
    bkt = jnp.asarray(_bucket_map())
    proj, ycn, qn, kn, v = _mix_in_fwd(x, g1, w_in_t, conv_w, gq, gk, gconv)
    y_attn, yan = _attn_fwd(qn, kn, v, tbl, sinks, bkt, gattn)
    h1, u2, up, dh2, dh2b, loss = _ffn_fwd(x, ycn, yan, w_out, g2, w_up, fcw, fcb, w_down, tgt)
    act, dup, dh1, dh1b, dfb, dfcw, dg2 = _ffn_bwd(dh2, dh2b, h1, g2, up, w_up, fcw, fcb, w_down)
    dw_down = _grad_tn_blocked(act, dh2b, "grad_w_down", a_is_blocked=True)
    dw_up = _grad_tn_blocked(u2, dup.reshape(N_DEV, SEQ, FFN_BLK), "grad_w_up", a_is_blocked=False)
    dycn, dy_attn, dgattn = _out_bwd(dh1b, w_out, y_attn, gattn)
    dw_out = _grad_tn([ycn, yan], dh1b, CONV_WIDTH, "grad_w_out")
    dqn, dkn, dv, dtbl_t, dsinks = _attn_bwd(qn, kn, v, dy_attn, tbl, sinks, bkt)
    dx, dproj, u1, dcw, dgconv, dgq, dgk, dg1 = _mix_in_bwd(x, dh1, proj, dycn, dqn, dkn, dv, w_in_t, conv_w,
                                                             g1, gq, gk, gconv)
    dw_in_t = _grad_w_in(dproj, u1)
    big = dict(w_in_t=dw_in_t, w_out=dw_out, w_up=dw_up, w_down=dw_down, conv_w=dcw,
               ffn_conv_w=dfcw.reshape(N_DEV, 3, FFN_BLK))
    small = dict(norm_mix_g=dg1, norm_ffn_g=dg2, out_norm_conv_g=dgconv, out_norm_attn_g=dgattn, ffn_conv_b=dfb,
                 q_norm_g=dgq, k_norm_g=dgk, sinks=dsinks[:, :N_HEADS], rel_bias_table=dtbl_t[:, :NUM_BUCKETS].T)
    return loss[0, 0], dx, big, small


_SMALL_NAMES = ("norm_mix_g", "norm_ffn_g", "out_norm_conv_g", "out_norm_attn_g", "ffn_conv_b", "q_norm_g", "k_norm_g",
                "sinks", "rel_bias_table")


def _pack_small(vals):
    flat = jnp.concatenate([vals[k].reshape(-1) for k in _SMALL_NAMES])
    return jnp.pad(flat, (0, SMALL_ROWS * SMALL_COLS - flat.shape[0])).reshape(SMALL_ROWS, SMALL_COLS)


def _unpack_small(packed, like):
    flat = packed.reshape(-1)
    out, off = {}, 0
    for k in _SMALL_NAMES:
        size = like[k].size
        out[k] = flat[off:off + size].reshape(like[k].shape)
        off += size
    return out


def kernel(x, norm_mix_g, w_in, conv_w, q_norm_g, k_norm_g, rel_bias_table, sinks, out_norm_conv_g, out_norm_attn_g, w_out, norm_ffn_g, w_up, ffn_conv_w, ffn_conv_b, w_down, loss_target, m_norm_mix_g, m_w_in, m_conv_w, m_q_norm_g, m_k_norm_g, m_rel_bias_table, m_sinks, m_out_norm_conv_g, m_out_norm_attn_g, m_w_out, m_norm_ffn_g, m_w_up, m_ffn_conv_w, m_ffn_conv_b, m_w_down, v_norm_mix_g, v_w_in, v_conv_w, v_q_norm_g, v_k_norm_g, v_rel_bias_table, v_sinks, v_out_norm_conv_g, v_out_norm_attn_g, v_w_out, v_norm_ffn_g, v_w_up, v_ffn_conv_w, v_ffn_conv_b, v_w_down):
    p = dict(norm_mix_g=norm_mix_g, w_in=w_in, conv_w=conv_w, q_norm_g=q_norm_g, k_norm_g=k_norm_g,
             rel_bias_table=rel_bias_table, sinks=sinks, out_norm_conv_g=out_norm_conv_g, out_norm_attn_g=out_norm_attn_g,
             w_out=w_out, norm_ffn_g=norm_ffn_g, w_up=w_up, ffn_conv_w=ffn_conv_w, ffn_conv_b=ffn_conv_b, w_down=w_down)
    m = dict(norm_mix_g=m_norm_mix_g, w_in=m_w_in, conv_w=m_conv_w, q_norm_g=m_q_norm_g, k_norm_g=m_k_norm_g,
             rel_bias_table=m_rel_bias_table, sinks=m_sinks, out_norm_conv_g=m_out_norm_conv_g,
             out_norm_attn_g=m_out_norm_attn_g, w_out=m_w_out, norm_ffn_g=m_norm_ffn_g, w_up=m_w_up,
             ffn_conv_w=m_ffn_conv_w, ffn_conv_b=m_ffn_conv_b, w_down=m_w_down)
    v = dict(norm_mix_g=v_norm_mix_g, w_in=v_w_in, conv_w=v_conv_w, q_norm_g=v_q_norm_g, k_norm_g=v_k_norm_g,
             rel_bias_table=v_rel_bias_table, sinks=v_sinks, out_norm_conv_g=v_out_norm_conv_g,
             out_norm_attn_g=v_out_norm_attn_g, w_out=v_w_out, norm_ffn_g=v_norm_ffn_g, w_up=v_w_up,
             ffn_conv_w=v_ffn_conv_w, ffn_conv_b=v_ffn_conv_b, w_down=v_w_down)

    wi_b, wo_b, wu_b, wd_b = _cast_shards(w_in[0], w_out[0], w_up[0], w_down[0])
    wi_g, wo_g, wu_g, wd_g, cw_g, fcw_g = _all_gather([wi_b, wo_b, wu_b, wd_b, conv_w[0], ffn_conv_w[0]], "gather_weights")
    w_in_t = wi_g.reshape(IN_WIDTH, D_MODEL)
    w_out_f = wo_g.reshape(D_MODEL, D_MODEL)
    w_down_f = wd_g.reshape(N_FFN_BLK, FFN_BLK, D_MODEL)
    conv_w_f = jnp.transpose(cw_g, (1, 0, 2)).reshape(3, CONV_WIDTH)
    w_up_f = wu_g.reshape(2, N_FFN_BLK, D_MODEL, FFN_BLK)
    fcw_f = fcw_g.reshape(2, N_FFN_BLK, 3, FFN_BLK)
    fcb = ffn_conv_b.reshape(2, N_FFN_BLK, 1, FFN_BLK)

    loss_local, dx, big, small = _local_step(
        x[0], loss_target[0], w_in_t, w_out_f, w_up_f, w_down_f, conv_w_f, fcw_f, fcb,
        norm_mix_g, norm_ffn_g, q_norm_g, k_norm_g, out_norm_conv_g, out_norm_attn_g, rel_bias_table, sinks)
    loss = lax.psum(loss_local, ("x", "y", "c"))

    send = [big["w_in_t"].reshape(N_DEV, IN_WIDTH // N_DEV, D_MODEL),
            big["w_out"].reshape(N_DEV, D_MODEL // N_DEV, D_MODEL),
            big["w_up"],
            big["w_down"].reshape(N_DEV, D_FF // N_DEV, D_MODEL),
            jnp.transpose(big["conv_w"].reshape(3, N_DEV, CONV_WIDTH // N_DEV), (1, 0, 2)),
            big["ffn_conv_w"]]
    r_wi, r_wo, r_wu, r_wd, r_cw, r_fcw = _exchange_direct(send, "scatter_grads", scatter=True)
    small_like = {k: p[k] for k in _SMALL_NAMES}
    small = {k: small[k].reshape(p[k].shape) for k in _SMALL_NAMES}
    (r_small,) = _exchange_direct([_pack_small(small)], "gather_small_grads", scatter=False)

    res = {}
    res["w_in"] = _adamw(w_in[0], m_w_in[0], v_w_in[0], r_wi, "adamw_w_in", transpose=True)
    res["w_out"] = _adamw(w_out[0], m_w_out[0], v_w_out[0], r_wo, "adamw_w_out")
    res["w_up"] = _adamw(w_up[0], m_w_up[0], v_w_up[0], r_wu, "adamw_w_up", row_blocks=4)
    res["w_down"] = _adamw(w_down[0], m_w_down[0], v_w_down[0], r_wd, "adamw_w_down", row_blocks=2)
    res["conv_w"] = _adamw(conv_w[0], m_conv_w[0], v_conv_w[0], r_cw, "adamw_conv_w")
    res["ffn_conv_w"] = _adamw(ffn_conv_w[0], m_ffn_conv_w[0], v_ffn_conv_w[0], r_fcw, "adamw_ffn_conv_w")
    res = {k: tuple(a[None] for a in t) for k, t in res.items()}
    sm = _adamw(_pack_small(small_like), _pack_small({k: m[k] for k in _SMALL_NAMES}),
                _pack_small({k: v[k] for k in _SMALL_NAMES}), r_small, "adamw_small")
    sm = [_unpack_small(a, small_like) for a in sm]
    for k in _SMALL_NAMES:
        res[k] = tuple(a[k] for a in sm)

    order = ("norm_mix_g", "w_in", "conv_w", "q_norm_g", "k_norm_g", "rel_bias_table", "sinks", "out_norm_conv_g",
             "out_norm_attn_g", "w_out", "norm_ffn_g", "w_up", "ffn_conv_w", "ffn_conv_b", "w_down")
    return (loss, dx[None], *[res[k][0] for k in order], *[res[k][1] for k in order],
            *[res[k][2] for k in order], *[res[k][3] for k in order])
```

```python
import functools
import math

import numpy as np
import jax
import jax.numpy as jnp
from jax import lax
from jax.experimental import pallas as pl
from jax.experimental.pallas import tpu as pltpu

F32 = jnp.float32
BF16 = jnp.bfloat16

SEQ = 2048
D_MODEL = 1024
CONV_WIDTH = 512
ATTN_WIDTH = 512
KV_WIDTH = 128
HEAD_DIM = 64
N_HEADS = 8
GQA_GROUP = 4
IN_WIDTH = 2304
D_FF = 2816
BLK = 128
NUM_BUCKETS = 32
EPS = 1e-6
NEG_INF = -1e30
ADAM_LR = 0.001
ADAM_B1 = 0.9
ADAM_B2 = 0.999
ADAM_EPS = 1e-08
ADAM_WD = 0.01
ADAM_STEP = 10

N_DEV = 8
FFN_BLK = 2 * D_FF // N_DEV
N_FFN_BLK = D_FF // FFN_BLK
SUBLANES = 8
VMEM_LIMIT = 56 * 1024 * 1024
SMALL_ROWS, SMALL_COLS = 16, 1024

_MESH = pl.DeviceIdType.MESH
_ANY = pl.BlockSpec(memory_space=pl.ANY)


def _params(sem):
    return pltpu.CompilerParams(dimension_semantics=sem, vmem_limit_bytes=VMEM_LIMIT)


def _dot(a, b):
    return jnp.dot(a, b, preferred_element_type=F32)


def _dot_nt(a, b):
    return lax.dot_general(a, b, (((1,), (1,)), ((), ())), preferred_element_type=F32)


def _dot_tn(a, b):
    return lax.dot_general(a, b, (((0,), (0,)), ((), ())), preferred_element_type=F32)


def _shift_down(x, s, halo):
    r = pltpu.roll(x, s, axis=0)
    hr = pltpu.roll(halo, s, axis=0)
    row = lax.broadcasted_iota(jnp.int32, halo.shape, 0)
    top = jnp.where(row < s, hr, r[:SUBLANES])
    return jnp.concatenate([top, r[SUBLANES:]], axis=0)


def _shift_up(x, s, halo):
    n = x.shape[0]
    r = pltpu.roll(x, n - s, axis=0)
    hr = pltpu.roll(halo, SUBLANES - s, axis=0)
    row = lax.broadcasted_iota(jnp.int32, halo.shape, 0)
    bot = jnp.where(row >= SUBLANES - s, hr, r[n - SUBLANES:])
    return jnp.concatenate([r[:n - SUBLANES], bot], axis=0)


def _conv3(x, w, halo):
    x2 = _shift_down(x, 2, halo)
    x1 = _shift_down(x, 1, halo)
    return x2 * w[0:1] + x1 * w[1:2] + x * w[2:3], x2, x1


def _conv3_bwd_input(dy, w, halo_next):
    return dy * w[2:3] + _shift_up(dy, 1, halo_next) * w[1:2] + _shift_up(dy, 2, halo_next) * w[0:1]


def _rstd(x):
    return lax.rsqrt(jnp.mean(x * x, axis=-1, keepdims=True) + EPS)


def _rms_bwd(x, g, dy):
    r = _rstd(x)
    n = x * r
    dn = dy * g
    dx = r * (dn - n * jnp.mean(dn * n, axis=-1, keepdims=True))
    return dx, jnp.sum(dy * n, axis=0, keepdims=True)


def _head_norm(x, g, heads):
    parts = []
    for h in range(heads):
        xh = x[:, HEAD_DIM * h:HEAD_DIM * (h + 1)]
        parts.append(xh * _rstd(xh) * g)
    return jnp.concatenate(parts, axis=-1)


def _head_norm_bwd(x, g, dy, heads):
    dxs, dg = [], jnp.zeros((1, HEAD_DIM), F32)
    for h in range(heads):
        sl = slice(HEAD_DIM * h, HEAD_DIM * (h + 1))
        dxh, dgh = _rms_bwd(x[:, sl], g, dy[:, sl])
        dxs.append(dxh)
        dg = dg + dgh
    return jnp.concatenate(dxs, axis=-1), dg


def _bucket_map():
    q = np.arange(BLK)[:, None]
    j = np.arange(2 * BLK)[None, :]
    n = np.maximum(q + BLK - j, 0)
    nf = np.maximum(n, 1).astype(np.float32)
    max_exact = NUM_BUCKETS // 2
    large = max_exact + (np.log(nf / max_exact) / math.log(BLK / max_exact) * (NUM_BUCKETS - max_exact)).astype(np.int32)
    large = np.minimum(large, NUM_BUCKETS - 1)
    return np.where(n < max_exact, n, large).astype(np.int32)


def _coords():
    return lax.axis_index("x"), lax.axis_index("y"), lax.axis_index("c")


def _lin(px, py, pc):
    return 4 * px + 2 * py + pc


_HBM = pl.BlockSpec(memory_space=pltpu.HBM)
_SEM = pl.BlockSpec(memory_space=pltpu.SEMAPHORE)
_EFFECT = pltpu.SideEffectType.DATAFLOW_SIDE_EFFECTING


def _in_hbm(a):
    return pltpu.with_memory_space_constraint(a, pltpu.HBM)


def _split_start(name, srcs, lands, plan, after):
    ns, nl = len(srcs), len(lands)
    n_copies = len(plan(0, 0, 0))
    n_after = 0 if after is None else 1

    def body(*refs):
        src_refs, land_refs = refs[:ns], refs[ns:ns + nl]
        send_sems, recv_sems = refs[ns + nl + n_after], refs[ns + nl + n_after + 1]
        token = refs[-1]
        for k, (a, s_slot, l, d_slot, dev) in enumerate(plan(*_coords())):
            src = src_refs[a] if s_slot is None else src_refs[a].at[s_slot]
            pltpu.make_async_remote_copy(src_ref=src, dst_ref=land_refs[l].at[d_slot], send_sem=send_sems.at[k],
                                         recv_sem=recv_sems.at[k], device_id=dev, device_id_type=_MESH).start()
        token[...] = jnp.zeros_like(token)

    arrs = list(srcs) + list(lands)
    out = pl.pallas_call(
        body, name=name,
        out_shape=(pltpu.SemaphoreType.DMA((n_copies,)), pltpu.SemaphoreType.DMA((n_copies,)),
                   *[pltpu.HBM(a.shape, a.dtype) for a in arrs], jax.ShapeDtypeStruct((SUBLANES, 128), F32)),
        in_specs=[_HBM] * (ns + nl) + [_ANY] * n_after,
        out_specs=(_SEM, _SEM, *[_HBM] * (ns + nl), pl.BlockSpec(memory_space=pltpu.VMEM)),
        input_output_aliases={i: 2 + i for i in range(ns + nl)},
        compiler_params=pltpu.CompilerParams(has_side_effects=_EFFECT),
    )(*[_in_hbm(a) for a in arrs], *([] if after is None else [after]))
    return out[0], out[1], list(out[2:2 + ns]), list(out[2 + ns:2 + ns + nl]), out[-1]


def _split_wait(name, send_sems, recv_sems, srcs, lands, plan, recv_slots, after):
    ns, nl = len(srcs), len(lands)

    def body(*refs):
        src_refs, land_refs = refs[:ns], refs[ns:ns + nl]
        send_sems, recv_sems = refs[ns + nl], refs[ns + nl + 1]
        coords = _coords()
        slots = recv_slots(*coords)
        for k, (a, s_slot, l, _, dev) in enumerate(plan(*coords)):
            src = src_refs[a] if s_slot is None else src_refs[a].at[s_slot]
            cp = pltpu.make_async_remote_copy(src_ref=src, dst_ref=land_refs[l].at[slots[k]], send_sem=send_sems.at[k],
                                              recv_sem=recv_sems.at[k], device_id=dev, device_id_type=_MESH)
            cp.wait_send()
            cp.wait_recv()

    arrs = list(srcs) + list(lands)
    out = pl.pallas_call(
        body, name=name,
        out_shape=tuple(pltpu.HBM(a.shape, a.dtype) for a in arrs),
        in_specs=[_HBM] * (ns + nl) + [_SEM, _SEM, _ANY],
        out_specs=tuple([_HBM] * (ns + nl)),
        input_output_aliases={i: i for i in range(ns + nl)},
        compiler_params=pltpu.CompilerParams(has_side_effects=_EFFECT),
    )(*arrs, send_sems, recv_sems, after)
    return list(out[:ns]), list(out[ns:])


def _chips(x, y):
    return [(1 - x, y), (x, 1 - y), (1 - x, 1 - y)]


def _gather_plan(n):
    def plan(x, y, c):
        me = _lin(x, y, c)
        out = []
        for a in range(n):
            out.append((a, None, a, me, (x, y, 1 - c)))
            out += [(a, None, a, me, (cx, cy, c)) for cx, cy in _chips(x, y)]
        return out

    def recv_slots(x, y, c):
        out = []
        for _ in range(n):
            out.append(_lin(x, y, 1 - c))
            out += [_lin(cx, cy, c) for cx, cy in _chips(x, y)]
        return out

    return plan, recv_slots


def _gather_finish(shards, lands, name):
    n = len(shards)

    def body(*refs):
        ins, lands_in, outs = refs[:n], refs[n:2 * n], refs[2 * n:3 * n]
        send_sems, recv_sems, loc_sems = refs[3 * n:]
        x, y, c = _coords()
        sib = (x, y, 1 - c)
        local = [pltpu.make_async_copy(ins[a], outs[a].at[_lin(x, y, c)], loc_sems.at[a]) for a in range(n)]
        sends, recvs = [], []
        for a in range(n):
            for j, (cx, cy) in enumerate(_chips(x, y)):
                for slot, to in ((_lin(cx, cy, c), sends), (_lin(cx, cy, 1 - c), recvs)):
                    to.append(pltpu.make_async_remote_copy(
                        src_ref=outs[a].at[slot], dst_ref=outs[a].at[slot], send_sem=send_sems.at[j, a],
                        recv_sem=recv_sems.at[j, a], device_id=sib, device_id_type=_MESH))
        for cp in local + sends:
            cp.start()
        for cp in recvs:
            cp.wait_recv()
        for cp in sends:
            cp.wait_send()
        for cp in local:
            cp.wait()

    return pl.pallas_call(
        body, name=name,
        out_shape=[jax.ShapeDtypeStruct(a.shape, a.dtype) for a in lands],
        in_specs=[_ANY] * (2 * n), out_specs=[_ANY] * n,
        input_output_aliases={n + a: a for a in range(n)},
        scratch_shapes=[pltpu.SemaphoreType.DMA((3, n)), pltpu.SemaphoreType.DMA((3, n)), pltpu.SemaphoreType.DMA((n,))],
    )(*shards, *lands)


def _all_gather_split(shards, tag, after):
    n = len(shards)
    plan, recv_slots = _gather_plan(n)
    lands = [lax.empty((N_DEV,) + a.shape, a.dtype) for a in shards]
    send_sems, recv_sems, srcs, lands, token = _split_start(f"gather_{tag}_start", shards, lands, plan, after)

    def finish(after):
        own, got = _split_wait(f"gather_{tag}_wait", send_sems, recv_sems, srcs, lands, plan, recv_slots, after)
        return _gather_finish(own, got, f"gather_{tag}_finish")

    return finish, token


_CHIP_LIST = ((0, 0), (0, 1), (1, 0), (1, 1))


def _reduce_plan_d2d(n):
    def plan(x, y, c):
        return [(a, _lin(qx, qy, 1 - c), a, q, (x, y, 1 - c)) for a in range(n) for q, (qx, qy) in enumerate(_CHIP_LIST)]

    def recv_slots(x, y, c):
        return [q for _ in range(n) for q in range(4)]

    return plan, recv_slots


def _reduce_plan_ici(n):
    def plan(x, y, c):
        return [(a, 2 * cx + cy, a, j, (cx, cy, c)) for a in range(n) for j, (cx, cy) in enumerate(_chips(x, y))]

    def recv_slots(x, y, c):
        return [j for _ in range(n) for j in range(3)]

    return plan, recv_slots


def _broadcast_plan():
    def peers(x, y, c):
        return [(1 - x if r & 4 else x, 1 - y if r & 2 else y, 1 - c if r & 1 else c) for r in range(1, N_DEV)]

    def plan(x, y, c):
        return [(0, None, 0, _lin(x, y, c), peer) for peer in peers(x, y, c)]

    def recv_slots(x, y, c):
        return [_lin(*peer) for peer in peers(x, y, c)]

    return plan, recv_slots


def _chip_partial(grads, recvd, core, name):
    n = len(grads)

    def body(c_ref, *refs):
        for a in range(n):
            g_ref, r_ref, o_ref = refs[a], refs[n + a], refs[2 * n + a]
            o_ref[...] = (g_ref[...].astype(F32) + r_ref[...].astype(F32)).astype(o_ref.dtype)

    blk = lambda a, own: pl.BlockSpec((None,) + a.shape[1:],
                                      (lambda q, c_ref: (2 * q + c_ref[0], 0, 0)) if own else (lambda q, c_ref: (q, 0, 0)))
    return pl.pallas_call(
        body, name=name,
        grid_spec=pltpu.PrefetchScalarGridSpec(
            num_scalar_prefetch=1, grid=(4,),
            in_specs=[blk(a, True) for a in grads] + [blk(a, False) for a in recvd],
            out_specs=[blk(a, False) for a in recvd]),
        out_shape=[jax.ShapeDtypeStruct(a.shape, a.dtype) for a in recvd],
        compiler_params=_params(("arbitrary",)),
    )(core, *grads, *recvd)


def _reduce_scatter_split(grads, tag, core, behind):
    n = len(grads)
    plan1, slots1 = _reduce_plan_d2d(n)
    lands1 = [lax.empty((4,) + a.shape[1:], a.dtype) for a in grads]
    s1, r1, srcs1, lands1, token1 = _split_start(f"reduce_{tag}_d2d_start", grads, lands1, plan1, None)
    own, got = _split_wait(f"reduce_{tag}_d2d_wait", s1, r1, srcs1, lands1, plan1, slots1, behind(token1))
    parts = _chip_partial(own, got, core, f"reduce_{tag}_partial")
    plan2, slots2 = _reduce_plan_ici(n)
    lands2 = [lax.empty((3,) + a.shape[1:], a.dtype) for a in grads]
    s2, r2, srcs2, lands2, token2 = _split_start(f"reduce_{tag}_ici_start", parts, lands2, plan2, None)

    def finish(after):
        return _split_wait(f"reduce_{tag}_ici_wait", s2, r2, srcs2, lands2, plan2, slots2, after)

    return finish, token2


def _cast_shards(w_in, w_out, w_up, w_down):
    def body(wi_ref, wo_ref, wu_ref, wd_ref, wi_o, wo_o, wu_o, wd_o):
        wi_o[...] = wi_ref[...].T.astype(BF16)
        wo_o[...] = wo_ref[...].astype(BF16)
        wu_o[...] = wu_ref[...].astype(BF16)
        wd_o[...] = wd_ref[...].astype(BF16)

    return pl.pallas_call(
        body, name="cast_shards",
        out_shape=[jax.ShapeDtypeStruct(w_in.shape[::-1], BF16), jax.ShapeDtypeStruct(w_out.shape, BF16),
                   jax.ShapeDtypeStruct(w_up.shape, BF16), jax.ShapeDtypeStruct(w_down.shape, BF16)],
        compiler_params=pltpu.CompilerParams(vmem_limit_bytes=VMEM_LIMIT),
    )(w_in, w_out, w_up, w_down)


def _mix_in_fwd(x, g1, w_in_t, conv_w, gq, gk, gconv):
    tm = 256
    n_t = SEQ // tm

    def body(x_ref, g1_ref, w_ref, cw_ref, gq_ref, gk_ref, gc_ref,
             proj_ref, ycn_ref, qn_ref, kn_ref, v_ref, halo_ref):
        @pl.when(pl.program_id(0) == 0)
        def _():
            halo_ref[...] = jnp.zeros_like(halo_ref)

        xv = x_ref[...]
        u = (xv * _rstd(xv) * g1_ref[...]).astype(BF16)
        proj = _dot_nt(u, w_ref[...])
        proj_ref[...] = proj
        gate_b = proj[:, 0:CONV_WIDTH]
        a = proj[:, CONV_WIDTH:2 * CONV_WIDTH] * proj[:, 2 * CONV_WIDTH:3 * CONV_WIDTH]
        cv, _, _ = _conv3(a, cw_ref[...], halo_ref[...])
        halo_ref[...] = a[tm - SUBLANES:]
        yc = gate_b * cv
        ycn_ref[...] = (yc * _rstd(yc) * gc_ref[...]).astype(BF16)
        q0 = 3 * CONV_WIDTH
        qn_ref[...] = _head_norm(proj[:, q0:q0 + ATTN_WIDTH], gq_ref[...], N_HEADS).astype(BF16)
        k0 = q0 + ATTN_WIDTH
        kn_ref[...] = _head_norm(proj[:, k0:k0 + KV_WIDTH], gk_ref[...], 2).astype(BF16)
        v_ref[...] = proj[:, k0 + KV_WIDTH:k0 + 2 * KV_WIDTH].astype(BF16)

    const = lambda shape: pl.BlockSpec(shape, lambda i: (0,) * len(shape))
    rows = lambda w: pl.BlockSpec((tm, w), lambda i: (i, 0))
    return pl.pallas_call(
        body, name="mix_in_fwd", grid=(n_t,),
        in_specs=[rows(D_MODEL), const((1, D_MODEL)), const((IN_WIDTH, D_MODEL)), const((3, CONV_WIDTH)),
                  const((1, HEAD_DIM)), const((1, HEAD_DIM)), const((1, CONV_WIDTH))],
        out_specs=[rows(IN_WIDTH), rows(CONV_WIDTH), rows(ATTN_WIDTH), rows(KV_WIDTH), rows(KV_WIDTH)],
        out_shape=[jax.ShapeDtypeStruct((SEQ, IN_WIDTH), F32), jax.ShapeDtypeStruct((SEQ, CONV_WIDTH), BF16),
                   jax.ShapeDtypeStruct((SEQ, ATTN_WIDTH), BF16), jax.ShapeDtypeStruct((SEQ, KV_WIDTH), BF16),
                   jax.ShapeDtypeStruct((SEQ, KV_WIDTH), BF16)],
        scratch_shapes=[pltpu.VMEM((SUBLANES, CONV_WIDTH), F32)],
        compiler_params=_params(("arbitrary",)),
    )(x, g1, w_in_t, conv_w, gq, gk, gconv)


def _band_bias(tbl_ref, bkt, bias_ref):
    for h in range(N_HEADS):
        acc = jnp.zeros(bkt.shape, F32)
        for b in range(NUM_BUCKETS):
            acc = jnp.where(bkt == b, tbl_ref[b, h], acc)
        bias_ref[h] = acc


def _band_valid(i):
    qi = lax.broadcasted_iota(jnp.int32, (BLK, 2 * BLK), 0)
    ji = lax.broadcasted_iota(jnp.int32, (BLK, 2 * BLK), 1)
    dist = qi + BLK - ji
    return (dist >= 0) & (dist < BLK) & ((ji >= BLK) | (i > 0))


def _band_rows(ref, i):
    prev = pl.multiple_of(jnp.maximum(i - 1, 0) * BLK, BLK)
    cur = pl.multiple_of(i * BLK, BLK)
    return jnp.concatenate([ref[pl.ds(prev, BLK), :], ref[pl.ds(cur, BLK), :]], axis=0), prev, cur


def _head_probs(qh, kh, bias, valid, sink):
    logits = _dot_nt(qh, kh) * (HEAD_DIM ** -0.5) + bias
    logits = jnp.where(valid, logits, NEG_INF)
    m = jnp.maximum(jnp.max(logits, axis=-1, keepdims=True), sink)
    p = jnp.exp(logits - m)
    es = jnp.exp(sink - m)
    den = jnp.sum(p, axis=-1, keepdims=True) + es
    return p / den, es / den


def _attn_fwd(qn, kn, v, tbl, sinks, bkt, gattn):
    n_b = SEQ // BLK

    def body(q_ref, k_ref, v_ref, tbl_ref, sink_ref, bkt_ref, ga_ref, y_ref, yn_ref, bias_ref):
        i = pl.program_id(0)

        @pl.when(i == 0)
        def _():
            _band_bias(tbl_ref, bkt_ref[...], bias_ref)

        kb, _, _ = _band_rows(k_ref, i)
        vb, _, _ = _band_rows(v_ref, i)
        valid = _band_valid(i)
        q = q_ref[...]
        outs = []
        for h in range(N_HEADS):
            hk = h // GQA_GROUP
            kv = slice(HEAD_DIM * hk, HEAD_DIM * (hk + 1))
            probs, _ = _head_probs(q[:, HEAD_DIM * h:HEAD_DIM * (h + 1)], kb[:, kv], bias_ref[h], valid, sink_ref[0, h])
            outs.append(_dot(probs.astype(BF16), vb[:, kv]))
        y = jnp.concatenate(outs, axis=-1)
        y_ref[...] = y
        yn_ref[...] = (y * _rstd(y) * ga_ref[...]).astype(BF16)

    const = lambda shape: pl.BlockSpec(shape, lambda i: (0,) * len(shape))
    rows = lambda w: pl.BlockSpec((BLK, w), lambda i: (i, 0))
    smem = pl.BlockSpec(memory_space=pltpu.SMEM)
    return pl.pallas_call(
        body, name="attn_fwd", grid=(n_b,),
        in_specs=[rows(ATTN_WIDTH), const((SEQ, KV_WIDTH)), const((SEQ, KV_WIDTH)), smem, smem,
                  const((BLK, 2 * BLK)), const((1, ATTN_WIDTH))],
        out_specs=[rows(ATTN_WIDTH), rows(ATTN_WIDTH)],
        out_shape=[jax.ShapeDtypeStruct((SEQ, ATTN_WIDTH), F32), jax.ShapeDtypeStruct((SEQ, ATTN_WIDTH), BF16)],
        scratch_shapes=[pltpu.VMEM((N_HEADS, BLK, 2 * BLK), F32)],
        compiler_params=_params(("arbitrary",)),
    )(qn, kn, v, tbl, sinks, bkt, gattn)


def _ffn_fwd(x, ycn, yan, w_out, g2, w_up, fcw, fcb, w_down, tgt):
    tm = 256
    n_t = SEQ // tm

    def body(x_ref, ycn_ref, yan_ref, wo_ref, g2_ref, wu_ref, cw_ref, b_ref, wd_ref, tgt_ref,
             h1_ref, u2_ref, up_ref, dh2_ref, dh2b_ref, loss_ref, acc_ref, halo_ref):
        i, j = pl.program_id(0), pl.program_id(1)

        @pl.when((i == 0) & (j == 0))
        def _():
            loss_ref[...] = jnp.zeros_like(loss_ref)

        @pl.when(j == 0)
        def _():
            h1 = x_ref[...] + _dot(ycn_ref[...], wo_ref[0:CONV_WIDTH, :]) + _dot(yan_ref[...], wo_ref[CONV_WIDTH:, :])
            h1_ref[...] = h1
            u2_ref[...] = (h1 * _rstd(h1) * g2_ref[...]).astype(BF16)
            acc_ref[...] = jnp.zeros_like(acc_ref)

        u2 = u2_ref[...]
        pre = []
        for s in range(2):
            up = _dot(u2, wu_ref[s])
            up_ref[s] = up
            halo = jnp.where(i == 0, 0.0, halo_ref[s, j])
            pre.append(_conv3(up, cw_ref[s], halo)[0] + b_ref[s])
            halo_ref[s, j] = up[tm - SUBLANES:]
        g, val = pre
        act = (g * jax.nn.sigmoid(g) * val).astype(BF16)
        acc_ref[...] += _dot(act, wd_ref[...])

        @pl.when(j == N_FFN_BLK - 1)
        def _():
            err = h1_ref[...] + acc_ref[...] - tgt_ref[...]
            loss_ref[...] += 0.5 * jnp.sum(err * err) / D_MODEL
            dh2 = err / D_MODEL
            dh2_ref[...] = dh2
            dh2b_ref[...] = dh2.astype(BF16)

    rows = lambda w: pl.BlockSpec((tm, w), lambda i, j: (i, 0))
    const = lambda shape: pl.BlockSpec(shape, lambda i, j: (0,) * len(shape))
    pair = lambda r, c: pl.BlockSpec((2, None, r, c), lambda i, j: (0, j, 0, 0))
    upb = pl.BlockSpec((2, None, tm, FFN_BLK), lambda i, j: (0, j, i, 0))
    return pl.pallas_call(
        body, name="ffn_fwd", grid=(n_t, N_FFN_BLK),
        in_specs=[rows(D_MODEL), rows(CONV_WIDTH), rows(ATTN_WIDTH), const((D_MODEL, D_MODEL)), const((1, D_MODEL)),
                  pair(D_MODEL, FFN_BLK), pair(3, FFN_BLK), pair(1, FFN_BLK),
                  pl.BlockSpec((None, FFN_BLK, D_MODEL), lambda i, j: (j, 0, 0)), rows(D_MODEL)],
        out_specs=[rows(D_MODEL), rows(D_MODEL), upb, rows(D_MODEL), rows(D_MODEL), const((SUBLANES, 128))],
        out_shape=[jax.ShapeDtypeStruct((SEQ, D_MODEL), F32), jax.ShapeDtypeStruct((SEQ, D_MODEL), BF16),
                   jax.ShapeDtypeStruct((2, N_FFN_BLK, SEQ, FFN_BLK), F32),
                   jax.ShapeDtypeStruct((SEQ, D_MODEL), F32), jax.ShapeDtypeStruct((SEQ, D_MODEL), BF16),
                   jax.ShapeDtypeStruct((SUBLANES, 128), F32)],
        scratch_shapes=[pltpu.VMEM((tm, D_MODEL), F32), pltpu.VMEM((2, N_FFN_BLK, SUBLANES, FFN_BLK), F32)],
        compiler_params=_params(("arbitrary", "arbitrary")),
    )(x, ycn, yan, w_out, g2, w_up, fcw, fcb, w_down, tgt)


def _ffn_bwd(dh2, dh2b, h1, g2, up, w_up, fcw, fcb, w_down):
    tm = 256
    n_t = SEQ // tm
    halo_blocks = tm // SUBLANES

    def body(dh2_ref, dh2b_ref, h1_ref, g2_ref, up_ref, uph_ref, wu_ref, cw_ref, b_ref, wd_ref,
             act_ref, dup_ref, dh1_ref, dh1b_ref, dfb_ref, dfcw_ref, dg2_ref, acc_ref, next_ref):
        i, j = pl.program_id(0), pl.program_id(1)
        first_tile = i == n_t - 1

        @pl.when((i == 0) & (j == 0))
        def _():
            dfb_ref[...] = jnp.zeros_like(dfb_ref)
            dfcw_ref[...] = jnp.zeros_like(dfcw_ref)
            dg2_ref[...] = jnp.zeros_like(dg2_ref)

        @pl.when(j == 0)
        def _():
            acc_ref[...] = jnp.zeros_like(acc_ref)

        ups, pre = [], []
        for s in range(2):
            up = up_ref[s]
            halo = jnp.where(first_tile, 0.0, uph_ref[s])
            p, up2, up1 = _conv3(up, cw_ref[s], halo)
            pre.append(p + b_ref[s])
            ups.append((up, up1, up2))
        g, val = pre
        sg = jax.nn.sigmoid(g)
        silu = g * sg
        act_ref[...] = (silu * val).astype(BF16)
        dact = _dot_nt(dh2b_ref[...], wd_ref[...])
        dpre = (dact * val * (sg * (1.0 + g * (1.0 - sg))), dact * silu)
        for s in range(2):
            d = dpre[s]
            u, u1, u2 = ups[s]
            dfb_ref[s, j] += jnp.sum(d, axis=0, keepdims=True)
            dfcw_ref[s, j, 0:1, :] += jnp.sum(d * u2, axis=0, keepdims=True)
            dfcw_ref[s, j, 1:2, :] += jnp.sum(d * u1, axis=0, keepdims=True)
            dfcw_ref[s, j, 2:3, :] += jnp.sum(d * u, axis=0, keepdims=True)
            nxt = jnp.where(i == 0, 0.0, next_ref[s, j])
            dup = _conv3_bwd_input(d, cw_ref[s], nxt).astype(BF16)
            next_ref[s, j] = d[:SUBLANES]
            dup_ref[s] = dup
            acc_ref[...] += _dot_nt(dup, wu_ref[s])

        @pl.when(j == N_FFN_BLK - 1)
        def _():
            dn, dgain = _rms_bwd(h1_ref[...], g2_ref[...], acc_ref[...])
            dh1 = dh2_ref[...] + dn
            dh1_ref[...] = dh1
            dh1b_ref[...] = dh1.astype(BF16)
            dg2_ref[...] += dgain

    rev = lambda i: n_t - 1 - i
    rows = lambda w: pl.BlockSpec((tm, w), lambda i, j: (rev(i), 0))
    const = lambda shape: pl.BlockSpec(shape, lambda i, j: (0,) * len(shape))
    pair = lambda r, c: pl.BlockSpec((2, None, r, c), lambda i, j: (0, j, 0, 0))
    upb = pl.BlockSpec((2, None, tm, FFN_BLK), lambda i, j: (0, j, rev(i), 0))
    halo = pl.BlockSpec((2, None, SUBLANES, FFN_BLK), lambda i, j: (0, j, jnp.maximum(rev(i) * halo_blocks - 1, 0), 0))
    return pl.pallas_call(
        body, name="ffn_bwd", grid=(n_t, N_FFN_BLK),
        in_specs=[rows(D_MODEL), rows(D_MODEL), rows(D_MODEL), const((1, D_MODEL)), upb, halo,
                  pair(D_MODEL, FFN_BLK), pair(3, FFN_BLK), pair(1, FFN_BLK),
                  pl.BlockSpec((None, FFN_BLK, D_MODEL), lambda i, j: (j, 0, 0))],
        out_specs=[pl.BlockSpec((None, tm, FFN_BLK), lambda i, j: (j, rev(i), 0)), upb, rows(D_MODEL), rows(D_MODEL),
                   const((2, N_FFN_BLK, 1, FFN_BLK)), const((2, N_FFN_BLK, 3, FFN_BLK)), const((1, D_MODEL))],
        out_shape=[jax.ShapeDtypeStruct((N_FFN_BLK, SEQ, FFN_BLK), BF16),
                   jax.ShapeDtypeStruct((2, N_FFN_BLK, SEQ, FFN_BLK), BF16), jax.ShapeDtypeStruct((SEQ, D_MODEL), F32),
                   jax.ShapeDtypeStruct((SEQ, D_MODEL), BF16), jax.ShapeDtypeStruct((2, N_FFN_BLK, 1, FFN_BLK), F32),
                   jax.ShapeDtypeStruct((2, N_FFN_BLK, 3, FFN_BLK), F32), jax.ShapeDtypeStruct((1, D_MODEL), F32)],
        scratch_shapes=[pltpu.VMEM((tm, D_MODEL), F32), pltpu.VMEM((2, N_FFN_BLK, SUBLANES, FFN_BLK), F32)],
        compiler_params=_params(("arbitrary", "arbitrary")),
    )(dh2, dh2b, h1, g2, up, up, w_up, fcw, fcb, w_down)


def _grad_tn(a_list, b, out_rows, name):
    n = len(a_list)
    ncol = b.shape[1]

    def body(*refs):
        a_refs, b_ref, o_ref = refs[:n], refs[n], refs[n + 1]
        j = pl.program_id(0)
        for k in range(n):
            @pl.when(j == k)
            def _(k=k):
                o_ref[...] = _dot_tn(a_refs[k][...], b_ref[...]).astype(BF16)

    full = lambda shape: pl.BlockSpec(shape, lambda j: (0,) * len(shape))
    return pl.pallas_call(
        body, name=name, grid=(n,),
        in_specs=[full((SEQ, out_rows))] * n + [full((SEQ, ncol))],
        out_specs=pl.BlockSpec((None, out_rows, ncol), lambda j: (j, 0, 0)),
        out_shape=jax.ShapeDtypeStruct((n, out_rows, ncol), BF16),
        compiler_params=_params(("arbitrary",)),
    )(*a_list, b)


def _grad_tn_blocked(a, b, name, a_is_blocked):
    nb = a.shape[0] if a_is_blocked else b.shape[0]
    a_w, b_w = a.shape[-1], b.shape[-1]

    def body(a_ref, b_ref, o_ref):
        o_ref[...] = _dot_tn(a_ref[...], b_ref[...]).astype(BF16)

    blocked = lambda w: pl.BlockSpec((None, SEQ, w), lambda k: (k, 0, 0))
    full = lambda w: pl.BlockSpec((SEQ, w), lambda k: (0, 0))
    return pl.pallas_call(
        body, name=name, grid=(nb,),
        in_specs=[blocked(a_w) if a_is_blocked else full(a_w), full(b_w) if a_is_blocked else blocked(b_w)],
        out_specs=pl.BlockSpec((None, a_w, b_w), lambda k: (k, 0, 0)),
        out_shape=jax.ShapeDtypeStruct((nb, a_w, b_w), BF16),
        compiler_params=_params(("arbitrary",)),
    )(a, b)


def _out_bwd(dh1b, w_out, y_attn, gattn):
    tm = 256
    n_t = SEQ // tm

    def body(dh_ref, wo_ref, y_ref, ga_ref, dycn_ref, dy_ref, dga_ref):
        @pl.when(pl.program_id(0) == 0)
        def _():
            dga_ref[...] = jnp.zeros_like(dga_ref)

        dycat = _dot_nt(dh_ref[...], wo_ref[...])
        dycn_ref[...] = dycat[:, :CONV_WIDTH]
        dy, dga = _rms_bwd(y_ref[...], ga_ref[...], dycat[:, CONV_WIDTH:])
        dy_ref[...] = dy
        dga_ref[...] += dga

    rows = lambda w: pl.BlockSpec((tm, w), lambda i: (i, 0))
    const = lambda shape: pl.BlockSpec(shape, lambda i: (0,) * len(shape))
    return pl.pallas_call(
        body, name="out_bwd", grid=(n_t,),
        in_specs=[rows(D_MODEL), const((D_MODEL, D_MODEL)), rows(ATTN_WIDTH), const((1, ATTN_WIDTH))],
        out_specs=[rows(CONV_WIDTH), rows(ATTN_WIDTH), const((1, ATTN_WIDTH))],
        out_shape=[jax.ShapeDtypeStruct((SEQ, CONV_WIDTH), F32), jax.ShapeDtypeStruct((SEQ, ATTN_WIDTH), F32),
                   jax.ShapeDtypeStruct((1, ATTN_WIDTH), F32)],
        compiler_params=_params(("arbitrary",)),
    )(dh1b, w_out, y_attn, gattn)


def _attn_bwd(qn, kn, v, dy, tbl, sinks, bkt):
    n_b = SEQ // BLK

    def body(q_ref, k_ref, v_ref, dy_ref, tbl_ref, sink_ref, bkt_ref,
             dq_ref, dk_ref, dv_ref, dtbl_ref, dsink_ref, bias_ref, dbias_ref, dsacc_ref):
        i = pl.program_id(0)

        @pl.when(i == 0)
        def _():
            _band_bias(tbl_ref, bkt_ref[...], bias_ref)
            dbias_ref[...] = jnp.zeros_like(dbias_ref)
            dsacc_ref[...] = jnp.zeros_like(dsacc_ref)
            dk_ref[...] = jnp.zeros_like(dk_ref)
            dv_ref[...] = jnp.zeros_like(dv_ref)

        kb, prev, cur = _band_rows(k_ref, i)
        vb, _, _ = _band_rows(v_ref, i)
        valid = _band_valid(i)
        q = q_ref[...]
        dy = dy_ref[...]
        lane = lax.broadcasted_iota(jnp.int32, (BLK, 128), 1)
        dqs, dks, dvs = [], [], []
        dsink = jnp.zeros((BLK, 128), F32)
        for h in range(N_HEADS):
            hk = h // GQA_GROUP
            kv = slice(HEAD_DIM * hk, HEAD_DIM * (hk + 1))
            qh = q[:, HEAD_DIM * h:HEAD_DIM * (h + 1)]
            probs, psink = _head_probs(qh, kb[:, kv], bias_ref[h], valid, sink_ref[0, h])
            doh = dy[:, HEAD_DIM * h:HEAD_DIM * (h + 1)].astype(BF16)
            dprobs = _dot_nt(doh, vb[:, kv])
            dvh = _dot_tn(probs.astype(BF16), doh)
            dsum = jnp.sum(probs * dprobs, axis=-1, keepdims=True)
            dlogits = probs * (dprobs - dsum)
            dsink = jnp.where(lane == h, -psink * dsum, dsink)
            dbias_ref[h] += dlogits
            ds = (dlogits * (HEAD_DIM ** -0.5)).astype(BF16)
            dqs.append(_dot(ds, kb[:, kv]))
            dkh = _dot_tn(ds, qh)
            if h % GQA_GROUP == 0:
                dks.append(dkh)
                dvs.append(dvh)
            else:
                dks[hk] = dks[hk] + dkh
                dvs[hk] = dvs[hk] + dvh
        dq_ref[...] = jnp.concatenate(dqs, axis=-1)
        dsacc_ref[...] += dsink
        dkb = jnp.concatenate(dks, axis=-1)
        dvb = jnp.concatenate(dvs, axis=-1)
        dk_ref[pl.ds(prev, BLK), :] += dkb[:BLK]
        dk_ref[pl.ds(cur, BLK), :] += dkb[BLK:]
        dv_ref[pl.ds(prev, BLK), :] += dvb[:BLK]
        dv_ref[pl.ds(cur, BLK), :] += dvb[BLK:]

        @pl.when(i == n_b - 1)
        def _():
            dsink_ref[...] = jnp.sum(dsacc_ref[...], axis=0, keepdims=True)
            bkt = bkt_ref[...]
            row8 = lax.broadcasted_iota(jnp.int32, (N_HEADS, 128), 0)
            lane8 = lax.broadcasted_iota(jnp.int32, (N_HEADS, 128), 1)
            acc = jnp.zeros((N_HEADS, 128), F32)
            for h in range(N_HEADS):
                dbh = dbias_ref[h]
                for b in range(NUM_BUCKETS):
                    acc = jnp.where((row8 == h) & (lane8 == b), jnp.sum(jnp.where(bkt == b, dbh, 0.0)), acc)
            dtbl_ref[...] = acc

    const = lambda shape: pl.BlockSpec(shape, lambda i: (0,) * len(shape))
    rows = lambda w: pl.BlockSpec((BLK, w), lambda i: (i, 0))
    smem = pl.BlockSpec(memory_space=pltpu.SMEM)
    return pl.pallas_call(
        body, name="attn_bwd", grid=(n_b,),
        in_specs=[rows(ATTN_WIDTH), const((SEQ, KV_WIDTH)), const((SEQ, KV_WIDTH)), rows(ATTN_WIDTH), smem, smem,
                  const((BLK, 2 * BLK))],
        out_specs=[rows(ATTN_WIDTH), const((SEQ, KV_WIDTH)), const((SEQ, KV_WIDTH)), const((N_HEADS, 128)), const((1, 128))],
        out_shape=[jax.ShapeDtypeStruct((SEQ, ATTN_WIDTH), F32), jax.ShapeDtypeStruct((SEQ, KV_WIDTH), F32),
                   jax.ShapeDtypeStruct((SEQ, KV_WIDTH), F32), jax.ShapeDtypeStruct((N_HEADS, 128), F32),
                   jax.ShapeDtypeStruct((1, 128), F32)],
        scratch_shapes=[pltpu.VMEM((N_HEADS, BLK, 2 * BLK), F32), pltpu.VMEM((N_HEADS, BLK, 2 * BLK), F32),
                        pltpu.VMEM((BLK, 128), F32)],
        compiler_params=_params(("arbitrary",)),
    )(qn, kn, v, dy, tbl, sinks, bkt)


def _mix_in_bwd(x, dh1, proj, dycn, dqn, dkn, dv, w_in_t, conv_w, g1, gq, gk, gconv):
    tm = 256
    n_t = SEQ // tm
    halo_blocks = tm // SUBLANES

    def body(x_ref, dh1_ref, proj_ref, halo_ref, dycn_ref, dqn_ref, dkn_ref, dv_ref, w_ref, cw_ref,
             g1_ref, gq_ref, gk_ref, gc_ref,
             dx_ref, dproj_ref, u1_ref, dcw_ref, dgc_ref, dgq_ref, dgk_ref, dg1_ref, next_ref):
        i = pl.program_id(0)
        first_tile = i == n_t - 1

        @pl.when(i == 0)
        def _():
            for r in (dcw_ref, dgc_ref, dgq_ref, dgk_ref, dg1_ref, next_ref):
                r[...] = jnp.zeros_like(r)

        proj = proj_ref[...]
        hp = halo_ref[...]
        gate_b = proj[:, 0:CONV_WIDTH]
        gate_c = proj[:, CONV_WIDTH:2 * CONV_WIDTH]
        hc = proj[:, 2 * CONV_WIDTH:3 * CONV_WIDTH]
        a = gate_c * hc
        a_halo = jnp.where(first_tile, 0.0, hp[:, CONV_WIDTH:2 * CONV_WIDTH] * hp[:, 2 * CONV_WIDTH:3 * CONV_WIDTH])
        cw = cw_ref[...]
        cv, a2, a1 = _conv3(a, cw, a_halo)
        dyc, dgc = _rms_bwd(gate_b * cv, gc_ref[...], dycn_ref[...])
        dgc_ref[...] += dgc
        dcv = dyc * gate_b
        dcw_ref[...] += jnp.concatenate(
            [jnp.sum(dcv * a2, axis=0, keepdims=True), jnp.sum(dcv * a1, axis=0, keepdims=True),
             jnp.sum(dcv * a, axis=0, keepdims=True)], axis=0)
        da = _conv3_bwd_input(dcv, cw, next_ref[...])
        next_ref[...] = dcv[:SUBLANES]
        q0 = 3 * CONV_WIDTH
        k0 = q0 + ATTN_WIDTH
        dq, dgq = _head_norm_bwd(proj[:, q0:k0], gq_ref[...], dqn_ref[...], N_HEADS)
        dk, dgk = _head_norm_bwd(proj[:, k0:k0 + KV_WIDTH], gk_ref[...], dkn_ref[...], 2)
        dgq_ref[...] += dgq
        dgk_ref[...] += dgk
        dproj = jnp.concatenate([dyc * cv, da * hc, da * gate_c, dq, dk, dv_ref[...]], axis=-1).astype(BF16)
        dproj_ref[...] = dproj
        du1 = _dot(dproj, w_ref[...])
        xv = x_ref[...]
        dn, dg1 = _rms_bwd(xv, g1_ref[...], du1)
        dx_ref[...] = dh1_ref[...] + dn
        dg1_ref[...] += dg1
        u1_ref[...] = (xv * _rstd(xv) * g1_ref[...]).astype(BF16)

    rev = lambda i: n_t - 1 - i
    rows = lambda w: pl.BlockSpec((tm, w), lambda i: (rev(i), 0))
    const = lambda shape: pl.BlockSpec(shape, lambda i: (0,) * len(shape))
    halo = pl.BlockSpec((SUBLANES, IN_WIDTH), lambda i: (jnp.maximum(rev(i) * halo_blocks - 1, 0), 0))
    return pl.pallas_call(
        body, name="mix_in_bwd", grid=(n_t,),
        in_specs=[rows(D_MODEL), rows(D_MODEL), rows(IN_WIDTH), halo, rows(CONV_WIDTH), rows(ATTN_WIDTH), rows(KV_WIDTH),
                  rows(KV_WIDTH), const((IN_WIDTH, D_MODEL)), const((3, CONV_WIDTH)), const((1, D_MODEL)),
                  const((1, HEAD_DIM)), const((1, HEAD_DIM)), const((1, CONV_WIDTH))],
        out_specs=[rows(D_MODEL), rows(IN_WIDTH), rows(D_MODEL), const((3, CONV_WIDTH)), const((1, CONV_WIDTH)),
                   const((1, HEAD_DIM)), const((1, HEAD_DIM)), const((1, D_MODEL))],
        out_shape=[jax.ShapeDtypeStruct((SEQ, D_MODEL), F32), jax.ShapeDtypeStruct((SEQ, IN_WIDTH), BF16),
                   jax.ShapeDtypeStruct((SEQ, D_MODEL), BF16), jax.ShapeDtypeStruct((3, CONV_WIDTH), F32),
                   jax.ShapeDtypeStruct((1, CONV_WIDTH), F32), jax.ShapeDtypeStruct((1, HEAD_DIM), F32),
                   jax.ShapeDtypeStruct((1, HEAD_DIM), F32), jax.ShapeDtypeStruct((1, D_MODEL), F32)],
        scratch_shapes=[pltpu.VMEM((SUBLANES, CONV_WIDTH), F32)],
        compiler_params=_params(("arbitrary",)),
    )(x, dh1, proj, proj, dycn, dqn, dkn, dv, w_in_t, conv_w, g1, gq, gk, gconv)


def _grad_w_in(dproj, u1):
    bw = 768

    def body(a_ref, b_ref, o_ref):
        o_ref[...] = _dot_tn(a_ref[...], b_ref[...]).astype(BF16)

    return pl.pallas_call(
        body, name="grad_w_in", grid=(IN_WIDTH // bw,),
        in_specs=[pl.BlockSpec((SEQ, bw), lambda k: (0, k)), pl.BlockSpec((SEQ, D_MODEL), lambda k: (0, 0))],
        out_specs=pl.BlockSpec((bw, D_MODEL), lambda k: (k, 0)),
        out_shape=jax.ShapeDtypeStruct((IN_WIDTH, D_MODEL), BF16),
        compiler_params=_params(("arbitrary",)),
    )(dproj, u1)


def _adamw_math(w, g, m, v):
    m = ADAM_B1 * m + (1.0 - ADAM_B1) * g
    v = ADAM_B2 * v + (1.0 - ADAM_B2) * (g * g)
    m_hat = m / (1.0 - ADAM_B1 ** ADAM_STEP)
    v_hat = v / (1.0 - ADAM_B2 ** ADAM_STEP)
    return -ADAM_LR * (m_hat / (jnp.sqrt(v_hat) + ADAM_EPS) + ADAM_WD * w), m, v


def _adamw_small(w, m, v, recv):
    def body(w_ref, m_ref, v_ref, r_ref, g_o, d_o, m_o, v_o):
        g = r_ref[0]
        for s in range(1, N_DEV):
            g = g + r_ref[s]
        g_o[...] = g
        d_o[...], m_o[...], v_o[...] = _adamw_math(w_ref[...], g, m_ref[...], v_ref[...])

    return pl.pallas_call(
        body, name="adamw_small", out_shape=[jax.ShapeDtypeStruct(w.shape, F32)] * 4,
        compiler_params=pltpu.CompilerParams(vmem_limit_bytes=VMEM_LIMIT),
    )(w, m, v, recv)


def _adamw(w, m, v, part, recv, chip, name, row_blocks=1, transpose=False):
    rows, cols = w.shape
    rb = rows // row_blocks

    def body(chip_ref, w_ref, m_ref, v_ref, p_ref, r_ref, g_o, d_o, m_o, v_o):
        g = p_ref[...].astype(F32)
        for s in range(3):
            g = g + r_ref[s].astype(F32)
        if transpose:
            g = g.T
        g_o[...] = g
        d_o[...], m_o[...], v_o[...] = _adamw_math(w_ref[...], g, m_ref[...], v_ref[...])

    blk = pl.BlockSpec((rb, cols), lambda i, chip_ref: (i, 0))
    if transpose:
        pblk = pl.BlockSpec((None,) + part.shape[1:], lambda i, chip_ref: (chip_ref[0], 0, 0))
        rblk = pl.BlockSpec(recv.shape, lambda i, chip_ref: (0, 0, 0))
    else:
        pblk = pl.BlockSpec((None, rb, cols), lambda i, chip_ref: (chip_ref[0], i, 0))
        rblk = pl.BlockSpec((3, rb, cols), lambda i, chip_ref: (0, i, 0))
    return pl.pallas_call(
        body, name=name,
        grid_spec=pltpu.PrefetchScalarGridSpec(num_scalar_prefetch=1, grid=(row_blocks,),
                                               in_specs=[blk, blk, blk, pblk, rblk], out_specs=[blk] * 4),
        out_shape=[jax.ShapeDtypeStruct((rows, cols), F32)] * 4,
        compiler_params=_params(("arbitrary",)),
    )(chip, w, m, v, part, recv)


_SMALL_NAMES = ("norm_mix_g", "norm_ffn_g", "out_norm_conv_g", "out_norm_attn_g", "ffn_conv_b", "q_norm_g", "k_norm_g",
                "sinks", "rel_bias_table")


def _pack_small(vals, extra=None):
    parts = [vals[k].reshape(-1) for k in _SMALL_NAMES] + ([] if extra is None else [extra.reshape(1)])
    flat = jnp.concatenate(parts)
    return jnp.pad(flat, (0, SMALL_ROWS * SMALL_COLS - flat.shape[0])).reshape(SMALL_ROWS, SMALL_COLS)


def _unpack_small(packed, like):
    flat = packed.reshape(-1)
    out, off = {}, 0
    for k in _SMALL_NAMES:
        size = like[k].size
        out[k] = flat[off:off + size].reshape(like[k].shape)
        off += size
    return out, flat[off]


def _behind(value, token):
    return lax.optimization_barrier((value, token))[0]


def kernel(x, norm_mix_g, w_in, conv_w, q_norm_g, k_norm_g, rel_bias_table, sinks, out_norm_conv_g, out_norm_attn_g, w_out, norm_ffn_g, w_up, ffn_conv_w, ffn_conv_b, w_down, loss_target, m_norm_mix_g, m_w_in, m_conv_w, m_q_norm_g, m_k_norm_g, m_rel_bias_table, m_sinks, m_out_norm_conv_g, m_out_norm_attn_g, m_w_out, m_norm_ffn_g, m_w_up, m_ffn_conv_w, m_ffn_conv_b, m_w_down, v_norm_mix_g, v_w_in, v_conv_w, v_q_norm_g, v_k_norm_g, v_rel_bias_table, v_sinks, v_out_norm_conv_g, v_out_norm_attn_g, v_w_out, v_norm_ffn_g, v_w_up, v_ffn_conv_w, v_ffn_conv_b, v_w_down):
    p = dict(norm_mix_g=norm_mix_g, w_in=w_in, conv_w=conv_w, q_norm_g=q_norm_g, k_norm_g=k_norm_g,
             rel_bias_table=rel_bias_table, sinks=sinks, out_norm_conv_g=out_norm_conv_g, out_norm_attn_g=out_norm_attn_g,
             w_out=w_out, norm_ffn_g=norm_ffn_g, w_up=w_up, ffn_conv_w=ffn_conv_w, ffn_conv_b=ffn_conv_b, w_down=w_down)
    m = dict(norm_mix_g=m_norm_mix_g, w_in=m_w_in, conv_w=m_conv_w, q_norm_g=m_q_norm_g, k_norm_g=m_k_norm_g,
             rel_bias_table=m_rel_bias_table, sinks=m_sinks, out_norm_conv_g=m_out_norm_conv_g,
             out_norm_attn_g=m_out_norm_attn_g, w_out=m_w_out, norm_ffn_g=m_norm_ffn_g, w_up=m_w_up,
             ffn_conv_w=m_ffn_conv_w, ffn_conv_b=m_ffn_conv_b, w_down=m_w_down)
    v = dict(norm_mix_g=v_norm_mix_g, w_in=v_w_in, conv_w=v_conv_w, q_norm_g=v_q_norm_g, k_norm_g=v_k_norm_g,
             rel_bias_table=v_rel_bias_table, sinks=v_sinks, out_norm_conv_g=v_out_norm_conv_g,
             out_norm_attn_g=v_out_norm_attn_g, w_out=v_w_out, norm_ffn_g=v_norm_ffn_g, w_up=v_w_up,
             ffn_conv_w=v_ffn_conv_w, ffn_conv_b=v_ffn_conv_b, w_down=v_w_down)

    xs, tgt = x[0], loss_target[0]
    g1, g2, gq, gk, gconv, gattn = norm_mix_g, norm_ffn_g, q_norm_g, k_norm_g, out_norm_conv_g, out_norm_attn_g
    core = lax.axis_index("c").astype(jnp.int32).reshape(1)
    chip = (2 * lax.axis_index("x") + lax.axis_index("y")).astype(jnp.int32).reshape(1)
    bkt = jnp.asarray(_bucket_map())

    wi_b, wo_b, wu_b, wd_b = _cast_shards(w_in[0], w_out[0], w_up[0], w_down[0])
    finish_a, token_a = _all_gather_split([wi_b, conv_w[0]], "mixer", None)
    finish_b, token_b = _all_gather_split([wo_b, wu_b, wd_b, ffn_conv_w[0]], "ffn", token_a)
    wi_g, cw_g = finish_a(token_b)
    w_in_t = wi_g.reshape(IN_WIDTH, D_MODEL)
    conv_w_f = jnp.transpose(cw_g, (1, 0, 2)).reshape(3, CONV_WIDTH)

    proj, ycn, qn, kn, vv = _mix_in_fwd(xs, g1, w_in_t, conv_w_f, gq, gk, gconv)
    y_attn, yan = _attn_fwd(qn, kn, vv, rel_bias_table, sinks, bkt, gattn)
    wo_g, wu_g, wd_g, fcw_g = finish_b(yan)
    w_out_f = wo_g.reshape(D_MODEL, D_MODEL)
    w_down_f = wd_g.reshape(N_FFN_BLK, FFN_BLK, D_MODEL)
    w_up_f = wu_g.reshape(2, N_FFN_BLK, D_MODEL, FFN_BLK)
    fcw_f = fcw_g.reshape(2, N_FFN_BLK, 3, FFN_BLK)
    fcb = ffn_conv_b.reshape(2, N_FFN_BLK, 1, FFN_BLK)
    h1, u2, up, dh2, dh2b, loss_acc = _ffn_fwd(xs, ycn, yan, w_out_f, g2, w_up_f, fcw_f, fcb, w_down_f, tgt)

    act, dup, dh1, dh1b, dfb, dfcw, dg2 = _ffn_bwd(dh2, dh2b, h1, g2, up, w_up_f, fcw_f, fcb, w_down_f)
    dw_down = _grad_tn_blocked(act, dh2b, "grad_w_down", a_is_blocked=True).reshape(N_DEV, D_FF // N_DEV, D_MODEL)
    dw_up = _grad_tn_blocked(u2, dup.reshape(N_DEV, SEQ, FFN_BLK), "grad_w_up", a_is_blocked=False)
    out_bwd = {}

    def behind_ffn(token):
        out_bwd["r"] = _out_bwd(_behind(dh1b, token), w_out_f, y_attn, gattn)
        return out_bwd["r"][0]

    finish_ffn, token_ffn = _reduce_scatter_split([dw_down, dw_up, dfcw.reshape(N_DEV, 3, FFN_BLK)], "ffn", core, behind_ffn)
    dycn, dy_attn, dgattn = out_bwd["r"]
    dw_out = _grad_tn([ycn, yan], dh1b, CONV_WIDTH, "grad_w_out").reshape(N_DEV, D_MODEL // N_DEV, D_MODEL)
    dqn, dkn, dv, dtbl_t, dsinks = _attn_bwd(qn, kn, vv, _behind(dy_attn, token_ffn), rel_bias_table, sinks, bkt)
    dx, dproj, u1, dcw, dgconv, dgq, dgk, dg1 = _mix_in_bwd(xs, dh1, proj, dycn, dqn, dkn, dv, w_in_t, conv_w_f,
                                                             g1, gq, gk, gconv)
    small = dict(norm_mix_g=dg1, norm_ffn_g=dg2, out_norm_conv_g=dgconv, out_norm_attn_g=dgattn, ffn_conv_b=dfb,
                 q_norm_g=dgq, k_norm_g=dgk, sinks=dsinks[:, :N_HEADS], rel_bias_table=dtbl_t[:, :NUM_BUCKETS].T)
    packed = _pack_small(small, loss_acc[0, 0])
    plan_s, slots_s = _broadcast_plan()
    s_sem, r_sem, src_s, land_s, token_s = _split_start(
        "gather_small_start", [packed], [jnp.broadcast_to(packed[None], (N_DEV,) + packed.shape)], plan_s, None)
    dw_in_t = _grad_w_in(_behind(dproj, token_s), u1).reshape(N_DEV, IN_WIDTH // N_DEV, D_MODEL)
    dcw_b = jnp.transpose(dcw.reshape(3, N_DEV, CONV_WIDTH // N_DEV), (1, 0, 2))
    adam = {}
    ffn_parts = {}

    def behind_mixer(token):
        (p_wd, p_wu, p_fcw), (r_wd, r_wu, r_fcw) = finish_ffn(token)
        ffn_parts["w_up"] = (p_wu, r_wu)
        adam["w_down"] = _adamw(w_down[0], m_w_down[0], v_w_down[0], p_wd, r_wd, chip, "adamw_w_down", row_blocks=2)
        adam["ffn_conv_w"] = _adamw(ffn_conv_w[0], m_ffn_conv_w[0], v_ffn_conv_w[0], p_fcw, r_fcw, chip, "adamw_ffn_conv_w")
        return adam["w_down"][0]

    finish_mixer, token_mixer = _reduce_scatter_split([dw_in_t, dw_out, dcw_b], "mixer", core, behind_mixer)
    adam["w_up"] = _adamw(_behind(w_up[0], token_mixer), m_w_up[0], v_w_up[0], *ffn_parts["w_up"], chip, "adamw_w_up",
                          row_blocks=4)
    _, (r_small,) = _split_wait("gather_small_wait", s_sem, r_sem, src_s, land_s, plan_s, slots_s, adam["w_up"][0])
    small_like = {k: p[k] for k in _SMALL_NAMES}
    sm = _adamw_small(_pack_small(small_like), _pack_small({k: m[k] for k in _SMALL_NAMES}),
                      _pack_small({k: v[k] for k in _SMALL_NAMES}), r_small)
    (p_wi, p_wo, p_cw), (r_wi, r_wo, r_cw) = finish_mixer(sm[0])
    adam["w_in"] = _adamw(w_in[0], m_w_in[0], v_w_in[0], p_wi, r_wi, chip, "adamw_w_in", transpose=True)
    adam["w_out"] = _adamw(w_out[0], m_w_out[0], v_w_out[0], p_wo, r_wo, chip, "adamw_w_out")
    adam["conv_w"] = _adamw(conv_w[0], m_conv_w[0], v_conv_w[0], p_cw, r_cw, chip, "adamw_conv_w")

    res = {k: tuple(a[None] for a in t) for k, t in adam.items()}
    unpacked = [_unpack_small(a, small_like) for a in sm]
    loss = unpacked[0][1]
    for k in _SMALL_NAMES:
        res[k] = tuple(u[0][k] for u in unpacked)
    order = ("norm_mix_g", "w_in", "conv_w", "q_norm_g", "k_norm_g", "rel_bias_table", "sinks", "out_norm_conv_g",
             "out_norm_attn_g", "w_out", "norm_ffn_g", "w_up", "ffn_conv_w", "ffn_conv_b", "w_down")
    return (loss, dx[None], *[res[k][0] for k in order], *[res[k][1] for k in order],
            *[res[k][2] for k in order], *[res[k][3] for k in order])
```

```python
import functools
import math

import numpy as np
import jax
import jax.numpy as jnp
from jax import lax
from jax.experimental import pallas as pl
from jax.experimental.pallas import tpu as pltpu

F32 = jnp.float32
BF16 = jnp.bfloat16

SEQ = 2048
D_MODEL = 1024
CONV_WIDTH = 512
ATTN_WIDTH = 512
KV_WIDTH = 128
HEAD_DIM = 64
N_HEADS = 8
GQA_GROUP = 4
IN_WIDTH = 2304
D_FF = 2816
BLK = 128
NUM_BUCKETS = 32
EPS = 1e-6
NEG_INF = -1e30
ADAM_LR = 0.001
ADAM_B1 = 0.9
ADAM_B2 = 0.999
ADAM_EPS = 1e-08
ADAM_WD = 0.01
ADAM_STEP = 10

N_DEV = 8
FFN_BLK = 2 * D_FF // N_DEV
N_FFN_BLK = D_FF // FFN_BLK
SUBLANES = 8
VMEM_LIMIT = 56 * 1024 * 1024
SMALL_ROWS, SMALL_COLS = 16, 1024

_MESH = pl.DeviceIdType.MESH
_ANY = pl.BlockSpec(memory_space=pl.ANY)


def _params(sem):
    return pltpu.CompilerParams(dimension_semantics=sem, vmem_limit_bytes=VMEM_LIMIT)


def _ordered_behind(body, pos, after):
    if after is None:
        return body, [], []
    return (lambda *refs: body(*refs[:pos], *refs[pos + 1:])), [_ANY], [after]


def _dot(a, b):
    return jnp.dot(a, b, preferred_element_type=F32)


def _dot_nt(a, b):
    return lax.dot_general(a, b, (((1,), (1,)), ((), ())), preferred_element_type=F32)


def _dot_tn(a, b):
    return lax.dot_general(a, b, (((0,), (0,)), ((), ())), preferred_element_type=F32)


def _shift_down(x, s, halo):
    r = pltpu.roll(x, s, axis=0)
    hr = pltpu.roll(halo, s, axis=0)
    row = lax.broadcasted_iota(jnp.int32, halo.shape, 0)
    top = jnp.where(row < s, hr, r[:SUBLANES])
    return jnp.concatenate([top, r[SUBLANES:]], axis=0)


def _shift_up(x, s, halo):
    n = x.shape[0]
    r = pltpu.roll(x, n - s, axis=0)
    hr = pltpu.roll(halo, SUBLANES - s, axis=0)
    row = lax.broadcasted_iota(jnp.int32, halo.shape, 0)
    bot = jnp.where(row >= SUBLANES - s, hr, r[n - SUBLANES:])
    return jnp.concatenate([r[:n - SUBLANES], bot], axis=0)


def _conv3(x, w, halo):
    x2 = _shift_down(x, 2, halo)
    x1 = _shift_down(x, 1, halo)
    return x2 * w[0:1] + x1 * w[1:2] + x * w[2:3], x2, x1


def _conv3_bwd_input(dy, w, halo_next):
    return dy * w[2:3] + _shift_up(dy, 1, halo_next) * w[1:2] + _shift_up(dy, 2, halo_next) * w[0:1]


def _rstd(x):
    return lax.rsqrt(jnp.mean(x * x, axis=-1, keepdims=True) + EPS)


def _rms_bwd(x, g, dy):
    r = _rstd(x)
    n = x * r
    dn = dy * g
    dx = r * (dn - n * jnp.mean(dn * n, axis=-1, keepdims=True))
    return dx, jnp.sum(dy * n, axis=0, keepdims=True)


def _head_norm(x, g, heads):
    parts = []
    for h in range(heads):
        xh = x[:, HEAD_DIM * h:HEAD_DIM * (h + 1)]
        parts.append(xh * _rstd(xh) * g)
    return jnp.concatenate(parts, axis=-1)


def _head_norm_bwd(x, g, dy, heads):
    dxs, dg = [], jnp.zeros((1, HEAD_DIM), F32)
    for h in range(heads):
        sl = slice(HEAD_DIM * h, HEAD_DIM * (h + 1))
        dxh, dgh = _rms_bwd(x[:, sl], g, dy[:, sl])
        dxs.append(dxh)
        dg = dg + dgh
    return jnp.concatenate(dxs, axis=-1), dg


def _bucket_map():
    q = np.arange(BLK)[:, None]
    j = np.arange(2 * BLK)[None, :]
    n = np.maximum(q + BLK - j, 0)
    nf = np.maximum(n, 1).astype(np.float32)
    max_exact = NUM_BUCKETS // 2
    large = max_exact + (np.log(nf / max_exact) / math.log(BLK / max_exact) * (NUM_BUCKETS - max_exact)).astype(np.int32)
    large = np.minimum(large, NUM_BUCKETS - 1)
    return np.where(n < max_exact, n, large).astype(np.int32)


def _coords():
    return lax.axis_index("x"), lax.axis_index("y"), lax.axis_index("c")


def _lin(px, py, pc):
    return 4 * px + 2 * py + pc


_HBM = pl.BlockSpec(memory_space=pltpu.HBM)
_SEM = pl.BlockSpec(memory_space=pltpu.SEMAPHORE)
_EFFECT = pltpu.SideEffectType.DATAFLOW_SIDE_EFFECTING


def _in_hbm(a):
    return pltpu.with_memory_space_constraint(a, pltpu.HBM)


def _split_start(name, srcs, lands, plan, after):
    ns, nl = len(srcs), len(lands)
    n_copies = len(plan(0, 0, 0))
    n_after = 0 if after is None else 1

    def body(*refs):
        src_refs, land_refs = refs[:ns + nl], refs[ns:ns + nl]
        send_sems, recv_sems = refs[ns + nl + n_after], refs[ns + nl + n_after + 1]
        token = refs[-1]
        for k, (a, s_slot, l, d_slot, dev) in enumerate(plan(*_coords())):
            src = src_refs[a] if s_slot is None else src_refs[a].at[s_slot]
            pltpu.make_async_remote_copy(src_ref=src, dst_ref=land_refs[l].at[d_slot], send_sem=send_sems.at[k],
                                         recv_sem=recv_sems.at[k], device_id=dev, device_id_type=_MESH).start()
        token[...] = jnp.zeros_like(token)

    arrs = list(srcs) + list(lands)
    out = pl.pallas_call(
        body, name=name,
        out_shape=(pltpu.SemaphoreType.DMA((n_copies,)), pltpu.SemaphoreType.DMA((n_copies,)),
                   *[pltpu.HBM(a.shape, a.dtype) for a in arrs], jax.ShapeDtypeStruct((SUBLANES, 128), F32)),
        in_specs=[_HBM] * (ns + nl) + [_ANY] * n_after,
        out_specs=(_SEM, _SEM, *[_HBM] * (ns + nl), pl.BlockSpec(memory_space=pltpu.VMEM)),
        input_output_aliases={i: 2 + i for i in range(ns + nl)},
        compiler_params=pltpu.CompilerParams(has_side_effects=_EFFECT),
    )(*[_in_hbm(a) for a in arrs], *([] if after is None else [after]))
    return out[0], out[1], list(out[2:2 + ns]), list(out[2 + ns:2 + ns + nl]), out[-1]


def _split_wait(name, send_sems, recv_sems, srcs, lands, plan, recv_slots, after):
    ns, nl = len(srcs), len(lands)

    def body(*refs):
        src_refs, land_refs = refs[:ns + nl], refs[ns:ns + nl]
        send_sems, recv_sems = refs[ns + nl], refs[ns + nl + 1]
        coords = _coords()
        slots = recv_slots(*coords)
        for k, (a, s_slot, l, _, dev) in enumerate(plan(*coords)):
            src = src_refs[a] if s_slot is None else src_refs[a].at[s_slot]
            cp = pltpu.make_async_remote_copy(src_ref=src, dst_ref=land_refs[l].at[slots[k]], send_sem=send_sems.at[k],
                                              recv_sem=recv_sems.at[k], device_id=dev, device_id_type=_MESH)
            cp.wait_send()
            cp.wait_recv()

    arrs = list(srcs) + list(lands)
    out = pl.pallas_call(
        body, name=name,
        out_shape=tuple(pltpu.HBM(a.shape, a.dtype) for a in arrs),
        in_specs=[_HBM] * (ns + nl) + [_SEM, _SEM, _ANY],
        out_specs=tuple([_HBM] * (ns + nl)),
        input_output_aliases={i: i for i in range(ns + nl)},
        compiler_params=pltpu.CompilerParams(has_side_effects=_EFFECT),
    )(*arrs, send_sems, recv_sems, after)
    return list(out[:ns]), list(out[ns:])


def _chips(x, y):
    return [(1 - x, y), (x, 1 - y), (1 - x, 1 - y)]


def _gather_plan_ici(n):
    def plan(x, y, c):
        me = _lin(x, y, c)
        out = []
        for a in range(n):
            out.append((a, me, a, me, (x, y, 1 - c)))
            out += [(a, me, a, me, (cx, cy, c)) for cx, cy in _chips(x, y)]
        return out

    def recv_slots(x, y, c):
        out = []
        for _ in range(n):
            out.append(_lin(x, y, 1 - c))
            out += [_lin(cx, cy, c) for cx, cy in _chips(x, y)]
        return out

    return plan, recv_slots


def _gather_plan_d2d(n):
    def plan(x, y, c):
        return [(a, _lin(cx, cy, c), a, _lin(cx, cy, c), (x, y, 1 - c)) for a in range(n) for cx, cy in _chips(x, y)]

    def recv_slots(x, y, c):
        return [_lin(cx, cy, 1 - c) for _ in range(n) for cx, cy in _chips(x, y)]

    return plan, recv_slots


def _all_gather_split(lands, tag, after):
    n = len(lands)
    plan1, slots1 = _gather_plan_ici(n)
    s1, r1, _, lands, token = _split_start(f"gather_{tag}_ici_start", [], lands, plan1, after)

    def finish(after):
        _, got = _split_wait(f"gather_{tag}_ici_wait", s1, r1, [], lands, plan1, slots1, after)
        plan2, slots2 = _gather_plan_d2d(n)
        s2, r2, _, got, token2 = _split_start(f"gather_{tag}_d2d_start", [], got, plan2, None)
        return _split_wait(f"gather_{tag}_d2d_wait", s2, r2, [], got, plan2, slots2, token2)[1]

    return finish, token


_CHIP_LIST = ((0, 0), (0, 1), (1, 0), (1, 1))


def _reduce_plan_d2d(n):
    def plan(x, y, c):
        return [(a, _lin(qx, qy, 1 - c), a, q, (x, y, 1 - c)) for a in range(n) for q, (qx, qy) in enumerate(_CHIP_LIST)]

    def recv_slots(x, y, c):
        return [q for _ in range(n) for q in range(4)]

    return plan, recv_slots


def _reduce_plan_ici(n):
    def plan(x, y, c):
        return [(a, 2 * cx + cy, a, j, (cx, cy, c)) for a in range(n) for j, (cx, cy) in enumerate(_chips(x, y))]

    def recv_slots(x, y, c):
        return [j for _ in range(n) for j in range(3)]

    return plan, recv_slots


def _broadcast_plan():
    def peers(x, y, c):
        return [(1 - x if r & 4 else x, 1 - y if r & 2 else y, 1 - c if r & 1 else c) for r in range(1, N_DEV)]

    def plan(x, y, c):
        return [(0, None, 0, _lin(x, y, c), peer) for peer in peers(x, y, c)]

    def recv_slots(x, y, c):
        return [_lin(*peer) for peer in peers(x, y, c)]

    return plan, recv_slots


def _chip_partial(grads, recvd, core, name):
    n = len(grads)

    def body(c_ref, *refs):
        for a in range(n):
            g_ref, r_ref, o_ref = refs[a], refs[n + a], refs[2 * n + a]
            o_ref[...] = (g_ref[...].astype(F32) + r_ref[...].astype(F32)).astype(o_ref.dtype)

    blk = lambda a, own: pl.BlockSpec((None,) + a.shape[1:],
                                      (lambda q, c_ref: (2 * q + c_ref[0], 0, 0)) if own else (lambda q, c_ref: (q, 0, 0)))
    return pl.pallas_call(
        body, name=name,
        grid_spec=pltpu.PrefetchScalarGridSpec(
            num_scalar_prefetch=1, grid=(4,),
            in_specs=[blk(a, True) for a in grads] + [blk(a, False) for a in recvd],
            out_specs=[blk(a, False) for a in recvd]),
        out_shape=[jax.ShapeDtypeStruct(a.shape, a.dtype) for a in recvd],
        compiler_params=_params(("arbitrary",)),
    )(core, *grads, *recvd)


def _reduce_scatter_split(grads, tag, core, behind):
    n = len(grads)
    plan1, slots1 = _reduce_plan_d2d(n)
    lands1 = [lax.empty((4,) + a.shape[1:], a.dtype) for a in grads]
    s1, r1, srcs1, lands1, token1 = _split_start(f"reduce_{tag}_d2d_start", grads, lands1, plan1, None)
    own, got = _split_wait(f"reduce_{tag}_d2d_wait", s1, r1, srcs1, lands1, plan1, slots1, behind(token1))
    parts = _chip_partial(own, got, core, f"reduce_{tag}_partial")
    plan2, slots2 = _reduce_plan_ici(n)
    lands2 = [lax.empty((3,) + a.shape[1:], a.dtype) for a in grads]
    s2, r2, srcs2, lands2, token2 = _split_start(f"reduce_{tag}_ici_start", parts, lands2, plan2, None)

    def finish(after):
        return _split_wait(f"reduce_{tag}_ici_wait", s2, r2, srcs2, lands2, plan2, slots2, after)

    return finish, token2


def _place_shards(me, shards, dtypes):
    n = len(shards)

    def body(me_ref, *refs):
        for a in range(n):
            refs[n + a][...] = refs[a][...].astype(dtypes[a])

    full = lambda s: pl.BlockSpec(s.shape, lambda i, me_ref: (0, 0))
    slot = lambda s: pl.BlockSpec((None,) + s.shape, lambda i, me_ref: (me_ref[0], 0, 0))
    return pl.pallas_call(
        body, name="place_shards",
        grid_spec=pltpu.PrefetchScalarGridSpec(num_scalar_prefetch=1, grid=(1,), in_specs=[full(s) for s in shards],
                                               out_specs=[slot(s) for s in shards]),
        out_shape=[jax.ShapeDtypeStruct((N_DEV,) + s.shape, d) for s, d in zip(shards, dtypes)],
        compiler_params=_params(("arbitrary",)),
    )(me, *shards)


def _mix_in_fwd(x, g1, w_in_t, conv_w, gq, gk, gconv):
    tm = 256
    n_t = SEQ // tm

    def body(x_ref, g1_ref, w_ref, cw_ref, gq_ref, gk_ref, gc_ref,
             proj_ref, ycn_ref, qn_ref, kn_ref, v_ref, halo_ref):
        @pl.when(pl.program_id(0) == 0)
        def _():
            halo_ref[...] = jnp.zeros_like(halo_ref)

        xv = x_ref[...]
        u = (xv * _rstd(xv) * g1_ref[...]).astype(BF16)
        proj = _dot_nt(u, w_ref[...])
        proj_ref[...] = proj
        gate_b = proj[:, 0:CONV_WIDTH]
        a = proj[:, CONV_WIDTH:2 * CONV_WIDTH] * proj[:, 2 * CONV_WIDTH:3 * CONV_WIDTH]
        cv, _, _ = _conv3(a, cw_ref[...], halo_ref[...])
        halo_ref[...] = a[tm - SUBLANES:]
        yc = gate_b * cv
        ycn_ref[...] = (yc * _rstd(yc) * gc_ref[...]).astype(BF16)
        q0 = 3 * CONV_WIDTH
        qn_ref[...] = _head_norm(proj[:, q0:q0 + ATTN_WIDTH], gq_ref[...], N_HEADS).astype(BF16)
        k0 = q0 + ATTN_WIDTH
        kn_ref[...] = _head_norm(proj[:, k0:k0 + KV_WIDTH], gk_ref[...], 2).astype(BF16)
        v_ref[...] = proj[:, k0 + KV_WIDTH:k0 + 2 * KV_WIDTH].astype(BF16)

    const = lambda shape: pl.BlockSpec(shape, lambda i: (0,) * len(shape))
    rows = lambda w: pl.BlockSpec((tm, w), lambda i: (i, 0))
    return pl.pallas_call(
        body, name="mix_in_fwd", grid=(n_t,),
        in_specs=[rows(D_MODEL), const((1, D_MODEL)), const((IN_WIDTH, D_MODEL)), const((3, CONV_WIDTH)),
                  const((1, HEAD_DIM)), const((1, HEAD_DIM)), const((1, CONV_WIDTH))],
        out_specs=[rows(IN_WIDTH), rows(CONV_WIDTH), rows(ATTN_WIDTH), rows(KV_WIDTH), rows(KV_WIDTH)],
        out_shape=[jax.ShapeDtypeStruct((SEQ, IN_WIDTH), F32), jax.ShapeDtypeStruct((SEQ, CONV_WIDTH), BF16),
                   jax.ShapeDtypeStruct((SEQ, ATTN_WIDTH), BF16), jax.ShapeDtypeStruct((SEQ, KV_WIDTH), BF16),
                   jax.ShapeDtypeStruct((SEQ, KV_WIDTH), BF16)],
        scratch_shapes=[pltpu.VMEM((SUBLANES, CONV_WIDTH), F32)],
        compiler_params=_params(("arbitrary",)),
    )(x, g1, w_in_t, conv_w, gq, gk, gconv)


def _band_bias(tbl_ref, bkt, bias_ref):
    for h in range(N_HEADS):
        acc = jnp.zeros(bkt.shape, F32)
        for b in range(NUM_BUCKETS):
            acc = jnp.where(bkt == b, tbl_ref[b, h], acc)
        bias_ref[h] = acc


def _band_valid(i):
    qi = lax.broadcasted_iota(jnp.int32, (BLK, 2 * BLK), 0)
    ji = lax.broadcasted_iota(jnp.int32, (BLK, 2 * BLK), 1)
    dist = qi + BLK - ji
    return (dist >= 0) & (dist < BLK) & ((ji >= BLK) | (i > 0))


def _band_rows(ref, i):
    prev = pl.multiple_of(jnp.maximum(i - 1, 0) * BLK, BLK)
    cur = pl.multiple_of(i * BLK, BLK)
    return jnp.concatenate([ref[pl.ds(prev, BLK), :], ref[pl.ds(cur, BLK), :]], axis=0), prev, cur


def _head_probs(qh, kh, bias, valid, sink):
    logits = _dot_nt(qh, kh) * (HEAD_DIM ** -0.5) + bias
    logits = jnp.where(valid, logits, NEG_INF)
    m = jnp.maximum(jnp.max(logits, axis=-1, keepdims=True), sink)
    p = jnp.exp(logits - m)
    es = jnp.exp(sink - m)
    den = jnp.sum(p, axis=-1, keepdims=True) + es
    return p / den, es / den


def _attn_fwd(qn, kn, v, tbl, sinks, bkt, gattn):
    n_b = SEQ // BLK

    def body(q_ref, k_ref, v_ref, tbl_ref, sink_ref, bkt_ref, ga_ref, y_ref, yn_ref, bias_ref):
        i = pl.program_id(0)

        @pl.when(i == 0)
        def _():
            _band_bias(tbl_ref, bkt_ref[...], bias_ref)

        kb, _, _ = _band_rows(k_ref, i)
        vb, _, _ = _band_rows(v_ref, i)
        valid = _band_valid(i)
        q = q_ref[...]
        outs = []
        for h in range(N_HEADS):
            hk = h // GQA_GROUP
            kv = slice(HEAD_DIM * hk, HEAD_DIM * (hk + 1))
            probs, _ = _head_probs(q[:, HEAD_DIM * h:HEAD_DIM * (h + 1)], kb[:, kv], bias_ref[h], valid, sink_ref[0, h])
            outs.append(_dot(probs.astype(BF16), vb[:, kv]))
        y = jnp.concatenate(outs, axis=-1)
        y_ref[...] = y
        yn_ref[...] = (y * _rstd(y) * ga_ref[...]).astype(BF16)

    const = lambda shape: pl.BlockSpec(shape, lambda i: (0,) * len(shape))
    rows = lambda w: pl.BlockSpec((BLK, w), lambda i: (i, 0))
    smem = pl.BlockSpec(memory_space=pltpu.SMEM)
    return pl.pallas_call(
        body, name="attn_fwd", grid=(n_b,),
        in_specs=[rows(ATTN_WIDTH), const((SEQ, KV_WIDTH)), const((SEQ, KV_WIDTH)), smem, smem,
                  const((BLK, 2 * BLK)), const((1, ATTN_WIDTH))],
        out_specs=[rows(ATTN_WIDTH), rows(ATTN_WIDTH)],
        out_shape=[jax.ShapeDtypeStruct((SEQ, ATTN_WIDTH), F32), jax.ShapeDtypeStruct((SEQ, ATTN_WIDTH), BF16)],
        scratch_shapes=[pltpu.VMEM((N_HEADS, BLK, 2 * BLK), F32)],
        compiler_params=_params(("arbitrary",)),
    )(qn, kn, v, tbl, sinks, bkt, gattn)


def _ffn_fwd(x, ycn, yan, w_out, g2, w_up, fcw, fcb, w_down, tgt):
    tm = 256
    n_t = SEQ // tm

    def body(x_ref, ycn_ref, yan_ref, wo_ref, g2_ref, wu_ref, cw_ref, b_ref, wd_ref, tgt_ref,
             h1_ref, u2_ref, up_ref, dh2_ref, dh2b_ref, loss_ref, acc_ref, halo_ref):
        i, j = pl.program_id(0), pl.program_id(1)

        @pl.when((i == 0) & (j == 0))
        def _():
            loss_ref[...] = jnp.zeros_like(loss_ref)

        @pl.when(j == 0)
        def _():
            h1 = x_ref[...] + _dot(ycn_ref[...], wo_ref[0:CONV_WIDTH, :]) + _dot(yan_ref[...], wo_ref[CONV_WIDTH:, :])
            h1_ref[...] = h1
            u2_ref[...] = (h1 * _rstd(h1) * g2_ref[...]).astype(BF16)
            acc_ref[...] = jnp.zeros_like(acc_ref)

        u2 = u2_ref[...]
        pre = []
        for s in range(2):
            up = _dot_nt(u2, wu_ref[s])
            up_ref[s] = up
            halo = jnp.where(i == 0, 0.0, halo_ref[s, j])
            pre.append(_conv3(up, cw_ref[s], halo)[0] + b_ref[s])
            halo_ref[s, j] = up[tm - SUBLANES:]
        g, val = pre
        act = (g * jax.nn.sigmoid(g) * val).astype(BF16)
        acc_ref[...] += _dot(act, wd_ref[...])

        @pl.when(j == N_FFN_BLK - 1)
        def _():
            err = h1_ref[...] + acc_ref[...] - tgt_ref[...]
            loss_ref[...] += 0.5 * jnp.sum(err * err) / D_MODEL
            dh2 = err / D_MODEL
            dh2_ref[...] = dh2
            dh2b_ref[...] = dh2.astype(BF16)

    rows = lambda w: pl.BlockSpec((tm, w), lambda i, j: (i, 0))
    const = lambda shape: pl.BlockSpec(shape, lambda i, j: (0,) * len(shape))
    pair = lambda r, c: pl.BlockSpec((2, None, r, c), lambda i, j: (0, j, 0, 0))
    upb = pl.BlockSpec((2, None, tm, FFN_BLK), lambda i, j: (0, j, i, 0))
    return pl.pallas_call(
        body, name="ffn_fwd", grid=(n_t, N_FFN_BLK),
        in_specs=[rows(D_MODEL), rows(CONV_WIDTH), rows(ATTN_WIDTH), const((D_MODEL, D_MODEL)), const((1, D_MODEL)),
                  pair(FFN_BLK, D_MODEL), pair(3, FFN_BLK), pair(1, FFN_BLK),
                  pl.BlockSpec((None, FFN_BLK, D_MODEL), lambda i, j: (j, 0, 0)), rows(D_MODEL)],
        out_specs=[rows(D_MODEL), rows(D_MODEL), upb, rows(D_MODEL), rows(D_MODEL), const((SUBLANES, 128))],
        out_shape=[jax.ShapeDtypeStruct((SEQ, D_MODEL), F32), jax.ShapeDtypeStruct((SEQ, D_MODEL), BF16),
                   jax.ShapeDtypeStruct((2, N_FFN_BLK, SEQ, FFN_BLK), F32),
                   jax.ShapeDtypeStruct((SEQ, D_MODEL), F32), jax.ShapeDtypeStruct((SEQ, D_MODEL), BF16),
                   jax.ShapeDtypeStruct((SUBLANES, 128), F32)],
        scratch_shapes=[pltpu.VMEM((tm, D_MODEL), F32), pltpu.VMEM((2, N_FFN_BLK, SUBLANES, FFN_BLK), F32)],
        compiler_params=_params(("arbitrary", "arbitrary")),
    )(x, ycn, yan, w_out, g2, w_up, fcw, fcb, w_down, tgt)


def _ffn_bwd(dh2, dh2b, h1, g2, up, w_up, fcw, fcb, w_down):
    tm = 256
    n_t = SEQ // tm
    halo_blocks = tm // SUBLANES

    def body(dh2_ref, dh2b_ref, h1_ref, g2_ref, up_ref, uph_ref, wu_ref, cw_ref, b_ref, wd_ref,
             act_ref, dup_ref, dh1_ref, dh1b_ref, dfb_ref, dfcw_ref, dg2_ref, acc_ref, next_ref):
        i, j = pl.program_id(0), pl.program_id(1)
        first_tile = i == n_t - 1

        @pl.when((i == 0) & (j == 0))
        def _():
            dfb_ref[...] = jnp.zeros_like(dfb_ref)
            dfcw_ref[...] = jnp.zeros_like(dfcw_ref)
            dg2_ref[...] = jnp.zeros_like(dg2_ref)

        @pl.when(j == 0)
        def _():
            acc_ref[...] = jnp.zeros_like(acc_ref)

        ups, pre = [], []
        for s in range(2):
            up = up_ref[s]
            halo = jnp.where(first_tile, 0.0, uph_ref[s])
            p, up2, up1 = _conv3(up, cw_ref[s], halo)
            pre.append(p + b_ref[s])
            ups.append((up, up1, up2))
        g, val = pre
        sg = jax.nn.sigmoid(g)
        silu = g * sg
        act_ref[...] = (silu * val).astype(BF16)
        dact = _dot_nt(dh2b_ref[...], wd_ref[...])
        dpre = (dact * val * (sg * (1.0 + g * (1.0 - sg))), dact * silu)
        for s in range(2):
            d = dpre[s]
            u, u1, u2 = ups[s]
            dfb_ref[s, j] += jnp.sum(d, axis=0, keepdims=True)
            dfcw_ref[s, j, 0:1, :] += jnp.sum(d * u2, axis=0, keepdims=True)
            dfcw_ref[s, j, 1:2, :] += jnp.sum(d * u1, axis=0, keepdims=True)
            dfcw_ref[s, j, 2:3, :] += jnp.sum(d * u, axis=0, keepdims=True)
            nxt = jnp.where(i == 0, 0.0, next_ref[s, j])
            dup = _conv3_bwd_input(d, cw_ref[s], nxt).astype(BF16)
            next_ref[s, j] = d[:SUBLANES]
            dup_ref[s] = dup
            acc_ref[...] += _dot(dup, wu_ref[s])

        @pl.when(j == N_FFN_BLK - 1)
        def _():
            dn, dgain = _rms_bwd(h1_ref[...], g2_ref[...], acc_ref[...])
            dh1 = dh2_ref[...] + dn
            dh1_ref[...] = dh1
            dh1b_ref[...] = dh1.astype(BF16)
            dg2_ref[...] += dgain

    rev = lambda i: n_t - 1 - i
    rows = lambda w: pl.BlockSpec((tm, w), lambda i, j: (rev(i), 0))
    const = lambda shape: pl.BlockSpec(shape, lambda i, j: (0,) * len(shape))
    pair = lambda r, c: pl.BlockSpec((2, None, r, c), lambda i, j: (0, j, 0, 0))
    upb = pl.BlockSpec((2, None, tm, FFN_BLK), lambda i, j: (0, j, rev(i), 0))
    halo = pl.BlockSpec((2, None, SUBLANES, FFN_BLK), lambda i, j: (0, j, jnp.maximum(rev(i) * halo_blocks - 1, 0), 0))
    return pl.pallas_call(
        body, name="ffn_bwd", grid=(n_t, N_FFN_BLK),
        in_specs=[rows(D_MODEL), rows(D_MODEL), rows(D_MODEL), const((1, D_MODEL)), upb, halo,
                  pair(FFN_BLK, D_MODEL), pair(3, FFN_BLK), pair(1, FFN_BLK),
                  pl.BlockSpec((None, FFN_BLK, D_MODEL), lambda i, j: (j, 0, 0))],
        out_specs=[pl.BlockSpec((None, tm, FFN_BLK), lambda i, j: (j, rev(i), 0)), upb, rows(D_MODEL), rows(D_MODEL),
                   const((2, N_FFN_BLK, 1, FFN_BLK)), const((2, N_FFN_BLK, 3, FFN_BLK)), const((1, D_MODEL))],
        out_shape=[jax.ShapeDtypeStruct((N_FFN_BLK, SEQ, FFN_BLK), BF16),
                   jax.ShapeDtypeStruct((2, N_FFN_BLK, SEQ, FFN_BLK), BF16), jax.ShapeDtypeStruct((SEQ, D_MODEL), F32),
                   jax.ShapeDtypeStruct((SEQ, D_MODEL), BF16), jax.ShapeDtypeStruct((2, N_FFN_BLK, 1, FFN_BLK), F32),
                   jax.ShapeDtypeStruct((2, N_FFN_BLK, 3, FFN_BLK), F32), jax.ShapeDtypeStruct((1, D_MODEL), F32)],
        scratch_shapes=[pltpu.VMEM((tm, D_MODEL), F32), pltpu.VMEM((2, N_FFN_BLK, SUBLANES, FFN_BLK), F32)],
        compiler_params=_params(("arbitrary", "arbitrary")),
    )(dh2, dh2b, h1, g2, up, up, w_up, fcw, fcb, w_down)


def _grad_tn(a_list, b, out_rows, name, after=None):
    n = len(a_list)
    ncol = b.shape[1]

    def body(*refs):
        a_refs, b_ref, o_ref = refs[:n], refs[n], refs[n + 1]
        j = pl.program_id(0)
        for k in range(n):
            @pl.when(j == k)
            def _(k=k):
                o_ref[...] = _dot_tn(a_refs[k][...], b_ref[...]).astype(BF16)

    full = lambda shape: pl.BlockSpec(shape, lambda j: (0,) * len(shape))
    body, more_specs, more = _ordered_behind(body, n + 1, after)
    return pl.pallas_call(
        body, name=name, grid=(n,),
        in_specs=[full((SEQ, out_rows))] * n + [full((SEQ, ncol))] + more_specs,
        out_specs=pl.BlockSpec((None, out_rows, ncol), lambda j: (j, 0, 0)),
        out_shape=jax.ShapeDtypeStruct((n, out_rows, ncol), BF16),
        compiler_params=_params(("arbitrary",)),
    )(*a_list, b, *more)


def _grad_tn_blocked(a, b, name, a_is_blocked):
    nb = a.shape[0] if a_is_blocked else b.shape[0]
    a_w, b_w = a.shape[-1], b.shape[-1]

    def body(a_ref, b_ref, o_ref):
        o_ref[...] = _dot_tn(a_ref[...], b_ref[...]).astype(BF16)

    blocked = lambda w: pl.BlockSpec((None, SEQ, w), lambda k: (k, 0, 0))
    full = lambda w: pl.BlockSpec((SEQ, w), lambda k: (0, 0))
    return pl.pallas_call(
        body, name=name, grid=(nb,),
        in_specs=[blocked(a_w) if a_is_blocked else full(a_w), full(b_w) if a_is_blocked else blocked(b_w)],
        out_specs=pl.BlockSpec((None, a_w, b_w), lambda k: (k, 0, 0)),
        out_shape=jax.ShapeDtypeStruct((nb, a_w, b_w), BF16),
        compiler_params=_params(("arbitrary",)),
    )(a, b)


def _out_bwd(dh1b, w_out, y_attn, gattn, after=None):
    tm = 256
    n_t = SEQ // tm

    def body(dh_ref, wo_ref, y_ref, ga_ref, dycn_ref, dy_ref, dga_ref):
        @pl.when(pl.program_id(0) == 0)
        def _():
            dga_ref[...] = jnp.zeros_like(dga_ref)

        dycat = _dot_nt(dh_ref[...], wo_ref[...])
        dycn_ref[...] = dycat[:, :CONV_WIDTH]
        dy, dga = _rms_bwd(y_ref[...], ga_ref[...], dycat[:, CONV_WIDTH:])
        dy_ref[...] = dy
        dga_ref[...] += dga

    rows = lambda w: pl.BlockSpec((tm, w), lambda i: (i, 0))
    const = lambda shape: pl.BlockSpec(shape, lambda i: (0,) * len(shape))
    body, more_specs, more = _ordered_behind(body, 4, after)
    return pl.pallas_call(
        body, name="out_bwd", grid=(n_t,),
        in_specs=[rows(D_MODEL), const((D_MODEL, D_MODEL)), rows(ATTN_WIDTH), const((1, ATTN_WIDTH))] + more_specs,
        out_specs=[rows(CONV_WIDTH), rows(ATTN_WIDTH), const((1, ATTN_WIDTH))],
        out_shape=[jax.ShapeDtypeStruct((SEQ, CONV_WIDTH), F32), jax.ShapeDtypeStruct((SEQ, ATTN_WIDTH), F32),
                   jax.ShapeDtypeStruct((1, ATTN_WIDTH), F32)],
        compiler_params=_params(("arbitrary",)),
    )(dh1b, w_out, y_attn, gattn, *more)


def _attn_bwd(qn, kn, v, dy, tbl, sinks, bkt, after=None):
    n_b = SEQ // BLK

    def body(q_ref, k_ref, v_ref, dy_ref, tbl_ref, sink_ref, bkt_ref,
             dq_ref, dk_ref, dv_ref, dtbl_ref, dsink_ref, bias_ref, dbias_ref, dsacc_ref):
        i = pl.program_id(0)

        @pl.when(i == 0)
        def _():
            _band_bias(tbl_ref, bkt_ref[...], bias_ref)
            dbias_ref[...] = jnp.zeros_like(dbias_ref)
            dsacc_ref[...] = jnp.zeros_like(dsacc_ref)
            dk_ref[...] = jnp.zeros_like(dk_ref)
            dv_ref[...] = jnp.zeros_like(dv_ref)

        kb, prev, cur = _band_rows(k_ref, i)
        vb, _, _ = _band_rows(v_ref, i)
        valid = _band_valid(i)
        q = q_ref[...]
        dy = dy_ref[...]
        lane = lax.broadcasted_iota(jnp.int32, (BLK, 128), 1)
        dqs, dks, dvs = [], [], []
        dsink = jnp.zeros((BLK, 128), F32)
        for h in range(N_HEADS):
            hk = h // GQA_GROUP
            kv = slice(HEAD_DIM * hk, HEAD_DIM * (hk + 1))
            qh = q[:, HEAD_DIM * h:HEAD_DIM * (h + 1)]
            probs, psink = _head_probs(qh, kb[:, kv], bias_ref[h], valid, sink_ref[0, h])
            doh = dy[:, HEAD_DIM * h:HEAD_DIM * (h + 1)].astype(BF16)
            dprobs = _dot_nt(doh, vb[:, kv])
            dvh = _dot_tn(probs.astype(BF16), doh)
            dsum = jnp.sum(probs * dprobs, axis=-1, keepdims=True)
            dlogits = probs * (dprobs - dsum)
            dsink = jnp.where(lane == h, -psink * dsum, dsink)
            dbias_ref[h] += dlogits
            ds = (dlogits * (HEAD_DIM ** -0.5)).astype(BF16)
            dqs.append(_dot(ds, kb[:, kv]))
            dkh = _dot_tn(ds, qh)
            if h % GQA_GROUP == 0:
                dks.append(dkh)
                dvs.append(dvh)
            else:
                dks[hk] = dks[hk] + dkh
                dvs[hk] = dvs[hk] + dvh
        dq_ref[...] = jnp.concatenate(dqs, axis=-1)
        dsacc_ref[...] += dsink
        dkb = jnp.concatenate(dks, axis=-1)
        dvb = jnp.concatenate(dvs, axis=-1)
        dk_ref[pl.ds(prev, BLK), :] += dkb[:BLK]
        dk_ref[pl.ds(cur, BLK), :] += dkb[BLK:]
        dv_ref[pl.ds(prev, BLK), :] += dvb[:BLK]
        dv_ref[pl.ds(cur, BLK), :] += dvb[BLK:]

        @pl.when(i == n_b - 1)
        def _():
            dsink_ref[...] = jnp.sum(dsacc_ref[...], axis=0, keepdims=True)
            bkt = bkt_ref[...]
            row8 = lax.broadcasted_iota(jnp.int32, (N_HEADS, 128), 0)
            lane8 = lax.broadcasted_iota(jnp.int32, (N_HEADS, 128), 1)
            acc = jnp.zeros((N_HEADS, 128), F32)
            for h in range(N_HEADS):
                dbh = dbias_ref[h]
                for b in range(NUM_BUCKETS):
                    acc = jnp.where((row8 == h) & (lane8 == b), jnp.sum(jnp.where(bkt == b, dbh, 0.0)), acc)
            dtbl_ref[...] = acc

    const = lambda shape: pl.BlockSpec(shape, lambda i: (0,) * len(shape))
    rows = lambda w: pl.BlockSpec((BLK, w), lambda i: (i, 0))
    smem = pl.BlockSpec(memory_space=pltpu.SMEM)
    body, more_specs, more = _ordered_behind(body, 7, after)
    return pl.pallas_call(
        body, name="attn_bwd", grid=(n_b,),
        in_specs=[rows(ATTN_WIDTH), const((SEQ, KV_WIDTH)), const((SEQ, KV_WIDTH)), rows(ATTN_WIDTH), smem, smem,
                  const((BLK, 2 * BLK))] + more_specs,
        out_specs=[rows(ATTN_WIDTH), const((SEQ, KV_WIDTH)), const((SEQ, KV_WIDTH)), const((N_HEADS, 128)), const((1, 128))],
        out_shape=[jax.ShapeDtypeStruct((SEQ, ATTN_WIDTH), F32), jax.ShapeDtypeStruct((SEQ, KV_WIDTH), F32),
                   jax.ShapeDtypeStruct((SEQ, KV_WIDTH), F32), jax.ShapeDtypeStruct((N_HEADS, 128), F32),
                   jax.ShapeDtypeStruct((1, 128), F32)],
        scratch_shapes=[pltpu.VMEM((N_HEADS, BLK, 2 * BLK), F32), pltpu.VMEM((N_HEADS, BLK, 2 * BLK), F32),
                        pltpu.VMEM((BLK, 128), F32)],
        compiler_params=_params(("arbitrary",)),
    )(qn, kn, v, dy, tbl, sinks, bkt, *more)


def _mix_in_bwd(x, dh1, proj, dycn, dqn, dkn, dv, w_in_t, conv_w, g1, gq, gk, gconv):
    tm = 256
    n_t = SEQ // tm
    halo_blocks = tm // SUBLANES

    def body(x_ref, dh1_ref, proj_ref, halo_ref, dycn_ref, dqn_ref, dkn_ref, dv_ref, w_ref, cw_ref,
             g1_ref, gq_ref, gk_ref, gc_ref,
             dx_ref, dproj_ref, u1_ref, dcw_ref, dgc_ref, dgq_ref, dgk_ref, dg1_ref, next_ref):
        i = pl.program_id(0)
        first_tile = i == n_t - 1

        @pl.when(i == 0)
        def _():
            for r in (dcw_ref, dgc_ref, dgq_ref, dgk_ref, dg1_ref, next_ref):
                r[...] = jnp.zeros_like(r)

        proj = proj_ref[...]
        hp = halo_ref[...]
        gate_b = proj[:, 0:CONV_WIDTH]
        gate_c = proj[:, CONV_WIDTH:2 * CONV_WIDTH]
        hc = proj[:, 2 * CONV_WIDTH:3 * CONV_WIDTH]
        a = gate_c * hc
        a_halo = jnp.where(first_tile, 0.0, hp[:, CONV_WIDTH:2 * CONV_WIDTH] * hp[:, 2 * CONV_WIDTH:3 * CONV_WIDTH])
        cw = cw_ref[...]
        cv, a2, a1 = _conv3(a, cw, a_halo)
        dyc, dgc = _rms_bwd(gate_b * cv, gc_ref[...], dycn_ref[...])
        dgc_ref[...] += dgc
        dcv = dyc * gate_b
        dcw_ref[...] += jnp.concatenate(
            [jnp.sum(dcv * a2, axis=0, keepdims=True), jnp.sum(dcv * a1, axis=0, keepdims=True),
             jnp.sum(dcv * a, axis=0, keepdims=True)], axis=0)
        da = _conv3_bwd_input(dcv, cw, next_ref[...])
        next_ref[...] = dcv[:SUBLANES]
        q0 = 3 * CONV_WIDTH
        k0 = q0 + ATTN_WIDTH
        dq, dgq = _head_norm_bwd(proj[:, q0:k0], gq_ref[...], dqn_ref[...], N_HEADS)
        dk, dgk = _head_norm_bwd(proj[:, k0:k0 + KV_WIDTH], gk_ref[...], dkn_ref[...], 2)
        dgq_ref[...] += dgq
        dgk_ref[...] += dgk
        dproj = jnp.concatenate([dyc * cv, da * hc, da * gate_c, dq, dk, dv_ref[...]], axis=-1).astype(BF16)
        dproj_ref[...] = dproj
        du1 = _dot(dproj, w_ref[...])
        xv = x_ref[...]
        dn, dg1 = _rms_bwd(xv, g1_ref[...], du1)
        dx_ref[...] = dh1_ref[...] + dn
        dg1_ref[...] += dg1
        u1_ref[...] = (xv * _rstd(xv) * g1_ref[...]).astype(BF16)

    rev = lambda i: n_t - 1 - i
    rows = lambda w: pl.BlockSpec((tm, w), lambda i: (rev(i), 0))
    const = lambda shape: pl.BlockSpec(shape, lambda i: (0,) * len(shape))
    halo = pl.BlockSpec((SUBLANES, IN_WIDTH), lambda i: (jnp.maximum(rev(i) * halo_blocks - 1, 0), 0))
    return pl.pallas_call(
        body, name="mix_in_bwd", grid=(n_t,),
        in_specs=[rows(D_MODEL), rows(D_MODEL), rows(IN_WIDTH), halo, rows(CONV_WIDTH), rows(ATTN_WIDTH), rows(KV_WIDTH),
                  rows(KV_WIDTH), const((IN_WIDTH, D_MODEL)), const((3, CONV_WIDTH)), const((1, D_MODEL)),
                  const((1, HEAD_DIM)), const((1, HEAD_DIM)), const((1, CONV_WIDTH))],
        out_specs=[rows(D_MODEL), rows(IN_WIDTH), rows(D_MODEL), const((3, CONV_WIDTH)), const((1, CONV_WIDTH)),
                   const((1, HEAD_DIM)), const((1, HEAD_DIM)), const((1, D_MODEL))],
        out_shape=[jax.ShapeDtypeStruct((SEQ, D_MODEL), F32), jax.ShapeDtypeStruct((SEQ, IN_WIDTH), BF16),
                   jax.ShapeDtypeStruct((SEQ, D_MODEL), BF16), jax.ShapeDtypeStruct((3, CONV_WIDTH), F32),
                   jax.ShapeDtypeStruct((1, CONV_WIDTH), F32), jax.ShapeDtypeStruct((1, HEAD_DIM), F32),
                   jax.ShapeDtypeStruct((1, HEAD_DIM), F32), jax.ShapeDtypeStruct((1, D_MODEL), F32)],
        scratch_shapes=[pltpu.VMEM((SUBLANES, CONV_WIDTH), F32)],
        compiler_params=_params(("arbitrary",)),
    )(x, dh1, proj, proj, dycn, dqn, dkn, dv, w_in_t, conv_w, g1, gq, gk, gconv)


def _grad_w_in(dproj, u1, after=None):
    bw = 768

    def body(a_ref, b_ref, o_ref):
        o_ref[...] = _dot_tn(a_ref[...], b_ref[...]).astype(BF16)

    body, more_specs, more = _ordered_behind(body, 2, after)
    return pl.pallas_call(
        body, name="grad_w_in", grid=(IN_WIDTH // bw,),
        in_specs=[pl.BlockSpec((SEQ, bw), lambda k: (0, k)), pl.BlockSpec((SEQ, D_MODEL), lambda k: (0, 0))] + more_specs,
        out_specs=pl.BlockSpec((bw, D_MODEL), lambda k: (k, 0)),
        out_shape=jax.ShapeDtypeStruct((IN_WIDTH, D_MODEL), BF16),
        compiler_params=_params(("arbitrary",)),
    )(dproj, u1, *more)


def _adamw_math(w, g, m, v):
    m = ADAM_B1 * m + (1.0 - ADAM_B1) * g
    v = ADAM_B2 * v + (1.0 - ADAM_B2) * (g * g)
    m_hat = m / (1.0 - ADAM_B1 ** ADAM_STEP)
    v_hat = v / (1.0 - ADAM_B2 ** ADAM_STEP)
    return -ADAM_LR * (m_hat / (jnp.sqrt(v_hat) + ADAM_EPS) + ADAM_WD * w), m, v


def _adamw_small(w, m, v, recv):
    def body(w_ref, m_ref, v_ref, r_ref, g_o, d_o, m_o, v_o):
        g = r_ref[0]
        for s in range(1, N_DEV):
            g = g + r_ref[s]
        g_o[...] = g
        d_o[...], m_o[...], v_o[...] = _adamw_math(w_ref[...], g, m_ref[...], v_ref[...])

    return pl.pallas_call(
        body, name="adamw_small", out_shape=[jax.ShapeDtypeStruct(w.shape, F32)] * 4,
        compiler_params=pltpu.CompilerParams(vmem_limit_bytes=VMEM_LIMIT),
    )(w, m, v, recv)


def _adamw(w, m, v, part, recv, chip, name, row_blocks=1, after=None):
    rows, cols = w.shape
    rb = rows // row_blocks

    def body(chip_ref, w_ref, m_ref, v_ref, p_ref, r_ref, g_o, d_o, m_o, v_o):
        g = p_ref[...].astype(F32)
        for s in range(3):
            g = g + r_ref[s].astype(F32)
        g_o[...] = g
        d_o[...], m_o[...], v_o[...] = _adamw_math(w_ref[...], g, m_ref[...], v_ref[...])

    blk = pl.BlockSpec((rb, cols), lambda i, chip_ref: (i, 0))
    pblk = pl.BlockSpec((None, rb, cols), lambda i, chip_ref: (chip_ref[0], i, 0))
    rblk = pl.BlockSpec((3, rb, cols), lambda i, chip_ref: (0, i, 0))
    body, more_specs, more = _ordered_behind(body, 6, after)
    return pl.pallas_call(
        body, name=name,
        grid_spec=pltpu.PrefetchScalarGridSpec(num_scalar_prefetch=1, grid=(row_blocks,),
                                               in_specs=[blk, blk, blk, pblk, rblk] + more_specs, out_specs=[blk] * 4),
        out_shape=[jax.ShapeDtypeStruct((rows, cols), F32)] * 4,
        compiler_params=_params(("arbitrary",)),
    )(chip, w, m, v, part, recv, *more)


_SMALL_NAMES = ("norm_mix_g", "norm_ffn_g", "out_norm_conv_g", "out_norm_attn_g", "ffn_conv_b", "q_norm_g", "k_norm_g",
                "sinks", "rel_bias_table")


def _pack_small(vals, extra=None):
    parts = [vals[k].reshape(-1) for k in _SMALL_NAMES] + ([] if extra is None else [extra.reshape(1)])
    flat = jnp.concatenate(parts)
    return jnp.pad(flat, (0, SMALL_ROWS * SMALL_COLS - flat.shape[0])).reshape(SMALL_ROWS, SMALL_COLS)


def _unpack_small(packed, like):
    flat = packed.reshape(-1)
    out, off = {}, 0
    for k in _SMALL_NAMES:
        size = like[k].size
        out[k] = flat[off:off + size].reshape(like[k].shape)
        off += size
    return out, flat[off]


def kernel(x, norm_mix_g, w_in, conv_w, q_norm_g, k_norm_g, rel_bias_table, sinks, out_norm_conv_g, out_norm_attn_g, w_out, norm_ffn_g, w_up, ffn_conv_w, ffn_conv_b, w_down, loss_target, m_norm_mix_g, m_w_in, m_conv_w, m_q_norm_g, m_k_norm_g, m_rel_bias_table, m_sinks, m_out_norm_conv_g, m_out_norm_attn_g, m_w_out, m_norm_ffn_g, m_w_up, m_ffn_conv_w, m_ffn_conv_b, m_w_down, v_norm_mix_g, v_w_in, v_conv_w, v_q_norm_g, v_k_norm_g, v_rel_bias_table, v_sinks, v_out_norm_conv_g, v_out_norm_attn_g, v_w_out, v_norm_ffn_g, v_w_up, v_ffn_conv_w, v_ffn_conv_b, v_w_down):
    p = dict(norm_mix_g=norm_mix_g, w_in=w_in, conv_w=conv_w, q_norm_g=q_norm_g, k_norm_g=k_norm_g,
             rel_bias_table=rel_bias_table, sinks=sinks, out_norm_conv_g=out_norm_conv_g, out_norm_attn_g=out_norm_attn_g,
             w_out=w_out, norm_ffn_g=norm_ffn_g, w_up=w_up, ffn_conv_w=ffn_conv_w, ffn_conv_b=ffn_conv_b, w_down=w_down)
    m = dict(norm_mix_g=m_norm_mix_g, w_in=m_w_in, conv_w=m_conv_w, q_norm_g=m_q_norm_g, k_norm_g=m_k_norm_g,
             rel_bias_table=m_rel_bias_table, sinks=m_sinks, out_norm_conv_g=m_out_norm_conv_g,
             out_norm_attn_g=m_out_norm_attn_g, w_out=m_w_out, norm_ffn_g=m_norm_ffn_g, w_up=m_w_up,
             ffn_conv_w=m_ffn_conv_w, ffn_conv_b=m_ffn_conv_b, w_down=m_w_down)
    v = dict(norm_mix_g=v_norm_mix_g, w_in=v_w_in, conv_w=v_conv_w, q_norm_g=v_q_norm_g, k_norm_g=v_k_norm_g,
             rel_bias_table=v_rel_bias_table, sinks=v_sinks, out_norm_conv_g=v_out_norm_conv_g,
             out_norm_attn_g=v_out_norm_attn_g, w_out=v_w_out, norm_ffn_g=v_norm_ffn_g, w_up=v_w_up,
             ffn_conv_w=v_ffn_conv_w, ffn_conv_b=v_ffn_conv_b, w_down=v_w_down)

    xs, tgt = x[0], loss_target[0]
    g1, g2, gq, gk, gconv, gattn = norm_mix_g, norm_ffn_g, q_norm_g, k_norm_g, out_norm_conv_g, out_norm_attn_g
    ix, iy, ic = _coords()
    core = ic.astype(jnp.int32).reshape(1)
    chip = (2 * ix + iy).astype(jnp.int32).reshape(1)
    me = _lin(ix, iy, ic).astype(jnp.int32).reshape(1)
    bkt = jnp.asarray(_bucket_map())
    tr = lambda a: a[0].T

    wi_l, cw_l, wo_l, wu_l, wd_l, fcw_l = _place_shards(
        me, [tr(w_in), conv_w[0], w_out[0], tr(w_up), w_down[0], ffn_conv_w[0]], [BF16, F32, BF16, BF16, BF16, F32])
    finish_a, token_a = _all_gather_split([wi_l, cw_l], "mixer", None)
    finish_b, token_b = _all_gather_split([wo_l, wu_l, wd_l, fcw_l], "ffn", token_a)
    wi_g, cw_g = finish_a(token_b)
    w_in_t = wi_g.reshape(IN_WIDTH, D_MODEL)
    conv_w_f = jnp.transpose(cw_g, (1, 0, 2)).reshape(3, CONV_WIDTH)

    proj, ycn, qn, kn, vv = _mix_in_fwd(xs, g1, w_in_t, conv_w_f, gq, gk, gconv)
    y_attn, yan = _attn_fwd(qn, kn, vv, rel_bias_table, sinks, bkt, gattn)
    wo_g, wu_g, wd_g, fcw_g = finish_b(yan)
    w_out_f = wo_g.reshape(D_MODEL, D_MODEL)
    w_down_f = wd_g.reshape(N_FFN_BLK, FFN_BLK, D_MODEL)
    w_up_f = wu_g.reshape(2, N_FFN_BLK, FFN_BLK, D_MODEL)
    fcw_f = fcw_g.reshape(2, N_FFN_BLK, 3, FFN_BLK)
    fcb = ffn_conv_b.reshape(2, N_FFN_BLK, 1, FFN_BLK)
    h1, u2, up, dh2, dh2b, loss_acc = _ffn_fwd(xs, ycn, yan, w_out_f, g2, w_up_f, fcw_f, fcb, w_down_f, tgt)

    act, dup, dh1, dh1b, dfb, dfcw, dg2 = _ffn_bwd(dh2, dh2b, h1, g2, up, w_up_f, fcw_f, fcb, w_down_f)
    dw_down = _grad_tn_blocked(act, dh2b, "grad_w_down", a_is_blocked=True).reshape(N_DEV, D_FF // N_DEV, D_MODEL)
    dw_up = _grad_tn_blocked(dup.reshape(N_DEV, SEQ, FFN_BLK), u2, "grad_w_up", a_is_blocked=True)
    out_bwd = {}

    def behind_ffn(token):
        out_bwd["r"] = _out_bwd(dh1b, w_out_f, y_attn, gattn, after=token)
        return out_bwd["r"][0]

    finish_ffn, token_ffn = _reduce_scatter_split([dw_down, dw_up, dfcw.reshape(N_DEV, 3, FFN_BLK)], "ffn", core, behind_ffn)
    dycn, dy_attn, dgattn = out_bwd["r"]
    dw_out = _grad_tn([ycn, yan], dh1b, CONV_WIDTH, "grad_w_out", after=token_ffn).reshape(
        N_DEV, D_MODEL // N_DEV, D_MODEL)
    dqn, dkn, dv, dtbl_t, dsinks = _attn_bwd(qn, kn, vv, dy_attn, rel_bias_table, sinks, bkt, after=token_ffn)
    dx, dproj, u1, dcw, dgconv, dgq, dgk, dg1 = _mix_in_bwd(xs, dh1, proj, dycn, dqn, dkn, dv, w_in_t, conv_w_f,
                                                             g1, gq, gk, gconv)
    small = dict(norm_mix_g=dg1, norm_ffn_g=dg2, out_norm_conv_g=dgconv, out_norm_attn_g=dgattn, ffn_conv_b=dfb,
                 q_norm_g=dgq, k_norm_g=dgk, sinks=dsinks[:, :N_HEADS], rel_bias_table=dtbl_t[:, :NUM_BUCKETS].T)
    packed = _pack_small(small, loss_acc[0, 0])
    plan_s, slots_s = _broadcast_plan()
    s_sem, r_sem, src_s, land_s, token_s = _split_start(
        "gather_small_start", [packed], [jnp.broadcast_to(packed[None], (N_DEV,) + packed.shape)], plan_s, None)
    dw_in_t = _grad_w_in(dproj, u1, after=token_s).reshape(N_DEV, IN_WIDTH // N_DEV, D_MODEL)
    dcw_b = jnp.transpose(dcw.reshape(3, N_DEV, CONV_WIDTH // N_DEV), (1, 0, 2))
    adam = {}
    ffn_parts = {}

    def behind_mixer(token):
        (p_wd, p_wu, p_fcw), (r_wd, r_wu, r_fcw) = finish_ffn(token)
        ffn_parts["w_up"] = (p_wu, r_wu)
        adam["w_down"] = _adamw(w_down[0], m_w_down[0], v_w_down[0], p_wd, r_wd, chip, "adamw_w_down", row_blocks=2)
        adam["ffn_conv_w"] = _adamw(ffn_conv_w[0], m_ffn_conv_w[0], v_ffn_conv_w[0], p_fcw, r_fcw, chip, "adamw_ffn_conv_w")
        return adam["w_down"][0]

    finish_mixer, token_mixer = _reduce_scatter_split([dw_in_t, dw_out, dcw_b], "mixer", core, behind_mixer)
    adam_up = _adamw(tr(w_up), tr(m_w_up), tr(v_w_up), *ffn_parts["w_up"], chip, "adamw_w_up", row_blocks=4,
                     after=token_mixer)
    _, (r_small,) = _split_wait("gather_small_wait", s_sem, r_sem, src_s, land_s, plan_s, slots_s, adam_up[0])
    small_like = {k: p[k] for k in _SMALL_NAMES}
    sm = _adamw_small(_pack_small(small_like), _pack_small({k: m[k] for k in _SMALL_NAMES}),
                      _pack_small({k: v[k] for k in _SMALL_NAMES}), r_small)
    (p_wi, p_wo, p_cw), (r_wi, r_wo, r_cw) = finish_mixer(sm[0])
    adam_in = _adamw(tr(w_in), tr(m_w_in), tr(v_w_in), p_wi, r_wi, chip, "adamw_w_in")
    adam["w_out"] = _adamw(w_out[0], m_w_out[0], v_w_out[0], p_wo, r_wo, chip, "adamw_w_out")
    adam["conv_w"] = _adamw(conv_w[0], m_conv_w[0], v_conv_w[0], p_cw, r_cw, chip, "adamw_conv_w")

    res = {k: tuple(a[None] for a in t) for k, t in adam.items()}
    res["w_up"] = tuple(a.T[None] for a in adam_up)
    res["w_in"] = tuple(a.T[None] for a in adam_in)
    unpacked = [_unpack_small(a, small_like) for a in sm]
    loss = unpacked[0][1]
    for k in _SMALL_NAMES:
        res[k] = tuple(u[0][k] for u in unpacked)
    order = ("norm_mix_g", "w_in", "conv_w", "q_norm_g", "k_norm_g", "rel_bias_table", "sinks", "out_norm_conv_g",
             "out_norm_attn_g", "w_out", "norm_ffn_g", "w_up", "ffn_conv_w", "ffn_conv_b", "w_down")
    return (loss, dx[None], *[res[k][0] for k in order], *[res[k][1] for k in order],
            *[res[k][2] for k in order], *[res[k][3] for k in order])
```

```python
import functools
import math

import numpy as np
import jax
import jax.numpy as jnp
from jax import lax
from jax.experimental import pallas as pl
from jax.experimental.pallas import tpu as pltpu

F32 = jnp.float32
BF16 = jnp.bfloat16

SEQ = 2048
D_MODEL = 1024
CONV_WIDTH = 512
ATTN_WIDTH = 512
KV_WIDTH = 128
HEAD_DIM = 64
N_HEADS = 8
GQA_GROUP = 4
IN_WIDTH = 2304
D_FF = 2816
BLK = 128
NUM_BUCKETS = 32
EPS = 1e-6
NEG_INF = -1e30
ADAM_LR = 0.001
ADAM_B1 = 0.9
ADAM_B2 = 0.999
ADAM_EPS = 1e-08
ADAM_WD = 0.01
ADAM_STEP = 10

N_DEV = 8
FFN_BLK = 2 * D_FF // N_DEV
N_FFN_BLK = D_FF // FFN_BLK
SUBLANES = 8
VMEM_LIMIT = 56 * 1024 * 1024

_MESH = pl.DeviceIdType.MESH
_ANY = pl.BlockSpec(memory_space=pl.ANY)


def _params(sem):
    return pltpu.CompilerParams(dimension_semantics=sem, vmem_limit_bytes=VMEM_LIMIT)


def _ordered_behind(body, pos, after):
    if after is None:
        return body, [], []
    return (lambda *refs: body(*refs[:pos], *refs[pos + 1:])), [_ANY], [after]


def _dot(a, b):
    return jnp.dot(a, b, preferred_element_type=F32)


def _dot_nt(a, b):
    return lax.dot_general(a, b, (((1,), (1,)), ((), ())), preferred_element_type=F32)


def _dot_tn(a, b):
    return lax.dot_general(a, b, (((0,), (0,)), ((), ())), preferred_element_type=F32)


def _shift_down(x, s, halo):
    r = pltpu.roll(x, s, axis=0)
    hr = pltpu.roll(halo, s, axis=0)
    row = lax.broadcasted_iota(jnp.int32, halo.shape, 0)
    top = jnp.where(row < s, hr, r[:SUBLANES])
    return jnp.concatenate([top, r[SUBLANES:]], axis=0)


def _shift_up(x, s, halo):
    n = x.shape[0]
    r = pltpu.roll(x, n - s, axis=0)
    hr = pltpu.roll(halo, SUBLANES - s, axis=0)
    row = lax.broadcasted_iota(jnp.int32, halo.shape, 0)
    bot = jnp.where(row >= SUBLANES - s, hr, r[n - SUBLANES:])
    return jnp.concatenate([r[:n - SUBLANES], bot], axis=0)


def _taps(w):
    return (w[0], w[1], w[2]) if len(w.shape) == 3 else (w[0:1], w[1:2], w[2:3])


def _conv3(x, w, halo):
    x2 = _shift_down(x, 2, halo)
    x1 = _shift_down(x, 1, halo)
    return x2 * w[0] + x1 * w[1] + x * w[2], x2, x1


def _conv3_bwd_input(dy, w, halo_next):
    return dy * w[2] + _shift_up(dy, 1, halo_next) * w[1] + _shift_up(dy, 2, halo_next) * w[0]


def _rstd(x):
    return lax.rsqrt(jnp.mean(x * x, axis=-1, keepdims=True) + EPS)


def _rms_bwd(x, g, dy):
    r = _rstd(x)
    n = x * r
    dn = dy * g
    dx = r * (dn - n * jnp.mean(dn * n, axis=-1, keepdims=True))
    return dx, jnp.sum(dy * n, axis=0, keepdims=True)


def _head_norm(x, g, heads):
    parts = []
    for h in range(heads):
        xh = x[:, HEAD_DIM * h:HEAD_DIM * (h + 1)]
        parts.append(xh * _rstd(xh) * g)
    return jnp.concatenate(parts, axis=-1)


def _head_norm_bwd(x, g, dy, heads):
    dxs, dg = [], jnp.zeros((1, HEAD_DIM), F32)
    for h in range(heads):
        sl = slice(HEAD_DIM * h, HEAD_DIM * (h + 1))
        dxh, dgh = _rms_bwd(x[:, sl], g, dy[:, sl])
        dxs.append(dxh)
        dg = dg + dgh
    return jnp.concatenate(dxs, axis=-1), dg


def _bucket_map():
    q = np.arange(BLK)[:, None]
    j = np.arange(2 * BLK)[None, :]
    n = np.maximum(q + BLK - j, 0)
    nf = np.maximum(n, 1).astype(np.float32)
    max_exact = NUM_BUCKETS // 2
    large = max_exact + (np.log(nf / max_exact) / math.log(BLK / max_exact) * (NUM_BUCKETS - max_exact)).astype(np.int32)
    large = np.minimum(large, NUM_BUCKETS - 1)
    return np.where(n < max_exact, n, large).astype(np.int32)


def _coords():
    return lax.axis_index("x"), lax.axis_index("y"), lax.axis_index("c")


def _lin(px, py, pc):
    return 4 * px + 2 * py + pc


_HBM = pl.BlockSpec(memory_space=pltpu.HBM)
_SEM = pl.BlockSpec(memory_space=pltpu.SEMAPHORE)
_EFFECT = pltpu.SideEffectType.DATAFLOW_SIDE_EFFECTING


def _in_hbm(a):
    return pltpu.with_memory_space_constraint(a, pltpu.HBM)


def _split_start(name, srcs, lands, plan, after):
    ns, nl = len(srcs), len(lands)
    n_copies = len(plan(0, 0, 0))
    n_after = 0 if after is None else 1

    def body(*refs):
        src_refs, land_refs = refs[:ns + nl], refs[ns:ns + nl]
        send_sems, recv_sems = refs[ns + nl + n_after], refs[ns + nl + n_after + 1]
        token = refs[-1]
        for k, (a, s_slot, l, d_slot, dev) in enumerate(plan(*_coords())):
            src = src_refs[a] if s_slot is None else src_refs[a].at[s_slot]
            pltpu.make_async_remote_copy(src_ref=src, dst_ref=land_refs[l].at[d_slot], send_sem=send_sems.at[k],
                                         recv_sem=recv_sems.at[k], device_id=dev, device_id_type=_MESH).start()
        token[...] = jnp.zeros_like(token)

    arrs = list(srcs) + list(lands)
    out = pl.pallas_call(
        body, name=name,
        out_shape=(pltpu.SemaphoreType.DMA((n_copies,)), pltpu.SemaphoreType.DMA((n_copies,)),
                   *[pltpu.HBM(a.shape, a.dtype) for a in arrs], jax.ShapeDtypeStruct((SUBLANES, 128), F32)),
        in_specs=[_HBM] * (ns + nl) + [_ANY] * n_after,
        out_specs=(_SEM, _SEM, *[_HBM] * (ns + nl), pl.BlockSpec(memory_space=pltpu.VMEM)),
        input_output_aliases={i: 2 + i for i in range(ns + nl)},
        compiler_params=pltpu.CompilerParams(has_side_effects=_EFFECT),
    )(*[_in_hbm(a) for a in arrs], *([] if after is None else [after]))
    return out[0], out[1], list(out[2:2 + ns]), list(out[2 + ns:2 + ns + nl]), out[-1]


def _split_wait(name, send_sems, recv_sems, srcs, lands, plan, recv_slots, after):
    ns, nl = len(srcs), len(lands)

    def body(*refs):
        src_refs, land_refs = refs[:ns + nl], refs[ns:ns + nl]
        send_sems, recv_sems = refs[ns + nl], refs[ns + nl + 1]
        coords = _coords()
        slots = recv_slots(*coords)
        for k, (a, s_slot, l, _, dev) in enumerate(plan(*coords)):
            src = src_refs[a] if s_slot is None else src_refs[a].at[s_slot]
            cp = pltpu.make_async_remote_copy(src_ref=src, dst_ref=land_refs[l].at[slots[k]], send_sem=send_sems.at[k],
                                              recv_sem=recv_sems.at[k], device_id=dev, device_id_type=_MESH)
            cp.wait_send()
            cp.wait_recv()

    arrs = list(srcs) + list(lands)
    out = pl.pallas_call(
        body, name=name,
        out_shape=tuple(pltpu.HBM(a.shape, a.dtype) for a in arrs),
        in_specs=[_HBM] * (ns + nl) + [_SEM, _SEM, _ANY],
        out_specs=tuple([_HBM] * (ns + nl)),
        input_output_aliases={i: i for i in range(ns + nl)},
        compiler_params=pltpu.CompilerParams(has_side_effects=_EFFECT),
    )(*arrs, send_sems, recv_sems, after)
    return list(out[:ns]), list(out[ns:])


def _chips(x, y):
    return [(1 - x, y), (x, 1 - y), (1 - x, 1 - y)]


def _gather_plan_ici(n):
    def plan(x, y, c):
        me = _lin(x, y, c)
        out = []
        for a in range(n):
            out.append((a, me, a, me, (x, y, 1 - c)))
            out += [(a, me, a, me, (cx, cy, c)) for cx, cy in _chips(x, y)]
        return out

    def recv_slots(x, y, c):
        out = []
        for _ in range(n):
            out.append(_lin(x, y, 1 - c))
            out += [_lin(cx, cy, c) for cx, cy in _chips(x, y)]
        return out

    return plan, recv_slots


def _gather_plan_d2d(n):
    def plan(x, y, c):
        return [(a, _lin(cx, cy, c), a, _lin(cx, cy, c), (x, y, 1 - c)) for a in range(n) for cx, cy in _chips(x, y)]

    def recv_slots(x, y, c):
        return [_lin(cx, cy, 1 - c) for _ in range(n) for cx, cy in _chips(x, y)]

    return plan, recv_slots


def _all_gather_split(lands, tag, after):
    n = len(lands)
    plan1, slots1 = _gather_plan_ici(n)
    s1, r1, _, lands, token = _split_start(f"gather_{tag}_ici_start", [], lands, plan1, after)

    def finish(after):
        _, got = _split_wait(f"gather_{tag}_ici_wait", s1, r1, [], lands, plan1, slots1, after)
        plan2, slots2 = _gather_plan_d2d(n)
        s2, r2, _, got, token2 = _split_start(f"gather_{tag}_d2d_start", [], got, plan2, None)
        return _split_wait(f"gather_{tag}_d2d_wait", s2, r2, [], got, plan2, slots2, token2)[1]

    return finish, token


_CHIP_LIST = ((0, 0), (0, 1), (1, 0), (1, 1))


def _reduce_plan_d2d(n):
    def plan(x, y, c):
        return [(a, _lin(qx, qy, 1 - c), a, q, (x, y, 1 - c)) for a in range(n) for q, (qx, qy) in enumerate(_CHIP_LIST)]

    def recv_slots(x, y, c):
        return [q for _ in range(n) for q in range(4)]

    return plan, recv_slots


def _reduce_plan_ici(n):
    def plan(x, y, c):
        return [(a, 2 * cx + cy, a, j, (cx, cy, c)) for a in range(n) for j, (cx, cy) in enumerate(_chips(x, y))]

    def recv_slots(x, y, c):
        return [j for _ in range(n) for j in range(3)]

    return plan, recv_slots


def _broadcast_plan():
    def peers(x, y, c):
        return [(1 - x if r & 4 else x, 1 - y if r & 2 else y, 1 - c if r & 1 else c) for r in range(1, N_DEV)]

    def plan(x, y, c):
        return [(0, None, 0, _lin(x, y, c), peer) for peer in peers(x, y, c)]

    def recv_slots(x, y, c):
        return [_lin(*peer) for peer in peers(x, y, c)]

    return plan, recv_slots


def _chip_partial(grads, recvd, core, name):
    n = len(grads)

    def body(c_ref, *refs):
        for a in range(n):
            g_ref, r_ref, o_ref = refs[a], refs[n + a], refs[2 * n + a]
            o_ref[...] = (g_ref[...].astype(F32) + r_ref[...].astype(F32)).astype(o_ref.dtype)

    def blk(a, own):
        zeros = (0,) * (a.ndim - 1)
        return pl.BlockSpec((None,) + a.shape[1:],
                            (lambda q, c_ref: (2 * q + c_ref[0],) + zeros) if own else (lambda q, c_ref: (q,) + zeros))

    return pl.pallas_call(
        body, name=name,
        grid_spec=pltpu.PrefetchScalarGridSpec(
            num_scalar_prefetch=1, grid=(4,),
            in_specs=[blk(a, True) for a in grads] + [blk(a, False) for a in recvd],
            out_specs=[blk(a, False) for a in recvd]),
        out_shape=[jax.ShapeDtypeStruct(a.shape, a.dtype) for a in recvd],
        compiler_params=_params(("arbitrary",)),
    )(core, *grads, *recvd)


def _reduce_scatter_split(grads, tag, core, behind):
    n = len(grads)
    plan1, slots1 = _reduce_plan_d2d(n)
    lands1 = [lax.empty((4,) + a.shape[1:], a.dtype) for a in grads]
    s1, r1, srcs1, lands1, token1 = _split_start(f"reduce_{tag}_d2d_start", grads, lands1, plan1, None)
    own, got = _split_wait(f"reduce_{tag}_d2d_wait", s1, r1, srcs1, lands1, plan1, slots1, behind(token1))
    parts = _chip_partial(own, got, core, f"reduce_{tag}_partial")
    plan2, slots2 = _reduce_plan_ici(n)
    lands2 = [lax.empty((3,) + a.shape[1:], a.dtype) for a in grads]
    s2, r2, srcs2, lands2, token2 = _split_start(f"reduce_{tag}_ici_start", parts, lands2, plan2, None)

    def finish(after):
        return _split_wait(f"reduce_{tag}_ici_wait", s2, r2, srcs2, lands2, plan2, slots2, after)

    return finish, token2


def _place_shards(me, shards, dtypes):
    n = len(shards)

    def body(me_ref, *refs):
        for a in range(n):
            refs[n + a][...] = refs[a][...].astype(dtypes[a])

    full = lambda s: pl.BlockSpec(s.shape, lambda i, me_ref: (0,) * s.ndim)
    slot = lambda s: pl.BlockSpec((None,) + s.shape, lambda i, me_ref: (me_ref[0],) + (0,) * s.ndim)
    return pl.pallas_call(
        body, name="place_shards",
        grid_spec=pltpu.PrefetchScalarGridSpec(num_scalar_prefetch=1, grid=(1,), in_specs=[full(s) for s in shards],
                                               out_specs=[slot(s) for s in shards]),
        out_shape=[jax.ShapeDtypeStruct((N_DEV,) + s.shape, d) for s, d in zip(shards, dtypes)],
        compiler_params=_params(("arbitrary",)),
    )(me, *shards)


def _mix_in_fwd(x, g1, w_in_t, conv_w, gq, gk, gconv):
    tm = 256
    n_t = SEQ // tm

    def body(x_ref, g1_ref, w_ref, cw_ref, gq_ref, gk_ref, gc_ref,
             proj_ref, ycn_ref, qn_ref, kn_ref, v_ref, halo_ref):
        @pl.when(pl.program_id(0) == 0)
        def _():
            halo_ref[...] = jnp.zeros_like(halo_ref)

        xv = x_ref[...]
        u = (xv * _rstd(xv) * g1_ref[...]).astype(BF16)
        proj = _dot_nt(u, w_ref[...])
        proj_ref[...] = proj
        gate_b = proj[:, 0:CONV_WIDTH]
        a = proj[:, CONV_WIDTH:2 * CONV_WIDTH] * proj[:, 2 * CONV_WIDTH:3 * CONV_WIDTH]
        cv, _, _ = _conv3(a, _taps(cw_ref[...]), halo_ref[...])
        halo_ref[...] = a[tm - SUBLANES:]
        yc = gate_b * cv
        ycn_ref[...] = (yc * _rstd(yc) * gc_ref[...]).astype(BF16)
        q0 = 3 * CONV_WIDTH
        qn_ref[...] = _head_norm(proj[:, q0:q0 + ATTN_WIDTH], gq_ref[...], N_HEADS).astype(BF16)
        k0 = q0 + ATTN_WIDTH
        kn_ref[...] = _head_norm(proj[:, k0:k0 + KV_WIDTH], gk_ref[...], 2).astype(BF16)
        v_ref[...] = proj[:, k0 + KV_WIDTH:k0 + 2 * KV_WIDTH].astype(BF16)

    const = lambda shape: pl.BlockSpec(shape, lambda i: (0,) * len(shape))
    rows = lambda w: pl.BlockSpec((tm, w), lambda i: (i, 0))
    return pl.pallas_call(
        body, name="mix_in_fwd", grid=(n_t,),
        in_specs=[rows(D_MODEL), const((1, D_MODEL)), const((IN_WIDTH, D_MODEL)), const((3, CONV_WIDTH)),
                  const((1, HEAD_DIM)), const((1, HEAD_DIM)), const((1, CONV_WIDTH))],
        out_specs=[rows(IN_WIDTH), rows(CONV_WIDTH), rows(ATTN_WIDTH), rows(KV_WIDTH), rows(KV_WIDTH)],
        out_shape=[jax.ShapeDtypeStruct((SEQ, IN_WIDTH), F32), jax.ShapeDtypeStruct((SEQ, CONV_WIDTH), BF16),
                   jax.ShapeDtypeStruct((SEQ, ATTN_WIDTH), BF16), jax.ShapeDtypeStruct((SEQ, KV_WIDTH), BF16),
                   jax.ShapeDtypeStruct((SEQ, KV_WIDTH), BF16)],
        scratch_shapes=[pltpu.VMEM((SUBLANES, CONV_WIDTH), F32)],
        compiler_params=_params(("arbitrary",)),
    )(x, g1, w_in_t, conv_w, gq, gk, gconv)


def _band_bias(tbl_ref, bkt, bias_ref):
    for h in range(N_HEADS):
        acc = jnp.zeros(bkt.shape, F32)
        for b in range(NUM_BUCKETS):
            acc = jnp.where(bkt == b, tbl_ref[h, b], acc)
        bias_ref[h] = acc


def _band_valid(i):
    qi = lax.broadcasted_iota(jnp.int32, (BLK, 2 * BLK), 0)
    ji = lax.broadcasted_iota(jnp.int32, (BLK, 2 * BLK), 1)
    dist = qi + BLK - ji
    return (dist >= 0) & (dist < BLK) & ((ji >= BLK) | (i > 0))


def _band_rows(ref, i):
    prev = pl.multiple_of(jnp.maximum(i - 1, 0) * BLK, BLK)
    cur = pl.multiple_of(i * BLK, BLK)
    return jnp.concatenate([ref[pl.ds(prev, BLK), :], ref[pl.ds(cur, BLK), :]], axis=0), prev, cur


def _head_probs(qh, kh, bias, valid, sink):
    logits = _dot_nt(qh, kh) * (HEAD_DIM ** -0.5) + bias
    logits = jnp.where(valid, logits, NEG_INF)
    m = jnp.maximum(jnp.max(logits, axis=-1, keepdims=True), sink)
    p = jnp.exp(logits - m)
    es = jnp.exp(sink - m)
    den = jnp.sum(p, axis=-1, keepdims=True) + es
    return p / den, es / den


def _attn_fwd(qn, kn, v, tbl, sinks, bkt, gattn):
    n_b = SEQ // BLK

    def body(q_ref, k_ref, v_ref, tbl_ref, sink_ref, bkt_ref, ga_ref, y_ref, yn_ref, bias_ref):
        i = pl.program_id(0)

        @pl.when(i == 0)
        def _():
            _band_bias(tbl_ref, bkt_ref[...], bias_ref)

        kb, _, _ = _band_rows(k_ref, i)
        vb, _, _ = _band_rows(v_ref, i)
        valid = _band_valid(i)
        q = q_ref[...]
        outs = []
        for h in range(N_HEADS):
            hk = h // GQA_GROUP
            kv = slice(HEAD_DIM * hk, HEAD_DIM * (hk + 1))
            probs, _ = _head_probs(q[:, HEAD_DIM * h:HEAD_DIM * (h + 1)], kb[:, kv], bias_ref[h], valid, sink_ref[0, h])
            outs.append(_dot(probs.astype(BF16), vb[:, kv]))
        y = jnp.concatenate(outs, axis=-1)
        y_ref[...] = y
        yn_ref[...] = (y * _rstd(y) * ga_ref[...]).astype(BF16)

    const = lambda shape: pl.BlockSpec(shape, lambda i: (0,) * len(shape))
    rows = lambda w: pl.BlockSpec((BLK, w), lambda i: (i, 0))
    smem = pl.BlockSpec(memory_space=pltpu.SMEM)
    return pl.pallas_call(
        body, name="attn_fwd", grid=(n_b,),
        in_specs=[rows(ATTN_WIDTH), const((SEQ, KV_WIDTH)), const((SEQ, KV_WIDTH)), smem, smem,
                  const((BLK, 2 * BLK)), const((1, ATTN_WIDTH))],
        out_specs=[rows(ATTN_WIDTH), rows(ATTN_WIDTH)],
        out_shape=[jax.ShapeDtypeStruct((SEQ, ATTN_WIDTH), F32), jax.ShapeDtypeStruct((SEQ, ATTN_WIDTH), BF16)],
        scratch_shapes=[pltpu.VMEM((N_HEADS, BLK, 2 * BLK), F32)],
        compiler_params=_params(("arbitrary",)),
    )(qn, kn, v, tbl, sinks, bkt, gattn)


def _ffn_fwd(x, ycn, yan, w_out, g2, w_up, fcw, fcb, w_down, tgt):
    tm = 256
    n_t = SEQ // tm

    def body(x_ref, ycn_ref, yan_ref, wo_ref, g2_ref, wu_ref, cw_ref, b_ref, wd_ref, tgt_ref,
             h1_ref, u2_ref, up_ref, dh2_ref, dh2b_ref, loss_ref, acc_ref, halo_ref):
        i, j = pl.program_id(0), pl.program_id(1)

        @pl.when((i == 0) & (j == 0))
        def _():
            loss_ref[...] = jnp.zeros_like(loss_ref)

        @pl.when(j == 0)
        def _():
            h1 = x_ref[...] + _dot(ycn_ref[...], wo_ref[0:CONV_WIDTH, :]) + _dot(yan_ref[...], wo_ref[CONV_WIDTH:, :])
            h1_ref[...] = h1
            u2_ref[...] = (h1 * _rstd(h1) * g2_ref[...]).astype(BF16)
            acc_ref[...] = jnp.zeros_like(acc_ref)

        u2 = u2_ref[...]
        pre = []
        for s in range(2):
            up = _dot_nt(u2, wu_ref[s])
            up_ref[s] = up
            halo = jnp.where(i == 0, 0.0, halo_ref[s, j])
            pre.append(_conv3(up, _taps(cw_ref.at[s]), halo)[0] + b_ref[s])
            halo_ref[s, j] = up[tm - SUBLANES:]
        g, val = pre
        act = (g * jax.nn.sigmoid(g) * val).astype(BF16)
        acc_ref[...] += _dot(act, wd_ref[...])

        @pl.when(j == N_FFN_BLK - 1)
        def _():
            err = h1_ref[...] + acc_ref[...] - tgt_ref[...]
            loss_ref[...] += 0.5 * jnp.sum(err * err) / D_MODEL
            dh2 = err / D_MODEL
            dh2_ref[...] = dh2
            dh2b_ref[...] = dh2.astype(BF16)

    rows = lambda w: pl.BlockSpec((tm, w), lambda i, j: (i, 0))
    const = lambda shape: pl.BlockSpec(shape, lambda i, j: (0,) * len(shape))
    pair = lambda *s: pl.BlockSpec((2, None) + s, lambda i, j: (0, j) + (0,) * len(s))
    upb = pl.BlockSpec((2, None, tm, FFN_BLK), lambda i, j: (0, j, i, 0))
    return pl.pallas_call(
        body, name="ffn_fwd", grid=(n_t, N_FFN_BLK),
        in_specs=[rows(D_MODEL), rows(CONV_WIDTH), rows(ATTN_WIDTH), const((D_MODEL, D_MODEL)), const((1, D_MODEL)),
                  pair(FFN_BLK, D_MODEL), pair(3, 1, FFN_BLK), pair(1, FFN_BLK),
                  pl.BlockSpec((None, FFN_BLK, D_MODEL), lambda i, j: (j, 0, 0)), rows(D_MODEL)],
        out_specs=[rows(D_MODEL), rows(D_MODEL), upb, rows(D_MODEL), rows(D_MODEL), const((SUBLANES, 128))],
        out_shape=[jax.ShapeDtypeStruct((SEQ, D_MODEL), F32), jax.ShapeDtypeStruct((SEQ, D_MODEL), BF16),
                   jax.ShapeDtypeStruct((2, N_FFN_BLK, SEQ, FFN_BLK), F32),
                   jax.ShapeDtypeStruct((SEQ, D_MODEL), F32), jax.ShapeDtypeStruct((SEQ, D_MODEL), BF16),
                   jax.ShapeDtypeStruct((SUBLANES, 128), F32)],
        scratch_shapes=[pltpu.VMEM((tm, D_MODEL), F32), pltpu.VMEM((2, N_FFN_BLK, SUBLANES, FFN_BLK), F32)],
        compiler_params=_params(("arbitrary", "arbitrary")),
    )(x, ycn, yan, w_out, g2, w_up, fcw, fcb, w_down, tgt)


def _ffn_bwd(dh2, dh2b, h1, g2, up, w_up, fcw, fcb, w_down):
    tm = 256
    n_t = SEQ // tm
    halo_blocks = tm // SUBLANES

    def body(dh2_ref, dh2b_ref, h1_ref, g2_ref, up_ref, uph_ref, wu_ref, cw_ref, b_ref, wd_ref,
             act_ref, dup_ref, dh1_ref, dh1b_ref, dfb_ref, dfcw_ref, dg2_ref, acc_ref, next_ref):
        i, j = pl.program_id(0), pl.program_id(1)
        first_tile = i == n_t - 1

        @pl.when((i == 0) & (j == 0))
        def _():
            dfb_ref[...] = jnp.zeros_like(dfb_ref)
            dfcw_ref[...] = jnp.zeros_like(dfcw_ref)
            dg2_ref[...] = jnp.zeros_like(dg2_ref)

        @pl.when(j == 0)
        def _():
            acc_ref[...] = jnp.zeros_like(acc_ref)

        ups, pre = [], []
        for s in range(2):
            up = up_ref[s]
            halo = jnp.where(first_tile, 0.0, uph_ref[s])
            p, up2, up1 = _conv3(up, _taps(cw_ref.at[s]), halo)
            pre.append(p + b_ref[s])
            ups.append((up, up1, up2))
        g, val = pre
        sg = jax.nn.sigmoid(g)
        silu = g * sg
        act_ref[...] = (silu * val).astype(BF16)
        dact = _dot_nt(dh2b_ref[...], wd_ref[...])
        dpre = (dact * val * (sg * (1.0 + g * (1.0 - sg))), dact * silu)
        for s in range(2):
            d = dpre[s]
            u, u1, u2 = ups[s]
            dfb_ref[s, j] += jnp.sum(d, axis=0, keepdims=True)
            dfcw_ref[s, j, 0] += jnp.sum(d * u2, axis=0, keepdims=True)
            dfcw_ref[s, j, 1] += jnp.sum(d * u1, axis=0, keepdims=True)
            dfcw_ref[s, j, 2] += jnp.sum(d * u, axis=0, keepdims=True)
            nxt = jnp.where(i == 0, 0.0, next_ref[s, j])
            dup = _conv3_bwd_input(d, _taps(cw_ref.at[s]), nxt).astype(BF16)
            next_ref[s, j] = d[:SUBLANES]
            dup_ref[s] = dup
            acc_ref[...] += _dot(dup, wu_ref[s])

        @pl.when(j == N_FFN_BLK - 1)
        def _():
            dn, dgain = _rms_bwd(h1_ref[...], g2_ref[...], acc_ref[...])
            dh1 = dh2_ref[...] + dn
            dh1_ref[...] = dh1
            dh1b_ref[...] = dh1.astype(BF16)
            dg2_ref[...] += dgain

    rev = lambda i: n_t - 1 - i
    rows = lambda w: pl.BlockSpec((tm, w), lambda i, j: (rev(i), 0))
    const = lambda shape: pl.BlockSpec(shape, lambda i, j: (0,) * len(shape))
    pair = lambda *s: pl.BlockSpec((2, None) + s, lambda i, j: (0, j) + (0,) * len(s))
    upb = pl.BlockSpec((2, None, tm, FFN_BLK), lambda i, j: (0, j, rev(i), 0))
    halo = pl.BlockSpec((2, None, SUBLANES, FFN_BLK), lambda i, j: (0, j, jnp.maximum(rev(i) * halo_blocks - 1, 0), 0))
    return pl.pallas_call(
        body, name="ffn_bwd", grid=(n_t, N_FFN_BLK),
        in_specs=[rows(D_MODEL), rows(D_MODEL), rows(D_MODEL), const((1, D_MODEL)), upb, halo,
                  pair(FFN_BLK, D_MODEL), pair(3, 1, FFN_BLK), pair(1, FFN_BLK),
                  pl.BlockSpec((None, FFN_BLK, D_MODEL), lambda i, j: (j, 0, 0))],
        out_specs=[pl.BlockSpec((None, tm, FFN_BLK), lambda i, j: (j, rev(i), 0)), upb, rows(D_MODEL), rows(D_MODEL),
                   const((2, N_FFN_BLK, 1, FFN_BLK)), const((2, N_FFN_BLK, 3, 1, FFN_BLK)), const((1, D_MODEL))],
        out_shape=[jax.ShapeDtypeStruct((N_FFN_BLK, SEQ, FFN_BLK), BF16),
                   jax.ShapeDtypeStruct((2, N_FFN_BLK, SEQ, FFN_BLK), BF16), jax.ShapeDtypeStruct((SEQ, D_MODEL), F32),
                   jax.ShapeDtypeStruct((SEQ, D_MODEL), BF16), jax.ShapeDtypeStruct((2, N_FFN_BLK, 1, FFN_BLK), F32),
                   jax.ShapeDtypeStruct((2, N_FFN_BLK, 3, 1, FFN_BLK), F32), jax.ShapeDtypeStruct((1, D_MODEL), F32)],
        scratch_shapes=[pltpu.VMEM((tm, D_MODEL), F32), pltpu.VMEM((2, N_FFN_BLK, SUBLANES, FFN_BLK), F32)],
        compiler_params=_params(("arbitrary", "arbitrary")),
    )(dh2, dh2b, h1, g2, up, up, w_up, fcw, fcb, w_down)


def _grad_tn(a_list, b, out_rows, name, after=None):
    n = len(a_list)
    ncol = b.shape[1]

    def body(*refs):
        a_refs, b_ref, o_ref = refs[:n], refs[n], refs[n + 1]
        j = pl.program_id(0)
        for k in range(n):
            @pl.when(j == k)
            def _(k=k):
                o_ref[...] = _dot_tn(a_refs[k][...], b_ref[...]).astype(BF16)

    full = lambda shape: pl.BlockSpec(shape, lambda j: (0,) * len(shape))
    body, more_specs, more = _ordered_behind(body, n + 1, after)
    return pl.pallas_call(
        body, name=name, grid=(n,),
        in_specs=[full((SEQ, out_rows))] * n + [full((SEQ, ncol))] + more_specs,
        out_specs=pl.BlockSpec((None, out_rows, ncol), lambda j: (j, 0, 0)),
        out_shape=jax.ShapeDtypeStruct((n, out_rows, ncol), BF16),
        compiler_params=_params(("arbitrary",)),
    )(*a_list, b, *more)


def _grad_tn_blocked(a, b, name, a_is_blocked):
    nb = a.shape[0] if a_is_blocked else b.shape[0]
    a_w, b_w = a.shape[-1], b.shape[-1]

    def body(a_ref, b_ref, o_ref):
        o_ref[...] = _dot_tn(a_ref[...], b_ref[...]).astype(BF16)

    blocked = lambda w: pl.BlockSpec((None, SEQ, w), lambda k: (k, 0, 0))
    full = lambda w: pl.BlockSpec((SEQ, w), lambda k: (0, 0))
    return pl.pallas_call(
        body, name=name, grid=(nb,),
        in_specs=[blocked(a_w) if a_is_blocked else full(a_w), full(b_w) if a_is_blocked else blocked(b_w)],
        out_specs=pl.BlockSpec((None, a_w, b_w), lambda k: (k, 0, 0)),
        out_shape=jax.ShapeDtypeStruct((nb, a_w, b_w), BF16),
        compiler_params=_params(("arbitrary",)),
    )(a, b)


def _out_bwd(dh1b, w_out, y_attn, gattn, after=None):
    tm = 256
    n_t = SEQ // tm

    def body(dh_ref, wo_ref, y_ref, ga_ref, dycn_ref, dy_ref, dga_ref):
        @pl.when(pl.program_id(0) == 0)
        def _():
            dga_ref[...] = jnp.zeros_like(dga_ref)

        dycat = _dot_nt(dh_ref[...], wo_ref[...])
        dycn_ref[...] = dycat[:, :CONV_WIDTH]
        dy, dga = _rms_bwd(y_ref[...], ga_ref[...], dycat[:, CONV_WIDTH:])
        dy_ref[...] = dy
        dga_ref[...] += dga

    rows = lambda w: pl.BlockSpec((tm, w), lambda i: (i, 0))
    const = lambda shape: pl.BlockSpec(shape, lambda i: (0,) * len(shape))
    body, more_specs, more = _ordered_behind(body, 4, after)
    return pl.pallas_call(
        body, name="out_bwd", grid=(n_t,),
        in_specs=[rows(D_MODEL), const((D_MODEL, D_MODEL)), rows(ATTN_WIDTH), const((1, ATTN_WIDTH))] + more_specs,
        out_specs=[rows(CONV_WIDTH), rows(ATTN_WIDTH), const((1, ATTN_WIDTH))],
        out_shape=[jax.ShapeDtypeStruct((SEQ, CONV_WIDTH), F32), jax.ShapeDtypeStruct((SEQ, ATTN_WIDTH), F32),
                   jax.ShapeDtypeStruct((1, ATTN_WIDTH), F32)],
        compiler_params=_params(("arbitrary",)),
    )(dh1b, w_out, y_attn, gattn, *more)


def _attn_bwd(qn, kn, v, dy, tbl, sinks, bkt, after=None):
    n_b = SEQ // BLK

    def body(q_ref, k_ref, v_ref, dy_ref, tbl_ref, sink_ref, bkt_ref,
             dq_ref, dk_ref, dv_ref, dtbl_ref, dsink_ref, bias_ref, dbias_ref, dsacc_ref):
        i = pl.program_id(0)

        @pl.when(i == 0)
        def _():
            _band_bias(tbl_ref, bkt_ref[...], bias_ref)
            dbias_ref[...] = jnp.zeros_like(dbias_ref)
            dsacc_ref[...] = jnp.zeros_like(dsacc_ref)
            dk_ref[...] = jnp.zeros_like(dk_ref)
            dv_ref[...] = jnp.zeros_like(dv_ref)

        kb, prev, cur = _band_rows(k_ref, i)
        vb, _, _ = _band_rows(v_ref, i)
        valid = _band_valid(i)
        q = q_ref[...]
        dy = dy_ref[...]
        lane = lax.broadcasted_iota(jnp.int32, (BLK, 128), 1)
        dqs, dks, dvs = [], [], []
        dsink = jnp.zeros((BLK, 128), F32)
        for h in range(N_HEADS):
            hk = h // GQA_GROUP
            kv = slice(HEAD_DIM * hk, HEAD_DIM * (hk + 1))
            qh = q[:, HEAD_DIM * h:HEAD_DIM * (h + 1)]
            probs, psink = _head_probs(qh, kb[:, kv], bias_ref[h], valid, sink_ref[0, h])
            doh = dy[:, HEAD_DIM * h:HEAD_DIM * (h + 1)].astype(BF16)
            dprobs = _dot_nt(doh, vb[:, kv])
            dvh = _dot_tn(probs.astype(BF16), doh)
            dsum = jnp.sum(probs * dprobs, axis=-1, keepdims=True)
            dlogits = probs * (dprobs - dsum)
            dsink = jnp.where(lane == h, -psink * dsum, dsink)
            dbias_ref[h] += dlogits
            ds = (dlogits * (HEAD_DIM ** -0.5)).astype(BF16)
            dqs.append(_dot(ds, kb[:, kv]))
            dkh = _dot_tn(ds, qh)
            if h % GQA_GROUP == 0:
                dks.append(dkh)
                dvs.append(dvh)
            else:
                dks[hk] = dks[hk] + dkh
                dvs[hk] = dvs[hk] + dvh
        dq_ref[...] = jnp.concatenate(dqs, axis=-1)
        dsacc_ref[...] += dsink
        dkb = jnp.concatenate(dks, axis=-1)
        dvb = jnp.concatenate(dvs, axis=-1)
        dk_ref[pl.ds(prev, BLK), :] += dkb[:BLK]
        dk_ref[pl.ds(cur, BLK), :] += dkb[BLK:]
        dv_ref[pl.ds(prev, BLK), :] += dvb[:BLK]
        dv_ref[pl.ds(cur, BLK), :] += dvb[BLK:]

        @pl.when(i == n_b - 1)
        def _():
            dsink_ref[...] = jnp.sum(dsacc_ref[...], axis=0, keepdims=True)
            bkt = bkt_ref[...]
            row8 = lax.broadcasted_iota(jnp.int32, (N_HEADS, 128), 0)
            lane8 = lax.broadcasted_iota(jnp.int32, (N_HEADS, 128), 1)
            acc = jnp.zeros((N_HEADS, 128), F32)
            for h in range(N_HEADS):
                dbh = dbias_ref[h]
                for b in range(NUM_BUCKETS):
                    acc = jnp.where((row8 == h) & (lane8 == b), jnp.sum(jnp.where(bkt == b, dbh, 0.0)), acc)
            dtbl_ref[...] = acc

    const = lambda shape: pl.BlockSpec(shape, lambda i: (0,) * len(shape))
    rows = lambda w: pl.BlockSpec((BLK, w), lambda i: (i, 0))
    smem = pl.BlockSpec(memory_space=pltpu.SMEM)
    body, more_specs, more = _ordered_behind(body, 7, after)
    return pl.pallas_call(
        body, name="attn_bwd", grid=(n_b,),
        in_specs=[rows(ATTN_WIDTH), const((SEQ, KV_WIDTH)), const((SEQ, KV_WIDTH)), rows(ATTN_WIDTH), smem, smem,
                  const((BLK, 2 * BLK))] + more_specs,
        out_specs=[rows(ATTN_WIDTH), const((SEQ, KV_WIDTH)), const((SEQ, KV_WIDTH)), const((N_HEADS, 128)), const((1, 128))],
        out_shape=[jax.ShapeDtypeStruct((SEQ, ATTN_WIDTH), F32), jax.ShapeDtypeStruct((SEQ, KV_WIDTH), F32),
                   jax.ShapeDtypeStruct((SEQ, KV_WIDTH), F32), jax.ShapeDtypeStruct((N_HEADS, 128), F32),
                   jax.ShapeDtypeStruct((1, 128), F32)],
        scratch_shapes=[pltpu.VMEM((N_HEADS, BLK, 2 * BLK), F32), pltpu.VMEM((N_HEADS, BLK, 2 * BLK), F32),
                        pltpu.VMEM((BLK, 128), F32)],
        compiler_params=_params(("arbitrary",)),
    )(qn, kn, v, dy, tbl, sinks, bkt, *more)


def _mix_in_bwd(x, dh1, proj, dycn, dqn, dkn, dv, w_in_t, conv_w, g1, gq, gk, gconv):
    tm = 256
    n_t = SEQ // tm
    halo_blocks = tm // SUBLANES

    def body(x_ref, dh1_ref, proj_ref, halo_ref, dycn_ref, dqn_ref, dkn_ref, dv_ref, w_ref, cw_ref,
             g1_ref, gq_ref, gk_ref, gc_ref,
             dx_ref, dproj_ref, u1_ref, dcw_ref, dgc_ref, dgq_ref, dgk_ref, dg1_ref, next_ref):
        i = pl.program_id(0)
        first_tile = i == n_t - 1

        @pl.when(i == 0)
        def _():
            for r in (dcw_ref, dgc_ref, dgq_ref, dgk_ref, dg1_ref, next_ref):
                r[...] = jnp.zeros_like(r)

        proj = proj_ref[...]
        hp = halo_ref[...]
        gate_b = proj[:, 0:CONV_WIDTH]
        gate_c = proj[:, CONV_WIDTH:2 * CONV_WIDTH]
        hc = proj[:, 2 * CONV_WIDTH:3 * CONV_WIDTH]
        a = gate_c * hc
        a_halo = jnp.where(first_tile, 0.0, hp[:, CONV_WIDTH:2 * CONV_WIDTH] * hp[:, 2 * CONV_WIDTH:3 * CONV_WIDTH])
        cw = _taps(cw_ref[...])
        cv, a2, a1 = _conv3(a, cw, a_halo)
        dyc, dgc = _rms_bwd(gate_b * cv, gc_ref[...], dycn_ref[...])
        dgc_ref[...] += dgc
        dcv = dyc * gate_b
        dcw_ref[...] += jnp.concatenate(
            [jnp.sum(dcv * a2, axis=0, keepdims=True), jnp.sum(dcv * a1, axis=0, keepdims=True),
             jnp.sum(dcv * a, axis=0, keepdims=True)], axis=0)
        da = _conv3_bwd_input(dcv, cw, next_ref[...])
        next_ref[...] = dcv[:SUBLANES]
        q0 = 3 * CONV_WIDTH
        k0 = q0 + ATTN_WIDTH
        dq, dgq = _head_norm_bwd(proj[:, q0:k0], gq_ref[...], dqn_ref[...], N_HEADS)
        dk, dgk = _head_norm_bwd(proj[:, k0:k0 + KV_WIDTH], gk_ref[...], dkn_ref[...], 2)
        dgq_ref[...] += dgq
        dgk_ref[...] += dgk
        dproj = jnp.concatenate([dyc * cv, da * hc, da * gate_c, dq, dk, dv_ref[...]], axis=-1).astype(BF16)
        dproj_ref[...] = dproj
        du1 = _dot(dproj, w_ref[...])
        xv = x_ref[...]
        dn, dg1 = _rms_bwd(xv, g1_ref[...], du1)
        dx_ref[...] = dh1_ref[...] + dn
        dg1_ref[...] += dg1
        u1_ref[...] = (xv * _rstd(xv) * g1_ref[...]).astype(BF16)

    rev = lambda i: n_t - 1 - i
    rows = lambda w: pl.BlockSpec((tm, w), lambda i: (rev(i), 0))
    const = lambda shape: pl.BlockSpec(shape, lambda i: (0,) * len(shape))
    halo = pl.BlockSpec((SUBLANES, IN_WIDTH), lambda i: (jnp.maximum(rev(i) * halo_blocks - 1, 0), 0))
    return pl.pallas_call(
        body, name="mix_in_bwd", grid=(n_t,),
        in_specs=[rows(D_MODEL), rows(D_MODEL), rows(IN_WIDTH), halo, rows(CONV_WIDTH), rows(ATTN_WIDTH), rows(KV_WIDTH),
                  rows(KV_WIDTH), const((IN_WIDTH, D_MODEL)), const((3, CONV_WIDTH)), const((1, D_MODEL)),
                  const((1, HEAD_DIM)), const((1, HEAD_DIM)), const((1, CONV_WIDTH))],
        out_specs=[rows(D_MODEL), rows(IN_WIDTH), rows(D_MODEL), const((3, CONV_WIDTH)), const((1, CONV_WIDTH)),
                   const((1, HEAD_DIM)), const((1, HEAD_DIM)), const((1, D_MODEL))],
        out_shape=[jax.ShapeDtypeStruct((SEQ, D_MODEL), F32), jax.ShapeDtypeStruct((SEQ, IN_WIDTH), BF16),
                   jax.ShapeDtypeStruct((SEQ, D_MODEL), BF16), jax.ShapeDtypeStruct((3, CONV_WIDTH), F32),
                   jax.ShapeDtypeStruct((1, CONV_WIDTH), F32), jax.ShapeDtypeStruct((1, HEAD_DIM), F32),
                   jax.ShapeDtypeStruct((1, HEAD_DIM), F32), jax.ShapeDtypeStruct((1, D_MODEL), F32)],
        scratch_shapes=[pltpu.VMEM((SUBLANES, CONV_WIDTH), F32)],
        compiler_params=_params(("arbitrary",)),
    )(x, dh1, proj, proj, dycn, dqn, dkn, dv, w_in_t, conv_w, g1, gq, gk, gconv)


def _grad_w_in(dproj, u1, after=None):
    bw = 768

    def body(a_ref, b_ref, o_ref):
        o_ref[...] = _dot_tn(a_ref[...], b_ref[...]).astype(BF16)

    body, more_specs, more = _ordered_behind(body, 2, after)
    return pl.pallas_call(
        body, name="grad_w_in", grid=(IN_WIDTH // bw,),
        in_specs=[pl.BlockSpec((SEQ, bw), lambda k: (0, k)), pl.BlockSpec((SEQ, D_MODEL), lambda k: (0, 0))] + more_specs,
        out_specs=pl.BlockSpec((bw, D_MODEL), lambda k: (k, 0)),
        out_shape=jax.ShapeDtypeStruct((IN_WIDTH, D_MODEL), BF16),
        compiler_params=_params(("arbitrary",)),
    )(dproj, u1, *more)


def _adamw_math(w, g, m, v):
    m = ADAM_B1 * m + (1.0 - ADAM_B1) * g
    v = ADAM_B2 * v + (1.0 - ADAM_B2) * (g * g)
    m_hat = m / (1.0 - ADAM_B1 ** ADAM_STEP)
    v_hat = v / (1.0 - ADAM_B2 ** ADAM_STEP)
    return -ADAM_LR * (m_hat / (jnp.sqrt(v_hat) + ADAM_EPS) + ADAM_WD * w), m, v


_ROW_G1, _ROW_G2, _ROW_OUT_NORMS, _ROW_FFN_B, _ROW_GQ, _ROW_GK, _ROW_SINKS, _ROW_LOSS, _ROW_TABLE = 0, 1, 2, 3, 11, 12, 13, 14, 16
SMALL_ROWS, SMALL_COLS = 24, 1024
_SMALL_NAMES = ("norm_mix_g", "norm_ffn_g", "out_norm_conv_g", "out_norm_attn_g", "ffn_conv_b", "q_norm_g", "k_norm_g",
                "sinks", "rel_bias_table")


def _pack_small_grads(dg1, dg2, dgconv, dgattn, dfb, dgq, dgk, dsinks, dtbl_t, loss_acc):
    def body(dg1_ref, dg2_ref, dgc_ref, dga_ref, dfb_ref, dgq_ref, dgk_ref, ds_ref, dt_ref, loss_ref, o_ref):
        o_ref[...] = jnp.zeros_like(o_ref)
        o_ref[_ROW_G1:_ROW_G1 + 1, :] = dg1_ref[...]
        o_ref[_ROW_G2:_ROW_G2 + 1, :] = dg2_ref[...]
        o_ref[_ROW_OUT_NORMS:_ROW_OUT_NORMS + 1, 0:CONV_WIDTH] = dgc_ref[...]
        o_ref[_ROW_OUT_NORMS:_ROW_OUT_NORMS + 1, CONV_WIDTH:] = dga_ref[...]
        for k in range(N_DEV):
            o_ref[_ROW_FFN_B + k:_ROW_FFN_B + k + 1, 0:FFN_BLK] = dfb_ref[k // N_FFN_BLK, k % N_FFN_BLK]
        o_ref[_ROW_GQ:_ROW_GQ + 1, 0:HEAD_DIM] = dgq_ref[...]
        o_ref[_ROW_GK:_ROW_GK + 1, 0:HEAD_DIM] = dgk_ref[...]
        o_ref[_ROW_SINKS:_ROW_SINKS + 1, 0:128] = ds_ref[...]
        o_ref[_ROW_LOSS:_ROW_LOSS + 1, 0:128] = loss_ref[0:1, :]
        o_ref[_ROW_TABLE:_ROW_TABLE + N_HEADS, 0:128] = dt_ref[...]

    return pl.pallas_call(body, name="pack_small_grads", out_shape=jax.ShapeDtypeStruct((SMALL_ROWS, SMALL_COLS), F32))(
        dg1, dg2, dgconv, dgattn, dfb, dgq, dgk, dsinks, dtbl_t, loss_acc)


def _adamw_small(recv, params, after):
    names = _SMALL_NAMES
    n = len(names)

    def grad_of(g, name, k=None):
        if name == "norm_mix_g":
            return g[_ROW_G1:_ROW_G1 + 1, :]
        if name == "norm_ffn_g":
            return g[_ROW_G2:_ROW_G2 + 1, :]
        if name == "out_norm_conv_g":
            return g[_ROW_OUT_NORMS:_ROW_OUT_NORMS + 1, 0:CONV_WIDTH]
        if name == "out_norm_attn_g":
            return g[_ROW_OUT_NORMS:_ROW_OUT_NORMS + 1, CONV_WIDTH:]
        if name == "ffn_conv_b":
            return g[_ROW_FFN_B + k:_ROW_FFN_B + k + 1, 0:FFN_BLK]
        if name == "q_norm_g":
            return g[_ROW_GQ:_ROW_GQ + 1, 0:HEAD_DIM]
        if name == "k_norm_g":
            return g[_ROW_GK:_ROW_GK + 1, 0:HEAD_DIM]
        if name == "sinks":
            return g[_ROW_SINKS:_ROW_SINKS + 1, 0:N_HEADS]
        return g[_ROW_TABLE:_ROW_TABLE + N_HEADS, 0:NUM_BUCKETS]

    def body(r_ref, *refs):
        ins, outs, loss_ref = refs[:3 * n], refs[3 * n:7 * n], refs[7 * n]
        g = r_ref[0]
        for s in range(1, N_DEV):
            g = g + r_ref[s]
        loss_ref[...] = g[_ROW_LOSS:_ROW_LOSS + 1, 0:128]
        for i, name in enumerate(names):
            w_ref, m_ref, v_ref = ins[3 * i:3 * i + 3]
            o = outs[4 * i:4 * i + 4]
            cols = [slice(FFN_BLK * k, FFN_BLK * (k + 1)) for k in range(N_DEV)] if name == "ffn_conv_b" else [slice(None)]
            for k, cs in enumerate(cols):
                gk = grad_of(g, name, k)
                d, m2, v2 = _adamw_math(w_ref[:, cs], gk, m_ref[:, cs], v_ref[:, cs])
                o[0][:, cs], o[1][:, cs], o[2][:, cs], o[3][:, cs] = gk, d, m2, v2

    flat = [a for name in names for a in params[name]]
    body, more_specs, more = _ordered_behind(body, 1 + 3 * n, after)
    vmem = pl.BlockSpec(memory_space=pltpu.VMEM)
    out = pl.pallas_call(
        body, name="adamw_small",
        in_specs=[vmem] * (1 + 3 * n) + more_specs,
        out_shape=[jax.ShapeDtypeStruct(params[name][0].shape, F32) for name in names for _ in range(4)]
        + [jax.ShapeDtypeStruct((1, 128), F32)],
        compiler_params=pltpu.CompilerParams(vmem_limit_bytes=VMEM_LIMIT),
    )(recv, *flat, *more)
    return {name: tuple(out[4 * i:4 * i + 4]) for i, name in enumerate(names)}, out[4 * n]


def _adamw(w, m, v, part, recv, chip, name, row_blocks=1, after=None):
    rb = w.shape[0] // row_blocks
    tail = w.shape[1:]
    zeros = (0,) * len(tail)

    def body(chip_ref, w_ref, m_ref, v_ref, p_ref, r_ref, g_o, d_o, m_o, v_o):
        g = p_ref[...].astype(F32)
        for s in range(3):
            g = g + r_ref[s].astype(F32)
        g_o[...] = g
        d_o[...], m_o[...], v_o[...] = _adamw_math(w_ref[...], g, m_ref[...], v_ref[...])

    blk = pl.BlockSpec((rb,) + tail, lambda i, chip_ref: (i,) + zeros)
    pblk = pl.BlockSpec((None, rb) + tail, lambda i, chip_ref: (chip_ref[0], i) + zeros)
    rblk = pl.BlockSpec((3, rb) + tail, lambda i, chip_ref: (0, i) + zeros)
    body, more_specs, more = _ordered_behind(body, 6, after)
    return pl.pallas_call(
        body, name=name,
        grid_spec=pltpu.PrefetchScalarGridSpec(num_scalar_prefetch=1, grid=(row_blocks,),
                                               in_specs=[blk, blk, blk, pblk, rblk] + more_specs, out_specs=[blk] * 4),
        out_shape=[jax.ShapeDtypeStruct(w.shape, F32)] * 4,
        compiler_params=_params(("arbitrary",)),
    )(chip, w, m, v, part, recv, *more)


def kernel(x, norm_mix_g, w_in, conv_w, q_norm_g, k_norm_g, rel_bias_table, sinks, out_norm_conv_g, out_norm_attn_g, w_out, norm_ffn_g, w_up, ffn_conv_w, ffn_conv_b, w_down, loss_target, m_norm_mix_g, m_w_in, m_conv_w, m_q_norm_g, m_k_norm_g, m_rel_bias_table, m_sinks, m_out_norm_conv_g, m_out_norm_attn_g, m_w_out, m_norm_ffn_g, m_w_up, m_ffn_conv_w, m_ffn_conv_b, m_w_down, v_norm_mix_g, v_w_in, v_conv_w, v_q_norm_g, v_k_norm_g, v_rel_bias_table, v_sinks, v_out_norm_conv_g, v_out_norm_attn_g, v_w_out, v_norm_ffn_g, v_w_up, v_ffn_conv_w, v_ffn_conv_b, v_w_down):
    p = dict(norm_mix_g=norm_mix_g, w_in=w_in, conv_w=conv_w, q_norm_g=q_norm_g, k_norm_g=k_norm_g,
             rel_bias_table=rel_bias_table, sinks=sinks, out_norm_conv_g=out_norm_conv_g, out_norm_attn_g=out_norm_attn_g,
             w_out=w_out, norm_ffn_g=norm_ffn_g, w_up=w_up, ffn_conv_w=ffn_conv_w, ffn_conv_b=ffn_conv_b, w_down=w_down)
    m = dict(norm_mix_g=m_norm_mix_g, w_in=m_w_in, conv_w=m_conv_w, q_norm_g=m_q_norm_g, k_norm_g=m_k_norm_g,
             rel_bias_table=m_rel_bias_table, sinks=m_sinks, out_norm_conv_g=m_out_norm_conv_g,
             out_norm_attn_g=m_out_norm_attn_g, w_out=m_w_out, norm_ffn_g=m_norm_ffn_g, w_up=m_w_up,
             ffn_conv_w=m_ffn_conv_w, ffn_conv_b=m_ffn_conv_b, w_down=m_w_down)
    v = dict(norm_mix_g=v_norm_mix_g, w_in=v_w_in, conv_w=v_conv_w, q_norm_g=v_q_norm_g, k_norm_g=v_k_norm_g,
             rel_bias_table=v_rel_bias_table, sinks=v_sinks, out_norm_conv_g=v_out_norm_conv_g,
             out_norm_attn_g=v_out_norm_attn_g, w_out=v_w_out, norm_ffn_g=v_norm_ffn_g, w_up=v_w_up,
             ffn_conv_w=v_ffn_conv_w, ffn_conv_b=v_ffn_conv_b, w_down=v_w_down)

    xs, tgt = x[0], loss_target[0]
    g1, g2, gq, gk, gconv, gattn = norm_mix_g, norm_ffn_g, q_norm_g, k_norm_g, out_norm_conv_g, out_norm_attn_g
    ix, iy, ic = _coords()
    core = ic.astype(jnp.int32).reshape(1)
    chip = (2 * ix + iy).astype(jnp.int32).reshape(1)
    me = _lin(ix, iy, ic).astype(jnp.int32).reshape(1)
    bkt = jnp.asarray(_bucket_map())
    tr = lambda a: a[0].T
    taps = lambda a: jnp.transpose(a, (1, 0, 2))
    tbl_t = rel_bias_table.T

    wi_l, cw_l, wo_l, wu_l, wd_l, fcw_l = _place_shards(
        me, [tr(w_in), taps(conv_w), w_out[0], tr(w_up), w_down[0], taps(ffn_conv_w)], [BF16, F32, BF16, BF16, BF16, F32])
    finish_a, token_a = _all_gather_split([wi_l, cw_l], "mixer", None)
    finish_b, token_b = _all_gather_split([wo_l, wu_l, wd_l, fcw_l], "ffn", token_a)
    wi_g, cw_g = finish_a(token_b)
    w_in_t = wi_g.reshape(IN_WIDTH, D_MODEL)
    conv_w_f = jnp.transpose(cw_g[:, :, 0, :], (1, 0, 2)).reshape(3, CONV_WIDTH)

    proj, ycn, qn, kn, vv = _mix_in_fwd(xs, g1, w_in_t, conv_w_f, gq, gk, gconv)
    y_attn, yan = _attn_fwd(qn, kn, vv, tbl_t, sinks, bkt, gattn)
    wo_g, wu_g, wd_g, fcw_g = finish_b(yan)
    w_out_f = wo_g.reshape(D_MODEL, D_MODEL)
    w_down_f = wd_g.reshape(N_FFN_BLK, FFN_BLK, D_MODEL)
    w_up_f = wu_g.reshape(2, N_FFN_BLK, FFN_BLK, D_MODEL)
    fcw_f = fcw_g.reshape(2, N_FFN_BLK, 3, 1, FFN_BLK)
    fcb = ffn_conv_b.reshape(2, N_FFN_BLK, 1, FFN_BLK)
    h1, u2, up, dh2, dh2b, loss_acc = _ffn_fwd(xs, ycn, yan, w_out_f, g2, w_up_f, fcw_f, fcb, w_down_f, tgt)

    act, dup, dh1, dh1b, dfb, dfcw, dg2 = _ffn_bwd(dh2, dh2b, h1, g2, up, w_up_f, fcw_f, fcb, w_down_f)
    dw_down = _grad_tn_blocked(act, dh2b, "grad_w_down", a_is_blocked=True).reshape(N_DEV, D_FF // N_DEV, D_MODEL)
    dw_up = _grad_tn_blocked(dup.reshape(N_DEV, SEQ, FFN_BLK), u2, "grad_w_up", a_is_blocked=True)
    dw_out = _grad_tn([ycn, yan], dh1b, CONV_WIDTH, "grad_w_out").reshape(N_DEV, D_MODEL // N_DEV, D_MODEL)
    out_bwd = {}

    def behind_ffn(token):
        out_bwd["r"] = _out_bwd(dh1b, w_out_f, y_attn, gattn, after=token)
        return out_bwd["r"][0]

    finish_ffn, token_ffn = _reduce_scatter_split(
        [dw_down, dw_up, dw_out, dfcw.reshape(N_DEV, 3, 1, FFN_BLK)], "ffn", core, behind_ffn)
    dycn, dy_attn, dgattn = out_bwd["r"]
    dqn, dkn, dv, dtbl_t, dsinks = _attn_bwd(qn, kn, vv, dy_attn, tbl_t, sinks, bkt, after=token_ffn)
    dx, dproj, u1, dcw, dgconv, dgq, dgk, dg1 = _mix_in_bwd(xs, dh1, proj, dycn, dqn, dkn, dv, w_in_t, conv_w_f,
                                                             g1, gq, gk, gconv)
    packed = _pack_small_grads(dg1, dg2, dgconv, dgattn, dfb, dgq, dgk, dsinks, dtbl_t, loss_acc)
    plan_s, slots_s = _broadcast_plan()
    s_sem, r_sem, src_s, land_s, token_s = _split_start(
        "gather_small_start", [packed], [jnp.broadcast_to(packed[None], (N_DEV,) + packed.shape)], plan_s, None)
    dw_in_t = _grad_w_in(dproj, u1, after=token_s).reshape(N_DEV, IN_WIDTH // N_DEV, D_MODEL)
    dcw_b = jnp.transpose(dcw.reshape(3, N_DEV, 1, CONV_WIDTH // N_DEV), (1, 0, 2, 3))
    adam = {}
    ffn_got = {}

    def behind_mixer(token):
        ffn_got["r"] = finish_ffn(token)
        (p_wd, _, _, _), (r_wd, _, _, _) = ffn_got["r"]
        adam["w_down"] = _adamw(w_down[0], m_w_down[0], v_w_down[0], p_wd, r_wd, chip, "adamw_w_down", row_blocks=2)
        return adam["w_down"][0]

    finish_mixer, token_mixer = _reduce_scatter_split([dw_in_t, dcw_b], "mixer", core, behind_mixer)
    (_, p_wu, p_wo, p_fcw), (_, r_wu, r_wo, r_fcw) = ffn_got["r"]
    adam_up = _adamw(tr(w_up), tr(m_w_up), tr(v_w_up), p_wu, r_wu, chip, "adamw_w_up", row_blocks=4, after=token_mixer)
    adam["w_out"] = _adamw(w_out[0], m_w_out[0], v_w_out[0], p_wo, r_wo, chip, "adamw_w_out", after=adam_up[0])
    adam_fcw = _adamw(taps(ffn_conv_w), taps(m_ffn_conv_w), taps(v_ffn_conv_w), p_fcw, r_fcw, chip, "adamw_ffn_conv_w",
                      after=adam["w_out"][0])
    _, (r_small,) = _split_wait("gather_small_wait", s_sem, r_sem, src_s, land_s, plan_s, slots_s, adam_fcw[0])
    small_in = {k: (p[k], m[k], v[k]) for k in _SMALL_NAMES}
    small_in["rel_bias_table"] = (tbl_t, m_rel_bias_table.T, v_rel_bias_table.T)
    small_out, loss_row = _adamw_small(r_small, small_in, None)
    (p_wi, p_cw), (r_wi, r_cw) = finish_mixer(loss_row)
    adam_in = _adamw(tr(w_in), tr(m_w_in), tr(v_w_in), p_wi, r_wi, chip, "adamw_w_in")
    adam_cw = _adamw(taps(conv_w), taps(m_conv_w), taps(v_conv_w), p_cw, r_cw, chip, "adamw_conv_w")

    res = {k: tuple(a[None] for a in t) for k, t in adam.items()}
    res["w_up"] = tuple(a.T[None] for a in adam_up)
    res["w_in"] = tuple(a.T[None] for a in adam_in)
    res["ffn_conv_w"] = tuple(taps(a) for a in adam_fcw)
    res["conv_w"] = tuple(taps(a) for a in adam_cw)
    res.update(small_out)
    res["rel_bias_table"] = tuple(a.T for a in small_out["rel_bias_table"])
    loss = loss_row[0, 0]
    order = ("norm_mix_g", "w_in", "conv_w", "q_norm_g", "k_norm_g", "rel_bias_table", "sinks", "out_norm_conv_g",
             "out_norm_attn_g", "w_out", "norm_ffn_g", "w_up", "ffn_conv_w", "ffn_conv_b", "w_down")
    return (loss, dx[None], *[res[k][0] for k in order], *[res[k][1] for k in order],
            *[res[k][2] for k in order], *[res[k][3] for k in order])
```

```python
import functools
import math

import numpy as np
import jax
import jax.numpy as jnp
from jax import lax
from jax.experimental import pallas as pl
from jax.experimental.pallas import tpu as pltpu

F32 = jnp.float32
BF16 = jnp.bfloat16

SEQ = 2048
D_MODEL = 1024
CONV_WIDTH = 512
ATTN_WIDTH = 512
KV_WIDTH = 128
HEAD_DIM = 64
N_HEADS = 8
GQA_GROUP = 4
IN_WIDTH = 2304
D_FF = 2816
BLK = 128
NUM_BUCKETS = 32
EPS = 1e-6
NEG_INF = -1e30
ADAM_LR = 0.001
ADAM_B1 = 0.9
ADAM_B2 = 0.999
ADAM_EPS = 1e-08
ADAM_WD = 0.01
ADAM_STEP = 10

N_DEV = 8
FFN_BLK = 2 * D_FF // N_DEV
N_FFN_BLK = D_FF // FFN_BLK
SUBLANES = 8
VMEM_LIMIT = 56 * 1024 * 1024

_MESH = pl.DeviceIdType.MESH
_ANY = pl.BlockSpec(memory_space=pl.ANY)


def _params(sem):
    return pltpu.CompilerParams(dimension_semantics=sem, vmem_limit_bytes=VMEM_LIMIT)


def _ordered_behind(body, pos, after):
    if after is None:
        return body, [], []
    return (lambda *refs: body(*refs[:pos], *refs[pos + 1:])), [_ANY], [after]


def _dot(a, b):
    return jnp.dot(a, b, preferred_element_type=F32)


def _dot_nt(a, b):
    return lax.dot_general(a, b, (((1,), (1,)), ((), ())), preferred_element_type=F32)


def _dot_tn(a, b):
    return lax.dot_general(a, b, (((0,), (0,)), ((), ())), preferred_element_type=F32)


def _shift_down(x, s, halo):
    r = pltpu.roll(x, s, axis=0)
    hr = pltpu.roll(halo, s, axis=0)
    row = lax.broadcasted_iota(jnp.int32, halo.shape, 0)
    top = jnp.where(row < s, hr, r[:SUBLANES])
    return jnp.concatenate([top, r[SUBLANES:]], axis=0)


def _shift_up(x, s, halo):
    n = x.shape[0]
    r = pltpu.roll(x, n - s, axis=0)
    hr = pltpu.roll(halo, SUBLANES - s, axis=0)
    row = lax.broadcasted_iota(jnp.int32, halo.shape, 0)
    bot = jnp.where(row >= SUBLANES - s, hr, r[n - SUBLANES:])
    return jnp.concatenate([r[:n - SUBLANES], bot], axis=0)


def _taps(w):
    return (w[0], w[1], w[2]) if len(w.shape) == 3 else (w[0:1], w[1:2], w[2:3])


def _conv3(x, w, halo):
    x2 = _shift_down(x, 2, halo)
    x1 = _shift_down(x, 1, halo)
    return x2 * w[0] + x1 * w[1] + x * w[2], x2, x1


def _conv3_bwd_input(dy, w, halo_next):
    return dy * w[2] + _shift_up(dy, 1, halo_next) * w[1] + _shift_up(dy, 2, halo_next) * w[0]


def _rstd(x):
    return lax.rsqrt(jnp.mean(x * x, axis=-1, keepdims=True) + EPS)


def _rms_bwd(x, g, dy):
    r = _rstd(x)
    n = x * r
    dn = dy * g
    dx = r * (dn - n * jnp.mean(dn * n, axis=-1, keepdims=True))
    return dx, jnp.sum(dy * n, axis=0, keepdims=True)


def _head_norm(x, g, heads):
    parts = []
    for h in range(heads):
        xh = x[:, HEAD_DIM * h:HEAD_DIM * (h + 1)]
        parts.append(xh * _rstd(xh) * g)
    return jnp.concatenate(parts, axis=-1)


def _head_norm_bwd(x, g, dy, heads):
    dxs, dg = [], jnp.zeros((1, HEAD_DIM), F32)
    for h in range(heads):
        sl = slice(HEAD_DIM * h, HEAD_DIM * (h + 1))
        dxh, dgh = _rms_bwd(x[:, sl], g, dy[:, sl])
        dxs.append(dxh)
        dg = dg + dgh
    return jnp.concatenate(dxs, axis=-1), dg


def _bucket_map():
    q = np.arange(BLK)[:, None]
    j = np.arange(2 * BLK)[None, :]
    n = np.maximum(q + BLK - j, 0)
    nf = np.maximum(n, 1).astype(np.float32)
    max_exact = NUM_BUCKETS // 2
    large = max_exact + (np.log(nf / max_exact) / math.log(BLK / max_exact) * (NUM_BUCKETS - max_exact)).astype(np.int32)
    large = np.minimum(large, NUM_BUCKETS - 1)
    return np.where(n < max_exact, n, large).astype(np.int32)


def _coords():
    return lax.axis_index("x"), lax.axis_index("y"), lax.axis_index("c")


def _lin(px, py, pc):
    return 4 * px + 2 * py + pc


_HBM = pl.BlockSpec(memory_space=pltpu.HBM)
_SEM = pl.BlockSpec(memory_space=pltpu.SEMAPHORE)
_EFFECT = pltpu.SideEffectType.DATAFLOW_SIDE_EFFECTING


def _in_hbm(a):
    return pltpu.with_memory_space_constraint(a, pltpu.HBM)


def _split_start(name, srcs, lands, plan, after):
    ns, nl = len(srcs), len(lands)
    n_copies = len(plan(0, 0, 0))
    n_after = 0 if after is None else 1

    def body(*refs):
        src_refs, land_refs = refs[:ns + nl], refs[ns:ns + nl]
        send_sems, recv_sems = refs[ns + nl + n_after], refs[ns + nl + n_after + 1]
        token = refs[-1]
        for k, (a, s_slot, l, d_slot, dev) in enumerate(plan(*_coords())):
            src = src_refs[a] if s_slot is None else src_refs[a].at[s_slot]
            pltpu.make_async_remote_copy(src_ref=src, dst_ref=land_refs[l].at[d_slot], send_sem=send_sems.at[k],
                                         recv_sem=recv_sems.at[k], device_id=dev, device_id_type=_MESH).start()
        token[...] = jnp.zeros_like(token)

    arrs = list(srcs) + list(lands)
    out = pl.pallas_call(
        body, name=name,
        out_shape=(pltpu.SemaphoreType.DMA((n_copies,)), pltpu.SemaphoreType.DMA((n_copies,)),
                   *[pltpu.HBM(a.shape, a.dtype) for a in arrs], jax.ShapeDtypeStruct((SUBLANES, 128), F32)),
        in_specs=[_HBM] * (ns + nl) + [_ANY] * n_after,
        out_specs=(_SEM, _SEM, *[_HBM] * (ns + nl), pl.BlockSpec(memory_space=pltpu.VMEM)),
        input_output_aliases={i: 2 + i for i in range(ns + nl)},
        compiler_params=pltpu.CompilerParams(has_side_effects=_EFFECT),
    )(*[_in_hbm(a) for a in arrs], *([] if after is None else [after]))
    return out[0], out[1], list(out[2:2 + ns]), list(out[2 + ns:2 + ns + nl]), out[-1]


def _split_wait(name, send_sems, recv_sems, srcs, lands, plan, recv_slots, after):
    ns, nl = len(srcs), len(lands)

    def body(*refs):
        src_refs, land_refs = refs[:ns + nl], refs[ns:ns + nl]
        send_sems, recv_sems = refs[ns + nl], refs[ns + nl + 1]
        coords = _coords()
        slots = recv_slots(*coords)
        for k, (a, s_slot, l, _, dev) in enumerate(plan(*coords)):
            src = src_refs[a] if s_slot is None else src_refs[a].at[s_slot]
            cp = pltpu.make_async_remote_copy(src_ref=src, dst_ref=land_refs[l].at[slots[k]], send_sem=send_sems.at[k],
                                              recv_sem=recv_sems.at[k], device_id=dev, device_id_type=_MESH)
            cp.wait_send()
            cp.wait_recv()

    arrs = list(srcs) + list(lands)
    out = pl.pallas_call(
        body, name=name,
        out_shape=tuple(pltpu.HBM(a.shape, a.dtype) for a in arrs),
        in_specs=[_HBM] * (ns + nl) + [_SEM, _SEM, _ANY],
        out_specs=tuple([_HBM] * (ns + nl)),
        input_output_aliases={i: i for i in range(ns + nl)},
        compiler_params=pltpu.CompilerParams(has_side_effects=_EFFECT),
    )(*arrs, send_sems, recv_sems, after)
    return list(out[:ns]), list(out[ns:])


def _chips(x, y):
    return [(1 - x, y), (x, 1 - y), (1 - x, 1 - y)]


def _gather_plan_ici(n):
    def plan(x, y, c):
        me = _lin(x, y, c)
        out = []
        for a in range(n):
            out.append((a, me, a, me, (x, y, 1 - c)))
            out += [(a, me, a, me, (cx, cy, c)) for cx, cy in _chips(x, y)]
        return out

    def recv_slots(x, y, c):
        out = []
        for _ in range(n):
            out.append(_lin(x, y, 1 - c))
            out += [_lin(cx, cy, c) for cx, cy in _chips(x, y)]
        return out

    return plan, recv_slots


def _gather_plan_d2d(n):
    def plan(x, y, c):
        return [(a, _lin(cx, cy, c), a, _lin(cx, cy, c), (x, y, 1 - c)) for a in range(n) for cx, cy in _chips(x, y)]

    def recv_slots(x, y, c):
        return [_lin(cx, cy, 1 - c) for _ in range(n) for cx, cy in _chips(x, y)]

    return plan, recv_slots


def _all_gather_split(lands, tag, after):
    n = len(lands)
    plan1, slots1 = _gather_plan_ici(n)
    s1, r1, _, lands, token = _split_start(f"gather_{tag}_ici_start", [], lands, plan1, after)

    def finish(after):
        _, got = _split_wait(f"gather_{tag}_ici_wait", s1, r1, [], lands, plan1, slots1, after)
        plan2, slots2 = _gather_plan_d2d(n)
        s2, r2, _, got, token2 = _split_start(f"gather_{tag}_d2d_start", [], got, plan2, None)
        return _split_wait(f"gather_{tag}_d2d_wait", s2, r2, [], got, plan2, slots2, token2)[1]

    return finish, token


_CHIP_LIST = ((0, 0), (0, 1), (1, 0), (1, 1))


def _reduce_plan_d2d(n):
    def plan(x, y, c):
        return [(a, _lin(qx, qy, 1 - c), a, q, (x, y, 1 - c)) for a in range(n) for q, (qx, qy) in enumerate(_CHIP_LIST)]

    def recv_slots(x, y, c):
        return [q for _ in range(n) for q in range(4)]

    return plan, recv_slots


def _reduce_plan_ici(n):
    def plan(x, y, c):
        return [(a, 2 * cx + cy, a, j, (cx, cy, c)) for a in range(n) for j, (cx, cy) in enumerate(_chips(x, y))]

    def recv_slots(x, y, c):
        return [j for _ in range(n) for j in range(3)]

    return plan, recv_slots


def _broadcast_plan():
    def peers(x, y, c):
        return [(1 - x if r & 4 else x, 1 - y if r & 2 else y, 1 - c if r & 1 else c) for r in range(1, N_DEV)]

    def plan(x, y, c):
        return [(0, None, 0, _lin(x, y, c), peer) for peer in peers(x, y, c)]

    def recv_slots(x, y, c):
        return [_lin(*peer) for peer in peers(x, y, c)]

    return plan, recv_slots


def _chip_partial(grads, recvd, core, name):
    n = len(grads)

    def body(c_ref, *refs):
        for a in range(n):
            g_ref, r_ref, o_ref = refs[a], refs[n + a], refs[2 * n + a]
            o_ref[...] = (g_ref[...].astype(F32) + r_ref[...].astype(F32)).astype(o_ref.dtype)

    def blk(a, own):
        zeros = (0,) * (a.ndim - 1)
        return pl.BlockSpec((None,) + a.shape[1:],
                            (lambda q, c_ref: (2 * q + c_ref[0],) + zeros) if own else (lambda q, c_ref: (q,) + zeros))

    return pl.pallas_call(
        body, name=name,
        grid_spec=pltpu.PrefetchScalarGridSpec(
            num_scalar_prefetch=1, grid=(4,),
            in_specs=[blk(a, True) for a in grads] + [blk(a, False) for a in recvd],
            out_specs=[blk(a, False) for a in recvd]),
        out_shape=[jax.ShapeDtypeStruct(a.shape, a.dtype) for a in recvd],
        compiler_params=_params(("arbitrary",)),
    )(core, *grads, *recvd)


def _reduce_scatter_split(grads, tag, core, behind):
    n = len(grads)
    plan1, slots1 = _reduce_plan_d2d(n)
    lands1 = [lax.empty((4,) + a.shape[1:], a.dtype) for a in grads]
    s1, r1, srcs1, lands1, token1 = _split_start(f"reduce_{tag}_d2d_start", grads, lands1, plan1, None)
    own, got = _split_wait(f"reduce_{tag}_d2d_wait", s1, r1, srcs1, lands1, plan1, slots1, behind(token1))
    parts = _chip_partial(own, got, core, f"reduce_{tag}_partial")
    plan2, slots2 = _reduce_plan_ici(n)
    lands2 = [lax.empty((3,) + a.shape[1:], a.dtype) for a in grads]
    s2, r2, srcs2, lands2, token2 = _split_start(f"reduce_{tag}_ici_start", parts, lands2, plan2, None)

    def finish(after):
        return _split_wait(f"reduce_{tag}_ici_wait", s2, r2, srcs2, lands2, plan2, slots2, after)

    return finish, token2


def _place_shards(me, shards, dtypes):
    n = len(shards)

    def body(me_ref, *refs):
        for a in range(n):
            refs[n + a][...] = refs[a][...].astype(dtypes[a])

    full = lambda s: pl.BlockSpec(s.shape, lambda i, me_ref: (0,) * s.ndim)
    slot = lambda s: pl.BlockSpec((None,) + s.shape, lambda i, me_ref: (me_ref[0],) + (0,) * s.ndim)
    return pl.pallas_call(
        body, name="place_shards",
        grid_spec=pltpu.PrefetchScalarGridSpec(num_scalar_prefetch=1, grid=(1,), in_specs=[full(s) for s in shards],
                                               out_specs=[slot(s) for s in shards]),
        out_shape=[jax.ShapeDtypeStruct((N_DEV,) + s.shape, d) for s, d in zip(shards, dtypes)],
        compiler_params=_params(("arbitrary",)),
    )(me, *shards)


def _mix_in_fwd(x, g1, w_in_t, conv_w, gq, gk, gconv):
    tm = 512
    n_t = SEQ // tm

    def body(x_ref, g1_ref, w_ref, cw_ref, gq_ref, gk_ref, gc_ref,
             proj_ref, ycn_ref, qn_ref, kn_ref, v_ref, halo_ref):
        @pl.when(pl.program_id(0) == 0)
        def _():
            halo_ref[...] = jnp.zeros_like(halo_ref)

        xv = x_ref[...]
        u = (xv * _rstd(xv) * g1_ref[...]).astype(BF16)
        proj = _dot_nt(u, w_ref[...])
        proj_ref[...] = proj
        gate_b = proj[:, 0:CONV_WIDTH]
        a = proj[:, CONV_WIDTH:2 * CONV_WIDTH] * proj[:, 2 * CONV_WIDTH:3 * CONV_WIDTH]
        cv, _, _ = _conv3(a, _taps(cw_ref[...]), halo_ref[...])
        halo_ref[...] = a[tm - SUBLANES:]
        yc = gate_b * cv
        ycn_ref[...] = (yc * _rstd(yc) * gc_ref[...]).astype(BF16)
        q0 = 3 * CONV_WIDTH
        qn_ref[...] = _head_norm(proj[:, q0:q0 + ATTN_WIDTH], gq_ref[...], N_HEADS).astype(BF16)
        k0 = q0 + ATTN_WIDTH
        kn_ref[...] = _head_norm(proj[:, k0:k0 + KV_WIDTH], gk_ref[...], 2).astype(BF16)
        v_ref[...] = proj[:, k0 + KV_WIDTH:k0 + 2 * KV_WIDTH].astype(BF16)

    const = lambda shape: pl.BlockSpec(shape, lambda i: (0,) * len(shape))
    rows = lambda w: pl.BlockSpec((tm, w), lambda i: (i, 0))
    return pl.pallas_call(
        body, name="mix_in_fwd", grid=(n_t,),
        in_specs=[rows(D_MODEL), const((1, D_MODEL)), const((IN_WIDTH, D_MODEL)), const((3, CONV_WIDTH)),
                  const((1, HEAD_DIM)), const((1, HEAD_DIM)), const((1, CONV_WIDTH))],
        out_specs=[rows(IN_WIDTH), rows(CONV_WIDTH), rows(ATTN_WIDTH), rows(KV_WIDTH), rows(KV_WIDTH)],
        out_shape=[jax.ShapeDtypeStruct((SEQ, IN_WIDTH), F32), jax.ShapeDtypeStruct((SEQ, CONV_WIDTH), BF16),
                   jax.ShapeDtypeStruct((SEQ, ATTN_WIDTH), BF16), jax.ShapeDtypeStruct((SEQ, KV_WIDTH), BF16),
                   jax.ShapeDtypeStruct((SEQ, KV_WIDTH), BF16)],
        scratch_shapes=[pltpu.VMEM((SUBLANES, CONV_WIDTH), F32)],
        compiler_params=_params(("arbitrary",)),
    )(x, g1, w_in_t, conv_w, gq, gk, gconv)


def _band_bias(tbl_ref, bkt, bias_ref):
    for h in range(N_HEADS):
        acc = jnp.zeros(bkt.shape, F32)
        for b in range(NUM_BUCKETS):
            acc = jnp.where(bkt == b, tbl_ref[h, b], acc)
        bias_ref[h] = acc


def _band_valid(i):
    qi = lax.broadcasted_iota(jnp.int32, (BLK, 2 * BLK), 0)
    ji = lax.broadcasted_iota(jnp.int32, (BLK, 2 * BLK), 1)
    dist = qi + BLK - ji
    return (dist >= 0) & (dist < BLK) & ((ji >= BLK) | (i > 0))


def _band_rows(ref, i):
    prev = pl.multiple_of(jnp.maximum(i - 1, 0) * BLK, BLK)
    cur = pl.multiple_of(i * BLK, BLK)
    return jnp.concatenate([ref[pl.ds(prev, BLK), :], ref[pl.ds(cur, BLK), :]], axis=0), prev, cur


def _head_probs(qh, kh, bias, valid, sink):
    logits = _dot_nt(qh, kh) * (HEAD_DIM ** -0.5) + bias
    logits = jnp.where(valid, logits, NEG_INF)
    m = jnp.maximum(jnp.max(logits, axis=-1, keepdims=True), sink)
    p = jnp.exp(logits - m)
    es = jnp.exp(sink - m)
    den = jnp.sum(p, axis=-1, keepdims=True) + es
    return p / den, es / den


def _attn_fwd(qn, kn, v, tbl, sinks, bkt, gattn):
    n_b = SEQ // BLK

    def body(q_ref, k_ref, v_ref, tbl_ref, sink_ref, bkt_ref, ga_ref, y_ref, yn_ref, bias_ref):
        i = pl.program_id(0)

        @pl.when(i == 0)
        def _():
            _band_bias(tbl_ref, bkt_ref[...], bias_ref)

        kb, _, _ = _band_rows(k_ref, i)
        vb, _, _ = _band_rows(v_ref, i)
        valid = _band_valid(i)
        q = q_ref[...]
        outs = []
        for h in range(N_HEADS):
            hk = h // GQA_GROUP
            kv = slice(HEAD_DIM * hk, HEAD_DIM * (hk + 1))
            probs, _ = _head_probs(q[:, HEAD_DIM * h:HEAD_DIM * (h + 1)], kb[:, kv], bias_ref[h], valid, sink_ref[0, h])
            outs.append(_dot(probs.astype(BF16), vb[:, kv]))
        y = jnp.concatenate(outs, axis=-1)
        y_ref[...] = y
        yn_ref[...] = (y * _rstd(y) * ga_ref[...]).astype(BF16)

    const = lambda shape: pl.BlockSpec(shape, lambda i: (0,) * len(shape))
    rows = lambda w: pl.BlockSpec((BLK, w), lambda i: (i, 0))
    smem = pl.BlockSpec(memory_space=pltpu.SMEM)
    return pl.pallas_call(
        body, name="attn_fwd", grid=(n_b,),
        in_specs=[rows(ATTN_WIDTH), const((SEQ, KV_WIDTH)), const((SEQ, KV_WIDTH)), smem, smem,
                  const((BLK, 2 * BLK)), const((1, ATTN_WIDTH))],
        out_specs=[rows(ATTN_WIDTH), rows(ATTN_WIDTH)],
        out_shape=[jax.ShapeDtypeStruct((SEQ, ATTN_WIDTH), F32), jax.ShapeDtypeStruct((SEQ, ATTN_WIDTH), BF16)],
        scratch_shapes=[pltpu.VMEM((N_HEADS, BLK, 2 * BLK), F32)],
        compiler_params=_params(("arbitrary",)),
    )(qn, kn, v, tbl, sinks, bkt, gattn)


def _ffn_fwd(x, ycn, yan, w_out, g2, w_up, fcw, fcb, w_down, tgt):
    tm = 512
    n_t = SEQ // tm

    def body(x_ref, ycn_ref, yan_ref, wo_ref, g2_ref, wu_ref, cw_ref, b_ref, wd_ref, tgt_ref,
             h1_ref, u2_ref, up_ref, dh2_ref, dh2b_ref, loss_ref, acc_ref, halo_ref):
        i, j = pl.program_id(0), pl.program_id(1)

        @pl.when((i == 0) & (j == 0))
        def _():
            loss_ref[...] = jnp.zeros_like(loss_ref)

        @pl.when(j == 0)
        def _():
            h1 = x_ref[...] + _dot(ycn_ref[...], wo_ref[0:CONV_WIDTH, :]) + _dot(yan_ref[...], wo_ref[CONV_WIDTH:, :])
            h1_ref[...] = h1
            u2_ref[...] = (h1 * _rstd(h1) * g2_ref[...]).astype(BF16)
            acc_ref[...] = jnp.zeros_like(acc_ref)

        u2 = u2_ref[...]
        pre = []
        for s in range(2):
            up = _dot_nt(u2, wu_ref[s])
            up_ref[s] = up
            halo = jnp.where(i == 0, 0.0, halo_ref[s, j])
            pre.append(_conv3(up, _taps(cw_ref.at[s]), halo)[0] + b_ref[s])
            halo_ref[s, j] = up[tm - SUBLANES:]
        g, val = pre
        act = (g * jax.nn.sigmoid(g) * val).astype(BF16)
        acc_ref[...] += _dot(act, wd_ref[...])

        @pl.when(j == N_FFN_BLK - 1)
        def _():
            err = h1_ref[...] + acc_ref[...] - tgt_ref[...]
            loss_ref[...] += 0.5 * jnp.sum(err * err) / D_MODEL
            dh2 = err / D_MODEL
            dh2_ref[...] = dh2
            dh2b_ref[...] = dh2.astype(BF16)

    rows = lambda w: pl.BlockSpec((tm, w), lambda i, j: (i, 0))
    const = lambda shape: pl.BlockSpec(shape, lambda i, j: (0,) * len(shape))
    pair = lambda *s: pl.BlockSpec((2, None) + s, lambda i, j: (0, j) + (0,) * len(s))
    upb = pl.BlockSpec((2, None, tm, FFN_BLK), lambda i, j: (0, j, i, 0))
    return pl.pallas_call(
        body, name="ffn_fwd", grid=(n_t, N_FFN_BLK),
        in_specs=[rows(D_MODEL), rows(CONV_WIDTH), rows(ATTN_WIDTH), const((D_MODEL, D_MODEL)), const((1, D_MODEL)),
                  pair(FFN_BLK, D_MODEL), pair(3, 1, FFN_BLK), pair(1, FFN_BLK),
                  pl.BlockSpec((None, FFN_BLK, D_MODEL), lambda i, j: (j, 0, 0)), rows(D_MODEL)],
        out_specs=[rows(D_MODEL), rows(D_MODEL), upb, rows(D_MODEL), rows(D_MODEL), const((SUBLANES, 128))],
        out_shape=[jax.ShapeDtypeStruct((SEQ, D_MODEL), F32), jax.ShapeDtypeStruct((SEQ, D_MODEL), BF16),
                   jax.ShapeDtypeStruct((2, N_FFN_BLK, SEQ, FFN_BLK), F32),
                   jax.ShapeDtypeStruct((SEQ, D_MODEL), F32), jax.ShapeDtypeStruct((SEQ, D_MODEL), BF16),
                   jax.ShapeDtypeStruct((SUBLANES, 128), F32)],
        scratch_shapes=[pltpu.VMEM((tm, D_MODEL), F32), pltpu.VMEM((2, N_FFN_BLK, SUBLANES, FFN_BLK), F32)],
        compiler_params=_params(("arbitrary", "arbitrary")),
    )(x, ycn, yan, w_out, g2, w_up, fcw, fcb, w_down, tgt)


def _ffn_bwd(dh2, dh2b, h1, g2, up, w_up, fcw, fcb, w_down):
    tm = 512
    n_t = SEQ // tm
    halo_blocks = tm // SUBLANES

    def body(dh2_ref, dh2b_ref, h1_ref, g2_ref, up_ref, uph_ref, wu_ref, cw_ref, b_ref, wd_ref,
             act_ref, dup_ref, dh1_ref, dh1b_ref, dfb_ref, dfcw_ref, dg2_ref, acc_ref, next_ref):
        i, j = pl.program_id(0), pl.program_id(1)
        first_tile = i == n_t - 1

        @pl.when((i == 0) & (j == 0))
        def _():
            dfb_ref[...] = jnp.zeros_like(dfb_ref)
            dfcw_ref[...] = jnp.zeros_like(dfcw_ref)
            dg2_ref[...] = jnp.zeros_like(dg2_ref)

        @pl.when(j == 0)
        def _():
            acc_ref[...] = jnp.zeros_like(acc_ref)

        ups, pre = [], []
        for s in range(2):
            up = up_ref[s]
            halo = jnp.where(first_tile, 0.0, uph_ref[s])
            p, up2, up1 = _conv3(up, _taps(cw_ref.at[s]), halo)
            pre.append(p + b_ref[s])
            ups.append((up, up1, up2))
        g, val = pre
        sg = jax.nn.sigmoid(g)
        silu = g * sg
        act_ref[...] = (silu * val).astype(BF16)
        dact = _dot_nt(dh2b_ref[...], wd_ref[...])
        dpre = (dact * val * (sg * (1.0 + g * (1.0 - sg))), dact * silu)
        for s in range(2):
            d = dpre[s]
            u, u1, u2 = ups[s]
            dfb_ref[s, j] += jnp.sum(d, axis=0, keepdims=True)
            dfcw_ref[s, j, 0] += jnp.sum(d * u2, axis=0, keepdims=True)
            dfcw_ref[s, j, 1] += jnp.sum(d * u1, axis=0, keepdims=True)
            dfcw_ref[s, j, 2] += jnp.sum(d * u, axis=0, keepdims=True)
            nxt = jnp.where(i == 0, 0.0, next_ref[s, j])
            dup = _conv3_bwd_input(d, _taps(cw_ref.at[s]), nxt).astype(BF16)
            next_ref[s, j] = d[:SUBLANES]
            dup_ref[s] = dup
            acc_ref[...] += _dot(dup, wu_ref[s])

        @pl.when(j == N_FFN_BLK - 1)
        def _():
            dn, dgain = _rms_bwd(h1_ref[...], g2_ref[...], acc_ref[...])
            dh1 = dh2_ref[...] + dn
            dh1_ref[...] = dh1
            dh1b_ref[...] = dh1.astype(BF16)
            dg2_ref[...] += dgain

    rev = lambda i: n_t - 1 - i
    rows = lambda w: pl.BlockSpec((tm, w), lambda i, j: (rev(i), 0))
    const = lambda shape: pl.BlockSpec(shape, lambda i, j: (0,) * len(shape))
    pair = lambda *s: pl.BlockSpec((2, None) + s, lambda i, j: (0, j) + (0,) * len(s))
    upb = pl.BlockSpec((2, None, tm, FFN_BLK), lambda i, j: (0, j, rev(i), 0))
    halo = pl.BlockSpec((2, None, SUBLANES, FFN_BLK), lambda i, j: (0, j, jnp.maximum(rev(i) * halo_blocks - 1, 0), 0))
    return pl.pallas_call(
        body, name="ffn_bwd", grid=(n_t, N_FFN_BLK),
        in_specs=[rows(D_MODEL), rows(D_MODEL), rows(D_MODEL), const((1, D_MODEL)), upb, halo,
                  pair(FFN_BLK, D_MODEL), pair(3, 1, FFN_BLK), pair(1, FFN_BLK),
                  pl.BlockSpec((None, FFN_BLK, D_MODEL), lambda i, j: (j, 0, 0))],
        out_specs=[pl.BlockSpec((None, tm, FFN_BLK), lambda i, j: (j, rev(i), 0)), upb, rows(D_MODEL), rows(D_MODEL),
                   const((2, N_FFN_BLK, 1, FFN_BLK)), const((2, N_FFN_BLK, 3, 1, FFN_BLK)), const((1, D_MODEL))],
        out_shape=[jax.ShapeDtypeStruct((N_FFN_BLK, SEQ, FFN_BLK), BF16),
                   jax.ShapeDtypeStruct((2, N_FFN_BLK, SEQ, FFN_BLK), BF16), jax.ShapeDtypeStruct((SEQ, D_MODEL), F32),
                   jax.ShapeDtypeStruct((SEQ, D_MODEL), BF16), jax.ShapeDtypeStruct((2, N_FFN_BLK, 1, FFN_BLK), F32),
                   jax.ShapeDtypeStruct((2, N_FFN_BLK, 3, 1, FFN_BLK), F32), jax.ShapeDtypeStruct((1, D_MODEL), F32)],
        scratch_shapes=[pltpu.VMEM((tm, D_MODEL), F32), pltpu.VMEM((2, N_FFN_BLK, SUBLANES, FFN_BLK), F32)],
        compiler_params=_params(("arbitrary", "arbitrary")),
    )(dh2, dh2b, h1, g2, up, up, w_up, fcw, fcb, w_down)


def _grad_tn(a_list, b, out_rows, name, after=None):
    n = len(a_list)
    ncol = b.shape[1]

    def body(*refs):
        a_refs, b_ref, o_ref = refs[:n], refs[n], refs[n + 1]
        j = pl.program_id(0)
        for k in range(n):
            @pl.when(j == k)
            def _(k=k):
                o_ref[...] = _dot_tn(a_refs[k][...], b_ref[...]).astype(BF16)

    full = lambda shape: pl.BlockSpec(shape, lambda j: (0,) * len(shape))
    body, more_specs, more = _ordered_behind(body, n + 1, after)
    return pl.pallas_call(
        body, name=name, grid=(n,),
        in_specs=[full((SEQ, out_rows))] * n + [full((SEQ, ncol))] + more_specs,
        out_specs=pl.BlockSpec((None, out_rows, ncol), lambda j: (j, 0, 0)),
        out_shape=jax.ShapeDtypeStruct((n, out_rows, ncol), BF16),
        compiler_params=_params(("arbitrary",)),
    )(*a_list, b, *more)


def _grad_tn_blocked(a, b, name, a_is_blocked):
    nb = a.shape[0] if a_is_blocked else b.shape[0]
    a_w, b_w = a.shape[-1], b.shape[-1]

    def body(a_ref, b_ref, o_ref):
        o_ref[...] = _dot_tn(a_ref[...], b_ref[...]).astype(BF16)

    blocked = lambda w: pl.BlockSpec((None, SEQ, w), lambda k: (k, 0, 0))
    full = lambda w: pl.BlockSpec((SEQ, w), lambda k: (0, 0))
    return pl.pallas_call(
        body, name=name, grid=(nb,),
        in_specs=[blocked(a_w) if a_is_blocked else full(a_w), full(b_w) if a_is_blocked else blocked(b_w)],
        out_specs=pl.BlockSpec((None, a_w, b_w), lambda k: (k, 0, 0)),
        out_shape=jax.ShapeDtypeStruct((nb, a_w, b_w), BF16),
        compiler_params=_params(("arbitrary",)),
    )(a, b)


def _out_bwd(dh1b, w_out, y_attn, gattn, after=None):
    tm = 512
    n_t = SEQ // tm

    def body(dh_ref, wo_ref, y_ref, ga_ref, dycn_ref, dy_ref, dga_ref):
        @pl.when(pl.program_id(0) == 0)
        def _():
            dga_ref[...] = jnp.zeros_like(dga_ref)

        dycat = _dot_nt(dh_ref[...], wo_ref[...])
        dycn_ref[...] = dycat[:, :CONV_WIDTH]
        dy, dga = _rms_bwd(y_ref[...], ga_ref[...], dycat[:, CONV_WIDTH:])
        dy_ref[...] = dy
        dga_ref[...] += dga

    rows = lambda w: pl.BlockSpec((tm, w), lambda i: (i, 0))
    const = lambda shape: pl.BlockSpec(shape, lambda i: (0,) * len(shape))
    body, more_specs, more = _ordered_behind(body, 4, after)
    return pl.pallas_call(
        body, name="out_bwd", grid=(n_t,),
        in_specs=[rows(D_MODEL), const((D_MODEL, D_MODEL)), rows(ATTN_WIDTH), const((1, ATTN_WIDTH))] + more_specs,
        out_specs=[rows(CONV_WIDTH), rows(ATTN_WIDTH), const((1, ATTN_WIDTH))],
        out_shape=[jax.ShapeDtypeStruct((SEQ, CONV_WIDTH), F32), jax.ShapeDtypeStruct((SEQ, ATTN_WIDTH), F32),
                   jax.ShapeDtypeStruct((1, ATTN_WIDTH), F32)],
        compiler_params=_params(("arbitrary",)),
    )(dh1b, w_out, y_attn, gattn, *more)


def _attn_bwd(qn, kn, v, dy, tbl, sinks, bkt, after=None):
    n_b = SEQ // BLK

    def body(q_ref, k_ref, v_ref, dy_ref, tbl_ref, sink_ref, bkt_ref,
             dq_ref, dk_ref, dv_ref, dtbl_ref, dsink_ref, bias_ref, dbias_ref, dsacc_ref):
        i = pl.program_id(0)

        @pl.when(i == 0)
        def _():
            _band_bias(tbl_ref, bkt_ref[...], bias_ref)
            dbias_ref[...] = jnp.zeros_like(dbias_ref)
            dsacc_ref[...] = jnp.zeros_like(dsacc_ref)
            dk_ref[...] = jnp.zeros_like(dk_ref)
            dv_ref[...] = jnp.zeros_like(dv_ref)

        kb, prev, cur = _band_rows(k_ref, i)
        vb, _, _ = _band_rows(v_ref, i)
        valid = _band_valid(i)
        q = q_ref[...]
        dy = dy_ref[...]
        lane = lax.broadcasted_iota(jnp.int32, (BLK, 128), 1)
        dqs, dks, dvs = [], [], []
        dsink = jnp.zeros((BLK, 128), F32)
        for h in range(N_HEADS):
            hk = h // GQA_GROUP
            kv = slice(HEAD_DIM * hk, HEAD_DIM * (hk + 1))
            qh = q[:, HEAD_DIM * h:HEAD_DIM * (h + 1)]
            probs, psink = _head_probs(qh, kb[:, kv], bias_ref[h], valid, sink_ref[0, h])
            doh = dy[:, HEAD_DIM * h:HEAD_DIM * (h + 1)].astype(BF16)
            dprobs = _dot_nt(doh, vb[:, kv])
            dvh = _dot_tn(probs.astype(BF16), doh)
            dsum = jnp.sum(probs * dprobs, axis=-1, keepdims=True)
            dlogits = probs * (dprobs - dsum)
            dsink = jnp.where(lane == h, -psink * dsum, dsink)
            dbias_ref[h] += dlogits
            ds = (dlogits * (HEAD_DIM ** -0.5)).astype(BF16)
            dqs.append(_dot(ds, kb[:, kv]))
            dkh = _dot_tn(ds, qh)
            if h % GQA_GROUP == 0:
                dks.append(dkh)
                dvs.append(dvh)
            else:
                dks[hk] = dks[hk] + dkh
                dvs[hk] = dvs[hk] + dvh
        dq_ref[...] = jnp.concatenate(dqs, axis=-1)
        dsacc_ref[...] += dsink
        dkb = jnp.concatenate(dks, axis=-1)
        dvb = jnp.concatenate(dvs, axis=-1)
        dk_ref[pl.ds(prev, BLK), :] += dkb[:BLK]
        dk_ref[pl.ds(cur, BLK), :] += dkb[BLK:]
        dv_ref[pl.ds(prev, BLK), :] += dvb[:BLK]
        dv_ref[pl.ds(cur, BLK), :] += dvb[BLK:]

        @pl.when(i == n_b - 1)
        def _():
            dsink_ref[...] = jnp.sum(dsacc_ref[...], axis=0, keepdims=True)
            bkt = bkt_ref[...]
            row8 = lax.broadcasted_iota(jnp.int32, (N_HEADS, 128), 0)
            lane8 = lax.broadcasted_iota(jnp.int32, (N_HEADS, 128), 1)
            acc = jnp.zeros((N_HEADS, 128), F32)
            for h in range(N_HEADS):
                dbh = dbias_ref[h]
                for b in range(NUM_BUCKETS):
                    acc = jnp.where((row8 == h) & (lane8 == b), jnp.sum(jnp.where(bkt == b, dbh, 0.0)), acc)
            dtbl_ref[...] = acc

    const = lambda shape: pl.BlockSpec(shape, lambda i: (0,) * len(shape))
    rows = lambda w: pl.BlockSpec((BLK, w), lambda i: (i, 0))
    smem = pl.BlockSpec(memory_space=pltpu.SMEM)
    body, more_specs, more = _ordered_behind(body, 7, after)
    return pl.pallas_call(
        body, name="attn_bwd", grid=(n_b,),
        in_specs=[rows(ATTN_WIDTH), const((SEQ, KV_WIDTH)), const((SEQ, KV_WIDTH)), rows(ATTN_WIDTH), smem, smem,
                  const((BLK, 2 * BLK))] + more_specs,
        out_specs=[rows(ATTN_WIDTH), const((SEQ, KV_WIDTH)), const((SEQ, KV_WIDTH)), const((N_HEADS, 128)), const((1, 128))],
        out_shape=[jax.ShapeDtypeStruct((SEQ, ATTN_WIDTH), F32), jax.ShapeDtypeStruct((SEQ, KV_WIDTH), F32),
                   jax.ShapeDtypeStruct((SEQ, KV_WIDTH), F32), jax.ShapeDtypeStruct((N_HEADS, 128), F32),
                   jax.ShapeDtypeStruct((1, 128), F32)],
        scratch_shapes=[pltpu.VMEM((N_HEADS, BLK, 2 * BLK), F32), pltpu.VMEM((N_HEADS, BLK, 2 * BLK), F32),
                        pltpu.VMEM((BLK, 128), F32)],
        compiler_params=_params(("arbitrary",)),
    )(qn, kn, v, dy, tbl, sinks, bkt, *more)


def _mix_in_bwd(x, dh1, proj, dycn, dqn, dkn, dv, w_in_t, conv_w, g1, gq, gk, gconv):
    tm = 512
    n_t = SEQ // tm
    halo_blocks = tm // SUBLANES

    def body(x_ref, dh1_ref, proj_ref, halo_ref, dycn_ref, dqn_ref, dkn_ref, dv_ref, w_ref, cw_ref,
             g1_ref, gq_ref, gk_ref, gc_ref,
             dx_ref, dproj_ref, u1_ref, dcw_ref, dgc_ref, dgq_ref, dgk_ref, dg1_ref, next_ref):
        i = pl.program_id(0)
        first_tile = i == n_t - 1

        @pl.when(i == 0)
        def _():
            for r in (dcw_ref, dgc_ref, dgq_ref, dgk_ref, dg1_ref, next_ref):
                r[...] = jnp.zeros_like(r)

        proj = proj_ref[...]
        hp = halo_ref[...]
        gate_b = proj[:, 0:CONV_WIDTH]
        gate_c = proj[:, CONV_WIDTH:2 * CONV_WIDTH]
        hc = proj[:, 2 * CONV_WIDTH:3 * CONV_WIDTH]
        a = gate_c * hc
        a_halo = jnp.where(first_tile, 0.0, hp[:, CONV_WIDTH:2 * CONV_WIDTH] * hp[:, 2 * CONV_WIDTH:3 * CONV_WIDTH])
        cw = _taps(cw_ref[...])
        cv, a2, a1 = _conv3(a, cw, a_halo)
        dyc, dgc = _rms_bwd(gate_b * cv, gc_ref[...], dycn_ref[...])
        dgc_ref[...] += dgc
        dcv = dyc * gate_b
        dcw_ref[...] += jnp.concatenate(
            [jnp.sum(dcv * a2, axis=0, keepdims=True), jnp.sum(dcv * a1, axis=0, keepdims=True),
             jnp.sum(dcv * a, axis=0, keepdims=True)], axis=0)
        da = _conv3_bwd_input(dcv, cw, next_ref[...])
        next_ref[...] = dcv[:SUBLANES]
        q0 = 3 * CONV_WIDTH
        k0 = q0 + ATTN_WIDTH
        dq, dgq = _head_norm_bwd(proj[:, q0:k0], gq_ref[...], dqn_ref[...], N_HEADS)
        dk, dgk = _head_norm_bwd(proj[:, k0:k0 + KV_WIDTH], gk_ref[...], dkn_ref[...], 2)
        dgq_ref[...] += dgq
        dgk_ref[...] += dgk
        dproj = jnp.concatenate([dyc * cv, da * hc, da * gate_c, dq, dk, dv_ref[...]], axis=-1).astype(BF16)
        dproj_ref[...] = dproj
        du1 = _dot(dproj, w_ref[...])
        xv = x_ref[...]
        dn, dg1 = _rms_bwd(xv, g1_ref[...], du1)
        dx_ref[...] = dh1_ref[...] + dn
        dg1_ref[...] += dg1
        u1_ref[...] = (xv * _rstd(xv) * g1_ref[...]).astype(BF16)

    rev = lambda i: n_t - 1 - i
    rows = lambda w: pl.BlockSpec((tm, w), lambda i: (rev(i), 0))
    const = lambda shape: pl.BlockSpec(shape, lambda i: (0,) * len(shape))
    halo = pl.BlockSpec((SUBLANES, IN_WIDTH), lambda i: (jnp.maximum(rev(i) * halo_blocks - 1, 0), 0))
    return pl.pallas_call(
        body, name="mix_in_bwd", grid=(n_t,),
        in_specs=[rows(D_MODEL), rows(D_MODEL), rows(IN_WIDTH), halo, rows(CONV_WIDTH), rows(ATTN_WIDTH), rows(KV_WIDTH),
                  rows(KV_WIDTH), const((IN_WIDTH, D_MODEL)), const((3, CONV_WIDTH)), const((1, D_MODEL)),
                  const((1, HEAD_DIM)), const((1, HEAD_DIM)), const((1, CONV_WIDTH))],
        out_specs=[rows(D_MODEL), rows(IN_WIDTH), rows(D_MODEL), const((3, CONV_WIDTH)), const((1, CONV_WIDTH)),
                   const((1, HEAD_DIM)), const((1, HEAD_DIM)), const((1, D_MODEL))],
        out_shape=[jax.ShapeDtypeStruct((SEQ, D_MODEL), F32), jax.ShapeDtypeStruct((SEQ, IN_WIDTH), BF16),
                   jax.ShapeDtypeStruct((SEQ, D_MODEL), BF16), jax.ShapeDtypeStruct((3, CONV_WIDTH), F32),
                   jax.ShapeDtypeStruct((1, CONV_WIDTH), F32), jax.ShapeDtypeStruct((1, HEAD_DIM), F32),
                   jax.ShapeDtypeStruct((1, HEAD_DIM), F32), jax.ShapeDtypeStruct((1, D_MODEL), F32)],
        scratch_shapes=[pltpu.VMEM((SUBLANES, CONV_WIDTH), F32)],
        compiler_params=_params(("arbitrary",)),
    )(x, dh1, proj, proj, dycn, dqn, dkn, dv, w_in_t, conv_w, g1, gq, gk, gconv)


def _grad_w_in(dproj, u1, after=None):
    bw = 768

    def body(a_ref, b_ref, o_ref):
        o_ref[...] = _dot_tn(a_ref[...], b_ref[...]).astype(BF16)

    body, more_specs, more = _ordered_behind(body, 2, after)
    return pl.pallas_call(
        body, name="grad_w_in", grid=(IN_WIDTH // bw,),
        in_specs=[pl.BlockSpec((SEQ, bw), lambda k: (0, k)), pl.BlockSpec((SEQ, D_MODEL), lambda k: (0, 0))] + more_specs,
        out_specs=pl.BlockSpec((bw, D_MODEL), lambda k: (k, 0)),
        out_shape=jax.ShapeDtypeStruct((IN_WIDTH, D_MODEL), BF16),
        compiler_params=_params(("arbitrary",)),
    )(dproj, u1, *more)


def _adamw_math(w, g, m, v):
    m = ADAM_B1 * m + (1.0 - ADAM_B1) * g
    v = ADAM_B2 * v + (1.0 - ADAM_B2) * (g * g)
    m_hat = m / (1.0 - ADAM_B1 ** ADAM_STEP)
    v_hat = v / (1.0 - ADAM_B2 ** ADAM_STEP)
    return -ADAM_LR * (m_hat / (jnp.sqrt(v_hat) + ADAM_EPS) + ADAM_WD * w), m, v


_ROW_G1, _ROW_G2, _ROW_OUT_NORMS, _ROW_FFN_B, _ROW_GQ, _ROW_GK, _ROW_SINKS, _ROW_LOSS, _ROW_TABLE = 0, 1, 2, 3, 11, 12, 13, 14, 16
SMALL_ROWS, SMALL_COLS = 24, 1024
_SMALL_NAMES = ("norm_mix_g", "norm_ffn_g", "out_norm_conv_g", "out_norm_attn_g", "ffn_conv_b", "q_norm_g", "k_norm_g",
                "sinks", "rel_bias_table")


def _pack_small_grads(dg1, dg2, dgconv, dgattn, dfb, dgq, dgk, dsinks, dtbl_t, loss_acc):
    def body(dg1_ref, dg2_ref, dgc_ref, dga_ref, dfb_ref, dgq_ref, dgk_ref, ds_ref, dt_ref, loss_ref, o_ref):
        o_ref[...] = jnp.zeros_like(o_ref)
        o_ref[_ROW_G1:_ROW_G1 + 1, :] = dg1_ref[...]
        o_ref[_ROW_G2:_ROW_G2 + 1, :] = dg2_ref[...]
        o_ref[_ROW_OUT_NORMS:_ROW_OUT_NORMS + 1, 0:CONV_WIDTH] = dgc_ref[...]
        o_ref[_ROW_OUT_NORMS:_ROW_OUT_NORMS + 1, CONV_WIDTH:] = dga_ref[...]
        for k in range(N_DEV):
            o_ref[_ROW_FFN_B + k:_ROW_FFN_B + k + 1, 0:FFN_BLK] = dfb_ref[k // N_FFN_BLK, k % N_FFN_BLK]
        o_ref[_ROW_GQ:_ROW_GQ + 1, 0:HEAD_DIM] = dgq_ref[...]
        o_ref[_ROW_GK:_ROW_GK + 1, 0:HEAD_DIM] = dgk_ref[...]
        o_ref[_ROW_SINKS:_ROW_SINKS + 1, 0:128] = ds_ref[...]
        o_ref[_ROW_LOSS:_ROW_LOSS + 1, 0:128] = loss_ref[0:1, :]
        o_ref[_ROW_TABLE:_ROW_TABLE + N_HEADS, 0:128] = dt_ref[...]

    return pl.pallas_call(body, name="pack_small_grads", out_shape=jax.ShapeDtypeStruct((SMALL_ROWS, SMALL_COLS), F32))(
        dg1, dg2, dgconv, dgattn, dfb, dgq, dgk, dsinks, dtbl_t, loss_acc)


def _adamw_small(recv, params, after):
    names = _SMALL_NAMES
    n = len(names)

    def grad_of(g, name, k=None):
        if name == "norm_mix_g":
            return g[_ROW_G1:_ROW_G1 + 1, :]
        if name == "norm_ffn_g":
            return g[_ROW_G2:_ROW_G2 + 1, :]
        if name == "out_norm_conv_g":
            return g[_ROW_OUT_NORMS:_ROW_OUT_NORMS + 1, 0:CONV_WIDTH]
        if name == "out_norm_attn_g":
            return g[_ROW_OUT_NORMS:_ROW_OUT_NORMS + 1, CONV_WIDTH:]
        if name == "ffn_conv_b":
            return g[_ROW_FFN_B + k:_ROW_FFN_B + k + 1, 0:FFN_BLK]
        if name == "q_norm_g":
            return g[_ROW_GQ:_ROW_GQ + 1, 0:HEAD_DIM]
        if name == "k_norm_g":
            return g[_ROW_GK:_ROW_GK + 1, 0:HEAD_DIM]
        if name == "sinks":
            return g[_ROW_SINKS:_ROW_SINKS + 1, 0:N_HEADS]
        return g[_ROW_TABLE:_ROW_TABLE + N_HEADS, 0:NUM_BUCKETS]

    def body(r_ref, *refs):
        ins, outs, loss_ref = refs[:3 * n], refs[3 * n:7 * n], refs[7 * n]
        g = r_ref[0]
        for s in range(1, N_DEV):
            g = g + r_ref[s]
        loss_ref[...] = g[_ROW_LOSS:_ROW_LOSS + 1, 0:128]
        for i, name in enumerate(names):
            w_ref, m_ref, v_ref = ins[3 * i:3 * i + 3]
            o = outs[4 * i:4 * i + 4]
            cols = [slice(FFN_BLK * k, FFN_BLK * (k + 1)) for k in range(N_DEV)] if name == "ffn_conv_b" else [slice(None)]
            for k, cs in enumerate(cols):
                gk = grad_of(g, name, k)
                d, m2, v2 = _adamw_math(w_ref[:, cs], gk, m_ref[:, cs], v_ref[:, cs])
                o[0][:, cs], o[1][:, cs], o[2][:, cs], o[3][:, cs] = gk, d, m2, v2

    flat = [a for name in names for a in params[name]]
    body, more_specs, more = _ordered_behind(body, 1 + 3 * n, after)
    vmem = pl.BlockSpec(memory_space=pltpu.VMEM)
    out = pl.pallas_call(
        body, name="adamw_small",
        in_specs=[vmem] * (1 + 3 * n) + more_specs,
        out_shape=[jax.ShapeDtypeStruct(params[name][0].shape, F32) for name in names for _ in range(4)]
        + [jax.ShapeDtypeStruct((1, 128), F32)],
        compiler_params=pltpu.CompilerParams(vmem_limit_bytes=VMEM_LIMIT),
    )(recv, *flat, *more)
    return {name: tuple(out[4 * i:4 * i + 4]) for i, name in enumerate(names)}, out[4 * n]


def _adamw(w, m, v, part, recv, chip, name, row_blocks=1, after=None):
    rb = w.shape[0] // row_blocks
    tail = w.shape[1:]
    zeros = (0,) * len(tail)

    def body(chip_ref, w_ref, m_ref, v_ref, p_ref, r_ref, g_o, d_o, m_o, v_o):
        g = p_ref[...].astype(F32)
        for s in range(3):
            g = g + r_ref[s].astype(F32)
        g_o[...] = g
        d_o[...], m_o[...], v_o[...] = _adamw_math(w_ref[...], g, m_ref[...], v_ref[...])

    blk = pl.BlockSpec((rb,) + tail, lambda i, chip_ref: (i,) + zeros)
    pblk = pl.BlockSpec((None, rb) + tail, lambda i, chip_ref: (chip_ref[0], i) + zeros)
    rblk = pl.BlockSpec((3, rb) + tail, lambda i, chip_ref: (0, i) + zeros)
    body, more_specs, more = _ordered_behind(body, 6, after)
    return pl.pallas_call(
        body, name=name,
        grid_spec=pltpu.PrefetchScalarGridSpec(num_scalar_prefetch=1, grid=(row_blocks,),
                                               in_specs=[blk, blk, blk, pblk, rblk] + more_specs, out_specs=[blk] * 4),
        out_shape=[jax.ShapeDtypeStruct(w.shape, F32)] * 4,
        compiler_params=_params(("arbitrary",)),
    )(chip, w, m, v, part, recv, *more)


def kernel(x, norm_mix_g, w_in, conv_w, q_norm_g, k_norm_g, rel_bias_table, sinks, out_norm_conv_g, out_norm_attn_g, w_out, norm_ffn_g, w_up, ffn_conv_w, ffn_conv_b, w_down, loss_target, m_norm_mix_g, m_w_in, m_conv_w, m_q_norm_g, m_k_norm_g, m_rel_bias_table, m_sinks, m_out_norm_conv_g, m_out_norm_attn_g, m_w_out, m_norm_ffn_g, m_w_up, m_ffn_conv_w, m_ffn_conv_b, m_w_down, v_norm_mix_g, v_w_in, v_conv_w, v_q_norm_g, v_k_norm_g, v_rel_bias_table, v_sinks, v_out_norm_conv_g, v_out_norm_attn_g, v_w_out, v_norm_ffn_g, v_w_up, v_ffn_conv_w, v_ffn_conv_b, v_w_down):
    p = dict(norm_mix_g=norm_mix_g, w_in=w_in, conv_w=conv_w, q_norm_g=q_norm_g, k_norm_g=k_norm_g,
             rel_bias_table=rel_bias_table, sinks=sinks, out_norm_conv_g=out_norm_conv_g, out_norm_attn_g=out_norm_attn_g,
             w_out=w_out, norm_ffn_g=norm_ffn_g, w_up=w_up, ffn_conv_w=ffn_conv_w, ffn_conv_b=ffn_conv_b, w_down=w_down)
    m = dict(norm_mix_g=m_norm_mix_g, w_in=m_w_in, conv_w=m_conv_w, q_norm_g=m_q_norm_g, k_norm_g=m_k_norm_g,
             rel_bias_table=m_rel_bias_table, sinks=m_sinks, out_norm_conv_g=m_out_norm_conv_g,
             out_norm_attn_g=m_out_norm_attn_g, w_out=m_w_out, norm_ffn_g=m_norm_ffn_g, w_up=m_w_up,
             ffn_conv_w=m_ffn_conv_w, ffn_conv_b=m_ffn_conv_b, w_down=m_w_down)
    v = dict(norm_mix_g=v_norm_mix_g, w_in=v_w_in, conv_w=v_conv_w, q_norm_g=v_q_norm_g, k_norm_g=v_k_norm_g,
             rel_bias_table=v_rel_bias_table, sinks=v_sinks, out_norm_conv_g=v_out_norm_conv_g,
             out_norm_attn_g=v_out_norm_attn_g, w_out=v_w_out, norm_ffn_g=v_norm_ffn_g, w_up=v_w_up,
             ffn_conv_w=v_ffn_conv_w, ffn_conv_b=v_ffn_conv_b, w_down=v_w_down)

    xs, tgt = x[0], loss_target[0]
    g1, g2, gq, gk, gconv, gattn = norm_mix_g, norm_ffn_g, q_norm_g, k_norm_g, out_norm_conv_g, out_norm_attn_g
    ix, iy, ic = _coords()
    core = ic.astype(jnp.int32).reshape(1)
    chip = (2 * ix + iy).astype(jnp.int32).reshape(1)
    me = _lin(ix, iy, ic).astype(jnp.int32).reshape(1)
    bkt = jnp.asarray(_bucket_map())
    tr = lambda a: a[0].T
    taps = lambda a: jnp.transpose(a, (1, 0, 2))
    tbl_t = rel_bias_table.T

    wi_l, cw_l, wo_l, wu_l, wd_l, fcw_l = _place_shards(
        me, [tr(w_in), taps(conv_w), w_out[0], tr(w_up), w_down[0], taps(ffn_conv_w)], [BF16, F32, BF16, BF16, BF16, F32])
    finish_a, token_a = _all_gather_split([wi_l, cw_l], "mixer", None)
    finish_b, token_b = _all_gather_split([wo_l, wu_l, wd_l, fcw_l], "ffn", token_a)
    wi_g, cw_g = finish_a(token_b)
    w_in_t = wi_g.reshape(IN_WIDTH, D_MODEL)
    conv_w_f = jnp.transpose(cw_g[:, :, 0, :], (1, 0, 2)).reshape(3, CONV_WIDTH)

    proj, ycn, qn, kn, vv = _mix_in_fwd(xs, g1, w_in_t, conv_w_f, gq, gk, gconv)
    y_attn, yan = _attn_fwd(qn, kn, vv, tbl_t, sinks, bkt, gattn)
    wo_g, wu_g, wd_g, fcw_g = finish_b(yan)
    w_out_f = wo_g.reshape(D_MODEL, D_MODEL)
    w_down_f = wd_g.reshape(N_FFN_BLK, FFN_BLK, D_MODEL)
    w_up_f = wu_g.reshape(2, N_FFN_BLK, FFN_BLK, D_MODEL)
    fcw_f = fcw_g.reshape(2, N_FFN_BLK, 3, 1, FFN_BLK)
    fcb = ffn_conv_b.reshape(2, N_FFN_BLK, 1, FFN_BLK)
    h1, u2, up, dh2, dh2b, loss_acc = _ffn_fwd(xs, ycn, yan, w_out_f, g2, w_up_f, fcw_f, fcb, w_down_f, tgt)

    act, dup, dh1, dh1b, dfb, dfcw, dg2 = _ffn_bwd(dh2, dh2b, h1, g2, up, w_up_f, fcw_f, fcb, w_down_f)
    dw_down = _grad_tn_blocked(act, dh2b, "grad_w_down", a_is_blocked=True).reshape(N_DEV, D_FF // N_DEV, D_MODEL)
    dw_up = _grad_tn_blocked(dup.reshape(N_DEV, SEQ, FFN_BLK), u2, "grad_w_up", a_is_blocked=True)
    dw_out = _grad_tn([ycn, yan], dh1b, CONV_WIDTH, "grad_w_out").reshape(N_DEV, D_MODEL // N_DEV, D_MODEL)
    out_bwd = {}

    def behind_ffn(token):
        out_bwd["r"] = _out_bwd(dh1b, w_out_f, y_attn, gattn, after=token)
        return out_bwd["r"][0]

    finish_ffn, token_ffn = _reduce_scatter_split(
        [dw_down, dw_up, dw_out, dfcw.reshape(N_DEV, 3, 1, FFN_BLK)], "ffn", core, behind_ffn)
    dycn, dy_attn, dgattn = out_bwd["r"]
    dqn, dkn, dv, dtbl_t, dsinks = _attn_bwd(qn, kn, vv, dy_attn, tbl_t, sinks, bkt, after=token_ffn)
    dx, dproj, u1, dcw, dgconv, dgq, dgk, dg1 = _mix_in_bwd(xs, dh1, proj, dycn, dqn, dkn, dv, w_in_t, conv_w_f,
                                                             g1, gq, gk, gconv)
    packed = _pack_small_grads(dg1, dg2, dgconv, dgattn, dfb, dgq, dgk, dsinks, dtbl_t, loss_acc)
    plan_s, slots_s = _broadcast_plan()
    s_sem, r_sem, src_s, land_s, token_s = _split_start(
        "gather_small_start", [packed], [jnp.broadcast_to(packed[None], (N_DEV,) + packed.shape)], plan_s, None)
    dw_in_t = _grad_w_in(dproj, u1, after=token_s).reshape(N_DEV, IN_WIDTH // N_DEV, D_MODEL)
    dcw_b = jnp.transpose(dcw.reshape(3, N_DEV, 1, CONV_WIDTH // N_DEV), (1, 0, 2, 3))
    adam = {}
    ffn_got = {}

    def behind_mixer(token):
        ffn_got["r"] = finish_ffn(token)
        (p_wd, _, _, _), (r_wd, _, _, _) = ffn_got["r"]
        adam["w_down"] = _adamw(w_down[0], m_w_down[0], v_w_down[0], p_wd, r_wd, chip, "adamw_w_down", row_blocks=2)
        return adam["w_down"][0]

    finish_mixer, token_mixer = _reduce_scatter_split([dw_in_t, dcw_b], "mixer", core, behind_mixer)
    (_, p_wu, p_wo, p_fcw), (_, r_wu, r_wo, r_fcw) = ffn_got["r"]
    adam_up = _adamw(tr(w_up), tr(m_w_up), tr(v_w_up), p_wu, r_wu, chip, "adamw_w_up", row_blocks=4, after=token_mixer)
    adam["w_out"] = _adamw(w_out[0], m_w_out[0], v_w_out[0], p_wo, r_wo, chip, "adamw_w_out", after=adam_up[0])
    adam_fcw = _adamw(taps(ffn_conv_w), taps(m_ffn_conv_w), taps(v_ffn_conv_w), p_fcw, r_fcw, chip, "adamw_ffn_conv_w",
                      after=adam["w_out"][0])
    _, (r_small,) = _split_wait("gather_small_wait", s_sem, r_sem, src_s, land_s, plan_s, slots_s, adam_fcw[0])
    small_in = {k: (p[k], m[k], v[k]) for k in _SMALL_NAMES}
    small_in["rel_bias_table"] = (tbl_t, m_rel_bias_table.T, v_rel_bias_table.T)
    small_out, loss_row = _adamw_small(r_small, small_in, None)
    (p_wi, p_cw), (r_wi, r_cw) = finish_mixer(loss_row)
    adam_in = _adamw(tr(w_in), tr(m_w_in), tr(v_w_in), p_wi, r_wi, chip, "adamw_w_in")
    adam_cw = _adamw(taps(conv_w), taps(m_conv_w), taps(v_conv_w), p_cw, r_cw, chip, "adamw_conv_w")

    res = {k: tuple(a[None] for a in t) for k, t in adam.items()}
    res["w_up"] = tuple(a.T[None] for a in adam_up)
    res["w_in"] = tuple(a.T[None] for a in adam_in)
    res["ffn_conv_w"] = tuple(taps(a) for a in adam_fcw)
    res["conv_w"] = tuple(taps(a) for a in adam_cw)
    res.update(small_out)
    res["rel_bias_table"] = tuple(a.T for a in small_out["rel_bias_table"])
    loss = loss_row[0, 0]
    order = ("norm_mix_g", "w_in", "conv_w", "q_norm_g", "k_norm_g", "rel_bias_table", "sinks", "out_norm_conv_g",
             "out_norm_attn_g", "w_out", "norm_ffn_g", "w_up", "ffn_conv_w", "ffn_conv_b", "w_down")
    return (loss, dx[None], *[res[k][0] for k in order], *[res[k][1] for k in order],
            *[res[k][2] for k in order], *[res[k][3] for k in order])
```

```python
import functools
import math

import numpy as np
import jax
import jax.numpy as jnp
from jax import lax
from jax.experimental import pallas as pl
from jax.experimental.pallas import tpu as pltpu

F32 = jnp.float32
BF16 = jnp.bfloat16

SEQ = 2048
D_MODEL = 1024
CONV_WIDTH = 512
ATTN_WIDTH = 512
KV_WIDTH = 128
HEAD_DIM = 64
N_HEADS = 8
GQA_GROUP = 4
IN_WIDTH = 2304
D_FF = 2816
BLK = 128
NUM_BUCKETS = 32
EPS = 1e-6
NEG_INF = -1e30
ADAM_LR = 0.001
ADAM_B1 = 0.9
ADAM_B2 = 0.999
ADAM_EPS = 1e-08
ADAM_WD = 0.01
ADAM_STEP = 10

N_DEV = 8
FFN_BLK = 2 * D_FF // N_DEV
N_FFN_BLK = D_FF // FFN_BLK
SUBLANES = 8
VMEM_LIMIT = 56 * 1024 * 1024

_MESH = pl.DeviceIdType.MESH
_ANY = pl.BlockSpec(memory_space=pl.ANY)


def _params(sem):
    return pltpu.CompilerParams(dimension_semantics=sem, vmem_limit_bytes=VMEM_LIMIT)


def _ordered_behind(body, pos, after):
    if after is None:
        return body, [], []
    return (lambda *refs: body(*refs[:pos], *refs[pos + 1:])), [_ANY], [after]


def _dot(a, b):
    return jnp.dot(a, b, preferred_element_type=F32)


def _dot_nt(a, b):
    return lax.dot_general(a, b, (((1,), (1,)), ((), ())), preferred_element_type=F32)


def _dot_tn(a, b):
    return lax.dot_general(a, b, (((0,), (0,)), ((), ())), preferred_element_type=F32)


def _shift_down(x, s, halo):
    r = pltpu.roll(x, s, axis=0)
    hr = pltpu.roll(halo, s, axis=0)
    row = lax.broadcasted_iota(jnp.int32, halo.shape, 0)
    top = jnp.where(row < s, hr, r[:SUBLANES])
    return jnp.concatenate([top, r[SUBLANES:]], axis=0)


def _shift_up(x, s, halo):
    n = x.shape[0]
    r = pltpu.roll(x, n - s, axis=0)
    hr = pltpu.roll(halo, SUBLANES - s, axis=0)
    row = lax.broadcasted_iota(jnp.int32, halo.shape, 0)
    bot = jnp.where(row >= SUBLANES - s, hr, r[n - SUBLANES:])
    return jnp.concatenate([r[:n - SUBLANES], bot], axis=0)


def _taps(w):
    return (w[0], w[1], w[2]) if len(w.shape) == 3 else (w[0:1], w[1:2], w[2:3])


def _conv3(x, w, halo):
    x2 = _shift_down(x, 2, halo)
    x1 = _shift_down(x, 1, halo)
    return x2 * w[0] + x1 * w[1] + x * w[2], x2, x1


def _conv3_bwd_input(dy, w, halo_next):
    return dy * w[2] + _shift_up(dy, 1, halo_next) * w[1] + _shift_up(dy, 2, halo_next) * w[0]


def _rstd(x):
    return lax.rsqrt(jnp.mean(x * x, axis=-1, keepdims=True) + EPS)


def _rms_bwd(x, g, dy):
    r = _rstd(x)
    n = x * r
    dn = dy * g
    dx = r * (dn - n * jnp.mean(dn * n, axis=-1, keepdims=True))
    return dx, jnp.sum(dy * n, axis=0, keepdims=True)


def _head_norm(x, g, heads):
    parts = []
    for h in range(heads):
        xh = x[:, HEAD_DIM * h:HEAD_DIM * (h + 1)]
        parts.append(xh * _rstd(xh) * g)
    return jnp.concatenate(parts, axis=-1)


def _head_norm_bwd(x, g, dy, heads):
    dxs, dg = [], jnp.zeros((1, HEAD_DIM), F32)
    for h in range(heads):
        sl = slice(HEAD_DIM * h, HEAD_DIM * (h + 1))
        dxh, dgh = _rms_bwd(x[:, sl], g, dy[:, sl])
        dxs.append(dxh)
        dg = dg + dgh
    return jnp.concatenate(dxs, axis=-1), dg


def _bucket_map():
    q = np.arange(BLK)[:, None]
    j = np.arange(BLK)[None, :]
    n = np.where(j > q, q + BLK - j, q - j)
    nf = np.maximum(n, 1).astype(np.float32)
    max_exact = NUM_BUCKETS // 2
    large = max_exact + (np.log(nf / max_exact) / math.log(BLK / max_exact) * (NUM_BUCKETS - max_exact)).astype(np.int32)
    large = np.minimum(large, NUM_BUCKETS - 1)
    return np.where(n < max_exact, n, large).astype(np.int32)


def _coords():
    return lax.axis_index("x"), lax.axis_index("y"), lax.axis_index("c")


def _lin(px, py, pc):
    return 4 * px + 2 * py + pc


_HBM = pl.BlockSpec(memory_space=pltpu.HBM)
_SEM = pl.BlockSpec(memory_space=pltpu.SEMAPHORE)
_EFFECT = pltpu.SideEffectType.DATAFLOW_SIDE_EFFECTING


def _in_hbm(a):
    return pltpu.with_memory_space_constraint(a, pltpu.HBM)


def _split_start(name, srcs, lands, plan, after):
    ns, nl = len(srcs), len(lands)
    n_copies = len(plan(0, 0, 0))
    n_after = 0 if after is None else 1

    def body(*refs):
        src_refs, land_refs = refs[:ns + nl], refs[ns:ns + nl]
        send_sems, recv_sems = refs[ns + nl + n_after], refs[ns + nl + n_after + 1]
        token = refs[-1]
        for k, (a, s_slot, l, d_slot, dev) in enumerate(plan(*_coords())):
            src = src_refs[a] if s_slot is None else src_refs[a].at[s_slot]
            pltpu.make_async_remote_copy(src_ref=src, dst_ref=land_refs[l].at[d_slot], send_sem=send_sems.at[k],
                                         recv_sem=recv_sems.at[k], device_id=dev, device_id_type=_MESH).start()
        token[...] = jnp.zeros_like(token)

    arrs = list(srcs) + list(lands)
    out = pl.pallas_call(
        body, name=name,
        out_shape=(pltpu.SemaphoreType.DMA((n_copies,)), pltpu.SemaphoreType.DMA((n_copies,)),
                   *[pltpu.HBM(a.shape, a.dtype) for a in arrs], jax.ShapeDtypeStruct((SUBLANES, 128), F32)),
        in_specs=[_HBM] * (ns + nl) + [_ANY] * n_after,
        out_specs=(_SEM, _SEM, *[_HBM] * (ns + nl), pl.BlockSpec(memory_space=pltpu.VMEM)),
        input_output_aliases={i: 2 + i for i in range(ns + nl)},
        compiler_params=pltpu.CompilerParams(has_side_effects=_EFFECT),
    )(*[_in_hbm(a) for a in arrs], *([] if after is None else [after]))
    return out[0], out[1], list(out[2:2 + ns]), list(out[2 + ns:2 + ns + nl]), out[-1]


def _split_wait(name, send_sems, recv_sems, srcs, lands, plan, recv_slots, after):
    ns, nl = len(srcs), len(lands)

    def body(*refs):
        src_refs, land_refs = refs[:ns + nl], refs[ns:ns + nl]
        send_sems, recv_sems = refs[ns + nl], refs[ns + nl + 1]
        coords = _coords()
        slots = recv_slots(*coords)
        for k, (a, s_slot, l, _, dev) in enumerate(plan(*coords)):
            src = src_refs[a] if s_slot is None else src_refs[a].at[s_slot]
            cp = pltpu.make_async_remote_copy(src_ref=src, dst_ref=land_refs[l].at[slots[k]], send_sem=send_sems.at[k],
                                              recv_sem=recv_sems.at[k], device_id=dev, device_id_type=_MESH)
            cp.wait_send()
            cp.wait_recv()

    arrs = list(srcs) + list(lands)
    out = pl.pallas_call(
        body, name=name,
        out_shape=tuple(pltpu.HBM(a.shape, a.dtype) for a in arrs),
        in_specs=[_HBM] * (ns + nl) + [_SEM, _SEM, _ANY],
        out_specs=tuple([_HBM] * (ns + nl)),
        input_output_aliases={i: i for i in range(ns + nl)},
        compiler_params=pltpu.CompilerParams(has_side_effects=_EFFECT),
    )(*arrs, send_sems, recv_sems, after)
    return list(out[:ns]), list(out[ns:])


def _chips(x, y):
    return [(1 - x, y), (x, 1 - y), (1 - x, 1 - y)]


def _gather_plan_ici(n):
    def plan(x, y, c):
        me = _lin(x, y, c)
        out = []
        for a in range(n):
            out.append((a, me, a, me, (x, y, 1 - c)))
            out += [(a, me, a, me, (cx, cy, c)) for cx, cy in _chips(x, y)]
        return out

    def recv_slots(x, y, c):
        out = []
        for _ in range(n):
            out.append(_lin(x, y, 1 - c))
            out += [_lin(cx, cy, c) for cx, cy in _chips(x, y)]
        return out

    return plan, recv_slots


def _gather_plan_d2d(n):
    def plan(x, y, c):
        return [(a, _lin(cx, cy, c), a, _lin(cx, cy, c), (x, y, 1 - c)) for a in range(n) for cx, cy in _chips(x, y)]

    def recv_slots(x, y, c):
        return [_lin(cx, cy, 1 - c) for _ in range(n) for cx, cy in _chips(x, y)]

    return plan, recv_slots


def _all_gather_split(lands, tag, after):
    n = len(lands)
    plan1, slots1 = _gather_plan_ici(n)
    s1, r1, _, lands, token = _split_start(f"gather_{tag}_ici_start", [], lands, plan1, after)

    def finish(after):
        _, got = _split_wait(f"gather_{tag}_ici_wait", s1, r1, [], lands, plan1, slots1, after)
        plan2, slots2 = _gather_plan_d2d(n)
        s2, r2, _, got, token2 = _split_start(f"gather_{tag}_d2d_start", [], got, plan2, None)
        return _split_wait(f"gather_{tag}_d2d_wait", s2, r2, [], got, plan2, slots2, token2)[1]

    return finish, token


_CHIP_LIST = ((0, 0), (0, 1), (1, 0), (1, 1))


def _reduce_plan_d2d(n):
    def plan(x, y, c):
        return [(a, _lin(qx, qy, 1 - c), a, q, (x, y, 1 - c)) for a in range(n) for q, (qx, qy) in enumerate(_CHIP_LIST)]

    def recv_slots(x, y, c):
        return [q for _ in range(n) for q in range(4)]

    return plan, recv_slots


def _reduce_plan_ici(n):
    def plan(x, y, c):
        return [(a, 2 * cx + cy, a, j, (cx, cy, c)) for a in range(n) for j, (cx, cy) in enumerate(_chips(x, y))]

    def recv_slots(x, y, c):
        return [j for _ in range(n) for j in range(3)]

    return plan, recv_slots


def _broadcast_plan():
    def peers(x, y, c):
        return [(1 - x if r & 4 else x, 1 - y if r & 2 else y, 1 - c if r & 1 else c) for r in range(1, N_DEV)]

    def plan(x, y, c):
        return [(0, None, 0, _lin(x, y, c), peer) for peer in peers(x, y, c)]

    def recv_slots(x, y, c):
        return [_lin(*peer) for peer in peers(x, y, c)]

    return plan, recv_slots


def _chip_partial(grads, recvd, core, name):
    n = len(grads)

    def body(c_ref, *refs):
        for a in range(n):
            g_ref, r_ref, o_ref = refs[a], refs[n + a], refs[2 * n + a]
            o_ref[...] = (g_ref[...].astype(F32) + r_ref[...].astype(F32)).astype(o_ref.dtype)

    def blk(a, own):
        zeros = (0,) * (a.ndim - 1)
        return pl.BlockSpec((None,) + a.shape[1:],
                            (lambda q, c_ref: (2 * q + c_ref[0],) + zeros) if own else (lambda q, c_ref: (q,) + zeros))

    return pl.pallas_call(
        body, name=name,
        grid_spec=pltpu.PrefetchScalarGridSpec(
            num_scalar_prefetch=1, grid=(4,),
            in_specs=[blk(a, True) for a in grads] + [blk(a, False) for a in recvd],
            out_specs=[blk(a, False) for a in recvd]),
        out_shape=[jax.ShapeDtypeStruct(a.shape, a.dtype) for a in recvd],
        compiler_params=_params(("arbitrary",)),
    )(core, *grads, *recvd)


def _reduce_scatter_split(grads, tag, core, behind):
    n = len(grads)
    plan1, slots1 = _reduce_plan_d2d(n)
    lands1 = [lax.empty((4,) + a.shape[1:], a.dtype) for a in grads]
    s1, r1, srcs1, lands1, token1 = _split_start(f"reduce_{tag}_d2d_start", grads, lands1, plan1, None)
    own, got = _split_wait(f"reduce_{tag}_d2d_wait", s1, r1, srcs1, lands1, plan1, slots1, behind(token1))
    parts = _chip_partial(own, got, core, f"reduce_{tag}_partial")
    plan2, slots2 = _reduce_plan_ici(n)
    lands2 = [lax.empty((3,) + a.shape[1:], a.dtype) for a in grads]
    s2, r2, srcs2, lands2, token2 = _split_start(f"reduce_{tag}_ici_start", parts, lands2, plan2, None)

    def finish(after):
        return _split_wait(f"reduce_{tag}_ici_wait", s2, r2, srcs2, lands2, plan2, slots2, after)

    return finish, token2


def _place_shards(me, shards, dtypes):
    n = len(shards)

    def body(me_ref, *refs):
        for a in range(n):
            refs[n + a][...] = refs[a][...].astype(dtypes[a])

    full = lambda s: pl.BlockSpec(s.shape, lambda i, me_ref: (0,) * s.ndim)
    slot = lambda s: pl.BlockSpec((None,) + s.shape, lambda i, me_ref: (me_ref[0],) + (0,) * s.ndim)
    return pl.pallas_call(
        body, name="place_shards",
        grid_spec=pltpu.PrefetchScalarGridSpec(num_scalar_prefetch=1, grid=(1,), in_specs=[full(s) for s in shards],
                                               out_specs=[slot(s) for s in shards]),
        out_shape=[jax.ShapeDtypeStruct((N_DEV,) + s.shape, d) for s, d in zip(shards, dtypes)],
        compiler_params=_params(("arbitrary",)),
    )(me, *shards)


def _mix_in_fwd(x, g1, w_in_t, conv_w, gq, gk, gconv):
    tm = 512
    n_t = SEQ // tm

    def body(x_ref, g1_ref, w_ref, cw_ref, gq_ref, gk_ref, gc_ref,
             proj_ref, ycn_ref, qn_ref, kn_ref, v_ref, halo_ref):
        @pl.when(pl.program_id(0) == 0)
        def _():
            halo_ref[...] = jnp.zeros_like(halo_ref)

        xv = x_ref[...]
        u = (xv * _rstd(xv) * g1_ref[...]).astype(BF16)
        proj = _dot_nt(u, w_ref[...])
        proj_ref[...] = proj
        gate_b = proj[:, 0:CONV_WIDTH]
        a = proj[:, CONV_WIDTH:2 * CONV_WIDTH] * proj[:, 2 * CONV_WIDTH:3 * CONV_WIDTH]
        cv, _, _ = _conv3(a, _taps(cw_ref[...]), halo_ref[...])
        halo_ref[...] = a[tm - SUBLANES:]
        yc = gate_b * cv
        ycn_ref[...] = (yc * _rstd(yc) * gc_ref[...]).astype(BF16)
        q0 = 3 * CONV_WIDTH
        qn_ref[...] = _head_norm(proj[:, q0:q0 + ATTN_WIDTH], gq_ref[...], N_HEADS).astype(BF16)
        k0 = q0 + ATTN_WIDTH
        kn_ref[...] = _head_norm(proj[:, k0:k0 + KV_WIDTH], gk_ref[...], 2).astype(BF16)
        v_ref[...] = proj[:, k0 + KV_WIDTH:k0 + 2 * KV_WIDTH].astype(BF16)

    const = lambda shape: pl.BlockSpec(shape, lambda i: (0,) * len(shape))
    rows = lambda w: pl.BlockSpec((tm, w), lambda i: (i, 0))
    return pl.pallas_call(
        body, name="mix_in_fwd", grid=(n_t,),
        in_specs=[rows(D_MODEL), const((1, D_MODEL)), const((IN_WIDTH, D_MODEL)), const((3, CONV_WIDTH)),
                  const((1, HEAD_DIM)), const((1, HEAD_DIM)), const((1, CONV_WIDTH))],
        out_specs=[rows(IN_WIDTH), rows(CONV_WIDTH), rows(ATTN_WIDTH), rows(KV_WIDTH), rows(KV_WIDTH)],
        out_shape=[jax.ShapeDtypeStruct((SEQ, IN_WIDTH), F32), jax.ShapeDtypeStruct((SEQ, CONV_WIDTH), BF16),
                   jax.ShapeDtypeStruct((SEQ, ATTN_WIDTH), BF16), jax.ShapeDtypeStruct((SEQ, KV_WIDTH), BF16),
                   jax.ShapeDtypeStruct((SEQ, KV_WIDTH), BF16)],
        scratch_shapes=[pltpu.VMEM((SUBLANES, CONV_WIDTH), F32)],
        compiler_params=_params(("arbitrary",)),
    )(x, g1, w_in_t, conv_w, gq, gk, gconv)


def _band_bias(tbl_ref, bkt, bias_ref):
    for h in range(N_HEADS):
        acc = jnp.zeros(bkt.shape, F32)
        for b in range(NUM_BUCKETS):
            acc = jnp.where(bkt == b, tbl_ref[h, b], acc)
        bias_ref[h] = acc


def _band_masks(i):
    qi = lax.broadcasted_iota(jnp.int32, (BLK, BLK), 0)
    ji = lax.broadcasted_iota(jnp.int32, (BLK, BLK), 1)
    upper = ji > qi
    return upper, upper & (i == 0)


def _band_rows(ref, i):
    prev = pl.multiple_of(jnp.maximum(i - 1, 0) * BLK, BLK)
    cur = pl.multiple_of(i * BLK, BLK)
    return jnp.concatenate([ref[pl.ds(prev, BLK), :], ref[pl.ds(cur, BLK), :]], axis=0), prev, cur


def _fold(band, upper):
    return jnp.where(upper, band[:, :BLK], band[:, BLK:])


def _unfold(tile, upper):
    return jnp.concatenate([jnp.where(upper, tile, 0.0), jnp.where(upper, 0.0, tile)], axis=1)


def _head_probs(qh, kh, bias, upper, dead, sink):
    logits = _fold(_dot_nt(qh, kh), upper) * (HEAD_DIM ** -0.5) + bias
    logits = jnp.where(dead, NEG_INF, logits)
    m = jnp.maximum(jnp.max(logits, axis=-1, keepdims=True), sink)
    p = jnp.exp(logits - m)
    es = jnp.exp(sink - m)
    den = jnp.sum(p, axis=-1, keepdims=True) + es
    return p / den, es / den


def _attn_fwd(qn, kn, v, tbl, sinks, bkt, gattn):
    n_b = SEQ // BLK

    def body(q_ref, k_ref, v_ref, tbl_ref, sink_ref, bkt_ref, ga_ref, y_ref, yn_ref, bias_ref):
        i = pl.program_id(0)

        @pl.when(i == 0)
        def _():
            _band_bias(tbl_ref, bkt_ref[...], bias_ref)

        kb, _, _ = _band_rows(k_ref, i)
        vb, _, _ = _band_rows(v_ref, i)
        upper, dead = _band_masks(i)
        q = q_ref[...]
        outs = []
        for h in range(N_HEADS):
            hk = h // GQA_GROUP
            kv = slice(HEAD_DIM * hk, HEAD_DIM * (hk + 1))
            probs, _ = _head_probs(q[:, HEAD_DIM * h:HEAD_DIM * (h + 1)], kb[:, kv], bias_ref[h], upper, dead,
                                   sink_ref[0, h])
            outs.append(_dot(_unfold(probs, upper).astype(BF16), vb[:, kv]))
        y = jnp.concatenate(outs, axis=-1)
        y_ref[...] = y
        yn_ref[...] = (y * _rstd(y) * ga_ref[...]).astype(BF16)

    const = lambda shape: pl.BlockSpec(shape, lambda i: (0,) * len(shape))
    rows = lambda w: pl.BlockSpec((BLK, w), lambda i: (i, 0))
    smem = pl.BlockSpec(memory_space=pltpu.SMEM)
    return pl.pallas_call(
        body, name="attn_fwd", grid=(n_b,),
        in_specs=[rows(ATTN_WIDTH), const((SEQ, KV_WIDTH)), const((SEQ, KV_WIDTH)), smem, smem,
                  const((BLK, BLK)), const((1, ATTN_WIDTH))],
        out_specs=[rows(ATTN_WIDTH), rows(ATTN_WIDTH)],
        out_shape=[jax.ShapeDtypeStruct((SEQ, ATTN_WIDTH), F32), jax.ShapeDtypeStruct((SEQ, ATTN_WIDTH), BF16)],
        scratch_shapes=[pltpu.VMEM((N_HEADS, BLK, BLK), F32)],
        compiler_params=_params(("arbitrary",)),
    )(qn, kn, v, tbl, sinks, bkt, gattn)


def _ffn_fwd(x, ycn, yan, w_out, g2, w_up, fcw, fcb, w_down, tgt):
    tm = 512
    n_t = SEQ // tm

    def body(x_ref, ycn_ref, yan_ref, wo_ref, g2_ref, wu_ref, cw_ref, b_ref, wd_ref, tgt_ref,
             h1_ref, u2_ref, up_ref, dh2_ref, dh2b_ref, loss_ref, acc_ref, halo_ref):
        i, j = pl.program_id(0), pl.program_id(1)

        @pl.when((i == 0) & (j == 0))
        def _():
            loss_ref[...] = jnp.zeros_like(loss_ref)

        @pl.when(j == 0)
        def _():
            h1 = x_ref[...] + _dot(ycn_ref[...], wo_ref[0:CONV_WIDTH, :]) + _dot(yan_ref[...], wo_ref[CONV_WIDTH:, :])
            h1_ref[...] = h1
            u2_ref[...] = (h1 * _rstd(h1) * g2_ref[...]).astype(BF16)
            acc_ref[...] = jnp.zeros_like(acc_ref)

        u2 = u2_ref[...]
        pre = []
        for s in range(2):
            up = _dot_nt(u2, wu_ref[s])
            up_ref[s] = up
            halo = jnp.where(i == 0, 0.0, halo_ref[s, j])
            pre.append(_conv3(up, _taps(cw_ref.at[s]), halo)[0] + b_ref[s])
            halo_ref[s, j] = up[tm - SUBLANES:]
        g, val = pre
        act = (g * jax.nn.sigmoid(g) * val).astype(BF16)
        acc_ref[...] += _dot(act, wd_ref[...])

        @pl.when(j == N_FFN_BLK - 1)
        def _():
            err = h1_ref[...] + acc_ref[...] - tgt_ref[...]
            loss_ref[...] += 0.5 * jnp.sum(err * err) / D_MODEL
            dh2 = err / D_MODEL
            dh2_ref[...] = dh2
            dh2b_ref[...] = dh2.astype(BF16)

    rows = lambda w: pl.BlockSpec((tm, w), lambda i, j: (i, 0))
    const = lambda shape: pl.BlockSpec(shape, lambda i, j: (0,) * len(shape))
    pair = lambda *s: pl.BlockSpec((2, None) + s, lambda i, j: (0, j) + (0,) * len(s))
    upb = pl.BlockSpec((2, None, tm, FFN_BLK), lambda i, j: (0, j, i, 0))
    return pl.pallas_call(
        body, name="ffn_fwd", grid=(n_t, N_FFN_BLK),
        in_specs=[rows(D_MODEL), rows(CONV_WIDTH), rows(ATTN_WIDTH), const((D_MODEL, D_MODEL)), const((1, D_MODEL)),
                  pair(FFN_BLK, D_MODEL), pair(3, 1, FFN_BLK), pair(1, FFN_BLK),
                  pl.BlockSpec((None, FFN_BLK, D_MODEL), lambda i, j: (j, 0, 0)), rows(D_MODEL)],
        out_specs=[rows(D_MODEL), rows(D_MODEL), upb, rows(D_MODEL), rows(D_MODEL), const((SUBLANES, 128))],
        out_shape=[jax.ShapeDtypeStruct((SEQ, D_MODEL), F32), jax.ShapeDtypeStruct((SEQ, D_MODEL), BF16),
                   jax.ShapeDtypeStruct((2, N_FFN_BLK, SEQ, FFN_BLK), F32),
                   jax.ShapeDtypeStruct((SEQ, D_MODEL), F32), jax.ShapeDtypeStruct((SEQ, D_MODEL), BF16),
                   jax.ShapeDtypeStruct((SUBLANES, 128), F32)],
        scratch_shapes=[pltpu.VMEM((tm, D_MODEL), F32), pltpu.VMEM((2, N_FFN_BLK, SUBLANES, FFN_BLK), F32)],
        compiler_params=_params(("arbitrary", "arbitrary")),
    )(x, ycn, yan, w_out, g2, w_up, fcw, fcb, w_down, tgt)


def _ffn_bwd(dh2, dh2b, h1, g2, up, w_up, fcw, fcb, w_down):
    tm = 512
    n_t = SEQ // tm
    halo_blocks = tm // SUBLANES

    def body(dh2_ref, dh2b_ref, h1_ref, g2_ref, up_ref, uph_ref, wu_ref, cw_ref, b_ref, wd_ref,
             act_ref, dup_ref, dh1_ref, dh1b_ref, dfb_ref, dfcw_ref, dg2_ref, acc_ref, next_ref):
        i, j = pl.program_id(0), pl.program_id(1)
        first_tile = i == n_t - 1

        @pl.when((i == 0) & (j == 0))
        def _():
            dfb_ref[...] = jnp.zeros_like(dfb_ref)
            dfcw_ref[...] = jnp.zeros_like(dfcw_ref)
            dg2_ref[...] = jnp.zeros_like(dg2_ref)

        @pl.when(j == 0)
        def _():
            acc_ref[...] = jnp.zeros_like(acc_ref)

        ups, pre = [], []
        for s in range(2):
            up = up_ref[s]
            halo = jnp.where(first_tile, 0.0, uph_ref[s])
            p, up2, up1 = _conv3(up, _taps(cw_ref.at[s]), halo)
            pre.append(p + b_ref[s])
            ups.append((up, up1, up2))
        g, val = pre
        sg = jax.nn.sigmoid(g)
        silu = g * sg
        act_ref[...] = (silu * val).astype(BF16)
        dact = _dot_nt(dh2b_ref[...], wd_ref[...])
        dpre = (dact * val * (sg * (1.0 + g * (1.0 - sg))), dact * silu)
        for s in range(2):
            d = dpre[s]
            u, u1, u2 = ups[s]
            dfb_ref[s, j] += jnp.sum(d, axis=0, keepdims=True)
            dfcw_ref[s, j, 0] += jnp.sum(d * u2, axis=0, keepdims=True)
            dfcw_ref[s, j, 1] += jnp.sum(d * u1, axis=0, keepdims=True)
            dfcw_ref[s, j, 2] += jnp.sum(d * u, axis=0, keepdims=True)
            nxt = jnp.where(i == 0, 0.0, next_ref[s, j])
            dup = _conv3_bwd_input(d, _taps(cw_ref.at[s]), nxt).astype(BF16)
            next_ref[s, j] = d[:SUBLANES]
            dup_ref[s] = dup
            acc_ref[...] += _dot(dup, wu_ref[s])

        @pl.when(j == N_FFN_BLK - 1)
        def _():
            dn, dgain = _rms_bwd(h1_ref[...], g2_ref[...], acc_ref[...])
            dh1 = dh2_ref[...] + dn
            dh1_ref[...] = dh1
            dh1b_ref[...] = dh1.astype(BF16)
            dg2_ref[...] += dgain

    rev = lambda i: n_t - 1 - i
    rows = lambda w: pl.BlockSpec((tm, w), lambda i, j: (rev(i), 0))
    const = lambda shape: pl.BlockSpec(shape, lambda i, j: (0,) * len(shape))
    pair = lambda *s: pl.BlockSpec((2, None) + s, lambda i, j: (0, j) + (0,) * len(s))
    upb = pl.BlockSpec((2, None, tm, FFN_BLK), lambda i, j: (0, j, rev(i), 0))
    halo = pl.BlockSpec((2, None, SUBLANES, FFN_BLK), lambda i, j: (0, j, jnp.maximum(rev(i) * halo_blocks - 1, 0), 0))
    return pl.pallas_call(
        body, name="ffn_bwd", grid=(n_t, N_FFN_BLK),
        in_specs=[rows(D_MODEL), rows(D_MODEL), rows(D_MODEL), const((1, D_MODEL)), upb, halo,
                  pair(FFN_BLK, D_MODEL), pair(3, 1, FFN_BLK), pair(1, FFN_BLK),
                  pl.BlockSpec((None, FFN_BLK, D_MODEL), lambda i, j: (j, 0, 0))],
        out_specs=[pl.BlockSpec((None, tm, FFN_BLK), lambda i, j: (j, rev(i), 0)), upb, rows(D_MODEL), rows(D_MODEL),
                   const((2, N_FFN_BLK, 1, FFN_BLK)), const((2, N_FFN_BLK, 3, 1, FFN_BLK)), const((1, D_MODEL))],
        out_shape=[jax.ShapeDtypeStruct((N_FFN_BLK, SEQ, FFN_BLK), BF16),
                   jax.ShapeDtypeStruct((2, N_FFN_BLK, SEQ, FFN_BLK), BF16), jax.ShapeDtypeStruct((SEQ, D_MODEL), F32),
                   jax.ShapeDtypeStruct((SEQ, D_MODEL), BF16), jax.ShapeDtypeStruct((2, N_FFN_BLK, 1, FFN_BLK), F32),
                   jax.ShapeDtypeStruct((2, N_FFN_BLK, 3, 1, FFN_BLK), F32), jax.ShapeDtypeStruct((1, D_MODEL), F32)],
        scratch_shapes=[pltpu.VMEM((tm, D_MODEL), F32), pltpu.VMEM((2, N_FFN_BLK, SUBLANES, FFN_BLK), F32)],
        compiler_params=_params(("arbitrary", "arbitrary")),
    )(dh2, dh2b, h1, g2, up, up, w_up, fcw, fcb, w_down)


def _grad_tn(a_list, b, out_rows, name, after=None):
    n = len(a_list)
    ncol = b.shape[1]

    def body(*refs):
        a_refs, b_ref, o_ref = refs[:n], refs[n], refs[n + 1]
        j = pl.program_id(0)
        for k in range(n):
            @pl.when(j == k)
            def _(k=k):
                o_ref[...] = _dot_tn(a_refs[k][...], b_ref[...]).astype(BF16)

    full = lambda shape: pl.BlockSpec(shape, lambda j: (0,) * len(shape))
    body, more_specs, more = _ordered_behind(body, n + 1, after)
    return pl.pallas_call(
        body, name=name, grid=(n,),
        in_specs=[full((SEQ, out_rows))] * n + [full((SEQ, ncol))] + more_specs,
        out_specs=pl.BlockSpec((None, out_rows, ncol), lambda j: (j, 0, 0)),
        out_shape=jax.ShapeDtypeStruct((n, out_rows, ncol), BF16),
        compiler_params=_params(("arbitrary",)),
    )(*a_list, b, *more)


def _grad_tn_blocked(a, b, name, a_is_blocked):
    nb = a.shape[0] if a_is_blocked else b.shape[0]
    a_w, b_w = a.shape[-1], b.shape[-1]

    def body(a_ref, b_ref, o_ref):
        o_ref[...] = _dot_tn(a_ref[...], b_ref[...]).astype(BF16)

    blocked = lambda w: pl.BlockSpec((None, SEQ, w), lambda k: (k, 0, 0))
    full = lambda w: pl.BlockSpec((SEQ, w), lambda k: (0, 0))
    return pl.pallas_call(
        body, name=name, grid=(nb,),
        in_specs=[blocked(a_w) if a_is_blocked else full(a_w), full(b_w) if a_is_blocked else blocked(b_w)],
        out_specs=pl.BlockSpec((None, a_w, b_w), lambda k: (k, 0, 0)),
        out_shape=jax.ShapeDtypeStruct((nb, a_w, b_w), BF16),
        compiler_params=_params(("arbitrary",)),
    )(a, b)


def _out_bwd(dh1b, w_out, y_attn, gattn, after=None):
    tm = 512
    n_t = SEQ // tm

    def body(dh_ref, wo_ref, y_ref, ga_ref, dycn_ref, dy_ref, dga_ref):
        @pl.when(pl.program_id(0) == 0)
        def _():
            dga_ref[...] = jnp.zeros_like(dga_ref)

        dycat = _dot_nt(dh_ref[...], wo_ref[...])
        dycn_ref[...] = dycat[:, :CONV_WIDTH]
        dy, dga = _rms_bwd(y_ref[...], ga_ref[...], dycat[:, CONV_WIDTH:])
        dy_ref[...] = dy
        dga_ref[...] += dga

    rows = lambda w: pl.BlockSpec((tm, w), lambda i: (i, 0))
    const = lambda shape: pl.BlockSpec(shape, lambda i: (0,) * len(shape))
    body, more_specs, more = _ordered_behind(body, 4, after)
    return pl.pallas_call(
        body, name="out_bwd", grid=(n_t,),
        in_specs=[rows(D_MODEL), const((D_MODEL, D_MODEL)), rows(ATTN_WIDTH), const((1, ATTN_WIDTH))] + more_specs,
        out_specs=[rows(CONV_WIDTH), rows(ATTN_WIDTH), const((1, ATTN_WIDTH))],
        out_shape=[jax.ShapeDtypeStruct((SEQ, CONV_WIDTH), F32), jax.ShapeDtypeStruct((SEQ, ATTN_WIDTH), F32),
                   jax.ShapeDtypeStruct((1, ATTN_WIDTH), F32)],
        compiler_params=_params(("arbitrary",)),
    )(dh1b, w_out, y_attn, gattn, *more)


def _attn_bwd(qn, kn, v, dy, tbl, sinks, bkt, after=None):
    n_b = SEQ // BLK

    def body(q_ref, k_ref, v_ref, dy_ref, tbl_ref, sink_ref, bkt_ref,
             dq_ref, dk_ref, dv_ref, dtbl_ref, dsink_ref, bias_ref, dbias_ref, dsacc_ref):
        i = pl.program_id(0)

        @pl.when(i == 0)
        def _():
            _band_bias(tbl_ref, bkt_ref[...], bias_ref)
            dbias_ref[...] = jnp.zeros_like(dbias_ref)
            dsacc_ref[...] = jnp.zeros_like(dsacc_ref)
            dk_ref[...] = jnp.zeros_like(dk_ref)
            dv_ref[...] = jnp.zeros_like(dv_ref)

        kb, prev, cur = _band_rows(k_ref, i)
        vb, _, _ = _band_rows(v_ref, i)
        upper, dead = _band_masks(i)
        q = q_ref[...]
        dy = dy_ref[...]
        lane = lax.broadcasted_iota(jnp.int32, (BLK, 128), 1)
        dqs, dks, dvs = [], [], []
        dsink = jnp.zeros((BLK, 128), F32)
        for h in range(N_HEADS):
            hk = h // GQA_GROUP
            kv = slice(HEAD_DIM * hk, HEAD_DIM * (hk + 1))
            qh = q[:, HEAD_DIM * h:HEAD_DIM * (h + 1)]
            probs, psink = _head_probs(qh, kb[:, kv], bias_ref[h], upper, dead, sink_ref[0, h])
            doh = dy[:, HEAD_DIM * h:HEAD_DIM * (h + 1)].astype(BF16)
            dprobs = _fold(_dot_nt(doh, vb[:, kv]), upper)
            dvh = _dot_tn(_unfold(probs, upper).astype(BF16), doh)
            dsum = jnp.sum(probs * dprobs, axis=-1, keepdims=True)
            dlogits = probs * (dprobs - dsum)
            dsink = jnp.where(lane == h, -psink * dsum, dsink)
            dbias_ref[h] += dlogits
            ds = _unfold(dlogits * (HEAD_DIM ** -0.5), upper).astype(BF16)
            dqs.append(_dot(ds, kb[:, kv]))
            dkh = _dot_tn(ds, qh)
            if h % GQA_GROUP == 0:
                dks.append(dkh)
                dvs.append(dvh)
            else:
                dks[hk] = dks[hk] + dkh
                dvs[hk] = dvs[hk] + dvh
        dq_ref[...] = jnp.concatenate(dqs, axis=-1)
        dsacc_ref[...] += dsink
        dkb = jnp.concatenate(dks, axis=-1)
        dvb = jnp.concatenate(dvs, axis=-1)
        dk_ref[pl.ds(prev, BLK), :] += dkb[:BLK]
        dk_ref[pl.ds(cur, BLK), :] += dkb[BLK:]
        dv_ref[pl.ds(prev, BLK), :] += dvb[:BLK]
        dv_ref[pl.ds(cur, BLK), :] += dvb[BLK:]

        @pl.when(i == n_b - 1)
        def _():
            dsink_ref[...] = jnp.sum(dsacc_ref[...], axis=0, keepdims=True)
            bkt = bkt_ref[...]
            row8 = lax.broadcasted_iota(jnp.int32, (N_HEADS, 128), 0)
            lane8 = lax.broadcasted_iota(jnp.int32, (N_HEADS, 128), 1)
            acc = jnp.zeros((N_HEADS, 128), F32)
            for h in range(N_HEADS):
                dbh = dbias_ref[h]
                for b in range(NUM_BUCKETS):
                    acc = jnp.where((row8 == h) & (lane8 == b), jnp.sum(jnp.where(bkt == b, dbh, 0.0)), acc)
            dtbl_ref[...] = acc

    const = lambda shape: pl.BlockSpec(shape, lambda i: (0,) * len(shape))
    rows = lambda w: pl.BlockSpec((BLK, w), lambda i: (i, 0))
    smem = pl.BlockSpec(memory_space=pltpu.SMEM)
    body, more_specs, more = _ordered_behind(body, 7, after)
    return pl.pallas_call(
        body, name="attn_bwd", grid=(n_b,),
        in_specs=[rows(ATTN_WIDTH), const((SEQ, KV_WIDTH)), const((SEQ, KV_WIDTH)), rows(ATTN_WIDTH), smem, smem,
                  const((BLK, BLK))] + more_specs,
        out_specs=[rows(ATTN_WIDTH), const((SEQ, KV_WIDTH)), const((SEQ, KV_WIDTH)), const((N_HEADS, 128)), const((1, 128))],
        out_shape=[jax.ShapeDtypeStruct((SEQ, ATTN_WIDTH), F32), jax.ShapeDtypeStruct((SEQ, KV_WIDTH), F32),
                   jax.ShapeDtypeStruct((SEQ, KV_WIDTH), F32), jax.ShapeDtypeStruct((N_HEADS, 128), F32),
                   jax.ShapeDtypeStruct((1, 128), F32)],
        scratch_shapes=[pltpu.VMEM((N_HEADS, BLK, BLK), F32), pltpu.VMEM((N_HEADS, BLK, BLK), F32),
                        pltpu.VMEM((BLK, 128), F32)],
        compiler_params=_params(("arbitrary",)),
    )(qn, kn, v, dy, tbl, sinks, bkt, *more)


def _mix_in_bwd(x, dh1, proj, dycn, dqn, dkn, dv, w_in_t, conv_w, g1, gq, gk, gconv):
    tm = 256
    n_t = SEQ // tm
    halo_blocks = tm // SUBLANES

    def body(x_ref, dh1_ref, proj_ref, halo_ref, dycn_ref, dqn_ref, dkn_ref, dv_ref, w_ref, cw_ref,
             g1_ref, gq_ref, gk_ref, gc_ref,
             dx_ref, dproj_ref, u1_ref, dcw_ref, dgc_ref, dgq_ref, dgk_ref, dg1_ref, next_ref):
        i = pl.program_id(0)
        first_tile = i == n_t - 1

        @pl.when(i == 0)
        def _():
            for r in (dcw_ref, dgc_ref, dgq_ref, dgk_ref, dg1_ref, next_ref):
                r[...] = jnp.zeros_like(r)

        proj = proj_ref[...]
        hp = halo_ref[...]
        gate_b = proj[:, 0:CONV_WIDTH]
        gate_c = proj[:, CONV_WIDTH:2 * CONV_WIDTH]
        hc = proj[:, 2 * CONV_WIDTH:3 * CONV_WIDTH]
        a = gate_c * hc
        a_halo = jnp.where(first_tile, 0.0, hp[:, CONV_WIDTH:2 * CONV_WIDTH] * hp[:, 2 * CONV_WIDTH:3 * CONV_WIDTH])
        cw = _taps(cw_ref[...])
        cv, a2, a1 = _conv3(a, cw, a_halo)
        dyc, dgc = _rms_bwd(gate_b * cv, gc_ref[...], dycn_ref[...])
        dgc_ref[...] += dgc
        dcv = dyc * gate_b
        dcw_ref[...] += jnp.concatenate(
            [jnp.sum(dcv * a2, axis=0, keepdims=True), jnp.sum(dcv * a1, axis=0, keepdims=True),
             jnp.sum(dcv * a, axis=0, keepdims=True)], axis=0)
        da = _conv3_bwd_input(dcv, cw, next_ref[...])
        next_ref[...] = dcv[:SUBLANES]
        q0 = 3 * CONV_WIDTH
        k0 = q0 + ATTN_WIDTH
        dq, dgq = _head_norm_bwd(proj[:, q0:k0], gq_ref[...], dqn_ref[...], N_HEADS)
        dk, dgk = _head_norm_bwd(proj[:, k0:k0 + KV_WIDTH], gk_ref[...], dkn_ref[...], 2)
        dgq_ref[...] += dgq
        dgk_ref[...] += dgk
        dproj = jnp.concatenate([dyc * cv, da * hc, da * gate_c, dq, dk, dv_ref[...]], axis=-1).astype(BF16)
        dproj_ref[...] = dproj
        du1 = _dot(dproj, w_ref[...])
        xv = x_ref[...]
        dn, dg1 = _rms_bwd(xv, g1_ref[...], du1)
        dx_ref[...] = dh1_ref[...] + dn
        dg1_ref[...] += dg1
        u1_ref[...] = (xv * _rstd(xv) * g1_ref[...]).astype(BF16)

    rev = lambda i: n_t - 1 - i
    rows = lambda w: pl.BlockSpec((tm, w), lambda i: (rev(i), 0))
    const = lambda shape: pl.BlockSpec(shape, lambda i: (0,) * len(shape))
    halo = pl.BlockSpec((SUBLANES, IN_WIDTH), lambda i: (jnp.maximum(rev(i) * halo_blocks - 1, 0), 0))
    return pl.pallas_call(
        body, name="mix_in_bwd", grid=(n_t,),
        in_specs=[rows(D_MODEL), rows(D_MODEL), rows(IN_WIDTH), halo, rows(CONV_WIDTH), rows(ATTN_WIDTH), rows(KV_WIDTH),
                  rows(KV_WIDTH), const((IN_WIDTH, D_MODEL)), const((3, CONV_WIDTH)), const((1, D_MODEL)),
                  const((1, HEAD_DIM)), const((1, HEAD_DIM)), const((1, CONV_WIDTH))],
        out_specs=[rows(D_MODEL), rows(IN_WIDTH), rows(D_MODEL), const((3, CONV_WIDTH)), const((1, CONV_WIDTH)),
                   const((1, HEAD_DIM)), const((1, HEAD_DIM)), const((1, D_MODEL))],
        out_shape=[jax.ShapeDtypeStruct((SEQ, D_MODEL), F32), jax.ShapeDtypeStruct((SEQ, IN_WIDTH), BF16),
                   jax.ShapeDtypeStruct((SEQ, D_MODEL), BF16), jax.ShapeDtypeStruct((3, CONV_WIDTH), F32),
                   jax.ShapeDtypeStruct((1, CONV_WIDTH), F32), jax.ShapeDtypeStruct((1, HEAD_DIM), F32),
                   jax.ShapeDtypeStruct((1, HEAD_DIM), F32), jax.ShapeDtypeStruct((1, D_MODEL), F32)],
        scratch_shapes=[pltpu.VMEM((SUBLANES, CONV_WIDTH), F32)],
        compiler_params=_params(("arbitrary",)),
    )(x, dh1, proj, proj, dycn, dqn, dkn, dv, w_in_t, conv_w, g1, gq, gk, gconv)


def _grad_w_in(dproj, u1, after=None):
    bw = 768

    def body(a_ref, b_ref, o_ref):
        o_ref[...] = _dot_tn(a_ref[...], b_ref[...]).astype(BF16)

    body, more_specs, more = _ordered_behind(body, 2, after)
    return pl.pallas_call(
        body, name="grad_w_in", grid=(IN_WIDTH // bw,),
        in_specs=[pl.BlockSpec((SEQ, bw), lambda k: (0, k)), pl.BlockSpec((SEQ, D_MODEL), lambda k: (0, 0))] + more_specs,
        out_specs=pl.BlockSpec((bw, D_MODEL), lambda k: (k, 0)),
        out_shape=jax.ShapeDtypeStruct((IN_WIDTH, D_MODEL), BF16),
        compiler_params=_params(("arbitrary",)),
    )(dproj, u1, *more)


def _adamw_math(w, g, m, v):
    m = ADAM_B1 * m + (1.0 - ADAM_B1) * g
    v = ADAM_B2 * v + (1.0 - ADAM_B2) * (g * g)
    m_hat = m / (1.0 - ADAM_B1 ** ADAM_STEP)
    v_hat = v / (1.0 - ADAM_B2 ** ADAM_STEP)
    return -ADAM_LR * (m_hat / (jnp.sqrt(v_hat) + ADAM_EPS) + ADAM_WD * w), m, v


_ROW_G1, _ROW_G2, _ROW_OUT_NORMS, _ROW_FFN_B, _ROW_GQ, _ROW_GK, _ROW_SINKS, _ROW_LOSS, _ROW_TABLE = 0, 1, 2, 3, 11, 12, 13, 14, 16
SMALL_ROWS, SMALL_COLS = 24, 1024
_SMALL_NAMES = ("norm_mix_g", "norm_ffn_g", "out_norm_conv_g", "out_norm_attn_g", "ffn_conv_b", "q_norm_g", "k_norm_g",
                "sinks", "rel_bias_table")


def _pack_small_grads(dg1, dg2, dgconv, dgattn, dfb, dgq, dgk, dsinks, dtbl_t, loss_acc):
    def body(dg1_ref, dg2_ref, dgc_ref, dga_ref, dfb_ref, dgq_ref, dgk_ref, ds_ref, dt_ref, loss_ref, o_ref):
        o_ref[...] = jnp.zeros_like(o_ref)
        o_ref[_ROW_G1:_ROW_G1 + 1, :] = dg1_ref[...]
        o_ref[_ROW_G2:_ROW_G2 + 1, :] = dg2_ref[...]
        o_ref[_ROW_OUT_NORMS:_ROW_OUT_NORMS + 1, 0:CONV_WIDTH] = dgc_ref[...]
        o_ref[_ROW_OUT_NORMS:_ROW_OUT_NORMS + 1, CONV_WIDTH:] = dga_ref[...]
        for k in range(N_DEV):
            o_ref[_ROW_FFN_B + k:_ROW_FFN_B + k + 1, 0:FFN_BLK] = dfb_ref[k // N_FFN_BLK, k % N_FFN_BLK]
        o_ref[_ROW_GQ:_ROW_GQ + 1, 0:HEAD_DIM] = dgq_ref[...]
        o_ref[_ROW_GK:_ROW_GK + 1, 0:HEAD_DIM] = dgk_ref[...]
        o_ref[_ROW_SINKS:_ROW_SINKS + 1, 0:128] = ds_ref[...]
        o_ref[_ROW_LOSS:_ROW_LOSS + 1, 0:128] = loss_ref[0:1, :]
        o_ref[_ROW_TABLE:_ROW_TABLE + N_HEADS, 0:128] = dt_ref[...]

    return pl.pallas_call(body, name="pack_small_grads", out_shape=jax.ShapeDtypeStruct((SMALL_ROWS, SMALL_COLS), F32))(
        dg1, dg2, dgconv, dgattn, dfb, dgq, dgk, dsinks, dtbl_t, loss_acc)


def _adamw_small(recv, params, after):
    names = _SMALL_NAMES
    n = len(names)

    def grad_of(g, name, k=None):
        if name == "norm_mix_g":
            return g[_ROW_G1:_ROW_G1 + 1, :]
        if name == "norm_ffn_g":
            return g[_ROW_G2:_ROW_G2 + 1, :]
        if name == "out_norm_conv_g":
            return g[_ROW_OUT_NORMS:_ROW_OUT_NORMS + 1, 0:CONV_WIDTH]
        if name == "out_norm_attn_g":
            return g[_ROW_OUT_NORMS:_ROW_OUT_NORMS + 1, CONV_WIDTH:]
        if name == "ffn_conv_b":
            return g[_ROW_FFN_B + k:_ROW_FFN_B + k + 1, 0:FFN_BLK]
        if name == "q_norm_g":
            return g[_ROW_GQ:_ROW_GQ + 1, 0:HEAD_DIM]
        if name == "k_norm_g":
            return g[_ROW_GK:_ROW_GK + 1, 0:HEAD_DIM]
        if name == "sinks":
            return g[_ROW_SINKS:_ROW_SINKS + 1, 0:N_HEADS]
        return g[_ROW_TABLE:_ROW_TABLE + N_HEADS, 0:NUM_BUCKETS]

    def body(r_ref, *refs):
        ins, outs, loss_ref = refs[:3 * n], refs[3 * n:7 * n], refs[7 * n]
        g = r_ref[0]
        for s in range(1, N_DEV):
            g = g + r_ref[s]
        loss_ref[...] = g[_ROW_LOSS:_ROW_LOSS + 1, 0:128]
        for i, name in enumerate(names):
            w_ref, m_ref, v_ref = ins[3 * i:3 * i + 3]
            o = outs[4 * i:4 * i + 4]
            cols = [slice(FFN_BLK * k, FFN_BLK * (k + 1)) for k in range(N_DEV)] if name == "ffn_conv_b" else [slice(None)]
            for k, cs in enumerate(cols):
                gk = grad_of(g, name, k)
                d, m2, v2 = _adamw_math(w_ref[:, cs], gk, m_ref[:, cs], v_ref[:, cs])
                o[0][:, cs], o[1][:, cs], o[2][:, cs], o[3][:, cs] = gk, d, m2, v2

    flat = [a for name in names for a in params[name]]
    body, more_specs, more = _ordered_behind(body, 1 + 3 * n, after)
    vmem = pl.BlockSpec(memory_space=pltpu.VMEM)
    out = pl.pallas_call(
        body, name="adamw_small",
        in_specs=[vmem] * (1 + 3 * n) + more_specs,
        out_shape=[jax.ShapeDtypeStruct(params[name][0].shape, F32) for name in names for _ in range(4)]
        + [jax.ShapeDtypeStruct((1, 128), F32)],
        compiler_params=pltpu.CompilerParams(vmem_limit_bytes=VMEM_LIMIT),
    )(recv, *flat, *more)
    return {name: tuple(out[4 * i:4 * i + 4]) for i, name in enumerate(names)}, out[4 * n]


def _adamw(w, m, v, part, recv, chip, name, row_blocks=1, after=None):
    rb = w.shape[0] // row_blocks
    tail = w.shape[1:]
    zeros = (0,) * len(tail)

    def body(chip_ref, w_ref, m_ref, v_ref, p_ref, r_ref, g_o, d_o, m_o, v_o):
        g = p_ref[...].astype(F32)
        for s in range(3):
            g = g + r_ref[s].astype(F32)
        g_o[...] = g
        d_o[...], m_o[...], v_o[...] = _adamw_math(w_ref[...], g, m_ref[...], v_ref[...])

    blk = pl.BlockSpec((rb,) + tail, lambda i, chip_ref: (i,) + zeros)
    pblk = pl.BlockSpec((None, rb) + tail, lambda i, chip_ref: (chip_ref[0], i) + zeros)
    rblk = pl.BlockSpec((3, rb) + tail, lambda i, chip_ref: (0, i) + zeros)
    body, more_specs, more = _ordered_behind(body, 6, after)
    return pl.pallas_call(
        body, name=name,
        grid_spec=pltpu.PrefetchScalarGridSpec(num_scalar_prefetch=1, grid=(row_blocks,),
                                               in_specs=[blk, blk, blk, pblk, rblk] + more_specs, out_specs=[blk] * 4),
        out_shape=[jax.ShapeDtypeStruct(w.shape, F32)] * 4,
        compiler_params=_params(("arbitrary",)),
    )(chip, w, m, v, part, recv, *more)


def kernel(x, norm_mix_g, w_in, conv_w, q_norm_g, k_norm_g, rel_bias_table, sinks, out_norm_conv_g, out_norm_attn_g, w_out, norm_ffn_g, w_up, ffn_conv_w, ffn_conv_b, w_down, loss_target, m_norm_mix_g, m_w_in, m_conv_w, m_q_norm_g, m_k_norm_g, m_rel_bias_table, m_sinks, m_out_norm_conv_g, m_out_norm_attn_g, m_w_out, m_norm_ffn_g, m_w_up, m_ffn_conv_w, m_ffn_conv_b, m_w_down, v_norm_mix_g, v_w_in, v_conv_w, v_q_norm_g, v_k_norm_g, v_rel_bias_table, v_sinks, v_out_norm_conv_g, v_out_norm_attn_g, v_w_out, v_norm_ffn_g, v_w_up, v_ffn_conv_w, v_ffn_conv_b, v_w_down):
    p = dict(norm_mix_g=norm_mix_g, w_in=w_in, conv_w=conv_w, q_norm_g=q_norm_g, k_norm_g=k_norm_g,
             rel_bias_table=rel_bias_table, sinks=sinks, out_norm_conv_g=out_norm_conv_g, out_norm_attn_g=out_norm_attn_g,
             w_out=w_out, norm_ffn_g=norm_ffn_g, w_up=w_up, ffn_conv_w=ffn_conv_w, ffn_conv_b=ffn_conv_b, w_down=w_down)
    m = dict(norm_mix_g=m_norm_mix_g, w_in=m_w_in, conv_w=m_conv_w, q_norm_g=m_q_norm_g, k_norm_g=m_k_norm_g,
             rel_bias_table=m_rel_bias_table, sinks=m_sinks, out_norm_conv_g=m_out_norm_conv_g,
             out_norm_attn_g=m_out_norm_attn_g, w_out=m_w_out, norm_ffn_g=m_norm_ffn_g, w_up=m_w_up,
             ffn_conv_w=m_ffn_conv_w, ffn_conv_b=m_ffn_conv_b, w_down=m_w_down)
    v = dict(norm_mix_g=v_norm_mix_g, w_in=v_w_in, conv_w=v_conv_w, q_norm_g=v_q_norm_g, k_norm_g=v_k_norm_g,
             rel_bias_table=v_rel_bias_table, sinks=v_sinks, out_norm_conv_g=v_out_norm_conv_g,
             out_norm_attn_g=v_out_norm_attn_g, w_out=v_w_out, norm_ffn_g=v_norm_ffn_g, w_up=v_w_up,
             ffn_conv_w=v_ffn_conv_w, ffn_conv_b=v_ffn_conv_b, w_down=v_w_down)

    xs, tgt = x[0], loss_target[0]
    g1, g2, gq, gk, gconv, gattn = norm_mix_g, norm_ffn_g, q_norm_g, k_norm_g, out_norm_conv_g, out_norm_attn_g
    ix, iy, ic = _coords()
    core = ic.astype(jnp.int32).reshape(1)
    chip = (2 * ix + iy).astype(jnp.int32).reshape(1)
    me = _lin(ix, iy, ic).astype(jnp.int32).reshape(1)
    bkt = jnp.asarray(_bucket_map())
    tr = lambda a: a[0].T
    taps = lambda a: jnp.transpose(a, (1, 0, 2))
    tbl_t = rel_bias_table.T

    wi_l, cw_l, wo_l, wu_l, wd_l, fcw_l = _place_shards(
        me, [tr(w_in), taps(conv_w), w_out[0], tr(w_up), w_down[0], taps(ffn_conv_w)], [BF16, F32, BF16, BF16, BF16, F32])
    finish_a, token_a = _all_gather_split([wi_l, cw_l], "mixer", None)
    finish_b, token_b = _all_gather_split([wo_l, wu_l, wd_l, fcw_l], "ffn", token_a)
    wi_g, cw_g = finish_a(token_b)
    w_in_t = wi_g.reshape(IN_WIDTH, D_MODEL)
    conv_w_f = jnp.transpose(cw_g[:, :, 0, :], (1, 0, 2)).reshape(3, CONV_WIDTH)

    proj, ycn, qn, kn, vv = _mix_in_fwd(xs, g1, w_in_t, conv_w_f, gq, gk, gconv)
    y_attn, yan = _attn_fwd(qn, kn, vv, tbl_t, sinks, bkt, gattn)
    wo_g, wu_g, wd_g, fcw_g = finish_b(yan)
    w_out_f = wo_g.reshape(D_MODEL, D_MODEL)
    w_down_f = wd_g.reshape(N_FFN_BLK, FFN_BLK, D_MODEL)
    w_up_f = wu_g.reshape(2, N_FFN_BLK, FFN_BLK, D_MODEL)
    fcw_f = fcw_g.reshape(2, N_FFN_BLK, 3, 1, FFN_BLK)
    fcb = ffn_conv_b.reshape(2, N_FFN_BLK, 1, FFN_BLK)
    h1, u2, up, dh2, dh2b, loss_acc = _ffn_fwd(xs, ycn, yan, w_out_f, g2, w_up_f, fcw_f, fcb, w_down_f, tgt)

    act, dup, dh1, dh1b, dfb, dfcw, dg2 = _ffn_bwd(dh2, dh2b, h1, g2, up, w_up_f, fcw_f, fcb, w_down_f)
    dw_down = _grad_tn_blocked(act, dh2b, "grad_w_down", a_is_blocked=True).reshape(N_DEV, D_FF // N_DEV, D_MODEL)
    dw_up = _grad_tn_blocked(dup.reshape(N_DEV, SEQ, FFN_BLK), u2, "grad_w_up", a_is_blocked=True)
    dw_out = _grad_tn([ycn, yan], dh1b, CONV_WIDTH, "grad_w_out").reshape(N_DEV, D_MODEL // N_DEV, D_MODEL)
    out_bwd = {}

    def behind_ffn(token):
        out_bwd["r"] = _out_bwd(dh1b, w_out_f, y_attn, gattn, after=token)
        return out_bwd["r"][0]

    finish_ffn, token_ffn = _reduce_scatter_split(
        [dw_down, dw_up, dw_out, dfcw.reshape(N_DEV, 3, 1, FFN_BLK)], "ffn", core, behind_ffn)
    dycn, dy_attn, dgattn = out_bwd["r"]
    dqn, dkn, dv, dtbl_t, dsinks = _attn_bwd(qn, kn, vv, dy_attn, tbl_t, sinks, bkt, after=token_ffn)
    dx, dproj, u1, dcw, dgconv, dgq, dgk, dg1 = _mix_in_bwd(xs, dh1, proj, dycn, dqn, dkn, dv, w_in_t, conv_w_f,
                                                             g1, gq, gk, gconv)
    packed = _pack_small_grads(dg1, dg2, dgconv, dgattn, dfb, dgq, dgk, dsinks, dtbl_t, loss_acc)
    plan_s, slots_s = _broadcast_plan()
    s_sem, r_sem, src_s, land_s, token_s = _split_start(
        "gather_small_start", [packed], [jnp.broadcast_to(packed[None], (N_DEV,) + packed.shape)], plan_s, None)
    dw_in_t = _grad_w_in(dproj, u1, after=token_s).reshape(N_DEV, IN_WIDTH // N_DEV, D_MODEL)
    dcw_b = jnp.transpose(dcw.reshape(3, N_DEV, 1, CONV_WIDTH // N_DEV), (1, 0, 2, 3))
    adam = {}
    ffn_got = {}

    def behind_mixer(token):
        ffn_got["r"] = finish_ffn(token)
        (p_wd, _, _, _), (r_wd, _, _, _) = ffn_got["r"]
        adam["w_down"] = _adamw(w_down[0], m_w_down[0], v_w_down[0], p_wd, r_wd, chip, "adamw_w_down", row_blocks=2)
        return adam["w_down"][0]

    finish_mixer, token_mixer = _reduce_scatter_split([dw_in_t, dcw_b], "mixer", core, behind_mixer)
    (_, p_wu, p_wo, p_fcw), (_, r_wu, r_wo, r_fcw) = ffn_got["r"]
    adam_up = _adamw(tr(w_up), tr(m_w_up), tr(v_w_up), p_wu, r_wu, chip, "adamw_w_up", row_blocks=4, after=token_mixer)
    adam["w_out"] = _adamw(w_out[0], m_w_out[0], v_w_out[0], p_wo, r_wo, chip, "adamw_w_out", after=adam_up[0])
    adam_fcw = _adamw(taps(ffn_conv_w), taps(m_ffn_conv_w), taps(v_ffn_conv_w), p_fcw, r_fcw, chip, "adamw_ffn_conv_w",
                      after=adam["w_out"][0])
    _, (r_small,) = _split_wait("gather_small_wait", s_sem, r_sem, src_s, land_s, plan_s, slots_s, adam_fcw[0])
    small_in = {k: (p[k], m[k], v[k]) for k in _SMALL_NAMES}
    small_in["rel_bias_table"] = (tbl_t, m_rel_bias_table.T, v_rel_bias_table.T)
    small_out, loss_row = _adamw_small(r_small, small_in, None)
    (p_wi, p_cw), (r_wi, r_cw) = finish_mixer(loss_row)
    adam_in = _adamw(tr(w_in), tr(m_w_in), tr(v_w_in), p_wi, r_wi, chip, "adamw_w_in")
    adam_cw = _adamw(taps(conv_w), taps(m_conv_w), taps(v_conv_w), p_cw, r_cw, chip, "adamw_conv_w")

    res = {k: tuple(a[None] for a in t) for k, t in adam.items()}
    res["w_up"] = tuple(a.T[None] for a in adam_up)
    res["w_in"] = tuple(a.T[None] for a in adam_in)
    res["ffn_conv_w"] = tuple(taps(a) for a in adam_fcw)
    res["conv_w"] = tuple(taps(a) for a in adam_cw)
    res.update(small_out)
    res["rel_bias_table"] = tuple(a.T for a in small_out["rel_bias_table"])
    loss = loss_row[0, 0]
    order = ("norm_mix_g", "w_in", "conv_w", "q_norm_g", "k_norm_g", "rel_bias_table", "sinks", "out_norm_conv_g",
             "out_norm_attn_g", "w_out", "norm_ffn_g", "w_up", "ffn_conv_w", "ffn_conv_b", "w_down")
    return (loss, dx[None], *[res[k][0] for k in order], *[res[k][1] for k in order],
            *[res[k][2] for k in order], *[res[k][3] for k in order])
```

```python
import functools
import math

import numpy as np
import jax
import jax.numpy as jnp
from jax import lax
from jax.experimental import pallas as pl
from jax.experimental.pallas import tpu as pltpu

F32 = jnp.float32
BF16 = jnp.bfloat16

SEQ = 2048
D_MODEL = 1024
CONV_WIDTH = 512
ATTN_WIDTH = 512
KV_WIDTH = 128
HEAD_DIM = 64
N_HEADS = 8
GQA_GROUP = 4
IN_WIDTH = 2304
D_FF = 2816
BLK = 128
NUM_BUCKETS = 32
EPS = 1e-6
NEG_INF = -1e30
ADAM_LR = 0.001
ADAM_B1 = 0.9
ADAM_B2 = 0.999
ADAM_EPS = 1e-08
ADAM_WD = 0.01
ADAM_STEP = 10

N_DEV = 8
FFN_BLK = 2 * D_FF // N_DEV
N_FFN_BLK = D_FF // FFN_BLK
SUBLANES = 8
VMEM_LIMIT = 56 * 1024 * 1024

_MESH = pl.DeviceIdType.MESH
_ANY = pl.BlockSpec(memory_space=pl.ANY)


def _params(sem):
    return pltpu.CompilerParams(dimension_semantics=sem, vmem_limit_bytes=VMEM_LIMIT)


def _ordered_behind(body, pos, after):
    if after is None:
        return body, [], []
    return (lambda *refs: body(*refs[:pos], *refs[pos + 1:])), [_ANY], [after]


def _dot(a, b):
    return jnp.dot(a, b, preferred_element_type=F32)


def _dot_nt(a, b):
    return lax.dot_general(a, b, (((1,), (1,)), ((), ())), preferred_element_type=F32)


def _dot_tn(a, b):
    return lax.dot_general(a, b, (((0,), (0,)), ((), ())), preferred_element_type=F32)


def _shift_down(x, s, halo):
    r = pltpu.roll(x, s, axis=0)
    hr = pltpu.roll(halo, s, axis=0)
    row = lax.broadcasted_iota(jnp.int32, halo.shape, 0)
    top = jnp.where(row < s, hr, r[:SUBLANES])
    return jnp.concatenate([top, r[SUBLANES:]], axis=0)


def _shift_up(x, s, halo):
    n = x.shape[0]
    r = pltpu.roll(x, n - s, axis=0)
    hr = pltpu.roll(halo, SUBLANES - s, axis=0)
    row = lax.broadcasted_iota(jnp.int32, halo.shape, 0)
    bot = jnp.where(row >= SUBLANES - s, hr, r[n - SUBLANES:])
    return jnp.concatenate([r[:n - SUBLANES], bot], axis=0)


def _taps(w):
    return (w[0], w[1], w[2]) if len(w.shape) == 3 else (w[0:1], w[1:2], w[2:3])


def _conv3(x, w, halo):
    x2 = _shift_down(x, 2, halo)
    x1 = _shift_down(x, 1, halo)
    return x2 * w[0] + x1 * w[1] + x * w[2], x2, x1


def _conv3_bwd_input(dy, w, halo_next):
    return dy * w[2] + _shift_up(dy, 1, halo_next) * w[1] + _shift_up(dy, 2, halo_next) * w[0]


def _rstd(x):
    return lax.rsqrt(jnp.mean(x * x, axis=-1, keepdims=True) + EPS)


def _rms_bwd(x, g, dy):
    r = _rstd(x)
    n = x * r
    dn = dy * g
    dx = r * (dn - n * jnp.mean(dn * n, axis=-1, keepdims=True))
    return dx, jnp.sum(dy * n, axis=0, keepdims=True)


def _head_norm(x, g, heads):
    parts = []
    for h in range(heads):
        xh = x[:, HEAD_DIM * h:HEAD_DIM * (h + 1)]
        parts.append(xh * _rstd(xh) * g)
    return jnp.concatenate(parts, axis=-1)


def _head_norm_bwd(x, g, dy, heads):
    dxs, dg = [], jnp.zeros((1, HEAD_DIM), F32)
    for h in range(heads):
        sl = slice(HEAD_DIM * h, HEAD_DIM * (h + 1))
        dxh, dgh = _rms_bwd(x[:, sl], g, dy[:, sl])
        dxs.append(dxh)
        dg = dg + dgh
    return jnp.concatenate(dxs, axis=-1), dg


def _bucket_map():
    q = np.arange(BLK)[:, None]
    j = np.arange(BLK)[None, :]
    n = np.where(j > q, q + BLK - j, q - j)
    nf = np.maximum(n, 1).astype(np.float32)
    max_exact = NUM_BUCKETS // 2
    large = max_exact + (np.log(nf / max_exact) / math.log(BLK / max_exact) * (NUM_BUCKETS - max_exact)).astype(np.int32)
    large = np.minimum(large, NUM_BUCKETS - 1)
    return np.where(n < max_exact, n, large).astype(np.int32)


def _coords():
    return lax.axis_index("x"), lax.axis_index("y"), lax.axis_index("c")


def _lin(px, py, pc):
    return 4 * px + 2 * py + pc


_HBM = pl.BlockSpec(memory_space=pltpu.HBM)
_SEM = pl.BlockSpec(memory_space=pltpu.SEMAPHORE)
_EFFECT = pltpu.SideEffectType.DATAFLOW_SIDE_EFFECTING


def _in_hbm(a):
    return pltpu.with_memory_space_constraint(a, pltpu.HBM)


def _split_start(name, srcs, lands, plan, after):
    ns, nl = len(srcs), len(lands)
    n_copies = len(plan(0, 0, 0))
    n_after = 0 if after is None else 1

    def body(*refs):
        src_refs, land_refs = refs[:ns + nl], refs[ns:ns + nl]
        send_sems, recv_sems = refs[ns + nl + n_after], refs[ns + nl + n_after + 1]
        token = refs[-1]
        for k, (a, s_slot, l, d_slot, dev) in enumerate(plan(*_coords())):
            src = src_refs[a] if s_slot is None else src_refs[a].at[s_slot]
            pltpu.make_async_remote_copy(src_ref=src, dst_ref=land_refs[l].at[d_slot], send_sem=send_sems.at[k],
                                         recv_sem=recv_sems.at[k], device_id=dev, device_id_type=_MESH).start()
        token[...] = jnp.zeros_like(token)

    arrs = list(srcs) + list(lands)
    out = pl.pallas_call(
        body, name=name,
        out_shape=(pltpu.SemaphoreType.DMA((n_copies,)), pltpu.SemaphoreType.DMA((n_copies,)),
                   *[pltpu.HBM(a.shape, a.dtype) for a in arrs], jax.ShapeDtypeStruct((SUBLANES, 128), F32)),
        in_specs=[_HBM] * (ns + nl) + [_ANY] * n_after,
        out_specs=(_SEM, _SEM, *[_HBM] * (ns + nl), pl.BlockSpec(memory_space=pltpu.VMEM)),
        input_output_aliases={i: 2 + i for i in range(ns + nl)},
        compiler_params=pltpu.CompilerParams(has_side_effects=_EFFECT),
    )(*[_in_hbm(a) for a in arrs], *([] if after is None else [after]))
    return out[0], out[1], list(out[2:2 + ns]), list(out[2 + ns:2 + ns + nl]), out[-1]


def _split_wait(name, send_sems, recv_sems, srcs, lands, plan, recv_slots, after):
    ns, nl = len(srcs), len(lands)

    def body(*refs):
        src_refs, land_refs = refs[:ns + nl], refs[ns:ns + nl]
        send_sems, recv_sems = refs[ns + nl], refs[ns + nl + 1]
        coords = _coords()
        slots = recv_slots(*coords)
        for k, (a, s_slot, l, _, dev) in enumerate(plan(*coords)):
            src = src_refs[a] if s_slot is None else src_refs[a].at[s_slot]
            cp = pltpu.make_async_remote_copy(src_ref=src, dst_ref=land_refs[l].at[slots[k]], send_sem=send_sems.at[k],
                                              recv_sem=recv_sems.at[k], device_id=dev, device_id_type=_MESH)
            cp.wait_send()
            cp.wait_recv()

    arrs = list(srcs) + list(lands)
    out = pl.pallas_call(
        body, name=name,
        out_shape=tuple(pltpu.HBM(a.shape, a.dtype) for a in arrs),
        in_specs=[_HBM] * (ns + nl) + [_SEM, _SEM, _ANY],
        out_specs=tuple([_HBM] * (ns + nl)),
        input_output_aliases={i: i for i in range(ns + nl)},
        compiler_params=pltpu.CompilerParams(has_side_effects=_EFFECT),
    )(*arrs, send_sems, recv_sems, after)
    return list(out[:ns]), list(out[ns:])


def _chips(x, y):
    return [(1 - x, y), (x, 1 - y), (1 - x, 1 - y)]


def _gather_plan_ici(n):
    def plan(x, y, c):
        me = _lin(x, y, c)
        out = []
        for a in range(n):
            out.append((a, me, a, me, (x, y, 1 - c)))
            out += [(a, me, a, me, (cx, cy, c)) for cx, cy in _chips(x, y)]
        return out

    def recv_slots(x, y, c):
        out = []
        for _ in range(n):
            out.append(_lin(x, y, 1 - c))
            out += [_lin(cx, cy, c) for cx, cy in _chips(x, y)]
        return out

    return plan, recv_slots


def _gather_plan_d2d(n):
    def plan(x, y, c):
        return [(a, _lin(cx, cy, c), a, _lin(cx, cy, c), (x, y, 1 - c)) for a in range(n) for cx, cy in _chips(x, y)]

    def recv_slots(x, y, c):
        return [_lin(cx, cy, 1 - c) for _ in range(n) for cx, cy in _chips(x, y)]

    return plan, recv_slots


def _all_gather_split(lands, tag, after):
    n = len(lands)
    plan1, slots1 = _gather_plan_ici(n)
    s1, r1, _, lands, token = _split_start(f"gather_{tag}_ici_start", [], lands, plan1, after)

    def finish(after):
        _, got = _split_wait(f"gather_{tag}_ici_wait", s1, r1, [], lands, plan1, slots1, after)
        plan2, slots2 = _gather_plan_d2d(n)
        s2, r2, _, got, token2 = _split_start(f"gather_{tag}_d2d_start", [], got, plan2, None)
        return _split_wait(f"gather_{tag}_d2d_wait", s2, r2, [], got, plan2, slots2, token2)[1]

    return finish, token


_CHIP_LIST = ((0, 0), (0, 1), (1, 0), (1, 1))


def _reduce_plan_d2d(n):
    def plan(x, y, c):
        return [(a, _lin(qx, qy, 1 - c), a, q, (x, y, 1 - c)) for a in range(n) for q, (qx, qy) in enumerate(_CHIP_LIST)]

    def recv_slots(x, y, c):
        return [q for _ in range(n) for q in range(4)]

    return plan, recv_slots


def _reduce_plan_ici(n):
    def plan(x, y, c):
        return [(a, 2 * cx + cy, a, j, (cx, cy, c)) for a in range(n) for j, (cx, cy) in enumerate(_chips(x, y))]

    def recv_slots(x, y, c):
        return [j for _ in range(n) for j in range(3)]

    return plan, recv_slots


def _broadcast_plan():
    def peers(x, y, c):
        return [(1 - x if r & 4 else x, 1 - y if r & 2 else y, 1 - c if r & 1 else c) for r in range(1, N_DEV)]

    def plan(x, y, c):
        return [(0, None, 0, _lin(x, y, c), peer) for peer in peers(x, y, c)]

    def recv_slots(x, y, c):
        return [_lin(*peer) for peer in peers(x, y, c)]

    return plan, recv_slots


def _chip_partial(grads, recvd, core, name):
    n = len(grads)

    def body(c_ref, *refs):
        for a in range(n):
            g_ref, r_ref, o_ref = refs[a], refs[n + a], refs[2 * n + a]
            o_ref[...] = (g_ref[...].astype(F32) + r_ref[...].astype(F32)).astype(o_ref.dtype)

    def blk(a, own):
        zeros = (0,) * (a.ndim - 1)
        return pl.BlockSpec((None,) + a.shape[1:],
                            (lambda q, c_ref: (2 * q + c_ref[0],) + zeros) if own else (lambda q, c_ref: (q,) + zeros))

    return pl.pallas_call(
        body, name=name,
        grid_spec=pltpu.PrefetchScalarGridSpec(
            num_scalar_prefetch=1, grid=(4,),
            in_specs=[blk(a, True) for a in grads] + [blk(a, False) for a in recvd],
            out_specs=[blk(a, False) for a in recvd]),
        out_shape=[jax.ShapeDtypeStruct(a.shape, a.dtype) for a in recvd],
        compiler_params=_params(("arbitrary",)),
    )(core, *grads, *recvd)


def _reduce_scatter_split(grads, tag, core, behind):
    n = len(grads)
    plan1, slots1 = _reduce_plan_d2d(n)
    lands1 = [lax.empty((4,) + a.shape[1:], a.dtype) for a in grads]
    s1, r1, srcs1, lands1, token1 = _split_start(f"reduce_{tag}_d2d_start", grads, lands1, plan1, None)
    own, got = _split_wait(f"reduce_{tag}_d2d_wait", s1, r1, srcs1, lands1, plan1, slots1, behind(token1))
    parts = _chip_partial(own, got, core, f"reduce_{tag}_partial")
    plan2, slots2 = _reduce_plan_ici(n)
    lands2 = [lax.empty((3,) + a.shape[1:], a.dtype) for a in grads]
    s2, r2, srcs2, lands2, token2 = _split_start(f"reduce_{tag}_ici_start", parts, lands2, plan2, None)

    def finish(after):
        return _split_wait(f"reduce_{tag}_ici_wait", s2, r2, srcs2, lands2, plan2, slots2, after)

    return finish, token2


def _place_shards(me, shards, dtypes):
    n = len(shards)

    def body(me_ref, *refs):
        for a in range(n):
            refs[n + a][...] = refs[a][...].astype(dtypes[a])

    full = lambda s: pl.BlockSpec(s.shape, lambda i, me_ref: (0,) * s.ndim)
    slot = lambda s: pl.BlockSpec((None,) + s.shape, lambda i, me_ref: (me_ref[0],) + (0,) * s.ndim)
    return pl.pallas_call(
        body, name="place_shards",
        grid_spec=pltpu.PrefetchScalarGridSpec(num_scalar_prefetch=1, grid=(1,), in_specs=[full(s) for s in shards],
                                               out_specs=[slot(s) for s in shards]),
        out_shape=[jax.ShapeDtypeStruct((N_DEV,) + s.shape, d) for s, d in zip(shards, dtypes)],
        compiler_params=_params(("arbitrary",)),
    )(me, *shards)


def _mix_in_fwd(x, g1, w_in_t, conv_w, gq, gk, gconv):
    tm = 512
    n_t = SEQ // tm

    def body(x_ref, g1_ref, w_ref, cw_ref, gq_ref, gk_ref, gc_ref,
             proj_ref, ycn_ref, qn_ref, kn_ref, v_ref, halo_ref):
        @pl.when(pl.program_id(0) == 0)
        def _():
            halo_ref[...] = jnp.zeros_like(halo_ref)

        xv = x_ref[...]
        u = (xv * _rstd(xv) * g1_ref[...]).astype(BF16)
        proj = _dot_nt(u, w_ref[...])
        proj_ref[...] = proj
        gate_b = proj[:, 0:CONV_WIDTH]
        a = proj[:, CONV_WIDTH:2 * CONV_WIDTH] * proj[:, 2 * CONV_WIDTH:3 * CONV_WIDTH]
        cv, _, _ = _conv3(a, _taps(cw_ref[...]), halo_ref[...])
        halo_ref[...] = a[tm - SUBLANES:]
        yc = gate_b * cv
        ycn_ref[...] = (yc * _rstd(yc) * gc_ref[...]).astype(BF16)
        q0 = 3 * CONV_WIDTH
        qn_ref[...] = _head_norm(proj[:, q0:q0 + ATTN_WIDTH], gq_ref[...], N_HEADS).astype(BF16)
        k0 = q0 + ATTN_WIDTH
        kn_ref[...] = _head_norm(proj[:, k0:k0 + KV_WIDTH], gk_ref[...], 2).astype(BF16)
        v_ref[...] = proj[:, k0 + KV_WIDTH:k0 + 2 * KV_WIDTH].astype(BF16)

    const = lambda shape: pl.BlockSpec(shape, lambda i: (0,) * len(shape))
    rows = lambda w: pl.BlockSpec((tm, w), lambda i: (i, 0))
    return pl.pallas_call(
        body, name="mix_in_fwd", grid=(n_t,),
        in_specs=[rows(D_MODEL), const((1, D_MODEL)), const((IN_WIDTH, D_MODEL)), const((3, CONV_WIDTH)),
                  const((1, HEAD_DIM)), const((1, HEAD_DIM)), const((1, CONV_WIDTH))],
        out_specs=[rows(IN_WIDTH), rows(CONV_WIDTH), rows(ATTN_WIDTH), rows(KV_WIDTH), rows(KV_WIDTH)],
        out_shape=[jax.ShapeDtypeStruct((SEQ, IN_WIDTH), F32), jax.ShapeDtypeStruct((SEQ, CONV_WIDTH), BF16),
                   jax.ShapeDtypeStruct((SEQ, ATTN_WIDTH), BF16), jax.ShapeDtypeStruct((SEQ, KV_WIDTH), BF16),
                   jax.ShapeDtypeStruct((SEQ, KV_WIDTH), BF16)],
        scratch_shapes=[pltpu.VMEM((SUBLANES, CONV_WIDTH), F32)],
        compiler_params=_params(("arbitrary",)),
    )(x, g1, w_in_t, conv_w, gq, gk, gconv)


GROUP_ROWS = GQA_GROUP * BLK


def _band_bias(tbl_ref, bkt, bias_ref):
    for h in range(N_HEADS):
        acc = jnp.zeros(bkt.shape, F32)
        for b in range(NUM_BUCKETS):
            acc = jnp.where(bkt == b, tbl_ref[h, b], acc)
        bias_ref[h // GQA_GROUP, BLK * (h % GQA_GROUP):BLK * (h % GQA_GROUP + 1), :] = acc


def _band_masks(i):
    qi = lax.broadcasted_iota(jnp.int32, (GROUP_ROWS, BLK), 0) & (BLK - 1)
    ji = lax.broadcasted_iota(jnp.int32, (GROUP_ROWS, BLK), 1)
    upper = ji > qi
    return upper, upper & (i == 0)


def _stack_heads(x, g):
    return jnp.concatenate([x[:, HEAD_DIM * h:HEAD_DIM * (h + 1)] for h in range(GQA_GROUP * g, GQA_GROUP * (g + 1))], axis=0)


def _unstack_heads(groups):
    return jnp.concatenate([p[BLK * t:BLK * (t + 1)] for p in groups for t in range(GQA_GROUP)], axis=-1)


def _per_head_rows(vals):
    row = lax.broadcasted_iota(jnp.int32, (GROUP_ROWS, 1), 0)
    col = jnp.full((GROUP_ROWS, 1), vals[GQA_GROUP - 1], F32)
    for t in range(GQA_GROUP - 2, -1, -1):
        col = jnp.where(row < BLK * (t + 1), vals[t], col)
    return col


def _band_rows(ref, i):
    prev = pl.multiple_of(jnp.maximum(i - 1, 0) * BLK, BLK)
    cur = pl.multiple_of(i * BLK, BLK)
    return jnp.concatenate([ref[pl.ds(prev, BLK), :], ref[pl.ds(cur, BLK), :]], axis=0), prev, cur


def _fold(band, upper):
    return jnp.where(upper, band[:, :BLK], band[:, BLK:])


def _unfold(tile, upper):
    return jnp.concatenate([jnp.where(upper, tile, 0.0), jnp.where(upper, 0.0, tile)], axis=1)


def _head_probs(qh, kh, bias, upper, dead, sink):
    logits = _fold(_dot_nt(qh, kh), upper) * (HEAD_DIM ** -0.5) + bias
    logits = jnp.where(dead, NEG_INF, logits)
    m = jnp.maximum(jnp.max(logits, axis=-1, keepdims=True), sink)
    p = jnp.exp(logits - m)
    es = jnp.exp(sink - m)
    den = jnp.sum(p, axis=-1, keepdims=True) + es
    return p / den, es / den


def _attn_fwd(qn, kn, v, tbl, sinks, bkt, gattn):
    n_b = SEQ // BLK

    def body(q_ref, k_ref, v_ref, tbl_ref, sink_ref, bkt_ref, ga_ref, y_ref, yn_ref, bias_ref):
        i = pl.program_id(0)

        @pl.when(i == 0)
        def _():
            _band_bias(tbl_ref, bkt_ref[...], bias_ref)

        kb, _, _ = _band_rows(k_ref, i)
        vb, _, _ = _band_rows(v_ref, i)
        upper, dead = _band_masks(i)
        q = q_ref[...]
        outs = []
        for g in range(N_HEADS // GQA_GROUP):
            kv = slice(HEAD_DIM * g, HEAD_DIM * (g + 1))
            sink = _per_head_rows([sink_ref[0, GQA_GROUP * g + t] for t in range(GQA_GROUP)])
            probs, _ = _head_probs(_stack_heads(q, g), kb[:, kv], bias_ref[g], upper, dead, sink)
            outs.append(_dot(_unfold(probs, upper).astype(BF16), vb[:, kv]))
        y = _unstack_heads(outs)
        y_ref[...] = y
        yn_ref[...] = (y * _rstd(y) * ga_ref[...]).astype(BF16)

    const = lambda shape: pl.BlockSpec(shape, lambda i: (0,) * len(shape))
    rows = lambda w: pl.BlockSpec((BLK, w), lambda i: (i, 0))
    smem = pl.BlockSpec(memory_space=pltpu.SMEM)
    return pl.pallas_call(
        body, name="attn_fwd", grid=(n_b,),
        in_specs=[rows(ATTN_WIDTH), const((SEQ, KV_WIDTH)), const((SEQ, KV_WIDTH)), smem, smem,
                  const((BLK, BLK)), const((1, ATTN_WIDTH))],
        out_specs=[rows(ATTN_WIDTH), rows(ATTN_WIDTH)],
        out_shape=[jax.ShapeDtypeStruct((SEQ, ATTN_WIDTH), F32), jax.ShapeDtypeStruct((SEQ, ATTN_WIDTH), BF16)],
        scratch_shapes=[pltpu.VMEM((N_HEADS // GQA_GROUP, GROUP_ROWS, BLK), F32)],
        compiler_params=_params(("arbitrary",)),
    )(qn, kn, v, tbl, sinks, bkt, gattn)


def _ffn_fwd(x, ycn, yan, w_out, g2, w_up, fcw, fcb, w_down, tgt):
    tm = 512
    n_t = SEQ // tm

    def body(x_ref, ycn_ref, yan_ref, wo_ref, g2_ref, wu_ref, cw_ref, b_ref, wd_ref, tgt_ref,
             h1_ref, u2_ref, up_ref, dh2_ref, dh2b_ref, loss_ref, acc_ref, halo_ref):
        i, j = pl.program_id(0), pl.program_id(1)

        @pl.when((i == 0) & (j == 0))
        def _():
            loss_ref[...] = jnp.zeros_like(loss_ref)

        @pl.when(j == 0)
        def _():
            h1 = x_ref[...] + _dot(ycn_ref[...], wo_ref[0:CONV_WIDTH, :]) + _dot(yan_ref[...], wo_ref[CONV_WIDTH:, :])
            h1_ref[...] = h1
            u2_ref[...] = (h1 * _rstd(h1) * g2_ref[...]).astype(BF16)
            acc_ref[...] = jnp.zeros_like(acc_ref)

        u2 = u2_ref[...]
        pre = []
        for s in range(2):
            up = _dot_nt(u2, wu_ref[s])
            up_ref[s] = up
            halo = jnp.where(i == 0, 0.0, halo_ref[s, j])
            pre.append(_conv3(up, _taps(cw_ref.at[s]), halo)[0] + b_ref[s])
            halo_ref[s, j] = up[tm - SUBLANES:]
        g, val = pre
        act = (g * jax.nn.sigmoid(g) * val).astype(BF16)
        acc_ref[...] += _dot(act, wd_ref[...])

        @pl.when(j == N_FFN_BLK - 1)
        def _():
            err = h1_ref[...] + acc_ref[...] - tgt_ref[...]
            loss_ref[...] += 0.5 * jnp.sum(err * err) / D_MODEL
            dh2 = err / D_MODEL
            dh2_ref[...] = dh2
            dh2b_ref[...] = dh2.astype(BF16)

    rows = lambda w: pl.BlockSpec((tm, w), lambda i, j: (i, 0))
    const = lambda shape: pl.BlockSpec(shape, lambda i, j: (0,) * len(shape))
    pair = lambda *s: pl.BlockSpec((2, None) + s, lambda i, j: (0, j) + (0,) * len(s))
    upb = pl.BlockSpec((2, None, tm, FFN_BLK), lambda i, j: (0, j, i, 0))
    return pl.pallas_call(
        body, name="ffn_fwd", grid=(n_t, N_FFN_BLK),
        in_specs=[rows(D_MODEL), rows(CONV_WIDTH), rows(ATTN_WIDTH), const((D_MODEL, D_MODEL)), const((1, D_MODEL)),
                  pair(FFN_BLK, D_MODEL), pair(3, 1, FFN_BLK), pair(1, FFN_BLK),
                  pl.BlockSpec((None, FFN_BLK, D_MODEL), lambda i, j: (j, 0, 0)), rows(D_MODEL)],
        out_specs=[rows(D_MODEL), rows(D_MODEL), upb, rows(D_MODEL), rows(D_MODEL), const((SUBLANES, 128))],
        out_shape=[jax.ShapeDtypeStruct((SEQ, D_MODEL), F32), jax.ShapeDtypeStruct((SEQ, D_MODEL), BF16),
                   jax.ShapeDtypeStruct((2, N_FFN_BLK, SEQ, FFN_BLK), F32),
                   jax.ShapeDtypeStruct((SEQ, D_MODEL), F32), jax.ShapeDtypeStruct((SEQ, D_MODEL), BF16),
                   jax.ShapeDtypeStruct((SUBLANES, 128), F32)],
        scratch_shapes=[pltpu.VMEM((tm, D_MODEL), F32), pltpu.VMEM((2, N_FFN_BLK, SUBLANES, FFN_BLK), F32)],
        compiler_params=_params(("arbitrary", "arbitrary")),
    )(x, ycn, yan, w_out, g2, w_up, fcw, fcb, w_down, tgt)


def _ffn_bwd(dh2, dh2b, h1, g2, up, w_up, fcw, fcb, w_down):
    tm = 512
    n_t = SEQ // tm
    halo_blocks = tm // SUBLANES

    def body(dh2_ref, dh2b_ref, h1_ref, g2_ref, up_ref, uph_ref, wu_ref, cw_ref, b_ref, wd_ref,
             act_ref, dup_ref, dh1_ref, dh1b_ref, dfb_ref, dfcw_ref, dg2_ref, acc_ref, next_ref):
        i, j = pl.program_id(0), pl.program_id(1)
        first_tile = i == n_t - 1

        @pl.when((i == 0) & (j == 0))
        def _():
            dfb_ref[...] = jnp.zeros_like(dfb_ref)
            dfcw_ref[...] = jnp.zeros_like(dfcw_ref)
            dg2_ref[...] = jnp.zeros_like(dg2_ref)

        @pl.when(j == 0)
        def _():
            acc_ref[...] = jnp.zeros_like(acc_ref)

        ups, pre = [], []
        for s in range(2):
            up = up_ref[s]
            halo = jnp.where(first_tile, 0.0, uph_ref[s])
            p, up2, up1 = _conv3(up, _taps(cw_ref.at[s]), halo)
            pre.append(p + b_ref[s])
            ups.append((up, up1, up2))
        g, val = pre
        sg = jax.nn.sigmoid(g)
        silu = g * sg
        act_ref[...] = (silu * val).astype(BF16)
        dact = _dot_nt(dh2b_ref[...], wd_ref[...])
        dpre = (dact * val * (sg * (1.0 + g * (1.0 - sg))), dact * silu)
        for s in range(2):
            d = dpre[s]
            u, u1, u2 = ups[s]
            dfb_ref[s, j] += jnp.sum(d, axis=0, keepdims=True)
            dfcw_ref[s, j, 0] += jnp.sum(d * u2, axis=0, keepdims=True)
            dfcw_ref[s, j, 1] += jnp.sum(d * u1, axis=0, keepdims=True)
            dfcw_ref[s, j, 2] += jnp.sum(d * u, axis=0, keepdims=True)
            nxt = jnp.where(i == 0, 0.0, next_ref[s, j])
            dup = _conv3_bwd_input(d, _taps(cw_ref.at[s]), nxt).astype(BF16)
            next_ref[s, j] = d[:SUBLANES]
            dup_ref[s] = dup
            acc_ref[...] += _dot(dup, wu_ref[s])

        @pl.when(j == N_FFN_BLK - 1)
        def _():
            dn, dgain = _rms_bwd(h1_ref[...], g2_ref[...], acc_ref[...])
            dh1 = dh2_ref[...] + dn
            dh1_ref[...] = dh1
            dh1b_ref[...] = dh1.astype(BF16)
            dg2_ref[...] += dgain

    rev = lambda i: n_t - 1 - i
    rows = lambda w: pl.BlockSpec((tm, w), lambda i, j: (rev(i), 0))
    const = lambda shape: pl.BlockSpec(shape, lambda i, j: (0,) * len(shape))
    pair = lambda *s: pl.BlockSpec((2, None) + s, lambda i, j: (0, j) + (0,) * len(s))
    upb = pl.BlockSpec((2, None, tm, FFN_BLK), lambda i, j: (0, j, rev(i), 0))
    halo = pl.BlockSpec((2, None, SUBLANES, FFN_BLK), lambda i, j: (0, j, jnp.maximum(rev(i) * halo_blocks - 1, 0), 0))
    return pl.pallas_call(
        body, name="ffn_bwd", grid=(n_t, N_FFN_BLK),
        in_specs=[rows(D_MODEL), rows(D_MODEL), rows(D_MODEL), const((1, D_MODEL)), upb, halo,
                  pair(FFN_BLK, D_MODEL), pair(3, 1, FFN_BLK), pair(1, FFN_BLK),
                  pl.BlockSpec((None, FFN_BLK, D_MODEL), lambda i, j: (j, 0, 0))],
        out_specs=[pl.BlockSpec((None, tm, FFN_BLK), lambda i, j: (j, rev(i), 0)), upb, rows(D_MODEL), rows(D_MODEL),
                   const((2, N_FFN_BLK, 1, FFN_BLK)), const((2, N_FFN_BLK, 3, 1, FFN_BLK)), const((1, D_MODEL))],
        out_shape=[jax.ShapeDtypeStruct((N_FFN_BLK, SEQ, FFN_BLK), BF16),
                   jax.ShapeDtypeStruct((2, N_FFN_BLK, SEQ, FFN_BLK), BF16), jax.ShapeDtypeStruct((SEQ, D_MODEL), F32),
                   jax.ShapeDtypeStruct((SEQ, D_MODEL), BF16), jax.ShapeDtypeStruct((2, N_FFN_BLK, 1, FFN_BLK), F32),
                   jax.ShapeDtypeStruct((2, N_FFN_BLK, 3, 1, FFN_BLK), F32), jax.ShapeDtypeStruct((1, D_MODEL), F32)],
        scratch_shapes=[pltpu.VMEM((tm, D_MODEL), F32), pltpu.VMEM((2, N_FFN_BLK, SUBLANES, FFN_BLK), F32)],
        compiler_params=_params(("arbitrary", "arbitrary")),
    )(dh2, dh2b, h1, g2, up, up, w_up, fcw, fcb, w_down)


def _grad_tn(a_list, b, out_rows, name, after=None):
    n = len(a_list)
    ncol = b.shape[1]

    def body(*refs):
        a_refs, b_ref, o_ref = refs[:n], refs[n], refs[n + 1]
        j = pl.program_id(0)
        for k in range(n):
            @pl.when(j == k)
            def _(k=k):
                o_ref[...] = _dot_tn(a_refs[k][...], b_ref[...]).astype(BF16)

    full = lambda shape: pl.BlockSpec(shape, lambda j: (0,) * len(shape))
    body, more_specs, more = _ordered_behind(body, n + 1, after)
    return pl.pallas_call(
        body, name=name, grid=(n,),
        in_specs=[full((SEQ, out_rows))] * n + [full((SEQ, ncol))] + more_specs,
        out_specs=pl.BlockSpec((None, out_rows, ncol), lambda j: (j, 0, 0)),
        out_shape=jax.ShapeDtypeStruct((n, out_rows, ncol), BF16),
        compiler_params=_params(("arbitrary",)),
    )(*a_list, b, *more)


def _grad_tn_blocked(a, b, name, a_is_blocked):
    nb = a.shape[0] if a_is_blocked else b.shape[0]
    a_w, b_w = a.shape[-1], b.shape[-1]

    def body(a_ref, b_ref, o_ref):
        o_ref[...] = _dot_tn(a_ref[...], b_ref[...]).astype(BF16)

    blocked = lambda w: pl.BlockSpec((None, SEQ, w), lambda k: (k, 0, 0))
    full = lambda w: pl.BlockSpec((SEQ, w), lambda k: (0, 0))
    return pl.pallas_call(
        body, name=name, grid=(nb,),
        in_specs=[blocked(a_w) if a_is_blocked else full(a_w), full(b_w) if a_is_blocked else blocked(b_w)],
        out_specs=pl.BlockSpec((None, a_w, b_w), lambda k: (k, 0, 0)),
        out_shape=jax.ShapeDtypeStruct((nb, a_w, b_w), BF16),
        compiler_params=_params(("arbitrary",)),
    )(a, b)


def _out_bwd(dh1b, w_out, y_attn, gattn, after=None):
    tm = 512
    n_t = SEQ // tm

    def body(dh_ref, wo_ref, y_ref, ga_ref, dycn_ref, dy_ref, dga_ref):
        @pl.when(pl.program_id(0) == 0)
        def _():
            dga_ref[...] = jnp.zeros_like(dga_ref)

        dycat = _dot_nt(dh_ref[...], wo_ref[...])
        dycn_ref[...] = dycat[:, :CONV_WIDTH]
        dy, dga = _rms_bwd(y_ref[...], ga_ref[...], dycat[:, CONV_WIDTH:])
        dy_ref[...] = dy
        dga_ref[...] += dga

    rows = lambda w: pl.BlockSpec((tm, w), lambda i: (i, 0))
    const = lambda shape: pl.BlockSpec(shape, lambda i: (0,) * len(shape))
    body, more_specs, more = _ordered_behind(body, 4, after)
    return pl.pallas_call(
        body, name="out_bwd", grid=(n_t,),
        in_specs=[rows(D_MODEL), const((D_MODEL, D_MODEL)), rows(ATTN_WIDTH), const((1, ATTN_WIDTH))] + more_specs,
        out_specs=[rows(CONV_WIDTH), rows(ATTN_WIDTH), const((1, ATTN_WIDTH))],
        out_shape=[jax.ShapeDtypeStruct((SEQ, CONV_WIDTH), F32), jax.ShapeDtypeStruct((SEQ, ATTN_WIDTH), F32),
                   jax.ShapeDtypeStruct((1, ATTN_WIDTH), F32)],
        compiler_params=_params(("arbitrary",)),
    )(dh1b, w_out, y_attn, gattn, *more)


def _attn_bwd(qn, kn, v, dy, tbl, sinks, bkt, after=None):
    n_b = SEQ // BLK

    def body(q_ref, k_ref, v_ref, dy_ref, tbl_ref, sink_ref, bkt_ref,
             dq_ref, dk_ref, dv_ref, dtbl_ref, dsink_ref, bias_ref, dbias_ref, dsacc_ref):
        i = pl.program_id(0)

        @pl.when(i == 0)
        def _():
            _band_bias(tbl_ref, bkt_ref[...], bias_ref)
            dbias_ref[...] = jnp.zeros_like(dbias_ref)
            dsacc_ref[...] = jnp.zeros_like(dsacc_ref)
            dk_ref[...] = jnp.zeros_like(dk_ref)
            dv_ref[...] = jnp.zeros_like(dv_ref)

        kb, prev, cur = _band_rows(k_ref, i)
        vb, _, _ = _band_rows(v_ref, i)
        upper, dead = _band_masks(i)
        q = q_ref[...]
        dy = dy_ref[...]
        dqs, dks, dvs = [], [], []
        for g in range(N_HEADS // GQA_GROUP):
            kv = slice(HEAD_DIM * g, HEAD_DIM * (g + 1))
            qg = _stack_heads(q, g)
            dog = _stack_heads(dy, g).astype(BF16)
            sink = _per_head_rows([sink_ref[0, GQA_GROUP * g + t] for t in range(GQA_GROUP)])
            probs, psink = _head_probs(qg, kb[:, kv], bias_ref[g], upper, dead, sink)
            dprobs = _fold(_dot_nt(dog, vb[:, kv]), upper)
            dvs.append(_dot_tn(_unfold(probs, upper).astype(BF16), dog))
            dsum = jnp.sum(probs * dprobs, axis=-1, keepdims=True)
            dlogits = probs * (dprobs - dsum)
            dsacc_ref[g] += jnp.broadcast_to(-psink * dsum, (GROUP_ROWS, 128))
            dbias_ref[g] += dlogits
            ds = _unfold(dlogits * (HEAD_DIM ** -0.5), upper).astype(BF16)
            dqs.append(_dot(ds, kb[:, kv]))
            dks.append(_dot_tn(ds, qg))
        dq_ref[...] = _unstack_heads(dqs)
        dkb = jnp.concatenate(dks, axis=-1)
        dvb = jnp.concatenate(dvs, axis=-1)
        dk_ref[pl.ds(prev, BLK), :] += dkb[:BLK]
        dk_ref[pl.ds(cur, BLK), :] += dkb[BLK:]
        dv_ref[pl.ds(prev, BLK), :] += dvb[:BLK]
        dv_ref[pl.ds(cur, BLK), :] += dvb[BLK:]

        @pl.when(i == n_b - 1)
        def _():
            bkt = bkt_ref[...]
            row8 = lax.broadcasted_iota(jnp.int32, (N_HEADS, 128), 0)
            lane8 = lax.broadcasted_iota(jnp.int32, (N_HEADS, 128), 1)
            lane1 = lax.broadcasted_iota(jnp.int32, (1, 128), 1)
            acc = jnp.zeros((N_HEADS, 128), F32)
            dsink = jnp.zeros((1, 128), F32)
            for h in range(N_HEADS):
                rows = slice(BLK * (h % GQA_GROUP), BLK * (h % GQA_GROUP + 1))
                dsink = jnp.where(lane1 == h, jnp.sum(dsacc_ref[h // GQA_GROUP, rows, :], axis=0, keepdims=True), dsink)
                dbh = dbias_ref[h // GQA_GROUP, rows, :]
                for b in range(NUM_BUCKETS):
                    acc = jnp.where((row8 == h) & (lane8 == b), jnp.sum(jnp.where(bkt == b, dbh, 0.0)), acc)
            dsink_ref[...] = dsink
            dtbl_ref[...] = acc

    const = lambda shape: pl.BlockSpec(shape, lambda i: (0,) * len(shape))
    rows = lambda w: pl.BlockSpec((BLK, w), lambda i: (i, 0))
    smem = pl.BlockSpec(memory_space=pltpu.SMEM)
    body, more_specs, more = _ordered_behind(body, 7, after)
    return pl.pallas_call(
        body, name="attn_bwd", grid=(n_b,),
        in_specs=[rows(ATTN_WIDTH), const((SEQ, KV_WIDTH)), const((SEQ, KV_WIDTH)), rows(ATTN_WIDTH), smem, smem,
                  const((BLK, BLK))] + more_specs,
        out_specs=[rows(ATTN_WIDTH), const((SEQ, KV_WIDTH)), const((SEQ, KV_WIDTH)), const((N_HEADS, 128)), const((1, 128))],
        out_shape=[jax.ShapeDtypeStruct((SEQ, ATTN_WIDTH), F32), jax.ShapeDtypeStruct((SEQ, KV_WIDTH), F32),
                   jax.ShapeDtypeStruct((SEQ, KV_WIDTH), F32), jax.ShapeDtypeStruct((N_HEADS, 128), F32),
                   jax.ShapeDtypeStruct((1, 128), F32)],
        scratch_shapes=[pltpu.VMEM((N_HEADS // GQA_GROUP, GROUP_ROWS, BLK), F32)] * 3,
        compiler_params=_params(("arbitrary",)),
    )(qn, kn, v, dy, tbl, sinks, bkt, *more)


def _mix_in_bwd(x, dh1, proj, dycn, dqn, dkn, dv, w_in_t, conv_w, g1, gq, gk, gconv):
    tm = 256
    n_t = SEQ // tm
    halo_blocks = tm // SUBLANES

    def body(x_ref, dh1_ref, proj_ref, halo_ref, dycn_ref, dqn_ref, dkn_ref, dv_ref, w_ref, cw_ref,
             g1_ref, gq_ref, gk_ref, gc_ref,
             dx_ref, dproj_ref, u1_ref, dcw_ref, dgc_ref, dgq_ref, dgk_ref, dg1_ref, next_ref):
        i = pl.program_id(0)
        first_tile = i == n_t - 1

        @pl.when(i == 0)
        def _():
            for r in (dcw_ref, dgc_ref, dgq_ref, dgk_ref, dg1_ref, next_ref):
                r[...] = jnp.zeros_like(r)

        proj = proj_ref[...]
        hp = halo_ref[...]
        gate_b = proj[:, 0:CONV_WIDTH]
        gate_c = proj[:, CONV_WIDTH:2 * CONV_WIDTH]
        hc = proj[:, 2 * CONV_WIDTH:3 * CONV_WIDTH]
        a = gate_c * hc
        a_halo = jnp.where(first_tile, 0.0, hp[:, CONV_WIDTH:2 * CONV_WIDTH] * hp[:, 2 * CONV_WIDTH:3 * CONV_WIDTH])
        cw = _taps(cw_ref[...])
        cv, a2, a1 = _conv3(a, cw, a_halo)
        dyc, dgc = _rms_bwd(gate_b * cv, gc_ref[...], dycn_ref[...])
        dgc_ref[...] += dgc
        dcv = dyc * gate_b
        dcw_ref[...] += jnp.concatenate(
            [jnp.sum(dcv * a2, axis=0, keepdims=True), jnp.sum(dcv * a1, axis=0, keepdims=True),
             jnp.sum(dcv * a, axis=0, keepdims=True)], axis=0)
        da = _conv3_bwd_input(dcv, cw, next_ref[...])
        next_ref[...] = dcv[:SUBLANES]
        q0 = 3 * CONV_WIDTH
        k0 = q0 + ATTN_WIDTH
        dq, dgq = _head_norm_bwd(proj[:, q0:k0], gq_ref[...], dqn_ref[...], N_HEADS)
        dk, dgk = _head_norm_bwd(proj[:, k0:k0 + KV_WIDTH], gk_ref[...], dkn_ref[...], 2)
        dgq_ref[...] += dgq
        dgk_ref[...] += dgk
        dproj = jnp.concatenate([dyc * cv, da * hc, da * gate_c, dq, dk, dv_ref[...]], axis=-1).astype(BF16)
        dproj_ref[...] = dproj
        du1 = _dot(dproj, w_ref[...])
        xv = x_ref[...]
        dn, dg1 = _rms_bwd(xv, g1_ref[...], du1)
        dx_ref[...] = dh1_ref[...] + dn
        dg1_ref[...] += dg1
        u1_ref[...] = (xv * _rstd(xv) * g1_ref[...]).astype(BF16)

    rev = lambda i: n_t - 1 - i
    rows = lambda w: pl.BlockSpec((tm, w), lambda i: (rev(i), 0))
    const = lambda shape: pl.BlockSpec(shape, lambda i: (0,) * len(shape))
    halo = pl.BlockSpec((SUBLANES, IN_WIDTH), lambda i: (jnp.maximum(rev(i) * halo_blocks - 1, 0), 0))
    return pl.pallas_call(
        body, name="mix_in_bwd", grid=(n_t,),
        in_specs=[rows(D_MODEL), rows(D_MODEL), rows(IN_WIDTH), halo, rows(CONV_WIDTH), rows(ATTN_WIDTH), rows(KV_WIDTH),
                  rows(KV_WIDTH), const((IN_WIDTH, D_MODEL)), const((3, CONV_WIDTH)), const((1, D_MODEL)),
                  const((1, HEAD_DIM)), const((1, HEAD_DIM)), const((1, CONV_WIDTH))],
        out_specs=[rows(D_MODEL), rows(IN_WIDTH), rows(D_MODEL), const((3, CONV_WIDTH)), const((1, CONV_WIDTH)),
                   const((1, HEAD_DIM)), const((1, HEAD_DIM)), const((1, D_MODEL))],
        out_shape=[jax.ShapeDtypeStruct((SEQ, D_MODEL), F32), jax.ShapeDtypeStruct((SEQ, IN_WIDTH), BF16),
                   jax.ShapeDtypeStruct((SEQ, D_MODEL), BF16), jax.ShapeDtypeStruct((3, CONV_WIDTH), F32),
                   jax.ShapeDtypeStruct((1, CONV_WIDTH), F32), jax.ShapeDtypeStruct((1, HEAD_DIM), F32),
                   jax.ShapeDtypeStruct((1, HEAD_DIM), F32), jax.ShapeDtypeStruct((1, D_MODEL), F32)],
        scratch_shapes=[pltpu.VMEM((SUBLANES, CONV_WIDTH), F32)],
        compiler_params=_params(("arbitrary",)),
    )(x, dh1, proj, proj, dycn, dqn, dkn, dv, w_in_t, conv_w, g1, gq, gk, gconv)


def _grad_w_in(dproj, u1, after=None):
    bw = 768

    def body(a_ref, b_ref, o_ref):
        o_ref[...] = _dot_tn(a_ref[...], b_ref[...]).astype(BF16)

    body, more_specs, more = _ordered_behind(body, 2, after)
    return pl.pallas_call(
        body, name="grad_w_in", grid=(IN_WIDTH // bw,),
        in_specs=[pl.BlockSpec((SEQ, bw), lambda k: (0, k)), pl.BlockSpec((SEQ, D_MODEL), lambda k: (0, 0))] + more_specs,
        out_specs=pl.BlockSpec((bw, D_MODEL), lambda k: (k, 0)),
        out_shape=jax.ShapeDtypeStruct((IN_WIDTH, D_MODEL), BF16),
        compiler_params=_params(("arbitrary",)),
    )(dproj, u1, *more)


def _adamw_math(w, g, m, v):
    m = ADAM_B1 * m + (1.0 - ADAM_B1) * g
    v = ADAM_B2 * v + (1.0 - ADAM_B2) * (g * g)
    m_hat = m / (1.0 - ADAM_B1 ** ADAM_STEP)
    v_hat = v / (1.0 - ADAM_B2 ** ADAM_STEP)
    return -ADAM_LR * (m_hat / (jnp.sqrt(v_hat) + ADAM_EPS) + ADAM_WD * w), m, v


_ROW_G1, _ROW_G2, _ROW_OUT_NORMS, _ROW_FFN_B, _ROW_GQ, _ROW_GK, _ROW_SINKS, _ROW_LOSS, _ROW_TABLE = 0, 1, 2, 3, 11, 12, 13, 14, 16
SMALL_ROWS, SMALL_COLS = 24, 1024
_SMALL_NAMES = ("norm_mix_g", "norm_ffn_g", "out_norm_conv_g", "out_norm_attn_g", "ffn_conv_b", "q_norm_g", "k_norm_g",
                "sinks", "rel_bias_table")


def _pack_small_grads(dg1, dg2, dgconv, dgattn, dfb, dgq, dgk, dsinks, dtbl_t, loss_acc):
    def body(dg1_ref, dg2_ref, dgc_ref, dga_ref, dfb_ref, dgq_ref, dgk_ref, ds_ref, dt_ref, loss_ref, o_ref):
        o_ref[...] = jnp.zeros_like(o_ref)
        o_ref[_ROW_G1:_ROW_G1 + 1, :] = dg1_ref[...]
        o_ref[_ROW_G2:_ROW_G2 + 1, :] = dg2_ref[...]
        o_ref[_ROW_OUT_NORMS:_ROW_OUT_NORMS + 1, 0:CONV_WIDTH] = dgc_ref[...]
        o_ref[_ROW_OUT_NORMS:_ROW_OUT_NORMS + 1, CONV_WIDTH:] = dga_ref[...]
        for k in range(N_DEV):
            o_ref[_ROW_FFN_B + k:_ROW_FFN_B + k + 1, 0:FFN_BLK] = dfb_ref[k // N_FFN_BLK, k % N_FFN_BLK]
        o_ref[_ROW_GQ:_ROW_GQ + 1, 0:HEAD_DIM] = dgq_ref[...]
        o_ref[_ROW_GK:_ROW_GK + 1, 0:HEAD_DIM] = dgk_ref[...]
        o_ref[_ROW_SINKS:_ROW_SINKS + 1, 0:128] = ds_ref[...]
        o_ref[_ROW_LOSS:_ROW_LOSS + 1, 0:128] = loss_ref[0:1, :]
        o_ref[_ROW_TABLE:_ROW_TABLE + N_HEADS, 0:128] = dt_ref[...]

    return pl.pallas_call(body, name="pack_small_grads", out_shape=jax.ShapeDtypeStruct((SMALL_ROWS, SMALL_COLS), F32))(
        dg1, dg2, dgconv, dgattn, dfb, dgq, dgk, dsinks, dtbl_t, loss_acc)


def _adamw_small(recv, params, after):
    names = _SMALL_NAMES
    n = len(names)

    def grad_of(g, name, k=None):
        if name == "norm_mix_g":
            return g[_ROW_G1:_ROW_G1 + 1, :]
        if name == "norm_ffn_g":
            return g[_ROW_G2:_ROW_G2 + 1, :]
        if name == "out_norm_conv_g":
            return g[_ROW_OUT_NORMS:_ROW_OUT_NORMS + 1, 0:CONV_WIDTH]
        if name == "out_norm_attn_g":
            return g[_ROW_OUT_NORMS:_ROW_OUT_NORMS + 1, CONV_WIDTH:]
        if name == "ffn_conv_b":
            return g[_ROW_FFN_B + k:_ROW_FFN_B + k + 1, 0:FFN_BLK]
        if name == "q_norm_g":
            return g[_ROW_GQ:_ROW_GQ + 1, 0:HEAD_DIM]
        if name == "k_norm_g":
            return g[_ROW_GK:_ROW_GK + 1, 0:HEAD_DIM]
        if name == "sinks":
            return g[_ROW_SINKS:_ROW_SINKS + 1, 0:N_HEADS]
        return g[_ROW_TABLE:_ROW_TABLE + N_HEADS, 0:NUM_BUCKETS]

    def body(r_ref, *refs):
        ins, outs, loss_ref = refs[:3 * n], refs[3 * n:7 * n], refs[7 * n]
        g = r_ref[0]
        for s in range(1, N_DEV):
            g = g + r_ref[s]
        loss_ref[...] = g[_ROW_LOSS:_ROW_LOSS + 1, 0:128]
        for i, name in enumerate(names):
            w_ref, m_ref, v_ref = ins[3 * i:3 * i + 3]
            o = outs[4 * i:4 * i + 4]
            cols = [slice(FFN_BLK * k, FFN_BLK * (k + 1)) for k in range(N_DEV)] if name == "ffn_conv_b" else [slice(None)]
            for k, cs in enumerate(cols):
                gk = grad_of(g, name, k)
                d, m2, v2 = _adamw_math(w_ref[:, cs], gk, m_ref[:, cs], v_ref[:, cs])
                o[0][:, cs], o[1][:, cs], o[2][:, cs], o[3][:, cs] = gk, d, m2, v2

    flat = [a for name in names for a in params[name]]
    body, more_specs, more = _ordered_behind(body, 1 + 3 * n, after)
    vmem = pl.BlockSpec(memory_space=pltpu.VMEM)
    out = pl.pallas_call(
        body, name="adamw_small",
        in_specs=[vmem] * (1 + 3 * n) + more_specs,
        out_shape=[jax.ShapeDtypeStruct(params[name][0].shape, F32) for name in names for _ in range(4)]
        + [jax.ShapeDtypeStruct((1, 128), F32)],
        compiler_params=pltpu.CompilerParams(vmem_limit_bytes=VMEM_LIMIT),
    )(recv, *flat, *more)
    return {name: tuple(out[4 * i:4 * i + 4]) for i, name in enumerate(names)}, out[4 * n]


def _adamw(w, m, v, part, recv, chip, name, row_blocks=1, after=None):
    rb = w.shape[0] // row_blocks
    tail = w.shape[1:]
    zeros = (0,) * len(tail)

    def body(chip_ref, w_ref, m_ref, v_ref, p_ref, r_ref, g_o, d_o, m_o, v_o):
        g = p_ref[...].astype(F32)
        for s in range(3):
            g = g + r_ref[s].astype(F32)
        g_o[...] = g
        d_o[...], m_o[...], v_o[...] = _adamw_math(w_ref[...], g, m_ref[...], v_ref[...])

    blk = pl.BlockSpec((rb,) + tail, lambda i, chip_ref: (i,) + zeros)
    pblk = pl.BlockSpec((None, rb) + tail, lambda i, chip_ref: (chip_ref[0], i) + zeros)
    rblk = pl.BlockSpec((3, rb) + tail, lambda i, chip_ref: (0, i) + zeros)
    body, more_specs, more = _ordered_behind(body, 6, after)
    return pl.pallas_call(
        body, name=name,
        grid_spec=pltpu.PrefetchScalarGridSpec(num_scalar_prefetch=1, grid=(row_blocks,),
                                               in_specs=[blk, blk, blk, pblk, rblk] + more_specs, out_specs=[blk] * 4),
        out_shape=[jax.ShapeDtypeStruct(w.shape, F32)] * 4,
        compiler_params=_params(("arbitrary",)),
    )(chip, w, m, v, part, recv, *more)


def kernel(x, norm_mix_g, w_in, conv_w, q_norm_g, k_norm_g, rel_bias_table, sinks, out_norm_conv_g, out_norm_attn_g, w_out, norm_ffn_g, w_up, ffn_conv_w, ffn_conv_b, w_down, loss_target, m_norm_mix_g, m_w_in, m_conv_w, m_q_norm_g, m_k_norm_g, m_rel_bias_table, m_sinks, m_out_norm_conv_g, m_out_norm_attn_g, m_w_out, m_norm_ffn_g, m_w_up, m_ffn_conv_w, m_ffn_conv_b, m_w_down, v_norm_mix_g, v_w_in, v_conv_w, v_q_norm_g, v_k_norm_g, v_rel_bias_table, v_sinks, v_out_norm_conv_g, v_out_norm_attn_g, v_w_out, v_norm_ffn_g, v_w_up, v_ffn_conv_w, v_ffn_conv_b, v_w_down):
    p = dict(norm_mix_g=norm_mix_g, w_in=w_in, conv_w=conv_w, q_norm_g=q_norm_g, k_norm_g=k_norm_g,
             rel_bias_table=rel_bias_table, sinks=sinks, out_norm_conv_g=out_norm_conv_g, out_norm_attn_g=out_norm_attn_g,
             w_out=w_out, norm_ffn_g=norm_ffn_g, w_up=w_up, ffn_conv_w=ffn_conv_w, ffn_conv_b=ffn_conv_b, w_down=w_down)
    m = dict(norm_mix_g=m_norm_mix_g, w_in=m_w_in, conv_w=m_conv_w, q_norm_g=m_q_norm_g, k_norm_g=m_k_norm_g,
             rel_bias_table=m_rel_bias_table, sinks=m_sinks, out_norm_conv_g=m_out_norm_conv_g,
             out_norm_attn_g=m_out_norm_attn_g, w_out=m_w_out, norm_ffn_g=m_norm_ffn_g, w_up=m_w_up,
             ffn_conv_w=m_ffn_conv_w, ffn_conv_b=m_ffn_conv_b, w_down=m_w_down)
    v = dict(norm_mix_g=v_norm_mix_g, w_in=v_w_in, conv_w=v_conv_w, q_norm_g=v_q_norm_g, k_norm_g=v_k_norm_g,
             rel_bias_table=v_rel_bias_table, sinks=v_sinks, out_norm_conv_g=v_out_norm_conv_g,
             out_norm_attn_g=v_out_norm_attn_g, w_out=v_w_out, norm_ffn_g=v_norm_ffn_g, w_up=v_w_up,
             ffn_conv_w=v_ffn_conv_w, ffn_conv_b=v_ffn_conv_b, w_down=v_w_down)

    xs, tgt = x[0], loss_target[0]
    g1, g2, gq, gk, gconv, gattn = norm_mix_g, norm_ffn_g, q_norm_g, k_norm_g, out_norm_conv_g, out_norm_attn_g
    ix, iy, ic = _coords()
    core = ic.astype(jnp.int32).reshape(1)
    chip = (2 * ix + iy).astype(jnp.int32).reshape(1)
    me = _lin(ix, iy, ic).astype(jnp.int32).reshape(1)
    bkt = jnp.asarray(_bucket_map())
    tr = lambda a: a[0].T
    taps = lambda a: jnp.transpose(a, (1, 0, 2))
    tbl_t = rel_bias_table.T

    wi_l, cw_l, wo_l, wu_l, wd_l, fcw_l = _place_shards(
        me, [tr(w_in), taps(conv_w), w_out[0], tr(w_up), w_down[0], taps(ffn_conv_w)], [BF16, F32, BF16, BF16, BF16, F32])
    finish_a, token_a = _all_gather_split([wi_l, cw_l], "mixer", None)
    finish_b, token_b = _all_gather_split([wo_l, wu_l, wd_l, fcw_l], "ffn", token_a)
    wi_g, cw_g = finish_a(token_b)
    w_in_t = wi_g.reshape(IN_WIDTH, D_MODEL)
    conv_w_f = jnp.transpose(cw_g[:, :, 0, :], (1, 0, 2)).reshape(3, CONV_WIDTH)

    proj, ycn, qn, kn, vv = _mix_in_fwd(xs, g1, w_in_t, conv_w_f, gq, gk, gconv)
    y_attn, yan = _attn_fwd(qn, kn, vv, tbl_t, sinks, bkt, gattn)
    wo_g, wu_g, wd_g, fcw_g = finish_b(yan)
    w_out_f = wo_g.reshape(D_MODEL, D_MODEL)
    w_down_f = wd_g.reshape(N_FFN_BLK, FFN_BLK, D_MODEL)
    w_up_f = wu_g.reshape(2, N_FFN_BLK, FFN_BLK, D_MODEL)
    fcw_f = fcw_g.reshape(2, N_FFN_BLK, 3, 1, FFN_BLK)
    fcb = ffn_conv_b.reshape(2, N_FFN_BLK, 1, FFN_BLK)
    h1, u2, up, dh2, dh2b, loss_acc = _ffn_fwd(xs, ycn, yan, w_out_f, g2, w_up_f, fcw_f, fcb, w_down_f, tgt)

    act, dup, dh1, dh1b, dfb, dfcw, dg2 = _ffn_bwd(dh2, dh2b, h1, g2, up, w_up_f, fcw_f, fcb, w_down_f)
    dw_down = _grad_tn_blocked(act, dh2b, "grad_w_down", a_is_blocked=True).reshape(N_DEV, D_FF // N_DEV, D_MODEL)
    dw_up = _grad_tn_blocked(dup.reshape(N_DEV, SEQ, FFN_BLK), u2, "grad_w_up", a_is_blocked=True)
    dw_out = _grad_tn([ycn, yan], dh1b, CONV_WIDTH, "grad_w_out").reshape(N_DEV, D_MODEL // N_DEV, D_MODEL)
    out_bwd = {}

    def behind_ffn(token):
        out_bwd["r"] = _out_bwd(dh1b, w_out_f, y_attn, gattn, after=token)
        return out_bwd["r"][0]

    finish_ffn, token_ffn = _reduce_scatter_split(
        [dw_down, dw_up, dw_out, dfcw.reshape(N_DEV, 3, 1, FFN_BLK)], "ffn", core, behind_ffn)
    dycn, dy_attn, dgattn = out_bwd["r"]
    dqn, dkn, dv, dtbl_t, dsinks = _attn_bwd(qn, kn, vv, dy_attn, tbl_t, sinks, bkt, after=token_ffn)
    dx, dproj, u1, dcw, dgconv, dgq, dgk, dg1 = _mix_in_bwd(xs, dh1, proj, dycn, dqn, dkn, dv, w_in_t, conv_w_f,
                                                             g1, gq, gk, gconv)
    packed = _pack_small_grads(dg1, dg2, dgconv, dgattn, dfb, dgq, dgk, dsinks, dtbl_t, loss_acc)
    plan_s, slots_s = _broadcast_plan()
    s_sem, r_sem, src_s, land_s, token_s = _split_start(
        "gather_small_start", [packed], [jnp.broadcast_to(packed[None], (N_DEV,) + packed.shape)], plan_s, None)
    dw_in_t = _grad_w_in(dproj, u1, after=token_s).reshape(N_DEV, IN_WIDTH // N_DEV, D_MODEL)
    dcw_b = jnp.transpose(dcw.reshape(3, N_DEV, 1, CONV_WIDTH // N_DEV), (1, 0, 2, 3))
    adam = {}
    ffn_got = {}

    def behind_mixer(token):
        ffn_got["r"] = finish_ffn(token)
        (p_wd, _, _, _), (r_wd, _, _, _) = ffn_got["r"]
        adam["w_down"] = _adamw(w_down[0], m_w_down[0], v_w_down[0], p_wd, r_wd, chip, "adamw_w_down", row_blocks=2)
        return adam["w_down"][0]

    finish_mixer, token_mixer = _reduce_scatter_split([dw_in_t, dcw_b], "mixer", core, behind_mixer)
    (_, p_wu, p_wo, p_fcw), (_, r_wu, r_wo, r_fcw) = ffn_got["r"]
    adam_up = _adamw(tr(w_up), tr(m_w_up), tr(v_w_up), p_wu, r_wu, chip, "adamw_w_up", row_blocks=4, after=token_mixer)
    adam["w_out"] = _adamw(w_out[0], m_w_out[0], v_w_out[0], p_wo, r_wo, chip, "adamw_w_out", after=adam_up[0])
    adam_fcw = _adamw(taps(ffn_conv_w), taps(m_ffn_conv_w), taps(v_ffn_conv_w), p_fcw, r_fcw, chip, "adamw_ffn_conv_w",
                      after=adam["w_out"][0])
    _, (r_small,) = _split_wait("gather_small_wait", s_sem, r_sem, src_s, land_s, plan_s, slots_s, adam_fcw[0])
    small_in = {k: (p[k], m[k], v[k]) for k in _SMALL_NAMES}
    small_in["rel_bias_table"] = (tbl_t, m_rel_bias_table.T, v_rel_bias_table.T)
    small_out, loss_row = _adamw_small(r_small, small_in, None)
    (p_wi, p_cw), (r_wi, r_cw) = finish_mixer(loss_row)
    adam_in = _adamw(tr(w_in), tr(m_w_in), tr(v_w_in), p_wi, r_wi, chip, "adamw_w_in")
    adam_cw = _adamw(taps(conv_w), taps(m_conv_w), taps(v_conv_w), p_cw, r_cw, chip, "adamw_conv_w")

    res = {k: tuple(a[None] for a in t) for k, t in adam.items()}
    res["w_up"] = tuple(a.T[None] for a in adam_up)
    res["w_in"] = tuple(a.T[None] for a in adam_in)
    res["ffn_conv_w"] = tuple(taps(a) for a in adam_fcw)
    res["conv_w"] = tuple(taps(a) for a in adam_cw)
    res.update(small_out)
    res["rel_bias_table"] = tuple(a.T for a in small_out["rel_bias_table"])
    loss = loss_row[0, 0]
    order = ("norm_mix_g", "w_in", "conv_w", "q_norm_g", "k_norm_g", "rel_bias_table", "sinks", "out_norm_conv_g",
             "out_norm_attn_g", "w_out", "norm_ffn_g", "w_up", "ffn_conv_w", "ffn_conv_b", "w_down")
    return (loss, dx[None], *[res[k][0] for k in order], *[res[k][1] for k in order],
            *[res[k][2] for k in order], *[res[k][3] for k in order])
```

```python
import functools
import math

import numpy as np
import jax
import jax.numpy as jnp
from jax import lax
from jax.experimental import pallas as pl
from jax.experimental.pallas import tpu as pltpu

F32 = jnp.float32
BF16 = jnp.bfloat16

SEQ = 2048
D_MODEL = 1024
CONV_WIDTH = 512
ATTN_WIDTH = 512
KV_WIDTH = 128
HEAD_DIM = 64
N_HEADS = 8
GQA_GROUP = 4
IN_WIDTH = 2304
D_FF = 2816
BLK = 128
NUM_BUCKETS = 32
EPS = 1e-6
NEG_INF = -1e30
ADAM_LR = 0.001
ADAM_B1 = 0.9
ADAM_B2 = 0.999
ADAM_EPS = 1e-08
ADAM_WD = 0.01
ADAM_STEP = 10

N_DEV = 8
FFN_BLK = 2 * D_FF // N_DEV
N_FFN_BLK = D_FF // FFN_BLK
SUBLANES = 8
VMEM_LIMIT = 56 * 1024 * 1024

_MESH = pl.DeviceIdType.MESH
_ANY = pl.BlockSpec(memory_space=pl.ANY)


def _params(sem):
    return pltpu.CompilerParams(dimension_semantics=sem, vmem_limit_bytes=VMEM_LIMIT)


def _ordered_behind(body, pos, after):
    if after is None:
        return body, [], []
    return (lambda *refs: body(*refs[:pos], *refs[pos + 1:])), [_ANY], [after]


def _dot(a, b):
    return jnp.dot(a, b, preferred_element_type=F32)


def _dot_nt(a, b):
    return lax.dot_general(a, b, (((1,), (1,)), ((), ())), preferred_element_type=F32)


def _dot_tn(a, b):
    return lax.dot_general(a, b, (((0,), (0,)), ((), ())), preferred_element_type=F32)


def _shift_down(x, s, halo):
    r = pltpu.roll(x, s, axis=0)
    hr = pltpu.roll(halo, s, axis=0)
    row = lax.broadcasted_iota(jnp.int32, halo.shape, 0)
    top = jnp.where(row < s, hr, r[:SUBLANES])
    return jnp.concatenate([top, r[SUBLANES:]], axis=0)


def _shift_up(x, s, halo):
    n = x.shape[0]
    r = pltpu.roll(x, n - s, axis=0)
    hr = pltpu.roll(halo, SUBLANES - s, axis=0)
    row = lax.broadcasted_iota(jnp.int32, halo.shape, 0)
    bot = jnp.where(row >= SUBLANES - s, hr, r[n - SUBLANES:])
    return jnp.concatenate([r[:n - SUBLANES], bot], axis=0)


def _taps(w):
    return (w[0], w[1], w[2]) if len(w.shape) == 3 else (w[0:1], w[1:2], w[2:3])


def _conv3(x, w, halo):
    x2 = _shift_down(x, 2, halo)
    x1 = _shift_down(x, 1, halo)
    return x2 * w[0] + x1 * w[1] + x * w[2], x2, x1


def _conv3_bwd_input(dy, w, halo_next):
    return dy * w[2] + _shift_up(dy, 1, halo_next) * w[1] + _shift_up(dy, 2, halo_next) * w[0]


def _rstd(x):
    return lax.rsqrt(jnp.mean(x * x, axis=-1, keepdims=True) + EPS)


def _rms_bwd(x, g, dy):
    r = _rstd(x)
    n = x * r
    dn = dy * g
    dx = r * (dn - n * jnp.mean(dn * n, axis=-1, keepdims=True))
    return dx, jnp.sum(dy * n, axis=0, keepdims=True)


def _head_norm(x, g, heads):
    parts = []
    for h in range(heads):
        xh = x[:, HEAD_DIM * h:HEAD_DIM * (h + 1)]
        parts.append(xh * _rstd(xh) * g)
    return jnp.concatenate(parts, axis=-1)


def _head_norm_bwd(x, g, dy, heads):
    dxs, dg = [], jnp.zeros((1, HEAD_DIM), F32)
    for h in range(heads):
        sl = slice(HEAD_DIM * h, HEAD_DIM * (h + 1))
        dxh, dgh = _rms_bwd(x[:, sl], g, dy[:, sl])
        dxs.append(dxh)
        dg = dg + dgh
    return jnp.concatenate(dxs, axis=-1), dg


def _bucket_map():
    q = np.arange(BLK)[:, None]
    j = np.arange(BLK)[None, :]
    n = np.where(j > q, q + BLK - j, q - j)
    nf = np.maximum(n, 1).astype(np.float32)
    max_exact = NUM_BUCKETS // 2
    large = max_exact + (np.log(nf / max_exact) / math.log(BLK / max_exact) * (NUM_BUCKETS - max_exact)).astype(np.int32)
    large = np.minimum(large, NUM_BUCKETS - 1)
    return np.where(n < max_exact, n, large).astype(np.int32)


def _coords():
    return lax.axis_index("x"), lax.axis_index("y"), lax.axis_index("c")


def _lin(px, py, pc):
    return 4 * px + 2 * py + pc


_HBM = pl.BlockSpec(memory_space=pltpu.HBM)
_SEM = pl.BlockSpec(memory_space=pltpu.SEMAPHORE)
_EFFECT = pltpu.SideEffectType.DATAFLOW_SIDE_EFFECTING


def _in_hbm(a):
    return pltpu.with_memory_space_constraint(a, pltpu.HBM)


def _split_start(name, srcs, lands, plan, after):
    ns, nl = len(srcs), len(lands)
    n_copies = len(plan(0, 0, 0))
    n_after = 0 if after is None else 1

    def body(*refs):
        src_refs, land_refs = refs[:ns + nl], refs[ns:ns + nl]
        send_sems, recv_sems = refs[ns + nl + n_after], refs[ns + nl + n_after + 1]
        token = refs[-1]
        for k, (a, s_slot, l, d_slot, dev) in enumerate(plan(*_coords())):
            src = src_refs[a] if s_slot is None else src_refs[a].at[s_slot]
            pltpu.make_async_remote_copy(src_ref=src, dst_ref=land_refs[l].at[d_slot], send_sem=send_sems.at[k],
                                         recv_sem=recv_sems.at[k], device_id=dev, device_id_type=_MESH).start()
        token[...] = jnp.zeros_like(token)

    arrs = list(srcs) + list(lands)
    out = pl.pallas_call(
        body, name=name,
        out_shape=(pltpu.SemaphoreType.DMA((n_copies,)), pltpu.SemaphoreType.DMA((n_copies,)),
                   *[pltpu.HBM(a.shape, a.dtype) for a in arrs], jax.ShapeDtypeStruct((SUBLANES, 128), F32)),
        in_specs=[_HBM] * (ns + nl) + [_ANY] * n_after,
        out_specs=(_SEM, _SEM, *[_HBM] * (ns + nl), pl.BlockSpec(memory_space=pltpu.VMEM)),
        input_output_aliases={i: 2 + i for i in range(ns + nl)},
        compiler_params=pltpu.CompilerParams(has_side_effects=_EFFECT),
    )(*[_in_hbm(a) for a in arrs], *([] if after is None else [after]))
    return out[0], out[1], list(out[2:2 + ns]), list(out[2 + ns:2 + ns + nl]), out[-1]


def _split_wait(name, send_sems, recv_sems, srcs, lands, plan, recv_slots, after):
    ns, nl = len(srcs), len(lands)

    def body(*refs):
        src_refs, land_refs = refs[:ns + nl], refs[ns:ns + nl]
        send_sems, recv_sems = refs[ns + nl], refs[ns + nl + 1]
        coords = _coords()
        slots = recv_slots(*coords)
        for k, (a, s_slot, l, _, dev) in enumerate(plan(*coords)):
            src = src_refs[a] if s_slot is None else src_refs[a].at[s_slot]
            cp = pltpu.make_async_remote_copy(src_ref=src, dst_ref=land_refs[l].at[slots[k]], send_sem=send_sems.at[k],
                                              recv_sem=recv_sems.at[k], device_id=dev, device_id_type=_MESH)
            cp.wait_send()
            cp.wait_recv()

    arrs = list(srcs) + list(lands)
    out = pl.pallas_call(
        body, name=name,
        out_shape=tuple(pltpu.HBM(a.shape, a.dtype) for a in arrs),
        in_specs=[_HBM] * (ns + nl) + [_SEM, _SEM, _ANY],
        out_specs=tuple([_HBM] * (ns + nl)),
        input_output_aliases={i: i for i in range(ns + nl)},
        compiler_params=pltpu.CompilerParams(has_side_effects=_EFFECT),
    )(*arrs, send_sems, recv_sems, after)
    return list(out[:ns]), list(out[ns:])


def _chips(x, y):
    return [(1 - x, y), (x, 1 - y), (1 - x, 1 - y)]


def _gather_plan_ici(n):
    def plan(x, y, c):
        me = _lin(x, y, c)
        out = []
        for a in range(n):
            out.append((a, me, a, me, (x, y, 1 - c)))
            out += [(a, me, a, me, (cx, cy, c)) for cx, cy in _chips(x, y)]
        return out

    def recv_slots(x, y, c):
        out = []
        for _ in range(n):
            out.append(_lin(x, y, 1 - c))
            out += [_lin(cx, cy, c) for cx, cy in _chips(x, y)]
        return out

    return plan, recv_slots


def _gather_plan_d2d(n):
    def plan(x, y, c):
        return [(a, _lin(cx, cy, c), a, _lin(cx, cy, c), (x, y, 1 - c)) for a in range(n) for cx, cy in _chips(x, y)]

    def recv_slots(x, y, c):
        return [_lin(cx, cy, 1 - c) for _ in range(n) for cx, cy in _chips(x, y)]

    return plan, recv_slots


def _all_gather_split(lands, tag, after):
    n = len(lands)
    plan1, slots1 = _gather_plan_ici(n)
    s1, r1, _, lands, token = _split_start(f"gather_{tag}_ici_start", [], lands, plan1, after)

    def finish(after):
        _, got = _split_wait(f"gather_{tag}_ici_wait", s1, r1, [], lands, plan1, slots1, after)
        plan2, slots2 = _gather_plan_d2d(n)
        s2, r2, _, got, token2 = _split_start(f"gather_{tag}_d2d_start", [], got, plan2, None)
        return _split_wait(f"gather_{tag}_d2d_wait", s2, r2, [], got, plan2, slots2, token2)[1]

    return finish, token


_CHIP_LIST = ((0, 0), (0, 1), (1, 0), (1, 1))


def _reduce_plan_d2d(n):
    def plan(x, y, c):
        return [(a, _lin(qx, qy, 1 - c), a, q, (x, y, 1 - c)) for a in range(n) for q, (qx, qy) in enumerate(_CHIP_LIST)]

    def recv_slots(x, y, c):
        return [q for _ in range(n) for q in range(4)]

    return plan, recv_slots


def _reduce_plan_ici(n):
    def plan(x, y, c):
        return [(a, 2 * cx + cy, a, j, (cx, cy, c)) for a in range(n) for j, (cx, cy) in enumerate(_chips(x, y))]

    def recv_slots(x, y, c):
        return [j for _ in range(n) for j in range(3)]

    return plan, recv_slots


def _broadcast_plan():
    def peers(x, y, c):
        return [(1 - x if r & 4 else x, 1 - y if r & 2 else y, 1 - c if r & 1 else c) for r in range(1, N_DEV)]

    def plan(x, y, c):
        return [(0, None, 0, _lin(x, y, c), peer) for peer in peers(x, y, c)]

    def recv_slots(x, y, c):
        return [_lin(*peer) for peer in peers(x, y, c)]

    return plan, recv_slots


def _chip_partial(grads, recvd, core, name):
    n = len(grads)

    def body(c_ref, *refs):
        for a in range(n):
            g_ref, r_ref, o_ref = refs[a], refs[n + a], refs[2 * n + a]
            o_ref[...] = (g_ref[...].astype(F32) + r_ref[...].astype(F32)).astype(o_ref.dtype)

    def blk(a, own):
        zeros = (0,) * (a.ndim - 1)
        return pl.BlockSpec((None,) + a.shape[1:],
                            (lambda q, c_ref: (2 * q + c_ref[0],) + zeros) if own else (lambda q, c_ref: (q,) + zeros))

    return pl.pallas_call(
        body, name=name,
        grid_spec=pltpu.PrefetchScalarGridSpec(
            num_scalar_prefetch=1, grid=(4,),
            in_specs=[blk(a, True) for a in grads] + [blk(a, False) for a in recvd],
            out_specs=[blk(a, False) for a in recvd]),
        out_shape=[jax.ShapeDtypeStruct(a.shape, a.dtype) for a in recvd],
        compiler_params=_params(("arbitrary",)),
    )(core, *grads, *recvd)


def _reduce_scatter_split(grads, tag, core, behind):
    n = len(grads)
    plan1, slots1 = _reduce_plan_d2d(n)
    lands1 = [lax.empty((4,) + a.shape[1:], a.dtype) for a in grads]
    s1, r1, srcs1, lands1, token1 = _split_start(f"reduce_{tag}_d2d_start", grads, lands1, plan1, None)
    own, got = _split_wait(f"reduce_{tag}_d2d_wait", s1, r1, srcs1, lands1, plan1, slots1, behind(token1))
    parts = _chip_partial(own, got, core, f"reduce_{tag}_partial")
    plan2, slots2 = _reduce_plan_ici(n)
    lands2 = [lax.empty((3,) + a.shape[1:], a.dtype) for a in grads]
    s2, r2, srcs2, lands2, token2 = _split_start(f"reduce_{tag}_ici_start", parts, lands2, plan2, None)

    def finish(after):
        return _split_wait(f"reduce_{tag}_ici_wait", s2, r2, srcs2, lands2, plan2, slots2, after)

    return finish, token2


def _place_shards(me, shards, dtypes):
    n = len(shards)

    def body(me_ref, *refs):
        for a in range(n):
            refs[n + a][...] = refs[a][...].astype(dtypes[a])

    full = lambda s: pl.BlockSpec(s.shape, lambda i, me_ref: (0,) * s.ndim)
    slot = lambda s: pl.BlockSpec((None,) + s.shape, lambda i, me_ref: (me_ref[0],) + (0,) * s.ndim)
    return pl.pallas_call(
        body, name="place_shards",
        grid_spec=pltpu.PrefetchScalarGridSpec(num_scalar_prefetch=1, grid=(1,), in_specs=[full(s) for s in shards],
                                               out_specs=[slot(s) for s in shards]),
        out_shape=[jax.ShapeDtypeStruct((N_DEV,) + s.shape, d) for s, d in zip(shards, dtypes)],
        compiler_params=_params(("arbitrary",)),
    )(me, *shards)


def _mix_in_fwd(x, g1, w_in_t, conv_w, gq, gk, gconv):
    tm = 512
    n_t = SEQ // tm

    def body(x_ref, g1_ref, w_ref, cw_ref, gq_ref, gk_ref, gc_ref,
             proj_ref, ycn_ref, qn_ref, kn_ref, v_ref, halo_ref):
        @pl.when(pl.program_id(0) == 0)
        def _():
            halo_ref[...] = jnp.zeros_like(halo_ref)

        xv = x_ref[...]
        u = (xv * _rstd(xv) * g1_ref[...]).astype(BF16)
        proj = _dot_nt(u, w_ref[...])
        proj_ref[...] = proj
        gate_b = proj[:, 0:CONV_WIDTH]
        a = proj[:, CONV_WIDTH:2 * CONV_WIDTH] * proj[:, 2 * CONV_WIDTH:3 * CONV_WIDTH]
        cv, _, _ = _conv3(a, _taps(cw_ref[...]), halo_ref[...])
        halo_ref[...] = a[tm - SUBLANES:]
        yc = gate_b * cv
        ycn_ref[...] = (yc * _rstd(yc) * gc_ref[...]).astype(BF16)
        q0 = 3 * CONV_WIDTH
        qn_ref[...] = _head_norm(proj[:, q0:q0 + ATTN_WIDTH], gq_ref[...], N_HEADS).astype(BF16)
        k0 = q0 + ATTN_WIDTH
        kn_ref[...] = _head_norm(proj[:, k0:k0 + KV_WIDTH], gk_ref[...], 2).astype(BF16)
        v_ref[...] = proj[:, k0 + KV_WIDTH:k0 + 2 * KV_WIDTH].astype(BF16)

    const = lambda shape: pl.BlockSpec(shape, lambda i: (0,) * len(shape))
    rows = lambda w: pl.BlockSpec((tm, w), lambda i: (i, 0))
    return pl.pallas_call(
        body, name="mix_in_fwd", grid=(n_t,),
        in_specs=[rows(D_MODEL), const((1, D_MODEL)), const((IN_WIDTH, D_MODEL)), const((3, CONV_WIDTH)),
                  const((1, HEAD_DIM)), const((1, HEAD_DIM)), const((1, CONV_WIDTH))],
        out_specs=[rows(IN_WIDTH), rows(CONV_WIDTH), rows(ATTN_WIDTH), rows(KV_WIDTH), rows(KV_WIDTH)],
        out_shape=[jax.ShapeDtypeStruct((SEQ, IN_WIDTH), F32), jax.ShapeDtypeStruct((SEQ, CONV_WIDTH), BF16),
                   jax.ShapeDtypeStruct((SEQ, ATTN_WIDTH), BF16), jax.ShapeDtypeStruct((SEQ, KV_WIDTH), BF16),
                   jax.ShapeDtypeStruct((SEQ, KV_WIDTH), BF16)],
        scratch_shapes=[pltpu.VMEM((SUBLANES, CONV_WIDTH), F32)],
        compiler_params=_params(("arbitrary",)),
    )(x, g1, w_in_t, conv_w, gq, gk, gconv)


GROUP_ROWS = GQA_GROUP * BLK


def _band_bias(tbl_ref, bkt, bias_ref):
    for h in range(N_HEADS):
        acc = jnp.zeros(bkt.shape, F32)
        for b in range(NUM_BUCKETS):
            acc = jnp.where(bkt == b, tbl_ref[h, b], acc)
        bias_ref[h // GQA_GROUP, BLK * (h % GQA_GROUP):BLK * (h % GQA_GROUP + 1), :] = acc


def _band_masks(i):
    qi = lax.broadcasted_iota(jnp.int32, (GROUP_ROWS, BLK), 0) & (BLK - 1)
    ji = lax.broadcasted_iota(jnp.int32, (GROUP_ROWS, BLK), 1)
    upper = ji > qi
    return upper, upper & (i == 0)


def _stack_heads(x, g):
    return jnp.concatenate([x[:, HEAD_DIM * h:HEAD_DIM * (h + 1)] for h in range(GQA_GROUP * g, GQA_GROUP * (g + 1))], axis=0)


def _unstack_heads(groups):
    return jnp.concatenate([p[BLK * t:BLK * (t + 1)] for p in groups for t in range(GQA_GROUP)], axis=-1)


def _per_head_rows(vals):
    row = lax.broadcasted_iota(jnp.int32, (GROUP_ROWS, 1), 0)
    col = jnp.full((GROUP_ROWS, 1), vals[GQA_GROUP - 1], F32)
    for t in range(GQA_GROUP - 2, -1, -1):
        col = jnp.where(row < BLK * (t + 1), vals[t], col)
    return col


def _band_rows(ref, i):
    prev = pl.multiple_of(jnp.maximum(i - 1, 0) * BLK, BLK)
    cur = pl.multiple_of(i * BLK, BLK)
    return jnp.concatenate([ref[pl.ds(prev, BLK), :], ref[pl.ds(cur, BLK), :]], axis=0), prev, cur


def _fold(band, upper):
    return jnp.where(upper, band[:, :BLK], band[:, BLK:])


def _unfold(tile, upper):
    return jnp.concatenate([jnp.where(upper, tile, 0.0), jnp.where(upper, 0.0, tile)], axis=1)


def _head_probs(qh, kh, bias, upper, dead, sink):
    logits = _fold(_dot_nt(qh, kh), upper) * (HEAD_DIM ** -0.5) + bias
    logits = jnp.where(dead, NEG_INF, logits)
    m = jnp.maximum(jnp.max(logits, axis=-1, keepdims=True), sink)
    p = jnp.exp(logits - m)
    es = jnp.exp(sink - m)
    den = jnp.sum(p, axis=-1, keepdims=True) + es
    return p / den, es / den


def _attn_fwd(qn, kn, v, tbl, sinks, bkt, gattn):
    n_b = SEQ // BLK

    def body(q_ref, k_ref, v_ref, tbl_ref, sink_ref, bkt_ref, ga_ref, y_ref, yn_ref, bias_ref):
        i = pl.program_id(0)

        @pl.when(i == 0)
        def _():
            _band_bias(tbl_ref, bkt_ref[...], bias_ref)

        kb, _, _ = _band_rows(k_ref, i)
        vb, _, _ = _band_rows(v_ref, i)
        upper, dead = _band_masks(i)
        q = q_ref[...]
        outs = []
        for g in range(N_HEADS // GQA_GROUP):
            kv = slice(HEAD_DIM * g, HEAD_DIM * (g + 1))
            sink = _per_head_rows([sink_ref[0, GQA_GROUP * g + t] for t in range(GQA_GROUP)])
            probs, _ = _head_probs(_stack_heads(q, g), kb[:, kv], bias_ref[g], upper, dead, sink)
            outs.append(_dot(_unfold(probs, upper).astype(BF16), vb[:, kv]))
        y = _unstack_heads(outs)
        y_ref[...] = y
        yn_ref[...] = (y * _rstd(y) * ga_ref[...]).astype(BF16)

    const = lambda shape: pl.BlockSpec(shape, lambda i: (0,) * len(shape))
    rows = lambda w: pl.BlockSpec((BLK, w), lambda i: (i, 0))
    smem = pl.BlockSpec(memory_space=pltpu.SMEM)
    return pl.pallas_call(
        body, name="attn_fwd", grid=(n_b,),
        in_specs=[rows(ATTN_WIDTH), const((SEQ, KV_WIDTH)), const((SEQ, KV_WIDTH)), smem, smem,
                  const((BLK, BLK)), const((1, ATTN_WIDTH))],
        out_specs=[rows(ATTN_WIDTH), rows(ATTN_WIDTH)],
        out_shape=[jax.ShapeDtypeStruct((SEQ, ATTN_WIDTH), F32), jax.ShapeDtypeStruct((SEQ, ATTN_WIDTH), BF16)],
        scratch_shapes=[pltpu.VMEM((N_HEADS // GQA_GROUP, GROUP_ROWS, BLK), F32)],
        compiler_params=_params(("arbitrary",)),
    )(qn, kn, v, tbl, sinks, bkt, gattn)


def _ffn_fwd(x, ycn, yan, w_out, g2, w_up, fcw, fcb, w_down, tgt):
    tm = 512
    n_t = SEQ // tm

    def body(x_ref, ycn_ref, yan_ref, wo_ref, g2_ref, wu_ref, cw_ref, b_ref, wd_ref, tgt_ref,
             h1_ref, u2_ref, up_ref, pre_ref, act_ref, dh2_ref, dh2b_ref, loss_ref, acc_ref, halo_ref):
        i, j = pl.program_id(0), pl.program_id(1)

        @pl.when((i == 0) & (j == 0))
        def _():
            loss_ref[...] = jnp.zeros_like(loss_ref)

        @pl.when(j == 0)
        def _():
            h1 = x_ref[...] + _dot(ycn_ref[...], wo_ref[0:CONV_WIDTH, :]) + _dot(yan_ref[...], wo_ref[CONV_WIDTH:, :])
            h1_ref[...] = h1
            u2_ref[...] = (h1 * _rstd(h1) * g2_ref[...]).astype(BF16)
            acc_ref[...] = jnp.zeros_like(acc_ref)

        u2 = u2_ref[...]
        pre = []
        for s in range(2):
            up = _dot_nt(u2, wu_ref[s])
            up_ref[s] = up.astype(BF16)
            halo = jnp.where(i == 0, 0.0, halo_ref[s, j])
            pre.append(_conv3(up, _taps(cw_ref.at[s]), halo)[0] + b_ref[s])
            pre_ref[s] = pre[s].astype(BF16)
            halo_ref[s, j] = up[tm - SUBLANES:]
        g, val = pre
        act = (g * jax.nn.sigmoid(g) * val).astype(BF16)
        act_ref[...] = act
        acc_ref[...] += _dot(act, wd_ref[...])

        @pl.when(j == N_FFN_BLK - 1)
        def _():
            err = h1_ref[...] + acc_ref[...] - tgt_ref[...]
            loss_ref[...] += 0.5 * jnp.sum(err * err) / D_MODEL
            dh2 = err / D_MODEL
            dh2_ref[...] = dh2
            dh2b_ref[...] = dh2.astype(BF16)

    rows = lambda w: pl.BlockSpec((tm, w), lambda i, j: (i, 0))
    const = lambda shape: pl.BlockSpec(shape, lambda i, j: (0,) * len(shape))
    pair = lambda *s: pl.BlockSpec((2, None) + s, lambda i, j: (0, j) + (0,) * len(s))
    upb = pl.BlockSpec((2, None, tm, FFN_BLK), lambda i, j: (0, j, i, 0))
    return pl.pallas_call(
        body, name="ffn_fwd", grid=(n_t, N_FFN_BLK),
        in_specs=[rows(D_MODEL), rows(CONV_WIDTH), rows(ATTN_WIDTH), const((D_MODEL, D_MODEL)), const((1, D_MODEL)),
                  pair(FFN_BLK, D_MODEL), pair(3, 1, FFN_BLK), pair(1, FFN_BLK),
                  pl.BlockSpec((None, FFN_BLK, D_MODEL), lambda i, j: (j, 0, 0)), rows(D_MODEL)],
        out_specs=[rows(D_MODEL), rows(D_MODEL), upb, upb, pl.BlockSpec((None, tm, FFN_BLK), lambda i, j: (j, i, 0)),
                   rows(D_MODEL), rows(D_MODEL), const((SUBLANES, 128))],
        out_shape=[jax.ShapeDtypeStruct((SEQ, D_MODEL), F32), jax.ShapeDtypeStruct((SEQ, D_MODEL), BF16),
                   jax.ShapeDtypeStruct((2, N_FFN_BLK, SEQ, FFN_BLK), BF16),
                   jax.ShapeDtypeStruct((2, N_FFN_BLK, SEQ, FFN_BLK), BF16),
                   jax.ShapeDtypeStruct((N_FFN_BLK, SEQ, FFN_BLK), BF16),
                   jax.ShapeDtypeStruct((SEQ, D_MODEL), F32), jax.ShapeDtypeStruct((SEQ, D_MODEL), BF16),
                   jax.ShapeDtypeStruct((SUBLANES, 128), F32)],
        scratch_shapes=[pltpu.VMEM((tm, D_MODEL), F32), pltpu.VMEM((2, N_FFN_BLK, SUBLANES, FFN_BLK), F32)],
        compiler_params=_params(("arbitrary", "arbitrary")),
    )(x, ycn, yan, w_out, g2, w_up, fcw, fcb, w_down, tgt)


def _ffn_bwd(dh2, dh2b, h1, g2, up, pre, w_up, fcw, w_down):
    tm = 512
    n_t = SEQ // tm

    def body(dh2_ref, dh2b_ref, h1_ref, g2_ref, up_ref, pre_ref, wu_ref, cw_ref, wd_ref,
             dup_ref, dh1_ref, dh1b_ref, dfb_ref, dfcw_ref, dg2_ref, acc_ref, next_ref):
        i, j = pl.program_id(0), pl.program_id(1)

        @pl.when((i == 0) & (j == 0))
        def _():
            dfb_ref[...] = jnp.zeros_like(dfb_ref)
            dfcw_ref[...] = jnp.zeros_like(dfcw_ref)
            dg2_ref[...] = jnp.zeros_like(dg2_ref)

        @pl.when(j == 0)
        def _():
            acc_ref[...] = jnp.zeros_like(acc_ref)

        g, val = pre_ref[0].astype(F32), pre_ref[1].astype(F32)
        sg = jax.nn.sigmoid(g)
        silu = g * sg
        dact = _dot_nt(dh2b_ref[...], wd_ref[...])
        dpre = (dact * val * (sg * (1.0 + g * (1.0 - sg))), dact * silu)
        for s in range(2):
            d = dpre[s]
            u = up_ref[s].astype(F32)
            w = _taps(cw_ref.at[s])
            nxt = jnp.where(i == 0, 0.0, next_ref[s, j])
            d1 = _shift_up(d, 1, nxt)
            d2 = _shift_up(d, 2, nxt)
            next_ref[s, j] = d[:SUBLANES]
            dfb_ref[s, j] += jnp.sum(d, axis=0, keepdims=True)
            dfcw_ref[s, j, 0] += jnp.sum(d2 * u, axis=0, keepdims=True)
            dfcw_ref[s, j, 1] += jnp.sum(d1 * u, axis=0, keepdims=True)
            dfcw_ref[s, j, 2] += jnp.sum(d * u, axis=0, keepdims=True)
            dup = (d * w[2] + d1 * w[1] + d2 * w[0]).astype(BF16)
            dup_ref[s] = dup
            acc_ref[...] += _dot(dup, wu_ref[s])

        @pl.when(j == N_FFN_BLK - 1)
        def _():
            dn, dgain = _rms_bwd(h1_ref[...], g2_ref[...], acc_ref[...])
            dh1 = dh2_ref[...] + dn
            dh1_ref[...] = dh1
            dh1b_ref[...] = dh1.astype(BF16)
            dg2_ref[...] += dgain

    rev = lambda i: n_t - 1 - i
    rows = lambda w: pl.BlockSpec((tm, w), lambda i, j: (rev(i), 0))
    const = lambda shape: pl.BlockSpec(shape, lambda i, j: (0,) * len(shape))
    pair = lambda *s: pl.BlockSpec((2, None) + s, lambda i, j: (0, j) + (0,) * len(s))
    upb = pl.BlockSpec((2, None, tm, FFN_BLK), lambda i, j: (0, j, rev(i), 0))
    return pl.pallas_call(
        body, name="ffn_bwd", grid=(n_t, N_FFN_BLK),
        in_specs=[rows(D_MODEL), rows(D_MODEL), rows(D_MODEL), const((1, D_MODEL)), upb, upb,
                  pair(FFN_BLK, D_MODEL), pair(3, 1, FFN_BLK),
                  pl.BlockSpec((None, FFN_BLK, D_MODEL), lambda i, j: (j, 0, 0))],
        out_specs=[upb, rows(D_MODEL), rows(D_MODEL),
                   const((2, N_FFN_BLK, 1, FFN_BLK)), const((2, N_FFN_BLK, 3, 1, FFN_BLK)), const((1, D_MODEL))],
        out_shape=[jax.ShapeDtypeStruct((2, N_FFN_BLK, SEQ, FFN_BLK), BF16), jax.ShapeDtypeStruct((SEQ, D_MODEL), F32),
                   jax.ShapeDtypeStruct((SEQ, D_MODEL), BF16), jax.ShapeDtypeStruct((2, N_FFN_BLK, 1, FFN_BLK), F32),
                   jax.ShapeDtypeStruct((2, N_FFN_BLK, 3, 1, FFN_BLK), F32), jax.ShapeDtypeStruct((1, D_MODEL), F32)],
        scratch_shapes=[pltpu.VMEM((tm, D_MODEL), F32), pltpu.VMEM((2, N_FFN_BLK, SUBLANES, FFN_BLK), F32)],
        compiler_params=_params(("arbitrary", "arbitrary")),
    )(dh2, dh2b, h1, g2, up, pre, w_up, fcw, w_down)


def _grad_tn(a_list, b, out_rows, name, after=None):
    n = len(a_list)
    ncol = b.shape[1]

    def body(*refs):
        a_refs, b_ref, o_ref = refs[:n], refs[n], refs[n + 1]
        j = pl.program_id(0)
        for k in range(n):
            @pl.when(j == k)
            def _(k=k):
                o_ref[...] = _dot_tn(a_refs[k][...], b_ref[...]).astype(BF16)

    full = lambda shape: pl.BlockSpec(shape, lambda j: (0,) * len(shape))
    body, more_specs, more = _ordered_behind(body, n + 1, after)
    return pl.pallas_call(
        body, name=name, grid=(n,),
        in_specs=[full((SEQ, out_rows))] * n + [full((SEQ, ncol))] + more_specs,
        out_specs=pl.BlockSpec((None, out_rows, ncol), lambda j: (j, 0, 0)),
        out_shape=jax.ShapeDtypeStruct((n, out_rows, ncol), BF16),
        compiler_params=_params(("arbitrary",)),
    )(*a_list, b, *more)


def _grad_tn_blocked(a, b, name, a_is_blocked):
    nb = a.shape[0] if a_is_blocked else b.shape[0]
    a_w, b_w = a.shape[-1], b.shape[-1]

    def body(a_ref, b_ref, o_ref):
        o_ref[...] = _dot_tn(a_ref[...], b_ref[...]).astype(BF16)

    blocked = lambda w: pl.BlockSpec((None, SEQ, w), lambda k: (k, 0, 0))
    full = lambda w: pl.BlockSpec((SEQ, w), lambda k: (0, 0))
    return pl.pallas_call(
        body, name=name, grid=(nb,),
        in_specs=[blocked(a_w) if a_is_blocked else full(a_w), full(b_w) if a_is_blocked else blocked(b_w)],
        out_specs=pl.BlockSpec((None, a_w, b_w), lambda k: (k, 0, 0)),
        out_shape=jax.ShapeDtypeStruct((nb, a_w, b_w), BF16),
        compiler_params=_params(("arbitrary",)),
    )(a, b)


def _out_bwd(dh1b, w_out, y_attn, gattn, after=None):
    tm = 512
    n_t = SEQ // tm

    def body(dh_ref, wo_ref, y_ref, ga_ref, dycn_ref, dy_ref, dga_ref):
        @pl.when(pl.program_id(0) == 0)
        def _():
            dga_ref[...] = jnp.zeros_like(dga_ref)

        dycat = _dot_nt(dh_ref[...], wo_ref[...])
        dycn_ref[...] = dycat[:, :CONV_WIDTH]
        dy, dga = _rms_bwd(y_ref[...], ga_ref[...], dycat[:, CONV_WIDTH:])
        dy_ref[...] = dy
        dga_ref[...] += dga

    rows = lambda w: pl.BlockSpec((tm, w), lambda i: (i, 0))
    const = lambda shape: pl.BlockSpec(shape, lambda i: (0,) * len(shape))
    body, more_specs, more = _ordered_behind(body, 4, after)
    return pl.pallas_call(
        body, name="out_bwd", grid=(n_t,),
        in_specs=[rows(D_MODEL), const((D_MODEL, D_MODEL)), rows(ATTN_WIDTH), const((1, ATTN_WIDTH))] + more_specs,
        out_specs=[rows(CONV_WIDTH), rows(ATTN_WIDTH), const((1, ATTN_WIDTH))],
        out_shape=[jax.ShapeDtypeStruct((SEQ, CONV_WIDTH), F32), jax.ShapeDtypeStruct((SEQ, ATTN_WIDTH), F32),
                   jax.ShapeDtypeStruct((1, ATTN_WIDTH), F32)],
        compiler_params=_params(("arbitrary",)),
    )(dh1b, w_out, y_attn, gattn, *more)


def _attn_bwd(qn, kn, v, dy, tbl, sinks, bkt, after=None):
    n_b = SEQ // BLK

    def body(q_ref, k_ref, v_ref, dy_ref, tbl_ref, sink_ref, bkt_ref,
             dq_ref, dk_ref, dv_ref, dtbl_ref, dsink_ref, bias_ref, dbias_ref, dsacc_ref):
        i = pl.program_id(0)

        @pl.when(i == 0)
        def _():
            _band_bias(tbl_ref, bkt_ref[...], bias_ref)
            dbias_ref[...] = jnp.zeros_like(dbias_ref)
            dsacc_ref[...] = jnp.zeros_like(dsacc_ref)
            dk_ref[...] = jnp.zeros_like(dk_ref)
            dv_ref[...] = jnp.zeros_like(dv_ref)

        kb, prev, cur = _band_rows(k_ref, i)
        vb, _, _ = _band_rows(v_ref, i)
        upper, dead = _band_masks(i)
        q = q_ref[...]
        dy = dy_ref[...]
        dqs, dks, dvs = [], [], []
        for g in range(N_HEADS // GQA_GROUP):
            kv = slice(HEAD_DIM * g, HEAD_DIM * (g + 1))
            qg = _stack_heads(q, g)
            dog = _stack_heads(dy, g).astype(BF16)
            sink = _per_head_rows([sink_ref[0, GQA_GROUP * g + t] for t in range(GQA_GROUP)])
            probs, psink = _head_probs(qg, kb[:, kv], bias_ref[g], upper, dead, sink)
            dprobs = _fold(_dot_nt(dog, vb[:, kv]), upper)
            dvs.append(_dot_tn(_unfold(probs, upper).astype(BF16), dog))
            dsum = jnp.sum(probs * dprobs, axis=-1, keepdims=True)
            dlogits = probs * (dprobs - dsum)
            dsacc_ref[g] += jnp.broadcast_to(-psink * dsum, (GROUP_ROWS, 128))
            dbias_ref[g] += dlogits
            ds = _unfold(dlogits * (HEAD_DIM ** -0.5), upper).astype(BF16)
            dqs.append(_dot(ds, kb[:, kv]))
            dks.append(_dot_tn(ds, qg))
        dq_ref[...] = _unstack_heads(dqs)
        dkb = jnp.concatenate(dks, axis=-1)
        dvb = jnp.concatenate(dvs, axis=-1)
        dk_ref[pl.ds(prev, BLK), :] += dkb[:BLK]
        dk_ref[pl.ds(cur, BLK), :] += dkb[BLK:]
        dv_ref[pl.ds(prev, BLK), :] += dvb[:BLK]
        dv_ref[pl.ds(cur, BLK), :] += dvb[BLK:]

        @pl.when(i == n_b - 1)
        def _():
            bkt = bkt_ref[...]
            row8 = lax.broadcasted_iota(jnp.int32, (N_HEADS, 128), 0)
            lane8 = lax.broadcasted_iota(jnp.int32, (N_HEADS, 128), 1)
            lane1 = lax.broadcasted_iota(jnp.int32, (1, 128), 1)
            acc = jnp.zeros((N_HEADS, 128), F32)
            dsink = jnp.zeros((1, 128), F32)
            for h in range(N_HEADS):
                rows = slice(BLK * (h % GQA_GROUP), BLK * (h % GQA_GROUP + 1))
                dsink = jnp.where(lane1 == h, jnp.sum(dsacc_ref[h // GQA_GROUP, rows, :], axis=0, keepdims=True), dsink)
                dbh = dbias_ref[h // GQA_GROUP, rows, :]
                for b in range(NUM_BUCKETS):
                    acc = jnp.where((row8 == h) & (lane8 == b), jnp.sum(jnp.where(bkt == b, dbh, 0.0)), acc)
            dsink_ref[...] = dsink
            dtbl_ref[...] = acc

    const = lambda shape: pl.BlockSpec(shape, lambda i: (0,) * len(shape))
    rows = lambda w: pl.BlockSpec((BLK, w), lambda i: (i, 0))
    smem = pl.BlockSpec(memory_space=pltpu.SMEM)
    body, more_specs, more = _ordered_behind(body, 7, after)
    return pl.pallas_call(
        body, name="attn_bwd", grid=(n_b,),
        in_specs=[rows(ATTN_WIDTH), const((SEQ, KV_WIDTH)), const((SEQ, KV_WIDTH)), rows(ATTN_WIDTH), smem, smem,
                  const((BLK, BLK))] + more_specs,
        out_specs=[rows(ATTN_WIDTH), const((SEQ, KV_WIDTH)), const((SEQ, KV_WIDTH)), const((N_HEADS, 128)), const((1, 128))],
        out_shape=[jax.ShapeDtypeStruct((SEQ, ATTN_WIDTH), F32), jax.ShapeDtypeStruct((SEQ, KV_WIDTH), F32),
                   jax.ShapeDtypeStruct((SEQ, KV_WIDTH), F32), jax.ShapeDtypeStruct((N_HEADS, 128), F32),
                   jax.ShapeDtypeStruct((1, 128), F32)],
        scratch_shapes=[pltpu.VMEM((N_HEADS // GQA_GROUP, GROUP_ROWS, BLK), F32)] * 3,
        compiler_params=_params(("arbitrary",)),
    )(qn, kn, v, dy, tbl, sinks, bkt, *more)


def _mix_in_bwd(x, dh1, proj, dycn, dqn, dkn, dv, w_in_t, conv_w, g1, gq, gk, gconv):
    tm = 256
    n_t = SEQ // tm
    halo_blocks = tm // SUBLANES

    def body(x_ref, dh1_ref, proj_ref, halo_ref, dycn_ref, dqn_ref, dkn_ref, dv_ref, w_ref, cw_ref,
             g1_ref, gq_ref, gk_ref, gc_ref,
             dx_ref, dproj_ref, u1_ref, dcw_ref, dgc_ref, dgq_ref, dgk_ref, dg1_ref, next_ref):
        i = pl.program_id(0)
        first_tile = i == n_t - 1

        @pl.when(i == 0)
        def _():
            for r in (dcw_ref, dgc_ref, dgq_ref, dgk_ref, dg1_ref, next_ref):
                r[...] = jnp.zeros_like(r)

        proj = proj_ref[...]
        hp = halo_ref[...]
        gate_b = proj[:, 0:CONV_WIDTH]
        gate_c = proj[:, CONV_WIDTH:2 * CONV_WIDTH]
        hc = proj[:, 2 * CONV_WIDTH:3 * CONV_WIDTH]
        a = gate_c * hc
        a_halo = jnp.where(first_tile, 0.0, hp[:, CONV_WIDTH:2 * CONV_WIDTH] * hp[:, 2 * CONV_WIDTH:3 * CONV_WIDTH])
        cw = _taps(cw_ref[...])
        cv, a2, a1 = _conv3(a, cw, a_halo)
        dyc, dgc = _rms_bwd(gate_b * cv, gc_ref[...], dycn_ref[...])
        dgc_ref[...] += dgc
        dcv = dyc * gate_b
        dcw_ref[...] += jnp.concatenate(
            [jnp.sum(dcv * a2, axis=0, keepdims=True), jnp.sum(dcv * a1, axis=0, keepdims=True),
             jnp.sum(dcv * a, axis=0, keepdims=True)], axis=0)
        da = _conv3_bwd_input(dcv, cw, next_ref[...])
        next_ref[...] = dcv[:SUBLANES]
        q0 = 3 * CONV_WIDTH
        k0 = q0 + ATTN_WIDTH
        dq, dgq = _head_norm_bwd(proj[:, q0:k0], gq_ref[...], dqn_ref[...], N_HEADS)
        dk, dgk = _head_norm_bwd(proj[:, k0:k0 + KV_WIDTH], gk_ref[...], dkn_ref[...], 2)
        dgq_ref[...] += dgq
        dgk_ref[...] += dgk
        dproj = jnp.concatenate([dyc * cv, da * hc, da * gate_c, dq, dk, dv_ref[...]], axis=-1).astype(BF16)
        dproj_ref[...] = dproj
        du1 = _dot(dproj, w_ref[...])
        xv = x_ref[...]
        dn, dg1 = _rms_bwd(xv, g1_ref[...], du1)
        dx_ref[...] = dh1_ref[...] + dn
        dg1_ref[...] += dg1
        u1_ref[...] = (xv * _rstd(xv) * g1_ref[...]).astype(BF16)

    rev = lambda i: n_t - 1 - i
    rows = lambda w: pl.BlockSpec((tm, w), lambda i: (rev(i), 0))
    const = lambda shape: pl.BlockSpec(shape, lambda i: (0,) * len(shape))
    halo = pl.BlockSpec((SUBLANES, IN_WIDTH), lambda i: (jnp.maximum(rev(i) * halo_blocks - 1, 0), 0))
    return pl.pallas_call(
        body, name="mix_in_bwd", grid=(n_t,),
        in_specs=[rows(D_MODEL), rows(D_MODEL), rows(IN_WIDTH), halo, rows(CONV_WIDTH), rows(ATTN_WIDTH), rows(KV_WIDTH),
                  rows(KV_WIDTH), const((IN_WIDTH, D_MODEL)), const((3, CONV_WIDTH)), const((1, D_MODEL)),
                  const((1, HEAD_DIM)), const((1, HEAD_DIM)), const((1, CONV_WIDTH))],
        out_specs=[rows(D_MODEL), rows(IN_WIDTH), rows(D_MODEL), const((3, CONV_WIDTH)), const((1, CONV_WIDTH)),
                   const((1, HEAD_DIM)), const((1, HEAD_DIM)), const((1, D_MODEL))],
        out_shape=[jax.ShapeDtypeStruct((SEQ, D_MODEL), F32), jax.ShapeDtypeStruct((SEQ, IN_WIDTH), BF16),
                   jax.ShapeDtypeStruct((SEQ, D_MODEL), BF16), jax.ShapeDtypeStruct((3, CONV_WIDTH), F32),
                   jax.ShapeDtypeStruct((1, CONV_WIDTH), F32), jax.ShapeDtypeStruct((1, HEAD_DIM), F32),
                   jax.ShapeDtypeStruct((1, HEAD_DIM), F32), jax.ShapeDtypeStruct((1, D_MODEL), F32)],
        scratch_shapes=[pltpu.VMEM((SUBLANES, CONV_WIDTH), F32)],
        compiler_params=_params(("arbitrary",)),
    )(x, dh1, proj, proj, dycn, dqn, dkn, dv, w_in_t, conv_w, g1, gq, gk, gconv)


def _grad_w_in(dproj, u1, after=None):
    bw = 768

    def body(a_ref, b_ref, o_ref):
        o_ref[...] = _dot_tn(a_ref[...], b_ref[...]).astype(BF16)

    body, more_specs, more = _ordered_behind(body, 2, after)
    return pl.pallas_call(
        body, name="grad_w_in", grid=(IN_WIDTH // bw,),
        in_specs=[pl.BlockSpec((SEQ, bw), lambda k: (0, k)), pl.BlockSpec((SEQ, D_MODEL), lambda k: (0, 0))] + more_specs,
        out_specs=pl.BlockSpec((bw, D_MODEL), lambda k: (k, 0)),
        out_shape=jax.ShapeDtypeStruct((IN_WIDTH, D_MODEL), BF16),
        compiler_params=_params(("arbitrary",)),
    )(dproj, u1, *more)


def _adamw_math(w, g, m, v):
    m = ADAM_B1 * m + (1.0 - ADAM_B1) * g
    v = ADAM_B2 * v + (1.0 - ADAM_B2) * (g * g)
    m_hat = m / (1.0 - ADAM_B1 ** ADAM_STEP)
    v_hat = v / (1.0 - ADAM_B2 ** ADAM_STEP)
    return -ADAM_LR * (m_hat / (jnp.sqrt(v_hat) + ADAM_EPS) + ADAM_WD * w), m, v


_ROW_G1, _ROW_G2, _ROW_OUT_NORMS, _ROW_FFN_B, _ROW_GQ, _ROW_GK, _ROW_SINKS, _ROW_LOSS, _ROW_TABLE = 0, 1, 2, 3, 11, 12, 13, 14, 16
SMALL_ROWS, SMALL_COLS = 24, 1024
_SMALL_NAMES = ("norm_mix_g", "norm_ffn_g", "out_norm_conv_g", "out_norm_attn_g", "ffn_conv_b", "q_norm_g", "k_norm_g",
                "sinks", "rel_bias_table")


def _pack_small_grads(dg1, dg2, dgconv, dgattn, dfb, dgq, dgk, dsinks, dtbl_t, loss_acc):
    def body(dg1_ref, dg2_ref, dgc_ref, dga_ref, dfb_ref, dgq_ref, dgk_ref, ds_ref, dt_ref, loss_ref, o_ref):
        o_ref[...] = jnp.zeros_like(o_ref)
        o_ref[_ROW_G1:_ROW_G1 + 1, :] = dg1_ref[...]
        o_ref[_ROW_G2:_ROW_G2 + 1, :] = dg2_ref[...]
        o_ref[_ROW_OUT_NORMS:_ROW_OUT_NORMS + 1, 0:CONV_WIDTH] = dgc_ref[...]
        o_ref[_ROW_OUT_NORMS:_ROW_OUT_NORMS + 1, CONV_WIDTH:] = dga_ref[...]
        for k in range(N_DEV):
            o_ref[_ROW_FFN_B + k:_ROW_FFN_B + k + 1, 0:FFN_BLK] = dfb_ref[k // N_FFN_BLK, k % N_FFN_BLK]
        o_ref[_ROW_GQ:_ROW_GQ + 1, 0:HEAD_DIM] = dgq_ref[...]
        o_ref[_ROW_GK:_ROW_GK + 1, 0:HEAD_DIM] = dgk_ref[...]
        o_ref[_ROW_SINKS:_ROW_SINKS + 1, 0:128] = ds_ref[...]
        o_ref[_ROW_LOSS:_ROW_LOSS + 1, 0:128] = loss_ref[0:1, :]
        o_ref[_ROW_TABLE:_ROW_TABLE + N_HEADS, 0:128] = dt_ref[...]

    return pl.pallas_call(body, name="pack_small_grads", out_shape=jax.ShapeDtypeStruct((SMALL_ROWS, SMALL_COLS), F32))(
        dg1, dg2, dgconv, dgattn, dfb, dgq, dgk, dsinks, dtbl_t, loss_acc)


def _adamw_small(recv, params, after):
    names = _SMALL_NAMES
    n = len(names)

    def grad_of(g, name, k=None):
        if name == "norm_mix_g":
            return g[_ROW_G1:_ROW_G1 + 1, :]
        if name == "norm_ffn_g":
            return g[_ROW_G2:_ROW_G2 + 1, :]
        if name == "out_norm_conv_g":
            return g[_ROW_OUT_NORMS:_ROW_OUT_NORMS + 1, 0:CONV_WIDTH]
        if name == "out_norm_attn_g":
            return g[_ROW_OUT_NORMS:_ROW_OUT_NORMS + 1, CONV_WIDTH:]
        if name == "ffn_conv_b":
            return g[_ROW_FFN_B + k:_ROW_FFN_B + k + 1, 0:FFN_BLK]
        if name == "q_norm_g":
            return g[_ROW_GQ:_ROW_GQ + 1, 0:HEAD_DIM]
        if name == "k_norm_g":
            return g[_ROW_GK:_ROW_GK + 1, 0:HEAD_DIM]
        if name == "sinks":
            return g[_ROW_SINKS:_ROW_SINKS + 1, 0:N_HEADS]
        return g[_ROW_TABLE:_ROW_TABLE + N_HEADS, 0:NUM_BUCKETS]

    def body(r_ref, *refs):
        ins, outs, loss_ref = refs[:3 * n], refs[3 * n:7 * n], refs[7 * n]
        g = r_ref[0]
        for s in range(1, N_DEV):
            g = g + r_ref[s]
        loss_ref[...] = g[_ROW_LOSS:_ROW_LOSS + 1, 0:128]
        for i, name in enumerate(names):
            w_ref, m_ref, v_ref = ins[3 * i:3 * i + 3]
            o = outs[4 * i:4 * i + 4]
            cols = [slice(FFN_BLK * k, FFN_BLK * (k + 1)) for k in range(N_DEV)] if name == "ffn_conv_b" else [slice(None)]
            for k, cs in enumerate(cols):
                gk = grad_of(g, name, k)
                d, m2, v2 = _adamw_math(w_ref[:, cs], gk, m_ref[:, cs], v_ref[:, cs])
                o[0][:, cs], o[1][:, cs], o[2][:, cs], o[3][:, cs] = gk, d, m2, v2

    flat = [a for name in names for a in params[name]]
    body, more_specs, more = _ordered_behind(body, 1 + 3 * n, after)
    vmem = pl.BlockSpec(memory_space=pltpu.VMEM)
    out = pl.pallas_call(
        body, name="adamw_small",
        in_specs=[vmem] * (1 + 3 * n) + more_specs,
        out_shape=[jax.ShapeDtypeStruct(params[name][0].shape, F32) for name in names for _ in range(4)]
        + [jax.ShapeDtypeStruct((1, 128), F32)],
        compiler_params=pltpu.CompilerParams(vmem_limit_bytes=VMEM_LIMIT),
    )(recv, *flat, *more)
    return {name: tuple(out[4 * i:4 * i + 4]) for i, name in enumerate(names)}, out[4 * n]


def _adamw(w, m, v, part, recv, chip, name, row_blocks=1, after=None):
    rb = w.shape[0] // row_blocks
    tail = w.shape[1:]
    zeros = (0,) * len(tail)

    def body(chip_ref, w_ref, m_ref, v_ref, p_ref, r_ref, g_o, d_o, m_o, v_o):
        g = p_ref[...].astype(F32)
        for s in range(3):
            g = g + r_ref[s].astype(F32)
        g_o[...] = g
        d_o[...], m_o[...], v_o[...] = _adamw_math(w_ref[...], g, m_ref[...], v_ref[...])

    blk = pl.BlockSpec((rb,) + tail, lambda i, chip_ref: (i,) + zeros)
    pblk = pl.BlockSpec((None, rb) + tail, lambda i, chip_ref: (chip_ref[0], i) + zeros)
    rblk = pl.BlockSpec((3, rb) + tail, lambda i, chip_ref: (0, i) + zeros)
    body, more_specs, more = _ordered_behind(body, 6, after)
    return pl.pallas_call(
        body, name=name,
        grid_spec=pltpu.PrefetchScalarGridSpec(num_scalar_prefetch=1, grid=(row_blocks,),
                                               in_specs=[blk, blk, blk, pblk, rblk] + more_specs, out_specs=[blk] * 4),
        out_shape=[jax.ShapeDtypeStruct(w.shape, F32)] * 4,
        compiler_params=_params(("arbitrary",)),
    )(chip, w, m, v, part, recv, *more)


def kernel(x, norm_mix_g, w_in, conv_w, q_norm_g, k_norm_g, rel_bias_table, sinks, out_norm_conv_g, out_norm_attn_g, w_out, norm_ffn_g, w_up, ffn_conv_w, ffn_conv_b, w_down, loss_target, m_norm_mix_g, m_w_in, m_conv_w, m_q_norm_g, m_k_norm_g, m_rel_bias_table, m_sinks, m_out_norm_conv_g, m_out_norm_attn_g, m_w_out, m_norm_ffn_g, m_w_up, m_ffn_conv_w, m_ffn_conv_b, m_w_down, v_norm_mix_g, v_w_in, v_conv_w, v_q_norm_g, v_k_norm_g, v_rel_bias_table, v_sinks, v_out_norm_conv_g, v_out_norm_attn_g, v_w_out, v_norm_ffn_g, v_w_up, v_ffn_conv_w, v_ffn_conv_b, v_w_down):
    p = dict(norm_mix_g=norm_mix_g, w_in=w_in, conv_w=conv_w, q_norm_g=q_norm_g, k_norm_g=k_norm_g,
             rel_bias_table=rel_bias_table, sinks=sinks, out_norm_conv_g=out_norm_conv_g, out_norm_attn_g=out_norm_attn_g,
             w_out=w_out, norm_ffn_g=norm_ffn_g, w_up=w_up, ffn_conv_w=ffn_conv_w, ffn_conv_b=ffn_conv_b, w_down=w_down)
    m = dict(norm_mix_g=m_norm_mix_g, w_in=m_w_in, conv_w=m_conv_w, q_norm_g=m_q_norm_g, k_norm_g=m_k_norm_g,
             rel_bias_table=m_rel_bias_table, sinks=m_sinks, out_norm_conv_g=m_out_norm_conv_g,
             out_norm_attn_g=m_out_norm_attn_g, w_out=m_w_out, norm_ffn_g=m_norm_ffn_g, w_up=m_w_up,
             ffn_conv_w=m_ffn_conv_w, ffn_conv_b=m_ffn_conv_b, w_down=m_w_down)
    v = dict(norm_mix_g=v_norm_mix_g, w_in=v_w_in, conv_w=v_conv_w, q_norm_g=v_q_norm_g, k_norm_g=v_k_norm_g,
             rel_bias_table=v_rel_bias_table, sinks=v_sinks, out_norm_conv_g=v_out_norm_conv_g,
             out_norm_attn_g=v_out_norm_attn_g, w_out=v_w_out, norm_ffn_g=v_norm_ffn_g, w_up=v_w_up,
             ffn_conv_w=v_ffn_conv_w, ffn_conv_b=v_ffn_conv_b, w_down=v_w_down)

    xs, tgt = x[0], loss_target[0]
    g1, g2, gq, gk, gconv, gattn = norm_mix_g, norm_ffn_g, q_norm_g, k_norm_g, out_norm_conv_g, out_norm_attn_g
    ix, iy, ic = _coords()
    core = ic.astype(jnp.int32).reshape(1)
    chip = (2 * ix + iy).astype(jnp.int32).reshape(1)
    me = _lin(ix, iy, ic).astype(jnp.int32).reshape(1)
    bkt = jnp.asarray(_bucket_map())
    tr = lambda a: a[0].T
    taps = lambda a: jnp.transpose(a, (1, 0, 2))
    tbl_t = rel_bias_table.T

    wi_l, cw_l, wo_l, wu_l, wd_l, fcw_l = _place_shards(
        me, [tr(w_in), taps(conv_w), w_out[0], tr(w_up), w_down[0], taps(ffn_conv_w)], [BF16, F32, BF16, BF16, BF16, F32])
    finish_a, token_a = _all_gather_split([wi_l, cw_l], "mixer", None)
    finish_b, token_b = _all_gather_split([wo_l, wu_l, wd_l, fcw_l], "ffn", token_a)
    wi_g, cw_g = finish_a(token_b)
    w_in_t = wi_g.reshape(IN_WIDTH, D_MODEL)
    conv_w_f = jnp.transpose(cw_g[:, :, 0, :], (1, 0, 2)).reshape(3, CONV_WIDTH)

    proj, ycn, qn, kn, vv = _mix_in_fwd(xs, g1, w_in_t, conv_w_f, gq, gk, gconv)
    y_attn, yan = _attn_fwd(qn, kn, vv, tbl_t, sinks, bkt, gattn)
    wo_g, wu_g, wd_g, fcw_g = finish_b(yan)
    w_out_f = wo_g.reshape(D_MODEL, D_MODEL)
    w_down_f = wd_g.reshape(N_FFN_BLK, FFN_BLK, D_MODEL)
    w_up_f = wu_g.reshape(2, N_FFN_BLK, FFN_BLK, D_MODEL)
    fcw_f = fcw_g.reshape(2, N_FFN_BLK, 3, 1, FFN_BLK)
    fcb = ffn_conv_b.reshape(2, N_FFN_BLK, 1, FFN_BLK)
    h1, u2, up, pre, act, dh2, dh2b, loss_acc = _ffn_fwd(xs, ycn, yan, w_out_f, g2, w_up_f, fcw_f, fcb, w_down_f, tgt)

    dup, dh1, dh1b, dfb, dfcw, dg2 = _ffn_bwd(dh2, dh2b, h1, g2, up, pre, w_up_f, fcw_f, w_down_f)
    dw_down = _grad_tn_blocked(act, dh2b, "grad_w_down", a_is_blocked=True).reshape(N_DEV, D_FF // N_DEV, D_MODEL)
    dw_up = _grad_tn_blocked(dup.reshape(N_DEV, SEQ, FFN_BLK), u2, "grad_w_up", a_is_blocked=True)
    dw_out = _grad_tn([ycn, yan], dh1b, CONV_WIDTH, "grad_w_out").reshape(N_DEV, D_MODEL // N_DEV, D_MODEL)
    out_bwd = {}

    def behind_ffn(token):
        out_bwd["r"] = _out_bwd(dh1b, w_out_f, y_attn, gattn, after=token)
        return out_bwd["r"][0]

    finish_ffn, token_ffn = _reduce_scatter_split(
        [dw_down, dw_up, dw_out, dfcw.reshape(N_DEV, 3, 1, FFN_BLK)], "ffn", core, behind_ffn)
    dycn, dy_attn, dgattn = out_bwd["r"]
    dqn, dkn, dv, dtbl_t, dsinks = _attn_bwd(qn, kn, vv, dy_attn, tbl_t, sinks, bkt, after=token_ffn)
    dx, dproj, u1, dcw, dgconv, dgq, dgk, dg1 = _mix_in_bwd(xs, dh1, proj, dycn, dqn, dkn, dv, w_in_t, conv_w_f,
                                                             g1, gq, gk, gconv)
    packed = _pack_small_grads(dg1, dg2, dgconv, dgattn, dfb, dgq, dgk, dsinks, dtbl_t, loss_acc)
    plan_s, slots_s = _broadcast_plan()
    s_sem, r_sem, src_s, land_s, token_s = _split_start(
        "gather_small_start", [packed], [jnp.broadcast_to(packed[None], (N_DEV,) + packed.shape)], plan_s, None)
    dw_in_t = _grad_w_in(dproj, u1, after=token_s).reshape(N_DEV, IN_WIDTH // N_DEV, D_MODEL)
    dcw_b = jnp.transpose(dcw.reshape(3, N_DEV, 1, CONV_WIDTH // N_DEV), (1, 0, 2, 3))
    adam = {}
    ffn_got = {}

    def behind_mixer(token):
        ffn_got["r"] = finish_ffn(token)
        (p_wd, _, _, _), (r_wd, _, _, _) = ffn_got["r"]
        adam["w_down"] = _adamw(w_down[0], m_w_down[0], v_w_down[0], p_wd, r_wd, chip, "adamw_w_down", row_blocks=2)
        return adam["w_down"][0]

    finish_mixer, token_mixer = _reduce_scatter_split([dw_in_t, dcw_b], "mixer", core, behind_mixer)
    (_, p_wu, p_wo, p_fcw), (_, r_wu, r_wo, r_fcw) = ffn_got["r"]
    adam_up = _adamw(tr(w_up), tr(m_w_up), tr(v_w_up), p_wu, r_wu, chip, "adamw_w_up", row_blocks=4, after=token_mixer)
    adam["w_out"] = _adamw(w_out[0], m_w_out[0], v_w_out[0], p_wo, r_wo, chip, "adamw_w_out", after=adam_up[0])
    adam_fcw = _adamw(taps(ffn_conv_w), taps(m_ffn_conv_w), taps(v_ffn_conv_w), p_fcw, r_fcw, chip, "adamw_ffn_conv_w",
                      after=adam["w_out"][0])
    _, (r_small,) = _split_wait("gather_small_wait", s_sem, r_sem, src_s, land_s, plan_s, slots_s, adam_fcw[0])
    small_in = {k: (p[k], m[k], v[k]) for k in _SMALL_NAMES}
    small_in["rel_bias_table"] = (tbl_t, m_rel_bias_table.T, v_rel_bias_table.T)
    small_out, loss_row = _adamw_small(r_small, small_in, None)
    (p_wi, p_cw), (r_wi, r_cw) = finish_mixer(loss_row)
    adam_in = _adamw(tr(w_in), tr(m_w_in), tr(v_w_in), p_wi, r_wi, chip, "adamw_w_in")
    adam_cw = _adamw(taps(conv_w), taps(m_conv_w), taps(v_conv_w), p_cw, r_cw, chip, "adamw_conv_w")

    res = {k: tuple(a[None] for a in t) for k, t in adam.items()}
    res["w_up"] = tuple(a.T[None] for a in adam_up)
    res["w_in"] = tuple(a.T[None] for a in adam_in)
    res["ffn_conv_w"] = tuple(taps(a) for a in adam_fcw)
    res["conv_w"] = tuple(taps(a) for a in adam_cw)
    res.update(small_out)
    res["rel_bias_table"] = tuple(a.T for a in small_out["rel_bias_table"])
    loss = loss_row[0, 0]
    order = ("norm_mix_g", "w_in", "conv_w", "q_norm_g", "k_norm_g", "rel_bias_table", "sinks", "out_norm_conv_g",
             "out_norm_attn_g", "w_out", "norm_ffn_g", "w_up", "ffn_conv_w", "ffn_conv_b", "w_down")
    return (loss, dx[None], *[res[k][0] for k in order], *[res[k][1] for k in order],
            *[res[k][2] for k in order], *[res[k][3] for k in order])
```

```python
import functools
import math

import numpy as np
import jax
import jax.numpy as jnp
from jax import lax
from jax.experimental import pallas as pl
from jax.experimental.pallas import tpu as pltpu

F32 = jnp.float32
BF16 = jnp.bfloat16

SEQ = 2048
D_MODEL = 1024
CONV_WIDTH = 512
ATTN_WIDTH = 512
KV_WIDTH = 128
HEAD_DIM = 64
N_HEADS = 8
GQA_GROUP = 4
IN_WIDTH = 2304
D_FF = 2816
BLK = 128
NUM_BUCKETS = 32
EPS = 1e-6
NEG_INF = -1e30
ADAM_LR = 0.001
ADAM_B1 = 0.9
ADAM_B2 = 0.999
ADAM_EPS = 1e-08
ADAM_WD = 0.01
ADAM_STEP = 10

N_DEV = 8
FFN_BLK = 2 * D_FF // N_DEV
N_FFN_BLK = D_FF // FFN_BLK
SUBLANES = 8
VMEM_LIMIT = 56 * 1024 * 1024

_MESH = pl.DeviceIdType.MESH
_ANY = pl.BlockSpec(memory_space=pl.ANY)


def _params(sem):
    return pltpu.CompilerParams(dimension_semantics=sem, vmem_limit_bytes=VMEM_LIMIT)


def _ordered_behind(body, pos, after):
    if after is None:
        return body, [], []
    return (lambda *refs: body(*refs[:pos], *refs[pos + 1:])), [_ANY], [after]


def _dot(a, b):
    return jnp.dot(a, b, preferred_element_type=F32)


def _dot_nt(a, b):
    return lax.dot_general(a, b, (((1,), (1,)), ((), ())), preferred_element_type=F32)


def _dot_tn(a, b):
    return lax.dot_general(a, b, (((0,), (0,)), ((), ())), preferred_element_type=F32)


def _shift_down(x, s, halo):
    r = pltpu.roll(x, s, axis=0)
    hr = pltpu.roll(halo, s, axis=0)
    row = lax.broadcasted_iota(jnp.int32, halo.shape, 0)
    top = jnp.where(row < s, hr, r[:SUBLANES])
    return jnp.concatenate([top, r[SUBLANES:]], axis=0)


def _shift_up(x, s, halo):
    n = x.shape[0]
    r = pltpu.roll(x, n - s, axis=0)
    hr = pltpu.roll(halo, SUBLANES - s, axis=0)
    row = lax.broadcasted_iota(jnp.int32, halo.shape, 0)
    bot = jnp.where(row >= SUBLANES - s, hr, r[n - SUBLANES:])
    return jnp.concatenate([r[:n - SUBLANES], bot], axis=0)


def _taps(w):
    return (w[0], w[1], w[2]) if len(w.shape) == 3 else (w[0:1], w[1:2], w[2:3])


def _conv3(x, w, halo):
    x2 = _shift_down(x, 2, halo)
    x1 = _shift_down(x, 1, halo)
    return x2 * w[0] + x1 * w[1] + x * w[2], x2, x1


def _conv3_bwd_input(dy, w, halo_next):
    return dy * w[2] + _shift_up(dy, 1, halo_next) * w[1] + _shift_up(dy, 2, halo_next) * w[0]


def _rstd(x):
    return lax.rsqrt(jnp.mean(x * x, axis=-1, keepdims=True) + EPS)


def _rms_bwd(x, g, dy):
    r = _rstd(x)
    n = x * r
    dn = dy * g
    dx = r * (dn - n * jnp.mean(dn * n, axis=-1, keepdims=True))
    return dx, jnp.sum(dy * n, axis=0, keepdims=True)


def _head_norm(x, g, heads):
    parts = []
    for h in range(heads):
        xh = x[:, HEAD_DIM * h:HEAD_DIM * (h + 1)]
        parts.append(xh * _rstd(xh) * g)
    return jnp.concatenate(parts, axis=-1)


def _head_norm_bwd(x, g, dy, heads):
    dxs, dg = [], jnp.zeros((1, HEAD_DIM), F32)
    for h in range(heads):
        sl = slice(HEAD_DIM * h, HEAD_DIM * (h + 1))
        dxh, dgh = _rms_bwd(x[:, sl], g, dy[:, sl])
        dxs.append(dxh)
        dg = dg + dgh
    return jnp.concatenate(dxs, axis=-1), dg


def _bucket_map():
    q = np.arange(BLK)[:, None]
    j = np.arange(BLK)[None, :]
    n = np.where(j > q, q + BLK - j, q - j)
    nf = np.maximum(n, 1).astype(np.float32)
    max_exact = NUM_BUCKETS // 2
    large = max_exact + (np.log(nf / max_exact) / math.log(BLK / max_exact) * (NUM_BUCKETS - max_exact)).astype(np.int32)
    large = np.minimum(large, NUM_BUCKETS - 1)
    return np.where(n < max_exact, n, large).astype(np.int32)


def _coords():
    return lax.axis_index("x"), lax.axis_index("y"), lax.axis_index("c")


def _lin(px, py, pc):
    return 4 * px + 2 * py + pc


_HBM = pl.BlockSpec(memory_space=pltpu.HBM)
_SEM = pl.BlockSpec(memory_space=pltpu.SEMAPHORE)
_EFFECT = pltpu.SideEffectType.DATAFLOW_SIDE_EFFECTING


def _in_hbm(a):
    return pltpu.with_memory_space_constraint(a, pltpu.HBM)


def _split_start(name, srcs, lands, plan, after):
    ns, nl = len(srcs), len(lands)
    n_copies = len(plan(0, 0, 0))
    n_after = 0 if after is None else 1

    def body(*refs):
        src_refs, land_refs = refs[:ns + nl], refs[ns:ns + nl]
        send_sems, recv_sems = refs[ns + nl + n_after], refs[ns + nl + n_after + 1]
        token = refs[-1]
        for k, (a, s_slot, l, d_slot, dev) in enumerate(plan(*_coords())):
            src = src_refs[a] if s_slot is None else src_refs[a].at[s_slot]
            pltpu.make_async_remote_copy(src_ref=src, dst_ref=land_refs[l].at[d_slot], send_sem=send_sems.at[k],
                                         recv_sem=recv_sems.at[k], device_id=dev, device_id_type=_MESH).start()
        token[...] = jnp.zeros_like(token)

    arrs = list(srcs) + list(lands)
    out = pl.pallas_call(
        body, name=name,
        out_shape=(pltpu.SemaphoreType.DMA((n_copies,)), pltpu.SemaphoreType.DMA((n_copies,)),
                   *[pltpu.HBM(a.shape, a.dtype) for a in arrs], jax.ShapeDtypeStruct((SUBLANES, 128), F32)),
        in_specs=[_HBM] * (ns + nl) + [_ANY] * n_after,
        out_specs=(_SEM, _SEM, *[_HBM] * (ns + nl), pl.BlockSpec(memory_space=pltpu.VMEM)),
        input_output_aliases={i: 2 + i for i in range(ns + nl)},
        compiler_params=pltpu.CompilerParams(has_side_effects=_EFFECT),
    )(*[_in_hbm(a) for a in arrs], *([] if after is None else [after]))
    return out[0], out[1], list(out[2:2 + ns]), list(out[2 + ns:2 + ns + nl]), out[-1]


def _split_wait(name, send_sems, recv_sems, srcs, lands, plan, recv_slots, after):
    ns, nl = len(srcs), len(lands)

    def body(*refs):
        src_refs, land_refs = refs[:ns + nl], refs[ns:ns + nl]
        send_sems, recv_sems = refs[ns + nl], refs[ns + nl + 1]
        coords = _coords()
        slots = recv_slots(*coords)
        for k, (a, s_slot, l, _, dev) in enumerate(plan(*coords)):
            src = src_refs[a] if s_slot is None else src_refs[a].at[s_slot]
            cp = pltpu.make_async_remote_copy(src_ref=src, dst_ref=land_refs[l].at[slots[k]], send_sem=send_sems.at[k],
                                              recv_sem=recv_sems.at[k], device_id=dev, device_id_type=_MESH)
            cp.wait_send()
            cp.wait_recv()

    arrs = list(srcs) + list(lands)
    out = pl.pallas_call(
        body, name=name,
        out_shape=tuple(pltpu.HBM(a.shape, a.dtype) for a in arrs),
        in_specs=[_HBM] * (ns + nl) + [_SEM, _SEM, _ANY],
        out_specs=tuple([_HBM] * (ns + nl)),
        input_output_aliases={i: i for i in range(ns + nl)},
        compiler_params=pltpu.CompilerParams(has_side_effects=_EFFECT),
    )(*arrs, send_sems, recv_sems, after)
    return list(out[:ns]), list(out[ns:])


def _chips(x, y):
    return [(1 - x, y), (x, 1 - y), (1 - x, 1 - y)]


def _gather_plan_ici(n):
    def plan(x, y, c):
        me = _lin(x, y, c)
        out = []
        for a in range(n):
            out.append((a, me, a, me, (x, y, 1 - c)))
            out += [(a, me, a, me, (cx, cy, c)) for cx, cy in _chips(x, y)]
        return out

    def recv_slots(x, y, c):
        out = []
        for _ in range(n):
            out.append(_lin(x, y, 1 - c))
            out += [_lin(cx, cy, c) for cx, cy in _chips(x, y)]
        return out

    return plan, recv_slots


def _gather_plan_d2d(n):
    def plan(x, y, c):
        return [(a, _lin(cx, cy, c), a, _lin(cx, cy, c), (x, y, 1 - c)) for a in range(n) for cx, cy in _chips(x, y)]

    def recv_slots(x, y, c):
        return [_lin(cx, cy, 1 - c) for _ in range(n) for cx, cy in _chips(x, y)]

    return plan, recv_slots


def _all_gather_split(lands, tag, after):
    n = len(lands)
    plan1, slots1 = _gather_plan_ici(n)
    s1, r1, _, lands, token = _split_start(f"gather_{tag}_ici_start", [], lands, plan1, after)

    def finish(after):
        _, got = _split_wait(f"gather_{tag}_ici_wait", s1, r1, [], lands, plan1, slots1, after)
        plan2, slots2 = _gather_plan_d2d(n)
        s2, r2, _, got, token2 = _split_start(f"gather_{tag}_d2d_start", [], got, plan2, None)
        return _split_wait(f"gather_{tag}_d2d_wait", s2, r2, [], got, plan2, slots2, token2)[1]

    return finish, token


def _all_gather_tree(lands, tag, after):
    n = len(lands)

    def plan1(x, y, c):
        me = _lin(x, y, c)
        return [(a, me, a, me, dev) for a in range(n) for dev in ((x, y, 1 - c), (1 - x, y, c), (x, 1 - y, c))]

    def slots1(x, y, c):
        return [s for _ in range(n) for s in (_lin(x, y, 1 - c), _lin(1 - x, y, c), _lin(x, 1 - y, c))]

    def plan2(x, y, c):
        from_x, from_y = _lin(1 - x, y, c), _lin(x, 1 - y, c)
        north = c == 1
        passed = jnp.where(north, from_x, from_y)
        onward = (jnp.where(north, x, 1 - x), jnp.where(north, 1 - y, y), c)
        sib = (x, y, 1 - c)
        return [cp for a in range(n) for cp in ((a, passed, a, passed, onward), (a, from_x, a, from_x, sib),
                                                (a, from_y, a, from_y, sib))]

    def slots2(x, y, c):
        return [s for _ in range(n) for s in (_lin(1 - x, 1 - y, c), _lin(1 - x, y, 1 - c), _lin(x, 1 - y, 1 - c))]

    def plan3(x, y, c):
        diag = _lin(1 - x, 1 - y, c)
        return [(a, diag, a, diag, (x, y, 1 - c)) for a in range(n)]

    def slots3(x, y, c):
        return [_lin(1 - x, 1 - y, 1 - c)] * n

    s1, r1, _, lands, token = _split_start(f"gather_{tag}_1_start", [], lands, plan1, after)
    state = {}

    def stage2(after):
        _, got = _split_wait(f"gather_{tag}_1_wait", s1, r1, [], lands, plan1, slots1, after)
        state["s"], state["r"], _, state["lands"], token2 = _split_start(f"gather_{tag}_2_start", [], got, plan2, None)
        return token2

    def stage3(after):
        _, got = _split_wait(f"gather_{tag}_2_wait", state["s"], state["r"], [], state["lands"], plan2, slots2, after)
        s3, r3, _, got, token3 = _split_start(f"gather_{tag}_3_start", [], got, plan3, None)
        return _split_wait(f"gather_{tag}_3_wait", s3, r3, [], got, plan3, slots3, token3)[1]

    return stage2, stage3, token


_CHIP_LIST = ((0, 0), (0, 1), (1, 0), (1, 1))


def _reduce_plan_d2d(n):
    def plan(x, y, c):
        return [(a, _lin(qx, qy, 1 - c), a, q, (x, y, 1 - c)) for a in range(n) for q, (qx, qy) in enumerate(_CHIP_LIST)]

    def recv_slots(x, y, c):
        return [q for _ in range(n) for q in range(4)]

    return plan, recv_slots


def _reduce_plan_ici(n):
    def plan(x, y, c):
        return [(a, 2 * cx + cy, a, j, (cx, cy, c)) for a in range(n) for j, (cx, cy) in enumerate(_chips(x, y))]

    def recv_slots(x, y, c):
        return [j for _ in range(n) for j in range(3)]

    return plan, recv_slots


def _broadcast_plan():
    def peers(x, y, c):
        return [(1 - x if r & 4 else x, 1 - y if r & 2 else y, 1 - c if r & 1 else c) for r in range(1, N_DEV)]

    def plan(x, y, c):
        return [(0, None, 0, _lin(x, y, c), peer) for peer in peers(x, y, c)]

    def recv_slots(x, y, c):
        return [_lin(*peer) for peer in peers(x, y, c)]

    return plan, recv_slots


def _chip_partial(grads, recvd, core, name):
    n = len(grads)

    def body(c_ref, *refs):
        for a in range(n):
            g_ref, r_ref, o_ref = refs[a], refs[n + a], refs[2 * n + a]
            o_ref[...] = (g_ref[...].astype(F32) + r_ref[...].astype(F32)).astype(o_ref.dtype)

    def blk(a, own):
        zeros = (0,) * (a.ndim - 1)
        return pl.BlockSpec((None,) + a.shape[1:],
                            (lambda q, c_ref: (2 * q + c_ref[0],) + zeros) if own else (lambda q, c_ref: (q,) + zeros))

    return pl.pallas_call(
        body, name=name,
        grid_spec=pltpu.PrefetchScalarGridSpec(
            num_scalar_prefetch=1, grid=(4,),
            in_specs=[blk(a, True) for a in grads] + [blk(a, False) for a in recvd],
            out_specs=[blk(a, False) for a in recvd]),
        out_shape=[jax.ShapeDtypeStruct(a.shape, a.dtype) for a in recvd],
        compiler_params=_params(("arbitrary",)),
    )(core, *grads, *recvd)


def _reduce_scatter_split(grads, tag, core, behind):
    n = len(grads)
    plan1, slots1 = _reduce_plan_d2d(n)
    lands1 = [lax.empty((4,) + a.shape[1:], a.dtype) for a in grads]
    s1, r1, srcs1, lands1, token1 = _split_start(f"reduce_{tag}_d2d_start", grads, lands1, plan1, None)
    own, got = _split_wait(f"reduce_{tag}_d2d_wait", s1, r1, srcs1, lands1, plan1, slots1, behind(token1))
    parts = _chip_partial(own, got, core, f"reduce_{tag}_partial")
    plan2, slots2 = _reduce_plan_ici(n)
    lands2 = [lax.empty((3,) + a.shape[1:], a.dtype) for a in grads]
    s2, r2, srcs2, lands2, token2 = _split_start(f"reduce_{tag}_ici_start", parts, lands2, plan2, None)

    def finish(after):
        return _split_wait(f"reduce_{tag}_ici_wait", s2, r2, srcs2, lands2, plan2, slots2, after)

    return finish, token2


def _place_shards(me, shards, dtypes):
    n = len(shards)

    def body(me_ref, *refs):
        for a in range(n):
            refs[n + a][...] = refs[a][...].astype(dtypes[a])

    full = lambda s: pl.BlockSpec(s.shape, lambda i, me_ref: (0,) * s.ndim)
    slot = lambda s: pl.BlockSpec((None,) + s.shape, lambda i, me_ref: (me_ref[0],) + (0,) * s.ndim)
    return pl.pallas_call(
        body, name="place_shards",
        grid_spec=pltpu.PrefetchScalarGridSpec(num_scalar_prefetch=1, grid=(1,), in_specs=[full(s) for s in shards],
                                               out_specs=[slot(s) for s in shards]),
        out_shape=[jax.ShapeDtypeStruct((N_DEV,) + s.shape, d) for s, d in zip(shards, dtypes)],
        compiler_params=_params(("arbitrary",)),
    )(me, *shards)


def _mix_in_fwd(x, g1, w_in_t, conv_w, gq, gk, gconv):
    tm = 512
    n_t = SEQ // tm

    def body(x_ref, g1_ref, w_ref, cw_ref, gq_ref, gk_ref, gc_ref,
             proj_ref, ycn_ref, qn_ref, kn_ref, v_ref, halo_ref):
        @pl.when(pl.program_id(0) == 0)
        def _():
            halo_ref[...] = jnp.zeros_like(halo_ref)

        xv = x_ref[...]
        u = (xv * _rstd(xv) * g1_ref[...]).astype(BF16)
        proj = _dot_nt(u, w_ref[...])
        proj_ref[...] = proj
        gate_b = proj[:, 0:CONV_WIDTH]
        a = proj[:, CONV_WIDTH:2 * CONV_WIDTH] * proj[:, 2 * CONV_WIDTH:3 * CONV_WIDTH]
        cv, _, _ = _conv3(a, _taps(cw_ref[...]), halo_ref[...])
        halo_ref[...] = a[tm - SUBLANES:]
        yc = gate_b * cv
        ycn_ref[...] = (yc * _rstd(yc) * gc_ref[...]).astype(BF16)
        q0 = 3 * CONV_WIDTH
        qn_ref[...] = _head_norm(proj[:, q0:q0 + ATTN_WIDTH], gq_ref[...], N_HEADS).astype(BF16)
        k0 = q0 + ATTN_WIDTH
        kn_ref[...] = _head_norm(proj[:, k0:k0 + KV_WIDTH], gk_ref[...], 2).astype(BF16)
        v_ref[...] = proj[:, k0 + KV_WIDTH:k0 + 2 * KV_WIDTH].astype(BF16)

    const = lambda shape: pl.BlockSpec(shape, lambda i: (0,) * len(shape))
    rows = lambda w: pl.BlockSpec((tm, w), lambda i: (i, 0))
    return pl.pallas_call(
        body, name="mix_in_fwd", grid=(n_t,),
        in_specs=[rows(D_MODEL), const((1, D_MODEL)), const((IN_WIDTH, D_MODEL)), const((3, CONV_WIDTH)),
                  const((1, HEAD_DIM)), const((1, HEAD_DIM)), const((1, CONV_WIDTH))],
        out_specs=[rows(IN_WIDTH), rows(CONV_WIDTH), rows(ATTN_WIDTH), rows(KV_WIDTH), rows(KV_WIDTH)],
        out_shape=[jax.ShapeDtypeStruct((SEQ, IN_WIDTH), F32), jax.ShapeDtypeStruct((SEQ, CONV_WIDTH), BF16),
                   jax.ShapeDtypeStruct((SEQ, ATTN_WIDTH), BF16), jax.ShapeDtypeStruct((SEQ, KV_WIDTH), BF16),
                   jax.ShapeDtypeStruct((SEQ, KV_WIDTH), BF16)],
        scratch_shapes=[pltpu.VMEM((SUBLANES, CONV_WIDTH), F32)],
        compiler_params=_params(("arbitrary",)),
    )(x, g1, w_in_t, conv_w, gq, gk, gconv)


GROUP_ROWS = GQA_GROUP * BLK


def _band_bias(tbl_ref, bkt, bias_ref):
    for h in range(N_HEADS):
        acc = jnp.zeros(bkt.shape, F32)
        for b in range(NUM_BUCKETS):
            acc = jnp.where(bkt == b, tbl_ref[h, b], acc)
        bias_ref[h // GQA_GROUP, BLK * (h % GQA_GROUP):BLK * (h % GQA_GROUP + 1), :] = acc


def _band_masks(i):
    qi = lax.broadcasted_iota(jnp.int32, (GROUP_ROWS, BLK), 0) & (BLK - 1)
    ji = lax.broadcasted_iota(jnp.int32, (GROUP_ROWS, BLK), 1)
    upper = ji > qi
    return upper, upper & (i == 0)


def _stack_heads(x, g):
    return jnp.concatenate([x[:, HEAD_DIM * h:HEAD_DIM * (h + 1)] for h in range(GQA_GROUP * g, GQA_GROUP * (g + 1))], axis=0)


def _unstack_heads(groups):
    return jnp.concatenate([p[BLK * t:BLK * (t + 1)] for p in groups for t in range(GQA_GROUP)], axis=-1)


def _per_head_rows(vals):
    row = lax.broadcasted_iota(jnp.int32, (GROUP_ROWS, 1), 0)
    col = jnp.full((GROUP_ROWS, 1), vals[GQA_GROUP - 1], F32)
    for t in range(GQA_GROUP - 2, -1, -1):
        col = jnp.where(row < BLK * (t + 1), vals[t], col)
    return col


def _band_rows(ref, i):
    prev = pl.multiple_of(jnp.maximum(i - 1, 0) * BLK, BLK)
    cur = pl.multiple_of(i * BLK, BLK)
    return jnp.concatenate([ref[pl.ds(prev, BLK), :], ref[pl.ds(cur, BLK), :]], axis=0), prev, cur


def _fold(band, upper):
    return jnp.where(upper, band[:, :BLK], band[:, BLK:])


def _unfold(tile, upper):
    return jnp.concatenate([jnp.where(upper, tile, 0.0), jnp.where(upper, 0.0, tile)], axis=1)


def _head_probs(qh, kh, bias, upper, dead, sink):
    logits = _fold(_dot_nt(qh, kh), upper) * (HEAD_DIM ** -0.5) + bias
    logits = jnp.where(dead, NEG_INF, logits)
    m = jnp.maximum(jnp.max(logits, axis=-1, keepdims=True), sink)
    p = jnp.exp(logits - m)
    es = jnp.exp(sink - m)
    den = jnp.sum(p, axis=-1, keepdims=True) + es
    return p / den, es / den


def _attn_fwd(qn, kn, v, tbl, sinks, bkt, gattn, after=None):
    n_b = SEQ // BLK

    def body(q_ref, k_ref, v_ref, tbl_ref, sink_ref, bkt_ref, ga_ref, y_ref, yn_ref, bias_ref):
        i = pl.program_id(0)

        @pl.when(i == 0)
        def _():
            _band_bias(tbl_ref, bkt_ref[...], bias_ref)

        kb, _, _ = _band_rows(k_ref, i)
        vb, _, _ = _band_rows(v_ref, i)
        upper, dead = _band_masks(i)
        q = q_ref[...]
        outs = []
        for g in range(N_HEADS // GQA_GROUP):
            kv = slice(HEAD_DIM * g, HEAD_DIM * (g + 1))
            sink = _per_head_rows([sink_ref[0, GQA_GROUP * g + t] for t in range(GQA_GROUP)])
            probs, _ = _head_probs(_stack_heads(q, g), kb[:, kv], bias_ref[g], upper, dead, sink)
            outs.append(_dot(_unfold(probs, upper).astype(BF16), vb[:, kv]))
        y = _unstack_heads(outs)
        y_ref[...] = y
        yn_ref[...] = (y * _rstd(y) * ga_ref[...]).astype(BF16)

    const = lambda shape: pl.BlockSpec(shape, lambda i: (0,) * len(shape))
    rows = lambda w: pl.BlockSpec((BLK, w), lambda i: (i, 0))
    smem = pl.BlockSpec(memory_space=pltpu.SMEM)
    body, more_specs, more = _ordered_behind(body, 7, after)
    return pl.pallas_call(
        body, name="attn_fwd", grid=(n_b,),
        in_specs=[rows(ATTN_WIDTH), const((SEQ, KV_WIDTH)), const((SEQ, KV_WIDTH)), smem, smem,
                  const((BLK, BLK)), const((1, ATTN_WIDTH))] + more_specs,
        out_specs=[rows(ATTN_WIDTH), rows(ATTN_WIDTH)],
        out_shape=[jax.ShapeDtypeStruct((SEQ, ATTN_WIDTH), F32), jax.ShapeDtypeStruct((SEQ, ATTN_WIDTH), BF16)],
        scratch_shapes=[pltpu.VMEM((N_HEADS // GQA_GROUP, GROUP_ROWS, BLK), F32)],
        compiler_params=_params(("arbitrary",)),
    )(qn, kn, v, tbl, sinks, bkt, gattn, *more)


def _ffn_fwd(x, ycn, yan, w_out, g2, w_up, fcw, fcb, w_down, tgt):
    tm = 512
    n_t = SEQ // tm

    def body(x_ref, ycn_ref, yan_ref, wo_ref, g2_ref, wu_ref, cw_ref, b_ref, wd_ref, tgt_ref,
             h1_ref, u2_ref, up_ref, pre_ref, act_ref, dh2_ref, dh2b_ref, loss_ref, acc_ref, halo_ref):
        i, j = pl.program_id(0), pl.program_id(1)

        @pl.when((i == 0) & (j == 0))
        def _():
            loss_ref[...] = jnp.zeros_like(loss_ref)

        @pl.when(j == 0)
        def _():
            h1 = x_ref[...] + _dot(ycn_ref[...], wo_ref[0:CONV_WIDTH, :]) + _dot(yan_ref[...], wo_ref[CONV_WIDTH:, :])
            h1_ref[...] = h1
            u2_ref[...] = (h1 * _rstd(h1) * g2_ref[...]).astype(BF16)
            acc_ref[...] = jnp.zeros_like(acc_ref)

        u2 = u2_ref[...]
        pre = []
        for s in range(2):
            up = _dot_nt(u2, wu_ref[s])
            up_ref[s] = up.astype(BF16)
            halo = jnp.where(i == 0, 0.0, halo_ref[s, j])
            pre.append(_conv3(up, _taps(cw_ref.at[s]), halo)[0] + b_ref[s])
            pre_ref[s] = pre[s].astype(BF16)
            halo_ref[s, j] = up[tm - SUBLANES:]
        g, val = pre
        act = (g * jax.nn.sigmoid(g) * val).astype(BF16)
        act_ref[...] = act
        acc_ref[...] += _dot(act, wd_ref[...])

        @pl.when(j == N_FFN_BLK - 1)
        def _():
            err = h1_ref[...] + acc_ref[...] - tgt_ref[...]
            loss_ref[...] += 0.5 * jnp.sum(err * err) / D_MODEL
            dh2 = err / D_MODEL
            dh2_ref[...] = dh2
            dh2b_ref[...] = dh2.astype(BF16)

    rows = lambda w: pl.BlockSpec((tm, w), lambda i, j: (i, 0))
    const = lambda shape: pl.BlockSpec(shape, lambda i, j: (0,) * len(shape))
    pair = lambda *s: pl.BlockSpec((2, None) + s, lambda i, j: (0, j) + (0,) * len(s))
    upb = pl.BlockSpec((2, None, tm, FFN_BLK), lambda i, j: (0, j, i, 0))
    return pl.pallas_call(
        body, name="ffn_fwd", grid=(n_t, N_FFN_BLK),
        in_specs=[rows(D_MODEL), rows(CONV_WIDTH), rows(ATTN_WIDTH), const((D_MODEL, D_MODEL)), const((1, D_MODEL)),
                  pair(FFN_BLK, D_MODEL), pair(3, 1, FFN_BLK), pair(1, FFN_BLK),
                  pl.BlockSpec((None, FFN_BLK, D_MODEL), lambda i, j: (j, 0, 0)), rows(D_MODEL)],
        out_specs=[rows(D_MODEL), rows(D_MODEL), upb, upb, pl.BlockSpec((None, tm, FFN_BLK), lambda i, j: (j, i, 0)),
                   rows(D_MODEL), rows(D_MODEL), const((SUBLANES, 128))],
        out_shape=[jax.ShapeDtypeStruct((SEQ, D_MODEL), F32), jax.ShapeDtypeStruct((SEQ, D_MODEL), BF16),
                   jax.ShapeDtypeStruct((2, N_FFN_BLK, SEQ, FFN_BLK), BF16),
                   jax.ShapeDtypeStruct((2, N_FFN_BLK, SEQ, FFN_BLK), BF16),
                   jax.ShapeDtypeStruct((N_FFN_BLK, SEQ, FFN_BLK), BF16),
                   jax.ShapeDtypeStruct((SEQ, D_MODEL), F32), jax.ShapeDtypeStruct((SEQ, D_MODEL), BF16),
                   jax.ShapeDtypeStruct((SUBLANES, 128), F32)],
        scratch_shapes=[pltpu.VMEM((tm, D_MODEL), F32), pltpu.VMEM((2, N_FFN_BLK, SUBLANES, FFN_BLK), F32)],
        compiler_params=_params(("arbitrary", "arbitrary")),
    )(x, ycn, yan, w_out, g2, w_up, fcw, fcb, w_down, tgt)


def _ffn_bwd(dh2, dh2b, h1, g2, up, pre, w_up, fcw, w_down):
    tm = 512
    n_t = SEQ // tm

    def body(dh2_ref, dh2b_ref, h1_ref, g2_ref, up_ref, pre_ref, wu_ref, cw_ref, wd_ref,
             dup_ref, dh1_ref, dh1b_ref, dfb_ref, dfcw_ref, dg2_ref, acc_ref, next_ref):
        i, j = pl.program_id(0), pl.program_id(1)

        @pl.when((i == 0) & (j == 0))
        def _():
            dfb_ref[...] = jnp.zeros_like(dfb_ref)
            dfcw_ref[...] = jnp.zeros_like(dfcw_ref)
            dg2_ref[...] = jnp.zeros_like(dg2_ref)

        @pl.when(j == 0)
        def _():
            acc_ref[...] = jnp.zeros_like(acc_ref)

        g, val = pre_ref[0].astype(F32), pre_ref[1].astype(F32)
        sg = jax.nn.sigmoid(g)
        silu = g * sg
        dact = _dot_nt(dh2b_ref[...], wd_ref[...])
        dpre = (dact * val * (sg * (1.0 + g * (1.0 - sg))), dact * silu)
        for s in range(2):
            d = dpre[s]
            u = up_ref[s].astype(F32)
            w = _taps(cw_ref.at[s])
            nxt = jnp.where(i == 0, 0.0, next_ref[s, j])
            d1 = _shift_up(d, 1, nxt)
            d2 = _shift_up(d, 2, nxt)
            next_ref[s, j] = d[:SUBLANES]
            dfb_ref[s, j] += jnp.sum(d, axis=0, keepdims=True)
            dfcw_ref[s, j, 0] += jnp.sum(d2 * u, axis=0, keepdims=True)
            dfcw_ref[s, j, 1] += jnp.sum(d1 * u, axis=0, keepdims=True)
            dfcw_ref[s, j, 2] += jnp.sum(d * u, axis=0, keepdims=True)
            dup = (d * w[2] + d1 * w[1] + d2 * w[0]).astype(BF16)
            dup_ref[s] = dup
            acc_ref[...] += _dot(dup, wu_ref[s])

        @pl.when(j == N_FFN_BLK - 1)
        def _():
            dn, dgain = _rms_bwd(h1_ref[...], g2_ref[...], acc_ref[...])
            dh1 = dh2_ref[...] + dn
            dh1_ref[...] = dh1
            dh1b_ref[...] = dh1.astype(BF16)
            dg2_ref[...] += dgain

    rev = lambda i: n_t - 1 - i
    rows = lambda w: pl.BlockSpec((tm, w), lambda i, j: (rev(i), 0))
    const = lambda shape: pl.BlockSpec(shape, lambda i, j: (0,) * len(shape))
    pair = lambda *s: pl.BlockSpec((2, None) + s, lambda i, j: (0, j) + (0,) * len(s))
    upb = pl.BlockSpec((2, None, tm, FFN_BLK), lambda i, j: (0, j, rev(i), 0))
    return pl.pallas_call(
        body, name="ffn_bwd", grid=(n_t, N_FFN_BLK),
        in_specs=[rows(D_MODEL), rows(D_MODEL), rows(D_MODEL), const((1, D_MODEL)), upb, upb,
                  pair(FFN_BLK, D_MODEL), pair(3, 1, FFN_BLK),
                  pl.BlockSpec((None, FFN_BLK, D_MODEL), lambda i, j: (j, 0, 0))],
        out_specs=[upb, rows(D_MODEL), rows(D_MODEL),
                   const((2, N_FFN_BLK, 1, FFN_BLK)), const((2, N_FFN_BLK, 3, 1, FFN_BLK)), const((1, D_MODEL))],
        out_shape=[jax.ShapeDtypeStruct((2, N_FFN_BLK, SEQ, FFN_BLK), BF16), jax.ShapeDtypeStruct((SEQ, D_MODEL), F32),
                   jax.ShapeDtypeStruct((SEQ, D_MODEL), BF16), jax.ShapeDtypeStruct((2, N_FFN_BLK, 1, FFN_BLK), F32),
                   jax.ShapeDtypeStruct((2, N_FFN_BLK, 3, 1, FFN_BLK), F32), jax.ShapeDtypeStruct((1, D_MODEL), F32)],
        scratch_shapes=[pltpu.VMEM((tm, D_MODEL), F32), pltpu.VMEM((2, N_FFN_BLK, SUBLANES, FFN_BLK), F32)],
        compiler_params=_params(("arbitrary", "arbitrary")),
    )(dh2, dh2b, h1, g2, up, pre, w_up, fcw, w_down)


def _grad_tn(a_list, b, out_rows, name, after=None):
    n = len(a_list)
    ncol = b.shape[1]

    def body(*refs):
        a_refs, b_ref, o_ref = refs[:n], refs[n], refs[n + 1]
        j = pl.program_id(0)
        for k in range(n):
            @pl.when(j == k)
            def _(k=k):
                o_ref[...] = _dot_tn(a_refs[k][...], b_ref[...]).astype(BF16)

    full = lambda shape: pl.BlockSpec(shape, lambda j: (0,) * len(shape))
    body, more_specs, more = _ordered_behind(body, n + 1, after)
    return pl.pallas_call(
        body, name=name, grid=(n,),
        in_specs=[full((SEQ, out_rows))] * n + [full((SEQ, ncol))] + more_specs,
        out_specs=pl.BlockSpec((None, out_rows, ncol), lambda j: (j, 0, 0)),
        out_shape=jax.ShapeDtypeStruct((n, out_rows, ncol), BF16),
        compiler_params=_params(("arbitrary",)),
    )(*a_list, b, *more)


def _grad_tn_blocked(a, b, name, a_is_blocked):
    nb = a.shape[0] if a_is_blocked else b.shape[0]
    a_w, b_w = a.shape[-1], b.shape[-1]

    def body(a_ref, b_ref, o_ref):
        o_ref[...] = _dot_tn(a_ref[...], b_ref[...]).astype(BF16)

    blocked = lambda w: pl.BlockSpec((None, SEQ, w), lambda k: (k, 0, 0))
    full = lambda w: pl.BlockSpec((SEQ, w), lambda k: (0, 0))
    return pl.pallas_call(
        body, name=name, grid=(nb,),
        in_specs=[blocked(a_w) if a_is_blocked else full(a_w), full(b_w) if a_is_blocked else blocked(b_w)],
        out_specs=pl.BlockSpec((None, a_w, b_w), lambda k: (k, 0, 0)),
        out_shape=jax.ShapeDtypeStruct((nb, a_w, b_w), BF16),
        compiler_params=_params(("arbitrary",)),
    )(a, b)


def _out_bwd(dh1b, w_out, y_attn, gattn, after=None):
    tm = 512
    n_t = SEQ // tm

    def body(dh_ref, wo_ref, y_ref, ga_ref, dycn_ref, dy_ref, dga_ref):
        @pl.when(pl.program_id(0) == 0)
        def _():
            dga_ref[...] = jnp.zeros_like(dga_ref)

        dycat = _dot_nt(dh_ref[...], wo_ref[...])
        dycn_ref[...] = dycat[:, :CONV_WIDTH]
        dy, dga = _rms_bwd(y_ref[...], ga_ref[...], dycat[:, CONV_WIDTH:])
        dy_ref[...] = dy
        dga_ref[...] += dga

    rows = lambda w: pl.BlockSpec((tm, w), lambda i: (i, 0))
    const = lambda shape: pl.BlockSpec(shape, lambda i: (0,) * len(shape))
    body, more_specs, more = _ordered_behind(body, 4, after)
    return pl.pallas_call(
        body, name="out_bwd", grid=(n_t,),
        in_specs=[rows(D_MODEL), const((D_MODEL, D_MODEL)), rows(ATTN_WIDTH), const((1, ATTN_WIDTH))] + more_specs,
        out_specs=[rows(CONV_WIDTH), rows(ATTN_WIDTH), const((1, ATTN_WIDTH))],
        out_shape=[jax.ShapeDtypeStruct((SEQ, CONV_WIDTH), F32), jax.ShapeDtypeStruct((SEQ, ATTN_WIDTH), F32),
                   jax.ShapeDtypeStruct((1, ATTN_WIDTH), F32)],
        compiler_params=_params(("arbitrary",)),
    )(dh1b, w_out, y_attn, gattn, *more)


def _attn_bwd(qn, kn, v, dy, tbl, sinks, bkt, after=None):
    n_b = SEQ // BLK

    def body(q_ref, k_ref, v_ref, dy_ref, tbl_ref, sink_ref, bkt_ref,
             dq_ref, dk_ref, dv_ref, dtbl_ref, dsink_ref, bias_ref, dbias_ref, dsacc_ref):
        i = pl.program_id(0)

        @pl.when(i == 0)
        def _():
            _band_bias(tbl_ref, bkt_ref[...], bias_ref)
            dbias_ref[...] = jnp.zeros_like(dbias_ref)
            dsacc_ref[...] = jnp.zeros_like(dsacc_ref)
            dk_ref[...] = jnp.zeros_like(dk_ref)
            dv_ref[...] = jnp.zeros_like(dv_ref)

        kb, prev, cur = _band_rows(k_ref, i)
        vb, _, _ = _band_rows(v_ref, i)
        upper, dead = _band_masks(i)
        q = q_ref[...]
        dy = dy_ref[...]
        dqs, dks, dvs = [], [], []
        for g in range(N_HEADS // GQA_GROUP):
            kv = slice(HEAD_DIM * g, HEAD_DIM * (g + 1))
            qg = _stack_heads(q, g)
            dog = _stack_heads(dy, g).astype(BF16)
            sink = _per_head_rows([sink_ref[0, GQA_GROUP * g + t] for t in range(GQA_GROUP)])
            probs, psink = _head_probs(qg, kb[:, kv], bias_ref[g], upper, dead, sink)
            dprobs = _fold(_dot_nt(dog, vb[:, kv]), upper)
            dvs.append(_dot_tn(_unfold(probs, upper).astype(BF16), dog))
            dsum = jnp.sum(probs * dprobs, axis=-1, keepdims=True)
            dlogits = probs * (dprobs - dsum)
            dsacc_ref[g] += jnp.broadcast_to(-psink * dsum, (GROUP_ROWS, 128))
            dbias_ref[g] += dlogits
            ds = _unfold(dlogits * (HEAD_DIM ** -0.5), upper).astype(BF16)
            dqs.append(_dot(ds, kb[:, kv]))
            dks.append(_dot_tn(ds, qg))
        dq_ref[...] = _unstack_heads(dqs)
        dkb = jnp.concatenate(dks, axis=-1)
        dvb = jnp.concatenate(dvs, axis=-1)
        dk_ref[pl.ds(prev, BLK), :] += dkb[:BLK]
        dk_ref[pl.ds(cur, BLK), :] += dkb[BLK:]
        dv_ref[pl.ds(prev, BLK), :] += dvb[:BLK]
        dv_ref[pl.ds(cur, BLK), :] += dvb[BLK:]

        @pl.when(i == n_b - 1)
        def _():
            bkt = bkt_ref[...]
            row8 = lax.broadcasted_iota(jnp.int32, (N_HEADS, 128), 0)
            lane8 = lax.broadcasted_iota(jnp.int32, (N_HEADS, 128), 1)
            lane1 = lax.broadcasted_iota(jnp.int32, (1, 128), 1)
            acc = jnp.zeros((N_HEADS, 128), F32)
            dsink = jnp.zeros((1, 128), F32)
            for h in range(N_HEADS):
                rows = slice(BLK * (h % GQA_GROUP), BLK * (h % GQA_GROUP + 1))
                dsink = jnp.where(lane1 == h, jnp.sum(dsacc_ref[h // GQA_GROUP, rows, :], axis=0, keepdims=True), dsink)
                dbh = dbias_ref[h // GQA_GROUP, rows, :]
                for b in range(NUM_BUCKETS):
                    acc = jnp.where((row8 == h) & (lane8 == b), jnp.sum(jnp.where(bkt == b, dbh, 0.0)), acc)
            dsink_ref[...] = dsink
            dtbl_ref[...] = acc

    const = lambda shape: pl.BlockSpec(shape, lambda i: (0,) * len(shape))
    rows = lambda w: pl.BlockSpec((BLK, w), lambda i: (i, 0))
    smem = pl.BlockSpec(memory_space=pltpu.SMEM)
    body, more_specs, more = _ordered_behind(body, 7, after)
    return pl.pallas_call(
        body, name="attn_bwd", grid=(n_b,),
        in_specs=[rows(ATTN_WIDTH), const((SEQ, KV_WIDTH)), const((SEQ, KV_WIDTH)), rows(ATTN_WIDTH), smem, smem,
                  const((BLK, BLK))] + more_specs,
        out_specs=[rows(ATTN_WIDTH), const((SEQ, KV_WIDTH)), const((SEQ, KV_WIDTH)), const((N_HEADS, 128)), const((1, 128))],
        out_shape=[jax.ShapeDtypeStruct((SEQ, ATTN_WIDTH), F32), jax.ShapeDtypeStruct((SEQ, KV_WIDTH), F32),
                   jax.ShapeDtypeStruct((SEQ, KV_WIDTH), F32), jax.ShapeDtypeStruct((N_HEADS, 128), F32),
                   jax.ShapeDtypeStruct((1, 128), F32)],
        scratch_shapes=[pltpu.VMEM((N_HEADS // GQA_GROUP, GROUP_ROWS, BLK), F32)] * 3,
        compiler_params=_params(("arbitrary",)),
    )(qn, kn, v, dy, tbl, sinks, bkt, *more)


def _mix_in_bwd(x, dh1, proj, dycn, dqn, dkn, dv, w_in_t, conv_w, g1, gq, gk, gconv):
    tm = 256
    n_t = SEQ // tm
    halo_blocks = tm // SUBLANES

    def body(x_ref, dh1_ref, proj_ref, halo_ref, dycn_ref, dqn_ref, dkn_ref, dv_ref, w_ref, cw_ref,
             g1_ref, gq_ref, gk_ref, gc_ref,
             dx_ref, dproj_ref, u1_ref, dcw_ref, dgc_ref, dgq_ref, dgk_ref, dg1_ref, next_ref):
        i = pl.program_id(0)
        first_tile = i == n_t - 1

        @pl.when(i == 0)
        def _():
            for r in (dcw_ref, dgc_ref, dgq_ref, dgk_ref, dg1_ref, next_ref):
                r[...] = jnp.zeros_like(r)

        proj = proj_ref[...]
        hp = halo_ref[...]
        gate_b = proj[:, 0:CONV_WIDTH]
        gate_c = proj[:, CONV_WIDTH:2 * CONV_WIDTH]
        hc = proj[:, 2 * CONV_WIDTH:3 * CONV_WIDTH]
        a = gate_c * hc
        a_halo = jnp.where(first_tile, 0.0, hp[:, CONV_WIDTH:2 * CONV_WIDTH] * hp[:, 2 * CONV_WIDTH:3 * CONV_WIDTH])
        cw = _taps(cw_ref[...])
        cv, a2, a1 = _conv3(a, cw, a_halo)
        dyc, dgc = _rms_bwd(gate_b * cv, gc_ref[...], dycn_ref[...])
        dgc_ref[...] += dgc
        dcv = dyc * gate_b
        dcw_ref[...] += jnp.concatenate(
            [jnp.sum(dcv * a2, axis=0, keepdims=True), jnp.sum(dcv * a1, axis=0, keepdims=True),
             jnp.sum(dcv * a, axis=0, keepdims=True)], axis=0)
        da = _conv3_bwd_input(dcv, cw, next_ref[...])
        next_ref[...] = dcv[:SUBLANES]
        q0 = 3 * CONV_WIDTH
        k0 = q0 + ATTN_WIDTH
        dq, dgq = _head_norm_bwd(proj[:, q0:k0], gq_ref[...], dqn_ref[...], N_HEADS)
        dk, dgk = _head_norm_bwd(proj[:, k0:k0 + KV_WIDTH], gk_ref[...], dkn_ref[...], 2)
        dgq_ref[...] += dgq
        dgk_ref[...] += dgk
        dproj = jnp.concatenate([dyc * cv, da * hc, da * gate_c, dq, dk, dv_ref[...]], axis=-1).astype(BF16)
        dproj_ref[...] = dproj
        du1 = _dot(dproj, w_ref[...])
        xv = x_ref[...]
        dn, dg1 = _rms_bwd(xv, g1_ref[...], du1)
        dx_ref[...] = dh1_ref[...] + dn
        dg1_ref[...] += dg1
        u1_ref[...] = (xv * _rstd(xv) * g1_ref[...]).astype(BF16)

    rev = lambda i: n_t - 1 - i
    rows = lambda w: pl.BlockSpec((tm, w), lambda i: (rev(i), 0))
    const = lambda shape: pl.BlockSpec(shape, lambda i: (0,) * len(shape))
    halo = pl.BlockSpec((SUBLANES, IN_WIDTH), lambda i: (jnp.maximum(rev(i) * halo_blocks - 1, 0), 0))
    return pl.pallas_call(
        body, name="mix_in_bwd", grid=(n_t,),
        in_specs=[rows(D_MODEL), rows(D_MODEL), rows(IN_WIDTH), halo, rows(CONV_WIDTH), rows(ATTN_WIDTH), rows(KV_WIDTH),
                  rows(KV_WIDTH), const((IN_WIDTH, D_MODEL)), const((3, CONV_WIDTH)), const((1, D_MODEL)),
                  const((1, HEAD_DIM)), const((1, HEAD_DIM)), const((1, CONV_WIDTH))],
        out_specs=[rows(D_MODEL), rows(IN_WIDTH), rows(D_MODEL), const((3, CONV_WIDTH)), const((1, CONV_WIDTH)),
                   const((1, HEAD_DIM)), const((1, HEAD_DIM)), const((1, D_MODEL))],
        out_shape=[jax.ShapeDtypeStruct((SEQ, D_MODEL), F32), jax.ShapeDtypeStruct((SEQ, IN_WIDTH), BF16),
                   jax.ShapeDtypeStruct((SEQ, D_MODEL), BF16), jax.ShapeDtypeStruct((3, CONV_WIDTH), F32),
                   jax.ShapeDtypeStruct((1, CONV_WIDTH), F32), jax.ShapeDtypeStruct((1, HEAD_DIM), F32),
                   jax.ShapeDtypeStruct((1, HEAD_DIM), F32), jax.ShapeDtypeStruct((1, D_MODEL), F32)],
        scratch_shapes=[pltpu.VMEM((SUBLANES, CONV_WIDTH), F32)],
        compiler_params=_params(("arbitrary",)),
    )(x, dh1, proj, proj, dycn, dqn, dkn, dv, w_in_t, conv_w, g1, gq, gk, gconv)


def _grad_w_in(dproj, u1, after=None):
    bw = 768

    def body(a_ref, b_ref, o_ref):
        o_ref[...] = _dot_tn(a_ref[...], b_ref[...]).astype(BF16)

    body, more_specs, more = _ordered_behind(body, 2, after)
    return pl.pallas_call(
        body, name="grad_w_in", grid=(IN_WIDTH // bw,),
        in_specs=[pl.BlockSpec((SEQ, bw), lambda k: (0, k)), pl.BlockSpec((SEQ, D_MODEL), lambda k: (0, 0))] + more_specs,
        out_specs=pl.BlockSpec((bw, D_MODEL), lambda k: (k, 0)),
        out_shape=jax.ShapeDtypeStruct((IN_WIDTH, D_MODEL), BF16),
        compiler_params=_params(("arbitrary",)),
    )(dproj, u1, *more)


def _adamw_math(w, g, m, v):
    m = ADAM_B1 * m + (1.0 - ADAM_B1) * g
    v = ADAM_B2 * v + (1.0 - ADAM_B2) * (g * g)
    m_hat = m / (1.0 - ADAM_B1 ** ADAM_STEP)
    v_hat = v / (1.0 - ADAM_B2 ** ADAM_STEP)
    return -ADAM_LR * (m_hat / (jnp.sqrt(v_hat) + ADAM_EPS) + ADAM_WD * w), m, v


_ROW_G1, _ROW_G2, _ROW_OUT_NORMS, _ROW_FFN_B, _ROW_GQ, _ROW_GK, _ROW_SINKS, _ROW_LOSS, _ROW_TABLE = 0, 1, 2, 3, 11, 12, 13, 14, 16
SMALL_ROWS, SMALL_COLS = 24, 1024
_SMALL_NAMES = ("norm_mix_g", "norm_ffn_g", "out_norm_conv_g", "out_norm_attn_g", "ffn_conv_b", "q_norm_g", "k_norm_g",
                "sinks", "rel_bias_table")


def _pack_small_grads(dg1, dg2, dgconv, dgattn, dfb, dgq, dgk, dsinks, dtbl_t, loss_acc):
    def body(dg1_ref, dg2_ref, dgc_ref, dga_ref, dfb_ref, dgq_ref, dgk_ref, ds_ref, dt_ref, loss_ref, o_ref):
        o_ref[...] = jnp.zeros_like(o_ref)
        o_ref[_ROW_G1:_ROW_G1 + 1, :] = dg1_ref[...]
        o_ref[_ROW_G2:_ROW_G2 + 1, :] = dg2_ref[...]
        o_ref[_ROW_OUT_NORMS:_ROW_OUT_NORMS + 1, 0:CONV_WIDTH] = dgc_ref[...]
        o_ref[_ROW_OUT_NORMS:_ROW_OUT_NORMS + 1, CONV_WIDTH:] = dga_ref[...]
        for k in range(N_DEV):
            o_ref[_ROW_FFN_B + k:_ROW_FFN_B + k + 1, 0:FFN_BLK] = dfb_ref[k // N_FFN_BLK, k % N_FFN_BLK]
        o_ref[_ROW_GQ:_ROW_GQ + 1, 0:HEAD_DIM] = dgq_ref[...]
        o_ref[_ROW_GK:_ROW_GK + 1, 0:HEAD_DIM] = dgk_ref[...]
        o_ref[_ROW_SINKS:_ROW_SINKS + 1, 0:128] = ds_ref[...]
        o_ref[_ROW_LOSS:_ROW_LOSS + 1, 0:128] = loss_ref[0:1, :]
        o_ref[_ROW_TABLE:_ROW_TABLE + N_HEADS, 0:128] = dt_ref[...]

    return pl.pallas_call(body, name="pack_small_grads", out_shape=jax.ShapeDtypeStruct((SMALL_ROWS, SMALL_COLS), F32))(
        dg1, dg2, dgconv, dgattn, dfb, dgq, dgk, dsinks, dtbl_t, loss_acc)


def _adamw_small(recv, params, after):
    names = _SMALL_NAMES
    n = len(names)

    def grad_of(g, name, k=None):
        if name == "norm_mix_g":
            return g[_ROW_G1:_ROW_G1 + 1, :]
        if name == "norm_ffn_g":
            return g[_ROW_G2:_ROW_G2 + 1, :]
        if name == "out_norm_conv_g":
            return g[_ROW_OUT_NORMS:_ROW_OUT_NORMS + 1, 0:CONV_WIDTH]
        if name == "out_norm_attn_g":
            return g[_ROW_OUT_NORMS:_ROW_OUT_NORMS + 1, CONV_WIDTH:]
        if name == "ffn_conv_b":
            return g[_ROW_FFN_B + k:_ROW_FFN_B + k + 1, 0:FFN_BLK]
        if name == "q_norm_g":
            return g[_ROW_GQ:_ROW_GQ + 1, 0:HEAD_DIM]
        if name == "k_norm_g":
            return g[_ROW_GK:_ROW_GK + 1, 0:HEAD_DIM]
        if name == "sinks":
            return g[_ROW_SINKS:_ROW_SINKS + 1, 0:N_HEADS]
        return g[_ROW_TABLE:_ROW_TABLE + N_HEADS, 0:NUM_BUCKETS]

    def body(r_ref, *refs):
        ins, outs, loss_ref = refs[:3 * n], refs[3 * n:7 * n], refs[7 * n]
        g = r_ref[0]
        for s in range(1, N_DEV):
            g = g + r_ref[s]
        loss_ref[...] = g[_ROW_LOSS:_ROW_LOSS + 1, 0:128]
        for i, name in enumerate(names):
            w_ref, m_ref, v_ref = ins[3 * i:3 * i + 3]
            o = outs[4 * i:4 * i + 4]
            cols = [slice(FFN_BLK * k, FFN_BLK * (k + 1)) for k in range(N_DEV)] if name == "ffn_conv_b" else [slice(None)]
            for k, cs in enumerate(cols):
                gk = grad_of(g, name, k)
                d, m2, v2 = _adamw_math(w_ref[:, cs], gk, m_ref[:, cs], v_ref[:, cs])
                o[0][:, cs], o[1][:, cs], o[2][:, cs], o[3][:, cs] = gk, d, m2, v2

    flat = [a for name in names for a in params[name]]
    body, more_specs, more = _ordered_behind(body, 1 + 3 * n, after)
    vmem = pl.BlockSpec(memory_space=pltpu.VMEM)
    out = pl.pallas_call(
        body, name="adamw_small",
        in_specs=[vmem] * (1 + 3 * n) + more_specs,
        out_shape=[jax.ShapeDtypeStruct(params[name][0].shape, F32) for name in names for _ in range(4)]
        + [jax.ShapeDtypeStruct((1, 128), F32)],
        compiler_params=pltpu.CompilerParams(vmem_limit_bytes=VMEM_LIMIT),
    )(recv, *flat, *more)
    return {name: tuple(out[4 * i:4 * i + 4]) for i, name in enumerate(names)}, out[4 * n]


def _adamw(w, m, v, part, recv, chip, name, row_blocks=1, after=None):
    rb = w.shape[0] // row_blocks
    tail = w.shape[1:]
    zeros = (0,) * len(tail)

    def body(chip_ref, w_ref, m_ref, v_ref, p_ref, r_ref, g_o, d_o, m_o, v_o):
        g = p_ref[...].astype(F32)
        for s in range(3):
            g = g + r_ref[s].astype(F32)
        g_o[...] = g
        d_o[...], m_o[...], v_o[...] = _adamw_math(w_ref[...], g, m_ref[...], v_ref[...])

    blk = pl.BlockSpec((rb,) + tail, lambda i, chip_ref: (i,) + zeros)
    pblk = pl.BlockSpec((None, rb) + tail, lambda i, chip_ref: (chip_ref[0], i) + zeros)
    rblk = pl.BlockSpec((3, rb) + tail, lambda i, chip_ref: (0, i) + zeros)
    body, more_specs, more = _ordered_behind(body, 6, after)
    return pl.pallas_call(
        body, name=name,
        grid_spec=pltpu.PrefetchScalarGridSpec(num_scalar_prefetch=1, grid=(row_blocks,),
                                               in_specs=[blk, blk, blk, pblk, rblk] + more_specs, out_specs=[blk] * 4),
        out_shape=[jax.ShapeDtypeStruct(w.shape, F32)] * 4,
        compiler_params=_params(("arbitrary",)),
    )(chip, w, m, v, part, recv, *more)


def kernel(x, norm_mix_g, w_in, conv_w, q_norm_g, k_norm_g, rel_bias_table, sinks, out_norm_conv_g, out_norm_attn_g, w_out, norm_ffn_g, w_up, ffn_conv_w, ffn_conv_b, w_down, loss_target, m_norm_mix_g, m_w_in, m_conv_w, m_q_norm_g, m_k_norm_g, m_rel_bias_table, m_sinks, m_out_norm_conv_g, m_out_norm_attn_g, m_w_out, m_norm_ffn_g, m_w_up, m_ffn_conv_w, m_ffn_conv_b, m_w_down, v_norm_mix_g, v_w_in, v_conv_w, v_q_norm_g, v_k_norm_g, v_rel_bias_table, v_sinks, v_out_norm_conv_g, v_out_norm_attn_g, v_w_out, v_norm_ffn_g, v_w_up, v_ffn_conv_w, v_ffn_conv_b, v_w_down):
    p = dict(norm_mix_g=norm_mix_g, w_in=w_in, conv_w=conv_w, q_norm_g=q_norm_g, k_norm_g=k_norm_g,
             rel_bias_table=rel_bias_table, sinks=sinks, out_norm_conv_g=out_norm_conv_g, out_norm_attn_g=out_norm_attn_g,
             w_out=w_out, norm_ffn_g=norm_ffn_g, w_up=w_up, ffn_conv_w=ffn_conv_w, ffn_conv_b=ffn_conv_b, w_down=w_down)
    m = dict(norm_mix_g=m_norm_mix_g, w_in=m_w_in, conv_w=m_conv_w, q_norm_g=m_q_norm_g, k_norm_g=m_k_norm_g,
             rel_bias_table=m_rel_bias_table, sinks=m_sinks, out_norm_conv_g=m_out_norm_conv_g,
             out_norm_attn_g=m_out_norm_attn_g, w_out=m_w_out, norm_ffn_g=m_norm_ffn_g, w_up=m_w_up,
             ffn_conv_w=m_ffn_conv_w, ffn_conv_b=m_ffn_conv_b, w_down=m_w_down)
    v = dict(norm_mix_g=v_norm_mix_g, w_in=v_w_in, conv_w=v_conv_w, q_norm_g=v_q_norm_g, k_norm_g=v_k_norm_g,
             rel_bias_table=v_rel_bias_table, sinks=v_sinks, out_norm_conv_g=v_out_norm_conv_g,
             out_norm_attn_g=v_out_norm_attn_g, w_out=v_w_out, norm_ffn_g=v_norm_ffn_g, w_up=v_w_up,
             ffn_conv_w=v_ffn_conv_w, ffn_conv_b=v_ffn_conv_b, w_down=v_w_down)

    xs, tgt = x[0], loss_target[0]
    g1, g2, gq, gk, gconv, gattn = norm_mix_g, norm_ffn_g, q_norm_g, k_norm_g, out_norm_conv_g, out_norm_attn_g
    ix, iy, ic = _coords()
    core = ic.astype(jnp.int32).reshape(1)
    chip = (2 * ix + iy).astype(jnp.int32).reshape(1)
    me = _lin(ix, iy, ic).astype(jnp.int32).reshape(1)
    bkt = jnp.asarray(_bucket_map())
    tr = lambda a: a[0].T
    taps = lambda a: jnp.transpose(a, (1, 0, 2))
    tbl_t = rel_bias_table.T

    wi_l, cw_l, wo_l, wu_l, wd_l, fcw_l = _place_shards(
        me, [tr(w_in), taps(conv_w), w_out[0], tr(w_up), w_down[0], taps(ffn_conv_w)], [BF16, F32, BF16, BF16, BF16, F32])
    finish_a, token_a = _all_gather_split([wi_l, cw_l], "mixer", None)
    ffn_stage2, ffn_stage3, token_b = _all_gather_tree([wo_l, wu_l, wd_l, fcw_l], "ffn", token_a)
    wi_g, cw_g = finish_a(token_b)
    w_in_t = wi_g.reshape(IN_WIDTH, D_MODEL)
    conv_w_f = jnp.transpose(cw_g[:, :, 0, :], (1, 0, 2)).reshape(3, CONV_WIDTH)

    proj, ycn, qn, kn, vv = _mix_in_fwd(xs, g1, w_in_t, conv_w_f, gq, gk, gconv)
    token_b2 = ffn_stage2(ycn)
    y_attn, yan = _attn_fwd(qn, kn, vv, tbl_t, sinks, bkt, gattn, after=token_b2)
    wo_g, wu_g, wd_g, fcw_g = ffn_stage3(yan)
    w_out_f = wo_g.reshape(D_MODEL, D_MODEL)
    w_down_f = wd_g.reshape(N_FFN_BLK, FFN_BLK, D_MODEL)
    w_up_f = wu_g.reshape(2, N_FFN_BLK, FFN_BLK, D_MODEL)
    fcw_f = fcw_g.reshape(2, N_FFN_BLK, 3, 1, FFN_BLK)
    fcb = ffn_conv_b.reshape(2, N_FFN_BLK, 1, FFN_BLK)
    h1, u2, up, pre, act, dh2, dh2b, loss_acc = _ffn_fwd(xs, ycn, yan, w_out_f, g2, w_up_f, fcw_f, fcb, w_down_f, tgt)

    dup, dh1, dh1b, dfb, dfcw, dg2 = _ffn_bwd(dh2, dh2b, h1, g2, up, pre, w_up_f, fcw_f, w_down_f)
    dw_down = _grad_tn_blocked(act, dh2b, "grad_w_down", a_is_blocked=True).reshape(N_DEV, D_FF // N_DEV, D_MODEL)
    dw_up = _grad_tn_blocked(dup.reshape(N_DEV, SEQ, FFN_BLK), u2, "grad_w_up", a_is_blocked=True)
    dw_out = _grad_tn([ycn, yan], dh1b, CONV_WIDTH, "grad_w_out").reshape(N_DEV, D_MODEL // N_DEV, D_MODEL)
    out_bwd = {}

    def behind_ffn(token):
        out_bwd["r"] = _out_bwd(dh1b, w_out_f, y_attn, gattn, after=token)
        return out_bwd["r"][0]

    finish_ffn, token_ffn = _reduce_scatter_split(
        [dw_down, dw_up, dw_out, dfcw.reshape(N_DEV, 3, 1, FFN_BLK)], "ffn", core, behind_ffn)
    dycn, dy_attn, dgattn = out_bwd["r"]
    dqn, dkn, dv, dtbl_t, dsinks = _attn_bwd(qn, kn, vv, dy_attn, tbl_t, sinks, bkt, after=token_ffn)
    dx, dproj, u1, dcw, dgconv, dgq, dgk, dg1 = _mix_in_bwd(xs, dh1, proj, dycn, dqn, dkn, dv, w_in_t, conv_w_f,
                                                             g1, gq, gk, gconv)
    packed = _pack_small_grads(dg1, dg2, dgconv, dgattn, dfb, dgq, dgk, dsinks, dtbl_t, loss_acc)
    plan_s, slots_s = _broadcast_plan()
    s_sem, r_sem, src_s, land_s, token_s = _split_start(
        "gather_small_start", [packed], [jnp.broadcast_to(packed[None], (N_DEV,) + packed.shape)], plan_s, None)
    dw_in_t = _grad_w_in(dproj, u1, after=token_s).reshape(N_DEV, IN_WIDTH // N_DEV, D_MODEL)
    dcw_b = jnp.transpose(dcw.reshape(3, N_DEV, 1, CONV_WIDTH // N_DEV), (1, 0, 2, 3))
    adam = {}
    ffn_got = {}

    def behind_mixer(token):
        ffn_got["r"] = finish_ffn(token)
        return ffn_got["r"][1][0]

    finish_mixer, token_mixer = _reduce_scatter_split([dw_in_t, dcw_b], "mixer", core, behind_mixer)
    (p_wd, p_wu, p_wo, p_fcw), (r_wd, r_wu, r_wo, r_fcw) = ffn_got["r"]
    adam["w_down"] = _adamw(w_down[0], m_w_down[0], v_w_down[0], p_wd, r_wd, chip, "adamw_w_down", row_blocks=2,
                            after=token_mixer)
    adam_up = _adamw(tr(w_up), tr(m_w_up), tr(v_w_up), p_wu, r_wu, chip, "adamw_w_up", row_blocks=4,
                     after=adam["w_down"][0])
    adam["w_out"] = _adamw(w_out[0], m_w_out[0], v_w_out[0], p_wo, r_wo, chip, "adamw_w_out", after=adam_up[0])
    adam_fcw = _adamw(taps(ffn_conv_w), taps(m_ffn_conv_w), taps(v_ffn_conv_w), p_fcw, r_fcw, chip, "adamw_ffn_conv_w",
                      after=adam["w_out"][0])
    _, (r_small,) = _split_wait("gather_small_wait", s_sem, r_sem, src_s, land_s, plan_s, slots_s, adam_fcw[0])
    small_in = {k: (p[k], m[k], v[k]) for k in _SMALL_NAMES}
    small_in["rel_bias_table"] = (tbl_t, m_rel_bias_table.T, v_rel_bias_table.T)
    small_out, loss_row = _adamw_small(r_small, small_in, None)
    (p_wi, p_cw), (r_wi, r_cw) = finish_mixer(loss_row)
    adam_in = _adamw(tr(w_in), tr(m_w_in), tr(v_w_in), p_wi, r_wi, chip, "adamw_w_in")
    adam_cw = _adamw(taps(conv_w), taps(m_conv_w), taps(v_conv_w), p_cw, r_cw, chip, "adamw_conv_w")

    res = {k: tuple(a[None] for a in t) for k, t in adam.items()}
    res["w_up"] = tuple(a.T[None] for a in adam_up)
    res["w_in"] = tuple(a.T[None] for a in adam_in)
    res["ffn_conv_w"] = tuple(taps(a) for a in adam_fcw)
    res["conv_w"] = tuple(taps(a) for a in adam_cw)
    res.update(small_out)
    res["rel_bias_table"] = tuple(a.T for a in small_out["rel_bias_table"])
    loss = loss_row[0, 0]
    order = ("norm_mix_g", "w_in", "conv_w", "q_norm_g", "k_norm_g", "rel_bias_table", "sinks", "out_norm_conv_g",
             "out_norm_attn_g", "w_out", "norm_ffn_g", "w_up", "ffn_conv_w", "ffn_conv_b", "w_down")
    return (loss, dx[None], *[res[k][0] for k in order], *[res[k][1] for k in order],
            *[res[k][2] for k in order], *[res[k][3] for k in order])
```

```python
import functools
import math

import numpy as np
import jax
import jax.numpy as jnp
from jax import lax
from jax.experimental import pallas as pl
from jax.experimental.pallas import tpu as pltpu

F32 = jnp.float32
BF16 = jnp.bfloat16

SEQ = 2048
D_MODEL = 1024
CONV_WIDTH = 512
ATTN_WIDTH = 512
KV_WIDTH = 128
HEAD_DIM = 64
N_HEADS = 8
GQA_GROUP = 4
IN_WIDTH = 2304
D_FF = 2816
BLK = 128
NUM_BUCKETS = 32
EPS = 1e-6
NEG_INF = -1e30
ADAM_LR = 0.001
ADAM_B1 = 0.9
ADAM_B2 = 0.999
ADAM_EPS = 1e-08
ADAM_WD = 0.01
ADAM_STEP = 10

N_DEV = 8
FFN_BLK = 2 * D_FF // N_DEV
N_FFN_BLK = D_FF // FFN_BLK
SUBLANES = 8
VMEM_LIMIT = 56 * 1024 * 1024

_MESH = pl.DeviceIdType.MESH
_ANY = pl.BlockSpec(memory_space=pl.ANY)


def _params(sem):
    return pltpu.CompilerParams(dimension_semantics=sem, vmem_limit_bytes=VMEM_LIMIT)


def _ordered_behind(body, pos, after):
    if after is None:
        return body, [], []
    return (lambda *refs: body(*refs[:pos], *refs[pos + 1:])), [_ANY], [after]


def _dot(a, b):
    return jnp.dot(a, b, preferred_element_type=F32)


def _dot_nt(a, b):
    return lax.dot_general(a, b, (((1,), (1,)), ((), ())), preferred_element_type=F32)


def _dot_tn(a, b):
    return lax.dot_general(a, b, (((0,), (0,)), ((), ())), preferred_element_type=F32)


def _shift_down(x, s, halo):
    r = pltpu.roll(x, s, axis=0)
    hr = pltpu.roll(halo, s, axis=0)
    row = lax.broadcasted_iota(jnp.int32, halo.shape, 0)
    top = jnp.where(row < s, hr, r[:SUBLANES])
    return jnp.concatenate([top, r[SUBLANES:]], axis=0)


def _shift_up(x, s, halo):
    n = x.shape[0]
    r = pltpu.roll(x, n - s, axis=0)
    hr = pltpu.roll(halo, SUBLANES - s, axis=0)
    row = lax.broadcasted_iota(jnp.int32, halo.shape, 0)
    bot = jnp.where(row >= SUBLANES - s, hr, r[n - SUBLANES:])
    return jnp.concatenate([r[:n - SUBLANES], bot], axis=0)


def _taps(w):
    return (w[0], w[1], w[2]) if len(w.shape) == 3 else (w[0:1], w[1:2], w[2:3])


def _conv3(x, w, halo):
    x2 = _shift_down(x, 2, halo)
    x1 = _shift_down(x, 1, halo)
    return x2 * w[0] + x1 * w[1] + x * w[2], x2, x1


def _conv3_bwd_input(dy, w, halo_next):
    return dy * w[2] + _shift_up(dy, 1, halo_next) * w[1] + _shift_up(dy, 2, halo_next) * w[0]


def _rstd(x):
    return lax.rsqrt(jnp.mean(x * x, axis=-1, keepdims=True) + EPS)


def _rms_bwd(x, g, dy):
    r = _rstd(x)
    n = x * r
    dn = dy * g
    dx = r * (dn - n * jnp.mean(dn * n, axis=-1, keepdims=True))
    return dx, jnp.sum(dy * n, axis=0, keepdims=True)


def _head_norm(x, g, heads):
    parts = []
    for h in range(heads):
        xh = x[:, HEAD_DIM * h:HEAD_DIM * (h + 1)]
        parts.append(xh * _rstd(xh) * g)
    return jnp.concatenate(parts, axis=-1)


def _head_norm_bwd(x, g, dy, heads):
    dxs, dg = [], jnp.zeros((1, HEAD_DIM), F32)
    for h in range(heads):
        sl = slice(HEAD_DIM * h, HEAD_DIM * (h + 1))
        dxh, dgh = _rms_bwd(x[:, sl], g, dy[:, sl])
        dxs.append(dxh)
        dg = dg + dgh
    return jnp.concatenate(dxs, axis=-1), dg


def _bucket_map():
    q = np.arange(BLK)[:, None]
    j = np.arange(BLK)[None, :]
    n = np.where(j > q, q + BLK - j, q - j)
    nf = np.maximum(n, 1).astype(np.float32)
    max_exact = NUM_BUCKETS // 2
    large = max_exact + (np.log(nf / max_exact) / math.log(BLK / max_exact) * (NUM_BUCKETS - max_exact)).astype(np.int32)
    large = np.minimum(large, NUM_BUCKETS - 1)
    return np.where(n < max_exact, n, large).astype(np.int32)


def _coords():
    return lax.axis_index("x"), lax.axis_index("y"), lax.axis_index("c")


def _lin(px, py, pc):
    return 4 * px + 2 * py + pc


_HBM = pl.BlockSpec(memory_space=pltpu.HBM)
_SEM = pl.BlockSpec(memory_space=pltpu.SEMAPHORE)
_EFFECT = pltpu.SideEffectType.DATAFLOW_SIDE_EFFECTING


def _in_hbm(a):
    return pltpu.with_memory_space_constraint(a, pltpu.HBM)


def _split_start(name, srcs, lands, plan, after):
    ns, nl = len(srcs), len(lands)
    n_copies = len(plan(0, 0, 0))
    n_after = 0 if after is None else 1

    def body(*refs):
        src_refs, land_refs = refs[:ns + nl], refs[ns:ns + nl]
        send_sems, recv_sems = refs[ns + nl + n_after], refs[ns + nl + n_after + 1]
        token = refs[-1]
        for k, (a, s_slot, l, d_slot, dev) in enumerate(plan(*_coords())):
            src = src_refs[a] if s_slot is None else src_refs[a].at[s_slot]
            pltpu.make_async_remote_copy(src_ref=src, dst_ref=land_refs[l].at[d_slot], send_sem=send_sems.at[k],
                                         recv_sem=recv_sems.at[k], device_id=dev, device_id_type=_MESH).start()
        token[...] = jnp.zeros_like(token)

    arrs = list(srcs) + list(lands)
    out = pl.pallas_call(
        body, name=name,
        out_shape=(pltpu.SemaphoreType.DMA((n_copies,)), pltpu.SemaphoreType.DMA((n_copies,)),
                   *[pltpu.HBM(a.shape, a.dtype) for a in arrs], jax.ShapeDtypeStruct((SUBLANES, 128), F32)),
        in_specs=[_HBM] * (ns + nl) + [_ANY] * n_after,
        out_specs=(_SEM, _SEM, *[_HBM] * (ns + nl), pl.BlockSpec(memory_space=pltpu.VMEM)),
        input_output_aliases={i: 2 + i for i in range(ns + nl)},
        compiler_params=pltpu.CompilerParams(has_side_effects=_EFFECT),
    )(*[_in_hbm(a) for a in arrs], *([] if after is None else [after]))
    return out[0], out[1], list(out[2:2 + ns]), list(out[2 + ns:2 + ns + nl]), out[-1]


def _split_wait(name, send_sems, recv_sems, srcs, lands, plan, recv_slots, after):
    ns, nl = len(srcs), len(lands)

    def body(*refs):
        src_refs, land_refs = refs[:ns + nl], refs[ns:ns + nl]
        send_sems, recv_sems = refs[ns + nl], refs[ns + nl + 1]
        coords = _coords()
        slots = recv_slots(*coords)
        for k, (a, s_slot, l, _, dev) in enumerate(plan(*coords)):
            src = src_refs[a] if s_slot is None else src_refs[a].at[s_slot]
            cp = pltpu.make_async_remote_copy(src_ref=src, dst_ref=land_refs[l].at[slots[k]], send_sem=send_sems.at[k],
                                              recv_sem=recv_sems.at[k], device_id=dev, device_id_type=_MESH)
            cp.wait_send()
            cp.wait_recv()

    arrs = list(srcs) + list(lands)
    out = pl.pallas_call(
        body, name=name,
        out_shape=tuple(pltpu.HBM(a.shape, a.dtype) for a in arrs),
        in_specs=[_HBM] * (ns + nl) + [_SEM, _SEM, _ANY],
        out_specs=tuple([_HBM] * (ns + nl)),
        input_output_aliases={i: i for i in range(ns + nl)},
        compiler_params=pltpu.CompilerParams(has_side_effects=_EFFECT),
    )(*arrs, send_sems, recv_sems, after)
    return list(out[:ns]), list(out[ns:])


def _chips(x, y):
    return [(1 - x, y), (x, 1 - y), (1 - x, 1 - y)]


def _gather_plan_ici(n):
    def plan(x, y, c):
        me = _lin(x, y, c)
        out = []
        for a in range(n):
            out.append((a, me, a, me, (x, y, 1 - c)))
            out += [(a, me, a, me, (cx, cy, c)) for cx, cy in _chips(x, y)]
        return out

    def recv_slots(x, y, c):
        out = []
        for _ in range(n):
            out.append(_lin(x, y, 1 - c))
            out += [_lin(cx, cy, c) for cx, cy in _chips(x, y)]
        return out

    return plan, recv_slots


def _gather_plan_d2d(n):
    def plan(x, y, c):
        return [(a, _lin(cx, cy, c), a, _lin(cx, cy, c), (x, y, 1 - c)) for a in range(n) for cx, cy in _chips(x, y)]

    def recv_slots(x, y, c):
        return [_lin(cx, cy, 1 - c) for _ in range(n) for cx, cy in _chips(x, y)]

    return plan, recv_slots


def _all_gather_split(lands, tag, after):
    n = len(lands)
    plan1, slots1 = _gather_plan_ici(n)
    s1, r1, _, lands, token = _split_start(f"gather_{tag}_ici_start", [], lands, plan1, after)

    def finish(after):
        _, got = _split_wait(f"gather_{tag}_ici_wait", s1, r1, [], lands, plan1, slots1, after)
        plan2, slots2 = _gather_plan_d2d(n)
        s2, r2, _, got, token2 = _split_start(f"gather_{tag}_d2d_start", [], got, plan2, None)
        return _split_wait(f"gather_{tag}_d2d_wait", s2, r2, [], got, plan2, slots2, token2)[1]

    return finish, token


def _all_gather_tree(lands, tag, after):
    n = len(lands)

    def plan1(x, y, c):
        me = _lin(x, y, c)
        return [(a, me, a, me, dev) for a in range(n) for dev in ((x, y, 1 - c), (1 - x, y, c), (x, 1 - y, c))]

    def slots1(x, y, c):
        return [s for _ in range(n) for s in (_lin(x, y, 1 - c), _lin(1 - x, y, c), _lin(x, 1 - y, c))]

    def plan2(x, y, c):
        from_x, from_y = _lin(1 - x, y, c), _lin(x, 1 - y, c)
        north = c == 1
        passed = jnp.where(north, from_x, from_y)
        onward = (jnp.where(north, x, 1 - x), jnp.where(north, 1 - y, y), c)
        sib = (x, y, 1 - c)
        return [cp for a in range(n) for cp in ((a, passed, a, passed, onward), (a, from_x, a, from_x, sib),
                                                (a, from_y, a, from_y, sib))]

    def slots2(x, y, c):
        return [s for _ in range(n) for s in (_lin(1 - x, 1 - y, c), _lin(1 - x, y, 1 - c), _lin(x, 1 - y, 1 - c))]

    def plan3(x, y, c):
        diag = _lin(1 - x, 1 - y, c)
        return [(a, diag, a, diag, (x, y, 1 - c)) for a in range(n)]

    def slots3(x, y, c):
        return [_lin(1 - x, 1 - y, 1 - c)] * n

    s1, r1, _, lands, token = _split_start(f"gather_{tag}_1_start", [], lands, plan1, after)
    state = {}

    def stage2(after):
        _, got = _split_wait(f"gather_{tag}_1_wait", s1, r1, [], lands, plan1, slots1, after)
        state["s"], state["r"], _, state["lands"], token2 = _split_start(f"gather_{tag}_2_start", [], got, plan2, None)
        return token2

    def stage3(after):
        _, got = _split_wait(f"gather_{tag}_2_wait", state["s"], state["r"], [], state["lands"], plan2, slots2, after)
        s3, r3, _, got, token3 = _split_start(f"gather_{tag}_3_start", [], got, plan3, None)
        return _split_wait(f"gather_{tag}_3_wait", s3, r3, [], got, plan3, slots3, token3)[1]

    return stage2, stage3, token


_CHIP_LIST = ((0, 0), (0, 1), (1, 0), (1, 1))


def _reduce_plan_d2d(n):
    def plan(x, y, c):
        return [(a, _lin(qx, qy, 1 - c), a, q, (x, y, 1 - c)) for a in range(n) for q, (qx, qy) in enumerate(_CHIP_LIST)]

    def recv_slots(x, y, c):
        return [q for _ in range(n) for q in range(4)]

    return plan, recv_slots


def _reduce_plan_ici(n):
    def plan(x, y, c):
        return [(a, 2 * cx + cy, a, j, (cx, cy, c)) for a in range(n) for j, (cx, cy) in enumerate(_chips(x, y))]

    def recv_slots(x, y, c):
        return [j for _ in range(n) for j in range(3)]

    return plan, recv_slots


def _broadcast_plan():
    def peers(x, y, c):
        return [(1 - x if r & 4 else x, 1 - y if r & 2 else y, 1 - c if r & 1 else c) for r in range(1, N_DEV)]

    def plan(x, y, c):
        return [(0, None, 0, _lin(x, y, c), peer) for peer in peers(x, y, c)]

    def recv_slots(x, y, c):
        return [_lin(*peer) for peer in peers(x, y, c)]

    return plan, recv_slots


def _chip_partial(grads, recvd, core, name):
    n = len(grads)

    def body(c_ref, *refs):
        for a in range(n):
            g_ref, r_ref, o_ref = refs[a], refs[n + a], refs[2 * n + a]
            o_ref[...] = (g_ref[...].astype(F32) + r_ref[...].astype(F32)).astype(o_ref.dtype)

    def blk(a, own):
        zeros = (0,) * (a.ndim - 1)
        return pl.BlockSpec((None,) + a.shape[1:],
                            (lambda q, c_ref: (2 * q + c_ref[0],) + zeros) if own else (lambda q, c_ref: (q,) + zeros))

    return pl.pallas_call(
        body, name=name,
        grid_spec=pltpu.PrefetchScalarGridSpec(
            num_scalar_prefetch=1, grid=(4,),
            in_specs=[blk(a, True) for a in grads] + [blk(a, False) for a in recvd],
            out_specs=[blk(a, False) for a in recvd]),
        out_shape=[jax.ShapeDtypeStruct(a.shape, a.dtype) for a in recvd],
        compiler_params=_params(("arbitrary",)),
    )(core, *grads, *recvd)


def _reduce_scatter_split(grads, tag, core, behind):
    n = len(grads)
    plan1, slots1 = _reduce_plan_d2d(n)
    lands1 = [lax.empty((4,) + a.shape[1:], a.dtype) for a in grads]
    s1, r1, srcs1, lands1, token1 = _split_start(f"reduce_{tag}_d2d_start", grads, lands1, plan1, None)
    own, got = _split_wait(f"reduce_{tag}_d2d_wait", s1, r1, srcs1, lands1, plan1, slots1, behind(token1))
    parts = _chip_partial(own, got, core, f"reduce_{tag}_partial")
    plan2, slots2 = _reduce_plan_ici(n)
    lands2 = [lax.empty((3,) + a.shape[1:], a.dtype) for a in grads]
    s2, r2, srcs2, lands2, token2 = _split_start(f"reduce_{tag}_ici_start", parts, lands2, plan2, None)

    def finish(after):
        return _split_wait(f"reduce_{tag}_ici_wait", s2, r2, srcs2, lands2, plan2, slots2, after)

    return finish, token2


def _place_shards(me, shards, dtypes):
    n = len(shards)

    def body(me_ref, *refs):
        for a in range(n):
            refs[n + a][...] = refs[a][...].astype(dtypes[a])

    full = lambda s: pl.BlockSpec(s.shape, lambda i, me_ref: (0,) * s.ndim)
    slot = lambda s: pl.BlockSpec((None,) + s.shape, lambda i, me_ref: (me_ref[0],) + (0,) * s.ndim)
    return pl.pallas_call(
        body, name="place_shards",
        grid_spec=pltpu.PrefetchScalarGridSpec(num_scalar_prefetch=1, grid=(1,), in_specs=[full(s) for s in shards],
                                               out_specs=[slot(s) for s in shards]),
        out_shape=[jax.ShapeDtypeStruct((N_DEV,) + s.shape, d) for s, d in zip(shards, dtypes)],
        compiler_params=_params(("arbitrary",)),
    )(me, *shards)


def _mix_in_fwd(x, g1, w_in_t, conv_w, gq, gk, gconv):
    tm = 512
    n_t = SEQ // tm

    def body(x_ref, g1_ref, w_ref, cw_ref, gq_ref, gk_ref, gc_ref,
             proj_ref, ycn_ref, qn_ref, kn_ref, v_ref, halo_ref):
        @pl.when(pl.program_id(0) == 0)
        def _():
            halo_ref[...] = jnp.zeros_like(halo_ref)

        xv = x_ref[...]
        u = (xv * _rstd(xv) * g1_ref[...]).astype(BF16)
        proj = _dot_nt(u, w_ref[...])
        proj_ref[...] = proj
        gate_b = proj[:, 0:CONV_WIDTH]
        a = proj[:, CONV_WIDTH:2 * CONV_WIDTH] * proj[:, 2 * CONV_WIDTH:3 * CONV_WIDTH]
        cv, _, _ = _conv3(a, _taps(cw_ref[...]), halo_ref[...])
        halo_ref[...] = a[tm - SUBLANES:]
        yc = gate_b * cv
        ycn_ref[...] = (yc * _rstd(yc) * gc_ref[...]).astype(BF16)
        q0 = 3 * CONV_WIDTH
        qn_ref[...] = _head_norm(proj[:, q0:q0 + ATTN_WIDTH], gq_ref[...], N_HEADS).astype(BF16)
        k0 = q0 + ATTN_WIDTH
        kn_ref[...] = _head_norm(proj[:, k0:k0 + KV_WIDTH], gk_ref[...], 2).astype(BF16)
        v_ref[...] = proj[:, k0 + KV_WIDTH:k0 + 2 * KV_WIDTH].astype(BF16)

    const = lambda shape: pl.BlockSpec(shape, lambda i: (0,) * len(shape))
    rows = lambda w: pl.BlockSpec((tm, w), lambda i: (i, 0))
    return pl.pallas_call(
        body, name="mix_in_fwd", grid=(n_t,),
        in_specs=[rows(D_MODEL), const((1, D_MODEL)), const((IN_WIDTH, D_MODEL)), const((3, CONV_WIDTH)),
                  const((1, HEAD_DIM)), const((1, HEAD_DIM)), const((1, CONV_WIDTH))],
        out_specs=[rows(IN_WIDTH), rows(CONV_WIDTH), rows(ATTN_WIDTH), rows(KV_WIDTH), rows(KV_WIDTH)],
        out_shape=[jax.ShapeDtypeStruct((SEQ, IN_WIDTH), F32), jax.ShapeDtypeStruct((SEQ, CONV_WIDTH), BF16),
                   jax.ShapeDtypeStruct((SEQ, ATTN_WIDTH), BF16), jax.ShapeDtypeStruct((SEQ, KV_WIDTH), BF16),
                   jax.ShapeDtypeStruct((SEQ, KV_WIDTH), BF16)],
        scratch_shapes=[pltpu.VMEM((SUBLANES, CONV_WIDTH), F32)],
        compiler_params=_params(("arbitrary",)),
    )(x, g1, w_in_t, conv_w, gq, gk, gconv)


GROUP_ROWS = GQA_GROUP * BLK


def _band_bias(tbl_ref, bkt, bias_ref):
    for h in range(N_HEADS):
        acc = jnp.zeros(bkt.shape, F32)
        for b in range(NUM_BUCKETS):
            acc = jnp.where(bkt == b, tbl_ref[h, b], acc)
        bias_ref[h // GQA_GROUP, BLK * (h % GQA_GROUP):BLK * (h % GQA_GROUP + 1), :] = acc


def _band_masks(i):
    qi = lax.broadcasted_iota(jnp.int32, (GROUP_ROWS, BLK), 0) & (BLK - 1)
    ji = lax.broadcasted_iota(jnp.int32, (GROUP_ROWS, BLK), 1)
    upper = ji > qi
    return upper, upper & (i == 0)


def _stack_heads(x, g):
    return jnp.concatenate([x[:, HEAD_DIM * h:HEAD_DIM * (h + 1)] for h in range(GQA_GROUP * g, GQA_GROUP * (g + 1))], axis=0)


def _unstack_heads(groups):
    return jnp.concatenate([p[BLK * t:BLK * (t + 1)] for p in groups for t in range(GQA_GROUP)], axis=-1)


def _per_head_rows(vals):
    row = lax.broadcasted_iota(jnp.int32, (GROUP_ROWS, 1), 0)
    col = jnp.full((GROUP_ROWS, 1), vals[GQA_GROUP - 1], F32)
    for t in range(GQA_GROUP - 2, -1, -1):
        col = jnp.where(row < BLK * (t + 1), vals[t], col)
    return col


def _band_rows(ref, i):
    prev = pl.multiple_of(jnp.maximum(i - 1, 0) * BLK, BLK)
    cur = pl.multiple_of(i * BLK, BLK)
    return jnp.concatenate([ref[pl.ds(prev, BLK), :], ref[pl.ds(cur, BLK), :]], axis=0), prev, cur


def _fold(band, upper):
    return jnp.where(upper, band[:, :BLK], band[:, BLK:])


def _unfold(tile, upper):
    return jnp.concatenate([jnp.where(upper, tile, 0.0), jnp.where(upper, 0.0, tile)], axis=1)


def _head_probs(qh, kh, bias, upper, dead, sink):
    logits = _fold(_dot_nt(qh, kh), upper) * (HEAD_DIM ** -0.5) + bias
    logits = jnp.where(dead, NEG_INF, logits)
    m = jnp.maximum(jnp.max(logits, axis=-1, keepdims=True), sink)
    p = jnp.exp(logits - m)
    es = jnp.exp(sink - m)
    den = jnp.sum(p, axis=-1, keepdims=True) + es
    return p / den, es / den


def _attn_fwd(qn, kn, v, tbl, sinks, bkt, gattn, after=None):
    n_b = SEQ // BLK

    def body(q_ref, k_ref, v_ref, tbl_ref, sink_ref, bkt_ref, ga_ref, y_ref, yn_ref, bias_ref):
        i = pl.program_id(0)

        @pl.when(i == 0)
        def _():
            _band_bias(tbl_ref, bkt_ref[...], bias_ref)

        kb, _, _ = _band_rows(k_ref, i)
        vb, _, _ = _band_rows(v_ref, i)
        upper, dead = _band_masks(i)
        q = q_ref[...]
        outs = []
        for g in range(N_HEADS // GQA_GROUP):
            kv = slice(HEAD_DIM * g, HEAD_DIM * (g + 1))
            sink = _per_head_rows([sink_ref[0, GQA_GROUP * g + t] for t in range(GQA_GROUP)])
            probs, _ = _head_probs(_stack_heads(q, g), kb[:, kv], bias_ref[g], upper, dead, sink)
            outs.append(_dot(_unfold(probs, upper).astype(BF16), vb[:, kv]))
        y = _unstack_heads(outs)
        y_ref[...] = y
        yn_ref[...] = (y * _rstd(y) * ga_ref[...]).astype(BF16)

    const = lambda shape: pl.BlockSpec(shape, lambda i: (0,) * len(shape))
    rows = lambda w: pl.BlockSpec((BLK, w), lambda i: (i, 0))
    smem = pl.BlockSpec(memory_space=pltpu.SMEM)
    body, more_specs, more = _ordered_behind(body, 7, after)
    return pl.pallas_call(
        body, name="attn_fwd", grid=(n_b,),
        in_specs=[rows(ATTN_WIDTH), const((SEQ, KV_WIDTH)), const((SEQ, KV_WIDTH)), smem, smem,
                  const((BLK, BLK)), const((1, ATTN_WIDTH))] + more_specs,
        out_specs=[rows(ATTN_WIDTH), rows(ATTN_WIDTH)],
        out_shape=[jax.ShapeDtypeStruct((SEQ, ATTN_WIDTH), F32), jax.ShapeDtypeStruct((SEQ, ATTN_WIDTH), BF16)],
        scratch_shapes=[pltpu.VMEM((N_HEADS // GQA_GROUP, GROUP_ROWS, BLK), F32)],
        compiler_params=_params(("arbitrary",)),
    )(qn, kn, v, tbl, sinks, bkt, gattn, *more)


def _out_proj(x, ycn, yan, w_out, g2):
    tm = 512

    def body(x_ref, ycn_ref, yan_ref, wo_ref, g2_ref, h1_ref, u2_ref):
        h1 = x_ref[...] + _dot(ycn_ref[...], wo_ref[0:CONV_WIDTH, :]) + _dot(yan_ref[...], wo_ref[CONV_WIDTH:, :])
        h1_ref[...] = h1
        u2_ref[...] = (h1 * _rstd(h1) * g2_ref[...]).astype(BF16)

    rows = lambda w: pl.BlockSpec((tm, w), lambda i: (i, 0))
    const = lambda shape: pl.BlockSpec(shape, lambda i: (0,) * len(shape))
    return pl.pallas_call(
        body, name="out_proj", grid=(SEQ // tm,),
        in_specs=[rows(D_MODEL), rows(CONV_WIDTH), rows(ATTN_WIDTH), const((D_MODEL, D_MODEL)), const((1, D_MODEL))],
        out_specs=[rows(D_MODEL), rows(D_MODEL)],
        out_shape=[jax.ShapeDtypeStruct((SEQ, D_MODEL), F32), jax.ShapeDtypeStruct((SEQ, D_MODEL), BF16)],
        compiler_params=_params(("arbitrary",)),
    )(x, ycn, yan, w_out, g2)


def _ffn_fwd(h1, u2, w_up, fcw, fcb, w_down, tgt):
    tm = 512
    n_t = SEQ // tm
    last = N_FFN_BLK - 1

    def body(u2_ref, wu_ref, cw_ref, b_ref, wd_ref, h1_ref, tgt_ref,
             up_ref, pre_ref, act_ref, dh2_ref, dh2b_ref, loss_ref, acc_ref, halo_ref):
        j, i = pl.program_id(0), pl.program_id(1)
        rows = pl.ds(pl.multiple_of(i * tm, tm), tm)

        @pl.when((i == 0) & (j == 0))
        def _():
            loss_ref[...] = jnp.zeros_like(loss_ref)

        u2 = u2_ref[...]
        pre = []
        for s in range(2):
            up = _dot_nt(u2, wu_ref[s])
            up_ref[s] = up.astype(BF16)
            halo = jnp.where(i == 0, 0.0, halo_ref[s])
            pre.append(_conv3(up, _taps(cw_ref.at[s]), halo)[0] + b_ref[s])
            pre_ref[s] = pre[s].astype(BF16)
            halo_ref[s] = up[tm - SUBLANES:]
        g, val = pre
        act = (g * jax.nn.sigmoid(g) * val).astype(BF16)
        act_ref[...] = act
        out = _dot(act, wd_ref[...])

        @pl.when(j == 0)
        def _():
            acc_ref[rows, :] = out

        @pl.when(j > 0)
        def _():
            acc_ref[rows, :] += out

        @pl.when(j == last)
        def _():
            err = h1_ref[...] + acc_ref[rows, :] - tgt_ref[...]
            loss_ref[...] += 0.5 * jnp.sum(err * err) / D_MODEL
            dh2 = err / D_MODEL
            dh2_ref[...] = dh2
            dh2b_ref[...] = dh2.astype(BF16)

    late = lambda w: pl.BlockSpec((tm, w), lambda j, i: (jnp.where(j == last, i, 0), 0))
    pair = lambda *s: pl.BlockSpec((2, None) + s, lambda j, i: (0, j) + (0,) * len(s))
    upb = pl.BlockSpec((2, None, tm, FFN_BLK), lambda j, i: (0, j, i, 0))
    return pl.pallas_call(
        body, name="ffn_fwd", grid=(N_FFN_BLK, n_t),
        in_specs=[pl.BlockSpec((tm, D_MODEL), lambda j, i: (i, 0)), pair(FFN_BLK, D_MODEL), pair(3, 1, FFN_BLK),
                  pair(1, FFN_BLK), pl.BlockSpec((None, FFN_BLK, D_MODEL), lambda j, i: (j, 0, 0)),
                  late(D_MODEL), late(D_MODEL)],
        out_specs=[upb, upb, pl.BlockSpec((None, tm, FFN_BLK), lambda j, i: (j, i, 0)), late(D_MODEL), late(D_MODEL),
                   pl.BlockSpec((SUBLANES, 128), lambda j, i: (0, 0))],
        out_shape=[jax.ShapeDtypeStruct((2, N_FFN_BLK, SEQ, FFN_BLK), BF16),
                   jax.ShapeDtypeStruct((2, N_FFN_BLK, SEQ, FFN_BLK), BF16),
                   jax.ShapeDtypeStruct((N_FFN_BLK, SEQ, FFN_BLK), BF16),
                   jax.ShapeDtypeStruct((SEQ, D_MODEL), F32), jax.ShapeDtypeStruct((SEQ, D_MODEL), BF16),
                   jax.ShapeDtypeStruct((SUBLANES, 128), F32)],
        scratch_shapes=[pltpu.VMEM((SEQ, D_MODEL), F32), pltpu.VMEM((2, SUBLANES, FFN_BLK), F32)],
        compiler_params=_params(("arbitrary", "arbitrary")),
    )(u2, w_up, fcw, fcb, w_down, h1, tgt)


def _ffn_fwd_token_major(x, ycn, yan, w_out, g2, w_up, fcw, fcb, w_down, tgt):
    tm = 512
    n_t = SEQ // tm

    def body(x_ref, ycn_ref, yan_ref, wo_ref, g2_ref, wu_ref, cw_ref, b_ref, wd_ref, tgt_ref,
             h1_ref, u2_ref, up_ref, pre_ref, act_ref, dh2_ref, dh2b_ref, loss_ref, acc_ref, halo_ref):
        i, j = pl.program_id(0), pl.program_id(1)

        @pl.when((i == 0) & (j == 0))
        def _():
            loss_ref[...] = jnp.zeros_like(loss_ref)

        @pl.when(j == 0)
        def _():
            h1 = x_ref[...] + _dot(ycn_ref[...], wo_ref[0:CONV_WIDTH, :]) + _dot(yan_ref[...], wo_ref[CONV_WIDTH:, :])
            h1_ref[...] = h1
            u2_ref[...] = (h1 * _rstd(h1) * g2_ref[...]).astype(BF16)
            acc_ref[...] = jnp.zeros_like(acc_ref)

        u2 = u2_ref[...]
        pre = []
        for s in range(2):
            up = _dot_nt(u2, wu_ref[s])
            up_ref[s] = up.astype(BF16)
            halo = jnp.where(i == 0, 0.0, halo_ref[s, j])
            pre.append(_conv3(up, _taps(cw_ref.at[s]), halo)[0] + b_ref[s])
            pre_ref[s] = pre[s].astype(BF16)
            halo_ref[s, j] = up[tm - SUBLANES:]
        g, val = pre
        act = (g * jax.nn.sigmoid(g) * val).astype(BF16)
        act_ref[...] = act
        acc_ref[...] += _dot(act, wd_ref[...])

        @pl.when(j == N_FFN_BLK - 1)
        def _():
            err = h1_ref[...] + acc_ref[...] - tgt_ref[...]
            loss_ref[...] += 0.5 * jnp.sum(err * err) / D_MODEL
            dh2 = err / D_MODEL
            dh2_ref[...] = dh2
            dh2b_ref[...] = dh2.astype(BF16)

    rows = lambda w: pl.BlockSpec((tm, w), lambda i, j: (i, 0))
    const = lambda shape: pl.BlockSpec(shape, lambda i, j: (0,) * len(shape))
    pair = lambda *s: pl.BlockSpec((2, None) + s, lambda i, j: (0, j) + (0,) * len(s))
    upb = pl.BlockSpec((2, None, tm, FFN_BLK), lambda i, j: (0, j, i, 0))
    return pl.pallas_call(
        body, name="ffn_fwd", grid=(n_t, N_FFN_BLK),
        in_specs=[rows(D_MODEL), rows(CONV_WIDTH), rows(ATTN_WIDTH), const((D_MODEL, D_MODEL)), const((1, D_MODEL)),
                  pair(FFN_BLK, D_MODEL), pair(3, 1, FFN_BLK), pair(1, FFN_BLK),
                  pl.BlockSpec((None, FFN_BLK, D_MODEL), lambda i, j: (j, 0, 0)), rows(D_MODEL)],
        out_specs=[rows(D_MODEL), rows(D_MODEL), upb, upb, pl.BlockSpec((None, tm, FFN_BLK), lambda i, j: (j, i, 0)),
                   rows(D_MODEL), rows(D_MODEL), const((SUBLANES, 128))],
        out_shape=[jax.ShapeDtypeStruct((SEQ, D_MODEL), F32), jax.ShapeDtypeStruct((SEQ, D_MODEL), BF16),
                   jax.ShapeDtypeStruct((2, N_FFN_BLK, SEQ, FFN_BLK), BF16),
                   jax.ShapeDtypeStruct((2, N_FFN_BLK, SEQ, FFN_BLK), BF16),
                   jax.ShapeDtypeStruct((N_FFN_BLK, SEQ, FFN_BLK), BF16),
                   jax.ShapeDtypeStruct((SEQ, D_MODEL), F32), jax.ShapeDtypeStruct((SEQ, D_MODEL), BF16),
                   jax.ShapeDtypeStruct((SUBLANES, 128), F32)],
        scratch_shapes=[pltpu.VMEM((tm, D_MODEL), F32), pltpu.VMEM((2, N_FFN_BLK, SUBLANES, FFN_BLK), F32)],
        compiler_params=_params(("arbitrary", "arbitrary")),
    )(x, ycn, yan, w_out, g2, w_up, fcw, fcb, w_down, tgt)


def _ffn_bwd(dh2, dh2b, h1, g2, up, pre, w_up, fcw, w_down):
    tm = 256
    n_t = SEQ // tm
    last = N_FFN_BLK - 1

    def body(dh2b_ref, up_ref, pre_ref, wu_ref, cw_ref, wd_ref, dh2_ref, h1_ref, g2_ref,
             dup_ref, dh1_ref, dh1b_ref, dfb_ref, dfcw_ref, dg2_ref, acc_ref, next_ref):
        j, i = pl.program_id(0), pl.program_id(1)
        rows = pl.ds(pl.multiple_of((n_t - 1 - i) * tm, tm), tm)

        @pl.when((i == 0) & (j == 0))
        def _():
            dfb_ref[...] = jnp.zeros_like(dfb_ref)
            dfcw_ref[...] = jnp.zeros_like(dfcw_ref)
            dg2_ref[...] = jnp.zeros_like(dg2_ref)

        g, val = pre_ref[0].astype(F32), pre_ref[1].astype(F32)
        sg = jax.nn.sigmoid(g)
        silu = g * sg
        dact = _dot_nt(dh2b_ref[...], wd_ref[...])
        dpre = (dact * val * (sg * (1.0 + g * (1.0 - sg))), dact * silu)
        du = None
        for s in range(2):
            d = dpre[s]
            u = up_ref[s].astype(F32)
            w = _taps(cw_ref.at[s])
            nxt = jnp.where(i == 0, 0.0, next_ref[s])
            d1 = _shift_up(d, 1, nxt)
            d2 = _shift_up(d, 2, nxt)
            next_ref[s] = d[:SUBLANES]
            dfb_ref[s, j] += jnp.sum(d, axis=0, keepdims=True)
            dfcw_ref[s, j, 0] += jnp.sum(d2 * u, axis=0, keepdims=True)
            dfcw_ref[s, j, 1] += jnp.sum(d1 * u, axis=0, keepdims=True)
            dfcw_ref[s, j, 2] += jnp.sum(d * u, axis=0, keepdims=True)
            dup = (d * w[2] + d1 * w[1] + d2 * w[0]).astype(BF16)
            dup_ref[s] = dup
            part = _dot(dup, wu_ref[s])
            du = part if du is None else du + part

        @pl.when(j == 0)
        def _():
            acc_ref[rows, :] = du

        @pl.when(j > 0)
        def _():
            acc_ref[rows, :] += du

        @pl.when(j == last)
        def _():
            dn, dgain = _rms_bwd(h1_ref[...], g2_ref[...], acc_ref[rows, :])
            dh1 = dh2_ref[...] + dn
            dh1_ref[...] = dh1
            dh1b_ref[...] = dh1.astype(BF16)
            dg2_ref[...] += dgain

    rev = lambda i: n_t - 1 - i
    const = lambda shape: pl.BlockSpec(shape, lambda j, i: (0,) * len(shape))
    late = lambda w: pl.BlockSpec((tm, w), lambda j, i: (jnp.where(j == last, rev(i), n_t - 1), 0))
    pair = lambda *s: pl.BlockSpec((2, None) + s, lambda j, i: (0, j) + (0,) * len(s))
    upb = pl.BlockSpec((2, None, tm, FFN_BLK), lambda j, i: (0, j, rev(i), 0))
    return pl.pallas_call(
        body, name="ffn_bwd", grid=(N_FFN_BLK, n_t),
        in_specs=[pl.BlockSpec((tm, D_MODEL), lambda j, i: (rev(i), 0)), upb, upb, pair(FFN_BLK, D_MODEL),
                  pair(3, 1, FFN_BLK), pl.BlockSpec((None, FFN_BLK, D_MODEL), lambda j, i: (j, 0, 0)),
                  late(D_MODEL), late(D_MODEL), const((1, D_MODEL))],
        out_specs=[upb, late(D_MODEL), late(D_MODEL),
                   const((2, N_FFN_BLK, 1, FFN_BLK)), const((2, N_FFN_BLK, 3, 1, FFN_BLK)), const((1, D_MODEL))],
        out_shape=[jax.ShapeDtypeStruct((2, N_FFN_BLK, SEQ, FFN_BLK), BF16), jax.ShapeDtypeStruct((SEQ, D_MODEL), F32),
                   jax.ShapeDtypeStruct((SEQ, D_MODEL), BF16), jax.ShapeDtypeStruct((2, N_FFN_BLK, 1, FFN_BLK), F32),
                   jax.ShapeDtypeStruct((2, N_FFN_BLK, 3, 1, FFN_BLK), F32), jax.ShapeDtypeStruct((1, D_MODEL), F32)],
        scratch_shapes=[pltpu.VMEM((SEQ, D_MODEL), F32), pltpu.VMEM((2, SUBLANES, FFN_BLK), F32)],
        compiler_params=_params(("arbitrary", "arbitrary")),
    )(dh2b, up, pre, w_up, fcw, w_down, dh2, h1, g2)


def _ffn_bwd_token_major(dh2, dh2b, h1, g2, up, pre, w_up, fcw, w_down):
    tm = 512
    n_t = SEQ // tm

    def body(dh2_ref, dh2b_ref, h1_ref, g2_ref, up_ref, pre_ref, wu_ref, cw_ref, wd_ref,
             dup_ref, dh1_ref, dh1b_ref, dfb_ref, dfcw_ref, dg2_ref, acc_ref, next_ref):
        i, j = pl.program_id(0), pl.program_id(1)

        @pl.when((i == 0) & (j == 0))
        def _():
            dfb_ref[...] = jnp.zeros_like(dfb_ref)
            dfcw_ref[...] = jnp.zeros_like(dfcw_ref)
            dg2_ref[...] = jnp.zeros_like(dg2_ref)

        @pl.when(j == 0)
        def _():
            acc_ref[...] = jnp.zeros_like(acc_ref)

        g, val = pre_ref[0].astype(F32), pre_ref[1].astype(F32)
        sg = jax.nn.sigmoid(g)
        silu = g * sg
        dact = _dot_nt(dh2b_ref[...], wd_ref[...])
        dpre = (dact * val * (sg * (1.0 + g * (1.0 - sg))), dact * silu)
        for s in range(2):
            d = dpre[s]
            u = up_ref[s].astype(F32)
            w = _taps(cw_ref.at[s])
            nxt = jnp.where(i == 0, 0.0, next_ref[s, j])
            d1 = _shift_up(d, 1, nxt)
            d2 = _shift_up(d, 2, nxt)
            next_ref[s, j] = d[:SUBLANES]
            dfb_ref[s, j] += jnp.sum(d, axis=0, keepdims=True)
            dfcw_ref[s, j, 0] += jnp.sum(d2 * u, axis=0, keepdims=True)
            dfcw_ref[s, j, 1] += jnp.sum(d1 * u, axis=0, keepdims=True)
            dfcw_ref[s, j, 2] += jnp.sum(d * u, axis=0, keepdims=True)
            dup = (d * w[2] + d1 * w[1] + d2 * w[0]).astype(BF16)
            dup_ref[s] = dup
            acc_ref[...] += _dot(dup, wu_ref[s])

        @pl.when(j == N_FFN_BLK - 1)
        def _():
            dn, dgain = _rms_bwd(h1_ref[...], g2_ref[...], acc_ref[...])
            dh1 = dh2_ref[...] + dn
            dh1_ref[...] = dh1
            dh1b_ref[...] = dh1.astype(BF16)
            dg2_ref[...] += dgain

    rev = lambda i: n_t - 1 - i
    rows = lambda w: pl.BlockSpec((tm, w), lambda i, j: (rev(i), 0))
    const = lambda shape: pl.BlockSpec(shape, lambda i, j: (0,) * len(shape))
    pair = lambda *s: pl.BlockSpec((2, None) + s, lambda i, j: (0, j) + (0,) * len(s))
    upb = pl.BlockSpec((2, None, tm, FFN_BLK), lambda i, j: (0, j, rev(i), 0))
    return pl.pallas_call(
        body, name="ffn_bwd", grid=(n_t, N_FFN_BLK),
        in_specs=[rows(D_MODEL), rows(D_MODEL), rows(D_MODEL), const((1, D_MODEL)), upb, upb,
                  pair(FFN_BLK, D_MODEL), pair(3, 1, FFN_BLK),
                  pl.BlockSpec((None, FFN_BLK, D_MODEL), lambda i, j: (j, 0, 0))],
        out_specs=[upb, rows(D_MODEL), rows(D_MODEL),
                   const((2, N_FFN_BLK, 1, FFN_BLK)), const((2, N_FFN_BLK, 3, 1, FFN_BLK)), const((1, D_MODEL))],
        out_shape=[jax.ShapeDtypeStruct((2, N_FFN_BLK, SEQ, FFN_BLK), BF16), jax.ShapeDtypeStruct((SEQ, D_MODEL), F32),
                   jax.ShapeDtypeStruct((SEQ, D_MODEL), BF16), jax.ShapeDtypeStruct((2, N_FFN_BLK, 1, FFN_BLK), F32),
                   jax.ShapeDtypeStruct((2, N_FFN_BLK, 3, 1, FFN_BLK), F32), jax.ShapeDtypeStruct((1, D_MODEL), F32)],
        scratch_shapes=[pltpu.VMEM((tm, D_MODEL), F32), pltpu.VMEM((2, N_FFN_BLK, SUBLANES, FFN_BLK), F32)],
        compiler_params=_params(("arbitrary", "arbitrary")),
    )(dh2, dh2b, h1, g2, up, pre, w_up, fcw, w_down)


def _grad_tn(a_list, b, out_rows, name, after=None):
    n = len(a_list)
    ncol = b.shape[1]

    def body(*refs):
        a_refs, b_ref, o_ref = refs[:n], refs[n], refs[n + 1]
        j = pl.program_id(0)
        for k in range(n):
            @pl.when(j == k)
            def _(k=k):
                o_ref[...] = _dot_tn(a_refs[k][...], b_ref[...]).astype(BF16)

    full = lambda shape: pl.BlockSpec(shape, lambda j: (0,) * len(shape))
    body, more_specs, more = _ordered_behind(body, n + 1, after)
    return pl.pallas_call(
        body, name=name, grid=(n,),
        in_specs=[full((SEQ, out_rows))] * n + [full((SEQ, ncol))] + more_specs,
        out_specs=pl.BlockSpec((None, out_rows, ncol), lambda j: (j, 0, 0)),
        out_shape=jax.ShapeDtypeStruct((n, out_rows, ncol), BF16),
        compiler_params=_params(("arbitrary",)),
    )(*a_list, b, *more)


def _grad_tn_blocked(a, b, name, a_is_blocked):
    nb = a.shape[0] if a_is_blocked else b.shape[0]
    a_w, b_w = a.shape[-1], b.shape[-1]

    def body(a_ref, b_ref, o_ref):
        o_ref[...] = _dot_tn(a_ref[...], b_ref[...]).astype(BF16)

    blocked = lambda w: pl.BlockSpec((None, SEQ, w), lambda k: (k, 0, 0))
    full = lambda w: pl.BlockSpec((SEQ, w), lambda k: (0, 0))
    return pl.pallas_call(
        body, name=name, grid=(nb,),
        in_specs=[blocked(a_w) if a_is_blocked else full(a_w), full(b_w) if a_is_blocked else blocked(b_w)],
        out_specs=pl.BlockSpec((None, a_w, b_w), lambda k: (k, 0, 0)),
        out_shape=jax.ShapeDtypeStruct((nb, a_w, b_w), BF16),
        compiler_params=_params(("arbitrary",)),
    )(a, b)


def _out_bwd(dh1b, w_out, y_attn, gattn, after=None):
    tm = 512
    n_t = SEQ // tm

    def body(dh_ref, wo_ref, y_ref, ga_ref, dycn_ref, dy_ref, dga_ref):
        @pl.when(pl.program_id(0) == 0)
        def _():
            dga_ref[...] = jnp.zeros_like(dga_ref)

        dycat = _dot_nt(dh_ref[...], wo_ref[...])
        dycn_ref[...] = dycat[:, :CONV_WIDTH]
        dy, dga = _rms_bwd(y_ref[...], ga_ref[...], dycat[:, CONV_WIDTH:])
        dy_ref[...] = dy
        dga_ref[...] += dga

    rows = lambda w: pl.BlockSpec((tm, w), lambda i: (i, 0))
    const = lambda shape: pl.BlockSpec(shape, lambda i: (0,) * len(shape))
    body, more_specs, more = _ordered_behind(body, 4, after)
    return pl.pallas_call(
        body, name="out_bwd", grid=(n_t,),
        in_specs=[rows(D_MODEL), const((D_MODEL, D_MODEL)), rows(ATTN_WIDTH), const((1, ATTN_WIDTH))] + more_specs,
        out_specs=[rows(CONV_WIDTH), rows(ATTN_WIDTH), const((1, ATTN_WIDTH))],
        out_shape=[jax.ShapeDtypeStruct((SEQ, CONV_WIDTH), F32), jax.ShapeDtypeStruct((SEQ, ATTN_WIDTH), F32),
                   jax.ShapeDtypeStruct((1, ATTN_WIDTH), F32)],
        compiler_params=_params(("arbitrary",)),
    )(dh1b, w_out, y_attn, gattn, *more)


def _attn_bwd(qn, kn, v, dy, tbl, sinks, bkt, after=None):
    n_b = SEQ // BLK

    def body(q_ref, k_ref, v_ref, dy_ref, tbl_ref, sink_ref, bkt_ref,
             dq_ref, dk_ref, dv_ref, dtbl_ref, dsink_ref, bias_ref, dbias_ref, dsacc_ref):
        i = pl.program_id(0)

        @pl.when(i == 0)
        def _():
            _band_bias(tbl_ref, bkt_ref[...], bias_ref)
            dbias_ref[...] = jnp.zeros_like(dbias_ref)
            dsacc_ref[...] = jnp.zeros_like(dsacc_ref)
            dk_ref[...] = jnp.zeros_like(dk_ref)
            dv_ref[...] = jnp.zeros_like(dv_ref)

        kb, prev, cur = _band_rows(k_ref, i)
        vb, _, _ = _band_rows(v_ref, i)
        upper, dead = _band_masks(i)
        q = q_ref[...]
        dy = dy_ref[...]
        dqs, dks, dvs = [], [], []
        for g in range(N_HEADS // GQA_GROUP):
            kv = slice(HEAD_DIM * g, HEAD_DIM * (g + 1))
            qg = _stack_heads(q, g)
            dog = _stack_heads(dy, g).astype(BF16)
            sink = _per_head_rows([sink_ref[0, GQA_GROUP * g + t] for t in range(GQA_GROUP)])
            probs, psink = _head_probs(qg, kb[:, kv], bias_ref[g], upper, dead, sink)
            dprobs = _fold(_dot_nt(dog, vb[:, kv]), upper)
            dvs.append(_dot_tn(_unfold(probs, upper).astype(BF16), dog))
            dsum = jnp.sum(probs * dprobs, axis=-1, keepdims=True)
            dlogits = probs * (dprobs - dsum)
            dsacc_ref[g] += jnp.broadcast_to(-psink * dsum, (GROUP_ROWS, 128))
            dbias_ref[g] += dlogits
            ds = _unfold(dlogits * (HEAD_DIM ** -0.5), upper).astype(BF16)
            dqs.append(_dot(ds, kb[:, kv]))
            dks.append(_dot_tn(ds, qg))
        dq_ref[...] = _unstack_heads(dqs)
        dkb = jnp.concatenate(dks, axis=-1)
        dvb = jnp.concatenate(dvs, axis=-1)
        dk_ref[pl.ds(prev, BLK), :] += dkb[:BLK]
        dk_ref[pl.ds(cur, BLK), :] += dkb[BLK:]
        dv_ref[pl.ds(prev, BLK), :] += dvb[:BLK]
        dv_ref[pl.ds(cur, BLK), :] += dvb[BLK:]

        @pl.when(i == n_b - 1)
        def _():
            bkt = bkt_ref[...]
            row8 = lax.broadcasted_iota(jnp.int32, (N_HEADS, 128), 0)
            lane8 = lax.broadcasted_iota(jnp.int32, (N_HEADS, 128), 1)
            lane1 = lax.broadcasted_iota(jnp.int32, (1, 128), 1)
            acc = jnp.zeros((N_HEADS, 128), F32)
            dsink = jnp.zeros((1, 128), F32)
            for h in range(N_HEADS):
                rows = slice(BLK * (h % GQA_GROUP), BLK * (h % GQA_GROUP + 1))
                dsink = jnp.where(lane1 == h, jnp.sum(dsacc_ref[h // GQA_GROUP, rows, :], axis=0, keepdims=True), dsink)
                dbh = dbias_ref[h // GQA_GROUP, rows, :]
                for b in range(NUM_BUCKETS):
                    acc = jnp.where((row8 == h) & (lane8 == b), jnp.sum(jnp.where(bkt == b, dbh, 0.0)), acc)
            dsink_ref[...] = dsink
            dtbl_ref[...] = acc

    const = lambda shape: pl.BlockSpec(shape, lambda i: (0,) * len(shape))
    rows = lambda w: pl.BlockSpec((BLK, w), lambda i: (i, 0))
    smem = pl.BlockSpec(memory_space=pltpu.SMEM)
    body, more_specs, more = _ordered_behind(body, 7, after)
    return pl.pallas_call(
        body, name="attn_bwd", grid=(n_b,),
        in_specs=[rows(ATTN_WIDTH), const((SEQ, KV_WIDTH)), const((SEQ, KV_WIDTH)), rows(ATTN_WIDTH), smem, smem,
                  const((BLK, BLK))] + more_specs,
        out_specs=[rows(ATTN_WIDTH), const((SEQ, KV_WIDTH)), const((SEQ, KV_WIDTH)), const((N_HEADS, 128)), const((1, 128))],
        out_shape=[jax.ShapeDtypeStruct((SEQ, ATTN_WIDTH), F32), jax.ShapeDtypeStruct((SEQ, KV_WIDTH), F32),
                   jax.ShapeDtypeStruct((SEQ, KV_WIDTH), F32), jax.ShapeDtypeStruct((N_HEADS, 128), F32),
                   jax.ShapeDtypeStruct((1, 128), F32)],
        scratch_shapes=[pltpu.VMEM((N_HEADS // GQA_GROUP, GROUP_ROWS, BLK), F32)] * 3,
        compiler_params=_params(("arbitrary",)),
    )(qn, kn, v, dy, tbl, sinks, bkt, *more)


def _mix_in_bwd(x, dh1, proj, dycn, dqn, dkn, dv, w_in_t, conv_w, g1, gq, gk, gconv):
    tm = 256
    n_t = SEQ // tm
    halo_blocks = tm // SUBLANES

    def body(x_ref, dh1_ref, proj_ref, halo_ref, dycn_ref, dqn_ref, dkn_ref, dv_ref, w_ref, cw_ref,
             g1_ref, gq_ref, gk_ref, gc_ref,
             dx_ref, dproj_ref, u1_ref, dcw_ref, dgc_ref, dgq_ref, dgk_ref, dg1_ref, next_ref):
        i = pl.program_id(0)
        first_tile = i == n_t - 1

        @pl.when(i == 0)
        def _():
            for r in (dcw_ref, dgc_ref, dgq_ref, dgk_ref, dg1_ref, next_ref):
                r[...] = jnp.zeros_like(r)

        proj = proj_ref[...]
        hp = halo_ref[...]
        gate_b = proj[:, 0:CONV_WIDTH]
        gate_c = proj[:, CONV_WIDTH:2 * CONV_WIDTH]
        hc = proj[:, 2 * CONV_WIDTH:3 * CONV_WIDTH]
        a = gate_c * hc
        a_halo = jnp.where(first_tile, 0.0, hp[:, CONV_WIDTH:2 * CONV_WIDTH] * hp[:, 2 * CONV_WIDTH:3 * CONV_WIDTH])
        cw = _taps(cw_ref[...])
        cv, a2, a1 = _conv3(a, cw, a_halo)
        dyc, dgc = _rms_bwd(gate_b * cv, gc_ref[...], dycn_ref[...])
        dgc_ref[...] += dgc
        dcv = dyc * gate_b
        dcw_ref[...] += jnp.concatenate(
            [jnp.sum(dcv * a2, axis=0, keepdims=True), jnp.sum(dcv * a1, axis=0, keepdims=True),
             jnp.sum(dcv * a, axis=0, keepdims=True)], axis=0)
        da = _conv3_bwd_input(dcv, cw, next_ref[...])
        next_ref[...] = dcv[:SUBLANES]
        q0 = 3 * CONV_WIDTH
        k0 = q0 + ATTN_WIDTH
        dq, dgq = _head_norm_bwd(proj[:, q0:k0], gq_ref[...], dqn_ref[...], N_HEADS)
        dk, dgk = _head_norm_bwd(proj[:, k0:k0 + KV_WIDTH], gk_ref[...], dkn_ref[...], 2)
        dgq_ref[...] += dgq
        dgk_ref[...] += dgk
        dproj = jnp.concatenate([dyc * cv, da * hc, da * gate_c, dq, dk, dv_ref[...]], axis=-1).astype(BF16)
        dproj_ref[...] = dproj
        du1 = _dot(dproj, w_ref[...])
        xv = x_ref[...]
        dn, dg1 = _rms_bwd(xv, g1_ref[...], du1)
        dx_ref[...] = dh1_ref[...] + dn
        dg1_ref[...] += dg1
        u1_ref[...] = (xv * _rstd(xv) * g1_ref[...]).astype(BF16)

    rev = lambda i: n_t - 1 - i
    rows = lambda w: pl.BlockSpec((tm, w), lambda i: (rev(i), 0))
    const = lambda shape: pl.BlockSpec(shape, lambda i: (0,) * len(shape))
    halo = pl.BlockSpec((SUBLANES, IN_WIDTH), lambda i: (jnp.maximum(rev(i) * halo_blocks - 1, 0), 0))
    return pl.pallas_call(
        body, name="mix_in_bwd", grid=(n_t,),
        in_specs=[rows(D_MODEL), rows(D_MODEL), rows(IN_WIDTH), halo, rows(CONV_WIDTH), rows(ATTN_WIDTH), rows(KV_WIDTH),
                  rows(KV_WIDTH), const((IN_WIDTH, D_MODEL)), const((3, CONV_WIDTH)), const((1, D_MODEL)),
                  const((1, HEAD_DIM)), const((1, HEAD_DIM)), const((1, CONV_WIDTH))],
        out_specs=[rows(D_MODEL), rows(IN_WIDTH), rows(D_MODEL), const((3, CONV_WIDTH)), const((1, CONV_WIDTH)),
                   const((1, HEAD_DIM)), const((1, HEAD_DIM)), const((1, D_MODEL))],
        out_shape=[jax.ShapeDtypeStruct((SEQ, D_MODEL), F32), jax.ShapeDtypeStruct((SEQ, IN_WIDTH), BF16),
                   jax.ShapeDtypeStruct((SEQ, D_MODEL), BF16), jax.ShapeDtypeStruct((3, CONV_WIDTH), F32),
                   jax.ShapeDtypeStruct((1, CONV_WIDTH), F32), jax.ShapeDtypeStruct((1, HEAD_DIM), F32),
                   jax.ShapeDtypeStruct((1, HEAD_DIM), F32), jax.ShapeDtypeStruct((1, D_MODEL), F32)],
        scratch_shapes=[pltpu.VMEM((SUBLANES, CONV_WIDTH), F32)],
        compiler_params=_params(("arbitrary",)),
    )(x, dh1, proj, proj, dycn, dqn, dkn, dv, w_in_t, conv_w, g1, gq, gk, gconv)


def _grad_w_in(dproj, u1, after=None):
    bw = 768

    def body(a_ref, b_ref, o_ref):
        o_ref[...] = _dot_tn(a_ref[...], b_ref[...]).astype(BF16)

    body, more_specs, more = _ordered_behind(body, 2, after)
    return pl.pallas_call(
        body, name="grad_w_in", grid=(IN_WIDTH // bw,),
        in_specs=[pl.BlockSpec((SEQ, bw), lambda k: (0, k)), pl.BlockSpec((SEQ, D_MODEL), lambda k: (0, 0))] + more_specs,
        out_specs=pl.BlockSpec((bw, D_MODEL), lambda k: (k, 0)),
        out_shape=jax.ShapeDtypeStruct((IN_WIDTH, D_MODEL), BF16),
        compiler_params=_params(("arbitrary",)),
    )(dproj, u1, *more)


def _adamw_math(w, g, m, v):
    m = ADAM_B1 * m + (1.0 - ADAM_B1) * g
    v = ADAM_B2 * v + (1.0 - ADAM_B2) * (g * g)
    m_hat = m / (1.0 - ADAM_B1 ** ADAM_STEP)
    v_hat = v / (1.0 - ADAM_B2 ** ADAM_STEP)
    return -ADAM_LR * (m_hat / (jnp.sqrt(v_hat) + ADAM_EPS) + ADAM_WD * w), m, v


_ROW_G1, _ROW_G2, _ROW_OUT_NORMS, _ROW_FFN_B, _ROW_GQ, _ROW_GK, _ROW_SINKS, _ROW_LOSS, _ROW_TABLE = 0, 1, 2, 3, 11, 12, 13, 14, 16
SMALL_ROWS, SMALL_COLS = 24, 1024
_SMALL_NAMES = ("norm_mix_g", "norm_ffn_g", "out_norm_conv_g", "out_norm_attn_g", "ffn_conv_b", "q_norm_g", "k_norm_g",
                "sinks", "rel_bias_table")


def _pack_small_grads(dg1, dg2, dgconv, dgattn, dfb, dgq, dgk, dsinks, dtbl_t, loss_acc):
    def body(dg1_ref, dg2_ref, dgc_ref, dga_ref, dfb_ref, dgq_ref, dgk_ref, ds_ref, dt_ref, loss_ref, o_ref):
        o_ref[...] = jnp.zeros_like(o_ref)
        o_ref[_ROW_G1:_ROW_G1 + 1, :] = dg1_ref[...]
        o_ref[_ROW_G2:_ROW_G2 + 1, :] = dg2_ref[...]
        o_ref[_ROW_OUT_NORMS:_ROW_OUT_NORMS + 1, 0:CONV_WIDTH] = dgc_ref[...]
        o_ref[_ROW_OUT_NORMS:_ROW_OUT_NORMS + 1, CONV_WIDTH:] = dga_ref[...]
        for k in range(N_DEV):
            o_ref[_ROW_FFN_B + k:_ROW_FFN_B + k + 1, 0:FFN_BLK] = dfb_ref[k // N_FFN_BLK, k % N_FFN_BLK]
        o_ref[_ROW_GQ:_ROW_GQ + 1, 0:HEAD_DIM] = dgq_ref[...]
        o_ref[_ROW_GK:_ROW_GK + 1, 0:HEAD_DIM] = dgk_ref[...]
        o_ref[_ROW_SINKS:_ROW_SINKS + 1, 0:128] = ds_ref[...]
        o_ref[_ROW_LOSS:_ROW_LOSS + 1, 0:128] = loss_ref[0:1, :]
        o_ref[_ROW_TABLE:_ROW_TABLE + N_HEADS, 0:128] = dt_ref[...]

    return pl.pallas_call(body, name="pack_small_grads", out_shape=jax.ShapeDtypeStruct((SMALL_ROWS, SMALL_COLS), F32))(
        dg1, dg2, dgconv, dgattn, dfb, dgq, dgk, dsinks, dtbl_t, loss_acc)


def _adamw_small(recv, params, after):
    names = _SMALL_NAMES
    n = len(names)

    def grad_of(g, name, k=None):
        if name == "norm_mix_g":
            return g[_ROW_G1:_ROW_G1 + 1, :]
        if name == "norm_ffn_g":
            return g[_ROW_G2:_ROW_G2 + 1, :]
        if name == "out_norm_conv_g":
            return g[_ROW_OUT_NORMS:_ROW_OUT_NORMS + 1, 0:CONV_WIDTH]
        if name == "out_norm_attn_g":
            return g[_ROW_OUT_NORMS:_ROW_OUT_NORMS + 1, CONV_WIDTH:]
        if name == "ffn_conv_b":
            return g[_ROW_FFN_B + k:_ROW_FFN_B + k + 1, 0:FFN_BLK]
        if name == "q_norm_g":
            return g[_ROW_GQ:_ROW_GQ + 1, 0:HEAD_DIM]
        if name == "k_norm_g":
            return g[_ROW_GK:_ROW_GK + 1, 0:HEAD_DIM]
        if name == "sinks":
            return g[_ROW_SINKS:_ROW_SINKS + 1, 0:N_HEADS]
        return g[_ROW_TABLE:_ROW_TABLE + N_HEADS, 0:NUM_BUCKETS]

    def body(r_ref, *refs):
        ins, outs, loss_ref = refs[:3 * n], refs[3 * n:7 * n], refs[7 * n]
        g = r_ref[0]
        for s in range(1, N_DEV):
            g = g + r_ref[s]
        loss_ref[...] = g[_ROW_LOSS:_ROW_LOSS + 1, 0:128]
        for i, name in enumerate(names):
            w_ref, m_ref, v_ref = ins[3 * i:3 * i + 3]
            o = outs[4 * i:4 * i + 4]
            cols = [slice(FFN_BLK * k, FFN_BLK * (k + 1)) for k in range(N_DEV)] if name == "ffn_conv_b" else [slice(None)]
            for k, cs in enumerate(cols):
                gk = grad_of(g, name, k)
                d, m2, v2 = _adamw_math(w_ref[:, cs], gk, m_ref[:, cs], v_ref[:, cs])
                o[0][:, cs], o[1][:, cs], o[2][:, cs], o[3][:, cs] = gk, d, m2, v2

    flat = [a for name in names for a in params[name]]
    body, more_specs, more = _ordered_behind(body, 1 + 3 * n, after)
    vmem = pl.BlockSpec(memory_space=pltpu.VMEM)
    out = pl.pallas_call(
        body, name="adamw_small",
        in_specs=[vmem] * (1 + 3 * n) + more_specs,
        out_shape=[jax.ShapeDtypeStruct(params[name][0].shape, F32) for name in names for _ in range(4)]
        + [jax.ShapeDtypeStruct((1, 128), F32)],
        compiler_params=pltpu.CompilerParams(vmem_limit_bytes=VMEM_LIMIT),
    )(recv, *flat, *more)
    return {name: tuple(out[4 * i:4 * i + 4]) for i, name in enumerate(names)}, out[4 * n]


def _adamw(w, m, v, part, recv, chip, name, row_blocks=1, after=None):
    rb = w.shape[0] // row_blocks
    tail = w.shape[1:]
    zeros = (0,) * len(tail)

    def body(chip_ref, w_ref, m_ref, v_ref, p_ref, r_ref, g_o, d_o, m_o, v_o):
        g = p_ref[...].astype(F32)
        for s in range(3):
            g = g + r_ref[s].astype(F32)
        g_o[...] = g
        d_o[...], m_o[...], v_o[...] = _adamw_math(w_ref[...], g, m_ref[...], v_ref[...])

    blk = pl.BlockSpec((rb,) + tail, lambda i, chip_ref: (i,) + zeros)
    pblk = pl.BlockSpec((None, rb) + tail, lambda i, chip_ref: (chip_ref[0], i) + zeros)
    rblk = pl.BlockSpec((3, rb) + tail, lambda i, chip_ref: (0, i) + zeros)
    body, more_specs, more = _ordered_behind(body, 6, after)
    return pl.pallas_call(
        body, name=name,
        grid_spec=pltpu.PrefetchScalarGridSpec(num_scalar_prefetch=1, grid=(row_blocks,),
                                               in_specs=[blk, blk, blk, pblk, rblk] + more_specs, out_specs=[blk] * 4),
        out_shape=[jax.ShapeDtypeStruct(w.shape, F32)] * 4,
        compiler_params=_params(("arbitrary",)),
    )(chip, w, m, v, part, recv, *more)


def kernel(x, norm_mix_g, w_in, conv_w, q_norm_g, k_norm_g, rel_bias_table, sinks, out_norm_conv_g, out_norm_attn_g, w_out, norm_ffn_g, w_up, ffn_conv_w, ffn_conv_b, w_down, loss_target, m_norm_mix_g, m_w_in, m_conv_w, m_q_norm_g, m_k_norm_g, m_rel_bias_table, m_sinks, m_out_norm_conv_g, m_out_norm_attn_g, m_w_out, m_norm_ffn_g, m_w_up, m_ffn_conv_w, m_ffn_conv_b, m_w_down, v_norm_mix_g, v_w_in, v_conv_w, v_q_norm_g, v_k_norm_g, v_rel_bias_table, v_sinks, v_out_norm_conv_g, v_out_norm_attn_g, v_w_out, v_norm_ffn_g, v_w_up, v_ffn_conv_w, v_ffn_conv_b, v_w_down):
    p = dict(norm_mix_g=norm_mix_g, w_in=w_in, conv_w=conv_w, q_norm_g=q_norm_g, k_norm_g=k_norm_g,
             rel_bias_table=rel_bias_table, sinks=sinks, out_norm_conv_g=out_norm_conv_g, out_norm_attn_g=out_norm_attn_g,
             w_out=w_out, norm_ffn_g=norm_ffn_g, w_up=w_up, ffn_conv_w=ffn_conv_w, ffn_conv_b=ffn_conv_b, w_down=w_down)
    m = dict(norm_mix_g=m_norm_mix_g, w_in=m_w_in, conv_w=m_conv_w, q_norm_g=m_q_norm_g, k_norm_g=m_k_norm_g,
             rel_bias_table=m_rel_bias_table, sinks=m_sinks, out_norm_conv_g=m_out_norm_conv_g,
             out_norm_attn_g=m_out_norm_attn_g, w_out=m_w_out, norm_ffn_g=m_norm_ffn_g, w_up=m_w_up,
             ffn_conv_w=m_ffn_conv_w, ffn_conv_b=m_ffn_conv_b, w_down=m_w_down)
    v = dict(norm_mix_g=v_norm_mix_g, w_in=v_w_in, conv_w=v_conv_w, q_norm_g=v_q_norm_g, k_norm_g=v_k_norm_g,
             rel_bias_table=v_rel_bias_table, sinks=v_sinks, out_norm_conv_g=v_out_norm_conv_g,
             out_norm_attn_g=v_out_norm_attn_g, w_out=v_w_out, norm_ffn_g=v_norm_ffn_g, w_up=v_w_up,
             ffn_conv_w=v_ffn_conv_w, ffn_conv_b=v_ffn_conv_b, w_down=v_w_down)

    xs, tgt = x[0], loss_target[0]
    g1, g2, gq, gk, gconv, gattn = norm_mix_g, norm_ffn_g, q_norm_g, k_norm_g, out_norm_conv_g, out_norm_attn_g
    ix, iy, ic = _coords()
    core = ic.astype(jnp.int32).reshape(1)
    chip = (2 * ix + iy).astype(jnp.int32).reshape(1)
    me = _lin(ix, iy, ic).astype(jnp.int32).reshape(1)
    bkt = jnp.asarray(_bucket_map())
    tr = lambda a: a[0].T
    taps = lambda a: jnp.transpose(a, (1, 0, 2))
    tbl_t = rel_bias_table.T

    wi_l, cw_l, wo_l, wu_l, wd_l, fcw_l = _place_shards(
        me, [tr(w_in), taps(conv_w), w_out[0], tr(w_up), w_down[0], taps(ffn_conv_w)], [BF16, F32, BF16, BF16, BF16, F32])
    finish_a, token_a = _all_gather_split([wi_l, cw_l], "mixer", None)
    ffn_stage2, ffn_stage3, token_b = _all_gather_tree([wo_l, wu_l, wd_l, fcw_l], "ffn", token_a)
    wi_g, cw_g = finish_a(token_b)
    w_in_t = wi_g.reshape(IN_WIDTH, D_MODEL)
    conv_w_f = jnp.transpose(cw_g[:, :, 0, :], (1, 0, 2)).reshape(3, CONV_WIDTH)

    proj, ycn, qn, kn, vv = _mix_in_fwd(xs, g1, w_in_t, conv_w_f, gq, gk, gconv)
    token_b2 = ffn_stage2(ycn)
    y_attn, yan = _attn_fwd(qn, kn, vv, tbl_t, sinks, bkt, gattn, after=token_b2)
    wo_g, wu_g, wd_g, fcw_g = ffn_stage3(yan)
    w_out_f = wo_g.reshape(D_MODEL, D_MODEL)
    w_down_f = wd_g.reshape(N_FFN_BLK, FFN_BLK, D_MODEL)
    w_up_f = wu_g.reshape(2, N_FFN_BLK, FFN_BLK, D_MODEL)
    fcw_f = fcw_g.reshape(2, N_FFN_BLK, 3, 1, FFN_BLK)
    fcb = ffn_conv_b.reshape(2, N_FFN_BLK, 1, FFN_BLK)
    h1, u2 = _out_proj(xs, ycn, yan, w_out_f, g2)
    up, pre, act, dh2, dh2b, loss_acc = _ffn_fwd(h1, u2, w_up_f, fcw_f, fcb, w_down_f, tgt)

    dup, dh1, dh1b, dfb, dfcw, dg2 = _ffn_bwd(dh2, dh2b, h1, g2, up, pre, w_up_f, fcw_f, w_down_f)
    dw_down = _grad_tn_blocked(act, dh2b, "grad_w_down", a_is_blocked=True).reshape(N_DEV, D_FF // N_DEV, D_MODEL)
    dw_up = _grad_tn_blocked(dup.reshape(N_DEV, SEQ, FFN_BLK), u2, "grad_w_up", a_is_blocked=True)
    dw_out = _grad_tn([ycn, yan], dh1b, CONV_WIDTH, "grad_w_out").reshape(N_DEV, D_MODEL // N_DEV, D_MODEL)
    out_bwd = {}

    def behind_ffn(token):
        out_bwd["r"] = _out_bwd(dh1b, w_out_f, y_attn, gattn, after=token)
        return out_bwd["r"][0]

    finish_ffn, token_ffn = _reduce_scatter_split(
        [dw_down, dw_up, dw_out, dfcw.reshape(N_DEV, 3, 1, FFN_BLK)], "ffn", core, behind_ffn)
    dycn, dy_attn, dgattn = out_bwd["r"]
    dqn, dkn, dv, dtbl_t, dsinks = _attn_bwd(qn, kn, vv, dy_attn, tbl_t, sinks, bkt, after=token_ffn)
    dx, dproj, u1, dcw, dgconv, dgq, dgk, dg1 = _mix_in_bwd(xs, dh1, proj, dycn, dqn, dkn, dv, w_in_t, conv_w_f,
                                                             g1, gq, gk, gconv)
    packed = _pack_small_grads(dg1, dg2, dgconv, dgattn, dfb, dgq, dgk, dsinks, dtbl_t, loss_acc)
    plan_s, slots_s = _broadcast_plan()
    s_sem, r_sem, src_s, land_s, token_s = _split_start(
        "gather_small_start", [packed], [jnp.broadcast_to(packed[None], (N_DEV,) + packed.shape)], plan_s, None)
    dw_in_t = _grad_w_in(dproj, u1, after=token_s).reshape(N_DEV, IN_WIDTH // N_DEV, D_MODEL)
    dcw_b = jnp.transpose(dcw.reshape(3, N_DEV, 1, CONV_WIDTH // N_DEV), (1, 0, 2, 3))
    adam = {}
    ffn_got = {}

    def behind_mixer(token):
        ffn_got["r"] = finish_ffn(token)
        return ffn_got["r"][1][0]

    finish_mixer, token_mixer = _reduce_scatter_split([dw_in_t, dcw_b], "mixer", core, behind_mixer)
    (p_wd, p_wu, p_wo, p_fcw), (r_wd, r_wu, r_wo, r_fcw) = ffn_got["r"]
    adam["w_down"] = _adamw(w_down[0], m_w_down[0], v_w_down[0], p_wd, r_wd, chip, "adamw_w_down", row_blocks=2,
                            after=token_mixer)
    adam_up = _adamw(tr(w_up), tr(m_w_up), tr(v_w_up), p_wu, r_wu, chip, "adamw_w_up", row_blocks=4,
                     after=adam["w_down"][0])
    adam["w_out"] = _adamw(w_out[0], m_w_out[0], v_w_out[0], p_wo, r_wo, chip, "adamw_w_out", after=adam_up[0])
    adam_fcw = _adamw(taps(ffn_conv_w), taps(m_ffn_conv_w), taps(v_ffn_conv_w), p_fcw, r_fcw, chip, "adamw_ffn_conv_w",
                      after=adam["w_out"][0])
    _, (r_small,) = _split_wait("gather_small_wait", s_sem, r_sem, src_s, land_s, plan_s, slots_s, adam_fcw[0])
    small_in = {k: (p[k], m[k], v[k]) for k in _SMALL_NAMES}
    small_in["rel_bias_table"] = (tbl_t, m_rel_bias_table.T, v_rel_bias_table.T)
    small_out, loss_row = _adamw_small(r_small, small_in, None)
    (p_wi, p_cw), (r_wi, r_cw) = finish_mixer(loss_row)
    adam_in = _adamw(tr(w_in), tr(m_w_in), tr(v_w_in), p_wi, r_wi, chip, "adamw_w_in")
    adam_cw = _adamw(taps(conv_w), taps(m_conv_w), taps(v_conv_w), p_cw, r_cw, chip, "adamw_conv_w")

    res = {k: tuple(a[None] for a in t) for k, t in adam.items()}
    res["w_up"] = tuple(a.T[None] for a in adam_up)
    res["w_in"] = tuple(a.T[None] for a in adam_in)
    res["ffn_conv_w"] = tuple(taps(a) for a in adam_fcw)
    res["conv_w"] = tuple(taps(a) for a in adam_cw)
    res.update(small_out)
    res["rel_bias_table"] = tuple(a.T for a in small_out["rel_bias_table"])
    loss = loss_row[0, 0]
    order = ("norm_mix_g", "w_in", "conv_w", "q_norm_g", "k_norm_g", "rel_bias_table", "sinks", "out_norm_conv_g",
             "out_norm_attn_g", "w_out", "norm_ffn_g", "w_up", "ffn_conv_w", "ffn_conv_b", "w_down")
    return (loss, dx[None], *[res[k][0] for k in order], *[res[k][1] for k in order],
            *[res[k][2] for k in order], *[res[k][3] for k in order])
```

```python
import functools
import math

import numpy as np
import jax
import jax.numpy as jnp
from jax import lax
from jax.experimental import pallas as pl
from jax.experimental.pallas import tpu as pltpu

F32 = jnp.float32
BF16 = jnp.bfloat16

SEQ = 2048
D_MODEL = 1024
CONV_WIDTH = 512
ATTN_WIDTH = 512
KV_WIDTH = 128
HEAD_DIM = 64
N_HEADS = 8
GQA_GROUP = 4
IN_WIDTH = 2304
D_FF = 2816
BLK = 128
NUM_BUCKETS = 32
EPS = 1e-6
NEG_INF = -1e30
ADAM_LR = 0.001
ADAM_B1 = 0.9
ADAM_B2 = 0.999
ADAM_EPS = 1e-08
ADAM_WD = 0.01
ADAM_STEP = 10

N_DEV = 8
FFN_BLK = 2 * D_FF // N_DEV
N_FFN_BLK = D_FF // FFN_BLK
SUBLANES = 8
VMEM_LIMIT = 56 * 1024 * 1024

_MESH = pl.DeviceIdType.MESH
_ANY = pl.BlockSpec(memory_space=pl.ANY)


def _params(sem):
    return pltpu.CompilerParams(dimension_semantics=sem, vmem_limit_bytes=VMEM_LIMIT)


def _ordered_behind(body, pos, after):
    if after is None:
        return body, [], []
    return (lambda *refs: body(*refs[:pos], *refs[pos + 1:])), [_ANY], [after]


def _dot(a, b):
    return jnp.dot(a, b, preferred_element_type=F32)


def _dot_nt(a, b):
    return lax.dot_general(a, b, (((1,), (1,)), ((), ())), preferred_element_type=F32)


def _dot_tn(a, b):
    return lax.dot_general(a, b, (((0,), (0,)), ((), ())), preferred_element_type=F32)


def _shift_down(x, s, halo):
    r = pltpu.roll(x, s, axis=0)
    hr = pltpu.roll(halo, s, axis=0)
    row = lax.broadcasted_iota(jnp.int32, halo.shape, 0)
    top = jnp.where(row < s, hr, r[:SUBLANES])
    return jnp.concatenate([top, r[SUBLANES:]], axis=0)


def _shift_up(x, s, halo):
    n = x.shape[0]
    r = pltpu.roll(x, n - s, axis=0)
    hr = pltpu.roll(halo, SUBLANES - s, axis=0)
    row = lax.broadcasted_iota(jnp.int32, halo.shape, 0)
    bot = jnp.where(row >= SUBLANES - s, hr, r[n - SUBLANES:])
    return jnp.concatenate([r[:n - SUBLANES], bot], axis=0)


def _taps(w):
    return (w[0], w[1], w[2]) if len(w.shape) == 3 else (w[0:1], w[1:2], w[2:3])


def _conv3(x, w, halo):
    x2 = _shift_down(x, 2, halo)
    x1 = _shift_down(x, 1, halo)
    return x2 * w[0] + x1 * w[1] + x * w[2], x2, x1


def _conv3_bwd_input(dy, w, halo_next):
    return dy * w[2] + _shift_up(dy, 1, halo_next) * w[1] + _shift_up(dy, 2, halo_next) * w[0]


def _rstd(x):
    return lax.rsqrt(jnp.mean(x * x, axis=-1, keepdims=True) + EPS)


def _rms_bwd(x, g, dy):
    r = _rstd(x)
    n = x * r
    dn = dy * g
    dx = r * (dn - n * jnp.mean(dn * n, axis=-1, keepdims=True))
    return dx, jnp.sum(dy * n, axis=0, keepdims=True)


def _head_mean(x):
    width = x.shape[-1]
    ri = lax.broadcasted_iota(jnp.int32, (width, width), 0) // HEAD_DIM
    ci = lax.broadcasted_iota(jnp.int32, (width, width), 1) // HEAD_DIM
    ones = jnp.where(ri == ci, 1.0, 0.0).astype(BF16)
    hi = x.astype(BF16)
    lo = (x - hi.astype(F32)).astype(BF16)
    return (_dot(hi, ones) + _dot(lo, ones)) * (1.0 / HEAD_DIM)


def _head_norm(x, g, heads):
    return x * lax.rsqrt(_head_mean(x * x) + EPS) * jnp.tile(g, (1, heads))


def _head_norm_bwd(x, g, dy, heads):
    r = lax.rsqrt(_head_mean(x * x) + EPS)
    n = x * r
    dn = dy * jnp.tile(g, (1, heads))
    dx = r * (dn - n * _head_mean(dn * n))
    per_lane = jnp.sum(dy * n, axis=0, keepdims=True)
    dg = per_lane[:, 0:HEAD_DIM]
    for h in range(1, heads):
        dg = dg + per_lane[:, HEAD_DIM * h:HEAD_DIM * (h + 1)]
    return dx, dg


def _bucket_map():
    q = np.arange(BLK)[:, None]
    j = np.arange(BLK)[None, :]
    n = np.where(j > q, q + BLK - j, q - j)
    nf = np.maximum(n, 1).astype(np.float32)
    max_exact = NUM_BUCKETS // 2
    large = max_exact + (np.log(nf / max_exact) / math.log(BLK / max_exact) * (NUM_BUCKETS - max_exact)).astype(np.int32)
    large = np.minimum(large, NUM_BUCKETS - 1)
    return np.where(n < max_exact, n, large).astype(np.int32)


def _coords():
    return lax.axis_index("x"), lax.axis_index("y"), lax.axis_index("c")


def _lin(px, py, pc):
    return 4 * px + 2 * py + pc


_HBM = pl.BlockSpec(memory_space=pltpu.HBM)
_SEM = pl.BlockSpec(memory_space=pltpu.SEMAPHORE)
_EFFECT = pltpu.SideEffectType.DATAFLOW_SIDE_EFFECTING


def _in_hbm(a):
    return pltpu.with_memory_space_constraint(a, pltpu.HBM)


def _split_start(name, srcs, lands, plan, after):
    ns, nl = len(srcs), len(lands)
    n_copies = len(plan(0, 0, 0))
    n_after = 0 if after is None else 1

    def body(*refs):
        src_refs, land_refs = refs[:ns + nl], refs[ns:ns + nl]
        send_sems, recv_sems = refs[ns + nl + n_after], refs[ns + nl + n_after + 1]
        token = refs[-1]
        for k, (a, s_slot, l, d_slot, dev) in enumerate(plan(*_coords())):
            src = src_refs[a] if s_slot is None else src_refs[a].at[s_slot]
            pltpu.make_async_remote_copy(src_ref=src, dst_ref=land_refs[l].at[d_slot], send_sem=send_sems.at[k],
                                         recv_sem=recv_sems.at[k], device_id=dev, device_id_type=_MESH).start()
        token[...] = jnp.zeros_like(token)

    arrs = list(srcs) + list(lands)
    out = pl.pallas_call(
        body, name=name,
        out_shape=(pltpu.SemaphoreType.DMA((n_copies,)), pltpu.SemaphoreType.DMA((n_copies,)),
                   *[pltpu.HBM(a.shape, a.dtype) for a in arrs], jax.ShapeDtypeStruct((SUBLANES, 128), F32)),
        in_specs=[_HBM] * (ns + nl) + [_ANY] * n_after,
        out_specs=(_SEM, _SEM, *[_HBM] * (ns + nl), pl.BlockSpec(memory_space=pltpu.VMEM)),
        input_output_aliases={i: 2 + i for i in range(ns + nl)},
        compiler_params=pltpu.CompilerParams(has_side_effects=_EFFECT),
    )(*[_in_hbm(a) for a in arrs], *([] if after is None else [after]))
    return out[0], out[1], list(out[2:2 + ns]), list(out[2 + ns:2 + ns + nl]), out[-1]


def _split_wait(name, send_sems, recv_sems, srcs, lands, plan, recv_slots, after):
    ns, nl = len(srcs), len(lands)

    def body(*refs):
        src_refs, land_refs = refs[:ns + nl], refs[ns:ns + nl]
        send_sems, recv_sems = refs[ns + nl], refs[ns + nl + 1]
        coords = _coords()
        slots = recv_slots(*coords)
        for k, (a, s_slot, l, _, dev) in enumerate(plan(*coords)):
            src = src_refs[a] if s_slot is None else src_refs[a].at[s_slot]
            cp = pltpu.make_async_remote_copy(src_ref=src, dst_ref=land_refs[l].at[slots[k]], send_sem=send_sems.at[k],
                                              recv_sem=recv_sems.at[k], device_id=dev, device_id_type=_MESH)
            cp.wait_send()
            cp.wait_recv()

    arrs = list(srcs) + list(lands)
    out = pl.pallas_call(
        body, name=name,
        out_shape=tuple(pltpu.HBM(a.shape, a.dtype) for a in arrs),
        in_specs=[_HBM] * (ns + nl) + [_SEM, _SEM, _ANY],
        out_specs=tuple([_HBM] * (ns + nl)),
        input_output_aliases={i: i for i in range(ns + nl)},
        compiler_params=pltpu.CompilerParams(has_side_effects=_EFFECT),
    )(*arrs, send_sems, recv_sems, after)
    return list(out[:ns]), list(out[ns:])


def _chips(x, y):
    return [(1 - x, y), (x, 1 - y), (1 - x, 1 - y)]


def _gather_plan_ici(n):
    def plan(x, y, c):
        me = _lin(x, y, c)
        out = []
        for a in range(n):
            out.append((a, me, a, me, (x, y, 1 - c)))
            out += [(a, me, a, me, (cx, cy, c)) for cx, cy in _chips(x, y)]
        return out

    def recv_slots(x, y, c):
        out = []
        for _ in range(n):
            out.append(_lin(x, y, 1 - c))
            out += [_lin(cx, cy, c) for cx, cy in _chips(x, y)]
        return out

    return plan, recv_slots


def _gather_plan_d2d(n):
    def plan(x, y, c):
        return [(a, _lin(cx, cy, c), a, _lin(cx, cy, c), (x, y, 1 - c)) for a in range(n) for cx, cy in _chips(x, y)]

    def recv_slots(x, y, c):
        return [_lin(cx, cy, 1 - c) for _ in range(n) for cx, cy in _chips(x, y)]

    return plan, recv_slots


def _all_gather_split(lands, tag, after):
    n = len(lands)
    plan1, slots1 = _gather_plan_ici(n)
    s1, r1, _, lands, token = _split_start(f"gather_{tag}_ici_start", [], lands, plan1, after)

    def finish(after):
        _, got = _split_wait(f"gather_{tag}_ici_wait", s1, r1, [], lands, plan1, slots1, after)
        plan2, slots2 = _gather_plan_d2d(n)
        s2, r2, _, got, token2 = _split_start(f"gather_{tag}_d2d_start", [], got, plan2, None)
        return _split_wait(f"gather_{tag}_d2d_wait", s2, r2, [], got, plan2, slots2, token2)[1]

    return finish, token


def _all_gather_tree(lands, tag, after):
    n = len(lands)

    def plan1(x, y, c):
        me = _lin(x, y, c)
        return [(a, me, a, me, dev) for a in range(n) for dev in ((x, y, 1 - c), (1 - x, y, c), (x, 1 - y, c))]

    def slots1(x, y, c):
        return [s for _ in range(n) for s in (_lin(x, y, 1 - c), _lin(1 - x, y, c), _lin(x, 1 - y, c))]

    def plan2(x, y, c):
        from_x, from_y = _lin(1 - x, y, c), _lin(x, 1 - y, c)
        north = c == 1
        passed = jnp.where(north, from_x, from_y)
        onward = (jnp.where(north, x, 1 - x), jnp.where(north, 1 - y, y), c)
        sib = (x, y, 1 - c)
        return [cp for a in range(n) for cp in ((a, passed, a, passed, onward), (a, from_x, a, from_x, sib),
                                                (a, from_y, a, from_y, sib))]

    def slots2(x, y, c):
        return [s for _ in range(n) for s in (_lin(1 - x, 1 - y, c), _lin(1 - x, y, 1 - c), _lin(x, 1 - y, 1 - c))]

    def plan3(x, y, c):
        diag = _lin(1 - x, 1 - y, c)
        return [(a, diag, a, diag, (x, y, 1 - c)) for a in range(n)]

    def slots3(x, y, c):
        return [_lin(1 - x, 1 - y, 1 - c)] * n

    s1, r1, _, lands, token = _split_start(f"gather_{tag}_1_start", [], lands, plan1, after)
    state = {}

    def stage2(after):
        _, got = _split_wait(f"gather_{tag}_1_wait", s1, r1, [], lands, plan1, slots1, after)
        state["s"], state["r"], _, state["lands"], token2 = _split_start(f"gather_{tag}_2_start", [], got, plan2, None)
        return token2

    def stage3(after):
        _, got = _split_wait(f"gather_{tag}_2_wait", state["s"], state["r"], [], state["lands"], plan2, slots2, after)
        s3, r3, _, got, token3 = _split_start(f"gather_{tag}_3_start", [], got, plan3, None)
        return _split_wait(f"gather_{tag}_3_wait", s3, r3, [], got, plan3, slots3, token3)[1]

    return stage2, stage3, token


_CHIP_LIST = ((0, 0), (0, 1), (1, 0), (1, 1))


def _reduce_plan_d2d(n):
    def plan(x, y, c):
        return [(a, _lin(qx, qy, 1 - c), a, q, (x, y, 1 - c)) for a in range(n) for q, (qx, qy) in enumerate(_CHIP_LIST)]

    def recv_slots(x, y, c):
        return [q for _ in range(n) for q in range(4)]

    return plan, recv_slots


def _reduce_plan_ici(n):
    def plan(x, y, c):
        return [(a, 2 * cx + cy, a, j, (cx, cy, c)) for a in range(n) for j, (cx, cy) in enumerate(_chips(x, y))]

    def recv_slots(x, y, c):
        return [j for _ in range(n) for j in range(3)]

    return plan, recv_slots


def _broadcast_plan():
    def peers(x, y, c):
        return [(1 - x if r & 4 else x, 1 - y if r & 2 else y, 1 - c if r & 1 else c) for r in range(1, N_DEV)]

    def plan(x, y, c):
        return [(0, None, 0, _lin(x, y, c), peer) for peer in peers(x, y, c)]

    def recv_slots(x, y, c):
        return [_lin(*peer) for peer in peers(x, y, c)]

    return plan, recv_slots


def _chip_partial(grads, recvd, core, name):
    n = len(grads)

    def body(c_ref, *refs):
        for a in range(n):
            g_ref, r_ref, o_ref = refs[a], refs[n + a], refs[2 * n + a]
            o_ref[...] = (g_ref[...].astype(F32) + r_ref[...].astype(F32)).astype(o_ref.dtype)

    def blk(a, own):
        zeros = (0,) * (a.ndim - 1)
        return pl.BlockSpec((None,) + a.shape[1:],
                            (lambda q, c_ref: (2 * q + c_ref[0],) + zeros) if own else (lambda q, c_ref: (q,) + zeros))

    return pl.pallas_call(
        body, name=name,
        grid_spec=pltpu.PrefetchScalarGridSpec(
            num_scalar_prefetch=1, grid=(4,),
            in_specs=[blk(a, True) for a in grads] + [blk(a, False) for a in recvd],
            out_specs=[blk(a, False) for a in recvd]),
        out_shape=[jax.ShapeDtypeStruct(a.shape, a.dtype) for a in recvd],
        compiler_params=_params(("arbitrary",)),
    )(core, *grads, *recvd)


def _reduce_scatter_split(grads, tag, core, behind):
    n = len(grads)
    plan1, slots1 = _reduce_plan_d2d(n)
    lands1 = [lax.empty((4,) + a.shape[1:], a.dtype) for a in grads]
    s1, r1, srcs1, lands1, token1 = _split_start(f"reduce_{tag}_d2d_start", grads, lands1, plan1, None)
    own, got = _split_wait(f"reduce_{tag}_d2d_wait", s1, r1, srcs1, lands1, plan1, slots1, behind(token1))
    parts = _chip_partial(own, got, core, f"reduce_{tag}_partial")
    plan2, slots2 = _reduce_plan_ici(n)
    lands2 = [lax.empty((3,) + a.shape[1:], a.dtype) for a in grads]
    s2, r2, srcs2, lands2, token2 = _split_start(f"reduce_{tag}_ici_start", parts, lands2, plan2, None)

    def finish(after):
        return _split_wait(f"reduce_{tag}_ici_wait", s2, r2, srcs2, lands2, plan2, slots2, after)

    return finish, token2


def _place_shards(me, shards, dtypes):
    n = len(shards)

    def body(me_ref, *refs):
        for a in range(n):
            refs[n + a][...] = refs[a][...].astype(dtypes[a])

    full = lambda s: pl.BlockSpec(s.shape, lambda i, me_ref: (0,) * s.ndim)
    slot = lambda s: pl.BlockSpec((None,) + s.shape, lambda i, me_ref: (me_ref[0],) + (0,) * s.ndim)
    return pl.pallas_call(
        body, name="place_shards",
        grid_spec=pltpu.PrefetchScalarGridSpec(num_scalar_prefetch=1, grid=(1,), in_specs=[full(s) for s in shards],
                                               out_specs=[slot(s) for s in shards]),
        out_shape=[jax.ShapeDtypeStruct((N_DEV,) + s.shape, d) for s, d in zip(shards, dtypes)],
        compiler_params=_params(("arbitrary",)),
    )(me, *shards)


def _mix_in_fwd(x, g1, w_in_t, conv_w, gq, gk, gconv):
    tm = 512
    n_t = SEQ // tm

    def body(x_ref, g1_ref, w_ref, cw_ref, gq_ref, gk_ref, gc_ref,
             proj_ref, ycn_ref, qn_ref, kn_ref, v_ref, halo_ref):
        @pl.when(pl.program_id(0) == 0)
        def _():
            halo_ref[...] = jnp.zeros_like(halo_ref)

        xv = x_ref[...]
        u = (xv * _rstd(xv) * g1_ref[...]).astype(BF16)
        proj = _dot_nt(u, w_ref[...])
        proj_ref[...] = proj
        gate_b = proj[:, 0:CONV_WIDTH]
        a = proj[:, CONV_WIDTH:2 * CONV_WIDTH] * proj[:, 2 * CONV_WIDTH:3 * CONV_WIDTH]
        cv, _, _ = _conv3(a, _taps(cw_ref[...]), halo_ref[...])
        halo_ref[...] = a[tm - SUBLANES:]
        yc = gate_b * cv
        ycn_ref[...] = (yc * _rstd(yc) * gc_ref[...]).astype(BF16)
        q0 = 3 * CONV_WIDTH
        qn_ref[...] = _head_norm(proj[:, q0:q0 + ATTN_WIDTH], gq_ref[...], N_HEADS).astype(BF16)
        k0 = q0 + ATTN_WIDTH
        kn_ref[...] = _head_norm(proj[:, k0:k0 + KV_WIDTH], gk_ref[...], 2).astype(BF16)
        v_ref[...] = proj[:, k0 + KV_WIDTH:k0 + 2 * KV_WIDTH].astype(BF16)

    const = lambda shape: pl.BlockSpec(shape, lambda i: (0,) * len(shape))
    rows = lambda w: pl.BlockSpec((tm, w), lambda i: (i, 0))
    return pl.pallas_call(
        body, name="mix_in_fwd", grid=(n_t,),
        in_specs=[rows(D_MODEL), const((1, D_MODEL)), const((IN_WIDTH, D_MODEL)), const((3, CONV_WIDTH)),
                  const((1, HEAD_DIM)), const((1, HEAD_DIM)), const((1, CONV_WIDTH))],
        out_specs=[rows(IN_WIDTH), rows(CONV_WIDTH), rows(ATTN_WIDTH), rows(KV_WIDTH), rows(KV_WIDTH)],
        out_shape=[jax.ShapeDtypeStruct((SEQ, IN_WIDTH), F32), jax.ShapeDtypeStruct((SEQ, CONV_WIDTH), BF16),
                   jax.ShapeDtypeStruct((SEQ, ATTN_WIDTH), BF16), jax.ShapeDtypeStruct((SEQ, KV_WIDTH), BF16),
                   jax.ShapeDtypeStruct((SEQ, KV_WIDTH), BF16)],
        scratch_shapes=[pltpu.VMEM((SUBLANES, CONV_WIDTH), F32)],
        compiler_params=_params(("arbitrary",)),
    )(x, g1, w_in_t, conv_w, gq, gk, gconv)


GROUP_ROWS = GQA_GROUP * BLK


def _band_bias(tbl_ref, bkt, bias_ref):
    for h in range(N_HEADS):
        acc = jnp.zeros(bkt.shape, F32)
        for b in range(NUM_BUCKETS):
            acc = jnp.where(bkt == b, tbl_ref[h, b], acc)
        bias_ref[h // GQA_GROUP, BLK * (h % GQA_GROUP):BLK * (h % GQA_GROUP + 1), :] = acc


def _band_masks(i):
    qi = lax.broadcasted_iota(jnp.int32, (GROUP_ROWS, BLK), 0) & (BLK - 1)
    ji = lax.broadcasted_iota(jnp.int32, (GROUP_ROWS, BLK), 1)
    upper = ji > qi
    return upper, upper & (i == 0)


def _stack_heads(x, g):
    return jnp.concatenate([x[:, HEAD_DIM * h:HEAD_DIM * (h + 1)] for h in range(GQA_GROUP * g, GQA_GROUP * (g + 1))], axis=0)


def _unstack_heads(groups):
    return jnp.concatenate([p[BLK * t:BLK * (t + 1)] for p in groups for t in range(GQA_GROUP)], axis=-1)


def _per_head_rows(vals):
    row = lax.broadcasted_iota(jnp.int32, (GROUP_ROWS, 1), 0)
    col = jnp.full((GROUP_ROWS, 1), vals[GQA_GROUP - 1], F32)
    for t in range(GQA_GROUP - 2, -1, -1):
        col = jnp.where(row < BLK * (t + 1), vals[t], col)
    return col


def _band_rows(ref, i):
    prev = pl.multiple_of(jnp.maximum(i - 1, 0) * BLK, BLK)
    cur = pl.multiple_of(i * BLK, BLK)
    return jnp.concatenate([ref[pl.ds(prev, BLK), :], ref[pl.ds(cur, BLK), :]], axis=0), prev, cur


def _fold(band, upper):
    return jnp.where(upper, band[:, :BLK], band[:, BLK:])


def _unfold(tile, upper):
    return jnp.concatenate([jnp.where(upper, tile, 0.0), jnp.where(upper, 0.0, tile)], axis=1)


def _head_probs(qh, kh, bias, upper, dead, sink):
    logits = _fold(_dot_nt(qh, kh), upper) * (HEAD_DIM ** -0.5) + bias
    logits = jnp.where(dead, NEG_INF, logits)
    m = jnp.maximum(jnp.max(logits, axis=-1, keepdims=True), sink)
    p = jnp.exp(logits - m)
    es = jnp.exp(sink - m)
    den = jnp.sum(p, axis=-1, keepdims=True) + es
    return p / den, es / den


def _attn_fwd(qn, kn, v, tbl, sinks, bkt, gattn, after=None):
    n_b = SEQ // BLK

    def body(q_ref, k_ref, v_ref, tbl_ref, sink_ref, bkt_ref, ga_ref, y_ref, yn_ref, bias_ref):
        i = pl.program_id(0)

        @pl.when(i == 0)
        def _():
            _band_bias(tbl_ref, bkt_ref[...], bias_ref)

        kb, _, _ = _band_rows(k_ref, i)
        vb, _, _ = _band_rows(v_ref, i)
        upper, dead = _band_masks(i)
        q = q_ref[...]
        outs = []
        for g in range(N_HEADS // GQA_GROUP):
            kv = slice(HEAD_DIM * g, HEAD_DIM * (g + 1))
            sink = _per_head_rows([sink_ref[0, GQA_GROUP * g + t] for t in range(GQA_GROUP)])
            probs, _ = _head_probs(_stack_heads(q, g), kb[:, kv], bias_ref[g], upper, dead, sink)
            outs.append(_dot(_unfold(probs, upper).astype(BF16), vb[:, kv]))
        y = _unstack_heads(outs)
        y_ref[...] = y
        yn_ref[...] = (y * _rstd(y) * ga_ref[...]).astype(BF16)

    const = lambda shape: pl.BlockSpec(shape, lambda i: (0,) * len(shape))
    rows = lambda w: pl.BlockSpec((BLK, w), lambda i: (i, 0))
    smem = pl.BlockSpec(memory_space=pltpu.SMEM)
    body, more_specs, more = _ordered_behind(body, 7, after)
    return pl.pallas_call(
        body, name="attn_fwd", grid=(n_b,),
        in_specs=[rows(ATTN_WIDTH), const((SEQ, KV_WIDTH)), const((SEQ, KV_WIDTH)), smem, smem,
                  const((BLK, BLK)), const((1, ATTN_WIDTH))] + more_specs,
        out_specs=[rows(ATTN_WIDTH), rows(ATTN_WIDTH)],
        out_shape=[jax.ShapeDtypeStruct((SEQ, ATTN_WIDTH), F32), jax.ShapeDtypeStruct((SEQ, ATTN_WIDTH), BF16)],
        scratch_shapes=[pltpu.VMEM((N_HEADS // GQA_GROUP, GROUP_ROWS, BLK), F32)],
        compiler_params=_params(("arbitrary",)),
    )(qn, kn, v, tbl, sinks, bkt, gattn, *more)


def _out_proj(x, ycn, yan, w_out, g2):
    tm = 512

    def body(x_ref, ycn_ref, yan_ref, wo_ref, g2_ref, h1_ref, u2_ref):
        h1 = x_ref[...] + _dot(ycn_ref[...], wo_ref[0:CONV_WIDTH, :]) + _dot(yan_ref[...], wo_ref[CONV_WIDTH:, :])
        h1_ref[...] = h1
        u2_ref[...] = (h1 * _rstd(h1) * g2_ref[...]).astype(BF16)

    rows = lambda w: pl.BlockSpec((tm, w), lambda i: (i, 0))
    const = lambda shape: pl.BlockSpec(shape, lambda i: (0,) * len(shape))
    return pl.pallas_call(
        body, name="out_proj", grid=(SEQ // tm,),
        in_specs=[rows(D_MODEL), rows(CONV_WIDTH), rows(ATTN_WIDTH), const((D_MODEL, D_MODEL)), const((1, D_MODEL))],
        out_specs=[rows(D_MODEL), rows(D_MODEL)],
        out_shape=[jax.ShapeDtypeStruct((SEQ, D_MODEL), F32), jax.ShapeDtypeStruct((SEQ, D_MODEL), BF16)],
        compiler_params=_params(("arbitrary",)),
    )(x, ycn, yan, w_out, g2)


def _ffn_fwd(h1, u2, w_up, fcw, fcb, w_down, tgt):
    tm = 512
    n_t = SEQ // tm
    last = N_FFN_BLK - 1

    def body(u2_ref, wu_ref, cw_ref, b_ref, wd_ref, h1_ref, tgt_ref,
             up_ref, pre_ref, act_ref, dh2_ref, dh2b_ref, loss_ref, acc_ref, halo_ref):
        j, i = pl.program_id(0), pl.program_id(1)
        rows = pl.ds(pl.multiple_of(i * tm, tm), tm)

        @pl.when((i == 0) & (j == 0))
        def _():
            loss_ref[...] = jnp.zeros_like(loss_ref)

        u2 = u2_ref[...]
        pre = []
        for s in range(2):
            up = _dot_nt(u2, wu_ref[s])
            up_ref[s] = up.astype(BF16)
            halo = jnp.where(i == 0, 0.0, halo_ref[s])
            pre.append(_conv3(up, _taps(cw_ref.at[s]), halo)[0] + b_ref[s])
            pre_ref[s] = pre[s].astype(BF16)
            halo_ref[s] = up[tm - SUBLANES:]
        g, val = pre
        act = (g * jax.nn.sigmoid(g) * val).astype(BF16)
        act_ref[...] = act
        out = _dot(act, wd_ref[...])

        @pl.when(j == 0)
        def _():
            acc_ref[rows, :] = out

        @pl.when(j > 0)
        def _():
            acc_ref[rows, :] += out

        @pl.when(j == last)
        def _():
            err = h1_ref[...] + acc_ref[rows, :] - tgt_ref[...]
            loss_ref[...] += 0.5 * jnp.sum(err * err) / D_MODEL
            dh2 = err / D_MODEL
            dh2_ref[...] = dh2
            dh2b_ref[...] = dh2.astype(BF16)

    late = lambda w: pl.BlockSpec((tm, w), lambda j, i: (jnp.where(j == last, i, 0), 0))
    pair = lambda *s: pl.BlockSpec((2, None) + s, lambda j, i: (0, j) + (0,) * len(s))
    upb = pl.BlockSpec((2, None, tm, FFN_BLK), lambda j, i: (0, j, i, 0))
    return pl.pallas_call(
        body, name="ffn_fwd", grid=(N_FFN_BLK, n_t),
        in_specs=[pl.BlockSpec((tm, D_MODEL), lambda j, i: (i, 0)), pair(FFN_BLK, D_MODEL), pair(3, 1, FFN_BLK),
                  pair(1, FFN_BLK), pl.BlockSpec((None, FFN_BLK, D_MODEL), lambda j, i: (j, 0, 0)),
                  late(D_MODEL), late(D_MODEL)],
        out_specs=[upb, upb, pl.BlockSpec((None, tm, FFN_BLK), lambda j, i: (j, i, 0)), late(D_MODEL), late(D_MODEL),
                   pl.BlockSpec((SUBLANES, 128), lambda j, i: (0, 0))],
        out_shape=[jax.ShapeDtypeStruct((2, N_FFN_BLK, SEQ, FFN_BLK), BF16),
                   jax.ShapeDtypeStruct((2, N_FFN_BLK, SEQ, FFN_BLK), BF16),
                   jax.ShapeDtypeStruct((N_FFN_BLK, SEQ, FFN_BLK), BF16),
                   jax.ShapeDtypeStruct((SEQ, D_MODEL), F32), jax.ShapeDtypeStruct((SEQ, D_MODEL), BF16),
                   jax.ShapeDtypeStruct((SUBLANES, 128), F32)],
        scratch_shapes=[pltpu.VMEM((SEQ, D_MODEL), F32), pltpu.VMEM((2, SUBLANES, FFN_BLK), F32)],
        compiler_params=_params(("arbitrary", "arbitrary")),
    )(u2, w_up, fcw, fcb, w_down, h1, tgt)


def _ffn_fwd_token_major(x, ycn, yan, w_out, g2, w_up, fcw, fcb, w_down, tgt):
    tm = 512
    n_t = SEQ // tm

    def body(x_ref, ycn_ref, yan_ref, wo_ref, g2_ref, wu_ref, cw_ref, b_ref, wd_ref, tgt_ref,
             h1_ref, u2_ref, up_ref, pre_ref, act_ref, dh2_ref, dh2b_ref, loss_ref, acc_ref, halo_ref):
        i, j = pl.program_id(0), pl.program_id(1)

        @pl.when((i == 0) & (j == 0))
        def _():
            loss_ref[...] = jnp.zeros_like(loss_ref)

        @pl.when(j == 0)
        def _():
            h1 = x_ref[...] + _dot(ycn_ref[...], wo_ref[0:CONV_WIDTH, :]) + _dot(yan_ref[...], wo_ref[CONV_WIDTH:, :])
            h1_ref[...] = h1
            u2_ref[...] = (h1 * _rstd(h1) * g2_ref[...]).astype(BF16)
            acc_ref[...] = jnp.zeros_like(acc_ref)

        u2 = u2_ref[...]
        pre = []
        for s in range(2):
            up = _dot_nt(u2, wu_ref[s])
            up_ref[s] = up.astype(BF16)
            halo = jnp.where(i == 0, 0.0, halo_ref[s, j])
            pre.append(_conv3(up, _taps(cw_ref.at[s]), halo)[0] + b_ref[s])
            pre_ref[s] = pre[s].astype(BF16)
            halo_ref[s, j] = up[tm - SUBLANES:]
        g, val = pre
        act = (g * jax.nn.sigmoid(g) * val).astype(BF16)
        act_ref[...] = act
        acc_ref[...] += _dot(act, wd_ref[...])

        @pl.when(j == N_FFN_BLK - 1)
        def _():
            err = h1_ref[...] + acc_ref[...] - tgt_ref[...]
            loss_ref[...] += 0.5 * jnp.sum(err * err) / D_MODEL
            dh2 = err / D_MODEL
            dh2_ref[...] = dh2
            dh2b_ref[...] = dh2.astype(BF16)

    rows = lambda w: pl.BlockSpec((tm, w), lambda i, j: (i, 0))
    const = lambda shape: pl.BlockSpec(shape, lambda i, j: (0,) * len(shape))
    pair = lambda *s: pl.BlockSpec((2, None) + s, lambda i, j: (0, j) + (0,) * len(s))
    upb = pl.BlockSpec((2, None, tm, FFN_BLK), lambda i, j: (0, j, i, 0))
    return pl.pallas_call(
        body, name="ffn_fwd", grid=(n_t, N_FFN_BLK),
        in_specs=[rows(D_MODEL), rows(CONV_WIDTH), rows(ATTN_WIDTH), const((D_MODEL, D_MODEL)), const((1, D_MODEL)),
                  pair(FFN_BLK, D_MODEL), pair(3, 1, FFN_BLK), pair(1, FFN_BLK),
                  pl.BlockSpec((None, FFN_BLK, D_MODEL), lambda i, j: (j, 0, 0)), rows(D_MODEL)],
        out_specs=[rows(D_MODEL), rows(D_MODEL), upb, upb, pl.BlockSpec((None, tm, FFN_BLK), lambda i, j: (j, i, 0)),
                   rows(D_MODEL), rows(D_MODEL), const((SUBLANES, 128))],
        out_shape=[jax.ShapeDtypeStruct((SEQ, D_MODEL), F32), jax.ShapeDtypeStruct((SEQ, D_MODEL), BF16),
                   jax.ShapeDtypeStruct((2, N_FFN_BLK, SEQ, FFN_BLK), BF16),
                   jax.ShapeDtypeStruct((2, N_FFN_BLK, SEQ, FFN_BLK), BF16),
                   jax.ShapeDtypeStruct((N_FFN_BLK, SEQ, FFN_BLK), BF16),
                   jax.ShapeDtypeStruct((SEQ, D_MODEL), F32), jax.ShapeDtypeStruct((SEQ, D_MODEL), BF16),
                   jax.ShapeDtypeStruct((SUBLANES, 128), F32)],
        scratch_shapes=[pltpu.VMEM((tm, D_MODEL), F32), pltpu.VMEM((2, N_FFN_BLK, SUBLANES, FFN_BLK), F32)],
        compiler_params=_params(("arbitrary", "arbitrary")),
    )(x, ycn, yan, w_out, g2, w_up, fcw, fcb, w_down, tgt)


def _ffn_bwd(dh2, dh2b, h1, g2, up, pre, w_up, fcw, w_down):
    tm = 256
    n_t = SEQ // tm
    last = N_FFN_BLK - 1

    def body(dh2b_ref, up_ref, pre_ref, wu_ref, cw_ref, wd_ref, dh2_ref, h1_ref, g2_ref,
             dup_ref, dh1_ref, dh1b_ref, dfb_ref, dfcw_ref, dg2_ref, acc_ref, next_ref):
        j, i = pl.program_id(0), pl.program_id(1)
        rows = pl.ds(pl.multiple_of((n_t - 1 - i) * tm, tm), tm)

        @pl.when((i == 0) & (j == 0))
        def _():
            dfb_ref[...] = jnp.zeros_like(dfb_ref)
            dfcw_ref[...] = jnp.zeros_like(dfcw_ref)
            dg2_ref[...] = jnp.zeros_like(dg2_ref)

        g, val = pre_ref[0].astype(F32), pre_ref[1].astype(F32)
        sg = jax.nn.sigmoid(g)
        silu = g * sg
        dact = _dot_nt(dh2b_ref[...], wd_ref[...])
        dpre = (dact * val * (sg * (1.0 + g * (1.0 - sg))), dact * silu)
        du = None
        for s in range(2):
            d = dpre[s]
            u = up_ref[s].astype(F32)
            w = _taps(cw_ref.at[s])
            nxt = jnp.where(i == 0, 0.0, next_ref[s])
            d1 = _shift_up(d, 1, nxt)
            d2 = _shift_up(d, 2, nxt)
            next_ref[s] = d[:SUBLANES]
            dfb_ref[s, j] += jnp.sum(d, axis=0, keepdims=True)
            dfcw_ref[s, j, 0] += jnp.sum(d2 * u, axis=0, keepdims=True)
            dfcw_ref[s, j, 1] += jnp.sum(d1 * u, axis=0, keepdims=True)
            dfcw_ref[s, j, 2] += jnp.sum(d * u, axis=0, keepdims=True)
            dup = (d * w[2] + d1 * w[1] + d2 * w[0]).astype(BF16)
            dup_ref[s] = dup
            part = _dot(dup, wu_ref[s])
            du = part if du is None else du + part

        @pl.when(j == 0)
        def _():
            acc_ref[rows, :] = du

        @pl.when(j > 0)
        def _():
            acc_ref[rows, :] += du

        @pl.when(j == last)
        def _():
            dn, dgain = _rms_bwd(h1_ref[...], g2_ref[...], acc_ref[rows, :])
            dh1 = dh2_ref[...] + dn
            dh1_ref[...] = dh1
            dh1b_ref[...] = dh1.astype(BF16)
            dg2_ref[...] += dgain

    rev = lambda i: n_t - 1 - i
    const = lambda shape: pl.BlockSpec(shape, lambda j, i: (0,) * len(shape))
    late = lambda w: pl.BlockSpec((tm, w), lambda j, i: (jnp.where(j == last, rev(i), n_t - 1), 0))
    pair = lambda *s: pl.BlockSpec((2, None) + s, lambda j, i: (0, j) + (0,) * len(s))
    upb = pl.BlockSpec((2, None, tm, FFN_BLK), lambda j, i: (0, j, rev(i), 0))
    return pl.pallas_call(
        body, name="ffn_bwd", grid=(N_FFN_BLK, n_t),
        in_specs=[pl.BlockSpec((tm, D_MODEL), lambda j, i: (rev(i), 0)), upb, upb, pair(FFN_BLK, D_MODEL),
                  pair(3, 1, FFN_BLK), pl.BlockSpec((None, FFN_BLK, D_MODEL), lambda j, i: (j, 0, 0)),
                  late(D_MODEL), late(D_MODEL), const((1, D_MODEL))],
        out_specs=[upb, late(D_MODEL), late(D_MODEL),
                   const((2, N_FFN_BLK, 1, FFN_BLK)), const((2, N_FFN_BLK, 3, 1, FFN_BLK)), const((1, D_MODEL))],
        out_shape=[jax.ShapeDtypeStruct((2, N_FFN_BLK, SEQ, FFN_BLK), BF16), jax.ShapeDtypeStruct((SEQ, D_MODEL), F32),
                   jax.ShapeDtypeStruct((SEQ, D_MODEL), BF16), jax.ShapeDtypeStruct((2, N_FFN_BLK, 1, FFN_BLK), F32),
                   jax.ShapeDtypeStruct((2, N_FFN_BLK, 3, 1, FFN_BLK), F32), jax.ShapeDtypeStruct((1, D_MODEL), F32)],
        scratch_shapes=[pltpu.VMEM((SEQ, D_MODEL), F32), pltpu.VMEM((2, SUBLANES, FFN_BLK), F32)],
        compiler_params=_params(("arbitrary", "arbitrary")),
    )(dh2b, up, pre, w_up, fcw, w_down, dh2, h1, g2)


def _ffn_bwd_token_major(dh2, dh2b, h1, g2, up, pre, w_up, fcw, w_down):
    tm = 512
    n_t = SEQ // tm

    def body(dh2_ref, dh2b_ref, h1_ref, g2_ref, up_ref, pre_ref, wu_ref, cw_ref, wd_ref,
             dup_ref, dh1_ref, dh1b_ref, dfb_ref, dfcw_ref, dg2_ref, acc_ref, next_ref):
        i, j = pl.program_id(0), pl.program_id(1)

        @pl.when((i == 0) & (j == 0))
        def _():
            dfb_ref[...] = jnp.zeros_like(dfb_ref)
            dfcw_ref[...] = jnp.zeros_like(dfcw_ref)
            dg2_ref[...] = jnp.zeros_like(dg2_ref)

        @pl.when(j == 0)
        def _():
            acc_ref[...] = jnp.zeros_like(acc_ref)

        g, val = pre_ref[0].astype(F32), pre_ref[1].astype(F32)
        sg = jax.nn.sigmoid(g)
        silu = g * sg
        dact = _dot_nt(dh2b_ref[...], wd_ref[...])
        dpre = (dact * val * (sg * (1.0 + g * (1.0 - sg))), dact * silu)
        for s in range(2):
            d = dpre[s]
            u = up_ref[s].astype(F32)
            w = _taps(cw_ref.at[s])
            nxt = jnp.where(i == 0, 0.0, next_ref[s, j])
            d1 = _shift_up(d, 1, nxt)
            d2 = _shift_up(d, 2, nxt)
            next_ref[s, j] = d[:SUBLANES]
            dfb_ref[s, j] += jnp.sum(d, axis=0, keepdims=True)
            dfcw_ref[s, j, 0] += jnp.sum(d2 * u, axis=0, keepdims=True)
            dfcw_ref[s, j, 1] += jnp.sum(d1 * u, axis=0, keepdims=True)
            dfcw_ref[s, j, 2] += jnp.sum(d * u, axis=0, keepdims=True)
            dup = (d * w[2] + d1 * w[1] + d2 * w[0]).astype(BF16)
            dup_ref[s] = dup
            acc_ref[...] += _dot(dup, wu_ref[s])

        @pl.when(j == N_FFN_BLK - 1)
        def _():
            dn, dgain = _rms_bwd(h1_ref[...], g2_ref[...], acc_ref[...])
            dh1 = dh2_ref[...] + dn
            dh1_ref[...] = dh1
            dh1b_ref[...] = dh1.astype(BF16)
            dg2_ref[...] += dgain

    rev = lambda i: n_t - 1 - i
    rows = lambda w: pl.BlockSpec((tm, w), lambda i, j: (rev(i), 0))
    const = lambda shape: pl.BlockSpec(shape, lambda i, j: (0,) * len(shape))
    pair = lambda *s: pl.BlockSpec((2, None) + s, lambda i, j: (0, j) + (0,) * len(s))
    upb = pl.BlockSpec((2, None, tm, FFN_BLK), lambda i, j: (0, j, rev(i), 0))
    return pl.pallas_call(
        body, name="ffn_bwd", grid=(n_t, N_FFN_BLK),
        in_specs=[rows(D_MODEL), rows(D_MODEL), rows(D_MODEL), const((1, D_MODEL)), upb, upb,
                  pair(FFN_BLK, D_MODEL), pair(3, 1, FFN_BLK),
                  pl.BlockSpec((None, FFN_BLK, D_MODEL), lambda i, j: (j, 0, 0))],
        out_specs=[upb, rows(D_MODEL), rows(D_MODEL),
                   const((2, N_FFN_BLK, 1, FFN_BLK)), const((2, N_FFN_BLK, 3, 1, FFN_BLK)), const((1, D_MODEL))],
        out_shape=[jax.ShapeDtypeStruct((2, N_FFN_BLK, SEQ, FFN_BLK), BF16), jax.ShapeDtypeStruct((SEQ, D_MODEL), F32),
                   jax.ShapeDtypeStruct((SEQ, D_MODEL), BF16), jax.ShapeDtypeStruct((2, N_FFN_BLK, 1, FFN_BLK), F32),
                   jax.ShapeDtypeStruct((2, N_FFN_BLK, 3, 1, FFN_BLK), F32), jax.ShapeDtypeStruct((1, D_MODEL), F32)],
        scratch_shapes=[pltpu.VMEM((tm, D_MODEL), F32), pltpu.VMEM((2, N_FFN_BLK, SUBLANES, FFN_BLK), F32)],
        compiler_params=_params(("arbitrary", "arbitrary")),
    )(dh2, dh2b, h1, g2, up, pre, w_up, fcw, w_down)


def _grad_tn(a_list, b, out_rows, name, after=None):
    n = len(a_list)
    ncol = b.shape[1]

    def body(*refs):
        a_refs, b_ref, o_ref = refs[:n], refs[n], refs[n + 1]
        j = pl.program_id(0)
        for k in range(n):
            @pl.when(j == k)
            def _(k=k):
                o_ref[...] = _dot_tn(a_refs[k][...], b_ref[...]).astype(BF16)

    full = lambda shape: pl.BlockSpec(shape, lambda j: (0,) * len(shape))
    body, more_specs, more = _ordered_behind(body, n + 1, after)
    return pl.pallas_call(
        body, name=name, grid=(n,),
        in_specs=[full((SEQ, out_rows))] * n + [full((SEQ, ncol))] + more_specs,
        out_specs=pl.BlockSpec((None, out_rows, ncol), lambda j: (j, 0, 0)),
        out_shape=jax.ShapeDtypeStruct((n, out_rows, ncol), BF16),
        compiler_params=_params(("arbitrary",)),
    )(*a_list, b, *more)


def _grad_tn_blocked(a, b, name, a_is_blocked):
    nb = a.shape[0] if a_is_blocked else b.shape[0]
    a_w, b_w = a.shape[-1], b.shape[-1]

    def body(a_ref, b_ref, o_ref):
        o_ref[...] = _dot_tn(a_ref[...], b_ref[...]).astype(BF16)

    blocked = lambda w: pl.BlockSpec((None, SEQ, w), lambda k: (k, 0, 0))
    full = lambda w: pl.BlockSpec((SEQ, w), lambda k: (0, 0))
    return pl.pallas_call(
        body, name=name, grid=(nb,),
        in_specs=[blocked(a_w) if a_is_blocked else full(a_w), full(b_w) if a_is_blocked else blocked(b_w)],
        out_specs=pl.BlockSpec((None, a_w, b_w), lambda k: (k, 0, 0)),
        out_shape=jax.ShapeDtypeStruct((nb, a_w, b_w), BF16),
        compiler_params=_params(("arbitrary",)),
    )(a, b)


def _out_bwd(dh1b, w_out, y_attn, gattn, after=None):
    tm = 512
    n_t = SEQ // tm

    def body(dh_ref, wo_ref, y_ref, ga_ref, dycn_ref, dy_ref, dga_ref):
        @pl.when(pl.program_id(0) == 0)
        def _():
            dga_ref[...] = jnp.zeros_like(dga_ref)

        dycat = _dot_nt(dh_ref[...], wo_ref[...])
        dycn_ref[...] = dycat[:, :CONV_WIDTH]
        dy, dga = _rms_bwd(y_ref[...], ga_ref[...], dycat[:, CONV_WIDTH:])
        dy_ref[...] = dy
        dga_ref[...] += dga

    rows = lambda w: pl.BlockSpec((tm, w), lambda i: (i, 0))
    const = lambda shape: pl.BlockSpec(shape, lambda i: (0,) * len(shape))
    body, more_specs, more = _ordered_behind(body, 4, after)
    return pl.pallas_call(
        body, name="out_bwd", grid=(n_t,),
        in_specs=[rows(D_MODEL), const((D_MODEL, D_MODEL)), rows(ATTN_WIDTH), const((1, ATTN_WIDTH))] + more_specs,
        out_specs=[rows(CONV_WIDTH), rows(ATTN_WIDTH), const((1, ATTN_WIDTH))],
        out_shape=[jax.ShapeDtypeStruct((SEQ, CONV_WIDTH), F32), jax.ShapeDtypeStruct((SEQ, ATTN_WIDTH), F32),
                   jax.ShapeDtypeStruct((1, ATTN_WIDTH), F32)],
        compiler_params=_params(("arbitrary",)),
    )(dh1b, w_out, y_attn, gattn, *more)


def _attn_bwd(qn, kn, v, dy, tbl, sinks, bkt, after=None):
    n_b = SEQ // BLK

    def body(q_ref, k_ref, v_ref, dy_ref, tbl_ref, sink_ref, bkt_ref,
             dq_ref, dk_ref, dv_ref, dtbl_ref, dsink_ref, bias_ref, dbias_ref, dsacc_ref):
        i = pl.program_id(0)

        @pl.when(i == 0)
        def _():
            _band_bias(tbl_ref, bkt_ref[...], bias_ref)
            dbias_ref[...] = jnp.zeros_like(dbias_ref)
            dsacc_ref[...] = jnp.zeros_like(dsacc_ref)
            dk_ref[...] = jnp.zeros_like(dk_ref)
            dv_ref[...] = jnp.zeros_like(dv_ref)

        kb, prev, cur = _band_rows(k_ref, i)
        vb, _, _ = _band_rows(v_ref, i)
        upper, dead = _band_masks(i)
        q = q_ref[...]
        dy = dy_ref[...]
        dqs, dks, dvs = [], [], []
        for g in range(N_HEADS // GQA_GROUP):
            kv = slice(HEAD_DIM * g, HEAD_DIM * (g + 1))
            qg = _stack_heads(q, g)
            dog = _stack_heads(dy, g).astype(BF16)
            sink = _per_head_rows([sink_ref[0, GQA_GROUP * g + t] for t in range(GQA_GROUP)])
            probs, psink = _head_probs(qg, kb[:, kv], bias_ref[g], upper, dead, sink)
            dprobs = _fold(_dot_nt(dog, vb[:, kv]), upper)
            dvs.append(_dot_tn(_unfold(probs, upper).astype(BF16), dog))
            dsum = jnp.sum(probs * dprobs, axis=-1, keepdims=True)
            dlogits = probs * (dprobs - dsum)
            dsacc_ref[g] += jnp.broadcast_to(-psink * dsum, (GROUP_ROWS, 128))
            dbias_ref[g] += dlogits
            ds = _unfold(dlogits * (HEAD_DIM ** -0.5), upper).astype(BF16)
            dqs.append(_dot(ds, kb[:, kv]))
            dks.append(_dot_tn(ds, qg))
        dq_ref[...] = _unstack_heads(dqs)
        dkb = jnp.concatenate(dks, axis=-1)
        dvb = jnp.concatenate(dvs, axis=-1)
        dk_ref[pl.ds(prev, BLK), :] += dkb[:BLK]
        dk_ref[pl.ds(cur, BLK), :] += dkb[BLK:]
        dv_ref[pl.ds(prev, BLK), :] += dvb[:BLK]
        dv_ref[pl.ds(cur, BLK), :] += dvb[BLK:]

        @pl.when(i == n_b - 1)
        def _():
            bkt = bkt_ref[...]
            row8 = lax.broadcasted_iota(jnp.int32, (N_HEADS, 128), 0)
            lane8 = lax.broadcasted_iota(jnp.int32, (N_HEADS, 128), 1)
            lane1 = lax.broadcasted_iota(jnp.int32, (1, 128), 1)
            acc = jnp.zeros((N_HEADS, 128), F32)
            dsink = jnp.zeros((1, 128), F32)
            for h in range(N_HEADS):
                rows = slice(BLK * (h % GQA_GROUP), BLK * (h % GQA_GROUP + 1))
                dsink = jnp.where(lane1 == h, jnp.sum(dsacc_ref[h // GQA_GROUP, rows, :], axis=0, keepdims=True), dsink)
                dbh = dbias_ref[h // GQA_GROUP, rows, :]
                for b in range(NUM_BUCKETS):
                    acc = jnp.where((row8 == h) & (lane8 == b), jnp.sum(jnp.where(bkt == b, dbh, 0.0)), acc)
            dsink_ref[...] = dsink
            dtbl_ref[...] = acc

    const = lambda shape: pl.BlockSpec(shape, lambda i: (0,) * len(shape))
    rows = lambda w: pl.BlockSpec((BLK, w), lambda i: (i, 0))
    smem = pl.BlockSpec(memory_space=pltpu.SMEM)
    body, more_specs, more = _ordered_behind(body, 7, after)
    return pl.pallas_call(
        body, name="attn_bwd", grid=(n_b,),
        in_specs=[rows(ATTN_WIDTH), const((SEQ, KV_WIDTH)), const((SEQ, KV_WIDTH)), rows(ATTN_WIDTH), smem, smem,
                  const((BLK, BLK))] + more_specs,
        out_specs=[rows(ATTN_WIDTH), const((SEQ, KV_WIDTH)), const((SEQ, KV_WIDTH)), const((N_HEADS, 128)), const((1, 128))],
        out_shape=[jax.ShapeDtypeStruct((SEQ, ATTN_WIDTH), F32), jax.ShapeDtypeStruct((SEQ, KV_WIDTH), F32),
                   jax.ShapeDtypeStruct((SEQ, KV_WIDTH), F32), jax.ShapeDtypeStruct((N_HEADS, 128), F32),
                   jax.ShapeDtypeStruct((1, 128), F32)],
        scratch_shapes=[pltpu.VMEM((N_HEADS // GQA_GROUP, GROUP_ROWS, BLK), F32)] * 3,
        compiler_params=_params(("arbitrary",)),
    )(qn, kn, v, dy, tbl, sinks, bkt, *more)


def _mix_in_bwd(x, dh1, proj, dycn, dqn, dkn, dv, w_in_t, conv_w, g1, gq, gk, gconv):
    tm = 256
    n_t = SEQ // tm
    halo_blocks = tm // SUBLANES

    def body(x_ref, dh1_ref, proj_ref, halo_ref, dycn_ref, dqn_ref, dkn_ref, dv_ref, w_ref, cw_ref,
             g1_ref, gq_ref, gk_ref, gc_ref,
             dx_ref, dproj_ref, u1_ref, dcw_ref, dgc_ref, dgq_ref, dgk_ref, dg1_ref, next_ref):
        i = pl.program_id(0)
        first_tile = i == n_t - 1

        @pl.when(i == 0)
        def _():
            for r in (dcw_ref, dgc_ref, dgq_ref, dgk_ref, dg1_ref, next_ref):
                r[...] = jnp.zeros_like(r)

        proj = proj_ref[...]
        hp = halo_ref[...]
        gate_b = proj[:, 0:CONV_WIDTH]
        gate_c = proj[:, CONV_WIDTH:2 * CONV_WIDTH]
        hc = proj[:, 2 * CONV_WIDTH:3 * CONV_WIDTH]
        a = gate_c * hc
        a_halo = jnp.where(first_tile, 0.0, hp[:, CONV_WIDTH:2 * CONV_WIDTH] * hp[:, 2 * CONV_WIDTH:3 * CONV_WIDTH])
        cw = _taps(cw_ref[...])
        cv, a2, a1 = _conv3(a, cw, a_halo)
        dyc, dgc = _rms_bwd(gate_b * cv, gc_ref[...], dycn_ref[...])
        dgc_ref[...] += dgc
        dcv = dyc * gate_b
        dcw_ref[...] += jnp.concatenate(
            [jnp.sum(dcv * a2, axis=0, keepdims=True), jnp.sum(dcv * a1, axis=0, keepdims=True),
             jnp.sum(dcv * a, axis=0, keepdims=True)], axis=0)
        da = _conv3_bwd_input(dcv, cw, next_ref[...])
        next_ref[...] = dcv[:SUBLANES]
        q0 = 3 * CONV_WIDTH
        k0 = q0 + ATTN_WIDTH
        dq, dgq = _head_norm_bwd(proj[:, q0:k0], gq_ref[...], dqn_ref[...], N_HEADS)
        dk, dgk = _head_norm_bwd(proj[:, k0:k0 + KV_WIDTH], gk_ref[...], dkn_ref[...], 2)
        dgq_ref[...] += dgq
        dgk_ref[...] += dgk
        dproj = jnp.concatenate([dyc * cv, da * hc, da * gate_c, dq, dk, dv_ref[...]], axis=-1).astype(BF16)
        dproj_ref[...] = dproj
        du1 = _dot(dproj, w_ref[...])
        xv = x_ref[...]
        dn, dg1 = _rms_bwd(xv, g1_ref[...], du1)
        dx_ref[...] = dh1_ref[...] + dn
        dg1_ref[...] += dg1
        u1_ref[...] = (xv * _rstd(xv) * g1_ref[...]).astype(BF16)

    rev = lambda i: n_t - 1 - i
    rows = lambda w: pl.BlockSpec((tm, w), lambda i: (rev(i), 0))
    const = lambda shape: pl.BlockSpec(shape, lambda i: (0,) * len(shape))
    halo = pl.BlockSpec((SUBLANES, IN_WIDTH), lambda i: (jnp.maximum(rev(i) * halo_blocks - 1, 0), 0))
    return pl.pallas_call(
        body, name="mix_in_bwd", grid=(n_t,),
        in_specs=[rows(D_MODEL), rows(D_MODEL), rows(IN_WIDTH), halo, rows(CONV_WIDTH), rows(ATTN_WIDTH), rows(KV_WIDTH),
                  rows(KV_WIDTH), const((IN_WIDTH, D_MODEL)), const((3, CONV_WIDTH)), const((1, D_MODEL)),
                  const((1, HEAD_DIM)), const((1, HEAD_DIM)), const((1, CONV_WIDTH))],
        out_specs=[rows(D_MODEL), rows(IN_WIDTH), rows(D_MODEL), const((3, CONV_WIDTH)), const((1, CONV_WIDTH)),
                   const((1, HEAD_DIM)), const((1, HEAD_DIM)), const((1, D_MODEL))],
        out_shape=[jax.ShapeDtypeStruct((SEQ, D_MODEL), F32), jax.ShapeDtypeStruct((SEQ, IN_WIDTH), BF16),
                   jax.ShapeDtypeStruct((SEQ, D_MODEL), BF16), jax.ShapeDtypeStruct((3, CONV_WIDTH), F32),
                   jax.ShapeDtypeStruct((1, CONV_WIDTH), F32), jax.ShapeDtypeStruct((1, HEAD_DIM), F32),
                   jax.ShapeDtypeStruct((1, HEAD_DIM), F32), jax.ShapeDtypeStruct((1, D_MODEL), F32)],
        scratch_shapes=[pltpu.VMEM((SUBLANES, CONV_WIDTH), F32)],
        compiler_params=_params(("arbitrary",)),
    )(x, dh1, proj, proj, dycn, dqn, dkn, dv, w_in_t, conv_w, g1, gq, gk, gconv)


def _grad_w_in(dproj, u1, after=None):
    bw = 768

    def body(a_ref, b_ref, o_ref):
        o_ref[...] = _dot_tn(a_ref[...], b_ref[...]).astype(BF16)

    body, more_specs, more = _ordered_behind(body, 2, after)
    return pl.pallas_call(
        body, name="grad_w_in", grid=(IN_WIDTH // bw,),
        in_specs=[pl.BlockSpec((SEQ, bw), lambda k: (0, k)), pl.BlockSpec((SEQ, D_MODEL), lambda k: (0, 0))] + more_specs,
        out_specs=pl.BlockSpec((bw, D_MODEL), lambda k: (k, 0)),
        out_shape=jax.ShapeDtypeStruct((IN_WIDTH, D_MODEL), BF16),
        compiler_params=_params(("arbitrary",)),
    )(dproj, u1, *more)


def _adamw_math(w, g, m, v):
    m = ADAM_B1 * m + (1.0 - ADAM_B1) * g
    v = ADAM_B2 * v + (1.0 - ADAM_B2) * (g * g)
    m_hat = m / (1.0 - ADAM_B1 ** ADAM_STEP)
    v_hat = v / (1.0 - ADAM_B2 ** ADAM_STEP)
    return -ADAM_LR * (m_hat / (jnp.sqrt(v_hat) + ADAM_EPS) + ADAM_WD * w), m, v


_ROW_G1, _ROW_G2, _ROW_OUT_NORMS, _ROW_FFN_B, _ROW_GQ, _ROW_GK, _ROW_SINKS, _ROW_LOSS, _ROW_TABLE = 0, 1, 2, 3, 11, 12, 13, 14, 16
SMALL_ROWS, SMALL_COLS = 24, 1024
_SMALL_NAMES = ("norm_mix_g", "norm_ffn_g", "out_norm_conv_g", "out_norm_attn_g", "ffn_conv_b", "q_norm_g", "k_norm_g",
                "sinks", "rel_bias_table")


def _pack_small_grads(dg1, dg2, dgconv, dgattn, dfb, dgq, dgk, dsinks, dtbl_t, loss_acc):
    def body(dg1_ref, dg2_ref, dgc_ref, dga_ref, dfb_ref, dgq_ref, dgk_ref, ds_ref, dt_ref, loss_ref, o_ref):
        o_ref[...] = jnp.zeros_like(o_ref)
        o_ref[_ROW_G1:_ROW_G1 + 1, :] = dg1_ref[...]
        o_ref[_ROW_G2:_ROW_G2 + 1, :] = dg2_ref[...]
        o_ref[_ROW_OUT_NORMS:_ROW_OUT_NORMS + 1, 0:CONV_WIDTH] = dgc_ref[...]
        o_ref[_ROW_OUT_NORMS:_ROW_OUT_NORMS + 1, CONV_WIDTH:] = dga_ref[...]
        for k in range(N_DEV):
            o_ref[_ROW_FFN_B + k:_ROW_FFN_B + k + 1, 0:FFN_BLK] = dfb_ref[k // N_FFN_BLK, k % N_FFN_BLK]
        o_ref[_ROW_GQ:_ROW_GQ + 1, 0:HEAD_DIM] = dgq_ref[...]
        o_ref[_ROW_GK:_ROW_GK + 1, 0:HEAD_DIM] = dgk_ref[...]
        o_ref[_ROW_SINKS:_ROW_SINKS + 1, 0:128] = ds_ref[...]
        o_ref[_ROW_LOSS:_ROW_LOSS + 1, 0:128] = loss_ref[0:1, :]
        o_ref[_ROW_TABLE:_ROW_TABLE + N_HEADS, 0:128] = dt_ref[...]

    return pl.pallas_call(body, name="pack_small_grads", out_shape=jax.ShapeDtypeStruct((SMALL_ROWS, SMALL_COLS), F32))(
        dg1, dg2, dgconv, dgattn, dfb, dgq, dgk, dsinks, dtbl_t, loss_acc)


def _adamw_small(recv, params, after):
    names = _SMALL_NAMES
    n = len(names)

    def grad_of(g, name, k=None):
        if name == "norm_mix_g":
            return g[_ROW_G1:_ROW_G1 + 1, :]
        if name == "norm_ffn_g":
            return g[_ROW_G2:_ROW_G2 + 1, :]
        if name == "out_norm_conv_g":
            return g[_ROW_OUT_NORMS:_ROW_OUT_NORMS + 1, 0:CONV_WIDTH]
        if name == "out_norm_attn_g":
            return g[_ROW_OUT_NORMS:_ROW_OUT_NORMS + 1, CONV_WIDTH:]
        if name == "ffn_conv_b":
            return g[_ROW_FFN_B + k:_ROW_FFN_B + k + 1, 0:FFN_BLK]
        if name == "q_norm_g":
            return g[_ROW_GQ:_ROW_GQ + 1, 0:HEAD_DIM]
        if name == "k_norm_g":
            return g[_ROW_GK:_ROW_GK + 1, 0:HEAD_DIM]
        if name == "sinks":
            return g[_ROW_SINKS:_ROW_SINKS + 1, 0:N_HEADS]
        return g[_ROW_TABLE:_ROW_TABLE + N_HEADS, 0:NUM_BUCKETS]

    def body(r_ref, *refs):
        ins, outs, loss_ref = refs[:3 * n], refs[3 * n:7 * n], refs[7 * n]
        g = r_ref[0]
        for s in range(1, N_DEV):
            g = g + r_ref[s]
        loss_ref[...] = g[_ROW_LOSS:_ROW_LOSS + 1, 0:128]
        for i, name in enumerate(names):
            w_ref, m_ref, v_ref = ins[3 * i:3 * i + 3]
            o = outs[4 * i:4 * i + 4]
            cols = [slice(FFN_BLK * k, FFN_BLK * (k + 1)) for k in range(N_DEV)] if name == "ffn_conv_b" else [slice(None)]
            for k, cs in enumerate(cols):
                gk = grad_of(g, name, k)
                d, m2, v2 = _adamw_math(w_ref[:, cs], gk, m_ref[:, cs], v_ref[:, cs])
                o[0][:, cs], o[1][:, cs], o[2][:, cs], o[3][:, cs] = gk, d, m2, v2

    flat = [a for name in names for a in params[name]]
    body, more_specs, more = _ordered_behind(body, 1 + 3 * n, after)
    vmem = pl.BlockSpec(memory_space=pltpu.VMEM)
    out = pl.pallas_call(
        body, name="adamw_small",
        in_specs=[vmem] * (1 + 3 * n) + more_specs,
        out_shape=[jax.ShapeDtypeStruct(params[name][0].shape, F32) for name in names for _ in range(4)]
        + [jax.ShapeDtypeStruct((1, 128), F32)],
        compiler_params=pltpu.CompilerParams(vmem_limit_bytes=VMEM_LIMIT),
    )(recv, *flat, *more)
    return {name: tuple(out[4 * i:4 * i + 4]) for i, name in enumerate(names)}, out[4 * n]


def _adamw(w, m, v, part, recv, chip, name, row_blocks=1, after=None):
    rb = w.shape[0] // row_blocks
    tail = w.shape[1:]
    zeros = (0,) * len(tail)

    def body(chip_ref, w_ref, m_ref, v_ref, p_ref, r_ref, g_o, d_o, m_o, v_o):
        g = p_ref[...].astype(F32)
        for s in range(3):
            g = g + r_ref[s].astype(F32)
        g_o[...] = g
        d_o[...], m_o[...], v_o[...] = _adamw_math(w_ref[...], g, m_ref[...], v_ref[...])

    blk = pl.BlockSpec((rb,) + tail, lambda i, chip_ref: (i,) + zeros)
    pblk = pl.BlockSpec((None, rb) + tail, lambda i, chip_ref: (chip_ref[0], i) + zeros)
    rblk = pl.BlockSpec((3, rb) + tail, lambda i, chip_ref: (0, i) + zeros)
    body, more_specs, more = _ordered_behind(body, 6, after)
    return pl.pallas_call(
        body, name=name,
        grid_spec=pltpu.PrefetchScalarGridSpec(num_scalar_prefetch=1, grid=(row_blocks,),
                                               in_specs=[blk, blk, blk, pblk, rblk] + more_specs, out_specs=[blk] * 4),
        out_shape=[jax.ShapeDtypeStruct(w.shape, F32)] * 4,
        compiler_params=_params(("arbitrary",)),
    )(chip, w, m, v, part, recv, *more)


def kernel(x, norm_mix_g, w_in, conv_w, q_norm_g, k_norm_g, rel_bias_table, sinks, out_norm_conv_g, out_norm_attn_g, w_out, norm_ffn_g, w_up, ffn_conv_w, ffn_conv_b, w_down, loss_target, m_norm_mix_g, m_w_in, m_conv_w, m_q_norm_g, m_k_norm_g, m_rel_bias_table, m_sinks, m_out_norm_conv_g, m_out_norm_attn_g, m_w_out, m_norm_ffn_g, m_w_up, m_ffn_conv_w, m_ffn_conv_b, m_w_down, v_norm_mix_g, v_w_in, v_conv_w, v_q_norm_g, v_k_norm_g, v_rel_bias_table, v_sinks, v_out_norm_conv_g, v_out_norm_attn_g, v_w_out, v_norm_ffn_g, v_w_up, v_ffn_conv_w, v_ffn_conv_b, v_w_down):
    p = dict(norm_mix_g=norm_mix_g, w_in=w_in, conv_w=conv_w, q_norm_g=q_norm_g, k_norm_g=k_norm_g,
             rel_bias_table=rel_bias_table, sinks=sinks, out_norm_conv_g=out_norm_conv_g, out_norm_attn_g=out_norm_attn_g,
             w_out=w_out, norm_ffn_g=norm_ffn_g, w_up=w_up, ffn_conv_w=ffn_conv_w, ffn_conv_b=ffn_conv_b, w_down=w_down)
    m = dict(norm_mix_g=m_norm_mix_g, w_in=m_w_in, conv_w=m_conv_w, q_norm_g=m_q_norm_g, k_norm_g=m_k_norm_g,
             rel_bias_table=m_rel_bias_table, sinks=m_sinks, out_norm_conv_g=m_out_norm_conv_g,
             out_norm_attn_g=m_out_norm_attn_g, w_out=m_w_out, norm_ffn_g=m_norm_ffn_g, w_up=m_w_up,
             ffn_conv_w=m_ffn_conv_w, ffn_conv_b=m_ffn_conv_b, w_down=m_w_down)
    v = dict(norm_mix_g=v_norm_mix_g, w_in=v_w_in, conv_w=v_conv_w, q_norm_g=v_q_norm_g, k_norm_g=v_k_norm_g,
             rel_bias_table=v_rel_bias_table, sinks=v_sinks, out_norm_conv_g=v_out_norm_conv_g,
             out_norm_attn_g=v_out_norm_attn_g, w_out=v_w_out, norm_ffn_g=v_norm_ffn_g, w_up=v_w_up,
             ffn_conv_w=v_ffn_conv_w, ffn_conv_b=v_ffn_conv_b, w_down=v_w_down)

    xs, tgt = x[0], loss_target[0]
    g1, g2, gq, gk, gconv, gattn = norm_mix_g, norm_ffn_g, q_norm_g, k_norm_g, out_norm_conv_g, out_norm_attn_g
    ix, iy, ic = _coords()
    core = ic.astype(jnp.int32).reshape(1)
    chip = (2 * ix + iy).astype(jnp.int32).reshape(1)
    me = _lin(ix, iy, ic).astype(jnp.int32).reshape(1)
    bkt = jnp.asarray(_bucket_map())
    tr = lambda a: a[0].T
    taps = lambda a: jnp.transpose(a, (1, 0, 2))
    tbl_t = rel_bias_table.T

    wi_l, cw_l, wo_l, wu_l, wd_l, fcw_l = _place_shards(
        me, [tr(w_in), taps(conv_w), w_out[0], tr(w_up), w_down[0], taps(ffn_conv_w)], [BF16, F32, BF16, BF16, BF16, F32])
    finish_a, token_a = _all_gather_split([wi_l, cw_l], "mixer", None)
    ffn_stage2, ffn_stage3, token_b = _all_gather_tree([wo_l, wu_l, wd_l, fcw_l], "ffn", token_a)
    wi_g, cw_g = finish_a(token_b)
    w_in_t = wi_g.reshape(IN_WIDTH, D_MODEL)
    conv_w_f = jnp.transpose(cw_g[:, :, 0, :], (1, 0, 2)).reshape(3, CONV_WIDTH)

    proj, ycn, qn, kn, vv = _mix_in_fwd(xs, g1, w_in_t, conv_w_f, gq, gk, gconv)
    token_b2 = ffn_stage2(ycn)
    y_attn, yan = _attn_fwd(qn, kn, vv, tbl_t, sinks, bkt, gattn, after=token_b2)
    wo_g, wu_g, wd_g, fcw_g = ffn_stage3(yan)
    w_out_f = wo_g.reshape(D_MODEL, D_MODEL)
    w_down_f = wd_g.reshape(N_FFN_BLK, FFN_BLK, D_MODEL)
    w_up_f = wu_g.reshape(2, N_FFN_BLK, FFN_BLK, D_MODEL)
    fcw_f = fcw_g.reshape(2, N_FFN_BLK, 3, 1, FFN_BLK)
    fcb = ffn_conv_b.reshape(2, N_FFN_BLK, 1, FFN_BLK)
    h1, u2, up, pre, act, dh2, dh2b, loss_acc = _ffn_fwd_token_major(
        xs, ycn, yan, w_out_f, g2, w_up_f, fcw_f, fcb, w_down_f, tgt)

    dup, dh1, dh1b, dfb, dfcw, dg2 = _ffn_bwd_token_major(dh2, dh2b, h1, g2, up, pre, w_up_f, fcw_f, w_down_f)
    dw_down = _grad_tn_blocked(act, dh2b, "grad_w_down", a_is_blocked=True).reshape(N_DEV, D_FF // N_DEV, D_MODEL)
    dw_up = _grad_tn_blocked(dup.reshape(N_DEV, SEQ, FFN_BLK), u2, "grad_w_up", a_is_blocked=True)
    dw_out = _grad_tn([ycn, yan], dh1b, CONV_WIDTH, "grad_w_out").reshape(N_DEV, D_MODEL // N_DEV, D_MODEL)
    out_bwd = {}

    def behind_ffn(token):
        out_bwd["r"] = _out_bwd(dh1b, w_out_f, y_attn, gattn, after=token)
        return out_bwd["r"][0]

    finish_ffn, token_ffn = _reduce_scatter_split(
        [dw_down, dw_up, dw_out, dfcw.reshape(N_DEV, 3, 1, FFN_BLK)], "ffn", core, behind_ffn)
    dycn, dy_attn, dgattn = out_bwd["r"]
    dqn, dkn, dv, dtbl_t, dsinks = _attn_bwd(qn, kn, vv, dy_attn, tbl_t, sinks, bkt, after=token_ffn)
    dx, dproj, u1, dcw, dgconv, dgq, dgk, dg1 = _mix_in_bwd(xs, dh1, proj, dycn, dqn, dkn, dv, w_in_t, conv_w_f,
                                                             g1, gq, gk, gconv)
    packed = _pack_small_grads(dg1, dg2, dgconv, dgattn, dfb, dgq, dgk, dsinks, dtbl_t, loss_acc)
    plan_s, slots_s = _broadcast_plan()
    s_sem, r_sem, src_s, land_s, token_s = _split_start(
        "gather_small_start", [packed], [jnp.broadcast_to(packed[None], (N_DEV,) + packed.shape)], plan_s, None)
    dw_in_t = _grad_w_in(dproj, u1, after=token_s).reshape(N_DEV, IN_WIDTH // N_DEV, D_MODEL)
    dcw_b = jnp.transpose(dcw.reshape(3, N_DEV, 1, CONV_WIDTH // N_DEV), (1, 0, 2, 3))
    adam = {}
    ffn_got = {}

    def behind_mixer(token):
        ffn_got["r"] = finish_ffn(token)
        return ffn_got["r"][1][0]

    finish_mixer, token_mixer = _reduce_scatter_split([dw_in_t, dcw_b], "mixer", core, behind_mixer)
    (p_wd, p_wu, p_wo, p_fcw), (r_wd, r_wu, r_wo, r_fcw) = ffn_got["r"]
    adam["w_down"] = _adamw(w_down[0], m_w_down[0], v_w_down[0], p_wd, r_wd, chip, "adamw_w_down", row_blocks=2,
                            after=token_mixer)
    adam_up = _adamw(tr(w_up), tr(m_w_up), tr(v_w_up), p_wu, r_wu, chip, "adamw_w_up", row_blocks=4,
                     after=adam["w_down"][0])
    adam["w_out"] = _adamw(w_out[0], m_w_out[0], v_w_out[0], p_wo, r_wo, chip, "adamw_w_out", after=adam_up[0])
    adam_fcw = _adamw(taps(ffn_conv_w), taps(m_ffn_conv_w), taps(v_ffn_conv_w), p_fcw, r_fcw, chip, "adamw_ffn_conv_w",
                      after=adam["w_out"][0])
    _, (r_small,) = _split_wait("gather_small_wait", s_sem, r_sem, src_s, land_s, plan_s, slots_s, adam_fcw[0])
    small_in = {k: (p[k], m[k], v[k]) for k in _SMALL_NAMES}
    small_in["rel_bias_table"] = (tbl_t, m_rel_bias_table.T, v_rel_bias_table.T)
    small_out, loss_row = _adamw_small(r_small, small_in, None)
    (p_wi, p_cw), (r_wi, r_cw) = finish_mixer(loss_row)
    adam_in = _adamw(tr(w_in), tr(m_w_in), tr(v_w_in), p_wi, r_wi, chip, "adamw_w_in")
    adam_cw = _adamw(taps(conv_w), taps(m_conv_w), taps(v_conv_w), p_cw, r_cw, chip, "adamw_conv_w")

    res = {k: tuple(a[None] for a in t) for k, t in adam.items()}
    res["w_up"] = tuple(a.T[None] for a in adam_up)
    res["w_in"] = tuple(a.T[None] for a in adam_in)
    res["ffn_conv_w"] = tuple(taps(a) for a in adam_fcw)
    res["conv_w"] = tuple(taps(a) for a in adam_cw)
    res.update(small_out)
    res["rel_bias_table"] = tuple(a.T for a in small_out["rel_bias_table"])
    loss = loss_row[0, 0]
    order = ("norm_mix_g", "w_in", "conv_w", "q_norm_g", "k_norm_g", "rel_bias_table", "sinks", "out_norm_conv_g",
             "out_norm_attn_g", "w_out", "norm_ffn_g", "w_up", "ffn_conv_w", "ffn_conv_b", "w_down")
    return (loss, dx[None], *[res[k][0] for k in order], *[res[k][1] for k in order],
            *[res[k][2] for k in order], *[res[k][3] for k in order])
```

```python
import functools
import math

import numpy as np
import jax
import jax.numpy as jnp
from jax import lax
from jax.experimental import pallas as pl
from jax.experimental.pallas import tpu as pltpu

F32 = jnp.float32
BF16 = jnp.bfloat16

SEQ = 2048
D_MODEL = 1024
CONV_WIDTH = 512
ATTN_WIDTH = 512
KV_WIDTH = 128
HEAD_DIM = 64
N_HEADS = 8
GQA_GROUP = 4
IN_WIDTH = 2304
D_FF = 2816
BLK = 128
NUM_BUCKETS = 32
EPS = 1e-6
NEG_INF = -1e30
ADAM_LR = 0.001
ADAM_B1 = 0.9
ADAM_B2 = 0.999
ADAM_EPS = 1e-08
ADAM_WD = 0.01
ADAM_STEP = 10

N_DEV = 8
FFN_BLK = 2 * D_FF // N_DEV
N_FFN_BLK = D_FF // FFN_BLK
SUBLANES = 8
VMEM_LIMIT = 56 * 1024 * 1024

_MESH = pl.DeviceIdType.MESH
_ANY = pl.BlockSpec(memory_space=pl.ANY)


def _params(sem):
    return pltpu.CompilerParams(dimension_semantics=sem, vmem_limit_bytes=VMEM_LIMIT)


def _ordered_behind(body, pos, after):
    if after is None:
        return body, [], []
    return (lambda *refs: body(*refs[:pos], *refs[pos + 1:])), [_ANY], [after]


def _dot(a, b):
    return jnp.dot(a, b, preferred_element_type=F32)


def _dot_nt(a, b):
    return lax.dot_general(a, b, (((1,), (1,)), ((), ())), preferred_element_type=F32)


def _dot_tn(a, b):
    return lax.dot_general(a, b, (((0,), (0,)), ((), ())), preferred_element_type=F32)


def _shift_down(x, s, halo):
    r = pltpu.roll(x, s, axis=0)
    hr = pltpu.roll(halo, s, axis=0)
    row = lax.broadcasted_iota(jnp.int32, halo.shape, 0)
    top = jnp.where(row < s, hr, r[:SUBLANES])
    return jnp.concatenate([top, r[SUBLANES:]], axis=0)


def _shift_up(x, s, halo):
    n = x.shape[0]
    r = pltpu.roll(x, n - s, axis=0)
    hr = pltpu.roll(halo, SUBLANES - s, axis=0)
    row = lax.broadcasted_iota(jnp.int32, halo.shape, 0)
    bot = jnp.where(row >= SUBLANES - s, hr, r[n - SUBLANES:])
    return jnp.concatenate([r[:n - SUBLANES], bot], axis=0)


def _taps(w):
    return (w[0], w[1], w[2]) if len(w.shape) == 3 else (w[0:1], w[1:2], w[2:3])


def _conv3(x, w, halo):
    x2 = _shift_down(x, 2, halo)
    x1 = _shift_down(x, 1, halo)
    return x2 * w[0] + x1 * w[1] + x * w[2], x2, x1


def _conv3_bwd_input(dy, w, halo_next):
    return dy * w[2] + _shift_up(dy, 1, halo_next) * w[1] + _shift_up(dy, 2, halo_next) * w[0]


def _rstd(x):
    return lax.rsqrt(jnp.mean(x * x, axis=-1, keepdims=True) + EPS)


def _rms_bwd(x, g, dy):
    r = _rstd(x)
    n = x * r
    dn = dy * g
    dx = r * (dn - n * jnp.mean(dn * n, axis=-1, keepdims=True))
    return dx, jnp.sum(dy * n, axis=0, keepdims=True)


def _head_mean(x):
    width = x.shape[-1]
    ri = lax.broadcasted_iota(jnp.int32, (width, width), 0) // HEAD_DIM
    ci = lax.broadcasted_iota(jnp.int32, (width, width), 1) // HEAD_DIM
    ones = jnp.where(ri == ci, 1.0, 0.0).astype(BF16)
    hi = x.astype(BF16)
    lo = (x - hi.astype(F32)).astype(BF16)
    return (_dot(hi, ones) + _dot(lo, ones)) * (1.0 / HEAD_DIM)


def _head_norm(x, g, heads):
    return x * lax.rsqrt(_head_mean(x * x) + EPS) * jnp.tile(g, (1, heads))


def _head_norm_bwd(x, g, dy, heads):
    r = lax.rsqrt(_head_mean(x * x) + EPS)
    n = x * r
    dn = dy * jnp.tile(g, (1, heads))
    dx = r * (dn - n * _head_mean(dn * n))
    per_lane = jnp.sum(dy * n, axis=0, keepdims=True)
    dg = per_lane[:, 0:HEAD_DIM]
    for h in range(1, heads):
        dg = dg + per_lane[:, HEAD_DIM * h:HEAD_DIM * (h + 1)]
    return dx, dg


def _bucket_map():
    q = np.arange(BLK)[:, None]
    j = np.arange(BLK)[None, :]
    n = np.where(j > q, q + BLK - j, q - j)
    nf = np.maximum(n, 1).astype(np.float32)
    max_exact = NUM_BUCKETS // 2
    large = max_exact + (np.log(nf / max_exact) / math.log(BLK / max_exact) * (NUM_BUCKETS - max_exact)).astype(np.int32)
    large = np.minimum(large, NUM_BUCKETS - 1)
    return np.where(n < max_exact, n, large).astype(np.int32)


def _coords():
    return lax.axis_index("x"), lax.axis_index("y"), lax.axis_index("c")


def _lin(px, py, pc):
    return 4 * px + 2 * py + pc


_HBM = pl.BlockSpec(memory_space=pltpu.HBM)
_SEM = pl.BlockSpec(memory_space=pltpu.SEMAPHORE)
_EFFECT = pltpu.SideEffectType.DATAFLOW_SIDE_EFFECTING


def _in_hbm(a):
    return pltpu.with_memory_space_constraint(a, pltpu.HBM)


def _split_start(name, srcs, lands, plan, after):
    ns, nl = len(srcs), len(lands)
    n_copies = len(plan(0, 0, 0))
    n_after = 0 if after is None else 1

    def body(*refs):
        src_refs, land_refs = refs[:ns + nl], refs[ns:ns + nl]
        send_sems, recv_sems = refs[ns + nl + n_after], refs[ns + nl + n_after + 1]
        token = refs[-1]
        for k, (a, s_slot, l, d_slot, dev) in enumerate(plan(*_coords())):
            src = src_refs[a] if s_slot is None else src_refs[a].at[s_slot]
            pltpu.make_async_remote_copy(src_ref=src, dst_ref=land_refs[l].at[d_slot], send_sem=send_sems.at[k],
                                         recv_sem=recv_sems.at[k], device_id=dev, device_id_type=_MESH).start()
        token[...] = jnp.zeros_like(token)

    arrs = list(srcs) + list(lands)
    out = pl.pallas_call(
        body, name=name,
        out_shape=(pltpu.SemaphoreType.DMA((n_copies,)), pltpu.SemaphoreType.DMA((n_copies,)),
                   *[pltpu.HBM(a.shape, a.dtype) for a in arrs], jax.ShapeDtypeStruct((SUBLANES, 128), F32)),
        in_specs=[_HBM] * (ns + nl) + [_ANY] * n_after,
        out_specs=(_SEM, _SEM, *[_HBM] * (ns + nl), pl.BlockSpec(memory_space=pltpu.VMEM)),
        input_output_aliases={i: 2 + i for i in range(ns + nl)},
        compiler_params=pltpu.CompilerParams(has_side_effects=_EFFECT),
    )(*[_in_hbm(a) for a in arrs], *([] if after is None else [after]))
    return out[0], out[1], list(out[2:2 + ns]), list(out[2 + ns:2 + ns + nl]), out[-1]


def _split_wait(name, send_sems, recv_sems, srcs, lands, plan, recv_slots, after):
    ns, nl = len(srcs), len(lands)

    def body(*refs):
        src_refs, land_refs = refs[:ns + nl], refs[ns:ns + nl]
        send_sems, recv_sems = refs[ns + nl], refs[ns + nl + 1]
        coords = _coords()
        slots = recv_slots(*coords)
        for k, (a, s_slot, l, _, dev) in enumerate(plan(*coords)):
            src = src_refs[a] if s_slot is None else src_refs[a].at[s_slot]
            cp = pltpu.make_async_remote_copy(src_ref=src, dst_ref=land_refs[l].at[slots[k]], send_sem=send_sems.at[k],
                                              recv_sem=recv_sems.at[k], device_id=dev, device_id_type=_MESH)
            cp.wait_send()
            cp.wait_recv()

    arrs = list(srcs) + list(lands)
    out = pl.pallas_call(
        body, name=name,
        out_shape=tuple(pltpu.HBM(a.shape, a.dtype) for a in arrs),
        in_specs=[_HBM] * (ns + nl) + [_SEM, _SEM, _ANY],
        out_specs=tuple([_HBM] * (ns + nl)),
        input_output_aliases={i: i for i in range(ns + nl)},
        compiler_params=pltpu.CompilerParams(has_side_effects=_EFFECT),
    )(*arrs, send_sems, recv_sems, after)
    return list(out[:ns]), list(out[ns:])


def _chips(x, y):
    return [(1 - x, y), (x, 1 - y), (1 - x, 1 - y)]


def _gather_plan_ici(n):
    def plan(x, y, c):
        me = _lin(x, y, c)
        out = []
        for a in range(n):
            out.append((a, me, a, me, (x, y, 1 - c)))
            out += [(a, me, a, me, (cx, cy, c)) for cx, cy in _chips(x, y)]
        return out

    def recv_slots(x, y, c):
        out = []
        for _ in range(n):
            out.append(_lin(x, y, 1 - c))
            out += [_lin(cx, cy, c) for cx, cy in _chips(x, y)]
        return out

    return plan, recv_slots


def _gather_plan_d2d(n):
    def plan(x, y, c):
        return [(a, _lin(cx, cy, c), a, _lin(cx, cy, c), (x, y, 1 - c)) for a in range(n) for cx, cy in _chips(x, y)]

    def recv_slots(x, y, c):
        return [_lin(cx, cy, 1 - c) for _ in range(n) for cx, cy in _chips(x, y)]

    return plan, recv_slots


def _all_gather_split(lands, tag, after):
    n = len(lands)
    plan1, slots1 = _gather_plan_ici(n)
    s1, r1, _, lands, token = _split_start(f"gather_{tag}_ici_start", [], lands, plan1, after)

    def finish(after):
        _, got = _split_wait(f"gather_{tag}_ici_wait", s1, r1, [], lands, plan1, slots1, after)
        plan2, slots2 = _gather_plan_d2d(n)
        s2, r2, _, got, token2 = _split_start(f"gather_{tag}_d2d_start", [], got, plan2, None)
        return _split_wait(f"gather_{tag}_d2d_wait", s2, r2, [], got, plan2, slots2, token2)[1]

    return finish, token


def _all_gather_tree(lands, tag, after):
    n = len(lands)

    def plan1(x, y, c):
        me = _lin(x, y, c)
        return [(a, me, a, me, dev) for a in range(n) for dev in ((x, y, 1 - c), (1 - x, y, c), (x, 1 - y, c))]

    def slots1(x, y, c):
        return [s for _ in range(n) for s in (_lin(x, y, 1 - c), _lin(1 - x, y, c), _lin(x, 1 - y, c))]

    def plan2(x, y, c):
        from_x, from_y = _lin(1 - x, y, c), _lin(x, 1 - y, c)
        north = c == 1
        passed = jnp.where(north, from_x, from_y)
        onward = (jnp.where(north, x, 1 - x), jnp.where(north, 1 - y, y), c)
        sib = (x, y, 1 - c)
        return [cp for a in range(n) for cp in ((a, passed, a, passed, onward), (a, from_x, a, from_x, sib),
                                                (a, from_y, a, from_y, sib))]

    def slots2(x, y, c):
        return [s for _ in range(n) for s in (_lin(1 - x, 1 - y, c), _lin(1 - x, y, 1 - c), _lin(x, 1 - y, 1 - c))]

    def plan3(x, y, c):
        diag = _lin(1 - x, 1 - y, c)
        return [(a, diag, a, diag, (x, y, 1 - c)) for a in range(n)]

    def slots3(x, y, c):
        return [_lin(1 - x, 1 - y, 1 - c)] * n

    s1, r1, _, lands, token = _split_start(f"gather_{tag}_1_start", [], lands, plan1, after)
    state = {}

    def stage2(after):
        _, got = _split_wait(f"gather_{tag}_1_wait", s1, r1, [], lands, plan1, slots1, after)
        state["s"], state["r"], _, state["lands"], token2 = _split_start(f"gather_{tag}_2_start", [], got, plan2, None)
        return token2

    def stage3(after):
        _, got = _split_wait(f"gather_{tag}_2_wait", state["s"], state["r"], [], state["lands"], plan2, slots2, after)
        s3, r3, _, got, token3 = _split_start(f"gather_{tag}_3_start", [], got, plan3, None)
        return _split_wait(f"gather_{tag}_3_wait", s3, r3, [], got, plan3, slots3, token3)[1]

    return stage2, stage3, token


_CHIP_LIST = ((0, 0), (0, 1), (1, 0), (1, 1))


def _reduce_plan_d2d(n):
    def plan(x, y, c):
        return [(a, _lin(qx, qy, 1 - c), a, q, (x, y, 1 - c)) for a in range(n) for q, (qx, qy) in enumerate(_CHIP_LIST)]

    def recv_slots(x, y, c):
        return [q for _ in range(n) for q in range(4)]

    return plan, recv_slots


def _reduce_plan_ici(n):
    def plan(x, y, c):
        return [(a, 2 * cx + cy, a, j, (cx, cy, c)) for a in range(n) for j, (cx, cy) in enumerate(_chips(x, y))]

    def recv_slots(x, y, c):
        return [j for _ in range(n) for j in range(3)]

    return plan, recv_slots


def _broadcast_plan():
    def peers(x, y, c):
        return [(1 - x if r & 4 else x, 1 - y if r & 2 else y, 1 - c if r & 1 else c) for r in range(1, N_DEV)]

    def plan(x, y, c):
        return [(0, None, 0, _lin(x, y, c), peer) for peer in peers(x, y, c)]

    def recv_slots(x, y, c):
        return [_lin(*peer) for peer in peers(x, y, c)]

    return plan, recv_slots


def _chip_partial(grads, recvd, core, name):
    n = len(grads)

    def body(c_ref, *refs):
        for a in range(n):
            g_ref, r_ref, o_ref = refs[a], refs[n + a], refs[2 * n + a]
            o_ref[...] = (g_ref[...].astype(F32) + r_ref[...].astype(F32)).astype(o_ref.dtype)

    def blk(a, own):
        zeros = (0,) * (a.ndim - 1)
        return pl.BlockSpec((None,) + a.shape[1:],
                            (lambda q, c_ref: (2 * q + c_ref[0],) + zeros) if own else (lambda q, c_ref: (q,) + zeros))

    return pl.pallas_call(
        body, name=name,
        grid_spec=pltpu.PrefetchScalarGridSpec(
            num_scalar_prefetch=1, grid=(4,),
            in_specs=[blk(a, True) for a in grads] + [blk(a, False) for a in recvd],
            out_specs=[blk(a, False) for a in recvd]),
        out_shape=[jax.ShapeDtypeStruct(a.shape, a.dtype) for a in recvd],
        compiler_params=_params(("arbitrary",)),
    )(core, *grads, *recvd)


def _reduce_scatter_split(grads, tag, core, behind):
    n = len(grads)
    plan1, slots1 = _reduce_plan_d2d(n)
    lands1 = [lax.empty((4,) + a.shape[1:], a.dtype) for a in grads]
    s1, r1, srcs1, lands1, token1 = _split_start(f"reduce_{tag}_d2d_start", grads, lands1, plan1, None)
    own, got = _split_wait(f"reduce_{tag}_d2d_wait", s1, r1, srcs1, lands1, plan1, slots1, behind(token1))
    parts = _chip_partial(own, got, core, f"reduce_{tag}_partial")
    plan2, slots2 = _reduce_plan_ici(n)
    lands2 = [lax.empty((3,) + a.shape[1:], a.dtype) for a in grads]
    s2, r2, srcs2, lands2, token2 = _split_start(f"reduce_{tag}_ici_start", parts, lands2, plan2, None)

    def finish(after):
        return _split_wait(f"reduce_{tag}_ici_wait", s2, r2, srcs2, lands2, plan2, slots2, after)

    return finish, token2


def _place_shards(me, shards, dtypes):
    n = len(shards)

    def body(me_ref, *refs):
        for a in range(n):
            refs[n + a][...] = refs[a][...].astype(dtypes[a])

    full = lambda s: pl.BlockSpec(s.shape, lambda i, me_ref: (0,) * s.ndim)
    slot = lambda s: pl.BlockSpec((None,) + s.shape, lambda i, me_ref: (me_ref[0],) + (0,) * s.ndim)
    return pl.pallas_call(
        body, name="place_shards",
        grid_spec=pltpu.PrefetchScalarGridSpec(num_scalar_prefetch=1, grid=(1,), in_specs=[full(s) for s in shards],
                                               out_specs=[slot(s) for s in shards]),
        out_shape=[jax.ShapeDtypeStruct((N_DEV,) + s.shape, d) for s, d in zip(shards, dtypes)],
        compiler_params=_params(("arbitrary",)),
    )(me, *shards)


def _mix_in_fwd(x, g1, w_in_t, conv_w, gq, gk, gconv):
    tm = 512
    n_t = SEQ // tm

    def body(x_ref, g1_ref, w_ref, cw_ref, gq_ref, gk_ref, gc_ref,
             proj_ref, u1_ref, ycn_ref, qn_ref, kn_ref, v_ref, halo_ref):
        @pl.when(pl.program_id(0) == 0)
        def _():
            halo_ref[...] = jnp.zeros_like(halo_ref)

        xv = x_ref[...]
        u = (xv * _rstd(xv) * g1_ref[...]).astype(BF16)
        u1_ref[...] = u
        proj = _dot_nt(u, w_ref[...])
        proj_ref[...] = proj
        gate_b = proj[:, 0:CONV_WIDTH]
        a = proj[:, CONV_WIDTH:2 * CONV_WIDTH] * proj[:, 2 * CONV_WIDTH:3 * CONV_WIDTH]
        cv, _, _ = _conv3(a, _taps(cw_ref[...]), halo_ref[...])
        halo_ref[...] = a[tm - SUBLANES:]
        yc = gate_b * cv
        ycn_ref[...] = (yc * _rstd(yc) * gc_ref[...]).astype(BF16)
        q0 = 3 * CONV_WIDTH
        qn_ref[...] = _head_norm(proj[:, q0:q0 + ATTN_WIDTH], gq_ref[...], N_HEADS).astype(BF16)
        k0 = q0 + ATTN_WIDTH
        kn_ref[...] = _head_norm(proj[:, k0:k0 + KV_WIDTH], gk_ref[...], 2).astype(BF16)
        v_ref[...] = proj[:, k0 + KV_WIDTH:k0 + 2 * KV_WIDTH].astype(BF16)

    const = lambda shape: pl.BlockSpec(shape, lambda i: (0,) * len(shape))
    rows = lambda w: pl.BlockSpec((tm, w), lambda i: (i, 0))
    return pl.pallas_call(
        body, name="mix_in_fwd", grid=(n_t,),
        in_specs=[rows(D_MODEL), const((1, D_MODEL)), const((IN_WIDTH, D_MODEL)), const((3, CONV_WIDTH)),
                  const((1, HEAD_DIM)), const((1, HEAD_DIM)), const((1, CONV_WIDTH))],
        out_specs=[rows(IN_WIDTH), rows(D_MODEL), rows(CONV_WIDTH), rows(ATTN_WIDTH), rows(KV_WIDTH), rows(KV_WIDTH)],
        out_shape=[jax.ShapeDtypeStruct((SEQ, IN_WIDTH), F32), jax.ShapeDtypeStruct((SEQ, D_MODEL), BF16),
                   jax.ShapeDtypeStruct((SEQ, CONV_WIDTH), BF16),
                   jax.ShapeDtypeStruct((SEQ, ATTN_WIDTH), BF16), jax.ShapeDtypeStruct((SEQ, KV_WIDTH), BF16),
                   jax.ShapeDtypeStruct((SEQ, KV_WIDTH), BF16)],
        scratch_shapes=[pltpu.VMEM((SUBLANES, CONV_WIDTH), F32)],
        compiler_params=_params(("arbitrary",)),
    )(x, g1, w_in_t, conv_w, gq, gk, gconv)


GROUP_ROWS = GQA_GROUP * BLK


def _band_bias(tbl_ref, bkt, bias_ref):
    for h in range(N_HEADS):
        acc = jnp.zeros(bkt.shape, F32)
        for b in range(NUM_BUCKETS):
            acc = jnp.where(bkt == b, tbl_ref[h, b], acc)
        bias_ref[h // GQA_GROUP, BLK * (h % GQA_GROUP):BLK * (h % GQA_GROUP + 1), :] = acc


def _band_masks(i):
    qi = lax.broadcasted_iota(jnp.int32, (GROUP_ROWS, BLK), 0) & (BLK - 1)
    ji = lax.broadcasted_iota(jnp.int32, (GROUP_ROWS, BLK), 1)
    upper = ji > qi
    return upper, upper & (i == 0)


def _stack_heads(x, g):
    return jnp.concatenate([x[:, HEAD_DIM * h:HEAD_DIM * (h + 1)] for h in range(GQA_GROUP * g, GQA_GROUP * (g + 1))], axis=0)


def _unstack_heads(groups):
    return jnp.concatenate([p[BLK * t:BLK * (t + 1)] for p in groups for t in range(GQA_GROUP)], axis=-1)


def _per_head_rows(vals):
    row = lax.broadcasted_iota(jnp.int32, (GROUP_ROWS, 1), 0)
    col = jnp.full((GROUP_ROWS, 1), vals[GQA_GROUP - 1], F32)
    for t in range(GQA_GROUP - 2, -1, -1):
        col = jnp.where(row < BLK * (t + 1), vals[t], col)
    return col


def _band_rows(ref, i):
    prev = pl.multiple_of(jnp.maximum(i - 1, 0) * BLK, BLK)
    cur = pl.multiple_of(i * BLK, BLK)
    return jnp.concatenate([ref[pl.ds(prev, BLK), :], ref[pl.ds(cur, BLK), :]], axis=0), prev, cur


def _fold(band, upper):
    return jnp.where(upper, band[:, :BLK], band[:, BLK:])


def _unfold(tile, upper):
    return jnp.concatenate([jnp.where(upper, tile, 0.0), jnp.where(upper, 0.0, tile)], axis=1)


def _head_probs(qh, kh, bias, upper, dead, sink):
    logits = _fold(_dot_nt(qh, kh), upper) * (HEAD_DIM ** -0.5) + bias
    logits = jnp.where(dead, NEG_INF, logits)
    m = jnp.maximum(jnp.max(logits, axis=-1, keepdims=True), sink)
    p = jnp.exp(logits - m)
    es = jnp.exp(sink - m)
    den = jnp.sum(p, axis=-1, keepdims=True) + es
    return p / den, es / den


def _attn_fwd(qn, kn, v, tbl, sinks, bkt, gattn, after=None):
    n_b = SEQ // BLK

    def body(q_ref, k_ref, v_ref, tbl_ref, sink_ref, bkt_ref, ga_ref, y_ref, yn_ref, p_ref, ps_ref, bias_ref):
        i = pl.program_id(0)

        @pl.when(i == 0)
        def _():
            _band_bias(tbl_ref, bkt_ref[...], bias_ref)

        kb, _, _ = _band_rows(k_ref, i)
        vb, _, _ = _band_rows(v_ref, i)
        upper, dead = _band_masks(i)
        q = q_ref[...]
        lane = lax.broadcasted_iota(jnp.int32, (BLK, 128), 1)
        outs = []
        psinks = jnp.zeros((BLK, 128), F32)
        for g in range(N_HEADS // GQA_GROUP):
            kv = slice(HEAD_DIM * g, HEAD_DIM * (g + 1))
            sink = _per_head_rows([sink_ref[0, GQA_GROUP * g + t] for t in range(GQA_GROUP)])
            probs, psink = _head_probs(_stack_heads(q, g), kb[:, kv], bias_ref[g], upper, dead, sink)
            p_ref[g] = probs.astype(BF16)
            for t in range(GQA_GROUP):
                psinks = jnp.where(lane == GQA_GROUP * g + t, psink[BLK * t:BLK * (t + 1)], psinks)
            outs.append(_dot(_unfold(probs, upper).astype(BF16), vb[:, kv]))
        ps_ref[...] = psinks
        y = _unstack_heads(outs)
        y_ref[...] = y
        yn_ref[...] = (y * _rstd(y) * ga_ref[...]).astype(BF16)

    const = lambda shape: pl.BlockSpec(shape, lambda i: (0,) * len(shape))
    rows = lambda w: pl.BlockSpec((BLK, w), lambda i: (i, 0))
    smem = pl.BlockSpec(memory_space=pltpu.SMEM)
    body, more_specs, more = _ordered_behind(body, 7, after)
    return pl.pallas_call(
        body, name="attn_fwd", grid=(n_b,),
        in_specs=[rows(ATTN_WIDTH), const((SEQ, KV_WIDTH)), const((SEQ, KV_WIDTH)), smem, smem,
                  const((BLK, BLK)), const((1, ATTN_WIDTH))] + more_specs,
        out_specs=[rows(ATTN_WIDTH), rows(ATTN_WIDTH),
                   pl.BlockSpec((None, N_HEADS // GQA_GROUP, GROUP_ROWS, BLK), lambda i: (i, 0, 0, 0)), rows(128)],
        out_shape=[jax.ShapeDtypeStruct((SEQ, ATTN_WIDTH), F32), jax.ShapeDtypeStruct((SEQ, ATTN_WIDTH), BF16),
                   jax.ShapeDtypeStruct((n_b, N_HEADS // GQA_GROUP, GROUP_ROWS, BLK), BF16),
                   jax.ShapeDtypeStruct((SEQ, 128), F32)],
        scratch_shapes=[pltpu.VMEM((N_HEADS // GQA_GROUP, GROUP_ROWS, BLK), F32)],
        compiler_params=_params(("arbitrary",)),
    )(qn, kn, v, tbl, sinks, bkt, gattn, *more)


def _out_proj(x, ycn, yan, w_out, g2):
    tm = 512

    def body(x_ref, ycn_ref, yan_ref, wo_ref, g2_ref, h1_ref, u2_ref):
        h1 = x_ref[...] + _dot(ycn_ref[...], wo_ref[0:CONV_WIDTH, :]) + _dot(yan_ref[...], wo_ref[CONV_WIDTH:, :])
        h1_ref[...] = h1
        u2_ref[...] = (h1 * _rstd(h1) * g2_ref[...]).astype(BF16)

    rows = lambda w: pl.BlockSpec((tm, w), lambda i: (i, 0))
    const = lambda shape: pl.BlockSpec(shape, lambda i: (0,) * len(shape))
    return pl.pallas_call(
        body, name="out_proj", grid=(SEQ // tm,),
        in_specs=[rows(D_MODEL), rows(CONV_WIDTH), rows(ATTN_WIDTH), const((D_MODEL, D_MODEL)), const((1, D_MODEL))],
        out_specs=[rows(D_MODEL), rows(D_MODEL)],
        out_shape=[jax.ShapeDtypeStruct((SEQ, D_MODEL), F32), jax.ShapeDtypeStruct((SEQ, D_MODEL), BF16)],
        compiler_params=_params(("arbitrary",)),
    )(x, ycn, yan, w_out, g2)


def _ffn_fwd(h1, u2, w_up, fcw, fcb, w_down, tgt):
    tm = 512
    n_t = SEQ // tm
    last = N_FFN_BLK - 1

    def body(u2_ref, wu_ref, cw_ref, b_ref, wd_ref, h1_ref, tgt_ref,
             up_ref, pre_ref, act_ref, dh2_ref, dh2b_ref, loss_ref, acc_ref, halo_ref):
        j, i = pl.program_id(0), pl.program_id(1)
        rows = pl.ds(pl.multiple_of(i * tm, tm), tm)

        @pl.when((i == 0) & (j == 0))
        def _():
            loss_ref[...] = jnp.zeros_like(loss_ref)

        u2 = u2_ref[...]
        pre = []
        for s in range(2):
            up = _dot_nt(u2, wu_ref[s])
            up_ref[s] = up.astype(BF16)
            halo = jnp.where(i == 0, 0.0, halo_ref[s])
            pre.append(_conv3(up, _taps(cw_ref.at[s]), halo)[0] + b_ref[s])
            pre_ref[s] = pre[s].astype(BF16)
            halo_ref[s] = up[tm - SUBLANES:]
        g, val = pre
        act = (g * jax.nn.sigmoid(g) * val).astype(BF16)
        act_ref[...] = act
        out = _dot(act, wd_ref[...])

        @pl.when(j == 0)
        def _():
            acc_ref[rows, :] = out

        @pl.when(j > 0)
        def _():
            acc_ref[rows, :] += out

        @pl.when(j == last)
        def _():
            err = h1_ref[...] + acc_ref[rows, :] - tgt_ref[...]
            loss_ref[...] += 0.5 * jnp.sum(err * err) / D_MODEL
            dh2 = err / D_MODEL
            dh2_ref[...] = dh2
            dh2b_ref[...] = dh2.astype(BF16)

    late = lambda w: pl.BlockSpec((tm, w), lambda j, i: (jnp.where(j == last, i, 0), 0))
    pair = lambda *s: pl.BlockSpec((2, None) + s, lambda j, i: (0, j) + (0,) * len(s))
    upb = pl.BlockSpec((2, None, tm, FFN_BLK), lambda j, i: (0, j, i, 0))
    return pl.pallas_call(
        body, name="ffn_fwd", grid=(N_FFN_BLK, n_t),
        in_specs=[pl.BlockSpec((tm, D_MODEL), lambda j, i: (i, 0)), pair(FFN_BLK, D_MODEL), pair(3, 1, FFN_BLK),
                  pair(1, FFN_BLK), pl.BlockSpec((None, FFN_BLK, D_MODEL), lambda j, i: (j, 0, 0)),
                  late(D_MODEL), late(D_MODEL)],
        out_specs=[upb, upb, pl.BlockSpec((None, tm, FFN_BLK), lambda j, i: (j, i, 0)), late(D_MODEL), late(D_MODEL),
                   pl.BlockSpec((SUBLANES, 128), lambda j, i: (0, 0))],
        out_shape=[jax.ShapeDtypeStruct((2, N_FFN_BLK, SEQ, FFN_BLK), BF16),
                   jax.ShapeDtypeStruct((2, N_FFN_BLK, SEQ, FFN_BLK), BF16),
                   jax.ShapeDtypeStruct((N_FFN_BLK, SEQ, FFN_BLK), BF16),
                   jax.ShapeDtypeStruct((SEQ, D_MODEL), F32), jax.ShapeDtypeStruct((SEQ, D_MODEL), BF16),
                   jax.ShapeDtypeStruct((SUBLANES, 128), F32)],
        scratch_shapes=[pltpu.VMEM((SEQ, D_MODEL), F32), pltpu.VMEM((2, SUBLANES, FFN_BLK), F32)],
        compiler_params=_params(("arbitrary", "arbitrary")),
    )(u2, w_up, fcw, fcb, w_down, h1, tgt)


def _ffn_fwd_token_major(x, ycn, yan, w_out, g2, w_up, fcw, fcb, w_down, tgt):
    tm = 512
    n_t = SEQ // tm

    def body(x_ref, ycn_ref, yan_ref, wo_ref, g2_ref, wu_ref, cw_ref, b_ref, wd_ref, tgt_ref,
             h1_ref, u2_ref, up_ref, pre_ref, act_ref, dh2_ref, dh2b_ref, loss_ref, acc_ref, halo_ref):
        i, j = pl.program_id(0), pl.program_id(1)

        @pl.when((i == 0) & (j == 0))
        def _():
            loss_ref[...] = jnp.zeros_like(loss_ref)

        @pl.when(j == 0)
        def _():
            h1 = x_ref[...] + _dot(ycn_ref[...], wo_ref[0:CONV_WIDTH, :]) + _dot(yan_ref[...], wo_ref[CONV_WIDTH:, :])
            h1_ref[...] = h1
            u2_ref[...] = (h1 * _rstd(h1) * g2_ref[...]).astype(BF16)
            acc_ref[...] = jnp.zeros_like(acc_ref)

        u2 = u2_ref[...]
        pre = []
        for s in range(2):
            up = _dot_nt(u2, wu_ref[s])
            up_ref[s] = up.astype(BF16)
            halo = jnp.where(i == 0, 0.0, halo_ref[s, j])
            pre.append(_conv3(up, _taps(cw_ref.at[s]), halo)[0] + b_ref[s])
            pre_ref[s] = pre[s].astype(BF16)
            halo_ref[s, j] = up[tm - SUBLANES:]
        g, val = pre
        act = (g * jax.nn.sigmoid(g) * val).astype(BF16)
        act_ref[...] = act
        acc_ref[...] += _dot(act, wd_ref[...])

        @pl.when(j == N_FFN_BLK - 1)
        def _():
            err = h1_ref[...] + acc_ref[...] - tgt_ref[...]
            loss_ref[...] += 0.5 * jnp.sum(err * err) / D_MODEL
            dh2 = err / D_MODEL
            dh2_ref[...] = dh2
            dh2b_ref[...] = dh2.astype(BF16)

    rows = lambda w: pl.BlockSpec((tm, w), lambda i, j: (i, 0))
    const = lambda shape: pl.BlockSpec(shape, lambda i, j: (0,) * len(shape))
    pair = lambda *s: pl.BlockSpec((2, None) + s, lambda i, j: (0, j) + (0,) * len(s))
    upb = pl.BlockSpec((2, None, tm, FFN_BLK), lambda i, j: (0, j, i, 0))
    return pl.pallas_call(
        body, name="ffn_fwd", grid=(n_t, N_FFN_BLK),
        in_specs=[rows(D_MODEL), rows(CONV_WIDTH), rows(ATTN_WIDTH), const((D_MODEL, D_MODEL)), const((1, D_MODEL)),
                  pair(FFN_BLK, D_MODEL), pair(3, 1, FFN_BLK), pair(1, FFN_BLK),
                  pl.BlockSpec((None, FFN_BLK, D_MODEL), lambda i, j: (j, 0, 0)), rows(D_MODEL)],
        out_specs=[rows(D_MODEL), rows(D_MODEL), upb, upb, pl.BlockSpec((None, tm, FFN_BLK), lambda i, j: (j, i, 0)),
                   rows(D_MODEL), rows(D_MODEL), const((SUBLANES, 128))],
        out_shape=[jax.ShapeDtypeStruct((SEQ, D_MODEL), F32), jax.ShapeDtypeStruct((SEQ, D_MODEL), BF16),
                   jax.ShapeDtypeStruct((2, N_FFN_BLK, SEQ, FFN_BLK), BF16),
                   jax.ShapeDtypeStruct((2, N_FFN_BLK, SEQ, FFN_BLK), BF16),
                   jax.ShapeDtypeStruct((N_FFN_BLK, SEQ, FFN_BLK), BF16),
                   jax.ShapeDtypeStruct((SEQ, D_MODEL), F32), jax.ShapeDtypeStruct((SEQ, D_MODEL), BF16),
                   jax.ShapeDtypeStruct((SUBLANES, 128), F32)],
        scratch_shapes=[pltpu.VMEM((tm, D_MODEL), F32), pltpu.VMEM((2, N_FFN_BLK, SUBLANES, FFN_BLK), F32)],
        compiler_params=_params(("arbitrary", "arbitrary")),
    )(x, ycn, yan, w_out, g2, w_up, fcw, fcb, w_down, tgt)


def _ffn_bwd(dh2, dh2b, h1, g2, up, pre, w_up, fcw, w_down):
    tm = 256
    n_t = SEQ // tm
    last = N_FFN_BLK - 1

    def body(dh2b_ref, up_ref, pre_ref, wu_ref, cw_ref, wd_ref, dh2_ref, h1_ref, g2_ref,
             dup_ref, dh1_ref, dh1b_ref, dfb_ref, dfcw_ref, dg2_ref, acc_ref, next_ref):
        j, i = pl.program_id(0), pl.program_id(1)
        rows = pl.ds(pl.multiple_of((n_t - 1 - i) * tm, tm), tm)

        @pl.when((i == 0) & (j == 0))
        def _():
            dfb_ref[...] = jnp.zeros_like(dfb_ref)
            dfcw_ref[...] = jnp.zeros_like(dfcw_ref)
            dg2_ref[...] = jnp.zeros_like(dg2_ref)

        g, val = pre_ref[0].astype(F32), pre_ref[1].astype(F32)
        sg = jax.nn.sigmoid(g)
        silu = g * sg
        dact = _dot_nt(dh2b_ref[...], wd_ref[...])
        dpre = (dact * val * (sg * (1.0 + g * (1.0 - sg))), dact * silu)
        du = None
        for s in range(2):
            d = dpre[s]
            u = up_ref[s].astype(F32)
            w = _taps(cw_ref.at[s])
            nxt = jnp.where(i == 0, 0.0, next_ref[s])
            d1 = _shift_up(d, 1, nxt)
            d2 = _shift_up(d, 2, nxt)
            next_ref[s] = d[:SUBLANES]
            dfb_ref[s, j] += jnp.sum(d, axis=0, keepdims=True)
            dfcw_ref[s, j, 0] += jnp.sum(d2 * u, axis=0, keepdims=True)
            dfcw_ref[s, j, 1] += jnp.sum(d1 * u, axis=0, keepdims=True)
            dfcw_ref[s, j, 2] += jnp.sum(d * u, axis=0, keepdims=True)
            dup = (d * w[2] + d1 * w[1] + d2 * w[0]).astype(BF16)
            dup_ref[s] = dup
            part = _dot(dup, wu_ref[s])
            du = part if du is None else du + part

        @pl.when(j == 0)
        def _():
            acc_ref[rows, :] = du

        @pl.when(j > 0)
        def _():
            acc_ref[rows, :] += du

        @pl.when(j == last)
        def _():
            dn, dgain = _rms_bwd(h1_ref[...], g2_ref[...], acc_ref[rows, :])
            dh1 = dh2_ref[...] + dn
            dh1_ref[...] = dh1
            dh1b_ref[...] = dh1.astype(BF16)
            dg2_ref[...] += dgain

    rev = lambda i: n_t - 1 - i
    const = lambda shape: pl.BlockSpec(shape, lambda j, i: (0,) * len(shape))
    late = lambda w: pl.BlockSpec((tm, w), lambda j, i: (jnp.where(j == last, rev(i), n_t - 1), 0))
    pair = lambda *s: pl.BlockSpec((2, None) + s, lambda j, i: (0, j) + (0,) * len(s))
    upb = pl.BlockSpec((2, None, tm, FFN_BLK), lambda j, i: (0, j, rev(i), 0))
    return pl.pallas_call(
        body, name="ffn_bwd", grid=(N_FFN_BLK, n_t),
        in_specs=[pl.BlockSpec((tm, D_MODEL), lambda j, i: (rev(i), 0)), upb, upb, pair(FFN_BLK, D_MODEL),
                  pair(3, 1, FFN_BLK), pl.BlockSpec((None, FFN_BLK, D_MODEL), lambda j, i: (j, 0, 0)),
                  late(D_MODEL), late(D_MODEL), const((1, D_MODEL))],
        out_specs=[upb, late(D_MODEL), late(D_MODEL),
                   const((2, N_FFN_BLK, 1, FFN_BLK)), const((2, N_FFN_BLK, 3, 1, FFN_BLK)), const((1, D_MODEL))],
        out_shape=[jax.ShapeDtypeStruct((2, N_FFN_BLK, SEQ, FFN_BLK), BF16), jax.ShapeDtypeStruct((SEQ, D_MODEL), F32),
                   jax.ShapeDtypeStruct((SEQ, D_MODEL), BF16), jax.ShapeDtypeStruct((2, N_FFN_BLK, 1, FFN_BLK), F32),
                   jax.ShapeDtypeStruct((2, N_FFN_BLK, 3, 1, FFN_BLK), F32), jax.ShapeDtypeStruct((1, D_MODEL), F32)],
        scratch_shapes=[pltpu.VMEM((SEQ, D_MODEL), F32), pltpu.VMEM((2, SUBLANES, FFN_BLK), F32)],
        compiler_params=_params(("arbitrary", "arbitrary")),
    )(dh2b, up, pre, w_up, fcw, w_down, dh2, h1, g2)


def _ffn_bwd_token_major(dh2, dh2b, h1, g2, up, pre, w_up, fcw, w_down):
    tm = 512
    n_t = SEQ // tm

    def body(dh2_ref, dh2b_ref, h1_ref, g2_ref, up_ref, pre_ref, wu_ref, cw_ref, wd_ref,
             dup_ref, dh1_ref, dh1b_ref, dfb_ref, dfcw_ref, dg2_ref, acc_ref, next_ref):
        i, j = pl.program_id(0), pl.program_id(1)

        @pl.when((i == 0) & (j == 0))
        def _():
            dfb_ref[...] = jnp.zeros_like(dfb_ref)
            dfcw_ref[...] = jnp.zeros_like(dfcw_ref)
            dg2_ref[...] = jnp.zeros_like(dg2_ref)

        @pl.when(j == 0)
        def _():
            acc_ref[...] = jnp.zeros_like(acc_ref)

        g, val = pre_ref[0].astype(F32), pre_ref[1].astype(F32)
        sg = jax.nn.sigmoid(g)
        silu = g * sg
        dact = _dot_nt(dh2b_ref[...], wd_ref[...])
        dpre = (dact * val * (sg * (1.0 + g * (1.0 - sg))), dact * silu)
        for s in range(2):
            d = dpre[s]
            u = up_ref[s].astype(F32)
            w = _taps(cw_ref.at[s])
            nxt = jnp.where(i == 0, 0.0, next_ref[s, j])
            d1 = _shift_up(d, 1, nxt)
            d2 = _shift_up(d, 2, nxt)
            next_ref[s, j] = d[:SUBLANES]
            dfb_ref[s, j] += jnp.sum(d, axis=0, keepdims=True)
            dfcw_ref[s, j, 0] += jnp.sum(d2 * u, axis=0, keepdims=True)
            dfcw_ref[s, j, 1] += jnp.sum(d1 * u, axis=0, keepdims=True)
            dfcw_ref[s, j, 2] += jnp.sum(d * u, axis=0, keepdims=True)
            dup = (d * w[2] + d1 * w[1] + d2 * w[0]).astype(BF16)
            dup_ref[s] = dup
            acc_ref[...] += _dot(dup, wu_ref[s])

        @pl.when(j == N_FFN_BLK - 1)
        def _():
            dn, dgain = _rms_bwd(h1_ref[...], g2_ref[...], acc_ref[...])
            dh1 = dh2_ref[...] + dn
            dh1_ref[...] = dh1
            dh1b_ref[...] = dh1.astype(BF16)
            dg2_ref[...] += dgain

    rev = lambda i: n_t - 1 - i
    rows = lambda w: pl.BlockSpec((tm, w), lambda i, j: (rev(i), 0))
    const = lambda shape: pl.BlockSpec(shape, lambda i, j: (0,) * len(shape))
    pair = lambda *s: pl.BlockSpec((2, None) + s, lambda i, j: (0, j) + (0,) * len(s))
    upb = pl.BlockSpec((2, None, tm, FFN_BLK), lambda i, j: (0, j, rev(i), 0))
    return pl.pallas_call(
        body, name="ffn_bwd", grid=(n_t, N_FFN_BLK),
        in_specs=[rows(D_MODEL), rows(D_MODEL), rows(D_MODEL), const((1, D_MODEL)), upb, upb,
                  pair(FFN_BLK, D_MODEL), pair(3, 1, FFN_BLK),
                  pl.BlockSpec((None, FFN_BLK, D_MODEL), lambda i, j: (j, 0, 0))],
        out_specs=[upb, rows(D_MODEL), rows(D_MODEL),
                   const((2, N_FFN_BLK, 1, FFN_BLK)), const((2, N_FFN_BLK, 3, 1, FFN_BLK)), const((1, D_MODEL))],
        out_shape=[jax.ShapeDtypeStruct((2, N_FFN_BLK, SEQ, FFN_BLK), BF16), jax.ShapeDtypeStruct((SEQ, D_MODEL), F32),
                   jax.ShapeDtypeStruct((SEQ, D_MODEL), BF16), jax.ShapeDtypeStruct((2, N_FFN_BLK, 1, FFN_BLK), F32),
                   jax.ShapeDtypeStruct((2, N_FFN_BLK, 3, 1, FFN_BLK), F32), jax.ShapeDtypeStruct((1, D_MODEL), F32)],
        scratch_shapes=[pltpu.VMEM((tm, D_MODEL), F32), pltpu.VMEM((2, N_FFN_BLK, SUBLANES, FFN_BLK), F32)],
        compiler_params=_params(("arbitrary", "arbitrary")),
    )(dh2, dh2b, h1, g2, up, pre, w_up, fcw, w_down)


def _grad_tn(a_list, b, out_rows, name, after=None):
    n = len(a_list)
    ncol = b.shape[1]

    def body(*refs):
        a_refs, b_ref, o_ref = refs[:n], refs[n], refs[n + 1]
        j = pl.program_id(0)
        for k in range(n):
            @pl.when(j == k)
            def _(k=k):
                o_ref[...] = _dot_tn(a_refs[k][...], b_ref[...]).astype(BF16)

    full = lambda shape: pl.BlockSpec(shape, lambda j: (0,) * len(shape))
    body, more_specs, more = _ordered_behind(body, n + 1, after)
    return pl.pallas_call(
        body, name=name, grid=(n,),
        in_specs=[full((SEQ, out_rows))] * n + [full((SEQ, ncol))] + more_specs,
        out_specs=pl.BlockSpec((None, out_rows, ncol), lambda j: (j, 0, 0)),
        out_shape=jax.ShapeDtypeStruct((n, out_rows, ncol), BF16),
        compiler_params=_params(("arbitrary",)),
    )(*a_list, b, *more)


def _grad_tn_blocked(a, b, name, a_is_blocked):
    nb = a.shape[0] if a_is_blocked else b.shape[0]
    a_w, b_w = a.shape[-1], b.shape[-1]

    def body(a_ref, b_ref, o_ref):
        o_ref[...] = _dot_tn(a_ref[...], b_ref[...]).astype(BF16)

    blocked = lambda w: pl.BlockSpec((None, SEQ, w), lambda k: (k, 0, 0))
    full = lambda w: pl.BlockSpec((SEQ, w), lambda k: (0, 0))
    return pl.pallas_call(
        body, name=name, grid=(nb,),
        in_specs=[blocked(a_w) if a_is_blocked else full(a_w), full(b_w) if a_is_blocked else blocked(b_w)],
        out_specs=pl.BlockSpec((None, a_w, b_w), lambda k: (k, 0, 0)),
        out_shape=jax.ShapeDtypeStruct((nb, a_w, b_w), BF16),
        compiler_params=_params(("arbitrary",)),
    )(a, b)


def _out_bwd(dh1b, w_out, y_attn, gattn, after=None):
    tm = 512
    n_t = SEQ // tm

    def body(dh_ref, wo_ref, y_ref, ga_ref, dycn_ref, dy_ref, dga_ref):
        @pl.when(pl.program_id(0) == 0)
        def _():
            dga_ref[...] = jnp.zeros_like(dga_ref)

        dycat = _dot_nt(dh_ref[...], wo_ref[...])
        dycn_ref[...] = dycat[:, :CONV_WIDTH]
        dy, dga = _rms_bwd(y_ref[...], ga_ref[...], dycat[:, CONV_WIDTH:])
        dy_ref[...] = dy
        dga_ref[...] += dga

    rows = lambda w: pl.BlockSpec((tm, w), lambda i: (i, 0))
    const = lambda shape: pl.BlockSpec(shape, lambda i: (0,) * len(shape))
    body, more_specs, more = _ordered_behind(body, 4, after)
    return pl.pallas_call(
        body, name="out_bwd", grid=(n_t,),
        in_specs=[rows(D_MODEL), const((D_MODEL, D_MODEL)), rows(ATTN_WIDTH), const((1, ATTN_WIDTH))] + more_specs,
        out_specs=[rows(CONV_WIDTH), rows(ATTN_WIDTH), const((1, ATTN_WIDTH))],
        out_shape=[jax.ShapeDtypeStruct((SEQ, CONV_WIDTH), F32), jax.ShapeDtypeStruct((SEQ, ATTN_WIDTH), F32),
                   jax.ShapeDtypeStruct((1, ATTN_WIDTH), F32)],
        compiler_params=_params(("arbitrary",)),
    )(dh1b, w_out, y_attn, gattn, *more)


def _attn_bwd(qn, kn, v, dy, probs, psinks, bkt, after=None):
    n_b = SEQ // BLK

    def body(q_ref, k_ref, v_ref, dy_ref, p_ref, ps_ref, bkt_ref,
             dq_ref, dk_ref, dv_ref, dtbl_ref, dsink_ref, dbias_ref, dsacc_ref):
        i = pl.program_id(0)

        @pl.when(i == 0)
        def _():
            dbias_ref[...] = jnp.zeros_like(dbias_ref)
            dsacc_ref[...] = jnp.zeros_like(dsacc_ref)
            dk_ref[...] = jnp.zeros_like(dk_ref)
            dv_ref[...] = jnp.zeros_like(dv_ref)

        kb, prev, cur = _band_rows(k_ref, i)
        vb, _, _ = _band_rows(v_ref, i)
        upper, _ = _band_masks(i)
        q = q_ref[...]
        dy = dy_ref[...]
        psink = ps_ref[...]
        lane = lax.broadcasted_iota(jnp.int32, (BLK, 128), 1)
        dsink = jnp.zeros((BLK, 128), F32)
        dqs, dks, dvs = [], [], []
        for g in range(N_HEADS // GQA_GROUP):
            kv = slice(HEAD_DIM * g, HEAD_DIM * (g + 1))
            qg = _stack_heads(q, g)
            dog = _stack_heads(dy, g).astype(BF16)
            pb = p_ref[g]
            pg = pb.astype(F32)
            dprobs = _fold(_dot_nt(dog, vb[:, kv]), upper)
            dvs.append(_dot_tn(_unfold(pb, upper), dog))
            dsum = jnp.sum(pg * dprobs, axis=-1, keepdims=True)
            dlogits = pg * (dprobs - dsum)
            for t in range(GQA_GROUP):
                dsink = jnp.where(lane == GQA_GROUP * g + t, -psink * dsum[BLK * t:BLK * (t + 1)], dsink)
            dbias_ref[g] += dlogits
            ds = _unfold(dlogits * (HEAD_DIM ** -0.5), upper).astype(BF16)
            dqs.append(_dot(ds, kb[:, kv]))
            dks.append(_dot_tn(ds, qg))
        dsacc_ref[...] += dsink
        dq_ref[...] = _unstack_heads(dqs)
        dkb = jnp.concatenate(dks, axis=-1)
        dvb = jnp.concatenate(dvs, axis=-1)
        dk_ref[pl.ds(prev, BLK), :] += dkb[:BLK]
        dk_ref[pl.ds(cur, BLK), :] += dkb[BLK:]
        dv_ref[pl.ds(prev, BLK), :] += dvb[:BLK]
        dv_ref[pl.ds(cur, BLK), :] += dvb[BLK:]

        @pl.when(i == n_b - 1)
        def _():
            bkt = bkt_ref[...]
            row8 = lax.broadcasted_iota(jnp.int32, (N_HEADS, 128), 0)
            lane8 = lax.broadcasted_iota(jnp.int32, (N_HEADS, 128), 1)
            acc = jnp.zeros((N_HEADS, 128), F32)
            for h in range(N_HEADS):
                rows = slice(BLK * (h % GQA_GROUP), BLK * (h % GQA_GROUP + 1))
                dbh = dbias_ref[h // GQA_GROUP, rows, :]
                for b in range(NUM_BUCKETS):
                    acc = jnp.where((row8 == h) & (lane8 == b), jnp.sum(jnp.where(bkt == b, dbh, 0.0)), acc)
            dsink_ref[...] = jnp.sum(dsacc_ref[...], axis=0, keepdims=True)
            dtbl_ref[...] = acc

    const = lambda shape: pl.BlockSpec(shape, lambda i: (0,) * len(shape))
    rows = lambda w: pl.BlockSpec((BLK, w), lambda i: (i, 0))
    n_g = N_HEADS // GQA_GROUP
    body, more_specs, more = _ordered_behind(body, 7, after)
    return pl.pallas_call(
        body, name="attn_bwd", grid=(n_b,),
        in_specs=[rows(ATTN_WIDTH), const((SEQ, KV_WIDTH)), const((SEQ, KV_WIDTH)), rows(ATTN_WIDTH),
                  pl.BlockSpec((None, n_g, GROUP_ROWS, BLK), lambda i: (i, 0, 0, 0)), rows(128),
                  const((BLK, BLK))] + more_specs,
        out_specs=[rows(ATTN_WIDTH), const((SEQ, KV_WIDTH)), const((SEQ, KV_WIDTH)), const((N_HEADS, 128)), const((1, 128))],
        out_shape=[jax.ShapeDtypeStruct((SEQ, ATTN_WIDTH), F32), jax.ShapeDtypeStruct((SEQ, KV_WIDTH), F32),
                   jax.ShapeDtypeStruct((SEQ, KV_WIDTH), F32), jax.ShapeDtypeStruct((N_HEADS, 128), F32),
                   jax.ShapeDtypeStruct((1, 128), F32)],
        scratch_shapes=[pltpu.VMEM((n_g, GROUP_ROWS, BLK), F32), pltpu.VMEM((BLK, 128), F32)],
        compiler_params=_params(("arbitrary",)),
    )(qn, kn, v, dy, probs, psinks, bkt, *more)


def _mix_in_bwd(x, dh1, proj, dycn, dqn, dkn, dv, w_in_t, conv_w, g1, gq, gk, gconv):
    tm = 256
    n_t = SEQ // tm
    halo_blocks = tm // SUBLANES

    def body(x_ref, dh1_ref, proj_ref, halo_ref, dycn_ref, dqn_ref, dkn_ref, dv_ref, w_ref, cw_ref,
             g1_ref, gq_ref, gk_ref, gc_ref,
             dx_ref, dproj_ref, dcw_ref, dgc_ref, dgq_ref, dgk_ref, dg1_ref, next_ref):
        i = pl.program_id(0)
        first_tile = i == n_t - 1

        @pl.when(i == 0)
        def _():
            for r in (dcw_ref, dgc_ref, dgq_ref, dgk_ref, dg1_ref, next_ref):
                r[...] = jnp.zeros_like(r)

        proj = proj_ref[...]
        hp = halo_ref[...]
        gate_b = proj[:, 0:CONV_WIDTH]
        gate_c = proj[:, CONV_WIDTH:2 * CONV_WIDTH]
        hc = proj[:, 2 * CONV_WIDTH:3 * CONV_WIDTH]
        a = gate_c * hc
        a_halo = jnp.where(first_tile, 0.0, hp[:, CONV_WIDTH:2 * CONV_WIDTH] * hp[:, 2 * CONV_WIDTH:3 * CONV_WIDTH])
        cw = _taps(cw_ref[...])
        cv, a2, a1 = _conv3(a, cw, a_halo)
        dyc, dgc = _rms_bwd(gate_b * cv, gc_ref[...], dycn_ref[...])
        dgc_ref[...] += dgc
        dcv = dyc * gate_b
        dcw_ref[...] += jnp.concatenate(
            [jnp.sum(dcv * a2, axis=0, keepdims=True), jnp.sum(dcv * a1, axis=0, keepdims=True),
             jnp.sum(dcv * a, axis=0, keepdims=True)], axis=0)
        da = _conv3_bwd_input(dcv, cw, next_ref[...])
        next_ref[...] = dcv[:SUBLANES]
        q0 = 3 * CONV_WIDTH
        k0 = q0 + ATTN_WIDTH
        dq, dgq = _head_norm_bwd(proj[:, q0:k0], gq_ref[...], dqn_ref[...], N_HEADS)
        dk, dgk = _head_norm_bwd(proj[:, k0:k0 + KV_WIDTH], gk_ref[...], dkn_ref[...], 2)
        dgq_ref[...] += dgq
        dgk_ref[...] += dgk
        dproj = jnp.concatenate([dyc * cv, da * hc, da * gate_c, dq, dk, dv_ref[...]], axis=-1).astype(BF16)
        dproj_ref[...] = dproj
        du1 = _dot(dproj, w_ref[...])
        xv = x_ref[...]
        dn, dg1 = _rms_bwd(xv, g1_ref[...], du1)
        dx_ref[...] = dh1_ref[...] + dn
        dg1_ref[...] += dg1

    rev = lambda i: n_t - 1 - i
    rows = lambda w: pl.BlockSpec((tm, w), lambda i: (rev(i), 0))
    const = lambda shape: pl.BlockSpec(shape, lambda i: (0,) * len(shape))
    halo = pl.BlockSpec((SUBLANES, IN_WIDTH), lambda i: (jnp.maximum(rev(i) * halo_blocks - 1, 0), 0))
    return pl.pallas_call(
        body, name="mix_in_bwd", grid=(n_t,),
        in_specs=[rows(D_MODEL), rows(D_MODEL), rows(IN_WIDTH), halo, rows(CONV_WIDTH), rows(ATTN_WIDTH), rows(KV_WIDTH),
                  rows(KV_WIDTH), const((IN_WIDTH, D_MODEL)), const((3, CONV_WIDTH)), const((1, D_MODEL)),
                  const((1, HEAD_DIM)), const((1, HEAD_DIM)), const((1, CONV_WIDTH))],
        out_specs=[rows(D_MODEL), rows(IN_WIDTH), const((3, CONV_WIDTH)), const((1, CONV_WIDTH)),
                   const((1, HEAD_DIM)), const((1, HEAD_DIM)), const((1, D_MODEL))],
        out_shape=[jax.ShapeDtypeStruct((SEQ, D_MODEL), F32), jax.ShapeDtypeStruct((SEQ, IN_WIDTH), BF16),
                   jax.ShapeDtypeStruct((3, CONV_WIDTH), F32),
                   jax.ShapeDtypeStruct((1, CONV_WIDTH), F32), jax.ShapeDtypeStruct((1, HEAD_DIM), F32),
                   jax.ShapeDtypeStruct((1, HEAD_DIM), F32), jax.ShapeDtypeStruct((1, D_MODEL), F32)],
        scratch_shapes=[pltpu.VMEM((SUBLANES, CONV_WIDTH), F32)],
        compiler_params=_params(("arbitrary",)),
    )(x, dh1, proj, proj, dycn, dqn, dkn, dv, w_in_t, conv_w, g1, gq, gk, gconv)


def _grad_w_in(dproj, u1, after=None):
    bw = 768

    def body(a_ref, b_ref, o_ref):
        o_ref[...] = _dot_tn(a_ref[...], b_ref[...]).astype(BF16)

    body, more_specs, more = _ordered_behind(body, 2, after)
    return pl.pallas_call(
        body, name="grad_w_in", grid=(IN_WIDTH // bw,),
        in_specs=[pl.BlockSpec((SEQ, bw), lambda k: (0, k)), pl.BlockSpec((SEQ, D_MODEL), lambda k: (0, 0))] + more_specs,
        out_specs=pl.BlockSpec((bw, D_MODEL), lambda k: (k, 0)),
        out_shape=jax.ShapeDtypeStruct((IN_WIDTH, D_MODEL), BF16),
        compiler_params=_params(("arbitrary",)),
    )(dproj, u1, *more)


def _adamw_math(w, g, m, v):
    m = ADAM_B1 * m + (1.0 - ADAM_B1) * g
    v = ADAM_B2 * v + (1.0 - ADAM_B2) * (g * g)
    m_hat = m / (1.0 - ADAM_B1 ** ADAM_STEP)
    v_hat = v / (1.0 - ADAM_B2 ** ADAM_STEP)
    return -ADAM_LR * (m_hat / (jnp.sqrt(v_hat) + ADAM_EPS) + ADAM_WD * w), m, v


_ROW_G1, _ROW_G2, _ROW_OUT_NORMS, _ROW_FFN_B, _ROW_GQ, _ROW_GK, _ROW_SINKS, _ROW_LOSS, _ROW_TABLE = 0, 1, 2, 3, 11, 12, 13, 14, 16
SMALL_ROWS, SMALL_COLS = 24, 1024
_SMALL_NAMES = ("norm_mix_g", "norm_ffn_g", "out_norm_conv_g", "out_norm_attn_g", "ffn_conv_b", "q_norm_g", "k_norm_g",
                "sinks", "rel_bias_table")


def _pack_small_grads(dg1, dg2, dgconv, dgattn, dfb, dgq, dgk, dsinks, dtbl_t, loss_acc):
    def body(dg1_ref, dg2_ref, dgc_ref, dga_ref, dfb_ref, dgq_ref, dgk_ref, ds_ref, dt_ref, loss_ref, o_ref):
        o_ref[...] = jnp.zeros_like(o_ref)
        o_ref[_ROW_G1:_ROW_G1 + 1, :] = dg1_ref[...]
        o_ref[_ROW_G2:_ROW_G2 + 1, :] = dg2_ref[...]
        o_ref[_ROW_OUT_NORMS:_ROW_OUT_NORMS + 1, 0:CONV_WIDTH] = dgc_ref[...]
        o_ref[_ROW_OUT_NORMS:_ROW_OUT_NORMS + 1, CONV_WIDTH:] = dga_ref[...]
        for k in range(N_DEV):
            o_ref[_ROW_FFN_B + k:_ROW_FFN_B + k + 1, 0:FFN_BLK] = dfb_ref[k // N_FFN_BLK, k % N_FFN_BLK]
        o_ref[_ROW_GQ:_ROW_GQ + 1, 0:HEAD_DIM] = dgq_ref[...]
        o_ref[_ROW_GK:_ROW_GK + 1, 0:HEAD_DIM] = dgk_ref[...]
        o_ref[_ROW_SINKS:_ROW_SINKS + 1, 0:128] = ds_ref[...]
        o_ref[_ROW_LOSS:_ROW_LOSS + 1, 0:128] = loss_ref[0:1, :]
        o_ref[_ROW_TABLE:_ROW_TABLE + N_HEADS, 0:128] = dt_ref[...]

    return pl.pallas_call(body, name="pack_small_grads", out_shape=jax.ShapeDtypeStruct((SMALL_ROWS, SMALL_COLS), F32))(
        dg1, dg2, dgconv, dgattn, dfb, dgq, dgk, dsinks, dtbl_t, loss_acc)


def _adamw_small(recv, params, after):
    names = _SMALL_NAMES
    n = len(names)

    def grad_of(g, name, k=None):
        if name == "norm_mix_g":
            return g[_ROW_G1:_ROW_G1 + 1, :]
        if name == "norm_ffn_g":
            return g[_ROW_G2:_ROW_G2 + 1, :]
        if name == "out_norm_conv_g":
            return g[_ROW_OUT_NORMS:_ROW_OUT_NORMS + 1, 0:CONV_WIDTH]
        if name == "out_norm_attn_g":
            return g[_ROW_OUT_NORMS:_ROW_OUT_NORMS + 1, CONV_WIDTH:]
        if name == "ffn_conv_b":
            return g[_ROW_FFN_B + k:_ROW_FFN_B + k + 1, 0:FFN_BLK]
        if name == "q_norm_g":
            return g[_ROW_GQ:_ROW_GQ + 1, 0:HEAD_DIM]
        if name == "k_norm_g":
            return g[_ROW_GK:_ROW_GK + 1, 0:HEAD_DIM]
        if name == "sinks":
            return g[_ROW_SINKS:_ROW_SINKS + 1, 0:N_HEADS]
        return g[_ROW_TABLE:_ROW_TABLE + N_HEADS, 0:NUM_BUCKETS]

    def body(r_ref, *refs):
        ins, outs, loss_ref = refs[:3 * n], refs[3 * n:7 * n], refs[7 * n]
        g = r_ref[0]
        for s in range(1, N_DEV):
            g = g + r_ref[s]
        loss_ref[...] = g[_ROW_LOSS:_ROW_LOSS + 1, 0:128]
        for i, name in enumerate(names):
            w_ref, m_ref, v_ref = ins[3 * i:3 * i + 3]
            o = outs[4 * i:4 * i + 4]
            cols = [slice(FFN_BLK * k, FFN_BLK * (k + 1)) for k in range(N_DEV)] if name == "ffn_conv_b" else [slice(None)]
            for k, cs in enumerate(cols):
                gk = grad_of(g, name, k)
                d, m2, v2 = _adamw_math(w_ref[:, cs], gk, m_ref[:, cs], v_ref[:, cs])
                o[0][:, cs], o[1][:, cs], o[2][:, cs], o[3][:, cs] = gk, d, m2, v2

    flat = [a for name in names for a in params[name]]
    body, more_specs, more = _ordered_behind(body, 1 + 3 * n, after)
    vmem = pl.BlockSpec(memory_space=pltpu.VMEM)
    out = pl.pallas_call(
        body, name="adamw_small",
        in_specs=[vmem] * (1 + 3 * n) + more_specs,
        out_shape=[jax.ShapeDtypeStruct(params[name][0].shape, F32) for name in names for _ in range(4)]
        + [jax.ShapeDtypeStruct((1, 128), F32)],
        compiler_params=pltpu.CompilerParams(vmem_limit_bytes=VMEM_LIMIT),
    )(recv, *flat, *more)
    return {name: tuple(out[4 * i:4 * i + 4]) for i, name in enumerate(names)}, out[4 * n]


def _adamw(w, m, v, part, recv, chip, name, row_blocks=1, after=None):
    rb = w.shape[0] // row_blocks
    tail = w.shape[1:]
    zeros = (0,) * len(tail)

    def body(chip_ref, w_ref, m_ref, v_ref, p_ref, r_ref, g_o, d_o, m_o, v_o):
        g = p_ref[...].astype(F32)
        for s in range(3):
            g = g + r_ref[s].astype(F32)
        g_o[...] = g
        d_o[...], m_o[...], v_o[...] = _adamw_math(w_ref[...], g, m_ref[...], v_ref[...])

    blk = pl.BlockSpec((rb,) + tail, lambda i, chip_ref: (i,) + zeros)
    pblk = pl.BlockSpec((None, rb) + tail, lambda i, chip_ref: (chip_ref[0], i) + zeros)
    rblk = pl.BlockSpec((3, rb) + tail, lambda i, chip_ref: (0, i) + zeros)
    body, more_specs, more = _ordered_behind(body, 6, after)
    return pl.pallas_call(
        body, name=name,
        grid_spec=pltpu.PrefetchScalarGridSpec(num_scalar_prefetch=1, grid=(row_blocks,),
                                               in_specs=[blk, blk, blk, pblk, rblk] + more_specs, out_specs=[blk] * 4),
        out_shape=[jax.ShapeDtypeStruct(w.shape, F32)] * 4,
        compiler_params=_params(("arbitrary",)),
    )(chip, w, m, v, part, recv, *more)


def kernel(x, norm_mix_g, w_in, conv_w, q_norm_g, k_norm_g, rel_bias_table, sinks, out_norm_conv_g, out_norm_attn_g, w_out, norm_ffn_g, w_up, ffn_conv_w, ffn_conv_b, w_down, loss_target, m_norm_mix_g, m_w_in, m_conv_w, m_q_norm_g, m_k_norm_g, m_rel_bias_table, m_sinks, m_out_norm_conv_g, m_out_norm_attn_g, m_w_out, m_norm_ffn_g, m_w_up, m_ffn_conv_w, m_ffn_conv_b, m_w_down, v_norm_mix_g, v_w_in, v_conv_w, v_q_norm_g, v_k_norm_g, v_rel_bias_table, v_sinks, v_out_norm_conv_g, v_out_norm_attn_g, v_w_out, v_norm_ffn_g, v_w_up, v_ffn_conv_w, v_ffn_conv_b, v_w_down):
    p = dict(norm_mix_g=norm_mix_g, w_in=w_in, conv_w=conv_w, q_norm_g=q_norm_g, k_norm_g=k_norm_g,
             rel_bias_table=rel_bias_table, sinks=sinks, out_norm_conv_g=out_norm_conv_g, out_norm_attn_g=out_norm_attn_g,
             w_out=w_out, norm_ffn_g=norm_ffn_g, w_up=w_up, ffn_conv_w=ffn_conv_w, ffn_conv_b=ffn_conv_b, w_down=w_down)
    m = dict(norm_mix_g=m_norm_mix_g, w_in=m_w_in, conv_w=m_conv_w, q_norm_g=m_q_norm_g, k_norm_g=m_k_norm_g,
             rel_bias_table=m_rel_bias_table, sinks=m_sinks, out_norm_conv_g=m_out_norm_conv_g,
             out_norm_attn_g=m_out_norm_attn_g, w_out=m_w_out, norm_ffn_g=m_norm_ffn_g, w_up=m_w_up,
             ffn_conv_w=m_ffn_conv_w, ffn_conv_b=m_ffn_conv_b, w_down=m_w_down)
    v = dict(norm_mix_g=v_norm_mix_g, w_in=v_w_in, conv_w=v_conv_w, q_norm_g=v_q_norm_g, k_norm_g=v_k_norm_g,
             rel_bias_table=v_rel_bias_table, sinks=v_sinks, out_norm_conv_g=v_out_norm_conv_g,
             out_norm_attn_g=v_out_norm_attn_g, w_out=v_w_out, norm_ffn_g=v_norm_ffn_g, w_up=v_w_up,
             ffn_conv_w=v_ffn_conv_w, ffn_conv_b=v_ffn_conv_b, w_down=v_w_down)

    xs, tgt = x[0], loss_target[0]
    g1, g2, gq, gk, gconv, gattn = norm_mix_g, norm_ffn_g, q_norm_g, k_norm_g, out_norm_conv_g, out_norm_attn_g
    ix, iy, ic = _coords()
    core = ic.astype(jnp.int32).reshape(1)
    chip = (2 * ix + iy).astype(jnp.int32).reshape(1)
    me = _lin(ix, iy, ic).astype(jnp.int32).reshape(1)
    bkt = jnp.asarray(_bucket_map())
    tr = lambda a: a[0].T
    taps = lambda a: jnp.transpose(a, (1, 0, 2))
    tbl_t = rel_bias_table.T

    wi_l, cw_l, wo_l, wu_l, wd_l, fcw_l = _place_shards(
        me, [tr(w_in), taps(conv_w), w_out[0], tr(w_up), w_down[0], taps(ffn_conv_w)], [BF16, F32, BF16, BF16, BF16, F32])
    finish_a, token_a = _all_gather_split([wi_l, cw_l], "mixer", None)
    ffn_stage2, ffn_stage3, token_b = _all_gather_tree([wo_l, wu_l, wd_l, fcw_l], "ffn", token_a)
    wi_g, cw_g = finish_a(token_b)
    w_in_t = wi_g.reshape(IN_WIDTH, D_MODEL)
    conv_w_f = jnp.transpose(cw_g[:, :, 0, :], (1, 0, 2)).reshape(3, CONV_WIDTH)

    proj, u1, ycn, qn, kn, vv = _mix_in_fwd(xs, g1, w_in_t, conv_w_f, gq, gk, gconv)
    token_b2 = ffn_stage2(ycn)
    y_attn, yan, probs, psinks = _attn_fwd(qn, kn, vv, tbl_t, sinks, bkt, gattn, after=token_b2)
    wo_g, wu_g, wd_g, fcw_g = ffn_stage3(yan)
    w_out_f = wo_g.reshape(D_MODEL, D_MODEL)
    w_down_f = wd_g.reshape(N_FFN_BLK, FFN_BLK, D_MODEL)
    w_up_f = wu_g.reshape(2, N_FFN_BLK, FFN_BLK, D_MODEL)
    fcw_f = fcw_g.reshape(2, N_FFN_BLK, 3, 1, FFN_BLK)
    fcb = ffn_conv_b.reshape(2, N_FFN_BLK, 1, FFN_BLK)
    h1, u2, up, pre, act, dh2, dh2b, loss_acc = _ffn_fwd_token_major(
        xs, ycn, yan, w_out_f, g2, w_up_f, fcw_f, fcb, w_down_f, tgt)

    dup, dh1, dh1b, dfb, dfcw, dg2 = _ffn_bwd_token_major(dh2, dh2b, h1, g2, up, pre, w_up_f, fcw_f, w_down_f)
    dw_down = _grad_tn_blocked(act, dh2b, "grad_w_down", a_is_blocked=True).reshape(N_DEV, D_FF // N_DEV, D_MODEL)
    dw_up = _grad_tn_blocked(dup.reshape(N_DEV, SEQ, FFN_BLK), u2, "grad_w_up", a_is_blocked=True)
    dw_out = _grad_tn([ycn, yan], dh1b, CONV_WIDTH, "grad_w_out").reshape(N_DEV, D_MODEL // N_DEV, D_MODEL)
    out_bwd = {}

    def behind_ffn(token):
        out_bwd["r"] = _out_bwd(dh1b, w_out_f, y_attn, gattn, after=token)
        return out_bwd["r"][0]

    finish_ffn, token_ffn = _reduce_scatter_split(
        [dw_down, dw_up, dw_out, dfcw.reshape(N_DEV, 3, 1, FFN_BLK)], "ffn", core, behind_ffn)
    dycn, dy_attn, dgattn = out_bwd["r"]
    dqn, dkn, dv, dtbl_t, dsinks = _attn_bwd(qn, kn, vv, dy_attn, probs, psinks, bkt, after=token_ffn)
    dx, dproj, dcw, dgconv, dgq, dgk, dg1 = _mix_in_bwd(xs, dh1, proj, dycn, dqn, dkn, dv, w_in_t, conv_w_f,
                                                             g1, gq, gk, gconv)
    packed = _pack_small_grads(dg1, dg2, dgconv, dgattn, dfb, dgq, dgk, dsinks, dtbl_t, loss_acc)
    plan_s, slots_s = _broadcast_plan()
    s_sem, r_sem, src_s, land_s, token_s = _split_start(
        "gather_small_start", [packed], [jnp.broadcast_to(packed[None], (N_DEV,) + packed.shape)], plan_s, None)
    dw_in_t = _grad_w_in(dproj, u1, after=token_s).reshape(N_DEV, IN_WIDTH // N_DEV, D_MODEL)
    dcw_b = jnp.transpose(dcw.reshape(3, N_DEV, 1, CONV_WIDTH // N_DEV), (1, 0, 2, 3))
    adam = {}
    ffn_got = {}

    def behind_mixer(token):
        ffn_got["r"] = finish_ffn(token)
        return ffn_got["r"][1][0]

    finish_mixer, token_mixer = _reduce_scatter_split([dw_in_t, dcw_b], "mixer", core, behind_mixer)
    (p_wd, p_wu, p_wo, p_fcw), (r_wd, r_wu, r_wo, r_fcw) = ffn_got["r"]
    adam["w_down"] = _adamw(w_down[0], m_w_down[0], v_w_down[0], p_wd, r_wd, chip, "adamw_w_down", row_blocks=2,
                            after=token_mixer)
    adam_up = _adamw(tr(w_up), tr(m_w_up), tr(v_w_up), p_wu, r_wu, chip, "adamw_w_up", row_blocks=4,
                     after=adam["w_down"][0])
    adam["w_out"] = _adamw(w_out[0], m_w_out[0], v_w_out[0], p_wo, r_wo, chip, "adamw_w_out", after=adam_up[0])
    adam_fcw = _adamw(taps(ffn_conv_w), taps(m_ffn_conv_w), taps(v_ffn_conv_w), p_fcw, r_fcw, chip, "adamw_ffn_conv_w",
                      after=adam["w_out"][0])
    _, (r_small,) = _split_wait("gather_small_wait", s_sem, r_sem, src_s, land_s, plan_s, slots_s, adam_fcw[0])
    small_in = {k: (p[k], m[k], v[k]) for k in _SMALL_NAMES}
    small_in["rel_bias_table"] = (tbl_t, m_rel_bias_table.T, v_rel_bias_table.T)
    small_out, loss_row = _adamw_small(r_small, small_in, None)
    (p_wi, p_cw), (r_wi, r_cw) = finish_mixer(loss_row)
    adam_in = _adamw(tr(w_in), tr(m_w_in), tr(v_w_in), p_wi, r_wi, chip, "adamw_w_in")
    adam_cw = _adamw(taps(conv_w), taps(m_conv_w), taps(v_conv_w), p_cw, r_cw, chip, "adamw_conv_w")

    res = {k: tuple(a[None] for a in t) for k, t in adam.items()}
    res["w_up"] = tuple(a.T[None] for a in adam_up)
    res["w_in"] = tuple(a.T[None] for a in adam_in)
    res["ffn_conv_w"] = tuple(taps(a) for a in adam_fcw)
    res["conv_w"] = tuple(taps(a) for a in adam_cw)
    res.update(small_out)
    res["rel_bias_table"] = tuple(a.T for a in small_out["rel_bias_table"])
    loss = loss_row[0, 0]
    order = ("norm_mix_g", "w_in", "conv_w", "q_norm_g", "k_norm_g", "rel_bias_table", "sinks", "out_norm_conv_g",
             "out_norm_attn_g", "w_out", "norm_ffn_g", "w_up", "ffn_conv_w", "ffn_conv_b", "w_down")
    return (loss, dx[None], *[res[k][0] for k in order], *[res[k][1] for k in order],
            *[res[k][2] for k in order], *[res[k][3] for k in order])
```

```python
import functools
import math

import numpy as np
import jax
import jax.numpy as jnp
from jax import lax
from jax.experimental import pallas as pl
from jax.experimental.pallas import tpu as pltpu

F32 = jnp.float32
BF16 = jnp.bfloat16

SEQ = 2048
D_MODEL = 1024
CONV_WIDTH = 512
ATTN_WIDTH = 512
KV_WIDTH = 128
HEAD_DIM = 64
N_HEADS = 8
GQA_GROUP = 4
IN_WIDTH = 2304
D_FF = 2816
BLK = 128
NUM_BUCKETS = 32
EPS = 1e-6
NEG_INF = -1e30
ADAM_LR = 0.001
ADAM_B1 = 0.9
ADAM_B2 = 0.999
ADAM_EPS = 1e-08
ADAM_WD = 0.01
ADAM_STEP = 10

N_DEV = 8
FFN_BLK = 2 * D_FF // N_DEV
N_FFN_BLK = D_FF // FFN_BLK
SUBLANES = 8
VMEM_LIMIT = 56 * 1024 * 1024

_MESH = pl.DeviceIdType.MESH
_ANY = pl.BlockSpec(memory_space=pl.ANY)


def _params(sem):
    return pltpu.CompilerParams(dimension_semantics=sem, vmem_limit_bytes=VMEM_LIMIT)


def _ordered_behind(body, pos, after):
    if after is None:
        return body, [], []
    return (lambda *refs: body(*refs[:pos], *refs[pos + 1:])), [_ANY], [after]


def _dot(a, b):
    return jnp.dot(a, b, preferred_element_type=F32)


def _dot_nt(a, b):
    return lax.dot_general(a, b, (((1,), (1,)), ((), ())), preferred_element_type=F32)


def _dot_tn(a, b):
    return lax.dot_general(a, b, (((0,), (0,)), ((), ())), preferred_element_type=F32)


def _shift_down(x, s, halo):
    r = pltpu.roll(x, s, axis=0)
    hr = pltpu.roll(halo, s, axis=0)
    row = lax.broadcasted_iota(jnp.int32, halo.shape, 0)
    top = jnp.where(row < s, hr, r[:SUBLANES])
    return jnp.concatenate([top, r[SUBLANES:]], axis=0)


def _shift_up(x, s, halo):
    n = x.shape[0]
    r = pltpu.roll(x, n - s, axis=0)
    hr = pltpu.roll(halo, SUBLANES - s, axis=0)
    row = lax.broadcasted_iota(jnp.int32, halo.shape, 0)
    bot = jnp.where(row >= SUBLANES - s, hr, r[n - SUBLANES:])
    return jnp.concatenate([r[:n - SUBLANES], bot], axis=0)


def _taps(w):
    return (w[0], w[1], w[2]) if len(w.shape) == 3 else (w[0:1], w[1:2], w[2:3])


def _conv3(x, w, halo):
    x2 = _shift_down(x, 2, halo)
    x1 = _shift_down(x, 1, halo)
    return x2 * w[0] + x1 * w[1] + x * w[2], x2, x1


def _conv3_bwd_input(dy, w, halo_next):
    return dy * w[2] + _shift_up(dy, 1, halo_next) * w[1] + _shift_up(dy, 2, halo_next) * w[0]


def _rstd(x):
    return lax.rsqrt(jnp.mean(x * x, axis=-1, keepdims=True) + EPS)


def _rms_bwd(x, g, dy):
    r = _rstd(x)
    n = x * r
    dn = dy * g
    dx = r * (dn - n * jnp.mean(dn * n, axis=-1, keepdims=True))
    return dx, jnp.sum(dy * n, axis=0, keepdims=True)


def _head_mean(x):
    width = x.shape[-1]
    ri = lax.broadcasted_iota(jnp.int32, (width, width), 0) // HEAD_DIM
    ci = lax.broadcasted_iota(jnp.int32, (width, width), 1) // HEAD_DIM
    ones = jnp.where(ri == ci, 1.0, 0.0).astype(BF16)
    hi = x.astype(BF16)
    lo = (x - hi.astype(F32)).astype(BF16)
    return (_dot(hi, ones) + _dot(lo, ones)) * (1.0 / HEAD_DIM)


def _head_norm(x, g, heads):
    return x * lax.rsqrt(_head_mean(x * x) + EPS) * jnp.tile(g, (1, heads))


def _head_norm_bwd(x, g, dy, heads):
    r = lax.rsqrt(_head_mean(x * x) + EPS)
    n = x * r
    dn = dy * jnp.tile(g, (1, heads))
    dx = r * (dn - n * _head_mean(dn * n))
    per_lane = jnp.sum(dy * n, axis=0, keepdims=True)
    dg = per_lane[:, 0:HEAD_DIM]
    for h in range(1, heads):
        dg = dg + per_lane[:, HEAD_DIM * h:HEAD_DIM * (h + 1)]
    return dx, dg


def _bucket_map():
    q = np.arange(BLK)[:, None]
    j = np.arange(BLK)[None, :]
    n = np.where(j > q, q + BLK - j, q - j)
    nf = np.maximum(n, 1).astype(np.float32)
    max_exact = NUM_BUCKETS // 2
    large = max_exact + (np.log(nf / max_exact) / math.log(BLK / max_exact) * (NUM_BUCKETS - max_exact)).astype(np.int32)
    large = np.minimum(large, NUM_BUCKETS - 1)
    return np.where(n < max_exact, n, large).astype(np.int32)


def _coords():
    return lax.axis_index("x"), lax.axis_index("y"), lax.axis_index("c")


def _lin(px, py, pc):
    return 4 * px + 2 * py + pc


_HBM = pl.BlockSpec(memory_space=pltpu.HBM)
_SEM = pl.BlockSpec(memory_space=pltpu.SEMAPHORE)
_EFFECT = pltpu.SideEffectType.DATAFLOW_SIDE_EFFECTING


def _in_hbm(a):
    return pltpu.with_memory_space_constraint(a, pltpu.HBM)


def _split_start(name, srcs, lands, plan, after):
    ns, nl = len(srcs), len(lands)
    n_copies = len(plan(0, 0, 0))
    n_after = 0 if after is None else 1

    def body(*refs):
        src_refs, land_refs = refs[:ns + nl], refs[ns:ns + nl]
        send_sems, recv_sems = refs[ns + nl + n_after], refs[ns + nl + n_after + 1]
        token = refs[-1]
        for k, (a, s_slot, l, d_slot, dev) in enumerate(plan(*_coords())):
            src = src_refs[a] if s_slot is None else src_refs[a].at[s_slot]
            pltpu.make_async_remote_copy(src_ref=src, dst_ref=land_refs[l].at[d_slot], send_sem=send_sems.at[k],
                                         recv_sem=recv_sems.at[k], device_id=dev, device_id_type=_MESH).start()
        token[...] = jnp.zeros_like(token)

    arrs = list(srcs) + list(lands)
    out = pl.pallas_call(
        body, name=name,
        out_shape=(pltpu.SemaphoreType.DMA((n_copies,)), pltpu.SemaphoreType.DMA((n_copies,)),
                   *[pltpu.HBM(a.shape, a.dtype) for a in arrs], jax.ShapeDtypeStruct((SUBLANES, 128), F32)),
        in_specs=[_HBM] * (ns + nl) + [_ANY] * n_after,
        out_specs=(_SEM, _SEM, *[_HBM] * (ns + nl), pl.BlockSpec(memory_space=pltpu.VMEM)),
        input_output_aliases={i: 2 + i for i in range(ns + nl)},
        compiler_params=pltpu.CompilerParams(has_side_effects=_EFFECT),
    )(*[_in_hbm(a) for a in arrs], *([] if after is None else [after]))
    return out[0], out[1], list(out[2:2 + ns]), list(out[2 + ns:2 + ns + nl]), out[-1]


def _split_wait(name, send_sems, recv_sems, srcs, lands, plan, recv_slots, after):
    ns, nl = len(srcs), len(lands)

    def body(*refs):
        src_refs, land_refs = refs[:ns + nl], refs[ns:ns + nl]
        send_sems, recv_sems = refs[ns + nl], refs[ns + nl + 1]
        coords = _coords()
        slots = recv_slots(*coords)
        for k, (a, s_slot, l, _, dev) in enumerate(plan(*coords)):
            src = src_refs[a] if s_slot is None else src_refs[a].at[s_slot]
            cp = pltpu.make_async_remote_copy(src_ref=src, dst_ref=land_refs[l].at[slots[k]], send_sem=send_sems.at[k],
                                              recv_sem=recv_sems.at[k], device_id=dev, device_id_type=_MESH)
            cp.wait_send()
            cp.wait_recv()

    arrs = list(srcs) + list(lands)
    out = pl.pallas_call(
        body, name=name,
        out_shape=tuple(pltpu.HBM(a.shape, a.dtype) for a in arrs),
        in_specs=[_HBM] * (ns + nl) + [_SEM, _SEM, _ANY],
        out_specs=tuple([_HBM] * (ns + nl)),
        input_output_aliases={i: i for i in range(ns + nl)},
        compiler_params=pltpu.CompilerParams(has_side_effects=_EFFECT),
    )(*arrs, send_sems, recv_sems, after)
    return list(out[:ns]), list(out[ns:])


def _chips(x, y):
    return [(1 - x, y), (x, 1 - y), (1 - x, 1 - y)]


def _gather_plan_ici(n):
    def plan(x, y, c):
        me = _lin(x, y, c)
        out = []
        for a in range(n):
            out.append((a, me, a, me, (x, y, 1 - c)))
            out += [(a, me, a, me, (cx, cy, c)) for cx, cy in _chips(x, y)]
        return out

    def recv_slots(x, y, c):
        out = []
        for _ in range(n):
            out.append(_lin(x, y, 1 - c))
            out += [_lin(cx, cy, c) for cx, cy in _chips(x, y)]
        return out

    return plan, recv_slots


def _gather_plan_d2d(n):
    def plan(x, y, c):
        return [(a, _lin(cx, cy, c), a, _lin(cx, cy, c), (x, y, 1 - c)) for a in range(n) for cx, cy in _chips(x, y)]

    def recv_slots(x, y, c):
        return [_lin(cx, cy, 1 - c) for _ in range(n) for cx, cy in _chips(x, y)]

    return plan, recv_slots


def _all_gather_split(lands, tag, after):
    n = len(lands)
    plan1, slots1 = _gather_plan_ici(n)
    s1, r1, _, lands, token = _split_start(f"gather_{tag}_ici_start", [], lands, plan1, after)

    def finish(after):
        _, got = _split_wait(f"gather_{tag}_ici_wait", s1, r1, [], lands, plan1, slots1, after)
        plan2, slots2 = _gather_plan_d2d(n)
        s2, r2, _, got, token2 = _split_start(f"gather_{tag}_d2d_start", [], got, plan2, None)
        return _split_wait(f"gather_{tag}_d2d_wait", s2, r2, [], got, plan2, slots2, token2)[1]

    return finish, token


def _all_gather_tree(lands, tag, after):
    n = len(lands)

    def plan1(x, y, c):
        me = _lin(x, y, c)
        return [(a, me, a, me, dev) for a in range(n) for dev in ((x, y, 1 - c), (1 - x, y, c), (x, 1 - y, c))]

    def slots1(x, y, c):
        return [s for _ in range(n) for s in (_lin(x, y, 1 - c), _lin(1 - x, y, c), _lin(x, 1 - y, c))]

    def plan2(x, y, c):
        from_x, from_y = _lin(1 - x, y, c), _lin(x, 1 - y, c)
        north = c == 1
        passed = jnp.where(north, from_x, from_y)
        onward = (jnp.where(north, x, 1 - x), jnp.where(north, 1 - y, y), c)
        sib = (x, y, 1 - c)
        return [cp for a in range(n) for cp in ((a, passed, a, passed, onward), (a, from_x, a, from_x, sib),
                                                (a, from_y, a, from_y, sib))]

    def slots2(x, y, c):
        return [s for _ in range(n) for s in (_lin(1 - x, 1 - y, c), _lin(1 - x, y, 1 - c), _lin(x, 1 - y, 1 - c))]

    def plan3(x, y, c):
        diag = _lin(1 - x, 1 - y, c)
        return [(a, diag, a, diag, (x, y, 1 - c)) for a in range(n)]

    def slots3(x, y, c):
        return [_lin(1 - x, 1 - y, 1 - c)] * n

    s1, r1, _, lands, token = _split_start(f"gather_{tag}_1_start", [], lands, plan1, after)
    state = {}

    def stage2(after):
        _, got = _split_wait(f"gather_{tag}_1_wait", s1, r1, [], lands, plan1, slots1, after)
        state["s"], state["r"], _, state["lands"], token2 = _split_start(f"gather_{tag}_2_start", [], got, plan2, None)
        return token2

    def stage3(after):
        _, got = _split_wait(f"gather_{tag}_2_wait", state["s"], state["r"], [], state["lands"], plan2, slots2, after)
        s3, r3, _, got, token3 = _split_start(f"gather_{tag}_3_start", [], got, plan3, None)
        return _split_wait(f"gather_{tag}_3_wait", s3, r3, [], got, plan3, slots3, token3)[1]

    return stage2, stage3, token


_CHIP_LIST = ((0, 0), (0, 1), (1, 0), (1, 1))


def _reduce_plan_d2d(n):
    def plan(x, y, c):
        return [(a, _lin(qx, qy, 1 - c), a, q, (x, y, 1 - c)) for a in range(n) for q, (qx, qy) in enumerate(_CHIP_LIST)]

    def recv_slots(x, y, c):
        return [q for _ in range(n) for q in range(4)]

    return plan, recv_slots


def _reduce_plan_ici(n):
    def plan(x, y, c):
        return [(a, 2 * cx + cy, a, j, (cx, cy, c)) for a in range(n) for j, (cx, cy) in enumerate(_chips(x, y))]

    def recv_slots(x, y, c):
        return [j for _ in range(n) for j in range(3)]

    return plan, recv_slots


def _peers(x, y, c):
    return [(1 - x if r & 4 else x, 1 - y if r & 2 else y, 1 - c if r & 1 else c) for r in range(1, N_DEV)]


def _scatter_plan(n):
    def plan(x, y, c):
        return [(a, _lin(*peer), a, r, peer) for a in range(n) for r, peer in enumerate(_peers(x, y, c))]

    def recv_slots(x, y, c):
        return [r for _ in range(n) for r in range(N_DEV - 1)]

    return plan, recv_slots


def _broadcast_plan():
    peers = _peers

    def plan(x, y, c):
        return [(0, None, 0, _lin(x, y, c), peer) for peer in peers(x, y, c)]

    def recv_slots(x, y, c):
        return [_lin(*peer) for peer in peers(x, y, c)]

    return plan, recv_slots


def _chip_partial(grads, recvd, core, name):
    n = len(grads)

    def body(c_ref, *refs):
        for a in range(n):
            g_ref, r_ref, o_ref = refs[a], refs[n + a], refs[2 * n + a]
            o_ref[...] = (g_ref[...].astype(F32) + r_ref[...].astype(F32)).astype(o_ref.dtype)

    def blk(a, own):
        zeros = (0,) * (a.ndim - 1)
        return pl.BlockSpec((None,) + a.shape[1:],
                            (lambda q, c_ref: (2 * q + c_ref[0],) + zeros) if own else (lambda q, c_ref: (q,) + zeros))

    return pl.pallas_call(
        body, name=name,
        grid_spec=pltpu.PrefetchScalarGridSpec(
            num_scalar_prefetch=1, grid=(4,),
            in_specs=[blk(a, True) for a in grads] + [blk(a, False) for a in recvd],
            out_specs=[blk(a, False) for a in recvd]),
        out_shape=[jax.ShapeDtypeStruct(a.shape, a.dtype) for a in recvd],
        compiler_params=_params(("arbitrary",)),
    )(core, *grads, *recvd)


def _reduce_scatter_split(grads, tag, core, behind):
    n = len(grads)
    plan1, slots1 = _reduce_plan_d2d(n)
    lands1 = [lax.empty((4,) + a.shape[1:], a.dtype) for a in grads]
    s1, r1, srcs1, lands1, token1 = _split_start(f"reduce_{tag}_d2d_start", grads, lands1, plan1, None)
    own, got = _split_wait(f"reduce_{tag}_d2d_wait", s1, r1, srcs1, lands1, plan1, slots1, behind(token1))
    parts = _chip_partial(own, got, core, f"reduce_{tag}_partial")
    plan2, slots2 = _reduce_plan_ici(n)
    lands2 = [lax.empty((3,) + a.shape[1:], a.dtype) for a in grads]
    s2, r2, srcs2, lands2, token2 = _split_start(f"reduce_{tag}_ici_start", parts, lands2, plan2, None)

    def finish(after):
        return _split_wait(f"reduce_{tag}_ici_wait", s2, r2, srcs2, lands2, plan2, slots2, after)

    return finish, token2


def _place_shards(me, shards, dtypes):
    n = len(shards)

    def body(me_ref, *refs):
        for a in range(n):
            refs[n + a][...] = refs[a][...].astype(dtypes[a])

    full = lambda s: pl.BlockSpec(s.shape, lambda i, me_ref: (0,) * s.ndim)
    slot = lambda s: pl.BlockSpec((None,) + s.shape, lambda i, me_ref: (me_ref[0],) + (0,) * s.ndim)
    return pl.pallas_call(
        body, name="place_shards",
        grid_spec=pltpu.PrefetchScalarGridSpec(num_scalar_prefetch=1, grid=(1,), in_specs=[full(s) for s in shards],
                                               out_specs=[slot(s) for s in shards]),
        out_shape=[jax.ShapeDtypeStruct((N_DEV,) + s.shape, d) for s, d in zip(shards, dtypes)],
        compiler_params=_params(("arbitrary",)),
    )(me, *shards)


def _mix_in_fwd(x, g1, w_in_t, conv_w, gq, gk, gconv):
    tm = 512
    n_t = SEQ // tm

    def body(x_ref, g1_ref, w_ref, cw_ref, gq_ref, gk_ref, gc_ref,
             proj_ref, u1_ref, ycn_ref, qn_ref, kn_ref, v_ref, halo_ref):
        @pl.when(pl.program_id(0) == 0)
        def _():
            halo_ref[...] = jnp.zeros_like(halo_ref)

        xv = x_ref[...]
        u = (xv * _rstd(xv) * g1_ref[...]).astype(BF16)
        u1_ref[...] = u
        proj = _dot_nt(u, w_ref[...])
        proj_ref[...] = proj
        gate_b = proj[:, 0:CONV_WIDTH]
        a = proj[:, CONV_WIDTH:2 * CONV_WIDTH] * proj[:, 2 * CONV_WIDTH:3 * CONV_WIDTH]
        cv, _, _ = _conv3(a, _taps(cw_ref[...]), halo_ref[...])
        halo_ref[...] = a[tm - SUBLANES:]
        yc = gate_b * cv
        ycn_ref[...] = (yc * _rstd(yc) * gc_ref[...]).astype(BF16)
        q0 = 3 * CONV_WIDTH
        qn_ref[...] = _head_norm(proj[:, q0:q0 + ATTN_WIDTH], gq_ref[...], N_HEADS).astype(BF16)
        k0 = q0 + ATTN_WIDTH
        kn_ref[...] = _head_norm(proj[:, k0:k0 + KV_WIDTH], gk_ref[...], 2).astype(BF16)
        v_ref[...] = proj[:, k0 + KV_WIDTH:k0 + 2 * KV_WIDTH].astype(BF16)

    const = lambda shape: pl.BlockSpec(shape, lambda i: (0,) * len(shape))
    rows = lambda w: pl.BlockSpec((tm, w), lambda i: (i, 0))
    return pl.pallas_call(
        body, name="mix_in_fwd", grid=(n_t,),
        in_specs=[rows(D_MODEL), const((1, D_MODEL)), const((IN_WIDTH, D_MODEL)), const((3, CONV_WIDTH)),
                  const((1, HEAD_DIM)), const((1, HEAD_DIM)), const((1, CONV_WIDTH))],
        out_specs=[rows(IN_WIDTH), rows(D_MODEL), rows(CONV_WIDTH), rows(ATTN_WIDTH), rows(KV_WIDTH), rows(KV_WIDTH)],
        out_shape=[jax.ShapeDtypeStruct((SEQ, IN_WIDTH), F32), jax.ShapeDtypeStruct((SEQ, D_MODEL), BF16),
                   jax.ShapeDtypeStruct((SEQ, CONV_WIDTH), BF16),
                   jax.ShapeDtypeStruct((SEQ, ATTN_WIDTH), BF16), jax.ShapeDtypeStruct((SEQ, KV_WIDTH), BF16),
                   jax.ShapeDtypeStruct((SEQ, KV_WIDTH), BF16)],
        scratch_shapes=[pltpu.VMEM((SUBLANES, CONV_WIDTH), F32)],
        compiler_params=_params(("arbitrary",)),
    )(x, g1, w_in_t, conv_w, gq, gk, gconv)


GROUP_ROWS = GQA_GROUP * BLK


def _band_bias(tbl_ref, bkt, bias_ref):
    for h in range(N_HEADS):
        acc = jnp.zeros(bkt.shape, F32)
        for b in range(NUM_BUCKETS):
            acc = jnp.where(bkt == b, tbl_ref[h, b], acc)
        bias_ref[h // GQA_GROUP, BLK * (h % GQA_GROUP):BLK * (h % GQA_GROUP + 1), :] = acc


def _band_masks(i):
    qi = lax.broadcasted_iota(jnp.int32, (GROUP_ROWS, BLK), 0) & (BLK - 1)
    ji = lax.broadcasted_iota(jnp.int32, (GROUP_ROWS, BLK), 1)
    upper = ji > qi
    return upper, upper & (i == 0)


def _stack_heads(x, g):
    return jnp.concatenate([x[:, HEAD_DIM * h:HEAD_DIM * (h + 1)] for h in range(GQA_GROUP * g, GQA_GROUP * (g + 1))], axis=0)


def _unstack_heads(groups):
    return jnp.concatenate([p[BLK * t:BLK * (t + 1)] for p in groups for t in range(GQA_GROUP)], axis=-1)


def _per_head_rows(vals):
    row = lax.broadcasted_iota(jnp.int32, (GROUP_ROWS, 1), 0)
    col = jnp.full((GROUP_ROWS, 1), vals[GQA_GROUP - 1], F32)
    for t in range(GQA_GROUP - 2, -1, -1):
        col = jnp.where(row < BLK * (t + 1), vals[t], col)
    return col


def _band_rows(ref, i):
    prev = pl.multiple_of(jnp.maximum(i - 1, 0) * BLK, BLK)
    cur = pl.multiple_of(i * BLK, BLK)
    return jnp.concatenate([ref[pl.ds(prev, BLK), :], ref[pl.ds(cur, BLK), :]], axis=0), prev, cur


def _fold(band, upper):
    return jnp.where(upper, band[:, :BLK], band[:, BLK:])


def _unfold(tile, upper):
    return jnp.concatenate([jnp.where(upper, tile, 0.0), jnp.where(upper, 0.0, tile)], axis=1)


def _head_probs(qh, kh, bias, upper, dead, sink):
    logits = _fold(_dot_nt(qh, kh), upper) * (HEAD_DIM ** -0.5) + bias
    logits = jnp.where(dead, NEG_INF, logits)
    m = jnp.maximum(jnp.max(logits, axis=-1, keepdims=True), sink)
    p = jnp.exp(logits - m)
    es = jnp.exp(sink - m)
    den = jnp.sum(p, axis=-1, keepdims=True) + es
    return p / den, es / den


def _attn_fwd(qn, kn, v, tbl, sinks, bkt, gattn, after=None):
    n_b = SEQ // BLK

    def body(q_ref, k_ref, v_ref, tbl_ref, sink_ref, bkt_ref, ga_ref, y_ref, yn_ref, p_ref, ps_ref, bias_ref):
        i = pl.program_id(0)

        @pl.when(i == 0)
        def _():
            _band_bias(tbl_ref, bkt_ref[...], bias_ref)

        kb, _, _ = _band_rows(k_ref, i)
        vb, _, _ = _band_rows(v_ref, i)
        upper, dead = _band_masks(i)
        q = q_ref[...]
        lane = lax.broadcasted_iota(jnp.int32, (BLK, 128), 1)
        outs = []
        psinks = jnp.zeros((BLK, 128), F32)
        for g in range(N_HEADS // GQA_GROUP):
            kv = slice(HEAD_DIM * g, HEAD_DIM * (g + 1))
            sink = _per_head_rows([sink_ref[0, GQA_GROUP * g + t] for t in range(GQA_GROUP)])
            probs, psink = _head_probs(_stack_heads(q, g), kb[:, kv], bias_ref[g], upper, dead, sink)
            p_ref[g] = probs.astype(BF16)
            for t in range(GQA_GROUP):
                psinks = jnp.where(lane == GQA_GROUP * g + t, psink[BLK * t:BLK * (t + 1)], psinks)
            outs.append(_dot(_unfold(probs, upper).astype(BF16), vb[:, kv]))
        ps_ref[...] = psinks
        y = _unstack_heads(outs)
        y_ref[...] = y
        yn_ref[...] = (y * _rstd(y) * ga_ref[...]).astype(BF16)

    const = lambda shape: pl.BlockSpec(shape, lambda i: (0,) * len(shape))
    rows = lambda w: pl.BlockSpec((BLK, w), lambda i: (i, 0))
    smem = pl.BlockSpec(memory_space=pltpu.SMEM)
    body, more_specs, more = _ordered_behind(body, 7, after)
    return pl.pallas_call(
        body, name="attn_fwd", grid=(n_b,),
        in_specs=[rows(ATTN_WIDTH), const((SEQ, KV_WIDTH)), const((SEQ, KV_WIDTH)), smem, smem,
                  const((BLK, BLK)), const((1, ATTN_WIDTH))] + more_specs,
        out_specs=[rows(ATTN_WIDTH), rows(ATTN_WIDTH),
                   pl.BlockSpec((None, N_HEADS // GQA_GROUP, GROUP_ROWS, BLK), lambda i: (i, 0, 0, 0)), rows(128)],
        out_shape=[jax.ShapeDtypeStruct((SEQ, ATTN_WIDTH), F32), jax.ShapeDtypeStruct((SEQ, ATTN_WIDTH), BF16),
                   jax.ShapeDtypeStruct((n_b, N_HEADS // GQA_GROUP, GROUP_ROWS, BLK), BF16),
                   jax.ShapeDtypeStruct((SEQ, 128), F32)],
        scratch_shapes=[pltpu.VMEM((N_HEADS // GQA_GROUP, GROUP_ROWS, BLK), F32)],
        compiler_params=_params(("arbitrary",)),
    )(qn, kn, v, tbl, sinks, bkt, gattn, *more)


def _out_proj(x, ycn, yan, w_out, g2):
    tm = 512

    def body(x_ref, ycn_ref, yan_ref, wo_ref, g2_ref, h1_ref, u2_ref):
        h1 = x_ref[...] + _dot(ycn_ref[...], wo_ref[0:CONV_WIDTH, :]) + _dot(yan_ref[...], wo_ref[CONV_WIDTH:, :])
        h1_ref[...] = h1
        u2_ref[...] = (h1 * _rstd(h1) * g2_ref[...]).astype(BF16)

    rows = lambda w: pl.BlockSpec((tm, w), lambda i: (i, 0))
    const = lambda shape: pl.BlockSpec(shape, lambda i: (0,) * len(shape))
    return pl.pallas_call(
        body, name="out_proj", grid=(SEQ // tm,),
        in_specs=[rows(D_MODEL), rows(CONV_WIDTH), rows(ATTN_WIDTH), const((D_MODEL, D_MODEL)), const((1, D_MODEL))],
        out_specs=[rows(D_MODEL), rows(D_MODEL)],
        out_shape=[jax.ShapeDtypeStruct((SEQ, D_MODEL), F32), jax.ShapeDtypeStruct((SEQ, D_MODEL), BF16)],
        compiler_params=_params(("arbitrary",)),
    )(x, ycn, yan, w_out, g2)


def _ffn_fwd(h1, u2, w_up, fcw, fcb, w_down, tgt):
    tm = 512
    n_t = SEQ // tm
    last = N_FFN_BLK - 1

    def body(u2_ref, wu_ref, cw_ref, b_ref, wd_ref, h1_ref, tgt_ref,
             up_ref, pre_ref, act_ref, dh2_ref, dh2b_ref, loss_ref, acc_ref, halo_ref):
        j, i = pl.program_id(0), pl.program_id(1)
        rows = pl.ds(pl.multiple_of(i * tm, tm), tm)

        @pl.when((i == 0) & (j == 0))
        def _():
            loss_ref[...] = jnp.zeros_like(loss_ref)

        u2 = u2_ref[...]
        pre = []
        for s in range(2):
            up = _dot_nt(u2, wu_ref[s])
            up_ref[s] = up.astype(BF16)
            halo = jnp.where(i == 0, 0.0, halo_ref[s])
            pre.append(_conv3(up, _taps(cw_ref.at[s]), halo)[0] + b_ref[s])
            pre_ref[s] = pre[s].astype(BF16)
            halo_ref[s] = up[tm - SUBLANES:]
        g, val = pre
        act = (g * jax.nn.sigmoid(g) * val).astype(BF16)
        act_ref[...] = act
        out = _dot(act, wd_ref[...])

        @pl.when(j == 0)
        def _():
            acc_ref[rows, :] = out

        @pl.when(j > 0)
        def _():
            acc_ref[rows, :] += out

        @pl.when(j == last)
        def _():
            err = h1_ref[...] + acc_ref[rows, :] - tgt_ref[...]
            loss_ref[...] += 0.5 * jnp.sum(err * err) / D_MODEL
            dh2 = err / D_MODEL
            dh2_ref[...] = dh2
            dh2b_ref[...] = dh2.astype(BF16)

    late = lambda w: pl.BlockSpec((tm, w), lambda j, i: (jnp.where(j == last, i, 0), 0))
    pair = lambda *s: pl.BlockSpec((2, None) + s, lambda j, i: (0, j) + (0,) * len(s))
    upb = pl.BlockSpec((2, None, tm, FFN_BLK), lambda j, i: (0, j, i, 0))
    return pl.pallas_call(
        body, name="ffn_fwd", grid=(N_FFN_BLK, n_t),
        in_specs=[pl.BlockSpec((tm, D_MODEL), lambda j, i: (i, 0)), pair(FFN_BLK, D_MODEL), pair(3, 1, FFN_BLK),
                  pair(1, FFN_BLK), pl.BlockSpec((None, FFN_BLK, D_MODEL), lambda j, i: (j, 0, 0)),
                  late(D_MODEL), late(D_MODEL)],
        out_specs=[upb, upb, pl.BlockSpec((None, tm, FFN_BLK), lambda j, i: (j, i, 0)), late(D_MODEL), late(D_MODEL),
                   pl.BlockSpec((SUBLANES, 128), lambda j, i: (0, 0))],
        out_shape=[jax.ShapeDtypeStruct((2, N_FFN_BLK, SEQ, FFN_BLK), BF16),
                   jax.ShapeDtypeStruct((2, N_FFN_BLK, SEQ, FFN_BLK), BF16),
                   jax.ShapeDtypeStruct((N_FFN_BLK, SEQ, FFN_BLK), BF16),
                   jax.ShapeDtypeStruct((SEQ, D_MODEL), F32), jax.ShapeDtypeStruct((SEQ, D_MODEL), BF16),
                   jax.ShapeDtypeStruct((SUBLANES, 128), F32)],
        scratch_shapes=[pltpu.VMEM((SEQ, D_MODEL), F32), pltpu.VMEM((2, SUBLANES, FFN_BLK), F32)],
        compiler_params=_params(("arbitrary", "arbitrary")),
    )(u2, w_up, fcw, fcb, w_down, h1, tgt)


def _ffn_fwd_token_major(x, ycn, yan, w_out, g2, w_up, fcw, fcb, w_down, tgt):
    tm = 512
    n_t = SEQ // tm

    def body(x_ref, ycn_ref, yan_ref, wo_ref, g2_ref, wu_ref, cw_ref, b_ref, wd_ref, tgt_ref,
             h1_ref, u2_ref, up_ref, pre_ref, act_ref, dh2_ref, dh2b_ref, loss_ref, acc_ref, halo_ref):
        i, j = pl.program_id(0), pl.program_id(1)

        @pl.when((i == 0) & (j == 0))
        def _():
            loss_ref[...] = jnp.zeros_like(loss_ref)

        @pl.when(j == 0)
        def _():
            h1 = x_ref[...] + _dot(ycn_ref[...], wo_ref[0:CONV_WIDTH, :]) + _dot(yan_ref[...], wo_ref[CONV_WIDTH:, :])
            h1_ref[...] = h1
            u2_ref[...] = (h1 * _rstd(h1) * g2_ref[...]).astype(BF16)
            acc_ref[...] = jnp.zeros_like(acc_ref)

        u2 = u2_ref[...]
        pre = []
        for s in range(2):
            up = _dot_nt(u2, wu_ref[s])
            up_ref[s] = up.astype(BF16)
            halo = jnp.where(i == 0, 0.0, halo_ref[s, j])
            pre.append(_conv3(up, _taps(cw_ref.at[s]), halo)[0] + b_ref[s])
            pre_ref[s] = pre[s].astype(BF16)
            halo_ref[s, j] = up[tm - SUBLANES:]
        g, val = pre
        act = (g * jax.nn.sigmoid(g) * val).astype(BF16)
        act_ref[...] = act
        acc_ref[...] += _dot(act, wd_ref[...])

        @pl.when(j == N_FFN_BLK - 1)
        def _():
            err = h1_ref[...] + acc_ref[...] - tgt_ref[...]
            loss_ref[...] += 0.5 * jnp.sum(err * err) / D_MODEL
            dh2 = err / D_MODEL
            dh2_ref[...] = dh2
            dh2b_ref[...] = dh2.astype(BF16)

    rows = lambda w: pl.BlockSpec((tm, w), lambda i, j: (i, 0))
    const = lambda shape: pl.BlockSpec(shape, lambda i, j: (0,) * len(shape))
    pair = lambda *s: pl.BlockSpec((2, None) + s, lambda i, j: (0, j) + (0,) * len(s))
    upb = pl.BlockSpec((2, None, tm, FFN_BLK), lambda i, j: (0, j, i, 0))
    return pl.pallas_call(
        body, name="ffn_fwd", grid=(n_t, N_FFN_BLK),
        in_specs=[rows(D_MODEL), rows(CONV_WIDTH), rows(ATTN_WIDTH), const((D_MODEL, D_MODEL)), const((1, D_MODEL)),
                  pair(FFN_BLK, D_MODEL), pair(3, 1, FFN_BLK), pair(1, FFN_BLK),
                  pl.BlockSpec((None, FFN_BLK, D_MODEL), lambda i, j: (j, 0, 0)), rows(D_MODEL)],
        out_specs=[rows(D_MODEL), rows(D_MODEL), upb, upb, pl.BlockSpec((None, tm, FFN_BLK), lambda i, j: (j, i, 0)),
                   rows(D_MODEL), rows(D_MODEL), const((SUBLANES, 128))],
        out_shape=[jax.ShapeDtypeStruct((SEQ, D_MODEL), F32), jax.ShapeDtypeStruct((SEQ, D_MODEL), BF16),
                   jax.ShapeDtypeStruct((2, N_FFN_BLK, SEQ, FFN_BLK), BF16),
                   jax.ShapeDtypeStruct((2, N_FFN_BLK, SEQ, FFN_BLK), BF16),
                   jax.ShapeDtypeStruct((N_FFN_BLK, SEQ, FFN_BLK), BF16),
                   jax.ShapeDtypeStruct((SEQ, D_MODEL), F32), jax.ShapeDtypeStruct((SEQ, D_MODEL), BF16),
                   jax.ShapeDtypeStruct((SUBLANES, 128), F32)],
        scratch_shapes=[pltpu.VMEM((tm, D_MODEL), F32), pltpu.VMEM((2, N_FFN_BLK, SUBLANES, FFN_BLK), F32)],
        compiler_params=_params(("arbitrary", "arbitrary")),
    )(x, ycn, yan, w_out, g2, w_up, fcw, fcb, w_down, tgt)


def _ffn_bwd(dh2, dh2b, h1, g2, up, pre, w_up, fcw, w_down):
    tm = 256
    n_t = SEQ // tm
    last = N_FFN_BLK - 1

    def body(dh2b_ref, up_ref, pre_ref, wu_ref, cw_ref, wd_ref, dh2_ref, h1_ref, g2_ref,
             dup_ref, dh1_ref, dh1b_ref, dfb_ref, dfcw_ref, dg2_ref, acc_ref, next_ref):
        j, i = pl.program_id(0), pl.program_id(1)
        rows = pl.ds(pl.multiple_of((n_t - 1 - i) * tm, tm), tm)

        @pl.when((i == 0) & (j == 0))
        def _():
            dfb_ref[...] = jnp.zeros_like(dfb_ref)
            dfcw_ref[...] = jnp.zeros_like(dfcw_ref)
            dg2_ref[...] = jnp.zeros_like(dg2_ref)

        g, val = pre_ref[0].astype(F32), pre_ref[1].astype(F32)
        sg = jax.nn.sigmoid(g)
        silu = g * sg
        dact = _dot_nt(dh2b_ref[...], wd_ref[...])
        dpre = (dact * val * (sg * (1.0 + g * (1.0 - sg))), dact * silu)
        du = None
        for s in range(2):
            d = dpre[s]
            u = up_ref[s].astype(F32)
            w = _taps(cw_ref.at[s])
            nxt = jnp.where(i == 0, 0.0, next_ref[s])
            d1 = _shift_up(d, 1, nxt)
            d2 = _shift_up(d, 2, nxt)
            next_ref[s] = d[:SUBLANES]
            dfb_ref[s, j] += jnp.sum(d, axis=0, keepdims=True)
            dfcw_ref[s, j, 0] += jnp.sum(d2 * u, axis=0, keepdims=True)
            dfcw_ref[s, j, 1] += jnp.sum(d1 * u, axis=0, keepdims=True)
            dfcw_ref[s, j, 2] += jnp.sum(d * u, axis=0, keepdims=True)
            dup = (d * w[2] + d1 * w[1] + d2 * w[0]).astype(BF16)
            dup_ref[s] = dup
            part = _dot(dup, wu_ref[s])
            du = part if du is None else du + part

        @pl.when(j == 0)
        def _():
            acc_ref[rows, :] = du

        @pl.when(j > 0)
        def _():
            acc_ref[rows, :] += du

        @pl.when(j == last)
        def _():
            dn, dgain = _rms_bwd(h1_ref[...], g2_ref[...], acc_ref[rows, :])
            dh1 = dh2_ref[...] + dn
            dh1_ref[...] = dh1
            dh1b_ref[...] = dh1.astype(BF16)
            dg2_ref[...] += dgain

    rev = lambda i: n_t - 1 - i
    const = lambda shape: pl.BlockSpec(shape, lambda j, i: (0,) * len(shape))
    late = lambda w: pl.BlockSpec((tm, w), lambda j, i: (jnp.where(j == last, rev(i), n_t - 1), 0))
    pair = lambda *s: pl.BlockSpec((2, None) + s, lambda j, i: (0, j) + (0,) * len(s))
    upb = pl.BlockSpec((2, None, tm, FFN_BLK), lambda j, i: (0, j, rev(i), 0))
    return pl.pallas_call(
        body, name="ffn_bwd", grid=(N_FFN_BLK, n_t),
        in_specs=[pl.BlockSpec((tm, D_MODEL), lambda j, i: (rev(i), 0)), upb, upb, pair(FFN_BLK, D_MODEL),
                  pair(3, 1, FFN_BLK), pl.BlockSpec((None, FFN_BLK, D_MODEL), lambda j, i: (j, 0, 0)),
                  late(D_MODEL), late(D_MODEL), const((1, D_MODEL))],
        out_specs=[upb, late(D_MODEL), late(D_MODEL),
                   const((2, N_FFN_BLK, 1, FFN_BLK)), const((2, N_FFN_BLK, 3, 1, FFN_BLK)), const((1, D_MODEL))],
        out_shape=[jax.ShapeDtypeStruct((2, N_FFN_BLK, SEQ, FFN_BLK), BF16), jax.ShapeDtypeStruct((SEQ, D_MODEL), F32),
                   jax.ShapeDtypeStruct((SEQ, D_MODEL), BF16), jax.ShapeDtypeStruct((2, N_FFN_BLK, 1, FFN_BLK), F32),
                   jax.ShapeDtypeStruct((2, N_FFN_BLK, 3, 1, FFN_BLK), F32), jax.ShapeDtypeStruct((1, D_MODEL), F32)],
        scratch_shapes=[pltpu.VMEM((SEQ, D_MODEL), F32), pltpu.VMEM((2, SUBLANES, FFN_BLK), F32)],
        compiler_params=_params(("arbitrary", "arbitrary")),
    )(dh2b, up, pre, w_up, fcw, w_down, dh2, h1, g2)


def _ffn_bwd_token_major(dh2, dh2b, h1, g2, up, pre, w_up, fcw, w_down, after=None):
    tm = 512
    n_t = SEQ // tm

    def body(dh2_ref, dh2b_ref, h1_ref, g2_ref, up_ref, pre_ref, wu_ref, cw_ref, wd_ref,
             dup_ref, dh1_ref, dh1b_ref, dfb_ref, dfcw_ref, dg2_ref, acc_ref, next_ref):
        i, j = pl.program_id(0), pl.program_id(1)

        @pl.when((i == 0) & (j == 0))
        def _():
            dfb_ref[...] = jnp.zeros_like(dfb_ref)
            dfcw_ref[...] = jnp.zeros_like(dfcw_ref)
            dg2_ref[...] = jnp.zeros_like(dg2_ref)

        @pl.when(j == 0)
        def _():
            acc_ref[...] = jnp.zeros_like(acc_ref)

        g, val = pre_ref[0].astype(F32), pre_ref[1].astype(F32)
        sg = jax.nn.sigmoid(g)
        silu = g * sg
        dact = _dot_nt(dh2b_ref[...], wd_ref[...])
        dpre = (dact * val * (sg * (1.0 + g * (1.0 - sg))), dact * silu)
        for s in range(2):
            d = dpre[s]
            u = up_ref[s].astype(F32)
            w = _taps(cw_ref.at[s])
            nxt = jnp.where(i == 0, 0.0, next_ref[s, j])
            d1 = _shift_up(d, 1, nxt)
            d2 = _shift_up(d, 2, nxt)
            next_ref[s, j] = d[:SUBLANES]
            dfb_ref[s, j] += jnp.sum(d, axis=0, keepdims=True)
            dfcw_ref[s, j, 0] += jnp.sum(d2 * u, axis=0, keepdims=True)
            dfcw_ref[s, j, 1] += jnp.sum(d1 * u, axis=0, keepdims=True)
            dfcw_ref[s, j, 2] += jnp.sum(d * u, axis=0, keepdims=True)
            dup = (d * w[2] + d1 * w[1] + d2 * w[0]).astype(BF16)
            dup_ref[s] = dup
            acc_ref[...] += _dot(dup, wu_ref[s])

        @pl.when(j == N_FFN_BLK - 1)
        def _():
            dn, dgain = _rms_bwd(h1_ref[...], g2_ref[...], acc_ref[...])
            dh1 = dh2_ref[...] + dn
            dh1_ref[...] = dh1
            dh1b_ref[...] = dh1.astype(BF16)
            dg2_ref[...] += dgain

    rev = lambda i: n_t - 1 - i
    rows = lambda w: pl.BlockSpec((tm, w), lambda i, j: (rev(i), 0))
    const = lambda shape: pl.BlockSpec(shape, lambda i, j: (0,) * len(shape))
    pair = lambda *s: pl.BlockSpec((2, None) + s, lambda i, j: (0, j) + (0,) * len(s))
    upb = pl.BlockSpec((2, None, tm, FFN_BLK), lambda i, j: (0, j, rev(i), 0))
    body, more_specs, more = _ordered_behind(body, 9, after)
    return pl.pallas_call(
        body, name="ffn_bwd", grid=(n_t, N_FFN_BLK),
        in_specs=[rows(D_MODEL), rows(D_MODEL), rows(D_MODEL), const((1, D_MODEL)), upb, upb,
                  pair(FFN_BLK, D_MODEL), pair(3, 1, FFN_BLK),
                  pl.BlockSpec((None, FFN_BLK, D_MODEL), lambda i, j: (j, 0, 0))] + more_specs,
        out_specs=[upb, rows(D_MODEL), rows(D_MODEL),
                   const((2, N_FFN_BLK, 1, FFN_BLK)), const((2, N_FFN_BLK, 3, 1, FFN_BLK)), const((1, D_MODEL))],
        out_shape=[jax.ShapeDtypeStruct((2, N_FFN_BLK, SEQ, FFN_BLK), BF16), jax.ShapeDtypeStruct((SEQ, D_MODEL), F32),
                   jax.ShapeDtypeStruct((SEQ, D_MODEL), BF16), jax.ShapeDtypeStruct((2, N_FFN_BLK, 1, FFN_BLK), F32),
                   jax.ShapeDtypeStruct((2, N_FFN_BLK, 3, 1, FFN_BLK), F32), jax.ShapeDtypeStruct((1, D_MODEL), F32)],
        scratch_shapes=[pltpu.VMEM((tm, D_MODEL), F32), pltpu.VMEM((2, N_FFN_BLK, SUBLANES, FFN_BLK), F32)],
        compiler_params=_params(("arbitrary", "arbitrary")),
    )(dh2, dh2b, h1, g2, up, pre, w_up, fcw, w_down, *more)


def _grad_tn(a_list, b, out_rows, name, after=None):
    n = len(a_list)
    ncol = b.shape[1]

    def body(*refs):
        a_refs, b_ref, o_ref = refs[:n], refs[n], refs[n + 1]
        j = pl.program_id(0)
        for k in range(n):
            @pl.when(j == k)
            def _(k=k):
                o_ref[...] = _dot_tn(a_refs[k][...], b_ref[...]).astype(BF16)

    full = lambda shape: pl.BlockSpec(shape, lambda j: (0,) * len(shape))
    body, more_specs, more = _ordered_behind(body, n + 1, after)
    return pl.pallas_call(
        body, name=name, grid=(n,),
        in_specs=[full((SEQ, out_rows))] * n + [full((SEQ, ncol))] + more_specs,
        out_specs=pl.BlockSpec((None, out_rows, ncol), lambda j: (j, 0, 0)),
        out_shape=jax.ShapeDtypeStruct((n, out_rows, ncol), BF16),
        compiler_params=_params(("arbitrary",)),
    )(*a_list, b, *more)


def _grad_tn_blocked(a, b, name, a_is_blocked):
    nb = a.shape[0] if a_is_blocked else b.shape[0]
    a_w, b_w = a.shape[-1], b.shape[-1]

    def body(a_ref, b_ref, o_ref):
        o_ref[...] = _dot_tn(a_ref[...], b_ref[...]).astype(BF16)

    blocked = lambda w: pl.BlockSpec((None, SEQ, w), lambda k: (k, 0, 0))
    full = lambda w: pl.BlockSpec((SEQ, w), lambda k: (0, 0))
    return pl.pallas_call(
        body, name=name, grid=(nb,),
        in_specs=[blocked(a_w) if a_is_blocked else full(a_w), full(b_w) if a_is_blocked else blocked(b_w)],
        out_specs=pl.BlockSpec((None, a_w, b_w), lambda k: (k, 0, 0)),
        out_shape=jax.ShapeDtypeStruct((nb, a_w, b_w), BF16),
        compiler_params=_params(("arbitrary",)),
    )(a, b)


def _out_bwd(dh1b, w_out, y_attn, gattn, after=None):
    tm = 512
    n_t = SEQ // tm

    def body(dh_ref, wo_ref, y_ref, ga_ref, dycn_ref, dy_ref, dga_ref):
        @pl.when(pl.program_id(0) == 0)
        def _():
            dga_ref[...] = jnp.zeros_like(dga_ref)

        dycat = _dot_nt(dh_ref[...], wo_ref[...])
        dycn_ref[...] = dycat[:, :CONV_WIDTH]
        dy, dga = _rms_bwd(y_ref[...], ga_ref[...], dycat[:, CONV_WIDTH:])
        dy_ref[...] = dy
        dga_ref[...] += dga

    rows = lambda w: pl.BlockSpec((tm, w), lambda i: (i, 0))
    const = lambda shape: pl.BlockSpec(shape, lambda i: (0,) * len(shape))
    body, more_specs, more = _ordered_behind(body, 4, after)
    return pl.pallas_call(
        body, name="out_bwd", grid=(n_t,),
        in_specs=[rows(D_MODEL), const((D_MODEL, D_MODEL)), rows(ATTN_WIDTH), const((1, ATTN_WIDTH))] + more_specs,
        out_specs=[rows(CONV_WIDTH), rows(ATTN_WIDTH), const((1, ATTN_WIDTH))],
        out_shape=[jax.ShapeDtypeStruct((SEQ, CONV_WIDTH), F32), jax.ShapeDtypeStruct((SEQ, ATTN_WIDTH), F32),
                   jax.ShapeDtypeStruct((1, ATTN_WIDTH), F32)],
        compiler_params=_params(("arbitrary",)),
    )(dh1b, w_out, y_attn, gattn, *more)


def _attn_bwd(qn, kn, v, dy, probs, psinks, bkt, after=None):
    n_b = SEQ // BLK

    def body(q_ref, k_ref, v_ref, dy_ref, p_ref, ps_ref, bkt_ref,
             dq_ref, dk_ref, dv_ref, dtbl_ref, dsink_ref, dbias_ref, dsacc_ref):
        i = pl.program_id(0)

        @pl.when(i == 0)
        def _():
            dbias_ref[...] = jnp.zeros_like(dbias_ref)
            dsacc_ref[...] = jnp.zeros_like(dsacc_ref)
            dk_ref[...] = jnp.zeros_like(dk_ref)
            dv_ref[...] = jnp.zeros_like(dv_ref)

        kb, prev, cur = _band_rows(k_ref, i)
        vb, _, _ = _band_rows(v_ref, i)
        upper, _ = _band_masks(i)
        q = q_ref[...]
        dy = dy_ref[...]
        psink = ps_ref[...]
        lane = lax.broadcasted_iota(jnp.int32, (BLK, 128), 1)
        dsink = jnp.zeros((BLK, 128), F32)
        dqs, dks, dvs = [], [], []
        for g in range(N_HEADS // GQA_GROUP):
            kv = slice(HEAD_DIM * g, HEAD_DIM * (g + 1))
            qg = _stack_heads(q, g)
            dog = _stack_heads(dy, g).astype(BF16)
            pb = p_ref[g]
            pg = pb.astype(F32)
            dprobs = _fold(_dot_nt(dog, vb[:, kv]), upper)
            dvs.append(_dot_tn(_unfold(pb, upper), dog))
            dsum = jnp.sum(pg * dprobs, axis=-1, keepdims=True)
            dlogits = pg * (dprobs - dsum)
            for t in range(GQA_GROUP):
                dsink = jnp.where(lane == GQA_GROUP * g + t, -psink * dsum[BLK * t:BLK * (t + 1)], dsink)
            dbias_ref[g] += dlogits
            ds = _unfold(dlogits * (HEAD_DIM ** -0.5), upper).astype(BF16)
            dqs.append(_dot(ds, kb[:, kv]))
            dks.append(_dot_tn(ds, qg))
        dsacc_ref[...] += dsink
        dq_ref[...] = _unstack_heads(dqs)
        dkb = jnp.concatenate(dks, axis=-1)
        dvb = jnp.concatenate(dvs, axis=-1)
        dk_ref[pl.ds(prev, BLK), :] += dkb[:BLK]
        dk_ref[pl.ds(cur, BLK), :] += dkb[BLK:]
        dv_ref[pl.ds(prev, BLK), :] += dvb[:BLK]
        dv_ref[pl.ds(cur, BLK), :] += dvb[BLK:]

        @pl.when(i == n_b - 1)
        def _():
            bkt = bkt_ref[...]
            row8 = lax.broadcasted_iota(jnp.int32, (N_HEADS, 128), 0)
            lane8 = lax.broadcasted_iota(jnp.int32, (N_HEADS, 128), 1)
            acc = jnp.zeros((N_HEADS, 128), F32)
            for h in range(N_HEADS):
                rows = slice(BLK * (h % GQA_GROUP), BLK * (h % GQA_GROUP + 1))
                dbh = dbias_ref[h // GQA_GROUP, rows, :]
                for b in range(NUM_BUCKETS):
                    acc = jnp.where((row8 == h) & (lane8 == b), jnp.sum(jnp.where(bkt == b, dbh, 0.0)), acc)
            dsink_ref[...] = jnp.sum(dsacc_ref[...], axis=0, keepdims=True)
            dtbl_ref[...] = acc

    const = lambda shape: pl.BlockSpec(shape, lambda i: (0,) * len(shape))
    rows = lambda w: pl.BlockSpec((BLK, w), lambda i: (i, 0))
    n_g = N_HEADS // GQA_GROUP
    body, more_specs, more = _ordered_behind(body, 7, after)
    return pl.pallas_call(
        body, name="attn_bwd", grid=(n_b,),
        in_specs=[rows(ATTN_WIDTH), const((SEQ, KV_WIDTH)), const((SEQ, KV_WIDTH)), rows(ATTN_WIDTH),
                  pl.BlockSpec((None, n_g, GROUP_ROWS, BLK), lambda i: (i, 0, 0, 0)), rows(128),
                  const((BLK, BLK))] + more_specs,
        out_specs=[rows(ATTN_WIDTH), const((SEQ, KV_WIDTH)), const((SEQ, KV_WIDTH)), const((N_HEADS, 128)), const((1, 128))],
        out_shape=[jax.ShapeDtypeStruct((SEQ, ATTN_WIDTH), F32), jax.ShapeDtypeStruct((SEQ, KV_WIDTH), F32),
                   jax.ShapeDtypeStruct((SEQ, KV_WIDTH), F32), jax.ShapeDtypeStruct((N_HEADS, 128), F32),
                   jax.ShapeDtypeStruct((1, 128), F32)],
        scratch_shapes=[pltpu.VMEM((n_g, GROUP_ROWS, BLK), F32), pltpu.VMEM((BLK, 128), F32)],
        compiler_params=_params(("arbitrary",)),
    )(qn, kn, v, dy, probs, psinks, bkt, *more)


def _mix_in_bwd(x, dh1, proj, dycn, dqn, dkn, dv, w_in_t, conv_w, g1, gq, gk, gconv):
    tm = 256
    n_t = SEQ // tm
    halo_blocks = tm // SUBLANES

    def body(x_ref, dh1_ref, proj_ref, halo_ref, dycn_ref, dqn_ref, dkn_ref, dv_ref, w_ref, cw_ref,
             g1_ref, gq_ref, gk_ref, gc_ref,
             dx_ref, dproj_ref, dcw_ref, dgc_ref, dgq_ref, dgk_ref, dg1_ref, next_ref):
        i = pl.program_id(0)
        first_tile = i == n_t - 1

        @pl.when(i == 0)
        def _():
            for r in (dcw_ref, dgc_ref, dgq_ref, dgk_ref, dg1_ref, next_ref):
                r[...] = jnp.zeros_like(r)

        proj = proj_ref[...]
        hp = halo_ref[...]
        gate_b = proj[:, 0:CONV_WIDTH]
        gate_c = proj[:, CONV_WIDTH:2 * CONV_WIDTH]
        hc = proj[:, 2 * CONV_WIDTH:3 * CONV_WIDTH]
        a = gate_c * hc
        a_halo = jnp.where(first_tile, 0.0, hp[:, CONV_WIDTH:2 * CONV_WIDTH] * hp[:, 2 * CONV_WIDTH:3 * CONV_WIDTH])
        cw = _taps(cw_ref[...])
        cv, a2, a1 = _conv3(a, cw, a_halo)
        dyc, dgc = _rms_bwd(gate_b * cv, gc_ref[...], dycn_ref[...])
        dgc_ref[...] += dgc
        dcv = dyc * gate_b
        dcw_ref[...] += jnp.concatenate(
            [jnp.sum(dcv * a2, axis=0, keepdims=True), jnp.sum(dcv * a1, axis=0, keepdims=True),
             jnp.sum(dcv * a, axis=0, keepdims=True)], axis=0)
        da = _conv3_bwd_input(dcv, cw, next_ref[...])
        next_ref[...] = dcv[:SUBLANES]
        q0 = 3 * CONV_WIDTH
        k0 = q0 + ATTN_WIDTH
        dq, dgq = _head_norm_bwd(proj[:, q0:k0], gq_ref[...], dqn_ref[...], N_HEADS)
        dk, dgk = _head_norm_bwd(proj[:, k0:k0 + KV_WIDTH], gk_ref[...], dkn_ref[...], 2)
        dgq_ref[...] += dgq
        dgk_ref[...] += dgk
        dproj = jnp.concatenate([dyc * cv, da * hc, da * gate_c, dq, dk, dv_ref[...]], axis=-1).astype(BF16)
        dproj_ref[...] = dproj
        du1 = _dot(dproj, w_ref[...])
        xv = x_ref[...]
        dn, dg1 = _rms_bwd(xv, g1_ref[...], du1)
        dx_ref[...] = dh1_ref[...] + dn
        dg1_ref[...] += dg1

    rev = lambda i: n_t - 1 - i
    rows = lambda w: pl.BlockSpec((tm, w), lambda i: (rev(i), 0))
    const = lambda shape: pl.BlockSpec(shape, lambda i: (0,) * len(shape))
    halo = pl.BlockSpec((SUBLANES, IN_WIDTH), lambda i: (jnp.maximum(rev(i) * halo_blocks - 1, 0), 0))
    return pl.pallas_call(
        body, name="mix_in_bwd", grid=(n_t,),
        in_specs=[rows(D_MODEL), rows(D_MODEL), rows(IN_WIDTH), halo, rows(CONV_WIDTH), rows(ATTN_WIDTH), rows(KV_WIDTH),
                  rows(KV_WIDTH), const((IN_WIDTH, D_MODEL)), const((3, CONV_WIDTH)), const((1, D_MODEL)),
                  const((1, HEAD_DIM)), const((1, HEAD_DIM)), const((1, CONV_WIDTH))],
        out_specs=[rows(D_MODEL), rows(IN_WIDTH), const((3, CONV_WIDTH)), const((1, CONV_WIDTH)),
                   const((1, HEAD_DIM)), const((1, HEAD_DIM)), const((1, D_MODEL))],
        out_shape=[jax.ShapeDtypeStruct((SEQ, D_MODEL), F32), jax.ShapeDtypeStruct((SEQ, IN_WIDTH), BF16),
                   jax.ShapeDtypeStruct((3, CONV_WIDTH), F32),
                   jax.ShapeDtypeStruct((1, CONV_WIDTH), F32), jax.ShapeDtypeStruct((1, HEAD_DIM), F32),
                   jax.ShapeDtypeStruct((1, HEAD_DIM), F32), jax.ShapeDtypeStruct((1, D_MODEL), F32)],
        scratch_shapes=[pltpu.VMEM((SUBLANES, CONV_WIDTH), F32)],
        compiler_params=_params(("arbitrary",)),
    )(x, dh1, proj, proj, dycn, dqn, dkn, dv, w_in_t, conv_w, g1, gq, gk, gconv)


def _grad_w_in(dproj, u1, after=None):
    bw = 768

    def body(a_ref, b_ref, o_ref):
        o_ref[...] = _dot_tn(a_ref[...], b_ref[...]).astype(BF16)

    body, more_specs, more = _ordered_behind(body, 2, after)
    return pl.pallas_call(
        body, name="grad_w_in", grid=(IN_WIDTH // bw,),
        in_specs=[pl.BlockSpec((SEQ, bw), lambda k: (0, k)), pl.BlockSpec((SEQ, D_MODEL), lambda k: (0, 0))] + more_specs,
        out_specs=pl.BlockSpec((bw, D_MODEL), lambda k: (k, 0)),
        out_shape=jax.ShapeDtypeStruct((IN_WIDTH, D_MODEL), BF16),
        compiler_params=_params(("arbitrary",)),
    )(dproj, u1, *more)


def _adamw_math(w, g, m, v):
    m = ADAM_B1 * m + (1.0 - ADAM_B1) * g
    v = ADAM_B2 * v + (1.0 - ADAM_B2) * (g * g)
    m_hat = m / (1.0 - ADAM_B1 ** ADAM_STEP)
    v_hat = v / (1.0 - ADAM_B2 ** ADAM_STEP)
    return -ADAM_LR * (m_hat / (jnp.sqrt(v_hat) + ADAM_EPS) + ADAM_WD * w), m, v


_ROW_G1, _ROW_G2, _ROW_OUT_NORMS, _ROW_FFN_B, _ROW_GQ, _ROW_GK, _ROW_SINKS, _ROW_LOSS, _ROW_TABLE = 0, 1, 2, 3, 11, 12, 13, 14, 16
SMALL_ROWS, SMALL_COLS = 24, 1024
_SMALL_NAMES = ("norm_mix_g", "norm_ffn_g", "out_norm_conv_g", "out_norm_attn_g", "ffn_conv_b", "q_norm_g", "k_norm_g",
                "sinks", "rel_bias_table")


def _pack_small_grads(dg1, dg2, dgconv, dgattn, dfb, dgq, dgk, dsinks, dtbl_t, loss_acc):
    def body(dg1_ref, dg2_ref, dgc_ref, dga_ref, dfb_ref, dgq_ref, dgk_ref, ds_ref, dt_ref, loss_ref, o_ref):
        o_ref[...] = jnp.zeros_like(o_ref)
        o_ref[_ROW_G1:_ROW_G1 + 1, :] = dg1_ref[...]
        o_ref[_ROW_G2:_ROW_G2 + 1, :] = dg2_ref[...]
        o_ref[_ROW_OUT_NORMS:_ROW_OUT_NORMS + 1, 0:CONV_WIDTH] = dgc_ref[...]
        o_ref[_ROW_OUT_NORMS:_ROW_OUT_NORMS + 1, CONV_WIDTH:] = dga_ref[...]
        for k in range(N_DEV):
            o_ref[_ROW_FFN_B + k:_ROW_FFN_B + k + 1, 0:FFN_BLK] = dfb_ref[k // N_FFN_BLK, k % N_FFN_BLK]
        o_ref[_ROW_GQ:_ROW_GQ + 1, 0:HEAD_DIM] = dgq_ref[...]
        o_ref[_ROW_GK:_ROW_GK + 1, 0:HEAD_DIM] = dgk_ref[...]
        o_ref[_ROW_SINKS:_ROW_SINKS + 1, 0:128] = ds_ref[...]
        o_ref[_ROW_LOSS:_ROW_LOSS + 1, 0:128] = loss_ref[0:1, :]
        o_ref[_ROW_TABLE:_ROW_TABLE + N_HEADS, 0:128] = dt_ref[...]

    return pl.pallas_call(body, name="pack_small_grads", out_shape=jax.ShapeDtypeStruct((SMALL_ROWS, SMALL_COLS), F32))(
        dg1, dg2, dgconv, dgattn, dfb, dgq, dgk, dsinks, dtbl_t, loss_acc)


def _adamw_small(recv, params, after):
    names = _SMALL_NAMES
    n = len(names)

    def grad_of(g, name, k=None):
        if name == "norm_mix_g":
            return g[_ROW_G1:_ROW_G1 + 1, :]
        if name == "norm_ffn_g":
            return g[_ROW_G2:_ROW_G2 + 1, :]
        if name == "out_norm_conv_g":
            return g[_ROW_OUT_NORMS:_ROW_OUT_NORMS + 1, 0:CONV_WIDTH]
        if name == "out_norm_attn_g":
            return g[_ROW_OUT_NORMS:_ROW_OUT_NORMS + 1, CONV_WIDTH:]
        if name == "ffn_conv_b":
            return g[_ROW_FFN_B + k:_ROW_FFN_B + k + 1, 0:FFN_BLK]
        if name == "q_norm_g":
            return g[_ROW_GQ:_ROW_GQ + 1, 0:HEAD_DIM]
        if name == "k_norm_g":
            return g[_ROW_GK:_ROW_GK + 1, 0:HEAD_DIM]
        if name == "sinks":
            return g[_ROW_SINKS:_ROW_SINKS + 1, 0:N_HEADS]
        return g[_ROW_TABLE:_ROW_TABLE + N_HEADS, 0:NUM_BUCKETS]

    def body(r_ref, *refs):
        ins, outs, loss_ref = refs[:3 * n], refs[3 * n:7 * n], refs[7 * n]
        g = r_ref[0]
        for s in range(1, N_DEV):
            g = g + r_ref[s]
        loss_ref[...] = g[_ROW_LOSS:_ROW_LOSS + 1, 0:128]
        for i, name in enumerate(names):
            w_ref, m_ref, v_ref = ins[3 * i:3 * i + 3]
            o = outs[4 * i:4 * i + 4]
            cols = [slice(FFN_BLK * k, FFN_BLK * (k + 1)) for k in range(N_DEV)] if name == "ffn_conv_b" else [slice(None)]
            for k, cs in enumerate(cols):
                gk = grad_of(g, name, k)
                d, m2, v2 = _adamw_math(w_ref[:, cs], gk, m_ref[:, cs], v_ref[:, cs])
                o[0][:, cs], o[1][:, cs], o[2][:, cs], o[3][:, cs] = gk, d, m2, v2

    flat = [a for name in names for a in params[name]]
    body, more_specs, more = _ordered_behind(body, 1 + 3 * n, after)
    vmem = pl.BlockSpec(memory_space=pltpu.VMEM)
    out = pl.pallas_call(
        body, name="adamw_small",
        in_specs=[vmem] * (1 + 3 * n) + more_specs,
        out_shape=[jax.ShapeDtypeStruct(params[name][0].shape, F32) for name in names for _ in range(4)]
        + [jax.ShapeDtypeStruct((1, 128), F32)],
        compiler_params=pltpu.CompilerParams(vmem_limit_bytes=VMEM_LIMIT),
    )(recv, *flat, *more)
    return {name: tuple(out[4 * i:4 * i + 4]) for i, name in enumerate(names)}, out[4 * n]


def _adamw_direct(w, m, v, own, recv, me, name, row_blocks=1, after=None):
    rb = w.shape[0] // row_blocks
    cols = w.shape[1]

    def body(me_ref, w_ref, m_ref, v_ref, o_ref, r_ref, g_o, d_o, m_o, v_o):
        g = o_ref[...].astype(F32)
        for s in range(N_DEV - 1):
            g = g + r_ref[s].astype(F32)
        g_o[...] = g
        d_o[...], m_o[...], v_o[...] = _adamw_math(w_ref[...], g, m_ref[...], v_ref[...])

    blk = pl.BlockSpec((rb, cols), lambda i, me_ref: (i, 0))
    oblk = pl.BlockSpec((None, rb, cols), lambda i, me_ref: (me_ref[0], i, 0))
    rblk = pl.BlockSpec((N_DEV - 1, rb, cols), lambda i, me_ref: (0, i, 0))
    body, more_specs, more = _ordered_behind(body, 6, after)
    return pl.pallas_call(
        body, name=name,
        grid_spec=pltpu.PrefetchScalarGridSpec(num_scalar_prefetch=1, grid=(row_blocks,),
                                               in_specs=[blk, blk, blk, oblk, rblk] + more_specs, out_specs=[blk] * 4),
        out_shape=[jax.ShapeDtypeStruct(w.shape, F32)] * 4,
        compiler_params=_params(("arbitrary",)),
    )(me, w, m, v, own, recv, *more)


def _adamw(w, m, v, part, recv, chip, name, row_blocks=1, after=None):
    rb = w.shape[0] // row_blocks
    tail = w.shape[1:]
    zeros = (0,) * len(tail)

    def body(chip_ref, w_ref, m_ref, v_ref, p_ref, r_ref, g_o, d_o, m_o, v_o):
        g = p_ref[...].astype(F32)
        for s in range(3):
            g = g + r_ref[s].astype(F32)
        g_o[...] = g
        d_o[...], m_o[...], v_o[...] = _adamw_math(w_ref[...], g, m_ref[...], v_ref[...])

    blk = pl.BlockSpec((rb,) + tail, lambda i, chip_ref: (i,) + zeros)
    pblk = pl.BlockSpec((None, rb) + tail, lambda i, chip_ref: (chip_ref[0], i) + zeros)
    rblk = pl.BlockSpec((3, rb) + tail, lambda i, chip_ref: (0, i) + zeros)
    body, more_specs, more = _ordered_behind(body, 6, after)
    return pl.pallas_call(
        body, name=name,
        grid_spec=pltpu.PrefetchScalarGridSpec(num_scalar_prefetch=1, grid=(row_blocks,),
                                               in_specs=[blk, blk, blk, pblk, rblk] + more_specs, out_specs=[blk] * 4),
        out_shape=[jax.ShapeDtypeStruct(w.shape, F32)] * 4,
        compiler_params=_params(("arbitrary",)),
    )(chip, w, m, v, part, recv, *more)


def kernel(x, norm_mix_g, w_in, conv_w, q_norm_g, k_norm_g, rel_bias_table, sinks, out_norm_conv_g, out_norm_attn_g, w_out, norm_ffn_g, w_up, ffn_conv_w, ffn_conv_b, w_down, loss_target, m_norm_mix_g, m_w_in, m_conv_w, m_q_norm_g, m_k_norm_g, m_rel_bias_table, m_sinks, m_out_norm_conv_g, m_out_norm_attn_g, m_w_out, m_norm_ffn_g, m_w_up, m_ffn_conv_w, m_ffn_conv_b, m_w_down, v_norm_mix_g, v_w_in, v_conv_w, v_q_norm_g, v_k_norm_g, v_rel_bias_table, v_sinks, v_out_norm_conv_g, v_out_norm_attn_g, v_w_out, v_norm_ffn_g, v_w_up, v_ffn_conv_w, v_ffn_conv_b, v_w_down):
    p = dict(norm_mix_g=norm_mix_g, w_in=w_in, conv_w=conv_w, q_norm_g=q_norm_g, k_norm_g=k_norm_g,
             rel_bias_table=rel_bias_table, sinks=sinks, out_norm_conv_g=out_norm_conv_g, out_norm_attn_g=out_norm_attn_g,
             w_out=w_out, norm_ffn_g=norm_ffn_g, w_up=w_up, ffn_conv_w=ffn_conv_w, ffn_conv_b=ffn_conv_b, w_down=w_down)
    m = dict(norm_mix_g=m_norm_mix_g, w_in=m_w_in, conv_w=m_conv_w, q_norm_g=m_q_norm_g, k_norm_g=m_k_norm_g,
             rel_bias_table=m_rel_bias_table, sinks=m_sinks, out_norm_conv_g=m_out_norm_conv_g,
             out_norm_attn_g=m_out_norm_attn_g, w_out=m_w_out, norm_ffn_g=m_norm_ffn_g, w_up=m_w_up,
             ffn_conv_w=m_ffn_conv_w, ffn_conv_b=m_ffn_conv_b, w_down=m_w_down)
    v = dict(norm_mix_g=v_norm_mix_g, w_in=v_w_in, conv_w=v_conv_w, q_norm_g=v_q_norm_g, k_norm_g=v_k_norm_g,
             rel_bias_table=v_rel_bias_table, sinks=v_sinks, out_norm_conv_g=v_out_norm_conv_g,
             out_norm_attn_g=v_out_norm_attn_g, w_out=v_w_out, norm_ffn_g=v_norm_ffn_g, w_up=v_w_up,
             ffn_conv_w=v_ffn_conv_w, ffn_conv_b=v_ffn_conv_b, w_down=v_w_down)

    xs, tgt = x[0], loss_target[0]
    g1, g2, gq, gk, gconv, gattn = norm_mix_g, norm_ffn_g, q_norm_g, k_norm_g, out_norm_conv_g, out_norm_attn_g
    ix, iy, ic = _coords()
    core = ic.astype(jnp.int32).reshape(1)
    chip = (2 * ix + iy).astype(jnp.int32).reshape(1)
    me = _lin(ix, iy, ic).astype(jnp.int32).reshape(1)
    bkt = jnp.asarray(_bucket_map())
    tr = lambda a: a[0].T
    taps = lambda a: jnp.transpose(a, (1, 0, 2))
    tbl_t = rel_bias_table.T

    wi_l, cw_l, wo_l, wu_l, wd_l, fcw_l = _place_shards(
        me, [tr(w_in), taps(conv_w), w_out[0], tr(w_up), w_down[0], taps(ffn_conv_w)], [BF16, F32, BF16, BF16, BF16, F32])
    finish_a, token_a = _all_gather_split([wi_l, cw_l], "mixer", None)
    ffn_stage2, ffn_stage3, token_b = _all_gather_tree([wo_l, wu_l, wd_l, fcw_l], "ffn", token_a)
    wi_g, cw_g = finish_a(token_b)
    w_in_t = wi_g.reshape(IN_WIDTH, D_MODEL)
    conv_w_f = jnp.transpose(cw_g[:, :, 0, :], (1, 0, 2)).reshape(3, CONV_WIDTH)

    proj, u1, ycn, qn, kn, vv = _mix_in_fwd(xs, g1, w_in_t, conv_w_f, gq, gk, gconv)
    token_b2 = ffn_stage2(ycn)
    y_attn, yan, probs, psinks = _attn_fwd(qn, kn, vv, tbl_t, sinks, bkt, gattn, after=token_b2)
    wo_g, wu_g, wd_g, fcw_g = ffn_stage3(yan)
    w_out_f = wo_g.reshape(D_MODEL, D_MODEL)
    w_down_f = wd_g.reshape(N_FFN_BLK, FFN_BLK, D_MODEL)
    w_up_f = wu_g.reshape(2, N_FFN_BLK, FFN_BLK, D_MODEL)
    fcw_f = fcw_g.reshape(2, N_FFN_BLK, 3, 1, FFN_BLK)
    fcb = ffn_conv_b.reshape(2, N_FFN_BLK, 1, FFN_BLK)
    h1, u2, up, pre, act, dh2, dh2b, loss_acc = _ffn_fwd_token_major(
        xs, ycn, yan, w_out_f, g2, w_up_f, fcw_f, fcb, w_down_f, tgt)

    dw_down = _grad_tn_blocked(act, dh2b, "grad_w_down", a_is_blocked=True).reshape(N_DEV, D_FF // N_DEV, D_MODEL)
    plan_d, slots_d = _scatter_plan(1)
    d_sem = _split_start("scatter_w_down_start", [dw_down], [lax.empty((N_DEV - 1,) + dw_down.shape[1:], BF16)],
                         plan_d, None)
    dup, dh1, dh1b, dfb, dfcw, dg2 = _ffn_bwd_token_major(dh2, dh2b, h1, g2, up, pre, w_up_f, fcw_f, w_down_f,
                                                          after=d_sem[4])
    dw_up = _grad_tn_blocked(dup.reshape(N_DEV, SEQ, FFN_BLK), u2, "grad_w_up", a_is_blocked=True)
    dw_out = _grad_tn([ycn, yan], dh1b, CONV_WIDTH, "grad_w_out").reshape(N_DEV, D_MODEL // N_DEV, D_MODEL)
    out_bwd = {}

    def behind_ffn(token):
        out_bwd["r"] = _out_bwd(dh1b, w_out_f, y_attn, gattn, after=token)
        return out_bwd["r"][0]

    finish_ffn, token_ffn = _reduce_scatter_split(
        [dw_up, dw_out, dfcw.reshape(N_DEV, 3, 1, FFN_BLK)], "ffn", core, behind_ffn)
    dycn, dy_attn, dgattn = out_bwd["r"]
    dqn, dkn, dv, dtbl_t, dsinks = _attn_bwd(qn, kn, vv, dy_attn, probs, psinks, bkt, after=token_ffn)
    dx, dproj, dcw, dgconv, dgq, dgk, dg1 = _mix_in_bwd(xs, dh1, proj, dycn, dqn, dkn, dv, w_in_t, conv_w_f,
                                                             g1, gq, gk, gconv)
    packed = _pack_small_grads(dg1, dg2, dgconv, dgattn, dfb, dgq, dgk, dsinks, dtbl_t, loss_acc)
    plan_s, slots_s = _broadcast_plan()
    s_sem, r_sem, src_s, land_s, token_s = _split_start(
        "gather_small_start", [packed], [jnp.broadcast_to(packed[None], (N_DEV,) + packed.shape)], plan_s, None)
    dw_in_t = _grad_w_in(dproj, u1, after=token_s).reshape(N_DEV, IN_WIDTH // N_DEV, D_MODEL)
    dcw_b = jnp.transpose(dcw.reshape(3, N_DEV, 1, CONV_WIDTH // N_DEV), (1, 0, 2, 3))
    adam = {}
    ffn_got = {}

    def behind_mixer(token):
        ffn_got["r"] = finish_ffn(token)
        return ffn_got["r"][1][0]

    finish_mixer, token_mixer = _reduce_scatter_split([dw_in_t, dcw_b], "mixer", core, behind_mixer)
    (p_wu, p_wo, p_fcw), (r_wu, r_wo, r_fcw) = ffn_got["r"]
    (own_wd,), (r_wd,) = _split_wait("scatter_w_down_wait", d_sem[0], d_sem[1], d_sem[2], d_sem[3], plan_d, slots_d,
                                     token_mixer)
    adam["w_down"] = _adamw_direct(w_down[0], m_w_down[0], v_w_down[0], own_wd, r_wd, me, "adamw_w_down", row_blocks=2)
    adam_up = _adamw(tr(w_up), tr(m_w_up), tr(v_w_up), p_wu, r_wu, chip, "adamw_w_up", row_blocks=4,
                     after=adam["w_down"][0])
    adam["w_out"] = _adamw(w_out[0], m_w_out[0], v_w_out[0], p_wo, r_wo, chip, "adamw_w_out", after=adam_up[0])
    adam_fcw = _adamw(taps(ffn_conv_w), taps(m_ffn_conv_w), taps(v_ffn_conv_w), p_fcw, r_fcw, chip, "adamw_ffn_conv_w",
                      after=adam["w_out"][0])
    _, (r_small,) = _split_wait("gather_small_wait", s_sem, r_sem, src_s, land_s, plan_s, slots_s, adam_fcw[0])
    small_in = {k: (p[k], m[k], v[k]) for k in _SMALL_NAMES}
    small_in["rel_bias_table"] = (tbl_t, m_rel_bias_table.T, v_rel_bias_table.T)
    small_out, loss_row = _adamw_small(r_small, small_in, None)
    (p_wi, p_cw), (r_wi, r_cw) = finish_mixer(loss_row)
    adam_in = _adamw(tr(w_in), tr(m_w_in), tr(v_w_in), p_wi, r_wi, chip, "adamw_w_in")
    adam_cw = _adamw(taps(conv_w), taps(m_conv_w), taps(v_conv_w), p_cw, r_cw, chip, "adamw_conv_w")

    res = {k: tuple(a[None] for a in t) for k, t in adam.items()}
    res["w_up"] = tuple(a.T[None] for a in adam_up)
    res["w_in"] = tuple(a.T[None] for a in adam_in)
    res["ffn_conv_w"] = tuple(taps(a) for a in adam_fcw)
    res["conv_w"] = tuple(taps(a) for a in adam_cw)
    res.update(small_out)
    res["rel_bias_table"] = tuple(a.T for a in small_out["rel_bias_table"])
    loss = loss_row[0, 0]
    order = ("norm_mix_g", "w_in", "conv_w", "q_norm_g", "k_norm_g", "rel_bias_table", "sinks", "out_norm_conv_g",
             "out_norm_attn_g", "w_out", "norm_ffn_g", "w_up", "ffn_conv_w", "ffn_conv_b", "w_down")
    return (loss, dx[None], *[res[k][0] for k in order], *[res[k][1] for k in order],
            *[res[k][2] for k in order], *[res[k][3] for k in order])
```

```python
import functools
import math

import numpy as np
import jax
import jax.numpy as jnp
from jax import lax
from jax.experimental import pallas as pl
from jax.experimental.pallas import tpu as pltpu

F32 = jnp.float32
BF16 = jnp.bfloat16

SEQ = 2048
D_MODEL = 1024
CONV_WIDTH = 512
ATTN_WIDTH = 512
KV_WIDTH = 128
HEAD_DIM = 64
N_HEADS = 8
GQA_GROUP = 4
IN_WIDTH = 2304
D_FF = 2816
BLK = 128
NUM_BUCKETS = 32
EPS = 1e-6
NEG_INF = -1e30
ADAM_LR = 0.001
ADAM_B1 = 0.9
ADAM_B2 = 0.999
ADAM_EPS = 1e-08
ADAM_WD = 0.01
ADAM_STEP = 10

N_DEV = 8
FFN_BLK = 2 * D_FF // N_DEV
N_FFN_BLK = D_FF // FFN_BLK
SUBLANES = 8
VMEM_LIMIT = 56 * 1024 * 1024

_MESH = pl.DeviceIdType.MESH
_ANY = pl.BlockSpec(memory_space=pl.ANY)


def _params(sem):
    return pltpu.CompilerParams(dimension_semantics=sem, vmem_limit_bytes=VMEM_LIMIT)


def _ordered_behind(body, pos, after):
    if after is None:
        return body, [], []
    return (lambda *refs: body(*refs[:pos], *refs[pos + 1:])), [_ANY], [after]


def _dot(a, b):
    return jnp.dot(a, b, preferred_element_type=F32)


def _dot_nt(a, b):
    return lax.dot_general(a, b, (((1,), (1,)), ((), ())), preferred_element_type=F32)


def _dot_tn(a, b):
    return lax.dot_general(a, b, (((0,), (0,)), ((), ())), preferred_element_type=F32)


def _shift_down(x, s, halo):
    r = pltpu.roll(x, s, axis=0)
    hr = pltpu.roll(halo, s, axis=0)
    row = lax.broadcasted_iota(jnp.int32, halo.shape, 0)
    top = jnp.where(row < s, hr, r[:SUBLANES])
    return jnp.concatenate([top, r[SUBLANES:]], axis=0)


def _shift_up(x, s, halo):
    n = x.shape[0]
    r = pltpu.roll(x, n - s, axis=0)
    hr = pltpu.roll(halo, SUBLANES - s, axis=0)
    row = lax.broadcasted_iota(jnp.int32, halo.shape, 0)
    bot = jnp.where(row >= SUBLANES - s, hr, r[n - SUBLANES:])
    return jnp.concatenate([r[:n - SUBLANES], bot], axis=0)


def _taps(w):
    return (w[0], w[1], w[2]) if len(w.shape) == 3 else (w[0:1], w[1:2], w[2:3])


def _conv3(x, w, halo):
    x2 = _shift_down(x, 2, halo)
    x1 = _shift_down(x, 1, halo)
    return x2 * w[0] + x1 * w[1] + x * w[2], x2, x1


def _conv3_bwd_input(dy, w, halo_next):
    return dy * w[2] + _shift_up(dy, 1, halo_next) * w[1] + _shift_up(dy, 2, halo_next) * w[0]


def _rstd(x):
    return lax.rsqrt(jnp.mean(x * x, axis=-1, keepdims=True) + EPS)


def _rms_bwd(x, g, dy):
    r = _rstd(x)
    n = x * r
    dn = dy * g
    dx = r * (dn - n * jnp.mean(dn * n, axis=-1, keepdims=True))
    return dx, jnp.sum(dy * n, axis=0, keepdims=True)


def _head_mean(x):
    width = x.shape[-1]
    ri = lax.broadcasted_iota(jnp.int32, (width, width), 0) // HEAD_DIM
    ci = lax.broadcasted_iota(jnp.int32, (width, width), 1) // HEAD_DIM
    ones = jnp.where(ri == ci, 1.0, 0.0).astype(BF16)
    hi = x.astype(BF16)
    lo = (x - hi.astype(F32)).astype(BF16)
    return (_dot(hi, ones) + _dot(lo, ones)) * (1.0 / HEAD_DIM)


def _head_norm(x, g, heads):
    return x * lax.rsqrt(_head_mean(x * x) + EPS) * jnp.tile(g, (1, heads))


def _head_norm_bwd(x, g, dy, heads):
    r = lax.rsqrt(_head_mean(x * x) + EPS)
    n = x * r
    dn = dy * jnp.tile(g, (1, heads))
    dx = r * (dn - n * _head_mean(dn * n))
    per_lane = jnp.sum(dy * n, axis=0, keepdims=True)
    dg = per_lane[:, 0:HEAD_DIM]
    for h in range(1, heads):
        dg = dg + per_lane[:, HEAD_DIM * h:HEAD_DIM * (h + 1)]
    return dx, dg


def _bucket_map():
    q = np.arange(BLK)[:, None]
    j = np.arange(BLK)[None, :]
    n = np.where(j > q, q + BLK - j, q - j)
    nf = np.maximum(n, 1).astype(np.float32)
    max_exact = NUM_BUCKETS // 2
    large = max_exact + (np.log(nf / max_exact) / math.log(BLK / max_exact) * (NUM_BUCKETS - max_exact)).astype(np.int32)
    large = np.minimum(large, NUM_BUCKETS - 1)
    return np.where(n < max_exact, n, large).astype(np.int32)


def _coords():
    return lax.axis_index("x"), lax.axis_index("y"), lax.axis_index("c")


def _lin(px, py, pc):
    return 4 * px + 2 * py + pc


_HBM = pl.BlockSpec(memory_space=pltpu.HBM)
_SEM = pl.BlockSpec(memory_space=pltpu.SEMAPHORE)
_EFFECT = pltpu.SideEffectType.DATAFLOW_SIDE_EFFECTING


def _in_hbm(a):
    return pltpu.with_memory_space_constraint(a, pltpu.HBM)


def _split_start(name, srcs, lands, plan, after):
    ns, nl = len(srcs), len(lands)
    n_copies = len(plan(0, 0, 0))
    n_after = 0 if after is None else 1

    def body(*refs):
        src_refs, land_refs = refs[:ns + nl], refs[ns:ns + nl]
        send_sems, recv_sems = refs[ns + nl + n_after], refs[ns + nl + n_after + 1]
        token = refs[-1]
        for k, (a, s_slot, l, d_slot, dev) in enumerate(plan(*_coords())):
            src = src_refs[a] if s_slot is None else src_refs[a].at[s_slot]
            pltpu.make_async_remote_copy(src_ref=src, dst_ref=land_refs[l].at[d_slot], send_sem=send_sems.at[k],
                                         recv_sem=recv_sems.at[k], device_id=dev, device_id_type=_MESH).start()
        token[...] = jnp.zeros_like(token)

    arrs = list(srcs) + list(lands)
    out = pl.pallas_call(
        body, name=name,
        out_shape=(pltpu.SemaphoreType.DMA((n_copies,)), pltpu.SemaphoreType.DMA((n_copies,)),
                   *[pltpu.HBM(a.shape, a.dtype) for a in arrs], jax.ShapeDtypeStruct((SUBLANES, 128), F32)),
        in_specs=[_HBM] * (ns + nl) + [_ANY] * n_after,
        out_specs=(_SEM, _SEM, *[_HBM] * (ns + nl), pl.BlockSpec(memory_space=pltpu.VMEM)),
        input_output_aliases={i: 2 + i for i in range(ns + nl)},
        compiler_params=pltpu.CompilerParams(has_side_effects=_EFFECT),
    )(*[_in_hbm(a) for a in arrs], *([] if after is None else [after]))
    return out[0], out[1], list(out[2:2 + ns]), list(out[2 + ns:2 + ns + nl]), out[-1]


def _split_wait(name, send_sems, recv_sems, srcs, lands, plan, recv_slots, after):
    ns, nl = len(srcs), len(lands)

    def body(*refs):
        src_refs, land_refs = refs[:ns + nl], refs[ns:ns + nl]
        send_sems, recv_sems = refs[ns + nl], refs[ns + nl + 1]
        coords = _coords()
        slots = recv_slots(*coords)
        for k, (a, s_slot, l, _, dev) in enumerate(plan(*coords)):
            src = src_refs[a] if s_slot is None else src_refs[a].at[s_slot]
            cp = pltpu.make_async_remote_copy(src_ref=src, dst_ref=land_refs[l].at[slots[k]], send_sem=send_sems.at[k],
                                              recv_sem=recv_sems.at[k], device_id=dev, device_id_type=_MESH)
            cp.wait_send()
            cp.wait_recv()

    arrs = list(srcs) + list(lands)
    out = pl.pallas_call(
        body, name=name,
        out_shape=tuple(pltpu.HBM(a.shape, a.dtype) for a in arrs),
        in_specs=[_HBM] * (ns + nl) + [_SEM, _SEM, _ANY],
        out_specs=tuple([_HBM] * (ns + nl)),
        input_output_aliases={i: i for i in range(ns + nl)},
        compiler_params=pltpu.CompilerParams(has_side_effects=_EFFECT),
    )(*arrs, send_sems, recv_sems, after)
    return list(out[:ns]), list(out[ns:])


def _chips(x, y):
    return [(1 - x, y), (x, 1 - y), (1 - x, 1 - y)]


def _gather_plan_ici(n):
    def plan(x, y, c):
        me = _lin(x, y, c)
        out = []
        for a in range(n):
            out.append((a, me, a, me, (x, y, 1 - c)))
            out += [(a, me, a, me, (cx, cy, c)) for cx, cy in _chips(x, y)]
        return out

    def recv_slots(x, y, c):
        out = []
        for _ in range(n):
            out.append(_lin(x, y, 1 - c))
            out += [_lin(cx, cy, c) for cx, cy in _chips(x, y)]
        return out

    return plan, recv_slots


def _gather_plan_d2d(n):
    def plan(x, y, c):
        return [(a, _lin(cx, cy, c), a, _lin(cx, cy, c), (x, y, 1 - c)) for a in range(n) for cx, cy in _chips(x, y)]

    def recv_slots(x, y, c):
        return [_lin(cx, cy, 1 - c) for _ in range(n) for cx, cy in _chips(x, y)]

    return plan, recv_slots


def _all_gather_split(lands, tag, after):
    n = len(lands)
    plan1, slots1 = _gather_plan_ici(n)
    s1, r1, _, lands, token = _split_start(f"gather_{tag}_ici_start", [], lands, plan1, after)

    def finish(after):
        _, got = _split_wait(f"gather_{tag}_ici_wait", s1, r1, [], lands, plan1, slots1, after)
        plan2, slots2 = _gather_plan_d2d(n)
        s2, r2, _, got, token2 = _split_start(f"gather_{tag}_d2d_start", [], got, plan2, None)
        return _split_wait(f"gather_{tag}_d2d_wait", s2, r2, [], got, plan2, slots2, token2)[1]

    return finish, token


def _all_gather_tree(lands, tag, after):
    n = len(lands)

    def plan1(x, y, c):
        me = _lin(x, y, c)
        return [(a, me, a, me, dev) for a in range(n) for dev in ((x, y, 1 - c), (1 - x, y, c), (x, 1 - y, c))]

    def slots1(x, y, c):
        return [s for _ in range(n) for s in (_lin(x, y, 1 - c), _lin(1 - x, y, c), _lin(x, 1 - y, c))]

    def plan2(x, y, c):
        from_x, from_y = _lin(1 - x, y, c), _lin(x, 1 - y, c)
        north = c == 1
        passed = jnp.where(north, from_x, from_y)
        onward = (jnp.where(north, x, 1 - x), jnp.where(north, 1 - y, y), c)
        sib = (x, y, 1 - c)
        return [cp for a in range(n) for cp in ((a, passed, a, passed, onward), (a, from_x, a, from_x, sib),
                                                (a, from_y, a, from_y, sib))]

    def slots2(x, y, c):
        return [s for _ in range(n) for s in (_lin(1 - x, 1 - y, c), _lin(1 - x, y, 1 - c), _lin(x, 1 - y, 1 - c))]

    def plan3(x, y, c):
        diag = _lin(1 - x, 1 - y, c)
        return [(a, diag, a, diag, (x, y, 1 - c)) for a in range(n)]

    def slots3(x, y, c):
        return [_lin(1 - x, 1 - y, 1 - c)] * n

    s1, r1, _, lands, token = _split_start(f"gather_{tag}_1_start", [], lands, plan1, after)
    state = {}

    def stage2(after):
        _, got = _split_wait(f"gather_{tag}_1_wait", s1, r1, [], lands, plan1, slots1, after)
        state["s"], state["r"], _, state["lands"], token2 = _split_start(f"gather_{tag}_2_start", [], got, plan2, None)
        return token2

    def stage3(after):
        _, got = _split_wait(f"gather_{tag}_2_wait", state["s"], state["r"], [], state["lands"], plan2, slots2, after)
        s3, r3, _, got, token3 = _split_start(f"gather_{tag}_3_start", [], got, plan3, None)
        return _split_wait(f"gather_{tag}_3_wait", s3, r3, [], got, plan3, slots3, token3)[1]

    return stage2, stage3, token


_CHIP_LIST = ((0, 0), (0, 1), (1, 0), (1, 1))


def _reduce_plan_d2d(n):
    def plan(x, y, c):
        return [(a, _lin(qx, qy, 1 - c), a, q, (x, y, 1 - c)) for a in range(n) for q, (qx, qy) in enumerate(_CHIP_LIST)]

    def recv_slots(x, y, c):
        return [q for _ in range(n) for q in range(4)]

    return plan, recv_slots


def _reduce_plan_ici(n):
    def plan(x, y, c):
        return [(a, 2 * cx + cy, a, j, (cx, cy, c)) for a in range(n) for j, (cx, cy) in enumerate(_chips(x, y))]

    def recv_slots(x, y, c):
        return [j for _ in range(n) for j in range(3)]

    return plan, recv_slots


def _peers(x, y, c):
    return [(1 - x if r & 4 else x, 1 - y if r & 2 else y, 1 - c if r & 1 else c) for r in range(1, N_DEV)]


def _scatter_plan(n):
    def plan(x, y, c):
        return [(a, _lin(*peer), a, r, peer) for a in range(n) for r, peer in enumerate(_peers(x, y, c))]

    def recv_slots(x, y, c):
        return [r for _ in range(n) for r in range(N_DEV - 1)]

    return plan, recv_slots


def _broadcast_plan():
    peers = _peers

    def plan(x, y, c):
        return [(0, None, 0, _lin(x, y, c), peer) for peer in peers(x, y, c)]

    def recv_slots(x, y, c):
        return [_lin(*peer) for peer in peers(x, y, c)]

    return plan, recv_slots


def _chip_partial(grads, recvd, core, name):
    n = len(grads)

    def body(c_ref, *refs):
        for a in range(n):
            g_ref, r_ref, o_ref = refs[a], refs[n + a], refs[2 * n + a]
            o_ref[...] = (g_ref[...].astype(F32) + r_ref[...].astype(F32)).astype(o_ref.dtype)

    def blk(a, own):
        zeros = (0,) * (a.ndim - 1)
        return pl.BlockSpec((None,) + a.shape[1:],
                            (lambda q, c_ref: (2 * q + c_ref[0],) + zeros) if own else (lambda q, c_ref: (q,) + zeros))

    return pl.pallas_call(
        body, name=name,
        grid_spec=pltpu.PrefetchScalarGridSpec(
            num_scalar_prefetch=1, grid=(4,),
            in_specs=[blk(a, True) for a in grads] + [blk(a, False) for a in recvd],
            out_specs=[blk(a, False) for a in recvd]),
        out_shape=[jax.ShapeDtypeStruct(a.shape, a.dtype) for a in recvd],
        compiler_params=_params(("arbitrary",)),
    )(core, *grads, *recvd)


def _reduce_scatter_split(grads, tag, core, behind):
    n = len(grads)
    plan1, slots1 = _reduce_plan_d2d(n)
    lands1 = [lax.empty((4,) + a.shape[1:], a.dtype) for a in grads]
    s1, r1, srcs1, lands1, token1 = _split_start(f"reduce_{tag}_d2d_start", grads, lands1, plan1, None)
    own, got = _split_wait(f"reduce_{tag}_d2d_wait", s1, r1, srcs1, lands1, plan1, slots1, behind(token1))
    parts = _chip_partial(own, got, core, f"reduce_{tag}_partial")
    plan2, slots2 = _reduce_plan_ici(n)
    lands2 = [lax.empty((3,) + a.shape[1:], a.dtype) for a in grads]
    s2, r2, srcs2, lands2, token2 = _split_start(f"reduce_{tag}_ici_start", parts, lands2, plan2, None)

    def finish(after):
        return _split_wait(f"reduce_{tag}_ici_wait", s2, r2, srcs2, lands2, plan2, slots2, after)

    return finish, token2


def _place_shards(me, shards, dtypes):
    n = len(shards)

    def body(me_ref, *refs):
        for a in range(n):
            refs[n + a][...] = refs[a][...].astype(dtypes[a])

    full = lambda s: pl.BlockSpec(s.shape, lambda i, me_ref: (0,) * s.ndim)
    slot = lambda s: pl.BlockSpec((None,) + s.shape, lambda i, me_ref: (me_ref[0],) + (0,) * s.ndim)
    return pl.pallas_call(
        body, name="place_shards",
        grid_spec=pltpu.PrefetchScalarGridSpec(num_scalar_prefetch=1, grid=(1,), in_specs=[full(s) for s in shards],
                                               out_specs=[slot(s) for s in shards]),
        out_shape=[jax.ShapeDtypeStruct((N_DEV,) + s.shape, d) for s, d in zip(shards, dtypes)],
        compiler_params=_params(("arbitrary",)),
    )(me, *shards)


def _mix_in_fwd(x, g1, w_in_t, conv_w, gq, gk, gconv):
    tm = 512
    n_t = SEQ // tm

    def body(x_ref, g1_ref, w_ref, cw_ref, gq_ref, gk_ref, gc_ref,
             proj_ref, u1_ref, ycn_ref, qn_ref, kn_ref, v_ref, halo_ref):
        @pl.when(pl.program_id(0) == 0)
        def _():
            halo_ref[...] = jnp.zeros_like(halo_ref)

        xv = x_ref[...]
        u = (xv * _rstd(xv) * g1_ref[...]).astype(BF16)
        u1_ref[...] = u
        proj = _dot_nt(u, w_ref[...])
        proj_ref[...] = proj
        gate_b = proj[:, 0:CONV_WIDTH]
        a = proj[:, CONV_WIDTH:2 * CONV_WIDTH] * proj[:, 2 * CONV_WIDTH:3 * CONV_WIDTH]
        cv, _, _ = _conv3(a, _taps(cw_ref[...]), halo_ref[...])
        halo_ref[...] = a[tm - SUBLANES:]
        yc = gate_b * cv
        ycn_ref[...] = (yc * _rstd(yc) * gc_ref[...]).astype(BF16)
        q0 = 3 * CONV_WIDTH
        qn_ref[...] = _head_norm(proj[:, q0:q0 + ATTN_WIDTH], gq_ref[...], N_HEADS).astype(BF16)
        k0 = q0 + ATTN_WIDTH
        kn_ref[...] = _head_norm(proj[:, k0:k0 + KV_WIDTH], gk_ref[...], 2).astype(BF16)
        v_ref[...] = proj[:, k0 + KV_WIDTH:k0 + 2 * KV_WIDTH].astype(BF16)

    const = lambda shape: pl.BlockSpec(shape, lambda i: (0,) * len(shape))
    rows = lambda w: pl.BlockSpec((tm, w), lambda i: (i, 0))
    return pl.pallas_call(
        body, name="mix_in_fwd", grid=(n_t,),
        in_specs=[rows(D_MODEL), const((1, D_MODEL)), const((IN_WIDTH, D_MODEL)), const((3, CONV_WIDTH)),
                  const((1, HEAD_DIM)), const((1, HEAD_DIM)), const((1, CONV_WIDTH))],
        out_specs=[rows(IN_WIDTH), rows(D_MODEL), rows(CONV_WIDTH), rows(ATTN_WIDTH), rows(KV_WIDTH), rows(KV_WIDTH)],
        out_shape=[jax.ShapeDtypeStruct((SEQ, IN_WIDTH), F32), jax.ShapeDtypeStruct((SEQ, D_MODEL), BF16),
                   jax.ShapeDtypeStruct((SEQ, CONV_WIDTH), BF16),
                   jax.ShapeDtypeStruct((SEQ, ATTN_WIDTH), BF16), jax.ShapeDtypeStruct((SEQ, KV_WIDTH), BF16),
                   jax.ShapeDtypeStruct((SEQ, KV_WIDTH), BF16)],
        scratch_shapes=[pltpu.VMEM((SUBLANES, CONV_WIDTH), F32)],
        compiler_params=_params(("arbitrary",)),
    )(x, g1, w_in_t, conv_w, gq, gk, gconv)


GROUP_ROWS = GQA_GROUP * BLK


def _band_bias(tbl_ref, bkt, bias_ref):
    for h in range(N_HEADS):
        acc = jnp.zeros(bkt.shape, F32)
        for b in range(NUM_BUCKETS):
            acc = jnp.where(bkt == b, tbl_ref[h, b], acc)
        bias_ref[h // GQA_GROUP, BLK * (h % GQA_GROUP):BLK * (h % GQA_GROUP + 1), :] = acc


def _band_masks(i):
    qi = lax.broadcasted_iota(jnp.int32, (GROUP_ROWS, BLK), 0) & (BLK - 1)
    ji = lax.broadcasted_iota(jnp.int32, (GROUP_ROWS, BLK), 1)
    upper = ji > qi
    return upper, upper & (i == 0)


def _stack_heads(x, g):
    return jnp.concatenate([x[:, HEAD_DIM * h:HEAD_DIM * (h + 1)] for h in range(GQA_GROUP * g, GQA_GROUP * (g + 1))], axis=0)


def _unstack_heads(groups):
    return jnp.concatenate([p[BLK * t:BLK * (t + 1)] for p in groups for t in range(GQA_GROUP)], axis=-1)


def _per_head_rows(vals):
    row = lax.broadcasted_iota(jnp.int32, (GROUP_ROWS, 1), 0)
    col = jnp.full((GROUP_ROWS, 1), vals[GQA_GROUP - 1], F32)
    for t in range(GQA_GROUP - 2, -1, -1):
        col = jnp.where(row < BLK * (t + 1), vals[t], col)
    return col


def _band_rows(ref, i):
    prev = pl.multiple_of(jnp.maximum(i - 1, 0) * BLK, BLK)
    cur = pl.multiple_of(i * BLK, BLK)
    return jnp.concatenate([ref[pl.ds(prev, BLK), :], ref[pl.ds(cur, BLK), :]], axis=0), prev, cur


def _fold(band, upper):
    return jnp.where(upper, band[:, :BLK], band[:, BLK:])


def _unfold(tile, upper):
    return jnp.concatenate([jnp.where(upper, tile, 0.0), jnp.where(upper, 0.0, tile)], axis=1)


def _head_probs(qh, kh, bias, upper, dead, sink):
    logits = _fold(_dot_nt(qh, kh), upper) * (HEAD_DIM ** -0.5) + bias
    logits = jnp.where(dead, NEG_INF, logits)
    m = jnp.maximum(jnp.max(logits, axis=-1, keepdims=True), sink)
    p = jnp.exp(logits - m)
    es = jnp.exp(sink - m)
    den = jnp.sum(p, axis=-1, keepdims=True) + es
    return p / den, es / den


def _attn_fwd(qn, kn, v, tbl, sinks, bkt, gattn, after=None):
    n_b = SEQ // BLK

    def body(q_ref, k_ref, v_ref, tbl_ref, sink_ref, bkt_ref, ga_ref, y_ref, yn_ref, p_ref, ps_ref, bias_ref):
        i = pl.program_id(0)

        @pl.when(i == 0)
        def _():
            _band_bias(tbl_ref, bkt_ref[...], bias_ref)

        kb, _, _ = _band_rows(k_ref, i)
        vb, _, _ = _band_rows(v_ref, i)
        upper, dead = _band_masks(i)
        q = q_ref[...]
        lane = lax.broadcasted_iota(jnp.int32, (BLK, 128), 1)
        outs = []
        psinks = jnp.zeros((BLK, 128), F32)
        for g in range(N_HEADS // GQA_GROUP):
            kv = slice(HEAD_DIM * g, HEAD_DIM * (g + 1))
            sink = _per_head_rows([sink_ref[0, GQA_GROUP * g + t] for t in range(GQA_GROUP)])
            probs, psink = _head_probs(_stack_heads(q, g), kb[:, kv], bias_ref[g], upper, dead, sink)
            p_ref[g] = probs.astype(BF16)
            for t in range(GQA_GROUP):
                psinks = jnp.where(lane == GQA_GROUP * g + t, psink[BLK * t:BLK * (t + 1)], psinks)
            outs.append(_dot(_unfold(probs, upper).astype(BF16), vb[:, kv]))
        ps_ref[...] = psinks
        y = _unstack_heads(outs)
        y_ref[...] = y
        yn_ref[...] = (y * _rstd(y) * ga_ref[...]).astype(BF16)

    const = lambda shape: pl.BlockSpec(shape, lambda i: (0,) * len(shape))
    rows = lambda w: pl.BlockSpec((BLK, w), lambda i: (i, 0))
    smem = pl.BlockSpec(memory_space=pltpu.SMEM)
    body, more_specs, more = _ordered_behind(body, 7, after)
    return pl.pallas_call(
        body, name="attn_fwd", grid=(n_b,),
        in_specs=[rows(ATTN_WIDTH), const((SEQ, KV_WIDTH)), const((SEQ, KV_WIDTH)), smem, smem,
                  const((BLK, BLK)), const((1, ATTN_WIDTH))] + more_specs,
        out_specs=[rows(ATTN_WIDTH), rows(ATTN_WIDTH),
                   pl.BlockSpec((None, N_HEADS // GQA_GROUP, GROUP_ROWS, BLK), lambda i: (i, 0, 0, 0)), rows(128)],
        out_shape=[jax.ShapeDtypeStruct((SEQ, ATTN_WIDTH), F32), jax.ShapeDtypeStruct((SEQ, ATTN_WIDTH), BF16),
                   jax.ShapeDtypeStruct((n_b, N_HEADS // GQA_GROUP, GROUP_ROWS, BLK), BF16),
                   jax.ShapeDtypeStruct((SEQ, 128), F32)],
        scratch_shapes=[pltpu.VMEM((N_HEADS // GQA_GROUP, GROUP_ROWS, BLK), F32)],
        compiler_params=_params(("arbitrary",)),
    )(qn, kn, v, tbl, sinks, bkt, gattn, *more)


def _ffn_up(x, ycn, yan, w_out, g2, w_up, fcw, fcb, after=None):
    tm = 512
    n_t = SEQ // tm

    def body(x_ref, ycn_ref, yan_ref, wo_ref, g2_ref, wu_ref, cw_ref, b_ref,
             h1_ref, u2_ref, up_ref, pre_ref, act_ref, halo_ref):
        i, j = pl.program_id(0), pl.program_id(1)

        @pl.when(j == 0)
        def _():
            h1 = x_ref[...] + _dot(ycn_ref[...], wo_ref[0:CONV_WIDTH, :]) + _dot(yan_ref[...], wo_ref[CONV_WIDTH:, :])
            h1_ref[...] = h1
            u2_ref[...] = (h1 * _rstd(h1) * g2_ref[...]).astype(BF16)

        u2 = u2_ref[...]
        pre = []
        for s in range(2):
            up = _dot_nt(u2, wu_ref[s])
            up_ref[s] = up.astype(BF16)
            halo = jnp.where(i == 0, 0.0, halo_ref[s, j])
            pre.append(_conv3(up, _taps(cw_ref.at[s]), halo)[0] + b_ref[s])
            pre_ref[s] = pre[s].astype(BF16)
            halo_ref[s, j] = up[tm - SUBLANES:]
        g, val = pre
        act_ref[...] = (g * jax.nn.sigmoid(g) * val).astype(BF16)

    rows = lambda w: pl.BlockSpec((tm, w), lambda i, j: (i, 0))
    const = lambda shape: pl.BlockSpec(shape, lambda i, j: (0,) * len(shape))
    pair = lambda *s: pl.BlockSpec((2, None) + s, lambda i, j: (0, j) + (0,) * len(s))
    upb = pl.BlockSpec((2, None, tm, FFN_BLK), lambda i, j: (0, j, i, 0))
    body, more_specs, more = _ordered_behind(body, 8, after)
    return pl.pallas_call(
        body, name="ffn_up", grid=(n_t, N_FFN_BLK),
        in_specs=[rows(D_MODEL), rows(CONV_WIDTH), rows(ATTN_WIDTH), const((D_MODEL, D_MODEL)), const((1, D_MODEL)),
                  pair(FFN_BLK, D_MODEL), pair(3, 1, FFN_BLK), pair(1, FFN_BLK)] + more_specs,
        out_specs=[rows(D_MODEL), rows(D_MODEL), upb, upb, pl.BlockSpec((None, tm, FFN_BLK), lambda i, j: (j, i, 0))],
        out_shape=[jax.ShapeDtypeStruct((SEQ, D_MODEL), F32), jax.ShapeDtypeStruct((SEQ, D_MODEL), BF16),
                   jax.ShapeDtypeStruct((2, N_FFN_BLK, SEQ, FFN_BLK), BF16),
                   jax.ShapeDtypeStruct((2, N_FFN_BLK, SEQ, FFN_BLK), BF16),
                   jax.ShapeDtypeStruct((N_FFN_BLK, SEQ, FFN_BLK), BF16)],
        scratch_shapes=[pltpu.VMEM((2, N_FFN_BLK, SUBLANES, FFN_BLK), F32)],
        compiler_params=_params(("arbitrary", "arbitrary")),
    )(x, ycn, yan, w_out, g2, w_up, fcw, fcb, *more)


def _ffn_down(act, w_down, h1, tgt):
    tm = 512
    n_t = SEQ // tm

    def body(act_ref, wd_ref, h1_ref, tgt_ref, dh2_ref, dh2b_ref, loss_ref):
        @pl.when(pl.program_id(0) == 0)
        def _():
            loss_ref[...] = jnp.zeros_like(loss_ref)

        out = _dot(act_ref[0], wd_ref[0])
        for j in range(1, N_FFN_BLK):
            out = out + _dot(act_ref[j], wd_ref[j])
        err = h1_ref[...] + out - tgt_ref[...]
        loss_ref[...] += 0.5 * jnp.sum(err * err) / D_MODEL
        dh2 = err / D_MODEL
        dh2_ref[...] = dh2
        dh2b_ref[...] = dh2.astype(BF16)

    rows = lambda w: pl.BlockSpec((tm, w), lambda i: (i, 0))
    return pl.pallas_call(
        body, name="ffn_down", grid=(n_t,),
        in_specs=[pl.BlockSpec((N_FFN_BLK, tm, FFN_BLK), lambda i: (0, i, 0)),
                  pl.BlockSpec((N_FFN_BLK, FFN_BLK, D_MODEL), lambda i: (0, 0, 0)), rows(D_MODEL), rows(D_MODEL)],
        out_specs=[rows(D_MODEL), rows(D_MODEL), pl.BlockSpec((SUBLANES, 128), lambda i: (0, 0))],
        out_shape=[jax.ShapeDtypeStruct((SEQ, D_MODEL), F32), jax.ShapeDtypeStruct((SEQ, D_MODEL), BF16),
                   jax.ShapeDtypeStruct((SUBLANES, 128), F32)],
        compiler_params=_params(("arbitrary",)),
    )(act, w_down, h1, tgt)


def _out_proj(x, ycn, yan, w_out, g2):
    tm = 512

    def body(x_ref, ycn_ref, yan_ref, wo_ref, g2_ref, h1_ref, u2_ref):
        h1 = x_ref[...] + _dot(ycn_ref[...], wo_ref[0:CONV_WIDTH, :]) + _dot(yan_ref[...], wo_ref[CONV_WIDTH:, :])
        h1_ref[...] = h1
        u2_ref[...] = (h1 * _rstd(h1) * g2_ref[...]).astype(BF16)

    rows = lambda w: pl.BlockSpec((tm, w), lambda i: (i, 0))
    const = lambda shape: pl.BlockSpec(shape, lambda i: (0,) * len(shape))
    return pl.pallas_call(
        body, name="out_proj", grid=(SEQ // tm,),
        in_specs=[rows(D_MODEL), rows(CONV_WIDTH), rows(ATTN_WIDTH), const((D_MODEL, D_MODEL)), const((1, D_MODEL))],
        out_specs=[rows(D_MODEL), rows(D_MODEL)],
        out_shape=[jax.ShapeDtypeStruct((SEQ, D_MODEL), F32), jax.ShapeDtypeStruct((SEQ, D_MODEL), BF16)],
        compiler_params=_params(("arbitrary",)),
    )(x, ycn, yan, w_out, g2)


def _ffn_fwd(h1, u2, w_up, fcw, fcb, w_down, tgt):
    tm = 512
    n_t = SEQ // tm
    last = N_FFN_BLK - 1

    def body(u2_ref, wu_ref, cw_ref, b_ref, wd_ref, h1_ref, tgt_ref,
             up_ref, pre_ref, act_ref, dh2_ref, dh2b_ref, loss_ref, acc_ref, halo_ref):
        j, i = pl.program_id(0), pl.program_id(1)
        rows = pl.ds(pl.multiple_of(i * tm, tm), tm)

        @pl.when((i == 0) & (j == 0))
        def _():
            loss_ref[...] = jnp.zeros_like(loss_ref)

        u2 = u2_ref[...]
        pre = []
        for s in range(2):
            up = _dot_nt(u2, wu_ref[s])
            up_ref[s] = up.astype(BF16)
            halo = jnp.where(i == 0, 0.0, halo_ref[s])
            pre.append(_conv3(up, _taps(cw_ref.at[s]), halo)[0] + b_ref[s])
            pre_ref[s] = pre[s].astype(BF16)
            halo_ref[s] = up[tm - SUBLANES:]
        g, val = pre
        act = (g * jax.nn.sigmoid(g) * val).astype(BF16)
        act_ref[...] = act
        out = _dot(act, wd_ref[...])

        @pl.when(j == 0)
        def _():
            acc_ref[rows, :] = out

        @pl.when(j > 0)
        def _():
            acc_ref[rows, :] += out

        @pl.when(j == last)
        def _():
            err = h1_ref[...] + acc_ref[rows, :] - tgt_ref[...]
            loss_ref[...] += 0.5 * jnp.sum(err * err) / D_MODEL
            dh2 = err / D_MODEL
            dh2_ref[...] = dh2
            dh2b_ref[...] = dh2.astype(BF16)

    late = lambda w: pl.BlockSpec((tm, w), lambda j, i: (jnp.where(j == last, i, 0), 0))
    pair = lambda *s: pl.BlockSpec((2, None) + s, lambda j, i: (0, j) + (0,) * len(s))
    upb = pl.BlockSpec((2, None, tm, FFN_BLK), lambda j, i: (0, j, i, 0))
    return pl.pallas_call(
        body, name="ffn_fwd", grid=(N_FFN_BLK, n_t),
        in_specs=[pl.BlockSpec((tm, D_MODEL), lambda j, i: (i, 0)), pair(FFN_BLK, D_MODEL), pair(3, 1, FFN_BLK),
                  pair(1, FFN_BLK), pl.BlockSpec((None, FFN_BLK, D_MODEL), lambda j, i: (j, 0, 0)),
                  late(D_MODEL), late(D_MODEL)],
        out_specs=[upb, upb, pl.BlockSpec((None, tm, FFN_BLK), lambda j, i: (j, i, 0)), late(D_MODEL), late(D_MODEL),
                   pl.BlockSpec((SUBLANES, 128), lambda j, i: (0, 0))],
        out_shape=[jax.ShapeDtypeStruct((2, N_FFN_BLK, SEQ, FFN_BLK), BF16),
                   jax.ShapeDtypeStruct((2, N_FFN_BLK, SEQ, FFN_BLK), BF16),
                   jax.ShapeDtypeStruct((N_FFN_BLK, SEQ, FFN_BLK), BF16),
                   jax.ShapeDtypeStruct((SEQ, D_MODEL), F32), jax.ShapeDtypeStruct((SEQ, D_MODEL), BF16),
                   jax.ShapeDtypeStruct((SUBLANES, 128), F32)],
        scratch_shapes=[pltpu.VMEM((SEQ, D_MODEL), F32), pltpu.VMEM((2, SUBLANES, FFN_BLK), F32)],
        compiler_params=_params(("arbitrary", "arbitrary")),
    )(u2, w_up, fcw, fcb, w_down, h1, tgt)


def _ffn_fwd_token_major(x, ycn, yan, w_out, g2, w_up, fcw, fcb, w_down, tgt):
    tm = 512
    n_t = SEQ // tm

    def body(x_ref, ycn_ref, yan_ref, wo_ref, g2_ref, wu_ref, cw_ref, b_ref, wd_ref, tgt_ref,
             h1_ref, u2_ref, up_ref, pre_ref, act_ref, dh2_ref, dh2b_ref, loss_ref, acc_ref, halo_ref):
        i, j = pl.program_id(0), pl.program_id(1)

        @pl.when((i == 0) & (j == 0))
        def _():
            loss_ref[...] = jnp.zeros_like(loss_ref)

        @pl.when(j == 0)
        def _():
            h1 = x_ref[...] + _dot(ycn_ref[...], wo_ref[0:CONV_WIDTH, :]) + _dot(yan_ref[...], wo_ref[CONV_WIDTH:, :])
            h1_ref[...] = h1
            u2_ref[...] = (h1 * _rstd(h1) * g2_ref[...]).astype(BF16)
            acc_ref[...] = jnp.zeros_like(acc_ref)

        u2 = u2_ref[...]
        pre = []
        for s in range(2):
            up = _dot_nt(u2, wu_ref[s])
            up_ref[s] = up.astype(BF16)
            halo = jnp.where(i == 0, 0.0, halo_ref[s, j])
            pre.append(_conv3(up, _taps(cw_ref.at[s]), halo)[0] + b_ref[s])
            pre_ref[s] = pre[s].astype(BF16)
            halo_ref[s, j] = up[tm - SUBLANES:]
        g, val = pre
        act = (g * jax.nn.sigmoid(g) * val).astype(BF16)
        act_ref[...] = act
        acc_ref[...] += _dot(act, wd_ref[...])

        @pl.when(j == N_FFN_BLK - 1)
        def _():
            err = h1_ref[...] + acc_ref[...] - tgt_ref[...]
            loss_ref[...] += 0.5 * jnp.sum(err * err) / D_MODEL
            dh2 = err / D_MODEL
            dh2_ref[...] = dh2
            dh2b_ref[...] = dh2.astype(BF16)

    rows = lambda w: pl.BlockSpec((tm, w), lambda i, j: (i, 0))
    const = lambda shape: pl.BlockSpec(shape, lambda i, j: (0,) * len(shape))
    pair = lambda *s: pl.BlockSpec((2, None) + s, lambda i, j: (0, j) + (0,) * len(s))
    upb = pl.BlockSpec((2, None, tm, FFN_BLK), lambda i, j: (0, j, i, 0))
    return pl.pallas_call(
        body, name="ffn_fwd", grid=(n_t, N_FFN_BLK),
        in_specs=[rows(D_MODEL), rows(CONV_WIDTH), rows(ATTN_WIDTH), const((D_MODEL, D_MODEL)), const((1, D_MODEL)),
                  pair(FFN_BLK, D_MODEL), pair(3, 1, FFN_BLK), pair(1, FFN_BLK),
                  pl.BlockSpec((None, FFN_BLK, D_MODEL), lambda i, j: (j, 0, 0)), rows(D_MODEL)],
        out_specs=[rows(D_MODEL), rows(D_MODEL), upb, upb, pl.BlockSpec((None, tm, FFN_BLK), lambda i, j: (j, i, 0)),
                   rows(D_MODEL), rows(D_MODEL), const((SUBLANES, 128))],
        out_shape=[jax.ShapeDtypeStruct((SEQ, D_MODEL), F32), jax.ShapeDtypeStruct((SEQ, D_MODEL), BF16),
                   jax.ShapeDtypeStruct((2, N_FFN_BLK, SEQ, FFN_BLK), BF16),
                   jax.ShapeDtypeStruct((2, N_FFN_BLK, SEQ, FFN_BLK), BF16),
                   jax.ShapeDtypeStruct((N_FFN_BLK, SEQ, FFN_BLK), BF16),
                   jax.ShapeDtypeStruct((SEQ, D_MODEL), F32), jax.ShapeDtypeStruct((SEQ, D_MODEL), BF16),
                   jax.ShapeDtypeStruct((SUBLANES, 128), F32)],
        scratch_shapes=[pltpu.VMEM((tm, D_MODEL), F32), pltpu.VMEM((2, N_FFN_BLK, SUBLANES, FFN_BLK), F32)],
        compiler_params=_params(("arbitrary", "arbitrary")),
    )(x, ycn, yan, w_out, g2, w_up, fcw, fcb, w_down, tgt)


def _ffn_bwd(dh2, dh2b, h1, g2, up, pre, w_up, fcw, w_down):
    tm = 256
    n_t = SEQ // tm
    last = N_FFN_BLK - 1

    def body(dh2b_ref, up_ref, pre_ref, wu_ref, cw_ref, wd_ref, dh2_ref, h1_ref, g2_ref,
             dup_ref, dh1_ref, dh1b_ref, dfb_ref, dfcw_ref, dg2_ref, acc_ref, next_ref):
        j, i = pl.program_id(0), pl.program_id(1)
        rows = pl.ds(pl.multiple_of((n_t - 1 - i) * tm, tm), tm)

        @pl.when((i == 0) & (j == 0))
        def _():
            dfb_ref[...] = jnp.zeros_like(dfb_ref)
            dfcw_ref[...] = jnp.zeros_like(dfcw_ref)
            dg2_ref[...] = jnp.zeros_like(dg2_ref)

        g, val = pre_ref[0].astype(F32), pre_ref[1].astype(F32)
        sg = jax.nn.sigmoid(g)
        silu = g * sg
        dact = _dot_nt(dh2b_ref[...], wd_ref[...])
        dpre = (dact * val * (sg * (1.0 + g * (1.0 - sg))), dact * silu)
        du = None
        for s in range(2):
            d = dpre[s]
            u = up_ref[s].astype(F32)
            w = _taps(cw_ref.at[s])
            nxt = jnp.where(i == 0, 0.0, next_ref[s])
            d1 = _shift_up(d, 1, nxt)
            d2 = _shift_up(d, 2, nxt)
            next_ref[s] = d[:SUBLANES]
            dfb_ref[s, j] += jnp.sum(d, axis=0, keepdims=True)
            dfcw_ref[s, j, 0] += jnp.sum(d2 * u, axis=0, keepdims=True)
            dfcw_ref[s, j, 1] += jnp.sum(d1 * u, axis=0, keepdims=True)
            dfcw_ref[s, j, 2] += jnp.sum(d * u, axis=0, keepdims=True)
            dup = (d * w[2] + d1 * w[1] + d2 * w[0]).astype(BF16)
            dup_ref[s] = dup
            part = _dot(dup, wu_ref[s])
            du = part if du is None else du + part

        @pl.when(j == 0)
        def _():
            acc_ref[rows, :] = du

        @pl.when(j > 0)
        def _():
            acc_ref[rows, :] += du

        @pl.when(j == last)
        def _():
            dn, dgain = _rms_bwd(h1_ref[...], g2_ref[...], acc_ref[rows, :])
            dh1 = dh2_ref[...] + dn
            dh1_ref[...] = dh1
            dh1b_ref[...] = dh1.astype(BF16)
            dg2_ref[...] += dgain

    rev = lambda i: n_t - 1 - i
    const = lambda shape: pl.BlockSpec(shape, lambda j, i: (0,) * len(shape))
    late = lambda w: pl.BlockSpec((tm, w), lambda j, i: (jnp.where(j == last, rev(i), n_t - 1), 0))
    pair = lambda *s: pl.BlockSpec((2, None) + s, lambda j, i: (0, j) + (0,) * len(s))
    upb = pl.BlockSpec((2, None, tm, FFN_BLK), lambda j, i: (0, j, rev(i), 0))
    return pl.pallas_call(
        body, name="ffn_bwd", grid=(N_FFN_BLK, n_t),
        in_specs=[pl.BlockSpec((tm, D_MODEL), lambda j, i: (rev(i), 0)), upb, upb, pair(FFN_BLK, D_MODEL),
                  pair(3, 1, FFN_BLK), pl.BlockSpec((None, FFN_BLK, D_MODEL), lambda j, i: (j, 0, 0)),
                  late(D_MODEL), late(D_MODEL), const((1, D_MODEL))],
        out_specs=[upb, late(D_MODEL), late(D_MODEL),
                   const((2, N_FFN_BLK, 1, FFN_BLK)), const((2, N_FFN_BLK, 3, 1, FFN_BLK)), const((1, D_MODEL))],
        out_shape=[jax.ShapeDtypeStruct((2, N_FFN_BLK, SEQ, FFN_BLK), BF16), jax.ShapeDtypeStruct((SEQ, D_MODEL), F32),
                   jax.ShapeDtypeStruct((SEQ, D_MODEL), BF16), jax.ShapeDtypeStruct((2, N_FFN_BLK, 1, FFN_BLK), F32),
                   jax.ShapeDtypeStruct((2, N_FFN_BLK, 3, 1, FFN_BLK), F32), jax.ShapeDtypeStruct((1, D_MODEL), F32)],
        scratch_shapes=[pltpu.VMEM((SEQ, D_MODEL), F32), pltpu.VMEM((2, SUBLANES, FFN_BLK), F32)],
        compiler_params=_params(("arbitrary", "arbitrary")),
    )(dh2b, up, pre, w_up, fcw, w_down, dh2, h1, g2)


def _ffn_bwd_token_major(dh2, dh2b, h1, g2, up, pre, w_up, fcw, w_down, after=None):
    tm = 512
    n_t = SEQ // tm

    def body(dh2_ref, dh2b_ref, h1_ref, g2_ref, up_ref, pre_ref, wu_ref, cw_ref, wd_ref,
             dup_ref, dh1_ref, dh1b_ref, dfb_ref, dfcw_ref, dg2_ref, acc_ref, next_ref):
        i, j = pl.program_id(0), pl.program_id(1)

        @pl.when((i == 0) & (j == 0))
        def _():
            dfb_ref[...] = jnp.zeros_like(dfb_ref)
            dfcw_ref[...] = jnp.zeros_like(dfcw_ref)
            dg2_ref[...] = jnp.zeros_like(dg2_ref)

        @pl.when(j == 0)
        def _():
            acc_ref[...] = jnp.zeros_like(acc_ref)

        g, val = pre_ref[0].astype(F32), pre_ref[1].astype(F32)
        sg = jax.nn.sigmoid(g)
        silu = g * sg
        dact = _dot_nt(dh2b_ref[...], wd_ref[...])
        dpre = (dact * val * (sg * (1.0 + g * (1.0 - sg))), dact * silu)
        for s in range(2):
            d = dpre[s]
            u = up_ref[s].astype(F32)
            w = _taps(cw_ref.at[s])
            nxt = jnp.where(i == 0, 0.0, next_ref[s, j])
            d1 = _shift_up(d, 1, nxt)
            d2 = _shift_up(d, 2, nxt)
            next_ref[s, j] = d[:SUBLANES]
            dfb_ref[s, j] += jnp.sum(d, axis=0, keepdims=True)
            dfcw_ref[s, j, 0] += jnp.sum(d2 * u, axis=0, keepdims=True)
            dfcw_ref[s, j, 1] += jnp.sum(d1 * u, axis=0, keepdims=True)
            dfcw_ref[s, j, 2] += jnp.sum(d * u, axis=0, keepdims=True)
            dup = (d * w[2] + d1 * w[1] + d2 * w[0]).astype(BF16)
            dup_ref[s] = dup
            acc_ref[...] += _dot(dup, wu_ref[s])

        @pl.when(j == N_FFN_BLK - 1)
        def _():
            dn, dgain = _rms_bwd(h1_ref[...], g2_ref[...], acc_ref[...])
            dh1 = dh2_ref[...] + dn
            dh1_ref[...] = dh1
            dh1b_ref[...] = dh1.astype(BF16)
            dg2_ref[...] += dgain

    rev = lambda i: n_t - 1 - i
    rows = lambda w: pl.BlockSpec((tm, w), lambda i, j: (rev(i), 0))
    const = lambda shape: pl.BlockSpec(shape, lambda i, j: (0,) * len(shape))
    pair = lambda *s: pl.BlockSpec((2, None) + s, lambda i, j: (0, j) + (0,) * len(s))
    upb = pl.BlockSpec((2, None, tm, FFN_BLK), lambda i, j: (0, j, rev(i), 0))
    body, more_specs, more = _ordered_behind(body, 9, after)
    return pl.pallas_call(
        body, name="ffn_bwd", grid=(n_t, N_FFN_BLK),
        in_specs=[rows(D_MODEL), rows(D_MODEL), rows(D_MODEL), const((1, D_MODEL)), upb, upb,
                  pair(FFN_BLK, D_MODEL), pair(3, 1, FFN_BLK),
                  pl.BlockSpec((None, FFN_BLK, D_MODEL), lambda i, j: (j, 0, 0))] + more_specs,
        out_specs=[upb, rows(D_MODEL), rows(D_MODEL),
                   const((2, N_FFN_BLK, 1, FFN_BLK)), const((2, N_FFN_BLK, 3, 1, FFN_BLK)), const((1, D_MODEL))],
        out_shape=[jax.ShapeDtypeStruct((2, N_FFN_BLK, SEQ, FFN_BLK), BF16), jax.ShapeDtypeStruct((SEQ, D_MODEL), F32),
                   jax.ShapeDtypeStruct((SEQ, D_MODEL), BF16), jax.ShapeDtypeStruct((2, N_FFN_BLK, 1, FFN_BLK), F32),
                   jax.ShapeDtypeStruct((2, N_FFN_BLK, 3, 1, FFN_BLK), F32), jax.ShapeDtypeStruct((1, D_MODEL), F32)],
        scratch_shapes=[pltpu.VMEM((tm, D_MODEL), F32), pltpu.VMEM((2, N_FFN_BLK, SUBLANES, FFN_BLK), F32)],
        compiler_params=_params(("arbitrary", "arbitrary")),
    )(dh2, dh2b, h1, g2, up, pre, w_up, fcw, w_down, *more)


def _grad_tn(a_list, b, out_rows, name, after=None):
    n = len(a_list)
    ncol = b.shape[1]

    def body(*refs):
        a_refs, b_ref, o_ref = refs[:n], refs[n], refs[n + 1]
        j = pl.program_id(0)
        for k in range(n):
            @pl.when(j == k)
            def _(k=k):
                o_ref[...] = _dot_tn(a_refs[k][...], b_ref[...]).astype(BF16)

    full = lambda shape: pl.BlockSpec(shape, lambda j: (0,) * len(shape))
    body, more_specs, more = _ordered_behind(body, n + 1, after)
    return pl.pallas_call(
        body, name=name, grid=(n,),
        in_specs=[full((SEQ, out_rows))] * n + [full((SEQ, ncol))] + more_specs,
        out_specs=pl.BlockSpec((None, out_rows, ncol), lambda j: (j, 0, 0)),
        out_shape=jax.ShapeDtypeStruct((n, out_rows, ncol), BF16),
        compiler_params=_params(("arbitrary",)),
    )(*a_list, b, *more)


def _grad_tn_blocked(a, b, name, a_is_blocked):
    nb = a.shape[0] if a_is_blocked else b.shape[0]
    a_w, b_w = a.shape[-1], b.shape[-1]

    def body(a_ref, b_ref, o_ref):
        o_ref[...] = _dot_tn(a_ref[...], b_ref[...]).astype(BF16)

    blocked = lambda w: pl.BlockSpec((None, SEQ, w), lambda k: (k, 0, 0))
    full = lambda w: pl.BlockSpec((SEQ, w), lambda k: (0, 0))
    return pl.pallas_call(
        body, name=name, grid=(nb,),
        in_specs=[blocked(a_w) if a_is_blocked else full(a_w), full(b_w) if a_is_blocked else blocked(b_w)],
        out_specs=pl.BlockSpec((None, a_w, b_w), lambda k: (k, 0, 0)),
        out_shape=jax.ShapeDtypeStruct((nb, a_w, b_w), BF16),
        compiler_params=_params(("arbitrary",)),
    )(a, b)


def _out_bwd(dh1b, w_out, y_attn, gattn, after=None):
    tm = 512
    n_t = SEQ // tm

    def body(dh_ref, wo_ref, y_ref, ga_ref, dycn_ref, dy_ref, dga_ref):
        @pl.when(pl.program_id(0) == 0)
        def _():
            dga_ref[...] = jnp.zeros_like(dga_ref)

        dycat = _dot_nt(dh_ref[...], wo_ref[...])
        dycn_ref[...] = dycat[:, :CONV_WIDTH]
        dy, dga = _rms_bwd(y_ref[...], ga_ref[...], dycat[:, CONV_WIDTH:])
        dy_ref[...] = dy
        dga_ref[...] += dga

    rows = lambda w: pl.BlockSpec((tm, w), lambda i: (i, 0))
    const = lambda shape: pl.BlockSpec(shape, lambda i: (0,) * len(shape))
    body, more_specs, more = _ordered_behind(body, 4, after)
    return pl.pallas_call(
        body, name="out_bwd", grid=(n_t,),
        in_specs=[rows(D_MODEL), const((D_MODEL, D_MODEL)), rows(ATTN_WIDTH), const((1, ATTN_WIDTH))] + more_specs,
        out_specs=[rows(CONV_WIDTH), rows(ATTN_WIDTH), const((1, ATTN_WIDTH))],
        out_shape=[jax.ShapeDtypeStruct((SEQ, CONV_WIDTH), F32), jax.ShapeDtypeStruct((SEQ, ATTN_WIDTH), F32),
                   jax.ShapeDtypeStruct((1, ATTN_WIDTH), F32)],
        compiler_params=_params(("arbitrary",)),
    )(dh1b, w_out, y_attn, gattn, *more)


def _attn_bwd(qn, kn, v, dy, probs, psinks, bkt, after=None):
    n_b = SEQ // BLK

    def body(q_ref, k_ref, v_ref, dy_ref, p_ref, ps_ref, bkt_ref,
             dq_ref, dk_ref, dv_ref, dtbl_ref, dsink_ref, dbias_ref, dsacc_ref):
        i = pl.program_id(0)

        @pl.when(i == 0)
        def _():
            dbias_ref[...] = jnp.zeros_like(dbias_ref)
            dsacc_ref[...] = jnp.zeros_like(dsacc_ref)
            dk_ref[...] = jnp.zeros_like(dk_ref)
            dv_ref[...] = jnp.zeros_like(dv_ref)

        kb, prev, cur = _band_rows(k_ref, i)
        vb, _, _ = _band_rows(v_ref, i)
        upper, _ = _band_masks(i)
        q = q_ref[...]
        dy = dy_ref[...]
        psink = ps_ref[...]
        lane = lax.broadcasted_iota(jnp.int32, (BLK, 128), 1)
        dsink = jnp.zeros((BLK, 128), F32)
        dqs, dks, dvs = [], [], []
        for g in range(N_HEADS // GQA_GROUP):
            kv = slice(HEAD_DIM * g, HEAD_DIM * (g + 1))
            qg = _stack_heads(q, g)
            dog = _stack_heads(dy, g).astype(BF16)
            pb = p_ref[g]
            pg = pb.astype(F32)
            dprobs = _fold(_dot_nt(dog, vb[:, kv]), upper)
            dvs.append(_dot_tn(_unfold(pb, upper), dog))
            dsum = jnp.sum(pg * dprobs, axis=-1, keepdims=True)
            dlogits = pg * (dprobs - dsum)
            for t in range(GQA_GROUP):
                dsink = jnp.where(lane == GQA_GROUP * g + t, -psink * dsum[BLK * t:BLK * (t + 1)], dsink)
            dbias_ref[g] += dlogits
            ds = _unfold(dlogits * (HEAD_DIM ** -0.5), upper).astype(BF16)
            dqs.append(_dot(ds, kb[:, kv]))
            dks.append(_dot_tn(ds, qg))
        dsacc_ref[...] += dsink
        dq_ref[...] = _unstack_heads(dqs)
        dkb = jnp.concatenate(dks, axis=-1)
        dvb = jnp.concatenate(dvs, axis=-1)
        dk_ref[pl.ds(prev, BLK), :] += dkb[:BLK]
        dk_ref[pl.ds(cur, BLK), :] += dkb[BLK:]
        dv_ref[pl.ds(prev, BLK), :] += dvb[:BLK]
        dv_ref[pl.ds(cur, BLK), :] += dvb[BLK:]

        @pl.when(i == n_b - 1)
        def _():
            bkt = bkt_ref[...]
            row8 = lax.broadcasted_iota(jnp.int32, (N_HEADS, 128), 0)
            lane8 = lax.broadcasted_iota(jnp.int32, (N_HEADS, 128), 1)
            acc = jnp.zeros((N_HEADS, 128), F32)
            for h in range(N_HEADS):
                rows = slice(BLK * (h % GQA_GROUP), BLK * (h % GQA_GROUP + 1))
                dbh = dbias_ref[h // GQA_GROUP, rows, :]
                for b in range(NUM_BUCKETS):
                    acc = jnp.where((row8 == h) & (lane8 == b), jnp.sum(jnp.where(bkt == b, dbh, 0.0)), acc)
            dsink_ref[...] = jnp.sum(dsacc_ref[...], axis=0, keepdims=True)
            dtbl_ref[...] = acc

    const = lambda shape: pl.BlockSpec(shape, lambda i: (0,) * len(shape))
    rows = lambda w: pl.BlockSpec((BLK, w), lambda i: (i, 0))
    n_g = N_HEADS // GQA_GROUP
    body, more_specs, more = _ordered_behind(body, 7, after)
    return pl.pallas_call(
        body, name="attn_bwd", grid=(n_b,),
        in_specs=[rows(ATTN_WIDTH), const((SEQ, KV_WIDTH)), const((SEQ, KV_WIDTH)), rows(ATTN_WIDTH),
                  pl.BlockSpec((None, n_g, GROUP_ROWS, BLK), lambda i: (i, 0, 0, 0)), rows(128),
                  const((BLK, BLK))] + more_specs,
        out_specs=[rows(ATTN_WIDTH), const((SEQ, KV_WIDTH)), const((SEQ, KV_WIDTH)), const((N_HEADS, 128)), const((1, 128))],
        out_shape=[jax.ShapeDtypeStruct((SEQ, ATTN_WIDTH), F32), jax.ShapeDtypeStruct((SEQ, KV_WIDTH), F32),
                   jax.ShapeDtypeStruct((SEQ, KV_WIDTH), F32), jax.ShapeDtypeStruct((N_HEADS, 128), F32),
                   jax.ShapeDtypeStruct((1, 128), F32)],
        scratch_shapes=[pltpu.VMEM((n_g, GROUP_ROWS, BLK), F32), pltpu.VMEM((BLK, 128), F32)],
        compiler_params=_params(("arbitrary",)),
    )(qn, kn, v, dy, probs, psinks, bkt, *more)


def _mix_in_bwd(x, dh1, proj, dycn, dqn, dkn, dv, w_in_t, conv_w, g1, gq, gk, gconv):
    tm = 256
    n_t = SEQ // tm
    halo_blocks = tm // SUBLANES

    def body(x_ref, dh1_ref, proj_ref, halo_ref, dycn_ref, dqn_ref, dkn_ref, dv_ref, w_ref, cw_ref,
             g1_ref, gq_ref, gk_ref, gc_ref,
             dx_ref, dproj_ref, dcw_ref, dgc_ref, dgq_ref, dgk_ref, dg1_ref, next_ref):
        i = pl.program_id(0)
        first_tile = i == n_t - 1

        @pl.when(i == 0)
        def _():
            for r in (dcw_ref, dgc_ref, dgq_ref, dgk_ref, dg1_ref, next_ref):
                r[...] = jnp.zeros_like(r)

        proj = proj_ref[...]
        hp = halo_ref[...]
        gate_b = proj[:, 0:CONV_WIDTH]
        gate_c = proj[:, CONV_WIDTH:2 * CONV_WIDTH]
        hc = proj[:, 2 * CONV_WIDTH:3 * CONV_WIDTH]
        a = gate_c * hc
        a_halo = jnp.where(first_tile, 0.0, hp[:, CONV_WIDTH:2 * CONV_WIDTH] * hp[:, 2 * CONV_WIDTH:3 * CONV_WIDTH])
        cw = _taps(cw_ref[...])
        cv, a2, a1 = _conv3(a, cw, a_halo)
        dyc, dgc = _rms_bwd(gate_b * cv, gc_ref[...], dycn_ref[...])
        dgc_ref[...] += dgc
        dcv = dyc * gate_b
        dcw_ref[...] += jnp.concatenate(
            [jnp.sum(dcv * a2, axis=0, keepdims=True), jnp.sum(dcv * a1, axis=0, keepdims=True),
             jnp.sum(dcv * a, axis=0, keepdims=True)], axis=0)
        da = _conv3_bwd_input(dcv, cw, next_ref[...])
        next_ref[...] = dcv[:SUBLANES]
        q0 = 3 * CONV_WIDTH
        k0 = q0 + ATTN_WIDTH
        dq, dgq = _head_norm_bwd(proj[:, q0:k0], gq_ref[...], dqn_ref[...], N_HEADS)
        dk, dgk = _head_norm_bwd(proj[:, k0:k0 + KV_WIDTH], gk_ref[...], dkn_ref[...], 2)
        dgq_ref[...] += dgq
        dgk_ref[...] += dgk
        dproj = jnp.concatenate([dyc * cv, da * hc, da * gate_c, dq, dk, dv_ref[...]], axis=-1).astype(BF16)
        dproj_ref[...] = dproj
        du1 = _dot(dproj, w_ref[...])
        xv = x_ref[...]
        dn, dg1 = _rms_bwd(xv, g1_ref[...], du1)
        dx_ref[...] = dh1_ref[...] + dn
        dg1_ref[...] += dg1

    rev = lambda i: n_t - 1 - i
    rows = lambda w: pl.BlockSpec((tm, w), lambda i: (rev(i), 0))
    const = lambda shape: pl.BlockSpec(shape, lambda i: (0,) * len(shape))
    halo = pl.BlockSpec((SUBLANES, IN_WIDTH), lambda i: (jnp.maximum(rev(i) * halo_blocks - 1, 0), 0))
    return pl.pallas_call(
        body, name="mix_in_bwd", grid=(n_t,),
        in_specs=[rows(D_MODEL), rows(D_MODEL), rows(IN_WIDTH), halo, rows(CONV_WIDTH), rows(ATTN_WIDTH), rows(KV_WIDTH),
                  rows(KV_WIDTH), const((IN_WIDTH, D_MODEL)), const((3, CONV_WIDTH)), const((1, D_MODEL)),
                  const((1, HEAD_DIM)), const((1, HEAD_DIM)), const((1, CONV_WIDTH))],
        out_specs=[rows(D_MODEL), rows(IN_WIDTH), const((3, CONV_WIDTH)), const((1, CONV_WIDTH)),
                   const((1, HEAD_DIM)), const((1, HEAD_DIM)), const((1, D_MODEL))],
        out_shape=[jax.ShapeDtypeStruct((SEQ, D_MODEL), F32), jax.ShapeDtypeStruct((SEQ, IN_WIDTH), BF16),
                   jax.ShapeDtypeStruct((3, CONV_WIDTH), F32),
                   jax.ShapeDtypeStruct((1, CONV_WIDTH), F32), jax.ShapeDtypeStruct((1, HEAD_DIM), F32),
                   jax.ShapeDtypeStruct((1, HEAD_DIM), F32), jax.ShapeDtypeStruct((1, D_MODEL), F32)],
        scratch_shapes=[pltpu.VMEM((SUBLANES, CONV_WIDTH), F32)],
        compiler_params=_params(("arbitrary",)),
    )(x, dh1, proj, proj, dycn, dqn, dkn, dv, w_in_t, conv_w, g1, gq, gk, gconv)


def _grad_w_in(dproj, u1, after=None):
    bw = 768

    def body(a_ref, b_ref, o_ref):
        o_ref[...] = _dot_tn(a_ref[...], b_ref[...]).astype(BF16)

    body, more_specs, more = _ordered_behind(body, 2, after)
    return pl.pallas_call(
        body, name="grad_w_in", grid=(IN_WIDTH // bw,),
        in_specs=[pl.BlockSpec((SEQ, bw), lambda k: (0, k)), pl.BlockSpec((SEQ, D_MODEL), lambda k: (0, 0))] + more_specs,
        out_specs=pl.BlockSpec((bw, D_MODEL), lambda k: (k, 0)),
        out_shape=jax.ShapeDtypeStruct((IN_WIDTH, D_MODEL), BF16),
        compiler_params=_params(("arbitrary",)),
    )(dproj, u1, *more)


def _adamw_math(w, g, m, v):
    m = ADAM_B1 * m + (1.0 - ADAM_B1) * g
    v = ADAM_B2 * v + (1.0 - ADAM_B2) * (g * g)
    m_hat = m / (1.0 - ADAM_B1 ** ADAM_STEP)
    v_hat = v / (1.0 - ADAM_B2 ** ADAM_STEP)
    return -ADAM_LR * (m_hat / (jnp.sqrt(v_hat) + ADAM_EPS) + ADAM_WD * w), m, v


_ROW_G1, _ROW_G2, _ROW_OUT_NORMS, _ROW_FFN_B, _ROW_GQ, _ROW_GK, _ROW_SINKS, _ROW_LOSS, _ROW_TABLE = 0, 1, 2, 3, 11, 12, 13, 14, 16
SMALL_ROWS, SMALL_COLS = 24, 1024
_SMALL_NAMES = ("norm_mix_g", "norm_ffn_g", "out_norm_conv_g", "out_norm_attn_g", "ffn_conv_b", "q_norm_g", "k_norm_g",
                "sinks", "rel_bias_table")


def _pack_small_grads(dg1, dg2, dgconv, dgattn, dfb, dgq, dgk, dsinks, dtbl_t, loss_acc):
    def body(dg1_ref, dg2_ref, dgc_ref, dga_ref, dfb_ref, dgq_ref, dgk_ref, ds_ref, dt_ref, loss_ref, o_ref):
        o_ref[...] = jnp.zeros_like(o_ref)
        o_ref[_ROW_G1:_ROW_G1 + 1, :] = dg1_ref[...]
        o_ref[_ROW_G2:_ROW_G2 + 1, :] = dg2_ref[...]
        o_ref[_ROW_OUT_NORMS:_ROW_OUT_NORMS + 1, 0:CONV_WIDTH] = dgc_ref[...]
        o_ref[_ROW_OUT_NORMS:_ROW_OUT_NORMS + 1, CONV_WIDTH:] = dga_ref[...]
        for k in range(N_DEV):
            o_ref[_ROW_FFN_B + k:_ROW_FFN_B + k + 1, 0:FFN_BLK] = dfb_ref[k // N_FFN_BLK, k % N_FFN_BLK]
        o_ref[_ROW_GQ:_ROW_GQ + 1, 0:HEAD_DIM] = dgq_ref[...]
        o_ref[_ROW_GK:_ROW_GK + 1, 0:HEAD_DIM] = dgk_ref[...]
        o_ref[_ROW_SINKS:_ROW_SINKS + 1, 0:128] = ds_ref[...]
        o_ref[_ROW_LOSS:_ROW_LOSS + 1, 0:128] = loss_ref[0:1, :]
        o_ref[_ROW_TABLE:_ROW_TABLE + N_HEADS, 0:128] = dt_ref[...]

    return pl.pallas_call(body, name="pack_small_grads", out_shape=jax.ShapeDtypeStruct((SMALL_ROWS, SMALL_COLS), F32))(
        dg1, dg2, dgconv, dgattn, dfb, dgq, dgk, dsinks, dtbl_t, loss_acc)


def _adamw_small(recv, params, after):
    names = _SMALL_NAMES
    n = len(names)

    def grad_of(g, name, k=None):
        if name == "norm_mix_g":
            return g[_ROW_G1:_ROW_G1 + 1, :]
        if name == "norm_ffn_g":
            return g[_ROW_G2:_ROW_G2 + 1, :]
        if name == "out_norm_conv_g":
            return g[_ROW_OUT_NORMS:_ROW_OUT_NORMS + 1, 0:CONV_WIDTH]
        if name == "out_norm_attn_g":
            return g[_ROW_OUT_NORMS:_ROW_OUT_NORMS + 1, CONV_WIDTH:]
        if name == "ffn_conv_b":
            return g[_ROW_FFN_B + k:_ROW_FFN_B + k + 1, 0:FFN_BLK]
        if name == "q_norm_g":
            return g[_ROW_GQ:_ROW_GQ + 1, 0:HEAD_DIM]
        if name == "k_norm_g":
            return g[_ROW_GK:_ROW_GK + 1, 0:HEAD_DIM]
        if name == "sinks":
            return g[_ROW_SINKS:_ROW_SINKS + 1, 0:N_HEADS]
        return g[_ROW_TABLE:_ROW_TABLE + N_HEADS, 0:NUM_BUCKETS]

    def body(r_ref, *refs):
        ins, outs, loss_ref = refs[:3 * n], refs[3 * n:7 * n], refs[7 * n]
        g = r_ref[0]
        for s in range(1, N_DEV):
            g = g + r_ref[s]
        loss_ref[...] = g[_ROW_LOSS:_ROW_LOSS + 1, 0:128]
        for i, name in enumerate(names):
            w_ref, m_ref, v_ref = ins[3 * i:3 * i + 3]
            o = outs[4 * i:4 * i + 4]
            cols = [slice(FFN_BLK * k, FFN_BLK * (k + 1)) for k in range(N_DEV)] if name == "ffn_conv_b" else [slice(None)]
            for k, cs in enumerate(cols):
                gk = grad_of(g, name, k)
                d, m2, v2 = _adamw_math(w_ref[:, cs], gk, m_ref[:, cs], v_ref[:, cs])
                o[0][:, cs], o[1][:, cs], o[2][:, cs], o[3][:, cs] = gk, d, m2, v2

    flat = [a for name in names for a in params[name]]
    body, more_specs, more = _ordered_behind(body, 1 + 3 * n, after)
    vmem = pl.BlockSpec(memory_space=pltpu.VMEM)
    out = pl.pallas_call(
        body, name="adamw_small",
        in_specs=[vmem] * (1 + 3 * n) + more_specs,
        out_shape=[jax.ShapeDtypeStruct(params[name][0].shape, F32) for name in names for _ in range(4)]
        + [jax.ShapeDtypeStruct((1, 128), F32)],
        compiler_params=pltpu.CompilerParams(vmem_limit_bytes=VMEM_LIMIT),
    )(recv, *flat, *more)
    return {name: tuple(out[4 * i:4 * i + 4]) for i, name in enumerate(names)}, out[4 * n]


def _adamw_direct(w, m, v, own, recv, me, name, row_blocks=1, after=None):
    rb = w.shape[0] // row_blocks
    cols = w.shape[1]

    def body(me_ref, w_ref, m_ref, v_ref, o_ref, r_ref, g_o, d_o, m_o, v_o):
        g = o_ref[...].astype(F32)
        for s in range(N_DEV - 1):
            g = g + r_ref[s].astype(F32)
        g_o[...] = g
        d_o[...], m_o[...], v_o[...] = _adamw_math(w_ref[...], g, m_ref[...], v_ref[...])

    blk = pl.BlockSpec((rb, cols), lambda i, me_ref: (i, 0))
    oblk = pl.BlockSpec((None, rb, cols), lambda i, me_ref: (me_ref[0], i, 0))
    rblk = pl.BlockSpec((N_DEV - 1, rb, cols), lambda i, me_ref: (0, i, 0))
    body, more_specs, more = _ordered_behind(body, 6, after)
    return pl.pallas_call(
        body, name=name,
        grid_spec=pltpu.PrefetchScalarGridSpec(num_scalar_prefetch=1, grid=(row_blocks,),
                                               in_specs=[blk, blk, blk, oblk, rblk] + more_specs, out_specs=[blk] * 4),
        out_shape=[jax.ShapeDtypeStruct(w.shape, F32)] * 4,
        compiler_params=_params(("arbitrary",)),
    )(me, w, m, v, own, recv, *more)


def _adamw(w, m, v, part, recv, chip, name, row_blocks=1, after=None):
    rb = w.shape[0] // row_blocks
    tail = w.shape[1:]
    zeros = (0,) * len(tail)

    def body(chip_ref, w_ref, m_ref, v_ref, p_ref, r_ref, g_o, d_o, m_o, v_o):
        g = p_ref[...].astype(F32)
        for s in range(3):
            g = g + r_ref[s].astype(F32)
        g_o[...] = g
        d_o[...], m_o[...], v_o[...] = _adamw_math(w_ref[...], g, m_ref[...], v_ref[...])

    blk = pl.BlockSpec((rb,) + tail, lambda i, chip_ref: (i,) + zeros)
    pblk = pl.BlockSpec((None, rb) + tail, lambda i, chip_ref: (chip_ref[0], i) + zeros)
    rblk = pl.BlockSpec((3, rb) + tail, lambda i, chip_ref: (0, i) + zeros)
    body, more_specs, more = _ordered_behind(body, 6, after)
    return pl.pallas_call(
        body, name=name,
        grid_spec=pltpu.PrefetchScalarGridSpec(num_scalar_prefetch=1, grid=(row_blocks,),
                                               in_specs=[blk, blk, blk, pblk, rblk] + more_specs, out_specs=[blk] * 4),
        out_shape=[jax.ShapeDtypeStruct(w.shape, F32)] * 4,
        compiler_params=_params(("arbitrary",)),
    )(chip, w, m, v, part, recv, *more)


def kernel(x, norm_mix_g, w_in, conv_w, q_norm_g, k_norm_g, rel_bias_table, sinks, out_norm_conv_g, out_norm_attn_g, w_out, norm_ffn_g, w_up, ffn_conv_w, ffn_conv_b, w_down, loss_target, m_norm_mix_g, m_w_in, m_conv_w, m_q_norm_g, m_k_norm_g, m_rel_bias_table, m_sinks, m_out_norm_conv_g, m_out_norm_attn_g, m_w_out, m_norm_ffn_g, m_w_up, m_ffn_conv_w, m_ffn_conv_b, m_w_down, v_norm_mix_g, v_w_in, v_conv_w, v_q_norm_g, v_k_norm_g, v_rel_bias_table, v_sinks, v_out_norm_conv_g, v_out_norm_attn_g, v_w_out, v_norm_ffn_g, v_w_up, v_ffn_conv_w, v_ffn_conv_b, v_w_down):
    p = dict(norm_mix_g=norm_mix_g, w_in=w_in, conv_w=conv_w, q_norm_g=q_norm_g, k_norm_g=k_norm_g,
             rel_bias_table=rel_bias_table, sinks=sinks, out_norm_conv_g=out_norm_conv_g, out_norm_attn_g=out_norm_attn_g,
             w_out=w_out, norm_ffn_g=norm_ffn_g, w_up=w_up, ffn_conv_w=ffn_conv_w, ffn_conv_b=ffn_conv_b, w_down=w_down)
    m = dict(norm_mix_g=m_norm_mix_g, w_in=m_w_in, conv_w=m_conv_w, q_norm_g=m_q_norm_g, k_norm_g=m_k_norm_g,
             rel_bias_table=m_rel_bias_table, sinks=m_sinks, out_norm_conv_g=m_out_norm_conv_g,
             out_norm_attn_g=m_out_norm_attn_g, w_out=m_w_out, norm_ffn_g=m_norm_ffn_g, w_up=m_w_up,
             ffn_conv_w=m_ffn_conv_w, ffn_conv_b=m_ffn_conv_b, w_down=m_w_down)
    v = dict(norm_mix_g=v_norm_mix_g, w_in=v_w_in, conv_w=v_conv_w, q_norm_g=v_q_norm_g, k_norm_g=v_k_norm_g,
             rel_bias_table=v_rel_bias_table, sinks=v_sinks, out_norm_conv_g=v_out_norm_conv_g,
             out_norm_attn_g=v_out_norm_attn_g, w_out=v_w_out, norm_ffn_g=v_norm_ffn_g, w_up=v_w_up,
             ffn_conv_w=v_ffn_conv_w, ffn_conv_b=v_ffn_conv_b, w_down=v_w_down)

    xs, tgt = x[0], loss_target[0]
    g1, g2, gq, gk, gconv, gattn = norm_mix_g, norm_ffn_g, q_norm_g, k_norm_g, out_norm_conv_g, out_norm_attn_g
    ix, iy, ic = _coords()
    core = ic.astype(jnp.int32).reshape(1)
    chip = (2 * ix + iy).astype(jnp.int32).reshape(1)
    me = _lin(ix, iy, ic).astype(jnp.int32).reshape(1)
    bkt = jnp.asarray(_bucket_map())
    tr = lambda a: a[0].T
    taps = lambda a: jnp.transpose(a, (1, 0, 2))
    tbl_t = rel_bias_table.T

    wi_l, cw_l, wo_l, wu_l, wd_l, fcw_l = _place_shards(
        me, [tr(w_in), taps(conv_w), w_out[0], tr(w_up), w_down[0], taps(ffn_conv_w)], [BF16, F32, BF16, BF16, BF16, F32])
    finish_a, token_a = _all_gather_split([wi_l, cw_l], "mixer", None)
    ffn_stage2, ffn_stage3, token_b = _all_gather_tree([wo_l, wu_l, fcw_l], "ffn", token_a)
    wi_g, cw_g = finish_a(token_b)
    w_in_t = wi_g.reshape(IN_WIDTH, D_MODEL)
    conv_w_f = jnp.transpose(cw_g[:, :, 0, :], (1, 0, 2)).reshape(3, CONV_WIDTH)

    proj, u1, ycn, qn, kn, vv = _mix_in_fwd(xs, g1, w_in_t, conv_w_f, gq, gk, gconv)
    token_b2 = ffn_stage2(ycn)
    y_attn, yan, probs, psinks = _attn_fwd(qn, kn, vv, tbl_t, sinks, bkt, gattn, after=token_b2)
    wo_g, wu_g, fcw_g = ffn_stage3(yan)
    finish_down, token_down = _all_gather_split([wd_l], "down", wo_g)
    w_out_f = wo_g.reshape(D_MODEL, D_MODEL)
    w_up_f = wu_g.reshape(2, N_FFN_BLK, FFN_BLK, D_MODEL)
    fcw_f = fcw_g.reshape(2, N_FFN_BLK, 3, 1, FFN_BLK)
    fcb = ffn_conv_b.reshape(2, N_FFN_BLK, 1, FFN_BLK)
    h1, u2, up, pre, act = _ffn_up(xs, ycn, yan, w_out_f, g2, w_up_f, fcw_f, fcb, after=token_down)
    (wd_g,) = finish_down(act)
    w_down_f = wd_g.reshape(N_FFN_BLK, FFN_BLK, D_MODEL)
    dh2, dh2b, loss_acc = _ffn_down(act, w_down_f, h1, tgt)

    dw_down = _grad_tn_blocked(act, dh2b, "grad_w_down", a_is_blocked=True).reshape(N_DEV, D_FF // N_DEV, D_MODEL)
    plan_d, slots_d = _scatter_plan(1)
    d_sem = _split_start("scatter_w_down_start", [dw_down], [lax.empty((N_DEV - 1,) + dw_down.shape[1:], BF16)],
                         plan_d, None)
    dup, dh1, dh1b, dfb, dfcw, dg2 = _ffn_bwd_token_major(dh2, dh2b, h1, g2, up, pre, w_up_f, fcw_f, w_down_f,
                                                          after=d_sem[4])
    dw_up = _grad_tn_blocked(dup.reshape(N_DEV, SEQ, FFN_BLK), u2, "grad_w_up", a_is_blocked=True)
    dw_out = _grad_tn([ycn, yan], dh1b, CONV_WIDTH, "grad_w_out").reshape(N_DEV, D_MODEL // N_DEV, D_MODEL)
    out_bwd = {}

    def behind_ffn(token):
        out_bwd["r"] = _out_bwd(dh1b, w_out_f, y_attn, gattn, after=token)
        return out_bwd["r"][0]

    finish_ffn, token_ffn = _reduce_scatter_split(
        [dw_up, dw_out, dfcw.reshape(N_DEV, 3, 1, FFN_BLK)], "ffn", core, behind_ffn)
    dycn, dy_attn, dgattn = out_bwd["r"]
    dqn, dkn, dv, dtbl_t, dsinks = _attn_bwd(qn, kn, vv, dy_attn, probs, psinks, bkt, after=token_ffn)
    dx, dproj, dcw, dgconv, dgq, dgk, dg1 = _mix_in_bwd(xs, dh1, proj, dycn, dqn, dkn, dv, w_in_t, conv_w_f,
                                                             g1, gq, gk, gconv)
    packed = _pack_small_grads(dg1, dg2, dgconv, dgattn, dfb, dgq, dgk, dsinks, dtbl_t, loss_acc)
    plan_s, slots_s = _broadcast_plan()
    s_sem, r_sem, src_s, land_s, token_s = _split_start(
        "gather_small_start", [packed], [jnp.broadcast_to(packed[None], (N_DEV,) + packed.shape)], plan_s, None)
    dw_in_t = _grad_w_in(dproj, u1, after=token_s).reshape(N_DEV, IN_WIDTH // N_DEV, D_MODEL)
    dcw_b = jnp.transpose(dcw.reshape(3, N_DEV, 1, CONV_WIDTH // N_DEV), (1, 0, 2, 3))
    adam = {}
    ffn_got = {}

    def behind_mixer(token):
        ffn_got["r"] = finish_ffn(token)
        return ffn_got["r"][1][0]

    finish_mixer, token_mixer = _reduce_scatter_split([dw_in_t, dcw_b], "mixer", core, behind_mixer)
    (p_wu, p_wo, p_fcw), (r_wu, r_wo, r_fcw) = ffn_got["r"]
    (own_wd,), (r_wd,) = _split_wait("scatter_w_down_wait", d_sem[0], d_sem[1], d_sem[2], d_sem[3], plan_d, slots_d,
                                     token_mixer)
    adam["w_down"] = _adamw_direct(w_down[0], m_w_down[0], v_w_down[0], own_wd, r_wd, me, "adamw_w_down", row_blocks=2)
    adam_up = _adamw(tr(w_up), tr(m_w_up), tr(v_w_up), p_wu, r_wu, chip, "adamw_w_up", row_blocks=4,
                     after=adam["w_down"][0])
    adam["w_out"] = _adamw(w_out[0], m_w_out[0], v_w_out[0], p_wo, r_wo, chip, "adamw_w_out", after=adam_up[0])
    adam_fcw = _adamw(taps(ffn_conv_w), taps(m_ffn_conv_w), taps(v_ffn_conv_w), p_fcw, r_fcw, chip, "adamw_ffn_conv_w",
                      after=adam["w_out"][0])
    _, (r_small,) = _split_wait("gather_small_wait", s_sem, r_sem, src_s, land_s, plan_s, slots_s, adam_fcw[0])
    small_in = {k: (p[k], m[k], v[k]) for k in _SMALL_NAMES}
    small_in["rel_bias_table"] = (tbl_t, m_rel_bias_table.T, v_rel_bias_table.T)
    small_out, loss_row = _adamw_small(r_small, small_in, None)
    (p_wi, p_cw), (r_wi, r_cw) = finish_mixer(loss_row)
    adam_in = _adamw(tr(w_in), tr(m_w_in), tr(v_w_in), p_wi, r_wi, chip, "adamw_w_in")
    adam_cw = _adamw(taps(conv_w), taps(m_conv_w), taps(v_conv_w), p_cw, r_cw, chip, "adamw_conv_w")

    res = {k: tuple(a[None] for a in t) for k, t in adam.items()}
    res["w_up"] = tuple(a.T[None] for a in adam_up)
    res["w_in"] = tuple(a.T[None] for a in adam_in)
    res["ffn_conv_w"] = tuple(taps(a) for a in adam_fcw)
    res["conv_w"] = tuple(taps(a) for a in adam_cw)
    res.update(small_out)
    res["rel_bias_table"] = tuple(a.T for a in small_out["rel_bias_table"])
    loss = loss_row[0, 0]
    order = ("norm_mix_g", "w_in", "conv_w", "q_norm_g", "k_norm_g", "rel_bias_table", "sinks", "out_norm_conv_g",
             "out_norm_attn_g", "w_out", "norm_ffn_g", "w_up", "ffn_conv_w", "ffn_conv_b", "w_down")
    return (loss, dx[None], *[res[k][0] for k in order], *[res[k][1] for k in order],
            *[res[k][2] for k in order], *[res[k][3] for k in order])
```

```python
import functools
import math

import numpy as np
import jax
import jax.numpy as jnp
from jax import lax
from jax.experimental import pallas as pl
from jax.experimental.pallas import tpu as pltpu

F32 = jnp.float32
BF16 = jnp.bfloat16

SEQ = 2048
D_MODEL = 1024
CONV_WIDTH = 512
ATTN_WIDTH = 512
KV_WIDTH = 128
HEAD_DIM = 64
N_HEADS = 8
GQA_GROUP = 4
IN_WIDTH = 2304
D_FF = 2816
BLK = 128
NUM_BUCKETS = 32
EPS = 1e-6
NEG_INF = -1e30
ADAM_LR = 0.001
ADAM_B1 = 0.9
ADAM_B2 = 0.999
ADAM_EPS = 1e-08
ADAM_WD = 0.01
ADAM_STEP = 10

N_DEV = 8
FFN_BLK = 2 * D_FF // N_DEV
N_FFN_BLK = D_FF // FFN_BLK
SUBLANES = 8
VMEM_LIMIT = 56 * 1024 * 1024

_MESH = pl.DeviceIdType.MESH
_ANY = pl.BlockSpec(memory_space=pl.ANY)


def _params(sem):
    return pltpu.CompilerParams(dimension_semantics=sem, vmem_limit_bytes=VMEM_LIMIT)


def _ordered_behind(body, pos, after):
    if after is None:
        return body, [], []
    return (lambda *refs: body(*refs[:pos], *refs[pos + 1:])), [_ANY], [after]


def _dot(a, b):
    return jnp.dot(a, b, preferred_element_type=F32)


def _dot_nt(a, b):
    return lax.dot_general(a, b, (((1,), (1,)), ((), ())), preferred_element_type=F32)


def _dot_tn(a, b):
    return lax.dot_general(a, b, (((0,), (0,)), ((), ())), preferred_element_type=F32)


def _shift_down(x, s, halo):
    r = pltpu.roll(x, s, axis=0)
    hr = pltpu.roll(halo, s, axis=0)
    row = lax.broadcasted_iota(jnp.int32, halo.shape, 0)
    top = jnp.where(row < s, hr, r[:SUBLANES])
    return jnp.concatenate([top, r[SUBLANES:]], axis=0)


def _shift_up(x, s, halo):
    n = x.shape[0]
    r = pltpu.roll(x, n - s, axis=0)
    hr = pltpu.roll(halo, SUBLANES - s, axis=0)
    row = lax.broadcasted_iota(jnp.int32, halo.shape, 0)
    bot = jnp.where(row >= SUBLANES - s, hr, r[n - SUBLANES:])
    return jnp.concatenate([r[:n - SUBLANES], bot], axis=0)


def _taps(w):
    return (w[0], w[1], w[2]) if len(w.shape) == 3 else (w[0:1], w[1:2], w[2:3])


def _conv3(x, w, halo):
    x2 = _shift_down(x, 2, halo)
    x1 = _shift_down(x, 1, halo)
    return x2 * w[0] + x1 * w[1] + x * w[2], x2, x1


def _conv3_bwd_input(dy, w, halo_next):
    return dy * w[2] + _shift_up(dy, 1, halo_next) * w[1] + _shift_up(dy, 2, halo_next) * w[0]


def _rstd(x):
    return lax.rsqrt(jnp.mean(x * x, axis=-1, keepdims=True) + EPS)


def _rms_bwd(x, g, dy):
    r = _rstd(x)
    n = x * r
    dn = dy * g
    dx = r * (dn - n * jnp.mean(dn * n, axis=-1, keepdims=True))
    return dx, jnp.sum(dy * n, axis=0, keepdims=True)


def _head_mean(x):
    width = x.shape[-1]
    ri = lax.broadcasted_iota(jnp.int32, (width, width), 0) // HEAD_DIM
    ci = lax.broadcasted_iota(jnp.int32, (width, width), 1) // HEAD_DIM
    ones = jnp.where(ri == ci, 1.0, 0.0).astype(BF16)
    hi = x.astype(BF16)
    lo = (x - hi.astype(F32)).astype(BF16)
    return (_dot(hi, ones) + _dot(lo, ones)) * (1.0 / HEAD_DIM)


def _head_norm(x, g, heads):
    return x * lax.rsqrt(_head_mean(x * x) + EPS) * jnp.tile(g, (1, heads))


def _head_norm_bwd(x, g, dy, heads):
    r = lax.rsqrt(_head_mean(x * x) + EPS)
    n = x * r
    dn = dy * jnp.tile(g, (1, heads))
    dx = r * (dn - n * _head_mean(dn * n))
    per_lane = jnp.sum(dy * n, axis=0, keepdims=True)
    dg = per_lane[:, 0:HEAD_DIM]
    for h in range(1, heads):
        dg = dg + per_lane[:, HEAD_DIM * h:HEAD_DIM * (h + 1)]
    return dx, dg


def _bucket_map():
    q = np.arange(BLK)[:, None]
    j = np.arange(BLK)[None, :]
    n = np.where(j > q, q + BLK - j, q - j)
    nf = np.maximum(n, 1).astype(np.float32)
    max_exact = NUM_BUCKETS // 2
    large = max_exact + (np.log(nf / max_exact) / math.log(BLK / max_exact) * (NUM_BUCKETS - max_exact)).astype(np.int32)
    large = np.minimum(large, NUM_BUCKETS - 1)
    return np.where(n < max_exact, n, large).astype(np.int32)


def _coords():
    return lax.axis_index("x"), lax.axis_index("y"), lax.axis_index("c")


def _lin(px, py, pc):
    return 4 * px + 2 * py + pc


_HBM = pl.BlockSpec(memory_space=pltpu.HBM)
_SEM = pl.BlockSpec(memory_space=pltpu.SEMAPHORE)
_EFFECT = pltpu.SideEffectType.DATAFLOW_SIDE_EFFECTING


def _in_hbm(a):
    return pltpu.with_memory_space_constraint(a, pltpu.HBM)


def _split_start(name, srcs, lands, plan, after):
    ns, nl = len(srcs), len(lands)
    n_copies = len(plan(0, 0, 0))
    n_after = 0 if after is None else 1

    def body(*refs):
        src_refs, land_refs = refs[:ns + nl], refs[ns:ns + nl]
        send_sems, recv_sems = refs[ns + nl + n_after], refs[ns + nl + n_after + 1]
        token = refs[-1]
        for k, (a, s_slot, l, d_slot, dev) in enumerate(plan(*_coords())):
            src = src_refs[a] if s_slot is None else src_refs[a].at[s_slot]
            pltpu.make_async_remote_copy(src_ref=src, dst_ref=land_refs[l].at[d_slot], send_sem=send_sems.at[k],
                                         recv_sem=recv_sems.at[k], device_id=dev, device_id_type=_MESH).start()
        token[...] = jnp.zeros_like(token)

    arrs = list(srcs) + list(lands)
    out = pl.pallas_call(
        body, name=name,
        out_shape=(pltpu.SemaphoreType.DMA((n_copies,)), pltpu.SemaphoreType.DMA((n_copies,)),
                   *[pltpu.HBM(a.shape, a.dtype) for a in arrs], jax.ShapeDtypeStruct((SUBLANES, 128), F32)),
        in_specs=[_HBM] * (ns + nl) + [_ANY] * n_after,
        out_specs=(_SEM, _SEM, *[_HBM] * (ns + nl), pl.BlockSpec(memory_space=pltpu.VMEM)),
        input_output_aliases={i: 2 + i for i in range(ns + nl)},
        compiler_params=pltpu.CompilerParams(has_side_effects=_EFFECT),
    )(*[_in_hbm(a) for a in arrs], *([] if after is None else [after]))
    return out[0], out[1], list(out[2:2 + ns]), list(out[2 + ns:2 + ns + nl]), out[-1]


def _split_wait(name, send_sems, recv_sems, srcs, lands, plan, recv_slots, after):
    ns, nl = len(srcs), len(lands)

    def body(*refs):
        src_refs, land_refs = refs[:ns + nl], refs[ns:ns + nl]
        send_sems, recv_sems = refs[ns + nl], refs[ns + nl + 1]
        coords = _coords()
        slots = recv_slots(*coords)
        for k, (a, s_slot, l, _, dev) in enumerate(plan(*coords)):
            src = src_refs[a] if s_slot is None else src_refs[a].at[s_slot]
            cp = pltpu.make_async_remote_copy(src_ref=src, dst_ref=land_refs[l].at[slots[k]], send_sem=send_sems.at[k],
                                              recv_sem=recv_sems.at[k], device_id=dev, device_id_type=_MESH)
            cp.wait_send()
            cp.wait_recv()

    arrs = list(srcs) + list(lands)
    out = pl.pallas_call(
        body, name=name,
        out_shape=tuple(pltpu.HBM(a.shape, a.dtype) for a in arrs),
        in_specs=[_HBM] * (ns + nl) + [_SEM, _SEM, _ANY],
        out_specs=tuple([_HBM] * (ns + nl)),
        input_output_aliases={i: i for i in range(ns + nl)},
        compiler_params=pltpu.CompilerParams(has_side_effects=_EFFECT),
    )(*arrs, send_sems, recv_sems, after)
    return list(out[:ns]), list(out[ns:])


def _chips(x, y):
    return [(1 - x, y), (x, 1 - y), (1 - x, 1 - y)]


def _gather_plan_ici(n):
    def plan(x, y, c):
        me = _lin(x, y, c)
        out = []
        for a in range(n):
            out.append((a, me, a, me, (x, y, 1 - c)))
            out += [(a, me, a, me, (cx, cy, c)) for cx, cy in _chips(x, y)]
        return out

    def recv_slots(x, y, c):
        out = []
        for _ in range(n):
            out.append(_lin(x, y, 1 - c))
            out += [_lin(cx, cy, c) for cx, cy in _chips(x, y)]
        return out

    return plan, recv_slots


def _gather_plan_d2d(n):
    def plan(x, y, c):
        return [(a, _lin(cx, cy, c), a, _lin(cx, cy, c), (x, y, 1 - c)) for a in range(n) for cx, cy in _chips(x, y)]

    def recv_slots(x, y, c):
        return [_lin(cx, cy, 1 - c) for _ in range(n) for cx, cy in _chips(x, y)]

    return plan, recv_slots


def _all_gather_split(lands, tag, after):
    n = len(lands)
    plan1, slots1 = _gather_plan_ici(n)
    s1, r1, _, lands, token = _split_start(f"gather_{tag}_ici_start", [], lands, plan1, after)

    def finish(after):
        _, got = _split_wait(f"gather_{tag}_ici_wait", s1, r1, [], lands, plan1, slots1, after)
        plan2, slots2 = _gather_plan_d2d(n)
        s2, r2, _, got, token2 = _split_start(f"gather_{tag}_d2d_start", [], got, plan2, None)
        return _split_wait(f"gather_{tag}_d2d_wait", s2, r2, [], got, plan2, slots2, token2)[1]

    return finish, token


def _all_gather_tree(lands, tag, after):
    n = len(lands)

    def plan1(x, y, c):
        me = _lin(x, y, c)
        return [(a, me, a, me, dev) for a in range(n) for dev in ((x, y, 1 - c), (1 - x, y, c), (x, 1 - y, c))]

    def slots1(x, y, c):
        return [s for _ in range(n) for s in (_lin(x, y, 1 - c), _lin(1 - x, y, c), _lin(x, 1 - y, c))]

    def plan2(x, y, c):
        from_x, from_y = _lin(1 - x, y, c), _lin(x, 1 - y, c)
        north = c == 1
        passed = jnp.where(north, from_x, from_y)
        onward = (jnp.where(north, x, 1 - x), jnp.where(north, 1 - y, y), c)
        sib = (x, y, 1 - c)
        return [cp for a in range(n) for cp in ((a, passed, a, passed, onward), (a, from_x, a, from_x, sib),
                                                (a, from_y, a, from_y, sib))]

    def slots2(x, y, c):
        return [s for _ in range(n) for s in (_lin(1 - x, 1 - y, c), _lin(1 - x, y, 1 - c), _lin(x, 1 - y, 1 - c))]

    def plan3(x, y, c):
        diag = _lin(1 - x, 1 - y, c)
        return [(a, diag, a, diag, (x, y, 1 - c)) for a in range(n)]

    def slots3(x, y, c):
        return [_lin(1 - x, 1 - y, 1 - c)] * n

    s1, r1, _, lands, token = _split_start(f"gather_{tag}_1_start", [], lands, plan1, after)
    state = {}

    def stage2(after):
        _, got = _split_wait(f"gather_{tag}_1_wait", s1, r1, [], lands, plan1, slots1, after)
        state["s"], state["r"], _, state["lands"], token2 = _split_start(f"gather_{tag}_2_start", [], got, plan2, None)
        return token2

    def stage3(after):
        _, got = _split_wait(f"gather_{tag}_2_wait", state["s"], state["r"], [], state["lands"], plan2, slots2, after)
        s3, r3, _, got, token3 = _split_start(f"gather_{tag}_3_start", [], got, plan3, None)
        return _split_wait(f"gather_{tag}_3_wait", s3, r3, [], got, plan3, slots3, token3)[1]

    return stage2, stage3, token


_CHIP_LIST = ((0, 0), (0, 1), (1, 0), (1, 1))


def _reduce_plan_d2d(n):
    def plan(x, y, c):
        return [(a, _lin(qx, qy, 1 - c), a, q, (x, y, 1 - c)) for a in range(n) for q, (qx, qy) in enumerate(_CHIP_LIST)]

    def recv_slots(x, y, c):
        return [q for _ in range(n) for q in range(4)]

    return plan, recv_slots


def _reduce_plan_ici(n):
    def plan(x, y, c):
        return [(a, 2 * cx + cy, a, j, (cx, cy, c)) for a in range(n) for j, (cx, cy) in enumerate(_chips(x, y))]

    def recv_slots(x, y, c):
        return [j for _ in range(n) for j in range(3)]

    return plan, recv_slots


def _peers(x, y, c):
    return [(1 - x if r & 4 else x, 1 - y if r & 2 else y, 1 - c if r & 1 else c) for r in range(1, N_DEV)]


def _scatter_plan(n):
    def plan(x, y, c):
        return [(a, _lin(*peer), a, r, peer) for a in range(n) for r, peer in enumerate(_peers(x, y, c))]

    def recv_slots(x, y, c):
        return [r for _ in range(n) for r in range(N_DEV - 1)]

    return plan, recv_slots


def _broadcast_plan():
    peers = _peers

    def plan(x, y, c):
        return [(0, None, 0, _lin(x, y, c), peer) for peer in peers(x, y, c)]

    def recv_slots(x, y, c):
        return [_lin(*peer) for peer in peers(x, y, c)]

    return plan, recv_slots


def _chip_partial(grads, recvd, core, name):
    n = len(grads)

    def body(c_ref, *refs):
        for a in range(n):
            g_ref, r_ref, o_ref = refs[a], refs[n + a], refs[2 * n + a]
            o_ref[...] = (g_ref[...].astype(F32) + r_ref[...].astype(F32)).astype(o_ref.dtype)

    def blk(a, own):
        zeros = (0,) * (a.ndim - 1)
        return pl.BlockSpec((None,) + a.shape[1:],
                            (lambda q, c_ref: (2 * q + c_ref[0],) + zeros) if own else (lambda q, c_ref: (q,) + zeros))

    return pl.pallas_call(
        body, name=name,
        grid_spec=pltpu.PrefetchScalarGridSpec(
            num_scalar_prefetch=1, grid=(4,),
            in_specs=[blk(a, True) for a in grads] + [blk(a, False) for a in recvd],
            out_specs=[blk(a, False) for a in recvd]),
        out_shape=[jax.ShapeDtypeStruct(a.shape, a.dtype) for a in recvd],
        compiler_params=_params(("arbitrary",)),
    )(core, *grads, *recvd)


def _reduce_scatter_split(grads, tag, core, behind):
    n = len(grads)
    plan1, slots1 = _reduce_plan_d2d(n)
    lands1 = [lax.empty((4,) + a.shape[1:], a.dtype) for a in grads]
    s1, r1, srcs1, lands1, token1 = _split_start(f"reduce_{tag}_d2d_start", grads, lands1, plan1, None)
    own, got = _split_wait(f"reduce_{tag}_d2d_wait", s1, r1, srcs1, lands1, plan1, slots1, behind(token1))
    parts = _chip_partial(own, got, core, f"reduce_{tag}_partial")
    plan2, slots2 = _reduce_plan_ici(n)
    lands2 = [lax.empty((3,) + a.shape[1:], a.dtype) for a in grads]
    s2, r2, srcs2, lands2, token2 = _split_start(f"reduce_{tag}_ici_start", parts, lands2, plan2, None)

    def finish(after):
        return _split_wait(f"reduce_{tag}_ici_wait", s2, r2, srcs2, lands2, plan2, slots2, after)

    return finish, token2


def _place_shards(me, shards, dtypes, name, after=None):
    n = len(shards)

    def body(me_ref, *refs):
        for a in range(n):
            refs[n + a][...] = refs[a][...].astype(dtypes[a])

    full = lambda s: pl.BlockSpec(s.shape, lambda i, me_ref: (0,) * s.ndim)
    slot = lambda s: pl.BlockSpec((None,) + s.shape, lambda i, me_ref: (me_ref[0],) + (0,) * s.ndim)
    body, more_specs, more = _ordered_behind(body, 1 + n, after)
    return pl.pallas_call(
        body, name=name,
        grid_spec=pltpu.PrefetchScalarGridSpec(num_scalar_prefetch=1, grid=(1,),
                                               in_specs=[full(s) for s in shards] + more_specs,
                                               out_specs=[slot(s) for s in shards]),
        out_shape=[jax.ShapeDtypeStruct((N_DEV,) + s.shape, d) for s, d in zip(shards, dtypes)],
        compiler_params=_params(("arbitrary",)),
    )(me, *shards, *more)


def _mix_in_fwd(x, g1, w_in_t, conv_w, gq, gk, gconv):
    tm = 512
    n_t = SEQ // tm

    def body(x_ref, g1_ref, w_ref, cw_ref, gq_ref, gk_ref, gc_ref,
             proj_ref, u1_ref, ycn_ref, qn_ref, kn_ref, v_ref, halo_ref):
        @pl.when(pl.program_id(0) == 0)
        def _():
            halo_ref[...] = jnp.zeros_like(halo_ref)

        xv = x_ref[...]
        u = (xv * _rstd(xv) * g1_ref[...]).astype(BF16)
        u1_ref[...] = u
        proj = _dot_nt(u, w_ref[...])
        proj_ref[...] = proj
        gate_b = proj[:, 0:CONV_WIDTH]
        a = proj[:, CONV_WIDTH:2 * CONV_WIDTH] * proj[:, 2 * CONV_WIDTH:3 * CONV_WIDTH]
        cv, _, _ = _conv3(a, _taps(cw_ref[...]), halo_ref[...])
        halo_ref[...] = a[tm - SUBLANES:]
        yc = gate_b * cv
        ycn_ref[...] = (yc * _rstd(yc) * gc_ref[...]).astype(BF16)
        q0 = 3 * CONV_WIDTH
        qn_ref[...] = _head_norm(proj[:, q0:q0 + ATTN_WIDTH], gq_ref[...], N_HEADS).astype(BF16)
        k0 = q0 + ATTN_WIDTH
        kn_ref[...] = _head_norm(proj[:, k0:k0 + KV_WIDTH], gk_ref[...], 2).astype(BF16)
        v_ref[...] = proj[:, k0 + KV_WIDTH:k0 + 2 * KV_WIDTH].astype(BF16)

    const = lambda shape: pl.BlockSpec(shape, lambda i: (0,) * len(shape))
    rows = lambda w: pl.BlockSpec((tm, w), lambda i: (i, 0))
    return pl.pallas_call(
        body, name="mix_in_fwd", grid=(n_t,),
        in_specs=[rows(D_MODEL), const((1, D_MODEL)), const((IN_WIDTH, D_MODEL)), const((3, CONV_WIDTH)),
                  const((1, HEAD_DIM)), const((1, HEAD_DIM)), const((1, CONV_WIDTH))],
        out_specs=[rows(IN_WIDTH), rows(D_MODEL), rows(CONV_WIDTH), rows(ATTN_WIDTH), rows(KV_WIDTH), rows(KV_WIDTH)],
        out_shape=[jax.ShapeDtypeStruct((SEQ, IN_WIDTH), F32), jax.ShapeDtypeStruct((SEQ, D_MODEL), BF16),
                   jax.ShapeDtypeStruct((SEQ, CONV_WIDTH), BF16),
                   jax.ShapeDtypeStruct((SEQ, ATTN_WIDTH), BF16), jax.ShapeDtypeStruct((SEQ, KV_WIDTH), BF16),
                   jax.ShapeDtypeStruct((SEQ, KV_WIDTH), BF16)],
        scratch_shapes=[pltpu.VMEM((SUBLANES, CONV_WIDTH), F32)],
        compiler_params=_params(("arbitrary",)),
    )(x, g1, w_in_t, conv_w, gq, gk, gconv)


GROUP_ROWS = GQA_GROUP * BLK


def _band_bias(tbl_ref, bkt, bias_ref):
    for h in range(N_HEADS):
        acc = jnp.zeros(bkt.shape, F32)
        for b in range(NUM_BUCKETS):
            acc = jnp.where(bkt == b, tbl_ref[h, b], acc)
        bias_ref[h // GQA_GROUP, BLK * (h % GQA_GROUP):BLK * (h % GQA_GROUP + 1), :] = acc


def _band_masks(i):
    qi = lax.broadcasted_iota(jnp.int32, (GROUP_ROWS, BLK), 0) & (BLK - 1)
    ji = lax.broadcasted_iota(jnp.int32, (GROUP_ROWS, BLK), 1)
    upper = ji > qi
    return upper, upper & (i == 0)


def _stack_heads(x, g):
    return jnp.concatenate([x[:, HEAD_DIM * h:HEAD_DIM * (h + 1)] for h in range(GQA_GROUP * g, GQA_GROUP * (g + 1))], axis=0)


def _unstack_heads(groups):
    return jnp.concatenate([p[BLK * t:BLK * (t + 1)] for p in groups for t in range(GQA_GROUP)], axis=-1)


def _per_head_rows(vals):
    row = lax.broadcasted_iota(jnp.int32, (GROUP_ROWS, 1), 0)
    col = jnp.full((GROUP_ROWS, 1), vals[GQA_GROUP - 1], F32)
    for t in range(GQA_GROUP - 2, -1, -1):
        col = jnp.where(row < BLK * (t + 1), vals[t], col)
    return col


def _band_rows(ref, i):
    prev = pl.multiple_of(jnp.maximum(i - 1, 0) * BLK, BLK)
    cur = pl.multiple_of(i * BLK, BLK)
    return jnp.concatenate([ref[pl.ds(prev, BLK), :], ref[pl.ds(cur, BLK), :]], axis=0), prev, cur


def _fold(band, upper):
    return jnp.where(upper, band[:, :BLK], band[:, BLK:])


def _unfold(tile, upper):
    return jnp.concatenate([jnp.where(upper, tile, 0.0), jnp.where(upper, 0.0, tile)], axis=1)


def _head_probs(qh, kh, bias, upper, dead, sink):
    logits = _fold(_dot_nt(qh, kh), upper) * (HEAD_DIM ** -0.5) + bias
    logits = jnp.where(dead, NEG_INF, logits)
    m = jnp.maximum(jnp.max(logits, axis=-1, keepdims=True), sink)
    p = jnp.exp(logits - m)
    es = jnp.exp(sink - m)
    den = jnp.sum(p, axis=-1, keepdims=True) + es
    return p / den, es / den


def _attn_fwd(qn, kn, v, tbl, sinks, bkt, gattn, after=None):
    n_b = SEQ // BLK

    def body(q_ref, k_ref, v_ref, tbl_ref, sink_ref, bkt_ref, ga_ref, y_ref, yn_ref, p_ref, ps_ref, bias_ref):
        i = pl.program_id(0)

        @pl.when(i == 0)
        def _():
            _band_bias(tbl_ref, bkt_ref[...], bias_ref)

        kb, _, _ = _band_rows(k_ref, i)
        vb, _, _ = _band_rows(v_ref, i)
        upper, dead = _band_masks(i)
        q = q_ref[...]
        lane = lax.broadcasted_iota(jnp.int32, (BLK, 128), 1)
        outs = []
        psinks = jnp.zeros((BLK, 128), F32)
        for g in range(N_HEADS // GQA_GROUP):
            kv = slice(HEAD_DIM * g, HEAD_DIM * (g + 1))
            sink = _per_head_rows([sink_ref[0, GQA_GROUP * g + t] for t in range(GQA_GROUP)])
            probs, psink = _head_probs(_stack_heads(q, g), kb[:, kv], bias_ref[g], upper, dead, sink)
            p_ref[g] = probs.astype(BF16)
            for t in range(GQA_GROUP):
                psinks = jnp.where(lane == GQA_GROUP * g + t, psink[BLK * t:BLK * (t + 1)], psinks)
            outs.append(_dot(_unfold(probs, upper).astype(BF16), vb[:, kv]))
        ps_ref[...] = psinks
        y = _unstack_heads(outs)
        y_ref[...] = y
        yn_ref[...] = (y * _rstd(y) * ga_ref[...]).astype(BF16)

    const = lambda shape: pl.BlockSpec(shape, lambda i: (0,) * len(shape))
    rows = lambda w: pl.BlockSpec((BLK, w), lambda i: (i, 0))
    smem = pl.BlockSpec(memory_space=pltpu.SMEM)
    body, more_specs, more = _ordered_behind(body, 7, after)
    return pl.pallas_call(
        body, name="attn_fwd", grid=(n_b,),
        in_specs=[rows(ATTN_WIDTH), const((SEQ, KV_WIDTH)), const((SEQ, KV_WIDTH)), smem, smem,
                  const((BLK, BLK)), const((1, ATTN_WIDTH))] + more_specs,
        out_specs=[rows(ATTN_WIDTH), rows(ATTN_WIDTH),
                   pl.BlockSpec((None, N_HEADS // GQA_GROUP, GROUP_ROWS, BLK), lambda i: (i, 0, 0, 0)), rows(128)],
        out_shape=[jax.ShapeDtypeStruct((SEQ, ATTN_WIDTH), F32), jax.ShapeDtypeStruct((SEQ, ATTN_WIDTH), BF16),
                   jax.ShapeDtypeStruct((n_b, N_HEADS // GQA_GROUP, GROUP_ROWS, BLK), BF16),
                   jax.ShapeDtypeStruct((SEQ, 128), F32)],
        scratch_shapes=[pltpu.VMEM((N_HEADS // GQA_GROUP, GROUP_ROWS, BLK), F32)],
        compiler_params=_params(("arbitrary",)),
    )(qn, kn, v, tbl, sinks, bkt, gattn, *more)


def _ffn_up(x, ycn, yan, w_out, g2, w_up, fcw, fcb, after=None):
    tm = 512
    n_t = SEQ // tm

    def body(x_ref, ycn_ref, yan_ref, wo_ref, g2_ref, wu_ref, cw_ref, b_ref,
             h1_ref, u2_ref, up_ref, pre_ref, act_ref, halo_ref):
        i, j = pl.program_id(0), pl.program_id(1)

        @pl.when(j == 0)
        def _():
            h1 = x_ref[...] + _dot(ycn_ref[...], wo_ref[0:CONV_WIDTH, :]) + _dot(yan_ref[...], wo_ref[CONV_WIDTH:, :])
            h1_ref[...] = h1
            u2_ref[...] = (h1 * _rstd(h1) * g2_ref[...]).astype(BF16)

        u2 = u2_ref[...]
        pre = []
        for s in range(2):
            up = _dot_nt(u2, wu_ref[s])
            up_ref[s] = up.astype(BF16)
            halo = jnp.where(i == 0, 0.0, halo_ref[s, j])
            pre.append(_conv3(up, _taps(cw_ref.at[s]), halo)[0] + b_ref[s])
            pre_ref[s] = pre[s].astype(BF16)
            halo_ref[s, j] = up[tm - SUBLANES:]
        g, val = pre
        act_ref[...] = (g * jax.nn.sigmoid(g) * val).astype(BF16)

    rows = lambda w: pl.BlockSpec((tm, w), lambda i, j: (i, 0))
    const = lambda shape: pl.BlockSpec(shape, lambda i, j: (0,) * len(shape))
    pair = lambda *s: pl.BlockSpec((2, None) + s, lambda i, j: (0, j) + (0,) * len(s))
    upb = pl.BlockSpec((2, None, tm, FFN_BLK), lambda i, j: (0, j, i, 0))
    body, more_specs, more = _ordered_behind(body, 8, after)
    return pl.pallas_call(
        body, name="ffn_up", grid=(n_t, N_FFN_BLK),
        in_specs=[rows(D_MODEL), rows(CONV_WIDTH), rows(ATTN_WIDTH), const((D_MODEL, D_MODEL)), const((1, D_MODEL)),
                  pair(FFN_BLK, D_MODEL), pair(3, 1, FFN_BLK), pair(1, FFN_BLK)] + more_specs,
        out_specs=[rows(D_MODEL), rows(D_MODEL), upb, upb, pl.BlockSpec((None, tm, FFN_BLK), lambda i, j: (j, i, 0))],
        out_shape=[jax.ShapeDtypeStruct((SEQ, D_MODEL), F32), jax.ShapeDtypeStruct((SEQ, D_MODEL), BF16),
                   jax.ShapeDtypeStruct((2, N_FFN_BLK, SEQ, FFN_BLK), BF16),
                   jax.ShapeDtypeStruct((2, N_FFN_BLK, SEQ, FFN_BLK), BF16),
                   jax.ShapeDtypeStruct((N_FFN_BLK, SEQ, FFN_BLK), BF16)],
        scratch_shapes=[pltpu.VMEM((2, N_FFN_BLK, SUBLANES, FFN_BLK), F32)],
        compiler_params=_params(("arbitrary", "arbitrary")),
    )(x, ycn, yan, w_out, g2, w_up, fcw, fcb, *more)


def _ffn_down(act, w_down, h1, tgt):
    tm = 512
    n_t = SEQ // tm

    def body(act_ref, wd_ref, h1_ref, tgt_ref, dh2_ref, dh2b_ref, loss_ref):
        @pl.when(pl.program_id(0) == 0)
        def _():
            loss_ref[...] = jnp.zeros_like(loss_ref)

        out = _dot(act_ref[0], wd_ref[0])
        for j in range(1, N_FFN_BLK):
            out = out + _dot(act_ref[j], wd_ref[j])
        err = h1_ref[...] + out - tgt_ref[...]
        loss_ref[...] += 0.5 * jnp.sum(err * err) / D_MODEL
        dh2 = err / D_MODEL
        dh2_ref[...] = dh2
        dh2b_ref[...] = dh2.astype(BF16)

    rows = lambda w: pl.BlockSpec((tm, w), lambda i: (i, 0))
    return pl.pallas_call(
        body, name="ffn_down", grid=(n_t,),
        in_specs=[pl.BlockSpec((N_FFN_BLK, tm, FFN_BLK), lambda i: (0, i, 0)),
                  pl.BlockSpec((N_FFN_BLK, FFN_BLK, D_MODEL), lambda i: (0, 0, 0)), rows(D_MODEL), rows(D_MODEL)],
        out_specs=[rows(D_MODEL), rows(D_MODEL), pl.BlockSpec((SUBLANES, 128), lambda i: (0, 0))],
        out_shape=[jax.ShapeDtypeStruct((SEQ, D_MODEL), F32), jax.ShapeDtypeStruct((SEQ, D_MODEL), BF16),
                   jax.ShapeDtypeStruct((SUBLANES, 128), F32)],
        compiler_params=_params(("arbitrary",)),
    )(act, w_down, h1, tgt)


def _out_proj(x, ycn, yan, w_out, g2):
    tm = 512

    def body(x_ref, ycn_ref, yan_ref, wo_ref, g2_ref, h1_ref, u2_ref):
        h1 = x_ref[...] + _dot(ycn_ref[...], wo_ref[0:CONV_WIDTH, :]) + _dot(yan_ref[...], wo_ref[CONV_WIDTH:, :])
        h1_ref[...] = h1
        u2_ref[...] = (h1 * _rstd(h1) * g2_ref[...]).astype(BF16)

    rows = lambda w: pl.BlockSpec((tm, w), lambda i: (i, 0))
    const = lambda shape: pl.BlockSpec(shape, lambda i: (0,) * len(shape))
    return pl.pallas_call(
        body, name="out_proj", grid=(SEQ // tm,),
        in_specs=[rows(D_MODEL), rows(CONV_WIDTH), rows(ATTN_WIDTH), const((D_MODEL, D_MODEL)), const((1, D_MODEL))],
        out_specs=[rows(D_MODEL), rows(D_MODEL)],
        out_shape=[jax.ShapeDtypeStruct((SEQ, D_MODEL), F32), jax.ShapeDtypeStruct((SEQ, D_MODEL), BF16)],
        compiler_params=_params(("arbitrary",)),
    )(x, ycn, yan, w_out, g2)


def _ffn_fwd(h1, u2, w_up, fcw, fcb, w_down, tgt):
    tm = 512
    n_t = SEQ // tm
    last = N_FFN_BLK - 1

    def body(u2_ref, wu_ref, cw_ref, b_ref, wd_ref, h1_ref, tgt_ref,
             up_ref, pre_ref, act_ref, dh2_ref, dh2b_ref, loss_ref, acc_ref, halo_ref):
        j, i = pl.program_id(0), pl.program_id(1)
        rows = pl.ds(pl.multiple_of(i * tm, tm), tm)

        @pl.when((i == 0) & (j == 0))
        def _():
            loss_ref[...] = jnp.zeros_like(loss_ref)

        u2 = u2_ref[...]
        pre = []
        for s in range(2):
            up = _dot_nt(u2, wu_ref[s])
            up_ref[s] = up.astype(BF16)
            halo = jnp.where(i == 0, 0.0, halo_ref[s])
            pre.append(_conv3(up, _taps(cw_ref.at[s]), halo)[0] + b_ref[s])
            pre_ref[s] = pre[s].astype(BF16)
            halo_ref[s] = up[tm - SUBLANES:]
        g, val = pre
        act = (g * jax.nn.sigmoid(g) * val).astype(BF16)
        act_ref[...] = act
        out = _dot(act, wd_ref[...])

        @pl.when(j == 0)
        def _():
            acc_ref[rows, :] = out

        @pl.when(j > 0)
        def _():
            acc_ref[rows, :] += out

        @pl.when(j == last)
        def _():
            err = h1_ref[...] + acc_ref[rows, :] - tgt_ref[...]
            loss_ref[...] += 0.5 * jnp.sum(err * err) / D_MODEL
            dh2 = err / D_MODEL
            dh2_ref[...] = dh2
            dh2b_ref[...] = dh2.astype(BF16)

    late = lambda w: pl.BlockSpec((tm, w), lambda j, i: (jnp.where(j == last, i, 0), 0))
    pair = lambda *s: pl.BlockSpec((2, None) + s, lambda j, i: (0, j) + (0,) * len(s))
    upb = pl.BlockSpec((2, None, tm, FFN_BLK), lambda j, i: (0, j, i, 0))
    return pl.pallas_call(
        body, name="ffn_fwd", grid=(N_FFN_BLK, n_t),
        in_specs=[pl.BlockSpec((tm, D_MODEL), lambda j, i: (i, 0)), pair(FFN_BLK, D_MODEL), pair(3, 1, FFN_BLK),
                  pair(1, FFN_BLK), pl.BlockSpec((None, FFN_BLK, D_MODEL), lambda j, i: (j, 0, 0)),
                  late(D_MODEL), late(D_MODEL)],
        out_specs=[upb, upb, pl.BlockSpec((None, tm, FFN_BLK), lambda j, i: (j, i, 0)), late(D_MODEL), late(D_MODEL),
                   pl.BlockSpec((SUBLANES, 128), lambda j, i: (0, 0))],
        out_shape=[jax.ShapeDtypeStruct((2, N_FFN_BLK, SEQ, FFN_BLK), BF16),
                   jax.ShapeDtypeStruct((2, N_FFN_BLK, SEQ, FFN_BLK), BF16),
                   jax.ShapeDtypeStruct((N_FFN_BLK, SEQ, FFN_BLK), BF16),
                   jax.ShapeDtypeStruct((SEQ, D_MODEL), F32), jax.ShapeDtypeStruct((SEQ, D_MODEL), BF16),
                   jax.ShapeDtypeStruct((SUBLANES, 128), F32)],
        scratch_shapes=[pltpu.VMEM((SEQ, D_MODEL), F32), pltpu.VMEM((2, SUBLANES, FFN_BLK), F32)],
        compiler_params=_params(("arbitrary", "arbitrary")),
    )(u2, w_up, fcw, fcb, w_down, h1, tgt)


def _ffn_fwd_token_major(x, ycn, yan, w_out, g2, w_up, fcw, fcb, w_down, tgt):
    tm = 512
    n_t = SEQ // tm

    def body(x_ref, ycn_ref, yan_ref, wo_ref, g2_ref, wu_ref, cw_ref, b_ref, wd_ref, tgt_ref,
             h1_ref, u2_ref, up_ref, pre_ref, act_ref, dh2_ref, dh2b_ref, loss_ref, acc_ref, halo_ref):
        i, j = pl.program_id(0), pl.program_id(1)

        @pl.when((i == 0) & (j == 0))
        def _():
            loss_ref[...] = jnp.zeros_like(loss_ref)

        @pl.when(j == 0)
        def _():
            h1 = x_ref[...] + _dot(ycn_ref[...], wo_ref[0:CONV_WIDTH, :]) + _dot(yan_ref[...], wo_ref[CONV_WIDTH:, :])
            h1_ref[...] = h1
            u2_ref[...] = (h1 * _rstd(h1) * g2_ref[...]).astype(BF16)
            acc_ref[...] = jnp.zeros_like(acc_ref)

        u2 = u2_ref[...]
        pre = []
        for s in range(2):
            up = _dot_nt(u2, wu_ref[s])
            up_ref[s] = up.astype(BF16)
            halo = jnp.where(i == 0, 0.0, halo_ref[s, j])
            pre.append(_conv3(up, _taps(cw_ref.at[s]), halo)[0] + b_ref[s])
            pre_ref[s] = pre[s].astype(BF16)
            halo_ref[s, j] = up[tm - SUBLANES:]
        g, val = pre
        act = (g * jax.nn.sigmoid(g) * val).astype(BF16)
        act_ref[...] = act
        acc_ref[...] += _dot(act, wd_ref[...])

        @pl.when(j == N_FFN_BLK - 1)
        def _():
            err = h1_ref[...] + acc_ref[...] - tgt_ref[...]
            loss_ref[...] += 0.5 * jnp.sum(err * err) / D_MODEL
            dh2 = err / D_MODEL
            dh2_ref[...] = dh2
            dh2b_ref[...] = dh2.astype(BF16)

    rows = lambda w: pl.BlockSpec((tm, w), lambda i, j: (i, 0))
    const = lambda shape: pl.BlockSpec(shape, lambda i, j: (0,) * len(shape))
    pair = lambda *s: pl.BlockSpec((2, None) + s, lambda i, j: (0, j) + (0,) * len(s))
    upb = pl.BlockSpec((2, None, tm, FFN_BLK), lambda i, j: (0, j, i, 0))
    return pl.pallas_call(
        body, name="ffn_fwd", grid=(n_t, N_FFN_BLK),
        in_specs=[rows(D_MODEL), rows(CONV_WIDTH), rows(ATTN_WIDTH), const((D_MODEL, D_MODEL)), const((1, D_MODEL)),
                  pair(FFN_BLK, D_MODEL), pair(3, 1, FFN_BLK), pair(1, FFN_BLK),
                  pl.BlockSpec((None, FFN_BLK, D_MODEL), lambda i, j: (j, 0, 0)), rows(D_MODEL)],
        out_specs=[rows(D_MODEL), rows(D_MODEL), upb, upb, pl.BlockSpec((None, tm, FFN_BLK), lambda i, j: (j, i, 0)),
                   rows(D_MODEL), rows(D_MODEL), const((SUBLANES, 128))],
        out_shape=[jax.ShapeDtypeStruct((SEQ, D_MODEL), F32), jax.ShapeDtypeStruct((SEQ, D_MODEL), BF16),
                   jax.ShapeDtypeStruct((2, N_FFN_BLK, SEQ, FFN_BLK), BF16),
                   jax.ShapeDtypeStruct((2, N_FFN_BLK, SEQ, FFN_BLK), BF16),
                   jax.ShapeDtypeStruct((N_FFN_BLK, SEQ, FFN_BLK), BF16),
                   jax.ShapeDtypeStruct((SEQ, D_MODEL), F32), jax.ShapeDtypeStruct((SEQ, D_MODEL), BF16),
                   jax.ShapeDtypeStruct((SUBLANES, 128), F32)],
        scratch_shapes=[pltpu.VMEM((tm, D_MODEL), F32), pltpu.VMEM((2, N_FFN_BLK, SUBLANES, FFN_BLK), F32)],
        compiler_params=_params(("arbitrary", "arbitrary")),
    )(x, ycn, yan, w_out, g2, w_up, fcw, fcb, w_down, tgt)


def _ffn_bwd(dh2, dh2b, h1, g2, up, pre, w_up, fcw, w_down):
    tm = 256
    n_t = SEQ // tm
    last = N_FFN_BLK - 1

    def body(dh2b_ref, up_ref, pre_ref, wu_ref, cw_ref, wd_ref, dh2_ref, h1_ref, g2_ref,
             dup_ref, dh1_ref, dh1b_ref, dfb_ref, dfcw_ref, dg2_ref, acc_ref, next_ref):
        j, i = pl.program_id(0), pl.program_id(1)
        rows = pl.ds(pl.multiple_of((n_t - 1 - i) * tm, tm), tm)

        @pl.when((i == 0) & (j == 0))
        def _():
            dfb_ref[...] = jnp.zeros_like(dfb_ref)
            dfcw_ref[...] = jnp.zeros_like(dfcw_ref)
            dg2_ref[...] = jnp.zeros_like(dg2_ref)

        g, val = pre_ref[0].astype(F32), pre_ref[1].astype(F32)
        sg = jax.nn.sigmoid(g)
        silu = g * sg
        dact = _dot_nt(dh2b_ref[...], wd_ref[...])
        dpre = (dact * val * (sg * (1.0 + g * (1.0 - sg))), dact * silu)
        du = None
        for s in range(2):
            d = dpre[s]
            u = up_ref[s].astype(F32)
            w = _taps(cw_ref.at[s])
            nxt = jnp.where(i == 0, 0.0, next_ref[s])
            d1 = _shift_up(d, 1, nxt)
            d2 = _shift_up(d, 2, nxt)
            next_ref[s] = d[:SUBLANES]
            dfb_ref[s, j] += jnp.sum(d, axis=0, keepdims=True)
            dfcw_ref[s, j, 0] += jnp.sum(d2 * u, axis=0, keepdims=True)
            dfcw_ref[s, j, 1] += jnp.sum(d1 * u, axis=0, keepdims=True)
            dfcw_ref[s, j, 2] += jnp.sum(d * u, axis=0, keepdims=True)
            dup = (d * w[2] + d1 * w[1] + d2 * w[0]).astype(BF16)
            dup_ref[s] = dup
            part = _dot(dup, wu_ref[s])
            du = part if du is None else du + part

        @pl.when(j == 0)
        def _():
            acc_ref[rows, :] = du

        @pl.when(j > 0)
        def _():
            acc_ref[rows, :] += du

        @pl.when(j == last)
        def _():
            dn, dgain = _rms_bwd(h1_ref[...], g2_ref[...], acc_ref[rows, :])
            dh1 = dh2_ref[...] + dn
            dh1_ref[...] = dh1
            dh1b_ref[...] = dh1.astype(BF16)
            dg2_ref[...] += dgain

    rev = lambda i: n_t - 1 - i
    const = lambda shape: pl.BlockSpec(shape, lambda j, i: (0,) * len(shape))
    late = lambda w: pl.BlockSpec((tm, w), lambda j, i: (jnp.where(j == last, rev(i), n_t - 1), 0))
    pair = lambda *s: pl.BlockSpec((2, None) + s, lambda j, i: (0, j) + (0,) * len(s))
    upb = pl.BlockSpec((2, None, tm, FFN_BLK), lambda j, i: (0, j, rev(i), 0))
    return pl.pallas_call(
        body, name="ffn_bwd", grid=(N_FFN_BLK, n_t),
        in_specs=[pl.BlockSpec((tm, D_MODEL), lambda j, i: (rev(i), 0)), upb, upb, pair(FFN_BLK, D_MODEL),
                  pair(3, 1, FFN_BLK), pl.BlockSpec((None, FFN_BLK, D_MODEL), lambda j, i: (j, 0, 0)),
                  late(D_MODEL), late(D_MODEL), const((1, D_MODEL))],
        out_specs=[upb, late(D_MODEL), late(D_MODEL),
                   const((2, N_FFN_BLK, 1, FFN_BLK)), const((2, N_FFN_BLK, 3, 1, FFN_BLK)), const((1, D_MODEL))],
        out_shape=[jax.ShapeDtypeStruct((2, N_FFN_BLK, SEQ, FFN_BLK), BF16), jax.ShapeDtypeStruct((SEQ, D_MODEL), F32),
                   jax.ShapeDtypeStruct((SEQ, D_MODEL), BF16), jax.ShapeDtypeStruct((2, N_FFN_BLK, 1, FFN_BLK), F32),
                   jax.ShapeDtypeStruct((2, N_FFN_BLK, 3, 1, FFN_BLK), F32), jax.ShapeDtypeStruct((1, D_MODEL), F32)],
        scratch_shapes=[pltpu.VMEM((SEQ, D_MODEL), F32), pltpu.VMEM((2, SUBLANES, FFN_BLK), F32)],
        compiler_params=_params(("arbitrary", "arbitrary")),
    )(dh2b, up, pre, w_up, fcw, w_down, dh2, h1, g2)


def _ffn_bwd_token_major(dh2, dh2b, h1, g2, up, pre, w_up, fcw, w_down, after=None):
    tm = 512
    n_t = SEQ // tm

    def body(dh2_ref, dh2b_ref, h1_ref, g2_ref, up_ref, pre_ref, wu_ref, cw_ref, wd_ref,
             dup_ref, dh1_ref, dh1b_ref, dfb_ref, dfcw_ref, dg2_ref, acc_ref, next_ref):
        i, j = pl.program_id(0), pl.program_id(1)

        @pl.when((i == 0) & (j == 0))
        def _():
            dfb_ref[...] = jnp.zeros_like(dfb_ref)
            dfcw_ref[...] = jnp.zeros_like(dfcw_ref)
            dg2_ref[...] = jnp.zeros_like(dg2_ref)

        @pl.when(j == 0)
        def _():
            acc_ref[...] = jnp.zeros_like(acc_ref)

        g, val = pre_ref[0].astype(F32), pre_ref[1].astype(F32)
        sg = jax.nn.sigmoid(g)
        silu = g * sg
        dact = _dot_nt(dh2b_ref[...], wd_ref[...])
        dpre = (dact * val * (sg * (1.0 + g * (1.0 - sg))), dact * silu)
        for s in range(2):
            d = dpre[s]
            u = up_ref[s].astype(F32)
            w = _taps(cw_ref.at[s])
            nxt = jnp.where(i == 0, 0.0, next_ref[s, j])
            d1 = _shift_up(d, 1, nxt)
            d2 = _shift_up(d, 2, nxt)
            next_ref[s, j] = d[:SUBLANES]
            dfb_ref[s, j] += jnp.sum(d, axis=0, keepdims=True)
            dfcw_ref[s, j, 0] += jnp.sum(d2 * u, axis=0, keepdims=True)
            dfcw_ref[s, j, 1] += jnp.sum(d1 * u, axis=0, keepdims=True)
            dfcw_ref[s, j, 2] += jnp.sum(d * u, axis=0, keepdims=True)
            dup = (d * w[2] + d1 * w[1] + d2 * w[0]).astype(BF16)
            dup_ref[s] = dup
            acc_ref[...] += _dot(dup, wu_ref[s])

        @pl.when(j == N_FFN_BLK - 1)
        def _():
            dn, dgain = _rms_bwd(h1_ref[...], g2_ref[...], acc_ref[...])
            dh1 = dh2_ref[...] + dn
            dh1_ref[...] = dh1
            dh1b_ref[...] = dh1.astype(BF16)
            dg2_ref[...] += dgain

    rev = lambda i: n_t - 1 - i
    rows = lambda w: pl.BlockSpec((tm, w), lambda i, j: (rev(i), 0))
    const = lambda shape: pl.BlockSpec(shape, lambda i, j: (0,) * len(shape))
    pair = lambda *s: pl.BlockSpec((2, None) + s, lambda i, j: (0, j) + (0,) * len(s))
    upb = pl.BlockSpec((2, None, tm, FFN_BLK), lambda i, j: (0, j, rev(i), 0))
    body, more_specs, more = _ordered_behind(body, 9, after)
    return pl.pallas_call(
        body, name="ffn_bwd", grid=(n_t, N_FFN_BLK),
        in_specs=[rows(D_MODEL), rows(D_MODEL), rows(D_MODEL), const((1, D_MODEL)), upb, upb,
                  pair(FFN_BLK, D_MODEL), pair(3, 1, FFN_BLK),
                  pl.BlockSpec((None, FFN_BLK, D_MODEL), lambda i, j: (j, 0, 0))] + more_specs,
        out_specs=[upb, rows(D_MODEL), rows(D_MODEL),
                   const((2, N_FFN_BLK, 1, FFN_BLK)), const((2, N_FFN_BLK, 3, 1, FFN_BLK)), const((1, D_MODEL))],
        out_shape=[jax.ShapeDtypeStruct((2, N_FFN_BLK, SEQ, FFN_BLK), BF16), jax.ShapeDtypeStruct((SEQ, D_MODEL), F32),
                   jax.ShapeDtypeStruct((SEQ, D_MODEL), BF16), jax.ShapeDtypeStruct((2, N_FFN_BLK, 1, FFN_BLK), F32),
                   jax.ShapeDtypeStruct((2, N_FFN_BLK, 3, 1, FFN_BLK), F32), jax.ShapeDtypeStruct((1, D_MODEL), F32)],
        scratch_shapes=[pltpu.VMEM((tm, D_MODEL), F32), pltpu.VMEM((2, N_FFN_BLK, SUBLANES, FFN_BLK), F32)],
        compiler_params=_params(("arbitrary", "arbitrary")),
    )(dh2, dh2b, h1, g2, up, pre, w_up, fcw, w_down, *more)


def _grad_tn(a_list, b, out_rows, name, after=None):
    n = len(a_list)
    ncol = b.shape[1]

    def body(*refs):
        a_refs, b_ref, o_ref = refs[:n], refs[n], refs[n + 1]
        j = pl.program_id(0)
        for k in range(n):
            @pl.when(j == k)
            def _(k=k):
                o_ref[...] = _dot_tn(a_refs[k][...], b_ref[...]).astype(BF16)

    full = lambda shape: pl.BlockSpec(shape, lambda j: (0,) * len(shape))
    body, more_specs, more = _ordered_behind(body, n + 1, after)
    return pl.pallas_call(
        body, name=name, grid=(n,),
        in_specs=[full((SEQ, out_rows))] * n + [full((SEQ, ncol))] + more_specs,
        out_specs=pl.BlockSpec((None, out_rows, ncol), lambda j: (j, 0, 0)),
        out_shape=jax.ShapeDtypeStruct((n, out_rows, ncol), BF16),
        compiler_params=_params(("arbitrary",)),
    )(*a_list, b, *more)


def _grad_tn_blocked(a, b, name, a_is_blocked):
    nb = a.shape[0] if a_is_blocked else b.shape[0]
    a_w, b_w = a.shape[-1], b.shape[-1]

    def body(a_ref, b_ref, o_ref):
        o_ref[...] = _dot_tn(a_ref[...], b_ref[...]).astype(BF16)

    blocked = lambda w: pl.BlockSpec((None, SEQ, w), lambda k: (k, 0, 0))
    full = lambda w: pl.BlockSpec((SEQ, w), lambda k: (0, 0))
    return pl.pallas_call(
        body, name=name, grid=(nb,),
        in_specs=[blocked(a_w) if a_is_blocked else full(a_w), full(b_w) if a_is_blocked else blocked(b_w)],
        out_specs=pl.BlockSpec((None, a_w, b_w), lambda k: (k, 0, 0)),
        out_shape=jax.ShapeDtypeStruct((nb, a_w, b_w), BF16),
        compiler_params=_params(("arbitrary",)),
    )(a, b)


def _out_bwd(dh1b, w_out, y_attn, gattn, after=None):
    tm = 512
    n_t = SEQ // tm

    def body(dh_ref, wo_ref, y_ref, ga_ref, dycn_ref, dy_ref, dga_ref):
        @pl.when(pl.program_id(0) == 0)
        def _():
            dga_ref[...] = jnp.zeros_like(dga_ref)

        dycat = _dot_nt(dh_ref[...], wo_ref[...])
        dycn_ref[...] = dycat[:, :CONV_WIDTH]
        dy, dga = _rms_bwd(y_ref[...], ga_ref[...], dycat[:, CONV_WIDTH:])
        dy_ref[...] = dy
        dga_ref[...] += dga

    rows = lambda w: pl.BlockSpec((tm, w), lambda i: (i, 0))
    const = lambda shape: pl.BlockSpec(shape, lambda i: (0,) * len(shape))
    body, more_specs, more = _ordered_behind(body, 4, after)
    return pl.pallas_call(
        body, name="out_bwd", grid=(n_t,),
        in_specs=[rows(D_MODEL), const((D_MODEL, D_MODEL)), rows(ATTN_WIDTH), const((1, ATTN_WIDTH))] + more_specs,
        out_specs=[rows(CONV_WIDTH), rows(ATTN_WIDTH), const((1, ATTN_WIDTH))],
        out_shape=[jax.ShapeDtypeStruct((SEQ, CONV_WIDTH), F32), jax.ShapeDtypeStruct((SEQ, ATTN_WIDTH), F32),
                   jax.ShapeDtypeStruct((1, ATTN_WIDTH), F32)],
        compiler_params=_params(("arbitrary",)),
    )(dh1b, w_out, y_attn, gattn, *more)


def _attn_bwd(qn, kn, v, dy, probs, psinks, bkt, after=None):
    n_b = SEQ // BLK

    def body(q_ref, k_ref, v_ref, dy_ref, p_ref, ps_ref, bkt_ref,
             dq_ref, dk_ref, dv_ref, dtbl_ref, dsink_ref, dbias_ref, dsacc_ref):
        i = pl.program_id(0)

        @pl.when(i == 0)
        def _():
            dbias_ref[...] = jnp.zeros_like(dbias_ref)
            dsacc_ref[...] = jnp.zeros_like(dsacc_ref)
            dk_ref[...] = jnp.zeros_like(dk_ref)
            dv_ref[...] = jnp.zeros_like(dv_ref)

        kb, prev, cur = _band_rows(k_ref, i)
        vb, _, _ = _band_rows(v_ref, i)
        upper, _ = _band_masks(i)
        q = q_ref[...]
        dy = dy_ref[...]
        psink = ps_ref[...]
        lane = lax.broadcasted_iota(jnp.int32, (BLK, 128), 1)
        dsink = jnp.zeros((BLK, 128), F32)
        dqs, dks, dvs = [], [], []
        for g in range(N_HEADS // GQA_GROUP):
            kv = slice(HEAD_DIM * g, HEAD_DIM * (g + 1))
            qg = _stack_heads(q, g)
            dog = _stack_heads(dy, g).astype(BF16)
            pb = p_ref[g]
            pg = pb.astype(F32)
            dprobs = _fold(_dot_nt(dog, vb[:, kv]), upper)
            dvs.append(_dot_tn(_unfold(pb, upper), dog))
            dsum = jnp.sum(pg * dprobs, axis=-1, keepdims=True)
            dlogits = pg * (dprobs - dsum)
            for t in range(GQA_GROUP):
                dsink = jnp.where(lane == GQA_GROUP * g + t, -psink * dsum[BLK * t:BLK * (t + 1)], dsink)
            dbias_ref[g] += dlogits
            ds = _unfold(dlogits * (HEAD_DIM ** -0.5), upper).astype(BF16)
            dqs.append(_dot(ds, kb[:, kv]))
            dks.append(_dot_tn(ds, qg))
        dsacc_ref[...] += dsink
        dq_ref[...] = _unstack_heads(dqs)
        dkb = jnp.concatenate(dks, axis=-1)
        dvb = jnp.concatenate(dvs, axis=-1)
        dk_ref[pl.ds(prev, BLK), :] += dkb[:BLK]
        dk_ref[pl.ds(cur, BLK), :] += dkb[BLK:]
        dv_ref[pl.ds(prev, BLK), :] += dvb[:BLK]
        dv_ref[pl.ds(cur, BLK), :] += dvb[BLK:]

        @pl.when(i == n_b - 1)
        def _():
            bkt = bkt_ref[...]
            row8 = lax.broadcasted_iota(jnp.int32, (N_HEADS, 128), 0)
            lane8 = lax.broadcasted_iota(jnp.int32, (N_HEADS, 128), 1)
            acc = jnp.zeros((N_HEADS, 128), F32)
            for h in range(N_HEADS):
                rows = slice(BLK * (h % GQA_GROUP), BLK * (h % GQA_GROUP + 1))
                dbh = dbias_ref[h // GQA_GROUP, rows, :]
                for b in range(NUM_BUCKETS):
                    acc = jnp.where((row8 == h) & (lane8 == b), jnp.sum(jnp.where(bkt == b, dbh, 0.0)), acc)
            dsink_ref[...] = jnp.sum(dsacc_ref[...], axis=0, keepdims=True)
            dtbl_ref[...] = acc

    const = lambda shape: pl.BlockSpec(shape, lambda i: (0,) * len(shape))
    rows = lambda w: pl.BlockSpec((BLK, w), lambda i: (i, 0))
    n_g = N_HEADS // GQA_GROUP
    body, more_specs, more = _ordered_behind(body, 7, after)
    return pl.pallas_call(
        body, name="attn_bwd", grid=(n_b,),
        in_specs=[rows(ATTN_WIDTH), const((SEQ, KV_WIDTH)), const((SEQ, KV_WIDTH)), rows(ATTN_WIDTH),
                  pl.BlockSpec((None, n_g, GROUP_ROWS, BLK), lambda i: (i, 0, 0, 0)), rows(128),
                  const((BLK, BLK))] + more_specs,
        out_specs=[rows(ATTN_WIDTH), const((SEQ, KV_WIDTH)), const((SEQ, KV_WIDTH)), const((N_HEADS, 128)), const((1, 128))],
        out_shape=[jax.ShapeDtypeStruct((SEQ, ATTN_WIDTH), F32), jax.ShapeDtypeStruct((SEQ, KV_WIDTH), F32),
                   jax.ShapeDtypeStruct((SEQ, KV_WIDTH), F32), jax.ShapeDtypeStruct((N_HEADS, 128), F32),
                   jax.ShapeDtypeStruct((1, 128), F32)],
        scratch_shapes=[pltpu.VMEM((n_g, GROUP_ROWS, BLK), F32), pltpu.VMEM((BLK, 128), F32)],
        compiler_params=_params(("arbitrary",)),
    )(qn, kn, v, dy, probs, psinks, bkt, *more)


def _mix_in_bwd(x, dh1, proj, dycn, dqn, dkn, dv, w_in_t, conv_w, g1, gq, gk, gconv):
    tm = 512
    n_t = SEQ // tm
    halo_blocks = tm // SUBLANES

    def body(x_ref, dh1_ref, proj_ref, halo_ref, dycn_ref, dqn_ref, dkn_ref, dv_ref, w_ref, cw_ref,
             g1_ref, gq_ref, gk_ref, gc_ref,
             dx_ref, dproj_ref, dcw_ref, dgc_ref, dgq_ref, dgk_ref, dg1_ref, next_ref):
        i = pl.program_id(0)
        first_tile = i == n_t - 1

        @pl.when(i == 0)
        def _():
            for r in (dcw_ref, dgc_ref, dgq_ref, dgk_ref, dg1_ref, next_ref):
                r[...] = jnp.zeros_like(r)

        proj = proj_ref[...]
        hp = halo_ref[...]
        gate_b = proj[:, 0:CONV_WIDTH]
        gate_c = proj[:, CONV_WIDTH:2 * CONV_WIDTH]
        hc = proj[:, 2 * CONV_WIDTH:3 * CONV_WIDTH]
        a = gate_c * hc
        a_halo = jnp.where(first_tile, 0.0, hp[:, CONV_WIDTH:2 * CONV_WIDTH] * hp[:, 2 * CONV_WIDTH:3 * CONV_WIDTH])
        cw = _taps(cw_ref[...])
        cv, a2, a1 = _conv3(a, cw, a_halo)
        dyc, dgc = _rms_bwd(gate_b * cv, gc_ref[...], dycn_ref[...])
        dgc_ref[...] += dgc
        dcv = dyc * gate_b
        dcw_ref[...] += jnp.concatenate(
            [jnp.sum(dcv * a2, axis=0, keepdims=True), jnp.sum(dcv * a1, axis=0, keepdims=True),
             jnp.sum(dcv * a, axis=0, keepdims=True)], axis=0)
        da = _conv3_bwd_input(dcv, cw, next_ref[...])
        next_ref[...] = dcv[:SUBLANES]
        q0 = 3 * CONV_WIDTH
        k0 = q0 + ATTN_WIDTH
        dq, dgq = _head_norm_bwd(proj[:, q0:k0], gq_ref[...], dqn_ref[...], N_HEADS)
        dk, dgk = _head_norm_bwd(proj[:, k0:k0 + KV_WIDTH], gk_ref[...], dkn_ref[...], 2)
        dgq_ref[...] += dgq
        dgk_ref[...] += dgk
        dproj = jnp.concatenate([dyc * cv, da * hc, da * gate_c, dq, dk, dv_ref[...]], axis=-1).astype(BF16)
        dproj_ref[...] = dproj
        du1 = _dot(dproj, w_ref[...])
        xv = x_ref[...]
        dn, dg1 = _rms_bwd(xv, g1_ref[...], du1)
        dx_ref[...] = dh1_ref[...] + dn
        dg1_ref[...] += dg1

    rev = lambda i: n_t - 1 - i
    rows = lambda w: pl.BlockSpec((tm, w), lambda i: (rev(i), 0))
    const = lambda shape: pl.BlockSpec(shape, lambda i: (0,) * len(shape))
    halo = pl.BlockSpec((SUBLANES, IN_WIDTH), lambda i: (jnp.maximum(rev(i) * halo_blocks - 1, 0), 0))
    return pl.pallas_call(
        body, name="mix_in_bwd", grid=(n_t,),
        in_specs=[rows(D_MODEL), rows(D_MODEL), rows(IN_WIDTH), halo, rows(CONV_WIDTH), rows(ATTN_WIDTH), rows(KV_WIDTH),
                  rows(KV_WIDTH), const((IN_WIDTH, D_MODEL)), const((3, CONV_WIDTH)), const((1, D_MODEL)),
                  const((1, HEAD_DIM)), const((1, HEAD_DIM)), const((1, CONV_WIDTH))],
        out_specs=[rows(D_MODEL), rows(IN_WIDTH), const((3, CONV_WIDTH)), const((1, CONV_WIDTH)),
                   const((1, HEAD_DIM)), const((1, HEAD_DIM)), const((1, D_MODEL))],
        out_shape=[jax.ShapeDtypeStruct((SEQ, D_MODEL), F32), jax.ShapeDtypeStruct((SEQ, IN_WIDTH), BF16),
                   jax.ShapeDtypeStruct((3, CONV_WIDTH), F32),
                   jax.ShapeDtypeStruct((1, CONV_WIDTH), F32), jax.ShapeDtypeStruct((1, HEAD_DIM), F32),
                   jax.ShapeDtypeStruct((1, HEAD_DIM), F32), jax.ShapeDtypeStruct((1, D_MODEL), F32)],
        scratch_shapes=[pltpu.VMEM((SUBLANES, CONV_WIDTH), F32)],
        compiler_params=_params(("arbitrary",)),
    )(x, dh1, proj, proj, dycn, dqn, dkn, dv, w_in_t, conv_w, g1, gq, gk, gconv)


def _grad_w_in(dproj, u1, after=None):
    bw = 768

    def body(a_ref, b_ref, o_ref):
        o_ref[...] = _dot_tn(a_ref[...], b_ref[...]).astype(BF16)

    body, more_specs, more = _ordered_behind(body, 2, after)
    return pl.pallas_call(
        body, name="grad_w_in", grid=(IN_WIDTH // bw,),
        in_specs=[pl.BlockSpec((SEQ, bw), lambda k: (0, k)), pl.BlockSpec((SEQ, D_MODEL), lambda k: (0, 0))] + more_specs,
        out_specs=pl.BlockSpec((bw, D_MODEL), lambda k: (k, 0)),
        out_shape=jax.ShapeDtypeStruct((IN_WIDTH, D_MODEL), BF16),
        compiler_params=_params(("arbitrary",)),
    )(dproj, u1, *more)


def _adamw_math(w, g, m, v):
    m = ADAM_B1 * m + (1.0 - ADAM_B1) * g
    v = ADAM_B2 * v + (1.0 - ADAM_B2) * (g * g)
    m_hat = m / (1.0 - ADAM_B1 ** ADAM_STEP)
    v_hat = v / (1.0 - ADAM_B2 ** ADAM_STEP)
    return -ADAM_LR * (m_hat / (jnp.sqrt(v_hat) + ADAM_EPS) + ADAM_WD * w), m, v


_ROW_G1, _ROW_G2, _ROW_OUT_NORMS, _ROW_FFN_B, _ROW_GQ, _ROW_GK, _ROW_SINKS, _ROW_LOSS, _ROW_TABLE = 0, 1, 2, 3, 11, 12, 13, 14, 16
SMALL_ROWS, SMALL_COLS = 24, 1024
_SMALL_NAMES = ("norm_mix_g", "norm_ffn_g", "out_norm_conv_g", "out_norm_attn_g", "ffn_conv_b", "q_norm_g", "k_norm_g",
                "sinks", "rel_bias_table")


def _pack_small_grads(dg1, dg2, dgconv, dgattn, dfb, dgq, dgk, dsinks, dtbl_t, loss_acc):
    def body(dg1_ref, dg2_ref, dgc_ref, dga_ref, dfb_ref, dgq_ref, dgk_ref, ds_ref, dt_ref, loss_ref, o_ref):
        o_ref[...] = jnp.zeros_like(o_ref)
        o_ref[_ROW_G1:_ROW_G1 + 1, :] = dg1_ref[...]
        o_ref[_ROW_G2:_ROW_G2 + 1, :] = dg2_ref[...]
        o_ref[_ROW_OUT_NORMS:_ROW_OUT_NORMS + 1, 0:CONV_WIDTH] = dgc_ref[...]
        o_ref[_ROW_OUT_NORMS:_ROW_OUT_NORMS + 1, CONV_WIDTH:] = dga_ref[...]
        for k in range(N_DEV):
            o_ref[_ROW_FFN_B + k:_ROW_FFN_B + k + 1, 0:FFN_BLK] = dfb_ref[k // N_FFN_BLK, k % N_FFN_BLK]
        o_ref[_ROW_GQ:_ROW_GQ + 1, 0:HEAD_DIM] = dgq_ref[...]
        o_ref[_ROW_GK:_ROW_GK + 1, 0:HEAD_DIM] = dgk_ref[...]
        o_ref[_ROW_SINKS:_ROW_SINKS + 1, 0:128] = ds_ref[...]
        o_ref[_ROW_LOSS:_ROW_LOSS + 1, 0:128] = loss_ref[0:1, :]
        o_ref[_ROW_TABLE:_ROW_TABLE + N_HEADS, 0:128] = dt_ref[...]

    return pl.pallas_call(body, name="pack_small_grads", out_shape=jax.ShapeDtypeStruct((SMALL_ROWS, SMALL_COLS), F32))(
        dg1, dg2, dgconv, dgattn, dfb, dgq, dgk, dsinks, dtbl_t, loss_acc)


def _adamw_small(recv, params, after):
    names = _SMALL_NAMES
    n = len(names)

    def grad_of(g, name, k=None):
        if name == "norm_mix_g":
            return g[_ROW_G1:_ROW_G1 + 1, :]
        if name == "norm_ffn_g":
            return g[_ROW_G2:_ROW_G2 + 1, :]
        if name == "out_norm_conv_g":
            return g[_ROW_OUT_NORMS:_ROW_OUT_NORMS + 1, 0:CONV_WIDTH]
        if name == "out_norm_attn_g":
            return g[_ROW_OUT_NORMS:_ROW_OUT_NORMS + 1, CONV_WIDTH:]
        if name == "ffn_conv_b":
            return g[_ROW_FFN_B + k:_ROW_FFN_B + k + 1, 0:FFN_BLK]
        if name == "q_norm_g":
            return g[_ROW_GQ:_ROW_GQ + 1, 0:HEAD_DIM]
        if name == "k_norm_g":
            return g[_ROW_GK:_ROW_GK + 1, 0:HEAD_DIM]
        if name == "sinks":
            return g[_ROW_SINKS:_ROW_SINKS + 1, 0:N_HEADS]
        return g[_ROW_TABLE:_ROW_TABLE + N_HEADS, 0:NUM_BUCKETS]

    def body(r_ref, *refs):
        ins, outs, loss_ref = refs[:3 * n], refs[3 * n:7 * n], refs[7 * n]
        g = r_ref[0]
        for s in range(1, N_DEV):
            g = g + r_ref[s]
        loss_ref[...] = g[_ROW_LOSS:_ROW_LOSS + 1, 0:128]
        for i, name in enumerate(names):
            w_ref, m_ref, v_ref = ins[3 * i:3 * i + 3]
            o = outs[4 * i:4 * i + 4]
            cols = [slice(FFN_BLK * k, FFN_BLK * (k + 1)) for k in range(N_DEV)] if name == "ffn_conv_b" else [slice(None)]
            for k, cs in enumerate(cols):
                gk = grad_of(g, name, k)
                d, m2, v2 = _adamw_math(w_ref[:, cs], gk, m_ref[:, cs], v_ref[:, cs])
                o[0][:, cs], o[1][:, cs], o[2][:, cs], o[3][:, cs] = gk, d, m2, v2

    flat = [a for name in names for a in params[name]]
    body, more_specs, more = _ordered_behind(body, 1 + 3 * n, after)
    vmem = pl.BlockSpec(memory_space=pltpu.VMEM)
    out = pl.pallas_call(
        body, name="adamw_small",
        in_specs=[vmem] * (1 + 3 * n) + more_specs,
        out_shape=[jax.ShapeDtypeStruct(params[name][0].shape, F32) for name in names for _ in range(4)]
        + [jax.ShapeDtypeStruct((1, 128), F32)],
        compiler_params=pltpu.CompilerParams(vmem_limit_bytes=VMEM_LIMIT),
    )(recv, *flat, *more)
    return {name: tuple(out[4 * i:4 * i + 4]) for i, name in enumerate(names)}, out[4 * n]


def _adamw_direct(w, m, v, own, recv, me, name, row_blocks=1, after=None):
    rb = w.shape[0] // row_blocks
    cols = w.shape[1]

    def body(me_ref, w_ref, m_ref, v_ref, o_ref, r_ref, g_o, d_o, m_o, v_o):
        g = o_ref[...].astype(F32)
        for s in range(N_DEV - 1):
            g = g + r_ref[s].astype(F32)
        g_o[...] = g
        d_o[...], m_o[...], v_o[...] = _adamw_math(w_ref[...], g, m_ref[...], v_ref[...])

    blk = pl.BlockSpec((rb, cols), lambda i, me_ref: (i, 0))
    oblk = pl.BlockSpec((None, rb, cols), lambda i, me_ref: (me_ref[0], i, 0))
    rblk = pl.BlockSpec((N_DEV - 1, rb, cols), lambda i, me_ref: (0, i, 0))
    body, more_specs, more = _ordered_behind(body, 6, after)
    return pl.pallas_call(
        body, name=name,
        grid_spec=pltpu.PrefetchScalarGridSpec(num_scalar_prefetch=1, grid=(row_blocks,),
                                               in_specs=[blk, blk, blk, oblk, rblk] + more_specs, out_specs=[blk] * 4),
        out_shape=[jax.ShapeDtypeStruct(w.shape, F32)] * 4,
        compiler_params=_params(("arbitrary",)),
    )(me, w, m, v, own, recv, *more)


def _adamw(w, m, v, part, recv, chip, name, row_blocks=1, after=None):
    rb = w.shape[0] // row_blocks
    tail = w.shape[1:]
    zeros = (0,) * len(tail)

    def body(chip_ref, w_ref, m_ref, v_ref, p_ref, r_ref, g_o, d_o, m_o, v_o):
        g = p_ref[...].astype(F32)
        for s in range(3):
            g = g + r_ref[s].astype(F32)
        g_o[...] = g
        d_o[...], m_o[...], v_o[...] = _adamw_math(w_ref[...], g, m_ref[...], v_ref[...])

    blk = pl.BlockSpec((rb,) + tail, lambda i, chip_ref: (i,) + zeros)
    pblk = pl.BlockSpec((None, rb) + tail, lambda i, chip_ref: (chip_ref[0], i) + zeros)
    rblk = pl.BlockSpec((3, rb) + tail, lambda i, chip_ref: (0, i) + zeros)
    body, more_specs, more = _ordered_behind(body, 6, after)
    return pl.pallas_call(
        body, name=name,
        grid_spec=pltpu.PrefetchScalarGridSpec(num_scalar_prefetch=1, grid=(row_blocks,),
                                               in_specs=[blk, blk, blk, pblk, rblk] + more_specs, out_specs=[blk] * 4),
        out_shape=[jax.ShapeDtypeStruct(w.shape, F32)] * 4,
        compiler_params=_params(("arbitrary",)),
    )(chip, w, m, v, part, recv, *more)


def kernel(x, norm_mix_g, w_in, conv_w, q_norm_g, k_norm_g, rel_bias_table, sinks, out_norm_conv_g, out_norm_attn_g, w_out, norm_ffn_g, w_up, ffn_conv_w, ffn_conv_b, w_down, loss_target, m_norm_mix_g, m_w_in, m_conv_w, m_q_norm_g, m_k_norm_g, m_rel_bias_table, m_sinks, m_out_norm_conv_g, m_out_norm_attn_g, m_w_out, m_norm_ffn_g, m_w_up, m_ffn_conv_w, m_ffn_conv_b, m_w_down, v_norm_mix_g, v_w_in, v_conv_w, v_q_norm_g, v_k_norm_g, v_rel_bias_table, v_sinks, v_out_norm_conv_g, v_out_norm_attn_g, v_w_out, v_norm_ffn_g, v_w_up, v_ffn_conv_w, v_ffn_conv_b, v_w_down):
    p = dict(norm_mix_g=norm_mix_g, w_in=w_in, conv_w=conv_w, q_norm_g=q_norm_g, k_norm_g=k_norm_g,
             rel_bias_table=rel_bias_table, sinks=sinks, out_norm_conv_g=out_norm_conv_g, out_norm_attn_g=out_norm_attn_g,
             w_out=w_out, norm_ffn_g=norm_ffn_g, w_up=w_up, ffn_conv_w=ffn_conv_w, ffn_conv_b=ffn_conv_b, w_down=w_down)
    m = dict(norm_mix_g=m_norm_mix_g, w_in=m_w_in, conv_w=m_conv_w, q_norm_g=m_q_norm_g, k_norm_g=m_k_norm_g,
             rel_bias_table=m_rel_bias_table, sinks=m_sinks, out_norm_conv_g=m_out_norm_conv_g,
             out_norm_attn_g=m_out_norm_attn_g, w_out=m_w_out, norm_ffn_g=m_norm_ffn_g, w_up=m_w_up,
             ffn_conv_w=m_ffn_conv_w, ffn_conv_b=m_ffn_conv_b, w_down=m_w_down)
    v = dict(norm_mix_g=v_norm_mix_g, w_in=v_w_in, conv_w=v_conv_w, q_norm_g=v_q_norm_g, k_norm_g=v_k_norm_g,
             rel_bias_table=v_rel_bias_table, sinks=v_sinks, out_norm_conv_g=v_out_norm_conv_g,
             out_norm_attn_g=v_out_norm_attn_g, w_out=v_w_out, norm_ffn_g=v_norm_ffn_g, w_up=v_w_up,
             ffn_conv_w=v_ffn_conv_w, ffn_conv_b=v_ffn_conv_b, w_down=v_w_down)

    xs, tgt = x[0], loss_target[0]
    g1, g2, gq, gk, gconv, gattn = norm_mix_g, norm_ffn_g, q_norm_g, k_norm_g, out_norm_conv_g, out_norm_attn_g
    ix, iy, ic = _coords()
    core = ic.astype(jnp.int32).reshape(1)
    chip = (2 * ix + iy).astype(jnp.int32).reshape(1)
    me = _lin(ix, iy, ic).astype(jnp.int32).reshape(1)
    bkt = jnp.asarray(_bucket_map())
    tr = lambda a: a[0].T
    taps = lambda a: jnp.transpose(a, (1, 0, 2))
    tbl_t = rel_bias_table.T

    wi_l, cw_l = _place_shards(me, [tr(w_in), taps(conv_w)], [BF16, F32], "place_mixer_shards")
    finish_a, token_a = _all_gather_split([wi_l, cw_l], "mixer", None)
    wo_l, wu_l, wd_l, fcw_l = _place_shards(me, [w_out[0], tr(w_up), w_down[0], taps(ffn_conv_w)],
                                            [BF16, BF16, BF16, F32], "place_ffn_shards", after=token_a)
    ffn_stage2, ffn_stage3, token_b = _all_gather_tree([wo_l, wu_l, wd_l, fcw_l], "ffn", token_a)
    wi_g, cw_g = finish_a(token_b)
    w_in_t = wi_g.reshape(IN_WIDTH, D_MODEL)
    conv_w_f = jnp.transpose(cw_g[:, :, 0, :], (1, 0, 2)).reshape(3, CONV_WIDTH)

    proj, u1, ycn, qn, kn, vv = _mix_in_fwd(xs, g1, w_in_t, conv_w_f, gq, gk, gconv)
    token_b2 = ffn_stage2(ycn)
    y_attn, yan, probs, psinks = _attn_fwd(qn, kn, vv, tbl_t, sinks, bkt, gattn, after=token_b2)
    wo_g, wu_g, wd_g, fcw_g = ffn_stage3(yan)
    w_out_f = wo_g.reshape(D_MODEL, D_MODEL)
    w_down_f = wd_g.reshape(N_FFN_BLK, FFN_BLK, D_MODEL)
    w_up_f = wu_g.reshape(2, N_FFN_BLK, FFN_BLK, D_MODEL)
    fcw_f = fcw_g.reshape(2, N_FFN_BLK, 3, 1, FFN_BLK)
    fcb = ffn_conv_b.reshape(2, N_FFN_BLK, 1, FFN_BLK)
    h1, u2, up, pre, act, dh2, dh2b, loss_acc = _ffn_fwd_token_major(
        xs, ycn, yan, w_out_f, g2, w_up_f, fcw_f, fcb, w_down_f, tgt)

    dw_down = _grad_tn_blocked(act, dh2b, "grad_w_down", a_is_blocked=True).reshape(N_DEV, D_FF // N_DEV, D_MODEL)
    plan_d, slots_d = _scatter_plan(1)
    d_sem = _split_start("scatter_w_down_start", [dw_down], [lax.empty((N_DEV - 1,) + dw_down.shape[1:], BF16)],
                         plan_d, None)
    dup, dh1, dh1b, dfb, dfcw, dg2 = _ffn_bwd_token_major(dh2, dh2b, h1, g2, up, pre, w_up_f, fcw_f, w_down_f,
                                                          after=d_sem[4])
    dw_up = _grad_tn_blocked(dup.reshape(N_DEV, SEQ, FFN_BLK), u2, "grad_w_up", a_is_blocked=True)
    dw_out = _grad_tn([ycn, yan], dh1b, CONV_WIDTH, "grad_w_out").reshape(N_DEV, D_MODEL // N_DEV, D_MODEL)
    out_bwd = {}

    def behind_ffn(token):
        out_bwd["r"] = _out_bwd(dh1b, w_out_f, y_attn, gattn, after=token)
        return out_bwd["r"][0]

    finish_ffn, token_ffn = _reduce_scatter_split(
        [dw_up, dw_out, dfcw.reshape(N_DEV, 3, 1, FFN_BLK)], "ffn", core, behind_ffn)
    dycn, dy_attn, dgattn = out_bwd["r"]
    dqn, dkn, dv, dtbl_t, dsinks = _attn_bwd(qn, kn, vv, dy_attn, probs, psinks, bkt, after=token_ffn)
    dx, dproj, dcw, dgconv, dgq, dgk, dg1 = _mix_in_bwd(xs, dh1, proj, dycn, dqn, dkn, dv, w_in_t, conv_w_f,
                                                             g1, gq, gk, gconv)
    packed = _pack_small_grads(dg1, dg2, dgconv, dgattn, dfb, dgq, dgk, dsinks, dtbl_t, loss_acc)
    plan_s, slots_s = _broadcast_plan()
    s_sem, r_sem, src_s, land_s, token_s = _split_start(
        "gather_small_start", [packed], [jnp.broadcast_to(packed[None], (N_DEV,) + packed.shape)], plan_s, None)
    dw_in_t = _grad_w_in(dproj, u1, after=token_s).reshape(N_DEV, IN_WIDTH // N_DEV, D_MODEL)
    dcw_b = jnp.transpose(dcw.reshape(3, N_DEV, 1, CONV_WIDTH // N_DEV), (1, 0, 2, 3))
    adam = {}
    ffn_got = {}

    def behind_mixer(token):
        ffn_got["r"] = finish_ffn(token)
        return ffn_got["r"][1][0]

    finish_mixer, token_mixer = _reduce_scatter_split([dw_in_t, dcw_b], "mixer", core, behind_mixer)
    (p_wu, p_wo, p_fcw), (r_wu, r_wo, r_fcw) = ffn_got["r"]
    (own_wd,), (r_wd,) = _split_wait("scatter_w_down_wait", d_sem[0], d_sem[1], d_sem[2], d_sem[3], plan_d, slots_d,
                                     token_mixer)
    adam["w_down"] = _adamw_direct(w_down[0], m_w_down[0], v_w_down[0], own_wd, r_wd, me, "adamw_w_down", row_blocks=2)
    adam_up = _adamw(tr(w_up), tr(m_w_up), tr(v_w_up), p_wu, r_wu, chip, "adamw_w_up", row_blocks=4,
                     after=adam["w_down"][0])
    adam["w_out"] = _adamw(w_out[0], m_w_out[0], v_w_out[0], p_wo, r_wo, chip, "adamw_w_out", after=adam_up[0])
    adam_fcw = _adamw(taps(ffn_conv_w), taps(m_ffn_conv_w), taps(v_ffn_conv_w), p_fcw, r_fcw, chip, "adamw_ffn_conv_w",
                      after=adam["w_out"][0])
    _, (r_small,) = _split_wait("gather_small_wait", s_sem, r_sem, src_s, land_s, plan_s, slots_s, adam_fcw[0])
    small_in = {k: (p[k], m[k], v[k]) for k in _SMALL_NAMES}
    small_in["rel_bias_table"] = (tbl_t, m_rel_bias_table.T, v_rel_bias_table.T)
    small_out, loss_row = _adamw_small(r_small, small_in, None)
    (p_wi, p_cw), (r_wi, r_cw) = finish_mixer(loss_row)
    adam_in = _adamw(tr(w_in), tr(m_w_in), tr(v_w_in), p_wi, r_wi, chip, "adamw_w_in")
    adam_cw = _adamw(taps(conv_w), taps(m_conv_w), taps(v_conv_w), p_cw, r_cw, chip, "adamw_conv_w")

    res = {k: tuple(a[None] for a in t) for k, t in adam.items()}
    res["w_up"] = tuple(a.T[None] for a in adam_up)
    res["w_in"] = tuple(a.T[None] for a in adam_in)
    res["ffn_conv_w"] = tuple(taps(a) for a in adam_fcw)
    res["conv_w"] = tuple(taps(a) for a in adam_cw)
    res.update(small_out)
    res["rel_bias_table"] = tuple(a.T for a in small_out["rel_bias_table"])
    loss = loss_row[0, 0]
    order = ("norm_mix_g", "w_in", "conv_w", "q_norm_g", "k_norm_g", "rel_bias_table", "sinks", "out_norm_conv_g",
             "out_norm_attn_g", "w_out", "norm_ffn_g", "w_up", "ffn_conv_w", "ffn_conv_b", "w_down")
    return (loss, dx[None], *[res[k][0] for k in order], *[res[k][1] for k in order],
            *[res[k][2] for k in order], *[res[k][3] for k in order])
```

```python
import functools
import math

import numpy as np
import jax
import jax.numpy as jnp
from jax import lax
from jax.experimental import pallas as pl
from jax.experimental.pallas import tpu as pltpu

F32 = jnp.float32
BF16 = jnp.bfloat16

SEQ = 2048
D_MODEL = 1024
CONV_WIDTH = 512
ATTN_WIDTH = 512
KV_WIDTH = 128
HEAD_DIM = 64
N_HEADS = 8
GQA_GROUP = 4
IN_WIDTH = 2304
D_FF = 2816
BLK = 128
NUM_BUCKETS = 32
EPS = 1e-6
NEG_INF = -1e30
ADAM_LR = 0.001
ADAM_B1 = 0.9
ADAM_B2 = 0.999
ADAM_EPS = 1e-08
ADAM_WD = 0.01
ADAM_STEP = 10

N_DEV = 8
FFN_BLK = 2 * D_FF // N_DEV
N_FFN_BLK = D_FF // FFN_BLK
SUBLANES = 8
VMEM_LIMIT = 56 * 1024 * 1024

_MESH = pl.DeviceIdType.MESH
_ANY = pl.BlockSpec(memory_space=pl.ANY)


def _params(sem):
    return pltpu.CompilerParams(dimension_semantics=sem, vmem_limit_bytes=VMEM_LIMIT)


def _ordered_behind(body, pos, after):
    if after is None:
        return body, [], []
    return (lambda *refs: body(*refs[:pos], *refs[pos + 1:])), [_ANY], [after]


def _dot(a, b):
    return jnp.dot(a, b, preferred_element_type=F32)


def _dot_nt(a, b):
    return lax.dot_general(a, b, (((1,), (1,)), ((), ())), preferred_element_type=F32)


def _dot_tn(a, b):
    return lax.dot_general(a, b, (((0,), (0,)), ((), ())), preferred_element_type=F32)


def _shift_down(x, s, halo):
    r = pltpu.roll(x, s, axis=0)
    hr = pltpu.roll(halo, s, axis=0)
    row = lax.broadcasted_iota(jnp.int32, halo.shape, 0)
    top = jnp.where(row < s, hr, r[:SUBLANES])
    return jnp.concatenate([top, r[SUBLANES:]], axis=0)


def _shift_up(x, s, halo):
    n = x.shape[0]
    r = pltpu.roll(x, n - s, axis=0)
    hr = pltpu.roll(halo, SUBLANES - s, axis=0)
    row = lax.broadcasted_iota(jnp.int32, halo.shape, 0)
    bot = jnp.where(row >= SUBLANES - s, hr, r[n - SUBLANES:])
    return jnp.concatenate([r[:n - SUBLANES], bot], axis=0)


def _taps(w):
    return (w[0], w[1], w[2]) if len(w.shape) == 3 else (w[0:1], w[1:2], w[2:3])


def _conv3(x, w, halo):
    x2 = _shift_down(x, 2, halo)
    x1 = _shift_down(x, 1, halo)
    return x2 * w[0] + x1 * w[1] + x * w[2], x2, x1


def _conv3_bwd_input(dy, w, halo_next):
    return dy * w[2] + _shift_up(dy, 1, halo_next) * w[1] + _shift_up(dy, 2, halo_next) * w[0]


def _rstd(x):
    return lax.rsqrt(jnp.mean(x * x, axis=-1, keepdims=True) + EPS)


def _rms_bwd(x, g, dy):
    r = _rstd(x)
    n = x * r
    dn = dy * g
    dx = r * (dn - n * jnp.mean(dn * n, axis=-1, keepdims=True))
    return dx, jnp.sum(dy * n, axis=0, keepdims=True)


def _head_mean(x):
    width = x.shape[-1]
    ri = lax.broadcasted_iota(jnp.int32, (width, width), 0) // HEAD_DIM
    ci = lax.broadcasted_iota(jnp.int32, (width, width), 1) // HEAD_DIM
    ones = jnp.where(ri == ci, 1.0, 0.0).astype(BF16)
    hi = x.astype(BF16)
    lo = (x - hi.astype(F32)).astype(BF16)
    return (_dot(hi, ones) + _dot(lo, ones)) * (1.0 / HEAD_DIM)


def _head_norm(x, g, heads):
    return x * lax.rsqrt(_head_mean(x * x) + EPS) * jnp.tile(g, (1, heads))


def _head_norm_bwd(x, g, dy, heads):
    r = lax.rsqrt(_head_mean(x * x) + EPS)
    n = x * r
    dn = dy * jnp.tile(g, (1, heads))
    dx = r * (dn - n * _head_mean(dn * n))
    per_lane = jnp.sum(dy * n, axis=0, keepdims=True)
    dg = per_lane[:, 0:HEAD_DIM]
    for h in range(1, heads):
        dg = dg + per_lane[:, HEAD_DIM * h:HEAD_DIM * (h + 1)]
    return dx, dg


def _bucket_map():
    q = np.arange(BLK)[:, None]
    j = np.arange(BLK)[None, :]
    n = np.where(j > q, q + BLK - j, q - j)
    nf = np.maximum(n, 1).astype(np.float32)
    max_exact = NUM_BUCKETS // 2
    large = max_exact + (np.log(nf / max_exact) / math.log(BLK / max_exact) * (NUM_BUCKETS - max_exact)).astype(np.int32)
    large = np.minimum(large, NUM_BUCKETS - 1)
    return np.where(n < max_exact, n, large).astype(np.int32)


def _coords():
    return lax.axis_index("x"), lax.axis_index("y"), lax.axis_index("c")


def _lin(px, py, pc):
    return 4 * px + 2 * py + pc


_HBM = pl.BlockSpec(memory_space=pltpu.HBM)
_SEM = pl.BlockSpec(memory_space=pltpu.SEMAPHORE)
_EFFECT = pltpu.SideEffectType.DATAFLOW_SIDE_EFFECTING


def _in_hbm(a):
    return pltpu.with_memory_space_constraint(a, pltpu.HBM)


def _split_start(name, srcs, lands, plan, after):
    ns, nl = len(srcs), len(lands)
    n_copies = len(plan(0, 0, 0))
    n_after = 0 if after is None else 1

    def body(*refs):
        src_refs, land_refs = refs[:ns + nl], refs[ns:ns + nl]
        send_sems, recv_sems = refs[ns + nl + n_after], refs[ns + nl + n_after + 1]
        token = refs[-1]
        for k, (a, s_slot, l, d_slot, dev) in enumerate(plan(*_coords())):
            src = src_refs[a] if s_slot is None else src_refs[a].at[s_slot]
            pltpu.make_async_remote_copy(src_ref=src, dst_ref=land_refs[l].at[d_slot], send_sem=send_sems.at[k],
                                         recv_sem=recv_sems.at[k], device_id=dev, device_id_type=_MESH).start()
        token[...] = jnp.zeros_like(token)

    arrs = list(srcs) + list(lands)
    out = pl.pallas_call(
        body, name=name,
        out_shape=(pltpu.SemaphoreType.DMA((n_copies,)), pltpu.SemaphoreType.DMA((n_copies,)),
                   *[pltpu.HBM(a.shape, a.dtype) for a in arrs], jax.ShapeDtypeStruct((SUBLANES, 128), F32)),
        in_specs=[_HBM] * (ns + nl) + [_ANY] * n_after,
        out_specs=(_SEM, _SEM, *[_HBM] * (ns + nl), pl.BlockSpec(memory_space=pltpu.VMEM)),
        input_output_aliases={i: 2 + i for i in range(ns + nl)},
        compiler_params=pltpu.CompilerParams(has_side_effects=_EFFECT),
    )(*[_in_hbm(a) for a in arrs], *([] if after is None else [after]))
    return out[0], out[1], list(out[2:2 + ns]), list(out[2 + ns:2 + ns + nl]), out[-1]


def _split_wait(name, send_sems, recv_sems, srcs, lands, plan, recv_slots, after):
    ns, nl = len(srcs), len(lands)

    def body(*refs):
        src_refs, land_refs = refs[:ns + nl], refs[ns:ns + nl]
        send_sems, recv_sems = refs[ns + nl], refs[ns + nl + 1]
        coords = _coords()
        slots = recv_slots(*coords)
        for k, (a, s_slot, l, _, dev) in enumerate(plan(*coords)):
            src = src_refs[a] if s_slot is None else src_refs[a].at[s_slot]
            cp = pltpu.make_async_remote_copy(src_ref=src, dst_ref=land_refs[l].at[slots[k]], send_sem=send_sems.at[k],
                                              recv_sem=recv_sems.at[k], device_id=dev, device_id_type=_MESH)
            cp.wait_send()
            cp.wait_recv()

    arrs = list(srcs) + list(lands)
    out = pl.pallas_call(
        body, name=name,
        out_shape=tuple(pltpu.HBM(a.shape, a.dtype) for a in arrs),
        in_specs=[_HBM] * (ns + nl) + [_SEM, _SEM, _ANY],
        out_specs=tuple([_HBM] * (ns + nl)),
        input_output_aliases={i: i for i in range(ns + nl)},
        compiler_params=pltpu.CompilerParams(has_side_effects=_EFFECT),
    )(*arrs, send_sems, recv_sems, after)
    return list(out[:ns]), list(out[ns:])


def _chips(x, y):
    return [(1 - x, y), (x, 1 - y), (1 - x, 1 - y)]


def _gather_plan_ici(n):
    def plan(x, y, c):
        me = _lin(x, y, c)
        out = []
        for a in range(n):
            out.append((a, me, a, me, (x, y, 1 - c)))
            out += [(a, me, a, me, (cx, cy, c)) for cx, cy in _chips(x, y)]
        return out

    def recv_slots(x, y, c):
        out = []
        for _ in range(n):
            out.append(_lin(x, y, 1 - c))
            out += [_lin(cx, cy, c) for cx, cy in _chips(x, y)]
        return out

    return plan, recv_slots


def _gather_plan_d2d(n):
    def plan(x, y, c):
        return [(a, _lin(cx, cy, c), a, _lin(cx, cy, c), (x, y, 1 - c)) for a in range(n) for cx, cy in _chips(x, y)]

    def recv_slots(x, y, c):
        return [_lin(cx, cy, 1 - c) for _ in range(n) for cx, cy in _chips(x, y)]

    return plan, recv_slots


def _all_gather_split(lands, tag, after):
    n = len(lands)
    plan1, slots1 = _gather_plan_ici(n)
    s1, r1, _, lands, token = _split_start(f"gather_{tag}_ici_start", [], lands, plan1, after)

    def finish(after):
        _, got = _split_wait(f"gather_{tag}_ici_wait", s1, r1, [], lands, plan1, slots1, after)
        plan2, slots2 = _gather_plan_d2d(n)
        s2, r2, _, got, token2 = _split_start(f"gather_{tag}_d2d_start", [], got, plan2, None)
        return _split_wait(f"gather_{tag}_d2d_wait", s2, r2, [], got, plan2, slots2, token2)[1]

    return finish, token


def _all_gather_tree(lands, tag, after):
    n = len(lands)

    def plan1(x, y, c):
        me = _lin(x, y, c)
        return [(a, me, a, me, dev) for a in range(n) for dev in ((x, y, 1 - c), (1 - x, y, c), (x, 1 - y, c))]

    def slots1(x, y, c):
        return [s for _ in range(n) for s in (_lin(x, y, 1 - c), _lin(1 - x, y, c), _lin(x, 1 - y, c))]

    def plan2(x, y, c):
        from_x, from_y = _lin(1 - x, y, c), _lin(x, 1 - y, c)
        north = c == 1
        passed = jnp.where(north, from_x, from_y)
        onward = (jnp.where(north, x, 1 - x), jnp.where(north, 1 - y, y), c)
        sib = (x, y, 1 - c)
        return [cp for a in range(n) for cp in ((a, passed, a, passed, onward), (a, from_x, a, from_x, sib),
                                                (a, from_y, a, from_y, sib))]

    def slots2(x, y, c):
        return [s for _ in range(n) for s in (_lin(1 - x, 1 - y, c), _lin(1 - x, y, 1 - c), _lin(x, 1 - y, 1 - c))]

    def plan3(x, y, c):
        diag = _lin(1 - x, 1 - y, c)
        return [(a, diag, a, diag, (x, y, 1 - c)) for a in range(n)]

    def slots3(x, y, c):
        return [_lin(1 - x, 1 - y, 1 - c)] * n

    s1, r1, _, lands, token = _split_start(f"gather_{tag}_1_start", [], lands, plan1, after)
    state = {}

    def stage2(after):
        _, got = _split_wait(f"gather_{tag}_1_wait", s1, r1, [], lands, plan1, slots1, after)
        state["s"], state["r"], _, state["lands"], token2 = _split_start(f"gather_{tag}_2_start", [], got, plan2, None)
        return token2

    def stage3(after):
        _, got = _split_wait(f"gather_{tag}_2_wait", state["s"], state["r"], [], state["lands"], plan2, slots2, after)
        s3, r3, _, got, token3 = _split_start(f"gather_{tag}_3_start", [], got, plan3, None)
        return _split_wait(f"gather_{tag}_3_wait", s3, r3, [], got, plan3, slots3, token3)[1]

    return stage2, stage3, token


_CHIP_LIST = ((0, 0), (0, 1), (1, 0), (1, 1))


def _reduce_plan_d2d(n):
    def plan(x, y, c):
        return [(a, _lin(qx, qy, 1 - c), a, q, (x, y, 1 - c)) for a in range(n) for q, (qx, qy) in enumerate(_CHIP_LIST)]

    def recv_slots(x, y, c):
        return [q for _ in range(n) for q in range(4)]

    return plan, recv_slots


def _reduce_plan_ici(n):
    def plan(x, y, c):
        return [(a, 2 * cx + cy, a, j, (cx, cy, c)) for a in range(n) for j, (cx, cy) in enumerate(_chips(x, y))]

    def recv_slots(x, y, c):
        return [j for _ in range(n) for j in range(3)]

    return plan, recv_slots


def _peers(x, y, c):
    return [(1 - x if r & 4 else x, 1 - y if r & 2 else y, 1 - c if r & 1 else c) for r in range(1, N_DEV)]


def _scatter_plan(n):
    def plan(x, y, c):
        return [(a, _lin(*peer), a, r, peer) for a in range(n) for r, peer in enumerate(_peers(x, y, c))]

    def recv_slots(x, y, c):
        return [r for _ in range(n) for r in range(N_DEV - 1)]

    return plan, recv_slots


def _broadcast_plan():
    peers = _peers

    def plan(x, y, c):
        return [(0, None, 0, _lin(x, y, c), peer) for peer in peers(x, y, c)]

    def recv_slots(x, y, c):
        return [_lin(*peer) for peer in peers(x, y, c)]

    return plan, recv_slots


def _chip_partial(grads, recvd, core, name):
    n = len(grads)

    def body(c_ref, *refs):
        for a in range(n):
            g_ref, r_ref, o_ref = refs[a], refs[n + a], refs[2 * n + a]
            o_ref[...] = (g_ref[...].astype(F32) + r_ref[...].astype(F32)).astype(o_ref.dtype)

    def blk(a, own):
        zeros = (0,) * (a.ndim - 1)
        return pl.BlockSpec((None,) + a.shape[1:],
                            (lambda q, c_ref: (2 * q + c_ref[0],) + zeros) if own else (lambda q, c_ref: (q,) + zeros))

    return pl.pallas_call(
        body, name=name,
        grid_spec=pltpu.PrefetchScalarGridSpec(
            num_scalar_prefetch=1, grid=(4,),
            in_specs=[blk(a, True) for a in grads] + [blk(a, False) for a in recvd],
            out_specs=[blk(a, False) for a in recvd]),
        out_shape=[jax.ShapeDtypeStruct(a.shape, a.dtype) for a in recvd],
        compiler_params=_params(("arbitrary",)),
    )(core, *grads, *recvd)


def _reduce_scatter_split(grads, tag, core, behind):
    n = len(grads)
    plan1, slots1 = _reduce_plan_d2d(n)
    lands1 = [lax.empty((4,) + a.shape[1:], a.dtype) for a in grads]
    s1, r1, srcs1, lands1, token1 = _split_start(f"reduce_{tag}_d2d_start", grads, lands1, plan1, None)
    own, got = _split_wait(f"reduce_{tag}_d2d_wait", s1, r1, srcs1, lands1, plan1, slots1, behind(token1))
    parts = _chip_partial(own, got, core, f"reduce_{tag}_partial")
    plan2, slots2 = _reduce_plan_ici(n)
    lands2 = [lax.empty((3,) + a.shape[1:], a.dtype) for a in grads]
    s2, r2, srcs2, lands2, token2 = _split_start(f"reduce_{tag}_ici_start", parts, lands2, plan2, None)

    def finish(after):
        return _split_wait(f"reduce_{tag}_ici_wait", s2, r2, srcs2, lands2, plan2, slots2, after)

    return finish, token2


def _place_shards(me, shards, dtypes, name, after=None):
    n = len(shards)

    def body(me_ref, *refs):
        for a in range(n):
            refs[n + a][...] = refs[a][...].astype(dtypes[a])

    full = lambda s: pl.BlockSpec(s.shape, lambda i, me_ref: (0,) * s.ndim)
    slot = lambda s: pl.BlockSpec((None,) + s.shape, lambda i, me_ref: (me_ref[0],) + (0,) * s.ndim)
    body, more_specs, more = _ordered_behind(body, 1 + n, after)
    return pl.pallas_call(
        body, name=name,
        grid_spec=pltpu.PrefetchScalarGridSpec(num_scalar_prefetch=1, grid=(1,),
                                               in_specs=[full(s) for s in shards] + more_specs,
                                               out_specs=[slot(s) for s in shards]),
        out_shape=[jax.ShapeDtypeStruct((N_DEV,) + s.shape, d) for s, d in zip(shards, dtypes)],
        compiler_params=_params(("arbitrary",)),
    )(me, *shards, *more)


def _mix_in_fwd(x, g1, w_in_t, conv_w, gq, gk, gconv):
    tm = 512
    n_t = SEQ // tm

    def body(x_ref, g1_ref, w_ref, cw_ref, gq_ref, gk_ref, gc_ref,
             proj_ref, u1_ref, ycn_ref, qn_ref, kn_ref, v_ref, halo_ref):
        @pl.when(pl.program_id(0) == 0)
        def _():
            halo_ref[...] = jnp.zeros_like(halo_ref)

        xv = x_ref[...]
        u = (xv * _rstd(xv) * g1_ref[...]).astype(BF16)
        u1_ref[...] = u
        proj = _dot_nt(u, w_ref[...])
        proj_ref[...] = proj
        gate_b = proj[:, 0:CONV_WIDTH]
        a = proj[:, CONV_WIDTH:2 * CONV_WIDTH] * proj[:, 2 * CONV_WIDTH:3 * CONV_WIDTH]
        cv, _, _ = _conv3(a, _taps(cw_ref[...]), halo_ref[...])
        halo_ref[...] = a[tm - SUBLANES:]
        yc = gate_b * cv
        ycn_ref[...] = (yc * _rstd(yc) * gc_ref[...]).astype(BF16)
        q0 = 3 * CONV_WIDTH
        qn_ref[...] = _head_norm(proj[:, q0:q0 + ATTN_WIDTH], gq_ref[...], N_HEADS).astype(BF16)
        k0 = q0 + ATTN_WIDTH
        kn_ref[...] = _head_norm(proj[:, k0:k0 + KV_WIDTH], gk_ref[...], 2).astype(BF16)
        v_ref[...] = proj[:, k0 + KV_WIDTH:k0 + 2 * KV_WIDTH].astype(BF16)

    const = lambda shape: pl.BlockSpec(shape, lambda i: (0,) * len(shape))
    rows = lambda w: pl.BlockSpec((tm, w), lambda i: (i, 0))
    return pl.pallas_call(
        body, name="mix_in_fwd", grid=(n_t,),
        in_specs=[rows(D_MODEL), const((1, D_MODEL)), const((IN_WIDTH, D_MODEL)), const((3, CONV_WIDTH)),
                  const((1, HEAD_DIM)), const((1, HEAD_DIM)), const((1, CONV_WIDTH))],
        out_specs=[rows(IN_WIDTH), rows(D_MODEL), rows(CONV_WIDTH), rows(ATTN_WIDTH), rows(KV_WIDTH), rows(KV_WIDTH)],
        out_shape=[jax.ShapeDtypeStruct((SEQ, IN_WIDTH), F32), jax.ShapeDtypeStruct((SEQ, D_MODEL), BF16),
                   jax.ShapeDtypeStruct((SEQ, CONV_WIDTH), BF16),
                   jax.ShapeDtypeStruct((SEQ, ATTN_WIDTH), BF16), jax.ShapeDtypeStruct((SEQ, KV_WIDTH), BF16),
                   jax.ShapeDtypeStruct((SEQ, KV_WIDTH), BF16)],
        scratch_shapes=[pltpu.VMEM((SUBLANES, CONV_WIDTH), F32)],
        compiler_params=_params(("arbitrary",)),
    )(x, g1, w_in_t, conv_w, gq, gk, gconv)


GROUP_ROWS = GQA_GROUP * BLK


def _band_bias(tbl_ref, bkt, bias_ref):
    for h in range(N_HEADS):
        acc = jnp.zeros(bkt.shape, F32)
        for b in range(NUM_BUCKETS):
            acc = jnp.where(bkt == b, tbl_ref[h, b], acc)
        bias_ref[h // GQA_GROUP, BLK * (h % GQA_GROUP):BLK * (h % GQA_GROUP + 1), :] = acc


def _band_masks(i):
    qi = lax.broadcasted_iota(jnp.int32, (GROUP_ROWS, BLK), 0) & (BLK - 1)
    ji = lax.broadcasted_iota(jnp.int32, (GROUP_ROWS, BLK), 1)
    upper = ji > qi
    return upper, upper & (i == 0)


def _stack_heads(x, g):
    return jnp.concatenate([x[:, HEAD_DIM * h:HEAD_DIM * (h + 1)] for h in range(GQA_GROUP * g, GQA_GROUP * (g + 1))], axis=0)


def _unstack_heads(groups):
    return jnp.concatenate([p[BLK * t:BLK * (t + 1)] for p in groups for t in range(GQA_GROUP)], axis=-1)


def _per_head_rows(vals):
    row = lax.broadcasted_iota(jnp.int32, (GROUP_ROWS, 1), 0)
    col = jnp.full((GROUP_ROWS, 1), vals[GQA_GROUP - 1], F32)
    for t in range(GQA_GROUP - 2, -1, -1):
        col = jnp.where(row < BLK * (t + 1), vals[t], col)
    return col


def _band_rows(ref, i):
    prev = pl.multiple_of(jnp.maximum(i - 1, 0) * BLK, BLK)
    cur = pl.multiple_of(i * BLK, BLK)
    return jnp.concatenate([ref[pl.ds(prev, BLK), :], ref[pl.ds(cur, BLK), :]], axis=0), prev, cur


def _fold(band, upper):
    return jnp.where(upper, band[:, :BLK], band[:, BLK:])


def _unfold(tile, upper):
    return jnp.concatenate([jnp.where(upper, tile, 0.0), jnp.where(upper, 0.0, tile)], axis=1)


def _head_probs(qh, kh, bias, upper, dead, sink):
    logits = _fold(_dot_nt(qh, kh), upper) * (HEAD_DIM ** -0.5) + bias
    logits = jnp.where(dead, NEG_INF, logits)
    m = jnp.maximum(jnp.max(logits, axis=-1, keepdims=True), sink)
    p = jnp.exp(logits - m)
    es = jnp.exp(sink - m)
    den = jnp.sum(p, axis=-1, keepdims=True) + es
    return p / den, es / den


def _attn_fwd(qn, kn, v, tbl, sinks, bkt, gattn, after=None):
    n_b = SEQ // BLK

    def body(q_ref, k_ref, v_ref, tbl_ref, sink_ref, bkt_ref, ga_ref, y_ref, yn_ref, p_ref, ps_ref, bias_ref):
        i = pl.program_id(0)

        @pl.when(i == 0)
        def _():
            _band_bias(tbl_ref, bkt_ref[...], bias_ref)

        kb, _, _ = _band_rows(k_ref, i)
        vb, _, _ = _band_rows(v_ref, i)
        upper, dead = _band_masks(i)
        q = q_ref[...]
        lane = lax.broadcasted_iota(jnp.int32, (BLK, 128), 1)
        outs = []
        psinks = jnp.zeros((BLK, 128), F32)
        for g in range(N_HEADS // GQA_GROUP):
            kv = slice(HEAD_DIM * g, HEAD_DIM * (g + 1))
            sink = _per_head_rows([sink_ref[0, GQA_GROUP * g + t] for t in range(GQA_GROUP)])
            probs, psink = _head_probs(_stack_heads(q, g), kb[:, kv], bias_ref[g], upper, dead, sink)
            p_ref[g] = probs.astype(BF16)
            for t in range(GQA_GROUP):
                psinks = jnp.where(lane == GQA_GROUP * g + t, psink[BLK * t:BLK * (t + 1)], psinks)
            outs.append(_dot(_unfold(probs, upper).astype(BF16), vb[:, kv]))
        ps_ref[...] = psinks
        y = _unstack_heads(outs)
        y_ref[...] = y
        yn_ref[...] = (y * _rstd(y) * ga_ref[...]).astype(BF16)

    const = lambda shape: pl.BlockSpec(shape, lambda i: (0,) * len(shape))
    rows = lambda w: pl.BlockSpec((BLK, w), lambda i: (i, 0))
    smem = pl.BlockSpec(memory_space=pltpu.SMEM)
    body, more_specs, more = _ordered_behind(body, 7, after)
    return pl.pallas_call(
        body, name="attn_fwd", grid=(n_b,),
        in_specs=[rows(ATTN_WIDTH), const((SEQ, KV_WIDTH)), const((SEQ, KV_WIDTH)), smem, smem,
                  const((BLK, BLK)), const((1, ATTN_WIDTH))] + more_specs,
        out_specs=[rows(ATTN_WIDTH), rows(ATTN_WIDTH),
                   pl.BlockSpec((None, N_HEADS // GQA_GROUP, GROUP_ROWS, BLK), lambda i: (i, 0, 0, 0)), rows(128)],
        out_shape=[jax.ShapeDtypeStruct((SEQ, ATTN_WIDTH), F32), jax.ShapeDtypeStruct((SEQ, ATTN_WIDTH), BF16),
                   jax.ShapeDtypeStruct((n_b, N_HEADS // GQA_GROUP, GROUP_ROWS, BLK), BF16),
                   jax.ShapeDtypeStruct((SEQ, 128), F32)],
        scratch_shapes=[pltpu.VMEM((N_HEADS // GQA_GROUP, GROUP_ROWS, BLK), F32)],
        compiler_params=_params(("arbitrary",)),
    )(qn, kn, v, tbl, sinks, bkt, gattn, *more)


def _ffn_up(x, ycn, yan, w_out, g2, w_up, fcw, fcb, after=None):
    tm = 512
    n_t = SEQ // tm

    def body(x_ref, ycn_ref, yan_ref, wo_ref, g2_ref, wu_ref, cw_ref, b_ref,
             h1_ref, u2_ref, up_ref, pre_ref, act_ref, halo_ref):
        i, j = pl.program_id(0), pl.program_id(1)

        @pl.when(j == 0)
        def _():
            h1 = x_ref[...] + _dot(ycn_ref[...], wo_ref[0:CONV_WIDTH, :]) + _dot(yan_ref[...], wo_ref[CONV_WIDTH:, :])
            h1_ref[...] = h1
            u2_ref[...] = (h1 * _rstd(h1) * g2_ref[...]).astype(BF16)

        u2 = u2_ref[...]
        pre = []
        for s in range(2):
            up = _dot_nt(u2, wu_ref[s])
            up_ref[s] = up.astype(BF16)
            halo = jnp.where(i == 0, 0.0, halo_ref[s, j])
            pre.append(_conv3(up, _taps(cw_ref.at[s]), halo)[0] + b_ref[s])
            pre_ref[s] = pre[s].astype(BF16)
            halo_ref[s, j] = up[tm - SUBLANES:]
        g, val = pre
        act_ref[...] = (g * jax.nn.sigmoid(g) * val).astype(BF16)

    rows = lambda w: pl.BlockSpec((tm, w), lambda i, j: (i, 0))
    const = lambda shape: pl.BlockSpec(shape, lambda i, j: (0,) * len(shape))
    pair = lambda *s: pl.BlockSpec((2, None) + s, lambda i, j: (0, j) + (0,) * len(s))
    upb = pl.BlockSpec((2, None, tm, FFN_BLK), lambda i, j: (0, j, i, 0))
    body, more_specs, more = _ordered_behind(body, 8, after)
    return pl.pallas_call(
        body, name="ffn_up", grid=(n_t, N_FFN_BLK),
        in_specs=[rows(D_MODEL), rows(CONV_WIDTH), rows(ATTN_WIDTH), const((D_MODEL, D_MODEL)), const((1, D_MODEL)),
                  pair(FFN_BLK, D_MODEL), pair(3, 1, FFN_BLK), pair(1, FFN_BLK)] + more_specs,
        out_specs=[rows(D_MODEL), rows(D_MODEL), upb, upb, pl.BlockSpec((None, tm, FFN_BLK), lambda i, j: (j, i, 0))],
        out_shape=[jax.ShapeDtypeStruct((SEQ, D_MODEL), F32), jax.ShapeDtypeStruct((SEQ, D_MODEL), BF16),
                   jax.ShapeDtypeStruct((2, N_FFN_BLK, SEQ, FFN_BLK), BF16),
                   jax.ShapeDtypeStruct((2, N_FFN_BLK, SEQ, FFN_BLK), BF16),
                   jax.ShapeDtypeStruct((N_FFN_BLK, SEQ, FFN_BLK), BF16)],
        scratch_shapes=[pltpu.VMEM((2, N_FFN_BLK, SUBLANES, FFN_BLK), F32)],
        compiler_params=_params(("arbitrary", "arbitrary")),
    )(x, ycn, yan, w_out, g2, w_up, fcw, fcb, *more)


def _ffn_down(act, w_down, h1, tgt):
    tm = 512
    n_t = SEQ // tm

    def body(act_ref, wd_ref, h1_ref, tgt_ref, dh2_ref, dh2b_ref, loss_ref):
        @pl.when(pl.program_id(0) == 0)
        def _():
            loss_ref[...] = jnp.zeros_like(loss_ref)

        out = _dot(act_ref[0], wd_ref[0])
        for j in range(1, N_FFN_BLK):
            out = out + _dot(act_ref[j], wd_ref[j])
        err = h1_ref[...] + out - tgt_ref[...]
        loss_ref[...] += 0.5 * jnp.sum(err * err) / D_MODEL
        dh2 = err / D_MODEL
        dh2_ref[...] = dh2
        dh2b_ref[...] = dh2.astype(BF16)

    rows = lambda w: pl.BlockSpec((tm, w), lambda i: (i, 0))
    return pl.pallas_call(
        body, name="ffn_down", grid=(n_t,),
        in_specs=[pl.BlockSpec((N_FFN_BLK, tm, FFN_BLK), lambda i: (0, i, 0)),
                  pl.BlockSpec((N_FFN_BLK, FFN_BLK, D_MODEL), lambda i: (0, 0, 0)), rows(D_MODEL), rows(D_MODEL)],
        out_specs=[rows(D_MODEL), rows(D_MODEL), pl.BlockSpec((SUBLANES, 128), lambda i: (0, 0))],
        out_shape=[jax.ShapeDtypeStruct((SEQ, D_MODEL), F32), jax.ShapeDtypeStruct((SEQ, D_MODEL), BF16),
                   jax.ShapeDtypeStruct((SUBLANES, 128), F32)],
        compiler_params=_params(("arbitrary",)),
    )(act, w_down, h1, tgt)


def _out_proj(x, ycn, yan, w_out, g2):
    tm = 512

    def body(x_ref, ycn_ref, yan_ref, wo_ref, g2_ref, h1_ref, u2_ref):
        h1 = x_ref[...] + _dot(ycn_ref[...], wo_ref[0:CONV_WIDTH, :]) + _dot(yan_ref[...], wo_ref[CONV_WIDTH:, :])
        h1_ref[...] = h1
        u2_ref[...] = (h1 * _rstd(h1) * g2_ref[...]).astype(BF16)

    rows = lambda w: pl.BlockSpec((tm, w), lambda i: (i, 0))
    const = lambda shape: pl.BlockSpec(shape, lambda i: (0,) * len(shape))
    return pl.pallas_call(
        body, name="out_proj", grid=(SEQ // tm,),
        in_specs=[rows(D_MODEL), rows(CONV_WIDTH), rows(ATTN_WIDTH), const((D_MODEL, D_MODEL)), const((1, D_MODEL))],
        out_specs=[rows(D_MODEL), rows(D_MODEL)],
        out_shape=[jax.ShapeDtypeStruct((SEQ, D_MODEL), F32), jax.ShapeDtypeStruct((SEQ, D_MODEL), BF16)],
        compiler_params=_params(("arbitrary",)),
    )(x, ycn, yan, w_out, g2)


def _ffn_fwd(h1, u2, w_up, fcw, fcb, w_down, tgt):
    tm = 512
    n_t = SEQ // tm
    last = N_FFN_BLK - 1

    def body(u2_ref, wu_ref, cw_ref, b_ref, wd_ref, h1_ref, tgt_ref,
             up_ref, pre_ref, act_ref, dh2_ref, dh2b_ref, loss_ref, acc_ref, halo_ref):
        j, i = pl.program_id(0), pl.program_id(1)
        rows = pl.ds(pl.multiple_of(i * tm, tm), tm)

        @pl.when((i == 0) & (j == 0))
        def _():
            loss_ref[...] = jnp.zeros_like(loss_ref)

        u2 = u2_ref[...]
        pre = []
        for s in range(2):
            up = _dot_nt(u2, wu_ref[s])
            up_ref[s] = up.astype(BF16)
            halo = jnp.where(i == 0, 0.0, halo_ref[s])
            pre.append(_conv3(up, _taps(cw_ref.at[s]), halo)[0] + b_ref[s])
            pre_ref[s] = pre[s].astype(BF16)
            halo_ref[s] = up[tm - SUBLANES:]
        g, val = pre
        act = (g * jax.nn.sigmoid(g) * val).astype(BF16)
        act_ref[...] = act
        out = _dot(act, wd_ref[...])

        @pl.when(j == 0)
        def _():
            acc_ref[rows, :] = out

        @pl.when(j > 0)
        def _():
            acc_ref[rows, :] += out

        @pl.when(j == last)
        def _():
            err = h1_ref[...] + acc_ref[rows, :] - tgt_ref[...]
            loss_ref[...] += 0.5 * jnp.sum(err * err) / D_MODEL
            dh2 = err / D_MODEL
            dh2_ref[...] = dh2
            dh2b_ref[...] = dh2.astype(BF16)

    late = lambda w: pl.BlockSpec((tm, w), lambda j, i: (jnp.where(j == last, i, 0), 0))
    pair = lambda *s: pl.BlockSpec((2, None) + s, lambda j, i: (0, j) + (0,) * len(s))
    upb = pl.BlockSpec((2, None, tm, FFN_BLK), lambda j, i: (0, j, i, 0))
    return pl.pallas_call(
        body, name="ffn_fwd", grid=(N_FFN_BLK, n_t),
        in_specs=[pl.BlockSpec((tm, D_MODEL), lambda j, i: (i, 0)), pair(FFN_BLK, D_MODEL), pair(3, 1, FFN_BLK),
                  pair(1, FFN_BLK), pl.BlockSpec((None, FFN_BLK, D_MODEL), lambda j, i: (j, 0, 0)),
                  late(D_MODEL), late(D_MODEL)],
        out_specs=[upb, upb, pl.BlockSpec((None, tm, FFN_BLK), lambda j, i: (j, i, 0)), late(D_MODEL), late(D_MODEL),
                   pl.BlockSpec((SUBLANES, 128), lambda j, i: (0, 0))],
        out_shape=[jax.ShapeDtypeStruct((2, N_FFN_BLK, SEQ, FFN_BLK), BF16),
                   jax.ShapeDtypeStruct((2, N_FFN_BLK, SEQ, FFN_BLK), BF16),
                   jax.ShapeDtypeStruct((N_FFN_BLK, SEQ, FFN_BLK), BF16),
                   jax.ShapeDtypeStruct((SEQ, D_MODEL), F32), jax.ShapeDtypeStruct((SEQ, D_MODEL), BF16),
                   jax.ShapeDtypeStruct((SUBLANES, 128), F32)],
        scratch_shapes=[pltpu.VMEM((SEQ, D_MODEL), F32), pltpu.VMEM((2, SUBLANES, FFN_BLK), F32)],
        compiler_params=_params(("arbitrary", "arbitrary")),
    )(u2, w_up, fcw, fcb, w_down, h1, tgt)


def _ffn_fwd_token_major(x, ycn, yan, w_out, g2, w_up, fcw, fcb, w_down, tgt):
    tm = 512
    n_t = SEQ // tm

    def body(x_ref, ycn_ref, yan_ref, wo_ref, g2_ref, wu_ref, cw_ref, b_ref, wd_ref, tgt_ref,
             h1_ref, u2_ref, up_ref, pre_ref, act_ref, dh2_ref, dh2b_ref, loss_ref, acc_ref, halo_ref):
        i, j = pl.program_id(0), pl.program_id(1)

        @pl.when((i == 0) & (j == 0))
        def _():
            loss_ref[...] = jnp.zeros_like(loss_ref)

        @pl.when(j == 0)
        def _():
            h1 = x_ref[...] + _dot(ycn_ref[...], wo_ref[0:CONV_WIDTH, :]) + _dot(yan_ref[...], wo_ref[CONV_WIDTH:, :])
            h1_ref[...] = h1
            u2_ref[...] = (h1 * _rstd(h1) * g2_ref[...]).astype(BF16)
            acc_ref[...] = jnp.zeros_like(acc_ref)

        u2 = u2_ref[...]
        pre = []
        for s in range(2):
            up = _dot_nt(u2, wu_ref[s])
            up_ref[s] = up.astype(BF16)
            halo = jnp.where(i == 0, 0.0, halo_ref[s, j])
            pre.append(_conv3(up, _taps(cw_ref.at[s]), halo)[0] + b_ref[s])
            pre_ref[s] = pre[s].astype(BF16)
            halo_ref[s, j] = up[tm - SUBLANES:]
        g, val = pre
        act = (g * jax.nn.sigmoid(g) * val).astype(BF16)
        act_ref[...] = act
        acc_ref[...] += _dot(act, wd_ref[...])

        @pl.when(j == N_FFN_BLK - 1)
        def _():
            err = h1_ref[...] + acc_ref[...] - tgt_ref[...]
            loss_ref[...] += 0.5 * jnp.sum(err * err) / D_MODEL
            dh2 = err / D_MODEL
            dh2_ref[...] = dh2
            dh2b_ref[...] = dh2.astype(BF16)

    rows = lambda w: pl.BlockSpec((tm, w), lambda i, j: (i, 0))
    const = lambda shape: pl.BlockSpec(shape, lambda i, j: (0,) * len(shape))
    pair = lambda *s: pl.BlockSpec((2, None) + s, lambda i, j: (0, j) + (0,) * len(s))
    upb = pl.BlockSpec((2, None, tm, FFN_BLK), lambda i, j: (0, j, i, 0))
    return pl.pallas_call(
        body, name="ffn_fwd", grid=(n_t, N_FFN_BLK),
        in_specs=[rows(D_MODEL), rows(CONV_WIDTH), rows(ATTN_WIDTH), const((D_MODEL, D_MODEL)), const((1, D_MODEL)),
                  pair(FFN_BLK, D_MODEL), pair(3, 1, FFN_BLK), pair(1, FFN_BLK),
                  pl.BlockSpec((None, FFN_BLK, D_MODEL), lambda i, j: (j, 0, 0)), rows(D_MODEL)],
        out_specs=[rows(D_MODEL), rows(D_MODEL), upb, upb, pl.BlockSpec((None, tm, FFN_BLK), lambda i, j: (j, i, 0)),
                   rows(D_MODEL), rows(D_MODEL), const((SUBLANES, 128))],
        out_shape=[jax.ShapeDtypeStruct((SEQ, D_MODEL), F32), jax.ShapeDtypeStruct((SEQ, D_MODEL), BF16),
                   jax.ShapeDtypeStruct((2, N_FFN_BLK, SEQ, FFN_BLK), BF16),
                   jax.ShapeDtypeStruct((2, N_FFN_BLK, SEQ, FFN_BLK), BF16),
                   jax.ShapeDtypeStruct((N_FFN_BLK, SEQ, FFN_BLK), BF16),
                   jax.ShapeDtypeStruct((SEQ, D_MODEL), F32), jax.ShapeDtypeStruct((SEQ, D_MODEL), BF16),
                   jax.ShapeDtypeStruct((SUBLANES, 128), F32)],
        scratch_shapes=[pltpu.VMEM((tm, D_MODEL), F32), pltpu.VMEM((2, N_FFN_BLK, SUBLANES, FFN_BLK), F32)],
        compiler_params=_params(("arbitrary", "arbitrary")),
    )(x, ycn, yan, w_out, g2, w_up, fcw, fcb, w_down, tgt)


def _ffn_bwd(dh2, dh2b, h1, g2, up, pre, w_up, fcw, w_down):
    tm = 256
    n_t = SEQ // tm
    last = N_FFN_BLK - 1

    def body(dh2b_ref, up_ref, pre_ref, wu_ref, cw_ref, wd_ref, dh2_ref, h1_ref, g2_ref,
             dup_ref, dh1_ref, dh1b_ref, dfb_ref, dfcw_ref, dg2_ref, acc_ref, next_ref):
        j, i = pl.program_id(0), pl.program_id(1)
        rows = pl.ds(pl.multiple_of((n_t - 1 - i) * tm, tm), tm)

        @pl.when((i == 0) & (j == 0))
        def _():
            dfb_ref[...] = jnp.zeros_like(dfb_ref)
            dfcw_ref[...] = jnp.zeros_like(dfcw_ref)
            dg2_ref[...] = jnp.zeros_like(dg2_ref)

        g, val = pre_ref[0].astype(F32), pre_ref[1].astype(F32)
        sg = jax.nn.sigmoid(g)
        silu = g * sg
        dact = _dot_nt(dh2b_ref[...], wd_ref[...])
        dpre = (dact * val * (sg * (1.0 + g * (1.0 - sg))), dact * silu)
        du = None
        for s in range(2):
            d = dpre[s]
            u = up_ref[s].astype(F32)
            w = _taps(cw_ref.at[s])
            nxt = jnp.where(i == 0, 0.0, next_ref[s])
            d1 = _shift_up(d, 1, nxt)
            d2 = _shift_up(d, 2, nxt)
            next_ref[s] = d[:SUBLANES]
            dfb_ref[s, j] += jnp.sum(d, axis=0, keepdims=True)
            dfcw_ref[s, j, 0] += jnp.sum(d2 * u, axis=0, keepdims=True)
            dfcw_ref[s, j, 1] += jnp.sum(d1 * u, axis=0, keepdims=True)
            dfcw_ref[s, j, 2] += jnp.sum(d * u, axis=0, keepdims=True)
            dup = (d * w[2] + d1 * w[1] + d2 * w[0]).astype(BF16)
            dup_ref[s] = dup
            part = _dot(dup, wu_ref[s])
            du = part if du is None else du + part

        @pl.when(j == 0)
        def _():
            acc_ref[rows, :] = du

        @pl.when(j > 0)
        def _():
            acc_ref[rows, :] += du

        @pl.when(j == last)
        def _():
            dn, dgain = _rms_bwd(h1_ref[...], g2_ref[...], acc_ref[rows, :])
            dh1 = dh2_ref[...] + dn
            dh1_ref[...] = dh1
            dh1b_ref[...] = dh1.astype(BF16)
            dg2_ref[...] += dgain

    rev = lambda i: n_t - 1 - i
    const = lambda shape: pl.BlockSpec(shape, lambda j, i: (0,) * len(shape))
    late = lambda w: pl.BlockSpec((tm, w), lambda j, i: (jnp.where(j == last, rev(i), n_t - 1), 0))
    pair = lambda *s: pl.BlockSpec((2, None) + s, lambda j, i: (0, j) + (0,) * len(s))
    upb = pl.BlockSpec((2, None, tm, FFN_BLK), lambda j, i: (0, j, rev(i), 0))
    return pl.pallas_call(
        body, name="ffn_bwd", grid=(N_FFN_BLK, n_t),
        in_specs=[pl.BlockSpec((tm, D_MODEL), lambda j, i: (rev(i), 0)), upb, upb, pair(FFN_BLK, D_MODEL),
                  pair(3, 1, FFN_BLK), pl.BlockSpec((None, FFN_BLK, D_MODEL), lambda j, i: (j, 0, 0)),
                  late(D_MODEL), late(D_MODEL), const((1, D_MODEL))],
        out_specs=[upb, late(D_MODEL), late(D_MODEL),
                   const((2, N_FFN_BLK, 1, FFN_BLK)), const((2, N_FFN_BLK, 3, 1, FFN_BLK)), const((1, D_MODEL))],
        out_shape=[jax.ShapeDtypeStruct((2, N_FFN_BLK, SEQ, FFN_BLK), BF16), jax.ShapeDtypeStruct((SEQ, D_MODEL), F32),
                   jax.ShapeDtypeStruct((SEQ, D_MODEL), BF16), jax.ShapeDtypeStruct((2, N_FFN_BLK, 1, FFN_BLK), F32),
                   jax.ShapeDtypeStruct((2, N_FFN_BLK, 3, 1, FFN_BLK), F32), jax.ShapeDtypeStruct((1, D_MODEL), F32)],
        scratch_shapes=[pltpu.VMEM((SEQ, D_MODEL), F32), pltpu.VMEM((2, SUBLANES, FFN_BLK), F32)],
        compiler_params=_params(("arbitrary", "arbitrary")),
    )(dh2b, up, pre, w_up, fcw, w_down, dh2, h1, g2)


def _ffn_bwd_token_major(dh2, dh2b, h1, g2, up, pre, w_up, fcw, w_down, after=None):
    tm = 512
    n_t = SEQ // tm

    def body(dh2_ref, dh2b_ref, h1_ref, g2_ref, up_ref, pre_ref, wu_ref, cw_ref, wd_ref,
             dup_ref, dh1_ref, dh1b_ref, dfb_ref, dfcw_ref, dg2_ref, acc_ref, next_ref):
        i, j = pl.program_id(0), pl.program_id(1)

        @pl.when((i == 0) & (j == 0))
        def _():
            dfb_ref[...] = jnp.zeros_like(dfb_ref)
            dfcw_ref[...] = jnp.zeros_like(dfcw_ref)
            dg2_ref[...] = jnp.zeros_like(dg2_ref)

        @pl.when(j == 0)
        def _():
            acc_ref[...] = jnp.zeros_like(acc_ref)

        g, val = pre_ref[0].astype(F32), pre_ref[1].astype(F32)
        sg = jax.nn.sigmoid(g)
        silu = g * sg
        dact = _dot_nt(dh2b_ref[...], wd_ref[...])
        dpre = (dact * val * (sg * (1.0 + g * (1.0 - sg))), dact * silu)
        for s in range(2):
            d = dpre[s]
            u = up_ref[s].astype(F32)
            w = _taps(cw_ref.at[s])
            nxt = jnp.where(i == 0, 0.0, next_ref[s, j])
            d1 = _shift_up(d, 1, nxt)
            d2 = _shift_up(d, 2, nxt)
            next_ref[s, j] = d[:SUBLANES]
            dfb_ref[s, j] += jnp.sum(d, axis=0, keepdims=True)
            dfcw_ref[s, j, 0] += jnp.sum(d2 * u, axis=0, keepdims=True)
            dfcw_ref[s, j, 1] += jnp.sum(d1 * u, axis=0, keepdims=True)
            dfcw_ref[s, j, 2] += jnp.sum(d * u, axis=0, keepdims=True)
            dup = (d * w[2] + d1 * w[1] + d2 * w[0]).astype(BF16)
            dup_ref[s] = dup
            acc_ref[...] += _dot(dup, wu_ref[s])

        @pl.when(j == N_FFN_BLK - 1)
        def _():
            dn, dgain = _rms_bwd(h1_ref[...], g2_ref[...], acc_ref[...])
            dh1 = dh2_ref[...] + dn
            dh1_ref[...] = dh1
            dh1b_ref[...] = dh1.astype(BF16)
            dg2_ref[...] += dgain

    rev = lambda i: n_t - 1 - i
    rows = lambda w: pl.BlockSpec((tm, w), lambda i, j: (rev(i), 0))
    const = lambda shape: pl.BlockSpec(shape, lambda i, j: (0,) * len(shape))
    pair = lambda *s: pl.BlockSpec((2, None) + s, lambda i, j: (0, j) + (0,) * len(s))
    upb = pl.BlockSpec((2, None, tm, FFN_BLK), lambda i, j: (0, j, rev(i), 0))
    body, more_specs, more = _ordered_behind(body, 9, after)
    return pl.pallas_call(
        body, name="ffn_bwd", grid=(n_t, N_FFN_BLK),
        in_specs=[rows(D_MODEL), rows(D_MODEL), rows(D_MODEL), const((1, D_MODEL)), upb, upb,
                  pair(FFN_BLK, D_MODEL), pair(3, 1, FFN_BLK),
                  pl.BlockSpec((None, FFN_BLK, D_MODEL), lambda i, j: (j, 0, 0))] + more_specs,
        out_specs=[upb, rows(D_MODEL), rows(D_MODEL),
                   const((2, N_FFN_BLK, 1, FFN_BLK)), const((2, N_FFN_BLK, 3, 1, FFN_BLK)), const((1, D_MODEL))],
        out_shape=[jax.ShapeDtypeStruct((2, N_FFN_BLK, SEQ, FFN_BLK), BF16), jax.ShapeDtypeStruct((SEQ, D_MODEL), F32),
                   jax.ShapeDtypeStruct((SEQ, D_MODEL), BF16), jax.ShapeDtypeStruct((2, N_FFN_BLK, 1, FFN_BLK), F32),
                   jax.ShapeDtypeStruct((2, N_FFN_BLK, 3, 1, FFN_BLK), F32), jax.ShapeDtypeStruct((1, D_MODEL), F32)],
        scratch_shapes=[pltpu.VMEM((tm, D_MODEL), F32), pltpu.VMEM((2, N_FFN_BLK, SUBLANES, FFN_BLK), F32)],
        compiler_params=_params(("arbitrary", "arbitrary")),
    )(dh2, dh2b, h1, g2, up, pre, w_up, fcw, w_down, *more)


def _grad_tn(a_list, b, out_rows, name, after=None):
    n = len(a_list)
    ncol = b.shape[1]

    def body(*refs):
        a_refs, b_ref, o_ref = refs[:n], refs[n], refs[n + 1]
        j = pl.program_id(0)
        for k in range(n):
            @pl.when(j == k)
            def _(k=k):
                o_ref[...] = _dot_tn(a_refs[k][...], b_ref[...]).astype(BF16)

    full = lambda shape: pl.BlockSpec(shape, lambda j: (0,) * len(shape))
    body, more_specs, more = _ordered_behind(body, n + 1, after)
    return pl.pallas_call(
        body, name=name, grid=(n,),
        in_specs=[full((SEQ, out_rows))] * n + [full((SEQ, ncol))] + more_specs,
        out_specs=pl.BlockSpec((None, out_rows, ncol), lambda j: (j, 0, 0)),
        out_shape=jax.ShapeDtypeStruct((n, out_rows, ncol), BF16),
        compiler_params=_params(("arbitrary",)),
    )(*a_list, b, *more)


def _grad_tn_blocked(a, b, name, a_is_blocked=True, per_step=2):
    assert a_is_blocked
    nb, _, a_w = a.shape
    b_w = b.shape[-1]

    def body(a_ref, b_ref, o_ref):
        for p in range(per_step):
            o_ref[p] = _dot_tn(a_ref[p], b_ref[...]).astype(BF16)

    return pl.pallas_call(
        body, name=name, grid=(nb // per_step,),
        in_specs=[pl.BlockSpec((per_step, SEQ, a_w), lambda k: (k, 0, 0)), pl.BlockSpec((SEQ, b_w), lambda k: (0, 0))],
        out_specs=pl.BlockSpec((per_step, a_w, b_w), lambda k: (k, 0, 0)),
        out_shape=jax.ShapeDtypeStruct((nb, a_w, b_w), BF16),
        compiler_params=_params(("arbitrary",)),
    )(a, b)


def _out_bwd(dh1b, w_out, y_attn, gattn, after=None):
    tm = 512
    n_t = SEQ // tm

    def body(dh_ref, wo_ref, y_ref, ga_ref, dycn_ref, dy_ref, dga_ref):
        @pl.when(pl.program_id(0) == 0)
        def _():
            dga_ref[...] = jnp.zeros_like(dga_ref)

        dycat = _dot_nt(dh_ref[...], wo_ref[...])
        dycn_ref[...] = dycat[:, :CONV_WIDTH]
        dy, dga = _rms_bwd(y_ref[...], ga_ref[...], dycat[:, CONV_WIDTH:])
        dy_ref[...] = dy
        dga_ref[...] += dga

    rows = lambda w: pl.BlockSpec((tm, w), lambda i: (i, 0))
    const = lambda shape: pl.BlockSpec(shape, lambda i: (0,) * len(shape))
    body, more_specs, more = _ordered_behind(body, 4, after)
    return pl.pallas_call(
        body, name="out_bwd", grid=(n_t,),
        in_specs=[rows(D_MODEL), const((D_MODEL, D_MODEL)), rows(ATTN_WIDTH), const((1, ATTN_WIDTH))] + more_specs,
        out_specs=[rows(CONV_WIDTH), rows(ATTN_WIDTH), const((1, ATTN_WIDTH))],
        out_shape=[jax.ShapeDtypeStruct((SEQ, CONV_WIDTH), F32), jax.ShapeDtypeStruct((SEQ, ATTN_WIDTH), F32),
                   jax.ShapeDtypeStruct((1, ATTN_WIDTH), F32)],
        compiler_params=_params(("arbitrary",)),
    )(dh1b, w_out, y_attn, gattn, *more)


def _attn_bwd(qn, kn, v, dy, probs, psinks, bkt, after=None):
    n_b = SEQ // BLK

    def body(q_ref, k_ref, v_ref, dy_ref, p_ref, ps_ref, bkt_ref,
             dq_ref, dk_ref, dv_ref, dtbl_ref, dsink_ref, dbias_ref, dsacc_ref):
        i = pl.program_id(0)

        @pl.when(i == 0)
        def _():
            dbias_ref[...] = jnp.zeros_like(dbias_ref)
            dsacc_ref[...] = jnp.zeros_like(dsacc_ref)
            dk_ref[...] = jnp.zeros_like(dk_ref)
            dv_ref[...] = jnp.zeros_like(dv_ref)

        kb, prev, cur = _band_rows(k_ref, i)
        vb, _, _ = _band_rows(v_ref, i)
        upper, _ = _band_masks(i)
        q = q_ref[...]
        dy = dy_ref[...]
        psink = ps_ref[...]
        lane = lax.broadcasted_iota(jnp.int32, (BLK, 128), 1)
        dsink = jnp.zeros((BLK, 128), F32)
        dqs, dks, dvs = [], [], []
        for g in range(N_HEADS // GQA_GROUP):
            kv = slice(HEAD_DIM * g, HEAD_DIM * (g + 1))
            qg = _stack_heads(q, g)
            dog = _stack_heads(dy, g).astype(BF16)
            pb = p_ref[g]
            pg = pb.astype(F32)
            dprobs = _fold(_dot_nt(dog, vb[:, kv]), upper)
            dvs.append(_dot_tn(_unfold(pb, upper), dog))
            dsum = jnp.sum(pg * dprobs, axis=-1, keepdims=True)
            dlogits = pg * (dprobs - dsum)
            for t in range(GQA_GROUP):
                dsink = jnp.where(lane == GQA_GROUP * g + t, -psink * dsum[BLK * t:BLK * (t + 1)], dsink)
            dbias_ref[g] += dlogits
            ds = _unfold(dlogits * (HEAD_DIM ** -0.5), upper).astype(BF16)
            dqs.append(_dot(ds, kb[:, kv]))
            dks.append(_dot_tn(ds, qg))
        dsacc_ref[...] += dsink
        dq_ref[...] = _unstack_heads(dqs)
        dkb = jnp.concatenate(dks, axis=-1)
        dvb = jnp.concatenate(dvs, axis=-1)
        dk_ref[pl.ds(prev, BLK), :] += dkb[:BLK]
        dk_ref[pl.ds(cur, BLK), :] += dkb[BLK:]
        dv_ref[pl.ds(prev, BLK), :] += dvb[:BLK]
        dv_ref[pl.ds(cur, BLK), :] += dvb[BLK:]

        @pl.when(i == n_b - 1)
        def _():
            bkt = bkt_ref[...]
            row8 = lax.broadcasted_iota(jnp.int32, (N_HEADS, 128), 0)
            lane8 = lax.broadcasted_iota(jnp.int32, (N_HEADS, 128), 1)
            acc = jnp.zeros((N_HEADS, 128), F32)
            for h in range(N_HEADS):
                rows = slice(BLK * (h % GQA_GROUP), BLK * (h % GQA_GROUP + 1))
                dbh = dbias_ref[h // GQA_GROUP, rows, :]
                for b in range(NUM_BUCKETS):
                    acc = jnp.where((row8 == h) & (lane8 == b), jnp.sum(jnp.where(bkt == b, dbh, 0.0)), acc)
            dsink_ref[...] = jnp.sum(dsacc_ref[...], axis=0, keepdims=True)
            dtbl_ref[...] = acc

    const = lambda shape: pl.BlockSpec(shape, lambda i: (0,) * len(shape))
    rows = lambda w: pl.BlockSpec((BLK, w), lambda i: (i, 0))
    n_g = N_HEADS // GQA_GROUP
    body, more_specs, more = _ordered_behind(body, 7, after)
    return pl.pallas_call(
        body, name="attn_bwd", grid=(n_b,),
        in_specs=[rows(ATTN_WIDTH), const((SEQ, KV_WIDTH)), const((SEQ, KV_WIDTH)), rows(ATTN_WIDTH),
                  pl.BlockSpec((None, n_g, GROUP_ROWS, BLK), lambda i: (i, 0, 0, 0)), rows(128),
                  const((BLK, BLK))] + more_specs,
        out_specs=[rows(ATTN_WIDTH), const((SEQ, KV_WIDTH)), const((SEQ, KV_WIDTH)), const((N_HEADS, 128)), const((1, 128))],
        out_shape=[jax.ShapeDtypeStruct((SEQ, ATTN_WIDTH), F32), jax.ShapeDtypeStruct((SEQ, KV_WIDTH), F32),
                   jax.ShapeDtypeStruct((SEQ, KV_WIDTH), F32), jax.ShapeDtypeStruct((N_HEADS, 128), F32),
                   jax.ShapeDtypeStruct((1, 128), F32)],
        scratch_shapes=[pltpu.VMEM((n_g, GROUP_ROWS, BLK), F32), pltpu.VMEM((BLK, 128), F32)],
        compiler_params=_params(("arbitrary",)),
    )(qn, kn, v, dy, probs, psinks, bkt, *more)


def _mix_in_bwd(x, dh1, proj, dycn, dqn, dkn, dv, w_in_t, conv_w, g1, gq, gk, gconv):
    tm = 512
    n_t = SEQ // tm
    halo_blocks = tm // SUBLANES

    def body(x_ref, dh1_ref, proj_ref, halo_ref, dycn_ref, dqn_ref, dkn_ref, dv_ref, w_ref, cw_ref,
             g1_ref, gq_ref, gk_ref, gc_ref,
             dx_ref, dproj_ref, dcw_ref, dgc_ref, dgq_ref, dgk_ref, dg1_ref, next_ref):
        i = pl.program_id(0)
        first_tile = i == n_t - 1

        @pl.when(i == 0)
        def _():
            for r in (dcw_ref, dgc_ref, dgq_ref, dgk_ref, dg1_ref, next_ref):
                r[...] = jnp.zeros_like(r)

        proj = proj_ref[...]
        hp = halo_ref[...]
        gate_b = proj[:, 0:CONV_WIDTH]
        gate_c = proj[:, CONV_WIDTH:2 * CONV_WIDTH]
        hc = proj[:, 2 * CONV_WIDTH:3 * CONV_WIDTH]
        a = gate_c * hc
        a_halo = jnp.where(first_tile, 0.0, hp[:, CONV_WIDTH:2 * CONV_WIDTH] * hp[:, 2 * CONV_WIDTH:3 * CONV_WIDTH])
        cw = _taps(cw_ref[...])
        cv, a2, a1 = _conv3(a, cw, a_halo)
        dyc, dgc = _rms_bwd(gate_b * cv, gc_ref[...], dycn_ref[...])
        dgc_ref[...] += dgc
        dcv = dyc * gate_b
        dcw_ref[...] += jnp.concatenate(
            [jnp.sum(dcv * a2, axis=0, keepdims=True), jnp.sum(dcv * a1, axis=0, keepdims=True),
             jnp.sum(dcv * a, axis=0, keepdims=True)], axis=0)
        da = _conv3_bwd_input(dcv, cw, next_ref[...])
        next_ref[...] = dcv[:SUBLANES]
        q0 = 3 * CONV_WIDTH
        k0 = q0 + ATTN_WIDTH
        dq, dgq = _head_norm_bwd(proj[:, q0:k0], gq_ref[...], dqn_ref[...], N_HEADS)
        dk, dgk = _head_norm_bwd(proj[:, k0:k0 + KV_WIDTH], gk_ref[...], dkn_ref[...], 2)
        dgq_ref[...] += dgq
        dgk_ref[...] += dgk
        dproj = jnp.concatenate([dyc * cv, da * hc, da * gate_c, dq, dk, dv_ref[...]], axis=-1).astype(BF16)
        dproj_ref[...] = dproj
        du1 = _dot(dproj, w_ref[...])
        xv = x_ref[...]
        dn, dg1 = _rms_bwd(xv, g1_ref[...], du1)
        dx_ref[...] = dh1_ref[...] + dn
        dg1_ref[...] += dg1

    rev = lambda i: n_t - 1 - i
    rows = lambda w: pl.BlockSpec((tm, w), lambda i: (rev(i), 0))
    const = lambda shape: pl.BlockSpec(shape, lambda i: (0,) * len(shape))
    halo = pl.BlockSpec((SUBLANES, IN_WIDTH), lambda i: (jnp.maximum(rev(i) * halo_blocks - 1, 0), 0))
    return pl.pallas_call(
        body, name="mix_in_bwd", grid=(n_t,),
        in_specs=[rows(D_MODEL), rows(D_MODEL), rows(IN_WIDTH), halo, rows(CONV_WIDTH), rows(ATTN_WIDTH), rows(KV_WIDTH),
                  rows(KV_WIDTH), const((IN_WIDTH, D_MODEL)), const((3, CONV_WIDTH)), const((1, D_MODEL)),
                  const((1, HEAD_DIM)), const((1, HEAD_DIM)), const((1, CONV_WIDTH))],
        out_specs=[rows(D_MODEL), rows(IN_WIDTH), const((3, CONV_WIDTH)), const((1, CONV_WIDTH)),
                   const((1, HEAD_DIM)), const((1, HEAD_DIM)), const((1, D_MODEL))],
        out_shape=[jax.ShapeDtypeStruct((SEQ, D_MODEL), F32), jax.ShapeDtypeStruct((SEQ, IN_WIDTH), BF16),
                   jax.ShapeDtypeStruct((3, CONV_WIDTH), F32),
                   jax.ShapeDtypeStruct((1, CONV_WIDTH), F32), jax.ShapeDtypeStruct((1, HEAD_DIM), F32),
                   jax.ShapeDtypeStruct((1, HEAD_DIM), F32), jax.ShapeDtypeStruct((1, D_MODEL), F32)],
        scratch_shapes=[pltpu.VMEM((SUBLANES, CONV_WIDTH), F32)],
        compiler_params=_params(("arbitrary",)),
    )(x, dh1, proj, proj, dycn, dqn, dkn, dv, w_in_t, conv_w, g1, gq, gk, gconv)


def _grad_w_in(dproj, u1, after=None):
    bw = 768

    def body(a_ref, b_ref, o_ref):
        o_ref[...] = _dot_tn(a_ref[...], b_ref[...]).astype(BF16)

    body, more_specs, more = _ordered_behind(body, 2, after)
    return pl.pallas_call(
        body, name="grad_w_in", grid=(IN_WIDTH // bw,),
        in_specs=[pl.BlockSpec((SEQ, bw), lambda k: (0, k)), pl.BlockSpec((SEQ, D_MODEL), lambda k: (0, 0))] + more_specs,
        out_specs=pl.BlockSpec((bw, D_MODEL), lambda k: (k, 0)),
        out_shape=jax.ShapeDtypeStruct((IN_WIDTH, D_MODEL), BF16),
        compiler_params=_params(("arbitrary",)),
    )(dproj, u1, *more)


def _adamw_math(w, g, m, v):
    m = ADAM_B1 * m + (1.0 - ADAM_B1) * g
    v = ADAM_B2 * v + (1.0 - ADAM_B2) * (g * g)
    m_hat = m / (1.0 - ADAM_B1 ** ADAM_STEP)
    v_hat = v / (1.0 - ADAM_B2 ** ADAM_STEP)
    return -ADAM_LR * (m_hat / (jnp.sqrt(v_hat) + ADAM_EPS) + ADAM_WD * w), m, v


_ROW_G1, _ROW_G2, _ROW_OUT_NORMS, _ROW_FFN_B, _ROW_GQ, _ROW_GK, _ROW_SINKS, _ROW_LOSS, _ROW_TABLE = 0, 1, 2, 3, 11, 12, 13, 14, 16
SMALL_ROWS, SMALL_COLS = 24, 1024
_SMALL_NAMES = ("norm_mix_g", "norm_ffn_g", "out_norm_conv_g", "out_norm_attn_g", "ffn_conv_b", "q_norm_g", "k_norm_g",
                "sinks", "rel_bias_table")


def _pack_small_grads(dg1, dg2, dgconv, dgattn, dfb, dgq, dgk, dsinks, dtbl_t, loss_acc):
    def body(dg1_ref, dg2_ref, dgc_ref, dga_ref, dfb_ref, dgq_ref, dgk_ref, ds_ref, dt_ref, loss_ref, o_ref):
        o_ref[...] = jnp.zeros_like(o_ref)
        o_ref[_ROW_G1:_ROW_G1 + 1, :] = dg1_ref[...]
        o_ref[_ROW_G2:_ROW_G2 + 1, :] = dg2_ref[...]
        o_ref[_ROW_OUT_NORMS:_ROW_OUT_NORMS + 1, 0:CONV_WIDTH] = dgc_ref[...]
        o_ref[_ROW_OUT_NORMS:_ROW_OUT_NORMS + 1, CONV_WIDTH:] = dga_ref[...]
        for k in range(N_DEV):
            o_ref[_ROW_FFN_B + k:_ROW_FFN_B + k + 1, 0:FFN_BLK] = dfb_ref[k // N_FFN_BLK, k % N_FFN_BLK]
        o_ref[_ROW_GQ:_ROW_GQ + 1, 0:HEAD_DIM] = dgq_ref[...]
        o_ref[_ROW_GK:_ROW_GK + 1, 0:HEAD_DIM] = dgk_ref[...]
        o_ref[_ROW_SINKS:_ROW_SINKS + 1, 0:128] = ds_ref[...]
        o_ref[_ROW_LOSS:_ROW_LOSS + 1, 0:128] = loss_ref[0:1, :]
        o_ref[_ROW_TABLE:_ROW_TABLE + N_HEADS, 0:128] = dt_ref[...]

    return pl.pallas_call(body, name="pack_small_grads", out_shape=jax.ShapeDtypeStruct((SMALL_ROWS, SMALL_COLS), F32))(
        dg1, dg2, dgconv, dgattn, dfb, dgq, dgk, dsinks, dtbl_t, loss_acc)


def _adamw_small(recv, params, after):
    names = _SMALL_NAMES
    n = len(names)

    def grad_of(g, name, k=None):
        if name == "norm_mix_g":
            return g[_ROW_G1:_ROW_G1 + 1, :]
        if name == "norm_ffn_g":
            return g[_ROW_G2:_ROW_G2 + 1, :]
        if name == "out_norm_conv_g":
            return g[_ROW_OUT_NORMS:_ROW_OUT_NORMS + 1, 0:CONV_WIDTH]
        if name == "out_norm_attn_g":
            return g[_ROW_OUT_NORMS:_ROW_OUT_NORMS + 1, CONV_WIDTH:]
        if name == "ffn_conv_b":
            return g[_ROW_FFN_B + k:_ROW_FFN_B + k + 1, 0:FFN_BLK]
        if name == "q_norm_g":
            return g[_ROW_GQ:_ROW_GQ + 1, 0:HEAD_DIM]
        if name == "k_norm_g":
            return g[_ROW_GK:_ROW_GK + 1, 0:HEAD_DIM]
        if name == "sinks":
            return g[_ROW_SINKS:_ROW_SINKS + 1, 0:N_HEADS]
        return g[_ROW_TABLE:_ROW_TABLE + N_HEADS, 0:NUM_BUCKETS]

    def body(r_ref, *refs):
        ins, outs, loss_ref = refs[:3 * n], refs[3 * n:7 * n], refs[7 * n]
        g = r_ref[0]
        for s in range(1, N_DEV):
            g = g + r_ref[s]
        loss_ref[...] = g[_ROW_LOSS:_ROW_LOSS + 1, 0:128]
        for i, name in enumerate(names):
            w_ref, m_ref, v_ref = ins[3 * i:3 * i + 3]
            o = outs[4 * i:4 * i + 4]
            cols = [slice(FFN_BLK * k, FFN_BLK * (k + 1)) for k in range(N_DEV)] if name == "ffn_conv_b" else [slice(None)]
            for k, cs in enumerate(cols):
                gk = grad_of(g, name, k)
                d, m2, v2 = _adamw_math(w_ref[:, cs], gk, m_ref[:, cs], v_ref[:, cs])
                o[0][:, cs], o[1][:, cs], o[2][:, cs], o[3][:, cs] = gk, d, m2, v2

    flat = [a for name in names for a in params[name]]
    body, more_specs, more = _ordered_behind(body, 1 + 3 * n, after)
    vmem = pl.BlockSpec(memory_space=pltpu.VMEM)
    out = pl.pallas_call(
        body, name="adamw_small",
        in_specs=[vmem] * (1 + 3 * n) + more_specs,
        out_shape=[jax.ShapeDtypeStruct(params[name][0].shape, F32) for name in names for _ in range(4)]
        + [jax.ShapeDtypeStruct((1, 128), F32)],
        compiler_params=pltpu.CompilerParams(vmem_limit_bytes=VMEM_LIMIT),
    )(recv, *flat, *more)
    return {name: tuple(out[4 * i:4 * i + 4]) for i, name in enumerate(names)}, out[4 * n]


def _adamw_direct(w, m, v, own, recv, me, name, row_blocks=1, after=None):
    rb = w.shape[0] // row_blocks
    cols = w.shape[1]

    def body(me_ref, w_ref, m_ref, v_ref, o_ref, r_ref, g_o, d_o, m_o, v_o):
        g = o_ref[...].astype(F32)
        for s in range(N_DEV - 1):
            g = g + r_ref[s].astype(F32)
        g_o[...] = g
        d_o[...], m_o[...], v_o[...] = _adamw_math(w_ref[...], g, m_ref[...], v_ref[...])

    blk = pl.BlockSpec((rb, cols), lambda i, me_ref: (i, 0))
    oblk = pl.BlockSpec((None, rb, cols), lambda i, me_ref: (me_ref[0], i, 0))
    rblk = pl.BlockSpec((N_DEV - 1, rb, cols), lambda i, me_ref: (0, i, 0))
    body, more_specs, more = _ordered_behind(body, 6, after)
    return pl.pallas_call(
        body, name=name,
        grid_spec=pltpu.PrefetchScalarGridSpec(num_scalar_prefetch=1, grid=(row_blocks,),
                                               in_specs=[blk, blk, blk, oblk, rblk] + more_specs, out_specs=[blk] * 4),
        out_shape=[jax.ShapeDtypeStruct(w.shape, F32)] * 4,
        compiler_params=_params(("arbitrary",)),
    )(me, w, m, v, own, recv, *more)


def _adamw(w, m, v, part, recv, chip, name, row_blocks=1, after=None):
    rb = w.shape[0] // row_blocks
    tail = w.shape[1:]
    zeros = (0,) * len(tail)

    def body(chip_ref, w_ref, m_ref, v_ref, p_ref, r_ref, g_o, d_o, m_o, v_o):
        g = p_ref[...].astype(F32)
        for s in range(3):
            g = g + r_ref[s].astype(F32)
        g_o[...] = g
        d_o[...], m_o[...], v_o[...] = _adamw_math(w_ref[...], g, m_ref[...], v_ref[...])

    blk = pl.BlockSpec((rb,) + tail, lambda i, chip_ref: (i,) + zeros)
    pblk = pl.BlockSpec((None, rb) + tail, lambda i, chip_ref: (chip_ref[0], i) + zeros)
    rblk = pl.BlockSpec((3, rb) + tail, lambda i, chip_ref: (0, i) + zeros)
    body, more_specs, more = _ordered_behind(body, 6, after)
    return pl.pallas_call(
        body, name=name,
        grid_spec=pltpu.PrefetchScalarGridSpec(num_scalar_prefetch=1, grid=(row_blocks,),
                                               in_specs=[blk, blk, blk, pblk, rblk] + more_specs, out_specs=[blk] * 4),
        out_shape=[jax.ShapeDtypeStruct(w.shape, F32)] * 4,
        compiler_params=_params(("arbitrary",)),
    )(chip, w, m, v, part, recv, *more)


def kernel(x, norm_mix_g, w_in, conv_w, q_norm_g, k_norm_g, rel_bias_table, sinks, out_norm_conv_g, out_norm_attn_g, w_out, norm_ffn_g, w_up, ffn_conv_w, ffn_conv_b, w_down, loss_target, m_norm_mix_g, m_w_in, m_conv_w, m_q_norm_g, m_k_norm_g, m_rel_bias_table, m_sinks, m_out_norm_conv_g, m_out_norm_attn_g, m_w_out, m_norm_ffn_g, m_w_up, m_ffn_conv_w, m_ffn_conv_b, m_w_down, v_norm_mix_g, v_w_in, v_conv_w, v_q_norm_g, v_k_norm_g, v_rel_bias_table, v_sinks, v_out_norm_conv_g, v_out_norm_attn_g, v_w_out, v_norm_ffn_g, v_w_up, v_ffn_conv_w, v_ffn_conv_b, v_w_down):
    p = dict(norm_mix_g=norm_mix_g, w_in=w_in, conv_w=conv_w, q_norm_g=q_norm_g, k_norm_g=k_norm_g,
             rel_bias_table=rel_bias_table, sinks=sinks, out_norm_conv_g=out_norm_conv_g, out_norm_attn_g=out_norm_attn_g,
             w_out=w_out, norm_ffn_g=norm_ffn_g, w_up=w_up, ffn_conv_w=ffn_conv_w, ffn_conv_b=ffn_conv_b, w_down=w_down)
    m = dict(norm_mix_g=m_norm_mix_g, w_in=m_w_in, conv_w=m_conv_w, q_norm_g=m_q_norm_g, k_norm_g=m_k_norm_g,
             rel_bias_table=m_rel_bias_table, sinks=m_sinks, out_norm_conv_g=m_out_norm_conv_g,
             out_norm_attn_g=m_out_norm_attn_g, w_out=m_w_out, norm_ffn_g=m_norm_ffn_g, w_up=m_w_up,
             ffn_conv_w=m_ffn_conv_w, ffn_conv_b=m_ffn_conv_b, w_down=m_w_down)
    v = dict(norm_mix_g=v_norm_mix_g, w_in=v_w_in, conv_w=v_conv_w, q_norm_g=v_q_norm_g, k_norm_g=v_k_norm_g,
             rel_bias_table=v_rel_bias_table, sinks=v_sinks, out_norm_conv_g=v_out_norm_conv_g,
             out_norm_attn_g=v_out_norm_attn_g, w_out=v_w_out, norm_ffn_g=v_norm_ffn_g, w_up=v_w_up,
             ffn_conv_w=v_ffn_conv_w, ffn_conv_b=v_ffn_conv_b, w_down=v_w_down)

    xs, tgt = x[0], loss_target[0]
    g1, g2, gq, gk, gconv, gattn = norm_mix_g, norm_ffn_g, q_norm_g, k_norm_g, out_norm_conv_g, out_norm_attn_g
    ix, iy, ic = _coords()
    core = ic.astype(jnp.int32).reshape(1)
    chip = (2 * ix + iy).astype(jnp.int32).reshape(1)
    me = _lin(ix, iy, ic).astype(jnp.int32).reshape(1)
    bkt = jnp.asarray(_bucket_map())
    tr = lambda a: a[0].T
    taps = lambda a: jnp.transpose(a, (1, 0, 2))
    tbl_t = rel_bias_table.T

    wi_l, cw_l = _place_shards(me, [tr(w_in), taps(conv_w)], [BF16, F32], "place_mixer_shards")
    finish_a, token_a = _all_gather_split([wi_l, cw_l], "mixer", None)
    wo_l, wu_l, wd_l, fcw_l = _place_shards(me, [w_out[0], tr(w_up), w_down[0], taps(ffn_conv_w)],
                                            [BF16, BF16, BF16, F32], "place_ffn_shards", after=token_a)
    ffn_stage2, ffn_stage3, token_b = _all_gather_tree([wo_l, wu_l, wd_l, fcw_l], "ffn", token_a)
    wi_g, cw_g = finish_a(token_b)
    w_in_t = wi_g.reshape(IN_WIDTH, D_MODEL)
    conv_w_f = jnp.transpose(cw_g[:, :, 0, :], (1, 0, 2)).reshape(3, CONV_WIDTH)

    proj, u1, ycn, qn, kn, vv = _mix_in_fwd(xs, g1, w_in_t, conv_w_f, gq, gk, gconv)
    token_b2 = ffn_stage2(ycn)
    y_attn, yan, probs, psinks = _attn_fwd(qn, kn, vv, tbl_t, sinks, bkt, gattn, after=token_b2)
    wo_g, wu_g, wd_g, fcw_g = ffn_stage3(yan)
    w_out_f = wo_g.reshape(D_MODEL, D_MODEL)
    w_down_f = wd_g.reshape(N_FFN_BLK, FFN_BLK, D_MODEL)
    w_up_f = wu_g.reshape(2, N_FFN_BLK, FFN_BLK, D_MODEL)
    fcw_f = fcw_g.reshape(2, N_FFN_BLK, 3, 1, FFN_BLK)
    fcb = ffn_conv_b.reshape(2, N_FFN_BLK, 1, FFN_BLK)
    h1, u2, up, pre, act, dh2, dh2b, loss_acc = _ffn_fwd_token_major(
        xs, ycn, yan, w_out_f, g2, w_up_f, fcw_f, fcb, w_down_f, tgt)

    dw_down = _grad_tn_blocked(act, dh2b, "grad_w_down", a_is_blocked=True).reshape(N_DEV, D_FF // N_DEV, D_MODEL)
    plan_d, slots_d = _scatter_plan(1)
    d_sem = _split_start("scatter_w_down_start", [dw_down], [lax.empty((N_DEV - 1,) + dw_down.shape[1:], BF16)],
                         plan_d, None)
    dup, dh1, dh1b, dfb, dfcw, dg2 = _ffn_bwd_token_major(dh2, dh2b, h1, g2, up, pre, w_up_f, fcw_f, w_down_f,
                                                          after=d_sem[4])
    dw_up = _grad_tn_blocked(dup.reshape(N_DEV, SEQ, FFN_BLK), u2, "grad_w_up", a_is_blocked=True)
    dw_out = _grad_tn([ycn, yan], dh1b, CONV_WIDTH, "grad_w_out").reshape(N_DEV, D_MODEL // N_DEV, D_MODEL)
    out_bwd = {}

    def behind_ffn(token):
        out_bwd["r"] = _out_bwd(dh1b, w_out_f, y_attn, gattn, after=token)
        return out_bwd["r"][0]

    finish_ffn, token_ffn = _reduce_scatter_split(
        [dw_up, dw_out, dfcw.reshape(N_DEV, 3, 1, FFN_BLK)], "ffn", core, behind_ffn)
    dycn, dy_attn, dgattn = out_bwd["r"]
    dqn, dkn, dv, dtbl_t, dsinks = _attn_bwd(qn, kn, vv, dy_attn, probs, psinks, bkt, after=token_ffn)
    dx, dproj, dcw, dgconv, dgq, dgk, dg1 = _mix_in_bwd(xs, dh1, proj, dycn, dqn, dkn, dv, w_in_t, conv_w_f,
                                                             g1, gq, gk, gconv)
    packed = _pack_small_grads(dg1, dg2, dgconv, dgattn, dfb, dgq, dgk, dsinks, dtbl_t, loss_acc)
    plan_s, slots_s = _broadcast_plan()
    s_sem, r_sem, src_s, land_s, token_s = _split_start(
        "gather_small_start", [packed], [jnp.broadcast_to(packed[None], (N_DEV,) + packed.shape)], plan_s, None)
    dw_in_t = _grad_w_in(dproj, u1, after=token_s).reshape(N_DEV, IN_WIDTH // N_DEV, D_MODEL)
    dcw_b = jnp.transpose(dcw.reshape(3, N_DEV, 1, CONV_WIDTH // N_DEV), (1, 0, 2, 3))
    adam = {}
    ffn_got = {}

    def behind_mixer(token):
        ffn_got["r"] = finish_ffn(token)
        return ffn_got["r"][1][0]

    finish_mixer, token_mixer = _reduce_scatter_split([dw_in_t, dcw_b], "mixer", core, behind_mixer)
    (p_wu, p_wo, p_fcw), (r_wu, r_wo, r_fcw) = ffn_got["r"]
    (own_wd,), (r_wd,) = _split_wait("scatter_w_down_wait", d_sem[0], d_sem[1], d_sem[2], d_sem[3], plan_d, slots_d,
                                     token_mixer)
    adam["w_down"] = _adamw_direct(w_down[0], m_w_down[0], v_w_down[0], own_wd, r_wd, me, "adamw_w_down", row_blocks=2)
    adam_up = _adamw(tr(w_up), tr(m_w_up), tr(v_w_up), p_wu, r_wu, chip, "adamw_w_up", row_blocks=4,
                     after=adam["w_down"][0])
    adam["w_out"] = _adamw(w_out[0], m_w_out[0], v_w_out[0], p_wo, r_wo, chip, "adamw_w_out", after=adam_up[0])
    adam_fcw = _adamw(taps(ffn_conv_w), taps(m_ffn_conv_w), taps(v_ffn_conv_w), p_fcw, r_fcw, chip, "adamw_ffn_conv_w",
                      after=adam["w_out"][0])
    _, (r_small,) = _split_wait("gather_small_wait", s_sem, r_sem, src_s, land_s, plan_s, slots_s, adam_fcw[0])
    small_in = {k: (p[k], m[k], v[k]) for k in _SMALL_NAMES}
    small_in["rel_bias_table"] = (tbl_t, m_rel_bias_table.T, v_rel_bias_table.T)
    small_out, loss_row = _adamw_small(r_small, small_in, None)
    (p_wi, p_cw), (r_wi, r_cw) = finish_mixer(loss_row)
    adam_in = _adamw(tr(w_in), tr(m_w_in), tr(v_w_in), p_wi, r_wi, chip, "adamw_w_in")
    adam_cw = _adamw(taps(conv_w), taps(m_conv_w), taps(v_conv_w), p_cw, r_cw, chip, "adamw_conv_w")

    res = {k: tuple(a[None] for a in t) for k, t in adam.items()}
    res["w_up"] = tuple(a.T[None] for a in adam_up)
    res["w_in"] = tuple(a.T[None] for a in adam_in)
    res["ffn_conv_w"] = tuple(taps(a) for a in adam_fcw)
    res["conv_w"] = tuple(taps(a) for a in adam_cw)
    res.update(small_out)
    res["rel_bias_table"] = tuple(a.T for a in small_out["rel_bias_table"])
    loss = loss_row[0, 0]
    order = ("norm_mix_g", "w_in", "conv_w", "q_norm_g", "k_norm_g", "rel_bias_table", "sinks", "out_norm_conv_g",
             "out_norm_attn_g", "w_out", "norm_ffn_g", "w_up", "ffn_conv_w", "ffn_conv_b", "w_down")
    return (loss, dx[None], *[res[k][0] for k in order], *[res[k][1] for k in order],
            *[res[k][2] for k in order], *[res[k][3] for k in order])
```

```python
import functools
import math

import numpy as np
import jax
import jax.numpy as jnp
from jax import lax
from jax.experimental import pallas as pl
from jax.experimental.pallas import tpu as pltpu

F32 = jnp.float32
BF16 = jnp.bfloat16

SEQ = 2048
D_MODEL = 1024
CONV_WIDTH = 512
ATTN_WIDTH = 512
KV_WIDTH = 128
HEAD_DIM = 64
N_HEADS = 8
GQA_GROUP = 4
IN_WIDTH = 2304
D_FF = 2816
BLK = 128
NUM_BUCKETS = 32
EPS = 1e-6
NEG_INF = -1e30
ADAM_LR = 0.001
ADAM_B1 = 0.9
ADAM_B2 = 0.999
ADAM_EPS = 1e-08
ADAM_WD = 0.01
ADAM_STEP = 10

N_DEV = 8
FFN_BLK = 2 * D_FF // N_DEV
N_FFN_BLK = D_FF // FFN_BLK
SUBLANES = 8
VMEM_LIMIT = 56 * 1024 * 1024

_MESH = pl.DeviceIdType.MESH
_ANY = pl.BlockSpec(memory_space=pl.ANY)


def _params(sem):
    return pltpu.CompilerParams(dimension_semantics=sem, vmem_limit_bytes=VMEM_LIMIT)


def _ordered_behind(body, pos, after):
    if after is None:
        return body, [], []
    return (lambda *refs: body(*refs[:pos], *refs[pos + 1:])), [_ANY], [after]


def _dot(a, b):
    return jnp.dot(a, b, preferred_element_type=F32)


def _dot_nt(a, b):
    return lax.dot_general(a, b, (((1,), (1,)), ((), ())), preferred_element_type=F32)


def _dot_tn(a, b):
    return lax.dot_general(a, b, (((0,), (0,)), ((), ())), preferred_element_type=F32)


def _shift_down(x, s, halo):
    r = pltpu.roll(x, s, axis=0)
    hr = pltpu.roll(halo, s, axis=0)
    row = lax.broadcasted_iota(jnp.int32, halo.shape, 0)
    top = jnp.where(row < s, hr, r[:SUBLANES])
    return jnp.concatenate([top, r[SUBLANES:]], axis=0)


def _shift_up(x, s, halo):
    n = x.shape[0]
    r = pltpu.roll(x, n - s, axis=0)
    hr = pltpu.roll(halo, SUBLANES - s, axis=0)
    row = lax.broadcasted_iota(jnp.int32, halo.shape, 0)
    bot = jnp.where(row >= SUBLANES - s, hr, r[n - SUBLANES:])
    return jnp.concatenate([r[:n - SUBLANES], bot], axis=0)


def _taps(w):
    return (w[0], w[1], w[2]) if len(w.shape) == 3 else (w[0:1], w[1:2], w[2:3])


def _conv3(x, w, halo):
    x2 = _shift_down(x, 2, halo)
    x1 = _shift_down(x, 1, halo)
    return x2 * w[0] + x1 * w[1] + x * w[2], x2, x1


def _conv3_bwd_input(dy, w, halo_next):
    return dy * w[2] + _shift_up(dy, 1, halo_next) * w[1] + _shift_up(dy, 2, halo_next) * w[0]


def _rstd(x):
    return lax.rsqrt(jnp.mean(x * x, axis=-1, keepdims=True) + EPS)


def _rms_bwd(x, g, dy):
    r = _rstd(x)
    n = x * r
    dn = dy * g
    dx = r * (dn - n * jnp.mean(dn * n, axis=-1, keepdims=True))
    return dx, jnp.sum(dy * n, axis=0, keepdims=True)


def _head_mean(x):
    width = x.shape[-1]
    ri = lax.broadcasted_iota(jnp.int32, (width, width), 0) // HEAD_DIM
    ci = lax.broadcasted_iota(jnp.int32, (width, width), 1) // HEAD_DIM
    ones = jnp.where(ri == ci, 1.0, 0.0).astype(BF16)
    hi = x.astype(BF16)
    lo = (x - hi.astype(F32)).astype(BF16)
    return (_dot(hi, ones) + _dot(lo, ones)) * (1.0 / HEAD_DIM)


def _head_norm(x, g, heads):
    return x * lax.rsqrt(_head_mean(x * x) + EPS) * jnp.tile(g, (1, heads))


def _head_norm_bwd(x, g, dy, heads):
    r = lax.rsqrt(_head_mean(x * x) + EPS)
    n = x * r
    dn = dy * jnp.tile(g, (1, heads))
    dx = r * (dn - n * _head_mean(dn * n))
    per_lane = jnp.sum(dy * n, axis=0, keepdims=True)
    dg = per_lane[:, 0:HEAD_DIM]
    for h in range(1, heads):
        dg = dg + per_lane[:, HEAD_DIM * h:HEAD_DIM * (h + 1)]
    return dx, dg


def _bucket_map():
    q = np.arange(BLK)[:, None]
    j = np.arange(BLK)[None, :]
    n = np.where(j > q, q + BLK - j, q - j)
    nf = np.maximum(n, 1).astype(np.float32)
    max_exact = NUM_BUCKETS // 2
    large = max_exact + (np.log(nf / max_exact) / math.log(BLK / max_exact) * (NUM_BUCKETS - max_exact)).astype(np.int32)
    large = np.minimum(large, NUM_BUCKETS - 1)
    return np.where(n < max_exact, n, large).astype(np.int32)


def _coords():
    return lax.axis_index("x"), lax.axis_index("y"), lax.axis_index("c")


def _lin(px, py, pc):
    return 4 * px + 2 * py + pc


_HBM = pl.BlockSpec(memory_space=pltpu.HBM)
_SEM = pl.BlockSpec(memory_space=pltpu.SEMAPHORE)
_EFFECT = pltpu.SideEffectType.DATAFLOW_SIDE_EFFECTING


def _in_hbm(a):
    return pltpu.with_memory_space_constraint(a, pltpu.HBM)


SIBLING_ONLY = 1


def _split_start(name, srcs, lands, plan, after, sibling_only=False):
    ns, nl = len(srcs), len(lands)
    n_copies = len(plan(0, 0, 0))
    n_after = 0 if after is None else 1

    def body(*refs):
        src_refs, land_refs = refs[:ns + nl], refs[ns:ns + nl]
        send_sems, recv_sems = refs[ns + nl + n_after], refs[ns + nl + n_after + 1]
        token = refs[-1]
        if sibling_only:
            x, y, c = _coords()
            barrier = pltpu.get_barrier_semaphore()
            pl.semaphore_signal(barrier, inc=1, device_id=(x, y, 1 - c), device_id_type=_MESH)
            pl.semaphore_wait(barrier, 1)
        for k, (a, s_slot, l, d_slot, dev) in enumerate(plan(*_coords())):
            src = src_refs[a] if s_slot is None else src_refs[a].at[s_slot]
            pltpu.make_async_remote_copy(src_ref=src, dst_ref=land_refs[l].at[d_slot], send_sem=send_sems.at[k],
                                         recv_sem=recv_sems.at[k], device_id=dev, device_id_type=_MESH).start()
        token[...] = jnp.zeros_like(token)

    arrs = list(srcs) + list(lands)
    out = pl.pallas_call(
        body, name=name,
        out_shape=(pltpu.SemaphoreType.DMA((n_copies,)), pltpu.SemaphoreType.DMA((n_copies,)),
                   *[pltpu.HBM(a.shape, a.dtype) for a in arrs], jax.ShapeDtypeStruct((SUBLANES, 128), F32)),
        in_specs=[_HBM] * (ns + nl) + [_ANY] * n_after,
        out_specs=(_SEM, _SEM, *[_HBM] * (ns + nl), pl.BlockSpec(memory_space=pltpu.VMEM)),
        input_output_aliases={i: 2 + i for i in range(ns + nl)},
        compiler_params=pltpu.CompilerParams(has_side_effects=_EFFECT,
                                             collective_id=SIBLING_ONLY if sibling_only else None),
    )(*[_in_hbm(a) for a in arrs], *([] if after is None else [after]))
    return out[0], out[1], list(out[2:2 + ns]), list(out[2 + ns:2 + ns + nl]), out[-1]


def _split_wait(name, send_sems, recv_sems, srcs, lands, plan, recv_slots, after):
    ns, nl = len(srcs), len(lands)

    def body(*refs):
        src_refs, land_refs = refs[:ns + nl], refs[ns:ns + nl]
        send_sems, recv_sems = refs[ns + nl], refs[ns + nl + 1]
        coords = _coords()
        slots = recv_slots(*coords)
        for k, (a, s_slot, l, _, dev) in enumerate(plan(*coords)):
            src = src_refs[a] if s_slot is None else src_refs[a].at[s_slot]
            cp = pltpu.make_async_remote_copy(src_ref=src, dst_ref=land_refs[l].at[slots[k]], send_sem=send_sems.at[k],
                                              recv_sem=recv_sems.at[k], device_id=dev, device_id_type=_MESH)
            cp.wait_send()
            cp.wait_recv()

    arrs = list(srcs) + list(lands)
    out = pl.pallas_call(
        body, name=name,
        out_shape=tuple(pltpu.HBM(a.shape, a.dtype) for a in arrs),
        in_specs=[_HBM] * (ns + nl) + [_SEM, _SEM, _ANY],
        out_specs=tuple([_HBM] * (ns + nl)),
        input_output_aliases={i: i for i in range(ns + nl)},
        compiler_params=pltpu.CompilerParams(has_side_effects=_EFFECT),
    )(*arrs, send_sems, recv_sems, after)
    return list(out[:ns]), list(out[ns:])


def _chips(x, y):
    return [(1 - x, y), (x, 1 - y), (1 - x, 1 - y)]


def _gather_plan_ici(n):
    def plan(x, y, c):
        me = _lin(x, y, c)
        out = []
        for a in range(n):
            out.append((a, me, a, me, (x, y, 1 - c)))
            out += [(a, me, a, me, (cx, cy, c)) for cx, cy in _chips(x, y)]
        return out

    def recv_slots(x, y, c):
        out = []
        for _ in range(n):
            out.append(_lin(x, y, 1 - c))
            out += [_lin(cx, cy, c) for cx, cy in _chips(x, y)]
        return out

    return plan, recv_slots


def _gather_plan_d2d(n):
    def plan(x, y, c):
        return [(a, _lin(cx, cy, c), a, _lin(cx, cy, c), (x, y, 1 - c)) for a in range(n) for cx, cy in _chips(x, y)]

    def recv_slots(x, y, c):
        return [_lin(cx, cy, 1 - c) for _ in range(n) for cx, cy in _chips(x, y)]

    return plan, recv_slots


def _all_gather_split(lands, tag, after):
    n = len(lands)
    plan1, slots1 = _gather_plan_ici(n)
    s1, r1, _, lands, token = _split_start(f"gather_{tag}_ici_start", [], lands, plan1, after)

    def finish(after):
        _, got = _split_wait(f"gather_{tag}_ici_wait", s1, r1, [], lands, plan1, slots1, after)
        plan2, slots2 = _gather_plan_d2d(n)
        s2, r2, _, got, token2 = _split_start(f"gather_{tag}_d2d_start", [], got, plan2, None, sibling_only=True)
        return _split_wait(f"gather_{tag}_d2d_wait", s2, r2, [], got, plan2, slots2, token2)[1]

    return finish, token


def _all_gather_tree(lands, tag, after):
    n = len(lands)

    def plan1(x, y, c):
        me = _lin(x, y, c)
        return [(a, me, a, me, dev) for a in range(n) for dev in ((x, y, 1 - c), (1 - x, y, c), (x, 1 - y, c))]

    def slots1(x, y, c):
        return [s for _ in range(n) for s in (_lin(x, y, 1 - c), _lin(1 - x, y, c), _lin(x, 1 - y, c))]

    def plan2(x, y, c):
        from_x, from_y = _lin(1 - x, y, c), _lin(x, 1 - y, c)
        north = c == 1
        passed = jnp.where(north, from_x, from_y)
        onward = (jnp.where(north, x, 1 - x), jnp.where(north, 1 - y, y), c)
        sib = (x, y, 1 - c)
        return [cp for a in range(n) for cp in ((a, passed, a, passed, onward), (a, from_x, a, from_x, sib),
                                                (a, from_y, a, from_y, sib))]

    def slots2(x, y, c):
        return [s for _ in range(n) for s in (_lin(1 - x, 1 - y, c), _lin(1 - x, y, 1 - c), _lin(x, 1 - y, 1 - c))]

    def plan3(x, y, c):
        diag = _lin(1 - x, 1 - y, c)
        return [(a, diag, a, diag, (x, y, 1 - c)) for a in range(n)]

    def slots3(x, y, c):
        return [_lin(1 - x, 1 - y, 1 - c)] * n

    s1, r1, _, lands, token = _split_start(f"gather_{tag}_1_start", [], lands, plan1, after)
    state = {}

    def stage2(after):
        _, got = _split_wait(f"gather_{tag}_1_wait", s1, r1, [], lands, plan1, slots1, after)
        state["s"], state["r"], _, state["lands"], token2 = _split_start(f"gather_{tag}_2_start", [], got, plan2, None)
        return token2

    def stage3(after):
        _, got = _split_wait(f"gather_{tag}_2_wait", state["s"], state["r"], [], state["lands"], plan2, slots2, after)
        s3, r3, _, got, token3 = _split_start(f"gather_{tag}_3_start", [], got, plan3, None, sibling_only=True)
        return _split_wait(f"gather_{tag}_3_wait", s3, r3, [], got, plan3, slots3, token3)[1]

    return stage2, stage3, token


_CHIP_LIST = ((0, 0), (0, 1), (1, 0), (1, 1))


def _reduce_plan_d2d(n):
    def plan(x, y, c):
        return [(a, _lin(qx, qy, 1 - c), a, q, (x, y, 1 - c)) for a in range(n) for q, (qx, qy) in enumerate(_CHIP_LIST)]

    def recv_slots(x, y, c):
        return [q for _ in range(n) for q in range(4)]

    return plan, recv_slots


def _reduce_plan_ici(n):
    def plan(x, y, c):
        return [(a, 2 * cx + cy, a, j, (cx, cy, c)) for a in range(n) for j, (cx, cy) in enumerate(_chips(x, y))]

    def recv_slots(x, y, c):
        return [j for _ in range(n) for j in range(3)]

    return plan, recv_slots


def _peers(x, y, c):
    return [(1 - x if r & 4 else x, 1 - y if r & 2 else y, 1 - c if r & 1 else c) for r in range(1, N_DEV)]


def _scatter_plan(n):
    def plan(x, y, c):
        return [(a, _lin(*peer), a, r, peer) for a in range(n) for r, peer in enumerate(_peers(x, y, c))]

    def recv_slots(x, y, c):
        return [r for _ in range(n) for r in range(N_DEV - 1)]

    return plan, recv_slots


def _broadcast_plan():
    peers = _peers

    def plan(x, y, c):
        return [(0, None, 0, _lin(x, y, c), peer) for peer in peers(x, y, c)]

    def recv_slots(x, y, c):
        return [_lin(*peer) for peer in peers(x, y, c)]

    return plan, recv_slots


def _chip_partial(grads, recvd, core, name):
    n = len(grads)

    def body(c_ref, *refs):
        for a in range(n):
            g_ref, r_ref, o_ref = refs[a], refs[n + a], refs[2 * n + a]
            o_ref[...] = (g_ref[...].astype(F32) + r_ref[...].astype(F32)).astype(o_ref.dtype)

    def blk(a, own):
        zeros = (0,) * (a.ndim - 1)
        return pl.BlockSpec((None,) + a.shape[1:],
                            (lambda q, c_ref: (2 * q + c_ref[0],) + zeros) if own else (lambda q, c_ref: (q,) + zeros))

    return pl.pallas_call(
        body, name=name,
        grid_spec=pltpu.PrefetchScalarGridSpec(
            num_scalar_prefetch=1, grid=(4,),
            in_specs=[blk(a, True) for a in grads] + [blk(a, False) for a in recvd],
            out_specs=[blk(a, False) for a in recvd]),
        out_shape=[jax.ShapeDtypeStruct(a.shape, a.dtype) for a in recvd],
        compiler_params=_params(("arbitrary",)),
    )(core, *grads, *recvd)


def _reduce_scatter_split(grads, tag, core, behind):
    n = len(grads)
    plan1, slots1 = _reduce_plan_d2d(n)
    lands1 = [lax.empty((4,) + a.shape[1:], a.dtype) for a in grads]
    s1, r1, srcs1, lands1, token1 = _split_start(f"reduce_{tag}_d2d_start", grads, lands1, plan1, None,
                                                 sibling_only=True)
    own, got = _split_wait(f"reduce_{tag}_d2d_wait", s1, r1, srcs1, lands1, plan1, slots1, behind(token1))
    parts = _chip_partial(own, got, core, f"reduce_{tag}_partial")
    plan2, slots2 = _reduce_plan_ici(n)
    lands2 = [lax.empty((3,) + a.shape[1:], a.dtype) for a in grads]
    s2, r2, srcs2, lands2, token2 = _split_start(f"reduce_{tag}_ici_start", parts, lands2, plan2, None)

    def finish(after):
        return _split_wait(f"reduce_{tag}_ici_wait", s2, r2, srcs2, lands2, plan2, slots2, after)

    return finish, token2


def _place_shards(me, shards, dtypes, name, after=None):
    n = len(shards)

    def body(me_ref, *refs):
        for a in range(n):
            refs[n + a][...] = refs[a][...].astype(dtypes[a])

    full = lambda s: pl.BlockSpec(s.shape, lambda i, me_ref: (0,) * s.ndim)
    slot = lambda s: pl.BlockSpec((None,) + s.shape, lambda i, me_ref: (me_ref[0],) + (0,) * s.ndim)
    body, more_specs, more = _ordered_behind(body, 1 + n, after)
    return pl.pallas_call(
        body, name=name,
        grid_spec=pltpu.PrefetchScalarGridSpec(num_scalar_prefetch=1, grid=(1,),
                                               in_specs=[full(s) for s in shards] + more_specs,
                                               out_specs=[slot(s) for s in shards]),
        out_shape=[jax.ShapeDtypeStruct((N_DEV,) + s.shape, d) for s, d in zip(shards, dtypes)],
        compiler_params=_params(("arbitrary",)),
    )(me, *shards, *more)


def _mix_in_fwd(x, g1, w_in_t, conv_w, gq, gk, gconv):
    tm = 512
    n_t = SEQ // tm

    def body(x_ref, g1_ref, w_ref, cw_ref, gq_ref, gk_ref, gc_ref,
             proj_ref, u1_ref, ycn_ref, qn_ref, kn_ref, v_ref, halo_ref):
        @pl.when(pl.program_id(0) == 0)
        def _():
            halo_ref[...] = jnp.zeros_like(halo_ref)

        xv = x_ref[...]
        u = (xv * _rstd(xv) * g1_ref[...]).astype(BF16)
        u1_ref[...] = u
        proj = _dot_nt(u, w_ref[...])
        proj_ref[...] = proj
        gate_b = proj[:, 0:CONV_WIDTH]
        a = proj[:, CONV_WIDTH:2 * CONV_WIDTH] * proj[:, 2 * CONV_WIDTH:3 * CONV_WIDTH]
        cv, _, _ = _conv3(a, _taps(cw_ref[...]), halo_ref[...])
        halo_ref[...] = a[tm - SUBLANES:]
        yc = gate_b * cv
        ycn_ref[...] = (yc * _rstd(yc) * gc_ref[...]).astype(BF16)
        q0 = 3 * CONV_WIDTH
        qn_ref[...] = _head_norm(proj[:, q0:q0 + ATTN_WIDTH], gq_ref[...], N_HEADS).astype(BF16)
        k0 = q0 + ATTN_WIDTH
        kn_ref[...] = _head_norm(proj[:, k0:k0 + KV_WIDTH], gk_ref[...], 2).astype(BF16)
        v_ref[...] = proj[:, k0 + KV_WIDTH:k0 + 2 * KV_WIDTH].astype(BF16)

    const = lambda shape: pl.BlockSpec(shape, lambda i: (0,) * len(shape))
    rows = lambda w: pl.BlockSpec((tm, w), lambda i: (i, 0))
    return pl.pallas_call(
        body, name="mix_in_fwd", grid=(n_t,),
        in_specs=[rows(D_MODEL), const((1, D_MODEL)), const((IN_WIDTH, D_MODEL)), const((3, CONV_WIDTH)),
                  const((1, HEAD_DIM)), const((1, HEAD_DIM)), const((1, CONV_WIDTH))],
        out_specs=[rows(IN_WIDTH), rows(D_MODEL), rows(CONV_WIDTH), rows(ATTN_WIDTH), rows(KV_WIDTH), rows(KV_WIDTH)],
        out_shape=[jax.ShapeDtypeStruct((SEQ, IN_WIDTH), F32), jax.ShapeDtypeStruct((SEQ, D_MODEL), BF16),
                   jax.ShapeDtypeStruct((SEQ, CONV_WIDTH), BF16),
                   jax.ShapeDtypeStruct((SEQ, ATTN_WIDTH), BF16), jax.ShapeDtypeStruct((SEQ, KV_WIDTH), BF16),
                   jax.ShapeDtypeStruct((SEQ, KV_WIDTH), BF16)],
        scratch_shapes=[pltpu.VMEM((SUBLANES, CONV_WIDTH), F32)],
        compiler_params=_params(("arbitrary",)),
    )(x, g1, w_in_t, conv_w, gq, gk, gconv)


GROUP_ROWS = GQA_GROUP * BLK


def _band_bias(tbl_ref, bkt, bias_ref):
    for h in range(N_HEADS):
        acc = jnp.zeros(bkt.shape, F32)
        for b in range(NUM_BUCKETS):
            acc = jnp.where(bkt == b, tbl_ref[h, b], acc)
        bias_ref[h // GQA_GROUP, BLK * (h % GQA_GROUP):BLK * (h % GQA_GROUP + 1), :] = acc


def _band_masks(i):
    qi = lax.broadcasted_iota(jnp.int32, (GROUP_ROWS, BLK), 0) & (BLK - 1)
    ji = lax.broadcasted_iota(jnp.int32, (GROUP_ROWS, BLK), 1)
    upper = ji > qi
    return upper, upper & (i == 0)


def _stack_heads(x, g):
    return jnp.concatenate([x[:, HEAD_DIM * h:HEAD_DIM * (h + 1)] for h in range(GQA_GROUP * g, GQA_GROUP * (g + 1))], axis=0)


def _unstack_heads(groups):
    return jnp.concatenate([p[BLK * t:BLK * (t + 1)] for p in groups for t in range(GQA_GROUP)], axis=-1)


def _per_head_rows(vals):
    row = lax.broadcasted_iota(jnp.int32, (GROUP_ROWS, 1), 0)
    col = jnp.full((GROUP_ROWS, 1), vals[GQA_GROUP - 1], F32)
    for t in range(GQA_GROUP - 2, -1, -1):
        col = jnp.where(row < BLK * (t + 1), vals[t], col)
    return col


def _band_rows(ref, i):
    prev = pl.multiple_of(jnp.maximum(i - 1, 0) * BLK, BLK)
    cur = pl.multiple_of(i * BLK, BLK)
    return jnp.concatenate([ref[pl.ds(prev, BLK), :], ref[pl.ds(cur, BLK), :]], axis=0), prev, cur


def _fold(band, upper):
    return jnp.where(upper, band[:, :BLK], band[:, BLK:])


def _unfold(tile, upper):
    return jnp.concatenate([jnp.where(upper, tile, 0.0), jnp.where(upper, 0.0, tile)], axis=1)


def _head_probs(qh, kh, bias, upper, dead, sink):
    logits = _fold(_dot_nt(qh, kh), upper) * (HEAD_DIM ** -0.5) + bias
    logits = jnp.where(dead, NEG_INF, logits)
    m = jnp.maximum(jnp.max(logits, axis=-1, keepdims=True), sink)
    p = jnp.exp(logits - m)
    es = jnp.exp(sink - m)
    den = jnp.sum(p, axis=-1, keepdims=True) + es
    return p / den, es / den


def _attn_fwd(qn, kn, v, tbl, sinks, bkt, gattn, after=None):
    n_b = SEQ // BLK

    def body(q_ref, k_ref, v_ref, tbl_ref, sink_ref, bkt_ref, ga_ref, y_ref, yn_ref, p_ref, ps_ref, bias_ref):
        i = pl.program_id(0)

        @pl.when(i == 0)
        def _():
            _band_bias(tbl_ref, bkt_ref[...], bias_ref)

        kb, _, _ = _band_rows(k_ref, i)
        vb, _, _ = _band_rows(v_ref, i)
        upper, dead = _band_masks(i)
        q = q_ref[...]
        lane = lax.broadcasted_iota(jnp.int32, (BLK, 128), 1)
        outs = []
        psinks = jnp.zeros((BLK, 128), F32)
        for g in range(N_HEADS // GQA_GROUP):
            kv = slice(HEAD_DIM * g, HEAD_DIM * (g + 1))
            sink = _per_head_rows([sink_ref[0, GQA_GROUP * g + t] for t in range(GQA_GROUP)])
            probs, psink = _head_probs(_stack_heads(q, g), kb[:, kv], bias_ref[g], upper, dead, sink)
            p_ref[g] = probs.astype(BF16)
            for t in range(GQA_GROUP):
                psinks = jnp.where(lane == GQA_GROUP * g + t, psink[BLK * t:BLK * (t + 1)], psinks)
            outs.append(_dot(_unfold(probs, upper).astype(BF16), vb[:, kv]))
        ps_ref[...] = psinks
        y = _unstack_heads(outs)
        y_ref[...] = y
        yn_ref[...] = (y * _rstd(y) * ga_ref[...]).astype(BF16)

    const = lambda shape: pl.BlockSpec(shape, lambda i: (0,) * len(shape))
    rows = lambda w: pl.BlockSpec((BLK, w), lambda i: (i, 0))
    smem = pl.BlockSpec(memory_space=pltpu.SMEM)
    body, more_specs, more = _ordered_behind(body, 7, after)
    return pl.pallas_call(
        body, name="attn_fwd", grid=(n_b,),
        in_specs=[rows(ATTN_WIDTH), const((SEQ, KV_WIDTH)), const((SEQ, KV_WIDTH)), smem, smem,
                  const((BLK, BLK)), const((1, ATTN_WIDTH))] + more_specs,
        out_specs=[rows(ATTN_WIDTH), rows(ATTN_WIDTH),
                   pl.BlockSpec((None, N_HEADS // GQA_GROUP, GROUP_ROWS, BLK), lambda i: (i, 0, 0, 0)), rows(128)],
        out_shape=[jax.ShapeDtypeStruct((SEQ, ATTN_WIDTH), F32), jax.ShapeDtypeStruct((SEQ, ATTN_WIDTH), BF16),
                   jax.ShapeDtypeStruct((n_b, N_HEADS // GQA_GROUP, GROUP_ROWS, BLK), BF16),
                   jax.ShapeDtypeStruct((SEQ, 128), F32)],
        scratch_shapes=[pltpu.VMEM((N_HEADS // GQA_GROUP, GROUP_ROWS, BLK), F32)],
        compiler_params=_params(("arbitrary",)),
    )(qn, kn, v, tbl, sinks, bkt, gattn, *more)


def _ffn_up(x, ycn, yan, w_out, g2, w_up, fcw, fcb, after=None):
    tm = 512
    n_t = SEQ // tm

    def body(x_ref, ycn_ref, yan_ref, wo_ref, g2_ref, wu_ref, cw_ref, b_ref,
             h1_ref, u2_ref, up_ref, pre_ref, act_ref, halo_ref):
        i, j = pl.program_id(0), pl.program_id(1)

        @pl.when(j == 0)
        def _():
            h1 = x_ref[...] + _dot(ycn_ref[...], wo_ref[0:CONV_WIDTH, :]) + _dot(yan_ref[...], wo_ref[CONV_WIDTH:, :])
            h1_ref[...] = h1
            u2_ref[...] = (h1 * _rstd(h1) * g2_ref[...]).astype(BF16)

        u2 = u2_ref[...]
        pre = []
        for s in range(2):
            up = _dot_nt(u2, wu_ref[s])
            up_ref[s] = up.astype(BF16)
            halo = jnp.where(i == 0, 0.0, halo_ref[s, j])
            pre.append(_conv3(up, _taps(cw_ref.at[s]), halo)[0] + b_ref[s])
            pre_ref[s] = pre[s].astype(BF16)
            halo_ref[s, j] = up[tm - SUBLANES:]
        g, val = pre
        act_ref[...] = (g * jax.nn.sigmoid(g) * val).astype(BF16)

    rows = lambda w: pl.BlockSpec((tm, w), lambda i, j: (i, 0))
    const = lambda shape: pl.BlockSpec(shape, lambda i, j: (0,) * len(shape))
    pair = lambda *s: pl.BlockSpec((2, None) + s, lambda i, j: (0, j) + (0,) * len(s))
    upb = pl.BlockSpec((2, None, tm, FFN_BLK), lambda i, j: (0, j, i, 0))
    body, more_specs, more = _ordered_behind(body, 8, after)
    return pl.pallas_call(
        body, name="ffn_up", grid=(n_t, N_FFN_BLK),
        in_specs=[rows(D_MODEL), rows(CONV_WIDTH), rows(ATTN_WIDTH), const((D_MODEL, D_MODEL)), const((1, D_MODEL)),
                  pair(FFN_BLK, D_MODEL), pair(3, 1, FFN_BLK), pair(1, FFN_BLK)] + more_specs,
        out_specs=[rows(D_MODEL), rows(D_MODEL), upb, upb, pl.BlockSpec((None, tm, FFN_BLK), lambda i, j: (j, i, 0))],
        out_shape=[jax.ShapeDtypeStruct((SEQ, D_MODEL), F32), jax.ShapeDtypeStruct((SEQ, D_MODEL), BF16),
                   jax.ShapeDtypeStruct((2, N_FFN_BLK, SEQ, FFN_BLK), BF16),
                   jax.ShapeDtypeStruct((2, N_FFN_BLK, SEQ, FFN_BLK), BF16),
                   jax.ShapeDtypeStruct((N_FFN_BLK, SEQ, FFN_BLK), BF16)],
        scratch_shapes=[pltpu.VMEM((2, N_FFN_BLK, SUBLANES, FFN_BLK), F32)],
        compiler_params=_params(("arbitrary", "arbitrary")),
    )(x, ycn, yan, w_out, g2, w_up, fcw, fcb, *more)


def _ffn_down(act, w_down, h1, tgt):
    tm = 512
    n_t = SEQ // tm

    def body(act_ref, wd_ref, h1_ref, tgt_ref, dh2_ref, dh2b_ref, loss_ref):
        @pl.when(pl.program_id(0) == 0)
        def _():
            loss_ref[...] = jnp.zeros_like(loss_ref)

        out = _dot(act_ref[0], wd_ref[0])
        for j in range(1, N_FFN_BLK):
            out = out + _dot(act_ref[j], wd_ref[j])
        err = h1_ref[...] + out - tgt_ref[...]
        loss_ref[...] += 0.5 * jnp.sum(err * err) / D_MODEL
        dh2 = err / D_MODEL
        dh2_ref[...] = dh2
        dh2b_ref[...] = dh2.astype(BF16)

    rows = lambda w: pl.BlockSpec((tm, w), lambda i: (i, 0))
    return pl.pallas_call(
        body, name="ffn_down", grid=(n_t,),
        in_specs=[pl.BlockSpec((N_FFN_BLK, tm, FFN_BLK), lambda i: (0, i, 0)),
                  pl.BlockSpec((N_FFN_BLK, FFN_BLK, D_MODEL), lambda i: (0, 0, 0)), rows(D_MODEL), rows(D_MODEL)],
        out_specs=[rows(D_MODEL), rows(D_MODEL), pl.BlockSpec((SUBLANES, 128), lambda i: (0, 0))],
        out_shape=[jax.ShapeDtypeStruct((SEQ, D_MODEL), F32), jax.ShapeDtypeStruct((SEQ, D_MODEL), BF16),
                   jax.ShapeDtypeStruct((SUBLANES, 128), F32)],
        compiler_params=_params(("arbitrary",)),
    )(act, w_down, h1, tgt)


def _out_proj(x, ycn, yan, w_out, g2):
    tm = 512

    def body(x_ref, ycn_ref, yan_ref, wo_ref, g2_ref, h1_ref, u2_ref):
        h1 = x_ref[...] + _dot(ycn_ref[...], wo_ref[0:CONV_WIDTH, :]) + _dot(yan_ref[...], wo_ref[CONV_WIDTH:, :])
        h1_ref[...] = h1
        u2_ref[...] = (h1 * _rstd(h1) * g2_ref[...]).astype(BF16)

    rows = lambda w: pl.BlockSpec((tm, w), lambda i: (i, 0))
    const = lambda shape: pl.BlockSpec(shape, lambda i: (0,) * len(shape))
    return pl.pallas_call(
        body, name="out_proj", grid=(SEQ // tm,),
        in_specs=[rows(D_MODEL), rows(CONV_WIDTH), rows(ATTN_WIDTH), const((D_MODEL, D_MODEL)), const((1, D_MODEL))],
        out_specs=[rows(D_MODEL), rows(D_MODEL)],
        out_shape=[jax.ShapeDtypeStruct((SEQ, D_MODEL), F32), jax.ShapeDtypeStruct((SEQ, D_MODEL), BF16)],
        compiler_params=_params(("arbitrary",)),
    )(x, ycn, yan, w_out, g2)


def _ffn_fwd(h1, u2, w_up, fcw, fcb, w_down, tgt):
    tm = 512
    n_t = SEQ // tm
    last = N_FFN_BLK - 1

    def body(u2_ref, wu_ref, cw_ref, b_ref, wd_ref, h1_ref, tgt_ref,
             up_ref, pre_ref, act_ref, dh2_ref, dh2b_ref, loss_ref, acc_ref, halo_ref):
        j, i = pl.program_id(0), pl.program_id(1)
        rows = pl.ds(pl.multiple_of(i * tm, tm), tm)

        @pl.when((i == 0) & (j == 0))
        def _():
            loss_ref[...] = jnp.zeros_like(loss_ref)

        u2 = u2_ref[...]
        pre = []
        for s in range(2):
            up = _dot_nt(u2, wu_ref[s])
            up_ref[s] = up.astype(BF16)
            halo = jnp.where(i == 0, 0.0, halo_ref[s])
            pre.append(_conv3(up, _taps(cw_ref.at[s]), halo)[0] + b_ref[s])
            pre_ref[s] = pre[s].astype(BF16)
            halo_ref[s] = up[tm - SUBLANES:]
        g, val = pre
        act = (g * jax.nn.sigmoid(g) * val).astype(BF16)
        act_ref[...] = act
        out = _dot(act, wd_ref[...])

        @pl.when(j == 0)
        def _():
            acc_ref[rows, :] = out

        @pl.when(j > 0)
        def _():
            acc_ref[rows, :] += out

        @pl.when(j == last)
        def _():
            err = h1_ref[...] + acc_ref[rows, :] - tgt_ref[...]
            loss_ref[...] += 0.5 * jnp.sum(err * err) / D_MODEL
            dh2 = err / D_MODEL
            dh2_ref[...] = dh2
            dh2b_ref[...] = dh2.astype(BF16)

    late = lambda w: pl.BlockSpec((tm, w), lambda j, i: (jnp.where(j == last, i, 0), 0))
    pair = lambda *s: pl.BlockSpec((2, None) + s, lambda j, i: (0, j) + (0,) * len(s))
    upb = pl.BlockSpec((2, None, tm, FFN_BLK), lambda j, i: (0, j, i, 0))
    return pl.pallas_call(
        body, name="ffn_fwd", grid=(N_FFN_BLK, n_t),
        in_specs=[pl.BlockSpec((tm, D_MODEL), lambda j, i: (i, 0)), pair(FFN_BLK, D_MODEL), pair(3, 1, FFN_BLK),
                  pair(1, FFN_BLK), pl.BlockSpec((None, FFN_BLK, D_MODEL), lambda j, i: (j, 0, 0)),
                  late(D_MODEL), late(D_MODEL)],
        out_specs=[upb, upb, pl.BlockSpec((None, tm, FFN_BLK), lambda j, i: (j, i, 0)), late(D_MODEL), late(D_MODEL),
                   pl.BlockSpec((SUBLANES, 128), lambda j, i: (0, 0))],
        out_shape=[jax.ShapeDtypeStruct((2, N_FFN_BLK, SEQ, FFN_BLK), BF16),
                   jax.ShapeDtypeStruct((2, N_FFN_BLK, SEQ, FFN_BLK), BF16),
                   jax.ShapeDtypeStruct((N_FFN_BLK, SEQ, FFN_BLK), BF16),
                   jax.ShapeDtypeStruct((SEQ, D_MODEL), F32), jax.ShapeDtypeStruct((SEQ, D_MODEL), BF16),
                   jax.ShapeDtypeStruct((SUBLANES, 128), F32)],
        scratch_shapes=[pltpu.VMEM((SEQ, D_MODEL), F32), pltpu.VMEM((2, SUBLANES, FFN_BLK), F32)],
        compiler_params=_params(("arbitrary", "arbitrary")),
    )(u2, w_up, fcw, fcb, w_down, h1, tgt)


def _ffn_fwd_token_major(x, ycn, yan, w_out, g2, w_up, fcw, fcb, w_down, tgt):
    tm = 512
    n_t = SEQ // tm

    def body(x_ref, ycn_ref, yan_ref, wo_ref, g2_ref, wu_ref, cw_ref, b_ref, wd_ref, tgt_ref,
             h1_ref, u2_ref, up_ref, pre_ref, act_ref, dh2_ref, dh2b_ref, loss_ref, acc_ref, halo_ref):
        i, j = pl.program_id(0), pl.program_id(1)

        @pl.when((i == 0) & (j == 0))
        def _():
            loss_ref[...] = jnp.zeros_like(loss_ref)

        @pl.when(j == 0)
        def _():
            h1 = x_ref[...] + _dot(ycn_ref[...], wo_ref[0:CONV_WIDTH, :]) + _dot(yan_ref[...], wo_ref[CONV_WIDTH:, :])
            h1_ref[...] = h1
            u2_ref[...] = (h1 * _rstd(h1) * g2_ref[...]).astype(BF16)
            acc_ref[...] = jnp.zeros_like(acc_ref)

        u2 = u2_ref[...]
        pre = []
        for s in range(2):
            up = _dot_nt(u2, wu_ref[s])
            up_ref[s] = up.astype(BF16)
            halo = jnp.where(i == 0, 0.0, halo_ref[s, j])
            pre.append(_conv3(up, _taps(cw_ref.at[s]), halo)[0] + b_ref[s])
            pre_ref[s] = pre[s].astype(BF16)
            halo_ref[s, j] = up[tm - SUBLANES:]
        g, val = pre
        act = (g * jax.nn.sigmoid(g) * val).astype(BF16)
        act_ref[...] = act
        acc_ref[...] += _dot(act, wd_ref[...])

        @pl.when(j == N_FFN_BLK - 1)
        def _():
            err = h1_ref[...] + acc_ref[...] - tgt_ref[...]
            loss_ref[...] += 0.5 * jnp.sum(err * err) / D_MODEL
            dh2 = err / D_MODEL
            dh2_ref[...] = dh2
            dh2b_ref[...] = dh2.astype(BF16)

    rows = lambda w: pl.BlockSpec((tm, w), lambda i, j: (i, 0))
    const = lambda shape: pl.BlockSpec(shape, lambda i, j: (0,) * len(shape))
    pair = lambda *s: pl.BlockSpec((2, None) + s, lambda i, j: (0, j) + (0,) * len(s))
    upb = pl.BlockSpec((2, None, tm, FFN_BLK), lambda i, j: (0, j, i, 0))
    return pl.pallas_call(
        body, name="ffn_fwd", grid=(n_t, N_FFN_BLK),
        in_specs=[rows(D_MODEL), rows(CONV_WIDTH), rows(ATTN_WIDTH), const((D_MODEL, D_MODEL)), const((1, D_MODEL)),
                  pair(FFN_BLK, D_MODEL), pair(3, 1, FFN_BLK), pair(1, FFN_BLK),
                  pl.BlockSpec((None, FFN_BLK, D_MODEL), lambda i, j: (j, 0, 0)), rows(D_MODEL)],
        out_specs=[rows(D_MODEL), rows(D_MODEL), upb, upb, pl.BlockSpec((None, tm, FFN_BLK), lambda i, j: (j, i, 0)),
                   rows(D_MODEL), rows(D_MODEL), const((SUBLANES, 128))],
        out_shape=[jax.ShapeDtypeStruct((SEQ, D_MODEL), F32), jax.ShapeDtypeStruct((SEQ, D_MODEL), BF16),
                   jax.ShapeDtypeStruct((2, N_FFN_BLK, SEQ, FFN_BLK), BF16),
                   jax.ShapeDtypeStruct((2, N_FFN_BLK, SEQ, FFN_BLK), BF16),
                   jax.ShapeDtypeStruct((N_FFN_BLK, SEQ, FFN_BLK), BF16),
                   jax.ShapeDtypeStruct((SEQ, D_MODEL), F32), jax.ShapeDtypeStruct((SEQ, D_MODEL), BF16),
                   jax.ShapeDtypeStruct((SUBLANES, 128), F32)],
        scratch_shapes=[pltpu.VMEM((tm, D_MODEL), F32), pltpu.VMEM((2, N_FFN_BLK, SUBLANES, FFN_BLK), F32)],
        compiler_params=_params(("arbitrary", "arbitrary")),
    )(x, ycn, yan, w_out, g2, w_up, fcw, fcb, w_down, tgt)


def _ffn_bwd(dh2, dh2b, h1, g2, up, pre, w_up, fcw, w_down):
    tm = 256
    n_t = SEQ // tm
    last = N_FFN_BLK - 1

    def body(dh2b_ref, up_ref, pre_ref, wu_ref, cw_ref, wd_ref, dh2_ref, h1_ref, g2_ref,
             dup_ref, dh1_ref, dh1b_ref, dfb_ref, dfcw_ref, dg2_ref, acc_ref, next_ref):
        j, i = pl.program_id(0), pl.program_id(1)
        rows = pl.ds(pl.multiple_of((n_t - 1 - i) * tm, tm), tm)

        @pl.when((i == 0) & (j == 0))
        def _():
            dfb_ref[...] = jnp.zeros_like(dfb_ref)
            dfcw_ref[...] = jnp.zeros_like(dfcw_ref)
            dg2_ref[...] = jnp.zeros_like(dg2_ref)

        g, val = pre_ref[0].astype(F32), pre_ref[1].astype(F32)
        sg = jax.nn.sigmoid(g)
        silu = g * sg
        dact = _dot_nt(dh2b_ref[...], wd_ref[...])
        dpre = (dact * val * (sg * (1.0 + g * (1.0 - sg))), dact * silu)
        du = None
        for s in range(2):
            d = dpre[s]
            u = up_ref[s].astype(F32)
            w = _taps(cw_ref.at[s])
            nxt = jnp.where(i == 0, 0.0, next_ref[s])
            d1 = _shift_up(d, 1, nxt)
            d2 = _shift_up(d, 2, nxt)
            next_ref[s] = d[:SUBLANES]
            dfb_ref[s, j] += jnp.sum(d, axis=0, keepdims=True)
            dfcw_ref[s, j, 0] += jnp.sum(d2 * u, axis=0, keepdims=True)
            dfcw_ref[s, j, 1] += jnp.sum(d1 * u, axis=0, keepdims=True)
            dfcw_ref[s, j, 2] += jnp.sum(d * u, axis=0, keepdims=True)
            dup = (d * w[2] + d1 * w[1] + d2 * w[0]).astype(BF16)
            dup_ref[s] = dup
            part = _dot(dup, wu_ref[s])
            du = part if du is None else du + part

        @pl.when(j == 0)
        def _():
            acc_ref[rows, :] = du

        @pl.when(j > 0)
        def _():
            acc_ref[rows, :] += du

        @pl.when(j == last)
        def _():
            dn, dgain = _rms_bwd(h1_ref[...], g2_ref[...], acc_ref[rows, :])
            dh1 = dh2_ref[...] + dn
            dh1_ref[...] = dh1
            dh1b_ref[...] = dh1.astype(BF16)
            dg2_ref[...] += dgain

    rev = lambda i: n_t - 1 - i
    const = lambda shape: pl.BlockSpec(shape, lambda j, i: (0,) * len(shape))
    late = lambda w: pl.BlockSpec((tm, w), lambda j, i: (jnp.where(j == last, rev(i), n_t - 1), 0))
    pair = lambda *s: pl.BlockSpec((2, None) + s, lambda j, i: (0, j) + (0,) * len(s))
    upb = pl.BlockSpec((2, None, tm, FFN_BLK), lambda j, i: (0, j, rev(i), 0))
    return pl.pallas_call(
        body, name="ffn_bwd", grid=(N_FFN_BLK, n_t),
        in_specs=[pl.BlockSpec((tm, D_MODEL), lambda j, i: (rev(i), 0)), upb, upb, pair(FFN_BLK, D_MODEL),
                  pair(3, 1, FFN_BLK), pl.BlockSpec((None, FFN_BLK, D_MODEL), lambda j, i: (j, 0, 0)),
                  late(D_MODEL), late(D_MODEL), const((1, D_MODEL))],
        out_specs=[upb, late(D_MODEL), late(D_MODEL),
                   const((2, N_FFN_BLK, 1, FFN_BLK)), const((2, N_FFN_BLK, 3, 1, FFN_BLK)), const((1, D_MODEL))],
        out_shape=[jax.ShapeDtypeStruct((2, N_FFN_BLK, SEQ, FFN_BLK), BF16), jax.ShapeDtypeStruct((SEQ, D_MODEL), F32),
                   jax.ShapeDtypeStruct((SEQ, D_MODEL), BF16), jax.ShapeDtypeStruct((2, N_FFN_BLK, 1, FFN_BLK), F32),
                   jax.ShapeDtypeStruct((2, N_FFN_BLK, 3, 1, FFN_BLK), F32), jax.ShapeDtypeStruct((1, D_MODEL), F32)],
        scratch_shapes=[pltpu.VMEM((SEQ, D_MODEL), F32), pltpu.VMEM((2, SUBLANES, FFN_BLK), F32)],
        compiler_params=_params(("arbitrary", "arbitrary")),
    )(dh2b, up, pre, w_up, fcw, w_down, dh2, h1, g2)


def _ffn_bwd_token_major(dh2, dh2b, h1, g2, up, pre, w_up, fcw, w_down, after=None):
    tm = 512
    n_t = SEQ // tm

    def body(dh2_ref, dh2b_ref, h1_ref, g2_ref, up_ref, pre_ref, wu_ref, cw_ref, wd_ref,
             dup_ref, dh1_ref, dh1b_ref, dfb_ref, dfcw_ref, dg2_ref, acc_ref, next_ref):
        i, j = pl.program_id(0), pl.program_id(1)

        @pl.when((i == 0) & (j == 0))
        def _():
            dfb_ref[...] = jnp.zeros_like(dfb_ref)
            dfcw_ref[...] = jnp.zeros_like(dfcw_ref)
            dg2_ref[...] = jnp.zeros_like(dg2_ref)

        @pl.when(j == 0)
        def _():
            acc_ref[...] = jnp.zeros_like(acc_ref)

        g, val = pre_ref[0].astype(F32), pre_ref[1].astype(F32)
        sg = jax.nn.sigmoid(g)
        silu = g * sg
        dact = _dot_nt(dh2b_ref[...], wd_ref[...])
        dpre = (dact * val * (sg * (1.0 + g * (1.0 - sg))), dact * silu)
        for s in range(2):
            d = dpre[s]
            u = up_ref[s].astype(F32)
            w = _taps(cw_ref.at[s])
            nxt = jnp.where(i == 0, 0.0, next_ref[s, j])
            d1 = _shift_up(d, 1, nxt)
            d2 = _shift_up(d, 2, nxt)
            next_ref[s, j] = d[:SUBLANES]
            dfb_ref[s, j] += jnp.sum(d, axis=0, keepdims=True)
            dfcw_ref[s, j, 0] += jnp.sum(d2 * u, axis=0, keepdims=True)
            dfcw_ref[s, j, 1] += jnp.sum(d1 * u, axis=0, keepdims=True)
            dfcw_ref[s, j, 2] += jnp.sum(d * u, axis=0, keepdims=True)
            dup = (d * w[2] + d1 * w[1] + d2 * w[0]).astype(BF16)
            dup_ref[s] = dup
            acc_ref[...] += _dot(dup, wu_ref[s])

        @pl.when(j == N_FFN_BLK - 1)
        def _():
            dn, dgain = _rms_bwd(h1_ref[...], g2_ref[...], acc_ref[...])
            dh1 = dh2_ref[...] + dn
            dh1_ref[...] = dh1
            dh1b_ref[...] = dh1.astype(BF16)
            dg2_ref[...] += dgain

    rev = lambda i: n_t - 1 - i
    rows = lambda w: pl.BlockSpec((tm, w), lambda i, j: (rev(i), 0))
    const = lambda shape: pl.BlockSpec(shape, lambda i, j: (0,) * len(shape))
    pair = lambda *s: pl.BlockSpec((2, None) + s, lambda i, j: (0, j) + (0,) * len(s))
    upb = pl.BlockSpec((2, None, tm, FFN_BLK), lambda i, j: (0, j, rev(i), 0))
    body, more_specs, more = _ordered_behind(body, 9, after)
    return pl.pallas_call(
        body, name="ffn_bwd", grid=(n_t, N_FFN_BLK),
        in_specs=[rows(D_MODEL), rows(D_MODEL), rows(D_MODEL), const((1, D_MODEL)), upb, upb,
                  pair(FFN_BLK, D_MODEL), pair(3, 1, FFN_BLK),
                  pl.BlockSpec((None, FFN_BLK, D_MODEL), lambda i, j: (j, 0, 0))] + more_specs,
        out_specs=[upb, rows(D_MODEL), rows(D_MODEL),
                   const((2, N_FFN_BLK, 1, FFN_BLK)), const((2, N_FFN_BLK, 3, 1, FFN_BLK)), const((1, D_MODEL))],
        out_shape=[jax.ShapeDtypeStruct((2, N_FFN_BLK, SEQ, FFN_BLK), BF16), jax.ShapeDtypeStruct((SEQ, D_MODEL), F32),
                   jax.ShapeDtypeStruct((SEQ, D_MODEL), BF16), jax.ShapeDtypeStruct((2, N_FFN_BLK, 1, FFN_BLK), F32),
                   jax.ShapeDtypeStruct((2, N_FFN_BLK, 3, 1, FFN_BLK), F32), jax.ShapeDtypeStruct((1, D_MODEL), F32)],
        scratch_shapes=[pltpu.VMEM((tm, D_MODEL), F32), pltpu.VMEM((2, N_FFN_BLK, SUBLANES, FFN_BLK), F32)],
        compiler_params=_params(("arbitrary", "arbitrary")),
    )(dh2, dh2b, h1, g2, up, pre, w_up, fcw, w_down, *more)


def _grad_tn(a_list, b, out_rows, name, after=None):
    n = len(a_list)
    ncol = b.shape[1]

    def body(*refs):
        a_refs, b_ref, o_ref = refs[:n], refs[n], refs[n + 1]
        j = pl.program_id(0)
        for k in range(n):
            @pl.when(j == k)
            def _(k=k):
                o_ref[...] = _dot_tn(a_refs[k][...], b_ref[...]).astype(BF16)

    full = lambda shape: pl.BlockSpec(shape, lambda j: (0,) * len(shape))
    body, more_specs, more = _ordered_behind(body, n + 1, after)
    return pl.pallas_call(
        body, name=name, grid=(n,),
        in_specs=[full((SEQ, out_rows))] * n + [full((SEQ, ncol))] + more_specs,
        out_specs=pl.BlockSpec((None, out_rows, ncol), lambda j: (j, 0, 0)),
        out_shape=jax.ShapeDtypeStruct((n, out_rows, ncol), BF16),
        compiler_params=_params(("arbitrary",)),
    )(*a_list, b, *more)


def _grad_tn_blocked(a, b, name, a_is_blocked=True, per_step=2):
    assert a_is_blocked
    nb, _, a_w = a.shape
    b_w = b.shape[-1]

    def body(a_ref, b_ref, o_ref):
        for p in range(per_step):
            o_ref[p] = _dot_tn(a_ref[p], b_ref[...]).astype(BF16)

    return pl.pallas_call(
        body, name=name, grid=(nb // per_step,),
        in_specs=[pl.BlockSpec((per_step, SEQ, a_w), lambda k: (k, 0, 0)), pl.BlockSpec((SEQ, b_w), lambda k: (0, 0))],
        out_specs=pl.BlockSpec((per_step, a_w, b_w), lambda k: (k, 0, 0)),
        out_shape=jax.ShapeDtypeStruct((nb, a_w, b_w), BF16),
        compiler_params=_params(("arbitrary",)),
    )(a, b)


def _out_bwd(dh1b, w_out, y_attn, gattn, after=None):
    tm = 512
    n_t = SEQ // tm

    def body(dh_ref, wo_ref, y_ref, ga_ref, dycn_ref, dy_ref, dga_ref):
        @pl.when(pl.program_id(0) == 0)
        def _():
            dga_ref[...] = jnp.zeros_like(dga_ref)

        dycat = _dot_nt(dh_ref[...], wo_ref[...])
        dycn_ref[...] = dycat[:, :CONV_WIDTH]
        dy, dga = _rms_bwd(y_ref[...], ga_ref[...], dycat[:, CONV_WIDTH:])
        dy_ref[...] = dy
        dga_ref[...] += dga

    rows = lambda w: pl.BlockSpec((tm, w), lambda i: (i, 0))
    const = lambda shape: pl.BlockSpec(shape, lambda i: (0,) * len(shape))
    body, more_specs, more = _ordered_behind(body, 4, after)
    return pl.pallas_call(
        body, name="out_bwd", grid=(n_t,),
        in_specs=[rows(D_MODEL), const((D_MODEL, D_MODEL)), rows(ATTN_WIDTH), const((1, ATTN_WIDTH))] + more_specs,
        out_specs=[rows(CONV_WIDTH), rows(ATTN_WIDTH), const((1, ATTN_WIDTH))],
        out_shape=[jax.ShapeDtypeStruct((SEQ, CONV_WIDTH), F32), jax.ShapeDtypeStruct((SEQ, ATTN_WIDTH), F32),
                   jax.ShapeDtypeStruct((1, ATTN_WIDTH), F32)],
        compiler_params=_params(("arbitrary",)),
    )(dh1b, w_out, y_attn, gattn, *more)


def _attn_bwd(qn, kn, v, dy, probs, psinks, bkt, after=None):
    n_b = SEQ // BLK

    def body(q_ref, k_ref, v_ref, dy_ref, p_ref, ps_ref, bkt_ref,
             dq_ref, dk_ref, dv_ref, dtbl_ref, dsink_ref, dbias_ref, dsacc_ref):
        i = pl.program_id(0)

        @pl.when(i == 0)
        def _():
            dbias_ref[...] = jnp.zeros_like(dbias_ref)
            dsacc_ref[...] = jnp.zeros_like(dsacc_ref)
            dk_ref[...] = jnp.zeros_like(dk_ref)
            dv_ref[...] = jnp.zeros_like(dv_ref)

        kb, prev, cur = _band_rows(k_ref, i)
        vb, _, _ = _band_rows(v_ref, i)
        upper, _ = _band_masks(i)
        q = q_ref[...]
        dy = dy_ref[...]
        psink = ps_ref[...]
        lane = lax.broadcasted_iota(jnp.int32, (BLK, 128), 1)
        dsink = jnp.zeros((BLK, 128), F32)
        dqs, dks, dvs = [], [], []
        for g in range(N_HEADS // GQA_GROUP):
            kv = slice(HEAD_DIM * g, HEAD_DIM * (g + 1))
            qg = _stack_heads(q, g)
            dog = _stack_heads(dy, g).astype(BF16)
            pb = p_ref[g]
            pg = pb.astype(F32)
            dprobs = _fold(_dot_nt(dog, vb[:, kv]), upper)
            dvs.append(_dot_tn(_unfold(pb, upper), dog))
            dsum = jnp.sum(pg * dprobs, axis=-1, keepdims=True)
            dlogits = pg * (dprobs - dsum)
            for t in range(GQA_GROUP):
                dsink = jnp.where(lane == GQA_GROUP * g + t, -psink * dsum[BLK * t:BLK * (t + 1)], dsink)
            dbias_ref[g] += dlogits
            ds = _unfold(dlogits * (HEAD_DIM ** -0.5), upper).astype(BF16)
            dqs.append(_dot(ds, kb[:, kv]))
            dks.append(_dot_tn(ds, qg))
        dsacc_ref[...] += dsink
        dq_ref[...] = _unstack_heads(dqs)
        dkb = jnp.concatenate(dks, axis=-1)
        dvb = jnp.concatenate(dvs, axis=-1)
        dk_ref[pl.ds(prev, BLK), :] += dkb[:BLK]
        dk_ref[pl.ds(cur, BLK), :] += dkb[BLK:]
        dv_ref[pl.ds(prev, BLK), :] += dvb[:BLK]
        dv_ref[pl.ds(cur, BLK), :] += dvb[BLK:]

        @pl.when(i == n_b - 1)
        def _():
            bkt = bkt_ref[...]
            row8 = lax.broadcasted_iota(jnp.int32, (N_HEADS, 128), 0)
            lane8 = lax.broadcasted_iota(jnp.int32, (N_HEADS, 128), 1)
            acc = jnp.zeros((N_HEADS, 128), F32)
            for h in range(N_HEADS):
                rows = slice(BLK * (h % GQA_GROUP), BLK * (h % GQA_GROUP + 1))
                dbh = dbias_ref[h // GQA_GROUP, rows, :]
                for b in range(NUM_BUCKETS):
                    acc = jnp.where((row8 == h) & (lane8 == b), jnp.sum(jnp.where(bkt == b, dbh, 0.0)), acc)
            dsink_ref[...] = jnp.sum(dsacc_ref[...], axis=0, keepdims=True)
            dtbl_ref[...] = acc

    const = lambda shape: pl.BlockSpec(shape, lambda i: (0,) * len(shape))
    rows = lambda w: pl.BlockSpec((BLK, w), lambda i: (i, 0))
    n_g = N_HEADS // GQA_GROUP
    body, more_specs, more = _ordered_behind(body, 7, after)
    return pl.pallas_call(
        body, name="attn_bwd", grid=(n_b,),
        in_specs=[rows(ATTN_WIDTH), const((SEQ, KV_WIDTH)), const((SEQ, KV_WIDTH)), rows(ATTN_WIDTH),
                  pl.BlockSpec((None, n_g, GROUP_ROWS, BLK), lambda i: (i, 0, 0, 0)), rows(128),
                  const((BLK, BLK))] + more_specs,
        out_specs=[rows(ATTN_WIDTH), const((SEQ, KV_WIDTH)), const((SEQ, KV_WIDTH)), const((N_HEADS, 128)), const((1, 128))],
        out_shape=[jax.ShapeDtypeStruct((SEQ, ATTN_WIDTH), F32), jax.ShapeDtypeStruct((SEQ, KV_WIDTH), F32),
                   jax.ShapeDtypeStruct((SEQ, KV_WIDTH), F32), jax.ShapeDtypeStruct((N_HEADS, 128), F32),
                   jax.ShapeDtypeStruct((1, 128), F32)],
        scratch_shapes=[pltpu.VMEM((n_g, GROUP_ROWS, BLK), F32), pltpu.VMEM((BLK, 128), F32)],
        compiler_params=_params(("arbitrary",)),
    )(qn, kn, v, dy, probs, psinks, bkt, *more)


def _mix_in_bwd(x, dh1, proj, dycn, dqn, dkn, dv, w_in_t, conv_w, g1, gq, gk, gconv):
    tm = 512
    n_t = SEQ // tm
    halo_blocks = tm // SUBLANES

    def body(x_ref, dh1_ref, proj_ref, halo_ref, dycn_ref, dqn_ref, dkn_ref, dv_ref, w_ref, cw_ref,
             g1_ref, gq_ref, gk_ref, gc_ref,
             dx_ref, dproj_ref, dcw_ref, dgc_ref, dgq_ref, dgk_ref, dg1_ref, next_ref):
        i = pl.program_id(0)
        first_tile = i == n_t - 1

        @pl.when(i == 0)
        def _():
            for r in (dcw_ref, dgc_ref, dgq_ref, dgk_ref, dg1_ref, next_ref):
                r[...] = jnp.zeros_like(r)

        proj = proj_ref[...]
        hp = halo_ref[...]
        gate_b = proj[:, 0:CONV_WIDTH]
        gate_c = proj[:, CONV_WIDTH:2 * CONV_WIDTH]
        hc = proj[:, 2 * CONV_WIDTH:3 * CONV_WIDTH]
        a = gate_c * hc
        a_halo = jnp.where(first_tile, 0.0, hp[:, CONV_WIDTH:2 * CONV_WIDTH] * hp[:, 2 * CONV_WIDTH:3 * CONV_WIDTH])
        cw = _taps(cw_ref[...])
        cv, a2, a1 = _conv3(a, cw, a_halo)
        dyc, dgc = _rms_bwd(gate_b * cv, gc_ref[...], dycn_ref[...])
        dgc_ref[...] += dgc
        dcv = dyc * gate_b
        dcw_ref[...] += jnp.concatenate(
            [jnp.sum(dcv * a2, axis=0, keepdims=True), jnp.sum(dcv * a1, axis=0, keepdims=True),
             jnp.sum(dcv * a, axis=0, keepdims=True)], axis=0)
        da = _conv3_bwd_input(dcv, cw, next_ref[...])
        next_ref[...] = dcv[:SUBLANES]
        q0 = 3 * CONV_WIDTH
        k0 = q0 + ATTN_WIDTH
        dq, dgq = _head_norm_bwd(proj[:, q0:k0], gq_ref[...], dqn_ref[...], N_HEADS)
        dk, dgk = _head_norm_bwd(proj[:, k0:k0 + KV_WIDTH], gk_ref[...], dkn_ref[...], 2)
        dgq_ref[...] += dgq
        dgk_ref[...] += dgk
        dproj = jnp.concatenate([dyc * cv, da * hc, da * gate_c, dq, dk, dv_ref[...]], axis=-1).astype(BF16)
        dproj_ref[...] = dproj
        du1 = _dot(dproj, w_ref[...])
        xv = x_ref[...]
        dn, dg1 = _rms_bwd(xv, g1_ref[...], du1)
        dx_ref[...] = dh1_ref[...] + dn
        dg1_ref[...] += dg1

    rev = lambda i: n_t - 1 - i
    rows = lambda w: pl.BlockSpec((tm, w), lambda i: (rev(i), 0))
    const = lambda shape: pl.BlockSpec(shape, lambda i: (0,) * len(shape))
    halo = pl.BlockSpec((SUBLANES, IN_WIDTH), lambda i: (jnp.maximum(rev(i) * halo_blocks - 1, 0), 0))
    return pl.pallas_call(
        body, name="mix_in_bwd", grid=(n_t,),
        in_specs=[rows(D_MODEL), rows(D_MODEL), rows(IN_WIDTH), halo, rows(CONV_WIDTH), rows(ATTN_WIDTH), rows(KV_WIDTH),
                  rows(KV_WIDTH), const((IN_WIDTH, D_MODEL)), const((3, CONV_WIDTH)), const((1, D_MODEL)),
                  const((1, HEAD_DIM)), const((1, HEAD_DIM)), const((1, CONV_WIDTH))],
        out_specs=[rows(D_MODEL), rows(IN_WIDTH), const((3, CONV_WIDTH)), const((1, CONV_WIDTH)),
                   const((1, HEAD_DIM)), const((1, HEAD_DIM)), const((1, D_MODEL))],
        out_shape=[jax.ShapeDtypeStruct((SEQ, D_MODEL), F32), jax.ShapeDtypeStruct((SEQ, IN_WIDTH), BF16),
                   jax.ShapeDtypeStruct((3, CONV_WIDTH), F32),
                   jax.ShapeDtypeStruct((1, CONV_WIDTH), F32), jax.ShapeDtypeStruct((1, HEAD_DIM), F32),
                   jax.ShapeDtypeStruct((1, HEAD_DIM), F32), jax.ShapeDtypeStruct((1, D_MODEL), F32)],
        scratch_shapes=[pltpu.VMEM((SUBLANES, CONV_WIDTH), F32)],
        compiler_params=_params(("arbitrary",)),
    )(x, dh1, proj, proj, dycn, dqn, dkn, dv, w_in_t, conv_w, g1, gq, gk, gconv)


def _grad_w_in(dproj, u1, after=None):
    bw = 768

    def body(a_ref, b_ref, o_ref):
        o_ref[...] = _dot_tn(a_ref[...], b_ref[...]).astype(BF16)

    body, more_specs, more = _ordered_behind(body, 2, after)
    return pl.pallas_call(
        body, name="grad_w_in", grid=(IN_WIDTH // bw,),
        in_specs=[pl.BlockSpec((SEQ, bw), lambda k: (0, k)), pl.BlockSpec((SEQ, D_MODEL), lambda k: (0, 0))] + more_specs,
        out_specs=pl.BlockSpec((bw, D_MODEL), lambda k: (k, 0)),
        out_shape=jax.ShapeDtypeStruct((IN_WIDTH, D_MODEL), BF16),
        compiler_params=_params(("arbitrary",)),
    )(dproj, u1, *more)


def _adamw_math(w, g, m, v):
    m = ADAM_B1 * m + (1.0 - ADAM_B1) * g
    v = ADAM_B2 * v + (1.0 - ADAM_B2) * (g * g)
    m_hat = m / (1.0 - ADAM_B1 ** ADAM_STEP)
    v_hat = v / (1.0 - ADAM_B2 ** ADAM_STEP)
    return -ADAM_LR * (m_hat / (jnp.sqrt(v_hat) + ADAM_EPS) + ADAM_WD * w), m, v


_ROW_G1, _ROW_G2, _ROW_OUT_NORMS, _ROW_FFN_B, _ROW_GQ, _ROW_GK, _ROW_SINKS, _ROW_LOSS, _ROW_TABLE = 0, 1, 2, 3, 11, 12, 13, 14, 16
SMALL_ROWS, SMALL_COLS = 24, 1024
_SMALL_NAMES = ("norm_mix_g", "norm_ffn_g", "out_norm_conv_g", "out_norm_attn_g", "ffn_conv_b", "q_norm_g", "k_norm_g",
                "sinks", "rel_bias_table")


def _pack_small_grads(dg1, dg2, dgconv, dgattn, dfb, dgq, dgk, dsinks, dtbl_t, loss_acc):
    def body(dg1_ref, dg2_ref, dgc_ref, dga_ref, dfb_ref, dgq_ref, dgk_ref, ds_ref, dt_ref, loss_ref, o_ref):
        o_ref[...] = jnp.zeros_like(o_ref)
        o_ref[_ROW_G1:_ROW_G1 + 1, :] = dg1_ref[...]
        o_ref[_ROW_G2:_ROW_G2 + 1, :] = dg2_ref[...]
        o_ref[_ROW_OUT_NORMS:_ROW_OUT_NORMS + 1, 0:CONV_WIDTH] = dgc_ref[...]
        o_ref[_ROW_OUT_NORMS:_ROW_OUT_NORMS + 1, CONV_WIDTH:] = dga_ref[...]
        for k in range(N_DEV):
            o_ref[_ROW_FFN_B + k:_ROW_FFN_B + k + 1, 0:FFN_BLK] = dfb_ref[k // N_FFN_BLK, k % N_FFN_BLK]
        o_ref[_ROW_GQ:_ROW_GQ + 1, 0:HEAD_DIM] = dgq_ref[...]
        o_ref[_ROW_GK:_ROW_GK + 1, 0:HEAD_DIM] = dgk_ref[...]
        o_ref[_ROW_SINKS:_ROW_SINKS + 1, 0:128] = ds_ref[...]
        o_ref[_ROW_LOSS:_ROW_LOSS + 1, 0:128] = loss_ref[0:1, :]
        o_ref[_ROW_TABLE:_ROW_TABLE + N_HEADS, 0:128] = dt_ref[...]

    return pl.pallas_call(body, name="pack_small_grads", out_shape=jax.ShapeDtypeStruct((SMALL_ROWS, SMALL_COLS), F32))(
        dg1, dg2, dgconv, dgattn, dfb, dgq, dgk, dsinks, dtbl_t, loss_acc)


def _adamw_small(recv, params, after):
    names = _SMALL_NAMES
    n = len(names)

    def grad_of(g, name, k=None):
        if name == "norm_mix_g":
            return g[_ROW_G1:_ROW_G1 + 1, :]
        if name == "norm_ffn_g":
            return g[_ROW_G2:_ROW_G2 + 1, :]
        if name == "out_norm_conv_g":
            return g[_ROW_OUT_NORMS:_ROW_OUT_NORMS + 1, 0:CONV_WIDTH]
        if name == "out_norm_attn_g":
            return g[_ROW_OUT_NORMS:_ROW_OUT_NORMS + 1, CONV_WIDTH:]
        if name == "ffn_conv_b":
            return g[_ROW_FFN_B + k:_ROW_FFN_B + k + 1, 0:FFN_BLK]
        if name == "q_norm_g":
            return g[_ROW_GQ:_ROW_GQ + 1, 0:HEAD_DIM]
        if name == "k_norm_g":
            return g[_ROW_GK:_ROW_GK + 1, 0:HEAD_DIM]
        if name == "sinks":
            return g[_ROW_SINKS:_ROW_SINKS + 1, 0:N_HEADS]
        return g[_ROW_TABLE:_ROW_TABLE + N_HEADS, 0:NUM_BUCKETS]

    def body(r_ref, *refs):
        ins, outs, loss_ref = refs[:3 * n], refs[3 * n:7 * n], refs[7 * n]
        g = r_ref[0]
        for s in range(1, N_DEV):
            g = g + r_ref[s]
        loss_ref[...] = g[_ROW_LOSS:_ROW_LOSS + 1, 0:128]
        for i, name in enumerate(names):
            w_ref, m_ref, v_ref = ins[3 * i:3 * i + 3]
            o = outs[4 * i:4 * i + 4]
            cols = [slice(FFN_BLK * k, FFN_BLK * (k + 1)) for k in range(N_DEV)] if name == "ffn_conv_b" else [slice(None)]
            for k, cs in enumerate(cols):
                gk = grad_of(g, name, k)
                d, m2, v2 = _adamw_math(w_ref[:, cs], gk, m_ref[:, cs], v_ref[:, cs])
                o[0][:, cs], o[1][:, cs], o[2][:, cs], o[3][:, cs] = gk, d, m2, v2

    flat = [a for name in names for a in params[name]]
    body, more_specs, more = _ordered_behind(body, 1 + 3 * n, after)
    vmem = pl.BlockSpec(memory_space=pltpu.VMEM)
    out = pl.pallas_call(
        body, name="adamw_small",
        in_specs=[vmem] * (1 + 3 * n) + more_specs,
        out_shape=[jax.ShapeDtypeStruct(params[name][0].shape, F32) for name in names for _ in range(4)]
        + [jax.ShapeDtypeStruct((1, 128), F32)],
        compiler_params=pltpu.CompilerParams(vmem_limit_bytes=VMEM_LIMIT),
    )(recv, *flat, *more)
    return {name: tuple(out[4 * i:4 * i + 4]) for i, name in enumerate(names)}, out[4 * n]


def _adamw_direct(w, m, v, own, recv, me, name, row_blocks=1, after=None):
    rb = w.shape[0] // row_blocks
    cols = w.shape[1]

    def body(me_ref, w_ref, m_ref, v_ref, o_ref, r_ref, g_o, d_o, m_o, v_o):
        g = o_ref[...].astype(F32)
        for s in range(N_DEV - 1):
            g = g + r_ref[s].astype(F32)
        g_o[...] = g
        d_o[...], m_o[...], v_o[...] = _adamw_math(w_ref[...], g, m_ref[...], v_ref[...])

    blk = pl.BlockSpec((rb, cols), lambda i, me_ref: (i, 0))
    oblk = pl.BlockSpec((None, rb, cols), lambda i, me_ref: (me_ref[0], i, 0))
    rblk = pl.BlockSpec((N_DEV - 1, rb, cols), lambda i, me_ref: (0, i, 0))
    body, more_specs, more = _ordered_behind(body, 6, after)
    return pl.pallas_call(
        body, name=name,
        grid_spec=pltpu.PrefetchScalarGridSpec(num_scalar_prefetch=1, grid=(row_blocks,),
                                               in_specs=[blk, blk, blk, oblk, rblk] + more_specs, out_specs=[blk] * 4),
        out_shape=[jax.ShapeDtypeStruct(w.shape, F32)] * 4,
        compiler_params=_params(("arbitrary",)),
    )(me, w, m, v, own, recv, *more)


def _adamw(w, m, v, part, recv, chip, name, row_blocks=1, after=None):
    rb = w.shape[0] // row_blocks
    tail = w.shape[1:]
    zeros = (0,) * len(tail)

    def body(chip_ref, w_ref, m_ref, v_ref, p_ref, r_ref, g_o, d_o, m_o, v_o):
        g = p_ref[...].astype(F32)
        for s in range(3):
            g = g + r_ref[s].astype(F32)
        g_o[...] = g
        d_o[...], m_o[...], v_o[...] = _adamw_math(w_ref[...], g, m_ref[...], v_ref[...])

    blk = pl.BlockSpec((rb,) + tail, lambda i, chip_ref: (i,) + zeros)
    pblk = pl.BlockSpec((None, rb) + tail, lambda i, chip_ref: (chip_ref[0], i) + zeros)
    rblk = pl.BlockSpec((3, rb) + tail, lambda i, chip_ref: (0, i) + zeros)
    body, more_specs, more = _ordered_behind(body, 6, after)
    return pl.pallas_call(
        body, name=name,
        grid_spec=pltpu.PrefetchScalarGridSpec(num_scalar_prefetch=1, grid=(row_blocks,),
                                               in_specs=[blk, blk, blk, pblk, rblk] + more_specs, out_specs=[blk] * 4),
        out_shape=[jax.ShapeDtypeStruct(w.shape, F32)] * 4,
        compiler_params=_params(("arbitrary",)),
    )(chip, w, m, v, part, recv, *more)


def kernel(x, norm_mix_g, w_in, conv_w, q_norm_g, k_norm_g, rel_bias_table, sinks, out_norm_conv_g, out_norm_attn_g, w_out, norm_ffn_g, w_up, ffn_conv_w, ffn_conv_b, w_down, loss_target, m_norm_mix_g, m_w_in, m_conv_w, m_q_norm_g, m_k_norm_g, m_rel_bias_table, m_sinks, m_out_norm_conv_g, m_out_norm_attn_g, m_w_out, m_norm_ffn_g, m_w_up, m_ffn_conv_w, m_ffn_conv_b, m_w_down, v_norm_mix_g, v_w_in, v_conv_w, v_q_norm_g, v_k_norm_g, v_rel_bias_table, v_sinks, v_out_norm_conv_g, v_out_norm_attn_g, v_w_out, v_norm_ffn_g, v_w_up, v_ffn_conv_w, v_ffn_conv_b, v_w_down):
    p = dict(norm_mix_g=norm_mix_g, w_in=w_in, conv_w=conv_w, q_norm_g=q_norm_g, k_norm_g=k_norm_g,
             rel_bias_table=rel_bias_table, sinks=sinks, out_norm_conv_g=out_norm_conv_g, out_norm_attn_g=out_norm_attn_g,
             w_out=w_out, norm_ffn_g=norm_ffn_g, w_up=w_up, ffn_conv_w=ffn_conv_w, ffn_conv_b=ffn_conv_b, w_down=w_down)
    m = dict(norm_mix_g=m_norm_mix_g, w_in=m_w_in, conv_w=m_conv_w, q_norm_g=m_q_norm_g, k_norm_g=m_k_norm_g,
             rel_bias_table=m_rel_bias_table, sinks=m_sinks, out_norm_conv_g=m_out_norm_conv_g,
             out_norm_attn_g=m_out_norm_attn_g, w_out=m_w_out, norm_ffn_g=m_norm_ffn_g, w_up=m_w_up,
             ffn_conv_w=m_ffn_conv_w, ffn_conv_b=m_ffn_conv_b, w_down=m_w_down)
    v = dict(norm_mix_g=v_norm_mix_g, w_in=v_w_in, conv_w=v_conv_w, q_norm_g=v_q_norm_g, k_norm_g=v_k_norm_g,
             rel_bias_table=v_rel_bias_table, sinks=v_sinks, out_norm_conv_g=v_out_norm_conv_g,
             out_norm_attn_g=v_out_norm_attn_g, w_out=v_w_out, norm_ffn_g=v_norm_ffn_g, w_up=v_w_up,
             ffn_conv_w=v_ffn_conv_w, ffn_conv_b=v_ffn_conv_b, w_down=v_w_down)

    xs, tgt = x[0], loss_target[0]
    g1, g2, gq, gk, gconv, gattn = norm_mix_g, norm_ffn_g, q_norm_g, k_norm_g, out_norm_conv_g, out_norm_attn_g
    ix, iy, ic = _coords()
    core = ic.astype(jnp.int32).reshape(1)
    chip = (2 * ix + iy).astype(jnp.int32).reshape(1)
    me = _lin(ix, iy, ic).astype(jnp.int32).reshape(1)
    bkt = jnp.asarray(_bucket_map())
    tr = lambda a: a[0].T
    taps = lambda a: jnp.transpose(a, (1, 0, 2))
    tbl_t = rel_bias_table.T

    wi_l, cw_l = _place_shards(me, [tr(w_in), taps(conv_w)], [BF16, F32], "place_mixer_shards")
    finish_a, token_a = _all_gather_split([wi_l, cw_l], "mixer", None)
    wo_l, wu_l, wd_l, fcw_l = _place_shards(me, [w_out[0], tr(w_up), w_down[0], taps(ffn_conv_w)],
                                            [BF16, BF16, BF16, F32], "place_ffn_shards", after=token_a)
    ffn_stage2, ffn_stage3, token_b = _all_gather_tree([wo_l, wu_l, wd_l, fcw_l], "ffn", token_a)
    wi_g, cw_g = finish_a(token_b)
    w_in_t = wi_g.reshape(IN_WIDTH, D_MODEL)
    conv_w_f = jnp.transpose(cw_g[:, :, 0, :], (1, 0, 2)).reshape(3, CONV_WIDTH)

    proj, u1, ycn, qn, kn, vv = _mix_in_fwd(xs, g1, w_in_t, conv_w_f, gq, gk, gconv)
    token_b2 = ffn_stage2(ycn)
    y_attn, yan, probs, psinks = _attn_fwd(qn, kn, vv, tbl_t, sinks, bkt, gattn, after=token_b2)
    wo_g, wu_g, wd_g, fcw_g = ffn_stage3(yan)
    w_out_f = wo_g.reshape(D_MODEL, D_MODEL)
    w_down_f = wd_g.reshape(N_FFN_BLK, FFN_BLK, D_MODEL)
    w_up_f = wu_g.reshape(2, N_FFN_BLK, FFN_BLK, D_MODEL)
    fcw_f = fcw_g.reshape(2, N_FFN_BLK, 3, 1, FFN_BLK)
    fcb = ffn_conv_b.reshape(2, N_FFN_BLK, 1, FFN_BLK)
    h1, u2, up, pre, act, dh2, dh2b, loss_acc = _ffn_fwd_token_major(
        xs, ycn, yan, w_out_f, g2, w_up_f, fcw_f, fcb, w_down_f, tgt)

    dw_down = _grad_tn_blocked(act, dh2b, "grad_w_down", a_is_blocked=True).reshape(N_DEV, D_FF // N_DEV, D_MODEL)
    plan_d, slots_d = _scatter_plan(1)
    d_sem = _split_start("scatter_w_down_start", [dw_down], [lax.empty((N_DEV - 1,) + dw_down.shape[1:], BF16)],
                         plan_d, None)
    dup, dh1, dh1b, dfb, dfcw, dg2 = _ffn_bwd_token_major(dh2, dh2b, h1, g2, up, pre, w_up_f, fcw_f, w_down_f,
                                                          after=d_sem[4])
    dw_up = _grad_tn_blocked(dup.reshape(N_DEV, SEQ, FFN_BLK), u2, "grad_w_up", a_is_blocked=True)
    dw_out = _grad_tn([ycn, yan], dh1b, CONV_WIDTH, "grad_w_out").reshape(N_DEV, D_MODEL // N_DEV, D_MODEL)
    out_bwd = {}

    def behind_ffn(token):
        out_bwd["r"] = _out_bwd(dh1b, w_out_f, y_attn, gattn, after=token)
        return out_bwd["r"][0]

    finish_ffn, token_ffn = _reduce_scatter_split(
        [dw_up, dw_out, dfcw.reshape(N_DEV, 3, 1, FFN_BLK)], "ffn", core, behind_ffn)
    dycn, dy_attn, dgattn = out_bwd["r"]
    dqn, dkn, dv, dtbl_t, dsinks = _attn_bwd(qn, kn, vv, dy_attn, probs, psinks, bkt, after=token_ffn)
    dx, dproj, dcw, dgconv, dgq, dgk, dg1 = _mix_in_bwd(xs, dh1, proj, dycn, dqn, dkn, dv, w_in_t, conv_w_f,
                                                             g1, gq, gk, gconv)
    packed = _pack_small_grads(dg1, dg2, dgconv, dgattn, dfb, dgq, dgk, dsinks, dtbl_t, loss_acc)
    plan_s, slots_s = _broadcast_plan()
    s_sem, r_sem, src_s, land_s, token_s = _split_start(
        "gather_small_start", [packed], [jnp.broadcast_to(packed[None], (N_DEV,) + packed.shape)], plan_s, None)
    dw_in_t = _grad_w_in(dproj, u1, after=token_s).reshape(N_DEV, IN_WIDTH // N_DEV, D_MODEL)
    dcw_b = jnp.transpose(dcw.reshape(3, N_DEV, 1, CONV_WIDTH // N_DEV), (1, 0, 2, 3))
    adam = {}
    ffn_got = {}

    def behind_mixer(token):
        ffn_got["r"] = finish_ffn(token)
        return ffn_got["r"][1][0]

    finish_mixer, token_mixer = _reduce_scatter_split([dw_in_t, dcw_b], "mixer", core, behind_mixer)
    (p_wu, p_wo, p_fcw), (r_wu, r_wo, r_fcw) = ffn_got["r"]
    (own_wd,), (r_wd,) = _split_wait("scatter_w_down_wait", d_sem[0], d_sem[1], d_sem[2], d_sem[3], plan_d, slots_d,
                                     token_mixer)
    adam["w_down"] = _adamw_direct(w_down[0], m_w_down[0], v_w_down[0], own_wd, r_wd, me, "adamw_w_down", row_blocks=2)
    adam_up = _adamw(tr(w_up), tr(m_w_up), tr(v_w_up), p_wu, r_wu, chip, "adamw_w_up", row_blocks=4,
                     after=adam["w_down"][0])
    adam["w_out"] = _adamw(w_out[0], m_w_out[0], v_w_out[0], p_wo, r_wo, chip, "adamw_w_out", after=adam_up[0])
    adam_fcw = _adamw(taps(ffn_conv_w), taps(m_ffn_conv_w), taps(v_ffn_conv_w), p_fcw, r_fcw, chip, "adamw_ffn_conv_w",
                      after=adam["w_out"][0])
    _, (r_small,) = _split_wait("gather_small_wait", s_sem, r_sem, src_s, land_s, plan_s, slots_s, adam_fcw[0])
    small_in = {k: (p[k], m[k], v[k]) for k in _SMALL_NAMES}
    small_in["rel_bias_table"] = (tbl_t, m_rel_bias_table.T, v_rel_bias_table.T)
    small_out, loss_row = _adamw_small(r_small, small_in, None)
    (p_wi, p_cw), (r_wi, r_cw) = finish_mixer(loss_row)
    adam_in = _adamw(tr(w_in), tr(m_w_in), tr(v_w_in), p_wi, r_wi, chip, "adamw_w_in")
    adam_cw = _adamw(taps(conv_w), taps(m_conv_w), taps(v_conv_w), p_cw, r_cw, chip, "adamw_conv_w")

    res = {k: tuple(a[None] for a in t) for k, t in adam.items()}
    res["w_up"] = tuple(a.T[None] for a in adam_up)
    res["w_in"] = tuple(a.T[None] for a in adam_in)
    res["ffn_conv_w"] = tuple(taps(a) for a in adam_fcw)
    res["conv_w"] = tuple(taps(a) for a in adam_cw)
    res.update(small_out)
    res["rel_bias_table"] = tuple(a.T for a in small_out["rel_bias_table"])
    loss = loss_row[0, 0]
    order = ("norm_mix_g", "w_in", "conv_w", "q_norm_g", "k_norm_g", "rel_bias_table", "sinks", "out_norm_conv_g",
             "out_norm_attn_g", "w_out", "norm_ffn_g", "w_up", "ffn_conv_w", "ffn_conv_b", "w_down")
    return (loss, dx[None], *[res[k][0] for k in order], *[res[k][1] for k in order],
            *[res[k][2] for k in order], *[res[k][3] for k in order])
```

```python
import functools
import math

import numpy as np
import jax
import jax.numpy as jnp
from jax import lax
from jax.experimental import pallas as pl
from jax.experimental.pallas import tpu as pltpu

F32 = jnp.float32
BF16 = jnp.bfloat16

SEQ = 2048
D_MODEL = 1024
CONV_WIDTH = 512
ATTN_WIDTH = 512
KV_WIDTH = 128
HEAD_DIM = 64
N_HEADS = 8
GQA_GROUP = 4
IN_WIDTH = 2304
D_FF = 2816
BLK = 128
NUM_BUCKETS = 32
EPS = 1e-6
NEG_INF = -1e30
ADAM_LR = 0.001
ADAM_B1 = 0.9
ADAM_B2 = 0.999
ADAM_EPS = 1e-08
ADAM_WD = 0.01
ADAM_STEP = 10

N_DEV = 8
FFN_BLK = 2 * D_FF // N_DEV
N_FFN_BLK = D_FF // FFN_BLK
SUBLANES = 8
VMEM_LIMIT = 56 * 1024 * 1024

_MESH = pl.DeviceIdType.MESH
_ANY = pl.BlockSpec(memory_space=pl.ANY)


def _params(sem):
    return pltpu.CompilerParams(dimension_semantics=sem, vmem_limit_bytes=VMEM_LIMIT)


def _ordered_behind(body, pos, after):
    if after is None:
        return body, [], []
    return (lambda *refs: body(*refs[:pos], *refs[pos + 1:])), [_ANY], [after]


def _dot(a, b):
    return jnp.dot(a, b, preferred_element_type=F32)


def _dot_nt(a, b):
    return lax.dot_general(a, b, (((1,), (1,)), ((), ())), preferred_element_type=F32)


def _dot_tn(a, b):
    return lax.dot_general(a, b, (((0,), (0,)), ((), ())), preferred_element_type=F32)


def _shift_down(x, s, halo):
    r = pltpu.roll(x, s, axis=0)
    hr = pltpu.roll(halo, s, axis=0)
    row = lax.broadcasted_iota(jnp.int32, halo.shape, 0)
    top = jnp.where(row < s, hr, r[:SUBLANES])
    return jnp.concatenate([top, r[SUBLANES:]], axis=0)


def _shift_up(x, s, halo):
    n = x.shape[0]
    r = pltpu.roll(x, n - s, axis=0)
    hr = pltpu.roll(halo, SUBLANES - s, axis=0)
    row = lax.broadcasted_iota(jnp.int32, halo.shape, 0)
    bot = jnp.where(row >= SUBLANES - s, hr, r[n - SUBLANES:])
    return jnp.concatenate([r[:n - SUBLANES], bot], axis=0)


def _taps(w):
    return (w[0], w[1], w[2]) if len(w.shape) == 3 else (w[0:1], w[1:2], w[2:3])


def _conv3(x, w, halo):
    x2 = _shift_down(x, 2, halo)
    x1 = _shift_down(x, 1, halo)
    return x2 * w[0] + x1 * w[1] + x * w[2], x2, x1


def _conv3_bwd_input(dy, w, halo_next):
    return dy * w[2] + _shift_up(dy, 1, halo_next) * w[1] + _shift_up(dy, 2, halo_next) * w[0]


def _rstd(x):
    return lax.rsqrt(jnp.mean(x * x, axis=-1, keepdims=True) + EPS)


def _rms_bwd(x, g, dy):
    r = _rstd(x)
    n = x * r
    dn = dy * g
    dx = r * (dn - n * jnp.mean(dn * n, axis=-1, keepdims=True))
    return dx, jnp.sum(dy * n, axis=0, keepdims=True)


def _head_mean(x):
    width = x.shape[-1]
    ri = lax.broadcasted_iota(jnp.int32, (width, width), 0) // HEAD_DIM
    ci = lax.broadcasted_iota(jnp.int32, (width, width), 1) // HEAD_DIM
    ones = jnp.where(ri == ci, 1.0, 0.0).astype(BF16)
    hi = x.astype(BF16)
    lo = (x - hi.astype(F32)).astype(BF16)
    return (_dot(hi, ones) + _dot(lo, ones)) * (1.0 / HEAD_DIM)


def _head_norm(x, g, heads):
    return x * lax.rsqrt(_head_mean(x * x) + EPS) * jnp.tile(g, (1, heads))


def _head_norm_bwd(x, g, dy, heads):
    r = lax.rsqrt(_head_mean(x * x) + EPS)
    n = x * r
    dn = dy * jnp.tile(g, (1, heads))
    dx = r * (dn - n * _head_mean(dn * n))
    per_lane = jnp.sum(dy * n, axis=0, keepdims=True)
    dg = per_lane[:, 0:HEAD_DIM]
    for h in range(1, heads):
        dg = dg + per_lane[:, HEAD_DIM * h:HEAD_DIM * (h + 1)]
    return dx, dg


def _bucket_map():
    q = np.arange(BLK)[:, None]
    j = np.arange(BLK)[None, :]
    n = np.where(j > q, q + BLK - j, q - j)
    nf = np.maximum(n, 1).astype(np.float32)
    max_exact = NUM_BUCKETS // 2
    large = max_exact + (np.log(nf / max_exact) / math.log(BLK / max_exact) * (NUM_BUCKETS - max_exact)).astype(np.int32)
    large = np.minimum(large, NUM_BUCKETS - 1)
    return np.where(n < max_exact, n, large).astype(np.int32)


def _coords():
    return lax.axis_index("x"), lax.axis_index("y"), lax.axis_index("c")


def _lin(px, py, pc):
    return 4 * px + 2 * py + pc


_HBM = pl.BlockSpec(memory_space=pltpu.HBM)
_SEM = pl.BlockSpec(memory_space=pltpu.SEMAPHORE)
_EFFECT = pltpu.SideEffectType.DATAFLOW_SIDE_EFFECTING


def _in_hbm(a):
    return pltpu.with_memory_space_constraint(a, pltpu.HBM)


_SIBLING = (1, lambda x, y, c: [(x, y, 1 - c)])
_SIBLING_AND_CHIPS = (2, lambda x, y, c: [(x, y, 1 - c)] + [(cx, cy, c) for cx, cy in _chips(x, y)])
_SIBLING_AND_NEIGHBOURS = (3, lambda x, y, c: [(x, y, 1 - c), (1 - x, y, c), (x, 1 - y, c)])
_ONWARD_AND_SIBLING = (4, lambda x, y, c: [(jnp.where(c == 1, x, 1 - x), jnp.where(c == 1, 1 - y, y), c), (x, y, 1 - c)])
_ALL_FOR_W_DOWN = (5, lambda x, y, c: _peers(x, y, c))
_CHIPS = (6, lambda x, y, c: [(cx, cy, c) for cx, cy in _chips(x, y)])
_ALL_FOR_SMALL = (7, lambda x, y, c: _peers(x, y, c))


def _split_start(name, srcs, lands, plan, after, handshake):
    ns, nl = len(srcs), len(lands)
    n_copies = len(plan(0, 0, 0))
    n_after = 0 if after is None else 1
    collective_id, peers_of = handshake

    def body(*refs):
        src_refs, land_refs = refs[:ns + nl], refs[ns:ns + nl]
        send_sems, recv_sems = refs[ns + nl + n_after], refs[ns + nl + n_after + 1]
        token = refs[-1]
        barrier = pltpu.get_barrier_semaphore()
        peers = peers_of(*_coords())
        for peer in peers:
            pl.semaphore_signal(barrier, inc=1, device_id=peer, device_id_type=_MESH)
        pl.semaphore_wait(barrier, len(peers))
        for k, (a, s_slot, l, d_slot, dev) in enumerate(plan(*_coords())):
            src = src_refs[a] if s_slot is None else src_refs[a].at[s_slot]
            pltpu.make_async_remote_copy(src_ref=src, dst_ref=land_refs[l].at[d_slot], send_sem=send_sems.at[k],
                                         recv_sem=recv_sems.at[k], device_id=dev, device_id_type=_MESH).start()
        token[...] = jnp.zeros_like(token)

    arrs = list(srcs) + list(lands)
    out = pl.pallas_call(
        body, name=name,
        out_shape=(pltpu.SemaphoreType.DMA((n_copies,)), pltpu.SemaphoreType.DMA((n_copies,)),
                   *[pltpu.HBM(a.shape, a.dtype) for a in arrs], jax.ShapeDtypeStruct((SUBLANES, 128), F32)),
        in_specs=[_HBM] * (ns + nl) + [_ANY] * n_after,
        out_specs=(_SEM, _SEM, *[_HBM] * (ns + nl), pl.BlockSpec(memory_space=pltpu.VMEM)),
        input_output_aliases={i: 2 + i for i in range(ns + nl)},
        compiler_params=pltpu.CompilerParams(has_side_effects=_EFFECT, collective_id=collective_id),
    )(*[_in_hbm(a) for a in arrs], *([] if after is None else [after]))
    return out[0], out[1], list(out[2:2 + ns]), list(out[2 + ns:2 + ns + nl]), out[-1]


def _split_wait(name, send_sems, recv_sems, srcs, lands, plan, recv_slots, after):
    ns, nl = len(srcs), len(lands)

    def body(*refs):
        src_refs, land_refs = refs[:ns + nl], refs[ns:ns + nl]
        send_sems, recv_sems = refs[ns + nl], refs[ns + nl + 1]
        coords = _coords()
        slots = recv_slots(*coords)
        for k, (a, s_slot, l, _, dev) in enumerate(plan(*coords)):
            src = src_refs[a] if s_slot is None else src_refs[a].at[s_slot]
            cp = pltpu.make_async_remote_copy(src_ref=src, dst_ref=land_refs[l].at[slots[k]], send_sem=send_sems.at[k],
                                              recv_sem=recv_sems.at[k], device_id=dev, device_id_type=_MESH)
            cp.wait_send()
            cp.wait_recv()

    arrs = list(srcs) + list(lands)
    out = pl.pallas_call(
        body, name=name,
        out_shape=tuple(pltpu.HBM(a.shape, a.dtype) for a in arrs),
        in_specs=[_HBM] * (ns + nl) + [_SEM, _SEM, _ANY],
        out_specs=tuple([_HBM] * (ns + nl)),
        input_output_aliases={i: i for i in range(ns + nl)},
        compiler_params=pltpu.CompilerParams(has_side_effects=_EFFECT),
    )(*arrs, send_sems, recv_sems, after)
    return list(out[:ns]), list(out[ns:])


def _chips(x, y):
    return [(1 - x, y), (x, 1 - y), (1 - x, 1 - y)]


def _gather_plan_ici(n):
    def plan(x, y, c):
        me = _lin(x, y, c)
        out = []
        for a in range(n):
            out.append((a, me, a, me, (x, y, 1 - c)))
            out += [(a, me, a, me, (cx, cy, c)) for cx, cy in _chips(x, y)]
        return out

    def recv_slots(x, y, c):
        out = []
        for _ in range(n):
            out.append(_lin(x, y, 1 - c))
            out += [_lin(cx, cy, c) for cx, cy in _chips(x, y)]
        return out

    return plan, recv_slots


def _gather_plan_d2d(n):
    def plan(x, y, c):
        return [(a, _lin(cx, cy, c), a, _lin(cx, cy, c), (x, y, 1 - c)) for a in range(n) for cx, cy in _chips(x, y)]

    def recv_slots(x, y, c):
        return [_lin(cx, cy, 1 - c) for _ in range(n) for cx, cy in _chips(x, y)]

    return plan, recv_slots


def _all_gather_split(lands, tag, after):
    n = len(lands)
    plan1, slots1 = _gather_plan_ici(n)
    s1, r1, _, lands, token = _split_start(f"gather_{tag}_ici_start", [], lands, plan1, after, _SIBLING_AND_CHIPS)

    def finish(after):
        _, got = _split_wait(f"gather_{tag}_ici_wait", s1, r1, [], lands, plan1, slots1, after)
        plan2, slots2 = _gather_plan_d2d(n)
        s2, r2, _, got, token2 = _split_start(f"gather_{tag}_d2d_start", [], got, plan2, None, _SIBLING)
        return _split_wait(f"gather_{tag}_d2d_wait", s2, r2, [], got, plan2, slots2, token2)[1]

    return finish, token


def _all_gather_tree(lands, tag, after):
    n = len(lands)

    def plan1(x, y, c):
        me = _lin(x, y, c)
        return [(a, me, a, me, dev) for a in range(n) for dev in ((x, y, 1 - c), (1 - x, y, c), (x, 1 - y, c))]

    def slots1(x, y, c):
        return [s for _ in range(n) for s in (_lin(x, y, 1 - c), _lin(1 - x, y, c), _lin(x, 1 - y, c))]

    def plan2(x, y, c):
        from_x, from_y = _lin(1 - x, y, c), _lin(x, 1 - y, c)
        north = c == 1
        passed = jnp.where(north, from_x, from_y)
        onward = (jnp.where(north, x, 1 - x), jnp.where(north, 1 - y, y), c)
        sib = (x, y, 1 - c)
        return [cp for a in range(n) for cp in ((a, passed, a, passed, onward), (a, from_x, a, from_x, sib),
                                                (a, from_y, a, from_y, sib))]

    def slots2(x, y, c):
        return [s for _ in range(n) for s in (_lin(1 - x, 1 - y, c), _lin(1 - x, y, 1 - c), _lin(x, 1 - y, 1 - c))]

    def plan3(x, y, c):
        diag = _lin(1 - x, 1 - y, c)
        return [(a, diag, a, diag, (x, y, 1 - c)) for a in range(n)]

    def slots3(x, y, c):
        return [_lin(1 - x, 1 - y, 1 - c)] * n

    s1, r1, _, lands, token = _split_start(f"gather_{tag}_1_start", [], lands, plan1, after, _SIBLING_AND_NEIGHBOURS)
    state = {}

    def stage2(after):
        _, got = _split_wait(f"gather_{tag}_1_wait", s1, r1, [], lands, plan1, slots1, after)
        state["s"], state["r"], _, state["lands"], token2 = _split_start(f"gather_{tag}_2_start", [], got, plan2, None,
                                                                         _ONWARD_AND_SIBLING)
        return token2

    def stage3(after):
        _, got = _split_wait(f"gather_{tag}_2_wait", state["s"], state["r"], [], state["lands"], plan2, slots2, after)
        s3, r3, _, got, token3 = _split_start(f"gather_{tag}_3_start", [], got, plan3, None, _SIBLING)
        return _split_wait(f"gather_{tag}_3_wait", s3, r3, [], got, plan3, slots3, token3)[1]

    return stage2, stage3, token


_CHIP_LIST = ((0, 0), (0, 1), (1, 0), (1, 1))


def _reduce_plan_d2d(n):
    def plan(x, y, c):
        return [(a, _lin(qx, qy, 1 - c), a, q, (x, y, 1 - c)) for a in range(n) for q, (qx, qy) in enumerate(_CHIP_LIST)]

    def recv_slots(x, y, c):
        return [q for _ in range(n) for q in range(4)]

    return plan, recv_slots


def _reduce_plan_ici(n):
    def plan(x, y, c):
        return [(a, 2 * cx + cy, a, j, (cx, cy, c)) for a in range(n) for j, (cx, cy) in enumerate(_chips(x, y))]

    def recv_slots(x, y, c):
        return [j for _ in range(n) for j in range(3)]

    return plan, recv_slots


def _peers(x, y, c):
    return [(1 - x if r & 4 else x, 1 - y if r & 2 else y, 1 - c if r & 1 else c) for r in range(1, N_DEV)]


def _scatter_plan(n):
    def plan(x, y, c):
        return [(a, _lin(*peer), a, r, peer) for a in range(n) for r, peer in enumerate(_peers(x, y, c))]

    def recv_slots(x, y, c):
        return [r for _ in range(n) for r in range(N_DEV - 1)]

    return plan, recv_slots


def _broadcast_plan():
    peers = _peers

    def plan(x, y, c):
        return [(0, None, 0, _lin(x, y, c), peer) for peer in peers(x, y, c)]

    def recv_slots(x, y, c):
        return [_lin(*peer) for peer in peers(x, y, c)]

    return plan, recv_slots


def _chip_partial(grads, recvd, core, name):
    n = len(grads)

    def body(c_ref, *refs):
        for a in range(n):
            g_ref, r_ref, o_ref = refs[a], refs[n + a], refs[2 * n + a]
            o_ref[...] = (g_ref[...].astype(F32) + r_ref[...].astype(F32)).astype(o_ref.dtype)

    def blk(a, own):
        zeros = (0,) * (a.ndim - 1)
        return pl.BlockSpec((None,) + a.shape[1:],
                            (lambda q, c_ref: (2 * q + c_ref[0],) + zeros) if own else (lambda q, c_ref: (q,) + zeros))

    return pl.pallas_call(
        body, name=name,
        grid_spec=pltpu.PrefetchScalarGridSpec(
            num_scalar_prefetch=1, grid=(4,),
            in_specs=[blk(a, True) for a in grads] + [blk(a, False) for a in recvd],
            out_specs=[blk(a, False) for a in recvd]),
        out_shape=[jax.ShapeDtypeStruct(a.shape, a.dtype) for a in recvd],
        compiler_params=_params(("arbitrary",)),
    )(core, *grads, *recvd)


def _reduce_scatter_split(grads, tag, core, behind):
    n = len(grads)
    plan1, slots1 = _reduce_plan_d2d(n)
    lands1 = [lax.empty((4,) + a.shape[1:], a.dtype) for a in grads]
    s1, r1, srcs1, lands1, token1 = _split_start(f"reduce_{tag}_d2d_start", grads, lands1, plan1, None, _SIBLING)
    own, got = _split_wait(f"reduce_{tag}_d2d_wait", s1, r1, srcs1, lands1, plan1, slots1, behind(token1))
    parts = _chip_partial(own, got, core, f"reduce_{tag}_partial")
    plan2, slots2 = _reduce_plan_ici(n)
    lands2 = [lax.empty((3,) + a.shape[1:], a.dtype) for a in grads]
    s2, r2, srcs2, lands2, token2 = _split_start(f"reduce_{tag}_ici_start", parts, lands2, plan2, None, _CHIPS)

    def finish(after):
        return _split_wait(f"reduce_{tag}_ici_wait", s2, r2, srcs2, lands2, plan2, slots2, after)

    return finish, token2


def _place_shards(me, shards, dtypes, name, after=None):
    n = len(shards)

    def body(me_ref, *refs):
        for a in range(n):
            refs[n + a][...] = refs[a][...].astype(dtypes[a])

    full = lambda s: pl.BlockSpec(s.shape, lambda i, me_ref: (0,) * s.ndim)
    slot = lambda s: pl.BlockSpec((None,) + s.shape, lambda i, me_ref: (me_ref[0],) + (0,) * s.ndim)
    body, more_specs, more = _ordered_behind(body, 1 + n, after)
    return pl.pallas_call(
        body, name=name,
        grid_spec=pltpu.PrefetchScalarGridSpec(num_scalar_prefetch=1, grid=(1,),
                                               in_specs=[full(s) for s in shards] + more_specs,
                                               out_specs=[slot(s) for s in shards]),
        out_shape=[jax.ShapeDtypeStruct((N_DEV,) + s.shape, d) for s, d in zip(shards, dtypes)],
        compiler_params=_params(("arbitrary",)),
    )(me, *shards, *more)


def _mix_in_fwd(x, g1, w_in_t, conv_w, gq, gk, gconv):
    tm = 512
    n_t = SEQ // tm

    def body(x_ref, g1_ref, w_ref, cw_ref, gq_ref, gk_ref, gc_ref,
             proj_ref, u1_ref, ycn_ref, qn_ref, kn_ref, v_ref, halo_ref):
        @pl.when(pl.program_id(0) == 0)
        def _():
            halo_ref[...] = jnp.zeros_like(halo_ref)

        xv = x_ref[...]
        u = (xv * _rstd(xv) * g1_ref[...]).astype(BF16)
        u1_ref[...] = u
        proj = _dot_nt(u, w_ref[...])
        proj_ref[...] = proj
        gate_b = proj[:, 0:CONV_WIDTH]
        a = proj[:, CONV_WIDTH:2 * CONV_WIDTH] * proj[:, 2 * CONV_WIDTH:3 * CONV_WIDTH]
        cv, _, _ = _conv3(a, _taps(cw_ref[...]), halo_ref[...])
        halo_ref[...] = a[tm - SUBLANES:]
        yc = gate_b * cv
        ycn_ref[...] = (yc * _rstd(yc) * gc_ref[...]).astype(BF16)
        q0 = 3 * CONV_WIDTH
        qn_ref[...] = _head_norm(proj[:, q0:q0 + ATTN_WIDTH], gq_ref[...], N_HEADS).astype(BF16)
        k0 = q0 + ATTN_WIDTH
        kn_ref[...] = _head_norm(proj[:, k0:k0 + KV_WIDTH], gk_ref[...], 2).astype(BF16)
        v_ref[...] = proj[:, k0 + KV_WIDTH:k0 + 2 * KV_WIDTH].astype(BF16)

    const = lambda shape: pl.BlockSpec(shape, lambda i: (0,) * len(shape))
    rows = lambda w: pl.BlockSpec((tm, w), lambda i: (i, 0))
    return pl.pallas_call(
        body, name="mix_in_fwd", grid=(n_t,),
        in_specs=[rows(D_MODEL), const((1, D_MODEL)), const((IN_WIDTH, D_MODEL)), const((3, CONV_WIDTH)),
                  const((1, HEAD_DIM)), const((1, HEAD_DIM)), const((1, CONV_WIDTH))],
        out_specs=[rows(IN_WIDTH), rows(D_MODEL), rows(CONV_WIDTH), rows(ATTN_WIDTH), rows(KV_WIDTH), rows(KV_WIDTH)],
        out_shape=[jax.ShapeDtypeStruct((SEQ, IN_WIDTH), F32), jax.ShapeDtypeStruct((SEQ, D_MODEL), BF16),
                   jax.ShapeDtypeStruct((SEQ, CONV_WIDTH), BF16),
                   jax.ShapeDtypeStruct((SEQ, ATTN_WIDTH), BF16), jax.ShapeDtypeStruct((SEQ, KV_WIDTH), BF16),
                   jax.ShapeDtypeStruct((SEQ, KV_WIDTH), BF16)],
        scratch_shapes=[pltpu.VMEM((SUBLANES, CONV_WIDTH), F32)],
        compiler_params=_params(("arbitrary",)),
    )(x, g1, w_in_t, conv_w, gq, gk, gconv)


GROUP_ROWS = GQA_GROUP * BLK


def _band_bias(tbl_ref, bkt, bias_ref):
    for h in range(N_HEADS):
        acc = jnp.zeros(bkt.shape, F32)
        for b in range(NUM_BUCKETS):
            acc = jnp.where(bkt == b, tbl_ref[h, b], acc)
        bias_ref[h // GQA_GROUP, BLK * (h % GQA_GROUP):BLK * (h % GQA_GROUP + 1), :] = acc


def _band_masks(i):
    qi = lax.broadcasted_iota(jnp.int32, (GROUP_ROWS, BLK), 0) & (BLK - 1)
    ji = lax.broadcasted_iota(jnp.int32, (GROUP_ROWS, BLK), 1)
    upper = ji > qi
    return upper, upper & (i == 0)


def _stack_heads(x, g):
    return jnp.concatenate([x[:, HEAD_DIM * h:HEAD_DIM * (h + 1)] for h in range(GQA_GROUP * g, GQA_GROUP * (g + 1))], axis=0)


def _unstack_heads(groups):
    return jnp.concatenate([p[BLK * t:BLK * (t + 1)] for p in groups for t in range(GQA_GROUP)], axis=-1)


def _per_head_rows(vals):
    row = lax.broadcasted_iota(jnp.int32, (GROUP_ROWS, 1), 0)
    col = jnp.full((GROUP_ROWS, 1), vals[GQA_GROUP - 1], F32)
    for t in range(GQA_GROUP - 2, -1, -1):
        col = jnp.where(row < BLK * (t + 1), vals[t], col)
    return col


def _band_rows(ref, i):
    prev = pl.multiple_of(jnp.maximum(i - 1, 0) * BLK, BLK)
    cur = pl.multiple_of(i * BLK, BLK)
    return jnp.concatenate([ref[pl.ds(prev, BLK), :], ref[pl.ds(cur, BLK), :]], axis=0), prev, cur


def _fold(band, upper):
    return jnp.where(upper, band[:, :BLK], band[:, BLK:])


def _unfold(tile, upper):
    return jnp.concatenate([jnp.where(upper, tile, 0.0), jnp.where(upper, 0.0, tile)], axis=1)


def _head_probs(qh, kh, bias, upper, dead, sink):
    logits = _fold(_dot_nt(qh, kh), upper) * (HEAD_DIM ** -0.5) + bias
    logits = jnp.where(dead, NEG_INF, logits)
    m = jnp.maximum(jnp.max(logits, axis=-1, keepdims=True), sink)
    p = jnp.exp(logits - m)
    es = jnp.exp(sink - m)
    den = jnp.sum(p, axis=-1, keepdims=True) + es
    return p / den, es / den


def _attn_fwd(qn, kn, v, tbl, sinks, bkt, gattn, after=None):
    n_b = SEQ // BLK

    def body(q_ref, k_ref, v_ref, tbl_ref, sink_ref, bkt_ref, ga_ref, y_ref, yn_ref, p_ref, ps_ref, bias_ref):
        i = pl.program_id(0)

        @pl.when(i == 0)
        def _():
            _band_bias(tbl_ref, bkt_ref[...], bias_ref)

        kb, _, _ = _band_rows(k_ref, i)
        vb, _, _ = _band_rows(v_ref, i)
        upper, dead = _band_masks(i)
        q = q_ref[...]
        lane = lax.broadcasted_iota(jnp.int32, (BLK, 128), 1)
        outs = []
        psinks = jnp.zeros((BLK, 128), F32)
        for g in range(N_HEADS // GQA_GROUP):
            kv = slice(HEAD_DIM * g, HEAD_DIM * (g + 1))
            sink = _per_head_rows([sink_ref[0, GQA_GROUP * g + t] for t in range(GQA_GROUP)])
            probs, psink = _head_probs(_stack_heads(q, g), kb[:, kv], bias_ref[g], upper, dead, sink)
            p_ref[g] = probs.astype(BF16)
            for t in range(GQA_GROUP):
                psinks = jnp.where(lane == GQA_GROUP * g + t, psink[BLK * t:BLK * (t + 1)], psinks)
            outs.append(_dot(_unfold(probs, upper).astype(BF16), vb[:, kv]))
        ps_ref[...] = psinks
        y = _unstack_heads(outs)
        y_ref[...] = y
        yn_ref[...] = (y * _rstd(y) * ga_ref[...]).astype(BF16)

    const = lambda shape: pl.BlockSpec(shape, lambda i: (0,) * len(shape))
    rows = lambda w: pl.BlockSpec((BLK, w), lambda i: (i, 0))
    smem = pl.BlockSpec(memory_space=pltpu.SMEM)
    body, more_specs, more = _ordered_behind(body, 7, after)
    return pl.pallas_call(
        body, name="attn_fwd", grid=(n_b,),
        in_specs=[rows(ATTN_WIDTH), const((SEQ, KV_WIDTH)), const((SEQ, KV_WIDTH)), smem, smem,
                  const((BLK, BLK)), const((1, ATTN_WIDTH))] + more_specs,
        out_specs=[rows(ATTN_WIDTH), rows(ATTN_WIDTH),
                   pl.BlockSpec((None, N_HEADS // GQA_GROUP, GROUP_ROWS, BLK), lambda i: (i, 0, 0, 0)), rows(128)],
        out_shape=[jax.ShapeDtypeStruct((SEQ, ATTN_WIDTH), F32), jax.ShapeDtypeStruct((SEQ, ATTN_WIDTH), BF16),
                   jax.ShapeDtypeStruct((n_b, N_HEADS // GQA_GROUP, GROUP_ROWS, BLK), BF16),
                   jax.ShapeDtypeStruct((SEQ, 128), F32)],
        scratch_shapes=[pltpu.VMEM((N_HEADS // GQA_GROUP, GROUP_ROWS, BLK), F32)],
        compiler_params=_params(("arbitrary",)),
    )(qn, kn, v, tbl, sinks, bkt, gattn, *more)


def _ffn_up(x, ycn, yan, w_out, g2, w_up, fcw, fcb, after=None):
    tm = 512
    n_t = SEQ // tm

    def body(x_ref, ycn_ref, yan_ref, wo_ref, g2_ref, wu_ref, cw_ref, b_ref,
             h1_ref, u2_ref, up_ref, pre_ref, act_ref, halo_ref):
        i, j = pl.program_id(0), pl.program_id(1)

        @pl.when(j == 0)
        def _():
            h1 = x_ref[...] + _dot(ycn_ref[...], wo_ref[0:CONV_WIDTH, :]) + _dot(yan_ref[...], wo_ref[CONV_WIDTH:, :])
            h1_ref[...] = h1
            u2_ref[...] = (h1 * _rstd(h1) * g2_ref[...]).astype(BF16)

        u2 = u2_ref[...]
        pre = []
        for s in range(2):
            up = _dot_nt(u2, wu_ref[s])
            up_ref[s] = up.astype(BF16)
            halo = jnp.where(i == 0, 0.0, halo_ref[s, j])
            pre.append(_conv3(up, _taps(cw_ref.at[s]), halo)[0] + b_ref[s])
            pre_ref[s] = pre[s].astype(BF16)
            halo_ref[s, j] = up[tm - SUBLANES:]
        g, val = pre
        act_ref[...] = (g * jax.nn.sigmoid(g) * val).astype(BF16)

    rows = lambda w: pl.BlockSpec((tm, w), lambda i, j: (i, 0))
    const = lambda shape: pl.BlockSpec(shape, lambda i, j: (0,) * len(shape))
    pair = lambda *s: pl.BlockSpec((2, None) + s, lambda i, j: (0, j) + (0,) * len(s))
    upb = pl.BlockSpec((2, None, tm, FFN_BLK), lambda i, j: (0, j, i, 0))
    body, more_specs, more = _ordered_behind(body, 8, after)
    return pl.pallas_call(
        body, name="ffn_up", grid=(n_t, N_FFN_BLK),
        in_specs=[rows(D_MODEL), rows(CONV_WIDTH), rows(ATTN_WIDTH), const((D_MODEL, D_MODEL)), const((1, D_MODEL)),
                  pair(FFN_BLK, D_MODEL), pair(3, 1, FFN_BLK), pair(1, FFN_BLK)] + more_specs,
        out_specs=[rows(D_MODEL), rows(D_MODEL), upb, upb, pl.BlockSpec((None, tm, FFN_BLK), lambda i, j: (j, i, 0))],
        out_shape=[jax.ShapeDtypeStruct((SEQ, D_MODEL), F32), jax.ShapeDtypeStruct((SEQ, D_MODEL), BF16),
                   jax.ShapeDtypeStruct((2, N_FFN_BLK, SEQ, FFN_BLK), BF16),
                   jax.ShapeDtypeStruct((2, N_FFN_BLK, SEQ, FFN_BLK), BF16),
                   jax.ShapeDtypeStruct((N_FFN_BLK, SEQ, FFN_BLK), BF16)],
        scratch_shapes=[pltpu.VMEM((2, N_FFN_BLK, SUBLANES, FFN_BLK), F32)],
        compiler_params=_params(("arbitrary", "arbitrary")),
    )(x, ycn, yan, w_out, g2, w_up, fcw, fcb, *more)


def _ffn_down(act, w_down, h1, tgt):
    tm = 512
    n_t = SEQ // tm

    def body(act_ref, wd_ref, h1_ref, tgt_ref, dh2_ref, dh2b_ref, loss_ref):
        @pl.when(pl.program_id(0) == 0)
        def _():
            loss_ref[...] = jnp.zeros_like(loss_ref)

        out = _dot(act_ref[0], wd_ref[0])
        for j in range(1, N_FFN_BLK):
            out = out + _dot(act_ref[j], wd_ref[j])
        err = h1_ref[...] + out - tgt_ref[...]
        loss_ref[...] += 0.5 * jnp.sum(err * err) / D_MODEL
        dh2 = err / D_MODEL
        dh2_ref[...] = dh2
        dh2b_ref[...] = dh2.astype(BF16)

    rows = lambda w: pl.BlockSpec((tm, w), lambda i: (i, 0))
    return pl.pallas_call(
        body, name="ffn_down", grid=(n_t,),
        in_specs=[pl.BlockSpec((N_FFN_BLK, tm, FFN_BLK), lambda i: (0, i, 0)),
                  pl.BlockSpec((N_FFN_BLK, FFN_BLK, D_MODEL), lambda i: (0, 0, 0)), rows(D_MODEL), rows(D_MODEL)],
        out_specs=[rows(D_MODEL), rows(D_MODEL), pl.BlockSpec((SUBLANES, 128), lambda i: (0, 0))],
        out_shape=[jax.ShapeDtypeStruct((SEQ, D_MODEL), F32), jax.ShapeDtypeStruct((SEQ, D_MODEL), BF16),
                   jax.ShapeDtypeStruct((SUBLANES, 128), F32)],
        compiler_params=_params(("arbitrary",)),
    )(act, w_down, h1, tgt)


def _out_proj(x, ycn, yan, w_out, g2):
    tm = 512

    def body(x_ref, ycn_ref, yan_ref, wo_ref, g2_ref, h1_ref, u2_ref):
        h1 = x_ref[...] + _dot(ycn_ref[...], wo_ref[0:CONV_WIDTH, :]) + _dot(yan_ref[...], wo_ref[CONV_WIDTH:, :])
        h1_ref[...] = h1
        u2_ref[...] = (h1 * _rstd(h1) * g2_ref[...]).astype(BF16)

    rows = lambda w: pl.BlockSpec((tm, w), lambda i: (i, 0))
    const = lambda shape: pl.BlockSpec(shape, lambda i: (0,) * len(shape))
    return pl.pallas_call(
        body, name="out_proj", grid=(SEQ // tm,),
        in_specs=[rows(D_MODEL), rows(CONV_WIDTH), rows(ATTN_WIDTH), const((D_MODEL, D_MODEL)), const((1, D_MODEL))],
        out_specs=[rows(D_MODEL), rows(D_MODEL)],
        out_shape=[jax.ShapeDtypeStruct((SEQ, D_MODEL), F32), jax.ShapeDtypeStruct((SEQ, D_MODEL), BF16)],
        compiler_params=_params(("arbitrary",)),
    )(x, ycn, yan, w_out, g2)


def _ffn_fwd(h1, u2, w_up, fcw, fcb, w_down, tgt):
    tm = 512
    n_t = SEQ // tm
    last = N_FFN_BLK - 1

    def body(u2_ref, wu_ref, cw_ref, b_ref, wd_ref, h1_ref, tgt_ref,
             up_ref, pre_ref, act_ref, dh2_ref, dh2b_ref, loss_ref, acc_ref, halo_ref):
        j, i = pl.program_id(0), pl.program_id(1)
        rows = pl.ds(pl.multiple_of(i * tm, tm), tm)

        @pl.when((i == 0) & (j == 0))
        def _():
            loss_ref[...] = jnp.zeros_like(loss_ref)

        u2 = u2_ref[...]
        pre = []
        for s in range(2):
            up = _dot_nt(u2, wu_ref[s])
            up_ref[s] = up.astype(BF16)
            halo = jnp.where(i == 0, 0.0, halo_ref[s])
            pre.append(_conv3(up, _taps(cw_ref.at[s]), halo)[0] + b_ref[s])
            pre_ref[s] = pre[s].astype(BF16)
            halo_ref[s] = up[tm - SUBLANES:]
        g, val = pre
        act = (g * jax.nn.sigmoid(g) * val).astype(BF16)
        act_ref[...] = act
        out = _dot(act, wd_ref[...])

        @pl.when(j == 0)
        def _():
            acc_ref[rows, :] = out

        @pl.when(j > 0)
        def _():
            acc_ref[rows, :] += out

        @pl.when(j == last)
        def _():
            err = h1_ref[...] + acc_ref[rows, :] - tgt_ref[...]
            loss_ref[...] += 0.5 * jnp.sum(err * err) / D_MODEL
            dh2 = err / D_MODEL
            dh2_ref[...] = dh2
            dh2b_ref[...] = dh2.astype(BF16)

    late = lambda w: pl.BlockSpec((tm, w), lambda j, i: (jnp.where(j == last, i, 0), 0))
    pair = lambda *s: pl.BlockSpec((2, None) + s, lambda j, i: (0, j) + (0,) * len(s))
    upb = pl.BlockSpec((2, None, tm, FFN_BLK), lambda j, i: (0, j, i, 0))
    return pl.pallas_call(
        body, name="ffn_fwd", grid=(N_FFN_BLK, n_t),
        in_specs=[pl.BlockSpec((tm, D_MODEL), lambda j, i: (i, 0)), pair(FFN_BLK, D_MODEL), pair(3, 1, FFN_BLK),
                  pair(1, FFN_BLK), pl.BlockSpec((None, FFN_BLK, D_MODEL), lambda j, i: (j, 0, 0)),
                  late(D_MODEL), late(D_MODEL)],
        out_specs=[upb, upb, pl.BlockSpec((None, tm, FFN_BLK), lambda j, i: (j, i, 0)), late(D_MODEL), late(D_MODEL),
                   pl.BlockSpec((SUBLANES, 128), lambda j, i: (0, 0))],
        out_shape=[jax.ShapeDtypeStruct((2, N_FFN_BLK, SEQ, FFN_BLK), BF16),
                   jax.ShapeDtypeStruct((2, N_FFN_BLK, SEQ, FFN_BLK), BF16),
                   jax.ShapeDtypeStruct((N_FFN_BLK, SEQ, FFN_BLK), BF16),
                   jax.ShapeDtypeStruct((SEQ, D_MODEL), F32), jax.ShapeDtypeStruct((SEQ, D_MODEL), BF16),
                   jax.ShapeDtypeStruct((SUBLANES, 128), F32)],
        scratch_shapes=[pltpu.VMEM((SEQ, D_MODEL), F32), pltpu.VMEM((2, SUBLANES, FFN_BLK), F32)],
        compiler_params=_params(("arbitrary", "arbitrary")),
    )(u2, w_up, fcw, fcb, w_down, h1, tgt)


def _ffn_fwd_token_major(x, ycn, yan, w_out, g2, w_up, fcw, fcb, w_down, tgt):
    tm = 512
    n_t = SEQ // tm

    def body(x_ref, ycn_ref, yan_ref, wo_ref, g2_ref, wu_ref, cw_ref, b_ref, wd_ref, tgt_ref,
             h1_ref, u2_ref, up_ref, pre_ref, act_ref, dh2_ref, dh2b_ref, loss_ref, acc_ref, halo_ref):
        i, j = pl.program_id(0), pl.program_id(1)

        @pl.when((i == 0) & (j == 0))
        def _():
            loss_ref[...] = jnp.zeros_like(loss_ref)

        @pl.when(j == 0)
        def _():
            h1 = x_ref[...] + _dot(ycn_ref[...], wo_ref[0:CONV_WIDTH, :]) + _dot(yan_ref[...], wo_ref[CONV_WIDTH:, :])
            h1_ref[...] = h1
            u2_ref[...] = (h1 * _rstd(h1) * g2_ref[...]).astype(BF16)
            acc_ref[...] = jnp.zeros_like(acc_ref)

        u2 = u2_ref[...]
        pre = []
        for s in range(2):
            up = _dot_nt(u2, wu_ref[s])
            up_ref[s] = up.astype(BF16)
            halo = jnp.where(i == 0, 0.0, halo_ref[s, j])
            pre.append(_conv3(up, _taps(cw_ref.at[s]), halo)[0] + b_ref[s])
            pre_ref[s] = pre[s].astype(BF16)
            halo_ref[s, j] = up[tm - SUBLANES:]
        g, val = pre
        act = (g * jax.nn.sigmoid(g) * val).astype(BF16)
        act_ref[...] = act
        acc_ref[...] += _dot(act, wd_ref[...])

        @pl.when(j == N_FFN_BLK - 1)
        def _():
            err = h1_ref[...] + acc_ref[...] - tgt_ref[...]
            loss_ref[...] += 0.5 * jnp.sum(err * err) / D_MODEL
            dh2 = err / D_MODEL
            dh2_ref[...] = dh2
            dh2b_ref[...] = dh2.astype(BF16)

    rows = lambda w: pl.BlockSpec((tm, w), lambda i, j: (i, 0))
    const = lambda shape: pl.BlockSpec(shape, lambda i, j: (0,) * len(shape))
    pair = lambda *s: pl.BlockSpec((2, None) + s, lambda i, j: (0, j) + (0,) * len(s))
    upb = pl.BlockSpec((2, None, tm, FFN_BLK), lambda i, j: (0, j, i, 0))
    return pl.pallas_call(
        body, name="ffn_fwd", grid=(n_t, N_FFN_BLK),
        in_specs=[rows(D_MODEL), rows(CONV_WIDTH), rows(ATTN_WIDTH), const((D_MODEL, D_MODEL)), const((1, D_MODEL)),
                  pair(FFN_BLK, D_MODEL), pair(3, 1, FFN_BLK), pair(1, FFN_BLK),
                  pl.BlockSpec((None, FFN_BLK, D_MODEL), lambda i, j: (j, 0, 0)), rows(D_MODEL)],
        out_specs=[rows(D_MODEL), rows(D_MODEL), upb, upb, pl.BlockSpec((None, tm, FFN_BLK), lambda i, j: (j, i, 0)),
                   rows(D_MODEL), rows(D_MODEL), const((SUBLANES, 128))],
        out_shape=[jax.ShapeDtypeStruct((SEQ, D_MODEL), F32), jax.ShapeDtypeStruct((SEQ, D_MODEL), BF16),
                   jax.ShapeDtypeStruct((2, N_FFN_BLK, SEQ, FFN_BLK), BF16),
                   jax.ShapeDtypeStruct((2, N_FFN_BLK, SEQ, FFN_BLK), BF16),
                   jax.ShapeDtypeStruct((N_FFN_BLK, SEQ, FFN_BLK), BF16),
                   jax.ShapeDtypeStruct((SEQ, D_MODEL), F32), jax.ShapeDtypeStruct((SEQ, D_MODEL), BF16),
                   jax.ShapeDtypeStruct((SUBLANES, 128), F32)],
        scratch_shapes=[pltpu.VMEM((tm, D_MODEL), F32), pltpu.VMEM((2, N_FFN_BLK, SUBLANES, FFN_BLK), F32)],
        compiler_params=_params(("arbitrary", "arbitrary")),
    )(x, ycn, yan, w_out, g2, w_up, fcw, fcb, w_down, tgt)


def _ffn_bwd(dh2, dh2b, h1, g2, up, pre, w_up, fcw, w_down):
    tm = 256
    n_t = SEQ // tm
    last = N_FFN_BLK - 1

    def body(dh2b_ref, up_ref, pre_ref, wu_ref, cw_ref, wd_ref, dh2_ref, h1_ref, g2_ref,
             dup_ref, dh1_ref, dh1b_ref, dfb_ref, dfcw_ref, dg2_ref, acc_ref, next_ref):
        j, i = pl.program_id(0), pl.program_id(1)
        rows = pl.ds(pl.multiple_of((n_t - 1 - i) * tm, tm), tm)

        @pl.when((i == 0) & (j == 0))
        def _():
            dfb_ref[...] = jnp.zeros_like(dfb_ref)
            dfcw_ref[...] = jnp.zeros_like(dfcw_ref)
            dg2_ref[...] = jnp.zeros_like(dg2_ref)

        g, val = pre_ref[0].astype(F32), pre_ref[1].astype(F32)
        sg = jax.nn.sigmoid(g)
        silu = g * sg
        dact = _dot_nt(dh2b_ref[...], wd_ref[...])
        dpre = (dact * val * (sg * (1.0 + g * (1.0 - sg))), dact * silu)
        du = None
        for s in range(2):
            d = dpre[s]
            u = up_ref[s].astype(F32)
            w = _taps(cw_ref.at[s])
            nxt = jnp.where(i == 0, 0.0, next_ref[s])
            d1 = _shift_up(d, 1, nxt)
            d2 = _shift_up(d, 2, nxt)
            next_ref[s] = d[:SUBLANES]
            dfb_ref[s, j] += jnp.sum(d, axis=0, keepdims=True)
            dfcw_ref[s, j, 0] += jnp.sum(d2 * u, axis=0, keepdims=True)
            dfcw_ref[s, j, 1] += jnp.sum(d1 * u, axis=0, keepdims=True)
            dfcw_ref[s, j, 2] += jnp.sum(d * u, axis=0, keepdims=True)
            dup = (d * w[2] + d1 * w[1] + d2 * w[0]).astype(BF16)
            dup_ref[s] = dup
            part = _dot(dup, wu_ref[s])
            du = part if du is None else du + part

        @pl.when(j == 0)
        def _():
            acc_ref[rows, :] = du

        @pl.when(j > 0)
        def _():
            acc_ref[rows, :] += du

        @pl.when(j == last)
        def _():
            dn, dgain = _rms_bwd(h1_ref[...], g2_ref[...], acc_ref[rows, :])
            dh1 = dh2_ref[...] + dn
            dh1_ref[...] = dh1
            dh1b_ref[...] = dh1.astype(BF16)
            dg2_ref[...] += dgain

    rev = lambda i: n_t - 1 - i
    const = lambda shape: pl.BlockSpec(shape, lambda j, i: (0,) * len(shape))
    late = lambda w: pl.BlockSpec((tm, w), lambda j, i: (jnp.where(j == last, rev(i), n_t - 1), 0))
    pair = lambda *s: pl.BlockSpec((2, None) + s, lambda j, i: (0, j) + (0,) * len(s))
    upb = pl.BlockSpec((2, None, tm, FFN_BLK), lambda j, i: (0, j, rev(i), 0))
    return pl.pallas_call(
        body, name="ffn_bwd", grid=(N_FFN_BLK, n_t),
        in_specs=[pl.BlockSpec((tm, D_MODEL), lambda j, i: (rev(i), 0)), upb, upb, pair(FFN_BLK, D_MODEL),
                  pair(3, 1, FFN_BLK), pl.BlockSpec((None, FFN_BLK, D_MODEL), lambda j, i: (j, 0, 0)),
                  late(D_MODEL), late(D_MODEL), const((1, D_MODEL))],
        out_specs=[upb, late(D_MODEL), late(D_MODEL),
                   const((2, N_FFN_BLK, 1, FFN_BLK)), const((2, N_FFN_BLK, 3, 1, FFN_BLK)), const((1, D_MODEL))],
        out_shape=[jax.ShapeDtypeStruct((2, N_FFN_BLK, SEQ, FFN_BLK), BF16), jax.ShapeDtypeStruct((SEQ, D_MODEL), F32),
                   jax.ShapeDtypeStruct((SEQ, D_MODEL), BF16), jax.ShapeDtypeStruct((2, N_FFN_BLK, 1, FFN_BLK), F32),
                   jax.ShapeDtypeStruct((2, N_FFN_BLK, 3, 1, FFN_BLK), F32), jax.ShapeDtypeStruct((1, D_MODEL), F32)],
        scratch_shapes=[pltpu.VMEM((SEQ, D_MODEL), F32), pltpu.VMEM((2, SUBLANES, FFN_BLK), F32)],
        compiler_params=_params(("arbitrary", "arbitrary")),
    )(dh2b, up, pre, w_up, fcw, w_down, dh2, h1, g2)


def _ffn_bwd_token_major(dh2, dh2b, h1, g2, up, pre, w_up, fcw, w_down, after=None):
    tm = 512
    n_t = SEQ // tm

    def body(dh2_ref, dh2b_ref, h1_ref, g2_ref, up_ref, pre_ref, wu_ref, cw_ref, wd_ref,
             dup_ref, dh1_ref, dh1b_ref, dfb_ref, dfcw_ref, dg2_ref, acc_ref, next_ref):
        i, j = pl.program_id(0), pl.program_id(1)

        @pl.when((i == 0) & (j == 0))
        def _():
            dfb_ref[...] = jnp.zeros_like(dfb_ref)
            dfcw_ref[...] = jnp.zeros_like(dfcw_ref)
            dg2_ref[...] = jnp.zeros_like(dg2_ref)

        @pl.when(j == 0)
        def _():
            acc_ref[...] = jnp.zeros_like(acc_ref)

        g, val = pre_ref[0].astype(F32), pre_ref[1].astype(F32)
        sg = jax.nn.sigmoid(g)
        silu = g * sg
        dact = _dot_nt(dh2b_ref[...], wd_ref[...])
        dpre = (dact * val * (sg * (1.0 + g * (1.0 - sg))), dact * silu)
        for s in range(2):
            d = dpre[s]
            u = up_ref[s].astype(F32)
            w = _taps(cw_ref.at[s])
            nxt = jnp.where(i == 0, 0.0, next_ref[s, j])
            d1 = _shift_up(d, 1, nxt)
            d2 = _shift_up(d, 2, nxt)
            next_ref[s, j] = d[:SUBLANES]
            dfb_ref[s, j] += jnp.sum(d, axis=0, keepdims=True)
            dfcw_ref[s, j, 0] += jnp.sum(d2 * u, axis=0, keepdims=True)
            dfcw_ref[s, j, 1] += jnp.sum(d1 * u, axis=0, keepdims=True)
            dfcw_ref[s, j, 2] += jnp.sum(d * u, axis=0, keepdims=True)
            dup = (d * w[2] + d1 * w[1] + d2 * w[0]).astype(BF16)
            dup_ref[s] = dup
            acc_ref[...] += _dot(dup, wu_ref[s])

        @pl.when(j == N_FFN_BLK - 1)
        def _():
            dn, dgain = _rms_bwd(h1_ref[...], g2_ref[...], acc_ref[...])
            dh1 = dh2_ref[...] + dn
            dh1_ref[...] = dh1
            dh1b_ref[...] = dh1.astype(BF16)
            dg2_ref[...] += dgain

    rev = lambda i: n_t - 1 - i
    rows = lambda w: pl.BlockSpec((tm, w), lambda i, j: (rev(i), 0))
    const = lambda shape: pl.BlockSpec(shape, lambda i, j: (0,) * len(shape))
    pair = lambda *s: pl.BlockSpec((2, None) + s, lambda i, j: (0, j) + (0,) * len(s))
    upb = pl.BlockSpec((2, None, tm, FFN_BLK), lambda i, j: (0, j, rev(i), 0))
    body, more_specs, more = _ordered_behind(body, 9, after)
    return pl.pallas_call(
        body, name="ffn_bwd", grid=(n_t, N_FFN_BLK),
        in_specs=[rows(D_MODEL), rows(D_MODEL), rows(D_MODEL), const((1, D_MODEL)), upb, upb,
                  pair(FFN_BLK, D_MODEL), pair(3, 1, FFN_BLK),
                  pl.BlockSpec((None, FFN_BLK, D_MODEL), lambda i, j: (j, 0, 0))] + more_specs,
        out_specs=[upb, rows(D_MODEL), rows(D_MODEL),
                   const((2, N_FFN_BLK, 1, FFN_BLK)), const((2, N_FFN_BLK, 3, 1, FFN_BLK)), const((1, D_MODEL))],
        out_shape=[jax.ShapeDtypeStruct((2, N_FFN_BLK, SEQ, FFN_BLK), BF16), jax.ShapeDtypeStruct((SEQ, D_MODEL), F32),
                   jax.ShapeDtypeStruct((SEQ, D_MODEL), BF16), jax.ShapeDtypeStruct((2, N_FFN_BLK, 1, FFN_BLK), F32),
                   jax.ShapeDtypeStruct((2, N_FFN_BLK, 3, 1, FFN_BLK), F32), jax.ShapeDtypeStruct((1, D_MODEL), F32)],
        scratch_shapes=[pltpu.VMEM((tm, D_MODEL), F32), pltpu.VMEM((2, N_FFN_BLK, SUBLANES, FFN_BLK), F32)],
        compiler_params=_params(("arbitrary", "arbitrary")),
    )(dh2, dh2b, h1, g2, up, pre, w_up, fcw, w_down, *more)


def _grad_tn(a_list, b, out_rows, name, after=None):
    n = len(a_list)
    ncol = b.shape[1]

    def body(*refs):
        a_refs, b_ref, o_ref = refs[:n], refs[n], refs[n + 1]
        j = pl.program_id(0)
        for k in range(n):
            @pl.when(j == k)
            def _(k=k):
                o_ref[...] = _dot_tn(a_refs[k][...], b_ref[...]).astype(BF16)

    full = lambda shape: pl.BlockSpec(shape, lambda j: (0,) * len(shape))
    body, more_specs, more = _ordered_behind(body, n + 1, after)
    return pl.pallas_call(
        body, name=name, grid=(n,),
        in_specs=[full((SEQ, out_rows))] * n + [full((SEQ, ncol))] + more_specs,
        out_specs=pl.BlockSpec((None, out_rows, ncol), lambda j: (j, 0, 0)),
        out_shape=jax.ShapeDtypeStruct((n, out_rows, ncol), BF16),
        compiler_params=_params(("arbitrary",)),
    )(*a_list, b, *more)


def _grad_tn_blocked(a, b, name, a_is_blocked=True, per_step=2):
    assert a_is_blocked
    nb, _, a_w = a.shape
    b_w = b.shape[-1]

    def body(a_ref, b_ref, o_ref):
        for p in range(per_step):
            o_ref[p] = _dot_tn(a_ref[p], b_ref[...]).astype(BF16)

    return pl.pallas_call(
        body, name=name, grid=(nb // per_step,),
        in_specs=[pl.BlockSpec((per_step, SEQ, a_w), lambda k: (k, 0, 0)), pl.BlockSpec((SEQ, b_w), lambda k: (0, 0))],
        out_specs=pl.BlockSpec((per_step, a_w, b_w), lambda k: (k, 0, 0)),
        out_shape=jax.ShapeDtypeStruct((nb, a_w, b_w), BF16),
        compiler_params=_params(("arbitrary",)),
    )(a, b)


def _out_bwd(dh1b, w_out, y_attn, gattn, after=None):
    tm = 512
    n_t = SEQ // tm

    def body(dh_ref, wo_ref, y_ref, ga_ref, dycn_ref, dy_ref, dga_ref):
        @pl.when(pl.program_id(0) == 0)
        def _():
            dga_ref[...] = jnp.zeros_like(dga_ref)

        dycat = _dot_nt(dh_ref[...], wo_ref[...])
        dycn_ref[...] = dycat[:, :CONV_WIDTH]
        dy, dga = _rms_bwd(y_ref[...], ga_ref[...], dycat[:, CONV_WIDTH:])
        dy_ref[...] = dy
        dga_ref[...] += dga

    rows = lambda w: pl.BlockSpec((tm, w), lambda i: (i, 0))
    const = lambda shape: pl.BlockSpec(shape, lambda i: (0,) * len(shape))
    body, more_specs, more = _ordered_behind(body, 4, after)
    return pl.pallas_call(
        body, name="out_bwd", grid=(n_t,),
        in_specs=[rows(D_MODEL), const((D_MODEL, D_MODEL)), rows(ATTN_WIDTH), const((1, ATTN_WIDTH))] + more_specs,
        out_specs=[rows(CONV_WIDTH), rows(ATTN_WIDTH), const((1, ATTN_WIDTH))],
        out_shape=[jax.ShapeDtypeStruct((SEQ, CONV_WIDTH), F32), jax.ShapeDtypeStruct((SEQ, ATTN_WIDTH), F32),
                   jax.ShapeDtypeStruct((1, ATTN_WIDTH), F32)],
        compiler_params=_params(("arbitrary",)),
    )(dh1b, w_out, y_attn, gattn, *more)


def _attn_bwd(qn, kn, v, dy, probs, psinks, bkt, after=None):
    n_b = SEQ // BLK

    def body(q_ref, k_ref, v_ref, dy_ref, p_ref, ps_ref, bkt_ref,
             dq_ref, dk_ref, dv_ref, dtbl_ref, dsink_ref, dbias_ref, dsacc_ref):
        i = pl.program_id(0)

        @pl.when(i == 0)
        def _():
            dbias_ref[...] = jnp.zeros_like(dbias_ref)
            dsacc_ref[...] = jnp.zeros_like(dsacc_ref)
            dk_ref[...] = jnp.zeros_like(dk_ref)
            dv_ref[...] = jnp.zeros_like(dv_ref)

        kb, prev, cur = _band_rows(k_ref, i)
        vb, _, _ = _band_rows(v_ref, i)
        upper, _ = _band_masks(i)
        q = q_ref[...]
        dy = dy_ref[...]
        psink = ps_ref[...]
        lane = lax.broadcasted_iota(jnp.int32, (BLK, 128), 1)
        dsink = jnp.zeros((BLK, 128), F32)
        dqs, dks, dvs = [], [], []
        for g in range(N_HEADS // GQA_GROUP):
            kv = slice(HEAD_DIM * g, HEAD_DIM * (g + 1))
            qg = _stack_heads(q, g)
            dog = _stack_heads(dy, g).astype(BF16)
            pb = p_ref[g]
            pg = pb.astype(F32)
            dprobs = _fold(_dot_nt(dog, vb[:, kv]), upper)
            dvs.append(_dot_tn(_unfold(pb, upper), dog))
            dsum = jnp.sum(pg * dprobs, axis=-1, keepdims=True)
            dlogits = pg * (dprobs - dsum)
            for t in range(GQA_GROUP):
                dsink = jnp.where(lane == GQA_GROUP * g + t, -psink * dsum[BLK * t:BLK * (t + 1)], dsink)
            dbias_ref[g] += dlogits
            ds = _unfold(dlogits * (HEAD_DIM ** -0.5), upper).astype(BF16)
            dqs.append(_dot(ds, kb[:, kv]))
            dks.append(_dot_tn(ds, qg))
        dsacc_ref[...] += dsink
        dq_ref[...] = _unstack_heads(dqs)
        dkb = jnp.concatenate(dks, axis=-1)
        dvb = jnp.concatenate(dvs, axis=-1)
        dk_ref[pl.ds(prev, BLK), :] += dkb[:BLK]
        dk_ref[pl.ds(cur, BLK), :] += dkb[BLK:]
        dv_ref[pl.ds(prev, BLK), :] += dvb[:BLK]
        dv_ref[pl.ds(cur, BLK), :] += dvb[BLK:]

        @pl.when(i == n_b - 1)
        def _():
            bkt = bkt_ref[...]
            row8 = lax.broadcasted_iota(jnp.int32, (N_HEADS, 128), 0)
            lane8 = lax.broadcasted_iota(jnp.int32, (N_HEADS, 128), 1)
            acc = jnp.zeros((N_HEADS, 128), F32)
            for h in range(N_HEADS):
                rows = slice(BLK * (h % GQA_GROUP), BLK * (h % GQA_GROUP + 1))
                dbh = dbias_ref[h // GQA_GROUP, rows, :]
                for b in range(NUM_BUCKETS):
                    acc = jnp.where((row8 == h) & (lane8 == b), jnp.sum(jnp.where(bkt == b, dbh, 0.0)), acc)
            dsink_ref[...] = jnp.sum(dsacc_ref[...], axis=0, keepdims=True)
            dtbl_ref[...] = acc

    const = lambda shape: pl.BlockSpec(shape, lambda i: (0,) * len(shape))
    rows = lambda w: pl.BlockSpec((BLK, w), lambda i: (i, 0))
    n_g = N_HEADS // GQA_GROUP
    body, more_specs, more = _ordered_behind(body, 7, after)
    return pl.pallas_call(
        body, name="attn_bwd", grid=(n_b,),
        in_specs=[rows(ATTN_WIDTH), const((SEQ, KV_WIDTH)), const((SEQ, KV_WIDTH)), rows(ATTN_WIDTH),
                  pl.BlockSpec((None, n_g, GROUP_ROWS, BLK), lambda i: (i, 0, 0, 0)), rows(128),
                  const((BLK, BLK))] + more_specs,
        out_specs=[rows(ATTN_WIDTH), const((SEQ, KV_WIDTH)), const((SEQ, KV_WIDTH)), const((N_HEADS, 128)), const((1, 128))],
        out_shape=[jax.ShapeDtypeStruct((SEQ, ATTN_WIDTH), F32), jax.ShapeDtypeStruct((SEQ, KV_WIDTH), F32),
                   jax.ShapeDtypeStruct((SEQ, KV_WIDTH), F32), jax.ShapeDtypeStruct((N_HEADS, 128), F32),
                   jax.ShapeDtypeStruct((1, 128), F32)],
        scratch_shapes=[pltpu.VMEM((n_g, GROUP_ROWS, BLK), F32), pltpu.VMEM((BLK, 128), F32)],
        compiler_params=_params(("arbitrary",)),
    )(qn, kn, v, dy, probs, psinks, bkt, *more)


def _mix_in_bwd(x, dh1, proj, dycn, dqn, dkn, dv, w_in_t, conv_w, g1, gq, gk, gconv):
    tm = 512
    n_t = SEQ // tm
    halo_blocks = tm // SUBLANES

    def body(x_ref, dh1_ref, proj_ref, halo_ref, dycn_ref, dqn_ref, dkn_ref, dv_ref, w_ref, cw_ref,
             g1_ref, gq_ref, gk_ref, gc_ref,
             dx_ref, dproj_ref, dcw_ref, dgc_ref, dgq_ref, dgk_ref, dg1_ref, next_ref):
        i = pl.program_id(0)
        first_tile = i == n_t - 1

        @pl.when(i == 0)
        def _():
            for r in (dcw_ref, dgc_ref, dgq_ref, dgk_ref, dg1_ref, next_ref):
                r[...] = jnp.zeros_like(r)

        proj = proj_ref[...]
        hp = halo_ref[...]
        gate_b = proj[:, 0:CONV_WIDTH]
        gate_c = proj[:, CONV_WIDTH:2 * CONV_WIDTH]
        hc = proj[:, 2 * CONV_WIDTH:3 * CONV_WIDTH]
        a = gate_c * hc
        a_halo = jnp.where(first_tile, 0.0, hp[:, CONV_WIDTH:2 * CONV_WIDTH] * hp[:, 2 * CONV_WIDTH:3 * CONV_WIDTH])
        cw = _taps(cw_ref[...])
        cv, a2, a1 = _conv3(a, cw, a_halo)
        dyc, dgc = _rms_bwd(gate_b * cv, gc_ref[...], dycn_ref[...])
        dgc_ref[...] += dgc
        dcv = dyc * gate_b
        dcw_ref[...] += jnp.concatenate(
            [jnp.sum(dcv * a2, axis=0, keepdims=True), jnp.sum(dcv * a1, axis=0, keepdims=True),
             jnp.sum(dcv * a, axis=0, keepdims=True)], axis=0)
        da = _conv3_bwd_input(dcv, cw, next_ref[...])
        next_ref[...] = dcv[:SUBLANES]
        q0 = 3 * CONV_WIDTH
        k0 = q0 + ATTN_WIDTH
        dq, dgq = _head_norm_bwd(proj[:, q0:k0], gq_ref[...], dqn_ref[...], N_HEADS)
        dk, dgk = _head_norm_bwd(proj[:, k0:k0 + KV_WIDTH], gk_ref[...], dkn_ref[...], 2)
        dgq_ref[...] += dgq
        dgk_ref[...] += dgk
        dproj = jnp.concatenate([dyc * cv, da * hc, da * gate_c, dq, dk, dv_ref[...]], axis=-1).astype(BF16)
        dproj_ref[...] = dproj
        du1 = _dot(dproj, w_ref[...])
        xv = x_ref[...]
        dn, dg1 = _rms_bwd(xv, g1_ref[...], du1)
        dx_ref[...] = dh1_ref[...] + dn
        dg1_ref[...] += dg1

    rev = lambda i: n_t - 1 - i
    rows = lambda w: pl.BlockSpec((tm, w), lambda i: (rev(i), 0))
    const = lambda shape: pl.BlockSpec(shape, lambda i: (0,) * len(shape))
    halo = pl.BlockSpec((SUBLANES, IN_WIDTH), lambda i: (jnp.maximum(rev(i) * halo_blocks - 1, 0), 0))
    return pl.pallas_call(
        body, name="mix_in_bwd", grid=(n_t,),
        in_specs=[rows(D_MODEL), rows(D_MODEL), rows(IN_WIDTH), halo, rows(CONV_WIDTH), rows(ATTN_WIDTH), rows(KV_WIDTH),
                  rows(KV_WIDTH), const((IN_WIDTH, D_MODEL)), const((3, CONV_WIDTH)), const((1, D_MODEL)),
                  const((1, HEAD_DIM)), const((1, HEAD_DIM)), const((1, CONV_WIDTH))],
        out_specs=[rows(D_MODEL), rows(IN_WIDTH), const((3, CONV_WIDTH)), const((1, CONV_WIDTH)),
                   const((1, HEAD_DIM)), const((1, HEAD_DIM)), const((1, D_MODEL))],
        out_shape=[jax.ShapeDtypeStruct((SEQ, D_MODEL), F32), jax.ShapeDtypeStruct((SEQ, IN_WIDTH), BF16),
                   jax.ShapeDtypeStruct((3, CONV_WIDTH), F32),
                   jax.ShapeDtypeStruct((1, CONV_WIDTH), F32), jax.ShapeDtypeStruct((1, HEAD_DIM), F32),
                   jax.ShapeDtypeStruct((1, HEAD_DIM), F32), jax.ShapeDtypeStruct((1, D_MODEL), F32)],
        scratch_shapes=[pltpu.VMEM((SUBLANES, CONV_WIDTH), F32)],
        compiler_params=_params(("arbitrary",)),
    )(x, dh1, proj, proj, dycn, dqn, dkn, dv, w_in_t, conv_w, g1, gq, gk, gconv)


def _grad_w_in(dproj, u1, after=None):
    bw = 768

    def body(a_ref, b_ref, o_ref):
        o_ref[...] = _dot_tn(a_ref[...], b_ref[...]).astype(BF16)

    body, more_specs, more = _ordered_behind(body, 2, after)
    return pl.pallas_call(
        body, name="grad_w_in", grid=(IN_WIDTH // bw,),
        in_specs=[pl.BlockSpec((SEQ, bw), lambda k: (0, k)), pl.BlockSpec((SEQ, D_MODEL), lambda k: (0, 0))] + more_specs,
        out_specs=pl.BlockSpec((bw, D_MODEL), lambda k: (k, 0)),
        out_shape=jax.ShapeDtypeStruct((IN_WIDTH, D_MODEL), BF16),
        compiler_params=_params(("arbitrary",)),
    )(dproj, u1, *more)


def _adamw_math(w, g, m, v):
    m = ADAM_B1 * m + (1.0 - ADAM_B1) * g
    v = ADAM_B2 * v + (1.0 - ADAM_B2) * (g * g)
    m_hat = m / (1.0 - ADAM_B1 ** ADAM_STEP)
    v_hat = v / (1.0 - ADAM_B2 ** ADAM_STEP)
    return -ADAM_LR * (m_hat / (jnp.sqrt(v_hat) + ADAM_EPS) + ADAM_WD * w), m, v


_ROW_G1, _ROW_G2, _ROW_OUT_NORMS, _ROW_FFN_B, _ROW_GQ, _ROW_GK, _ROW_SINKS, _ROW_LOSS, _ROW_TABLE = 0, 1, 2, 3, 11, 12, 13, 14, 16
SMALL_ROWS, SMALL_COLS = 24, 1024
_SMALL_NAMES = ("norm_mix_g", "norm_ffn_g", "out_norm_conv_g", "out_norm_attn_g", "ffn_conv_b", "q_norm_g", "k_norm_g",
                "sinks", "rel_bias_table")


def _pack_small_grads(dg1, dg2, dgconv, dgattn, dfb, dgq, dgk, dsinks, dtbl_t, loss_acc):
    def body(dg1_ref, dg2_ref, dgc_ref, dga_ref, dfb_ref, dgq_ref, dgk_ref, ds_ref, dt_ref, loss_ref, o_ref):
        o_ref[...] = jnp.zeros_like(o_ref)
        o_ref[_ROW_G1:_ROW_G1 + 1, :] = dg1_ref[...]
        o_ref[_ROW_G2:_ROW_G2 + 1, :] = dg2_ref[...]
        o_ref[_ROW_OUT_NORMS:_ROW_OUT_NORMS + 1, 0:CONV_WIDTH] = dgc_ref[...]
        o_ref[_ROW_OUT_NORMS:_ROW_OUT_NORMS + 1, CONV_WIDTH:] = dga_ref[...]
        for k in range(N_DEV):
            o_ref[_ROW_FFN_B + k:_ROW_FFN_B + k + 1, 0:FFN_BLK] = dfb_ref[k // N_FFN_BLK, k % N_FFN_BLK]
        o_ref[_ROW_GQ:_ROW_GQ + 1, 0:HEAD_DIM] = dgq_ref[...]
        o_ref[_ROW_GK:_ROW_GK + 1, 0:HEAD_DIM] = dgk_ref[...]
        o_ref[_ROW_SINKS:_ROW_SINKS + 1, 0:128] = ds_ref[...]
        o_ref[_ROW_LOSS:_ROW_LOSS + 1, 0:128] = loss_ref[0:1, :]
        o_ref[_ROW_TABLE:_ROW_TABLE + N_HEADS, 0:128] = dt_ref[...]

    return pl.pallas_call(body, name="pack_small_grads", out_shape=jax.ShapeDtypeStruct((SMALL_ROWS, SMALL_COLS), F32))(
        dg1, dg2, dgconv, dgattn, dfb, dgq, dgk, dsinks, dtbl_t, loss_acc)


def _adamw_small(recv, params, after):
    names = _SMALL_NAMES
    n = len(names)

    def grad_of(g, name, k=None):
        if name == "norm_mix_g":
            return g[_ROW_G1:_ROW_G1 + 1, :]
        if name == "norm_ffn_g":
            return g[_ROW_G2:_ROW_G2 + 1, :]
        if name == "out_norm_conv_g":
            return g[_ROW_OUT_NORMS:_ROW_OUT_NORMS + 1, 0:CONV_WIDTH]
        if name == "out_norm_attn_g":
            return g[_ROW_OUT_NORMS:_ROW_OUT_NORMS + 1, CONV_WIDTH:]
        if name == "ffn_conv_b":
            return g[_ROW_FFN_B + k:_ROW_FFN_B + k + 1, 0:FFN_BLK]
        if name == "q_norm_g":
            return g[_ROW_GQ:_ROW_GQ + 1, 0:HEAD_DIM]
        if name == "k_norm_g":
            return g[_ROW_GK:_ROW_GK + 1, 0:HEAD_DIM]
        if name == "sinks":
            return g[_ROW_SINKS:_ROW_SINKS + 1, 0:N_HEADS]
        return g[_ROW_TABLE:_ROW_TABLE + N_HEADS, 0:NUM_BUCKETS]

    def body(r_ref, *refs):
        ins, outs, loss_ref = refs[:3 * n], refs[3 * n:7 * n], refs[7 * n]
        g = r_ref[0]
        for s in range(1, N_DEV):
            g = g + r_ref[s]
        loss_ref[...] = g[_ROW_LOSS:_ROW_LOSS + 1, 0:128]
        for i, name in enumerate(names):
            w_ref, m_ref, v_ref = ins[3 * i:3 * i + 3]
            o = outs[4 * i:4 * i + 4]
            cols = [slice(FFN_BLK * k, FFN_BLK * (k + 1)) for k in range(N_DEV)] if name == "ffn_conv_b" else [slice(None)]
            for k, cs in enumerate(cols):
                gk = grad_of(g, name, k)
                d, m2, v2 = _adamw_math(w_ref[:, cs], gk, m_ref[:, cs], v_ref[:, cs])
                o[0][:, cs], o[1][:, cs], o[2][:, cs], o[3][:, cs] = gk, d, m2, v2

    flat = [a for name in names for a in params[name]]
    body, more_specs, more = _ordered_behind(body, 1 + 3 * n, after)
    vmem = pl.BlockSpec(memory_space=pltpu.VMEM)
    out = pl.pallas_call(
        body, name="adamw_small",
        in_specs=[vmem] * (1 + 3 * n) + more_specs,
        out_shape=[jax.ShapeDtypeStruct(params[name][0].shape, F32) for name in names for _ in range(4)]
        + [jax.ShapeDtypeStruct((1, 128), F32)],
        compiler_params=pltpu.CompilerParams(vmem_limit_bytes=VMEM_LIMIT),
    )(recv, *flat, *more)
    return {name: tuple(out[4 * i:4 * i + 4]) for i, name in enumerate(names)}, out[4 * n]


def _adamw_direct(w, m, v, own, recv, me, name, row_blocks=1, after=None):
    rb = w.shape[0] // row_blocks
    cols = w.shape[1]

    def body(me_ref, w_ref, m_ref, v_ref, o_ref, r_ref, g_o, d_o, m_o, v_o):
        g = o_ref[...].astype(F32)
        for s in range(N_DEV - 1):
            g = g + r_ref[s].astype(F32)
        g_o[...] = g
        d_o[...], m_o[...], v_o[...] = _adamw_math(w_ref[...], g, m_ref[...], v_ref[...])

    blk = pl.BlockSpec((rb, cols), lambda i, me_ref: (i, 0))
    oblk = pl.BlockSpec((None, rb, cols), lambda i, me_ref: (me_ref[0], i, 0))
    rblk = pl.BlockSpec((N_DEV - 1, rb, cols), lambda i, me_ref: (0, i, 0))
    body, more_specs, more = _ordered_behind(body, 6, after)
    return pl.pallas_call(
        body, name=name,
        grid_spec=pltpu.PrefetchScalarGridSpec(num_scalar_prefetch=1, grid=(row_blocks,),
                                               in_specs=[blk, blk, blk, oblk, rblk] + more_specs, out_specs=[blk] * 4),
        out_shape=[jax.ShapeDtypeStruct(w.shape, F32)] * 4,
        compiler_params=_params(("arbitrary",)),
    )(me, w, m, v, own, recv, *more)


def _adamw(w, m, v, part, recv, chip, name, row_blocks=1, after=None):
    rb = w.shape[0] // row_blocks
    tail = w.shape[1:]
    zeros = (0,) * len(tail)

    def body(chip_ref, w_ref, m_ref, v_ref, p_ref, r_ref, g_o, d_o, m_o, v_o):
        g = p_ref[...].astype(F32)
        for s in range(3):
            g = g + r_ref[s].astype(F32)
        g_o[...] = g
        d_o[...], m_o[...], v_o[...] = _adamw_math(w_ref[...], g, m_ref[...], v_ref[...])

    blk = pl.BlockSpec((rb,) + tail, lambda i, chip_ref: (i,) + zeros)
    pblk = pl.BlockSpec((None, rb) + tail, lambda i, chip_ref: (chip_ref[0], i) + zeros)
    rblk = pl.BlockSpec((3, rb) + tail, lambda i, chip_ref: (0, i) + zeros)
    body, more_specs, more = _ordered_behind(body, 6, after)
    return pl.pallas_call(
        body, name=name,
        grid_spec=pltpu.PrefetchScalarGridSpec(num_scalar_prefetch=1, grid=(row_blocks,),
                                               in_specs=[blk, blk, blk, pblk, rblk] + more_specs, out_specs=[blk] * 4),
        out_shape=[jax.ShapeDtypeStruct(w.shape, F32)] * 4,
        compiler_params=_params(("arbitrary",)),
    )(chip, w, m, v, part, recv, *more)


def kernel(x, norm_mix_g, w_in, conv_w, q_norm_g, k_norm_g, rel_bias_table, sinks, out_norm_conv_g, out_norm_attn_g, w_out, norm_ffn_g, w_up, ffn_conv_w, ffn_conv_b, w_down, loss_target, m_norm_mix_g, m_w_in, m_conv_w, m_q_norm_g, m_k_norm_g, m_rel_bias_table, m_sinks, m_out_norm_conv_g, m_out_norm_attn_g, m_w_out, m_norm_ffn_g, m_w_up, m_ffn_conv_w, m_ffn_conv_b, m_w_down, v_norm_mix_g, v_w_in, v_conv_w, v_q_norm_g, v_k_norm_g, v_rel_bias_table, v_sinks, v_out_norm_conv_g, v_out_norm_attn_g, v_w_out, v_norm_ffn_g, v_w_up, v_ffn_conv_w, v_ffn_conv_b, v_w_down):
    p = dict(norm_mix_g=norm_mix_g, w_in=w_in, conv_w=conv_w, q_norm_g=q_norm_g, k_norm_g=k_norm_g,
             rel_bias_table=rel_bias_table, sinks=sinks, out_norm_conv_g=out_norm_conv_g, out_norm_attn_g=out_norm_attn_g,
             w_out=w_out, norm_ffn_g=norm_ffn_g, w_up=w_up, ffn_conv_w=ffn_conv_w, ffn_conv_b=ffn_conv_b, w_down=w_down)
    m = dict(norm_mix_g=m_norm_mix_g, w_in=m_w_in, conv_w=m_conv_w, q_norm_g=m_q_norm_g, k_norm_g=m_k_norm_g,
             rel_bias_table=m_rel_bias_table, sinks=m_sinks, out_norm_conv_g=m_out_norm_conv_g,
             out_norm_attn_g=m_out_norm_attn_g, w_out=m_w_out, norm_ffn_g=m_norm_ffn_g, w_up=m_w_up,
             ffn_conv_w=m_ffn_conv_w, ffn_conv_b=m_ffn_conv_b, w_down=m_w_down)
    v = dict(norm_mix_g=v_norm_mix_g, w_in=v_w_in, conv_w=v_conv_w, q_norm_g=v_q_norm_g, k_norm_g=v_k_norm_g,
             rel_bias_table=v_rel_bias_table, sinks=v_sinks, out_norm_conv_g=v_out_norm_conv_g,
             out_norm_attn_g=v_out_norm_attn_g, w_out=v_w_out, norm_ffn_g=v_norm_ffn_g, w_up=v_w_up,
             ffn_conv_w=v_ffn_conv_w, ffn_conv_b=v_ffn_conv_b, w_down=v_w_down)

    xs, tgt = x[0], loss_target[0]
    g1, g2, gq, gk, gconv, gattn = norm_mix_g, norm_ffn_g, q_norm_g, k_norm_g, out_norm_conv_g, out_norm_attn_g
    ix, iy, ic = _coords()
    core = ic.astype(jnp.int32).reshape(1)
    chip = (2 * ix + iy).astype(jnp.int32).reshape(1)
    me = _lin(ix, iy, ic).astype(jnp.int32).reshape(1)
    bkt = jnp.asarray(_bucket_map())
    tr = lambda a: a[0].T
    taps = lambda a: jnp.transpose(a, (1, 0, 2))
    tbl_t = rel_bias_table.T

    wi_l, cw_l = _place_shards(me, [tr(w_in), taps(conv_w)], [BF16, F32], "place_mixer_shards")
    finish_a, token_a = _all_gather_split([wi_l, cw_l], "mixer", None)
    wo_l, wu_l, wd_l, fcw_l = _place_shards(me, [w_out[0], tr(w_up), w_down[0], taps(ffn_conv_w)],
                                            [BF16, BF16, BF16, F32], "place_ffn_shards", after=token_a)
    ffn_stage2, ffn_stage3, token_b = _all_gather_tree([wo_l, wu_l, wd_l, fcw_l], "ffn", token_a)
    wi_g, cw_g = finish_a(token_b)
    w_in_t = wi_g.reshape(IN_WIDTH, D_MODEL)
    conv_w_f = jnp.transpose(cw_g[:, :, 0, :], (1, 0, 2)).reshape(3, CONV_WIDTH)

    proj, u1, ycn, qn, kn, vv = _mix_in_fwd(xs, g1, w_in_t, conv_w_f, gq, gk, gconv)
    token_b2 = ffn_stage2(ycn)
    y_attn, yan, probs, psinks = _attn_fwd(qn, kn, vv, tbl_t, sinks, bkt, gattn, after=token_b2)
    wo_g, wu_g, wd_g, fcw_g = ffn_stage3(yan)
    w_out_f = wo_g.reshape(D_MODEL, D_MODEL)
    w_down_f = wd_g.reshape(N_FFN_BLK, FFN_BLK, D_MODEL)
    w_up_f = wu_g.reshape(2, N_FFN_BLK, FFN_BLK, D_MODEL)
    fcw_f = fcw_g.reshape(2, N_FFN_BLK, 3, 1, FFN_BLK)
    fcb = ffn_conv_b.reshape(2, N_FFN_BLK, 1, FFN_BLK)
    h1, u2, up, pre, act, dh2, dh2b, loss_acc = _ffn_fwd_token_major(
        xs, ycn, yan, w_out_f, g2, w_up_f, fcw_f, fcb, w_down_f, tgt)

    dw_down = _grad_tn_blocked(act, dh2b, "grad_w_down", a_is_blocked=True).reshape(N_DEV, D_FF // N_DEV, D_MODEL)
    plan_d, slots_d = _scatter_plan(1)
    d_sem = _split_start("scatter_w_down_start", [dw_down], [lax.empty((N_DEV - 1,) + dw_down.shape[1:], BF16)],
                         plan_d, None, _ALL_FOR_W_DOWN)
    dup, dh1, dh1b, dfb, dfcw, dg2 = _ffn_bwd_token_major(dh2, dh2b, h1, g2, up, pre, w_up_f, fcw_f, w_down_f,
                                                          after=d_sem[4])
    dw_up = _grad_tn_blocked(dup.reshape(N_DEV, SEQ, FFN_BLK), u2, "grad_w_up", a_is_blocked=True)
    dw_out = _grad_tn([ycn, yan], dh1b, CONV_WIDTH, "grad_w_out").reshape(N_DEV, D_MODEL // N_DEV, D_MODEL)
    out_bwd = {}

    def behind_ffn(token):
        out_bwd["r"] = _out_bwd(dh1b, w_out_f, y_attn, gattn, after=token)
        return out_bwd["r"][0]

    finish_ffn, token_ffn = _reduce_scatter_split(
        [dw_up, dw_out, dfcw.reshape(N_DEV, 3, 1, FFN_BLK)], "ffn", core, behind_ffn)
    dycn, dy_attn, dgattn = out_bwd["r"]
    dqn, dkn, dv, dtbl_t, dsinks = _attn_bwd(qn, kn, vv, dy_attn, probs, psinks, bkt, after=token_ffn)
    dx, dproj, dcw, dgconv, dgq, dgk, dg1 = _mix_in_bwd(xs, dh1, proj, dycn, dqn, dkn, dv, w_in_t, conv_w_f,
                                                             g1, gq, gk, gconv)
    packed = _pack_small_grads(dg1, dg2, dgconv, dgattn, dfb, dgq, dgk, dsinks, dtbl_t, loss_acc)
    plan_s, slots_s = _broadcast_plan()
    s_sem, r_sem, src_s, land_s, token_s = _split_start(
        "gather_small_start", [packed], [jnp.broadcast_to(packed[None], (N_DEV,) + packed.shape)], plan_s, None,
        _ALL_FOR_SMALL)
    dw_in_t = _grad_w_in(dproj, u1, after=token_s).reshape(N_DEV, IN_WIDTH // N_DEV, D_MODEL)
    dcw_b = jnp.transpose(dcw.reshape(3, N_DEV, 1, CONV_WIDTH // N_DEV), (1, 0, 2, 3))
    adam = {}
    ffn_got = {}

    def behind_mixer(token):
        ffn_got["r"] = finish_ffn(token)
        return ffn_got["r"][1][0]

    finish_mixer, token_mixer = _reduce_scatter_split([dw_in_t, dcw_b], "mixer", core, behind_mixer)
    (p_wu, p_wo, p_fcw), (r_wu, r_wo, r_fcw) = ffn_got["r"]
    (own_wd,), (r_wd,) = _split_wait("scatter_w_down_wait", d_sem[0], d_sem[1], d_sem[2], d_sem[3], plan_d, slots_d,
                                     token_mixer)
    adam["w_down"] = _adamw_direct(w_down[0], m_w_down[0], v_w_down[0], own_wd, r_wd, me, "adamw_w_down", row_blocks=2)
    adam_up = _adamw(tr(w_up), tr(m_w_up), tr(v_w_up), p_wu, r_wu, chip, "adamw_w_up", row_blocks=4,
                     after=adam["w_down"][0])
    adam["w_out"] = _adamw(w_out[0], m_w_out[0], v_w_out[0], p_wo, r_wo, chip, "adamw_w_out", after=adam_up[0])
    adam_fcw = _adamw(taps(ffn_conv_w), taps(m_ffn_conv_w), taps(v_ffn_conv_w), p_fcw, r_fcw, chip, "adamw_ffn_conv_w",
                      after=adam["w_out"][0])
    _, (r_small,) = _split_wait("gather_small_wait", s_sem, r_sem, src_s, land_s, plan_s, slots_s, adam_fcw[0])
    small_in = {k: (p[k], m[k], v[k]) for k in _SMALL_NAMES}
    small_in["rel_bias_table"] = (tbl_t, m_rel_bias_table.T, v_rel_bias_table.T)
    small_out, loss_row = _adamw_small(r_small, small_in, None)
    (p_wi, p_cw), (r_wi, r_cw) = finish_mixer(loss_row)
    adam_in = _adamw(tr(w_in), tr(m_w_in), tr(v_w_in), p_wi, r_wi, chip, "adamw_w_in")
    adam_cw = _adamw(taps(conv_w), taps(m_conv_w), taps(v_conv_w), p_cw, r_cw, chip, "adamw_conv_w")

    res = {k: tuple(a[None] for a in t) for k, t in adam.items()}
    res["w_up"] = tuple(a.T[None] for a in adam_up)
    res["w_in"] = tuple(a.T[None] for a in adam_in)
    res["ffn_conv_w"] = tuple(taps(a) for a in adam_fcw)
    res["conv_w"] = tuple(taps(a) for a in adam_cw)
    res.update(small_out)
    res["rel_bias_table"] = tuple(a.T for a in small_out["rel_bias_table"])
    loss = loss_row[0, 0]
    order = ("norm_mix_g", "w_in", "conv_w", "q_norm_g", "k_norm_g", "rel_bias_table", "sinks", "out_norm_conv_g",
             "out_norm_attn_g", "w_out", "norm_ffn_g", "w_up", "ffn_conv_w", "ffn_conv_b", "w_down")
    return (loss, dx[None], *[res[k][0] for k in order], *[res[k][1] for k in order],
            *[res[k][2] for k in order], *[res[k][3] for k in order])
```

```python
import functools
import math

import numpy as np
import jax
import jax.numpy as jnp
from jax import lax
from jax.experimental import pallas as pl
from jax.experimental.pallas import tpu as pltpu

F32 = jnp.float32
BF16 = jnp.bfloat16

SEQ = 2048
D_MODEL = 1024
CONV_WIDTH = 512
ATTN_WIDTH = 512
KV_WIDTH = 128
HEAD_DIM = 64
N_HEADS = 8
GQA_GROUP = 4
IN_WIDTH = 2304
D_FF = 2816
BLK = 128
NUM_BUCKETS = 32
EPS = 1e-6
NEG_INF = -1e30
ADAM_LR = 0.001
ADAM_B1 = 0.9
ADAM_B2 = 0.999
ADAM_EPS = 1e-08
ADAM_WD = 0.01
ADAM_STEP = 10

N_DEV = 8
FFN_BLK = 2 * D_FF // N_DEV
N_FFN_BLK = D_FF // FFN_BLK
SUBLANES = 8
VMEM_LIMIT = 56 * 1024 * 1024

_MESH = pl.DeviceIdType.MESH
_ANY = pl.BlockSpec(memory_space=pl.ANY)


def _params(sem):
    return pltpu.CompilerParams(dimension_semantics=sem, vmem_limit_bytes=VMEM_LIMIT)


def _ordered_behind(body, pos, after):
    if after is None:
        return body, [], []
    return (lambda *refs: body(*refs[:pos], *refs[pos + 1:])), [_ANY], [after]


def _dot(a, b):
    return jnp.dot(a, b, preferred_element_type=F32)


def _dot_nt(a, b):
    return lax.dot_general(a, b, (((1,), (1,)), ((), ())), preferred_element_type=F32)


def _dot_tn(a, b):
    return lax.dot_general(a, b, (((0,), (0,)), ((), ())), preferred_element_type=F32)


def _shift_down(x, s, halo):
    r = pltpu.roll(x, s, axis=0)
    hr = pltpu.roll(halo, s, axis=0)
    row = lax.broadcasted_iota(jnp.int32, halo.shape, 0)
    top = jnp.where(row < s, hr, r[:SUBLANES])
    return jnp.concatenate([top, r[SUBLANES:]], axis=0)


def _shift_up(x, s, halo):
    n = x.shape[0]
    r = pltpu.roll(x, n - s, axis=0)
    hr = pltpu.roll(halo, SUBLANES - s, axis=0)
    row = lax.broadcasted_iota(jnp.int32, halo.shape, 0)
    bot = jnp.where(row >= SUBLANES - s, hr, r[n - SUBLANES:])
    return jnp.concatenate([r[:n - SUBLANES], bot], axis=0)


def _taps(w):
    return (w[0], w[1], w[2]) if len(w.shape) == 3 else (w[0:1], w[1:2], w[2:3])


def _conv3(x, w, halo):
    x2 = _shift_down(x, 2, halo)
    x1 = _shift_down(x, 1, halo)
    return x2 * w[0] + x1 * w[1] + x * w[2], x2, x1


def _conv3_bwd_input(dy, w, halo_next):
    return dy * w[2] + _shift_up(dy, 1, halo_next) * w[1] + _shift_up(dy, 2, halo_next) * w[0]


def _rstd(x):
    return lax.rsqrt(jnp.mean(x * x, axis=-1, keepdims=True) + EPS)


def _rms_bwd(x, g, dy):
    r = _rstd(x)
    n = x * r
    dn = dy * g
    dx = r * (dn - n * jnp.mean(dn * n, axis=-1, keepdims=True))
    return dx, jnp.sum(dy * n, axis=0, keepdims=True)


def _head_mean(x):
    width = x.shape[-1]
    ri = lax.broadcasted_iota(jnp.int32, (width, width), 0) // HEAD_DIM
    ci = lax.broadcasted_iota(jnp.int32, (width, width), 1) // HEAD_DIM
    ones = jnp.where(ri == ci, 1.0, 0.0).astype(BF16)
    hi = x.astype(BF16)
    lo = (x - hi.astype(F32)).astype(BF16)
    return (_dot(hi, ones) + _dot(lo, ones)) * (1.0 / HEAD_DIM)


def _head_norm(x, g, heads):
    return x * lax.rsqrt(_head_mean(x * x) + EPS) * jnp.tile(g, (1, heads))


def _head_norm_bwd(x, g, dy, heads):
    r = lax.rsqrt(_head_mean(x * x) + EPS)
    n = x * r
    dn = dy * jnp.tile(g, (1, heads))
    dx = r * (dn - n * _head_mean(dn * n))
    per_lane = jnp.sum(dy * n, axis=0, keepdims=True)
    dg = per_lane[:, 0:HEAD_DIM]
    for h in range(1, heads):
        dg = dg + per_lane[:, HEAD_DIM * h:HEAD_DIM * (h + 1)]
    return dx, dg


def _bucket_map():
    q = np.arange(BLK)[:, None]
    j = np.arange(BLK)[None, :]
    n = np.where(j > q, q + BLK - j, q - j)
    nf = np.maximum(n, 1).astype(np.float32)
    max_exact = NUM_BUCKETS // 2
    large = max_exact + (np.log(nf / max_exact) / math.log(BLK / max_exact) * (NUM_BUCKETS - max_exact)).astype(np.int32)
    large = np.minimum(large, NUM_BUCKETS - 1)
    return np.where(n < max_exact, n, large).astype(np.int32)


def _coords():
    return lax.axis_index("x"), lax.axis_index("y"), lax.axis_index("c")


def _lin(px, py, pc):
    return 4 * px + 2 * py + pc


_HBM = pl.BlockSpec(memory_space=pltpu.HBM)
_SEM = pl.BlockSpec(memory_space=pltpu.SEMAPHORE)
_EFFECT = pltpu.SideEffectType.DATAFLOW_SIDE_EFFECTING


def _in_hbm(a):
    return pltpu.with_memory_space_constraint(a, pltpu.HBM)


_SIBLING = (1, lambda x, y, c: [(x, y, 1 - c)])
_SIBLING_AND_CHIPS = (2, lambda x, y, c: [(x, y, 1 - c)] + [(cx, cy, c) for cx, cy in _chips(x, y)])
_SIBLING_AND_NEIGHBOURS = (3, lambda x, y, c: [(x, y, 1 - c), (1 - x, y, c), (x, 1 - y, c)])
_ONWARD_AND_SIBLING = (4, lambda x, y, c: [(jnp.where(c == 1, x, 1 - x), jnp.where(c == 1, 1 - y, y), c), (x, y, 1 - c)])
_ALL_FOR_W_DOWN = (5, lambda x, y, c: _peers(x, y, c))
_CHIPS = (6, lambda x, y, c: [(cx, cy, c) for cx, cy in _chips(x, y)])
_ALL_FOR_SMALL = (7, lambda x, y, c: _peers(x, y, c))


def _split_start(name, srcs, lands, plan, after, handshake):
    ns, nl = len(srcs), len(lands)
    n_copies = len(plan(0, 0, 0))
    n_after = 0 if after is None else 1
    collective_id, peers_of = handshake

    def body(*refs):
        src_refs, land_refs = refs[:ns + nl], refs[ns:ns + nl]
        send_sems, recv_sems = refs[ns + nl + n_after], refs[ns + nl + n_after + 1]
        token = refs[-1]
        barrier = pltpu.get_barrier_semaphore()
        peers = peers_of(*_coords())
        for peer in peers:
            pl.semaphore_signal(barrier, inc=1, device_id=peer, device_id_type=_MESH)
        pl.semaphore_wait(barrier, len(peers))
        for k, (a, s_slot, l, d_slot, dev) in enumerate(plan(*_coords())):
            src = src_refs[a] if s_slot is None else src_refs[a].at[s_slot]
            pltpu.make_async_remote_copy(src_ref=src, dst_ref=land_refs[l].at[d_slot], send_sem=send_sems.at[k],
                                         recv_sem=recv_sems.at[k], device_id=dev, device_id_type=_MESH).start()
        token[...] = jnp.zeros_like(token)

    arrs = list(srcs) + list(lands)
    out = pl.pallas_call(
        body, name=name,
        out_shape=(pltpu.SemaphoreType.DMA((n_copies,)), pltpu.SemaphoreType.DMA((n_copies,)),
                   *[pltpu.HBM(a.shape, a.dtype) for a in arrs], jax.ShapeDtypeStruct((SUBLANES, 128), F32)),
        in_specs=[_HBM] * (ns + nl) + [_ANY] * n_after,
        out_specs=(_SEM, _SEM, *[_HBM] * (ns + nl), pl.BlockSpec(memory_space=pltpu.VMEM)),
        input_output_aliases={i: 2 + i for i in range(ns + nl)},
        compiler_params=pltpu.CompilerParams(has_side_effects=_EFFECT, collective_id=collective_id),
    )(*[_in_hbm(a) for a in arrs], *([] if after is None else [after]))
    return out[0], out[1], list(out[2:2 + ns]), list(out[2 + ns:2 + ns + nl]), out[-1]


def _split_wait(name, send_sems, recv_sems, srcs, lands, plan, recv_slots, after):
    ns, nl = len(srcs), len(lands)

    def body(*refs):
        src_refs, land_refs = refs[:ns + nl], refs[ns:ns + nl]
        send_sems, recv_sems = refs[ns + nl], refs[ns + nl + 1]
        coords = _coords()
        slots = recv_slots(*coords)
        for k, (a, s_slot, l, _, dev) in enumerate(plan(*coords)):
            src = src_refs[a] if s_slot is None else src_refs[a].at[s_slot]
            cp = pltpu.make_async_remote_copy(src_ref=src, dst_ref=land_refs[l].at[slots[k]], send_sem=send_sems.at[k],
                                              recv_sem=recv_sems.at[k], device_id=dev, device_id_type=_MESH)
            cp.wait_send()
            cp.wait_recv()

    arrs = list(srcs) + list(lands)
    out = pl.pallas_call(
        body, name=name,
        out_shape=tuple(pltpu.HBM(a.shape, a.dtype) for a in arrs),
        in_specs=[_HBM] * (ns + nl) + [_SEM, _SEM, _ANY],
        out_specs=tuple([_HBM] * (ns + nl)),
        input_output_aliases={i: i for i in range(ns + nl)},
        compiler_params=pltpu.CompilerParams(has_side_effects=_EFFECT),
    )(*arrs, send_sems, recv_sems, after)
    return list(out[:ns]), list(out[ns:])


def _chips(x, y):
    return [(1 - x, y), (x, 1 - y), (1 - x, 1 - y)]


def _gather_plan_ici(n):
    def plan(x, y, c):
        me = _lin(x, y, c)
        out = []
        for a in range(n):
            out.append((a, me, a, me, (x, y, 1 - c)))
            out += [(a, me, a, me, (cx, cy, c)) for cx, cy in _chips(x, y)]
        return out

    def recv_slots(x, y, c):
        out = []
        for _ in range(n):
            out.append(_lin(x, y, 1 - c))
            out += [_lin(cx, cy, c) for cx, cy in _chips(x, y)]
        return out

    return plan, recv_slots


def _gather_plan_d2d(n):
    def plan(x, y, c):
        return [(a, _lin(cx, cy, c), a, _lin(cx, cy, c), (x, y, 1 - c)) for a in range(n) for cx, cy in _chips(x, y)]

    def recv_slots(x, y, c):
        return [_lin(cx, cy, 1 - c) for _ in range(n) for cx, cy in _chips(x, y)]

    return plan, recv_slots


def _all_gather_split(lands, tag, after):
    n = len(lands)
    plan1, slots1 = _gather_plan_ici(n)
    s1, r1, _, lands, token = _split_start(f"gather_{tag}_ici_start", [], lands, plan1, after, _SIBLING_AND_CHIPS)

    def finish(after):
        _, got = _split_wait(f"gather_{tag}_ici_wait", s1, r1, [], lands, plan1, slots1, after)
        plan2, slots2 = _gather_plan_d2d(n)
        s2, r2, _, got, token2 = _split_start(f"gather_{tag}_d2d_start", [], got, plan2, None, _SIBLING)
        return _split_wait(f"gather_{tag}_d2d_wait", s2, r2, [], got, plan2, slots2, token2)[1]

    return finish, token


def _all_gather_tree(lands, tag, after):
    n = len(lands)

    def plan1(x, y, c):
        me = _lin(x, y, c)
        return [(a, me, a, me, dev) for a in range(n) for dev in ((x, y, 1 - c), (1 - x, y, c), (x, 1 - y, c))]

    def slots1(x, y, c):
        return [s for _ in range(n) for s in (_lin(x, y, 1 - c), _lin(1 - x, y, c), _lin(x, 1 - y, c))]

    def plan2(x, y, c):
        from_x, from_y = _lin(1 - x, y, c), _lin(x, 1 - y, c)
        north = c == 1
        passed = jnp.where(north, from_x, from_y)
        onward = (jnp.where(north, x, 1 - x), jnp.where(north, 1 - y, y), c)
        sib = (x, y, 1 - c)
        return [cp for a in range(n) for cp in ((a, passed, a, passed, onward), (a, from_x, a, from_x, sib),
                                                (a, from_y, a, from_y, sib))]

    def slots2(x, y, c):
        return [s for _ in range(n) for s in (_lin(1 - x, 1 - y, c), _lin(1 - x, y, 1 - c), _lin(x, 1 - y, 1 - c))]

    def plan3(x, y, c):
        diag = _lin(1 - x, 1 - y, c)
        return [(a, diag, a, diag, (x, y, 1 - c)) for a in range(n)]

    def slots3(x, y, c):
        return [_lin(1 - x, 1 - y, 1 - c)] * n

    s1, r1, _, lands, token = _split_start(f"gather_{tag}_1_start", [], lands, plan1, after, _SIBLING_AND_NEIGHBOURS)
    state = {}

    def stage2(after):
        _, got = _split_wait(f"gather_{tag}_1_wait", s1, r1, [], lands, plan1, slots1, after)
        state["s"], state["r"], _, state["lands"], token2 = _split_start(f"gather_{tag}_2_start", [], got, plan2, None,
                                                                         _ONWARD_AND_SIBLING)
        return token2

    def stage3(after):
        _, got = _split_wait(f"gather_{tag}_2_wait", state["s"], state["r"], [], state["lands"], plan2, slots2, after)
        s3, r3, _, got, token3 = _split_start(f"gather_{tag}_3_start", [], got, plan3, None, _SIBLING)
        return _split_wait(f"gather_{tag}_3_wait", s3, r3, [], got, plan3, slots3, token3)[1]

    return stage2, stage3, token


_CHIP_LIST = ((0, 0), (0, 1), (1, 0), (1, 1))


def _reduce_plan_d2d(n):
    def plan(x, y, c):
        return [(a, _lin(qx, qy, 1 - c), a, q, (x, y, 1 - c)) for a in range(n) for q, (qx, qy) in enumerate(_CHIP_LIST)]

    def recv_slots(x, y, c):
        return [q for _ in range(n) for q in range(4)]

    return plan, recv_slots


def _reduce_plan_ici(n):
    def plan(x, y, c):
        return [(a, 2 * cx + cy, a, j, (cx, cy, c)) for a in range(n) for j, (cx, cy) in enumerate(_chips(x, y))]

    def recv_slots(x, y, c):
        return [j for _ in range(n) for j in range(3)]

    return plan, recv_slots


def _peers(x, y, c):
    return [(1 - x if r & 4 else x, 1 - y if r & 2 else y, 1 - c if r & 1 else c) for r in range(1, N_DEV)]


def _scatter_plan(n):
    def plan(x, y, c):
        return [(a, _lin(*peer), a, r, peer) for a in range(n) for r, peer in enumerate(_peers(x, y, c))]

    def recv_slots(x, y, c):
        return [r for _ in range(n) for r in range(N_DEV - 1)]

    return plan, recv_slots


def _broadcast_plan():
    peers = _peers

    def plan(x, y, c):
        return [(0, None, 0, _lin(x, y, c), peer) for peer in peers(x, y, c)]

    def recv_slots(x, y, c):
        return [_lin(*peer) for peer in peers(x, y, c)]

    return plan, recv_slots


def _chip_partial(grads, recvd, core, name):
    n = len(grads)

    def body(c_ref, *refs):
        for a in range(n):
            g_ref, r_ref, o_ref = refs[a], refs[n + a], refs[2 * n + a]
            o_ref[...] = (g_ref[...].astype(F32) + r_ref[...].astype(F32)).astype(o_ref.dtype)

    def blk(a, own):
        zeros = (0,) * (a.ndim - 1)
        return pl.BlockSpec((None,) + a.shape[1:],
                            (lambda q, c_ref: (2 * q + c_ref[0],) + zeros) if own else (lambda q, c_ref: (q,) + zeros))

    return pl.pallas_call(
        body, name=name,
        grid_spec=pltpu.PrefetchScalarGridSpec(
            num_scalar_prefetch=1, grid=(4,),
            in_specs=[blk(a, True) for a in grads] + [blk(a, False) for a in recvd],
            out_specs=[blk(a, False) for a in recvd]),
        out_shape=[jax.ShapeDtypeStruct(a.shape, a.dtype) for a in recvd],
        compiler_params=_params(("arbitrary",)),
    )(core, *grads, *recvd)


def _reduce_scatter_split(grads, tag, core, behind):
    n = len(grads)
    plan1, slots1 = _reduce_plan_d2d(n)
    lands1 = [lax.empty((4,) + a.shape[1:], a.dtype) for a in grads]
    s1, r1, srcs1, lands1, token1 = _split_start(f"reduce_{tag}_d2d_start", grads, lands1, plan1, None, _SIBLING)
    own, got = _split_wait(f"reduce_{tag}_d2d_wait", s1, r1, srcs1, lands1, plan1, slots1, behind(token1))
    parts = _chip_partial(own, got, core, f"reduce_{tag}_partial")
    plan2, slots2 = _reduce_plan_ici(n)
    lands2 = [lax.empty((3,) + a.shape[1:], a.dtype) for a in grads]
    s2, r2, srcs2, lands2, token2 = _split_start(f"reduce_{tag}_ici_start", parts, lands2, plan2, None, _CHIPS)

    def finish(after):
        return _split_wait(f"reduce_{tag}_ici_wait", s2, r2, srcs2, lands2, plan2, slots2, after)

    return finish, token2


def _place_shards(me, shards, dtypes, name, after=None):
    n = len(shards)

    def body(me_ref, *refs):
        for a in range(n):
            refs[n + a][...] = refs[a][...].astype(dtypes[a])

    full = lambda s: pl.BlockSpec(s.shape, lambda i, me_ref: (0,) * s.ndim)
    slot = lambda s: pl.BlockSpec((None,) + s.shape, lambda i, me_ref: (me_ref[0],) + (0,) * s.ndim)
    body, more_specs, more = _ordered_behind(body, 1 + n, after)
    return pl.pallas_call(
        body, name=name,
        grid_spec=pltpu.PrefetchScalarGridSpec(num_scalar_prefetch=1, grid=(1,),
                                               in_specs=[full(s) for s in shards] + more_specs,
                                               out_specs=[slot(s) for s in shards]),
        out_shape=[jax.ShapeDtypeStruct((N_DEV,) + s.shape, d) for s, d in zip(shards, dtypes)],
        compiler_params=_params(("arbitrary",)),
    )(me, *shards, *more)


def _mix_in_fwd(x, g1, w_in_t, conv_w, gq, gk, gconv):
    tm = 512
    n_t = SEQ // tm

    def body(x_ref, g1_ref, w_ref, cw_ref, gq_ref, gk_ref, gc_ref,
             proj_ref, u1_ref, ycn_ref, qn_ref, kn_ref, v_ref, halo_ref):
        @pl.when(pl.program_id(0) == 0)
        def _():
            halo_ref[...] = jnp.zeros_like(halo_ref)

        xv = x_ref[...]
        u = (xv * _rstd(xv) * g1_ref[...]).astype(BF16)
        u1_ref[...] = u
        proj = _dot_nt(u, w_ref[...])
        proj_ref[...] = proj
        gate_b = proj[:, 0:CONV_WIDTH]
        a = proj[:, CONV_WIDTH:2 * CONV_WIDTH] * proj[:, 2 * CONV_WIDTH:3 * CONV_WIDTH]
        cv, _, _ = _conv3(a, _taps(cw_ref[...]), halo_ref[...])
        halo_ref[...] = a[tm - SUBLANES:]
        yc = gate_b * cv
        ycn_ref[...] = (yc * _rstd(yc) * gc_ref[...]).astype(BF16)
        q0 = 3 * CONV_WIDTH
        qn_ref[...] = _head_norm(proj[:, q0:q0 + ATTN_WIDTH], gq_ref[...], N_HEADS).astype(BF16)
        k0 = q0 + ATTN_WIDTH
        kn_ref[...] = _head_norm(proj[:, k0:k0 + KV_WIDTH], gk_ref[...], 2).astype(BF16)
        v_ref[...] = proj[:, k0 + KV_WIDTH:k0 + 2 * KV_WIDTH].astype(BF16)

    const = lambda shape: pl.BlockSpec(shape, lambda i: (0,) * len(shape))
    rows = lambda w: pl.BlockSpec((tm, w), lambda i: (i, 0))
    return pl.pallas_call(
        body, name="mix_in_fwd", grid=(n_t,),
        in_specs=[rows(D_MODEL), const((1, D_MODEL)), const((IN_WIDTH, D_MODEL)), const((3, CONV_WIDTH)),
                  const((1, HEAD_DIM)), const((1, HEAD_DIM)), const((1, CONV_WIDTH))],
        out_specs=[rows(IN_WIDTH), rows(D_MODEL), rows(CONV_WIDTH), rows(ATTN_WIDTH), rows(KV_WIDTH), rows(KV_WIDTH)],
        out_shape=[jax.ShapeDtypeStruct((SEQ, IN_WIDTH), F32), jax.ShapeDtypeStruct((SEQ, D_MODEL), BF16),
                   jax.ShapeDtypeStruct((SEQ, CONV_WIDTH), BF16),
                   jax.ShapeDtypeStruct((SEQ, ATTN_WIDTH), BF16), jax.ShapeDtypeStruct((SEQ, KV_WIDTH), BF16),
                   jax.ShapeDtypeStruct((SEQ, KV_WIDTH), BF16)],
        scratch_shapes=[pltpu.VMEM((SUBLANES, CONV_WIDTH), F32)],
        compiler_params=_params(("arbitrary",)),
    )(x, g1, w_in_t, conv_w, gq, gk, gconv)


GROUP_ROWS = GQA_GROUP * BLK


def _band_bias(tbl_ref, bkt, bias_ref):
    for h in range(N_HEADS):
        acc = jnp.zeros(bkt.shape, F32)
        for b in range(NUM_BUCKETS):
            acc = jnp.where(bkt == b, tbl_ref[h, b], acc)
        bias_ref[h // GQA_GROUP, BLK * (h % GQA_GROUP):BLK * (h % GQA_GROUP + 1), :] = acc


def _band_masks(i):
    qi = lax.broadcasted_iota(jnp.int32, (GROUP_ROWS, BLK), 0) & (BLK - 1)
    ji = lax.broadcasted_iota(jnp.int32, (GROUP_ROWS, BLK), 1)
    upper = ji > qi
    return upper, upper & (i == 0)


def _stack_heads(x, g):
    return jnp.concatenate([x[:, HEAD_DIM * h:HEAD_DIM * (h + 1)] for h in range(GQA_GROUP * g, GQA_GROUP * (g + 1))], axis=0)


def _unstack_heads(groups):
    return jnp.concatenate([p[BLK * t:BLK * (t + 1)] for p in groups for t in range(GQA_GROUP)], axis=-1)


def _per_head_rows(vals):
    row = lax.broadcasted_iota(jnp.int32, (GROUP_ROWS, 1), 0)
    col = jnp.full((GROUP_ROWS, 1), vals[GQA_GROUP - 1], F32)
    for t in range(GQA_GROUP - 2, -1, -1):
        col = jnp.where(row < BLK * (t + 1), vals[t], col)
    return col


def _band_rows(ref, i):
    prev = pl.multiple_of(jnp.maximum(i - 1, 0) * BLK, BLK)
    cur = pl.multiple_of(i * BLK, BLK)
    return jnp.concatenate([ref[pl.ds(prev, BLK), :], ref[pl.ds(cur, BLK), :]], axis=0), prev, cur


def _fold(band, upper):
    return jnp.where(upper, band[:, :BLK], band[:, BLK:])


def _unfold(tile, upper):
    return jnp.concatenate([jnp.where(upper, tile, 0.0), jnp.where(upper, 0.0, tile)], axis=1)


def _head_probs(qh, kh, bias, upper, dead, sink):
    logits = _fold(_dot_nt(qh, kh), upper) * (HEAD_DIM ** -0.5) + bias
    logits = jnp.where(dead, NEG_INF, logits)
    m = jnp.maximum(jnp.max(logits, axis=-1, keepdims=True), sink)
    p = jnp.exp(logits - m)
    es = jnp.exp(sink - m)
    den = jnp.sum(p, axis=-1, keepdims=True) + es
    return p / den, es / den


def _attn_fwd(qn, kn, v, tbl, sinks, bkt, gattn, after=None):
    n_b = SEQ // BLK

    def body(q_ref, k_ref, v_ref, tbl_ref, sink_ref, bkt_ref, ga_ref, y_ref, yn_ref, p_ref, ps_ref, bias_ref):
        i = pl.program_id(0)

        @pl.when(i == 0)
        def _():
            _band_bias(tbl_ref, bkt_ref[...], bias_ref)

        kb, _, _ = _band_rows(k_ref, i)
        vb, _, _ = _band_rows(v_ref, i)
        upper, dead = _band_masks(i)
        q = q_ref[...]
        lane = lax.broadcasted_iota(jnp.int32, (BLK, 128), 1)
        outs = []
        psinks = jnp.zeros((BLK, 128), F32)
        for g in range(N_HEADS // GQA_GROUP):
            kv = slice(HEAD_DIM * g, HEAD_DIM * (g + 1))
            sink = _per_head_rows([sink_ref[0, GQA_GROUP * g + t] for t in range(GQA_GROUP)])
            probs, psink = _head_probs(_stack_heads(q, g), kb[:, kv], bias_ref[g], upper, dead, sink)
            p_ref[g] = probs.astype(BF16)
            for t in range(GQA_GROUP):
                psinks = jnp.where(lane == GQA_GROUP * g + t, psink[BLK * t:BLK * (t + 1)], psinks)
            outs.append(_dot(_unfold(probs, upper).astype(BF16), vb[:, kv]))
        ps_ref[...] = psinks
        y = _unstack_heads(outs)
        y_ref[...] = y
        yn_ref[...] = (y * _rstd(y) * ga_ref[...]).astype(BF16)

    const = lambda shape: pl.BlockSpec(shape, lambda i: (0,) * len(shape))
    rows = lambda w: pl.BlockSpec((BLK, w), lambda i: (i, 0))
    smem = pl.BlockSpec(memory_space=pltpu.SMEM)
    body, more_specs, more = _ordered_behind(body, 7, after)
    return pl.pallas_call(
        body, name="attn_fwd", grid=(n_b,),
        in_specs=[rows(ATTN_WIDTH), const((SEQ, KV_WIDTH)), const((SEQ, KV_WIDTH)), smem, smem,
                  const((BLK, BLK)), const((1, ATTN_WIDTH))] + more_specs,
        out_specs=[rows(ATTN_WIDTH), rows(ATTN_WIDTH),
                   pl.BlockSpec((None, N_HEADS // GQA_GROUP, GROUP_ROWS, BLK), lambda i: (i, 0, 0, 0)), rows(128)],
        out_shape=[jax.ShapeDtypeStruct((SEQ, ATTN_WIDTH), F32), jax.ShapeDtypeStruct((SEQ, ATTN_WIDTH), BF16),
                   jax.ShapeDtypeStruct((n_b, N_HEADS // GQA_GROUP, GROUP_ROWS, BLK), BF16),
                   jax.ShapeDtypeStruct((SEQ, 128), F32)],
        scratch_shapes=[pltpu.VMEM((N_HEADS // GQA_GROUP, GROUP_ROWS, BLK), F32)],
        compiler_params=_params(("arbitrary",)),
    )(qn, kn, v, tbl, sinks, bkt, gattn, *more)


def _ffn_up(x, ycn, yan, w_out, g2, w_up, fcw, fcb, after=None):
    tm = 512
    n_t = SEQ // tm

    def body(x_ref, ycn_ref, yan_ref, wo_ref, g2_ref, wu_ref, cw_ref, b_ref,
             h1_ref, u2_ref, up_ref, pre_ref, act_ref, halo_ref):
        i, j = pl.program_id(0), pl.program_id(1)

        @pl.when(j == 0)
        def _():
            h1 = x_ref[...] + _dot(ycn_ref[...], wo_ref[0:CONV_WIDTH, :]) + _dot(yan_ref[...], wo_ref[CONV_WIDTH:, :])
            h1_ref[...] = h1
            u2_ref[...] = (h1 * _rstd(h1) * g2_ref[...]).astype(BF16)

        u2 = u2_ref[...]
        pre = []
        for s in range(2):
            up = _dot_nt(u2, wu_ref[s])
            up_ref[s] = up.astype(BF16)
            halo = jnp.where(i == 0, 0.0, halo_ref[s, j])
            pre.append(_conv3(up, _taps(cw_ref.at[s]), halo)[0] + b_ref[s])
            pre_ref[s] = pre[s].astype(BF16)
            halo_ref[s, j] = up[tm - SUBLANES:]
        g, val = pre
        act_ref[...] = (g * jax.nn.sigmoid(g) * val).astype(BF16)

    rows = lambda w: pl.BlockSpec((tm, w), lambda i, j: (i, 0))
    const = lambda shape: pl.BlockSpec(shape, lambda i, j: (0,) * len(shape))
    pair = lambda *s: pl.BlockSpec((2, None) + s, lambda i, j: (0, j) + (0,) * len(s))
    upb = pl.BlockSpec((2, None, tm, FFN_BLK), lambda i, j: (0, j, i, 0))
    body, more_specs, more = _ordered_behind(body, 8, after)
    return pl.pallas_call(
        body, name="ffn_up", grid=(n_t, N_FFN_BLK),
        in_specs=[rows(D_MODEL), rows(CONV_WIDTH), rows(ATTN_WIDTH), const((D_MODEL, D_MODEL)), const((1, D_MODEL)),
                  pair(FFN_BLK, D_MODEL), pair(3, 1, FFN_BLK), pair(1, FFN_BLK)] + more_specs,
        out_specs=[rows(D_MODEL), rows(D_MODEL), upb, upb, pl.BlockSpec((None, tm, FFN_BLK), lambda i, j: (j, i, 0))],
        out_shape=[jax.ShapeDtypeStruct((SEQ, D_MODEL), F32), jax.ShapeDtypeStruct((SEQ, D_MODEL), BF16),
                   jax.ShapeDtypeStruct((2, N_FFN_BLK, SEQ, FFN_BLK), BF16),
                   jax.ShapeDtypeStruct((2, N_FFN_BLK, SEQ, FFN_BLK), BF16),
                   jax.ShapeDtypeStruct((N_FFN_BLK, SEQ, FFN_BLK), BF16)],
        scratch_shapes=[pltpu.VMEM((2, N_FFN_BLK, SUBLANES, FFN_BLK), F32)],
        compiler_params=_params(("arbitrary", "arbitrary")),
    )(x, ycn, yan, w_out, g2, w_up, fcw, fcb, *more)


def _ffn_down(act, w_down, h1, tgt):
    tm = 512
    n_t = SEQ // tm

    def body(act_ref, wd_ref, h1_ref, tgt_ref, dh2_ref, dh2b_ref, loss_ref):
        @pl.when(pl.program_id(0) == 0)
        def _():
            loss_ref[...] = jnp.zeros_like(loss_ref)

        out = _dot(act_ref[0], wd_ref[0])
        for j in range(1, N_FFN_BLK):
            out = out + _dot(act_ref[j], wd_ref[j])
        err = h1_ref[...] + out - tgt_ref[...]
        loss_ref[...] += 0.5 * jnp.sum(err * err) / D_MODEL
        dh2 = err / D_MODEL
        dh2_ref[...] = dh2
        dh2b_ref[...] = dh2.astype(BF16)

    rows = lambda w: pl.BlockSpec((tm, w), lambda i: (i, 0))
    return pl.pallas_call(
        body, name="ffn_down", grid=(n_t,),
        in_specs=[pl.BlockSpec((N_FFN_BLK, tm, FFN_BLK), lambda i: (0, i, 0)),
                  pl.BlockSpec((N_FFN_BLK, FFN_BLK, D_MODEL), lambda i: (0, 0, 0)), rows(D_MODEL), rows(D_MODEL)],
        out_specs=[rows(D_MODEL), rows(D_MODEL), pl.BlockSpec((SUBLANES, 128), lambda i: (0, 0))],
        out_shape=[jax.ShapeDtypeStruct((SEQ, D_MODEL), F32), jax.ShapeDtypeStruct((SEQ, D_MODEL), BF16),
                   jax.ShapeDtypeStruct((SUBLANES, 128), F32)],
        compiler_params=_params(("arbitrary",)),
    )(act, w_down, h1, tgt)


def _out_proj(x, ycn, yan, w_out, g2):
    tm = 512

    def body(x_ref, ycn_ref, yan_ref, wo_ref, g2_ref, h1_ref, u2_ref):
        h1 = x_ref[...] + _dot(ycn_ref[...], wo_ref[0:CONV_WIDTH, :]) + _dot(yan_ref[...], wo_ref[CONV_WIDTH:, :])
        h1_ref[...] = h1
        u2_ref[...] = (h1 * _rstd(h1) * g2_ref[...]).astype(BF16)

    rows = lambda w: pl.BlockSpec((tm, w), lambda i: (i, 0))
    const = lambda shape: pl.BlockSpec(shape, lambda i: (0,) * len(shape))
    return pl.pallas_call(
        body, name="out_proj", grid=(SEQ // tm,),
        in_specs=[rows(D_MODEL), rows(CONV_WIDTH), rows(ATTN_WIDTH), const((D_MODEL, D_MODEL)), const((1, D_MODEL))],
        out_specs=[rows(D_MODEL), rows(D_MODEL)],
        out_shape=[jax.ShapeDtypeStruct((SEQ, D_MODEL), F32), jax.ShapeDtypeStruct((SEQ, D_MODEL), BF16)],
        compiler_params=_params(("arbitrary",)),
    )(x, ycn, yan, w_out, g2)


def _ffn_fwd(h1, u2, w_up, fcw, fcb, w_down, tgt):
    tm = 512
    n_t = SEQ // tm
    last = N_FFN_BLK - 1

    def body(u2_ref, wu_ref, cw_ref, b_ref, wd_ref, h1_ref, tgt_ref,
             up_ref, pre_ref, act_ref, dh2_ref, dh2b_ref, loss_ref, acc_ref, halo_ref):
        j, i = pl.program_id(0), pl.program_id(1)
        rows = pl.ds(pl.multiple_of(i * tm, tm), tm)

        @pl.when((i == 0) & (j == 0))
        def _():
            loss_ref[...] = jnp.zeros_like(loss_ref)

        u2 = u2_ref[...]
        pre = []
        for s in range(2):
            up = _dot_nt(u2, wu_ref[s])
            up_ref[s] = up.astype(BF16)
            halo = jnp.where(i == 0, 0.0, halo_ref[s])
            pre.append(_conv3(up, _taps(cw_ref.at[s]), halo)[0] + b_ref[s])
            pre_ref[s] = pre[s].astype(BF16)
            halo_ref[s] = up[tm - SUBLANES:]
        g, val = pre
        act = (g * jax.nn.sigmoid(g) * val).astype(BF16)
        act_ref[...] = act
        out = _dot(act, wd_ref[...])

        @pl.when(j == 0)
        def _():
            acc_ref[rows, :] = out

        @pl.when(j > 0)
        def _():
            acc_ref[rows, :] += out

        @pl.when(j == last)
        def _():
            err = h1_ref[...] + acc_ref[rows, :] - tgt_ref[...]
            loss_ref[...] += 0.5 * jnp.sum(err * err) / D_MODEL
            dh2 = err / D_MODEL
            dh2_ref[...] = dh2
            dh2b_ref[...] = dh2.astype(BF16)

    late = lambda w: pl.BlockSpec((tm, w), lambda j, i: (jnp.where(j == last, i, 0), 0))
    pair = lambda *s: pl.BlockSpec((2, None) + s, lambda j, i: (0, j) + (0,) * len(s))
    upb = pl.BlockSpec((2, None, tm, FFN_BLK), lambda j, i: (0, j, i, 0))
    return pl.pallas_call(
        body, name="ffn_fwd", grid=(N_FFN_BLK, n_t),
        in_specs=[pl.BlockSpec((tm, D_MODEL), lambda j, i: (i, 0)), pair(FFN_BLK, D_MODEL), pair(3, 1, FFN_BLK),
                  pair(1, FFN_BLK), pl.BlockSpec((None, FFN_BLK, D_MODEL), lambda j, i: (j, 0, 0)),
                  late(D_MODEL), late(D_MODEL)],
        out_specs=[upb, upb, pl.BlockSpec((None, tm, FFN_BLK), lambda j, i: (j, i, 0)), late(D_MODEL), late(D_MODEL),
                   pl.BlockSpec((SUBLANES, 128), lambda j, i: (0, 0))],
        out_shape=[jax.ShapeDtypeStruct((2, N_FFN_BLK, SEQ, FFN_BLK), BF16),
                   jax.ShapeDtypeStruct((2, N_FFN_BLK, SEQ, FFN_BLK), BF16),
                   jax.ShapeDtypeStruct((N_FFN_BLK, SEQ, FFN_BLK), BF16),
                   jax.ShapeDtypeStruct((SEQ, D_MODEL), F32), jax.ShapeDtypeStruct((SEQ, D_MODEL), BF16),
                   jax.ShapeDtypeStruct((SUBLANES, 128), F32)],
        scratch_shapes=[pltpu.VMEM((SEQ, D_MODEL), F32), pltpu.VMEM((2, SUBLANES, FFN_BLK), F32)],
        compiler_params=_params(("arbitrary", "arbitrary")),
    )(u2, w_up, fcw, fcb, w_down, h1, tgt)


def _ffn_fwd_token_major(x, ycn, yan, w_out, g2, w_up, fcw, fcb, w_down, tgt):
    tm = 512
    n_t = SEQ // tm

    def body(x_ref, ycn_ref, yan_ref, wo_ref, g2_ref, wu_ref, cw_ref, b_ref, wd_ref, tgt_ref,
             h1_ref, u2_ref, up_ref, pre_ref, act_ref, dh2_ref, dh2b_ref, loss_ref, acc_ref, halo_ref):
        i, j = pl.program_id(0), pl.program_id(1)

        @pl.when((i == 0) & (j == 0))
        def _():
            loss_ref[...] = jnp.zeros_like(loss_ref)

        @pl.when(j == 0)
        def _():
            h1 = x_ref[...] + _dot(ycn_ref[...], wo_ref[0:CONV_WIDTH, :]) + _dot(yan_ref[...], wo_ref[CONV_WIDTH:, :])
            h1_ref[...] = h1
            u2_ref[...] = (h1 * _rstd(h1) * g2_ref[...]).astype(BF16)
            acc_ref[...] = jnp.zeros_like(acc_ref)

        u2 = u2_ref[...]
        pre = []
        for s in range(2):
            up = _dot_nt(u2, wu_ref[s])
            up_ref[s] = up.astype(BF16)
            halo = jnp.where(i == 0, 0.0, halo_ref[s, j])
            pre.append(_conv3(up, _taps(cw_ref.at[s]), halo)[0] + b_ref[s])
            pre_ref[s] = pre[s].astype(BF16)
            halo_ref[s, j] = up[tm - SUBLANES:]
        g, val = pre
        act = (g * jax.nn.sigmoid(g) * val).astype(BF16)
        act_ref[...] = act
        acc_ref[...] += _dot(act, wd_ref[...])

        @pl.when(j == N_FFN_BLK - 1)
        def _():
            err = h1_ref[...] + acc_ref[...] - tgt_ref[...]
            loss_ref[...] += 0.5 * jnp.sum(err * err) / D_MODEL
            dh2 = err / D_MODEL
            dh2_ref[...] = dh2
            dh2b_ref[...] = dh2.astype(BF16)

    rows = lambda w: pl.BlockSpec((tm, w), lambda i, j: (i, 0))
    const = lambda shape: pl.BlockSpec(shape, lambda i, j: (0,) * len(shape))
    pair = lambda *s: pl.BlockSpec((2, None) + s, lambda i, j: (0, j) + (0,) * len(s))
    upb = pl.BlockSpec((2, None, tm, FFN_BLK), lambda i, j: (0, j, i, 0))
    return pl.pallas_call(
        body, name="ffn_fwd", grid=(n_t, N_FFN_BLK),
        in_specs=[rows(D_MODEL), rows(CONV_WIDTH), rows(ATTN_WIDTH), const((D_MODEL, D_MODEL)), const((1, D_MODEL)),
                  pair(FFN_BLK, D_MODEL), pair(3, 1, FFN_BLK), pair(1, FFN_BLK),
                  pl.BlockSpec((None, FFN_BLK, D_MODEL), lambda i, j: (j, 0, 0)), rows(D_MODEL)],
        out_specs=[rows(D_MODEL), rows(D_MODEL), upb, upb, pl.BlockSpec((None, tm, FFN_BLK), lambda i, j: (j, i, 0)),
                   rows(D_MODEL), rows(D_MODEL), const((SUBLANES, 128))],
        out_shape=[jax.ShapeDtypeStruct((SEQ, D_MODEL), F32), jax.ShapeDtypeStruct((SEQ, D_MODEL), BF16),
                   jax.ShapeDtypeStruct((2, N_FFN_BLK, SEQ, FFN_BLK), BF16),
                   jax.ShapeDtypeStruct((2, N_FFN_BLK, SEQ, FFN_BLK), BF16),
                   jax.ShapeDtypeStruct((N_FFN_BLK, SEQ, FFN_BLK), BF16),
                   jax.ShapeDtypeStruct((SEQ, D_MODEL), F32), jax.ShapeDtypeStruct((SEQ, D_MODEL), BF16),
                   jax.ShapeDtypeStruct((SUBLANES, 128), F32)],
        scratch_shapes=[pltpu.VMEM((tm, D_MODEL), F32), pltpu.VMEM((2, N_FFN_BLK, SUBLANES, FFN_BLK), F32)],
        compiler_params=_params(("arbitrary", "arbitrary")),
    )(x, ycn, yan, w_out, g2, w_up, fcw, fcb, w_down, tgt)


def _ffn_bwd(dh2, dh2b, h1, g2, up, pre, w_up, fcw, w_down):
    tm = 256
    n_t = SEQ // tm
    last = N_FFN_BLK - 1

    def body(dh2b_ref, up_ref, pre_ref, wu_ref, cw_ref, wd_ref, dh2_ref, h1_ref, g2_ref,
             dup_ref, dh1_ref, dh1b_ref, dfb_ref, dfcw_ref, dg2_ref, acc_ref, next_ref):
        j, i = pl.program_id(0), pl.program_id(1)
        rows = pl.ds(pl.multiple_of((n_t - 1 - i) * tm, tm), tm)

        @pl.when((i == 0) & (j == 0))
        def _():
            dfb_ref[...] = jnp.zeros_like(dfb_ref)
            dfcw_ref[...] = jnp.zeros_like(dfcw_ref)
            dg2_ref[...] = jnp.zeros_like(dg2_ref)

        g, val = pre_ref[0].astype(F32), pre_ref[1].astype(F32)
        sg = jax.nn.sigmoid(g)
        silu = g * sg
        dact = _dot_nt(dh2b_ref[...], wd_ref[...])
        dpre = (dact * val * (sg * (1.0 + g * (1.0 - sg))), dact * silu)
        du = None
        for s in range(2):
            d = dpre[s]
            u = up_ref[s].astype(F32)
            w = _taps(cw_ref.at[s])
            nxt = jnp.where(i == 0, 0.0, next_ref[s])
            d1 = _shift_up(d, 1, nxt)
            d2 = _shift_up(d, 2, nxt)
            next_ref[s] = d[:SUBLANES]
            dfb_ref[s, j] += jnp.sum(d, axis=0, keepdims=True)
            dfcw_ref[s, j, 0] += jnp.sum(d2 * u, axis=0, keepdims=True)
            dfcw_ref[s, j, 1] += jnp.sum(d1 * u, axis=0, keepdims=True)
            dfcw_ref[s, j, 2] += jnp.sum(d * u, axis=0, keepdims=True)
            dup = (d * w[2] + d1 * w[1] + d2 * w[0]).astype(BF16)
            dup_ref[s] = dup
            part = _dot(dup, wu_ref[s])
            du = part if du is None else du + part

        @pl.when(j == 0)
        def _():
            acc_ref[rows, :] = du

        @pl.when(j > 0)
        def _():
            acc_ref[rows, :] += du

        @pl.when(j == last)
        def _():
            dn, dgain = _rms_bwd(h1_ref[...], g2_ref[...], acc_ref[rows, :])
            dh1 = dh2_ref[...] + dn
            dh1_ref[...] = dh1
            dh1b_ref[...] = dh1.astype(BF16)
            dg2_ref[...] += dgain

    rev = lambda i: n_t - 1 - i
    const = lambda shape: pl.BlockSpec(shape, lambda j, i: (0,) * len(shape))
    late = lambda w: pl.BlockSpec((tm, w), lambda j, i: (jnp.where(j == last, rev(i), n_t - 1), 0))
    pair = lambda *s: pl.BlockSpec((2, None) + s, lambda j, i: (0, j) + (0,) * len(s))
    upb = pl.BlockSpec((2, None, tm, FFN_BLK), lambda j, i: (0, j, rev(i), 0))
    return pl.pallas_call(
        body, name="ffn_bwd", grid=(N_FFN_BLK, n_t),
        in_specs=[pl.BlockSpec((tm, D_MODEL), lambda j, i: (rev(i), 0)), upb, upb, pair(FFN_BLK, D_MODEL),
                  pair(3, 1, FFN_BLK), pl.BlockSpec((None, FFN_BLK, D_MODEL), lambda j, i: (j, 0, 0)),
                  late(D_MODEL), late(D_MODEL), const((1, D_MODEL))],
        out_specs=[upb, late(D_MODEL), late(D_MODEL),
                   const((2, N_FFN_BLK, 1, FFN_BLK)), const((2, N_FFN_BLK, 3, 1, FFN_BLK)), const((1, D_MODEL))],
        out_shape=[jax.ShapeDtypeStruct((2, N_FFN_BLK, SEQ, FFN_BLK), BF16), jax.ShapeDtypeStruct((SEQ, D_MODEL), F32),
                   jax.ShapeDtypeStruct((SEQ, D_MODEL), BF16), jax.ShapeDtypeStruct((2, N_FFN_BLK, 1, FFN_BLK), F32),
                   jax.ShapeDtypeStruct((2, N_FFN_BLK, 3, 1, FFN_BLK), F32), jax.ShapeDtypeStruct((1, D_MODEL), F32)],
        scratch_shapes=[pltpu.VMEM((SEQ, D_MODEL), F32), pltpu.VMEM((2, SUBLANES, FFN_BLK), F32)],
        compiler_params=_params(("arbitrary", "arbitrary")),
    )(dh2b, up, pre, w_up, fcw, w_down, dh2, h1, g2)


def _ffn_bwd_token_major(dh2, dh2b, h1, g2, up, pre, w_up, fcw, w_down, after=None):
    tm = 256
    n_t = SEQ // tm

    def body(dh2_ref, dh2b_ref, h1_ref, g2_ref, up_ref, pre_ref, wu_ref, cw_ref, wd_ref,
             dup_ref, dh1_ref, dh1b_ref, dfb_ref, dfcw_ref, dg2_ref, acc_ref, next_ref):
        i, j = pl.program_id(0), pl.program_id(1)

        @pl.when((i == 0) & (j == 0))
        def _():
            dfb_ref[...] = jnp.zeros_like(dfb_ref)
            dfcw_ref[...] = jnp.zeros_like(dfcw_ref)
            dg2_ref[...] = jnp.zeros_like(dg2_ref)

        @pl.when(j == 0)
        def _():
            acc_ref[...] = jnp.zeros_like(acc_ref)

        g, val = pre_ref[0].astype(F32), pre_ref[1].astype(F32)
        sg = jax.nn.sigmoid(g)
        silu = g * sg
        dact = _dot_nt(dh2b_ref[...], wd_ref[...])
        dpre = (dact * val * (sg * (1.0 + g * (1.0 - sg))), dact * silu)
        for s in range(2):
            d = dpre[s]
            u = up_ref[s].astype(F32)
            w = _taps(cw_ref.at[s])
            nxt = jnp.where(i == 0, 0.0, next_ref[s, j])
            d1 = _shift_up(d, 1, nxt)
            d2 = _shift_up(d, 2, nxt)
            next_ref[s, j] = d[:SUBLANES]
            dfb_ref[s, j] += jnp.sum(d, axis=0, keepdims=True)
            dfcw_ref[s, j, 0] += jnp.sum(d2 * u, axis=0, keepdims=True)
            dfcw_ref[s, j, 1] += jnp.sum(d1 * u, axis=0, keepdims=True)
            dfcw_ref[s, j, 2] += jnp.sum(d * u, axis=0, keepdims=True)
            dup = (d * w[2] + d1 * w[1] + d2 * w[0]).astype(BF16)
            dup_ref[s] = dup
            acc_ref[...] += _dot(dup, wu_ref[s])

        @pl.when(j == N_FFN_BLK - 1)
        def _():
            dn, dgain = _rms_bwd(h1_ref[...], g2_ref[...], acc_ref[...])
            dh1 = dh2_ref[...] + dn
            dh1_ref[...] = dh1
            dh1b_ref[...] = dh1.astype(BF16)
            dg2_ref[...] += dgain

    rev = lambda i: n_t - 1 - i
    rows = lambda w: pl.BlockSpec((tm, w), lambda i, j: (rev(i), 0))
    const = lambda shape: pl.BlockSpec(shape, lambda i, j: (0,) * len(shape))
    pair = lambda *s: pl.BlockSpec((2, None) + s, lambda i, j: (0, j) + (0,) * len(s))
    upb = pl.BlockSpec((2, None, tm, FFN_BLK), lambda i, j: (0, j, rev(i), 0))
    body, more_specs, more = _ordered_behind(body, 9, after)
    return pl.pallas_call(
        body, name="ffn_bwd", grid=(n_t, N_FFN_BLK),
        in_specs=[rows(D_MODEL), rows(D_MODEL), rows(D_MODEL), const((1, D_MODEL)), upb, upb,
                  pair(FFN_BLK, D_MODEL), pair(3, 1, FFN_BLK),
                  pl.BlockSpec((None, FFN_BLK, D_MODEL), lambda i, j: (j, 0, 0))] + more_specs,
        out_specs=[upb, rows(D_MODEL), rows(D_MODEL),
                   const((2, N_FFN_BLK, 1, FFN_BLK)), const((2, N_FFN_BLK, 3, 1, FFN_BLK)), const((1, D_MODEL))],
        out_shape=[jax.ShapeDtypeStruct((2, N_FFN_BLK, SEQ, FFN_BLK), BF16), jax.ShapeDtypeStruct((SEQ, D_MODEL), F32),
                   jax.ShapeDtypeStruct((SEQ, D_MODEL), BF16), jax.ShapeDtypeStruct((2, N_FFN_BLK, 1, FFN_BLK), F32),
                   jax.ShapeDtypeStruct((2, N_FFN_BLK, 3, 1, FFN_BLK), F32), jax.ShapeDtypeStruct((1, D_MODEL), F32)],
        scratch_shapes=[pltpu.VMEM((tm, D_MODEL), F32), pltpu.VMEM((2, N_FFN_BLK, SUBLANES, FFN_BLK), F32)],
        compiler_params=_params(("arbitrary", "arbitrary")),
    )(dh2, dh2b, h1, g2, up, pre, w_up, fcw, w_down, *more)


def _grad_tn(a_list, b, out_rows, name, after=None):
    n = len(a_list)
    ncol = b.shape[1]

    def body(*refs):
        a_refs, b_ref, o_ref = refs[:n], refs[n], refs[n + 1]
        j = pl.program_id(0)
        for k in range(n):
            @pl.when(j == k)
            def _(k=k):
                o_ref[...] = _dot_tn(a_refs[k][...], b_ref[...]).astype(BF16)

    full = lambda shape: pl.BlockSpec(shape, lambda j: (0,) * len(shape))
    body, more_specs, more = _ordered_behind(body, n + 1, after)
    return pl.pallas_call(
        body, name=name, grid=(n,),
        in_specs=[full((SEQ, out_rows))] * n + [full((SEQ, ncol))] + more_specs,
        out_specs=pl.BlockSpec((None, out_rows, ncol), lambda j: (j, 0, 0)),
        out_shape=jax.ShapeDtypeStruct((n, out_rows, ncol), BF16),
        compiler_params=_params(("arbitrary",)),
    )(*a_list, b, *more)


def _grad_tn_blocked(a, b, name, a_is_blocked=True, per_step=2):
    assert a_is_blocked
    nb, _, a_w = a.shape
    b_w = b.shape[-1]

    def body(a_ref, b_ref, o_ref):
        for p in range(per_step):
            o_ref[p] = _dot_tn(a_ref[p], b_ref[...]).astype(BF16)

    return pl.pallas_call(
        body, name=name, grid=(nb // per_step,),
        in_specs=[pl.BlockSpec((per_step, SEQ, a_w), lambda k: (k, 0, 0)), pl.BlockSpec((SEQ, b_w), lambda k: (0, 0))],
        out_specs=pl.BlockSpec((per_step, a_w, b_w), lambda k: (k, 0, 0)),
        out_shape=jax.ShapeDtypeStruct((nb, a_w, b_w), BF16),
        compiler_params=_params(("arbitrary",)),
    )(a, b)


def _out_bwd(dh1b, w_out, y_attn, gattn, after=None):
    tm = 1024
    n_t = SEQ // tm

    def body(dh_ref, wo_ref, y_ref, ga_ref, dycn_ref, dy_ref, dga_ref):
        @pl.when(pl.program_id(0) == 0)
        def _():
            dga_ref[...] = jnp.zeros_like(dga_ref)

        dycat = _dot_nt(dh_ref[...], wo_ref[...])
        dycn_ref[...] = dycat[:, :CONV_WIDTH]
        dy, dga = _rms_bwd(y_ref[...], ga_ref[...], dycat[:, CONV_WIDTH:])
        dy_ref[...] = dy
        dga_ref[...] += dga

    rows = lambda w: pl.BlockSpec((tm, w), lambda i: (i, 0))
    const = lambda shape: pl.BlockSpec(shape, lambda i: (0,) * len(shape))
    body, more_specs, more = _ordered_behind(body, 4, after)
    return pl.pallas_call(
        body, name="out_bwd", grid=(n_t,),
        in_specs=[rows(D_MODEL), const((D_MODEL, D_MODEL)), rows(ATTN_WIDTH), const((1, ATTN_WIDTH))] + more_specs,
        out_specs=[rows(CONV_WIDTH), rows(ATTN_WIDTH), const((1, ATTN_WIDTH))],
        out_shape=[jax.ShapeDtypeStruct((SEQ, CONV_WIDTH), F32), jax.ShapeDtypeStruct((SEQ, ATTN_WIDTH), F32),
                   jax.ShapeDtypeStruct((1, ATTN_WIDTH), F32)],
        compiler_params=_params(("arbitrary",)),
    )(dh1b, w_out, y_attn, gattn, *more)


def _attn_bwd(qn, kn, v, dy, probs, psinks, bkt, after=None):
    n_b = SEQ // BLK

    def body(q_ref, k_ref, v_ref, dy_ref, p_ref, ps_ref, bkt_ref,
             dq_ref, dk_ref, dv_ref, dtbl_ref, dsink_ref, dbias_ref, dsacc_ref):
        i = pl.program_id(0)

        @pl.when(i == 0)
        def _():
            dbias_ref[...] = jnp.zeros_like(dbias_ref)
            dsacc_ref[...] = jnp.zeros_like(dsacc_ref)
            dk_ref[...] = jnp.zeros_like(dk_ref)
            dv_ref[...] = jnp.zeros_like(dv_ref)

        kb, prev, cur = _band_rows(k_ref, i)
        vb, _, _ = _band_rows(v_ref, i)
        upper, _ = _band_masks(i)
        q = q_ref[...]
        dy = dy_ref[...]
        psink = ps_ref[...]
        lane = lax.broadcasted_iota(jnp.int32, (BLK, 128), 1)
        dsink = jnp.zeros((BLK, 128), F32)
        dqs, dks, dvs = [], [], []
        for g in range(N_HEADS // GQA_GROUP):
            kv = slice(HEAD_DIM * g, HEAD_DIM * (g + 1))
            qg = _stack_heads(q, g)
            dog = _stack_heads(dy, g).astype(BF16)
            pb = p_ref[g]
            pg = pb.astype(F32)
            dprobs = _fold(_dot_nt(dog, vb[:, kv]), upper)
            dvs.append(_dot_tn(_unfold(pb, upper), dog))
            dsum = jnp.sum(pg * dprobs, axis=-1, keepdims=True)
            dlogits = pg * (dprobs - dsum)
            for t in range(GQA_GROUP):
                dsink = jnp.where(lane == GQA_GROUP * g + t, -psink * dsum[BLK * t:BLK * (t + 1)], dsink)
            dbias_ref[g] += dlogits
            ds = _unfold(dlogits * (HEAD_DIM ** -0.5), upper).astype(BF16)
            dqs.append(_dot(ds, kb[:, kv]))
            dks.append(_dot_tn(ds, qg))
        dsacc_ref[...] += dsink
        dq_ref[...] = _unstack_heads(dqs)
        dkb = jnp.concatenate(dks, axis=-1)
        dvb = jnp.concatenate(dvs, axis=-1)
        dk_ref[pl.ds(prev, BLK), :] += dkb[:BLK]
        dk_ref[pl.ds(cur, BLK), :] += dkb[BLK:]
        dv_ref[pl.ds(prev, BLK), :] += dvb[:BLK]
        dv_ref[pl.ds(cur, BLK), :] += dvb[BLK:]

        @pl.when(i == n_b - 1)
        def _():
            bkt = bkt_ref[...]
            row8 = lax.broadcasted_iota(jnp.int32, (N_HEADS, 128), 0)
            lane8 = lax.broadcasted_iota(jnp.int32, (N_HEADS, 128), 1)
            acc = jnp.zeros((N_HEADS, 128), F32)
            for h in range(N_HEADS):
                rows = slice(BLK * (h % GQA_GROUP), BLK * (h % GQA_GROUP + 1))
                dbh = dbias_ref[h // GQA_GROUP, rows, :]
                for b in range(NUM_BUCKETS):
                    acc = jnp.where((row8 == h) & (lane8 == b), jnp.sum(jnp.where(bkt == b, dbh, 0.0)), acc)
            dsink_ref[...] = jnp.sum(dsacc_ref[...], axis=0, keepdims=True)
            dtbl_ref[...] = acc

    const = lambda shape: pl.BlockSpec(shape, lambda i: (0,) * len(shape))
    rows = lambda w: pl.BlockSpec((BLK, w), lambda i: (i, 0))
    n_g = N_HEADS // GQA_GROUP
    body, more_specs, more = _ordered_behind(body, 7, after)
    return pl.pallas_call(
        body, name="attn_bwd", grid=(n_b,),
        in_specs=[rows(ATTN_WIDTH), const((SEQ, KV_WIDTH)), const((SEQ, KV_WIDTH)), rows(ATTN_WIDTH),
                  pl.BlockSpec((None, n_g, GROUP_ROWS, BLK), lambda i: (i, 0, 0, 0)), rows(128),
                  const((BLK, BLK))] + more_specs,
        out_specs=[rows(ATTN_WIDTH), const((SEQ, KV_WIDTH)), const((SEQ, KV_WIDTH)), const((N_HEADS, 128)), const((1, 128))],
        out_shape=[jax.ShapeDtypeStruct((SEQ, ATTN_WIDTH), F32), jax.ShapeDtypeStruct((SEQ, KV_WIDTH), F32),
                   jax.ShapeDtypeStruct((SEQ, KV_WIDTH), F32), jax.ShapeDtypeStruct((N_HEADS, 128), F32),
                   jax.ShapeDtypeStruct((1, 128), F32)],
        scratch_shapes=[pltpu.VMEM((n_g, GROUP_ROWS, BLK), F32), pltpu.VMEM((BLK, 128), F32)],
        compiler_params=_params(("arbitrary",)),
    )(qn, kn, v, dy, probs, psinks, bkt, *more)


def _mix_in_bwd(x, dh1, proj, dycn, dqn, dkn, dv, w_in_t, conv_w, g1, gq, gk, gconv):
    tm = 512
    n_t = SEQ // tm
    halo_blocks = tm // SUBLANES

    def body(x_ref, dh1_ref, proj_ref, halo_ref, dycn_ref, dqn_ref, dkn_ref, dv_ref, w_ref, cw_ref,
             g1_ref, gq_ref, gk_ref, gc_ref,
             dx_ref, dproj_ref, dcw_ref, dgc_ref, dgq_ref, dgk_ref, dg1_ref, next_ref):
        i = pl.program_id(0)
        first_tile = i == n_t - 1

        @pl.when(i == 0)
        def _():
            for r in (dcw_ref, dgc_ref, dgq_ref, dgk_ref, dg1_ref, next_ref):
                r[...] = jnp.zeros_like(r)

        proj = proj_ref[...]
        hp = halo_ref[...]
        gate_b = proj[:, 0:CONV_WIDTH]
        gate_c = proj[:, CONV_WIDTH:2 * CONV_WIDTH]
        hc = proj[:, 2 * CONV_WIDTH:3 * CONV_WIDTH]
        a = gate_c * hc
        a_halo = jnp.where(first_tile, 0.0, hp[:, CONV_WIDTH:2 * CONV_WIDTH] * hp[:, 2 * CONV_WIDTH:3 * CONV_WIDTH])
        cw = _taps(cw_ref[...])
        cv, a2, a1 = _conv3(a, cw, a_halo)
        dyc, dgc = _rms_bwd(gate_b * cv, gc_ref[...], dycn_ref[...])
        dgc_ref[...] += dgc
        dcv = dyc * gate_b
        dcw_ref[...] += jnp.concatenate(
            [jnp.sum(dcv * a2, axis=0, keepdims=True), jnp.sum(dcv * a1, axis=0, keepdims=True),
             jnp.sum(dcv * a, axis=0, keepdims=True)], axis=0)
        da = _conv3_bwd_input(dcv, cw, next_ref[...])
        next_ref[...] = dcv[:SUBLANES]
        q0 = 3 * CONV_WIDTH
        k0 = q0 + ATTN_WIDTH
        dq, dgq = _head_norm_bwd(proj[:, q0:k0], gq_ref[...], dqn_ref[...], N_HEADS)
        dk, dgk = _head_norm_bwd(proj[:, k0:k0 + KV_WIDTH], gk_ref[...], dkn_ref[...], 2)
        dgq_ref[...] += dgq
        dgk_ref[...] += dgk
        dproj = jnp.concatenate([dyc * cv, da * hc, da * gate_c, dq, dk, dv_ref[...]], axis=-1).astype(BF16)
        dproj_ref[...] = dproj
        du1 = _dot(dproj, w_ref[...])
        xv = x_ref[...]
        dn, dg1 = _rms_bwd(xv, g1_ref[...], du1)
        dx_ref[...] = dh1_ref[...] + dn
        dg1_ref[...] += dg1

    rev = lambda i: n_t - 1 - i
    rows = lambda w: pl.BlockSpec((tm, w), lambda i: (rev(i), 0))
    const = lambda shape: pl.BlockSpec(shape, lambda i: (0,) * len(shape))
    halo = pl.BlockSpec((SUBLANES, IN_WIDTH), lambda i: (jnp.maximum(rev(i) * halo_blocks - 1, 0), 0))
    return pl.pallas_call(
        body, name="mix_in_bwd", grid=(n_t,),
        in_specs=[rows(D_MODEL), rows(D_MODEL), rows(IN_WIDTH), halo, rows(CONV_WIDTH), rows(ATTN_WIDTH), rows(KV_WIDTH),
                  rows(KV_WIDTH), const((IN_WIDTH, D_MODEL)), const((3, CONV_WIDTH)), const((1, D_MODEL)),
                  const((1, HEAD_DIM)), const((1, HEAD_DIM)), const((1, CONV_WIDTH))],
        out_specs=[rows(D_MODEL), rows(IN_WIDTH), const((3, CONV_WIDTH)), const((1, CONV_WIDTH)),
                   const((1, HEAD_DIM)), const((1, HEAD_DIM)), const((1, D_MODEL))],
        out_shape=[jax.ShapeDtypeStruct((SEQ, D_MODEL), F32), jax.ShapeDtypeStruct((SEQ, IN_WIDTH), BF16),
                   jax.ShapeDtypeStruct((3, CONV_WIDTH), F32),
                   jax.ShapeDtypeStruct((1, CONV_WIDTH), F32), jax.ShapeDtypeStruct((1, HEAD_DIM), F32),
                   jax.ShapeDtypeStruct((1, HEAD_DIM), F32), jax.ShapeDtypeStruct((1, D_MODEL), F32)],
        scratch_shapes=[pltpu.VMEM((SUBLANES, CONV_WIDTH), F32)],
        compiler_params=_params(("arbitrary",)),
    )(x, dh1, proj, proj, dycn, dqn, dkn, dv, w_in_t, conv_w, g1, gq, gk, gconv)


def _grad_w_in(dproj, u1, after=None):
    bw = 768

    def body(a_ref, b_ref, o_ref):
        o_ref[...] = _dot_tn(a_ref[...], b_ref[...]).astype(BF16)

    body, more_specs, more = _ordered_behind(body, 2, after)
    return pl.pallas_call(
        body, name="grad_w_in", grid=(IN_WIDTH // bw,),
        in_specs=[pl.BlockSpec((SEQ, bw), lambda k: (0, k)), pl.BlockSpec((SEQ, D_MODEL), lambda k: (0, 0))] + more_specs,
        out_specs=pl.BlockSpec((bw, D_MODEL), lambda k: (k, 0)),
        out_shape=jax.ShapeDtypeStruct((IN_WIDTH, D_MODEL), BF16),
        compiler_params=_params(("arbitrary",)),
    )(dproj, u1, *more)


def _adamw_math(w, g, m, v):
    m = ADAM_B1 * m + (1.0 - ADAM_B1) * g
    v = ADAM_B2 * v + (1.0 - ADAM_B2) * (g * g)
    m_hat = m / (1.0 - ADAM_B1 ** ADAM_STEP)
    v_hat = v / (1.0 - ADAM_B2 ** ADAM_STEP)
    return -ADAM_LR * (m_hat / (jnp.sqrt(v_hat) + ADAM_EPS) + ADAM_WD * w), m, v


_ROW_G1, _ROW_G2, _ROW_OUT_NORMS, _ROW_FFN_B, _ROW_GQ, _ROW_GK, _ROW_SINKS, _ROW_LOSS, _ROW_TABLE = 0, 1, 2, 3, 11, 12, 13, 14, 16
SMALL_ROWS, SMALL_COLS = 24, 1024
_SMALL_NAMES = ("norm_mix_g", "norm_ffn_g", "out_norm_conv_g", "out_norm_attn_g", "ffn_conv_b", "q_norm_g", "k_norm_g",
                "sinks", "rel_bias_table")


def _pack_small_grads(dg1, dg2, dgconv, dgattn, dfb, dgq, dgk, dsinks, dtbl_t, loss_acc):
    def body(dg1_ref, dg2_ref, dgc_ref, dga_ref, dfb_ref, dgq_ref, dgk_ref, ds_ref, dt_ref, loss_ref, o_ref):
        o_ref[...] = jnp.zeros_like(o_ref)
        o_ref[_ROW_G1:_ROW_G1 + 1, :] = dg1_ref[...]
        o_ref[_ROW_G2:_ROW_G2 + 1, :] = dg2_ref[...]
        o_ref[_ROW_OUT_NORMS:_ROW_OUT_NORMS + 1, 0:CONV_WIDTH] = dgc_ref[...]
        o_ref[_ROW_OUT_NORMS:_ROW_OUT_NORMS + 1, CONV_WIDTH:] = dga_ref[...]
        for k in range(N_DEV):
            o_ref[_ROW_FFN_B + k:_ROW_FFN_B + k + 1, 0:FFN_BLK] = dfb_ref[k // N_FFN_BLK, k % N_FFN_BLK]
        o_ref[_ROW_GQ:_ROW_GQ + 1, 0:HEAD_DIM] = dgq_ref[...]
        o_ref[_ROW_GK:_ROW_GK + 1, 0:HEAD_DIM] = dgk_ref[...]
        o_ref[_ROW_SINKS:_ROW_SINKS + 1, 0:128] = ds_ref[...]
        o_ref[_ROW_LOSS:_ROW_LOSS + 1, 0:128] = loss_ref[0:1, :]
        o_ref[_ROW_TABLE:_ROW_TABLE + N_HEADS, 0:128] = dt_ref[...]

    return pl.pallas_call(body, name="pack_small_grads", out_shape=jax.ShapeDtypeStruct((SMALL_ROWS, SMALL_COLS), F32))(
        dg1, dg2, dgconv, dgattn, dfb, dgq, dgk, dsinks, dtbl_t, loss_acc)


def _adamw_small(recv, params, after):
    names = _SMALL_NAMES
    n = len(names)

    def grad_of(g, name, k=None):
        if name == "norm_mix_g":
            return g[_ROW_G1:_ROW_G1 + 1, :]
        if name == "norm_ffn_g":
            return g[_ROW_G2:_ROW_G2 + 1, :]
        if name == "out_norm_conv_g":
            return g[_ROW_OUT_NORMS:_ROW_OUT_NORMS + 1, 0:CONV_WIDTH]
        if name == "out_norm_attn_g":
            return g[_ROW_OUT_NORMS:_ROW_OUT_NORMS + 1, CONV_WIDTH:]
        if name == "ffn_conv_b":
            return g[_ROW_FFN_B + k:_ROW_FFN_B + k + 1, 0:FFN_BLK]
        if name == "q_norm_g":
            return g[_ROW_GQ:_ROW_GQ + 1, 0:HEAD_DIM]
        if name == "k_norm_g":
            return g[_ROW_GK:_ROW_GK + 1, 0:HEAD_DIM]
        if name == "sinks":
            return g[_ROW_SINKS:_ROW_SINKS + 1, 0:N_HEADS]
        return g[_ROW_TABLE:_ROW_TABLE + N_HEADS, 0:NUM_BUCKETS]

    def body(r_ref, *refs):
        ins, outs, loss_ref = refs[:3 * n], refs[3 * n:7 * n], refs[7 * n]
        g = r_ref[0]
        for s in range(1, N_DEV):
            g = g + r_ref[s]
        loss_ref[...] = g[_ROW_LOSS:_ROW_LOSS + 1, 0:128]
        for i, name in enumerate(names):
            w_ref, m_ref, v_ref = ins[3 * i:3 * i + 3]
            o = outs[4 * i:4 * i + 4]
            cols = [slice(FFN_BLK * k, FFN_BLK * (k + 1)) for k in range(N_DEV)] if name == "ffn_conv_b" else [slice(None)]
            for k, cs in enumerate(cols):
                gk = grad_of(g, name, k)
                d, m2, v2 = _adamw_math(w_ref[:, cs], gk, m_ref[:, cs], v_ref[:, cs])
                o[0][:, cs], o[1][:, cs], o[2][:, cs], o[3][:, cs] = gk, d, m2, v2

    flat = [a for name in names for a in params[name]]
    body, more_specs, more = _ordered_behind(body, 1 + 3 * n, after)
    vmem = pl.BlockSpec(memory_space=pltpu.VMEM)
    out = pl.pallas_call(
        body, name="adamw_small",
        in_specs=[vmem] * (1 + 3 * n) + more_specs,
        out_shape=[jax.ShapeDtypeStruct(params[name][0].shape, F32) for name in names for _ in range(4)]
        + [jax.ShapeDtypeStruct((1, 128), F32)],
        compiler_params=pltpu.CompilerParams(vmem_limit_bytes=VMEM_LIMIT),
    )(recv, *flat, *more)
    return {name: tuple(out[4 * i:4 * i + 4]) for i, name in enumerate(names)}, out[4 * n]


def _adamw_direct(w, m, v, own, recv, me, name, row_blocks=1, after=None):
    rb = w.shape[0] // row_blocks
    cols = w.shape[1]

    def body(me_ref, w_ref, m_ref, v_ref, o_ref, r_ref, g_o, d_o, m_o, v_o):
        g = o_ref[...].astype(F32)
        for s in range(N_DEV - 1):
            g = g + r_ref[s].astype(F32)
        g_o[...] = g
        d_o[...], m_o[...], v_o[...] = _adamw_math(w_ref[...], g, m_ref[...], v_ref[...])

    blk = pl.BlockSpec((rb, cols), lambda i, me_ref: (i, 0))
    oblk = pl.BlockSpec((None, rb, cols), lambda i, me_ref: (me_ref[0], i, 0))
    rblk = pl.BlockSpec((N_DEV - 1, rb, cols), lambda i, me_ref: (0, i, 0))
    body, more_specs, more = _ordered_behind(body, 6, after)
    return pl.pallas_call(
        body, name=name,
        grid_spec=pltpu.PrefetchScalarGridSpec(num_scalar_prefetch=1, grid=(row_blocks,),
                                               in_specs=[blk, blk, blk, oblk, rblk] + more_specs, out_specs=[blk] * 4),
        out_shape=[jax.ShapeDtypeStruct(w.shape, F32)] * 4,
        compiler_params=_params(("arbitrary",)),
    )(me, w, m, v, own, recv, *more)


def _adamw(w, m, v, part, recv, chip, name, row_blocks=1, after=None):
    rb = w.shape[0] // row_blocks
    tail = w.shape[1:]
    zeros = (0,) * len(tail)

    def body(chip_ref, w_ref, m_ref, v_ref, p_ref, r_ref, g_o, d_o, m_o, v_o):
        g = p_ref[...].astype(F32)
        for s in range(3):
            g = g + r_ref[s].astype(F32)
        g_o[...] = g
        d_o[...], m_o[...], v_o[...] = _adamw_math(w_ref[...], g, m_ref[...], v_ref[...])

    blk = pl.BlockSpec((rb,) + tail, lambda i, chip_ref: (i,) + zeros)
    pblk = pl.BlockSpec((None, rb) + tail, lambda i, chip_ref: (chip_ref[0], i) + zeros)
    rblk = pl.BlockSpec((3, rb) + tail, lambda i, chip_ref: (0, i) + zeros)
    body, more_specs, more = _ordered_behind(body, 6, after)
    return pl.pallas_call(
        body, name=name,
        grid_spec=pltpu.PrefetchScalarGridSpec(num_scalar_prefetch=1, grid=(row_blocks,),
                                               in_specs=[blk, blk, blk, pblk, rblk] + more_specs, out_specs=[blk] * 4),
        out_shape=[jax.ShapeDtypeStruct(w.shape, F32)] * 4,
        compiler_params=_params(("arbitrary",)),
    )(chip, w, m, v, part, recv, *more)


def kernel(x, norm_mix_g, w_in, conv_w, q_norm_g, k_norm_g, rel_bias_table, sinks, out_norm_conv_g, out_norm_attn_g, w_out, norm_ffn_g, w_up, ffn_conv_w, ffn_conv_b, w_down, loss_target, m_norm_mix_g, m_w_in, m_conv_w, m_q_norm_g, m_k_norm_g, m_rel_bias_table, m_sinks, m_out_norm_conv_g, m_out_norm_attn_g, m_w_out, m_norm_ffn_g, m_w_up, m_ffn_conv_w, m_ffn_conv_b, m_w_down, v_norm_mix_g, v_w_in, v_conv_w, v_q_norm_g, v_k_norm_g, v_rel_bias_table, v_sinks, v_out_norm_conv_g, v_out_norm_attn_g, v_w_out, v_norm_ffn_g, v_w_up, v_ffn_conv_w, v_ffn_conv_b, v_w_down):
    p = dict(norm_mix_g=norm_mix_g, w_in=w_in, conv_w=conv_w, q_norm_g=q_norm_g, k_norm_g=k_norm_g,
             rel_bias_table=rel_bias_table, sinks=sinks, out_norm_conv_g=out_norm_conv_g, out_norm_attn_g=out_norm_attn_g,
             w_out=w_out, norm_ffn_g=norm_ffn_g, w_up=w_up, ffn_conv_w=ffn_conv_w, ffn_conv_b=ffn_conv_b, w_down=w_down)
    m = dict(norm_mix_g=m_norm_mix_g, w_in=m_w_in, conv_w=m_conv_w, q_norm_g=m_q_norm_g, k_norm_g=m_k_norm_g,
             rel_bias_table=m_rel_bias_table, sinks=m_sinks, out_norm_conv_g=m_out_norm_conv_g,
             out_norm_attn_g=m_out_norm_attn_g, w_out=m_w_out, norm_ffn_g=m_norm_ffn_g, w_up=m_w_up,
             ffn_conv_w=m_ffn_conv_w, ffn_conv_b=m_ffn_conv_b, w_down=m_w_down)
    v = dict(norm_mix_g=v_norm_mix_g, w_in=v_w_in, conv_w=v_conv_w, q_norm_g=v_q_norm_g, k_norm_g=v_k_norm_g,
             rel_bias_table=v_rel_bias_table, sinks=v_sinks, out_norm_conv_g=v_out_norm_conv_g,
             out_norm_attn_g=v_out_norm_attn_g, w_out=v_w_out, norm_ffn_g=v_norm_ffn_g, w_up=v_w_up,
             ffn_conv_w=v_ffn_conv_w, ffn_conv_b=v_ffn_conv_b, w_down=v_w_down)

    xs, tgt = x[0], loss_target[0]
    g1, g2, gq, gk, gconv, gattn = norm_mix_g, norm_ffn_g, q_norm_g, k_norm_g, out_norm_conv_g, out_norm_attn_g
    ix, iy, ic = _coords()
    core = ic.astype(jnp.int32).reshape(1)
    chip = (2 * ix + iy).astype(jnp.int32).reshape(1)
    me = _lin(ix, iy, ic).astype(jnp.int32).reshape(1)
    bkt = jnp.asarray(_bucket_map())
    tr = lambda a: a[0].T
    taps = lambda a: jnp.transpose(a, (1, 0, 2))
    tbl_t = rel_bias_table.T

    wi_l, cw_l = _place_shards(me, [tr(w_in), taps(conv_w)], [BF16, F32], "place_mixer_shards")
    finish_a, token_a = _all_gather_split([wi_l, cw_l], "mixer", None)
    wo_l, wu_l, wd_l, fcw_l = _place_shards(me, [w_out[0], tr(w_up), w_down[0], taps(ffn_conv_w)],
                                            [BF16, BF16, BF16, F32], "place_ffn_shards", after=token_a)
    ffn_stage2, ffn_stage3, token_b = _all_gather_tree([wo_l, wu_l, wd_l, fcw_l], "ffn", token_a)
    wi_g, cw_g = finish_a(token_b)
    w_in_t = wi_g.reshape(IN_WIDTH, D_MODEL)
    conv_w_f = jnp.transpose(cw_g[:, :, 0, :], (1, 0, 2)).reshape(3, CONV_WIDTH)

    proj, u1, ycn, qn, kn, vv = _mix_in_fwd(xs, g1, w_in_t, conv_w_f, gq, gk, gconv)
    token_b2 = ffn_stage2(ycn)
    y_attn, yan, probs, psinks = _attn_fwd(qn, kn, vv, tbl_t, sinks, bkt, gattn, after=token_b2)
    wo_g, wu_g, wd_g, fcw_g = ffn_stage3(yan)
    w_out_f = wo_g.reshape(D_MODEL, D_MODEL)
    w_down_f = wd_g.reshape(N_FFN_BLK, FFN_BLK, D_MODEL)
    w_up_f = wu_g.reshape(2, N_FFN_BLK, FFN_BLK, D_MODEL)
    fcw_f = fcw_g.reshape(2, N_FFN_BLK, 3, 1, FFN_BLK)
    fcb = ffn_conv_b.reshape(2, N_FFN_BLK, 1, FFN_BLK)
    h1, u2, up, pre, act, dh2, dh2b, loss_acc = _ffn_fwd_token_major(
        xs, ycn, yan, w_out_f, g2, w_up_f, fcw_f, fcb, w_down_f, tgt)

    dw_down = _grad_tn_blocked(act, dh2b, "grad_w_down", a_is_blocked=True).reshape(N_DEV, D_FF // N_DEV, D_MODEL)
    plan_d, slots_d = _scatter_plan(1)
    d_sem = _split_start("scatter_w_down_start", [dw_down], [lax.empty((N_DEV - 1,) + dw_down.shape[1:], BF16)],
                         plan_d, None, _ALL_FOR_W_DOWN)
    dup, dh1, dh1b, dfb, dfcw, dg2 = _ffn_bwd_token_major(dh2, dh2b, h1, g2, up, pre, w_up_f, fcw_f, w_down_f,
                                                          after=d_sem[4])
    dw_up = _grad_tn_blocked(dup.reshape(N_DEV, SEQ, FFN_BLK), u2, "grad_w_up", a_is_blocked=True)
    dw_out = _grad_tn([ycn, yan], dh1b, CONV_WIDTH, "grad_w_out").reshape(N_DEV, D_MODEL // N_DEV, D_MODEL)
    out_bwd = {}

    def behind_ffn(token):
        out_bwd["r"] = _out_bwd(dh1b, w_out_f, y_attn, gattn, after=token)
        return out_bwd["r"][0]

    finish_ffn, token_ffn = _reduce_scatter_split(
        [dw_up, dw_out, dfcw.reshape(N_DEV, 3, 1, FFN_BLK)], "ffn", core, behind_ffn)
    dycn, dy_attn, dgattn = out_bwd["r"]
    dqn, dkn, dv, dtbl_t, dsinks = _attn_bwd(qn, kn, vv, dy_attn, probs, psinks, bkt, after=token_ffn)
    dx, dproj, dcw, dgconv, dgq, dgk, dg1 = _mix_in_bwd(xs, dh1, proj, dycn, dqn, dkn, dv, w_in_t, conv_w_f,
                                                             g1, gq, gk, gconv)
    packed = _pack_small_grads(dg1, dg2, dgconv, dgattn, dfb, dgq, dgk, dsinks, dtbl_t, loss_acc)
    plan_s, slots_s = _broadcast_plan()
    s_sem, r_sem, src_s, land_s, token_s = _split_start(
        "gather_small_start", [packed], [jnp.broadcast_to(packed[None], (N_DEV,) + packed.shape)], plan_s, None,
        _ALL_FOR_SMALL)
    dw_in_t = _grad_w_in(dproj, u1, after=token_s).reshape(N_DEV, IN_WIDTH // N_DEV, D_MODEL)
    dcw_b = jnp.transpose(dcw.reshape(3, N_DEV, 1, CONV_WIDTH // N_DEV), (1, 0, 2, 3))
    adam = {}
    ffn_got = {}

    def behind_mixer(token):
        ffn_got["r"] = finish_ffn(token)
        return ffn_got["r"][1][0]

    finish_mixer, token_mixer = _reduce_scatter_split([dw_in_t, dcw_b], "mixer", core, behind_mixer)
    (p_wu, p_wo, p_fcw), (r_wu, r_wo, r_fcw) = ffn_got["r"]
    (own_wd,), (r_wd,) = _split_wait("scatter_w_down_wait", d_sem[0], d_sem[1], d_sem[2], d_sem[3], plan_d, slots_d,
                                     token_mixer)
    adam["w_down"] = _adamw_direct(w_down[0], m_w_down[0], v_w_down[0], own_wd, r_wd, me, "adamw_w_down", row_blocks=2)
    adam_up = _adamw(tr(w_up), tr(m_w_up), tr(v_w_up), p_wu, r_wu, chip, "adamw_w_up", row_blocks=4,
                     after=adam["w_down"][0])
    adam["w_out"] = _adamw(w_out[0], m_w_out[0], v_w_out[0], p_wo, r_wo, chip, "adamw_w_out", after=adam_up[0])
    adam_fcw = _adamw(taps(ffn_conv_w), taps(m_ffn_conv_w), taps(v_ffn_conv_w), p_fcw, r_fcw, chip, "adamw_ffn_conv_w",
                      after=adam["w_out"][0])
    _, (r_small,) = _split_wait("gather_small_wait", s_sem, r_sem, src_s, land_s, plan_s, slots_s, adam_fcw[0])
    small_in = {k: (p[k], m[k], v[k]) for k in _SMALL_NAMES}
    small_in["rel_bias_table"] = (tbl_t, m_rel_bias_table.T, v_rel_bias_table.T)
    small_out, loss_row = _adamw_small(r_small, small_in, None)
    (p_wi, p_cw), (r_wi, r_cw) = finish_mixer(loss_row)
    adam_in = _adamw(tr(w_in), tr(m_w_in), tr(v_w_in), p_wi, r_wi, chip, "adamw_w_in")
    adam_cw = _adamw(taps(conv_w), taps(m_conv_w), taps(v_conv_w), p_cw, r_cw, chip, "adamw_conv_w")

    res = {k: tuple(a[None] for a in t) for k, t in adam.items()}
    res["w_up"] = tuple(a.T[None] for a in adam_up)
    res["w_in"] = tuple(a.T[None] for a in adam_in)
    res["ffn_conv_w"] = tuple(taps(a) for a in adam_fcw)
    res["conv_w"] = tuple(taps(a) for a in adam_cw)
    res.update(small_out)
    res["rel_bias_table"] = tuple(a.T for a in small_out["rel_bias_table"])
    loss = loss_row[0, 0]
    order = ("norm_mix_g", "w_in", "conv_w", "q_norm_g", "k_norm_g", "rel_bias_table", "sinks", "out_norm_conv_g",
             "out_norm_attn_g", "w_out", "norm_ffn_g", "w_up", "ffn_conv_w", "ffn_conv_b", "w_down")
    return (loss, dx[None], *[res[k][0] for k in order], *[res[k][1] for k in order],
            *[res[k][2] for k in order], *[res[k][3] for k in order])
```

```python
import math

import numpy as np
import jax
import jax.numpy as jnp
from jax import lax
from jax.experimental import pallas as pl
from jax.experimental.pallas import tpu as pltpu

F32 = jnp.float32
BF16 = jnp.bfloat16

SEQ = 2048
D_MODEL = 1024
CONV_WIDTH = 512
ATTN_WIDTH = 512
KV_WIDTH = 128
HEAD_DIM = 64
N_HEADS = 8
GQA_GROUP = 4
IN_WIDTH = 2304
D_FF = 2816
BLK = 128
NUM_BUCKETS = 32
EPS = 1e-6
NEG_INF = -1e30
ADAM_LR = 0.001
ADAM_B1 = 0.9
ADAM_B2 = 0.999
ADAM_EPS = 1e-08
ADAM_WD = 0.01
ADAM_STEP = 10

N_DEV = 8
FFN_BLK = 2 * D_FF // N_DEV
N_FFN_BLK = D_FF // FFN_BLK
SUBLANES = 8
VMEM_LIMIT = 56 * 1024 * 1024

_MESH = pl.DeviceIdType.MESH
_ANY = pl.BlockSpec(memory_space=pl.ANY)


def _params(sem):
    return pltpu.CompilerParams(dimension_semantics=sem, vmem_limit_bytes=VMEM_LIMIT)


def _ordered_behind(body, pos, after):
    if after is None:
        return body, [], []
    return (lambda *refs: body(*refs[:pos], *refs[pos + 1:])), [_ANY], [after]


def _dot(a, b):
    return jnp.dot(a, b, preferred_element_type=F32)


def _dot_nt(a, b):
    return lax.dot_general(a, b, (((1,), (1,)), ((), ())), preferred_element_type=F32)


def _dot_tn(a, b):
    return lax.dot_general(a, b, (((0,), (0,)), ((), ())), preferred_element_type=F32)


def _shift_down(x, s, halo):
    r = pltpu.roll(x, s, axis=0)
    hr = pltpu.roll(halo, s, axis=0)
    row = lax.broadcasted_iota(jnp.int32, halo.shape, 0)
    top = jnp.where(row < s, hr, r[:SUBLANES])
    return jnp.concatenate([top, r[SUBLANES:]], axis=0)


def _shift_up(x, s, halo):
    n = x.shape[0]
    r = pltpu.roll(x, n - s, axis=0)
    hr = pltpu.roll(halo, SUBLANES - s, axis=0)
    row = lax.broadcasted_iota(jnp.int32, halo.shape, 0)
    bot = jnp.where(row >= SUBLANES - s, hr, r[n - SUBLANES:])
    return jnp.concatenate([r[:n - SUBLANES], bot], axis=0)


def _taps(w):
    return (w[0], w[1], w[2]) if len(w.shape) == 3 else (w[0:1], w[1:2], w[2:3])


def _conv3(x, w, halo):
    x2 = _shift_down(x, 2, halo)
    x1 = _shift_down(x, 1, halo)
    return x2 * w[0] + x1 * w[1] + x * w[2], x2, x1


def _conv3_bwd_input(dy, w, halo_next):
    return dy * w[2] + _shift_up(dy, 1, halo_next) * w[1] + _shift_up(dy, 2, halo_next) * w[0]


def _rstd(x):
    return lax.rsqrt(jnp.mean(x * x, axis=-1, keepdims=True) + EPS)


def _rms_bwd(x, g, dy):
    r = _rstd(x)
    n = x * r
    dn = dy * g
    dx = r * (dn - n * jnp.mean(dn * n, axis=-1, keepdims=True))
    return dx, jnp.sum(dy * n, axis=0, keepdims=True)


def _head_mean(x):
    width = x.shape[-1]
    ri = lax.broadcasted_iota(jnp.int32, (width, width), 0) // HEAD_DIM
    ci = lax.broadcasted_iota(jnp.int32, (width, width), 1) // HEAD_DIM
    ones = jnp.where(ri == ci, 1.0, 0.0).astype(BF16)
    hi = x.astype(BF16)
    lo = (x - hi.astype(F32)).astype(BF16)
    return (_dot(hi, ones) + _dot(lo, ones)) * (1.0 / HEAD_DIM)


def _head_norm(x, g, heads):
    return x * lax.rsqrt(_head_mean(x * x) + EPS) * jnp.tile(g, (1, heads))


def _head_norm_bwd(x, g, dy, heads):
    r = lax.rsqrt(_head_mean(x * x) + EPS)
    n = x * r
    dn = dy * jnp.tile(g, (1, heads))
    dx = r * (dn - n * _head_mean(dn * n))
    per_lane = jnp.sum(dy * n, axis=0, keepdims=True)
    dg = per_lane[:, 0:HEAD_DIM]
    for h in range(1, heads):
        dg = dg + per_lane[:, HEAD_DIM * h:HEAD_DIM * (h + 1)]
    return dx, dg


def _bucket_map():
    q = np.arange(BLK)[:, None]
    j = np.arange(BLK)[None, :]
    n = np.where(j > q, q + BLK - j, q - j)
    nf = np.maximum(n, 1).astype(np.float32)
    max_exact = NUM_BUCKETS // 2
    large = max_exact + (np.log(nf / max_exact) / math.log(BLK / max_exact) * (NUM_BUCKETS - max_exact)).astype(np.int32)
    large = np.minimum(large, NUM_BUCKETS - 1)
    return np.where(n < max_exact, n, large).astype(np.int32)


def _coords():
    return lax.axis_index("x"), lax.axis_index("y"), lax.axis_index("c")


def _lin(px, py, pc):
    return 4 * px + 2 * py + pc


def _chips(x, y):
    return [(1 - x, y), (x, 1 - y), (1 - x, 1 - y)]


def _peers(x, y, c):
    return [(1 - x if r & 4 else x, 1 - y if r & 2 else y, 1 - c if r & 1 else c) for r in range(1, N_DEV)]


_HBM = pl.BlockSpec(memory_space=pltpu.HBM)
_SEM = pl.BlockSpec(memory_space=pltpu.SEMAPHORE)
_EFFECT = pltpu.SideEffectType.DATAFLOW_SIDE_EFFECTING


def _in_hbm(a):
    return pltpu.with_memory_space_constraint(a, pltpu.HBM)


_SIBLING = (1, lambda x, y, c: [(x, y, 1 - c)])
_SIBLING_AND_CHIPS = (2, lambda x, y, c: [(x, y, 1 - c)] + [(cx, cy, c) for cx, cy in _chips(x, y)])
_SIBLING_AND_NEIGHBOURS = (3, lambda x, y, c: [(x, y, 1 - c), (1 - x, y, c), (x, 1 - y, c)])
_ONWARD_AND_SIBLING = (4, lambda x, y, c: [(jnp.where(c == 1, x, 1 - x), jnp.where(c == 1, 1 - y, y), c), (x, y, 1 - c)])
_ALL_FOR_W_DOWN = (5, lambda x, y, c: _peers(x, y, c))
_CHIPS = (6, lambda x, y, c: [(cx, cy, c) for cx, cy in _chips(x, y)])
_ALL_FOR_SMALL = (7, lambda x, y, c: _peers(x, y, c))


def _split_start(name, srcs, lands, plan, after, handshake):
    ns, nl = len(srcs), len(lands)
    n_copies = len(plan(0, 0, 0))
    n_after = 0 if after is None else 1
    collective_id, peers_of = handshake

    def body(*refs):
        src_refs, land_refs = refs[:ns + nl], refs[ns:ns + nl]
        send_sems, recv_sems = refs[ns + nl + n_after], refs[ns + nl + n_after + 1]
        token = refs[-1]
        barrier = pltpu.get_barrier_semaphore()
        peers = peers_of(*_coords())
        for peer in peers:
            pl.semaphore_signal(barrier, inc=1, device_id=peer, device_id_type=_MESH)
        pl.semaphore_wait(barrier, len(peers))
        for k, (a, s_slot, l, d_slot, dev) in enumerate(plan(*_coords())):
            src = src_refs[a] if s_slot is None else src_refs[a].at[s_slot]
            pltpu.make_async_remote_copy(src_ref=src, dst_ref=land_refs[l].at[d_slot], send_sem=send_sems.at[k],
                                         recv_sem=recv_sems.at[k], device_id=dev, device_id_type=_MESH).start()
        token[...] = jnp.zeros_like(token)

    arrs = list(srcs) + list(lands)
    out = pl.pallas_call(
        body, name=name,
        out_shape=(pltpu.SemaphoreType.DMA((n_copies,)), pltpu.SemaphoreType.DMA((n_copies,)),
                   *[pltpu.HBM(a.shape, a.dtype) for a in arrs], jax.ShapeDtypeStruct((SUBLANES, 128), F32)),
        in_specs=[_HBM] * (ns + nl) + [_ANY] * n_after,
        out_specs=(_SEM, _SEM, *[_HBM] * (ns + nl), pl.BlockSpec(memory_space=pltpu.VMEM)),
        input_output_aliases={i: 2 + i for i in range(ns + nl)},
        compiler_params=pltpu.CompilerParams(has_side_effects=_EFFECT, collective_id=collective_id),
    )(*[_in_hbm(a) for a in arrs], *([] if after is None else [after]))
    return out[0], out[1], list(out[2:2 + ns]), list(out[2 + ns:2 + ns + nl]), out[-1]


def _split_wait(name, send_sems, recv_sems, srcs, lands, plan, recv_slots, after):
    ns, nl = len(srcs), len(lands)

    def body(*refs):
        src_refs, land_refs = refs[:ns + nl], refs[ns:ns + nl]
        send_sems, recv_sems = refs[ns + nl], refs[ns + nl + 1]
        coords = _coords()
        slots = recv_slots(*coords)
        for k, (a, s_slot, l, _, dev) in enumerate(plan(*coords)):
            src = src_refs[a] if s_slot is None else src_refs[a].at[s_slot]
            cp = pltpu.make_async_remote_copy(src_ref=src, dst_ref=land_refs[l].at[slots[k]], send_sem=send_sems.at[k],
                                              recv_sem=recv_sems.at[k], device_id=dev, device_id_type=_MESH)
            cp.wait_send()
            cp.wait_recv()

    arrs = list(srcs) + list(lands)
    out = pl.pallas_call(
        body, name=name,
        out_shape=tuple(pltpu.HBM(a.shape, a.dtype) for a in arrs),
        in_specs=[_HBM] * (ns + nl) + [_SEM, _SEM, _ANY],
        out_specs=tuple([_HBM] * (ns + nl)),
        input_output_aliases={i: i for i in range(ns + nl)},
        compiler_params=pltpu.CompilerParams(has_side_effects=_EFFECT),
    )(*arrs, send_sems, recv_sems, after)
    return list(out[:ns]), list(out[ns:])


def _gather_plan_ici(n):
    def plan(x, y, c):
        me = _lin(x, y, c)
        out = []
        for a in range(n):
            out.append((a, me, a, me, (x, y, 1 - c)))
            out += [(a, me, a, me, (cx, cy, c)) for cx, cy in _chips(x, y)]
        return out

    def recv_slots(x, y, c):
        out = []
        for _ in range(n):
            out.append(_lin(x, y, 1 - c))
            out += [_lin(cx, cy, c) for cx, cy in _chips(x, y)]
        return out

    return plan, recv_slots


def _gather_plan_d2d(n):
    def plan(x, y, c):
        return [(a, _lin(cx, cy, c), a, _lin(cx, cy, c), (x, y, 1 - c)) for a in range(n) for cx, cy in _chips(x, y)]

    def recv_slots(x, y, c):
        return [_lin(cx, cy, 1 - c) for _ in range(n) for cx, cy in _chips(x, y)]

    return plan, recv_slots


def _all_gather_split(lands, tag, after):
    n = len(lands)
    plan1, slots1 = _gather_plan_ici(n)
    s1, r1, _, lands, token = _split_start(f"gather_{tag}_ici_start", [], lands, plan1, after, _SIBLING_AND_CHIPS)

    def finish(after):
        _, got = _split_wait(f"gather_{tag}_ici_wait", s1, r1, [], lands, plan1, slots1, after)
        plan2, slots2 = _gather_plan_d2d(n)
        s2, r2, _, got, token2 = _split_start(f"gather_{tag}_d2d_start", [], got, plan2, None, _SIBLING)
        return _split_wait(f"gather_{tag}_d2d_wait", s2, r2, [], got, plan2, slots2, token2)[1]

    return finish, token


def _all_gather_tree(lands, tag, after):
    n = len(lands)

    def plan1(x, y, c):
        me = _lin(x, y, c)
        return [(a, me, a, me, dev) for a in range(n) for dev in ((x, y, 1 - c), (1 - x, y, c), (x, 1 - y, c))]

    def slots1(x, y, c):
        return [s for _ in range(n) for s in (_lin(x, y, 1 - c), _lin(1 - x, y, c), _lin(x, 1 - y, c))]

    def plan2(x, y, c):
        from_x, from_y = _lin(1 - x, y, c), _lin(x, 1 - y, c)
        north = c == 1
        passed = jnp.where(north, from_x, from_y)
        onward = (jnp.where(north, x, 1 - x), jnp.where(north, 1 - y, y), c)
        sib = (x, y, 1 - c)
        return [cp for a in range(n) for cp in ((a, passed, a, passed, onward), (a, from_x, a, from_x, sib),
                                                (a, from_y, a, from_y, sib))]

    def slots2(x, y, c):
        return [s for _ in range(n) for s in (_lin(1 - x, 1 - y, c), _lin(1 - x, y, 1 - c), _lin(x, 1 - y, 1 - c))]

    def plan3(x, y, c):
        diag = _lin(1 - x, 1 - y, c)
        return [(a, diag, a, diag, (x, y, 1 - c)) for a in range(n)]

    def slots3(x, y, c):
        return [_lin(1 - x, 1 - y, 1 - c)] * n

    s1, r1, _, lands, token = _split_start(f"gather_{tag}_1_start", [], lands, plan1, after, _SIBLING_AND_NEIGHBOURS)
    state = {}

    def stage2(after):
        _, got = _split_wait(f"gather_{tag}_1_wait", s1, r1, [], lands, plan1, slots1, after)
        state["s"], state["r"], _, state["lands"], token2 = _split_start(f"gather_{tag}_2_start", [], got, plan2, None,
                                                                         _ONWARD_AND_SIBLING)
        return token2

    def stage3(after):
        _, got = _split_wait(f"gather_{tag}_2_wait", state["s"], state["r"], [], state["lands"], plan2, slots2, after)
        s3, r3, _, got, token3 = _split_start(f"gather_{tag}_3_start", [], got, plan3, None, _SIBLING)
        return _split_wait(f"gather_{tag}_3_wait", s3, r3, [], got, plan3, slots3, token3)[1]

    return stage2, stage3, token


_CHIP_LIST = ((0, 0), (0, 1), (1, 0), (1, 1))


def _reduce_plan_d2d(n):
    def plan(x, y, c):
        return [(a, _lin(qx, qy, 1 - c), a, q, (x, y, 1 - c)) for a in range(n) for q, (qx, qy) in enumerate(_CHIP_LIST)]

    def recv_slots(x, y, c):
        return [q for _ in range(n) for q in range(4)]

    return plan, recv_slots


def _reduce_plan_ici(n):
    def plan(x, y, c):
        return [(a, 2 * cx + cy, a, j, (cx, cy, c)) for a in range(n) for j, (cx, cy) in enumerate(_chips(x, y))]

    def recv_slots(x, y, c):
        return [j for _ in range(n) for j in range(3)]

    return plan, recv_slots


def _scatter_plan(n):
    def plan(x, y, c):
        return [(a, _lin(*peer), a, r, peer) for a in range(n) for r, peer in enumerate(_peers(x, y, c))]

    def recv_slots(x, y, c):
        return [r for _ in range(n) for r in range(N_DEV - 1)]

    return plan, recv_slots


def _broadcast_plan():
    def plan(x, y, c):
        return [(0, None, 0, _lin(x, y, c), peer) for peer in _peers(x, y, c)]

    def recv_slots(x, y, c):
        return [_lin(*peer) for peer in _peers(x, y, c)]

    return plan, recv_slots


def _chip_partial(grads, recvd, core, name):
    n = len(grads)

    def body(c_ref, *refs):
        for a in range(n):
            g_ref, r_ref, o_ref = refs[a], refs[n + a], refs[2 * n + a]
            o_ref[...] = (g_ref[...].astype(F32) + r_ref[...].astype(F32)).astype(o_ref.dtype)

    def blk(a, own):
        zeros = (0,) * (a.ndim - 1)
        return pl.BlockSpec((None,) + a.shape[1:],
                            (lambda q, c_ref: (2 * q + c_ref[0],) + zeros) if own else (lambda q, c_ref: (q,) + zeros))

    return pl.pallas_call(
        body, name=name,
        grid_spec=pltpu.PrefetchScalarGridSpec(
            num_scalar_prefetch=1, grid=(4,),
            in_specs=[blk(a, True) for a in grads] + [blk(a, False) for a in recvd],
            out_specs=[blk(a, False) for a in recvd]),
        out_shape=[jax.ShapeDtypeStruct(a.shape, a.dtype) for a in recvd],
        compiler_params=_params(("arbitrary",)),
    )(core, *grads, *recvd)


def _reduce_scatter_split(grads, tag, core, behind):
    n = len(grads)
    plan1, slots1 = _reduce_plan_d2d(n)
    lands1 = [lax.empty((4,) + a.shape[1:], a.dtype) for a in grads]
    s1, r1, srcs1, lands1, token1 = _split_start(f"reduce_{tag}_d2d_start", grads, lands1, plan1, None, _SIBLING)
    own, got = _split_wait(f"reduce_{tag}_d2d_wait", s1, r1, srcs1, lands1, plan1, slots1, behind(token1))
    parts = _chip_partial(own, got, core, f"reduce_{tag}_partial")
    plan2, slots2 = _reduce_plan_ici(n)
    lands2 = [lax.empty((3,) + a.shape[1:], a.dtype) for a in grads]
    s2, r2, srcs2, lands2, token2 = _split_start(f"reduce_{tag}_ici_start", parts, lands2, plan2, None, _CHIPS)

    def finish(after):
        return _split_wait(f"reduce_{tag}_ici_wait", s2, r2, srcs2, lands2, plan2, slots2, after)

    return finish, token2


def _place_shards(me, shards, dtypes, name, after=None):
    n = len(shards)

    def body(me_ref, *refs):
        for a in range(n):
            refs[n + a][...] = refs[a][...].astype(dtypes[a])

    full = lambda s: pl.BlockSpec(s.shape, lambda i, me_ref: (0,) * s.ndim)
    slot = lambda s: pl.BlockSpec((None,) + s.shape, lambda i, me_ref: (me_ref[0],) + (0,) * s.ndim)
    body, more_specs, more = _ordered_behind(body, 1 + n, after)
    return pl.pallas_call(
        body, name=name,
        grid_spec=pltpu.PrefetchScalarGridSpec(num_scalar_prefetch=1, grid=(1,),
                                               in_specs=[full(s) for s in shards] + more_specs,
                                               out_specs=[slot(s) for s in shards]),
        out_shape=[jax.ShapeDtypeStruct((N_DEV,) + s.shape, d) for s, d in zip(shards, dtypes)],
        compiler_params=_params(("arbitrary",)),
    )(me, *shards, *more)


def _mix_in_fwd(x, g1, w_in_t, conv_w, gq, gk, gconv):
    tm = 512
    n_t = SEQ // tm

    def body(x_ref, g1_ref, w_ref, cw_ref, gq_ref, gk_ref, gc_ref,
             proj_ref, u1_ref, ycn_ref, qn_ref, kn_ref, v_ref, halo_ref):
        @pl.when(pl.program_id(0) == 0)
        def _():
            halo_ref[...] = jnp.zeros_like(halo_ref)

        xv = x_ref[...]
        u = (xv * _rstd(xv) * g1_ref[...]).astype(BF16)
        u1_ref[...] = u
        proj = _dot_nt(u, w_ref[...])
        proj_ref[...] = proj
        gate_b = proj[:, 0:CONV_WIDTH]
        a = proj[:, CONV_WIDTH:2 * CONV_WIDTH] * proj[:, 2 * CONV_WIDTH:3 * CONV_WIDTH]
        cv, _, _ = _conv3(a, _taps(cw_ref[...]), halo_ref[...])
        halo_ref[...] = a[tm - SUBLANES:]
        yc = gate_b * cv
        ycn_ref[...] = (yc * _rstd(yc) * gc_ref[...]).astype(BF16)
        q0 = 3 * CONV_WIDTH
        qn_ref[...] = _head_norm(proj[:, q0:q0 + ATTN_WIDTH], gq_ref[...], N_HEADS).astype(BF16)
        k0 = q0 + ATTN_WIDTH
        kn_ref[...] = _head_norm(proj[:, k0:k0 + KV_WIDTH], gk_ref[...], 2).astype(BF16)
        v_ref[...] = proj[:, k0 + KV_WIDTH:k0 + 2 * KV_WIDTH].astype(BF16)

    const = lambda shape: pl.BlockSpec(shape, lambda i: (0,) * len(shape))
    rows = lambda w: pl.BlockSpec((tm, w), lambda i: (i, 0))
    return pl.pallas_call(
        body, name="mix_in_fwd", grid=(n_t,),
        in_specs=[rows(D_MODEL), const((1, D_MODEL)), const((IN_WIDTH, D_MODEL)), const((3, CONV_WIDTH)),
                  const((1, HEAD_DIM)), const((1, HEAD_DIM)), const((1, CONV_WIDTH))],
        out_specs=[rows(IN_WIDTH), rows(D_MODEL), rows(CONV_WIDTH), rows(ATTN_WIDTH), rows(KV_WIDTH), rows(KV_WIDTH)],
        out_shape=[jax.ShapeDtypeStruct((SEQ, IN_WIDTH), F32), jax.ShapeDtypeStruct((SEQ, D_MODEL), BF16),
                   jax.ShapeDtypeStruct((SEQ, CONV_WIDTH), BF16),
                   jax.ShapeDtypeStruct((SEQ, ATTN_WIDTH), BF16), jax.ShapeDtypeStruct((SEQ, KV_WIDTH), BF16),
                   jax.ShapeDtypeStruct((SEQ, KV_WIDTH), BF16)],
        scratch_shapes=[pltpu.VMEM((SUBLANES, CONV_WIDTH), F32)],
        compiler_params=_params(("arbitrary",)),
    )(x, g1, w_in_t, conv_w, gq, gk, gconv)


GROUP_ROWS = GQA_GROUP * BLK


def _band_bias(tbl_ref, bkt, bias_ref):
    for h in range(N_HEADS):
        acc = jnp.zeros(bkt.shape, F32)
        for b in range(NUM_BUCKETS):
            acc = jnp.where(bkt == b, tbl_ref[h, b], acc)
        bias_ref[h // GQA_GROUP, BLK * (h % GQA_GROUP):BLK * (h % GQA_GROUP + 1), :] = acc


def _band_masks(i):
    qi = lax.broadcasted_iota(jnp.int32, (GROUP_ROWS, BLK), 0) & (BLK - 1)
    ji = lax.broadcasted_iota(jnp.int32, (GROUP_ROWS, BLK), 1)
    upper = ji > qi
    return upper, upper & (i == 0)


def _stack_heads(x, g):
    return jnp.concatenate([x[:, HEAD_DIM * h:HEAD_DIM * (h + 1)] for h in range(GQA_GROUP * g, GQA_GROUP * (g + 1))], axis=0)


def _unstack_heads(groups):
    return jnp.concatenate([p[BLK * t:BLK * (t + 1)] for p in groups for t in range(GQA_GROUP)], axis=-1)


def _per_head_rows(vals):
    row = lax.broadcasted_iota(jnp.int32, (GROUP_ROWS, 1), 0)
    col = jnp.full((GROUP_ROWS, 1), vals[GQA_GROUP - 1], F32)
    for t in range(GQA_GROUP - 2, -1, -1):
        col = jnp.where(row < BLK * (t + 1), vals[t], col)
    return col


def _band_rows(ref, i):
    prev = pl.multiple_of(jnp.maximum(i - 1, 0) * BLK, BLK)
    cur = pl.multiple_of(i * BLK, BLK)
    return jnp.concatenate([ref[pl.ds(prev, BLK), :], ref[pl.ds(cur, BLK), :]], axis=0), prev, cur


def _fold(band, upper):
    return jnp.where(upper, band[:, :BLK], band[:, BLK:])


def _unfold(tile, upper):
    return jnp.concatenate([jnp.where(upper, tile, 0.0), jnp.where(upper, 0.0, tile)], axis=1)


def _head_probs(qh, kh, bias, upper, dead, sink):
    logits = _fold(_dot_nt(qh, kh), upper) * (HEAD_DIM ** -0.5) + bias
    logits = jnp.where(dead, NEG_INF, logits)
    m = jnp.maximum(jnp.max(logits, axis=-1, keepdims=True), sink)
    p = jnp.exp(logits - m)
    es = jnp.exp(sink - m)
    den = jnp.sum(p, axis=-1, keepdims=True) + es
    return p / den, es / den


def _attn_fwd(qn, kn, v, tbl, sinks, bkt, gattn, after=None):
    n_b = SEQ // BLK

    def body(q_ref, k_ref, v_ref, tbl_ref, sink_ref, bkt_ref, ga_ref, y_ref, yn_ref, p_ref, ps_ref, bias_ref):
        i = pl.program_id(0)

        @pl.when(i == 0)
        def _():
            _band_bias(tbl_ref, bkt_ref[...], bias_ref)

        kb, _, _ = _band_rows(k_ref, i)
        vb, _, _ = _band_rows(v_ref, i)
        upper, dead = _band_masks(i)
        q = q_ref[...]
        lane = lax.broadcasted_iota(jnp.int32, (BLK, 128), 1)
        outs = []
        psinks = jnp.zeros((BLK, 128), F32)
        for g in range(N_HEADS // GQA_GROUP):
            kv = slice(HEAD_DIM * g, HEAD_DIM * (g + 1))
            sink = _per_head_rows([sink_ref[0, GQA_GROUP * g + t] for t in range(GQA_GROUP)])
            probs, psink = _head_probs(_stack_heads(q, g), kb[:, kv], bias_ref[g], upper, dead, sink)
            p_ref[g] = probs.astype(BF16)
            for t in range(GQA_GROUP):
                psinks = jnp.where(lane == GQA_GROUP * g + t, psink[BLK * t:BLK * (t + 1)], psinks)
            outs.append(_dot(_unfold(probs, upper).astype(BF16), vb[:, kv]))
        ps_ref[...] = psinks
        y = _unstack_heads(outs)
        y_ref[...] = y
        yn_ref[...] = (y * _rstd(y) * ga_ref[...]).astype(BF16)

    const = lambda shape: pl.BlockSpec(shape, lambda i: (0,) * len(shape))
    rows = lambda w: pl.BlockSpec((BLK, w), lambda i: (i, 0))
    smem = pl.BlockSpec(memory_space=pltpu.SMEM)
    body, more_specs, more = _ordered_behind(body, 7, after)
    return pl.pallas_call(
        body, name="attn_fwd", grid=(n_b,),
        in_specs=[rows(ATTN_WIDTH), const((SEQ, KV_WIDTH)), const((SEQ, KV_WIDTH)), smem, smem,
                  const((BLK, BLK)), const((1, ATTN_WIDTH))] + more_specs,
        out_specs=[rows(ATTN_WIDTH), rows(ATTN_WIDTH),
                   pl.BlockSpec((None, N_HEADS // GQA_GROUP, GROUP_ROWS, BLK), lambda i: (i, 0, 0, 0)), rows(128)],
        out_shape=[jax.ShapeDtypeStruct((SEQ, ATTN_WIDTH), F32), jax.ShapeDtypeStruct((SEQ, ATTN_WIDTH), BF16),
                   jax.ShapeDtypeStruct((n_b, N_HEADS // GQA_GROUP, GROUP_ROWS, BLK), BF16),
                   jax.ShapeDtypeStruct((SEQ, 128), F32)],
        scratch_shapes=[pltpu.VMEM((N_HEADS // GQA_GROUP, GROUP_ROWS, BLK), F32)],
        compiler_params=_params(("arbitrary",)),
    )(qn, kn, v, tbl, sinks, bkt, gattn, *more)


def _ffn_fwd(x, ycn, yan, w_out, g2, w_up, fcw, fcb, w_down, tgt):
    tm = 512
    n_t = SEQ // tm

    def body(x_ref, ycn_ref, yan_ref, wo_ref, g2_ref, wu_ref, cw_ref, b_ref, wd_ref, tgt_ref,
             h1_ref, u2_ref, up_ref, pre_ref, act_ref, dh2_ref, dh2b_ref, loss_ref, acc_ref, halo_ref):
        i, j = pl.program_id(0), pl.program_id(1)

        @pl.when((i == 0) & (j == 0))
        def _():
            loss_ref[...] = jnp.zeros_like(loss_ref)

        @pl.when(j == 0)
        def _():
            h1 = x_ref[...] + _dot(ycn_ref[...], wo_ref[0:CONV_WIDTH, :]) + _dot(yan_ref[...], wo_ref[CONV_WIDTH:, :])
            h1_ref[...] = h1
            u2_ref[...] = (h1 * _rstd(h1) * g2_ref[...]).astype(BF16)
            acc_ref[...] = jnp.zeros_like(acc_ref)

        u2 = u2_ref[...]
        pre = []
        for s in range(2):
            up = _dot_nt(u2, wu_ref[s])
            up_ref[s] = up.astype(BF16)
            halo = jnp.where(i == 0, 0.0, halo_ref[s, j])
            pre.append(_conv3(up, _taps(cw_ref.at[s]), halo)[0] + b_ref[s])
            pre_ref[s] = pre[s].astype(BF16)
            halo_ref[s, j] = up[tm - SUBLANES:]
        g, val = pre
        act = (g * jax.nn.sigmoid(g) * val).astype(BF16)
        act_ref[...] = act
        acc_ref[...] += _dot(act, wd_ref[...])

        @pl.when(j == N_FFN_BLK - 1)
        def _():
            err = h1_ref[...] + acc_ref[...] - tgt_ref[...]
            loss_ref[...] += 0.5 * jnp.sum(err * err) / D_MODEL
            dh2 = err / D_MODEL
            dh2_ref[...] = dh2
            dh2b_ref[...] = dh2.astype(BF16)

    rows = lambda w: pl.BlockSpec((tm, w), lambda i, j: (i, 0))
    const = lambda shape: pl.BlockSpec(shape, lambda i, j: (0,) * len(shape))
    pair = lambda *s: pl.BlockSpec((2, None) + s, lambda i, j: (0, j) + (0,) * len(s))
    upb = pl.BlockSpec((2, None, tm, FFN_BLK), lambda i, j: (0, j, i, 0))
    return pl.pallas_call(
        body, name="ffn_fwd", grid=(n_t, N_FFN_BLK),
        in_specs=[rows(D_MODEL), rows(CONV_WIDTH), rows(ATTN_WIDTH), const((D_MODEL, D_MODEL)), const((1, D_MODEL)),
                  pair(FFN_BLK, D_MODEL), pair(3, 1, FFN_BLK), pair(1, FFN_BLK),
                  pl.BlockSpec((None, FFN_BLK, D_MODEL), lambda i, j: (j, 0, 0)), rows(D_MODEL)],
        out_specs=[rows(D_MODEL), rows(D_MODEL), upb, upb, pl.BlockSpec((None, tm, FFN_BLK), lambda i, j: (j, i, 0)),
                   rows(D_MODEL), rows(D_MODEL), const((SUBLANES, 128))],
        out_shape=[jax.ShapeDtypeStruct((SEQ, D_MODEL), F32), jax.ShapeDtypeStruct((SEQ, D_MODEL), BF16),
                   jax.ShapeDtypeStruct((2, N_FFN_BLK, SEQ, FFN_BLK), BF16),
                   jax.ShapeDtypeStruct((2, N_FFN_BLK, SEQ, FFN_BLK), BF16),
                   jax.ShapeDtypeStruct((N_FFN_BLK, SEQ, FFN_BLK), BF16),
                   jax.ShapeDtypeStruct((SEQ, D_MODEL), F32), jax.ShapeDtypeStruct((SEQ, D_MODEL), BF16),
                   jax.ShapeDtypeStruct((SUBLANES, 128), F32)],
        scratch_shapes=[pltpu.VMEM((tm, D_MODEL), F32), pltpu.VMEM((2, N_FFN_BLK, SUBLANES, FFN_BLK), F32)],
        compiler_params=_params(("arbitrary", "arbitrary")),
    )(x, ycn, yan, w_out, g2, w_up, fcw, fcb, w_down, tgt)


def _ffn_bwd(dh2, dh2b, h1, g2, up, pre, w_up, fcw, w_down, after=None):
    tm = 512
    n_t = SEQ // tm

    def body(dh2_ref, dh2b_ref, h1_ref, g2_ref, up_ref, pre_ref, wu_ref, cw_ref, wd_ref,
             dup_ref, dh1_ref, dh1b_ref, dfb_ref, dfcw_ref, dg2_ref, acc_ref, next_ref):
        i, j = pl.program_id(0), pl.program_id(1)

        @pl.when((i == 0) & (j == 0))
        def _():
            dfb_ref[...] = jnp.zeros_like(dfb_ref)
            dfcw_ref[...] = jnp.zeros_like(dfcw_ref)
            dg2_ref[...] = jnp.zeros_like(dg2_ref)

        @pl.when(j == 0)
        def _():
            acc_ref[...] = jnp.zeros_like(acc_ref)

        g, val = pre_ref[0].astype(F32), pre_ref[1].astype(F32)
        sg = jax.nn.sigmoid(g)
        silu = g * sg
        dact = _dot_nt(dh2b_ref[...], wd_ref[...])
        dpre = (dact * val * (sg * (1.0 + g * (1.0 - sg))), dact * silu)
        for s in range(2):
            d = dpre[s]
            u = up_ref[s].astype(F32)
            w = _taps(cw_ref.at[s])
            nxt = jnp.where(i == 0, 0.0, next_ref[s, j])
            d1 = _shift_up(d, 1, nxt)
            d2 = _shift_up(d, 2, nxt)
            next_ref[s, j] = d[:SUBLANES]
            dfb_ref[s, j] += jnp.sum(d, axis=0, keepdims=True)
            dfcw_ref[s, j, 0] += jnp.sum(d2 * u, axis=0, keepdims=True)
            dfcw_ref[s, j, 1] += jnp.sum(d1 * u, axis=0, keepdims=True)
            dfcw_ref[s, j, 2] += jnp.sum(d * u, axis=0, keepdims=True)
            dup = (d * w[2] + d1 * w[1] + d2 * w[0]).astype(BF16)
            dup_ref[s] = dup
            acc_ref[...] += _dot(dup, wu_ref[s])

        @pl.when(j == N_FFN_BLK - 1)
        def _():
            dn, dgain = _rms_bwd(h1_ref[...], g2_ref[...], acc_ref[...])
            dh1 = dh2_ref[...] + dn
            dh1_ref[...] = dh1
            dh1b_ref[...] = dh1.astype(BF16)
            dg2_ref[...] += dgain

    rev = lambda i: n_t - 1 - i
    rows = lambda w: pl.BlockSpec((tm, w), lambda i, j: (rev(i), 0))
    const = lambda shape: pl.BlockSpec(shape, lambda i, j: (0,) * len(shape))
    pair = lambda *s: pl.BlockSpec((2, None) + s, lambda i, j: (0, j) + (0,) * len(s))
    upb = pl.BlockSpec((2, None, tm, FFN_BLK), lambda i, j: (0, j, rev(i), 0))
    body, more_specs, more = _ordered_behind(body, 9, after)
    return pl.pallas_call(
        body, name="ffn_bwd", grid=(n_t, N_FFN_BLK),
        in_specs=[rows(D_MODEL), rows(D_MODEL), rows(D_MODEL), const((1, D_MODEL)), upb, upb,
                  pair(FFN_BLK, D_MODEL), pair(3, 1, FFN_BLK),
                  pl.BlockSpec((None, FFN_BLK, D_MODEL), lambda i, j: (j, 0, 0))] + more_specs,
        out_specs=[upb, rows(D_MODEL), rows(D_MODEL),
                   const((2, N_FFN_BLK, 1, FFN_BLK)), const((2, N_FFN_BLK, 3, 1, FFN_BLK)), const((1, D_MODEL))],
        out_shape=[jax.ShapeDtypeStruct((2, N_FFN_BLK, SEQ, FFN_BLK), BF16), jax.ShapeDtypeStruct((SEQ, D_MODEL), F32),
                   jax.ShapeDtypeStruct((SEQ, D_MODEL), BF16), jax.ShapeDtypeStruct((2, N_FFN_BLK, 1, FFN_BLK), F32),
                   jax.ShapeDtypeStruct((2, N_FFN_BLK, 3, 1, FFN_BLK), F32), jax.ShapeDtypeStruct((1, D_MODEL), F32)],
        scratch_shapes=[pltpu.VMEM((tm, D_MODEL), F32), pltpu.VMEM((2, N_FFN_BLK, SUBLANES, FFN_BLK), F32)],
        compiler_params=_params(("arbitrary", "arbitrary")),
    )(dh2, dh2b, h1, g2, up, pre, w_up, fcw, w_down, *more)


def _grad_tn(a_list, b, out_rows, name, after=None):
    n = len(a_list)
    ncol = b.shape[1]

    def body(*refs):
        a_refs, b_ref, o_ref = refs[:n], refs[n], refs[n + 1]
        j = pl.program_id(0)
        for k in range(n):
            @pl.when(j == k)
            def _(k=k):
                o_ref[...] = _dot_tn(a_refs[k][...], b_ref[...]).astype(BF16)

    full = lambda shape: pl.BlockSpec(shape, lambda j: (0,) * len(shape))
    body, more_specs, more = _ordered_behind(body, n + 1, after)
    return pl.pallas_call(
        body, name=name, grid=(n,),
        in_specs=[full((SEQ, out_rows))] * n + [full((SEQ, ncol))] + more_specs,
        out_specs=pl.BlockSpec((None, out_rows, ncol), lambda j: (j, 0, 0)),
        out_shape=jax.ShapeDtypeStruct((n, out_rows, ncol), BF16),
        compiler_params=_params(("arbitrary",)),
    )(*a_list, b, *more)


def _grad_tn_blocked(a, b, name, per_step=2):
    nb, _, a_w = a.shape
    b_w = b.shape[-1]

    def body(a_ref, b_ref, o_ref):
        for p in range(per_step):
            o_ref[p] = _dot_tn(a_ref[p], b_ref[...]).astype(BF16)

    return pl.pallas_call(
        body, name=name, grid=(nb // per_step,),
        in_specs=[pl.BlockSpec((per_step, SEQ, a_w), lambda k: (k, 0, 0)), pl.BlockSpec((SEQ, b_w), lambda k: (0, 0))],
        out_specs=pl.BlockSpec((per_step, a_w, b_w), lambda k: (k, 0, 0)),
        out_shape=jax.ShapeDtypeStruct((nb, a_w, b_w), BF16),
        compiler_params=_params(("arbitrary",)),
    )(a, b)


def _out_bwd(dh1b, w_out, y_attn, gattn, after=None):
    tm = 1024
    n_t = SEQ // tm

    def body(dh_ref, wo_ref, y_ref, ga_ref, dycn_ref, dy_ref, dga_ref):
        @pl.when(pl.program_id(0) == 0)
        def _():
            dga_ref[...] = jnp.zeros_like(dga_ref)

        dycat = _dot_nt(dh_ref[...], wo_ref[...])
        dycn_ref[...] = dycat[:, :CONV_WIDTH]
        dy, dga = _rms_bwd(y_ref[...], ga_ref[...], dycat[:, CONV_WIDTH:])
        dy_ref[...] = dy
        dga_ref[...] += dga

    rows = lambda w: pl.BlockSpec((tm, w), lambda i: (i, 0))
    const = lambda shape: pl.BlockSpec(shape, lambda i: (0,) * len(shape))
    body, more_specs, more = _ordered_behind(body, 4, after)
    return pl.pallas_call(
        body, name="out_bwd", grid=(n_t,),
        in_specs=[rows(D_MODEL), const((D_MODEL, D_MODEL)), rows(ATTN_WIDTH), const((1, ATTN_WIDTH))] + more_specs,
        out_specs=[rows(CONV_WIDTH), rows(ATTN_WIDTH), const((1, ATTN_WIDTH))],
        out_shape=[jax.ShapeDtypeStruct((SEQ, CONV_WIDTH), F32), jax.ShapeDtypeStruct((SEQ, ATTN_WIDTH), F32),
                   jax.ShapeDtypeStruct((1, ATTN_WIDTH), F32)],
        compiler_params=_params(("arbitrary",)),
    )(dh1b, w_out, y_attn, gattn, *more)


def _attn_bwd(qn, kn, v, dy, probs, psinks, bkt, after=None):
    n_b = SEQ // BLK

    def body(q_ref, k_ref, v_ref, dy_ref, p_ref, ps_ref, bkt_ref,
             dq_ref, dk_ref, dv_ref, dtbl_ref, dsink_ref, dbias_ref, dsacc_ref):
        i = pl.program_id(0)

        @pl.when(i == 0)
        def _():
            dbias_ref[...] = jnp.zeros_like(dbias_ref)
            dsacc_ref[...] = jnp.zeros_like(dsacc_ref)
            dk_ref[...] = jnp.zeros_like(dk_ref)
            dv_ref[...] = jnp.zeros_like(dv_ref)

        kb, prev, cur = _band_rows(k_ref, i)
        vb, _, _ = _band_rows(v_ref, i)
        upper, _ = _band_masks(i)
        q = q_ref[...]
        dy = dy_ref[...]
        psink = ps_ref[...]
        lane = lax.broadcasted_iota(jnp.int32, (BLK, 128), 1)
        dsink = jnp.zeros((BLK, 128), F32)
        dqs, dks, dvs = [], [], []
        for g in range(N_HEADS // GQA_GROUP):
            kv = slice(HEAD_DIM * g, HEAD_DIM * (g + 1))
            qg = _stack_heads(q, g)
            dog = _stack_heads(dy, g).astype(BF16)
            pb = p_ref[g]
            pg = pb.astype(F32)
            dprobs = _fold(_dot_nt(dog, vb[:, kv]), upper)
            dvs.append(_dot_tn(_unfold(pb, upper), dog))
            dsum = jnp.sum(pg * dprobs, axis=-1, keepdims=True)
            dlogits = pg * (dprobs - dsum)
            for t in range(GQA_GROUP):
                dsink = jnp.where(lane == GQA_GROUP * g + t, -psink * dsum[BLK * t:BLK * (t + 1)], dsink)
            dbias_ref[g] += dlogits
            ds = _unfold(dlogits * (HEAD_DIM ** -0.5), upper).astype(BF16)
            dqs.append(_dot(ds, kb[:, kv]))
            dks.append(_dot_tn(ds, qg))
        dsacc_ref[...] += dsink
        dq_ref[...] = _unstack_heads(dqs)
        dkb = jnp.concatenate(dks, axis=-1)
        dvb = jnp.concatenate(dvs, axis=-1)
        dk_ref[pl.ds(prev, BLK), :] += dkb[:BLK]
        dk_ref[pl.ds(cur, BLK), :] += dkb[BLK:]
        dv_ref[pl.ds(prev, BLK), :] += dvb[:BLK]
        dv_ref[pl.ds(cur, BLK), :] += dvb[BLK:]

        @pl.when(i == n_b - 1)
        def _():
            bkt = bkt_ref[...]
            row8 = lax.broadcasted_iota(jnp.int32, (N_HEADS, 128), 0)
            lane8 = lax.broadcasted_iota(jnp.int32, (N_HEADS, 128), 1)
            acc = jnp.zeros((N_HEADS, 128), F32)
            for h in range(N_HEADS):
                rows = slice(BLK * (h % GQA_GROUP), BLK * (h % GQA_GROUP + 1))
                dbh = dbias_ref[h // GQA_GROUP, rows, :]
                for b in range(NUM_BUCKETS):
                    acc = jnp.where((row8 == h) & (lane8 == b), jnp.sum(jnp.where(bkt == b, dbh, 0.0)), acc)
            dsink_ref[...] = jnp.sum(dsacc_ref[...], axis=0, keepdims=True)
            dtbl_ref[...] = acc

    const = lambda shape: pl.BlockSpec(shape, lambda i: (0,) * len(shape))
    rows = lambda w: pl.BlockSpec((BLK, w), lambda i: (i, 0))
    n_g = N_HEADS // GQA_GROUP
    body, more_specs, more = _ordered_behind(body, 7, after)
    return pl.pallas_call(
        body, name="attn_bwd", grid=(n_b,),
        in_specs=[rows(ATTN_WIDTH), const((SEQ, KV_WIDTH)), const((SEQ, KV_WIDTH)), rows(ATTN_WIDTH),
                  pl.BlockSpec((None, n_g, GROUP_ROWS, BLK), lambda i: (i, 0, 0, 0)), rows(128),
                  const((BLK, BLK))] + more_specs,
        out_specs=[rows(ATTN_WIDTH), const((SEQ, KV_WIDTH)), const((SEQ, KV_WIDTH)), const((N_HEADS, 128)), const((1, 128))],
        out_shape=[jax.ShapeDtypeStruct((SEQ, ATTN_WIDTH), F32), jax.ShapeDtypeStruct((SEQ, KV_WIDTH), F32),
                   jax.ShapeDtypeStruct((SEQ, KV_WIDTH), F32), jax.ShapeDtypeStruct((N_HEADS, 128), F32),
                   jax.ShapeDtypeStruct((1, 128), F32)],
        scratch_shapes=[pltpu.VMEM((n_g, GROUP_ROWS, BLK), F32), pltpu.VMEM((BLK, 128), F32)],
        compiler_params=_params(("arbitrary",)),
    )(qn, kn, v, dy, probs, psinks, bkt, *more)


def _mix_in_bwd(x, dh1, proj, dycn, dqn, dkn, dv, w_in_t, conv_w, g1, gq, gk, gconv):
    tm = 512
    n_t = SEQ // tm
    halo_blocks = tm // SUBLANES

    def body(x_ref, dh1_ref, proj_ref, halo_ref, dycn_ref, dqn_ref, dkn_ref, dv_ref, w_ref, cw_ref,
             g1_ref, gq_ref, gk_ref, gc_ref,
             dx_ref, dproj_ref, dcw_ref, dgc_ref, dgq_ref, dgk_ref, dg1_ref, next_ref):
        i = pl.program_id(0)
        first_tile = i == n_t - 1

        @pl.when(i == 0)
        def _():
            for r in (dcw_ref, dgc_ref, dgq_ref, dgk_ref, dg1_ref, next_ref):
                r[...] = jnp.zeros_like(r)

        proj = proj_ref[...]
        hp = halo_ref[...]
        gate_b = proj[:, 0:CONV_WIDTH]
        gate_c = proj[:, CONV_WIDTH:2 * CONV_WIDTH]
        hc = proj[:, 2 * CONV_WIDTH:3 * CONV_WIDTH]
        a = gate_c * hc
        a_halo = jnp.where(first_tile, 0.0, hp[:, CONV_WIDTH:2 * CONV_WIDTH] * hp[:, 2 * CONV_WIDTH:3 * CONV_WIDTH])
        cw = _taps(cw_ref[...])
        cv, a2, a1 = _conv3(a, cw, a_halo)
        dyc, dgc = _rms_bwd(gate_b * cv, gc_ref[...], dycn_ref[...])
        dgc_ref[...] += dgc
        dcv = dyc * gate_b
        dcw_ref[...] += jnp.concatenate(
            [jnp.sum(dcv * a2, axis=0, keepdims=True), jnp.sum(dcv * a1, axis=0, keepdims=True),
             jnp.sum(dcv * a, axis=0, keepdims=True)], axis=0)
        da = _conv3_bwd_input(dcv, cw, next_ref[...])
        next_ref[...] = dcv[:SUBLANES]
        q0 = 3 * CONV_WIDTH
        k0 = q0 + ATTN_WIDTH
        dq, dgq = _head_norm_bwd(proj[:, q0:k0], gq_ref[...], dqn_ref[...], N_HEADS)
        dk, dgk = _head_norm_bwd(proj[:, k0:k0 + KV_WIDTH], gk_ref[...], dkn_ref[...], 2)
        dgq_ref[...] += dgq
        dgk_ref[...] += dgk
        dproj = jnp.concatenate([dyc * cv, da * hc, da * gate_c, dq, dk, dv_ref[...]], axis=-1).astype(BF16)
        dproj_ref[...] = dproj
        du1 = _dot(dproj, w_ref[...])
        xv = x_ref[...]
        dn, dg1 = _rms_bwd(xv, g1_ref[...], du1)
        dx_ref[...] = dh1_ref[...] + dn
        dg1_ref[...] += dg1

    rev = lambda i: n_t - 1 - i
    rows = lambda w: pl.BlockSpec((tm, w), lambda i: (rev(i), 0))
    const = lambda shape: pl.BlockSpec(shape, lambda i: (0,) * len(shape))
    halo = pl.BlockSpec((SUBLANES, IN_WIDTH), lambda i: (jnp.maximum(rev(i) * halo_blocks - 1, 0), 0))
    return pl.pallas_call(
        body, name="mix_in_bwd", grid=(n_t,),
        in_specs=[rows(D_MODEL), rows(D_MODEL), rows(IN_WIDTH), halo, rows(CONV_WIDTH), rows(ATTN_WIDTH), rows(KV_WIDTH),
                  rows(KV_WIDTH), const((IN_WIDTH, D_MODEL)), const((3, CONV_WIDTH)), const((1, D_MODEL)),
                  const((1, HEAD_DIM)), const((1, HEAD_DIM)), const((1, CONV_WIDTH))],
        out_specs=[rows(D_MODEL), rows(IN_WIDTH), const((3, CONV_WIDTH)), const((1, CONV_WIDTH)),
                   const((1, HEAD_DIM)), const((1, HEAD_DIM)), const((1, D_MODEL))],
        out_shape=[jax.ShapeDtypeStruct((SEQ, D_MODEL), F32), jax.ShapeDtypeStruct((SEQ, IN_WIDTH), BF16),
                   jax.ShapeDtypeStruct((3, CONV_WIDTH), F32),
                   jax.ShapeDtypeStruct((1, CONV_WIDTH), F32), jax.ShapeDtypeStruct((1, HEAD_DIM), F32),
                   jax.ShapeDtypeStruct((1, HEAD_DIM), F32), jax.ShapeDtypeStruct((1, D_MODEL), F32)],
        scratch_shapes=[pltpu.VMEM((SUBLANES, CONV_WIDTH), F32)],
        compiler_params=_params(("arbitrary",)),
    )(x, dh1, proj, proj, dycn, dqn, dkn, dv, w_in_t, conv_w, g1, gq, gk, gconv)


def _grad_w_in(dproj, u1, after=None):
    bw = 768

    def body(a_ref, b_ref, o_ref):
        o_ref[...] = _dot_tn(a_ref[...], b_ref[...]).astype(BF16)

    body, more_specs, more = _ordered_behind(body, 2, after)
    return pl.pallas_call(
        body, name="grad_w_in", grid=(IN_WIDTH // bw,),
        in_specs=[pl.BlockSpec((SEQ, bw), lambda k: (0, k)), pl.BlockSpec((SEQ, D_MODEL), lambda k: (0, 0))] + more_specs,
        out_specs=pl.BlockSpec((bw, D_MODEL), lambda k: (k, 0)),
        out_shape=jax.ShapeDtypeStruct((IN_WIDTH, D_MODEL), BF16),
        compiler_params=_params(("arbitrary",)),
    )(dproj, u1, *more)


def _adamw_math(w, g, m, v):
    m = ADAM_B1 * m + (1.0 - ADAM_B1) * g
    v = ADAM_B2 * v + (1.0 - ADAM_B2) * (g * g)
    m_hat = m / (1.0 - ADAM_B1 ** ADAM_STEP)
    v_hat = v / (1.0 - ADAM_B2 ** ADAM_STEP)
    return -ADAM_LR * (m_hat / (jnp.sqrt(v_hat) + ADAM_EPS) + ADAM_WD * w), m, v


_ROW_G1, _ROW_G2, _ROW_OUT_NORMS, _ROW_FFN_B, _ROW_GQ, _ROW_GK, _ROW_SINKS, _ROW_LOSS, _ROW_TABLE = 0, 1, 2, 3, 11, 12, 13, 14, 16
SMALL_ROWS, SMALL_COLS = 24, 1024
_SMALL_NAMES = ("norm_mix_g", "norm_ffn_g", "out_norm_conv_g", "out_norm_attn_g", "ffn_conv_b", "q_norm_g", "k_norm_g",
                "sinks", "rel_bias_table")


def _pack_small_grads(dg1, dg2, dgconv, dgattn, dfb, dgq, dgk, dsinks, dtbl_t, loss_acc):
    def body(dg1_ref, dg2_ref, dgc_ref, dga_ref, dfb_ref, dgq_ref, dgk_ref, ds_ref, dt_ref, loss_ref, o_ref, all_ref):
        o_ref[...] = jnp.zeros_like(o_ref)
        o_ref[_ROW_G1:_ROW_G1 + 1, :] = dg1_ref[...]
        o_ref[_ROW_G2:_ROW_G2 + 1, :] = dg2_ref[...]
        o_ref[_ROW_OUT_NORMS:_ROW_OUT_NORMS + 1, 0:CONV_WIDTH] = dgc_ref[...]
        o_ref[_ROW_OUT_NORMS:_ROW_OUT_NORMS + 1, CONV_WIDTH:] = dga_ref[...]
        for k in range(N_DEV):
            o_ref[_ROW_FFN_B + k:_ROW_FFN_B + k + 1, 0:FFN_BLK] = dfb_ref[k // N_FFN_BLK, k % N_FFN_BLK]
        o_ref[_ROW_GQ:_ROW_GQ + 1, 0:HEAD_DIM] = dgq_ref[...]
        o_ref[_ROW_GK:_ROW_GK + 1, 0:HEAD_DIM] = dgk_ref[...]
        o_ref[_ROW_SINKS:_ROW_SINKS + 1, 0:128] = ds_ref[...]
        o_ref[_ROW_LOSS:_ROW_LOSS + 1, 0:128] = loss_ref[0:1, :]
        o_ref[_ROW_TABLE:_ROW_TABLE + N_HEADS, 0:128] = dt_ref[...]
        for s in range(N_DEV):
            all_ref[s] = o_ref[...]

    return pl.pallas_call(
        body, name="pack_small_grads",
        out_shape=[jax.ShapeDtypeStruct((SMALL_ROWS, SMALL_COLS), F32),
                   jax.ShapeDtypeStruct((N_DEV, SMALL_ROWS, SMALL_COLS), F32)],
    )(dg1, dg2, dgconv, dgattn, dfb, dgq, dgk, dsinks, dtbl_t, loss_acc)


def _adamw_small(recv, params, after):
    names = _SMALL_NAMES
    n = len(names)

    def grad_of(g, name, k=None):
        if name == "norm_mix_g":
            return g[_ROW_G1:_ROW_G1 + 1, :]
        if name == "norm_ffn_g":
            return g[_ROW_G2:_ROW_G2 + 1, :]
        if name == "out_norm_conv_g":
            return g[_ROW_OUT_NORMS:_ROW_OUT_NORMS + 1, 0:CONV_WIDTH]
        if name == "out_norm_attn_g":
            return g[_ROW_OUT_NORMS:_ROW_OUT_NORMS + 1, CONV_WIDTH:]
        if name == "ffn_conv_b":
            return g[_ROW_FFN_B + k:_ROW_FFN_B + k + 1, 0:FFN_BLK]
        if name == "q_norm_g":
            return g[_ROW_GQ:_ROW_GQ + 1, 0:HEAD_DIM]
        if name == "k_norm_g":
            return g[_ROW_GK:_ROW_GK + 1, 0:HEAD_DIM]
        if name == "sinks":
            return g[_ROW_SINKS:_ROW_SINKS + 1, 0:N_HEADS]
        return g[_ROW_TABLE:_ROW_TABLE + N_HEADS, 0:NUM_BUCKETS]

    def body(r_ref, *refs):
        ins, outs, loss_ref = refs[:3 * n], refs[3 * n:7 * n], refs[7 * n]
        g = r_ref[0]
        for s in range(1, N_DEV):
            g = g + r_ref[s]
        loss_ref[...] = g[_ROW_LOSS:_ROW_LOSS + 1, 0:128]
        for i, name in enumerate(names):
            w_ref, m_ref, v_ref = ins[3 * i:3 * i + 3]
            o = outs[4 * i:4 * i + 4]
            cols = [slice(FFN_BLK * k, FFN_BLK * (k + 1)) for k in range(N_DEV)] if name == "ffn_conv_b" else [slice(None)]
            for k, cs in enumerate(cols):
                gk = grad_of(g, name, k)
                d, m2, v2 = _adamw_math(w_ref[:, cs], gk, m_ref[:, cs], v_ref[:, cs])
                o[0][:, cs], o[1][:, cs], o[2][:, cs], o[3][:, cs] = gk, d, m2, v2

    flat = [a for name in names for a in params[name]]
    body, more_specs, more = _ordered_behind(body, 1 + 3 * n, after)
    vmem = pl.BlockSpec(memory_space=pltpu.VMEM)
    out = pl.pallas_call(
        body, name="adamw_small",
        in_specs=[vmem] * (1 + 3 * n) + more_specs,
        out_shape=[jax.ShapeDtypeStruct(params[name][0].shape, F32) for name in names for _ in range(4)]
        + [jax.ShapeDtypeStruct((1, 128), F32)],
        compiler_params=pltpu.CompilerParams(vmem_limit_bytes=VMEM_LIMIT),
    )(recv, *flat, *more)
    return {name: tuple(out[4 * i:4 * i + 4]) for i, name in enumerate(names)}, out[4 * n]


def _adamw_direct(w, m, v, own, recv, me, name, row_blocks=1, after=None):
    rb = w.shape[0] // row_blocks
    cols = w.shape[1]

    def body(me_ref, w_ref, m_ref, v_ref, o_ref, r_ref, g_o, d_o, m_o, v_o):
        g = o_ref[...].astype(F32)
        for s in range(N_DEV - 1):
            g = g + r_ref[s].astype(F32)
        g_o[...] = g
        d_o[...], m_o[...], v_o[...] = _adamw_math(w_ref[...], g, m_ref[...], v_ref[...])

    blk = pl.BlockSpec((rb, cols), lambda i, me_ref: (i, 0))
    oblk = pl.BlockSpec((None, rb, cols), lambda i, me_ref: (me_ref[0], i, 0))
    rblk = pl.BlockSpec((N_DEV - 1, rb, cols), lambda i, me_ref: (0, i, 0))
    body, more_specs, more = _ordered_behind(body, 6, after)
    return pl.pallas_call(
        body, name=name,
        grid_spec=pltpu.PrefetchScalarGridSpec(num_scalar_prefetch=1, grid=(row_blocks,),
                                               in_specs=[blk, blk, blk, oblk, rblk] + more_specs, out_specs=[blk] * 4),
        out_shape=[jax.ShapeDtypeStruct(w.shape, F32)] * 4,
        compiler_params=_params(("arbitrary",)),
    )(me, w, m, v, own, recv, *more)


def _adamw(w, m, v, part, recv, chip, name, row_blocks=1, after=None):
    rb = w.shape[0] // row_blocks
    tail = w.shape[1:]
    zeros = (0,) * len(tail)

    def body(chip_ref, w_ref, m_ref, v_ref, p_ref, r_ref, g_o, d_o, m_o, v_o):
        g = p_ref[...].astype(F32)
        for s in range(3):
            g = g + r_ref[s].astype(F32)
        g_o[...] = g
        d_o[...], m_o[...], v_o[...] = _adamw_math(w_ref[...], g, m_ref[...], v_ref[...])

    blk = pl.BlockSpec((rb,) + tail, lambda i, chip_ref: (i,) + zeros)
    pblk = pl.BlockSpec((None, rb) + tail, lambda i, chip_ref: (chip_ref[0], i) + zeros)
    rblk = pl.BlockSpec((3, rb) + tail, lambda i, chip_ref: (0, i) + zeros)
    body, more_specs, more = _ordered_behind(body, 6, after)
    return pl.pallas_call(
        body, name=name,
        grid_spec=pltpu.PrefetchScalarGridSpec(num_scalar_prefetch=1, grid=(row_blocks,),
                                               in_specs=[blk, blk, blk, pblk, rblk] + more_specs, out_specs=[blk] * 4),
        out_shape=[jax.ShapeDtypeStruct(w.shape, F32)] * 4,
        compiler_params=_params(("arbitrary",)),
    )(chip, w, m, v, part, recv, *more)


def kernel(x, norm_mix_g, w_in, conv_w, q_norm_g, k_norm_g, rel_bias_table, sinks, out_norm_conv_g, out_norm_attn_g, w_out, norm_ffn_g, w_up, ffn_conv_w, ffn_conv_b, w_down, loss_target, m_norm_mix_g, m_w_in, m_conv_w, m_q_norm_g, m_k_norm_g, m_rel_bias_table, m_sinks, m_out_norm_conv_g, m_out_norm_attn_g, m_w_out, m_norm_ffn_g, m_w_up, m_ffn_conv_w, m_ffn_conv_b, m_w_down, v_norm_mix_g, v_w_in, v_conv_w, v_q_norm_g, v_k_norm_g, v_rel_bias_table, v_sinks, v_out_norm_conv_g, v_out_norm_attn_g, v_w_out, v_norm_ffn_g, v_w_up, v_ffn_conv_w, v_ffn_conv_b, v_w_down):
    p = dict(norm_mix_g=norm_mix_g, w_in=w_in, conv_w=conv_w, q_norm_g=q_norm_g, k_norm_g=k_norm_g,
             rel_bias_table=rel_bias_table, sinks=sinks, out_norm_conv_g=out_norm_conv_g, out_norm_attn_g=out_norm_attn_g,
             w_out=w_out, norm_ffn_g=norm_ffn_g, w_up=w_up, ffn_conv_w=ffn_conv_w, ffn_conv_b=ffn_conv_b, w_down=w_down)
    m = dict(norm_mix_g=m_norm_mix_g, w_in=m_w_in, conv_w=m_conv_w, q_norm_g=m_q_norm_g, k_norm_g=m_k_norm_g,
             rel_bias_table=m_rel_bias_table, sinks=m_sinks, out_norm_conv_g=m_out_norm_conv_g,
             out_norm_attn_g=m_out_norm_attn_g, w_out=m_w_out, norm_ffn_g=m_norm_ffn_g, w_up=m_w_up,
             ffn_conv_w=m_ffn_conv_w, ffn_conv_b=m_ffn_conv_b, w_down=m_w_down)
    v = dict(norm_mix_g=v_norm_mix_g, w_in=v_w_in, conv_w=v_conv_w, q_norm_g=v_q_norm_g, k_norm_g=v_k_norm_g,
             rel_bias_table=v_rel_bias_table, sinks=v_sinks, out_norm_conv_g=v_out_norm_conv_g,
             out_norm_attn_g=v_out_norm_attn_g, w_out=v_w_out, norm_ffn_g=v_norm_ffn_g, w_up=v_w_up,
             ffn_conv_w=v_ffn_conv_w, ffn_conv_b=v_ffn_conv_b, w_down=v_w_down)

    xs, tgt = x[0], loss_target[0]
    g1, g2, gq, gk, gconv, gattn = norm_mix_g, norm_ffn_g, q_norm_g, k_norm_g, out_norm_conv_g, out_norm_attn_g
    ix, iy, ic = _coords()
    core = ic.astype(jnp.int32).reshape(1)
    chip = (2 * ix + iy).astype(jnp.int32).reshape(1)
    me = _lin(ix, iy, ic).astype(jnp.int32).reshape(1)
    bkt = jnp.asarray(_bucket_map())
    tr = lambda a: a[0].T
    taps = lambda a: jnp.transpose(a, (1, 0, 2))
    tbl_t = rel_bias_table.T

    wi_l, cw_l = _place_shards(me, [tr(w_in), taps(conv_w)], [BF16, F32], "place_mixer_shards")
    finish_a, token_a = _all_gather_split([wi_l, cw_l], "mixer", None)
    wo_l, wu_l, wd_l, fcw_l = _place_shards(me, [w_out[0], tr(w_up), w_down[0], taps(ffn_conv_w)],
                                            [BF16, BF16, BF16, F32], "place_ffn_shards", after=token_a)
    ffn_stage2, ffn_stage3, token_b = _all_gather_tree([wo_l, wu_l, wd_l, fcw_l], "ffn", token_a)
    wi_g, cw_g = finish_a(token_b)
    w_in_t = wi_g.reshape(IN_WIDTH, D_MODEL)
    conv_w_f = jnp.transpose(cw_g[:, :, 0, :], (1, 0, 2)).reshape(3, CONV_WIDTH)

    proj, u1, ycn, qn, kn, vv = _mix_in_fwd(xs, g1, w_in_t, conv_w_f, gq, gk, gconv)
    token_b2 = ffn_stage2(ycn)
    y_attn, yan, probs, psinks = _attn_fwd(qn, kn, vv, tbl_t, sinks, bkt, gattn, after=token_b2)
    wo_g, wu_g, wd_g, fcw_g = ffn_stage3(yan)
    w_out_f = wo_g.reshape(D_MODEL, D_MODEL)
    w_down_f = wd_g.reshape(N_FFN_BLK, FFN_BLK, D_MODEL)
    w_up_f = wu_g.reshape(2, N_FFN_BLK, FFN_BLK, D_MODEL)
    fcw_f = fcw_g.reshape(2, N_FFN_BLK, 3, 1, FFN_BLK)
    fcb = ffn_conv_b.reshape(2, N_FFN_BLK, 1, FFN_BLK)
    h1, u2, up, pre, act, dh2, dh2b, loss_acc = _ffn_fwd(xs, ycn, yan, w_out_f, g2, w_up_f, fcw_f, fcb, w_down_f, tgt)

    dw_down = _grad_tn_blocked(act, dh2b, "grad_w_down").reshape(N_DEV, D_FF // N_DEV, D_MODEL)
    plan_d, slots_d = _scatter_plan(1)
    d_sem = _split_start("scatter_w_down_start", [dw_down], [lax.empty((N_DEV - 1,) + dw_down.shape[1:], BF16)],
                         plan_d, None, _ALL_FOR_W_DOWN)
    dup, dh1, dh1b, dfb, dfcw, dg2 = _ffn_bwd(dh2, dh2b, h1, g2, up, pre, w_up_f, fcw_f, w_down_f, after=d_sem[4])
    dw_up = _grad_tn_blocked(dup.reshape(N_DEV, SEQ, FFN_BLK), u2, "grad_w_up")
    dw_out = _grad_tn([ycn, yan], dh1b, CONV_WIDTH, "grad_w_out").reshape(N_DEV, D_MODEL // N_DEV, D_MODEL)
    out_bwd = {}

    def behind_ffn(token):
        out_bwd["r"] = _out_bwd(dh1b, w_out_f, y_attn, gattn, after=token)
        return out_bwd["r"][0]

    finish_ffn, token_ffn = _reduce_scatter_split(
        [dw_up, dw_out, dfcw.reshape(N_DEV, 3, 1, FFN_BLK)], "ffn", core, behind_ffn)
    dycn, dy_attn, dgattn = out_bwd["r"]
    dqn, dkn, dv, dtbl_t, dsinks = _attn_bwd(qn, kn, vv, dy_attn, probs, psinks, bkt, after=token_ffn)
    dx, dproj, dcw, dgconv, dgq, dgk, dg1 = _mix_in_bwd(xs, dh1, proj, dycn, dqn, dkn, dv, w_in_t, conv_w_f,
                                                         g1, gq, gk, gconv)
    packed, packed_all = _pack_small_grads(dg1, dg2, dgconv, dgattn, dfb, dgq, dgk, dsinks, dtbl_t, loss_acc)
    plan_s, slots_s = _broadcast_plan()
    s_sem, r_sem, src_s, land_s, token_s = _split_start("gather_small_start", [packed], [packed_all], plan_s, None,
                                                        _ALL_FOR_SMALL)
    dw_in_t = _grad_w_in(dproj, u1, after=token_s).reshape(N_DEV, IN_WIDTH // N_DEV, D_MODEL)
    dcw_b = jnp.transpose(dcw.reshape(3, N_DEV, 1, CONV_WIDTH // N_DEV), (1, 0, 2, 3))
    adam = {}
    ffn_got = {}

    def behind_mixer(token):
        ffn_got["r"] = finish_ffn(token)
        return ffn_got["r"][1][0]

    finish_mixer, token_mixer = _reduce_scatter_split([dw_in_t, dcw_b], "mixer", core, behind_mixer)
    (p_wu, p_wo, p_fcw), (r_wu, r_wo, r_fcw) = ffn_got["r"]
    (own_wd,), (r_wd,) = _split_wait("scatter_w_down_wait", d_sem[0], d_sem[1], d_sem[2], d_sem[3], plan_d, slots_d,
                                     token_mixer)
    adam["w_down"] = _adamw_direct(w_down[0], m_w_down[0], v_w_down[0], own_wd, r_wd, me, "adamw_w_down", row_blocks=2)
    adam_up = _adamw(tr(w_up), tr(m_w_up), tr(v_w_up), p_wu, r_wu, chip, "adamw_w_up", row_blocks=4,
                     after=adam["w_down"][0])
    adam["w_out"] = _adamw(w_out[0], m_w_out[0], v_w_out[0], p_wo, r_wo, chip, "adamw_w_out", after=adam_up[0])
    adam_fcw = _adamw(taps(ffn_conv_w), taps(m_ffn_conv_w), taps(v_ffn_conv_w), p_fcw, r_fcw, chip, "adamw_ffn_conv_w",
                      after=adam["w_out"][0])
    _, (r_small,) = _split_wait("gather_small_wait", s_sem, r_sem, src_s, land_s, plan_s, slots_s, adam_fcw[0])
    small_in = {k: (p[k], m[k], v[k]) for k in _SMALL_NAMES}
    small_in["rel_bias_table"] = (tbl_t, m_rel_bias_table.T, v_rel_bias_table.T)
    small_out, loss_row = _adamw_small(r_small, small_in, None)
    (p_wi, p_cw), (r_wi, r_cw) = finish_mixer(loss_row)
    adam_in = _adamw(tr(w_in), tr(m_w_in), tr(v_w_in), p_wi, r_wi, chip, "adamw_w_in")
    adam_cw = _adamw(taps(conv_w), taps(m_conv_w), taps(v_conv_w), p_cw, r_cw, chip, "adamw_conv_w")

    res = {k: tuple(a[None] for a in t) for k, t in adam.items()}
    res["w_up"] = tuple(a.T[None] for a in adam_up)
    res["w_in"] = tuple(a.T[None] for a in adam_in)
    res["ffn_conv_w"] = tuple(taps(a) for a in adam_fcw)
    res["conv_w"] = tuple(taps(a) for a in adam_cw)
    res.update(small_out)
    res["rel_bias_table"] = tuple(a.T for a in small_out["rel_bias_table"])
    loss = loss_row[0, 0]
    order = ("norm_mix_g", "w_in", "conv_w", "q_norm_g", "k_norm_g", "rel_bias_table", "sinks", "out_norm_conv_g",
             "out_norm_attn_g", "w_out", "norm_ffn_g", "w_up", "ffn_conv_w", "ffn_conv_b", "w_down")
    return (loss, dx[None], *[res[k][0] for k in order], *[res[k][1] for k in order],
            *[res[k][2] for k in order], *[res[k][3] for k in order])
```

```python
import math

import numpy as np
import jax
import jax.numpy as jnp
from jax import lax
from jax.experimental import pallas as pl
from jax.experimental.pallas import tpu as pltpu

F32 = jnp.float32
BF16 = jnp.bfloat16

SEQ = 2048
D_MODEL = 1024
CONV_WIDTH = 512
ATTN_WIDTH = 512
KV_WIDTH = 128
HEAD_DIM = 64
N_HEADS = 8
GQA_GROUP = 4
IN_WIDTH = 2304
D_FF = 2816
BLK = 128
NUM_BUCKETS = 32
EPS = 1e-6
NEG_INF = -1e30
ADAM_LR = 0.001
ADAM_B1 = 0.9
ADAM_B2 = 0.999
ADAM_EPS = 1e-08
ADAM_WD = 0.01
ADAM_STEP = 10

N_DEV = 8
FFN_BLK = 2 * D_FF // N_DEV
N_FFN_BLK = D_FF // FFN_BLK
SUBLANES = 8
VMEM_LIMIT = 56 * 1024 * 1024

_MESH = pl.DeviceIdType.MESH
_ANY = pl.BlockSpec(memory_space=pl.ANY)


def _params(sem):
    return pltpu.CompilerParams(dimension_semantics=sem, vmem_limit_bytes=VMEM_LIMIT)


def _ordered_behind(body, pos, after):
    if after is None:
        return body, [], []
    return (lambda *refs: body(*refs[:pos], *refs[pos + 1:])), [_ANY], [after]


def _dot(a, b):
    return jnp.dot(a, b, preferred_element_type=F32)


def _dot_nt(a, b):
    return lax.dot_general(a, b, (((1,), (1,)), ((), ())), preferred_element_type=F32)


def _dot_tn(a, b):
    return lax.dot_general(a, b, (((0,), (0,)), ((), ())), preferred_element_type=F32)


def _shift_down(x, s, halo):
    r = pltpu.roll(x, s, axis=0)
    hr = pltpu.roll(halo, s, axis=0)
    row = lax.broadcasted_iota(jnp.int32, halo.shape, 0)
    top = jnp.where(row < s, hr, r[:SUBLANES])
    return jnp.concatenate([top, r[SUBLANES:]], axis=0)


def _shift_up(x, s, halo):
    n = x.shape[0]
    r = pltpu.roll(x, n - s, axis=0)
    hr = pltpu.roll(halo, SUBLANES - s, axis=0)
    row = lax.broadcasted_iota(jnp.int32, halo.shape, 0)
    bot = jnp.where(row >= SUBLANES - s, hr, r[n - SUBLANES:])
    return jnp.concatenate([r[:n - SUBLANES], bot], axis=0)


def _taps(w):
    return (w[0], w[1], w[2]) if len(w.shape) == 3 else (w[0:1], w[1:2], w[2:3])


def _conv3(x, w, halo):
    x2 = _shift_down(x, 2, halo)
    x1 = _shift_down(x, 1, halo)
    return x2 * w[0] + x1 * w[1] + x * w[2], x2, x1


def _conv3_bwd_input(dy, w, halo_next):
    return dy * w[2] + _shift_up(dy, 1, halo_next) * w[1] + _shift_up(dy, 2, halo_next) * w[0]


def _rstd(x):
    return lax.rsqrt(jnp.mean(x * x, axis=-1, keepdims=True) + EPS)


def _rms_bwd(x, g, dy):
    r = _rstd(x)
    n = x * r
    dn = dy * g
    dx = r * (dn - n * jnp.mean(dn * n, axis=-1, keepdims=True))
    return dx, jnp.sum(dy * n, axis=0, keepdims=True)


def _head_mean(x):
    width = x.shape[-1]
    ri = lax.broadcasted_iota(jnp.int32, (width, width), 0) // HEAD_DIM
    ci = lax.broadcasted_iota(jnp.int32, (width, width), 1) // HEAD_DIM
    ones = jnp.where(ri == ci, 1.0, 0.0).astype(BF16)
    hi = x.astype(BF16)
    lo = (x - hi.astype(F32)).astype(BF16)
    return (_dot(hi, ones) + _dot(lo, ones)) * (1.0 / HEAD_DIM)


def _head_norm(x, g, heads):
    return x * lax.rsqrt(_head_mean(x * x) + EPS) * jnp.tile(g, (1, heads))


def _head_norm_bwd(x, g, dy, heads):
    r = lax.rsqrt(_head_mean(x * x) + EPS)
    n = x * r
    dn = dy * jnp.tile(g, (1, heads))
    dx = r * (dn - n * _head_mean(dn * n))
    per_lane = jnp.sum(dy * n, axis=0, keepdims=True)
    dg = per_lane[:, 0:HEAD_DIM]
    for h in range(1, heads):
        dg = dg + per_lane[:, HEAD_DIM * h:HEAD_DIM * (h + 1)]
    return dx, dg


def _bucket_map():
    q = np.arange(BLK)[:, None]
    j = np.arange(BLK)[None, :]
    n = np.where(j > q, q + BLK - j, q - j)
    nf = np.maximum(n, 1).astype(np.float32)
    max_exact = NUM_BUCKETS // 2
    large = max_exact + (np.log(nf / max_exact) / math.log(BLK / max_exact) * (NUM_BUCKETS - max_exact)).astype(np.int32)
    large = np.minimum(large, NUM_BUCKETS - 1)
    return np.where(n < max_exact, n, large).astype(np.int32)


def _coords():
    return lax.axis_index("x"), lax.axis_index("y"), lax.axis_index("c")


def _lin(px, py, pc):
    return 4 * px + 2 * py + pc


def _chips(x, y):
    return [(1 - x, y), (x, 1 - y), (1 - x, 1 - y)]


def _peers(x, y, c):
    return [(1 - x if r & 4 else x, 1 - y if r & 2 else y, 1 - c if r & 1 else c) for r in range(1, N_DEV)]


_HBM = pl.BlockSpec(memory_space=pltpu.HBM)
_SEM = pl.BlockSpec(memory_space=pltpu.SEMAPHORE)
_EFFECT = pltpu.SideEffectType.DATAFLOW_SIDE_EFFECTING


def _in_hbm(a):
    return pltpu.with_memory_space_constraint(a, pltpu.HBM)


_SIBLING = (1, lambda x, y, c: [(x, y, 1 - c)])
_SIBLING_AND_CHIPS = (2, lambda x, y, c: [(x, y, 1 - c)] + [(cx, cy, c) for cx, cy in _chips(x, y)])
_SIBLING_AND_NEIGHBOURS = (3, lambda x, y, c: [(x, y, 1 - c), (1 - x, y, c), (x, 1 - y, c)])
_ONWARD_AND_SIBLING = (4, lambda x, y, c: [(jnp.where(c == 1, x, 1 - x), jnp.where(c == 1, 1 - y, y), c), (x, y, 1 - c)])
_ALL_FOR_W_DOWN = (5, lambda x, y, c: _peers(x, y, c))
_CHIPS = (6, lambda x, y, c: [(cx, cy, c) for cx, cy in _chips(x, y)])
_ALL_FOR_SMALL = (7, lambda x, y, c: _peers(x, y, c))


def _split_start(name, srcs, lands, plan, after, handshake):
    ns, nl = len(srcs), len(lands)
    n_copies = len(plan(0, 0, 0))
    n_after = 0 if after is None else 1
    collective_id, peers_of = handshake

    def body(*refs):
        src_refs, land_refs = refs[:ns + nl], refs[ns:ns + nl]
        send_sems, recv_sems = refs[ns + nl + n_after], refs[ns + nl + n_after + 1]
        token = refs[-1]
        barrier = pltpu.get_barrier_semaphore()
        peers = peers_of(*_coords())
        for peer in peers:
            pl.semaphore_signal(barrier, inc=1, device_id=peer, device_id_type=_MESH)
        pl.semaphore_wait(barrier, len(peers))
        for k, (a, s_slot, l, d_slot, dev) in enumerate(plan(*_coords())):
            src = src_refs[a] if s_slot is None else src_refs[a].at[s_slot]
            pltpu.make_async_remote_copy(src_ref=src, dst_ref=land_refs[l].at[d_slot], send_sem=send_sems.at[k],
                                         recv_sem=recv_sems.at[k], device_id=dev, device_id_type=_MESH).start()
        token[...] = jnp.zeros_like(token)

    arrs = list(srcs) + list(lands)
    out = pl.pallas_call(
        body, name=name,
        out_shape=(pltpu.SemaphoreType.DMA((n_copies,)), pltpu.SemaphoreType.DMA((n_copies,)),
                   *[pltpu.HBM(a.shape, a.dtype) for a in arrs], jax.ShapeDtypeStruct((SUBLANES, 128), F32)),
        in_specs=[_HBM] * (ns + nl) + [_ANY] * n_after,
        out_specs=(_SEM, _SEM, *[_HBM] * (ns + nl), pl.BlockSpec(memory_space=pltpu.VMEM)),
        input_output_aliases={i: 2 + i for i in range(ns + nl)},
        compiler_params=pltpu.CompilerParams(has_side_effects=_EFFECT, collective_id=collective_id),
    )(*[_in_hbm(a) for a in arrs], *([] if after is None else [after]))
    return out[0], out[1], list(out[2:2 + ns]), list(out[2 + ns:2 + ns + nl]), out[-1]


def _split_wait(name, send_sems, recv_sems, srcs, lands, plan, recv_slots, after):
    ns, nl = len(srcs), len(lands)

    def body(*refs):
        src_refs, land_refs = refs[:ns + nl], refs[ns:ns + nl]
        send_sems, recv_sems = refs[ns + nl], refs[ns + nl + 1]
        coords = _coords()
        slots = recv_slots(*coords)
        for k, (a, s_slot, l, _, dev) in enumerate(plan(*coords)):
            src = src_refs[a] if s_slot is None else src_refs[a].at[s_slot]
            cp = pltpu.make_async_remote_copy(src_ref=src, dst_ref=land_refs[l].at[slots[k]], send_sem=send_sems.at[k],
                                              recv_sem=recv_sems.at[k], device_id=dev, device_id_type=_MESH)
            cp.wait_send()
            cp.wait_recv()

    arrs = list(srcs) + list(lands)
    out = pl.pallas_call(
        body, name=name,
        out_shape=tuple(pltpu.HBM(a.shape, a.dtype) for a in arrs),
        in_specs=[_HBM] * (ns + nl) + [_SEM, _SEM, _ANY],
        out_specs=tuple([_HBM] * (ns + nl)),
        input_output_aliases={i: i for i in range(ns + nl)},
        compiler_params=pltpu.CompilerParams(has_side_effects=_EFFECT),
    )(*arrs, send_sems, recv_sems, after)
    return list(out[:ns]), list(out[ns:])


def _gather_plan_ici(n):
    def plan(x, y, c):
        me = _lin(x, y, c)
        out = []
        for a in range(n):
            out.append((a, me, a, me, (x, y, 1 - c)))
            out += [(a, me, a, me, (cx, cy, c)) for cx, cy in _chips(x, y)]
        return out

    def recv_slots(x, y, c):
        out = []
        for _ in range(n):
            out.append(_lin(x, y, 1 - c))
            out += [_lin(cx, cy, c) for cx, cy in _chips(x, y)]
        return out

    return plan, recv_slots


def _gather_plan_d2d(n):
    def plan(x, y, c):
        return [(a, _lin(cx, cy, c), a, _lin(cx, cy, c), (x, y, 1 - c)) for a in range(n) for cx, cy in _chips(x, y)]

    def recv_slots(x, y, c):
        return [_lin(cx, cy, 1 - c) for _ in range(n) for cx, cy in _chips(x, y)]

    return plan, recv_slots


def _all_gather_split(lands, tag, after):
    n = len(lands)
    plan1, slots1 = _gather_plan_ici(n)
    s1, r1, _, lands, token = _split_start(f"gather_{tag}_ici_start", [], lands, plan1, after, _SIBLING_AND_CHIPS)

    def finish(after):
        _, got = _split_wait(f"gather_{tag}_ici_wait", s1, r1, [], lands, plan1, slots1, after)
        plan2, slots2 = _gather_plan_d2d(n)
        s2, r2, _, got, token2 = _split_start(f"gather_{tag}_d2d_start", [], got, plan2, None, _SIBLING)
        return _split_wait(f"gather_{tag}_d2d_wait", s2, r2, [], got, plan2, slots2, token2)[1]

    return finish, token


def _all_gather_tree(lands, tag, after):
    n = len(lands)

    def plan1(x, y, c):
        me = _lin(x, y, c)
        return [(a, me, a, me, dev) for a in range(n) for dev in ((x, y, 1 - c), (1 - x, y, c), (x, 1 - y, c))]

    def slots1(x, y, c):
        return [s for _ in range(n) for s in (_lin(x, y, 1 - c), _lin(1 - x, y, c), _lin(x, 1 - y, c))]

    def plan2(x, y, c):
        from_x, from_y = _lin(1 - x, y, c), _lin(x, 1 - y, c)
        north = c == 1
        passed = jnp.where(north, from_x, from_y)
        onward = (jnp.where(north, x, 1 - x), jnp.where(north, 1 - y, y), c)
        sib = (x, y, 1 - c)
        return [cp for a in range(n) for cp in ((a, passed, a, passed, onward), (a, from_x, a, from_x, sib),
                                                (a, from_y, a, from_y, sib))]

    def slots2(x, y, c):
        return [s for _ in range(n) for s in (_lin(1 - x, 1 - y, c), _lin(1 - x, y, 1 - c), _lin(x, 1 - y, 1 - c))]

    def plan3(x, y, c):
        diag = _lin(1 - x, 1 - y, c)
        return [(a, diag, a, diag, (x, y, 1 - c)) for a in range(n)]

    def slots3(x, y, c):
        return [_lin(1 - x, 1 - y, 1 - c)] * n

    s1, r1, _, lands, token = _split_start(f"gather_{tag}_1_start", [], lands, plan1, after, _SIBLING_AND_NEIGHBOURS)
    state = {}

    def stage2(after):
        _, got = _split_wait(f"gather_{tag}_1_wait", s1, r1, [], lands, plan1, slots1, after)
        state["s"], state["r"], _, state["lands"], token2 = _split_start(f"gather_{tag}_2_start", [], got, plan2, None,
                                                                         _ONWARD_AND_SIBLING)
        return token2

    def stage3(after):
        _, got = _split_wait(f"gather_{tag}_2_wait", state["s"], state["r"], [], state["lands"], plan2, slots2, after)
        s3, r3, _, got, token3 = _split_start(f"gather_{tag}_3_start", [], got, plan3, None, _SIBLING)
        return _split_wait(f"gather_{tag}_3_wait", s3, r3, [], got, plan3, slots3, token3)[1]

    return stage2, stage3, token


_CHIP_LIST = ((0, 0), (0, 1), (1, 0), (1, 1))


def _reduce_plan_d2d(n):
    def plan(x, y, c):
        return [(a, _lin(qx, qy, 1 - c), a, q, (x, y, 1 - c)) for a in range(n) for q, (qx, qy) in enumerate(_CHIP_LIST)]

    def recv_slots(x, y, c):
        return [q for _ in range(n) for q in range(4)]

    return plan, recv_slots


def _reduce_plan_ici(n):
    def plan(x, y, c):
        return [(a, 2 * cx + cy, a, j, (cx, cy, c)) for a in range(n) for j, (cx, cy) in enumerate(_chips(x, y))]

    def recv_slots(x, y, c):
        return [j for _ in range(n) for j in range(3)]

    return plan, recv_slots


def _scatter_plan(n):
    def plan(x, y, c):
        return [(a, _lin(*peer), a, r, peer) for a in range(n) for r, peer in enumerate(_peers(x, y, c))]

    def recv_slots(x, y, c):
        return [r for _ in range(n) for r in range(N_DEV - 1)]

    return plan, recv_slots


def _broadcast_plan():
    def plan(x, y, c):
        return [(0, None, 0, _lin(x, y, c), peer) for peer in _peers(x, y, c)]

    def recv_slots(x, y, c):
        return [_lin(*peer) for peer in _peers(x, y, c)]

    return plan, recv_slots


def _chip_partial(grads, recvd, core, name):
    n = len(grads)

    def body(c_ref, *refs):
        for a in range(n):
            g_ref, r_ref, o_ref = refs[a], refs[n + a], refs[2 * n + a]
            o_ref[...] = (g_ref[...].astype(F32) + r_ref[...].astype(F32)).astype(o_ref.dtype)

    def blk(a, own):
        zeros = (0,) * (a.ndim - 1)
        return pl.BlockSpec((None,) + a.shape[1:],
                            (lambda q, c_ref: (2 * q + c_ref[0],) + zeros) if own else (lambda q, c_ref: (q,) + zeros))

    return pl.pallas_call(
        body, name=name,
        grid_spec=pltpu.PrefetchScalarGridSpec(
            num_scalar_prefetch=1, grid=(4,),
            in_specs=[blk(a, True) for a in grads] + [blk(a, False) for a in recvd],
            out_specs=[blk(a, False) for a in recvd]),
        out_shape=[jax.ShapeDtypeStruct(a.shape, a.dtype) for a in recvd],
        compiler_params=_params(("arbitrary",)),
    )(core, *grads, *recvd)


def _reduce_scatter_split(grads, tag, core, behind):
    n = len(grads)
    plan1, slots1 = _reduce_plan_d2d(n)
    lands1 = [lax.empty((4,) + a.shape[1:], a.dtype) for a in grads]
    s1, r1, srcs1, lands1, token1 = _split_start(f"reduce_{tag}_d2d_start", grads, lands1, plan1, None, _SIBLING)
    own, got = _split_wait(f"reduce_{tag}_d2d_wait", s1, r1, srcs1, lands1, plan1, slots1, behind(token1))
    parts = _chip_partial(own, got, core, f"reduce_{tag}_partial")
    plan2, slots2 = _reduce_plan_ici(n)
    lands2 = [lax.empty((3,) + a.shape[1:], a.dtype) for a in grads]
    s2, r2, srcs2, lands2, token2 = _split_start(f"reduce_{tag}_ici_start", parts, lands2, plan2, None, _CHIPS)

    def finish(after):
        return _split_wait(f"reduce_{tag}_ici_wait", s2, r2, srcs2, lands2, plan2, slots2, after)

    return finish, token2


def _place_shards(me, shards, dtypes, name, after=None):
    n = len(shards)

    def body(me_ref, *refs):
        for a in range(n):
            refs[n + a][...] = refs[a][...].astype(dtypes[a])

    full = lambda s: pl.BlockSpec(s.shape, lambda i, me_ref: (0,) * s.ndim)
    slot = lambda s: pl.BlockSpec((None,) + s.shape, lambda i, me_ref: (me_ref[0],) + (0,) * s.ndim)
    body, more_specs, more = _ordered_behind(body, 1 + n, after)
    return pl.pallas_call(
        body, name=name,
        grid_spec=pltpu.PrefetchScalarGridSpec(num_scalar_prefetch=1, grid=(1,),
                                               in_specs=[full(s) for s in shards] + more_specs,
                                               out_specs=[slot(s) for s in shards]),
        out_shape=[jax.ShapeDtypeStruct((N_DEV,) + s.shape, d) for s, d in zip(shards, dtypes)],
        compiler_params=_params(("arbitrary",)),
    )(me, *shards, *more)


def _mix_in_fwd(x, g1, w_in_t, conv_w, gq, gk, gconv):
    tm = 512
    n_t = SEQ // tm

    def body(x_ref, g1_ref, w_ref, cw_ref, gq_ref, gk_ref, gc_ref,
             proj_ref, u1_ref, ycn_ref, qn_ref, kn_ref, v_ref, halo_ref):
        @pl.when(pl.program_id(0) == 0)
        def _():
            halo_ref[...] = jnp.zeros_like(halo_ref)

        xv = x_ref[...]
        u = (xv * _rstd(xv) * g1_ref[...]).astype(BF16)
        u1_ref[...] = u
        proj = _dot_nt(u, w_ref[...])
        proj_ref[...] = proj
        gate_b = proj[:, 0:CONV_WIDTH]
        a = proj[:, CONV_WIDTH:2 * CONV_WIDTH] * proj[:, 2 * CONV_WIDTH:3 * CONV_WIDTH]
        cv, _, _ = _conv3(a, _taps(cw_ref[...]), halo_ref[...])
        halo_ref[...] = a[tm - SUBLANES:]
        yc = gate_b * cv
        ycn_ref[...] = (yc * _rstd(yc) * gc_ref[...]).astype(BF16)
        q0 = 3 * CONV_WIDTH
        qn_ref[...] = _head_norm(proj[:, q0:q0 + ATTN_WIDTH], gq_ref[...], N_HEADS).astype(BF16)
        k0 = q0 + ATTN_WIDTH
        kn_ref[...] = _head_norm(proj[:, k0:k0 + KV_WIDTH], gk_ref[...], 2).astype(BF16)
        v_ref[...] = proj[:, k0 + KV_WIDTH:k0 + 2 * KV_WIDTH].astype(BF16)

    const = lambda shape: pl.BlockSpec(shape, lambda i: (0,) * len(shape))
    rows = lambda w: pl.BlockSpec((tm, w), lambda i: (i, 0))
    return pl.pallas_call(
        body, name="mix_in_fwd", grid=(n_t,),
        in_specs=[rows(D_MODEL), const((1, D_MODEL)), const((IN_WIDTH, D_MODEL)), const((3, CONV_WIDTH)),
                  const((1, HEAD_DIM)), const((1, HEAD_DIM)), const((1, CONV_WIDTH))],
        out_specs=[rows(IN_WIDTH), rows(D_MODEL), rows(CONV_WIDTH), rows(ATTN_WIDTH), rows(KV_WIDTH), rows(KV_WIDTH)],
        out_shape=[jax.ShapeDtypeStruct((SEQ, IN_WIDTH), F32), jax.ShapeDtypeStruct((SEQ, D_MODEL), BF16),
                   jax.ShapeDtypeStruct((SEQ, CONV_WIDTH), BF16),
                   jax.ShapeDtypeStruct((SEQ, ATTN_WIDTH), BF16), jax.ShapeDtypeStruct((SEQ, KV_WIDTH), BF16),
                   jax.ShapeDtypeStruct((SEQ, KV_WIDTH), BF16)],
        scratch_shapes=[pltpu.VMEM((SUBLANES, CONV_WIDTH), F32)],
        compiler_params=_params(("arbitrary",)),
    )(x, g1, w_in_t, conv_w, gq, gk, gconv)


GROUP_ROWS = GQA_GROUP * BLK


def _band_bias(tbl_ref, bkt, bias_ref):
    for h in range(N_HEADS):
        acc = jnp.zeros(bkt.shape, F32)
        for b in range(NUM_BUCKETS):
            acc = jnp.where(bkt == b, tbl_ref[h, b], acc)
        bias_ref[h // GQA_GROUP, BLK * (h % GQA_GROUP):BLK * (h % GQA_GROUP + 1), :] = acc


def _band_masks(i):
    qi = lax.broadcasted_iota(jnp.int32, (GROUP_ROWS, BLK), 0) & (BLK - 1)
    ji = lax.broadcasted_iota(jnp.int32, (GROUP_ROWS, BLK), 1)
    upper = ji > qi
    return upper, upper & (i == 0)


def _stack_heads(x, g):
    return jnp.concatenate([x[:, HEAD_DIM * h:HEAD_DIM * (h + 1)] for h in range(GQA_GROUP * g, GQA_GROUP * (g + 1))], axis=0)


def _unstack_heads(groups):
    return jnp.concatenate([p[BLK * t:BLK * (t + 1)] for p in groups for t in range(GQA_GROUP)], axis=-1)


def _per_head_rows(vals):
    row = lax.broadcasted_iota(jnp.int32, (GROUP_ROWS, 1), 0)
    col = jnp.full((GROUP_ROWS, 1), vals[GQA_GROUP - 1], F32)
    for t in range(GQA_GROUP - 2, -1, -1):
        col = jnp.where(row < BLK * (t + 1), vals[t], col)
    return col


def _band_rows(ref, i):
    prev = pl.multiple_of(jnp.maximum(i - 1, 0) * BLK, BLK)
    cur = pl.multiple_of(i * BLK, BLK)
    return jnp.concatenate([ref[pl.ds(prev, BLK), :], ref[pl.ds(cur, BLK), :]], axis=0), prev, cur


def _fold(band, upper):
    return jnp.where(upper, band[:, :BLK], band[:, BLK:])


def _unfold(tile, upper):
    return jnp.concatenate([jnp.where(upper, tile, 0.0), jnp.where(upper, 0.0, tile)], axis=1)


def _head_probs(qh, kh, bias, upper, dead, sink):
    logits = _fold(_dot_nt(qh, kh), upper) * (HEAD_DIM ** -0.5) + bias
    logits = jnp.where(dead, NEG_INF, logits)
    m = jnp.maximum(jnp.max(logits, axis=-1, keepdims=True), sink)
    p = jnp.exp(logits - m)
    es = jnp.exp(sink - m)
    den = jnp.sum(p, axis=-1, keepdims=True) + es
    return p / den, es / den


def _attn_fwd(qn, kn, v, tbl, sinks, bkt, gattn, after=None):
    n_b = SEQ // BLK

    def body(q_ref, k_ref, v_ref, tbl_ref, sink_ref, bkt_ref, ga_ref, y_ref, yn_ref, p_ref, ps_ref, bias_ref):
        i = pl.program_id(0)

        @pl.when(i == 0)
        def _():
            _band_bias(tbl_ref, bkt_ref[...], bias_ref)

        kb, _, _ = _band_rows(k_ref, i)
        vb, _, _ = _band_rows(v_ref, i)
        upper, dead = _band_masks(i)
        q = q_ref[...]
        lane = lax.broadcasted_iota(jnp.int32, (BLK, 128), 1)
        outs = []
        psinks = jnp.zeros((BLK, 128), F32)
        for g in range(N_HEADS // GQA_GROUP):
            kv = slice(HEAD_DIM * g, HEAD_DIM * (g + 1))
            sink = _per_head_rows([sink_ref[0, GQA_GROUP * g + t] for t in range(GQA_GROUP)])
            probs, psink = _head_probs(_stack_heads(q, g), kb[:, kv], bias_ref[g], upper, dead, sink)
            p_ref[g] = probs.astype(BF16)
            for t in range(GQA_GROUP):
                psinks = jnp.where(lane == GQA_GROUP * g + t, psink[BLK * t:BLK * (t + 1)], psinks)
            outs.append(_dot(_unfold(probs, upper).astype(BF16), vb[:, kv]))
        ps_ref[...] = psinks
        y = _unstack_heads(outs)
        y_ref[...] = y
        yn_ref[...] = (y * _rstd(y) * ga_ref[...]).astype(BF16)

    const = lambda shape: pl.BlockSpec(shape, lambda i: (0,) * len(shape))
    rows = lambda w: pl.BlockSpec((BLK, w), lambda i: (i, 0))
    smem = pl.BlockSpec(memory_space=pltpu.SMEM)
    body, more_specs, more = _ordered_behind(body, 7, after)
    return pl.pallas_call(
        body, name="attn_fwd", grid=(n_b,),
        in_specs=[rows(ATTN_WIDTH), const((SEQ, KV_WIDTH)), const((SEQ, KV_WIDTH)), smem, smem,
                  const((BLK, BLK)), const((1, ATTN_WIDTH))] + more_specs,
        out_specs=[rows(ATTN_WIDTH), rows(ATTN_WIDTH),
                   pl.BlockSpec((None, N_HEADS // GQA_GROUP, GROUP_ROWS, BLK), lambda i: (i, 0, 0, 0)), rows(128)],
        out_shape=[jax.ShapeDtypeStruct((SEQ, ATTN_WIDTH), F32), jax.ShapeDtypeStruct((SEQ, ATTN_WIDTH), BF16),
                   jax.ShapeDtypeStruct((n_b, N_HEADS // GQA_GROUP, GROUP_ROWS, BLK), BF16),
                   jax.ShapeDtypeStruct((SEQ, 128), F32)],
        scratch_shapes=[pltpu.VMEM((N_HEADS // GQA_GROUP, GROUP_ROWS, BLK), F32)],
        compiler_params=_params(("arbitrary",)),
    )(qn, kn, v, tbl, sinks, bkt, gattn, *more)


def _ffn_fwd(x, ycn, yan, w_out, g2, w_up, fcw, fcb, w_down, tgt):
    tm = 512
    n_t = SEQ // tm

    def body(x_ref, ycn_ref, yan_ref, wo_ref, g2_ref, wu_ref, cw_ref, b_ref, wd_ref, tgt_ref,
             h1_ref, u2_ref, up_ref, pre_ref, act_ref, dh2_ref, dh2b_ref, loss_ref, acc_ref, halo_ref):
        i, j = pl.program_id(0), pl.program_id(1)

        @pl.when((i == 0) & (j == 0))
        def _():
            loss_ref[...] = jnp.zeros_like(loss_ref)

        @pl.when(j == 0)
        def _():
            h1 = x_ref[...] + _dot(ycn_ref[...], wo_ref[0:CONV_WIDTH, :]) + _dot(yan_ref[...], wo_ref[CONV_WIDTH:, :])
            h1_ref[...] = h1
            u2_ref[...] = (h1 * _rstd(h1) * g2_ref[...]).astype(BF16)
            acc_ref[...] = jnp.zeros_like(acc_ref)

        u2 = u2_ref[...]
        pre = []
        for s in range(2):
            up = _dot_nt(u2, wu_ref[s])
            up_ref[s] = up.astype(BF16)
            halo = jnp.where(i == 0, 0.0, halo_ref[s, j])
            pre.append(_conv3(up, _taps(cw_ref.at[s]), halo)[0] + b_ref[s])
            pre_ref[s] = pre[s].astype(BF16)
            halo_ref[s, j] = up[tm - SUBLANES:]
        g, val = pre
        act = (g * jax.nn.sigmoid(g) * val).astype(BF16)
        act_ref[...] = act
        acc_ref[...] += _dot(act, wd_ref[...])

        @pl.when(j == N_FFN_BLK - 1)
        def _():
            err = h1_ref[...] + acc_ref[...] - tgt_ref[...]
            loss_ref[...] += 0.5 * jnp.sum(err * err) / D_MODEL
            dh2 = err / D_MODEL
            dh2_ref[...] = dh2
            dh2b_ref[...] = dh2.astype(BF16)

    rows = lambda w: pl.BlockSpec((tm, w), lambda i, j: (i, 0))
    const = lambda shape: pl.BlockSpec(shape, lambda i, j: (0,) * len(shape))
    pair = lambda *s: pl.BlockSpec((2, None) + s, lambda i, j: (0, j) + (0,) * len(s))
    upb = pl.BlockSpec((2, None, tm, FFN_BLK), lambda i, j: (0, j, i, 0))
    return pl.pallas_call(
        body, name="ffn_fwd", grid=(n_t, N_FFN_BLK),
        in_specs=[rows(D_MODEL), rows(CONV_WIDTH), rows(ATTN_WIDTH), const((D_MODEL, D_MODEL)), const((1, D_MODEL)),
                  pair(FFN_BLK, D_MODEL), pair(3, 1, FFN_BLK), pair(1, FFN_BLK),
                  pl.BlockSpec((None, FFN_BLK, D_MODEL), lambda i, j: (j, 0, 0)), rows(D_MODEL)],
        out_specs=[rows(D_MODEL), rows(D_MODEL), upb, upb, pl.BlockSpec((None, tm, FFN_BLK), lambda i, j: (j, i, 0)),
                   rows(D_MODEL), rows(D_MODEL), const((SUBLANES, 128))],
        out_shape=[jax.ShapeDtypeStruct((SEQ, D_MODEL), F32), jax.ShapeDtypeStruct((SEQ, D_MODEL), BF16),
                   jax.ShapeDtypeStruct((2, N_FFN_BLK, SEQ, FFN_BLK), BF16),
                   jax.ShapeDtypeStruct((2, N_FFN_BLK, SEQ, FFN_BLK), BF16),
                   jax.ShapeDtypeStruct((N_FFN_BLK, SEQ, FFN_BLK), BF16),
                   jax.ShapeDtypeStruct((SEQ, D_MODEL), F32), jax.ShapeDtypeStruct((SEQ, D_MODEL), BF16),
                   jax.ShapeDtypeStruct((SUBLANES, 128), F32)],
        scratch_shapes=[pltpu.VMEM((tm, D_MODEL), F32), pltpu.VMEM((2, N_FFN_BLK, SUBLANES, FFN_BLK), F32)],
        compiler_params=_params(("arbitrary", "arbitrary")),
    )(x, ycn, yan, w_out, g2, w_up, fcw, fcb, w_down, tgt)


def _ffn_bwd(dh2, dh2b, h1, g2, up, pre, w_up, fcw, w_down, after=None):
    tm = 512
    units = ((288, 224), (0, 288))
    n_t = SEQ // tm

    def body(dh2_ref, dh2b_ref, h1_ref, g2_ref, up_ref, pre_ref, wu_ref, cw_ref, wd_ref,
             dup_ref, dh1_ref, dh1b_ref, dfb_ref, dfcw_ref, dg2_ref, acc_ref, next_ref):
        i, j = pl.program_id(0), pl.program_id(1)

        @pl.when((i == 0) & (j == 0))
        def _():
            dfb_ref[...] = jnp.zeros_like(dfb_ref)
            dfcw_ref[...] = jnp.zeros_like(dfcw_ref)
            dg2_ref[...] = jnp.zeros_like(dg2_ref)

        @pl.when(j == 0)
        def _():
            acc_ref[...] = jnp.zeros_like(acc_ref)

        nxt = [jnp.where(i == 0, 0.0, next_ref[s, j]) for s in range(2)]
        sums = [[0.0] * 4 for _ in range(2)]
        stores = []
        for r0, rn in units:
            rows = slice(r0, r0 + rn)
            g, val = pre_ref[0, rows, :].astype(F32), pre_ref[1, rows, :].astype(F32)
            sg = jax.nn.sigmoid(g)
            silu = g * sg
            dact = _dot_nt(dh2b_ref[rows, :], wd_ref[...])
            dpre = (dact * val * (sg * (1.0 + g * (1.0 - sg))), dact * silu)
            dups = []
            for s in range(2):
                d = dpre[s]
                u = up_ref[s, rows, :].astype(F32)
                w = _taps(cw_ref.at[s])
                d1 = _shift_up(d, 1, nxt[s])
                d2 = _shift_up(d, 2, nxt[s])
                nxt[s] = d[:SUBLANES]
                for t, term in enumerate((d, d2 * u, d1 * u, d * u)):
                    sums[s][t] = sums[s][t] + jnp.sum(term, axis=0, keepdims=True)
                dups.append((d * w[2] + d1 * w[1] + d2 * w[0]).astype(BF16))
                stores.append((s, rows, dups[s]))
            acc_ref[rows, :] += _dot(dups[0], wu_ref[0]) + _dot(dups[1], wu_ref[1])
        for s, rows, dup in stores:
            dup_ref[s, rows, :] = dup
        for s in range(2):
            next_ref[s, j] = nxt[s]
            dfb_ref[s, j] += sums[s][0]
            for t in range(3):
                dfcw_ref[s, j, t] += sums[s][1 + t]

        @pl.when(j == N_FFN_BLK - 1)
        def _():
            dn, dgain = _rms_bwd(h1_ref[...], g2_ref[...], acc_ref[...])
            dh1 = dh2_ref[...] + dn
            dh1_ref[...] = dh1
            dh1b_ref[...] = dh1.astype(BF16)
            dg2_ref[...] += dgain

    rev = lambda i: n_t - 1 - i
    rows = lambda w: pl.BlockSpec((tm, w), lambda i, j: (rev(i), 0))
    const = lambda shape: pl.BlockSpec(shape, lambda i, j: (0,) * len(shape))
    pair = lambda *s: pl.BlockSpec((2, None) + s, lambda i, j: (0, j) + (0,) * len(s))
    upb = pl.BlockSpec((2, None, tm, FFN_BLK), lambda i, j: (0, j, rev(i), 0))
    body, more_specs, more = _ordered_behind(body, 9, after)
    return pl.pallas_call(
        body, name="ffn_bwd", grid=(n_t, N_FFN_BLK),
        in_specs=[rows(D_MODEL), rows(D_MODEL), rows(D_MODEL), const((1, D_MODEL)), upb, upb,
                  pair(FFN_BLK, D_MODEL), pair(3, 1, FFN_BLK),
                  pl.BlockSpec((None, FFN_BLK, D_MODEL), lambda i, j: (j, 0, 0))] + more_specs,
        out_specs=[upb, rows(D_MODEL), rows(D_MODEL),
                   const((2, N_FFN_BLK, 1, FFN_BLK)), const((2, N_FFN_BLK, 3, 1, FFN_BLK)), const((1, D_MODEL))],
        out_shape=[jax.ShapeDtypeStruct((2, N_FFN_BLK, SEQ, FFN_BLK), BF16), jax.ShapeDtypeStruct((SEQ, D_MODEL), F32),
                   jax.ShapeDtypeStruct((SEQ, D_MODEL), BF16), jax.ShapeDtypeStruct((2, N_FFN_BLK, 1, FFN_BLK), F32),
                   jax.ShapeDtypeStruct((2, N_FFN_BLK, 3, 1, FFN_BLK), F32), jax.ShapeDtypeStruct((1, D_MODEL), F32)],
        scratch_shapes=[pltpu.VMEM((tm, D_MODEL), F32), pltpu.VMEM((2, N_FFN_BLK, SUBLANES, FFN_BLK), F32)],
        compiler_params=_params(("arbitrary", "arbitrary")),
    )(dh2, dh2b, h1, g2, up, pre, w_up, fcw, w_down, *more)


def _grad_tn(a_list, b, out_rows, name, after=None):
    n = len(a_list)
    ncol = b.shape[1]

    def body(*refs):
        a_refs, b_ref, o_ref = refs[:n], refs[n], refs[n + 1]
        j = pl.program_id(0)
        for k in range(n):
            @pl.when(j == k)
            def _(k=k):
                o_ref[...] = _dot_tn(a_refs[k][...], b_ref[...]).astype(BF16)

    full = lambda shape: pl.BlockSpec(shape, lambda j: (0,) * len(shape))
    body, more_specs, more = _ordered_behind(body, n + 1, after)
    return pl.pallas_call(
        body, name=name, grid=(n,),
        in_specs=[full((SEQ, out_rows))] * n + [full((SEQ, ncol))] + more_specs,
        out_specs=pl.BlockSpec((None, out_rows, ncol), lambda j: (j, 0, 0)),
        out_shape=jax.ShapeDtypeStruct((n, out_rows, ncol), BF16),
        compiler_params=_params(("arbitrary",)),
    )(*a_list, b, *more)


def _grad_tn_blocked(a, b, name, per_step=2):
    nb, _, a_w = a.shape
    b_w = b.shape[-1]

    def body(a_ref, b_ref, o_ref):
        for p in range(per_step):
            o_ref[p] = _dot_tn(a_ref[p], b_ref[...]).astype(BF16)

    return pl.pallas_call(
        body, name=name, grid=(nb // per_step,),
        in_specs=[pl.BlockSpec((per_step, SEQ, a_w), lambda k: (k, 0, 0)), pl.BlockSpec((SEQ, b_w), lambda k: (0, 0))],
        out_specs=pl.BlockSpec((per_step, a_w, b_w), lambda k: (k, 0, 0)),
        out_shape=jax.ShapeDtypeStruct((nb, a_w, b_w), BF16),
        compiler_params=_params(("arbitrary",)),
    )(a, b)


def _out_bwd(dh1b, w_out, y_attn, gattn, after=None):
    tm = 1024
    n_t = SEQ // tm

    def body(dh_ref, wo_ref, y_ref, ga_ref, dycn_ref, dy_ref, dga_ref):
        @pl.when(pl.program_id(0) == 0)
        def _():
            dga_ref[...] = jnp.zeros_like(dga_ref)

        dycat = _dot_nt(dh_ref[...], wo_ref[...])
        dycn_ref[...] = dycat[:, :CONV_WIDTH]
        dy, dga = _rms_bwd(y_ref[...], ga_ref[...], dycat[:, CONV_WIDTH:])
        dy_ref[...] = dy
        dga_ref[...] += dga

    rows = lambda w: pl.BlockSpec((tm, w), lambda i: (i, 0))
    const = lambda shape: pl.BlockSpec(shape, lambda i: (0,) * len(shape))
    body, more_specs, more = _ordered_behind(body, 4, after)
    return pl.pallas_call(
        body, name="out_bwd", grid=(n_t,),
        in_specs=[rows(D_MODEL), const((D_MODEL, D_MODEL)), rows(ATTN_WIDTH), const((1, ATTN_WIDTH))] + more_specs,
        out_specs=[rows(CONV_WIDTH), rows(ATTN_WIDTH), const((1, ATTN_WIDTH))],
        out_shape=[jax.ShapeDtypeStruct((SEQ, CONV_WIDTH), F32), jax.ShapeDtypeStruct((SEQ, ATTN_WIDTH), F32),
                   jax.ShapeDtypeStruct((1, ATTN_WIDTH), F32)],
        compiler_params=_params(("arbitrary",)),
    )(dh1b, w_out, y_attn, gattn, *more)


def _attn_bwd(qn, kn, v, dy, probs, psinks, bkt, after=None):
    n_b = SEQ // BLK

    def body(q_ref, k_ref, v_ref, dy_ref, p_ref, ps_ref, bkt_ref,
             dq_ref, dk_ref, dv_ref, dtbl_ref, dsink_ref, dbias_ref, dsacc_ref):
        i = pl.program_id(0)

        @pl.when(i == 0)
        def _():
            dbias_ref[...] = jnp.zeros_like(dbias_ref)
            dsacc_ref[...] = jnp.zeros_like(dsacc_ref)
            dk_ref[...] = jnp.zeros_like(dk_ref)
            dv_ref[...] = jnp.zeros_like(dv_ref)

        kb, prev, cur = _band_rows(k_ref, i)
        vb, _, _ = _band_rows(v_ref, i)
        upper, _ = _band_masks(i)
        q = q_ref[...]
        dy = dy_ref[...]
        psink = ps_ref[...]
        lane = lax.broadcasted_iota(jnp.int32, (BLK, 128), 1)
        dsink = jnp.zeros((BLK, 128), F32)
        dqs, dks, dvs = [], [], []
        for g in range(N_HEADS // GQA_GROUP):
            kv = slice(HEAD_DIM * g, HEAD_DIM * (g + 1))
            qg = _stack_heads(q, g)
            dog = _stack_heads(dy, g).astype(BF16)
            pb = p_ref[g]
            pg = pb.astype(F32)
            dprobs = _fold(_dot_nt(dog, vb[:, kv]), upper)
            dvs.append(_dot_tn(_unfold(pb, upper), dog))
            dsum = jnp.sum(pg * dprobs, axis=-1, keepdims=True)
            dlogits = pg * (dprobs - dsum)
            for t in range(GQA_GROUP):
                dsink = jnp.where(lane == GQA_GROUP * g + t, -psink * dsum[BLK * t:BLK * (t + 1)], dsink)
            dbias_ref[g] += dlogits
            ds = _unfold(dlogits * (HEAD_DIM ** -0.5), upper).astype(BF16)
            dqs.append(_dot(ds, kb[:, kv]))
            dks.append(_dot_tn(ds, qg))
        dsacc_ref[...] += dsink
        dq_ref[...] = _unstack_heads(dqs)
        dkb = jnp.concatenate(dks, axis=-1)
        dvb = jnp.concatenate(dvs, axis=-1)
        dk_ref[pl.ds(prev, BLK), :] += dkb[:BLK]
        dk_ref[pl.ds(cur, BLK), :] += dkb[BLK:]
        dv_ref[pl.ds(prev, BLK), :] += dvb[:BLK]
        dv_ref[pl.ds(cur, BLK), :] += dvb[BLK:]

        @pl.when(i == n_b - 1)
        def _():
            bkt = bkt_ref[...]
            row8 = lax.broadcasted_iota(jnp.int32, (N_HEADS, 128), 0)
            lane8 = lax.broadcasted_iota(jnp.int32, (N_HEADS, 128), 1)
            acc = jnp.zeros((N_HEADS, 128), F32)
            for h in range(N_HEADS):
                rows = slice(BLK * (h % GQA_GROUP), BLK * (h % GQA_GROUP + 1))
                dbh = dbias_ref[h // GQA_GROUP, rows, :]
                for b in range(NUM_BUCKETS):
                    acc = jnp.where((row8 == h) & (lane8 == b), jnp.sum(jnp.where(bkt == b, dbh, 0.0)), acc)
            dsink_ref[...] = jnp.sum(dsacc_ref[...], axis=0, keepdims=True)
            dtbl_ref[...] = acc

    const = lambda shape: pl.BlockSpec(shape, lambda i: (0,) * len(shape))
    rows = lambda w: pl.BlockSpec((BLK, w), lambda i: (i, 0))
    n_g = N_HEADS // GQA_GROUP
    body, more_specs, more = _ordered_behind(body, 7, after)
    return pl.pallas_call(
        body, name="attn_bwd", grid=(n_b,),
        in_specs=[rows(ATTN_WIDTH), const((SEQ, KV_WIDTH)), const((SEQ, KV_WIDTH)), rows(ATTN_WIDTH),
                  pl.BlockSpec((None, n_g, GROUP_ROWS, BLK), lambda i: (i, 0, 0, 0)), rows(128),
                  const((BLK, BLK))] + more_specs,
        out_specs=[rows(ATTN_WIDTH), const((SEQ, KV_WIDTH)), const((SEQ, KV_WIDTH)), const((N_HEADS, 128)), const((1, 128))],
        out_shape=[jax.ShapeDtypeStruct((SEQ, ATTN_WIDTH), F32), jax.ShapeDtypeStruct((SEQ, KV_WIDTH), F32),
                   jax.ShapeDtypeStruct((SEQ, KV_WIDTH), F32), jax.ShapeDtypeStruct((N_HEADS, 128), F32),
                   jax.ShapeDtypeStruct((1, 128), F32)],
        scratch_shapes=[pltpu.VMEM((n_g, GROUP_ROWS, BLK), F32), pltpu.VMEM((BLK, 128), F32)],
        compiler_params=_params(("arbitrary",)),
    )(qn, kn, v, dy, probs, psinks, bkt, *more)


def _mix_in_bwd(x, dh1, proj, dycn, dqn, dkn, dv, w_in_t, conv_w, g1, gq, gk, gconv):
    tm = 512
    n_t = SEQ // tm
    halo_blocks = tm // SUBLANES

    def body(x_ref, dh1_ref, proj_ref, halo_ref, dycn_ref, dqn_ref, dkn_ref, dv_ref, w_ref, cw_ref,
             g1_ref, gq_ref, gk_ref, gc_ref,
             dx_ref, dproj_ref, dcw_ref, dgc_ref, dgq_ref, dgk_ref, dg1_ref, next_ref):
        i = pl.program_id(0)
        first_tile = i == n_t - 1

        @pl.when(i == 0)
        def _():
            for r in (dcw_ref, dgc_ref, dgq_ref, dgk_ref, dg1_ref, next_ref):
                r[...] = jnp.zeros_like(r)

        proj = proj_ref[...]
        hp = halo_ref[...]
        gate_b = proj[:, 0:CONV_WIDTH]
        gate_c = proj[:, CONV_WIDTH:2 * CONV_WIDTH]
        hc = proj[:, 2 * CONV_WIDTH:3 * CONV_WIDTH]
        a = gate_c * hc
        a_halo = jnp.where(first_tile, 0.0, hp[:, CONV_WIDTH:2 * CONV_WIDTH] * hp[:, 2 * CONV_WIDTH:3 * CONV_WIDTH])
        cw = _taps(cw_ref[...])
        cv, a2, a1 = _conv3(a, cw, a_halo)
        dyc, dgc = _rms_bwd(gate_b * cv, gc_ref[...], dycn_ref[...])
        dgc_ref[...] += dgc
        dcv = dyc * gate_b
        dcw_ref[...] += jnp.concatenate(
            [jnp.sum(dcv * a2, axis=0, keepdims=True), jnp.sum(dcv * a1, axis=0, keepdims=True),
             jnp.sum(dcv * a, axis=0, keepdims=True)], axis=0)
        da = _conv3_bwd_input(dcv, cw, next_ref[...])
        next_ref[...] = dcv[:SUBLANES]
        q0 = 3 * CONV_WIDTH
        k0 = q0 + ATTN_WIDTH
        dq, dgq = _head_norm_bwd(proj[:, q0:k0], gq_ref[...], dqn_ref[...], N_HEADS)
        dk, dgk = _head_norm_bwd(proj[:, k0:k0 + KV_WIDTH], gk_ref[...], dkn_ref[...], 2)
        dgq_ref[...] += dgq
        dgk_ref[...] += dgk
        dproj = jnp.concatenate([dyc * cv, da * hc, da * gate_c, dq, dk, dv_ref[...]], axis=-1).astype(BF16)
        dproj_ref[...] = dproj
        du1 = _dot(dproj, w_ref[...])
        xv = x_ref[...]
        dn, dg1 = _rms_bwd(xv, g1_ref[...], du1)
        dx_ref[...] = dh1_ref[...] + dn
        dg1_ref[...] += dg1

    rev = lambda i: n_t - 1 - i
    rows = lambda w: pl.BlockSpec((tm, w), lambda i: (rev(i), 0))
    const = lambda shape: pl.BlockSpec(shape, lambda i: (0,) * len(shape))
    halo = pl.BlockSpec((SUBLANES, IN_WIDTH), lambda i: (jnp.maximum(rev(i) * halo_blocks - 1, 0), 0))
    return pl.pallas_call(
        body, name="mix_in_bwd", grid=(n_t,),
        in_specs=[rows(D_MODEL), rows(D_MODEL), rows(IN_WIDTH), halo, rows(CONV_WIDTH), rows(ATTN_WIDTH), rows(KV_WIDTH),
                  rows(KV_WIDTH), const((IN_WIDTH, D_MODEL)), const((3, CONV_WIDTH)), const((1, D_MODEL)),
                  const((1, HEAD_DIM)), const((1, HEAD_DIM)), const((1, CONV_WIDTH))],
        out_specs=[rows(D_MODEL), rows(IN_WIDTH), const((3, CONV_WIDTH)), const((1, CONV_WIDTH)),
                   const((1, HEAD_DIM)), const((1, HEAD_DIM)), const((1, D_MODEL))],
        out_shape=[jax.ShapeDtypeStruct((SEQ, D_MODEL), F32), jax.ShapeDtypeStruct((SEQ, IN_WIDTH), BF16),
                   jax.ShapeDtypeStruct((3, CONV_WIDTH), F32),
                   jax.ShapeDtypeStruct((1, CONV_WIDTH), F32), jax.ShapeDtypeStruct((1, HEAD_DIM), F32),
                   jax.ShapeDtypeStruct((1, HEAD_DIM), F32), jax.ShapeDtypeStruct((1, D_MODEL), F32)],
        scratch_shapes=[pltpu.VMEM((SUBLANES, CONV_WIDTH), F32)],
        compiler_params=_params(("arbitrary",)),
    )(x, dh1, proj, proj, dycn, dqn, dkn, dv, w_in_t, conv_w, g1, gq, gk, gconv)


def _grad_w_in(dproj, u1, after=None):
    bw = 768

    def body(a_ref, b_ref, o_ref):
        o_ref[...] = _dot_tn(a_ref[...], b_ref[...]).astype(BF16)

    body, more_specs, more = _ordered_behind(body, 2, after)
    return pl.pallas_call(
        body, name="grad_w_in", grid=(IN_WIDTH // bw,),
        in_specs=[pl.BlockSpec((SEQ, bw), lambda k: (0, k)), pl.BlockSpec((SEQ, D_MODEL), lambda k: (0, 0))] + more_specs,
        out_specs=pl.BlockSpec((bw, D_MODEL), lambda k: (k, 0)),
        out_shape=jax.ShapeDtypeStruct((IN_WIDTH, D_MODEL), BF16),
        compiler_params=_params(("arbitrary",)),
    )(dproj, u1, *more)


def _adamw_math(w, g, m, v):
    m = ADAM_B1 * m + (1.0 - ADAM_B1) * g
    v = ADAM_B2 * v + (1.0 - ADAM_B2) * (g * g)
    m_hat = m / (1.0 - ADAM_B1 ** ADAM_STEP)
    v_hat = v / (1.0 - ADAM_B2 ** ADAM_STEP)
    return -ADAM_LR * (m_hat / (jnp.sqrt(v_hat) + ADAM_EPS) + ADAM_WD * w), m, v


_ROW_G1, _ROW_G2, _ROW_OUT_NORMS, _ROW_FFN_B, _ROW_GQ, _ROW_GK, _ROW_SINKS, _ROW_LOSS, _ROW_TABLE = 0, 1, 2, 3, 11, 12, 13, 14, 16
SMALL_ROWS, SMALL_COLS = 24, 1024
_SMALL_NAMES = ("norm_mix_g", "norm_ffn_g", "out_norm_conv_g", "out_norm_attn_g", "ffn_conv_b", "q_norm_g", "k_norm_g",
                "sinks", "rel_bias_table")


def _pack_small_grads(dg1, dg2, dgconv, dgattn, dfb, dgq, dgk, dsinks, dtbl_t, loss_acc):
    def body(dg1_ref, dg2_ref, dgc_ref, dga_ref, dfb_ref, dgq_ref, dgk_ref, ds_ref, dt_ref, loss_ref, o_ref, all_ref):
        o_ref[...] = jnp.zeros_like(o_ref)
        o_ref[_ROW_G1:_ROW_G1 + 1, :] = dg1_ref[...]
        o_ref[_ROW_G2:_ROW_G2 + 1, :] = dg2_ref[...]
        o_ref[_ROW_OUT_NORMS:_ROW_OUT_NORMS + 1, 0:CONV_WIDTH] = dgc_ref[...]
        o_ref[_ROW_OUT_NORMS:_ROW_OUT_NORMS + 1, CONV_WIDTH:] = dga_ref[...]
        for k in range(N_DEV):
            o_ref[_ROW_FFN_B + k:_ROW_FFN_B + k + 1, 0:FFN_BLK] = dfb_ref[k // N_FFN_BLK, k % N_FFN_BLK]
        o_ref[_ROW_GQ:_ROW_GQ + 1, 0:HEAD_DIM] = dgq_ref[...]
        o_ref[_ROW_GK:_ROW_GK + 1, 0:HEAD_DIM] = dgk_ref[...]
        o_ref[_ROW_SINKS:_ROW_SINKS + 1, 0:128] = ds_ref[...]
        o_ref[_ROW_LOSS:_ROW_LOSS + 1, 0:128] = loss_ref[0:1, :]
        o_ref[_ROW_TABLE:_ROW_TABLE + N_HEADS, 0:128] = dt_ref[...]
        for s in range(N_DEV):
            all_ref[s] = o_ref[...]

    return pl.pallas_call(
        body, name="pack_small_grads",
        out_shape=[jax.ShapeDtypeStruct((SMALL_ROWS, SMALL_COLS), F32),
                   jax.ShapeDtypeStruct((N_DEV, SMALL_ROWS, SMALL_COLS), F32)],
    )(dg1, dg2, dgconv, dgattn, dfb, dgq, dgk, dsinks, dtbl_t, loss_acc)


def _adamw_small(recv, params, after):
    names = _SMALL_NAMES
    n = len(names)

    def grad_of(g, name, k=None):
        if name == "norm_mix_g":
            return g[_ROW_G1:_ROW_G1 + 1, :]
        if name == "norm_ffn_g":
            return g[_ROW_G2:_ROW_G2 + 1, :]
        if name == "out_norm_conv_g":
            return g[_ROW_OUT_NORMS:_ROW_OUT_NORMS + 1, 0:CONV_WIDTH]
        if name == "out_norm_attn_g":
            return g[_ROW_OUT_NORMS:_ROW_OUT_NORMS + 1, CONV_WIDTH:]
        if name == "ffn_conv_b":
            return g[_ROW_FFN_B + k:_ROW_FFN_B + k + 1, 0:FFN_BLK]
        if name == "q_norm_g":
            return g[_ROW_GQ:_ROW_GQ + 1, 0:HEAD_DIM]
        if name == "k_norm_g":
            return g[_ROW_GK:_ROW_GK + 1, 0:HEAD_DIM]
        if name == "sinks":
            return g[_ROW_SINKS:_ROW_SINKS + 1, 0:N_HEADS]
        return g[_ROW_TABLE:_ROW_TABLE + N_HEADS, 0:NUM_BUCKETS]

    def body(r_ref, *refs):
        ins, outs, loss_ref = refs[:3 * n], refs[3 * n:7 * n], refs[7 * n]
        g = r_ref[0]
        for s in range(1, N_DEV):
            g = g + r_ref[s]
        loss_ref[...] = g[_ROW_LOSS:_ROW_LOSS + 1, 0:128]
        for i, name in enumerate(names):
            w_ref, m_ref, v_ref = ins[3 * i:3 * i + 3]
            o = outs[4 * i:4 * i + 4]
            cols = [slice(FFN_BLK * k, FFN_BLK * (k + 1)) for k in range(N_DEV)] if name == "ffn_conv_b" else [slice(None)]
            for k, cs in enumerate(cols):
                gk = grad_of(g, name, k)
                d, m2, v2 = _adamw_math(w_ref[:, cs], gk, m_ref[:, cs], v_ref[:, cs])
                o[0][:, cs], o[1][:, cs], o[2][:, cs], o[3][:, cs] = gk, d, m2, v2

    flat = [a for name in names for a in params[name]]
    body, more_specs, more = _ordered_behind(body, 1 + 3 * n, after)
    vmem = pl.BlockSpec(memory_space=pltpu.VMEM)
    out = pl.pallas_call(
        body, name="adamw_small",
        in_specs=[vmem] * (1 + 3 * n) + more_specs,
        out_shape=[jax.ShapeDtypeStruct(params[name][0].shape, F32) for name in names for _ in range(4)]
        + [jax.ShapeDtypeStruct((1, 128), F32)],
        compiler_params=pltpu.CompilerParams(vmem_limit_bytes=VMEM_LIMIT),
    )(recv, *flat, *more)
    return {name: tuple(out[4 * i:4 * i + 4]) for i, name in enumerate(names)}, out[4 * n]


def _adamw_direct(w, m, v, own, recv, me, name, row_blocks=1, after=None):
    rb = w.shape[0] // row_blocks
    cols = w.shape[1]

    def body(me_ref, w_ref, m_ref, v_ref, o_ref, r_ref, g_o, d_o, m_o, v_o):
        g = o_ref[...].astype(F32)
        for s in range(N_DEV - 1):
            g = g + r_ref[s].astype(F32)
        g_o[...] = g
        d_o[...], m_o[...], v_o[...] = _adamw_math(w_ref[...], g, m_ref[...], v_ref[...])

    blk = pl.BlockSpec((rb, cols), lambda i, me_ref: (i, 0))
    oblk = pl.BlockSpec((None, rb, cols), lambda i, me_ref: (me_ref[0], i, 0))
    rblk = pl.BlockSpec((N_DEV - 1, rb, cols), lambda i, me_ref: (0, i, 0))
    body, more_specs, more = _ordered_behind(body, 6, after)
    return pl.pallas_call(
        body, name=name,
        grid_spec=pltpu.PrefetchScalarGridSpec(num_scalar_prefetch=1, grid=(row_blocks,),
                                               in_specs=[blk, blk, blk, oblk, rblk] + more_specs, out_specs=[blk] * 4),
        out_shape=[jax.ShapeDtypeStruct(w.shape, F32)] * 4,
        compiler_params=_params(("arbitrary",)),
    )(me, w, m, v, own, recv, *more)


def _adamw(w, m, v, part, recv, chip, name, row_blocks=1, after=None):
    rb = w.shape[0] // row_blocks
    tail = w.shape[1:]
    zeros = (0,) * len(tail)

    def body(chip_ref, w_ref, m_ref, v_ref, p_ref, r_ref, g_o, d_o, m_o, v_o):
        g = p_ref[...].astype(F32)
        for s in range(3):
            g = g + r_ref[s].astype(F32)
        g_o[...] = g
        d_o[...], m_o[...], v_o[...] = _adamw_math(w_ref[...], g, m_ref[...], v_ref[...])

    blk = pl.BlockSpec((rb,) + tail, lambda i, chip_ref: (i,) + zeros)
    pblk = pl.BlockSpec((None, rb) + tail, lambda i, chip_ref: (chip_ref[0], i) + zeros)
    rblk = pl.BlockSpec((3, rb) + tail, lambda i, chip_ref: (0, i) + zeros)
    body, more_specs, more = _ordered_behind(body, 6, after)
    return pl.pallas_call(
        body, name=name,
        grid_spec=pltpu.PrefetchScalarGridSpec(num_scalar_prefetch=1, grid=(row_blocks,),
                                               in_specs=[blk, blk, blk, pblk, rblk] + more_specs, out_specs=[blk] * 4),
        out_shape=[jax.ShapeDtypeStruct(w.shape, F32)] * 4,
        compiler_params=_params(("arbitrary",)),
    )(chip, w, m, v, part, recv, *more)


def kernel(x, norm_mix_g, w_in, conv_w, q_norm_g, k_norm_g, rel_bias_table, sinks, out_norm_conv_g, out_norm_attn_g, w_out, norm_ffn_g, w_up, ffn_conv_w, ffn_conv_b, w_down, loss_target, m_norm_mix_g, m_w_in, m_conv_w, m_q_norm_g, m_k_norm_g, m_rel_bias_table, m_sinks, m_out_norm_conv_g, m_out_norm_attn_g, m_w_out, m_norm_ffn_g, m_w_up, m_ffn_conv_w, m_ffn_conv_b, m_w_down, v_norm_mix_g, v_w_in, v_conv_w, v_q_norm_g, v_k_norm_g, v_rel_bias_table, v_sinks, v_out_norm_conv_g, v_out_norm_attn_g, v_w_out, v_norm_ffn_g, v_w_up, v_ffn_conv_w, v_ffn_conv_b, v_w_down):
    p = dict(norm_mix_g=norm_mix_g, w_in=w_in, conv_w=conv_w, q_norm_g=q_norm_g, k_norm_g=k_norm_g,
             rel_bias_table=rel_bias_table, sinks=sinks, out_norm_conv_g=out_norm_conv_g, out_norm_attn_g=out_norm_attn_g,
             w_out=w_out, norm_ffn_g=norm_ffn_g, w_up=w_up, ffn_conv_w=ffn_conv_w, ffn_conv_b=ffn_conv_b, w_down=w_down)
    m = dict(norm_mix_g=m_norm_mix_g, w_in=m_w_in, conv_w=m_conv_w, q_norm_g=m_q_norm_g, k_norm_g=m_k_norm_g,
             rel_bias_table=m_rel_bias_table, sinks=m_sinks, out_norm_conv_g=m_out_norm_conv_g,
             out_norm_attn_g=m_out_norm_attn_g, w_out=m_w_out, norm_ffn_g=m_norm_ffn_g, w_up=m_w_up,
             ffn_conv_w=m_ffn_conv_w, ffn_conv_b=m_ffn_conv_b, w_down=m_w_down)
    v = dict(norm_mix_g=v_norm_mix_g, w_in=v_w_in, conv_w=v_conv_w, q_norm_g=v_q_norm_g, k_norm_g=v_k_norm_g,
             rel_bias_table=v_rel_bias_table, sinks=v_sinks, out_norm_conv_g=v_out_norm_conv_g,
             out_norm_attn_g=v_out_norm_attn_g, w_out=v_w_out, norm_ffn_g=v_norm_ffn_g, w_up=v_w_up,
             ffn_conv_w=v_ffn_conv_w, ffn_conv_b=v_ffn_conv_b, w_down=v_w_down)

    xs, tgt = x[0], loss_target[0]
    g1, g2, gq, gk, gconv, gattn = norm_mix_g, norm_ffn_g, q_norm_g, k_norm_g, out_norm_conv_g, out_norm_attn_g
    ix, iy, ic = _coords()
    core = ic.astype(jnp.int32).reshape(1)
    chip = (2 * ix + iy).astype(jnp.int32).reshape(1)
    me = _lin(ix, iy, ic).astype(jnp.int32).reshape(1)
    bkt = jnp.asarray(_bucket_map())
    tr = lambda a: a[0].T
    taps = lambda a: jnp.transpose(a, (1, 0, 2))
    tbl_t = rel_bias_table.T

    wi_l, cw_l = _place_shards(me, [tr(w_in), taps(conv_w)], [BF16, F32], "place_mixer_shards")
    finish_a, token_a = _all_gather_split([wi_l, cw_l], "mixer", None)
    wo_l, wu_l, wd_l, fcw_l = _place_shards(me, [w_out[0], tr(w_up), w_down[0], taps(ffn_conv_w)],
                                            [BF16, BF16, BF16, F32], "place_ffn_shards", after=token_a)
    ffn_stage2, ffn_stage3, token_b = _all_gather_tree([wo_l, wu_l, wd_l, fcw_l], "ffn", token_a)
    wi_g, cw_g = finish_a(token_b)
    w_in_t = wi_g.reshape(IN_WIDTH, D_MODEL)
    conv_w_f = jnp.transpose(cw_g[:, :, 0, :], (1, 0, 2)).reshape(3, CONV_WIDTH)

    proj, u1, ycn, qn, kn, vv = _mix_in_fwd(xs, g1, w_in_t, conv_w_f, gq, gk, gconv)
    token_b2 = ffn_stage2(ycn)
    y_attn, yan, probs, psinks = _attn_fwd(qn, kn, vv, tbl_t, sinks, bkt, gattn, after=token_b2)
    wo_g, wu_g, wd_g, fcw_g = ffn_stage3(yan)
    w_out_f = wo_g.reshape(D_MODEL, D_MODEL)
    w_down_f = wd_g.reshape(N_FFN_BLK, FFN_BLK, D_MODEL)
    w_up_f = wu_g.reshape(2, N_FFN_BLK, FFN_BLK, D_MODEL)
    fcw_f = fcw_g.reshape(2, N_FFN_BLK, 3, 1, FFN_BLK)
    fcb = ffn_conv_b.reshape(2, N_FFN_BLK, 1, FFN_BLK)
    h1, u2, up, pre, act, dh2, dh2b, loss_acc = _ffn_fwd(xs, ycn, yan, w_out_f, g2, w_up_f, fcw_f, fcb, w_down_f, tgt)

    dw_down = _grad_tn_blocked(act, dh2b, "grad_w_down").reshape(N_DEV, D_FF // N_DEV, D_MODEL)
    plan_d, slots_d = _scatter_plan(1)
    d_sem = _split_start("scatter_w_down_start", [dw_down], [lax.empty((N_DEV - 1,) + dw_down.shape[1:], BF16)],
                         plan_d, None, _ALL_FOR_W_DOWN)
    dup, dh1, dh1b, dfb, dfcw, dg2 = _ffn_bwd(dh2, dh2b, h1, g2, up, pre, w_up_f, fcw_f, w_down_f, after=d_sem[4])
    dw_up = _grad_tn_blocked(dup.reshape(N_DEV, SEQ, FFN_BLK), u2, "grad_w_up")
    dw_out = _grad_tn([ycn, yan], dh1b, CONV_WIDTH, "grad_w_out").reshape(N_DEV, D_MODEL // N_DEV, D_MODEL)
    out_bwd = {}

    def behind_ffn(token):
        out_bwd["r"] = _out_bwd(dh1b, w_out_f, y_attn, gattn, after=token)
        return out_bwd["r"][0]

    finish_ffn, token_ffn = _reduce_scatter_split(
        [dw_up, dw_out, dfcw.reshape(N_DEV, 3, 1, FFN_BLK)], "ffn", core, behind_ffn)
    dycn, dy_attn, dgattn = out_bwd["r"]
    dqn, dkn, dv, dtbl_t, dsinks = _attn_bwd(qn, kn, vv, dy_attn, probs, psinks, bkt, after=token_ffn)
    dx, dproj, dcw, dgconv, dgq, dgk, dg1 = _mix_in_bwd(xs, dh1, proj, dycn, dqn, dkn, dv, w_in_t, conv_w_f,
                                                         g1, gq, gk, gconv)
    packed, packed_all = _pack_small_grads(dg1, dg2, dgconv, dgattn, dfb, dgq, dgk, dsinks, dtbl_t, loss_acc)
    plan_s, slots_s = _broadcast_plan()
    s_sem, r_sem, src_s, land_s, token_s = _split_start("gather_small_start", [packed], [packed_all], plan_s, None,
                                                        _ALL_FOR_SMALL)
    dw_in_t = _grad_w_in(dproj, u1, after=token_s).reshape(N_DEV, IN_WIDTH // N_DEV, D_MODEL)
    dcw_b = jnp.transpose(dcw.reshape(3, N_DEV, 1, CONV_WIDTH // N_DEV), (1, 0, 2, 3))
    adam = {}
    ffn_got = {}

    def behind_mixer(token):
        ffn_got["r"] = finish_ffn(token)
        return ffn_got["r"][1][0]

    finish_mixer, token_mixer = _reduce_scatter_split([dw_in_t, dcw_b], "mixer", core, behind_mixer)
    (p_wu, p_wo, p_fcw), (r_wu, r_wo, r_fcw) = ffn_got["r"]
    (own_wd,), (r_wd,) = _split_wait("scatter_w_down_wait", d_sem[0], d_sem[1], d_sem[2], d_sem[3], plan_d, slots_d,
                                     token_mixer)
    adam["w_down"] = _adamw_direct(w_down[0], m_w_down[0], v_w_down[0], own_wd, r_wd, me, "adamw_w_down", row_blocks=2)
    adam_up = _adamw(tr(w_up), tr(m_w_up), tr(v_w_up), p_wu, r_wu, chip, "adamw_w_up", row_blocks=4,
                     after=adam["w_down"][0])
    adam["w_out"] = _adamw(w_out[0], m_w_out[0], v_w_out[0], p_wo, r_wo, chip, "adamw_w_out", after=adam_up[0])
    adam_fcw = _adamw(taps(ffn_conv_w), taps(m_ffn_conv_w), taps(v_ffn_conv_w), p_fcw, r_fcw, chip, "adamw_ffn_conv_w",
                      after=adam["w_out"][0])
    _, (r_small,) = _split_wait("gather_small_wait", s_sem, r_sem, src_s, land_s, plan_s, slots_s, adam_fcw[0])
    small_in = {k: (p[k], m[k], v[k]) for k in _SMALL_NAMES}
    small_in["rel_bias_table"] = (tbl_t, m_rel_bias_table.T, v_rel_bias_table.T)
    small_out, loss_row = _adamw_small(r_small, small_in, None)
    (p_wi, p_cw), (r_wi, r_cw) = finish_mixer(loss_row)
    adam_in = _adamw(tr(w_in), tr(m_w_in), tr(v_w_in), p_wi, r_wi, chip, "adamw_w_in")
    adam_cw = _adamw(taps(conv_w), taps(m_conv_w), taps(v_conv_w), p_cw, r_cw, chip, "adamw_conv_w")

    res = {k: tuple(a[None] for a in t) for k, t in adam.items()}
    res["w_up"] = tuple(a.T[None] for a in adam_up)
    res["w_in"] = tuple(a.T[None] for a in adam_in)
    res["ffn_conv_w"] = tuple(taps(a) for a in adam_fcw)
    res["conv_w"] = tuple(taps(a) for a in adam_cw)
    res.update(small_out)
    res["rel_bias_table"] = tuple(a.T for a in small_out["rel_bias_table"])
    loss = loss_row[0, 0]
    order = ("norm_mix_g", "w_in", "conv_w", "q_norm_g", "k_norm_g", "rel_bias_table", "sinks", "out_norm_conv_g",
             "out_norm_attn_g", "w_out", "norm_ffn_g", "w_up", "ffn_conv_w", "ffn_conv_b", "w_down")
    return (loss, dx[None], *[res[k][0] for k in order], *[res[k][1] for k in order],
            *[res[k][2] for k in order], *[res[k][3] for k in order])
```

```python
import math

import numpy as np
import jax
import jax.numpy as jnp
from jax import lax
from jax.experimental import pallas as pl
from jax.experimental.pallas import tpu as pltpu

F32 = jnp.float32
BF16 = jnp.bfloat16

SEQ = 2048
D_MODEL = 1024
CONV_WIDTH = 512
ATTN_WIDTH = 512
KV_WIDTH = 128
HEAD_DIM = 64
N_HEADS = 8
GQA_GROUP = 4
IN_WIDTH = 2304
D_FF = 2816
BLK = 128
NUM_BUCKETS = 32
EPS = 1e-6
NEG_INF = -1e30
ADAM_LR = 0.001
ADAM_B1 = 0.9
ADAM_B2 = 0.999
ADAM_EPS = 1e-08
ADAM_WD = 0.01
ADAM_STEP = 10

N_DEV = 8
FFN_BLK = 2 * D_FF // N_DEV
N_FFN_BLK = D_FF // FFN_BLK
SUBLANES = 8
VMEM_LIMIT = 56 * 1024 * 1024

_MESH = pl.DeviceIdType.MESH
_ANY = pl.BlockSpec(memory_space=pl.ANY)


def _params(sem):
    return pltpu.CompilerParams(dimension_semantics=sem, vmem_limit_bytes=VMEM_LIMIT)


def _ordered_behind(body, pos, after):
    if after is None:
        return body, [], []
    return (lambda *refs: body(*refs[:pos], *refs[pos + 1:])), [_ANY], [after]


def _dot(a, b):
    return jnp.dot(a, b, preferred_element_type=F32)


def _dot_nt(a, b):
    return lax.dot_general(a, b, (((1,), (1,)), ((), ())), preferred_element_type=F32)


def _dot_tn(a, b):
    return lax.dot_general(a, b, (((0,), (0,)), ((), ())), preferred_element_type=F32)


def _shift_down(x, s, halo):
    r = pltpu.roll(x, s, axis=0)
    hr = pltpu.roll(halo, s, axis=0)
    row = lax.broadcasted_iota(jnp.int32, halo.shape, 0)
    top = jnp.where(row < s, hr, r[:SUBLANES])
    return jnp.concatenate([top, r[SUBLANES:]], axis=0)


def _shift_up(x, s, halo):
    n = x.shape[0]
    r = pltpu.roll(x, n - s, axis=0)
    hr = pltpu.roll(halo, SUBLANES - s, axis=0)
    row = lax.broadcasted_iota(jnp.int32, halo.shape, 0)
    bot = jnp.where(row >= SUBLANES - s, hr, r[n - SUBLANES:])
    return jnp.concatenate([r[:n - SUBLANES], bot], axis=0)


def _taps(w):
    return (w[0], w[1], w[2]) if len(w.shape) == 3 else (w[0:1], w[1:2], w[2:3])


def _conv3(x, w, halo):
    x2 = _shift_down(x, 2, halo)
    x1 = _shift_down(x, 1, halo)
    return x2 * w[0] + x1 * w[1] + x * w[2], x2, x1


def _conv3_bwd_input(dy, w, halo_next):
    return dy * w[2] + _shift_up(dy, 1, halo_next) * w[1] + _shift_up(dy, 2, halo_next) * w[0]


def _rstd(x):
    return lax.rsqrt(jnp.mean(x * x, axis=-1, keepdims=True) + EPS)


def _rms_bwd(x, g, dy):
    r = _rstd(x)
    n = x * r
    dn = dy * g
    dx = r * (dn - n * jnp.mean(dn * n, axis=-1, keepdims=True))
    return dx, jnp.sum(dy * n, axis=0, keepdims=True)


def _head_mean(x):
    width = x.shape[-1]
    ri = lax.broadcasted_iota(jnp.int32, (width, width), 0) // HEAD_DIM
    ci = lax.broadcasted_iota(jnp.int32, (width, width), 1) // HEAD_DIM
    ones = jnp.where(ri == ci, 1.0, 0.0).astype(BF16)
    hi = x.astype(BF16)
    lo = (x - hi.astype(F32)).astype(BF16)
    return (_dot(hi, ones) + _dot(lo, ones)) * (1.0 / HEAD_DIM)


def _head_norm(x, g, heads):
    return x * lax.rsqrt(_head_mean(x * x) + EPS) * jnp.tile(g, (1, heads))


def _head_norm_bwd(x, g, dy, heads):
    r = lax.rsqrt(_head_mean(x * x) + EPS)
    n = x * r
    dn = dy * jnp.tile(g, (1, heads))
    dx = r * (dn - n * _head_mean(dn * n))
    per_lane = jnp.sum(dy * n, axis=0, keepdims=True)
    dg = per_lane[:, 0:HEAD_DIM]
    for h in range(1, heads):
        dg = dg + per_lane[:, HEAD_DIM * h:HEAD_DIM * (h + 1)]
    return dx, dg


def _bucket_map():
    q = np.arange(BLK)[:, None]
    j = np.arange(BLK)[None, :]
    n = np.where(j > q, q + BLK - j, q - j)
    nf = np.maximum(n, 1).astype(np.float32)
    max_exact = NUM_BUCKETS // 2
    large = max_exact + (np.log(nf / max_exact) / math.log(BLK / max_exact) * (NUM_BUCKETS - max_exact)).astype(np.int32)
    large = np.minimum(large, NUM_BUCKETS - 1)
    return np.where(n < max_exact, n, large).astype(np.int32)


def _coords():
    return lax.axis_index("x"), lax.axis_index("y"), lax.axis_index("c")


def _lin(px, py, pc):
    return 4 * px + 2 * py + pc


def _chips(x, y):
    return [(1 - x, y), (x, 1 - y), (1 - x, 1 - y)]


def _peers(x, y, c):
    return [(1 - x if r & 4 else x, 1 - y if r & 2 else y, 1 - c if r & 1 else c) for r in range(1, N_DEV)]


_HBM = pl.BlockSpec(memory_space=pltpu.HBM)
_SEM = pl.BlockSpec(memory_space=pltpu.SEMAPHORE)
_EFFECT = pltpu.SideEffectType.DATAFLOW_SIDE_EFFECTING


def _in_hbm(a):
    return pltpu.with_memory_space_constraint(a, pltpu.HBM)


_SIBLING = (1, lambda x, y, c: [(x, y, 1 - c)])
_SIBLING_AND_CHIPS = (2, lambda x, y, c: [(x, y, 1 - c)] + [(cx, cy, c) for cx, cy in _chips(x, y)])
_SIBLING_AND_NEIGHBOURS = (3, lambda x, y, c: [(x, y, 1 - c), (1 - x, y, c), (x, 1 - y, c)])
_ONWARD_AND_SIBLING = (4, lambda x, y, c: [(jnp.where(c == 1, x, 1 - x), jnp.where(c == 1, 1 - y, y), c), (x, y, 1 - c)])
_ALL_FOR_W_DOWN = (5, lambda x, y, c: _peers(x, y, c))
_CHIPS = (6, lambda x, y, c: [(cx, cy, c) for cx, cy in _chips(x, y)])
_ALL_FOR_SMALL = (7, lambda x, y, c: _peers(x, y, c))


def _split_start(name, srcs, lands, plan, after, handshake):
    ns, nl = len(srcs), len(lands)
    n_copies = len(plan(0, 0, 0))
    n_after = 0 if after is None else 1
    collective_id, peers_of = handshake

    def body(*refs):
        src_refs, land_refs = refs[:ns + nl], refs[ns:ns + nl]
        send_sems, recv_sems = refs[ns + nl + n_after], refs[ns + nl + n_after + 1]
        token = refs[-1]
        barrier = pltpu.get_barrier_semaphore()
        peers = peers_of(*_coords())
        for peer in peers:
            pl.semaphore_signal(barrier, inc=1, device_id=peer, device_id_type=_MESH)
        pl.semaphore_wait(barrier, len(peers))
        for k, (a, s_slot, l, d_slot, dev) in enumerate(plan(*_coords())):
            src = src_refs[a] if s_slot is None else src_refs[a].at[s_slot]
            pltpu.make_async_remote_copy(src_ref=src, dst_ref=land_refs[l].at[d_slot], send_sem=send_sems.at[k],
                                         recv_sem=recv_sems.at[k], device_id=dev, device_id_type=_MESH).start()
        token[...] = jnp.zeros_like(token)

    arrs = list(srcs) + list(lands)
    out = pl.pallas_call(
        body, name=name,
        out_shape=(pltpu.SemaphoreType.DMA((n_copies,)), pltpu.SemaphoreType.DMA((n_copies,)),
                   *[pltpu.HBM(a.shape, a.dtype) for a in arrs], jax.ShapeDtypeStruct((SUBLANES, 128), F32)),
        in_specs=[_HBM] * (ns + nl) + [_ANY] * n_after,
        out_specs=(_SEM, _SEM, *[_HBM] * (ns + nl), pl.BlockSpec(memory_space=pltpu.VMEM)),
        input_output_aliases={i: 2 + i for i in range(ns + nl)},
        compiler_params=pltpu.CompilerParams(has_side_effects=_EFFECT, collective_id=collective_id),
    )(*[_in_hbm(a) for a in arrs], *([] if after is None else [after]))
    return out[0], out[1], list(out[2:2 + ns]), list(out[2 + ns:2 + ns + nl]), out[-1]


def _split_wait(name, send_sems, recv_sems, srcs, lands, plan, recv_slots, after):
    ns, nl = len(srcs), len(lands)

    def body(*refs):
        src_refs, land_refs = refs[:ns + nl], refs[ns:ns + nl]
        send_sems, recv_sems = refs[ns + nl], refs[ns + nl + 1]
        coords = _coords()
        slots = recv_slots(*coords)
        for k, (a, s_slot, l, _, dev) in enumerate(plan(*coords)):
            src = src_refs[a] if s_slot is None else src_refs[a].at[s_slot]
            cp = pltpu.make_async_remote_copy(src_ref=src, dst_ref=land_refs[l].at[slots[k]], send_sem=send_sems.at[k],
                                              recv_sem=recv_sems.at[k], device_id=dev, device_id_type=_MESH)
            cp.wait_send()
            cp.wait_recv()

    arrs = list(srcs) + list(lands)
    out = pl.pallas_call(
        body, name=name,
        out_shape=tuple(pltpu.HBM(a.shape, a.dtype) for a in arrs),
        in_specs=[_HBM] * (ns + nl) + [_SEM, _SEM, _ANY],
        out_specs=tuple([_HBM] * (ns + nl)),
        input_output_aliases={i: i for i in range(ns + nl)},
        compiler_params=pltpu.CompilerParams(has_side_effects=_EFFECT),
    )(*arrs, send_sems, recv_sems, after)
    return list(out[:ns]), list(out[ns:])


def _gather_plan_ici(n):
    def plan(x, y, c):
        me = _lin(x, y, c)
        out = []
        for a in range(n):
            out.append((a, me, a, me, (x, y, 1 - c)))
            out += [(a, me, a, me, (cx, cy, c)) for cx, cy in _chips(x, y)]
        return out

    def recv_slots(x, y, c):
        out = []
        for _ in range(n):
            out.append(_lin(x, y, 1 - c))
            out += [_lin(cx, cy, c) for cx, cy in _chips(x, y)]
        return out

    return plan, recv_slots


def _gather_plan_d2d(n):
    def plan(x, y, c):
        return [(a, _lin(cx, cy, c), a, _lin(cx, cy, c), (x, y, 1 - c)) for a in range(n) for cx, cy in _chips(x, y)]

    def recv_slots(x, y, c):
        return [_lin(cx, cy, 1 - c) for _ in range(n) for cx, cy in _chips(x, y)]

    return plan, recv_slots


def _all_gather_split(lands, tag, after):
    n = len(lands)
    plan1, slots1 = _gather_plan_ici(n)
    s1, r1, _, lands, token = _split_start(f"gather_{tag}_ici_start", [], lands, plan1, after, _SIBLING_AND_CHIPS)

    def finish(after):
        _, got = _split_wait(f"gather_{tag}_ici_wait", s1, r1, [], lands, plan1, slots1, after)
        plan2, slots2 = _gather_plan_d2d(n)
        s2, r2, _, got, token2 = _split_start(f"gather_{tag}_d2d_start", [], got, plan2, None, _SIBLING)
        return _split_wait(f"gather_{tag}_d2d_wait", s2, r2, [], got, plan2, slots2, token2)[1]

    return finish, token


def _all_gather_tree(lands, tag, after):
    n = len(lands)

    def plan1(x, y, c):
        me = _lin(x, y, c)
        return [(a, me, a, me, dev) for a in range(n) for dev in ((x, y, 1 - c), (1 - x, y, c), (x, 1 - y, c))]

    def slots1(x, y, c):
        return [s for _ in range(n) for s in (_lin(x, y, 1 - c), _lin(1 - x, y, c), _lin(x, 1 - y, c))]

    def plan2(x, y, c):
        from_x, from_y = _lin(1 - x, y, c), _lin(x, 1 - y, c)
        north = c == 1
        passed = jnp.where(north, from_x, from_y)
        onward = (jnp.where(north, x, 1 - x), jnp.where(north, 1 - y, y), c)
        sib = (x, y, 1 - c)
        return [cp for a in range(n) for cp in ((a, passed, a, passed, onward), (a, from_x, a, from_x, sib),
                                                (a, from_y, a, from_y, sib))]

    def slots2(x, y, c):
        return [s for _ in range(n) for s in (_lin(1 - x, 1 - y, c), _lin(1 - x, y, 1 - c), _lin(x, 1 - y, 1 - c))]

    def plan3(x, y, c):
        diag = _lin(1 - x, 1 - y, c)
        return [(a, diag, a, diag, (x, y, 1 - c)) for a in range(n)]

    def slots3(x, y, c):
        return [_lin(1 - x, 1 - y, 1 - c)] * n

    s1, r1, _, lands, token = _split_start(f"gather_{tag}_1_start", [], lands, plan1, after, _SIBLING_AND_NEIGHBOURS)
    state = {}

    def stage2(after):
        _, got = _split_wait(f"gather_{tag}_1_wait", s1, r1, [], lands, plan1, slots1, after)
        state["s"], state["r"], _, state["lands"], token2 = _split_start(f"gather_{tag}_2_start", [], got, plan2, None,
                                                                         _ONWARD_AND_SIBLING)
        return token2

    def stage3(after):
        _, got = _split_wait(f"gather_{tag}_2_wait", state["s"], state["r"], [], state["lands"], plan2, slots2, after)
        s3, r3, _, got, token3 = _split_start(f"gather_{tag}_3_start", [], got, plan3, None, _SIBLING)
        return _split_wait(f"gather_{tag}_3_wait", s3, r3, [], got, plan3, slots3, token3)[1]

    return stage2, stage3, token


_CHIP_LIST = ((0, 0), (0, 1), (1, 0), (1, 1))


def _reduce_plan_d2d(n):
    def plan(x, y, c):
        return [(a, _lin(qx, qy, 1 - c), a, q, (x, y, 1 - c)) for a in range(n) for q, (qx, qy) in enumerate(_CHIP_LIST)]

    def recv_slots(x, y, c):
        return [q for _ in range(n) for q in range(4)]

    return plan, recv_slots


def _reduce_plan_ici(n):
    def plan(x, y, c):
        return [(a, 2 * cx + cy, a, j, (cx, cy, c)) for a in range(n) for j, (cx, cy) in enumerate(_chips(x, y))]

    def recv_slots(x, y, c):
        return [j for _ in range(n) for j in range(3)]

    return plan, recv_slots


def _scatter_plan(n):
    def plan(x, y, c):
        return [(a, _lin(*peer), a, r, peer) for a in range(n) for r, peer in enumerate(_peers(x, y, c))]

    def recv_slots(x, y, c):
        return [r for _ in range(n) for r in range(N_DEV - 1)]

    return plan, recv_slots


def _broadcast_plan():
    def plan(x, y, c):
        return [(0, None, 0, _lin(x, y, c), peer) for peer in _peers(x, y, c)]

    def recv_slots(x, y, c):
        return [_lin(*peer) for peer in _peers(x, y, c)]

    return plan, recv_slots


def _chip_partial(grads, recvd, core, name):
    n = len(grads)

    def body(c_ref, *refs):
        for a in range(n):
            g_ref, r_ref, o_ref = refs[a], refs[n + a], refs[2 * n + a]
            o_ref[...] = (g_ref[...].astype(F32) + r_ref[...].astype(F32)).astype(o_ref.dtype)

    def blk(a, own):
        zeros = (0,) * (a.ndim - 1)
        return pl.BlockSpec((None,) + a.shape[1:],
                            (lambda q, c_ref: (2 * q + c_ref[0],) + zeros) if own else (lambda q, c_ref: (q,) + zeros))

    return pl.pallas_call(
        body, name=name,
        grid_spec=pltpu.PrefetchScalarGridSpec(
            num_scalar_prefetch=1, grid=(4,),
            in_specs=[blk(a, True) for a in grads] + [blk(a, False) for a in recvd],
            out_specs=[blk(a, False) for a in recvd]),
        out_shape=[jax.ShapeDtypeStruct(a.shape, a.dtype) for a in recvd],
        compiler_params=_params(("arbitrary",)),
    )(core, *grads, *recvd)


def _reduce_scatter_split(grads, tag, core, behind):
    n = len(grads)
    plan1, slots1 = _reduce_plan_d2d(n)
    lands1 = [lax.empty((4,) + a.shape[1:], a.dtype) for a in grads]
    s1, r1, srcs1, lands1, token1 = _split_start(f"reduce_{tag}_d2d_start", grads, lands1, plan1, None, _SIBLING)
    own, got = _split_wait(f"reduce_{tag}_d2d_wait", s1, r1, srcs1, lands1, plan1, slots1, behind(token1))
    parts = _chip_partial(own, got, core, f"reduce_{tag}_partial")
    plan2, slots2 = _reduce_plan_ici(n)
    lands2 = [lax.empty((3,) + a.shape[1:], a.dtype) for a in grads]
    s2, r2, srcs2, lands2, token2 = _split_start(f"reduce_{tag}_ici_start", parts, lands2, plan2, None, _CHIPS)

    def finish(after):
        return _split_wait(f"reduce_{tag}_ici_wait", s2, r2, srcs2, lands2, plan2, slots2, after)

    return finish, token2


def _place_shards(me, shards, dtypes, name, after=None):
    n = len(shards)

    def body(me_ref, *refs):
        for a in range(n):
            refs[n + a][...] = refs[a][...].astype(dtypes[a])

    full = lambda s: pl.BlockSpec(s.shape, lambda i, me_ref: (0,) * s.ndim)
    slot = lambda s: pl.BlockSpec((None,) + s.shape, lambda i, me_ref: (me_ref[0],) + (0,) * s.ndim)
    body, more_specs, more = _ordered_behind(body, 1 + n, after)
    return pl.pallas_call(
        body, name=name,
        grid_spec=pltpu.PrefetchScalarGridSpec(num_scalar_prefetch=1, grid=(1,),
                                               in_specs=[full(s) for s in shards] + more_specs,
                                               out_specs=[slot(s) for s in shards]),
        out_shape=[jax.ShapeDtypeStruct((N_DEV,) + s.shape, d) for s, d in zip(shards, dtypes)],
        compiler_params=_params(("arbitrary",)),
    )(me, *shards, *more)


def _mix_in_fwd(x, g1, w_in_t, conv_w, gq, gk, gconv):
    tm = 512
    n_t = SEQ // tm

    def body(x_ref, g1_ref, w_ref, cw_ref, gq_ref, gk_ref, gc_ref,
             proj_ref, u1_ref, ycn_ref, qn_ref, kn_ref, v_ref, halo_ref):
        @pl.when(pl.program_id(0) == 0)
        def _():
            halo_ref[...] = jnp.zeros_like(halo_ref)

        xv = x_ref[...]
        u = (xv * _rstd(xv) * g1_ref[...]).astype(BF16)
        u1_ref[...] = u
        proj = _dot_nt(u, w_ref[...])
        proj_ref[...] = proj
        gate_b = proj[:, 0:CONV_WIDTH]
        a = proj[:, CONV_WIDTH:2 * CONV_WIDTH] * proj[:, 2 * CONV_WIDTH:3 * CONV_WIDTH]
        cv, _, _ = _conv3(a, _taps(cw_ref[...]), halo_ref[...])
        halo_ref[...] = a[tm - SUBLANES:]
        yc = gate_b * cv
        ycn_ref[...] = (yc * _rstd(yc) * gc_ref[...]).astype(BF16)
        q0 = 3 * CONV_WIDTH
        qn_ref[...] = _head_norm(proj[:, q0:q0 + ATTN_WIDTH], gq_ref[...], N_HEADS).astype(BF16)
        k0 = q0 + ATTN_WIDTH
        kn_ref[...] = _head_norm(proj[:, k0:k0 + KV_WIDTH], gk_ref[...], 2).astype(BF16)
        v_ref[...] = proj[:, k0 + KV_WIDTH:k0 + 2 * KV_WIDTH].astype(BF16)

    const = lambda shape: pl.BlockSpec(shape, lambda i: (0,) * len(shape))
    rows = lambda w: pl.BlockSpec((tm, w), lambda i: (i, 0))
    return pl.pallas_call(
        body, name="mix_in_fwd", grid=(n_t,),
        in_specs=[rows(D_MODEL), const((1, D_MODEL)), const((IN_WIDTH, D_MODEL)), const((3, CONV_WIDTH)),
                  const((1, HEAD_DIM)), const((1, HEAD_DIM)), const((1, CONV_WIDTH))],
        out_specs=[rows(IN_WIDTH), rows(D_MODEL), rows(CONV_WIDTH), rows(ATTN_WIDTH), rows(KV_WIDTH), rows(KV_WIDTH)],
        out_shape=[jax.ShapeDtypeStruct((SEQ, IN_WIDTH), F32), jax.ShapeDtypeStruct((SEQ, D_MODEL), BF16),
                   jax.ShapeDtypeStruct((SEQ, CONV_WIDTH), BF16),
                   jax.ShapeDtypeStruct((SEQ, ATTN_WIDTH), BF16), jax.ShapeDtypeStruct((SEQ, KV_WIDTH), BF16),
                   jax.ShapeDtypeStruct((SEQ, KV_WIDTH), BF16)],
        scratch_shapes=[pltpu.VMEM((SUBLANES, CONV_WIDTH), F32)],
        compiler_params=_params(("arbitrary",)),
    )(x, g1, w_in_t, conv_w, gq, gk, gconv)


GROUP_ROWS = GQA_GROUP * BLK


def _band_bias(tbl_ref, bkt, bias_ref):
    for h in range(N_HEADS):
        acc = jnp.zeros(bkt.shape, F32)
        for b in range(NUM_BUCKETS):
            acc = jnp.where(bkt == b, tbl_ref[h, b], acc)
        bias_ref[h // GQA_GROUP, BLK * (h % GQA_GROUP):BLK * (h % GQA_GROUP + 1), :] = acc


def _band_masks(i):
    qi = lax.broadcasted_iota(jnp.int32, (GROUP_ROWS, BLK), 0) & (BLK - 1)
    ji = lax.broadcasted_iota(jnp.int32, (GROUP_ROWS, BLK), 1)
    upper = ji > qi
    return upper, upper & (i == 0)


def _stack_heads(x, g):
    return jnp.concatenate([x[:, HEAD_DIM * h:HEAD_DIM * (h + 1)] for h in range(GQA_GROUP * g, GQA_GROUP * (g + 1))], axis=0)


def _unstack_heads(groups):
    return jnp.concatenate([p[BLK * t:BLK * (t + 1)] for p in groups for t in range(GQA_GROUP)], axis=-1)


def _per_head_rows(vals):
    row = lax.broadcasted_iota(jnp.int32, (GROUP_ROWS, 1), 0)
    col = jnp.full((GROUP_ROWS, 1), vals[GQA_GROUP - 1], F32)
    for t in range(GQA_GROUP - 2, -1, -1):
        col = jnp.where(row < BLK * (t + 1), vals[t], col)
    return col


def _band_rows(ref, i):
    prev = pl.multiple_of(jnp.maximum(i - 1, 0) * BLK, BLK)
    cur = pl.multiple_of(i * BLK, BLK)
    return jnp.concatenate([ref[pl.ds(prev, BLK), :], ref[pl.ds(cur, BLK), :]], axis=0), prev, cur


def _fold(band, upper):
    return jnp.where(upper, band[:, :BLK], band[:, BLK:])


def _unfold(tile, upper):
    return jnp.concatenate([jnp.where(upper, tile, 0.0), jnp.where(upper, 0.0, tile)], axis=1)


def _head_probs(qh, kh, bias, upper, dead, sink):
    logits = _fold(_dot_nt(qh, kh), upper) * (HEAD_DIM ** -0.5) + bias
    logits = jnp.where(dead, NEG_INF, logits)
    m = jnp.maximum(jnp.max(logits, axis=-1, keepdims=True), sink)
    p = jnp.exp(logits - m)
    es = jnp.exp(sink - m)
    den = jnp.sum(p, axis=-1, keepdims=True) + es
    return p / den, es / den


def _attn_fwd(qn, kn, v, tbl, sinks, bkt, gattn, after=None):
    n_b = SEQ // BLK

    def body(q_ref, k_ref, v_ref, tbl_ref, sink_ref, bkt_ref, ga_ref, y_ref, yn_ref, p_ref, ps_ref, bias_ref):
        i = pl.program_id(0)

        @pl.when(i == 0)
        def _():
            _band_bias(tbl_ref, bkt_ref[...], bias_ref)

        kb, _, _ = _band_rows(k_ref, i)
        vb, _, _ = _band_rows(v_ref, i)
        upper, dead = _band_masks(i)
        q = q_ref[...]
        lane = lax.broadcasted_iota(jnp.int32, (BLK, 128), 1)
        outs = []
        psinks = jnp.zeros((BLK, 128), F32)
        for g in range(N_HEADS // GQA_GROUP):
            kv = slice(HEAD_DIM * g, HEAD_DIM * (g + 1))
            sink = _per_head_rows([sink_ref[0, GQA_GROUP * g + t] for t in range(GQA_GROUP)])
            probs, psink = _head_probs(_stack_heads(q, g), kb[:, kv], bias_ref[g], upper, dead, sink)
            p_ref[g] = probs.astype(BF16)
            for t in range(GQA_GROUP):
                psinks = jnp.where(lane == GQA_GROUP * g + t, psink[BLK * t:BLK * (t + 1)], psinks)
            outs.append(_dot(_unfold(probs, upper).astype(BF16), vb[:, kv]))
        ps_ref[...] = psinks
        y = _unstack_heads(outs)
        y_ref[...] = y
        yn_ref[...] = (y * _rstd(y) * ga_ref[...]).astype(BF16)

    const = lambda shape: pl.BlockSpec(shape, lambda i: (0,) * len(shape))
    rows = lambda w: pl.BlockSpec((BLK, w), lambda i: (i, 0))
    smem = pl.BlockSpec(memory_space=pltpu.SMEM)
    body, more_specs, more = _ordered_behind(body, 7, after)
    return pl.pallas_call(
        body, name="attn_fwd", grid=(n_b,),
        in_specs=[rows(ATTN_WIDTH), const((SEQ, KV_WIDTH)), const((SEQ, KV_WIDTH)), smem, smem,
                  const((BLK, BLK)), const((1, ATTN_WIDTH))] + more_specs,
        out_specs=[rows(ATTN_WIDTH), rows(ATTN_WIDTH),
                   pl.BlockSpec((None, N_HEADS // GQA_GROUP, GROUP_ROWS, BLK), lambda i: (i, 0, 0, 0)), rows(128)],
        out_shape=[jax.ShapeDtypeStruct((SEQ, ATTN_WIDTH), F32), jax.ShapeDtypeStruct((SEQ, ATTN_WIDTH), BF16),
                   jax.ShapeDtypeStruct((n_b, N_HEADS // GQA_GROUP, GROUP_ROWS, BLK), BF16),
                   jax.ShapeDtypeStruct((SEQ, 128), F32)],
        scratch_shapes=[pltpu.VMEM((N_HEADS // GQA_GROUP, GROUP_ROWS, BLK), F32)],
        compiler_params=_params(("arbitrary",)),
    )(qn, kn, v, tbl, sinks, bkt, gattn, *more)


def _ffn_block(i, step):
    return jnp.where(i % 2 == 0, step, N_FFN_BLK - 1 - step)


def _ffn_fwd(x, ycn, yan, w_out, g2, w_up, fcw, fcb, w_down, tgt):
    tm = 512
    n_t = SEQ // tm

    def body(x_ref, ycn_ref, yan_ref, wo_ref, g2_ref, wu_ref, cw_ref, b_ref, wd_ref, tgt_ref,
             h1_ref, u2_ref, up_ref, pre_ref, act_ref, dh2_ref, dh2b_ref, loss_ref, acc_ref, halo_ref):
        i, step = pl.program_id(0), pl.program_id(1)
        j = _ffn_block(i, step)

        @pl.when((i == 0) & (step == 0))
        def _():
            loss_ref[...] = jnp.zeros_like(loss_ref)

        @pl.when(step == 0)
        def _():
            h1 = x_ref[...] + _dot(ycn_ref[...], wo_ref[0:CONV_WIDTH, :]) + _dot(yan_ref[...], wo_ref[CONV_WIDTH:, :])
            h1_ref[...] = h1
            u2_ref[...] = (h1 * _rstd(h1) * g2_ref[...]).astype(BF16)
            acc_ref[...] = jnp.zeros_like(acc_ref)

        u2 = u2_ref[...]
        pre = []
        for s in range(2):
            up = _dot_nt(u2, wu_ref[s])
            up_ref[s] = up.astype(BF16)
            halo = jnp.where(i == 0, 0.0, halo_ref[s, j])
            pre.append(_conv3(up, _taps(cw_ref.at[s]), halo)[0] + b_ref[s])
            pre_ref[s] = pre[s].astype(BF16)
            halo_ref[s, j] = up[tm - SUBLANES:]
        g, val = pre
        act = (g * jax.nn.sigmoid(g) * val).astype(BF16)
        act_ref[...] = act
        acc_ref[...] += _dot(act, wd_ref[...])

        @pl.when(step == N_FFN_BLK - 1)
        def _():
            err = h1_ref[...] + acc_ref[...] - tgt_ref[...]
            loss_ref[...] += 0.5 * jnp.sum(err * err) / D_MODEL
            dh2 = err / D_MODEL
            dh2_ref[...] = dh2
            dh2b_ref[...] = dh2.astype(BF16)

    rows = lambda w: pl.BlockSpec((tm, w), lambda i, step: (i, 0))
    const = lambda shape: pl.BlockSpec(shape, lambda i, step: (0,) * len(shape))
    pair = lambda *s: pl.BlockSpec((2, None) + s, lambda i, step: (0, _ffn_block(i, step)) + (0,) * len(s))
    upb = pl.BlockSpec((2, None, tm, FFN_BLK), lambda i, step: (0, _ffn_block(i, step), i, 0))
    return pl.pallas_call(
        body, name="ffn_fwd", grid=(n_t, N_FFN_BLK),
        in_specs=[rows(D_MODEL), rows(CONV_WIDTH), rows(ATTN_WIDTH), const((D_MODEL, D_MODEL)), const((1, D_MODEL)),
                  pair(FFN_BLK, D_MODEL), pair(3, 1, FFN_BLK), pair(1, FFN_BLK),
                  pl.BlockSpec((None, FFN_BLK, D_MODEL), lambda i, step: (_ffn_block(i, step), 0, 0)), rows(D_MODEL)],
        out_specs=[rows(D_MODEL), rows(D_MODEL), upb, upb,
                   pl.BlockSpec((None, tm, FFN_BLK), lambda i, step: (_ffn_block(i, step), i, 0)),
                   rows(D_MODEL), rows(D_MODEL), const((SUBLANES, 128))],
        out_shape=[jax.ShapeDtypeStruct((SEQ, D_MODEL), F32), jax.ShapeDtypeStruct((SEQ, D_MODEL), BF16),
                   jax.ShapeDtypeStruct((2, N_FFN_BLK, SEQ, FFN_BLK), BF16),
                   jax.ShapeDtypeStruct((2, N_FFN_BLK, SEQ, FFN_BLK), BF16),
                   jax.ShapeDtypeStruct((N_FFN_BLK, SEQ, FFN_BLK), BF16),
                   jax.ShapeDtypeStruct((SEQ, D_MODEL), F32), jax.ShapeDtypeStruct((SEQ, D_MODEL), BF16),
                   jax.ShapeDtypeStruct((SUBLANES, 128), F32)],
        scratch_shapes=[pltpu.VMEM((tm, D_MODEL), F32), pltpu.VMEM((2, N_FFN_BLK, SUBLANES, FFN_BLK), F32)],
        compiler_params=_params(("arbitrary", "arbitrary")),
    )(x, ycn, yan, w_out, g2, w_up, fcw, fcb, w_down, tgt)


def _ffn_bwd(dh2, dh2b, h1, g2, up, pre, w_up, fcw, w_down, after=None):
    tm = 512
    units = ((288, 224), (0, 288))
    n_t = SEQ // tm

    def body(dh2_ref, dh2b_ref, h1_ref, g2_ref, up_ref, pre_ref, wu_ref, cw_ref, wd_ref,
             dup_ref, dh1_ref, dh1b_ref, dfb_ref, dfcw_ref, dg2_ref, acc_ref, next_ref):
        i, step = pl.program_id(0), pl.program_id(1)
        j = _ffn_block(i, step)

        @pl.when((i == 0) & (step == 0))
        def _():
            dfb_ref[...] = jnp.zeros_like(dfb_ref)
            dfcw_ref[...] = jnp.zeros_like(dfcw_ref)
            dg2_ref[...] = jnp.zeros_like(dg2_ref)

        @pl.when(step == 0)
        def _():
            acc_ref[...] = jnp.zeros_like(acc_ref)

        nxt = [jnp.where(i == 0, 0.0, next_ref[s, j]) for s in range(2)]
        sums = [[0.0] * 4 for _ in range(2)]
        stores = []
        for r0, rn in units:
            rows = slice(r0, r0 + rn)
            g, val = pre_ref[0, rows, :].astype(F32), pre_ref[1, rows, :].astype(F32)
            sg = jax.nn.sigmoid(g)
            silu = g * sg
            dact = _dot_nt(dh2b_ref[rows, :], wd_ref[...])
            dpre = (dact * val * (sg * (1.0 + g * (1.0 - sg))), dact * silu)
            dups = []
            for s in range(2):
                d = dpre[s]
                u = up_ref[s, rows, :].astype(F32)
                w = _taps(cw_ref.at[s])
                d1 = _shift_up(d, 1, nxt[s])
                d2 = _shift_up(d, 2, nxt[s])
                nxt[s] = d[:SUBLANES]
                for t, term in enumerate((d, d2 * u, d1 * u, d * u)):
                    sums[s][t] = sums[s][t] + jnp.sum(term, axis=0, keepdims=True)
                dups.append((d * w[2] + d1 * w[1] + d2 * w[0]).astype(BF16))
                stores.append((s, rows, dups[s]))
            acc_ref[rows, :] += _dot(dups[0], wu_ref[0]) + _dot(dups[1], wu_ref[1])
        for s, rows, dup in stores:
            dup_ref[s, rows, :] = dup
        for s in range(2):
            next_ref[s, j] = nxt[s]
            dfb_ref[s, j] += sums[s][0]
            for t in range(3):
                dfcw_ref[s, j, t] += sums[s][1 + t]

        @pl.when(step == N_FFN_BLK - 1)
        def _():
            dn, dgain = _rms_bwd(h1_ref[...], g2_ref[...], acc_ref[...])
            dh1 = dh2_ref[...] + dn
            dh1_ref[...] = dh1
            dh1b_ref[...] = dh1.astype(BF16)
            dg2_ref[...] += dgain

    rev = lambda i: n_t - 1 - i
    rows = lambda w: pl.BlockSpec((tm, w), lambda i, step: (rev(i), 0))
    const = lambda shape: pl.BlockSpec(shape, lambda i, step: (0,) * len(shape))
    pair = lambda *s: pl.BlockSpec((2, None) + s, lambda i, step: (0, _ffn_block(i, step)) + (0,) * len(s))
    upb = pl.BlockSpec((2, None, tm, FFN_BLK), lambda i, step: (0, _ffn_block(i, step), rev(i), 0))
    body, more_specs, more = _ordered_behind(body, 9, after)
    return pl.pallas_call(
        body, name="ffn_bwd", grid=(n_t, N_FFN_BLK),
        in_specs=[rows(D_MODEL), rows(D_MODEL), rows(D_MODEL), const((1, D_MODEL)), upb, upb,
                  pair(FFN_BLK, D_MODEL), pair(3, 1, FFN_BLK),
                  pl.BlockSpec((None, FFN_BLK, D_MODEL), lambda i, step: (_ffn_block(i, step), 0, 0))] + more_specs,
        out_specs=[upb, rows(D_MODEL), rows(D_MODEL),
                   const((2, N_FFN_BLK, 1, FFN_BLK)), const((2, N_FFN_BLK, 3, 1, FFN_BLK)), const((1, D_MODEL))],
        out_shape=[jax.ShapeDtypeStruct((2, N_FFN_BLK, SEQ, FFN_BLK), BF16), jax.ShapeDtypeStruct((SEQ, D_MODEL), F32),
                   jax.ShapeDtypeStruct((SEQ, D_MODEL), BF16), jax.ShapeDtypeStruct((2, N_FFN_BLK, 1, FFN_BLK), F32),
                   jax.ShapeDtypeStruct((2, N_FFN_BLK, 3, 1, FFN_BLK), F32), jax.ShapeDtypeStruct((1, D_MODEL), F32)],
        scratch_shapes=[pltpu.VMEM((tm, D_MODEL), F32), pltpu.VMEM((2, N_FFN_BLK, SUBLANES, FFN_BLK), F32)],
        compiler_params=_params(("arbitrary", "arbitrary")),
    )(dh2, dh2b, h1, g2, up, pre, w_up, fcw, w_down, *more)


def _grad_tn(a_list, b, out_rows, name, after=None):
    n = len(a_list)
    ncol = b.shape[1]

    def body(*refs):
        a_refs, b_ref, o_ref = refs[:n], refs[n], refs[n + 1]
        j = pl.program_id(0)
        for k in range(n):
            @pl.when(j == k)
            def _(k=k):
                o_ref[...] = _dot_tn(a_refs[k][...], b_ref[...]).astype(BF16)

    full = lambda shape: pl.BlockSpec(shape, lambda j: (0,) * len(shape))
    body, more_specs, more = _ordered_behind(body, n + 1, after)
    return pl.pallas_call(
        body, name=name, grid=(n,),
        in_specs=[full((SEQ, out_rows))] * n + [full((SEQ, ncol))] + more_specs,
        out_specs=pl.BlockSpec((None, out_rows, ncol), lambda j: (j, 0, 0)),
        out_shape=jax.ShapeDtypeStruct((n, out_rows, ncol), BF16),
        compiler_params=_params(("arbitrary",)),
    )(*a_list, b, *more)


def _grad_tn_blocked(a, b, name, per_step=2):
    nb, _, a_w = a.shape
    b_w = b.shape[-1]

    def body(a_ref, b_ref, o_ref):
        for p in range(per_step):
            o_ref[p] = _dot_tn(a_ref[p], b_ref[...]).astype(BF16)

    return pl.pallas_call(
        body, name=name, grid=(nb // per_step,),
        in_specs=[pl.BlockSpec((per_step, SEQ, a_w), lambda k: (k, 0, 0)), pl.BlockSpec((SEQ, b_w), lambda k: (0, 0))],
        out_specs=pl.BlockSpec((per_step, a_w, b_w), lambda k: (k, 0, 0)),
        out_shape=jax.ShapeDtypeStruct((nb, a_w, b_w), BF16),
        compiler_params=_params(("arbitrary",)),
    )(a, b)


def _out_bwd(dh1b, w_out, y_attn, gattn, after=None):
    tm = 1024
    n_t = SEQ // tm

    def body(dh_ref, wo_ref, y_ref, ga_ref, dycn_ref, dy_ref, dga_ref):
        @pl.when(pl.program_id(0) == 0)
        def _():
            dga_ref[...] = jnp.zeros_like(dga_ref)

        dycat = _dot_nt(dh_ref[...], wo_ref[...])
        dycn_ref[...] = dycat[:, :CONV_WIDTH]
        dy, dga = _rms_bwd(y_ref[...], ga_ref[...], dycat[:, CONV_WIDTH:])
        dy_ref[...] = dy
        dga_ref[...] += dga

    rows = lambda w: pl.BlockSpec((tm, w), lambda i: (i, 0))
    const = lambda shape: pl.BlockSpec(shape, lambda i: (0,) * len(shape))
    body, more_specs, more = _ordered_behind(body, 4, after)
    return pl.pallas_call(
        body, name="out_bwd", grid=(n_t,),
        in_specs=[rows(D_MODEL), const((D_MODEL, D_MODEL)), rows(ATTN_WIDTH), const((1, ATTN_WIDTH))] + more_specs,
        out_specs=[rows(CONV_WIDTH), rows(ATTN_WIDTH), const((1, ATTN_WIDTH))],
        out_shape=[jax.ShapeDtypeStruct((SEQ, CONV_WIDTH), F32), jax.ShapeDtypeStruct((SEQ, ATTN_WIDTH), F32),
                   jax.ShapeDtypeStruct((1, ATTN_WIDTH), F32)],
        compiler_params=_params(("arbitrary",)),
    )(dh1b, w_out, y_attn, gattn, *more)


def _attn_bwd(qn, kn, v, dy, probs, psinks, bkt, after=None):
    n_b = SEQ // BLK

    def body(q_ref, k_ref, v_ref, dy_ref, p_ref, ps_ref, bkt_ref,
             dq_ref, dk_ref, dv_ref, dtbl_ref, dsink_ref, dbias_ref, dsacc_ref):
        i = pl.program_id(0)

        @pl.when(i == 0)
        def _():
            dbias_ref[...] = jnp.zeros_like(dbias_ref)
            dsacc_ref[...] = jnp.zeros_like(dsacc_ref)
            dk_ref[...] = jnp.zeros_like(dk_ref)
            dv_ref[...] = jnp.zeros_like(dv_ref)

        kb, prev, cur = _band_rows(k_ref, i)
        vb, _, _ = _band_rows(v_ref, i)
        upper, _ = _band_masks(i)
        q = q_ref[...]
        dy = dy_ref[...]
        psink = ps_ref[...]
        lane = lax.broadcasted_iota(jnp.int32, (BLK, 128), 1)
        dsink = jnp.zeros((BLK, 128), F32)
        dqs, dks, dvs = [], [], []
        for g in range(N_HEADS // GQA_GROUP):
            kv = slice(HEAD_DIM * g, HEAD_DIM * (g + 1))
            qg = _stack_heads(q, g)
            dog = _stack_heads(dy, g).astype(BF16)
            pb = p_ref[g]
            pg = pb.astype(F32)
            dprobs = _fold(_dot_nt(dog, vb[:, kv]), upper)
            dvs.append(_dot_tn(_unfold(pb, upper), dog))
            dsum = jnp.sum(pg * dprobs, axis=-1, keepdims=True)
            dlogits = pg * (dprobs - dsum)
            for t in range(GQA_GROUP):
                dsink = jnp.where(lane == GQA_GROUP * g + t, -psink * dsum[BLK * t:BLK * (t + 1)], dsink)
            dbias_ref[g] += dlogits
            ds = _unfold(dlogits * (HEAD_DIM ** -0.5), upper).astype(BF16)
            dqs.append(_dot(ds, kb[:, kv]))
            dks.append(_dot_tn(ds, qg))
        dsacc_ref[...] += dsink
        dq_ref[...] = _unstack_heads(dqs)
        dkb = jnp.concatenate(dks, axis=-1)
        dvb = jnp.concatenate(dvs, axis=-1)
        dk_ref[pl.ds(prev, BLK), :] += dkb[:BLK]
        dk_ref[pl.ds(cur, BLK), :] += dkb[BLK:]
        dv_ref[pl.ds(prev, BLK), :] += dvb[:BLK]
        dv_ref[pl.ds(cur, BLK), :] += dvb[BLK:]

        @pl.when(i == n_b - 1)
        def _():
            bkt = bkt_ref[...]
            row8 = lax.broadcasted_iota(jnp.int32, (N_HEADS, 128), 0)
            lane8 = lax.broadcasted_iota(jnp.int32, (N_HEADS, 128), 1)
            acc = jnp.zeros((N_HEADS, 128), F32)
            for h in range(N_HEADS):
                rows = slice(BLK * (h % GQA_GROUP), BLK * (h % GQA_GROUP + 1))
                dbh = dbias_ref[h // GQA_GROUP, rows, :]
                for b in range(NUM_BUCKETS):
                    acc = jnp.where((row8 == h) & (lane8 == b), jnp.sum(jnp.where(bkt == b, dbh, 0.0)), acc)
            dsink_ref[...] = jnp.sum(dsacc_ref[...], axis=0, keepdims=True)
            dtbl_ref[...] = acc

    const = lambda shape: pl.BlockSpec(shape, lambda i: (0,) * len(shape))
    rows = lambda w: pl.BlockSpec((BLK, w), lambda i: (i, 0))
    n_g = N_HEADS // GQA_GROUP
    body, more_specs, more = _ordered_behind(body, 7, after)
    return pl.pallas_call(
        body, name="attn_bwd", grid=(n_b,),
        in_specs=[rows(ATTN_WIDTH), const((SEQ, KV_WIDTH)), const((SEQ, KV_WIDTH)), rows(ATTN_WIDTH),
                  pl.BlockSpec((None, n_g, GROUP_ROWS, BLK), lambda i: (i, 0, 0, 0)), rows(128),
                  const((BLK, BLK))] + more_specs,
        out_specs=[rows(ATTN_WIDTH), const((SEQ, KV_WIDTH)), const((SEQ, KV_WIDTH)), const((N_HEADS, 128)), const((1, 128))],
        out_shape=[jax.ShapeDtypeStruct((SEQ, ATTN_WIDTH), F32), jax.ShapeDtypeStruct((SEQ, KV_WIDTH), F32),
                   jax.ShapeDtypeStruct((SEQ, KV_WIDTH), F32), jax.ShapeDtypeStruct((N_HEADS, 128), F32),
                   jax.ShapeDtypeStruct((1, 128), F32)],
        scratch_shapes=[pltpu.VMEM((n_g, GROUP_ROWS, BLK), F32), pltpu.VMEM((BLK, 128), F32)],
        compiler_params=_params(("arbitrary",)),
    )(qn, kn, v, dy, probs, psinks, bkt, *more)


def _mix_in_bwd(x, dh1, proj, dycn, dqn, dkn, dv, w_in_t, conv_w, g1, gq, gk, gconv):
    tm = 512
    n_t = SEQ // tm
    halo_blocks = tm // SUBLANES

    def body(x_ref, dh1_ref, proj_ref, halo_ref, dycn_ref, dqn_ref, dkn_ref, dv_ref, w_ref, cw_ref,
             g1_ref, gq_ref, gk_ref, gc_ref,
             dx_ref, dproj_ref, dcw_ref, dgc_ref, dgq_ref, dgk_ref, dg1_ref, next_ref):
        i = pl.program_id(0)
        first_tile = i == n_t - 1

        @pl.when(i == 0)
        def _():
            for r in (dcw_ref, dgc_ref, dgq_ref, dgk_ref, dg1_ref, next_ref):
                r[...] = jnp.zeros_like(r)

        proj = proj_ref[...]
        hp = halo_ref[...]
        gate_b = proj[:, 0:CONV_WIDTH]
        gate_c = proj[:, CONV_WIDTH:2 * CONV_WIDTH]
        hc = proj[:, 2 * CONV_WIDTH:3 * CONV_WIDTH]
        a = gate_c * hc
        a_halo = jnp.where(first_tile, 0.0, hp[:, CONV_WIDTH:2 * CONV_WIDTH] * hp[:, 2 * CONV_WIDTH:3 * CONV_WIDTH])
        cw = _taps(cw_ref[...])
        cv, a2, a1 = _conv3(a, cw, a_halo)
        dyc, dgc = _rms_bwd(gate_b * cv, gc_ref[...], dycn_ref[...])
        dgc_ref[...] += dgc
        dcv = dyc * gate_b
        dcw_ref[...] += jnp.concatenate(
            [jnp.sum(dcv * a2, axis=0, keepdims=True), jnp.sum(dcv * a1, axis=0, keepdims=True),
             jnp.sum(dcv * a, axis=0, keepdims=True)], axis=0)
        da = _conv3_bwd_input(dcv, cw, next_ref[...])
        next_ref[...] = dcv[:SUBLANES]
        q0 = 3 * CONV_WIDTH
        k0 = q0 + ATTN_WIDTH
        dq, dgq = _head_norm_bwd(proj[:, q0:k0], gq_ref[...], dqn_ref[...], N_HEADS)
        dk, dgk = _head_norm_bwd(proj[:, k0:k0 + KV_WIDTH], gk_ref[...], dkn_ref[...], 2)
        dgq_ref[...] += dgq
        dgk_ref[...] += dgk
        dproj = jnp.concatenate([dyc * cv, da * hc, da * gate_c, dq, dk, dv_ref[...]], axis=-1).astype(BF16)
        dproj_ref[...] = dproj
        du1 = _dot(dproj, w_ref[...])
        xv = x_ref[...]
        dn, dg1 = _rms_bwd(xv, g1_ref[...], du1)
        dx_ref[...] = dh1_ref[...] + dn
        dg1_ref[...] += dg1

    rev = lambda i: n_t - 1 - i
    rows = lambda w: pl.BlockSpec((tm, w), lambda i: (rev(i), 0))
    const = lambda shape: pl.BlockSpec(shape, lambda i: (0,) * len(shape))
    halo = pl.BlockSpec((SUBLANES, IN_WIDTH), lambda i: (jnp.maximum(rev(i) * halo_blocks - 1, 0), 0))
    return pl.pallas_call(
        body, name="mix_in_bwd", grid=(n_t,),
        in_specs=[rows(D_MODEL), rows(D_MODEL), rows(IN_WIDTH), halo, rows(CONV_WIDTH), rows(ATTN_WIDTH), rows(KV_WIDTH),
                  rows(KV_WIDTH), const((IN_WIDTH, D_MODEL)), const((3, CONV_WIDTH)), const((1, D_MODEL)),
                  const((1, HEAD_DIM)), const((1, HEAD_DIM)), const((1, CONV_WIDTH))],
        out_specs=[rows(D_MODEL), rows(IN_WIDTH), const((3, CONV_WIDTH)), const((1, CONV_WIDTH)),
                   const((1, HEAD_DIM)), const((1, HEAD_DIM)), const((1, D_MODEL))],
        out_shape=[jax.ShapeDtypeStruct((SEQ, D_MODEL), F32), jax.ShapeDtypeStruct((SEQ, IN_WIDTH), BF16),
                   jax.ShapeDtypeStruct((3, CONV_WIDTH), F32),
                   jax.ShapeDtypeStruct((1, CONV_WIDTH), F32), jax.ShapeDtypeStruct((1, HEAD_DIM), F32),
                   jax.ShapeDtypeStruct((1, HEAD_DIM), F32), jax.ShapeDtypeStruct((1, D_MODEL), F32)],
        scratch_shapes=[pltpu.VMEM((SUBLANES, CONV_WIDTH), F32)],
        compiler_params=_params(("arbitrary",)),
    )(x, dh1, proj, proj, dycn, dqn, dkn, dv, w_in_t, conv_w, g1, gq, gk, gconv)


def _grad_w_in(dproj, u1, after=None):
    bw = 768

    def body(a_ref, b_ref, o_ref):
        o_ref[...] = _dot_tn(a_ref[...], b_ref[...]).astype(BF16)

    body, more_specs, more = _ordered_behind(body, 2, after)
    return pl.pallas_call(
        body, name="grad_w_in", grid=(IN_WIDTH // bw,),
        in_specs=[pl.BlockSpec((SEQ, bw), lambda k: (0, k)), pl.BlockSpec((SEQ, D_MODEL), lambda k: (0, 0))] + more_specs,
        out_specs=pl.BlockSpec((bw, D_MODEL), lambda k: (k, 0)),
        out_shape=jax.ShapeDtypeStruct((IN_WIDTH, D_MODEL), BF16),
        compiler_params=_params(("arbitrary",)),
    )(dproj, u1, *more)


def _adamw_math(w, g, m, v):
    m = ADAM_B1 * m + (1.0 - ADAM_B1) * g
    v = ADAM_B2 * v + (1.0 - ADAM_B2) * (g * g)
    m_hat = m / (1.0 - ADAM_B1 ** ADAM_STEP)
    v_hat = v / (1.0 - ADAM_B2 ** ADAM_STEP)
    return -ADAM_LR * (m_hat / (jnp.sqrt(v_hat) + ADAM_EPS) + ADAM_WD * w), m, v


_ROW_G1, _ROW_G2, _ROW_OUT_NORMS, _ROW_FFN_B, _ROW_GQ, _ROW_GK, _ROW_SINKS, _ROW_LOSS, _ROW_TABLE = 0, 1, 2, 3, 11, 12, 13, 14, 16
SMALL_ROWS, SMALL_COLS = 24, 1024
_SMALL_NAMES = ("norm_mix_g", "norm_ffn_g", "out_norm_conv_g", "out_norm_attn_g", "ffn_conv_b", "q_norm_g", "k_norm_g",
                "sinks", "rel_bias_table")


def _pack_small_grads(dg1, dg2, dgconv, dgattn, dfb, dgq, dgk, dsinks, dtbl_t, loss_acc):
    def body(dg1_ref, dg2_ref, dgc_ref, dga_ref, dfb_ref, dgq_ref, dgk_ref, ds_ref, dt_ref, loss_ref, o_ref, all_ref):
        o_ref[...] = jnp.zeros_like(o_ref)
        o_ref[_ROW_G1:_ROW_G1 + 1, :] = dg1_ref[...]
        o_ref[_ROW_G2:_ROW_G2 + 1, :] = dg2_ref[...]
        o_ref[_ROW_OUT_NORMS:_ROW_OUT_NORMS + 1, 0:CONV_WIDTH] = dgc_ref[...]
        o_ref[_ROW_OUT_NORMS:_ROW_OUT_NORMS + 1, CONV_WIDTH:] = dga_ref[...]
        for k in range(N_DEV):
            o_ref[_ROW_FFN_B + k:_ROW_FFN_B + k + 1, 0:FFN_BLK] = dfb_ref[k // N_FFN_BLK, k % N_FFN_BLK]
        o_ref[_ROW_GQ:_ROW_GQ + 1, 0:HEAD_DIM] = dgq_ref[...]
        o_ref[_ROW_GK:_ROW_GK + 1, 0:HEAD_DIM] = dgk_ref[...]
        o_ref[_ROW_SINKS:_ROW_SINKS + 1, 0:128] = ds_ref[...]
        o_ref[_ROW_LOSS:_ROW_LOSS + 1, 0:128] = loss_ref[0:1, :]
        o_ref[_ROW_TABLE:_ROW_TABLE + N_HEADS, 0:128] = dt_ref[...]
        for s in range(N_DEV):
            all_ref[s] = o_ref[...]

    return pl.pallas_call(
        body, name="pack_small_grads",
        out_shape=[jax.ShapeDtypeStruct((SMALL_ROWS, SMALL_COLS), F32),
                   jax.ShapeDtypeStruct((N_DEV, SMALL_ROWS, SMALL_COLS), F32)],
    )(dg1, dg2, dgconv, dgattn, dfb, dgq, dgk, dsinks, dtbl_t, loss_acc)


def _adamw_small(recv, params, after):
    names = _SMALL_NAMES
    n = len(names)

    def grad_of(g, name, k=None):
        if name == "norm_mix_g":
            return g[_ROW_G1:_ROW_G1 + 1, :]
        if name == "norm_ffn_g":
            return g[_ROW_G2:_ROW_G2 + 1, :]
        if name == "out_norm_conv_g":
            return g[_ROW_OUT_NORMS:_ROW_OUT_NORMS + 1, 0:CONV_WIDTH]
        if name == "out_norm_attn_g":
            return g[_ROW_OUT_NORMS:_ROW_OUT_NORMS + 1, CONV_WIDTH:]
        if name == "ffn_conv_b":
            return g[_ROW_FFN_B + k:_ROW_FFN_B + k + 1, 0:FFN_BLK]
        if name == "q_norm_g":
            return g[_ROW_GQ:_ROW_GQ + 1, 0:HEAD_DIM]
        if name == "k_norm_g":
            return g[_ROW_GK:_ROW_GK + 1, 0:HEAD_DIM]
        if name == "sinks":
            return g[_ROW_SINKS:_ROW_SINKS + 1, 0:N_HEADS]
        return g[_ROW_TABLE:_ROW_TABLE + N_HEADS, 0:NUM_BUCKETS]

    def body(r_ref, *refs):
        ins, outs, loss_ref = refs[:3 * n], refs[3 * n:7 * n], refs[7 * n]
        g = r_ref[0]
        for s in range(1, N_DEV):
            g = g + r_ref[s]
        loss_ref[...] = g[_ROW_LOSS:_ROW_LOSS + 1, 0:128]
        for i, name in enumerate(names):
            w_ref, m_ref, v_ref = ins[3 * i:3 * i + 3]
            o = outs[4 * i:4 * i + 4]
            cols = [slice(FFN_BLK * k, FFN_BLK * (k + 1)) for k in range(N_DEV)] if name == "ffn_conv_b" else [slice(None)]
            for k, cs in enumerate(cols):
                gk = grad_of(g, name, k)
                d, m2, v2 = _adamw_math(w_ref[:, cs], gk, m_ref[:, cs], v_ref[:, cs])
                o[0][:, cs], o[1][:, cs], o[2][:, cs], o[3][:, cs] = gk, d, m2, v2

    flat = [a for name in names for a in params[name]]
    body, more_specs, more = _ordered_behind(body, 1 + 3 * n, after)
    vmem = pl.BlockSpec(memory_space=pltpu.VMEM)
    out = pl.pallas_call(
        body, name="adamw_small",
        in_specs=[vmem] * (1 + 3 * n) + more_specs,
        out_shape=[jax.ShapeDtypeStruct(params[name][0].shape, F32) for name in names for _ in range(4)]
        + [jax.ShapeDtypeStruct((1, 128), F32)],
        compiler_params=pltpu.CompilerParams(vmem_limit_bytes=VMEM_LIMIT),
    )(recv, *flat, *more)
    return {name: tuple(out[4 * i:4 * i + 4]) for i, name in enumerate(names)}, out[4 * n]


def _adamw_direct(w, m, v, own, recv, me, name, row_blocks=1, after=None):
    rb = w.shape[0] // row_blocks
    cols = w.shape[1]

    def body(me_ref, w_ref, m_ref, v_ref, o_ref, r_ref, g_o, d_o, m_o, v_o):
        g = o_ref[...].astype(F32)
        for s in range(N_DEV - 1):
            g = g + r_ref[s].astype(F32)
        g_o[...] = g
        d_o[...], m_o[...], v_o[...] = _adamw_math(w_ref[...], g, m_ref[...], v_ref[...])

    blk = pl.BlockSpec((rb, cols), lambda i, me_ref: (i, 0))
    oblk = pl.BlockSpec((None, rb, cols), lambda i, me_ref: (me_ref[0], i, 0))
    rblk = pl.BlockSpec((N_DEV - 1, rb, cols), lambda i, me_ref: (0, i, 0))
    body, more_specs, more = _ordered_behind(body, 6, after)
    return pl.pallas_call(
        body, name=name,
        grid_spec=pltpu.PrefetchScalarGridSpec(num_scalar_prefetch=1, grid=(row_blocks,),
                                               in_specs=[blk, blk, blk, oblk, rblk] + more_specs, out_specs=[blk] * 4),
        out_shape=[jax.ShapeDtypeStruct(w.shape, F32)] * 4,
        compiler_params=_params(("arbitrary",)),
    )(me, w, m, v, own, recv, *more)


def _adamw(w, m, v, part, recv, chip, name, row_blocks=1, after=None):
    rb = w.shape[0] // row_blocks
    tail = w.shape[1:]
    zeros = (0,) * len(tail)

    def body(chip_ref, w_ref, m_ref, v_ref, p_ref, r_ref, g_o, d_o, m_o, v_o):
        g = p_ref[...].astype(F32)
        for s in range(3):
            g = g + r_ref[s].astype(F32)
        g_o[...] = g
        d_o[...], m_o[...], v_o[...] = _adamw_math(w_ref[...], g, m_ref[...], v_ref[...])

    blk = pl.BlockSpec((rb,) + tail, lambda i, chip_ref: (i,) + zeros)
    pblk = pl.BlockSpec((None, rb) + tail, lambda i, chip_ref: (chip_ref[0], i) + zeros)
    rblk = pl.BlockSpec((3, rb) + tail, lambda i, chip_ref: (0, i) + zeros)
    body, more_specs, more = _ordered_behind(body, 6, after)
    return pl.pallas_call(
        body, name=name,
        grid_spec=pltpu.PrefetchScalarGridSpec(num_scalar_prefetch=1, grid=(row_blocks,),
                                               in_specs=[blk, blk, blk, pblk, rblk] + more_specs, out_specs=[blk] * 4),
        out_shape=[jax.ShapeDtypeStruct(w.shape, F32)] * 4,
        compiler_params=_params(("arbitrary",)),
    )(chip, w, m, v, part, recv, *more)


def kernel(x, norm_mix_g, w_in, conv_w, q_norm_g, k_norm_g, rel_bias_table, sinks, out_norm_conv_g, out_norm_attn_g, w_out, norm_ffn_g, w_up, ffn_conv_w, ffn_conv_b, w_down, loss_target, m_norm_mix_g, m_w_in, m_conv_w, m_q_norm_g, m_k_norm_g, m_rel_bias_table, m_sinks, m_out_norm_conv_g, m_out_norm_attn_g, m_w_out, m_norm_ffn_g, m_w_up, m_ffn_conv_w, m_ffn_conv_b, m_w_down, v_norm_mix_g, v_w_in, v_conv_w, v_q_norm_g, v_k_norm_g, v_rel_bias_table, v_sinks, v_out_norm_conv_g, v_out_norm_attn_g, v_w_out, v_norm_ffn_g, v_w_up, v_ffn_conv_w, v_ffn_conv_b, v_w_down):
    p = dict(norm_mix_g=norm_mix_g, w_in=w_in, conv_w=conv_w, q_norm_g=q_norm_g, k_norm_g=k_norm_g,
             rel_bias_table=rel_bias_table, sinks=sinks, out_norm_conv_g=out_norm_conv_g, out_norm_attn_g=out_norm_attn_g,
             w_out=w_out, norm_ffn_g=norm_ffn_g, w_up=w_up, ffn_conv_w=ffn_conv_w, ffn_conv_b=ffn_conv_b, w_down=w_down)
    m = dict(norm_mix_g=m_norm_mix_g, w_in=m_w_in, conv_w=m_conv_w, q_norm_g=m_q_norm_g, k_norm_g=m_k_norm_g,
             rel_bias_table=m_rel_bias_table, sinks=m_sinks, out_norm_conv_g=m_out_norm_conv_g,
             out_norm_attn_g=m_out_norm_attn_g, w_out=m_w_out, norm_ffn_g=m_norm_ffn_g, w_up=m_w_up,
             ffn_conv_w=m_ffn_conv_w, ffn_conv_b=m_ffn_conv_b, w_down=m_w_down)
    v = dict(norm_mix_g=v_norm_mix_g, w_in=v_w_in, conv_w=v_conv_w, q_norm_g=v_q_norm_g, k_norm_g=v_k_norm_g,
             rel_bias_table=v_rel_bias_table, sinks=v_sinks, out_norm_conv_g=v_out_norm_conv_g,
             out_norm_attn_g=v_out_norm_attn_g, w_out=v_w_out, norm_ffn_g=v_norm_ffn_g, w_up=v_w_up,
             ffn_conv_w=v_ffn_conv_w, ffn_conv_b=v_ffn_conv_b, w_down=v_w_down)

    xs, tgt = x[0], loss_target[0]
    g1, g2, gq, gk, gconv, gattn = norm_mix_g, norm_ffn_g, q_norm_g, k_norm_g, out_norm_conv_g, out_norm_attn_g
    ix, iy, ic = _coords()
    core = ic.astype(jnp.int32).reshape(1)
    chip = (2 * ix + iy).astype(jnp.int32).reshape(1)
    me = _lin(ix, iy, ic).astype(jnp.int32).reshape(1)
    bkt = jnp.asarray(_bucket_map())
    tr = lambda a: a[0].T
    taps = lambda a: jnp.transpose(a, (1, 0, 2))
    tbl_t = rel_bias_table.T

    wi_l, cw_l = _place_shards(me, [tr(w_in), taps(conv_w)], [BF16, F32], "place_mixer_shards")
    finish_a, token_a = _all_gather_split([wi_l, cw_l], "mixer", None)
    wo_l, wu_l, wd_l, fcw_l = _place_shards(me, [w_out[0], tr(w_up), w_down[0], taps(ffn_conv_w)],
                                            [BF16, BF16, BF16, F32], "place_ffn_shards", after=token_a)
    ffn_stage2, ffn_stage3, token_b = _all_gather_tree([wo_l, wu_l, wd_l, fcw_l], "ffn", token_a)
    wi_g, cw_g = finish_a(token_b)
    w_in_t = wi_g.reshape(IN_WIDTH, D_MODEL)
    conv_w_f = jnp.transpose(cw_g[:, :, 0, :], (1, 0, 2)).reshape(3, CONV_WIDTH)

    proj, u1, ycn, qn, kn, vv = _mix_in_fwd(xs, g1, w_in_t, conv_w_f, gq, gk, gconv)
    token_b2 = ffn_stage2(ycn)
    y_attn, yan, probs, psinks = _attn_fwd(qn, kn, vv, tbl_t, sinks, bkt, gattn, after=token_b2)
    wo_g, wu_g, wd_g, fcw_g = ffn_stage3(yan)
    w_out_f = wo_g.reshape(D_MODEL, D_MODEL)
    w_down_f = wd_g.reshape(N_FFN_BLK, FFN_BLK, D_MODEL)
    w_up_f = wu_g.reshape(2, N_FFN_BLK, FFN_BLK, D_MODEL)
    fcw_f = fcw_g.reshape(2, N_FFN_BLK, 3, 1, FFN_BLK)
    fcb = ffn_conv_b.reshape(2, N_FFN_BLK, 1, FFN_BLK)
    h1, u2, up, pre, act, dh2, dh2b, loss_acc = _ffn_fwd(xs, ycn, yan, w_out_f, g2, w_up_f, fcw_f, fcb, w_down_f, tgt)

    dw_down = _grad_tn_blocked(act, dh2b, "grad_w_down").reshape(N_DEV, D_FF // N_DEV, D_MODEL)
    plan_d, slots_d = _scatter_plan(1)
    d_sem = _split_start("scatter_w_down_start", [dw_down], [lax.empty((N_DEV - 1,) + dw_down.shape[1:], BF16)],
                         plan_d, None, _ALL_FOR_W_DOWN)
    dup, dh1, dh1b, dfb, dfcw, dg2 = _ffn_bwd(dh2, dh2b, h1, g2, up, pre, w_up_f, fcw_f, w_down_f, after=d_sem[4])
    dw_up = _grad_tn_blocked(dup.reshape(N_DEV, SEQ, FFN_BLK), u2, "grad_w_up")
    dw_out = _grad_tn([ycn, yan], dh1b, CONV_WIDTH, "grad_w_out").reshape(N_DEV, D_MODEL // N_DEV, D_MODEL)
    out_bwd = {}

    def behind_ffn(token):
        out_bwd["r"] = _out_bwd(dh1b, w_out_f, y_attn, gattn, after=token)
        return out_bwd["r"][0]

    finish_ffn, token_ffn = _reduce_scatter_split(
        [dw_up, dw_out, dfcw.reshape(N_DEV, 3, 1, FFN_BLK)], "ffn", core, behind_ffn)
    dycn, dy_attn, dgattn = out_bwd["r"]
    dqn, dkn, dv, dtbl_t, dsinks = _attn_bwd(qn, kn, vv, dy_attn, probs, psinks, bkt, after=token_ffn)
    dx, dproj, dcw, dgconv, dgq, dgk, dg1 = _mix_in_bwd(xs, dh1, proj, dycn, dqn, dkn, dv, w_in_t, conv_w_f,
                                                         g1, gq, gk, gconv)
    packed, packed_all = _pack_small_grads(dg1, dg2, dgconv, dgattn, dfb, dgq, dgk, dsinks, dtbl_t, loss_acc)
    plan_s, slots_s = _broadcast_plan()
    s_sem, r_sem, src_s, land_s, token_s = _split_start("gather_small_start", [packed], [packed_all], plan_s, None,
                                                        _ALL_FOR_SMALL)
    dw_in_t = _grad_w_in(dproj, u1, after=token_s).reshape(N_DEV, IN_WIDTH // N_DEV, D_MODEL)
    dcw_b = jnp.transpose(dcw.reshape(3, N_DEV, 1, CONV_WIDTH // N_DEV), (1, 0, 2, 3))
    adam = {}
    ffn_got = {}

    def behind_mixer(token):
        ffn_got["r"] = finish_ffn(token)
        return ffn_got["r"][1][0]

    finish_mixer, token_mixer = _reduce_scatter_split([dw_in_t, dcw_b], "mixer", core, behind_mixer)
    (p_wu, p_wo, p_fcw), (r_wu, r_wo, r_fcw) = ffn_got["r"]
    (own_wd,), (r_wd,) = _split_wait("scatter_w_down_wait", d_sem[0], d_sem[1], d_sem[2], d_sem[3], plan_d, slots_d,
                                     token_mixer)
    adam["w_down"] = _adamw_direct(w_down[0], m_w_down[0], v_w_down[0], own_wd, r_wd, me, "adamw_w_down", row_blocks=2)
    adam_up = _adamw(tr(w_up), tr(m_w_up), tr(v_w_up), p_wu, r_wu, chip, "adamw_w_up", row_blocks=4,
                     after=adam["w_down"][0])
    adam["w_out"] = _adamw(w_out[0], m_w_out[0], v_w_out[0], p_wo, r_wo, chip, "adamw_w_out", after=adam_up[0])
    adam_fcw = _adamw(taps(ffn_conv_w), taps(m_ffn_conv_w), taps(v_ffn_conv_w), p_fcw, r_fcw, chip, "adamw_ffn_conv_w",
                      after=adam["w_out"][0])
    _, (r_small,) = _split_wait("gather_small_wait", s_sem, r_sem, src_s, land_s, plan_s, slots_s, adam_fcw[0])
    small_in = {k: (p[k], m[k], v[k]) for k in _SMALL_NAMES}
    small_in["rel_bias_table"] = (tbl_t, m_rel_bias_table.T, v_rel_bias_table.T)
    small_out, loss_row = _adamw_small(r_small, small_in, None)
    (p_wi, p_cw), (r_wi, r_cw) = finish_mixer(loss_row)
    adam_in = _adamw(tr(w_in), tr(m_w_in), tr(v_w_in), p_wi, r_wi, chip, "adamw_w_in")
    adam_cw = _adamw(taps(conv_w), taps(m_conv_w), taps(v_conv_w), p_cw, r_cw, chip, "adamw_conv_w")

    res = {k: tuple(a[None] for a in t) for k, t in adam.items()}
    res["w_up"] = tuple(a.T[None] for a in adam_up)
    res["w_in"] = tuple(a.T[None] for a in adam_in)
    res["ffn_conv_w"] = tuple(taps(a) for a in adam_fcw)
    res["conv_w"] = tuple(taps(a) for a in adam_cw)
    res.update(small_out)
    res["rel_bias_table"] = tuple(a.T for a in small_out["rel_bias_table"])
    loss = loss_row[0, 0]
    order = ("norm_mix_g", "w_in", "conv_w", "q_norm_g", "k_norm_g", "rel_bias_table", "sinks", "out_norm_conv_g",
             "out_norm_attn_g", "w_out", "norm_ffn_g", "w_up", "ffn_conv_w", "ffn_conv_b", "w_down")
    return (loss, dx[None], *[res[k][0] for k in order], *[res[k][1] for k in order],
            *[res[k][2] for k in order], *[res[k][3] for k in order])
```

```python
import math

import numpy as np
import jax
import jax.numpy as jnp
from jax import lax
from jax.experimental import pallas as pl
from jax.experimental.pallas import tpu as pltpu

F32 = jnp.float32
BF16 = jnp.bfloat16

SEQ = 2048
D_MODEL = 1024
CONV_WIDTH = 512
ATTN_WIDTH = 512
KV_WIDTH = 128
HEAD_DIM = 64
N_HEADS = 8
GQA_GROUP = 4
IN_WIDTH = 2304
D_FF = 2816
BLK = 128
NUM_BUCKETS = 32
EPS = 1e-6
NEG_INF = -1e30
ADAM_LR = 0.001
ADAM_B1 = 0.9
ADAM_B2 = 0.999
ADAM_EPS = 1e-08
ADAM_WD = 0.01
ADAM_STEP = 10

N_DEV = 8
FFN_BLK = 2 * D_FF // N_DEV
N_FFN_BLK = D_FF // FFN_BLK
SUBLANES = 8
VMEM_LIMIT = 56 * 1024 * 1024

_MESH = pl.DeviceIdType.MESH
_ANY = pl.BlockSpec(memory_space=pl.ANY)


def _params(sem):
    return pltpu.CompilerParams(dimension_semantics=sem, vmem_limit_bytes=VMEM_LIMIT)


def _ordered_behind(body, pos, after):
    if after is None:
        return body, [], []
    return (lambda *refs: body(*refs[:pos], *refs[pos + 1:])), [_ANY], [after]


def _dot(a, b):
    return jnp.dot(a, b, preferred_element_type=F32)


def _dot_nt(a, b):
    return lax.dot_general(a, b, (((1,), (1,)), ((), ())), preferred_element_type=F32)


def _dot_tn(a, b):
    return lax.dot_general(a, b, (((0,), (0,)), ((), ())), preferred_element_type=F32)


def _shift_down(x, s, halo):
    r = pltpu.roll(x, s, axis=0)
    hr = pltpu.roll(halo, s, axis=0)
    row = lax.broadcasted_iota(jnp.int32, halo.shape, 0)
    top = jnp.where(row < s, hr, r[:SUBLANES])
    return jnp.concatenate([top, r[SUBLANES:]], axis=0)


def _shift_up(x, s, halo):
    n = x.shape[0]
    r = pltpu.roll(x, n - s, axis=0)
    hr = pltpu.roll(halo, SUBLANES - s, axis=0)
    row = lax.broadcasted_iota(jnp.int32, halo.shape, 0)
    bot = jnp.where(row >= SUBLANES - s, hr, r[n - SUBLANES:])
    return jnp.concatenate([r[:n - SUBLANES], bot], axis=0)


def _taps(w):
    return (w[0], w[1], w[2]) if len(w.shape) == 3 else (w[0:1], w[1:2], w[2:3])


def _conv3(x, w, halo):
    x2 = _shift_down(x, 2, halo)
    x1 = _shift_down(x, 1, halo)
    return x2 * w[0] + x1 * w[1] + x * w[2], x2, x1


def _conv3_bwd_input(dy, w, halo_next):
    return dy * w[2] + _shift_up(dy, 1, halo_next) * w[1] + _shift_up(dy, 2, halo_next) * w[0]


def _rstd(x):
    return lax.rsqrt(jnp.mean(x * x, axis=-1, keepdims=True) + EPS)


def _rms_bwd(x, g, dy):
    r = _rstd(x)
    n = x * r
    dn = dy * g
    dx = r * (dn - n * jnp.mean(dn * n, axis=-1, keepdims=True))
    return dx, jnp.sum(dy * n, axis=0, keepdims=True)


def _head_mean(x):
    width = x.shape[-1]
    ri = lax.broadcasted_iota(jnp.int32, (width, width), 0) // HEAD_DIM
    ci = lax.broadcasted_iota(jnp.int32, (width, width), 1) // HEAD_DIM
    ones = jnp.where(ri == ci, 1.0, 0.0).astype(BF16)
    hi = x.astype(BF16)
    lo = (x - hi.astype(F32)).astype(BF16)
    return (_dot(hi, ones) + _dot(lo, ones)) * (1.0 / HEAD_DIM)


def _head_norm(x, g, heads):
    return x * lax.rsqrt(_head_mean(x * x) + EPS) * jnp.tile(g, (1, heads))


def _head_norm_bwd(x, g, dy, heads):
    r = lax.rsqrt(_head_mean(x * x) + EPS)
    n = x * r
    dn = dy * jnp.tile(g, (1, heads))
    dx = r * (dn - n * _head_mean(dn * n))
    per_lane = jnp.sum(dy * n, axis=0, keepdims=True)
    dg = per_lane[:, 0:HEAD_DIM]
    for h in range(1, heads):
        dg = dg + per_lane[:, HEAD_DIM * h:HEAD_DIM * (h + 1)]
    return dx, dg


def _bucket_map():
    q = np.arange(BLK)[:, None]
    j = np.arange(BLK)[None, :]
    n = np.where(j > q, q + BLK - j, q - j)
    nf = np.maximum(n, 1).astype(np.float32)
    max_exact = NUM_BUCKETS // 2
    large = max_exact + (np.log(nf / max_exact) / math.log(BLK / max_exact) * (NUM_BUCKETS - max_exact)).astype(np.int32)
    large = np.minimum(large, NUM_BUCKETS - 1)
    return np.where(n < max_exact, n, large).astype(np.int32)


def _coords():
    return lax.axis_index("x"), lax.axis_index("y"), lax.axis_index("c")


def _lin(px, py, pc):
    return 4 * px + 2 * py + pc


def _chips(x, y):
    return [(1 - x, y), (x, 1 - y), (1 - x, 1 - y)]


def _peers(x, y, c):
    return [(1 - x if r & 4 else x, 1 - y if r & 2 else y, 1 - c if r & 1 else c) for r in range(1, N_DEV)]


_HBM = pl.BlockSpec(memory_space=pltpu.HBM)
_SEM = pl.BlockSpec(memory_space=pltpu.SEMAPHORE)
_EFFECT = pltpu.SideEffectType.DATAFLOW_SIDE_EFFECTING


def _in_hbm(a):
    return pltpu.with_memory_space_constraint(a, pltpu.HBM)


_SIBLING = (1, lambda x, y, c: [(x, y, 1 - c)])
_SIBLING_AND_CHIPS = (2, lambda x, y, c: [(x, y, 1 - c)] + [(cx, cy, c) for cx, cy in _chips(x, y)])
_SIBLING_AND_NEIGHBOURS = (3, lambda x, y, c: [(x, y, 1 - c), (1 - x, y, c), (x, 1 - y, c)])
_ONWARD_AND_SIBLING = (4, lambda x, y, c: [(jnp.where(c == 1, x, 1 - x), jnp.where(c == 1, 1 - y, y), c), (x, y, 1 - c)])
_ALL_FOR_W_DOWN = (5, lambda x, y, c: _peers(x, y, c))
_CHIPS = (6, lambda x, y, c: [(cx, cy, c) for cx, cy in _chips(x, y)])
_ALL_FOR_SMALL = (7, lambda x, y, c: _peers(x, y, c))


def _split_start(name, srcs, lands, plan, after, handshake):
    ns, nl = len(srcs), len(lands)
    n_copies = len(plan(0, 0, 0))
    n_after = 0 if after is None else 1
    collective_id, peers_of = handshake

    def body(*refs):
        src_refs, land_refs = refs[:ns + nl], refs[ns:ns + nl]
        send_sems, recv_sems = refs[ns + nl + n_after], refs[ns + nl + n_after + 1]
        token = refs[-1]
        barrier = pltpu.get_barrier_semaphore()
        peers = peers_of(*_coords())
        for peer in peers:
            pl.semaphore_signal(barrier, inc=1, device_id=peer, device_id_type=_MESH)
        pl.semaphore_wait(barrier, len(peers))
        for k, (a, s_slot, l, d_slot, dev) in enumerate(plan(*_coords())):
            src = src_refs[a] if s_slot is None else src_refs[a].at[s_slot]
            pltpu.make_async_remote_copy(src_ref=src, dst_ref=land_refs[l].at[d_slot], send_sem=send_sems.at[k],
                                         recv_sem=recv_sems.at[k], device_id=dev, device_id_type=_MESH).start()
        token[...] = jnp.zeros_like(token)

    arrs = list(srcs) + list(lands)
    out = pl.pallas_call(
        body, name=name,
        out_shape=(pltpu.SemaphoreType.DMA((n_copies,)), pltpu.SemaphoreType.DMA((n_copies,)),
                   *[pltpu.HBM(a.shape, a.dtype) for a in arrs], jax.ShapeDtypeStruct((SUBLANES, 128), F32)),
        in_specs=[_HBM] * (ns + nl) + [_ANY] * n_after,
        out_specs=(_SEM, _SEM, *[_HBM] * (ns + nl), pl.BlockSpec(memory_space=pltpu.VMEM)),
        input_output_aliases={i: 2 + i for i in range(ns + nl)},
        compiler_params=pltpu.CompilerParams(has_side_effects=_EFFECT, collective_id=collective_id),
    )(*[_in_hbm(a) for a in arrs], *([] if after is None else [after]))
    return out[0], out[1], list(out[2:2 + ns]), list(out[2 + ns:2 + ns + nl]), out[-1]


def _split_wait(name, send_sems, recv_sems, srcs, lands, plan, recv_slots, after):
    ns, nl = len(srcs), len(lands)

    def body(*refs):
        src_refs, land_refs = refs[:ns + nl], refs[ns:ns + nl]
        send_sems, recv_sems = refs[ns + nl], refs[ns + nl + 1]
        coords = _coords()
        slots = recv_slots(*coords)
        for k, (a, s_slot, l, _, dev) in enumerate(plan(*coords)):
            src = src_refs[a] if s_slot is None else src_refs[a].at[s_slot]
            cp = pltpu.make_async_remote_copy(src_ref=src, dst_ref=land_refs[l].at[slots[k]], send_sem=send_sems.at[k],
                                              recv_sem=recv_sems.at[k], device_id=dev, device_id_type=_MESH)
            cp.wait_send()
            cp.wait_recv()

    arrs = list(srcs) + list(lands)
    out = pl.pallas_call(
        body, name=name,
        out_shape=tuple(pltpu.HBM(a.shape, a.dtype) for a in arrs),
        in_specs=[_HBM] * (ns + nl) + [_SEM, _SEM, _ANY],
        out_specs=tuple([_HBM] * (ns + nl)),
        input_output_aliases={i: i for i in range(ns + nl)},
        compiler_params=pltpu.CompilerParams(has_side_effects=_EFFECT),
    )(*arrs, send_sems, recv_sems, after)
    return list(out[:ns]), list(out[ns:])


def _gather_plan_ici(n):
    def plan(x, y, c):
        me = _lin(x, y, c)
        out = []
        for a in range(n):
            out.append((a, me, a, me, (x, y, 1 - c)))
            out += [(a, me, a, me, (cx, cy, c)) for cx, cy in _chips(x, y)]
        return out

    def recv_slots(x, y, c):
        out = []
        for _ in range(n):
            out.append(_lin(x, y, 1 - c))
            out += [_lin(cx, cy, c) for cx, cy in _chips(x, y)]
        return out

    return plan, recv_slots


def _gather_plan_d2d(n):
    def plan(x, y, c):
        return [(a, _lin(cx, cy, c), a, _lin(cx, cy, c), (x, y, 1 - c)) for a in range(n) for cx, cy in _chips(x, y)]

    def recv_slots(x, y, c):
        return [_lin(cx, cy, 1 - c) for _ in range(n) for cx, cy in _chips(x, y)]

    return plan, recv_slots


def _all_gather_split(lands, tag, after):
    n = len(lands)
    plan1, slots1 = _gather_plan_ici(n)
    s1, r1, _, lands, token = _split_start(f"gather_{tag}_ici_start", [], lands, plan1, after, _SIBLING_AND_CHIPS)

    def finish(after):
        _, got = _split_wait(f"gather_{tag}_ici_wait", s1, r1, [], lands, plan1, slots1, after)
        plan2, slots2 = _gather_plan_d2d(n)
        s2, r2, _, got, token2 = _split_start(f"gather_{tag}_d2d_start", [], got, plan2, None, _SIBLING)
        return _split_wait(f"gather_{tag}_d2d_wait", s2, r2, [], got, plan2, slots2, token2)[1]

    return finish, token


def _all_gather_tree(lands, tag, after):
    n = len(lands)

    def plan1(x, y, c):
        me = _lin(x, y, c)
        return [(a, me, a, me, dev) for a in range(n) for dev in ((x, y, 1 - c), (1 - x, y, c), (x, 1 - y, c))]

    def slots1(x, y, c):
        return [s for _ in range(n) for s in (_lin(x, y, 1 - c), _lin(1 - x, y, c), _lin(x, 1 - y, c))]

    def plan2(x, y, c):
        from_x, from_y = _lin(1 - x, y, c), _lin(x, 1 - y, c)
        north = c == 1
        passed = jnp.where(north, from_x, from_y)
        onward = (jnp.where(north, x, 1 - x), jnp.where(north, 1 - y, y), c)
        sib = (x, y, 1 - c)
        return [cp for a in range(n) for cp in ((a, passed, a, passed, onward), (a, from_x, a, from_x, sib),
                                                (a, from_y, a, from_y, sib))]

    def slots2(x, y, c):
        return [s for _ in range(n) for s in (_lin(1 - x, 1 - y, c), _lin(1 - x, y, 1 - c), _lin(x, 1 - y, 1 - c))]

    def plan3(x, y, c):
        diag = _lin(1 - x, 1 - y, c)
        return [(a, diag, a, diag, (x, y, 1 - c)) for a in range(n)]

    def slots3(x, y, c):
        return [_lin(1 - x, 1 - y, 1 - c)] * n

    s1, r1, _, lands, token = _split_start(f"gather_{tag}_1_start", [], lands, plan1, after, _SIBLING_AND_NEIGHBOURS)
    state = {}

    def stage2(after):
        _, got = _split_wait(f"gather_{tag}_1_wait", s1, r1, [], lands, plan1, slots1, after)
        state["s"], state["r"], _, state["lands"], token2 = _split_start(f"gather_{tag}_2_start", [], got, plan2, None,
                                                                         _ONWARD_AND_SIBLING)
        return token2

    def stage3(after):
        _, got = _split_wait(f"gather_{tag}_2_wait", state["s"], state["r"], [], state["lands"], plan2, slots2, after)
        s3, r3, _, got, token3 = _split_start(f"gather_{tag}_3_start", [], got, plan3, None, _SIBLING)
        return _split_wait(f"gather_{tag}_3_wait", s3, r3, [], got, plan3, slots3, token3)[1]

    return stage2, stage3, token


_CHIP_LIST = ((0, 0), (0, 1), (1, 0), (1, 1))


def _reduce_plan_d2d(n):
    def plan(x, y, c):
        return [(a, _lin(qx, qy, 1 - c), a, q, (x, y, 1 - c)) for a in range(n) for q, (qx, qy) in enumerate(_CHIP_LIST)]

    def recv_slots(x, y, c):
        return [q for _ in range(n) for q in range(4)]

    return plan, recv_slots


def _reduce_plan_ici(n):
    def plan(x, y, c):
        return [(a, 2 * cx + cy, a, j, (cx, cy, c)) for a in range(n) for j, (cx, cy) in enumerate(_chips(x, y))]

    def recv_slots(x, y, c):
        return [j for _ in range(n) for j in range(3)]

    return plan, recv_slots


def _scatter_plan(n):
    def plan(x, y, c):
        return [(a, _lin(*peer), a, r, peer) for a in range(n) for r, peer in enumerate(_peers(x, y, c))]

    def recv_slots(x, y, c):
        return [r for _ in range(n) for r in range(N_DEV - 1)]

    return plan, recv_slots


def _broadcast_plan():
    def plan(x, y, c):
        return [(0, None, 0, _lin(x, y, c), peer) for peer in _peers(x, y, c)]

    def recv_slots(x, y, c):
        return [_lin(*peer) for peer in _peers(x, y, c)]

    return plan, recv_slots


def _chip_partial(grads, recvd, core, name):
    n = len(grads)

    def body(c_ref, *refs):
        for a in range(n):
            g_ref, r_ref, o_ref = refs[a], refs[n + a], refs[2 * n + a]
            o_ref[...] = (g_ref[...].astype(F32) + r_ref[...].astype(F32)).astype(o_ref.dtype)

    def blk(a, own):
        zeros = (0,) * (a.ndim - 1)
        return pl.BlockSpec((None,) + a.shape[1:],
                            (lambda q, c_ref: (2 * q + c_ref[0],) + zeros) if own else (lambda q, c_ref: (q,) + zeros))

    return pl.pallas_call(
        body, name=name,
        grid_spec=pltpu.PrefetchScalarGridSpec(
            num_scalar_prefetch=1, grid=(4,),
            in_specs=[blk(a, True) for a in grads] + [blk(a, False) for a in recvd],
            out_specs=[blk(a, False) for a in recvd]),
        out_shape=[jax.ShapeDtypeStruct(a.shape, a.dtype) for a in recvd],
        compiler_params=_params(("arbitrary",)),
    )(core, *grads, *recvd)


def _reduce_scatter_split(grads, tag, core, behind):
    n = len(grads)
    plan1, slots1 = _reduce_plan_d2d(n)
    lands1 = [lax.empty((4,) + a.shape[1:], a.dtype) for a in grads]
    s1, r1, srcs1, lands1, token1 = _split_start(f"reduce_{tag}_d2d_start", grads, lands1, plan1, None, _SIBLING)
    own, got = _split_wait(f"reduce_{tag}_d2d_wait", s1, r1, srcs1, lands1, plan1, slots1, behind(token1))
    parts = _chip_partial(own, got, core, f"reduce_{tag}_partial")
    plan2, slots2 = _reduce_plan_ici(n)
    lands2 = [lax.empty((3,) + a.shape[1:], a.dtype) for a in grads]
    s2, r2, srcs2, lands2, token2 = _split_start(f"reduce_{tag}_ici_start", parts, lands2, plan2, None, _CHIPS)

    def finish(after):
        return _split_wait(f"reduce_{tag}_ici_wait", s2, r2, srcs2, lands2, plan2, slots2, after)

    return finish, token2


def _place_shards(me, shards, dtypes, name, after=None):
    n = len(shards)

    def body(me_ref, *refs):
        for a in range(n):
            refs[n + a][...] = refs[a][...].astype(dtypes[a])

    full = lambda s: pl.BlockSpec(s.shape, lambda i, me_ref: (0,) * s.ndim)
    slot = lambda s: pl.BlockSpec((None,) + s.shape, lambda i, me_ref: (me_ref[0],) + (0,) * s.ndim)
    body, more_specs, more = _ordered_behind(body, 1 + n, after)
    return pl.pallas_call(
        body, name=name,
        grid_spec=pltpu.PrefetchScalarGridSpec(num_scalar_prefetch=1, grid=(1,),
                                               in_specs=[full(s) for s in shards] + more_specs,
                                               out_specs=[slot(s) for s in shards]),
        out_shape=[jax.ShapeDtypeStruct((N_DEV,) + s.shape, d) for s, d in zip(shards, dtypes)],
        compiler_params=_params(("arbitrary",)),
    )(me, *shards, *more)


def _mix_in_fwd(x, g1, w_in_t, conv_w, gq, gk, gconv):
    tm = 512
    n_t = SEQ // tm

    def body(x_ref, g1_ref, w_ref, cw_ref, gq_ref, gk_ref, gc_ref,
             proj_ref, u1_ref, ycn_ref, qn_ref, kn_ref, v_ref, halo_ref):
        @pl.when(pl.program_id(0) == 0)
        def _():
            halo_ref[...] = jnp.zeros_like(halo_ref)

        xv = x_ref[...]
        u = (xv * _rstd(xv) * g1_ref[...]).astype(BF16)
        u1_ref[...] = u
        proj = _dot_nt(u, w_ref[...])
        proj_ref[...] = proj
        gate_b = proj[:, 0:CONV_WIDTH]
        a = proj[:, CONV_WIDTH:2 * CONV_WIDTH] * proj[:, 2 * CONV_WIDTH:3 * CONV_WIDTH]
        cv, _, _ = _conv3(a, _taps(cw_ref[...]), halo_ref[...])
        halo_ref[...] = a[tm - SUBLANES:]
        yc = gate_b * cv
        ycn_ref[...] = (yc * _rstd(yc) * gc_ref[...]).astype(BF16)
        q0 = 3 * CONV_WIDTH
        qn_ref[...] = _head_norm(proj[:, q0:q0 + ATTN_WIDTH], gq_ref[...], N_HEADS).astype(BF16)
        k0 = q0 + ATTN_WIDTH
        kn_ref[...] = _head_norm(proj[:, k0:k0 + KV_WIDTH], gk_ref[...], 2).astype(BF16)
        v_ref[...] = proj[:, k0 + KV_WIDTH:k0 + 2 * KV_WIDTH].astype(BF16)

    const = lambda shape: pl.BlockSpec(shape, lambda i: (0,) * len(shape))
    rows = lambda w: pl.BlockSpec((tm, w), lambda i: (i, 0))
    return pl.pallas_call(
        body, name="mix_in_fwd", grid=(n_t,),
        in_specs=[rows(D_MODEL), const((1, D_MODEL)), const((IN_WIDTH, D_MODEL)), const((3, CONV_WIDTH)),
                  const((1, HEAD_DIM)), const((1, HEAD_DIM)), const((1, CONV_WIDTH))],
        out_specs=[rows(IN_WIDTH), rows(D_MODEL), rows(CONV_WIDTH), rows(ATTN_WIDTH), rows(KV_WIDTH), rows(KV_WIDTH)],
        out_shape=[jax.ShapeDtypeStruct((SEQ, IN_WIDTH), F32), jax.ShapeDtypeStruct((SEQ, D_MODEL), BF16),
                   jax.ShapeDtypeStruct((SEQ, CONV_WIDTH), BF16),
                   jax.ShapeDtypeStruct((SEQ, ATTN_WIDTH), BF16), jax.ShapeDtypeStruct((SEQ, KV_WIDTH), BF16),
                   jax.ShapeDtypeStruct((SEQ, KV_WIDTH), BF16)],
        scratch_shapes=[pltpu.VMEM((SUBLANES, CONV_WIDTH), F32)],
        compiler_params=_params(("arbitrary",)),
    )(x, g1, w_in_t, conv_w, gq, gk, gconv)


GROUP_ROWS = GQA_GROUP * BLK
QUERY_BLOCKS_PER_STEP = 2


def _band_bias(tbl_ref, bkt, bias_ref):
    for h in range(N_HEADS):
        acc = jnp.zeros(bkt.shape, F32)
        for b in range(NUM_BUCKETS):
            acc = jnp.where(bkt == b, tbl_ref[h, b], acc)
        bias_ref[h // GQA_GROUP, BLK * (h % GQA_GROUP):BLK * (h % GQA_GROUP + 1), :] = acc


def _band_masks(i):
    qi = lax.broadcasted_iota(jnp.int32, (GROUP_ROWS, BLK), 0) & (BLK - 1)
    ji = lax.broadcasted_iota(jnp.int32, (GROUP_ROWS, BLK), 1)
    upper = ji > qi
    return upper, upper & (i == 0)


def _stack_heads(x, g):
    return jnp.concatenate([x[:, HEAD_DIM * h:HEAD_DIM * (h + 1)] for h in range(GQA_GROUP * g, GQA_GROUP * (g + 1))], axis=0)


def _unstack_heads(groups):
    return jnp.concatenate([p[BLK * t:BLK * (t + 1)] for p in groups for t in range(GQA_GROUP)], axis=-1)


def _per_head_rows(vals):
    row = lax.broadcasted_iota(jnp.int32, (GROUP_ROWS, 1), 0)
    col = jnp.full((GROUP_ROWS, 1), vals[GQA_GROUP - 1], F32)
    for t in range(GQA_GROUP - 2, -1, -1):
        col = jnp.where(row < BLK * (t + 1), vals[t], col)
    return col


def _band_rows(ref, i):
    prev = pl.multiple_of(jnp.maximum(i - 1, 0) * BLK, BLK)
    cur = pl.multiple_of(i * BLK, BLK)
    return jnp.concatenate([ref[pl.ds(prev, BLK), :], ref[pl.ds(cur, BLK), :]], axis=0), prev, cur


def _fold(band, upper):
    return jnp.where(upper, band[:, :BLK], band[:, BLK:])


def _unfold(tile, upper):
    return jnp.concatenate([jnp.where(upper, tile, 0.0), jnp.where(upper, 0.0, tile)], axis=1)


def _head_probs(qh, kh, bias, upper, dead, sink):
    logits = _fold(_dot_nt(qh, kh), upper) * (HEAD_DIM ** -0.5) + bias
    logits = jnp.where(dead, NEG_INF, logits)
    m = jnp.maximum(jnp.max(logits, axis=-1, keepdims=True), sink)
    p = jnp.exp(logits - m)
    es = jnp.exp(sink - m)
    den = jnp.sum(p, axis=-1, keepdims=True) + es
    return p / den, es / den


def _attn_fwd(qn, kn, v, tbl, sinks, bkt, gattn, after=None):
    n_b = SEQ // BLK

    def body(q_ref, k_ref, v_ref, tbl_ref, sink_ref, bkt_ref, ga_ref, y_ref, yn_ref, p_ref, ps_ref, bias_ref):
        step = pl.program_id(0)

        @pl.when(step == 0)
        def _():
            _band_bias(tbl_ref, bkt_ref[...], bias_ref)

        lane = lax.broadcasted_iota(jnp.int32, (BLK, 128), 1)
        for b in range(QUERY_BLOCKS_PER_STEP):
            i = QUERY_BLOCKS_PER_STEP * step + b
            rows = slice(BLK * b, BLK * (b + 1))
            kb, _, _ = _band_rows(k_ref, i)
            vb, _, _ = _band_rows(v_ref, i)
            upper, dead = _band_masks(i)
            q = q_ref[rows, :]
            outs = []
            psinks = jnp.zeros((BLK, 128), F32)
            for g in range(N_HEADS // GQA_GROUP):
                kv = slice(HEAD_DIM * g, HEAD_DIM * (g + 1))
                sink = _per_head_rows([sink_ref[0, GQA_GROUP * g + t] for t in range(GQA_GROUP)])
                probs, psink = _head_probs(_stack_heads(q, g), kb[:, kv], bias_ref[g], upper, dead, sink)
                p_ref[b, g] = probs.astype(BF16)
                for t in range(GQA_GROUP):
                    psinks = jnp.where(lane == GQA_GROUP * g + t, psink[BLK * t:BLK * (t + 1)], psinks)
                outs.append(_dot(_unfold(probs, upper).astype(BF16), vb[:, kv]))
            ps_ref[rows, :] = psinks
            y = _unstack_heads(outs)
            y_ref[rows, :] = y
            yn_ref[rows, :] = (y * _rstd(y) * ga_ref[...]).astype(BF16)

    const = lambda shape: pl.BlockSpec(shape, lambda i: (0,) * len(shape))
    rows = lambda w: pl.BlockSpec((QUERY_BLOCKS_PER_STEP * BLK, w), lambda i: (i, 0))
    smem = pl.BlockSpec(memory_space=pltpu.SMEM)
    body, more_specs, more = _ordered_behind(body, 7, after)
    return pl.pallas_call(
        body, name="attn_fwd", grid=(n_b // QUERY_BLOCKS_PER_STEP,),
        in_specs=[rows(ATTN_WIDTH), const((SEQ, KV_WIDTH)), const((SEQ, KV_WIDTH)), smem, smem,
                  const((BLK, BLK)), const((1, ATTN_WIDTH))] + more_specs,
        out_specs=[rows(ATTN_WIDTH), rows(ATTN_WIDTH),
                   pl.BlockSpec((QUERY_BLOCKS_PER_STEP, N_HEADS // GQA_GROUP, GROUP_ROWS, BLK), lambda i: (i, 0, 0, 0)),
                   rows(128)],
        out_shape=[jax.ShapeDtypeStruct((SEQ, ATTN_WIDTH), F32), jax.ShapeDtypeStruct((SEQ, ATTN_WIDTH), BF16),
                   jax.ShapeDtypeStruct((n_b, N_HEADS // GQA_GROUP, GROUP_ROWS, BLK), BF16),
                   jax.ShapeDtypeStruct((SEQ, 128), F32)],
        scratch_shapes=[pltpu.VMEM((N_HEADS // GQA_GROUP, GROUP_ROWS, BLK), F32)],
        compiler_params=_params(("arbitrary",)),
    )(qn, kn, v, tbl, sinks, bkt, gattn, *more)


def _ffn_block(i, step):
    return jnp.where(i % 2 == 0, step, N_FFN_BLK - 1 - step)


def _ffn_fwd(x, ycn, yan, w_out, g2, w_up, fcw, fcb, w_down, tgt):
    tm = 512
    n_t = SEQ // tm

    def body(x_ref, ycn_ref, yan_ref, wo_ref, g2_ref, wu_ref, cw_ref, b_ref, wd_ref, tgt_ref,
             h1_ref, u2_ref, up_ref, pre_ref, act_ref, dh2_ref, dh2b_ref, loss_ref, acc_ref, halo_ref):
        i, step = pl.program_id(0), pl.program_id(1)
        j = _ffn_block(i, step)

        @pl.when((i == 0) & (step == 0))
        def _():
            loss_ref[...] = jnp.zeros_like(loss_ref)

        @pl.when(step == 0)
        def _():
            h1 = x_ref[...] + _dot(ycn_ref[...], wo_ref[0:CONV_WIDTH, :]) + _dot(yan_ref[...], wo_ref[CONV_WIDTH:, :])
            h1_ref[...] = h1
            u2_ref[...] = (h1 * _rstd(h1) * g2_ref[...]).astype(BF16)
            acc_ref[...] = jnp.zeros_like(acc_ref)

        u2 = u2_ref[...]
        pre = []
        for s in range(2):
            up = _dot_nt(u2, wu_ref[s])
            up_ref[s] = up.astype(BF16)
            halo = jnp.where(i == 0, 0.0, halo_ref[s, j])
            pre.append(_conv3(up, _taps(cw_ref.at[s]), halo)[0] + b_ref[s])
            pre_ref[s] = pre[s].astype(BF16)
            halo_ref[s, j] = up[tm - SUBLANES:]
        g, val = pre
        act = (g * jax.nn.sigmoid(g) * val).astype(BF16)
        act_ref[...] = act
        acc_ref[...] += _dot(act, wd_ref[...])

        @pl.when(step == N_FFN_BLK - 1)
        def _():
            err = h1_ref[...] + acc_ref[...] - tgt_ref[...]
            loss_ref[...] += 0.5 * jnp.sum(err * err) / D_MODEL
            dh2 = err / D_MODEL
            dh2_ref[...] = dh2
            dh2b_ref[...] = dh2.astype(BF16)

    rows = lambda w: pl.BlockSpec((tm, w), lambda i, step: (i, 0))
    const = lambda shape: pl.BlockSpec(shape, lambda i, step: (0,) * len(shape))
    pair = lambda *s: pl.BlockSpec((2, None) + s, lambda i, step: (0, _ffn_block(i, step)) + (0,) * len(s))
    upb = pl.BlockSpec((2, None, tm, FFN_BLK), lambda i, step: (0, _ffn_block(i, step), i, 0))
    return pl.pallas_call(
        body, name="ffn_fwd", grid=(n_t, N_FFN_BLK),
        in_specs=[rows(D_MODEL), rows(CONV_WIDTH), rows(ATTN_WIDTH), const((D_MODEL, D_MODEL)), const((1, D_MODEL)),
                  pair(FFN_BLK, D_MODEL), pair(3, 1, FFN_BLK), pair(1, FFN_BLK),
                  pl.BlockSpec((None, FFN_BLK, D_MODEL), lambda i, step: (_ffn_block(i, step), 0, 0)), rows(D_MODEL)],
        out_specs=[rows(D_MODEL), rows(D_MODEL), upb, upb,
                   pl.BlockSpec((None, tm, FFN_BLK), lambda i, step: (_ffn_block(i, step), i, 0)),
                   rows(D_MODEL), rows(D_MODEL), const((SUBLANES, 128))],
        out_shape=[jax.ShapeDtypeStruct((SEQ, D_MODEL), F32), jax.ShapeDtypeStruct((SEQ, D_MODEL), BF16),
                   jax.ShapeDtypeStruct((2, N_FFN_BLK, SEQ, FFN_BLK), BF16),
                   jax.ShapeDtypeStruct((2, N_FFN_BLK, SEQ, FFN_BLK), BF16),
                   jax.ShapeDtypeStruct((N_FFN_BLK, SEQ, FFN_BLK), BF16),
                   jax.ShapeDtypeStruct((SEQ, D_MODEL), F32), jax.ShapeDtypeStruct((SEQ, D_MODEL), BF16),
                   jax.ShapeDtypeStruct((SUBLANES, 128), F32)],
        scratch_shapes=[pltpu.VMEM((tm, D_MODEL), F32), pltpu.VMEM((2, N_FFN_BLK, SUBLANES, FFN_BLK), F32)],
        compiler_params=_params(("arbitrary", "arbitrary")),
    )(x, ycn, yan, w_out, g2, w_up, fcw, fcb, w_down, tgt)


def _ffn_bwd(dh2, dh2b, h1, g2, up, pre, w_up, fcw, w_down, after=None):
    tm = 512
    units = ((288, 224), (0, 288))
    n_t = SEQ // tm

    def body(dh2_ref, dh2b_ref, h1_ref, g2_ref, up_ref, pre_ref, wu_ref, cw_ref, wd_ref,
             dup_ref, dh1_ref, dh1b_ref, dfb_ref, dfcw_ref, dg2_ref, acc_ref, next_ref):
        i, step = pl.program_id(0), pl.program_id(1)
        j = _ffn_block(i, step)

        @pl.when((i == 0) & (step == 0))
        def _():
            dfb_ref[...] = jnp.zeros_like(dfb_ref)
            dfcw_ref[...] = jnp.zeros_like(dfcw_ref)
            dg2_ref[...] = jnp.zeros_like(dg2_ref)

        @pl.when(step == 0)
        def _():
            acc_ref[...] = jnp.zeros_like(acc_ref)

        nxt = [jnp.where(i == 0, 0.0, next_ref[s, j]) for s in range(2)]
        sums = [[0.0] * 4 for _ in range(2)]
        stores = []
        for r0, rn in units:
            rows = slice(r0, r0 + rn)
            g, val = pre_ref[0, rows, :].astype(F32), pre_ref[1, rows, :].astype(F32)
            sg = jax.nn.sigmoid(g)
            silu = g * sg
            dact = _dot_nt(dh2b_ref[rows, :], wd_ref[...])
            dpre = (dact * val * (sg * (1.0 + g * (1.0 - sg))), dact * silu)
            dups = []
            for s in range(2):
                d = dpre[s]
                u = up_ref[s, rows, :].astype(F32)
                w = _taps(cw_ref.at[s])
                d1 = _shift_up(d, 1, nxt[s])
                d2 = _shift_up(d, 2, nxt[s])
                nxt[s] = d[:SUBLANES]
                for t, term in enumerate((d, d2 * u, d1 * u, d * u)):
                    sums[s][t] = sums[s][t] + jnp.sum(term, axis=0, keepdims=True)
                dups.append((d * w[2] + d1 * w[1] + d2 * w[0]).astype(BF16))
                stores.append((s, rows, dups[s]))
            acc_ref[rows, :] += _dot(dups[0], wu_ref[0]) + _dot(dups[1], wu_ref[1])
        for s, rows, dup in stores:
            dup_ref[s, rows, :] = dup
        for s in range(2):
            next_ref[s, j] = nxt[s]
            dfb_ref[s, j] += sums[s][0]
            for t in range(3):
                dfcw_ref[s, j, t] += sums[s][1 + t]

        @pl.when(step == N_FFN_BLK - 1)
        def _():
            dn, dgain = _rms_bwd(h1_ref[...], g2_ref[...], acc_ref[...])
            dh1 = dh2_ref[...] + dn
            dh1_ref[...] = dh1
            dh1b_ref[...] = dh1.astype(BF16)
            dg2_ref[...] += dgain

    rev = lambda i: n_t - 1 - i
    rows = lambda w: pl.BlockSpec((tm, w), lambda i, step: (rev(i), 0))
    const = lambda shape: pl.BlockSpec(shape, lambda i, step: (0,) * len(shape))
    pair = lambda *s: pl.BlockSpec((2, None) + s, lambda i, step: (0, _ffn_block(i, step)) + (0,) * len(s))
    upb = pl.BlockSpec((2, None, tm, FFN_BLK), lambda i, step: (0, _ffn_block(i, step), rev(i), 0))
    body, more_specs, more = _ordered_behind(body, 9, after)
    return pl.pallas_call(
        body, name="ffn_bwd", grid=(n_t, N_FFN_BLK),
        in_specs=[rows(D_MODEL), rows(D_MODEL), rows(D_MODEL), const((1, D_MODEL)), upb, upb,
                  pair(FFN_BLK, D_MODEL), pair(3, 1, FFN_BLK),
                  pl.BlockSpec((None, FFN_BLK, D_MODEL), lambda i, step: (_ffn_block(i, step), 0, 0))] + more_specs,
        out_specs=[upb, rows(D_MODEL), rows(D_MODEL),
                   const((2, N_FFN_BLK, 1, FFN_BLK)), const((2, N_FFN_BLK, 3, 1, FFN_BLK)), const((1, D_MODEL))],
        out_shape=[jax.ShapeDtypeStruct((2, N_FFN_BLK, SEQ, FFN_BLK), BF16), jax.ShapeDtypeStruct((SEQ, D_MODEL), F32),
                   jax.ShapeDtypeStruct((SEQ, D_MODEL), BF16), jax.ShapeDtypeStruct((2, N_FFN_BLK, 1, FFN_BLK), F32),
                   jax.ShapeDtypeStruct((2, N_FFN_BLK, 3, 1, FFN_BLK), F32), jax.ShapeDtypeStruct((1, D_MODEL), F32)],
        scratch_shapes=[pltpu.VMEM((tm, D_MODEL), F32), pltpu.VMEM((2, N_FFN_BLK, SUBLANES, FFN_BLK), F32)],
        compiler_params=_params(("arbitrary", "arbitrary")),
    )(dh2, dh2b, h1, g2, up, pre, w_up, fcw, w_down, *more)


def _grad_tn(a_list, b, out_rows, name, after=None):
    n = len(a_list)
    ncol = b.shape[1]

    def body(*refs):
        a_refs, b_ref, o_ref = refs[:n], refs[n], refs[n + 1]
        j = pl.program_id(0)
        for k in range(n):
            @pl.when(j == k)
            def _(k=k):
                o_ref[...] = _dot_tn(a_refs[k][...], b_ref[...]).astype(BF16)

    full = lambda shape: pl.BlockSpec(shape, lambda j: (0,) * len(shape))
    body, more_specs, more = _ordered_behind(body, n + 1, after)
    return pl.pallas_call(
        body, name=name, grid=(n,),
        in_specs=[full((SEQ, out_rows))] * n + [full((SEQ, ncol))] + more_specs,
        out_specs=pl.BlockSpec((None, out_rows, ncol), lambda j: (j, 0, 0)),
        out_shape=jax.ShapeDtypeStruct((n, out_rows, ncol), BF16),
        compiler_params=_params(("arbitrary",)),
    )(*a_list, b, *more)


def _grad_tn_blocked(a, b, name, per_step=2):
    nb, _, a_w = a.shape
    b_w = b.shape[-1]

    def body(a_ref, b_ref, o_ref):
        for p in range(per_step):
            o_ref[p] = _dot_tn(a_ref[p], b_ref[...]).astype(BF16)

    return pl.pallas_call(
        body, name=name, grid=(nb // per_step,),
        in_specs=[pl.BlockSpec((per_step, SEQ, a_w), lambda k: (k, 0, 0)), pl.BlockSpec((SEQ, b_w), lambda k: (0, 0))],
        out_specs=pl.BlockSpec((per_step, a_w, b_w), lambda k: (k, 0, 0)),
        out_shape=jax.ShapeDtypeStruct((nb, a_w, b_w), BF16),
        compiler_params=_params(("arbitrary",)),
    )(a, b)


def _out_bwd(dh1b, w_out, y_attn, gattn, after=None):
    tm = 1024
    n_t = SEQ // tm

    def body(dh_ref, wo_ref, y_ref, ga_ref, dycn_ref, dy_ref, dga_ref):
        @pl.when(pl.program_id(0) == 0)
        def _():
            dga_ref[...] = jnp.zeros_like(dga_ref)

        dycat = _dot_nt(dh_ref[...], wo_ref[...])
        dycn_ref[...] = dycat[:, :CONV_WIDTH]
        dy, dga = _rms_bwd(y_ref[...], ga_ref[...], dycat[:, CONV_WIDTH:])
        dy_ref[...] = dy
        dga_ref[...] += dga

    rows = lambda w: pl.BlockSpec((tm, w), lambda i: (i, 0))
    const = lambda shape: pl.BlockSpec(shape, lambda i: (0,) * len(shape))
    body, more_specs, more = _ordered_behind(body, 4, after)
    return pl.pallas_call(
        body, name="out_bwd", grid=(n_t,),
        in_specs=[rows(D_MODEL), const((D_MODEL, D_MODEL)), rows(ATTN_WIDTH), const((1, ATTN_WIDTH))] + more_specs,
        out_specs=[rows(CONV_WIDTH), rows(ATTN_WIDTH), const((1, ATTN_WIDTH))],
        out_shape=[jax.ShapeDtypeStruct((SEQ, CONV_WIDTH), F32), jax.ShapeDtypeStruct((SEQ, ATTN_WIDTH), F32),
                   jax.ShapeDtypeStruct((1, ATTN_WIDTH), F32)],
        compiler_params=_params(("arbitrary",)),
    )(dh1b, w_out, y_attn, gattn, *more)


def _attn_bwd(qn, kn, v, dy, probs, psinks, bkt, after=None):
    n_b = SEQ // BLK

    def body(q_ref, k_ref, v_ref, dy_ref, p_ref, ps_ref, bkt_ref,
             dq_ref, dk_ref, dv_ref, dtbl_ref, dsink_ref, dbias_ref, dsacc_ref):
        step = pl.program_id(0)

        @pl.when(step == 0)
        def _():
            dbias_ref[...] = jnp.zeros_like(dbias_ref)
            dsacc_ref[...] = jnp.zeros_like(dsacc_ref)
            dk_ref[...] = jnp.zeros_like(dk_ref)
            dv_ref[...] = jnp.zeros_like(dv_ref)

        lane = lax.broadcasted_iota(jnp.int32, (BLK, 128), 1)
        for blk in range(QUERY_BLOCKS_PER_STEP):
            i = QUERY_BLOCKS_PER_STEP * step + blk
            rows = slice(BLK * blk, BLK * (blk + 1))
            kb, prev, cur = _band_rows(k_ref, i)
            vb, _, _ = _band_rows(v_ref, i)
            upper, _ = _band_masks(i)
            q = q_ref[rows, :]
            dy = dy_ref[rows, :]
            psink = ps_ref[rows, :]
            dsink = jnp.zeros((BLK, 128), F32)
            dqs, dks, dvs = [], [], []
            for g in range(N_HEADS // GQA_GROUP):
                kv = slice(HEAD_DIM * g, HEAD_DIM * (g + 1))
                qg = _stack_heads(q, g)
                dog = _stack_heads(dy, g).astype(BF16)
                pb = p_ref[blk, g]
                pg = pb.astype(F32)
                dprobs = _fold(_dot_nt(dog, vb[:, kv]), upper)
                dvs.append(_dot_tn(_unfold(pb, upper), dog))
                dsum = jnp.sum(pg * dprobs, axis=-1, keepdims=True)
                dlogits = pg * (dprobs - dsum)
                for t in range(GQA_GROUP):
                    dsink = jnp.where(lane == GQA_GROUP * g + t, -psink * dsum[BLK * t:BLK * (t + 1)], dsink)
                dbias_ref[g] += dlogits
                ds = _unfold(dlogits * (HEAD_DIM ** -0.5), upper).astype(BF16)
                dqs.append(_dot(ds, kb[:, kv]))
                dks.append(_dot_tn(ds, qg))
            dsacc_ref[...] += dsink
            dq_ref[rows, :] = _unstack_heads(dqs)
            dkb = jnp.concatenate(dks, axis=-1)
            dvb = jnp.concatenate(dvs, axis=-1)
            dk_ref[pl.ds(prev, BLK), :] += dkb[:BLK]
            dk_ref[pl.ds(cur, BLK), :] += dkb[BLK:]
            dv_ref[pl.ds(prev, BLK), :] += dvb[:BLK]
            dv_ref[pl.ds(cur, BLK), :] += dvb[BLK:]

        @pl.when(step == n_b // QUERY_BLOCKS_PER_STEP - 1)
        def _():
            bkt = bkt_ref[...]
            row8 = lax.broadcasted_iota(jnp.int32, (N_HEADS, 128), 0)
            lane8 = lax.broadcasted_iota(jnp.int32, (N_HEADS, 128), 1)
            acc = jnp.zeros((N_HEADS, 128), F32)
            for h in range(N_HEADS):
                rows = slice(BLK * (h % GQA_GROUP), BLK * (h % GQA_GROUP + 1))
                dbh = dbias_ref[h // GQA_GROUP, rows, :]
                for b in range(NUM_BUCKETS):
                    acc = jnp.where((row8 == h) & (lane8 == b), jnp.sum(jnp.where(bkt == b, dbh, 0.0)), acc)
            dsink_ref[...] = jnp.sum(dsacc_ref[...], axis=0, keepdims=True)
            dtbl_ref[...] = acc

    const = lambda shape: pl.BlockSpec(shape, lambda i: (0,) * len(shape))
    rows = lambda w: pl.BlockSpec((QUERY_BLOCKS_PER_STEP * BLK, w), lambda i: (i, 0))
    n_g = N_HEADS // GQA_GROUP
    body, more_specs, more = _ordered_behind(body, 7, after)
    return pl.pallas_call(
        body, name="attn_bwd", grid=(n_b // QUERY_BLOCKS_PER_STEP,),
        in_specs=[rows(ATTN_WIDTH), const((SEQ, KV_WIDTH)), const((SEQ, KV_WIDTH)), rows(ATTN_WIDTH),
                  pl.BlockSpec((QUERY_BLOCKS_PER_STEP, n_g, GROUP_ROWS, BLK), lambda i: (i, 0, 0, 0)), rows(128),
                  const((BLK, BLK))] + more_specs,
        out_specs=[rows(ATTN_WIDTH), const((SEQ, KV_WIDTH)), const((SEQ, KV_WIDTH)), const((N_HEADS, 128)), const((1, 128))],
        out_shape=[jax.ShapeDtypeStruct((SEQ, ATTN_WIDTH), F32), jax.ShapeDtypeStruct((SEQ, KV_WIDTH), F32),
                   jax.ShapeDtypeStruct((SEQ, KV_WIDTH), F32), jax.ShapeDtypeStruct((N_HEADS, 128), F32),
                   jax.ShapeDtypeStruct((1, 128), F32)],
        scratch_shapes=[pltpu.VMEM((n_g, GROUP_ROWS, BLK), F32), pltpu.VMEM((BLK, 128), F32)],
        compiler_params=_params(("arbitrary",)),
    )(qn, kn, v, dy, probs, psinks, bkt, *more)


def _mix_in_bwd(x, dh1, proj, dycn, dqn, dkn, dv, w_in_t, conv_w, g1, gq, gk, gconv):
    tm = 512
    n_t = SEQ // tm
    halo_blocks = tm // SUBLANES

    def body(x_ref, dh1_ref, proj_ref, halo_ref, dycn_ref, dqn_ref, dkn_ref, dv_ref, w_ref, cw_ref,
             g1_ref, gq_ref, gk_ref, gc_ref,
             dx_ref, dproj_ref, dcw_ref, dgc_ref, dgq_ref, dgk_ref, dg1_ref, next_ref):
        i = pl.program_id(0)
        first_tile = i == n_t - 1

        @pl.when(i == 0)
        def _():
            for r in (dcw_ref, dgc_ref, dgq_ref, dgk_ref, dg1_ref, next_ref):
                r[...] = jnp.zeros_like(r)

        proj = proj_ref[...]
        hp = halo_ref[...]
        gate_b = proj[:, 0:CONV_WIDTH]
        gate_c = proj[:, CONV_WIDTH:2 * CONV_WIDTH]
        hc = proj[:, 2 * CONV_WIDTH:3 * CONV_WIDTH]
        a = gate_c * hc
        a_halo = jnp.where(first_tile, 0.0, hp[:, CONV_WIDTH:2 * CONV_WIDTH] * hp[:, 2 * CONV_WIDTH:3 * CONV_WIDTH])
        cw = _taps(cw_ref[...])
        cv, a2, a1 = _conv3(a, cw, a_halo)
        dyc, dgc = _rms_bwd(gate_b * cv, gc_ref[...], dycn_ref[...])
        dgc_ref[...] += dgc
        dcv = dyc * gate_b
        dcw_ref[...] += jnp.concatenate(
            [jnp.sum(dcv * a2, axis=0, keepdims=True), jnp.sum(dcv * a1, axis=0, keepdims=True),
             jnp.sum(dcv * a, axis=0, keepdims=True)], axis=0)
        da = _conv3_bwd_input(dcv, cw, next_ref[...])
        next_ref[...] = dcv[:SUBLANES]
        q0 = 3 * CONV_WIDTH
        k0 = q0 + ATTN_WIDTH
        dq, dgq = _head_norm_bwd(proj[:, q0:k0], gq_ref[...], dqn_ref[...], N_HEADS)
        dk, dgk = _head_norm_bwd(proj[:, k0:k0 + KV_WIDTH], gk_ref[...], dkn_ref[...], 2)
        dgq_ref[...] += dgq
        dgk_ref[...] += dgk
        dproj = jnp.concatenate([dyc * cv, da * hc, da * gate_c, dq, dk, dv_ref[...]], axis=-1).astype(BF16)
        dproj_ref[...] = dproj
        du1 = _dot(dproj, w_ref[...])
        xv = x_ref[...]
        dn, dg1 = _rms_bwd(xv, g1_ref[...], du1)
        dx_ref[...] = dh1_ref[...] + dn
        dg1_ref[...] += dg1

    rev = lambda i: n_t - 1 - i
    rows = lambda w: pl.BlockSpec((tm, w), lambda i: (rev(i), 0))
    const = lambda shape: pl.BlockSpec(shape, lambda i: (0,) * len(shape))
    halo = pl.BlockSpec((SUBLANES, IN_WIDTH), lambda i: (jnp.maximum(rev(i) * halo_blocks - 1, 0), 0))
    return pl.pallas_call(
        body, name="mix_in_bwd", grid=(n_t,),
        in_specs=[rows(D_MODEL), rows(D_MODEL), rows(IN_WIDTH), halo, rows(CONV_WIDTH), rows(ATTN_WIDTH), rows(KV_WIDTH),
                  rows(KV_WIDTH), const((IN_WIDTH, D_MODEL)), const((3, CONV_WIDTH)), const((1, D_MODEL)),
                  const((1, HEAD_DIM)), const((1, HEAD_DIM)), const((1, CONV_WIDTH))],
        out_specs=[rows(D_MODEL), rows(IN_WIDTH), const((3, CONV_WIDTH)), const((1, CONV_WIDTH)),
                   const((1, HEAD_DIM)), const((1, HEAD_DIM)), const((1, D_MODEL))],
        out_shape=[jax.ShapeDtypeStruct((SEQ, D_MODEL), F32), jax.ShapeDtypeStruct((SEQ, IN_WIDTH), BF16),
                   jax.ShapeDtypeStruct((3, CONV_WIDTH), F32),
                   jax.ShapeDtypeStruct((1, CONV_WIDTH), F32), jax.ShapeDtypeStruct((1, HEAD_DIM), F32),
                   jax.ShapeDtypeStruct((1, HEAD_DIM), F32), jax.ShapeDtypeStruct((1, D_MODEL), F32)],
        scratch_shapes=[pltpu.VMEM((SUBLANES, CONV_WIDTH), F32)],
        compiler_params=_params(("arbitrary",)),
    )(x, dh1, proj, proj, dycn, dqn, dkn, dv, w_in_t, conv_w, g1, gq, gk, gconv)


def _grad_w_in(dproj, u1, after=None):
    bw = 768

    def body(a_ref, b_ref, o_ref):
        o_ref[...] = _dot_tn(a_ref[...], b_ref[...]).astype(BF16)

    body, more_specs, more = _ordered_behind(body, 2, after)
    return pl.pallas_call(
        body, name="grad_w_in", grid=(IN_WIDTH // bw,),
        in_specs=[pl.BlockSpec((SEQ, bw), lambda k: (0, k)), pl.BlockSpec((SEQ, D_MODEL), lambda k: (0, 0))] + more_specs,
        out_specs=pl.BlockSpec((bw, D_MODEL), lambda k: (k, 0)),
        out_shape=jax.ShapeDtypeStruct((IN_WIDTH, D_MODEL), BF16),
        compiler_params=_params(("arbitrary",)),
    )(dproj, u1, *more)


def _adamw_math(w, g, m, v):
    m = ADAM_B1 * m + (1.0 - ADAM_B1) * g
    v = ADAM_B2 * v + (1.0 - ADAM_B2) * (g * g)
    m_hat = m / (1.0 - ADAM_B1 ** ADAM_STEP)
    v_hat = v / (1.0 - ADAM_B2 ** ADAM_STEP)
    return -ADAM_LR * (m_hat / (jnp.sqrt(v_hat) + ADAM_EPS) + ADAM_WD * w), m, v


_ROW_G1, _ROW_G2, _ROW_OUT_NORMS, _ROW_FFN_B, _ROW_GQ, _ROW_GK, _ROW_SINKS, _ROW_LOSS, _ROW_TABLE = 0, 1, 2, 3, 11, 12, 13, 14, 16
SMALL_ROWS, SMALL_COLS = 24, 1024
_SMALL_NAMES = ("norm_mix_g", "norm_ffn_g", "out_norm_conv_g", "out_norm_attn_g", "ffn_conv_b", "q_norm_g", "k_norm_g",
                "sinks", "rel_bias_table")


def _pack_small_grads(dg1, dg2, dgconv, dgattn, dfb, dgq, dgk, dsinks, dtbl_t, loss_acc):
    def body(dg1_ref, dg2_ref, dgc_ref, dga_ref, dfb_ref, dgq_ref, dgk_ref, ds_ref, dt_ref, loss_ref, o_ref, all_ref):
        o_ref[...] = jnp.zeros_like(o_ref)
        o_ref[_ROW_G1:_ROW_G1 + 1, :] = dg1_ref[...]
        o_ref[_ROW_G2:_ROW_G2 + 1, :] = dg2_ref[...]
        o_ref[_ROW_OUT_NORMS:_ROW_OUT_NORMS + 1, 0:CONV_WIDTH] = dgc_ref[...]
        o_ref[_ROW_OUT_NORMS:_ROW_OUT_NORMS + 1, CONV_WIDTH:] = dga_ref[...]
        for k in range(N_DEV):
            o_ref[_ROW_FFN_B + k:_ROW_FFN_B + k + 1, 0:FFN_BLK] = dfb_ref[k // N_FFN_BLK, k % N_FFN_BLK]
        o_ref[_ROW_GQ:_ROW_GQ + 1, 0:HEAD_DIM] = dgq_ref[...]
        o_ref[_ROW_GK:_ROW_GK + 1, 0:HEAD_DIM] = dgk_ref[...]
        o_ref[_ROW_SINKS:_ROW_SINKS + 1, 0:128] = ds_ref[...]
        o_ref[_ROW_LOSS:_ROW_LOSS + 1, 0:128] = loss_ref[0:1, :]
        o_ref[_ROW_TABLE:_ROW_TABLE + N_HEADS, 0:128] = dt_ref[...]
        for s in range(N_DEV):
            all_ref[s] = o_ref[...]

    return pl.pallas_call(
        body, name="pack_small_grads",
        out_shape=[jax.ShapeDtypeStruct((SMALL_ROWS, SMALL_COLS), F32),
                   jax.ShapeDtypeStruct((N_DEV, SMALL_ROWS, SMALL_COLS), F32)],
    )(dg1, dg2, dgconv, dgattn, dfb, dgq, dgk, dsinks, dtbl_t, loss_acc)


def _adamw_small(recv, params, after):
    names = _SMALL_NAMES
    n = len(names)

    def grad_of(g, name, k=None):
        if name == "norm_mix_g":
            return g[_ROW_G1:_ROW_G1 + 1, :]
        if name == "norm_ffn_g":
            return g[_ROW_G2:_ROW_G2 + 1, :]
        if name == "out_norm_conv_g":
            return g[_ROW_OUT_NORMS:_ROW_OUT_NORMS + 1, 0:CONV_WIDTH]
        if name == "out_norm_attn_g":
            return g[_ROW_OUT_NORMS:_ROW_OUT_NORMS + 1, CONV_WIDTH:]
        if name == "ffn_conv_b":
            return g[_ROW_FFN_B + k:_ROW_FFN_B + k + 1, 0:FFN_BLK]
        if name == "q_norm_g":
            return g[_ROW_GQ:_ROW_GQ + 1, 0:HEAD_DIM]
        if name == "k_norm_g":
            return g[_ROW_GK:_ROW_GK + 1, 0:HEAD_DIM]
        if name == "sinks":
            return g[_ROW_SINKS:_ROW_SINKS + 1, 0:N_HEADS]
        return g[_ROW_TABLE:_ROW_TABLE + N_HEADS, 0:NUM_BUCKETS]

    def body(r_ref, *refs):
        ins, outs, loss_ref = refs[:3 * n], refs[3 * n:7 * n], refs[7 * n]
        g = r_ref[0]
        for s in range(1, N_DEV):
            g = g + r_ref[s]
        loss_ref[...] = g[_ROW_LOSS:_ROW_LOSS + 1, 0:128]
        for i, name in enumerate(names):
            w_ref, m_ref, v_ref = ins[3 * i:3 * i + 3]
            o = outs[4 * i:4 * i + 4]
            cols = [slice(FFN_BLK * k, FFN_BLK * (k + 1)) for k in range(N_DEV)] if name == "ffn_conv_b" else [slice(None)]
            for k, cs in enumerate(cols):
                gk = grad_of(g, name, k)
                d, m2, v2 = _adamw_math(w_ref[:, cs], gk, m_ref[:, cs], v_ref[:, cs])
                o[0][:, cs], o[1][:, cs], o[2][:, cs], o[3][:, cs] = gk, d, m2, v2

    flat = [a for name in names for a in params[name]]
    body, more_specs, more = _ordered_behind(body, 1 + 3 * n, after)
    vmem = pl.BlockSpec(memory_space=pltpu.VMEM)
    out = pl.pallas_call(
        body, name="adamw_small",
        in_specs=[vmem] * (1 + 3 * n) + more_specs,
        out_shape=[jax.ShapeDtypeStruct(params[name][0].shape, F32) for name in names for _ in range(4)]
        + [jax.ShapeDtypeStruct((1, 128), F32)],
        compiler_params=pltpu.CompilerParams(vmem_limit_bytes=VMEM_LIMIT),
    )(recv, *flat, *more)
    return {name: tuple(out[4 * i:4 * i + 4]) for i, name in enumerate(names)}, out[4 * n]


def _adamw_direct(w, m, v, own, recv, me, name, row_blocks=1, after=None):
    rb = w.shape[0] // row_blocks
    cols = w.shape[1]

    def body(me_ref, w_ref, m_ref, v_ref, o_ref, r_ref, g_o, d_o, m_o, v_o):
        g = o_ref[...].astype(F32)
        for s in range(N_DEV - 1):
            g = g + r_ref[s].astype(F32)
        g_o[...] = g
        d_o[...], m_o[...], v_o[...] = _adamw_math(w_ref[...], g, m_ref[...], v_ref[...])

    blk = pl.BlockSpec((rb, cols), lambda i, me_ref: (i, 0))
    oblk = pl.BlockSpec((None, rb, cols), lambda i, me_ref: (me_ref[0], i, 0))
    rblk = pl.BlockSpec((N_DEV - 1, rb, cols), lambda i, me_ref: (0, i, 0))
    body, more_specs, more = _ordered_behind(body, 6, after)
    return pl.pallas_call(
        body, name=name,
        grid_spec=pltpu.PrefetchScalarGridSpec(num_scalar_prefetch=1, grid=(row_blocks,),
                                               in_specs=[blk, blk, blk, oblk, rblk] + more_specs, out_specs=[blk] * 4),
        out_shape=[jax.ShapeDtypeStruct(w.shape, F32)] * 4,
        compiler_params=_params(("arbitrary",)),
    )(me, w, m, v, own, recv, *more)


def _adamw(w, m, v, part, recv, chip, name, row_blocks=1, after=None):
    rb = w.shape[0] // row_blocks
    tail = w.shape[1:]
    zeros = (0,) * len(tail)

    def body(chip_ref, w_ref, m_ref, v_ref, p_ref, r_ref, g_o, d_o, m_o, v_o):
        g = p_ref[...].astype(F32)
        for s in range(3):
            g = g + r_ref[s].astype(F32)
        g_o[...] = g
        d_o[...], m_o[...], v_o[...] = _adamw_math(w_ref[...], g, m_ref[...], v_ref[...])

    blk = pl.BlockSpec((rb,) + tail, lambda i, chip_ref: (i,) + zeros)
    pblk = pl.BlockSpec((None, rb) + tail, lambda i, chip_ref: (chip_ref[0], i) + zeros)
    rblk = pl.BlockSpec((3, rb) + tail, lambda i, chip_ref: (0, i) + zeros)
    body, more_specs, more = _ordered_behind(body, 6, after)
    return pl.pallas_call(
        body, name=name,
        grid_spec=pltpu.PrefetchScalarGridSpec(num_scalar_prefetch=1, grid=(row_blocks,),
                                               in_specs=[blk, blk, blk, pblk, rblk] + more_specs, out_specs=[blk] * 4),
        out_shape=[jax.ShapeDtypeStruct(w.shape, F32)] * 4,
        compiler_params=_params(("arbitrary",)),
    )(chip, w, m, v, part, recv, *more)


def kernel(x, norm_mix_g, w_in, conv_w, q_norm_g, k_norm_g, rel_bias_table, sinks, out_norm_conv_g, out_norm_attn_g, w_out, norm_ffn_g, w_up, ffn_conv_w, ffn_conv_b, w_down, loss_target, m_norm_mix_g, m_w_in, m_conv_w, m_q_norm_g, m_k_norm_g, m_rel_bias_table, m_sinks, m_out_norm_conv_g, m_out_norm_attn_g, m_w_out, m_norm_ffn_g, m_w_up, m_ffn_conv_w, m_ffn_conv_b, m_w_down, v_norm_mix_g, v_w_in, v_conv_w, v_q_norm_g, v_k_norm_g, v_rel_bias_table, v_sinks, v_out_norm_conv_g, v_out_norm_attn_g, v_w_out, v_norm_ffn_g, v_w_up, v_ffn_conv_w, v_ffn_conv_b, v_w_down):
    p = dict(norm_mix_g=norm_mix_g, w_in=w_in, conv_w=conv_w, q_norm_g=q_norm_g, k_norm_g=k_norm_g,
             rel_bias_table=rel_bias_table, sinks=sinks, out_norm_conv_g=out_norm_conv_g, out_norm_attn_g=out_norm_attn_g,
             w_out=w_out, norm_ffn_g=norm_ffn_g, w_up=w_up, ffn_conv_w=ffn_conv_w, ffn_conv_b=ffn_conv_b, w_down=w_down)
    m = dict(norm_mix_g=m_norm_mix_g, w_in=m_w_in, conv_w=m_conv_w, q_norm_g=m_q_norm_g, k_norm_g=m_k_norm_g,
             rel_bias_table=m_rel_bias_table, sinks=m_sinks, out_norm_conv_g=m_out_norm_conv_g,
             out_norm_attn_g=m_out_norm_attn_g, w_out=m_w_out, norm_ffn_g=m_norm_ffn_g, w_up=m_w_up,
             ffn_conv_w=m_ffn_conv_w, ffn_conv_b=m_ffn_conv_b, w_down=m_w_down)
    v = dict(norm_mix_g=v_norm_mix_g, w_in=v_w_in, conv_w=v_conv_w, q_norm_g=v_q_norm_g, k_norm_g=v_k_norm_g,
             rel_bias_table=v_rel_bias_table, sinks=v_sinks, out_norm_conv_g=v_out_norm_conv_g,
             out_norm_attn_g=v_out_norm_attn_g, w_out=v_w_out, norm_ffn_g=v_norm_ffn_g, w_up=v_w_up,
             ffn_conv_w=v_ffn_conv_w, ffn_conv_b=v_ffn_conv_b, w_down=v_w_down)

    xs, tgt = x[0], loss_target[0]
    g1, g2, gq, gk, gconv, gattn = norm_mix_g, norm_ffn_g, q_norm_g, k_norm_g, out_norm_conv_g, out_norm_attn_g
    ix, iy, ic = _coords()
    core = ic.astype(jnp.int32).reshape(1)
    chip = (2 * ix + iy).astype(jnp.int32).reshape(1)
    me = _lin(ix, iy, ic).astype(jnp.int32).reshape(1)
    bkt = jnp.asarray(_bucket_map())
    tr = lambda a: a[0].T
    taps = lambda a: jnp.transpose(a, (1, 0, 2))
    tbl_t = rel_bias_table.T

    wi_l, cw_l = _place_shards(me, [tr(w_in), taps(conv_w)], [BF16, F32], "place_mixer_shards")
    finish_a, token_a = _all_gather_split([wi_l, cw_l], "mixer", None)
    wo_l, wu_l, wd_l, fcw_l = _place_shards(me, [w_out[0], tr(w_up), w_down[0], taps(ffn_conv_w)],
                                            [BF16, BF16, BF16, F32], "place_ffn_shards", after=token_a)
    ffn_stage2, ffn_stage3, token_b = _all_gather_tree([wo_l, wu_l, wd_l, fcw_l], "ffn", token_a)
    wi_g, cw_g = finish_a(token_b)
    w_in_t = wi_g.reshape(IN_WIDTH, D_MODEL)
    conv_w_f = jnp.transpose(cw_g[:, :, 0, :], (1, 0, 2)).reshape(3, CONV_WIDTH)

    proj, u1, ycn, qn, kn, vv = _mix_in_fwd(xs, g1, w_in_t, conv_w_f, gq, gk, gconv)
    token_b2 = ffn_stage2(ycn)
    y_attn, yan, probs, psinks = _attn_fwd(qn, kn, vv, tbl_t, sinks, bkt, gattn, after=token_b2)
    wo_g, wu_g, wd_g, fcw_g = ffn_stage3(yan)
    w_out_f = wo_g.reshape(D_MODEL, D_MODEL)
    w_down_f = wd_g.reshape(N_FFN_BLK, FFN_BLK, D_MODEL)
    w_up_f = wu_g.reshape(2, N_FFN_BLK, FFN_BLK, D_MODEL)
    fcw_f = fcw_g.reshape(2, N_FFN_BLK, 3, 1, FFN_BLK)
    fcb = ffn_conv_b.reshape(2, N_FFN_BLK, 1, FFN_BLK)
    h1, u2, up, pre, act, dh2, dh2b, loss_acc = _ffn_fwd(xs, ycn, yan, w_out_f, g2, w_up_f, fcw_f, fcb, w_down_f, tgt)

    dw_down = _grad_tn_blocked(act, dh2b, "grad_w_down").reshape(N_DEV, D_FF // N_DEV, D_MODEL)
    plan_d, slots_d = _scatter_plan(1)
    d_sem = _split_start("scatter_w_down_start", [dw_down], [lax.empty((N_DEV - 1,) + dw_down.shape[1:], BF16)],
                         plan_d, None, _ALL_FOR_W_DOWN)
    dup, dh1, dh1b, dfb, dfcw, dg2 = _ffn_bwd(dh2, dh2b, h1, g2, up, pre, w_up_f, fcw_f, w_down_f, after=d_sem[4])
    dw_up = _grad_tn_blocked(dup.reshape(N_DEV, SEQ, FFN_BLK), u2, "grad_w_up")
    dw_out = _grad_tn([ycn, yan], dh1b, CONV_WIDTH, "grad_w_out").reshape(N_DEV, D_MODEL // N_DEV, D_MODEL)
    out_bwd = {}

    def behind_ffn(token):
        out_bwd["r"] = _out_bwd(dh1b, w_out_f, y_attn, gattn, after=token)
        return out_bwd["r"][0]

    finish_ffn, token_ffn = _reduce_scatter_split(
        [dw_up, dw_out, dfcw.reshape(N_DEV, 3, 1, FFN_BLK)], "ffn", core, behind_ffn)
    dycn, dy_attn, dgattn = out_bwd["r"]
    dqn, dkn, dv, dtbl_t, dsinks = _attn_bwd(qn, kn, vv, dy_attn, probs, psinks, bkt, after=token_ffn)
    dx, dproj, dcw, dgconv, dgq, dgk, dg1 = _mix_in_bwd(xs, dh1, proj, dycn, dqn, dkn, dv, w_in_t, conv_w_f,
                                                         g1, gq, gk, gconv)
    packed, packed_all = _pack_small_grads(dg1, dg2, dgconv, dgattn, dfb, dgq, dgk, dsinks, dtbl_t, loss_acc)
    plan_s, slots_s = _broadcast_plan()
    s_sem, r_sem, src_s, land_s, token_s = _split_start("gather_small_start", [packed], [packed_all], plan_s, None,
                                                        _ALL_FOR_SMALL)
    dw_in_t = _grad_w_in(dproj, u1, after=token_s).reshape(N_DEV, IN_WIDTH // N_DEV, D_MODEL)
    dcw_b = jnp.transpose(dcw.reshape(3, N_DEV, 1, CONV_WIDTH // N_DEV), (1, 0, 2, 3))
    adam = {}
    ffn_got = {}

    def behind_mixer(token):
        ffn_got["r"] = finish_ffn(token)
        return ffn_got["r"][1][0]

    finish_mixer, token_mixer = _reduce_scatter_split([dw_in_t, dcw_b], "mixer", core, behind_mixer)
    (p_wu, p_wo, p_fcw), (r_wu, r_wo, r_fcw) = ffn_got["r"]
    (own_wd,), (r_wd,) = _split_wait("scatter_w_down_wait", d_sem[0], d_sem[1], d_sem[2], d_sem[3], plan_d, slots_d,
                                     token_mixer)
    adam["w_down"] = _adamw_direct(w_down[0], m_w_down[0], v_w_down[0], own_wd, r_wd, me, "adamw_w_down", row_blocks=2)
    adam_up = _adamw(tr(w_up), tr(m_w_up), tr(v_w_up), p_wu, r_wu, chip, "adamw_w_up", row_blocks=4,
                     after=adam["w_down"][0])
    adam["w_out"] = _adamw(w_out[0], m_w_out[0], v_w_out[0], p_wo, r_wo, chip, "adamw_w_out", after=adam_up[0])
    adam_fcw = _adamw(taps(ffn_conv_w), taps(m_ffn_conv_w), taps(v_ffn_conv_w), p_fcw, r_fcw, chip, "adamw_ffn_conv_w",
                      after=adam["w_out"][0])
    _, (r_small,) = _split_wait("gather_small_wait", s_sem, r_sem, src_s, land_s, plan_s, slots_s, adam_fcw[0])
    small_in = {k: (p[k], m[k], v[k]) for k in _SMALL_NAMES}
    small_in["rel_bias_table"] = (tbl_t, m_rel_bias_table.T, v_rel_bias_table.T)
    small_out, loss_row = _adamw_small(r_small, small_in, None)
    (p_wi, p_cw), (r_wi, r_cw) = finish_mixer(loss_row)
    adam_in = _adamw(tr(w_in), tr(m_w_in), tr(v_w_in), p_wi, r_wi, chip, "adamw_w_in")
    adam_cw = _adamw(taps(conv_w), taps(m_conv_w), taps(v_conv_w), p_cw, r_cw, chip, "adamw_conv_w")

    res = {k: tuple(a[None] for a in t) for k, t in adam.items()}
    res["w_up"] = tuple(a.T[None] for a in adam_up)
    res["w_in"] = tuple(a.T[None] for a in adam_in)
    res["ffn_conv_w"] = tuple(taps(a) for a in adam_fcw)
    res["conv_w"] = tuple(taps(a) for a in adam_cw)
    res.update(small_out)
    res["rel_bias_table"] = tuple(a.T for a in small_out["rel_bias_table"])
    loss = loss_row[0, 0]
    order = ("norm_mix_g", "w_in", "conv_w", "q_norm_g", "k_norm_g", "rel_bias_table", "sinks", "out_norm_conv_g",
             "out_norm_attn_g", "w_out", "norm_ffn_g", "w_up", "ffn_conv_w", "ffn_conv_b", "w_down")
    return (loss, dx[None], *[res[k][0] for k in order], *[res[k][1] for k in order],
            *[res[k][2] for k in order], *[res[k][3] for k in order])
```

```python
import math

import numpy as np
import jax
import jax.numpy as jnp
from jax import lax
from jax.experimental import pallas as pl
from jax.experimental.pallas import tpu as pltpu

F32 = jnp.float32
BF16 = jnp.bfloat16

SEQ = 2048
D_MODEL = 1024
CONV_WIDTH = 512
ATTN_WIDTH = 512
KV_WIDTH = 128
HEAD_DIM = 64
N_HEADS = 8
GQA_GROUP = 4
IN_WIDTH = 2304
D_FF = 2816
BLK = 128
NUM_BUCKETS = 32
EPS = 1e-6
NEG_INF = -1e30
ADAM_LR = 0.001
ADAM_B1 = 0.9
ADAM_B2 = 0.999
ADAM_EPS = 1e-08
ADAM_WD = 0.01
ADAM_STEP = 10

N_DEV = 8
FFN_BLK = 2 * D_FF // N_DEV
N_FFN_BLK = D_FF // FFN_BLK
SUBLANES = 8
VMEM_LIMIT = 56 * 1024 * 1024

_MESH = pl.DeviceIdType.MESH
_ANY = pl.BlockSpec(memory_space=pl.ANY)


def _params(sem):
    return pltpu.CompilerParams(dimension_semantics=sem, vmem_limit_bytes=VMEM_LIMIT)


def _ordered_behind(body, pos, after):
    if after is None:
        return body, [], []
    return (lambda *refs: body(*refs[:pos], *refs[pos + 1:])), [_ANY], [after]


def _dot(a, b):
    return jnp.dot(a, b, preferred_element_type=F32)


def _dot_nt(a, b):
    return lax.dot_general(a, b, (((1,), (1,)), ((), ())), preferred_element_type=F32)


def _dot_tn(a, b):
    return lax.dot_general(a, b, (((0,), (0,)), ((), ())), preferred_element_type=F32)


def _shift_down(x, s, halo):
    r = pltpu.roll(x, s, axis=0)
    hr = pltpu.roll(halo, s, axis=0)
    row = lax.broadcasted_iota(jnp.int32, halo.shape, 0)
    top = jnp.where(row < s, hr, r[:SUBLANES])
    return jnp.concatenate([top, r[SUBLANES:]], axis=0)


def _shift_up(x, s, halo):
    n = x.shape[0]
    r = pltpu.roll(x, n - s, axis=0)
    hr = pltpu.roll(halo, SUBLANES - s, axis=0)
    row = lax.broadcasted_iota(jnp.int32, halo.shape, 0)
    bot = jnp.where(row >= SUBLANES - s, hr, r[n - SUBLANES:])
    return jnp.concatenate([r[:n - SUBLANES], bot], axis=0)


def _taps(w):
    return (w[0], w[1], w[2]) if len(w.shape) == 3 else (w[0:1], w[1:2], w[2:3])


def _conv3(x, w, halo):
    x2 = _shift_down(x, 2, halo)
    x1 = _shift_down(x, 1, halo)
    return x2 * w[0] + x1 * w[1] + x * w[2], x2, x1


def _conv3_bwd_input(dy, w, halo_next):
    return dy * w[2] + _shift_up(dy, 1, halo_next) * w[1] + _shift_up(dy, 2, halo_next) * w[0]


def _rstd(x):
    return lax.rsqrt(jnp.mean(x * x, axis=-1, keepdims=True) + EPS)


def _rms_bwd(x, g, dy):
    r = _rstd(x)
    n = x * r
    dn = dy * g
    dx = r * (dn - n * jnp.mean(dn * n, axis=-1, keepdims=True))
    return dx, jnp.sum(dy * n, axis=0, keepdims=True)


def _head_mean(x):
    width = x.shape[-1]
    ri = lax.broadcasted_iota(jnp.int32, (width, width), 0) // HEAD_DIM
    ci = lax.broadcasted_iota(jnp.int32, (width, width), 1) // HEAD_DIM
    ones = jnp.where(ri == ci, 1.0, 0.0).astype(BF16)
    hi = x.astype(BF16)
    lo = (x - hi.astype(F32)).astype(BF16)
    return (_dot(hi, ones) + _dot(lo, ones)) * (1.0 / HEAD_DIM)


def _head_norm(x, g, heads):
    return x * lax.rsqrt(_head_mean(x * x) + EPS) * jnp.tile(g, (1, heads))


def _head_norm_bwd(x, g, dy, heads):
    r = lax.rsqrt(_head_mean(x * x) + EPS)
    n = x * r
    dn = dy * jnp.tile(g, (1, heads))
    dx = r * (dn - n * _head_mean(dn * n))
    per_lane = jnp.sum(dy * n, axis=0, keepdims=True)
    dg = per_lane[:, 0:HEAD_DIM]
    for h in range(1, heads):
        dg = dg + per_lane[:, HEAD_DIM * h:HEAD_DIM * (h + 1)]
    return dx, dg


def _bucket_map():
    q = np.arange(BLK)[:, None]
    j = np.arange(BLK)[None, :]
    n = np.where(j > q, q + BLK - j, q - j)
    nf = np.maximum(n, 1).astype(np.float32)
    max_exact = NUM_BUCKETS // 2
    large = max_exact + (np.log(nf / max_exact) / math.log(BLK / max_exact) * (NUM_BUCKETS - max_exact)).astype(np.int32)
    large = np.minimum(large, NUM_BUCKETS - 1)
    return np.where(n < max_exact, n, large).astype(np.int32)


def _coords():
    return lax.axis_index("x"), lax.axis_index("y"), lax.axis_index("c")


def _lin(px, py, pc):
    return 4 * px + 2 * py + pc


def _chips(x, y):
    return [(1 - x, y), (x, 1 - y), (1 - x, 1 - y)]


def _peers(x, y, c):
    return [(1 - x if r & 4 else x, 1 - y if r & 2 else y, 1 - c if r & 1 else c) for r in range(1, N_DEV)]


_HBM = pl.BlockSpec(memory_space=pltpu.HBM)
_SEM = pl.BlockSpec(memory_space=pltpu.SEMAPHORE)
_EFFECT = pltpu.SideEffectType.DATAFLOW_SIDE_EFFECTING


def _in_hbm(a):
    return pltpu.with_memory_space_constraint(a, pltpu.HBM)


_SIBLING = (1, lambda x, y, c: [(x, y, 1 - c)])
_SIBLING_AND_CHIPS = (2, lambda x, y, c: [(x, y, 1 - c)] + [(cx, cy, c) for cx, cy in _chips(x, y)])
_SIBLING_AND_NEIGHBOURS = (3, lambda x, y, c: [(x, y, 1 - c), (1 - x, y, c), (x, 1 - y, c)])
_ONWARD_AND_SIBLING = (4, lambda x, y, c: [(jnp.where(c == 1, x, 1 - x), jnp.where(c == 1, 1 - y, y), c), (x, y, 1 - c)])
_ALL_FOR_W_DOWN = (5, lambda x, y, c: _peers(x, y, c))
_CHIPS = (6, lambda x, y, c: [(cx, cy, c) for cx, cy in _chips(x, y)])
_ALL_FOR_SMALL = (7, lambda x, y, c: _peers(x, y, c))


def _split_start(name, srcs, lands, plan, after, handshake):
    ns, nl = len(srcs), len(lands)
    n_copies = len(plan(0, 0, 0))
    n_after = 0 if after is None else 1
    collective_id, peers_of = handshake

    def body(*refs):
        src_refs, land_refs = refs[:ns + nl], refs[ns:ns + nl]
        send_sems, recv_sems = refs[ns + nl + n_after], refs[ns + nl + n_after + 1]
        token = refs[-1]
        barrier = pltpu.get_barrier_semaphore()
        peers = peers_of(*_coords())
        for peer in peers:
            pl.semaphore_signal(barrier, inc=1, device_id=peer, device_id_type=_MESH)
        pl.semaphore_wait(barrier, len(peers))
        for k, (a, s_slot, l, d_slot, dev) in enumerate(plan(*_coords())):
            src = src_refs[a] if s_slot is None else src_refs[a].at[s_slot]
            pltpu.make_async_remote_copy(src_ref=src, dst_ref=land_refs[l].at[d_slot], send_sem=send_sems.at[k],
                                         recv_sem=recv_sems.at[k], device_id=dev, device_id_type=_MESH).start()
        token[...] = jnp.zeros_like(token)

    arrs = list(srcs) + list(lands)
    out = pl.pallas_call(
        body, name=name,
        out_shape=(pltpu.SemaphoreType.DMA((n_copies,)), pltpu.SemaphoreType.DMA((n_copies,)),
                   *[pltpu.HBM(a.shape, a.dtype) for a in arrs], jax.ShapeDtypeStruct((SUBLANES, 128), F32)),
        in_specs=[_HBM] * (ns + nl) + [_ANY] * n_after,
        out_specs=(_SEM, _SEM, *[_HBM] * (ns + nl), pl.BlockSpec(memory_space=pltpu.VMEM)),
        input_output_aliases={i: 2 + i for i in range(ns + nl)},
        compiler_params=pltpu.CompilerParams(has_side_effects=_EFFECT, collective_id=collective_id),
    )(*[_in_hbm(a) for a in arrs], *([] if after is None else [after]))
    return out[0], out[1], list(out[2:2 + ns]), list(out[2 + ns:2 + ns + nl]), out[-1]


def _split_wait(name, send_sems, recv_sems, srcs, lands, plan, recv_slots, after):
    ns, nl = len(srcs), len(lands)

    def body(*refs):
        src_refs, land_refs = refs[:ns + nl], refs[ns:ns + nl]
        send_sems, recv_sems = refs[ns + nl], refs[ns + nl + 1]
        coords = _coords()
        slots = recv_slots(*coords)
        for k, (a, s_slot, l, _, dev) in enumerate(plan(*coords)):
            src = src_refs[a] if s_slot is None else src_refs[a].at[s_slot]
            cp = pltpu.make_async_remote_copy(src_ref=src, dst_ref=land_refs[l].at[slots[k]], send_sem=send_sems.at[k],
                                              recv_sem=recv_sems.at[k], device_id=dev, device_id_type=_MESH)
            cp.wait_send()
            cp.wait_recv()

    arrs = list(srcs) + list(lands)
    out = pl.pallas_call(
        body, name=name,
        out_shape=tuple(pltpu.HBM(a.shape, a.dtype) for a in arrs),
        in_specs=[_HBM] * (ns + nl) + [_SEM, _SEM, _ANY],
        out_specs=tuple([_HBM] * (ns + nl)),
        input_output_aliases={i: i for i in range(ns + nl)},
        compiler_params=pltpu.CompilerParams(has_side_effects=_EFFECT),
    )(*arrs, send_sems, recv_sems, after)
    return list(out[:ns]), list(out[ns:])


def _gather_plan_ici(n):
    def plan(x, y, c):
        me = _lin(x, y, c)
        out = []
        for a in range(n):
            out.append((a, me, a, me, (x, y, 1 - c)))
            out += [(a, me, a, me, (cx, cy, c)) for cx, cy in _chips(x, y)]
        return out

    def recv_slots(x, y, c):
        out = []
        for _ in range(n):
            out.append(_lin(x, y, 1 - c))
            out += [_lin(cx, cy, c) for cx, cy in _chips(x, y)]
        return out

    return plan, recv_slots


def _gather_plan_d2d(n):
    def plan(x, y, c):
        return [(a, _lin(cx, cy, c), a, _lin(cx, cy, c), (x, y, 1 - c)) for a in range(n) for cx, cy in _chips(x, y)]

    def recv_slots(x, y, c):
        return [_lin(cx, cy, 1 - c) for _ in range(n) for cx, cy in _chips(x, y)]

    return plan, recv_slots


def _all_gather_split(lands, tag, after):
    n = len(lands)
    plan1, slots1 = _gather_plan_ici(n)
    s1, r1, _, lands, token = _split_start(f"gather_{tag}_ici_start", [], lands, plan1, after, _SIBLING_AND_CHIPS)

    def finish(after):
        _, got = _split_wait(f"gather_{tag}_ici_wait", s1, r1, [], lands, plan1, slots1, after)
        plan2, slots2 = _gather_plan_d2d(n)
        s2, r2, _, got, token2 = _split_start(f"gather_{tag}_d2d_start", [], got, plan2, None, _SIBLING)
        return _split_wait(f"gather_{tag}_d2d_wait", s2, r2, [], got, plan2, slots2, token2)[1]

    return finish, token


def _all_gather_tree(lands, tag, after):
    n = len(lands)

    def plan1(x, y, c):
        me = _lin(x, y, c)
        return [(a, me, a, me, dev) for a in range(n) for dev in ((x, y, 1 - c), (1 - x, y, c), (x, 1 - y, c))]

    def slots1(x, y, c):
        return [s for _ in range(n) for s in (_lin(x, y, 1 - c), _lin(1 - x, y, c), _lin(x, 1 - y, c))]

    def plan2(x, y, c):
        from_x, from_y = _lin(1 - x, y, c), _lin(x, 1 - y, c)
        north = c == 1
        passed = jnp.where(north, from_x, from_y)
        onward = (jnp.where(north, x, 1 - x), jnp.where(north, 1 - y, y), c)
        sib = (x, y, 1 - c)
        return [cp for a in range(n) for cp in ((a, passed, a, passed, onward), (a, from_x, a, from_x, sib),
                                                (a, from_y, a, from_y, sib))]

    def slots2(x, y, c):
        return [s for _ in range(n) for s in (_lin(1 - x, 1 - y, c), _lin(1 - x, y, 1 - c), _lin(x, 1 - y, 1 - c))]

    def plan3(x, y, c):
        diag = _lin(1 - x, 1 - y, c)
        return [(a, diag, a, diag, (x, y, 1 - c)) for a in range(n)]

    def slots3(x, y, c):
        return [_lin(1 - x, 1 - y, 1 - c)] * n

    s1, r1, _, lands, token = _split_start(f"gather_{tag}_1_start", [], lands, plan1, after, _SIBLING_AND_NEIGHBOURS)
    state = {}

    def stage2(after):
        _, got = _split_wait(f"gather_{tag}_1_wait", s1, r1, [], lands, plan1, slots1, after)
        state["s"], state["r"], _, state["lands"], token2 = _split_start(f"gather_{tag}_2_start", [], got, plan2, None,
                                                                         _ONWARD_AND_SIBLING)
        return token2

    def stage3(after):
        _, got = _split_wait(f"gather_{tag}_2_wait", state["s"], state["r"], [], state["lands"], plan2, slots2, after)
        s3, r3, _, got, token3 = _split_start(f"gather_{tag}_3_start", [], got, plan3, None, _SIBLING)
        return _split_wait(f"gather_{tag}_3_wait", s3, r3, [], got, plan3, slots3, token3)[1]

    return stage2, stage3, token


_CHIP_LIST = ((0, 0), (0, 1), (1, 0), (1, 1))


def _reduce_plan_d2d(n):
    def plan(x, y, c):
        return [(a, _lin(qx, qy, 1 - c), a, q, (x, y, 1 - c)) for a in range(n) for q, (qx, qy) in enumerate(_CHIP_LIST)]

    def recv_slots(x, y, c):
        return [q for _ in range(n) for q in range(4)]

    return plan, recv_slots


def _reduce_plan_ici(n):
    def plan(x, y, c):
        return [(a, 2 * cx + cy, a, j, (cx, cy, c)) for a in range(n) for j, (cx, cy) in enumerate(_chips(x, y))]

    def recv_slots(x, y, c):
        return [j for _ in range(n) for j in range(3)]

    return plan, recv_slots


def _scatter_plan(n):
    def plan(x, y, c):
        return [(a, _lin(*peer), a, r, peer) for a in range(n) for r, peer in enumerate(_peers(x, y, c))]

    def recv_slots(x, y, c):
        return [r for _ in range(n) for r in range(N_DEV - 1)]

    return plan, recv_slots


def _broadcast_plan():
    def plan(x, y, c):
        return [(0, None, 0, _lin(x, y, c), peer) for peer in _peers(x, y, c)]

    def recv_slots(x, y, c):
        return [_lin(*peer) for peer in _peers(x, y, c)]

    return plan, recv_slots


def _chip_partial(grads, recvd, core, name):
    n = len(grads)

    def body(c_ref, *refs):
        for a in range(n):
            g_ref, r_ref, o_ref = refs[a], refs[n + a], refs[2 * n + a]
            o_ref[...] = (g_ref[...].astype(F32) + r_ref[...].astype(F32)).astype(o_ref.dtype)

    def blk(a, own):
        zeros = (0,) * (a.ndim - 1)
        return pl.BlockSpec((None,) + a.shape[1:],
                            (lambda q, c_ref: (2 * q + c_ref[0],) + zeros) if own else (lambda q, c_ref: (q,) + zeros))

    return pl.pallas_call(
        body, name=name,
        grid_spec=pltpu.PrefetchScalarGridSpec(
            num_scalar_prefetch=1, grid=(4,),
            in_specs=[blk(a, True) for a in grads] + [blk(a, False) for a in recvd],
            out_specs=[blk(a, False) for a in recvd]),
        out_shape=[jax.ShapeDtypeStruct(a.shape, a.dtype) for a in recvd],
        compiler_params=_params(("arbitrary",)),
    )(core, *grads, *recvd)


def _reduce_scatter_split(grads, tag, core, behind):
    n = len(grads)
    plan1, slots1 = _reduce_plan_d2d(n)
    lands1 = [lax.empty((4,) + a.shape[1:], a.dtype) for a in grads]
    s1, r1, srcs1, lands1, token1 = _split_start(f"reduce_{tag}_d2d_start", grads, lands1, plan1, None, _SIBLING)
    own, got = _split_wait(f"reduce_{tag}_d2d_wait", s1, r1, srcs1, lands1, plan1, slots1, behind(token1))
    parts = _chip_partial(own, got, core, f"reduce_{tag}_partial")
    plan2, slots2 = _reduce_plan_ici(n)
    lands2 = [lax.empty((3,) + a.shape[1:], a.dtype) for a in grads]
    s2, r2, srcs2, lands2, token2 = _split_start(f"reduce_{tag}_ici_start", parts, lands2, plan2, None, _CHIPS)

    def finish(after):
        return _split_wait(f"reduce_{tag}_ici_wait", s2, r2, srcs2, lands2, plan2, slots2, after)

    return finish, token2


def _place_shards(me, shards, dtypes, name, after=None):
    n = len(shards)

    def body(me_ref, *refs):
        for a in range(n):
            refs[n + a][...] = refs[a][...].astype(dtypes[a])

    full = lambda s: pl.BlockSpec(s.shape, lambda i, me_ref: (0,) * s.ndim)
    slot = lambda s: pl.BlockSpec((None,) + s.shape, lambda i, me_ref: (me_ref[0],) + (0,) * s.ndim)
    body, more_specs, more = _ordered_behind(body, 1 + n, after)
    return pl.pallas_call(
        body, name=name,
        grid_spec=pltpu.PrefetchScalarGridSpec(num_scalar_prefetch=1, grid=(1,),
                                               in_specs=[full(s) for s in shards] + more_specs,
                                               out_specs=[slot(s) for s in shards]),
        out_shape=[jax.ShapeDtypeStruct((N_DEV,) + s.shape, d) for s, d in zip(shards, dtypes)],
        compiler_params=_params(("arbitrary",)),
    )(me, *shards, *more)


def _mix_in_fwd(x, g1, w_in_t, conv_w, gq, gk, gconv):
    tm = 512
    n_t = SEQ // tm

    def body(x_ref, g1_ref, w_ref, cw_ref, gq_ref, gk_ref, gc_ref,
             proj_ref, u1_ref, ycn_ref, qn_ref, kn_ref, v_ref, halo_ref):
        @pl.when(pl.program_id(0) == 0)
        def _():
            halo_ref[...] = jnp.zeros_like(halo_ref)

        xv = x_ref[...]
        u = (xv * _rstd(xv) * g1_ref[...]).astype(BF16)
        u1_ref[...] = u
        proj = _dot_nt(u, w_ref[...])
        proj_ref[...] = proj
        gate_b = proj[:, 0:CONV_WIDTH]
        a = proj[:, CONV_WIDTH:2 * CONV_WIDTH] * proj[:, 2 * CONV_WIDTH:3 * CONV_WIDTH]
        cv, _, _ = _conv3(a, _taps(cw_ref[...]), halo_ref[...])
        halo_ref[...] = a[tm - SUBLANES:]
        yc = gate_b * cv
        ycn_ref[...] = (yc * _rstd(yc) * gc_ref[...]).astype(BF16)
        q0 = 3 * CONV_WIDTH
        qn_ref[...] = _head_norm(proj[:, q0:q0 + ATTN_WIDTH], gq_ref[...], N_HEADS).astype(BF16)
        k0 = q0 + ATTN_WIDTH
        kn_ref[...] = _head_norm(proj[:, k0:k0 + KV_WIDTH], gk_ref[...], 2).astype(BF16)
        v_ref[...] = proj[:, k0 + KV_WIDTH:k0 + 2 * KV_WIDTH].astype(BF16)

    const = lambda shape: pl.BlockSpec(shape, lambda i: (0,) * len(shape))
    rows = lambda w: pl.BlockSpec((tm, w), lambda i: (i, 0))
    return pl.pallas_call(
        body, name="mix_in_fwd", grid=(n_t,),
        in_specs=[rows(D_MODEL), const((1, D_MODEL)), const((IN_WIDTH, D_MODEL)), const((3, CONV_WIDTH)),
                  const((1, HEAD_DIM)), const((1, HEAD_DIM)), const((1, CONV_WIDTH))],
        out_specs=[rows(IN_WIDTH), rows(D_MODEL), rows(CONV_WIDTH), rows(ATTN_WIDTH), rows(KV_WIDTH), rows(KV_WIDTH)],
        out_shape=[jax.ShapeDtypeStruct((SEQ, IN_WIDTH), F32), jax.ShapeDtypeStruct((SEQ, D_MODEL), BF16),
                   jax.ShapeDtypeStruct((SEQ, CONV_WIDTH), BF16),
                   jax.ShapeDtypeStruct((SEQ, ATTN_WIDTH), BF16), jax.ShapeDtypeStruct((SEQ, KV_WIDTH), BF16),
                   jax.ShapeDtypeStruct((SEQ, KV_WIDTH), BF16)],
        scratch_shapes=[pltpu.VMEM((SUBLANES, CONV_WIDTH), F32)],
        compiler_params=_params(("arbitrary",)),
    )(x, g1, w_in_t, conv_w, gq, gk, gconv)


GROUP_ROWS = GQA_GROUP * BLK
QUERY_BLOCKS_PER_STEP = 2


def _band_bias(tbl_ref, bkt, bias_ref):
    for h in range(N_HEADS):
        acc = jnp.zeros(bkt.shape, F32)
        for b in range(NUM_BUCKETS):
            acc = jnp.where(bkt == b, tbl_ref[h, b], acc)
        bias_ref[h // GQA_GROUP, BLK * (h % GQA_GROUP):BLK * (h % GQA_GROUP + 1), :] = acc


def _band_masks(i):
    qi = lax.broadcasted_iota(jnp.int32, (GROUP_ROWS, BLK), 0) & (BLK - 1)
    ji = lax.broadcasted_iota(jnp.int32, (GROUP_ROWS, BLK), 1)
    upper = ji > qi
    return upper, upper & (i == 0)


def _stack_heads(x, g):
    return jnp.concatenate([x[:, HEAD_DIM * h:HEAD_DIM * (h + 1)] for h in range(GQA_GROUP * g, GQA_GROUP * (g + 1))], axis=0)


def _unstack_heads(groups):
    return jnp.concatenate([p[BLK * t:BLK * (t + 1)] for p in groups for t in range(GQA_GROUP)], axis=-1)


def _per_head_rows(vals):
    row = lax.broadcasted_iota(jnp.int32, (GROUP_ROWS, 1), 0)
    col = jnp.full((GROUP_ROWS, 1), vals[GQA_GROUP - 1], F32)
    for t in range(GQA_GROUP - 2, -1, -1):
        col = jnp.where(row < BLK * (t + 1), vals[t], col)
    return col


def _band_rows(ref, i):
    prev = pl.multiple_of(jnp.maximum(i - 1, 0) * BLK, BLK)
    cur = pl.multiple_of(i * BLK, BLK)
    return jnp.concatenate([ref[pl.ds(prev, BLK), :], ref[pl.ds(cur, BLK), :]], axis=0), prev, cur


def _fold(band, upper):
    return jnp.where(upper, band[:, :BLK], band[:, BLK:])


def _unfold(tile, upper):
    return jnp.concatenate([jnp.where(upper, tile, 0.0), jnp.where(upper, 0.0, tile)], axis=1)


def _head_probs(qh, kh, bias, upper, dead, sink):
    logits = _fold(_dot_nt(qh, kh), upper) * (HEAD_DIM ** -0.5) + bias
    logits = jnp.where(dead, NEG_INF, logits)
    m = jnp.maximum(jnp.max(logits, axis=-1, keepdims=True), sink)
    p = jnp.exp(logits - m)
    es = jnp.exp(sink - m)
    den = jnp.sum(p, axis=-1, keepdims=True) + es
    return p / den, es / den


def _attn_fwd(qn, kn, v, tbl, sinks, bkt, gattn, after=None):
    n_b = SEQ // BLK

    def body(q_ref, k_ref, v_ref, tbl_ref, sink_ref, bkt_ref, ga_ref, y_ref, yn_ref, p_ref, ps_ref, bias_ref):
        step = pl.program_id(0)

        @pl.when(step == 0)
        def _():
            _band_bias(tbl_ref, bkt_ref[...], bias_ref)

        lane = lax.broadcasted_iota(jnp.int32, (BLK, 128), 1)
        for b in range(QUERY_BLOCKS_PER_STEP):
            i = QUERY_BLOCKS_PER_STEP * step + b
            rows = slice(BLK * b, BLK * (b + 1))
            kb, _, _ = _band_rows(k_ref, i)
            vb, _, _ = _band_rows(v_ref, i)
            upper, dead = _band_masks(i)
            q = q_ref[rows, :]
            outs = []
            psinks = jnp.zeros((BLK, 128), F32)
            for g in range(N_HEADS // GQA_GROUP):
                kv = slice(HEAD_DIM * g, HEAD_DIM * (g + 1))
                sink = _per_head_rows([sink_ref[0, GQA_GROUP * g + t] for t in range(GQA_GROUP)])
                probs, psink = _head_probs(_stack_heads(q, g), kb[:, kv], bias_ref[g], upper, dead, sink)
                p_ref[b, g] = probs.astype(BF16)
                for t in range(GQA_GROUP):
                    psinks = jnp.where(lane == GQA_GROUP * g + t, psink[BLK * t:BLK * (t + 1)], psinks)
                outs.append(_dot(_unfold(probs, upper).astype(BF16), vb[:, kv]))
            ps_ref[rows, :] = psinks
            y = _unstack_heads(outs)
            y_ref[rows, :] = y
            yn_ref[rows, :] = (y * _rstd(y) * ga_ref[...]).astype(BF16)

    const = lambda shape: pl.BlockSpec(shape, lambda i: (0,) * len(shape))
    rows = lambda w: pl.BlockSpec((QUERY_BLOCKS_PER_STEP * BLK, w), lambda i: (i, 0))
    smem = pl.BlockSpec(memory_space=pltpu.SMEM)
    body, more_specs, more = _ordered_behind(body, 7, after)
    return pl.pallas_call(
        body, name="attn_fwd", grid=(n_b // QUERY_BLOCKS_PER_STEP,),
        in_specs=[rows(ATTN_WIDTH), const((SEQ, KV_WIDTH)), const((SEQ, KV_WIDTH)), smem, smem,
                  const((BLK, BLK)), const((1, ATTN_WIDTH))] + more_specs,
        out_specs=[rows(ATTN_WIDTH), rows(ATTN_WIDTH),
                   pl.BlockSpec((QUERY_BLOCKS_PER_STEP, N_HEADS // GQA_GROUP, GROUP_ROWS, BLK), lambda i: (i, 0, 0, 0)),
                   rows(128)],
        out_shape=[jax.ShapeDtypeStruct((SEQ, ATTN_WIDTH), F32), jax.ShapeDtypeStruct((SEQ, ATTN_WIDTH), BF16),
                   jax.ShapeDtypeStruct((n_b, N_HEADS // GQA_GROUP, GROUP_ROWS, BLK), BF16),
                   jax.ShapeDtypeStruct((SEQ, 128), F32)],
        scratch_shapes=[pltpu.VMEM((N_HEADS // GQA_GROUP, GROUP_ROWS, BLK), F32)],
        compiler_params=_params(("arbitrary",)),
    )(qn, kn, v, tbl, sinks, bkt, gattn, *more)


def _ffn_block(i, step):
    return jnp.where(i % 2 == 0, step, N_FFN_BLK - 1 - step)


def _ffn_fwd(x, ycn, yan, w_out, g2, w_up, fcw, fcb, w_down, tgt):
    tm = 512
    n_t = SEQ // tm

    def body(x_ref, ycn_ref, yan_ref, wo_ref, g2_ref, wu_ref, cw_ref, b_ref, wd_ref, tgt_ref,
             h1_ref, u2_ref, up_ref, pre_ref, act_ref, dh2_ref, dh2b_ref, loss_ref, acc_ref, halo_ref):
        i, step = pl.program_id(0), pl.program_id(1)
        j = _ffn_block(i, step)

        @pl.when((i == 0) & (step == 0))
        def _():
            loss_ref[...] = jnp.zeros_like(loss_ref)

        @pl.when(step == 0)
        def _():
            h1 = x_ref[...] + _dot(ycn_ref[...], wo_ref[0:CONV_WIDTH, :]) + _dot(yan_ref[...], wo_ref[CONV_WIDTH:, :])
            h1_ref[...] = h1
            u2_ref[...] = (h1 * _rstd(h1) * g2_ref[...]).astype(BF16)
            acc_ref[...] = jnp.zeros_like(acc_ref)

        u2 = u2_ref[...]
        pre = []
        for s in range(2):
            up = _dot_nt(u2, wu_ref[s])
            up_ref[s] = up.astype(BF16)
            halo = jnp.where(i == 0, 0.0, halo_ref[s, j])
            pre.append(_conv3(up, _taps(cw_ref.at[s]), halo)[0] + b_ref[s])
            pre_ref[s] = pre[s].astype(BF16)
            halo_ref[s, j] = up[tm - SUBLANES:]
        g, val = pre
        act = (g * jax.nn.sigmoid(g) * val).astype(BF16)
        act_ref[...] = act
        acc_ref[...] += _dot(act, wd_ref[...])

        @pl.when(step == N_FFN_BLK - 1)
        def _():
            err = h1_ref[...] + acc_ref[...] - tgt_ref[...]
            loss_ref[...] += 0.5 * jnp.sum(err * err) / D_MODEL
            dh2 = err / D_MODEL
            dh2_ref[...] = dh2
            dh2b_ref[...] = dh2.astype(BF16)

    rows = lambda w: pl.BlockSpec((tm, w), lambda i, step: (i, 0))
    const = lambda shape: pl.BlockSpec(shape, lambda i, step: (0,) * len(shape))
    pair = lambda *s: pl.BlockSpec((2, None) + s, lambda i, step: (0, _ffn_block(i, step)) + (0,) * len(s))
    upb = pl.BlockSpec((2, None, tm, FFN_BLK), lambda i, step: (0, _ffn_block(i, step), i, 0))
    return pl.pallas_call(
        body, name="ffn_fwd", grid=(n_t, N_FFN_BLK),
        in_specs=[rows(D_MODEL), rows(CONV_WIDTH), rows(ATTN_WIDTH), const((D_MODEL, D_MODEL)), const((1, D_MODEL)),
                  pair(FFN_BLK, D_MODEL), pair(3, 1, FFN_BLK), pair(1, FFN_BLK),
                  pl.BlockSpec((None, FFN_BLK, D_MODEL), lambda i, step: (_ffn_block(i, step), 0, 0)), rows(D_MODEL)],
        out_specs=[rows(D_MODEL), rows(D_MODEL), upb, upb,
                   pl.BlockSpec((None, tm, FFN_BLK), lambda i, step: (_ffn_block(i, step), i, 0)),
                   rows(D_MODEL), rows(D_MODEL), const((SUBLANES, 128))],
        out_shape=[jax.ShapeDtypeStruct((SEQ, D_MODEL), F32), jax.ShapeDtypeStruct((SEQ, D_MODEL), BF16),
                   jax.ShapeDtypeStruct((2, N_FFN_BLK, SEQ, FFN_BLK), BF16),
                   jax.ShapeDtypeStruct((2, N_FFN_BLK, SEQ, FFN_BLK), BF16),
                   jax.ShapeDtypeStruct((N_FFN_BLK, SEQ, FFN_BLK), BF16),
                   jax.ShapeDtypeStruct((SEQ, D_MODEL), F32), jax.ShapeDtypeStruct((SEQ, D_MODEL), BF16),
                   jax.ShapeDtypeStruct((SUBLANES, 128), F32)],
        scratch_shapes=[pltpu.VMEM((tm, D_MODEL), F32), pltpu.VMEM((2, N_FFN_BLK, SUBLANES, FFN_BLK), F32)],
        compiler_params=_params(("arbitrary", "arbitrary")),
    )(x, ycn, yan, w_out, g2, w_up, fcw, fcb, w_down, tgt)


def _ffn_bwd(dh2, dh2b, h1, g2, up, pre, w_up, fcw, w_down, after=None):
    tm = 512
    units = ((288, 224), (0, 288))
    n_t = SEQ // tm

    def body(dh2_ref, dh2b_ref, h1_ref, g2_ref, up_ref, pre_ref, wu_ref, cw_ref, wd_ref,
             dup_ref, dh1_ref, dh1b_ref, dfb_ref, dfcw_ref, dg2_ref, acc_ref, next_ref):
        i, step = pl.program_id(0), pl.program_id(1)
        j = _ffn_block(i, step)

        @pl.when((i == 0) & (step == 0))
        def _():
            dfb_ref[...] = jnp.zeros_like(dfb_ref)
            dfcw_ref[...] = jnp.zeros_like(dfcw_ref)
            dg2_ref[...] = jnp.zeros_like(dg2_ref)

        @pl.when(step == 0)
        def _():
            acc_ref[...] = jnp.zeros_like(acc_ref)

        nxt = [jnp.where(i == 0, 0.0, next_ref[s, j]) for s in range(2)]
        sums = [[0.0] * 4 for _ in range(2)]
        for r0, rn in units:
            rows = slice(r0, r0 + rn)
            g, val = pre_ref[0, rows, :].astype(F32), pre_ref[1, rows, :].astype(F32)
            sg = jax.nn.sigmoid(g)
            silu = g * sg
            dact = _dot_nt(dh2b_ref[rows, :], wd_ref[...])
            dpre = (dact * val * (sg * (1.0 + g * (1.0 - sg))), dact * silu)
            dups = []
            for s in range(2):
                d = dpre[s]
                u = up_ref[s, rows, :].astype(F32)
                w = _taps(cw_ref.at[s])
                d1 = _shift_up(d, 1, nxt[s])
                d2 = _shift_up(d, 2, nxt[s])
                nxt[s] = d[:SUBLANES]
                for t, term in enumerate((d, d2 * u, d1 * u, d * u)):
                    sums[s][t] = sums[s][t] + jnp.sum(term, axis=0, keepdims=True)
                dups.append((d * w[2] + d1 * w[1] + d2 * w[0]).astype(BF16))
                dup_ref[s, rows, :] = dups[s]
            acc_ref[rows, :] += _dot(dups[0], wu_ref[0]) + _dot(dups[1], wu_ref[1])
        for s in range(2):
            next_ref[s, j] = nxt[s]
            dfb_ref[s, j] += sums[s][0]
            for t in range(3):
                dfcw_ref[s, j, t] += sums[s][1 + t]

        @pl.when(step == N_FFN_BLK - 1)
        def _():
            dn, dgain = _rms_bwd(h1_ref[...], g2_ref[...], acc_ref[...])
            dh1 = dh2_ref[...] + dn
            dh1_ref[...] = dh1
            dh1b_ref[...] = dh1.astype(BF16)
            dg2_ref[...] += dgain

    rev = lambda i: n_t - 1 - i
    rows = lambda w: pl.BlockSpec((tm, w), lambda i, step: (rev(i), 0))
    const = lambda shape: pl.BlockSpec(shape, lambda i, step: (0,) * len(shape))
    pair = lambda *s: pl.BlockSpec((2, None) + s, lambda i, step: (0, _ffn_block(i, step)) + (0,) * len(s))
    upb = pl.BlockSpec((2, None, tm, FFN_BLK), lambda i, step: (0, _ffn_block(i, step), rev(i), 0))
    body, more_specs, more = _ordered_behind(body, 9, after)
    return pl.pallas_call(
        body, name="ffn_bwd", grid=(n_t, N_FFN_BLK),
        in_specs=[rows(D_MODEL), rows(D_MODEL), rows(D_MODEL), const((1, D_MODEL)), upb, upb,
                  pair(FFN_BLK, D_MODEL), pair(3, 1, FFN_BLK),
                  pl.BlockSpec((None, FFN_BLK, D_MODEL), lambda i, step: (_ffn_block(i, step), 0, 0))] + more_specs,
        out_specs=[upb, rows(D_MODEL), rows(D_MODEL),
                   const((2, N_FFN_BLK, 1, FFN_BLK)), const((2, N_FFN_BLK, 3, 1, FFN_BLK)), const((1, D_MODEL))],
        out_shape=[jax.ShapeDtypeStruct((2, N_FFN_BLK, SEQ, FFN_BLK), BF16), jax.ShapeDtypeStruct((SEQ, D_MODEL), F32),
                   jax.ShapeDtypeStruct((SEQ, D_MODEL), BF16), jax.ShapeDtypeStruct((2, N_FFN_BLK, 1, FFN_BLK), F32),
                   jax.ShapeDtypeStruct((2, N_FFN_BLK, 3, 1, FFN_BLK), F32), jax.ShapeDtypeStruct((1, D_MODEL), F32)],
        scratch_shapes=[pltpu.VMEM((tm, D_MODEL), F32), pltpu.VMEM((2, N_FFN_BLK, SUBLANES, FFN_BLK), F32)],
        compiler_params=_params(("arbitrary", "arbitrary")),
    )(dh2, dh2b, h1, g2, up, pre, w_up, fcw, w_down, *more)


def _grad_tn(a_list, b, out_rows, name, after=None):
    n = len(a_list)
    ncol = b.shape[1]

    def body(*refs):
        a_refs, b_ref, o_ref = refs[:n], refs[n], refs[n + 1]
        j = pl.program_id(0)
        for k in range(n):
            @pl.when(j == k)
            def _(k=k):
                o_ref[...] = _dot_tn(a_refs[k][...], b_ref[...]).astype(BF16)

    full = lambda shape: pl.BlockSpec(shape, lambda j: (0,) * len(shape))
    body, more_specs, more = _ordered_behind(body, n + 1, after)
    return pl.pallas_call(
        body, name=name, grid=(n,),
        in_specs=[full((SEQ, out_rows))] * n + [full((SEQ, ncol))] + more_specs,
        out_specs=pl.BlockSpec((None, out_rows, ncol), lambda j: (j, 0, 0)),
        out_shape=jax.ShapeDtypeStruct((n, out_rows, ncol), BF16),
        compiler_params=_params(("arbitrary",)),
    )(*a_list, b, *more)


def _grad_tn_blocked(a, b, name, per_step=2):
    nb, _, a_w = a.shape
    b_w = b.shape[-1]

    def body(a_ref, b_ref, o_ref):
        for p in range(per_step):
            o_ref[p] = _dot_tn(a_ref[p], b_ref[...]).astype(BF16)

    return pl.pallas_call(
        body, name=name, grid=(nb // per_step,),
        in_specs=[pl.BlockSpec((per_step, SEQ, a_w), lambda k: (k, 0, 0)), pl.BlockSpec((SEQ, b_w), lambda k: (0, 0))],
        out_specs=pl.BlockSpec((per_step, a_w, b_w), lambda k: (k, 0, 0)),
        out_shape=jax.ShapeDtypeStruct((nb, a_w, b_w), BF16),
        compiler_params=_params(("arbitrary",)),
    )(a, b)


def _out_bwd(dh1b, w_out, y_attn, gattn, after=None):
    tm = 1024
    n_t = SEQ // tm

    def body(dh_ref, wo_ref, y_ref, ga_ref, dycn_ref, dy_ref, dga_ref):
        @pl.when(pl.program_id(0) == 0)
        def _():
            dga_ref[...] = jnp.zeros_like(dga_ref)

        dycat = _dot_nt(dh_ref[...], wo_ref[...])
        dycn_ref[...] = dycat[:, :CONV_WIDTH]
        dy, dga = _rms_bwd(y_ref[...], ga_ref[...], dycat[:, CONV_WIDTH:])
        dy_ref[...] = dy
        dga_ref[...] += dga

    rows = lambda w: pl.BlockSpec((tm, w), lambda i: (i, 0))
    const = lambda shape: pl.BlockSpec(shape, lambda i: (0,) * len(shape))
    body, more_specs, more = _ordered_behind(body, 4, after)
    return pl.pallas_call(
        body, name="out_bwd", grid=(n_t,),
        in_specs=[rows(D_MODEL), const((D_MODEL, D_MODEL)), rows(ATTN_WIDTH), const((1, ATTN_WIDTH))] + more_specs,
        out_specs=[rows(CONV_WIDTH), rows(ATTN_WIDTH), const((1, ATTN_WIDTH))],
        out_shape=[jax.ShapeDtypeStruct((SEQ, CONV_WIDTH), F32), jax.ShapeDtypeStruct((SEQ, ATTN_WIDTH), F32),
                   jax.ShapeDtypeStruct((1, ATTN_WIDTH), F32)],
        compiler_params=_params(("arbitrary",)),
    )(dh1b, w_out, y_attn, gattn, *more)


def _attn_bwd(qn, kn, v, dy, probs, psinks, bkt, after=None):
    n_b = SEQ // BLK

    def body(q_ref, k_ref, v_ref, dy_ref, p_ref, ps_ref, bkt_ref,
             dq_ref, dk_ref, dv_ref, dtbl_ref, dsink_ref, dbias_ref, dsacc_ref):
        step = pl.program_id(0)

        @pl.when(step == 0)
        def _():
            dbias_ref[...] = jnp.zeros_like(dbias_ref)
            dsacc_ref[...] = jnp.zeros_like(dsacc_ref)
            dk_ref[...] = jnp.zeros_like(dk_ref)
            dv_ref[...] = jnp.zeros_like(dv_ref)

        lane = lax.broadcasted_iota(jnp.int32, (BLK, 128), 1)
        for blk in range(QUERY_BLOCKS_PER_STEP):
            i = QUERY_BLOCKS_PER_STEP * step + blk
            rows = slice(BLK * blk, BLK * (blk + 1))
            kb, prev, cur = _band_rows(k_ref, i)
            vb, _, _ = _band_rows(v_ref, i)
            upper, _ = _band_masks(i)
            q = q_ref[rows, :]
            dy = dy_ref[rows, :]
            psink = ps_ref[rows, :]
            dsink = jnp.zeros((BLK, 128), F32)
            dqs, dks, dvs = [], [], []
            for g in range(N_HEADS // GQA_GROUP):
                kv = slice(HEAD_DIM * g, HEAD_DIM * (g + 1))
                qg = _stack_heads(q, g)
                dog = _stack_heads(dy, g).astype(BF16)
                pb = p_ref[blk, g]
                pg = pb.astype(F32)
                dprobs = _fold(_dot_nt(dog, vb[:, kv]), upper)
                dvs.append(_dot_tn(_unfold(pb, upper), dog))
                dsum = jnp.sum(pg * dprobs, axis=-1, keepdims=True)
                dlogits = pg * (dprobs - dsum)
                for t in range(GQA_GROUP):
                    dsink = jnp.where(lane == GQA_GROUP * g + t, -psink * dsum[BLK * t:BLK * (t + 1)], dsink)
                dbias_ref[g] += dlogits
                ds = _unfold(dlogits * (HEAD_DIM ** -0.5), upper).astype(BF16)
                dqs.append(_dot(ds, kb[:, kv]))
                dks.append(_dot_tn(ds, qg))
            dsacc_ref[...] += dsink
            dq_ref[rows, :] = _unstack_heads(dqs)
            dkb = jnp.concatenate(dks, axis=-1)
            dvb = jnp.concatenate(dvs, axis=-1)
            dk_ref[pl.ds(prev, BLK), :] += dkb[:BLK]
            dk_ref[pl.ds(cur, BLK), :] += dkb[BLK:]
            dv_ref[pl.ds(prev, BLK), :] += dvb[:BLK]
            dv_ref[pl.ds(cur, BLK), :] += dvb[BLK:]

        @pl.when(step == n_b // QUERY_BLOCKS_PER_STEP - 1)
        def _():
            bkt = bkt_ref[...]
            row8 = lax.broadcasted_iota(jnp.int32, (N_HEADS, 128), 0)
            lane8 = lax.broadcasted_iota(jnp.int32, (N_HEADS, 128), 1)
            acc = jnp.zeros((N_HEADS, 128), F32)
            for h in range(N_HEADS):
                rows = slice(BLK * (h % GQA_GROUP), BLK * (h % GQA_GROUP + 1))
                dbh = dbias_ref[h // GQA_GROUP, rows, :]
                for b in range(NUM_BUCKETS):
                    acc = jnp.where((row8 == h) & (lane8 == b), jnp.sum(jnp.where(bkt == b, dbh, 0.0)), acc)
            dsink_ref[...] = jnp.sum(dsacc_ref[...], axis=0, keepdims=True)
            dtbl_ref[...] = acc

    const = lambda shape: pl.BlockSpec(shape, lambda i: (0,) * len(shape))
    rows = lambda w: pl.BlockSpec((QUERY_BLOCKS_PER_STEP * BLK, w), lambda i: (i, 0))
    n_g = N_HEADS // GQA_GROUP
    body, more_specs, more = _ordered_behind(body, 7, after)
    return pl.pallas_call(
        body, name="attn_bwd", grid=(n_b // QUERY_BLOCKS_PER_STEP,),
        in_specs=[rows(ATTN_WIDTH), const((SEQ, KV_WIDTH)), const((SEQ, KV_WIDTH)), rows(ATTN_WIDTH),
                  pl.BlockSpec((QUERY_BLOCKS_PER_STEP, n_g, GROUP_ROWS, BLK), lambda i: (i, 0, 0, 0)), rows(128),
                  const((BLK, BLK))] + more_specs,
        out_specs=[rows(ATTN_WIDTH), const((SEQ, KV_WIDTH)), const((SEQ, KV_WIDTH)), const((N_HEADS, 128)), const((1, 128))],
        out_shape=[jax.ShapeDtypeStruct((SEQ, ATTN_WIDTH), F32), jax.ShapeDtypeStruct((SEQ, KV_WIDTH), F32),
                   jax.ShapeDtypeStruct((SEQ, KV_WIDTH), F32), jax.ShapeDtypeStruct((N_HEADS, 128), F32),
                   jax.ShapeDtypeStruct((1, 128), F32)],
        scratch_shapes=[pltpu.VMEM((n_g, GROUP_ROWS, BLK), F32), pltpu.VMEM((BLK, 128), F32)],
        compiler_params=_params(("arbitrary",)),
    )(qn, kn, v, dy, probs, psinks, bkt, *more)


def _mix_in_bwd(x, dh1, proj, dycn, dqn, dkn, dv, w_in_t, conv_w, g1, gq, gk, gconv):
    tm = 512
    n_t = SEQ // tm
    halo_blocks = tm // SUBLANES

    def body(x_ref, dh1_ref, proj_ref, halo_ref, dycn_ref, dqn_ref, dkn_ref, dv_ref, w_ref, cw_ref,
             g1_ref, gq_ref, gk_ref, gc_ref,
             dx_ref, dproj_ref, dcw_ref, dgc_ref, dgq_ref, dgk_ref, dg1_ref, next_ref):
        i = pl.program_id(0)
        first_tile = i == n_t - 1

        @pl.when(i == 0)
        def _():
            for r in (dcw_ref, dgc_ref, dgq_ref, dgk_ref, dg1_ref, next_ref):
                r[...] = jnp.zeros_like(r)

        proj = proj_ref[...]
        hp = halo_ref[...]
        gate_b = proj[:, 0:CONV_WIDTH]
        gate_c = proj[:, CONV_WIDTH:2 * CONV_WIDTH]
        hc = proj[:, 2 * CONV_WIDTH:3 * CONV_WIDTH]
        a = gate_c * hc
        a_halo = jnp.where(first_tile, 0.0, hp[:, CONV_WIDTH:2 * CONV_WIDTH] * hp[:, 2 * CONV_WIDTH:3 * CONV_WIDTH])
        cw = _taps(cw_ref[...])
        cv, a2, a1 = _conv3(a, cw, a_halo)
        dyc, dgc = _rms_bwd(gate_b * cv, gc_ref[...], dycn_ref[...])
        dgc_ref[...] += dgc
        dcv = dyc * gate_b
        dcw_ref[...] += jnp.concatenate(
            [jnp.sum(dcv * a2, axis=0, keepdims=True), jnp.sum(dcv * a1, axis=0, keepdims=True),
             jnp.sum(dcv * a, axis=0, keepdims=True)], axis=0)
        da = _conv3_bwd_input(dcv, cw, next_ref[...])
        next_ref[...] = dcv[:SUBLANES]
        q0 = 3 * CONV_WIDTH
        k0 = q0 + ATTN_WIDTH
        dq, dgq = _head_norm_bwd(proj[:, q0:k0], gq_ref[...], dqn_ref[...], N_HEADS)
        dk, dgk = _head_norm_bwd(proj[:, k0:k0 + KV_WIDTH], gk_ref[...], dkn_ref[...], 2)
        dgq_ref[...] += dgq
        dgk_ref[...] += dgk
        dproj = jnp.concatenate([dyc * cv, da * hc, da * gate_c, dq, dk, dv_ref[...]], axis=-1).astype(BF16)
        dproj_ref[...] = dproj
        du1 = _dot(dproj, w_ref[...])
        xv = x_ref[...]
        dn, dg1 = _rms_bwd(xv, g1_ref[...], du1)
        dx_ref[...] = dh1_ref[...] + dn
        dg1_ref[...] += dg1

    rev = lambda i: n_t - 1 - i
    rows = lambda w: pl.BlockSpec((tm, w), lambda i: (rev(i), 0))
    const = lambda shape: pl.BlockSpec(shape, lambda i: (0,) * len(shape))
    halo = pl.BlockSpec((SUBLANES, IN_WIDTH), lambda i: (jnp.maximum(rev(i) * halo_blocks - 1, 0), 0))
    return pl.pallas_call(
        body, name="mix_in_bwd", grid=(n_t,),
        in_specs=[rows(D_MODEL), rows(D_MODEL), rows(IN_WIDTH), halo, rows(CONV_WIDTH), rows(ATTN_WIDTH), rows(KV_WIDTH),
                  rows(KV_WIDTH), const((IN_WIDTH, D_MODEL)), const((3, CONV_WIDTH)), const((1, D_MODEL)),
                  const((1, HEAD_DIM)), const((1, HEAD_DIM)), const((1, CONV_WIDTH))],
        out_specs=[rows(D_MODEL), rows(IN_WIDTH), const((3, CONV_WIDTH)), const((1, CONV_WIDTH)),
                   const((1, HEAD_DIM)), const((1, HEAD_DIM)), const((1, D_MODEL))],
        out_shape=[jax.ShapeDtypeStruct((SEQ, D_MODEL), F32), jax.ShapeDtypeStruct((SEQ, IN_WIDTH), BF16),
                   jax.ShapeDtypeStruct((3, CONV_WIDTH), F32),
                   jax.ShapeDtypeStruct((1, CONV_WIDTH), F32), jax.ShapeDtypeStruct((1, HEAD_DIM), F32),
                   jax.ShapeDtypeStruct((1, HEAD_DIM), F32), jax.ShapeDtypeStruct((1, D_MODEL), F32)],
        scratch_shapes=[pltpu.VMEM((SUBLANES, CONV_WIDTH), F32)],
        compiler_params=_params(("arbitrary",)),
    )(x, dh1, proj, proj, dycn, dqn, dkn, dv, w_in_t, conv_w, g1, gq, gk, gconv)


def _grad_w_in(dproj, u1, after=None):
    bw = 768

    def body(a_ref, b_ref, o_ref):
        o_ref[...] = _dot_tn(a_ref[...], b_ref[...]).astype(BF16)

    body, more_specs, more = _ordered_behind(body, 2, after)
    return pl.pallas_call(
        body, name="grad_w_in", grid=(IN_WIDTH // bw,),
        in_specs=[pl.BlockSpec((SEQ, bw), lambda k: (0, k)), pl.BlockSpec((SEQ, D_MODEL), lambda k: (0, 0))] + more_specs,
        out_specs=pl.BlockSpec((bw, D_MODEL), lambda k: (k, 0)),
        out_shape=jax.ShapeDtypeStruct((IN_WIDTH, D_MODEL), BF16),
        compiler_params=_params(("arbitrary",)),
    )(dproj, u1, *more)


def _adamw_math(w, g, m, v):
    m = ADAM_B1 * m + (1.0 - ADAM_B1) * g
    v = ADAM_B2 * v + (1.0 - ADAM_B2) * (g * g)
    m_hat = m / (1.0 - ADAM_B1 ** ADAM_STEP)
    v_hat = v / (1.0 - ADAM_B2 ** ADAM_STEP)
    return -ADAM_LR * (m_hat / (jnp.sqrt(v_hat) + ADAM_EPS) + ADAM_WD * w), m, v


_ROW_G1, _ROW_G2, _ROW_OUT_NORMS, _ROW_FFN_B, _ROW_GQ, _ROW_GK, _ROW_SINKS, _ROW_LOSS, _ROW_TABLE = 0, 1, 2, 3, 11, 12, 13, 14, 16
SMALL_ROWS, SMALL_COLS = 24, 1024
_SMALL_NAMES = ("norm_mix_g", "norm_ffn_g", "out_norm_conv_g", "out_norm_attn_g", "ffn_conv_b", "q_norm_g", "k_norm_g",
                "sinks", "rel_bias_table")


def _pack_small_grads(dg1, dg2, dgconv, dgattn, dfb, dgq, dgk, dsinks, dtbl_t, loss_acc):
    def body(dg1_ref, dg2_ref, dgc_ref, dga_ref, dfb_ref, dgq_ref, dgk_ref, ds_ref, dt_ref, loss_ref, o_ref, all_ref):
        o_ref[...] = jnp.zeros_like(o_ref)
        o_ref[_ROW_G1:_ROW_G1 + 1, :] = dg1_ref[...]
        o_ref[_ROW_G2:_ROW_G2 + 1, :] = dg2_ref[...]
        o_ref[_ROW_OUT_NORMS:_ROW_OUT_NORMS + 1, 0:CONV_WIDTH] = dgc_ref[...]
        o_ref[_ROW_OUT_NORMS:_ROW_OUT_NORMS + 1, CONV_WIDTH:] = dga_ref[...]
        for k in range(N_DEV):
            o_ref[_ROW_FFN_B + k:_ROW_FFN_B + k + 1, 0:FFN_BLK] = dfb_ref[k // N_FFN_BLK, k % N_FFN_BLK]
        o_ref[_ROW_GQ:_ROW_GQ + 1, 0:HEAD_DIM] = dgq_ref[...]
        o_ref[_ROW_GK:_ROW_GK + 1, 0:HEAD_DIM] = dgk_ref[...]
        o_ref[_ROW_SINKS:_ROW_SINKS + 1, 0:128] = ds_ref[...]
        o_ref[_ROW_LOSS:_ROW_LOSS + 1, 0:128] = loss_ref[0:1, :]
        o_ref[_ROW_TABLE:_ROW_TABLE + N_HEADS, 0:128] = dt_ref[...]
        for s in range(N_DEV):
            all_ref[s] = o_ref[...]

    return pl.pallas_call(
        body, name="pack_small_grads",
        out_shape=[jax.ShapeDtypeStruct((SMALL_ROWS, SMALL_COLS), F32),
                   jax.ShapeDtypeStruct((N_DEV, SMALL_ROWS, SMALL_COLS), F32)],
    )(dg1, dg2, dgconv, dgattn, dfb, dgq, dgk, dsinks, dtbl_t, loss_acc)


def _adamw_small(recv, params, after):
    names = _SMALL_NAMES
    n = len(names)

    def grad_of(g, name, k=None):
        if name == "norm_mix_g":
            return g[_ROW_G1:_ROW_G1 + 1, :]
        if name == "norm_ffn_g":
            return g[_ROW_G2:_ROW_G2 + 1, :]
        if name == "out_norm_conv_g":
            return g[_ROW_OUT_NORMS:_ROW_OUT_NORMS + 1, 0:CONV_WIDTH]
        if name == "out_norm_attn_g":
            return g[_ROW_OUT_NORMS:_ROW_OUT_NORMS + 1, CONV_WIDTH:]
        if name == "ffn_conv_b":
            return g[_ROW_FFN_B + k:_ROW_FFN_B + k + 1, 0:FFN_BLK]
        if name == "q_norm_g":
            return g[_ROW_GQ:_ROW_GQ + 1, 0:HEAD_DIM]
        if name == "k_norm_g":
            return g[_ROW_GK:_ROW_GK + 1, 0:HEAD_DIM]
        if name == "sinks":
            return g[_ROW_SINKS:_ROW_SINKS + 1, 0:N_HEADS]
        return g[_ROW_TABLE:_ROW_TABLE + N_HEADS, 0:NUM_BUCKETS]

    def body(r_ref, *refs):
        ins, outs, loss_ref = refs[:3 * n], refs[3 * n:7 * n], refs[7 * n]
        g = r_ref[0]
        for s in range(1, N_DEV):
            g = g + r_ref[s]
        loss_ref[...] = g[_ROW_LOSS:_ROW_LOSS + 1, 0:128]
        for i, name in enumerate(names):
            w_ref, m_ref, v_ref = ins[3 * i:3 * i + 3]
            o = outs[4 * i:4 * i + 4]
            cols = [slice(FFN_BLK * k, FFN_BLK * (k + 1)) for k in range(N_DEV)] if name == "ffn_conv_b" else [slice(None)]
            for k, cs in enumerate(cols):
                gk = grad_of(g, name, k)
                d, m2, v2 = _adamw_math(w_ref[:, cs], gk, m_ref[:, cs], v_ref[:, cs])
                o[0][:, cs], o[1][:, cs], o[2][:, cs], o[3][:, cs] = gk, d, m2, v2

    flat = [a for name in names for a in params[name]]
    body, more_specs, more = _ordered_behind(body, 1 + 3 * n, after)
    vmem = pl.BlockSpec(memory_space=pltpu.VMEM)
    out = pl.pallas_call(
        body, name="adamw_small",
        in_specs=[vmem] * (1 + 3 * n) + more_specs,
        out_shape=[jax.ShapeDtypeStruct(params[name][0].shape, F32) for name in names for _ in range(4)]
        + [jax.ShapeDtypeStruct((1, 128), F32)],
        compiler_params=pltpu.CompilerParams(vmem_limit_bytes=VMEM_LIMIT),
    )(recv, *flat, *more)
    return {name: tuple(out[4 * i:4 * i + 4]) for i, name in enumerate(names)}, out[4 * n]


def _adamw_direct(w, m, v, own, recv, me, name, row_blocks=1, after=None):
    rb = w.shape[0] // row_blocks
    cols = w.shape[1]

    def body(me_ref, w_ref, m_ref, v_ref, o_ref, r_ref, g_o, d_o, m_o, v_o):
        g = o_ref[...].astype(F32)
        for s in range(N_DEV - 1):
            g = g + r_ref[s].astype(F32)
        g_o[...] = g
        d_o[...], m_o[...], v_o[...] = _adamw_math(w_ref[...], g, m_ref[...], v_ref[...])

    blk = pl.BlockSpec((rb, cols), lambda i, me_ref: (i, 0))
    oblk = pl.BlockSpec((None, rb, cols), lambda i, me_ref: (me_ref[0], i, 0))
    rblk = pl.BlockSpec((N_DEV - 1, rb, cols), lambda i, me_ref: (0, i, 0))
    body, more_specs, more = _ordered_behind(body, 6, after)
    return pl.pallas_call(
        body, name=name,
        grid_spec=pltpu.PrefetchScalarGridSpec(num_scalar_prefetch=1, grid=(row_blocks,),
                                               in_specs=[blk, blk, blk, oblk, rblk] + more_specs, out_specs=[blk] * 4),
        out_shape=[jax.ShapeDtypeStruct(w.shape, F32)] * 4,
        compiler_params=_params(("arbitrary",)),
    )(me, w, m, v, own, recv, *more)


def _adamw(w, m, v, part, recv, chip, name, row_blocks=1, after=None):
    rb = w.shape[0] // row_blocks
    tail = w.shape[1:]
    zeros = (0,) * len(tail)

    def body(chip_ref, w_ref, m_ref, v_ref, p_ref, r_ref, g_o, d_o, m_o, v_o):
        g = p_ref[...].astype(F32)
        for s in range(3):
            g = g + r_ref[s].astype(F32)
        g_o[...] = g
        d_o[...], m_o[...], v_o[...] = _adamw_math(w_ref[...], g, m_ref[...], v_ref[...])

    blk = pl.BlockSpec((rb,) + tail, lambda i, chip_ref: (i,) + zeros)
    pblk = pl.BlockSpec((None, rb) + tail, lambda i, chip_ref: (chip_ref[0], i) + zeros)
    rblk = pl.BlockSpec((3, rb) + tail, lambda i, chip_ref: (0, i) + zeros)
    body, more_specs, more = _ordered_behind(body, 6, after)
    return pl.pallas_call(
        body, name=name,
        grid_spec=pltpu.PrefetchScalarGridSpec(num_scalar_prefetch=1, grid=(row_blocks,),
                                               in_specs=[blk, blk, blk, pblk, rblk] + more_specs, out_specs=[blk] * 4),
        out_shape=[jax.ShapeDtypeStruct(w.shape, F32)] * 4,
        compiler_params=_params(("arbitrary",)),
    )(chip, w, m, v, part, recv, *more)


def kernel(x, norm_mix_g, w_in, conv_w, q_norm_g, k_norm_g, rel_bias_table, sinks, out_norm_conv_g, out_norm_attn_g, w_out, norm_ffn_g, w_up, ffn_conv_w, ffn_conv_b, w_down, loss_target, m_norm_mix_g, m_w_in, m_conv_w, m_q_norm_g, m_k_norm_g, m_rel_bias_table, m_sinks, m_out_norm_conv_g, m_out_norm_attn_g, m_w_out, m_norm_ffn_g, m_w_up, m_ffn_conv_w, m_ffn_conv_b, m_w_down, v_norm_mix_g, v_w_in, v_conv_w, v_q_norm_g, v_k_norm_g, v_rel_bias_table, v_sinks, v_out_norm_conv_g, v_out_norm_attn_g, v_w_out, v_norm_ffn_g, v_w_up, v_ffn_conv_w, v_ffn_conv_b, v_w_down):
    p = dict(norm_mix_g=norm_mix_g, w_in=w_in, conv_w=conv_w, q_norm_g=q_norm_g, k_norm_g=k_norm_g,
             rel_bias_table=rel_bias_table, sinks=sinks, out_norm_conv_g=out_norm_conv_g, out_norm_attn_g=out_norm_attn_g,
             w_out=w_out, norm_ffn_g=norm_ffn_g, w_up=w_up, ffn_conv_w=ffn_conv_w, ffn_conv_b=ffn_conv_b, w_down=w_down)
    m = dict(norm_mix_g=m_norm_mix_g, w_in=m_w_in, conv_w=m_conv_w, q_norm_g=m_q_norm_g, k_norm_g=m_k_norm_g,
             rel_bias_table=m_rel_bias_table, sinks=m_sinks, out_norm_conv_g=m_out_norm_conv_g,
             out_norm_attn_g=m_out_norm_attn_g, w_out=m_w_out, norm_ffn_g=m_norm_ffn_g, w_up=m_w_up,
             ffn_conv_w=m_ffn_conv_w, ffn_conv_b=m_ffn_conv_b, w_down=m_w_down)
    v = dict(norm_mix_g=v_norm_mix_g, w_in=v_w_in, conv_w=v_conv_w, q_norm_g=v_q_norm_g, k_norm_g=v_k_norm_g,
             rel_bias_table=v_rel_bias_table, sinks=v_sinks, out_norm_conv_g=v_out_norm_conv_g,
             out_norm_attn_g=v_out_norm_attn_g, w_out=v_w_out, norm_ffn_g=v_norm_ffn_g, w_up=v_w_up,
             ffn_conv_w=v_ffn_conv_w, ffn_conv_b=v_ffn_conv_b, w_down=v_w_down)

    xs, tgt = x[0], loss_target[0]
    g1, g2, gq, gk, gconv, gattn = norm_mix_g, norm_ffn_g, q_norm_g, k_norm_g, out_norm_conv_g, out_norm_attn_g
    ix, iy, ic = _coords()
    core = ic.astype(jnp.int32).reshape(1)
    chip = (2 * ix + iy).astype(jnp.int32).reshape(1)
    me = _lin(ix, iy, ic).astype(jnp.int32).reshape(1)
    bkt = jnp.asarray(_bucket_map())
    tr = lambda a: a[0].T
    taps = lambda a: jnp.transpose(a, (1, 0, 2))
    tbl_t = rel_bias_table.T

    wi_l, cw_l = _place_shards(me, [tr(w_in), taps(conv_w)], [BF16, F32], "place_mixer_shards")
    finish_a, token_a = _all_gather_split([wi_l, cw_l], "mixer", None)
    wo_l, wu_l, wd_l, fcw_l = _place_shards(me, [w_out[0], tr(w_up), w_down[0], taps(ffn_conv_w)],
                                            [BF16, BF16, BF16, F32], "place_ffn_shards", after=token_a)
    ffn_stage2, ffn_stage3, token_b = _all_gather_tree([wo_l, wu_l, wd_l, fcw_l], "ffn", token_a)
    wi_g, cw_g = finish_a(token_b)
    w_in_t = wi_g.reshape(IN_WIDTH, D_MODEL)
    conv_w_f = jnp.transpose(cw_g[:, :, 0, :], (1, 0, 2)).reshape(3, CONV_WIDTH)

    proj, u1, ycn, qn, kn, vv = _mix_in_fwd(xs, g1, w_in_t, conv_w_f, gq, gk, gconv)
    token_b2 = ffn_stage2(ycn)
    y_attn, yan, probs, psinks = _attn_fwd(qn, kn, vv, tbl_t, sinks, bkt, gattn, after=token_b2)
    wo_g, wu_g, wd_g, fcw_g = ffn_stage3(yan)
    w_out_f = wo_g.reshape(D_MODEL, D_MODEL)
    w_down_f = wd_g.reshape(N_FFN_BLK, FFN_BLK, D_MODEL)
    w_up_f = wu_g.reshape(2, N_FFN_BLK, FFN_BLK, D_MODEL)
    fcw_f = fcw_g.reshape(2, N_FFN_BLK, 3, 1, FFN_BLK)
    fcb = ffn_conv_b.reshape(2, N_FFN_BLK, 1, FFN_BLK)
    h1, u2, up, pre, act, dh2, dh2b, loss_acc = _ffn_fwd(xs, ycn, yan, w_out_f, g2, w_up_f, fcw_f, fcb, w_down_f, tgt)

    dw_down = _grad_tn_blocked(act, dh2b, "grad_w_down").reshape(N_DEV, D_FF // N_DEV, D_MODEL)
    plan_d, slots_d = _scatter_plan(1)
    d_sem = _split_start("scatter_w_down_start", [dw_down], [lax.empty((N_DEV - 1,) + dw_down.shape[1:], BF16)],
                         plan_d, None, _ALL_FOR_W_DOWN)
    dup, dh1, dh1b, dfb, dfcw, dg2 = _ffn_bwd(dh2, dh2b, h1, g2, up, pre, w_up_f, fcw_f, w_down_f, after=d_sem[4])
    dw_up = _grad_tn_blocked(dup.reshape(N_DEV, SEQ, FFN_BLK), u2, "grad_w_up")
    dw_out = _grad_tn([ycn, yan], dh1b, CONV_WIDTH, "grad_w_out").reshape(N_DEV, D_MODEL // N_DEV, D_MODEL)
    out_bwd = {}

    def behind_ffn(token):
        out_bwd["r"] = _out_bwd(dh1b, w_out_f, y_attn, gattn, after=token)
        return out_bwd["r"][0]

    finish_ffn, token_ffn = _reduce_scatter_split(
        [dw_up, dw_out, dfcw.reshape(N_DEV, 3, 1, FFN_BLK)], "ffn", core, behind_ffn)
    dycn, dy_attn, dgattn = out_bwd["r"]
    dqn, dkn, dv, dtbl_t, dsinks = _attn_bwd(qn, kn, vv, dy_attn, probs, psinks, bkt, after=token_ffn)
    dx, dproj, dcw, dgconv, dgq, dgk, dg1 = _mix_in_bwd(xs, dh1, proj, dycn, dqn, dkn, dv, w_in_t, conv_w_f,
                                                         g1, gq, gk, gconv)
    packed, packed_all = _pack_small_grads(dg1, dg2, dgconv, dgattn, dfb, dgq, dgk, dsinks, dtbl_t, loss_acc)
    plan_s, slots_s = _broadcast_plan()
    s_sem, r_sem, src_s, land_s, token_s = _split_start("gather_small_start", [packed], [packed_all], plan_s, None,
                                                        _ALL_FOR_SMALL)
    dw_in_t = _grad_w_in(dproj, u1, after=token_s).reshape(N_DEV, IN_WIDTH // N_DEV, D_MODEL)
    dcw_b = jnp.transpose(dcw.reshape(3, N_DEV, 1, CONV_WIDTH // N_DEV), (1, 0, 2, 3))
    adam = {}
    ffn_got = {}

    def behind_mixer(token):
        ffn_got["r"] = finish_ffn(token)
        return ffn_got["r"][1][0]

    finish_mixer, token_mixer = _reduce_scatter_split([dw_in_t, dcw_b], "mixer", core, behind_mixer)
    (p_wu, p_wo, p_fcw), (r_wu, r_wo, r_fcw) = ffn_got["r"]
    (own_wd,), (r_wd,) = _split_wait("scatter_w_down_wait", d_sem[0], d_sem[1], d_sem[2], d_sem[3], plan_d, slots_d,
                                     token_mixer)
    adam["w_down"] = _adamw_direct(w_down[0], m_w_down[0], v_w_down[0], own_wd, r_wd, me, "adamw_w_down", row_blocks=2)
    adam_up = _adamw(tr(w_up), tr(m_w_up), tr(v_w_up), p_wu, r_wu, chip, "adamw_w_up", row_blocks=4,
                     after=adam["w_down"][0])
    adam["w_out"] = _adamw(w_out[0], m_w_out[0], v_w_out[0], p_wo, r_wo, chip, "adamw_w_out", after=adam_up[0])
    adam_fcw = _adamw(taps(ffn_conv_w), taps(m_ffn_conv_w), taps(v_ffn_conv_w), p_fcw, r_fcw, chip, "adamw_ffn_conv_w",
                      after=adam["w_out"][0])
    _, (r_small,) = _split_wait("gather_small_wait", s_sem, r_sem, src_s, land_s, plan_s, slots_s, adam_fcw[0])
    small_in = {k: (p[k], m[k], v[k]) for k in _SMALL_NAMES}
    small_in["rel_bias_table"] = (tbl_t, m_rel_bias_table.T, v_rel_bias_table.T)
    small_out, loss_row = _adamw_small(r_small, small_in, None)
    (p_wi, p_cw), (r_wi, r_cw) = finish_mixer(loss_row)
    adam_in = _adamw(tr(w_in), tr(m_w_in), tr(v_w_in), p_wi, r_wi, chip, "adamw_w_in")
    adam_cw = _adamw(taps(conv_w), taps(m_conv_w), taps(v_conv_w), p_cw, r_cw, chip, "adamw_conv_w")

    res = {k: tuple(a[None] for a in t) for k, t in adam.items()}
    res["w_up"] = tuple(a.T[None] for a in adam_up)
    res["w_in"] = tuple(a.T[None] for a in adam_in)
    res["ffn_conv_w"] = tuple(taps(a) for a in adam_fcw)
    res["conv_w"] = tuple(taps(a) for a in adam_cw)
    res.update(small_out)
    res["rel_bias_table"] = tuple(a.T for a in small_out["rel_bias_table"])
    loss = loss_row[0, 0]
    order = ("norm_mix_g", "w_in", "conv_w", "q_norm_g", "k_norm_g", "rel_bias_table", "sinks", "out_norm_conv_g",
             "out_norm_attn_g", "w_out", "norm_ffn_g", "w_up", "ffn_conv_w", "ffn_conv_b", "w_down")
    return (loss, dx[None], *[res[k][0] for k in order], *[res[k][1] for k in order],
            *[res[k][2] for k in order], *[res[k][3] for k in order])
```

```python
import math

import numpy as np
import jax
import jax.numpy as jnp
from jax import lax
from jax.experimental import pallas as pl
from jax.experimental.pallas import tpu as pltpu

F32 = jnp.float32
BF16 = jnp.bfloat16

SEQ = 2048
D_MODEL = 1024
CONV_WIDTH = 512
ATTN_WIDTH = 512
KV_WIDTH = 128
HEAD_DIM = 64
N_HEADS = 8
GQA_GROUP = 4
IN_WIDTH = 2304
D_FF = 2816
BLK = 128
NUM_BUCKETS = 32
EPS = 1e-6
NEG_INF = -1e30
ADAM_LR = 0.001
ADAM_B1 = 0.9
ADAM_B2 = 0.999
ADAM_EPS = 1e-08
ADAM_WD = 0.01
ADAM_STEP = 10

N_DEV = 8
FFN_BLK = 2 * D_FF // N_DEV
N_FFN_BLK = D_FF // FFN_BLK
SUBLANES = 8
VMEM_LIMIT = 56 * 1024 * 1024

_MESH = pl.DeviceIdType.MESH
_ANY = pl.BlockSpec(memory_space=pl.ANY)


def _params(sem):
    return pltpu.CompilerParams(dimension_semantics=sem, vmem_limit_bytes=VMEM_LIMIT)


def _ordered_behind(body, pos, after):
    if after is None:
        return body, [], []
    return (lambda *refs: body(*refs[:pos], *refs[pos + 1:])), [_ANY], [after]


def _dot(a, b):
    return jnp.dot(a, b, preferred_element_type=F32)


def _dot_nt(a, b):
    return lax.dot_general(a, b, (((1,), (1,)), ((), ())), preferred_element_type=F32)


def _dot_tn(a, b):
    return lax.dot_general(a, b, (((0,), (0,)), ((), ())), preferred_element_type=F32)


def _shift_down(x, s, halo):
    r = pltpu.roll(x, s, axis=0)
    hr = pltpu.roll(halo, s, axis=0)
    row = lax.broadcasted_iota(jnp.int32, halo.shape, 0)
    top = jnp.where(row < s, hr, r[:SUBLANES])
    return jnp.concatenate([top, r[SUBLANES:]], axis=0)


def _shift_up(x, s, halo):
    n = x.shape[0]
    r = pltpu.roll(x, n - s, axis=0)
    hr = pltpu.roll(halo, SUBLANES - s, axis=0)
    row = lax.broadcasted_iota(jnp.int32, halo.shape, 0)
    bot = jnp.where(row >= SUBLANES - s, hr, r[n - SUBLANES:])
    return jnp.concatenate([r[:n - SUBLANES], bot], axis=0)


def _taps(w):
    return (w[0], w[1], w[2]) if len(w.shape) == 3 else (w[0:1], w[1:2], w[2:3])


def _conv3(x, w, halo):
    x2 = _shift_down(x, 2, halo)
    x1 = _shift_down(x, 1, halo)
    return x2 * w[0] + x1 * w[1] + x * w[2], x2, x1


def _conv3_bwd_input(dy, w, halo_next):
    return dy * w[2] + _shift_up(dy, 1, halo_next) * w[1] + _shift_up(dy, 2, halo_next) * w[0]


def _rstd(x):
    return lax.rsqrt(jnp.mean(x * x, axis=-1, keepdims=True) + EPS)


def _rms_bwd(x, g, dy):
    r = _rstd(x)
    n = x * r
    dn = dy * g
    dx = r * (dn - n * jnp.mean(dn * n, axis=-1, keepdims=True))
    return dx, jnp.sum(dy * n, axis=0, keepdims=True)


def _head_mean(x):
    width = x.shape[-1]
    ri = lax.broadcasted_iota(jnp.int32, (width, width), 0) // HEAD_DIM
    ci = lax.broadcasted_iota(jnp.int32, (width, width), 1) // HEAD_DIM
    ones = jnp.where(ri == ci, 1.0, 0.0).astype(BF16)
    hi = x.astype(BF16)
    lo = (x - hi.astype(F32)).astype(BF16)
    return (_dot(hi, ones) + _dot(lo, ones)) * (1.0 / HEAD_DIM)


def _head_norm(x, g, heads):
    return x * lax.rsqrt(_head_mean(x * x) + EPS) * jnp.tile(g, (1, heads))


def _head_norm_bwd(x, g, dy, heads):
    r = lax.rsqrt(_head_mean(x * x) + EPS)
    n = x * r
    dn = dy * jnp.tile(g, (1, heads))
    dx = r * (dn - n * _head_mean(dn * n))
    per_lane = jnp.sum(dy * n, axis=0, keepdims=True)
    dg = per_lane[:, 0:HEAD_DIM]
    for h in range(1, heads):
        dg = dg + per_lane[:, HEAD_DIM * h:HEAD_DIM * (h + 1)]
    return dx, dg


def _bucket_map():
    q = np.arange(BLK)[:, None]
    j = np.arange(BLK)[None, :]
    n = np.where(j > q, q + BLK - j, q - j)
    nf = np.maximum(n, 1).astype(np.float32)
    max_exact = NUM_BUCKETS // 2
    large = max_exact + (np.log(nf / max_exact) / math.log(BLK / max_exact) * (NUM_BUCKETS - max_exact)).astype(np.int32)
    large = np.minimum(large, NUM_BUCKETS - 1)
    return np.where(n < max_exact, n, large).astype(np.int32)


def _coords():
    return lax.axis_index("x"), lax.axis_index("y"), lax.axis_index("c")


def _lin(px, py, pc):
    return 4 * px + 2 * py + pc


def _chips(x, y):
    return [(1 - x, y), (x, 1 - y), (1 - x, 1 - y)]


def _peers(x, y, c):
    return [(1 - x if r & 4 else x, 1 - y if r & 2 else y, 1 - c if r & 1 else c) for r in range(1, N_DEV)]


_HBM = pl.BlockSpec(memory_space=pltpu.HBM)
_SEM = pl.BlockSpec(memory_space=pltpu.SEMAPHORE)
_EFFECT = pltpu.SideEffectType.DATAFLOW_SIDE_EFFECTING


def _in_hbm(a):
    return pltpu.with_memory_space_constraint(a, pltpu.HBM)


_SIBLING = (1, lambda x, y, c: [(x, y, 1 - c)])
_SIBLING_AND_CHIPS = (2, lambda x, y, c: [(x, y, 1 - c)] + [(cx, cy, c) for cx, cy in _chips(x, y)])
_SIBLING_AND_NEIGHBOURS = (3, lambda x, y, c: [(x, y, 1 - c), (1 - x, y, c), (x, 1 - y, c)])
_ONWARD_AND_SIBLING = (4, lambda x, y, c: [(jnp.where(c == 1, x, 1 - x), jnp.where(c == 1, 1 - y, y), c), (x, y, 1 - c)])
_ALL_FOR_W_DOWN = (5, lambda x, y, c: _peers(x, y, c))
_CHIPS = (6, lambda x, y, c: [(cx, cy, c) for cx, cy in _chips(x, y)])
_ALL_FOR_SMALL = (7, lambda x, y, c: _peers(x, y, c))


def _split_start(name, srcs, lands, plan, after, handshake):
    ns, nl = len(srcs), len(lands)
    n_copies = len(plan(0, 0, 0))
    n_after = 0 if after is None else 1
    collective_id, peers_of = handshake

    def body(*refs):
        src_refs, land_refs = refs[:ns + nl], refs[ns:ns + nl]
        send_sems, recv_sems = refs[ns + nl + n_after], refs[ns + nl + n_after + 1]
        token = refs[-1]
        barrier = pltpu.get_barrier_semaphore()
        peers = peers_of(*_coords())
        for peer in peers:
            pl.semaphore_signal(barrier, inc=1, device_id=peer, device_id_type=_MESH)
        pl.semaphore_wait(barrier, len(peers))
        for k, (a, s_slot, l, d_slot, dev) in enumerate(plan(*_coords())):
            src = src_refs[a] if s_slot is None else src_refs[a].at[s_slot]
            pltpu.make_async_remote_copy(src_ref=src, dst_ref=land_refs[l].at[d_slot], send_sem=send_sems.at[k],
                                         recv_sem=recv_sems.at[k], device_id=dev, device_id_type=_MESH).start()
        token[...] = jnp.zeros_like(token)

    arrs = list(srcs) + list(lands)
    out = pl.pallas_call(
        body, name=name,
        out_shape=(pltpu.SemaphoreType.DMA((n_copies,)), pltpu.SemaphoreType.DMA((n_copies,)),
                   *[pltpu.HBM(a.shape, a.dtype) for a in arrs], jax.ShapeDtypeStruct((SUBLANES, 128), F32)),
        in_specs=[_HBM] * (ns + nl) + [_ANY] * n_after,
        out_specs=(_SEM, _SEM, *[_HBM] * (ns + nl), pl.BlockSpec(memory_space=pltpu.VMEM)),
        input_output_aliases={i: 2 + i for i in range(ns + nl)},
        compiler_params=pltpu.CompilerParams(has_side_effects=_EFFECT, collective_id=collective_id),
    )(*[_in_hbm(a) for a in arrs], *([] if after is None else [after]))
    return out[0], out[1], list(out[2:2 + ns]), list(out[2 + ns:2 + ns + nl]), out[-1]


def _split_wait(name, send_sems, recv_sems, srcs, lands, plan, recv_slots, after):
    ns, nl = len(srcs), len(lands)

    def body(*refs):
        src_refs, land_refs = refs[:ns + nl], refs[ns:ns + nl]
        send_sems, recv_sems = refs[ns + nl], refs[ns + nl + 1]
        coords = _coords()
        slots = recv_slots(*coords)
        for k, (a, s_slot, l, _, dev) in enumerate(plan(*coords)):
            src = src_refs[a] if s_slot is None else src_refs[a].at[s_slot]
            cp = pltpu.make_async_remote_copy(src_ref=src, dst_ref=land_refs[l].at[slots[k]], send_sem=send_sems.at[k],
                                              recv_sem=recv_sems.at[k], device_id=dev, device_id_type=_MESH)
            cp.wait_send()
            cp.wait_recv()

    arrs = list(srcs) + list(lands)
    out = pl.pallas_call(
        body, name=name,
        out_shape=tuple(pltpu.HBM(a.shape, a.dtype) for a in arrs),
        in_specs=[_HBM] * (ns + nl) + [_SEM, _SEM, _ANY],
        out_specs=tuple([_HBM] * (ns + nl)),
        input_output_aliases={i: i for i in range(ns + nl)},
        compiler_params=pltpu.CompilerParams(has_side_effects=_EFFECT),
    )(*arrs, send_sems, recv_sems, after)
    return list(out[:ns]), list(out[ns:])


def _gather_plan_ici(n):
    def plan(x, y, c):
        me = _lin(x, y, c)
        out = []
        for a in range(n):
            out.append((a, me, a, me, (x, y, 1 - c)))
            out += [(a, me, a, me, (cx, cy, c)) for cx, cy in _chips(x, y)]
        return out

    def recv_slots(x, y, c):
        out = []
        for _ in range(n):
            out.append(_lin(x, y, 1 - c))
            out += [_lin(cx, cy, c) for cx, cy in _chips(x, y)]
        return out

    return plan, recv_slots


def _gather_plan_d2d(n):
    def plan(x, y, c):
        return [(a, _lin(cx, cy, c), a, _lin(cx, cy, c), (x, y, 1 - c)) for a in range(n) for cx, cy in _chips(x, y)]

    def recv_slots(x, y, c):
        return [_lin(cx, cy, 1 - c) for _ in range(n) for cx, cy in _chips(x, y)]

    return plan, recv_slots


def _all_gather_split(lands, tag, after):
    n = len(lands)
    plan1, slots1 = _gather_plan_ici(n)
    s1, r1, _, lands, token = _split_start(f"gather_{tag}_ici_start", [], lands, plan1, after, _SIBLING_AND_CHIPS)

    def finish(after):
        _, got = _split_wait(f"gather_{tag}_ici_wait", s1, r1, [], lands, plan1, slots1, after)
        plan2, slots2 = _gather_plan_d2d(n)
        s2, r2, _, got, token2 = _split_start(f"gather_{tag}_d2d_start", [], got, plan2, None, _SIBLING)
        return _split_wait(f"gather_{tag}_d2d_wait", s2, r2, [], got, plan2, slots2, token2)[1]

    return finish, token


def _all_gather_tree(lands, tag, after):
    n = len(lands)

    def plan1(x, y, c):
        me = _lin(x, y, c)
        return [(a, me, a, me, dev) for a in range(n) for dev in ((x, y, 1 - c), (1 - x, y, c), (x, 1 - y, c))]

    def slots1(x, y, c):
        return [s for _ in range(n) for s in (_lin(x, y, 1 - c), _lin(1 - x, y, c), _lin(x, 1 - y, c))]

    def plan2(x, y, c):
        from_x, from_y = _lin(1 - x, y, c), _lin(x, 1 - y, c)
        north = c == 1
        passed = jnp.where(north, from_x, from_y)
        onward = (jnp.where(north, x, 1 - x), jnp.where(north, 1 - y, y), c)
        sib = (x, y, 1 - c)
        return [cp for a in range(n) for cp in ((a, passed, a, passed, onward), (a, from_x, a, from_x, sib),
                                                (a, from_y, a, from_y, sib))]

    def slots2(x, y, c):
        return [s for _ in range(n) for s in (_lin(1 - x, 1 - y, c), _lin(1 - x, y, 1 - c), _lin(x, 1 - y, 1 - c))]

    def plan3(x, y, c):
        diag = _lin(1 - x, 1 - y, c)
        return [(a, diag, a, diag, (x, y, 1 - c)) for a in range(n)]

    def slots3(x, y, c):
        return [_lin(1 - x, 1 - y, 1 - c)] * n

    s1, r1, _, lands, token = _split_start(f"gather_{tag}_1_start", [], lands, plan1, after, _SIBLING_AND_NEIGHBOURS)
    state = {}

    def stage2(after):
        _, got = _split_wait(f"gather_{tag}_1_wait", s1, r1, [], lands, plan1, slots1, after)
        state["s"], state["r"], _, state["lands"], token2 = _split_start(f"gather_{tag}_2_start", [], got, plan2, None,
                                                                         _ONWARD_AND_SIBLING)
        return token2

    def stage3(after):
        _, got = _split_wait(f"gather_{tag}_2_wait", state["s"], state["r"], [], state["lands"], plan2, slots2, after)
        s3, r3, _, got, token3 = _split_start(f"gather_{tag}_3_start", [], got, plan3, None, _SIBLING)
        return _split_wait(f"gather_{tag}_3_wait", s3, r3, [], got, plan3, slots3, token3)[1]

    return stage2, stage3, token


_CHIP_LIST = ((0, 0), (0, 1), (1, 0), (1, 1))


def _reduce_plan_d2d(n):
    def plan(x, y, c):
        return [(a, _lin(qx, qy, 1 - c), a, q, (x, y, 1 - c)) for a in range(n) for q, (qx, qy) in enumerate(_CHIP_LIST)]

    def recv_slots(x, y, c):
        return [q for _ in range(n) for q in range(4)]

    return plan, recv_slots


def _reduce_plan_ici(n):
    def plan(x, y, c):
        return [(a, 2 * cx + cy, a, j, (cx, cy, c)) for a in range(n) for j, (cx, cy) in enumerate(_chips(x, y))]

    def recv_slots(x, y, c):
        return [j for _ in range(n) for j in range(3)]

    return plan, recv_slots


def _scatter_plan(n):
    def plan(x, y, c):
        return [(a, _lin(*peer), a, r, peer) for a in range(n) for r, peer in enumerate(_peers(x, y, c))]

    def recv_slots(x, y, c):
        return [r for _ in range(n) for r in range(N_DEV - 1)]

    return plan, recv_slots


def _broadcast_plan():
    def plan(x, y, c):
        return [(0, None, 0, _lin(x, y, c), peer) for peer in _peers(x, y, c)]

    def recv_slots(x, y, c):
        return [_lin(*peer) for peer in _peers(x, y, c)]

    return plan, recv_slots


def _chip_partial(grads, recvd, core, name):
    n = len(grads)

    def body(c_ref, *refs):
        for a in range(n):
            g_ref, r_ref, o_ref = refs[a], refs[n + a], refs[2 * n + a]
            o_ref[...] = (g_ref[...].astype(F32) + r_ref[...].astype(F32)).astype(o_ref.dtype)

    def blk(a, own):
        zeros = (0,) * (a.ndim - 1)
        return pl.BlockSpec((None,) + a.shape[1:],
                            (lambda q, c_ref: (2 * q + c_ref[0],) + zeros) if own else (lambda q, c_ref: (q,) + zeros))

    return pl.pallas_call(
        body, name=name,
        grid_spec=pltpu.PrefetchScalarGridSpec(
            num_scalar_prefetch=1, grid=(4,),
            in_specs=[blk(a, True) for a in grads] + [blk(a, False) for a in recvd],
            out_specs=[blk(a, False) for a in recvd]),
        out_shape=[jax.ShapeDtypeStruct(a.shape, a.dtype) for a in recvd],
        compiler_params=_params(("arbitrary",)),
    )(core, *grads, *recvd)


def _reduce_scatter_split(grads, tag, core, behind):
    n = len(grads)
    plan1, slots1 = _reduce_plan_d2d(n)
    lands1 = [lax.empty((4,) + a.shape[1:], a.dtype) for a in grads]
    s1, r1, srcs1, lands1, token1 = _split_start(f"reduce_{tag}_d2d_start", grads, lands1, plan1, None, _SIBLING)
    own, got = _split_wait(f"reduce_{tag}_d2d_wait", s1, r1, srcs1, lands1, plan1, slots1, behind(token1))
    parts = _chip_partial(own, got, core, f"reduce_{tag}_partial")
    plan2, slots2 = _reduce_plan_ici(n)
    lands2 = [lax.empty((3,) + a.shape[1:], a.dtype) for a in grads]
    s2, r2, srcs2, lands2, token2 = _split_start(f"reduce_{tag}_ici_start", parts, lands2, plan2, None, _CHIPS)

    def finish(after):
        return _split_wait(f"reduce_{tag}_ici_wait", s2, r2, srcs2, lands2, plan2, slots2, after)

    return finish, token2


def _place_shards(me, shards, dtypes, name, after=None):
    n = len(shards)

    def body(me_ref, *refs):
        for a in range(n):
            refs[n + a][...] = refs[a][...].astype(dtypes[a])

    full = lambda s: pl.BlockSpec(s.shape, lambda i, me_ref: (0,) * s.ndim)
    slot = lambda s: pl.BlockSpec((None,) + s.shape, lambda i, me_ref: (me_ref[0],) + (0,) * s.ndim)
    body, more_specs, more = _ordered_behind(body, 1 + n, after)
    return pl.pallas_call(
        body, name=name,
        grid_spec=pltpu.PrefetchScalarGridSpec(num_scalar_prefetch=1, grid=(1,),
                                               in_specs=[full(s) for s in shards] + more_specs,
                                               out_specs=[slot(s) for s in shards]),
        out_shape=[jax.ShapeDtypeStruct((N_DEV,) + s.shape, d) for s, d in zip(shards, dtypes)],
        compiler_params=_params(("arbitrary",)),
    )(me, *shards, *more)


def _mix_in_fwd(x, g1, w_in_t, conv_w, gq, gk, gconv):
    tm = 512
    n_t = SEQ // tm

    def body(x_ref, g1_ref, w_ref, cw_ref, gq_ref, gk_ref, gc_ref,
             proj_ref, u1_ref, ycn_ref, qn_ref, kn_ref, v_ref, halo_ref):
        @pl.when(pl.program_id(0) == 0)
        def _():
            halo_ref[...] = jnp.zeros_like(halo_ref)

        xv = x_ref[...]
        u = (xv * _rstd(xv) * g1_ref[...]).astype(BF16)
        u1_ref[...] = u
        proj = _dot_nt(u, w_ref[...])
        proj_ref[...] = proj
        gate_b = proj[:, 0:CONV_WIDTH]
        a = proj[:, CONV_WIDTH:2 * CONV_WIDTH] * proj[:, 2 * CONV_WIDTH:3 * CONV_WIDTH]
        cv, _, _ = _conv3(a, _taps(cw_ref[...]), halo_ref[...])
        halo_ref[...] = a[tm - SUBLANES:]
        yc = gate_b * cv
        ycn_ref[...] = (yc * _rstd(yc) * gc_ref[...]).astype(BF16)
        q0 = 3 * CONV_WIDTH
        qn_ref[...] = _head_norm(proj[:, q0:q0 + ATTN_WIDTH], gq_ref[...], N_HEADS).astype(BF16)
        k0 = q0 + ATTN_WIDTH
        kn_ref[...] = _head_norm(proj[:, k0:k0 + KV_WIDTH], gk_ref[...], 2).astype(BF16)
        v_ref[...] = proj[:, k0 + KV_WIDTH:k0 + 2 * KV_WIDTH].astype(BF16)

    const = lambda shape: pl.BlockSpec(shape, lambda i: (0,) * len(shape))
    rows = lambda w: pl.BlockSpec((tm, w), lambda i: (i, 0))
    return pl.pallas_call(
        body, name="mix_in_fwd", grid=(n_t,),
        in_specs=[rows(D_MODEL), const((1, D_MODEL)), const((IN_WIDTH, D_MODEL)), const((3, CONV_WIDTH)),
                  const((1, HEAD_DIM)), const((1, HEAD_DIM)), const((1, CONV_WIDTH))],
        out_specs=[rows(IN_WIDTH), rows(D_MODEL), rows(CONV_WIDTH), rows(ATTN_WIDTH), rows(KV_WIDTH), rows(KV_WIDTH)],
        out_shape=[jax.ShapeDtypeStruct((SEQ, IN_WIDTH), F32), jax.ShapeDtypeStruct((SEQ, D_MODEL), BF16),
                   jax.ShapeDtypeStruct((SEQ, CONV_WIDTH), BF16),
                   jax.ShapeDtypeStruct((SEQ, ATTN_WIDTH), BF16), jax.ShapeDtypeStruct((SEQ, KV_WIDTH), BF16),
                   jax.ShapeDtypeStruct((SEQ, KV_WIDTH), BF16)],
        scratch_shapes=[pltpu.VMEM((SUBLANES, CONV_WIDTH), F32)],
        compiler_params=_params(("arbitrary",)),
    )(x, g1, w_in_t, conv_w, gq, gk, gconv)


GROUP_ROWS = GQA_GROUP * BLK
QUERY_BLOCKS_PER_STEP = 2


def _band_bias(tbl_ref, bkt, bias_ref):
    for h in range(N_HEADS):
        acc = jnp.zeros(bkt.shape, F32)
        for b in range(NUM_BUCKETS):
            acc = jnp.where(bkt == b, tbl_ref[h, b], acc)
        bias_ref[h // GQA_GROUP, BLK * (h % GQA_GROUP):BLK * (h % GQA_GROUP + 1), :] = acc


def _band_masks(i):
    qi = lax.broadcasted_iota(jnp.int32, (GROUP_ROWS, BLK), 0) & (BLK - 1)
    ji = lax.broadcasted_iota(jnp.int32, (GROUP_ROWS, BLK), 1)
    upper = ji > qi
    return upper, upper & (i == 0)


def _stack_heads(x, g):
    return jnp.concatenate([x[:, HEAD_DIM * h:HEAD_DIM * (h + 1)] for h in range(GQA_GROUP * g, GQA_GROUP * (g + 1))], axis=0)


def _unstack_heads(groups):
    return jnp.concatenate([p[BLK * t:BLK * (t + 1)] for p in groups for t in range(GQA_GROUP)], axis=-1)


def _per_head_rows(vals):
    row = lax.broadcasted_iota(jnp.int32, (GROUP_ROWS, 1), 0)
    col = jnp.full((GROUP_ROWS, 1), vals[GQA_GROUP - 1], F32)
    for t in range(GQA_GROUP - 2, -1, -1):
        col = jnp.where(row < BLK * (t + 1), vals[t], col)
    return col


def _band_rows(ref, i):
    prev = pl.multiple_of(jnp.maximum(i - 1, 0) * BLK, BLK)
    cur = pl.multiple_of(i * BLK, BLK)
    return jnp.concatenate([ref[pl.ds(prev, BLK), :], ref[pl.ds(cur, BLK), :]], axis=0), prev, cur


def _fold(band, upper):
    return jnp.where(upper, band[:, :BLK], band[:, BLK:])


def _unfold(tile, upper):
    return jnp.concatenate([jnp.where(upper, tile, 0.0), jnp.where(upper, 0.0, tile)], axis=1)


def _head_probs(qh, kh, bias, upper, dead, sink):
    logits = _fold(_dot_nt(qh, kh), upper) * (HEAD_DIM ** -0.5) + bias
    logits = jnp.where(dead, NEG_INF, logits)
    m = jnp.maximum(jnp.max(logits, axis=-1, keepdims=True), sink)
    p = jnp.exp(logits - m)
    es = jnp.exp(sink - m)
    den = jnp.sum(p, axis=-1, keepdims=True) + es
    return p / den, es / den


def _attn_fwd(qn, kn, v, tbl, sinks, bkt, gattn, after=None):
    n_b = SEQ // BLK

    def body(q_ref, k_ref, v_ref, tbl_ref, sink_ref, bkt_ref, ga_ref, y_ref, yn_ref, p_ref, ps_ref, bias_ref):
        step = pl.program_id(0)

        @pl.when(step == 0)
        def _():
            _band_bias(tbl_ref, bkt_ref[...], bias_ref)

        lane = lax.broadcasted_iota(jnp.int32, (BLK, 128), 1)
        for b in range(QUERY_BLOCKS_PER_STEP):
            i = QUERY_BLOCKS_PER_STEP * step + b
            rows = slice(BLK * b, BLK * (b + 1))
            kb, _, _ = _band_rows(k_ref, i)
            vb, _, _ = _band_rows(v_ref, i)
            upper, dead = _band_masks(i)
            q = q_ref[rows, :]
            outs = []
            psinks = jnp.zeros((BLK, 128), F32)
            for g in range(N_HEADS // GQA_GROUP):
                kv = slice(HEAD_DIM * g, HEAD_DIM * (g + 1))
                sink = _per_head_rows([sink_ref[0, GQA_GROUP * g + t] for t in range(GQA_GROUP)])
                probs, psink = _head_probs(_stack_heads(q, g), kb[:, kv], bias_ref[g], upper, dead, sink)
                p_ref[b, g] = probs.astype(BF16)
                for t in range(GQA_GROUP):
                    psinks = jnp.where(lane == GQA_GROUP * g + t, psink[BLK * t:BLK * (t + 1)], psinks)
                outs.append(_dot(_unfold(probs, upper).astype(BF16), vb[:, kv]))
            ps_ref[rows, :] = psinks
            y = _unstack_heads(outs)
            y_ref[rows, :] = y
            yn_ref[rows, :] = (y * _rstd(y) * ga_ref[...]).astype(BF16)

    const = lambda shape: pl.BlockSpec(shape, lambda i: (0,) * len(shape))
    rows = lambda w: pl.BlockSpec((QUERY_BLOCKS_PER_STEP * BLK, w), lambda i: (i, 0))
    smem = pl.BlockSpec(memory_space=pltpu.SMEM)
    body, more_specs, more = _ordered_behind(body, 7, after)
    return pl.pallas_call(
        body, name="attn_fwd", grid=(n_b // QUERY_BLOCKS_PER_STEP,),
        in_specs=[rows(ATTN_WIDTH), const((SEQ, KV_WIDTH)), const((SEQ, KV_WIDTH)), smem, smem,
                  const((BLK, BLK)), const((1, ATTN_WIDTH))] + more_specs,
        out_specs=[rows(ATTN_WIDTH), rows(ATTN_WIDTH),
                   pl.BlockSpec((QUERY_BLOCKS_PER_STEP, N_HEADS // GQA_GROUP, GROUP_ROWS, BLK), lambda i: (i, 0, 0, 0)),
                   rows(128)],
        out_shape=[jax.ShapeDtypeStruct((SEQ, ATTN_WIDTH), F32), jax.ShapeDtypeStruct((SEQ, ATTN_WIDTH), BF16),
                   jax.ShapeDtypeStruct((n_b, N_HEADS // GQA_GROUP, GROUP_ROWS, BLK), BF16),
                   jax.ShapeDtypeStruct((SEQ, 128), F32)],
        scratch_shapes=[pltpu.VMEM((N_HEADS // GQA_GROUP, GROUP_ROWS, BLK), F32)],
        compiler_params=_params(("arbitrary",)),
    )(qn, kn, v, tbl, sinks, bkt, gattn, *more)


def _ffn_block(i, step):
    return jnp.where(i % 2 == 0, step, N_FFN_BLK - 1 - step)


def _late_tile(i, step):
    return jnp.where(step == 0, jnp.maximum(i - 1, 0), i)


def _ffn_fwd(x, ycn, yan, w_out, g2, w_up, fcw, fcb, w_down, tgt):
    tm = 512
    n_t = SEQ // tm

    def body(x_ref, ycn_ref, yan_ref, wo_ref, g2_ref, wu_ref, cw_ref, b_ref, wd_ref, tgt_ref,
             h1_ref, u2_ref, up_ref, pre_ref, act_ref, dh2_ref, dh2b_ref, loss_ref, acc_ref, halo_ref):
        i, step = pl.program_id(0), pl.program_id(1)
        j = _ffn_block(i, step)

        @pl.when((i == 0) & (step == 0))
        def _():
            loss_ref[...] = jnp.zeros_like(loss_ref)

        @pl.when(step == 0)
        def _():
            h1 = x_ref[...] + _dot(ycn_ref[...], wo_ref[0:CONV_WIDTH, :]) + _dot(yan_ref[...], wo_ref[CONV_WIDTH:, :])
            h1_ref[...] = h1
            u2_ref[...] = (h1 * _rstd(h1) * g2_ref[...]).astype(BF16)
            acc_ref[...] = jnp.zeros_like(acc_ref)

        u2 = u2_ref[...]
        pre = []
        for s in range(2):
            up = _dot_nt(u2, wu_ref[s])
            up_ref[s] = up.astype(BF16)
            halo = jnp.where(i == 0, 0.0, halo_ref[s, j])
            pre.append(_conv3(up, _taps(cw_ref.at[s]), halo)[0] + b_ref[s])
            pre_ref[s] = pre[s].astype(BF16)
            halo_ref[s, j] = up[tm - SUBLANES:]
        g, val = pre
        act = (g * jax.nn.sigmoid(g) * val).astype(BF16)
        act_ref[...] = act
        acc_ref[...] += _dot(act, wd_ref[...])

        @pl.when(step == N_FFN_BLK - 1)
        def _():
            err = h1_ref[...] + acc_ref[...] - tgt_ref[...]
            loss_ref[...] += 0.5 * jnp.sum(err * err) / D_MODEL
            dh2 = err / D_MODEL
            dh2_ref[...] = dh2
            dh2b_ref[...] = dh2.astype(BF16)

    rows = lambda w: pl.BlockSpec((tm, w), lambda i, step: (i, 0))
    late_rows = lambda w: pl.BlockSpec((tm, w), lambda i, step: (_late_tile(i, step), 0))
    const = lambda shape: pl.BlockSpec(shape, lambda i, step: (0,) * len(shape))
    pair = lambda *s: pl.BlockSpec((2, None) + s, lambda i, step: (0, _ffn_block(i, step)) + (0,) * len(s))
    upb = pl.BlockSpec((2, None, tm, FFN_BLK), lambda i, step: (0, _ffn_block(i, step), i, 0))
    return pl.pallas_call(
        body, name="ffn_fwd", grid=(n_t, N_FFN_BLK),
        in_specs=[rows(D_MODEL), rows(CONV_WIDTH), rows(ATTN_WIDTH), const((D_MODEL, D_MODEL)), const((1, D_MODEL)),
                  pair(FFN_BLK, D_MODEL), pair(3, 1, FFN_BLK), pair(1, FFN_BLK),
                  pl.BlockSpec((None, FFN_BLK, D_MODEL), lambda i, step: (_ffn_block(i, step), 0, 0)), late_rows(D_MODEL)],
        out_specs=[rows(D_MODEL), rows(D_MODEL), upb, upb,
                   pl.BlockSpec((None, tm, FFN_BLK), lambda i, step: (_ffn_block(i, step), i, 0)),
                   rows(D_MODEL), rows(D_MODEL), const((SUBLANES, 128))],
        out_shape=[jax.ShapeDtypeStruct((SEQ, D_MODEL), F32), jax.ShapeDtypeStruct((SEQ, D_MODEL), BF16),
                   jax.ShapeDtypeStruct((2, N_FFN_BLK, SEQ, FFN_BLK), BF16),
                   jax.ShapeDtypeStruct((2, N_FFN_BLK, SEQ, FFN_BLK), BF16),
                   jax.ShapeDtypeStruct((N_FFN_BLK, SEQ, FFN_BLK), BF16),
                   jax.ShapeDtypeStruct((SEQ, D_MODEL), F32), jax.ShapeDtypeStruct((SEQ, D_MODEL), BF16),
                   jax.ShapeDtypeStruct((SUBLANES, 128), F32)],
        scratch_shapes=[pltpu.VMEM((tm, D_MODEL), F32), pltpu.VMEM((2, N_FFN_BLK, SUBLANES, FFN_BLK), F32)],
        compiler_params=_params(("arbitrary", "arbitrary")),
    )(x, ycn, yan, w_out, g2, w_up, fcw, fcb, w_down, tgt)


def _ffn_bwd(dh2, dh2b, h1, g2, up, pre, w_up, fcw, w_down, after=None):
    tm = 512
    units = ((288, 224), (0, 288))
    n_t = SEQ // tm

    def body(dh2_ref, dh2b_ref, h1_ref, g2_ref, up_ref, pre_ref, wu_ref, cw_ref, wd_ref,
             dup_ref, dh1_ref, dh1b_ref, dfb_ref, dfcw_ref, dg2_ref, acc_ref, next_ref):
        i, step = pl.program_id(0), pl.program_id(1)
        j = _ffn_block(i, step)

        @pl.when((i == 0) & (step == 0))
        def _():
            dfb_ref[...] = jnp.zeros_like(dfb_ref)
            dfcw_ref[...] = jnp.zeros_like(dfcw_ref)
            dg2_ref[...] = jnp.zeros_like(dg2_ref)

        @pl.when(step == 0)
        def _():
            acc_ref[...] = jnp.zeros_like(acc_ref)

        nxt = [jnp.where(i == 0, 0.0, next_ref[s, j]) for s in range(2)]
        sums = [[0.0] * 4 for _ in range(2)]
        for r0, rn in units:
            rows = slice(r0, r0 + rn)
            g, val = pre_ref[0, rows, :].astype(F32), pre_ref[1, rows, :].astype(F32)
            sg = jax.nn.sigmoid(g)
            silu = g * sg
            dact = _dot_nt(dh2b_ref[rows, :], wd_ref[...])
            dpre = (dact * val * (sg * (1.0 + g * (1.0 - sg))), dact * silu)
            dups = []
            for s in range(2):
                d = dpre[s]
                u = up_ref[s, rows, :].astype(F32)
                w = _taps(cw_ref.at[s])
                d1 = _shift_up(d, 1, nxt[s])
                d2 = _shift_up(d, 2, nxt[s])
                nxt[s] = d[:SUBLANES]
                for t, term in enumerate((d, d2 * u, d1 * u, d * u)):
                    sums[s][t] = sums[s][t] + jnp.sum(term, axis=0, keepdims=True)
                dups.append((d * w[2] + d1 * w[1] + d2 * w[0]).astype(BF16))
                dup_ref[s, rows, :] = dups[s]
            acc_ref[rows, :] += _dot(dups[0], wu_ref[0]) + _dot(dups[1], wu_ref[1])
        for s in range(2):
            next_ref[s, j] = nxt[s]
            dfb_ref[s, j] += sums[s][0]
            for t in range(3):
                dfcw_ref[s, j, t] += sums[s][1 + t]

        @pl.when(step == N_FFN_BLK - 1)
        def _():
            dn, dgain = _rms_bwd(h1_ref[...], g2_ref[...], acc_ref[...])
            dh1 = dh2_ref[...] + dn
            dh1_ref[...] = dh1
            dh1b_ref[...] = dh1.astype(BF16)
            dg2_ref[...] += dgain

    rev = lambda i: n_t - 1 - i
    rows = lambda w: pl.BlockSpec((tm, w), lambda i, step: (rev(i), 0))
    late_rows = lambda w: pl.BlockSpec((tm, w), lambda i, step: (rev(_late_tile(i, step)), 0))
    const = lambda shape: pl.BlockSpec(shape, lambda i, step: (0,) * len(shape))
    pair = lambda *s: pl.BlockSpec((2, None) + s, lambda i, step: (0, _ffn_block(i, step)) + (0,) * len(s))
    upb = pl.BlockSpec((2, None, tm, FFN_BLK), lambda i, step: (0, _ffn_block(i, step), rev(i), 0))
    body, more_specs, more = _ordered_behind(body, 9, after)
    return pl.pallas_call(
        body, name="ffn_bwd", grid=(n_t, N_FFN_BLK),
        in_specs=[late_rows(D_MODEL), rows(D_MODEL), late_rows(D_MODEL), const((1, D_MODEL)), upb, upb,
                  pair(FFN_BLK, D_MODEL), pair(3, 1, FFN_BLK),
                  pl.BlockSpec((None, FFN_BLK, D_MODEL), lambda i, step: (_ffn_block(i, step), 0, 0))] + more_specs,
        out_specs=[upb, rows(D_MODEL), rows(D_MODEL),
                   const((2, N_FFN_BLK, 1, FFN_BLK)), const((2, N_FFN_BLK, 3, 1, FFN_BLK)), const((1, D_MODEL))],
        out_shape=[jax.ShapeDtypeStruct((2, N_FFN_BLK, SEQ, FFN_BLK), BF16), jax.ShapeDtypeStruct((SEQ, D_MODEL), F32),
                   jax.ShapeDtypeStruct((SEQ, D_MODEL), BF16), jax.ShapeDtypeStruct((2, N_FFN_BLK, 1, FFN_BLK), F32),
                   jax.ShapeDtypeStruct((2, N_FFN_BLK, 3, 1, FFN_BLK), F32), jax.ShapeDtypeStruct((1, D_MODEL), F32)],
        scratch_shapes=[pltpu.VMEM((tm, D_MODEL), F32), pltpu.VMEM((2, N_FFN_BLK, SUBLANES, FFN_BLK), F32)],
        compiler_params=_params(("arbitrary", "arbitrary")),
    )(dh2, dh2b, h1, g2, up, pre, w_up, fcw, w_down, *more)


def _grad_tn(a_list, b, out_rows, name, after=None):
    n = len(a_list)
    ncol = b.shape[1]

    def body(*refs):
        a_refs, b_ref, o_ref = refs[:n], refs[n], refs[n + 1]
        j = pl.program_id(0)
        for k in range(n):
            @pl.when(j == k)
            def _(k=k):
                o_ref[...] = _dot_tn(a_refs[k][...], b_ref[...]).astype(BF16)

    full = lambda shape: pl.BlockSpec(shape, lambda j: (0,) * len(shape))
    body, more_specs, more = _ordered_behind(body, n + 1, after)
    return pl.pallas_call(
        body, name=name, grid=(n,),
        in_specs=[full((SEQ, out_rows))] * n + [full((SEQ, ncol))] + more_specs,
        out_specs=pl.BlockSpec((None, out_rows, ncol), lambda j: (j, 0, 0)),
        out_shape=jax.ShapeDtypeStruct((n, out_rows, ncol), BF16),
        compiler_params=_params(("arbitrary",)),
    )(*a_list, b, *more)


def _grad_tn_blocked(a, b, name, per_step=2):
    nb, _, a_w = a.shape
    b_w = b.shape[-1]

    def body(a_ref, b_ref, o_ref):
        for p in range(per_step):
            o_ref[p] = _dot_tn(a_ref[p], b_ref[...]).astype(BF16)

    return pl.pallas_call(
        body, name=name, grid=(nb // per_step,),
        in_specs=[pl.BlockSpec((per_step, SEQ, a_w), lambda k: (k, 0, 0)), pl.BlockSpec((SEQ, b_w), lambda k: (0, 0))],
        out_specs=pl.BlockSpec((per_step, a_w, b_w), lambda k: (k, 0, 0)),
        out_shape=jax.ShapeDtypeStruct((nb, a_w, b_w), BF16),
        compiler_params=_params(("arbitrary",)),
    )(a, b)


def _out_bwd(dh1b, w_out, y_attn, gattn, after=None):
    tm = 1024
    n_t = SEQ // tm

    def body(dh_ref, wo_ref, y_ref, ga_ref, dycn_ref, dy_ref, dga_ref):
        @pl.when(pl.program_id(0) == 0)
        def _():
            dga_ref[...] = jnp.zeros_like(dga_ref)

        dycat = _dot_nt(dh_ref[...], wo_ref[...])
        dycn_ref[...] = dycat[:, :CONV_WIDTH]
        dy, dga = _rms_bwd(y_ref[...], ga_ref[...], dycat[:, CONV_WIDTH:])
        dy_ref[...] = dy
        dga_ref[...] += dga

    rows = lambda w: pl.BlockSpec((tm, w), lambda i: (i, 0))
    const = lambda shape: pl.BlockSpec(shape, lambda i: (0,) * len(shape))
    body, more_specs, more = _ordered_behind(body, 4, after)
    return pl.pallas_call(
        body, name="out_bwd", grid=(n_t,),
        in_specs=[rows(D_MODEL), const((D_MODEL, D_MODEL)), rows(ATTN_WIDTH), const((1, ATTN_WIDTH))] + more_specs,
        out_specs=[rows(CONV_WIDTH), rows(ATTN_WIDTH), const((1, ATTN_WIDTH))],
        out_shape=[jax.ShapeDtypeStruct((SEQ, CONV_WIDTH), F32), jax.ShapeDtypeStruct((SEQ, ATTN_WIDTH), F32),
                   jax.ShapeDtypeStruct((1, ATTN_WIDTH), F32)],
        compiler_params=_params(("arbitrary",)),
    )(dh1b, w_out, y_attn, gattn, *more)


def _attn_bwd(qn, kn, v, dy, probs, psinks, bkt, after=None):
    n_b = SEQ // BLK

    def body(q_ref, k_ref, v_ref, dy_ref, p_ref, ps_ref, bkt_ref,
             dq_ref, dk_ref, dv_ref, dtbl_ref, dsink_ref, dbias_ref, dsacc_ref):
        step = pl.program_id(0)

        @pl.when(step == 0)
        def _():
            dbias_ref[...] = jnp.zeros_like(dbias_ref)
            dsacc_ref[...] = jnp.zeros_like(dsacc_ref)
            dk_ref[...] = jnp.zeros_like(dk_ref)
            dv_ref[...] = jnp.zeros_like(dv_ref)

        lane = lax.broadcasted_iota(jnp.int32, (BLK, 128), 1)
        for blk in range(QUERY_BLOCKS_PER_STEP):
            i = QUERY_BLOCKS_PER_STEP * step + blk
            rows = slice(BLK * blk, BLK * (blk + 1))
            kb, prev, cur = _band_rows(k_ref, i)
            vb, _, _ = _band_rows(v_ref, i)
            upper, _ = _band_masks(i)
            q = q_ref[rows, :]
            dy = dy_ref[rows, :]
            psink = ps_ref[rows, :]
            dsink = jnp.zeros((BLK, 128), F32)
            dqs, dks, dvs = [], [], []
            for g in range(N_HEADS // GQA_GROUP):
                kv = slice(HEAD_DIM * g, HEAD_DIM * (g + 1))
                qg = _stack_heads(q, g)
                dog = _stack_heads(dy, g).astype(BF16)
                pb = p_ref[blk, g]
                pg = pb.astype(F32)
                dprobs = _fold(_dot_nt(dog, vb[:, kv]), upper)
                dvs.append(_dot_tn(_unfold(pb, upper), dog))
                dsum = jnp.sum(pg * dprobs, axis=-1, keepdims=True)
                dlogits = pg * (dprobs - dsum)
                for t in range(GQA_GROUP):
                    dsink = jnp.where(lane == GQA_GROUP * g + t, -psink * dsum[BLK * t:BLK * (t + 1)], dsink)
                dbias_ref[g] += dlogits
                ds = _unfold(dlogits * (HEAD_DIM ** -0.5), upper).astype(BF16)
                dqs.append(_dot(ds, kb[:, kv]))
                dks.append(_dot_tn(ds, qg))
            dsacc_ref[...] += dsink
            dq_ref[rows, :] = _unstack_heads(dqs)
            dkb = jnp.concatenate(dks, axis=-1)
            dvb = jnp.concatenate(dvs, axis=-1)
            dk_ref[pl.ds(prev, BLK), :] += dkb[:BLK]
            dk_ref[pl.ds(cur, BLK), :] += dkb[BLK:]
            dv_ref[pl.ds(prev, BLK), :] += dvb[:BLK]
            dv_ref[pl.ds(cur, BLK), :] += dvb[BLK:]

        @pl.when(step == n_b // QUERY_BLOCKS_PER_STEP - 1)
        def _():
            bkt = bkt_ref[...]
            row8 = lax.broadcasted_iota(jnp.int32, (N_HEADS, 128), 0)
            lane8 = lax.broadcasted_iota(jnp.int32, (N_HEADS, 128), 1)
            acc = jnp.zeros((N_HEADS, 128), F32)
            for h in range(N_HEADS):
                rows = slice(BLK * (h % GQA_GROUP), BLK * (h % GQA_GROUP + 1))
                dbh = dbias_ref[h // GQA_GROUP, rows, :]
                for b in range(NUM_BUCKETS):
                    acc = jnp.where((row8 == h) & (lane8 == b), jnp.sum(jnp.where(bkt == b, dbh, 0.0)), acc)
            dsink_ref[...] = jnp.sum(dsacc_ref[...], axis=0, keepdims=True)
            dtbl_ref[...] = acc

    const = lambda shape: pl.BlockSpec(shape, lambda i: (0,) * len(shape))
    rows = lambda w: pl.BlockSpec((QUERY_BLOCKS_PER_STEP * BLK, w), lambda i: (i, 0))
    n_g = N_HEADS // GQA_GROUP
    body, more_specs, more = _ordered_behind(body, 7, after)
    return pl.pallas_call(
        body, name="attn_bwd", grid=(n_b // QUERY_BLOCKS_PER_STEP,),
        in_specs=[rows(ATTN_WIDTH), const((SEQ, KV_WIDTH)), const((SEQ, KV_WIDTH)), rows(ATTN_WIDTH),
                  pl.BlockSpec((QUERY_BLOCKS_PER_STEP, n_g, GROUP_ROWS, BLK), lambda i: (i, 0, 0, 0)), rows(128),
                  const((BLK, BLK))] + more_specs,
        out_specs=[rows(ATTN_WIDTH), const((SEQ, KV_WIDTH)), const((SEQ, KV_WIDTH)), const((N_HEADS, 128)), const((1, 128))],
        out_shape=[jax.ShapeDtypeStruct((SEQ, ATTN_WIDTH), F32), jax.ShapeDtypeStruct((SEQ, KV_WIDTH), F32),
                   jax.ShapeDtypeStruct((SEQ, KV_WIDTH), F32), jax.ShapeDtypeStruct((N_HEADS, 128), F32),
                   jax.ShapeDtypeStruct((1, 128), F32)],
        scratch_shapes=[pltpu.VMEM((n_g, GROUP_ROWS, BLK), F32), pltpu.VMEM((BLK, 128), F32)],
        compiler_params=_params(("arbitrary",)),
    )(qn, kn, v, dy, probs, psinks, bkt, *more)


def _mix_in_bwd(x, dh1, proj, dycn, dqn, dkn, dv, w_in_t, conv_w, g1, gq, gk, gconv):
    tm = 512
    n_t = SEQ // tm
    halo_blocks = tm // SUBLANES

    def body(x_ref, dh1_ref, proj_ref, halo_ref, dycn_ref, dqn_ref, dkn_ref, dv_ref, w_ref, cw_ref,
             g1_ref, gq_ref, gk_ref, gc_ref,
             dx_ref, dproj_ref, dcw_ref, dgc_ref, dgq_ref, dgk_ref, dg1_ref, next_ref):
        i = pl.program_id(0)
        first_tile = i == n_t - 1

        @pl.when(i == 0)
        def _():
            for r in (dcw_ref, dgc_ref, dgq_ref, dgk_ref, dg1_ref, next_ref):
                r[...] = jnp.zeros_like(r)

        proj = proj_ref[...]
        hp = halo_ref[...]
        gate_b = proj[:, 0:CONV_WIDTH]
        gate_c = proj[:, CONV_WIDTH:2 * CONV_WIDTH]
        hc = proj[:, 2 * CONV_WIDTH:3 * CONV_WIDTH]
        a = gate_c * hc
        a_halo = jnp.where(first_tile, 0.0, hp[:, CONV_WIDTH:2 * CONV_WIDTH] * hp[:, 2 * CONV_WIDTH:3 * CONV_WIDTH])
        cw = _taps(cw_ref[...])
        cv, a2, a1 = _conv3(a, cw, a_halo)
        dyc, dgc = _rms_bwd(gate_b * cv, gc_ref[...], dycn_ref[...])
        dgc_ref[...] += dgc
        dcv = dyc * gate_b
        dcw_ref[...] += jnp.concatenate(
            [jnp.sum(dcv * a2, axis=0, keepdims=True), jnp.sum(dcv * a1, axis=0, keepdims=True),
             jnp.sum(dcv * a, axis=0, keepdims=True)], axis=0)
        da = _conv3_bwd_input(dcv, cw, next_ref[...])
        next_ref[...] = dcv[:SUBLANES]
        q0 = 3 * CONV_WIDTH
        k0 = q0 + ATTN_WIDTH
        dq, dgq = _head_norm_bwd(proj[:, q0:k0], gq_ref[...], dqn_ref[...], N_HEADS)
        dk, dgk = _head_norm_bwd(proj[:, k0:k0 + KV_WIDTH], gk_ref[...], dkn_ref[...], 2)
        dgq_ref[...] += dgq
        dgk_ref[...] += dgk
        dproj = jnp.concatenate([dyc * cv, da * hc, da * gate_c, dq, dk, dv_ref[...]], axis=-1).astype(BF16)
        dproj_ref[...] = dproj
        du1 = _dot(dproj, w_ref[...])
        xv = x_ref[...]
        dn, dg1 = _rms_bwd(xv, g1_ref[...], du1)
        dx_ref[...] = dh1_ref[...] + dn
        dg1_ref[...] += dg1

    rev = lambda i: n_t - 1 - i
    rows = lambda w: pl.BlockSpec((tm, w), lambda i: (rev(i), 0))
    const = lambda shape: pl.BlockSpec(shape, lambda i: (0,) * len(shape))
    halo = pl.BlockSpec((SUBLANES, IN_WIDTH), lambda i: (jnp.maximum(rev(i) * halo_blocks - 1, 0), 0))
    return pl.pallas_call(
        body, name="mix_in_bwd", grid=(n_t,),
        in_specs=[rows(D_MODEL), rows(D_MODEL), rows(IN_WIDTH), halo, rows(CONV_WIDTH), rows(ATTN_WIDTH), rows(KV_WIDTH),
                  rows(KV_WIDTH), const((IN_WIDTH, D_MODEL)), const((3, CONV_WIDTH)), const((1, D_MODEL)),
                  const((1, HEAD_DIM)), const((1, HEAD_DIM)), const((1, CONV_WIDTH))],
        out_specs=[rows(D_MODEL), rows(IN_WIDTH), const((3, CONV_WIDTH)), const((1, CONV_WIDTH)),
                   const((1, HEAD_DIM)), const((1, HEAD_DIM)), const((1, D_MODEL))],
        out_shape=[jax.ShapeDtypeStruct((SEQ, D_MODEL), F32), jax.ShapeDtypeStruct((SEQ, IN_WIDTH), BF16),
                   jax.ShapeDtypeStruct((3, CONV_WIDTH), F32),
                   jax.ShapeDtypeStruct((1, CONV_WIDTH), F32), jax.ShapeDtypeStruct((1, HEAD_DIM), F32),
                   jax.ShapeDtypeStruct((1, HEAD_DIM), F32), jax.ShapeDtypeStruct((1, D_MODEL), F32)],
        scratch_shapes=[pltpu.VMEM((SUBLANES, CONV_WIDTH), F32)],
        compiler_params=_params(("arbitrary",)),
    )(x, dh1, proj, proj, dycn, dqn, dkn, dv, w_in_t, conv_w, g1, gq, gk, gconv)


def _grad_w_in(dproj, u1, after=None):
    bw = 768

    def body(a_ref, b_ref, o_ref):
        o_ref[...] = _dot_tn(a_ref[...], b_ref[...]).astype(BF16)

    body, more_specs, more = _ordered_behind(body, 2, after)
    return pl.pallas_call(
        body, name="grad_w_in", grid=(IN_WIDTH // bw,),
        in_specs=[pl.BlockSpec((SEQ, bw), lambda k: (0, k)), pl.BlockSpec((SEQ, D_MODEL), lambda k: (0, 0))] + more_specs,
        out_specs=pl.BlockSpec((bw, D_MODEL), lambda k: (k, 0)),
        out_shape=jax.ShapeDtypeStruct((IN_WIDTH, D_MODEL), BF16),
        compiler_params=_params(("arbitrary",)),
    )(dproj, u1, *more)


def _adamw_math(w, g, m, v):
    m = ADAM_B1 * m + (1.0 - ADAM_B1) * g
    v = ADAM_B2 * v + (1.0 - ADAM_B2) * (g * g)
    m_hat = m / (1.0 - ADAM_B1 ** ADAM_STEP)
    v_hat = v / (1.0 - ADAM_B2 ** ADAM_STEP)
    return -ADAM_LR * (m_hat / (jnp.sqrt(v_hat) + ADAM_EPS) + ADAM_WD * w), m, v


_ROW_G1, _ROW_G2, _ROW_OUT_NORMS, _ROW_FFN_B, _ROW_GQ, _ROW_GK, _ROW_SINKS, _ROW_LOSS, _ROW_TABLE = 0, 1, 2, 3, 11, 12, 13, 14, 16
SMALL_ROWS, SMALL_COLS = 24, 1024
_SMALL_NAMES = ("norm_mix_g", "norm_ffn_g", "out_norm_conv_g", "out_norm_attn_g", "ffn_conv_b", "q_norm_g", "k_norm_g",
                "sinks", "rel_bias_table")


def _pack_small_grads(dg1, dg2, dgconv, dgattn, dfb, dgq, dgk, dsinks, dtbl_t, loss_acc):
    def body(dg1_ref, dg2_ref, dgc_ref, dga_ref, dfb_ref, dgq_ref, dgk_ref, ds_ref, dt_ref, loss_ref, o_ref, all_ref):
        o_ref[...] = jnp.zeros_like(o_ref)
        o_ref[_ROW_G1:_ROW_G1 + 1, :] = dg1_ref[...]
        o_ref[_ROW_G2:_ROW_G2 + 1, :] = dg2_ref[...]
        o_ref[_ROW_OUT_NORMS:_ROW_OUT_NORMS + 1, 0:CONV_WIDTH] = dgc_ref[...]
        o_ref[_ROW_OUT_NORMS:_ROW_OUT_NORMS + 1, CONV_WIDTH:] = dga_ref[...]
        for k in range(N_DEV):
            o_ref[_ROW_FFN_B + k:_ROW_FFN_B + k + 1, 0:FFN_BLK] = dfb_ref[k // N_FFN_BLK, k % N_FFN_BLK]
        o_ref[_ROW_GQ:_ROW_GQ + 1, 0:HEAD_DIM] = dgq_ref[...]
        o_ref[_ROW_GK:_ROW_GK + 1, 0:HEAD_DIM] = dgk_ref[...]
        o_ref[_ROW_SINKS:_ROW_SINKS + 1, 0:128] = ds_ref[...]
        o_ref[_ROW_LOSS:_ROW_LOSS + 1, 0:128] = loss_ref[0:1, :]
        o_ref[_ROW_TABLE:_ROW_TABLE + N_HEADS, 0:128] = dt_ref[...]
        for s in range(N_DEV):
            all_ref[s] = o_ref[...]

    return pl.pallas_call(
        body, name="pack_small_grads",
        out_shape=[jax.ShapeDtypeStruct((SMALL_ROWS, SMALL_COLS), F32),
                   jax.ShapeDtypeStruct((N_DEV, SMALL_ROWS, SMALL_COLS), F32)],
    )(dg1, dg2, dgconv, dgattn, dfb, dgq, dgk, dsinks, dtbl_t, loss_acc)


def _adamw_small(recv, params, after):
    names = _SMALL_NAMES
    n = len(names)

    def grad_of(g, name, k=None):
        if name == "norm_mix_g":
            return g[_ROW_G1:_ROW_G1 + 1, :]
        if name == "norm_ffn_g":
            return g[_ROW_G2:_ROW_G2 + 1, :]
        if name == "out_norm_conv_g":
            return g[_ROW_OUT_NORMS:_ROW_OUT_NORMS + 1, 0:CONV_WIDTH]
        if name == "out_norm_attn_g":
            return g[_ROW_OUT_NORMS:_ROW_OUT_NORMS + 1, CONV_WIDTH:]
        if name == "ffn_conv_b":
            return g[_ROW_FFN_B + k:_ROW_FFN_B + k + 1, 0:FFN_BLK]
        if name == "q_norm_g":
            return g[_ROW_GQ:_ROW_GQ + 1, 0:HEAD_DIM]
        if name == "k_norm_g":
            return g[_ROW_GK:_ROW_GK + 1, 0:HEAD_DIM]
        if name == "sinks":
            return g[_ROW_SINKS:_ROW_SINKS + 1, 0:N_HEADS]
        return g[_ROW_TABLE:_ROW_TABLE + N_HEADS, 0:NUM_BUCKETS]

    def body(r_ref, *refs):
        ins, outs, loss_ref = refs[:3 * n], refs[3 * n:7 * n], refs[7 * n]
        g = r_ref[0]
        for s in range(1, N_DEV):
            g = g + r_ref[s]
        loss_ref[...] = g[_ROW_LOSS:_ROW_LOSS + 1, 0:128]
        for i, name in enumerate(names):
            w_ref, m_ref, v_ref = ins[3 * i:3 * i + 3]
            o = outs[4 * i:4 * i + 4]
            cols = [slice(FFN_BLK * k, FFN_BLK * (k + 1)) for k in range(N_DEV)] if name == "ffn_conv_b" else [slice(None)]
            for k, cs in enumerate(cols):
                gk = grad_of(g, name, k)
                d, m2, v2 = _adamw_math(w_ref[:, cs], gk, m_ref[:, cs], v_ref[:, cs])
                o[0][:, cs], o[1][:, cs], o[2][:, cs], o[3][:, cs] = gk, d, m2, v2

    flat = [a for name in names for a in params[name]]
    body, more_specs, more = _ordered_behind(body, 1 + 3 * n, after)
    vmem = pl.BlockSpec(memory_space=pltpu.VMEM)
    out = pl.pallas_call(
        body, name="adamw_small",
        in_specs=[vmem] * (1 + 3 * n) + more_specs,
        out_shape=[jax.ShapeDtypeStruct(params[name][0].shape, F32) for name in names for _ in range(4)]
        + [jax.ShapeDtypeStruct((1, 128), F32)],
        compiler_params=pltpu.CompilerParams(vmem_limit_bytes=VMEM_LIMIT),
    )(recv, *flat, *more)
    return {name: tuple(out[4 * i:4 * i + 4]) for i, name in enumerate(names)}, out[4 * n]


def _adamw_direct(w, m, v, own, recv, me, name, row_blocks=1, after=None):
    rb = w.shape[0] // row_blocks
    cols = w.shape[1]

    def body(me_ref, w_ref, m_ref, v_ref, o_ref, r_ref, g_o, d_o, m_o, v_o):
        g = o_ref[...].astype(F32)
        for s in range(N_DEV - 1):
            g = g + r_ref[s].astype(F32)
        g_o[...] = g
        d_o[...], m_o[...], v_o[...] = _adamw_math(w_ref[...], g, m_ref[...], v_ref[...])

    blk = pl.BlockSpec((rb, cols), lambda i, me_ref: (i, 0))
    oblk = pl.BlockSpec((None, rb, cols), lambda i, me_ref: (me_ref[0], i, 0))
    rblk = pl.BlockSpec((N_DEV - 1, rb, cols), lambda i, me_ref: (0, i, 0))
    body, more_specs, more = _ordered_behind(body, 6, after)
    return pl.pallas_call(
        body, name=name,
        grid_spec=pltpu.PrefetchScalarGridSpec(num_scalar_prefetch=1, grid=(row_blocks,),
                                               in_specs=[blk, blk, blk, oblk, rblk] + more_specs, out_specs=[blk] * 4),
        out_shape=[jax.ShapeDtypeStruct(w.shape, F32)] * 4,
        compiler_params=_params(("arbitrary",)),
    )(me, w, m, v, own, recv, *more)


def _adamw(w, m, v, part, recv, chip, name, row_blocks=1, after=None):
    rb = w.shape[0] // row_blocks
    tail = w.shape[1:]
    zeros = (0,) * len(tail)

    def body(chip_ref, w_ref, m_ref, v_ref, p_ref, r_ref, g_o, d_o, m_o, v_o):
        g = p_ref[...].astype(F32)
        for s in range(3):
            g = g + r_ref[s].astype(F32)
        g_o[...] = g
        d_o[...], m_o[...], v_o[...] = _adamw_math(w_ref[...], g, m_ref[...], v_ref[...])

    blk = pl.BlockSpec((rb,) + tail, lambda i, chip_ref: (i,) + zeros)
    pblk = pl.BlockSpec((None, rb) + tail, lambda i, chip_ref: (chip_ref[0], i) + zeros)
    rblk = pl.BlockSpec((3, rb) + tail, lambda i, chip_ref: (0, i) + zeros)
    body, more_specs, more = _ordered_behind(body, 6, after)
    return pl.pallas_call(
        body, name=name,
        grid_spec=pltpu.PrefetchScalarGridSpec(num_scalar_prefetch=1, grid=(row_blocks,),
                                               in_specs=[blk, blk, blk, pblk, rblk] + more_specs, out_specs=[blk] * 4),
        out_shape=[jax.ShapeDtypeStruct(w.shape, F32)] * 4,
        compiler_params=_params(("arbitrary",)),
    )(chip, w, m, v, part, recv, *more)


def kernel(x, norm_mix_g, w_in, conv_w, q_norm_g, k_norm_g, rel_bias_table, sinks, out_norm_conv_g, out_norm_attn_g, w_out, norm_ffn_g, w_up, ffn_conv_w, ffn_conv_b, w_down, loss_target, m_norm_mix_g, m_w_in, m_conv_w, m_q_norm_g, m_k_norm_g, m_rel_bias_table, m_sinks, m_out_norm_conv_g, m_out_norm_attn_g, m_w_out, m_norm_ffn_g, m_w_up, m_ffn_conv_w, m_ffn_conv_b, m_w_down, v_norm_mix_g, v_w_in, v_conv_w, v_q_norm_g, v_k_norm_g, v_rel_bias_table, v_sinks, v_out_norm_conv_g, v_out_norm_attn_g, v_w_out, v_norm_ffn_g, v_w_up, v_ffn_conv_w, v_ffn_conv_b, v_w_down):
    p = dict(norm_mix_g=norm_mix_g, w_in=w_in, conv_w=conv_w, q_norm_g=q_norm_g, k_norm_g=k_norm_g,
             rel_bias_table=rel_bias_table, sinks=sinks, out_norm_conv_g=out_norm_conv_g, out_norm_attn_g=out_norm_attn_g,
             w_out=w_out, norm_ffn_g=norm_ffn_g, w_up=w_up, ffn_conv_w=ffn_conv_w, ffn_conv_b=ffn_conv_b, w_down=w_down)
    m = dict(norm_mix_g=m_norm_mix_g, w_in=m_w_in, conv_w=m_conv_w, q_norm_g=m_q_norm_g, k_norm_g=m_k_norm_g,
             rel_bias_table=m_rel_bias_table, sinks=m_sinks, out_norm_conv_g=m_out_norm_conv_g,
             out_norm_attn_g=m_out_norm_attn_g, w_out=m_w_out, norm_ffn_g=m_norm_ffn_g, w_up=m_w_up,
             ffn_conv_w=m_ffn_conv_w, ffn_conv_b=m_ffn_conv_b, w_down=m_w_down)
    v = dict(norm_mix_g=v_norm_mix_g, w_in=v_w_in, conv_w=v_conv_w, q_norm_g=v_q_norm_g, k_norm_g=v_k_norm_g,
             rel_bias_table=v_rel_bias_table, sinks=v_sinks, out_norm_conv_g=v_out_norm_conv_g,
             out_norm_attn_g=v_out_norm_attn_g, w_out=v_w_out, norm_ffn_g=v_norm_ffn_g, w_up=v_w_up,
             ffn_conv_w=v_ffn_conv_w, ffn_conv_b=v_ffn_conv_b, w_down=v_w_down)

    xs, tgt = x[0], loss_target[0]
    g1, g2, gq, gk, gconv, gattn = norm_mix_g, norm_ffn_g, q_norm_g, k_norm_g, out_norm_conv_g, out_norm_attn_g
    ix, iy, ic = _coords()
    core = ic.astype(jnp.int32).reshape(1)
    chip = (2 * ix + iy).astype(jnp.int32).reshape(1)
    me = _lin(ix, iy, ic).astype(jnp.int32).reshape(1)
    bkt = jnp.asarray(_bucket_map())
    tr = lambda a: a[0].T
    taps = lambda a: jnp.transpose(a, (1, 0, 2))
    tbl_t = rel_bias_table.T

    wi_l, cw_l = _place_shards(me, [tr(w_in), taps(conv_w)], [BF16, F32], "place_mixer_shards")
    finish_a, token_a = _all_gather_split([wi_l, cw_l], "mixer", None)
    wo_l, wu_l, wd_l, fcw_l = _place_shards(me, [w_out[0], tr(w_up), w_down[0], taps(ffn_conv_w)],
                                            [BF16, BF16, BF16, F32], "place_ffn_shards", after=token_a)
    ffn_stage2, ffn_stage3, token_b = _all_gather_tree([wo_l, wu_l, wd_l, fcw_l], "ffn", token_a)
    wi_g, cw_g = finish_a(token_b)
    w_in_t = wi_g.reshape(IN_WIDTH, D_MODEL)
    conv_w_f = jnp.transpose(cw_g[:, :, 0, :], (1, 0, 2)).reshape(3, CONV_WIDTH)

    proj, u1, ycn, qn, kn, vv = _mix_in_fwd(xs, g1, w_in_t, conv_w_f, gq, gk, gconv)
    token_b2 = ffn_stage2(ycn)
    y_attn, yan, probs, psinks = _attn_fwd(qn, kn, vv, tbl_t, sinks, bkt, gattn, after=token_b2)
    wo_g, wu_g, wd_g, fcw_g = ffn_stage3(yan)
    w_out_f = wo_g.reshape(D_MODEL, D_MODEL)
    w_down_f = wd_g.reshape(N_FFN_BLK, FFN_BLK, D_MODEL)
    w_up_f = wu_g.reshape(2, N_FFN_BLK, FFN_BLK, D_MODEL)
    fcw_f = fcw_g.reshape(2, N_FFN_BLK, 3, 1, FFN_BLK)
    fcb = ffn_conv_b.reshape(2, N_FFN_BLK, 1, FFN_BLK)
    h1, u2, up, pre, act, dh2, dh2b, loss_acc = _ffn_fwd(xs, ycn, yan, w_out_f, g2, w_up_f, fcw_f, fcb, w_down_f, tgt)

    dw_down = _grad_tn_blocked(act, dh2b, "grad_w_down").reshape(N_DEV, D_FF // N_DEV, D_MODEL)
    plan_d, slots_d = _scatter_plan(1)
    d_sem = _split_start("scatter_w_down_start", [dw_down], [lax.empty((N_DEV - 1,) + dw_down.shape[1:], BF16)],
                         plan_d, None, _ALL_FOR_W_DOWN)
    dup, dh1, dh1b, dfb, dfcw, dg2 = _ffn_bwd(dh2, dh2b, h1, g2, up, pre, w_up_f, fcw_f, w_down_f, after=d_sem[4])
    dw_up = _grad_tn_blocked(dup.reshape(N_DEV, SEQ, FFN_BLK), u2, "grad_w_up")
    dw_out = _grad_tn([ycn, yan], dh1b, CONV_WIDTH, "grad_w_out").reshape(N_DEV, D_MODEL // N_DEV, D_MODEL)
    out_bwd = {}

    def behind_ffn(token):
        out_bwd["r"] = _out_bwd(dh1b, w_out_f, y_attn, gattn, after=token)
        return out_bwd["r"][0]

    finish_ffn, token_ffn = _reduce_scatter_split(
        [dw_up, dw_out, dfcw.reshape(N_DEV, 3, 1, FFN_BLK)], "ffn", core, behind_ffn)
    dycn, dy_attn, dgattn = out_bwd["r"]
    dqn, dkn, dv, dtbl_t, dsinks = _attn_bwd(qn, kn, vv, dy_attn, probs, psinks, bkt, after=token_ffn)
    dx, dproj, dcw, dgconv, dgq, dgk, dg1 = _mix_in_bwd(xs, dh1, proj, dycn, dqn, dkn, dv, w_in_t, conv_w_f,
                                                         g1, gq, gk, gconv)
    packed, packed_all = _pack_small_grads(dg1, dg2, dgconv, dgattn, dfb, dgq, dgk, dsinks, dtbl_t, loss_acc)
    plan_s, slots_s = _broadcast_plan()
    s_sem, r_sem, src_s, land_s, token_s = _split_start("gather_small_start", [packed], [packed_all], plan_s, None,
                                                        _ALL_FOR_SMALL)
    dw_in_t = _grad_w_in(dproj, u1, after=token_s).reshape(N_DEV, IN_WIDTH // N_DEV, D_MODEL)
    dcw_b = jnp.transpose(dcw.reshape(3, N_DEV, 1, CONV_WIDTH // N_DEV), (1, 0, 2, 3))
    adam = {}
    ffn_got = {}

    def behind_mixer(token):
        ffn_got["r"] = finish_ffn(token)
        return ffn_got["r"][1][0]

    finish_mixer, token_mixer = _reduce_scatter_split([dw_in_t, dcw_b], "mixer", core, behind_mixer)
    (p_wu, p_wo, p_fcw), (r_wu, r_wo, r_fcw) = ffn_got["r"]
    (own_wd,), (r_wd,) = _split_wait("scatter_w_down_wait", d_sem[0], d_sem[1], d_sem[2], d_sem[3], plan_d, slots_d,
                                     token_mixer)
    adam["w_down"] = _adamw_direct(w_down[0], m_w_down[0], v_w_down[0], own_wd, r_wd, me, "adamw_w_down", row_blocks=2)
    adam_up = _adamw(tr(w_up), tr(m_w_up), tr(v_w_up), p_wu, r_wu, chip, "adamw_w_up", row_blocks=4,
                     after=adam["w_down"][0])
    adam["w_out"] = _adamw(w_out[0], m_w_out[0], v_w_out[0], p_wo, r_wo, chip, "adamw_w_out", after=adam_up[0])
    adam_fcw = _adamw(taps(ffn_conv_w), taps(m_ffn_conv_w), taps(v_ffn_conv_w), p_fcw, r_fcw, chip, "adamw_ffn_conv_w",
                      after=adam["w_out"][0])
    _, (r_small,) = _split_wait("gather_small_wait", s_sem, r_sem, src_s, land_s, plan_s, slots_s, adam_fcw[0])
    small_in = {k: (p[k], m[k], v[k]) for k in _SMALL_NAMES}
    small_in["rel_bias_table"] = (tbl_t, m_rel_bias_table.T, v_rel_bias_table.T)
    small_out, loss_row = _adamw_small(r_small, small_in, None)
    (p_wi, p_cw), (r_wi, r_cw) = finish_mixer(loss_row)
    adam_in = _adamw(tr(w_in), tr(m_w_in), tr(v_w_in), p_wi, r_wi, chip, "adamw_w_in")
    adam_cw = _adamw(taps(conv_w), taps(m_conv_w), taps(v_conv_w), p_cw, r_cw, chip, "adamw_conv_w")

    res = {k: tuple(a[None] for a in t) for k, t in adam.items()}
    res["w_up"] = tuple(a.T[None] for a in adam_up)
    res["w_in"] = tuple(a.T[None] for a in adam_in)
    res["ffn_conv_w"] = tuple(taps(a) for a in adam_fcw)
    res["conv_w"] = tuple(taps(a) for a in adam_cw)
    res.update(small_out)
    res["rel_bias_table"] = tuple(a.T for a in small_out["rel_bias_table"])
    loss = loss_row[0, 0]
    order = ("norm_mix_g", "w_in", "conv_w", "q_norm_g", "k_norm_g", "rel_bias_table", "sinks", "out_norm_conv_g",
             "out_norm_attn_g", "w_out", "norm_ffn_g", "w_up", "ffn_conv_w", "ffn_conv_b", "w_down")
    return (loss, dx[None], *[res[k][0] for k in order], *[res[k][1] for k in order],
            *[res[k][2] for k in order], *[res[k][3] for k in order])
```

```python
import math

import numpy as np
import jax
import jax.numpy as jnp
from jax import lax
from jax.experimental import pallas as pl
from jax.experimental.pallas import tpu as pltpu

F32 = jnp.float32
BF16 = jnp.bfloat16

SEQ = 2048
D_MODEL = 1024
CONV_WIDTH = 512
ATTN_WIDTH = 512
KV_WIDTH = 128
HEAD_DIM = 64
N_HEADS = 8
GQA_GROUP = 4
IN_WIDTH = 2304
D_FF = 2816
BLK = 128
NUM_BUCKETS = 32
EPS = 1e-6
NEG_INF = -1e30
ADAM_LR = 0.001
ADAM_B1 = 0.9
ADAM_B2 = 0.999
ADAM_EPS = 1e-08
ADAM_WD = 0.01
ADAM_STEP = 10

N_DEV = 8
FFN_BLK = 2 * D_FF // N_DEV
N_FFN_BLK = D_FF // FFN_BLK
SUBLANES = 8
VMEM_LIMIT = 56 * 1024 * 1024

_MESH = pl.DeviceIdType.MESH
_ANY = pl.BlockSpec(memory_space=pl.ANY)


def _params(sem):
    return pltpu.CompilerParams(dimension_semantics=sem, vmem_limit_bytes=VMEM_LIMIT)


def _ordered_behind(body, pos, after):
    if after is None:
        return body, [], []
    return (lambda *refs: body(*refs[:pos], *refs[pos + 1:])), [_ANY], [after]


def _dot(a, b):
    return jnp.dot(a, b, preferred_element_type=F32)


def _dot_nt(a, b):
    return lax.dot_general(a, b, (((1,), (1,)), ((), ())), preferred_element_type=F32)


def _dot_tn(a, b):
    return lax.dot_general(a, b, (((0,), (0,)), ((), ())), preferred_element_type=F32)


def _shift_down(x, s, halo):
    r = pltpu.roll(x, s, axis=0)
    hr = pltpu.roll(halo, s, axis=0)
    row = lax.broadcasted_iota(jnp.int32, halo.shape, 0)
    top = jnp.where(row < s, hr, r[:SUBLANES])
    return jnp.concatenate([top, r[SUBLANES:]], axis=0)


def _shift_up(x, s, halo):
    n = x.shape[0]
    r = pltpu.roll(x, n - s, axis=0)
    hr = pltpu.roll(halo, SUBLANES - s, axis=0)
    row = lax.broadcasted_iota(jnp.int32, halo.shape, 0)
    bot = jnp.where(row >= SUBLANES - s, hr, r[n - SUBLANES:])
    return jnp.concatenate([r[:n - SUBLANES], bot], axis=0)


def _taps(w):
    return (w[0], w[1], w[2]) if len(w.shape) == 3 else (w[0:1], w[1:2], w[2:3])


def _conv3(x, w, halo):
    x2 = _shift_down(x, 2, halo)
    x1 = _shift_down(x, 1, halo)
    return x2 * w[0] + x1 * w[1] + x * w[2], x2, x1


def _conv3_bwd_input(dy, w, halo_next):
    return dy * w[2] + _shift_up(dy, 1, halo_next) * w[1] + _shift_up(dy, 2, halo_next) * w[0]


def _rstd(x):
    return lax.rsqrt(jnp.mean(x * x, axis=-1, keepdims=True) + EPS)


def _rms_bwd(x, g, dy):
    r = _rstd(x)
    n = x * r
    dn = dy * g
    dx = r * (dn - n * jnp.mean(dn * n, axis=-1, keepdims=True))
    return dx, jnp.sum(dy * n, axis=0, keepdims=True)


def _head_mean(x):
    width = x.shape[-1]
    ri = lax.broadcasted_iota(jnp.int32, (width, width), 0) // HEAD_DIM
    ci = lax.broadcasted_iota(jnp.int32, (width, width), 1) // HEAD_DIM
    ones = jnp.where(ri == ci, 1.0, 0.0).astype(BF16)
    hi = x.astype(BF16)
    lo = (x - hi.astype(F32)).astype(BF16)
    return (_dot(hi, ones) + _dot(lo, ones)) * (1.0 / HEAD_DIM)


def _head_norm(x, g, heads):
    return x * lax.rsqrt(_head_mean(x * x) + EPS) * jnp.tile(g, (1, heads))


def _head_norm_bwd(x, g, dy, heads):
    r = lax.rsqrt(_head_mean(x * x) + EPS)
    n = x * r
    dn = dy * jnp.tile(g, (1, heads))
    dx = r * (dn - n * _head_mean(dn * n))
    per_lane = jnp.sum(dy * n, axis=0, keepdims=True)
    dg = per_lane[:, 0:HEAD_DIM]
    for h in range(1, heads):
        dg = dg + per_lane[:, HEAD_DIM * h:HEAD_DIM * (h + 1)]
    return dx, dg


def _bucket_map():
    q = np.arange(BLK)[:, None]
    j = np.arange(BLK)[None, :]
    n = np.where(j > q, q + BLK - j, q - j)
    nf = np.maximum(n, 1).astype(np.float32)
    max_exact = NUM_BUCKETS // 2
    large = max_exact + (np.log(nf / max_exact) / math.log(BLK / max_exact) * (NUM_BUCKETS - max_exact)).astype(np.int32)
    large = np.minimum(large, NUM_BUCKETS - 1)
    return np.where(n < max_exact, n, large).astype(np.int32)


def _coords():
    return lax.axis_index("x"), lax.axis_index("y"), lax.axis_index("c")


def _lin(px, py, pc):
    return 4 * px + 2 * py + pc


def _chips(x, y):
    return [(1 - x, y), (x, 1 - y), (1 - x, 1 - y)]


def _peers(x, y, c):
    return [(1 - x if r & 4 else x, 1 - y if r & 2 else y, 1 - c if r & 1 else c) for r in range(1, N_DEV)]


_HBM = pl.BlockSpec(memory_space=pltpu.HBM)
_SEM = pl.BlockSpec(memory_space=pltpu.SEMAPHORE)
_EFFECT = pltpu.SideEffectType.DATAFLOW_SIDE_EFFECTING


def _in_hbm(a):
    return pltpu.with_memory_space_constraint(a, pltpu.HBM)


_SIBLING = (1, lambda x, y, c: [(x, y, 1 - c)])
_SIBLING_AND_CHIPS = (2, lambda x, y, c: [(x, y, 1 - c)] + [(cx, cy, c) for cx, cy in _chips(x, y)])
_SIBLING_AND_NEIGHBOURS = (3, lambda x, y, c: [(x, y, 1 - c), (1 - x, y, c), (x, 1 - y, c)])
_ONWARD_AND_SIBLING = (4, lambda x, y, c: [(jnp.where(c == 1, x, 1 - x), jnp.where(c == 1, 1 - y, y), c), (x, y, 1 - c)])
_ALL_FOR_W_DOWN = (5, lambda x, y, c: _peers(x, y, c))
_CHIPS = (6, lambda x, y, c: [(cx, cy, c) for cx, cy in _chips(x, y)])
_ALL_FOR_SMALL = (7, lambda x, y, c: _peers(x, y, c))


def _split_start(name, srcs, lands, plan, after, handshake):
    ns, nl = len(srcs), len(lands)
    n_copies = len(plan(0, 0, 0))
    n_after = 0 if after is None else 1
    collective_id, peers_of = handshake

    def body(*refs):
        src_refs, land_refs = refs[:ns + nl], refs[ns:ns + nl]
        send_sems, recv_sems = refs[ns + nl + n_after], refs[ns + nl + n_after + 1]
        token = refs[-1]
        barrier = pltpu.get_barrier_semaphore()
        peers = peers_of(*_coords())
        for peer in peers:
            pl.semaphore_signal(barrier, inc=1, device_id=peer, device_id_type=_MESH)
        pl.semaphore_wait(barrier, len(peers))
        for k, (a, s_slot, l, d_slot, dev) in enumerate(plan(*_coords())):
            src = src_refs[a] if s_slot is None else src_refs[a].at[s_slot]
            pltpu.make_async_remote_copy(src_ref=src, dst_ref=land_refs[l].at[d_slot], send_sem=send_sems.at[k],
                                         recv_sem=recv_sems.at[k], device_id=dev, device_id_type=_MESH).start()
        token[...] = jnp.zeros_like(token)

    arrs = list(srcs) + list(lands)
    out = pl.pallas_call(
        body, name=name,
        out_shape=(pltpu.SemaphoreType.DMA((n_copies,)), pltpu.SemaphoreType.DMA((n_copies,)),
                   *[pltpu.HBM(a.shape, a.dtype) for a in arrs], jax.ShapeDtypeStruct((SUBLANES, 128), F32)),
        in_specs=[_HBM] * (ns + nl) + [_ANY] * n_after,
        out_specs=(_SEM, _SEM, *[_HBM] * (ns + nl), pl.BlockSpec(memory_space=pltpu.VMEM)),
        input_output_aliases={i: 2 + i for i in range(ns + nl)},
        compiler_params=pltpu.CompilerParams(has_side_effects=_EFFECT, collective_id=collective_id),
    )(*[_in_hbm(a) for a in arrs], *([] if after is None else [after]))
    return out[0], out[1], list(out[2:2 + ns]), list(out[2 + ns:2 + ns + nl]), out[-1]


def _split_wait(name, send_sems, recv_sems, srcs, lands, plan, recv_slots, after):
    ns, nl = len(srcs), len(lands)

    def body(*refs):
        src_refs, land_refs = refs[:ns + nl], refs[ns:ns + nl]
        send_sems, recv_sems = refs[ns + nl], refs[ns + nl + 1]
        coords = _coords()
        slots = recv_slots(*coords)
        for k, (a, s_slot, l, _, dev) in enumerate(plan(*coords)):
            src = src_refs[a] if s_slot is None else src_refs[a].at[s_slot]
            cp = pltpu.make_async_remote_copy(src_ref=src, dst_ref=land_refs[l].at[slots[k]], send_sem=send_sems.at[k],
                                              recv_sem=recv_sems.at[k], device_id=dev, device_id_type=_MESH)
            cp.wait_send()
            cp.wait_recv()

    arrs = list(srcs) + list(lands)
    out = pl.pallas_call(
        body, name=name,
        out_shape=tuple(pltpu.HBM(a.shape, a.dtype) for a in arrs),
        in_specs=[_HBM] * (ns + nl) + [_SEM, _SEM, _ANY],
        out_specs=tuple([_HBM] * (ns + nl)),
        input_output_aliases={i: i for i in range(ns + nl)},
        compiler_params=pltpu.CompilerParams(has_side_effects=_EFFECT),
    )(*arrs, send_sems, recv_sems, after)
    return list(out[:ns]), list(out[ns:])


def _gather_plan_ici(n):
    def plan(x, y, c):
        me = _lin(x, y, c)
        out = []
        for a in range(n):
            out.append((a, me, a, me, (x, y, 1 - c)))
            out += [(a, me, a, me, (cx, cy, c)) for cx, cy in _chips(x, y)]
        return out

    def recv_slots(x, y, c):
        out = []
        for _ in range(n):
            out.append(_lin(x, y, 1 - c))
            out += [_lin(cx, cy, c) for cx, cy in _chips(x, y)]
        return out

    return plan, recv_slots


def _gather_plan_d2d(n):
    def plan(x, y, c):
        return [(a, _lin(cx, cy, c), a, _lin(cx, cy, c), (x, y, 1 - c)) for a in range(n) for cx, cy in _chips(x, y)]

    def recv_slots(x, y, c):
        return [_lin(cx, cy, 1 - c) for _ in range(n) for cx, cy in _chips(x, y)]

    return plan, recv_slots


def _all_gather_split(lands, tag, after):
    n = len(lands)
    plan1, slots1 = _gather_plan_ici(n)
    s1, r1, _, lands, token = _split_start(f"gather_{tag}_ici_start", [], lands, plan1, after, _SIBLING_AND_CHIPS)

    def finish(after):
        _, got = _split_wait(f"gather_{tag}_ici_wait", s1, r1, [], lands, plan1, slots1, after)
        plan2, slots2 = _gather_plan_d2d(n)
        s2, r2, _, got, token2 = _split_start(f"gather_{tag}_d2d_start", [], got, plan2, None, _SIBLING)
        return _split_wait(f"gather_{tag}_d2d_wait", s2, r2, [], got, plan2, slots2, token2)[1]

    return finish, token


def _all_gather_tree(lands, tag, after):
    n = len(lands)

    def plan1(x, y, c):
        me = _lin(x, y, c)
        return [(a, me, a, me, dev) for a in range(n) for dev in ((x, y, 1 - c), (1 - x, y, c), (x, 1 - y, c))]

    def slots1(x, y, c):
        return [s for _ in range(n) for s in (_lin(x, y, 1 - c), _lin(1 - x, y, c), _lin(x, 1 - y, c))]

    def plan2(x, y, c):
        from_x, from_y = _lin(1 - x, y, c), _lin(x, 1 - y, c)
        north = c == 1
        passed = jnp.where(north, from_x, from_y)
        onward = (jnp.where(north, x, 1 - x), jnp.where(north, 1 - y, y), c)
        sib = (x, y, 1 - c)
        return [cp for a in range(n) for cp in ((a, passed, a, passed, onward), (a, from_x, a, from_x, sib),
                                                (a, from_y, a, from_y, sib))]

    def slots2(x, y, c):
        return [s for _ in range(n) for s in (_lin(1 - x, 1 - y, c), _lin(1 - x, y, 1 - c), _lin(x, 1 - y, 1 - c))]

    def plan3(x, y, c):
        diag = _lin(1 - x, 1 - y, c)
        return [(a, diag, a, diag, (x, y, 1 - c)) for a in range(n)]

    def slots3(x, y, c):
        return [_lin(1 - x, 1 - y, 1 - c)] * n

    s1, r1, _, lands, token = _split_start(f"gather_{tag}_1_start", [], lands, plan1, after, _SIBLING_AND_NEIGHBOURS)
    state = {}

    def stage2(after):
        _, got = _split_wait(f"gather_{tag}_1_wait", s1, r1, [], lands, plan1, slots1, after)
        state["s"], state["r"], _, state["lands"], token2 = _split_start(f"gather_{tag}_2_start", [], got, plan2, None,
                                                                         _ONWARD_AND_SIBLING)
        return token2

    def stage3(after):
        _, got = _split_wait(f"gather_{tag}_2_wait", state["s"], state["r"], [], state["lands"], plan2, slots2, after)
        s3, r3, _, got, token3 = _split_start(f"gather_{tag}_3_start", [], got, plan3, None, _SIBLING)
        return _split_wait(f"gather_{tag}_3_wait", s3, r3, [], got, plan3, slots3, token3)[1]

    return stage2, stage3, token


_CHIP_LIST = ((0, 0), (0, 1), (1, 0), (1, 1))


def _reduce_plan_d2d(n):
    def plan(x, y, c):
        return [(a, _lin(qx, qy, 1 - c), a, q, (x, y, 1 - c)) for a in range(n) for q, (qx, qy) in enumerate(_CHIP_LIST)]

    def recv_slots(x, y, c):
        return [q for _ in range(n) for q in range(4)]

    return plan, recv_slots


def _reduce_plan_ici(n):
    def plan(x, y, c):
        return [(a, 2 * cx + cy, a, j, (cx, cy, c)) for a in range(n) for j, (cx, cy) in enumerate(_chips(x, y))]

    def recv_slots(x, y, c):
        return [j for _ in range(n) for j in range(3)]

    return plan, recv_slots


def _scatter_plan(n):
    def plan(x, y, c):
        return [(a, _lin(*peer), a, r, peer) for a in range(n) for r, peer in enumerate(_peers(x, y, c))]

    def recv_slots(x, y, c):
        return [r for _ in range(n) for r in range(N_DEV - 1)]

    return plan, recv_slots


def _broadcast_plan():
    def plan(x, y, c):
        return [(0, None, 0, _lin(x, y, c), peer) for peer in _peers(x, y, c)]

    def recv_slots(x, y, c):
        return [_lin(*peer) for peer in _peers(x, y, c)]

    return plan, recv_slots


def _chip_partial(grads, recvd, core, name):
    n = len(grads)

    def body(c_ref, *refs):
        for a in range(n):
            g_ref, r_ref, o_ref = refs[a], refs[n + a], refs[2 * n + a]
            o_ref[...] = (g_ref[...].astype(F32) + r_ref[...].astype(F32)).astype(o_ref.dtype)

    def blk(a, own):
        zeros = (0,) * (a.ndim - 1)
        return pl.BlockSpec((None,) + a.shape[1:],
                            (lambda q, c_ref: (2 * q + c_ref[0],) + zeros) if own else (lambda q, c_ref: (q,) + zeros))

    return pl.pallas_call(
        body, name=name,
        grid_spec=pltpu.PrefetchScalarGridSpec(
            num_scalar_prefetch=1, grid=(4,),
            in_specs=[blk(a, True) for a in grads] + [blk(a, False) for a in recvd],
            out_specs=[blk(a, False) for a in recvd]),
        out_shape=[jax.ShapeDtypeStruct(a.shape, a.dtype) for a in recvd],
        compiler_params=_params(("arbitrary",)),
    )(core, *grads, *recvd)


def _reduce_scatter_split(grads, tag, core, behind):
    n = len(grads)
    plan1, slots1 = _reduce_plan_d2d(n)
    lands1 = [lax.empty((4,) + a.shape[1:], a.dtype) for a in grads]
    s1, r1, srcs1, lands1, token1 = _split_start(f"reduce_{tag}_d2d_start", grads, lands1, plan1, None, _SIBLING)
    own, got = _split_wait(f"reduce_{tag}_d2d_wait", s1, r1, srcs1, lands1, plan1, slots1, behind(token1))
    parts = _chip_partial(own, got, core, f"reduce_{tag}_partial")
    plan2, slots2 = _reduce_plan_ici(n)
    lands2 = [lax.empty((3,) + a.shape[1:], a.dtype) for a in grads]
    s2, r2, srcs2, lands2, token2 = _split_start(f"reduce_{tag}_ici_start", parts, lands2, plan2, None, _CHIPS)

    def finish(after):
        return _split_wait(f"reduce_{tag}_ici_wait", s2, r2, srcs2, lands2, plan2, slots2, after)

    return finish, token2


def _place_shards(me, shards, dtypes, name, after=None):
    n = len(shards)

    def body(me_ref, *refs):
        for a in range(n):
            refs[n + a][...] = refs[a][...].astype(dtypes[a])

    full = lambda s: pl.BlockSpec(s.shape, lambda i, me_ref: (0,) * s.ndim)
    slot = lambda s: pl.BlockSpec((None,) + s.shape, lambda i, me_ref: (me_ref[0],) + (0,) * s.ndim)
    body, more_specs, more = _ordered_behind(body, 1 + n, after)
    return pl.pallas_call(
        body, name=name,
        grid_spec=pltpu.PrefetchScalarGridSpec(num_scalar_prefetch=1, grid=(1,),
                                               in_specs=[full(s) for s in shards] + more_specs,
                                               out_specs=[slot(s) for s in shards]),
        out_shape=[jax.ShapeDtypeStruct((N_DEV,) + s.shape, d) for s, d in zip(shards, dtypes)],
        compiler_params=_params(("arbitrary",)),
    )(me, *shards, *more)


def _mix_in_fwd(x, g1, w_in_t, conv_w, gq, gk, gconv):
    tm = 512
    n_t = SEQ // tm

    def body(x_ref, g1_ref, w_ref, cw_ref, gq_ref, gk_ref, gc_ref,
             proj_ref, u1_ref, ycn_ref, qn_ref, kn_ref, v_ref, halo_ref):
        @pl.when(pl.program_id(0) == 0)
        def _():
            halo_ref[...] = jnp.zeros_like(halo_ref)

        xv = x_ref[...]
        u = (xv * _rstd(xv) * g1_ref[...]).astype(BF16)
        u1_ref[...] = u
        proj = _dot_nt(u, w_ref[...])
        proj_ref[...] = proj
        gate_b = proj[:, 0:CONV_WIDTH]
        a = proj[:, CONV_WIDTH:2 * CONV_WIDTH] * proj[:, 2 * CONV_WIDTH:3 * CONV_WIDTH]
        cv, _, _ = _conv3(a, _taps(cw_ref[...]), halo_ref[...])
        halo_ref[...] = a[tm - SUBLANES:]
        yc = gate_b * cv
        ycn_ref[...] = (yc * _rstd(yc) * gc_ref[...]).astype(BF16)
        q0 = 3 * CONV_WIDTH
        qn_ref[...] = _head_norm(proj[:, q0:q0 + ATTN_WIDTH], gq_ref[...], N_HEADS).astype(BF16)
        k0 = q0 + ATTN_WIDTH
        kn_ref[...] = _head_norm(proj[:, k0:k0 + KV_WIDTH], gk_ref[...], 2).astype(BF16)
        v_ref[...] = proj[:, k0 + KV_WIDTH:k0 + 2 * KV_WIDTH].astype(BF16)

    const = lambda shape: pl.BlockSpec(shape, lambda i: (0,) * len(shape))
    rows = lambda w: pl.BlockSpec((tm, w), lambda i: (i, 0))
    return pl.pallas_call(
        body, name="mix_in_fwd", grid=(n_t,),
        in_specs=[rows(D_MODEL), const((1, D_MODEL)), const((IN_WIDTH, D_MODEL)), const((3, CONV_WIDTH)),
                  const((1, HEAD_DIM)), const((1, HEAD_DIM)), const((1, CONV_WIDTH))],
        out_specs=[rows(IN_WIDTH), rows(D_MODEL), rows(CONV_WIDTH), rows(ATTN_WIDTH), rows(KV_WIDTH), rows(KV_WIDTH)],
        out_shape=[jax.ShapeDtypeStruct((SEQ, IN_WIDTH), F32), jax.ShapeDtypeStruct((SEQ, D_MODEL), BF16),
                   jax.ShapeDtypeStruct((SEQ, CONV_WIDTH), BF16),
                   jax.ShapeDtypeStruct((SEQ, ATTN_WIDTH), BF16), jax.ShapeDtypeStruct((SEQ, KV_WIDTH), BF16),
                   jax.ShapeDtypeStruct((SEQ, KV_WIDTH), BF16)],
        scratch_shapes=[pltpu.VMEM((SUBLANES, CONV_WIDTH), F32)],
        compiler_params=_params(("arbitrary",)),
    )(x, g1, w_in_t, conv_w, gq, gk, gconv)


GROUP_ROWS = GQA_GROUP * BLK
QUERY_BLOCKS_PER_STEP = 2


def _band_bias(tbl_ref, bkt, bias_ref):
    for h in range(N_HEADS):
        acc = jnp.zeros(bkt.shape, F32)
        for b in range(NUM_BUCKETS):
            acc = jnp.where(bkt == b, tbl_ref[h, b], acc)
        bias_ref[h // GQA_GROUP, BLK * (h % GQA_GROUP):BLK * (h % GQA_GROUP + 1), :] = acc


def _band_masks(i):
    qi = lax.broadcasted_iota(jnp.int32, (GROUP_ROWS, BLK), 0) & (BLK - 1)
    ji = lax.broadcasted_iota(jnp.int32, (GROUP_ROWS, BLK), 1)
    upper = ji > qi
    return upper, upper & (i == 0)


def _stack_heads(x, g):
    return jnp.concatenate([x[:, HEAD_DIM * h:HEAD_DIM * (h + 1)] for h in range(GQA_GROUP * g, GQA_GROUP * (g + 1))], axis=0)


def _unstack_heads(groups):
    return jnp.concatenate([p[BLK * t:BLK * (t + 1)] for p in groups for t in range(GQA_GROUP)], axis=-1)


def _per_head_rows(vals):
    row = lax.broadcasted_iota(jnp.int32, (GROUP_ROWS, 1), 0)
    col = jnp.full((GROUP_ROWS, 1), vals[GQA_GROUP - 1], F32)
    for t in range(GQA_GROUP - 2, -1, -1):
        col = jnp.where(row < BLK * (t + 1), vals[t], col)
    return col


def _band_rows(ref, i):
    prev = pl.multiple_of(jnp.maximum(i - 1, 0) * BLK, BLK)
    cur = pl.multiple_of(i * BLK, BLK)
    return jnp.concatenate([ref[pl.ds(prev, BLK), :], ref[pl.ds(cur, BLK), :]], axis=0), prev, cur


def _fold(band, upper):
    return jnp.where(upper, band[:, :BLK], band[:, BLK:])


def _unfold(tile, upper):
    return jnp.concatenate([jnp.where(upper, tile, 0.0), jnp.where(upper, 0.0, tile)], axis=1)


def _head_probs(qh, kh, bias, upper, dead, sink):
    logits = _fold(_dot_nt(qh, kh), upper) * (HEAD_DIM ** -0.5) + bias
    logits = jnp.where(dead, NEG_INF, logits)
    m = jnp.maximum(jnp.max(logits, axis=-1, keepdims=True), sink)
    p = jnp.exp(logits - m)
    es = jnp.exp(sink - m)
    den = jnp.sum(p, axis=-1, keepdims=True) + es
    return p / den, es / den


def _attn_fwd(qn, kn, v, tbl, sinks, bkt, gattn, after=None):
    n_b = SEQ // BLK

    def body(q_ref, k_ref, v_ref, tbl_ref, sink_ref, bkt_ref, ga_ref, y_ref, yn_ref, p_ref, ps_ref, bias_ref):
        step = pl.program_id(0)

        @pl.when(step == 0)
        def _():
            _band_bias(tbl_ref, bkt_ref[...], bias_ref)

        lane = lax.broadcasted_iota(jnp.int32, (BLK, 128), 1)
        for b in range(QUERY_BLOCKS_PER_STEP):
            i = QUERY_BLOCKS_PER_STEP * step + b
            rows = slice(BLK * b, BLK * (b + 1))
            kb, _, _ = _band_rows(k_ref, i)
            vb, _, _ = _band_rows(v_ref, i)
            upper, dead = _band_masks(i)
            q = q_ref[rows, :]
            outs = []
            psinks = jnp.zeros((BLK, 128), F32)
            for g in range(N_HEADS // GQA_GROUP):
                kv = slice(HEAD_DIM * g, HEAD_DIM * (g + 1))
                sink = _per_head_rows([sink_ref[0, GQA_GROUP * g + t] for t in range(GQA_GROUP)])
                probs, psink = _head_probs(_stack_heads(q, g), kb[:, kv], bias_ref[g], upper, dead, sink)
                p_ref[b, g] = probs.astype(BF16)
                for t in range(GQA_GROUP):
                    psinks = jnp.where(lane == GQA_GROUP * g + t, psink[BLK * t:BLK * (t + 1)], psinks)
                outs.append(_dot(_unfold(probs, upper).astype(BF16), vb[:, kv]))
            ps_ref[rows, :] = psinks
            y = _unstack_heads(outs)
            y_ref[rows, :] = y
            yn_ref[rows, :] = (y * _rstd(y) * ga_ref[...]).astype(BF16)

    const = lambda shape: pl.BlockSpec(shape, lambda i: (0,) * len(shape))
    rows = lambda w: pl.BlockSpec((QUERY_BLOCKS_PER_STEP * BLK, w), lambda i: (i, 0))
    smem = pl.BlockSpec(memory_space=pltpu.SMEM)
    body, more_specs, more = _ordered_behind(body, 7, after)
    return pl.pallas_call(
        body, name="attn_fwd", grid=(n_b // QUERY_BLOCKS_PER_STEP,),
        in_specs=[rows(ATTN_WIDTH), const((SEQ, KV_WIDTH)), const((SEQ, KV_WIDTH)), smem, smem,
                  const((BLK, BLK)), const((1, ATTN_WIDTH))] + more_specs,
        out_specs=[rows(ATTN_WIDTH), rows(ATTN_WIDTH),
                   pl.BlockSpec((QUERY_BLOCKS_PER_STEP, N_HEADS // GQA_GROUP, GROUP_ROWS, BLK), lambda i: (i, 0, 0, 0)),
                   rows(128)],
        out_shape=[jax.ShapeDtypeStruct((SEQ, ATTN_WIDTH), F32), jax.ShapeDtypeStruct((SEQ, ATTN_WIDTH), BF16),
                   jax.ShapeDtypeStruct((n_b, N_HEADS // GQA_GROUP, GROUP_ROWS, BLK), BF16),
                   jax.ShapeDtypeStruct((SEQ, 128), F32)],
        scratch_shapes=[pltpu.VMEM((N_HEADS // GQA_GROUP, GROUP_ROWS, BLK), F32)],
        compiler_params=_params(("arbitrary",)),
    )(qn, kn, v, tbl, sinks, bkt, gattn, *more)


def _ffn_block(i, step):
    return jnp.where(i % 2 == 0, step, N_FFN_BLK - 1 - step)


def _ffn_fwd(x, ycn, yan, w_out, g2, w_up, fcw, fcb, w_down, tgt):
    tm = 512
    n_t = SEQ // tm

    def body(x_ref, ycn_ref, yan_ref, wo_ref, g2_ref, wu_ref, cw_ref, b_ref, wd_ref, tgt_ref,
             h1_ref, u2_ref, up_ref, pre_ref, act_ref, dh2_ref, dh2b_ref, loss_ref, acc_ref, halo_ref):
        i, step = pl.program_id(0), pl.program_id(1)
        j = _ffn_block(i, step)

        @pl.when((i == 0) & (step == 0))
        def _():
            loss_ref[...] = jnp.zeros_like(loss_ref)

        @pl.when(step == 0)
        def _():
            h1 = x_ref[...] + _dot(ycn_ref[...], wo_ref[0:CONV_WIDTH, :]) + _dot(yan_ref[...], wo_ref[CONV_WIDTH:, :])
            h1_ref[...] = h1
            u2_ref[...] = (h1 * _rstd(h1) * g2_ref[...]).astype(BF16)
            acc_ref[...] = jnp.zeros_like(acc_ref)

        u2 = u2_ref[...]
        pre = []
        for s in range(2):
            up = _dot_nt(u2, wu_ref[s])
            up_ref[s] = up.astype(BF16)
            halo = jnp.where(i == 0, 0.0, halo_ref[s, j])
            pre.append(_conv3(up, _taps(cw_ref.at[s]), halo)[0] + b_ref[s])
            pre_ref[s] = pre[s].astype(BF16)
            halo_ref[s, j] = up[tm - SUBLANES:]
        g, val = pre
        act = (g * jax.nn.sigmoid(g) * val).astype(BF16)
        act_ref[...] = act
        acc_ref[...] += _dot(act, wd_ref[...])

        @pl.when(step == N_FFN_BLK - 1)
        def _():
            err = h1_ref[...] + acc_ref[...] - tgt_ref[...]
            loss_ref[...] += 0.5 * jnp.sum(err * err) / D_MODEL
            dh2 = err / D_MODEL
            dh2_ref[...] = dh2
            dh2b_ref[...] = dh2.astype(BF16)

    rows = lambda w: pl.BlockSpec((tm, w), lambda i, step: (i, 0))
    const = lambda shape: pl.BlockSpec(shape, lambda i, step: (0,) * len(shape))
    pair = lambda *s: pl.BlockSpec((2, None) + s, lambda i, step: (0, _ffn_block(i, step)) + (0,) * len(s))
    upb = pl.BlockSpec((2, None, tm, FFN_BLK), lambda i, step: (0, _ffn_block(i, step), i, 0))
    return pl.pallas_call(
        body, name="ffn_fwd", grid=(n_t, N_FFN_BLK),
        in_specs=[rows(D_MODEL), rows(CONV_WIDTH), rows(ATTN_WIDTH), const((D_MODEL, D_MODEL)), const((1, D_MODEL)),
                  pair(FFN_BLK, D_MODEL), pair(3, 1, FFN_BLK), pair(1, FFN_BLK),
                  pl.BlockSpec((None, FFN_BLK, D_MODEL), lambda i, step: (_ffn_block(i, step), 0, 0)), rows(D_MODEL)],
        out_specs=[rows(D_MODEL), rows(D_MODEL), upb, upb,
                   pl.BlockSpec((None, tm, FFN_BLK), lambda i, step: (_ffn_block(i, step), i, 0)),
                   rows(D_MODEL), rows(D_MODEL), const((SUBLANES, 128))],
        out_shape=[jax.ShapeDtypeStruct((SEQ, D_MODEL), F32), jax.ShapeDtypeStruct((SEQ, D_MODEL), BF16),
                   jax.ShapeDtypeStruct((2, N_FFN_BLK, SEQ, FFN_BLK), BF16),
                   jax.ShapeDtypeStruct((2, N_FFN_BLK, SEQ, FFN_BLK), BF16),
                   jax.ShapeDtypeStruct((N_FFN_BLK, SEQ, FFN_BLK), BF16),
                   jax.ShapeDtypeStruct((SEQ, D_MODEL), F32), jax.ShapeDtypeStruct((SEQ, D_MODEL), BF16),
                   jax.ShapeDtypeStruct((SUBLANES, 128), F32)],
        scratch_shapes=[pltpu.VMEM((tm, D_MODEL), F32), pltpu.VMEM((2, N_FFN_BLK, SUBLANES, FFN_BLK), F32)],
        compiler_params=_params(("arbitrary", "arbitrary")),
    )(x, ycn, yan, w_out, g2, w_up, fcw, fcb, w_down, tgt)


def _ffn_bwd(dh2, dh2b, h1, g2, up, pre, w_up, fcw, w_down, after=None):
    tm = 512
    units = ((288, 224), (0, 288))
    n_t = SEQ // tm

    def body(dh2_ref, dh2b_ref, h1_ref, g2_ref, up_ref, pre_ref, wu_ref, cw_ref, wd_ref,
             dup_ref, dh1_ref, dh1b_ref, dfb_ref, dfcw_ref, dg2_ref, acc_ref, next_ref):
        j, i = pl.program_id(0), pl.program_id(1)
        tile = pl.ds(pl.multiple_of((n_t - 1 - i) * tm, tm), tm)

        @pl.when((j == 0) & (i == 0))
        def _():
            dfb_ref[...] = jnp.zeros_like(dfb_ref)
            dfcw_ref[...] = jnp.zeros_like(dfcw_ref)
            dg2_ref[...] = jnp.zeros_like(dg2_ref)

        @pl.when(j == 0)
        def _():
            acc_ref[tile, :] = jnp.zeros((tm, D_MODEL), F32)

        nxt = [jnp.where(i == 0, 0.0, next_ref[s]) for s in range(2)]
        sums = [[0.0] * 4 for _ in range(2)]
        for r0, rn in units:
            rows = slice(r0, r0 + rn)
            g, val = pre_ref[0, rows, :].astype(F32), pre_ref[1, rows, :].astype(F32)
            sg = jax.nn.sigmoid(g)
            silu = g * sg
            dact = _dot_nt(dh2b_ref[rows, :], wd_ref[...])
            dpre = (dact * val * (sg * (1.0 + g * (1.0 - sg))), dact * silu)
            dups = []
            for s in range(2):
                d = dpre[s]
                u = up_ref[s, rows, :].astype(F32)
                w = _taps(cw_ref.at[s])
                d1 = _shift_up(d, 1, nxt[s])
                d2 = _shift_up(d, 2, nxt[s])
                nxt[s] = d[:SUBLANES]
                for t, term in enumerate((d, d2 * u, d1 * u, d * u)):
                    sums[s][t] = sums[s][t] + jnp.sum(term, axis=0, keepdims=True)
                dups.append((d * w[2] + d1 * w[1] + d2 * w[0]).astype(BF16))
                dup_ref[s, rows, :] = dups[s]
            acc_rows = pl.ds(pl.multiple_of((n_t - 1 - i) * tm + r0, SUBLANES), rn)
            acc_ref[acc_rows, :] += _dot(dups[0], wu_ref[0]) + _dot(dups[1], wu_ref[1])
        for s in range(2):
            next_ref[s] = nxt[s]
            dfb_ref[s, j] += sums[s][0]
            for t in range(3):
                dfcw_ref[s, j, t] += sums[s][1 + t]

        @pl.when(j == N_FFN_BLK - 1)
        def _():
            dn, dgain = _rms_bwd(h1_ref[...], g2_ref[...], acc_ref[tile, :])
            dh1 = dh2_ref[...] + dn
            dh1_ref[...] = dh1
            dh1b_ref[...] = dh1.astype(BF16)
            dg2_ref[...] += dgain

    rev = lambda i: n_t - 1 - i
    rows = lambda w: pl.BlockSpec((tm, w), lambda j, i: (rev(i), 0))
    last_rows = lambda w: pl.BlockSpec((tm, w), lambda j, i: (jnp.where(j == N_FFN_BLK - 1, rev(i), rev(0)), 0))
    const = lambda shape: pl.BlockSpec(shape, lambda j, i: (0,) * len(shape))
    pair = lambda *s: pl.BlockSpec((2, None) + s, lambda j, i: (0, j) + (0,) * len(s))
    upb = pl.BlockSpec((2, None, tm, FFN_BLK), lambda j, i: (0, j, rev(i), 0))
    body, more_specs, more = _ordered_behind(body, 9, after)
    return pl.pallas_call(
        body, name="ffn_bwd", grid=(N_FFN_BLK, n_t),
        in_specs=[last_rows(D_MODEL), rows(D_MODEL), last_rows(D_MODEL), const((1, D_MODEL)), upb, upb,
                  pair(FFN_BLK, D_MODEL), pair(3, 1, FFN_BLK),
                  pl.BlockSpec((None, FFN_BLK, D_MODEL), lambda j, i: (j, 0, 0))] + more_specs,
        out_specs=[upb, last_rows(D_MODEL), last_rows(D_MODEL),
                   const((2, N_FFN_BLK, 1, FFN_BLK)), const((2, N_FFN_BLK, 3, 1, FFN_BLK)), const((1, D_MODEL))],
        out_shape=[jax.ShapeDtypeStruct((2, N_FFN_BLK, SEQ, FFN_BLK), BF16), jax.ShapeDtypeStruct((SEQ, D_MODEL), F32),
                   jax.ShapeDtypeStruct((SEQ, D_MODEL), BF16), jax.ShapeDtypeStruct((2, N_FFN_BLK, 1, FFN_BLK), F32),
                   jax.ShapeDtypeStruct((2, N_FFN_BLK, 3, 1, FFN_BLK), F32), jax.ShapeDtypeStruct((1, D_MODEL), F32)],
        scratch_shapes=[pltpu.VMEM((SEQ, D_MODEL), F32), pltpu.VMEM((2, SUBLANES, FFN_BLK), F32)],
        compiler_params=_params(("arbitrary", "arbitrary")),
    )(dh2, dh2b, h1, g2, up, pre, w_up, fcw, w_down, *more)


def _grad_tn(a_list, b, out_rows, name, after=None):
    n = len(a_list)
    ncol = b.shape[1]

    def body(*refs):
        a_refs, b_ref, o_ref = refs[:n], refs[n], refs[n + 1]
        j = pl.program_id(0)
        for k in range(n):
            @pl.when(j == k)
            def _(k=k):
                o_ref[...] = _dot_tn(a_refs[k][...], b_ref[...]).astype(BF16)

    full = lambda shape: pl.BlockSpec(shape, lambda j: (0,) * len(shape))
    body, more_specs, more = _ordered_behind(body, n + 1, after)
    return pl.pallas_call(
        body, name=name, grid=(n,),
        in_specs=[full((SEQ, out_rows))] * n + [full((SEQ, ncol))] + more_specs,
        out_specs=pl.BlockSpec((None, out_rows, ncol), lambda j: (j, 0, 0)),
        out_shape=jax.ShapeDtypeStruct((n, out_rows, ncol), BF16),
        compiler_params=_params(("arbitrary",)),
    )(*a_list, b, *more)


def _grad_tn_blocked(a, b, name, per_step=2):
    nb, _, a_w = a.shape
    b_w = b.shape[-1]

    def body(a_ref, b_ref, o_ref):
        for p in range(per_step):
            o_ref[p] = _dot_tn(a_ref[p], b_ref[...]).astype(BF16)

    return pl.pallas_call(
        body, name=name, grid=(nb // per_step,),
        in_specs=[pl.BlockSpec((per_step, SEQ, a_w), lambda k: (k, 0, 0)), pl.BlockSpec((SEQ, b_w), lambda k: (0, 0))],
        out_specs=pl.BlockSpec((per_step, a_w, b_w), lambda k: (k, 0, 0)),
        out_shape=jax.ShapeDtypeStruct((nb, a_w, b_w), BF16),
        compiler_params=_params(("arbitrary",)),
    )(a, b)


def _out_bwd(dh1b, w_out, y_attn, gattn, after=None):
    tm = 1024
    n_t = SEQ // tm

    def body(dh_ref, wo_ref, y_ref, ga_ref, dycn_ref, dy_ref, dga_ref):
        @pl.when(pl.program_id(0) == 0)
        def _():
            dga_ref[...] = jnp.zeros_like(dga_ref)

        dycat = _dot_nt(dh_ref[...], wo_ref[...])
        dycn_ref[...] = dycat[:, :CONV_WIDTH]
        dy, dga = _rms_bwd(y_ref[...], ga_ref[...], dycat[:, CONV_WIDTH:])
        dy_ref[...] = dy
        dga_ref[...] += dga

    rows = lambda w: pl.BlockSpec((tm, w), lambda i: (i, 0))
    const = lambda shape: pl.BlockSpec(shape, lambda i: (0,) * len(shape))
    body, more_specs, more = _ordered_behind(body, 4, after)
    return pl.pallas_call(
        body, name="out_bwd", grid=(n_t,),
        in_specs=[rows(D_MODEL), const((D_MODEL, D_MODEL)), rows(ATTN_WIDTH), const((1, ATTN_WIDTH))] + more_specs,
        out_specs=[rows(CONV_WIDTH), rows(ATTN_WIDTH), const((1, ATTN_WIDTH))],
        out_shape=[jax.ShapeDtypeStruct((SEQ, CONV_WIDTH), F32), jax.ShapeDtypeStruct((SEQ, ATTN_WIDTH), F32),
                   jax.ShapeDtypeStruct((1, ATTN_WIDTH), F32)],
        compiler_params=_params(("arbitrary",)),
    )(dh1b, w_out, y_attn, gattn, *more)


def _attn_bwd(qn, kn, v, dy, probs, psinks, bkt, after=None):
    n_b = SEQ // BLK

    def body(q_ref, k_ref, v_ref, dy_ref, p_ref, ps_ref, bkt_ref,
             dq_ref, dk_ref, dv_ref, dtbl_ref, dsink_ref, dbias_ref, dsacc_ref):
        step = pl.program_id(0)

        @pl.when(step == 0)
        def _():
            dbias_ref[...] = jnp.zeros_like(dbias_ref)
            dsacc_ref[...] = jnp.zeros_like(dsacc_ref)
            dk_ref[...] = jnp.zeros_like(dk_ref)
            dv_ref[...] = jnp.zeros_like(dv_ref)

        lane = lax.broadcasted_iota(jnp.int32, (BLK, 128), 1)
        for blk in range(QUERY_BLOCKS_PER_STEP):
            i = QUERY_BLOCKS_PER_STEP * step + blk
            rows = slice(BLK * blk, BLK * (blk + 1))
            kb, prev, cur = _band_rows(k_ref, i)
            vb, _, _ = _band_rows(v_ref, i)
            upper, _ = _band_masks(i)
            q = q_ref[rows, :]
            dy = dy_ref[rows, :]
            psink = ps_ref[rows, :]
            dsink = jnp.zeros((BLK, 128), F32)
            dqs, dks, dvs = [], [], []
            for g in range(N_HEADS // GQA_GROUP):
                kv = slice(HEAD_DIM * g, HEAD_DIM * (g + 1))
                qg = _stack_heads(q, g)
                dog = _stack_heads(dy, g).astype(BF16)
                pb = p_ref[blk, g]
                pg = pb.astype(F32)
                dprobs = _fold(_dot_nt(dog, vb[:, kv]), upper)
                dvs.append(_dot_tn(_unfold(pb, upper), dog))
                dsum = jnp.sum(pg * dprobs, axis=-1, keepdims=True)
                dlogits = pg * (dprobs - dsum)
                for t in range(GQA_GROUP):
                    dsink = jnp.where(lane == GQA_GROUP * g + t, -psink * dsum[BLK * t:BLK * (t + 1)], dsink)
                dbias_ref[g] += dlogits
                ds = _unfold(dlogits * (HEAD_DIM ** -0.5), upper).astype(BF16)
                dqs.append(_dot(ds, kb[:, kv]))
                dks.append(_dot_tn(ds, qg))
            dsacc_ref[...] += dsink
            dq_ref[rows, :] = _unstack_heads(dqs)
            dkb = jnp.concatenate(dks, axis=-1)
            dvb = jnp.concatenate(dvs, axis=-1)
            dk_ref[pl.ds(prev, BLK), :] += dkb[:BLK]
            dk_ref[pl.ds(cur, BLK), :] += dkb[BLK:]
            dv_ref[pl.ds(prev, BLK), :] += dvb[:BLK]
            dv_ref[pl.ds(cur, BLK), :] += dvb[BLK:]

        @pl.when(step == n_b // QUERY_BLOCKS_PER_STEP - 1)
        def _():
            bkt = bkt_ref[...]
            row8 = lax.broadcasted_iota(jnp.int32, (N_HEADS, 128), 0)
            lane8 = lax.broadcasted_iota(jnp.int32, (N_HEADS, 128), 1)
            acc = jnp.zeros((N_HEADS, 128), F32)
            for h in range(N_HEADS):
                rows = slice(BLK * (h % GQA_GROUP), BLK * (h % GQA_GROUP + 1))
                dbh = dbias_ref[h // GQA_GROUP, rows, :]
                for b in range(NUM_BUCKETS):
                    acc = jnp.where((row8 == h) & (lane8 == b), jnp.sum(jnp.where(bkt == b, dbh, 0.0)), acc)
            dsink_ref[...] = jnp.sum(dsacc_ref[...], axis=0, keepdims=True)
            dtbl_ref[...] = acc

    const = lambda shape: pl.BlockSpec(shape, lambda i: (0,) * len(shape))
    rows = lambda w: pl.BlockSpec((QUERY_BLOCKS_PER_STEP * BLK, w), lambda i: (i, 0))
    n_g = N_HEADS // GQA_GROUP
    body, more_specs, more = _ordered_behind(body, 7, after)
    return pl.pallas_call(
        body, name="attn_bwd", grid=(n_b // QUERY_BLOCKS_PER_STEP,),
        in_specs=[rows(ATTN_WIDTH), const((SEQ, KV_WIDTH)), const((SEQ, KV_WIDTH)), rows(ATTN_WIDTH),
                  pl.BlockSpec((QUERY_BLOCKS_PER_STEP, n_g, GROUP_ROWS, BLK), lambda i: (i, 0, 0, 0)), rows(128),
                  const((BLK, BLK))] + more_specs,
        out_specs=[rows(ATTN_WIDTH), const((SEQ, KV_WIDTH)), const((SEQ, KV_WIDTH)), const((N_HEADS, 128)), const((1, 128))],
        out_shape=[jax.ShapeDtypeStruct((SEQ, ATTN_WIDTH), F32), jax.ShapeDtypeStruct((SEQ, KV_WIDTH), F32),
                   jax.ShapeDtypeStruct((SEQ, KV_WIDTH), F32), jax.ShapeDtypeStruct((N_HEADS, 128), F32),
                   jax.ShapeDtypeStruct((1, 128), F32)],
        scratch_shapes=[pltpu.VMEM((n_g, GROUP_ROWS, BLK), F32), pltpu.VMEM((BLK, 128), F32)],
        compiler_params=_params(("arbitrary",)),
    )(qn, kn, v, dy, probs, psinks, bkt, *more)


def _mix_in_bwd(x, dh1, proj, dycn, dqn, dkn, dv, w_in_t, conv_w, g1, gq, gk, gconv):
    tm = 512
    n_t = SEQ // tm
    halo_blocks = tm // SUBLANES

    def body(x_ref, dh1_ref, proj_ref, halo_ref, dycn_ref, dqn_ref, dkn_ref, dv_ref, w_ref, cw_ref,
             g1_ref, gq_ref, gk_ref, gc_ref,
             dx_ref, dproj_ref, dcw_ref, dgc_ref, dgq_ref, dgk_ref, dg1_ref, next_ref):
        i = pl.program_id(0)
        first_tile = i == n_t - 1

        @pl.when(i == 0)
        def _():
            for r in (dcw_ref, dgc_ref, dgq_ref, dgk_ref, dg1_ref, next_ref):
                r[...] = jnp.zeros_like(r)

        proj = proj_ref[...]
        hp = halo_ref[...]
        gate_b = proj[:, 0:CONV_WIDTH]
        gate_c = proj[:, CONV_WIDTH:2 * CONV_WIDTH]
        hc = proj[:, 2 * CONV_WIDTH:3 * CONV_WIDTH]
        a = gate_c * hc
        a_halo = jnp.where(first_tile, 0.0, hp[:, CONV_WIDTH:2 * CONV_WIDTH] * hp[:, 2 * CONV_WIDTH:3 * CONV_WIDTH])
        cw = _taps(cw_ref[...])
        cv, a2, a1 = _conv3(a, cw, a_halo)
        dyc, dgc = _rms_bwd(gate_b * cv, gc_ref[...], dycn_ref[...])
        dgc_ref[...] += dgc
        dcv = dyc * gate_b
        dcw_ref[...] += jnp.concatenate(
            [jnp.sum(dcv * a2, axis=0, keepdims=True), jnp.sum(dcv * a1, axis=0, keepdims=True),
             jnp.sum(dcv * a, axis=0, keepdims=True)], axis=0)
        da = _conv3_bwd_input(dcv, cw, next_ref[...])
        next_ref[...] = dcv[:SUBLANES]
        q0 = 3 * CONV_WIDTH
        k0 = q0 + ATTN_WIDTH
        dq, dgq = _head_norm_bwd(proj[:, q0:k0], gq_ref[...], dqn_ref[...], N_HEADS)
        dk, dgk = _head_norm_bwd(proj[:, k0:k0 + KV_WIDTH], gk_ref[...], dkn_ref[...], 2)
        dgq_ref[...] += dgq
        dgk_ref[...] += dgk
        dproj = jnp.concatenate([dyc * cv, da * hc, da * gate_c, dq, dk, dv_ref[...]], axis=-1).astype(BF16)
        dproj_ref[...] = dproj
        du1 = _dot(dproj, w_ref[...])
        xv = x_ref[...]
        dn, dg1 = _rms_bwd(xv, g1_ref[...], du1)
        dx_ref[...] = dh1_ref[...] + dn
        dg1_ref[...] += dg1

    rev = lambda i: n_t - 1 - i
    rows = lambda w: pl.BlockSpec((tm, w), lambda i: (rev(i), 0))
    const = lambda shape: pl.BlockSpec(shape, lambda i: (0,) * len(shape))
    halo = pl.BlockSpec((SUBLANES, IN_WIDTH), lambda i: (jnp.maximum(rev(i) * halo_blocks - 1, 0), 0))
    return pl.pallas_call(
        body, name="mix_in_bwd", grid=(n_t,),
        in_specs=[rows(D_MODEL), rows(D_MODEL), rows(IN_WIDTH), halo, rows(CONV_WIDTH), rows(ATTN_WIDTH), rows(KV_WIDTH),
                  rows(KV_WIDTH), const((IN_WIDTH, D_MODEL)), const((3, CONV_WIDTH)), const((1, D_MODEL)),
                  const((1, HEAD_DIM)), const((1, HEAD_DIM)), const((1, CONV_WIDTH))],
        out_specs=[rows(D_MODEL), rows(IN_WIDTH), const((3, CONV_WIDTH)), const((1, CONV_WIDTH)),
                   const((1, HEAD_DIM)), const((1, HEAD_DIM)), const((1, D_MODEL))],
        out_shape=[jax.ShapeDtypeStruct((SEQ, D_MODEL), F32), jax.ShapeDtypeStruct((SEQ, IN_WIDTH), BF16),
                   jax.ShapeDtypeStruct((3, CONV_WIDTH), F32),
                   jax.ShapeDtypeStruct((1, CONV_WIDTH), F32), jax.ShapeDtypeStruct((1, HEAD_DIM), F32),
                   jax.ShapeDtypeStruct((1, HEAD_DIM), F32), jax.ShapeDtypeStruct((1, D_MODEL), F32)],
        scratch_shapes=[pltpu.VMEM((SUBLANES, CONV_WIDTH), F32)],
        compiler_params=_params(("arbitrary",)),
    )(x, dh1, proj, proj, dycn, dqn, dkn, dv, w_in_t, conv_w, g1, gq, gk, gconv)


def _grad_w_in(dproj, u1, after=None):
    bw = 768

    def body(a_ref, b_ref, o_ref):
        o_ref[...] = _dot_tn(a_ref[...], b_ref[...]).astype(BF16)

    body, more_specs, more = _ordered_behind(body, 2, after)
    return pl.pallas_call(
        body, name="grad_w_in", grid=(IN_WIDTH // bw,),
        in_specs=[pl.BlockSpec((SEQ, bw), lambda k: (0, k)), pl.BlockSpec((SEQ, D_MODEL), lambda k: (0, 0))] + more_specs,
        out_specs=pl.BlockSpec((bw, D_MODEL), lambda k: (k, 0)),
        out_shape=jax.ShapeDtypeStruct((IN_WIDTH, D_MODEL), BF16),
        compiler_params=_params(("arbitrary",)),
    )(dproj, u1, *more)


def _adamw_math(w, g, m, v):
    m = ADAM_B1 * m + (1.0 - ADAM_B1) * g
    v = ADAM_B2 * v + (1.0 - ADAM_B2) * (g * g)
    m_hat = m / (1.0 - ADAM_B1 ** ADAM_STEP)
    v_hat = v / (1.0 - ADAM_B2 ** ADAM_STEP)
    return -ADAM_LR * (m_hat / (jnp.sqrt(v_hat) + ADAM_EPS) + ADAM_WD * w), m, v


_ROW_G1, _ROW_G2, _ROW_OUT_NORMS, _ROW_FFN_B, _ROW_GQ, _ROW_GK, _ROW_SINKS, _ROW_LOSS, _ROW_TABLE = 0, 1, 2, 3, 11, 12, 13, 14, 16
SMALL_ROWS, SMALL_COLS = 24, 1024
_SMALL_NAMES = ("norm_mix_g", "norm_ffn_g", "out_norm_conv_g", "out_norm_attn_g", "ffn_conv_b", "q_norm_g", "k_norm_g",
                "sinks", "rel_bias_table")


def _pack_small_grads(dg1, dg2, dgconv, dgattn, dfb, dgq, dgk, dsinks, dtbl_t, loss_acc):
    def body(dg1_ref, dg2_ref, dgc_ref, dga_ref, dfb_ref, dgq_ref, dgk_ref, ds_ref, dt_ref, loss_ref, o_ref, all_ref):
        o_ref[...] = jnp.zeros_like(o_ref)
        o_ref[_ROW_G1:_ROW_G1 + 1, :] = dg1_ref[...]
        o_ref[_ROW_G2:_ROW_G2 + 1, :] = dg2_ref[...]
        o_ref[_ROW_OUT_NORMS:_ROW_OUT_NORMS + 1, 0:CONV_WIDTH] = dgc_ref[...]
        o_ref[_ROW_OUT_NORMS:_ROW_OUT_NORMS + 1, CONV_WIDTH:] = dga_ref[...]
        for k in range(N_DEV):
            o_ref[_ROW_FFN_B + k:_ROW_FFN_B + k + 1, 0:FFN_BLK] = dfb_ref[k // N_FFN_BLK, k % N_FFN_BLK]
        o_ref[_ROW_GQ:_ROW_GQ + 1, 0:HEAD_DIM] = dgq_ref[...]
        o_ref[_ROW_GK:_ROW_GK + 1, 0:HEAD_DIM] = dgk_ref[...]
        o_ref[_ROW_SINKS:_ROW_SINKS + 1, 0:128] = ds_ref[...]
        o_ref[_ROW_LOSS:_ROW_LOSS + 1, 0:128] = loss_ref[0:1, :]
        o_ref[_ROW_TABLE:_ROW_TABLE + N_HEADS, 0:128] = dt_ref[...]
        for s in range(N_DEV):
            all_ref[s] = o_ref[...]

    return pl.pallas_call(
        body, name="pack_small_grads",
        out_shape=[jax.ShapeDtypeStruct((SMALL_ROWS, SMALL_COLS), F32),
                   jax.ShapeDtypeStruct((N_DEV, SMALL_ROWS, SMALL_COLS), F32)],
    )(dg1, dg2, dgconv, dgattn, dfb, dgq, dgk, dsinks, dtbl_t, loss_acc)


def _adamw_small(recv, params, after):
    names = _SMALL_NAMES
    n = len(names)

    def grad_of(g, name, k=None):
        if name == "norm_mix_g":
            return g[_ROW_G1:_ROW_G1 + 1, :]
        if name == "norm_ffn_g":
            return g[_ROW_G2:_ROW_G2 + 1, :]
        if name == "out_norm_conv_g":
            return g[_ROW_OUT_NORMS:_ROW_OUT_NORMS + 1, 0:CONV_WIDTH]
        if name == "out_norm_attn_g":
            return g[_ROW_OUT_NORMS:_ROW_OUT_NORMS + 1, CONV_WIDTH:]
        if name == "ffn_conv_b":
            return g[_ROW_FFN_B + k:_ROW_FFN_B + k + 1, 0:FFN_BLK]
        if name == "q_norm_g":
            return g[_ROW_GQ:_ROW_GQ + 1, 0:HEAD_DIM]
        if name == "k_norm_g":
            return g[_ROW_GK:_ROW_GK + 1, 0:HEAD_DIM]
        if name == "sinks":
            return g[_ROW_SINKS:_ROW_SINKS + 1, 0:N_HEADS]
        return g[_ROW_TABLE:_ROW_TABLE + N_HEADS, 0:NUM_BUCKETS]

    def body(r_ref, *refs):
        ins, outs, loss_ref = refs[:3 * n], refs[3 * n:7 * n], refs[7 * n]
        g = r_ref[0]
        for s in range(1, N_DEV):
            g = g + r_ref[s]
        loss_ref[...] = g[_ROW_LOSS:_ROW_LOSS + 1, 0:128]
        for i, name in enumerate(names):
            w_ref, m_ref, v_ref = ins[3 * i:3 * i + 3]
            o = outs[4 * i:4 * i + 4]
            cols = [slice(FFN_BLK * k, FFN_BLK * (k + 1)) for k in range(N_DEV)] if name == "ffn_conv_b" else [slice(None)]
            for k, cs in enumerate(cols):
                gk = grad_of(g, name, k)
                d, m2, v2 = _adamw_math(w_ref[:, cs], gk, m_ref[:, cs], v_ref[:, cs])
                o[0][:, cs], o[1][:, cs], o[2][:, cs], o[3][:, cs] = gk, d, m2, v2

    flat = [a for name in names for a in params[name]]
    body, more_specs, more = _ordered_behind(body, 1 + 3 * n, after)
    vmem = pl.BlockSpec(memory_space=pltpu.VMEM)
    out = pl.pallas_call(
        body, name="adamw_small",
        in_specs=[vmem] * (1 + 3 * n) + more_specs,
        out_shape=[jax.ShapeDtypeStruct(params[name][0].shape, F32) for name in names for _ in range(4)]
        + [jax.ShapeDtypeStruct((1, 128), F32)],
        compiler_params=pltpu.CompilerParams(vmem_limit_bytes=VMEM_LIMIT),
    )(recv, *flat, *more)
    return {name: tuple(out[4 * i:4 * i + 4]) for i, name in enumerate(names)}, out[4 * n]


def _adamw_direct(w, m, v, own, recv, me, name, row_blocks=1, after=None):
    rb = w.shape[0] // row_blocks
    cols = w.shape[1]

    def body(me_ref, w_ref, m_ref, v_ref, o_ref, r_ref, g_o, d_o, m_o, v_o):
        g = o_ref[...].astype(F32)
        for s in range(N_DEV - 1):
            g = g + r_ref[s].astype(F32)
        g_o[...] = g
        d_o[...], m_o[...], v_o[...] = _adamw_math(w_ref[...], g, m_ref[...], v_ref[...])

    blk = pl.BlockSpec((rb, cols), lambda i, me_ref: (i, 0))
    oblk = pl.BlockSpec((None, rb, cols), lambda i, me_ref: (me_ref[0], i, 0))
    rblk = pl.BlockSpec((N_DEV - 1, rb, cols), lambda i, me_ref: (0, i, 0))
    body, more_specs, more = _ordered_behind(body, 6, after)
    return pl.pallas_call(
        body, name=name,
        grid_spec=pltpu.PrefetchScalarGridSpec(num_scalar_prefetch=1, grid=(row_blocks,),
                                               in_specs=[blk, blk, blk, oblk, rblk] + more_specs, out_specs=[blk] * 4),
        out_shape=[jax.ShapeDtypeStruct(w.shape, F32)] * 4,
        compiler_params=_params(("arbitrary",)),
    )(me, w, m, v, own, recv, *more)


def _adamw(w, m, v, part, recv, chip, name, row_blocks=1, after=None):
    rb = w.shape[0] // row_blocks
    tail = w.shape[1:]
    zeros = (0,) * len(tail)

    def body(chip_ref, w_ref, m_ref, v_ref, p_ref, r_ref, g_o, d_o, m_o, v_o):
        g = p_ref[...].astype(F32)
        for s in range(3):
            g = g + r_ref[s].astype(F32)
        g_o[...] = g
        d_o[...], m_o[...], v_o[...] = _adamw_math(w_ref[...], g, m_ref[...], v_ref[...])

    blk = pl.BlockSpec((rb,) + tail, lambda i, chip_ref: (i,) + zeros)
    pblk = pl.BlockSpec((None, rb) + tail, lambda i, chip_ref: (chip_ref[0], i) + zeros)
    rblk = pl.BlockSpec((3, rb) + tail, lambda i, chip_ref: (0, i) + zeros)
    body, more_specs, more = _ordered_behind(body, 6, after)
    return pl.pallas_call(
        body, name=name,
        grid_spec=pltpu.PrefetchScalarGridSpec(num_scalar_prefetch=1, grid=(row_blocks,),
                                               in_specs=[blk, blk, blk, pblk, rblk] + more_specs, out_specs=[blk] * 4),
        out_shape=[jax.ShapeDtypeStruct(w.shape, F32)] * 4,
        compiler_params=_params(("arbitrary",)),
    )(chip, w, m, v, part, recv, *more)


def kernel(x, norm_mix_g, w_in, conv_w, q_norm_g, k_norm_g, rel_bias_table, sinks, out_norm_conv_g, out_norm_attn_g, w_out, norm_ffn_g, w_up, ffn_conv_w, ffn_conv_b, w_down, loss_target, m_norm_mix_g, m_w_in, m_conv_w, m_q_norm_g, m_k_norm_g, m_rel_bias_table, m_sinks, m_out_norm_conv_g, m_out_norm_attn_g, m_w_out, m_norm_ffn_g, m_w_up, m_ffn_conv_w, m_ffn_conv_b, m_w_down, v_norm_mix_g, v_w_in, v_conv_w, v_q_norm_g, v_k_norm_g, v_rel_bias_table, v_sinks, v_out_norm_conv_g, v_out_norm_attn_g, v_w_out, v_norm_ffn_g, v_w_up, v_ffn_conv_w, v_ffn_conv_b, v_w_down):
    p = dict(norm_mix_g=norm_mix_g, w_in=w_in, conv_w=conv_w, q_norm_g=q_norm_g, k_norm_g=k_norm_g,
             rel_bias_table=rel_bias_table, sinks=sinks, out_norm_conv_g=out_norm_conv_g, out_norm_attn_g=out_norm_attn_g,
             w_out=w_out, norm_ffn_g=norm_ffn_g, w_up=w_up, ffn_conv_w=ffn_conv_w, ffn_conv_b=ffn_conv_b, w_down=w_down)
    m = dict(norm_mix_g=m_norm_mix_g, w_in=m_w_in, conv_w=m_conv_w, q_norm_g=m_q_norm_g, k_norm_g=m_k_norm_g,
             rel_bias_table=m_rel_bias_table, sinks=m_sinks, out_norm_conv_g=m_out_norm_conv_g,
             out_norm_attn_g=m_out_norm_attn_g, w_out=m_w_out, norm_ffn_g=m_norm_ffn_g, w_up=m_w_up,
             ffn_conv_w=m_ffn_conv_w, ffn_conv_b=m_ffn_conv_b, w_down=m_w_down)
    v = dict(norm_mix_g=v_norm_mix_g, w_in=v_w_in, conv_w=v_conv_w, q_norm_g=v_q_norm_g, k_norm_g=v_k_norm_g,
             rel_bias_table=v_rel_bias_table, sinks=v_sinks, out_norm_conv_g=v_out_norm_conv_g,
             out_norm_attn_g=v_out_norm_attn_g, w_out=v_w_out, norm_ffn_g=v_norm_ffn_g, w_up=v_w_up,
             ffn_conv_w=v_ffn_conv_w, ffn_conv_b=v_ffn_conv_b, w_down=v_w_down)

    xs, tgt = x[0], loss_target[0]
    g1, g2, gq, gk, gconv, gattn = norm_mix_g, norm_ffn_g, q_norm_g, k_norm_g, out_norm_conv_g, out_norm_attn_g
    ix, iy, ic = _coords()
    core = ic.astype(jnp.int32).reshape(1)
    chip = (2 * ix + iy).astype(jnp.int32).reshape(1)
    me = _lin(ix, iy, ic).astype(jnp.int32).reshape(1)
    bkt = jnp.asarray(_bucket_map())
    tr = lambda a: a[0].T
    taps = lambda a: jnp.transpose(a, (1, 0, 2))
    tbl_t = rel_bias_table.T

    wi_l, cw_l = _place_shards(me, [tr(w_in), taps(conv_w)], [BF16, F32], "place_mixer_shards")
    finish_a, token_a = _all_gather_split([wi_l, cw_l], "mixer", None)
    wo_l, wu_l, wd_l, fcw_l = _place_shards(me, [w_out[0], tr(w_up), w_down[0], taps(ffn_conv_w)],
                                            [BF16, BF16, BF16, F32], "place_ffn_shards", after=token_a)
    ffn_stage2, ffn_stage3, token_b = _all_gather_tree([wo_l, wu_l, wd_l, fcw_l], "ffn", token_a)
    wi_g, cw_g = finish_a(token_b)
    w_in_t = wi_g.reshape(IN_WIDTH, D_MODEL)
    conv_w_f = jnp.transpose(cw_g[:, :, 0, :], (1, 0, 2)).reshape(3, CONV_WIDTH)

    proj, u1, ycn, qn, kn, vv = _mix_in_fwd(xs, g1, w_in_t, conv_w_f, gq, gk, gconv)
    token_b2 = ffn_stage2(ycn)
    y_attn, yan, probs, psinks = _attn_fwd(qn, kn, vv, tbl_t, sinks, bkt, gattn, after=token_b2)
    wo_g, wu_g, wd_g, fcw_g = ffn_stage3(yan)
    w_out_f = wo_g.reshape(D_MODEL, D_MODEL)
    w_down_f = wd_g.reshape(N_FFN_BLK, FFN_BLK, D_MODEL)
    w_up_f = wu_g.reshape(2, N_FFN_BLK, FFN_BLK, D_MODEL)
    fcw_f = fcw_g.reshape(2, N_FFN_BLK, 3, 1, FFN_BLK)
    fcb = ffn_conv_b.reshape(2, N_FFN_BLK, 1, FFN_BLK)
    h1, u2, up, pre, act, dh2, dh2b, loss_acc = _ffn_fwd(xs, ycn, yan, w_out_f, g2, w_up_f, fcw_f, fcb, w_down_f, tgt)

    dw_down = _grad_tn_blocked(act, dh2b, "grad_w_down").reshape(N_DEV, D_FF // N_DEV, D_MODEL)
    plan_d, slots_d = _scatter_plan(1)
    d_sem = _split_start("scatter_w_down_start", [dw_down], [lax.empty((N_DEV - 1,) + dw_down.shape[1:], BF16)],
                         plan_d, None, _ALL_FOR_W_DOWN)
    dup, dh1, dh1b, dfb, dfcw, dg2 = _ffn_bwd(dh2, dh2b, h1, g2, up, pre, w_up_f, fcw_f, w_down_f, after=d_sem[4])
    dw_up = _grad_tn_blocked(dup.reshape(N_DEV, SEQ, FFN_BLK), u2, "grad_w_up")
    dw_out = _grad_tn([ycn, yan], dh1b, CONV_WIDTH, "grad_w_out").reshape(N_DEV, D_MODEL // N_DEV, D_MODEL)
    out_bwd = {}

    def behind_ffn(token):
        out_bwd["r"] = _out_bwd(dh1b, w_out_f, y_attn, gattn, after=token)
        return out_bwd["r"][0]

    finish_ffn, token_ffn = _reduce_scatter_split(
        [dw_up, dw_out, dfcw.reshape(N_DEV, 3, 1, FFN_BLK)], "ffn", core, behind_ffn)
    dycn, dy_attn, dgattn = out_bwd["r"]
    dqn, dkn, dv, dtbl_t, dsinks = _attn_bwd(qn, kn, vv, dy_attn, probs, psinks, bkt, after=token_ffn)
    dx, dproj, dcw, dgconv, dgq, dgk, dg1 = _mix_in_bwd(xs, dh1, proj, dycn, dqn, dkn, dv, w_in_t, conv_w_f,
                                                         g1, gq, gk, gconv)
    packed, packed_all = _pack_small_grads(dg1, dg2, dgconv, dgattn, dfb, dgq, dgk, dsinks, dtbl_t, loss_acc)
    plan_s, slots_s = _broadcast_plan()
    s_sem, r_sem, src_s, land_s, token_s = _split_start("gather_small_start", [packed], [packed_all], plan_s, None,
                                                        _ALL_FOR_SMALL)
    dw_in_t = _grad_w_in(dproj, u1, after=token_s).reshape(N_DEV, IN_WIDTH // N_DEV, D_MODEL)
    dcw_b = jnp.transpose(dcw.reshape(3, N_DEV, 1, CONV_WIDTH // N_DEV), (1, 0, 2, 3))
    adam = {}
    ffn_got = {}

    def behind_mixer(token):
        ffn_got["r"] = finish_ffn(token)
        return ffn_got["r"][1][0]

    finish_mixer, token_mixer = _reduce_scatter_split([dw_in_t, dcw_b], "mixer", core, behind_mixer)
    (p_wu, p_wo, p_fcw), (r_wu, r_wo, r_fcw) = ffn_got["r"]
    (own_wd,), (r_wd,) = _split_wait("scatter_w_down_wait", d_sem[0], d_sem[1], d_sem[2], d_sem[3], plan_d, slots_d,
                                     token_mixer)
    adam["w_down"] = _adamw_direct(w_down[0], m_w_down[0], v_w_down[0], own_wd, r_wd, me, "adamw_w_down", row_blocks=2)
    adam_up = _adamw(tr(w_up), tr(m_w_up), tr(v_w_up), p_wu, r_wu, chip, "adamw_w_up", row_blocks=4,
                     after=adam["w_down"][0])
    adam["w_out"] = _adamw(w_out[0], m_w_out[0], v_w_out[0], p_wo, r_wo, chip, "adamw_w_out", after=adam_up[0])
    adam_fcw = _adamw(taps(ffn_conv_w), taps(m_ffn_conv_w), taps(v_ffn_conv_w), p_fcw, r_fcw, chip, "adamw_ffn_conv_w",
                      after=adam["w_out"][0])
    _, (r_small,) = _split_wait("gather_small_wait", s_sem, r_sem, src_s, land_s, plan_s, slots_s, adam_fcw[0])
    small_in = {k: (p[k], m[k], v[k]) for k in _SMALL_NAMES}
    small_in["rel_bias_table"] = (tbl_t, m_rel_bias_table.T, v_rel_bias_table.T)
    small_out, loss_row = _adamw_small(r_small, small_in, None)
    (p_wi, p_cw), (r_wi, r_cw) = finish_mixer(loss_row)
    adam_in = _adamw(tr(w_in), tr(m_w_in), tr(v_w_in), p_wi, r_wi, chip, "adamw_w_in")
    adam_cw = _adamw(taps(conv_w), taps(m_conv_w), taps(v_conv_w), p_cw, r_cw, chip, "adamw_conv_w")

    res = {k: tuple(a[None] for a in t) for k, t in adam.items()}
    res["w_up"] = tuple(a.T[None] for a in adam_up)
    res["w_in"] = tuple(a.T[None] for a in adam_in)
    res["ffn_conv_w"] = tuple(taps(a) for a in adam_fcw)
    res["conv_w"] = tuple(taps(a) for a in adam_cw)
    res.update(small_out)
    res["rel_bias_table"] = tuple(a.T for a in small_out["rel_bias_table"])
    loss = loss_row[0, 0]
    order = ("norm_mix_g", "w_in", "conv_w", "q_norm_g", "k_norm_g", "rel_bias_table", "sinks", "out_norm_conv_g",
             "out_norm_attn_g", "w_out", "norm_ffn_g", "w_up", "ffn_conv_w", "ffn_conv_b", "w_down")
    return (loss, dx[None], *[res[k][0] for k in order], *[res[k][1] for k in order],
            *[res[k][2] for k in order], *[res[k][3] for k in order])
```

```python
import math

import numpy as np
import jax
import jax.numpy as jnp
from jax import lax
from jax.experimental import pallas as pl
from jax.experimental.pallas import tpu as pltpu

F32 = jnp.float32
BF16 = jnp.bfloat16

SEQ = 2048
D_MODEL = 1024
CONV_WIDTH = 512
ATTN_WIDTH = 512
KV_WIDTH = 128
HEAD_DIM = 64
N_HEADS = 8
GQA_GROUP = 4
IN_WIDTH = 2304
D_FF = 2816
BLK = 128
NUM_BUCKETS = 32
EPS = 1e-6
NEG_INF = -1e30
ADAM_LR = 0.001
ADAM_B1 = 0.9
ADAM_B2 = 0.999
ADAM_EPS = 1e-08
ADAM_WD = 0.01
ADAM_STEP = 10

N_DEV = 8
FFN_BLK = 2 * D_FF // N_DEV
N_FFN_BLK = D_FF // FFN_BLK
SUBLANES = 8
VMEM_LIMIT = 56 * 1024 * 1024

_MESH = pl.DeviceIdType.MESH
_ANY = pl.BlockSpec(memory_space=pl.ANY)


def _params(sem):
    return pltpu.CompilerParams(dimension_semantics=sem, vmem_limit_bytes=VMEM_LIMIT)


def _ordered_behind(body, pos, after):
    if after is None:
        return body, [], []
    return (lambda *refs: body(*refs[:pos], *refs[pos + 1:])), [_ANY], [after]


def _dot(a, b):
    return jnp.dot(a, b, preferred_element_type=F32)


def _dot_nt(a, b):
    return lax.dot_general(a, b, (((1,), (1,)), ((), ())), preferred_element_type=F32)


def _dot_tn(a, b):
    return lax.dot_general(a, b, (((0,), (0,)), ((), ())), preferred_element_type=F32)


def _shift_down(x, s, halo):
    r = pltpu.roll(x, s, axis=0)
    hr = pltpu.roll(halo, s, axis=0)
    row = lax.broadcasted_iota(jnp.int32, halo.shape, 0)
    top = jnp.where(row < s, hr, r[:SUBLANES])
    return jnp.concatenate([top, r[SUBLANES:]], axis=0)


def _shift_up(x, s, halo):
    n = x.shape[0]
    r = pltpu.roll(x, n - s, axis=0)
    hr = pltpu.roll(halo, SUBLANES - s, axis=0)
    row = lax.broadcasted_iota(jnp.int32, halo.shape, 0)
    bot = jnp.where(row >= SUBLANES - s, hr, r[n - SUBLANES:])
    return jnp.concatenate([r[:n - SUBLANES], bot], axis=0)


def _taps(w):
    return (w[0], w[1], w[2]) if len(w.shape) == 3 else (w[0:1], w[1:2], w[2:3])


def _conv3(x, w, halo):
    x2 = _shift_down(x, 2, halo)
    x1 = _shift_down(x, 1, halo)
    return x2 * w[0] + x1 * w[1] + x * w[2], x2, x1


def _conv3_bwd_input(dy, w, halo_next):
    return dy * w[2] + _shift_up(dy, 1, halo_next) * w[1] + _shift_up(dy, 2, halo_next) * w[0]


def _rstd(x):
    return lax.rsqrt(jnp.mean(x * x, axis=-1, keepdims=True) + EPS)


def _rms_bwd(x, g, dy):
    r = _rstd(x)
    n = x * r
    dn = dy * g
    dx = r * (dn - n * jnp.mean(dn * n, axis=-1, keepdims=True))
    return dx, jnp.sum(dy * n, axis=0, keepdims=True)


def _head_mean(x):
    width = x.shape[-1]
    ri = lax.broadcasted_iota(jnp.int32, (width, width), 0) // HEAD_DIM
    ci = lax.broadcasted_iota(jnp.int32, (width, width), 1) // HEAD_DIM
    ones = jnp.where(ri == ci, 1.0, 0.0).astype(BF16)
    hi = x.astype(BF16)
    lo = (x - hi.astype(F32)).astype(BF16)
    return (_dot(hi, ones) + _dot(lo, ones)) * (1.0 / HEAD_DIM)


def _head_norm(x, g, heads):
    return x * lax.rsqrt(_head_mean(x * x) + EPS) * jnp.tile(g, (1, heads))


def _head_norm_bwd(x, g, dy, heads):
    r = lax.rsqrt(_head_mean(x * x) + EPS)
    n = x * r
    dn = dy * jnp.tile(g, (1, heads))
    dx = r * (dn - n * _head_mean(dn * n))
    per_lane = jnp.sum(dy * n, axis=0, keepdims=True)
    dg = per_lane[:, 0:HEAD_DIM]
    for h in range(1, heads):
        dg = dg + per_lane[:, HEAD_DIM * h:HEAD_DIM * (h + 1)]
    return dx, dg


def _bucket_map():
    q = np.arange(BLK)[:, None]
    j = np.arange(BLK)[None, :]
    n = np.where(j > q, q + BLK - j, q - j)
    nf = np.maximum(n, 1).astype(np.float32)
    max_exact = NUM_BUCKETS // 2
    large = max_exact + (np.log(nf / max_exact) / math.log(BLK / max_exact) * (NUM_BUCKETS - max_exact)).astype(np.int32)
    large = np.minimum(large, NUM_BUCKETS - 1)
    return np.where(n < max_exact, n, large).astype(np.int32)


def _coords():
    return lax.axis_index("x"), lax.axis_index("y"), lax.axis_index("c")


def _lin(px, py, pc):
    return 4 * px + 2 * py + pc


def _chips(x, y):
    return [(1 - x, y), (x, 1 - y), (1 - x, 1 - y)]


def _peers(x, y, c):
    return [(1 - x if r & 4 else x, 1 - y if r & 2 else y, 1 - c if r & 1 else c) for r in range(1, N_DEV)]


_HBM = pl.BlockSpec(memory_space=pltpu.HBM)
_SEM = pl.BlockSpec(memory_space=pltpu.SEMAPHORE)
_EFFECT = pltpu.SideEffectType.DATAFLOW_SIDE_EFFECTING


def _in_hbm(a):
    return pltpu.with_memory_space_constraint(a, pltpu.HBM)


_SIBLING = (1, lambda x, y, c: [(x, y, 1 - c)])
_SIBLING_AND_CHIPS = (2, lambda x, y, c: [(x, y, 1 - c)] + [(cx, cy, c) for cx, cy in _chips(x, y)])
_SIBLING_AND_NEIGHBOURS = (3, lambda x, y, c: [(x, y, 1 - c), (1 - x, y, c), (x, 1 - y, c)])
_ONWARD_AND_SIBLING = (4, lambda x, y, c: [(jnp.where(c == 1, x, 1 - x), jnp.where(c == 1, 1 - y, y), c), (x, y, 1 - c)])
_FFN_GATHER = (_SIBLING_AND_NEIGHBOURS, _ONWARD_AND_SIBLING)
_MIXER_GATHER = ((8, _SIBLING_AND_NEIGHBOURS[1]), (9, _ONWARD_AND_SIBLING[1]))
_ALL_FOR_W_DOWN = (5, lambda x, y, c: _peers(x, y, c))
_CHIPS = (6, lambda x, y, c: [(cx, cy, c) for cx, cy in _chips(x, y)])
_ALL_FOR_SMALL = (7, lambda x, y, c: _peers(x, y, c))


def _split_start(name, srcs, lands, plan, after, handshake):
    ns, nl = len(srcs), len(lands)
    n_copies = len(plan(0, 0, 0))
    n_after = 0 if after is None else 1
    collective_id, peers_of = handshake

    def body(*refs):
        src_refs, land_refs = refs[:ns + nl], refs[ns:ns + nl]
        send_sems, recv_sems = refs[ns + nl + n_after], refs[ns + nl + n_after + 1]
        token = refs[-1]
        barrier = pltpu.get_barrier_semaphore()
        peers = peers_of(*_coords())
        for peer in peers:
            pl.semaphore_signal(barrier, inc=1, device_id=peer, device_id_type=_MESH)
        pl.semaphore_wait(barrier, len(peers))
        for k, (a, s_slot, l, d_slot, dev) in enumerate(plan(*_coords())):
            src = src_refs[a] if s_slot is None else src_refs[a].at[s_slot]
            pltpu.make_async_remote_copy(src_ref=src, dst_ref=land_refs[l].at[d_slot], send_sem=send_sems.at[k],
                                         recv_sem=recv_sems.at[k], device_id=dev, device_id_type=_MESH).start()
        token[...] = jnp.zeros_like(token)

    arrs = list(srcs) + list(lands)
    out = pl.pallas_call(
        body, name=name,
        out_shape=(pltpu.SemaphoreType.DMA((n_copies,)), pltpu.SemaphoreType.DMA((n_copies,)),
                   *[pltpu.HBM(a.shape, a.dtype) for a in arrs], jax.ShapeDtypeStruct((SUBLANES, 128), F32)),
        in_specs=[_HBM] * (ns + nl) + [_ANY] * n_after,
        out_specs=(_SEM, _SEM, *[_HBM] * (ns + nl), pl.BlockSpec(memory_space=pltpu.VMEM)),
        input_output_aliases={i: 2 + i for i in range(ns + nl)},
        compiler_params=pltpu.CompilerParams(has_side_effects=_EFFECT, collective_id=collective_id),
    )(*[_in_hbm(a) for a in arrs], *([] if after is None else [after]))
    return out[0], out[1], list(out[2:2 + ns]), list(out[2 + ns:2 + ns + nl]), out[-1]


def _split_wait(name, send_sems, recv_sems, srcs, lands, plan, recv_slots, after):
    ns, nl = len(srcs), len(lands)

    def body(*refs):
        src_refs, land_refs = refs[:ns + nl], refs[ns:ns + nl]
        send_sems, recv_sems = refs[ns + nl], refs[ns + nl + 1]
        coords = _coords()
        slots = recv_slots(*coords)
        for k, (a, s_slot, l, _, dev) in enumerate(plan(*coords)):
            src = src_refs[a] if s_slot is None else src_refs[a].at[s_slot]
            cp = pltpu.make_async_remote_copy(src_ref=src, dst_ref=land_refs[l].at[slots[k]], send_sem=send_sems.at[k],
                                              recv_sem=recv_sems.at[k], device_id=dev, device_id_type=_MESH)
            cp.wait_send()
            cp.wait_recv()

    arrs = list(srcs) + list(lands)
    out = pl.pallas_call(
        body, name=name,
        out_shape=tuple(pltpu.HBM(a.shape, a.dtype) for a in arrs),
        in_specs=[_HBM] * (ns + nl) + [_SEM, _SEM, _ANY],
        out_specs=tuple([_HBM] * (ns + nl)),
        input_output_aliases={i: i for i in range(ns + nl)},
        compiler_params=pltpu.CompilerParams(has_side_effects=_EFFECT),
    )(*arrs, send_sems, recv_sems, after)
    return list(out[:ns]), list(out[ns:])


def _gather_plan_ici(n):
    def plan(x, y, c):
        me = _lin(x, y, c)
        out = []
        for a in range(n):
            out.append((a, me, a, me, (x, y, 1 - c)))
            out += [(a, me, a, me, (cx, cy, c)) for cx, cy in _chips(x, y)]
        return out

    def recv_slots(x, y, c):
        out = []
        for _ in range(n):
            out.append(_lin(x, y, 1 - c))
            out += [_lin(cx, cy, c) for cx, cy in _chips(x, y)]
        return out

    return plan, recv_slots


def _gather_plan_d2d(n):
    def plan(x, y, c):
        return [(a, _lin(cx, cy, c), a, _lin(cx, cy, c), (x, y, 1 - c)) for a in range(n) for cx, cy in _chips(x, y)]

    def recv_slots(x, y, c):
        return [_lin(cx, cy, 1 - c) for _ in range(n) for cx, cy in _chips(x, y)]

    return plan, recv_slots


def _all_gather_split(lands, tag, after):
    n = len(lands)
    plan1, slots1 = _gather_plan_ici(n)
    s1, r1, _, lands, token = _split_start(f"gather_{tag}_ici_start", [], lands, plan1, after, _SIBLING_AND_CHIPS)

    def finish(after):
        _, got = _split_wait(f"gather_{tag}_ici_wait", s1, r1, [], lands, plan1, slots1, after)
        plan2, slots2 = _gather_plan_d2d(n)
        s2, r2, _, got, token2 = _split_start(f"gather_{tag}_d2d_start", [], got, plan2, None, _SIBLING)
        return _split_wait(f"gather_{tag}_d2d_wait", s2, r2, [], got, plan2, slots2, token2)[1]

    return finish, token


def _all_gather_tree(lands, tag, after, handshakes):
    n = len(lands)

    def plan1(x, y, c):
        me = _lin(x, y, c)
        return [(a, me, a, me, dev) for a in range(n) for dev in ((x, y, 1 - c), (1 - x, y, c), (x, 1 - y, c))]

    def slots1(x, y, c):
        return [s for _ in range(n) for s in (_lin(x, y, 1 - c), _lin(1 - x, y, c), _lin(x, 1 - y, c))]

    def plan2(x, y, c):
        from_x, from_y = _lin(1 - x, y, c), _lin(x, 1 - y, c)
        north = c == 1
        passed = jnp.where(north, from_x, from_y)
        onward = (jnp.where(north, x, 1 - x), jnp.where(north, 1 - y, y), c)
        sib = (x, y, 1 - c)
        return [cp for a in range(n) for cp in ((a, passed, a, passed, onward), (a, from_x, a, from_x, sib),
                                                (a, from_y, a, from_y, sib))]

    def slots2(x, y, c):
        return [s for _ in range(n) for s in (_lin(1 - x, 1 - y, c), _lin(1 - x, y, 1 - c), _lin(x, 1 - y, 1 - c))]

    def plan3(x, y, c):
        diag = _lin(1 - x, 1 - y, c)
        return [(a, diag, a, diag, (x, y, 1 - c)) for a in range(n)]

    def slots3(x, y, c):
        return [_lin(1 - x, 1 - y, 1 - c)] * n

    s1, r1, _, lands, token = _split_start(f"gather_{tag}_1_start", [], lands, plan1, after, handshakes[0])
    state = {}

    def stage2(after):
        _, got = _split_wait(f"gather_{tag}_1_wait", s1, r1, [], lands, plan1, slots1, after)
        state["s"], state["r"], _, state["lands"], token2 = _split_start(f"gather_{tag}_2_start", [], got, plan2, None,
                                                                         handshakes[1])
        return token2

    def stage3(after):
        _, got = _split_wait(f"gather_{tag}_2_wait", state["s"], state["r"], [], state["lands"], plan2, slots2, after)
        s3, r3, _, got, token3 = _split_start(f"gather_{tag}_3_start", [], got, plan3, None, _SIBLING)
        return _split_wait(f"gather_{tag}_3_wait", s3, r3, [], got, plan3, slots3, token3)[1]

    return stage2, stage3, token


_CHIP_LIST = ((0, 0), (0, 1), (1, 0), (1, 1))


def _reduce_plan_d2d(n):
    def plan(x, y, c):
        return [(a, _lin(qx, qy, 1 - c), a, q, (x, y, 1 - c)) for a in range(n) for q, (qx, qy) in enumerate(_CHIP_LIST)]

    def recv_slots(x, y, c):
        return [q for _ in range(n) for q in range(4)]

    return plan, recv_slots


def _reduce_plan_ici(n):
    def plan(x, y, c):
        return [(a, 2 * cx + cy, a, j, (cx, cy, c)) for a in range(n) for j, (cx, cy) in enumerate(_chips(x, y))]

    def recv_slots(x, y, c):
        return [j for _ in range(n) for j in range(3)]

    return plan, recv_slots


def _scatter_plan(n):
    def plan(x, y, c):
        return [(a, _lin(*peer), a, r, peer) for a in range(n) for r, peer in enumerate(_peers(x, y, c))]

    def recv_slots(x, y, c):
        return [r for _ in range(n) for r in range(N_DEV - 1)]

    return plan, recv_slots


def _broadcast_plan():
    def plan(x, y, c):
        return [(0, None, 0, _lin(x, y, c), peer) for peer in _peers(x, y, c)]

    def recv_slots(x, y, c):
        return [_lin(*peer) for peer in _peers(x, y, c)]

    return plan, recv_slots


def _chip_partial(grads, recvd, core, name):
    n = len(grads)

    def body(c_ref, *refs):
        for a in range(n):
            g_ref, r_ref, o_ref = refs[a], refs[n + a], refs[2 * n + a]
            o_ref[...] = (g_ref[...].astype(F32) + r_ref[...].astype(F32)).astype(o_ref.dtype)

    def blk(a, own):
        zeros = (0,) * (a.ndim - 1)
        return pl.BlockSpec((None,) + a.shape[1:],
                            (lambda q, c_ref: (2 * q + c_ref[0],) + zeros) if own else (lambda q, c_ref: (q,) + zeros))

    return pl.pallas_call(
        body, name=name,
        grid_spec=pltpu.PrefetchScalarGridSpec(
            num_scalar_prefetch=1, grid=(4,),
            in_specs=[blk(a, True) for a in grads] + [blk(a, False) for a in recvd],
            out_specs=[blk(a, False) for a in recvd]),
        out_shape=[jax.ShapeDtypeStruct(a.shape, a.dtype) for a in recvd],
        compiler_params=_params(("arbitrary",)),
    )(core, *grads, *recvd)


def _reduce_scatter_split(grads, tag, core, behind):
    n = len(grads)
    plan1, slots1 = _reduce_plan_d2d(n)
    lands1 = [lax.empty((4,) + a.shape[1:], a.dtype) for a in grads]
    s1, r1, srcs1, lands1, token1 = _split_start(f"reduce_{tag}_d2d_start", grads, lands1, plan1, None, _SIBLING)
    own, got = _split_wait(f"reduce_{tag}_d2d_wait", s1, r1, srcs1, lands1, plan1, slots1, behind(token1))
    parts = _chip_partial(own, got, core, f"reduce_{tag}_partial")
    plan2, slots2 = _reduce_plan_ici(n)
    lands2 = [lax.empty((3,) + a.shape[1:], a.dtype) for a in grads]
    s2, r2, srcs2, lands2, token2 = _split_start(f"reduce_{tag}_ici_start", parts, lands2, plan2, None, _CHIPS)

    def finish(after):
        return _split_wait(f"reduce_{tag}_ici_wait", s2, r2, srcs2, lands2, plan2, slots2, after)

    return finish, token2


def _place_shards(me, shards, dtypes, name, after=None):
    n = len(shards)

    def body(me_ref, *refs):
        for a in range(n):
            refs[n + a][...] = refs[a][...].astype(dtypes[a])

    full = lambda s: pl.BlockSpec(s.shape, lambda i, me_ref: (0,) * s.ndim)
    slot = lambda s: pl.BlockSpec((None,) + s.shape, lambda i, me_ref: (me_ref[0],) + (0,) * s.ndim)
    body, more_specs, more = _ordered_behind(body, 1 + n, after)
    return pl.pallas_call(
        body, name=name,
        grid_spec=pltpu.PrefetchScalarGridSpec(num_scalar_prefetch=1, grid=(1,),
                                               in_specs=[full(s) for s in shards] + more_specs,
                                               out_specs=[slot(s) for s in shards]),
        out_shape=[jax.ShapeDtypeStruct((N_DEV,) + s.shape, d) for s, d in zip(shards, dtypes)],
        compiler_params=_params(("arbitrary",)),
    )(me, *shards, *more)


def _mix_in_fwd(x, g1, w_in_t, conv_w, gq, gk, gconv):
    tm = 512
    n_t = SEQ // tm

    def body(x_ref, g1_ref, w_ref, cw_ref, gq_ref, gk_ref, gc_ref,
             proj_ref, u1_ref, ycn_ref, qn_ref, kn_ref, v_ref, halo_ref):
        @pl.when(pl.program_id(0) == 0)
        def _():
            halo_ref[...] = jnp.zeros_like(halo_ref)

        xv = x_ref[...]
        u = (xv * _rstd(xv) * g1_ref[...]).astype(BF16)
        u1_ref[...] = u
        proj = _dot_nt(u, w_ref[...])
        proj_ref[...] = proj
        gate_b = proj[:, 0:CONV_WIDTH]
        a = proj[:, CONV_WIDTH:2 * CONV_WIDTH] * proj[:, 2 * CONV_WIDTH:3 * CONV_WIDTH]
        cv, _, _ = _conv3(a, _taps(cw_ref[...]), halo_ref[...])
        halo_ref[...] = a[tm - SUBLANES:]
        yc = gate_b * cv
        ycn_ref[...] = (yc * _rstd(yc) * gc_ref[...]).astype(BF16)
        q0 = 3 * CONV_WIDTH
        qn_ref[...] = _head_norm(proj[:, q0:q0 + ATTN_WIDTH], gq_ref[...], N_HEADS).astype(BF16)
        k0 = q0 + ATTN_WIDTH
        kn_ref[...] = _head_norm(proj[:, k0:k0 + KV_WIDTH], gk_ref[...], 2).astype(BF16)
        v_ref[...] = proj[:, k0 + KV_WIDTH:k0 + 2 * KV_WIDTH].astype(BF16)

    const = lambda shape: pl.BlockSpec(shape, lambda i: (0,) * len(shape))
    rows = lambda w: pl.BlockSpec((tm, w), lambda i: (i, 0))
    return pl.pallas_call(
        body, name="mix_in_fwd", grid=(n_t,),
        in_specs=[rows(D_MODEL), const((1, D_MODEL)), const((IN_WIDTH, D_MODEL)), const((3, CONV_WIDTH)),
                  const((1, HEAD_DIM)), const((1, HEAD_DIM)), const((1, CONV_WIDTH))],
        out_specs=[rows(IN_WIDTH), rows(D_MODEL), rows(CONV_WIDTH), rows(ATTN_WIDTH), rows(KV_WIDTH), rows(KV_WIDTH)],
        out_shape=[jax.ShapeDtypeStruct((SEQ, IN_WIDTH), F32), jax.ShapeDtypeStruct((SEQ, D_MODEL), BF16),
                   jax.ShapeDtypeStruct((SEQ, CONV_WIDTH), BF16),
                   jax.ShapeDtypeStruct((SEQ, ATTN_WIDTH), BF16), jax.ShapeDtypeStruct((SEQ, KV_WIDTH), BF16),
                   jax.ShapeDtypeStruct((SEQ, KV_WIDTH), BF16)],
        scratch_shapes=[pltpu.VMEM((SUBLANES, CONV_WIDTH), F32)],
        compiler_params=_params(("arbitrary",)),
    )(x, g1, w_in_t, conv_w, gq, gk, gconv)


GROUP_ROWS = GQA_GROUP * BLK
QUERY_BLOCKS_PER_STEP = 2


def _band_bias(tbl_ref, bkt, bias_ref):
    for h in range(N_HEADS):
        acc = jnp.zeros(bkt.shape, F32)
        for b in range(NUM_BUCKETS):
            acc = jnp.where(bkt == b, tbl_ref[h, b], acc)
        bias_ref[h // GQA_GROUP, BLK * (h % GQA_GROUP):BLK * (h % GQA_GROUP + 1), :] = acc


def _band_masks(i):
    qi = lax.broadcasted_iota(jnp.int32, (GROUP_ROWS, BLK), 0) & (BLK - 1)
    ji = lax.broadcasted_iota(jnp.int32, (GROUP_ROWS, BLK), 1)
    upper = ji > qi
    return upper, upper & (i == 0)


def _stack_heads(x, g):
    return jnp.concatenate([x[:, HEAD_DIM * h:HEAD_DIM * (h + 1)] for h in range(GQA_GROUP * g, GQA_GROUP * (g + 1))], axis=0)


def _unstack_heads(groups):
    return jnp.concatenate([p[BLK * t:BLK * (t + 1)] for p in groups for t in range(GQA_GROUP)], axis=-1)


def _per_head_rows(vals):
    row = lax.broadcasted_iota(jnp.int32, (GROUP_ROWS, 1), 0)
    col = jnp.full((GROUP_ROWS, 1), vals[GQA_GROUP - 1], F32)
    for t in range(GQA_GROUP - 2, -1, -1):
        col = jnp.where(row < BLK * (t + 1), vals[t], col)
    return col


def _band_rows(ref, i):
    prev = pl.multiple_of(jnp.maximum(i - 1, 0) * BLK, BLK)
    cur = pl.multiple_of(i * BLK, BLK)
    return jnp.concatenate([ref[pl.ds(prev, BLK), :], ref[pl.ds(cur, BLK), :]], axis=0), prev, cur


def _fold(band, upper):
    return jnp.where(upper, band[:, :BLK], band[:, BLK:])


def _unfold(tile, upper):
    return jnp.concatenate([jnp.where(upper, tile, 0.0), jnp.where(upper, 0.0, tile)], axis=1)


def _head_probs(qh, kh, bias, upper, dead, sink):
    logits = _fold(_dot_nt(qh, kh), upper) * (HEAD_DIM ** -0.5) + bias
    logits = jnp.where(dead, NEG_INF, logits)
    m = jnp.maximum(jnp.max(logits, axis=-1, keepdims=True), sink)
    p = jnp.exp(logits - m)
    es = jnp.exp(sink - m)
    den = jnp.sum(p, axis=-1, keepdims=True) + es
    return p / den, es / den


def _attn_fwd(qn, kn, v, tbl, sinks, bkt, gattn, after=None):
    n_b = SEQ // BLK

    def body(q_ref, k_ref, v_ref, tbl_ref, sink_ref, bkt_ref, ga_ref, y_ref, yn_ref, p_ref, ps_ref, bias_ref):
        step = pl.program_id(0)

        @pl.when(step == 0)
        def _():
            _band_bias(tbl_ref, bkt_ref[...], bias_ref)

        lane = lax.broadcasted_iota(jnp.int32, (BLK, 128), 1)
        for b in range(QUERY_BLOCKS_PER_STEP):
            i = QUERY_BLOCKS_PER_STEP * step + b
            rows = slice(BLK * b, BLK * (b + 1))
            kb, _, _ = _band_rows(k_ref, i)
            vb, _, _ = _band_rows(v_ref, i)
            upper, dead = _band_masks(i)
            q = q_ref[rows, :]
            outs = []
            psinks = jnp.zeros((BLK, 128), F32)
            for g in range(N_HEADS // GQA_GROUP):
                kv = slice(HEAD_DIM * g, HEAD_DIM * (g + 1))
                sink = _per_head_rows([sink_ref[0, GQA_GROUP * g + t] for t in range(GQA_GROUP)])
                probs, psink = _head_probs(_stack_heads(q, g), kb[:, kv], bias_ref[g], upper, dead, sink)
                p_ref[b, g] = probs.astype(BF16)
                for t in range(GQA_GROUP):
                    psinks = jnp.where(lane == GQA_GROUP * g + t, psink[BLK * t:BLK * (t + 1)], psinks)
                outs.append(_dot(_unfold(probs, upper).astype(BF16), vb[:, kv]))
            ps_ref[rows, :] = psinks
            y = _unstack_heads(outs)
            y_ref[rows, :] = y
            yn_ref[rows, :] = (y * _rstd(y) * ga_ref[...]).astype(BF16)

    const = lambda shape: pl.BlockSpec(shape, lambda i: (0,) * len(shape))
    rows = lambda w: pl.BlockSpec((QUERY_BLOCKS_PER_STEP * BLK, w), lambda i: (i, 0))
    smem = pl.BlockSpec(memory_space=pltpu.SMEM)
    body, more_specs, more = _ordered_behind(body, 7, after)
    return pl.pallas_call(
        body, name="attn_fwd", grid=(n_b // QUERY_BLOCKS_PER_STEP,),
        in_specs=[rows(ATTN_WIDTH), const((SEQ, KV_WIDTH)), const((SEQ, KV_WIDTH)), smem, smem,
                  const((BLK, BLK)), const((1, ATTN_WIDTH))] + more_specs,
        out_specs=[rows(ATTN_WIDTH), rows(ATTN_WIDTH),
                   pl.BlockSpec((QUERY_BLOCKS_PER_STEP, N_HEADS // GQA_GROUP, GROUP_ROWS, BLK), lambda i: (i, 0, 0, 0)),
                   rows(128)],
        out_shape=[jax.ShapeDtypeStruct((SEQ, ATTN_WIDTH), F32), jax.ShapeDtypeStruct((SEQ, ATTN_WIDTH), BF16),
                   jax.ShapeDtypeStruct((n_b, N_HEADS // GQA_GROUP, GROUP_ROWS, BLK), BF16),
                   jax.ShapeDtypeStruct((SEQ, 128), F32)],
        scratch_shapes=[pltpu.VMEM((N_HEADS // GQA_GROUP, GROUP_ROWS, BLK), F32)],
        compiler_params=_params(("arbitrary",)),
    )(qn, kn, v, tbl, sinks, bkt, gattn, *more)


def _ffn_block(i, step):
    return jnp.where(i % 2 == 0, step, N_FFN_BLK - 1 - step)


def _ffn_fwd(x, ycn, yan, w_out, g2, w_up, fcw, fcb, w_down, tgt):
    tm = 512
    n_t = SEQ // tm

    def body(x_ref, ycn_ref, yan_ref, wo_ref, g2_ref, wu_ref, cw_ref, b_ref, wd_ref, tgt_ref,
             h1_ref, u2_ref, up_ref, pre_ref, act_ref, dh2_ref, dh2b_ref, loss_ref, acc_ref, halo_ref):
        i, step = pl.program_id(0), pl.program_id(1)
        j = _ffn_block(i, step)

        @pl.when((i == 0) & (step == 0))
        def _():
            loss_ref[...] = jnp.zeros_like(loss_ref)

        @pl.when(step == 0)
        def _():
            h1 = x_ref[...] + _dot(ycn_ref[...], wo_ref[0:CONV_WIDTH, :]) + _dot(yan_ref[...], wo_ref[CONV_WIDTH:, :])
            h1_ref[...] = h1
            u2_ref[...] = (h1 * _rstd(h1) * g2_ref[...]).astype(BF16)
            acc_ref[...] = jnp.zeros_like(acc_ref)

        u2 = u2_ref[...]
        pre = []
        for s in range(2):
            up = _dot_nt(u2, wu_ref[s])
            up_ref[s] = up.astype(BF16)
            halo = jnp.where(i == 0, 0.0, halo_ref[s, j])
            pre.append(_conv3(up, _taps(cw_ref.at[s]), halo)[0] + b_ref[s])
            pre_ref[s] = pre[s].astype(BF16)
            halo_ref[s, j] = up[tm - SUBLANES:]
        g, val = pre
        act = (g * jax.nn.sigmoid(g) * val).astype(BF16)
        act_ref[...] = act
        acc_ref[...] += _dot(act, wd_ref[...])

        @pl.when(step == N_FFN_BLK - 1)
        def _():
            err = h1_ref[...] + acc_ref[...] - tgt_ref[...]
            loss_ref[...] += 0.5 * jnp.sum(err * err) / D_MODEL
            dh2 = err / D_MODEL
            dh2_ref[...] = dh2
            dh2b_ref[...] = dh2.astype(BF16)

    rows = lambda w: pl.BlockSpec((tm, w), lambda i, step: (i, 0))
    const = lambda shape: pl.BlockSpec(shape, lambda i, step: (0,) * len(shape))
    pair = lambda *s: pl.BlockSpec((2, None) + s, lambda i, step: (0, _ffn_block(i, step)) + (0,) * len(s))
    upb = pl.BlockSpec((2, None, tm, FFN_BLK), lambda i, step: (0, _ffn_block(i, step), i, 0))
    return pl.pallas_call(
        body, name="ffn_fwd", grid=(n_t, N_FFN_BLK),
        in_specs=[rows(D_MODEL), rows(CONV_WIDTH), rows(ATTN_WIDTH), const((D_MODEL, D_MODEL)), const((1, D_MODEL)),
                  pair(FFN_BLK, D_MODEL), pair(3, 1, FFN_BLK), pair(1, FFN_BLK),
                  pl.BlockSpec((None, FFN_BLK, D_MODEL), lambda i, step: (_ffn_block(i, step), 0, 0)), rows(D_MODEL)],
        out_specs=[rows(D_MODEL), rows(D_MODEL), upb, upb,
                   pl.BlockSpec((None, tm, FFN_BLK), lambda i, step: (_ffn_block(i, step), i, 0)),
                   rows(D_MODEL), rows(D_MODEL), const((SUBLANES, 128))],
        out_shape=[jax.ShapeDtypeStruct((SEQ, D_MODEL), F32), jax.ShapeDtypeStruct((SEQ, D_MODEL), BF16),
                   jax.ShapeDtypeStruct((2, N_FFN_BLK, SEQ, FFN_BLK), BF16),
                   jax.ShapeDtypeStruct((2, N_FFN_BLK, SEQ, FFN_BLK), BF16),
                   jax.ShapeDtypeStruct((N_FFN_BLK, SEQ, FFN_BLK), BF16),
                   jax.ShapeDtypeStruct((SEQ, D_MODEL), F32), jax.ShapeDtypeStruct((SEQ, D_MODEL), BF16),
                   jax.ShapeDtypeStruct((SUBLANES, 128), F32)],
        scratch_shapes=[pltpu.VMEM((tm, D_MODEL), F32), pltpu.VMEM((2, N_FFN_BLK, SUBLANES, FFN_BLK), F32)],
        compiler_params=_params(("arbitrary", "arbitrary")),
    )(x, ycn, yan, w_out, g2, w_up, fcw, fcb, w_down, tgt)


def _ffn_bwd(dh2, dh2b, h1, g2, up, pre, w_up, fcw, w_down, after=None):
    tm = 512
    units = ((288, 224), (0, 288))
    n_t = SEQ // tm

    def body(dh2_ref, dh2b_ref, h1_ref, g2_ref, up_ref, pre_ref, wu_ref, cw_ref, wd_ref,
             dup_ref, dh1_ref, dh1b_ref, dfb_ref, dfcw_ref, dg2_ref, acc_ref, next_ref):
        j, i = pl.program_id(0), pl.program_id(1)
        tile = pl.ds(pl.multiple_of((n_t - 1 - i) * tm, tm), tm)

        @pl.when((j == 0) & (i == 0))
        def _():
            dfb_ref[...] = jnp.zeros_like(dfb_ref)
            dfcw_ref[...] = jnp.zeros_like(dfcw_ref)
            dg2_ref[...] = jnp.zeros_like(dg2_ref)

        @pl.when(j == 0)
        def _():
            acc_ref[tile, :] = jnp.zeros((tm, D_MODEL), F32)

        nxt = [jnp.where(i == 0, 0.0, next_ref[s]) for s in range(2)]
        sums = [[0.0] * 4 for _ in range(2)]
        for r0, rn in units:
            rows = slice(r0, r0 + rn)
            g, val = pre_ref[0, rows, :].astype(F32), pre_ref[1, rows, :].astype(F32)
            sg = jax.nn.sigmoid(g)
            silu = g * sg
            dact = _dot_nt(dh2b_ref[rows, :], wd_ref[...])
            dpre = (dact * val * (sg * (1.0 + g * (1.0 - sg))), dact * silu)
            dups = []
            for s in range(2):
                d = dpre[s]
                u = up_ref[s, rows, :].astype(F32)
                w = _taps(cw_ref.at[s])
                d1 = _shift_up(d, 1, nxt[s])
                d2 = _shift_up(d, 2, nxt[s])
                nxt[s] = d[:SUBLANES]
                for t, term in enumerate((d, d2 * u, d1 * u, d * u)):
                    sums[s][t] = sums[s][t] + jnp.sum(term, axis=0, keepdims=True)
                dups.append((d * w[2] + d1 * w[1] + d2 * w[0]).astype(BF16))
                dup_ref[s, rows, :] = dups[s]
            acc_rows = pl.ds(pl.multiple_of((n_t - 1 - i) * tm + r0, SUBLANES), rn)
            acc_ref[acc_rows, :] += _dot(dups[0], wu_ref[0]) + _dot(dups[1], wu_ref[1])
        for s in range(2):
            next_ref[s] = nxt[s]
            dfb_ref[s, j] += sums[s][0]
            for t in range(3):
                dfcw_ref[s, j, t] += sums[s][1 + t]

        @pl.when(j == N_FFN_BLK - 1)
        def _():
            dn, dgain = _rms_bwd(h1_ref[...], g2_ref[...], acc_ref[tile, :])
            dh1 = dh2_ref[...] + dn
            dh1_ref[...] = dh1
            dh1b_ref[...] = dh1.astype(BF16)
            dg2_ref[...] += dgain

    rev = lambda i: n_t - 1 - i
    rows = lambda w: pl.BlockSpec((tm, w), lambda j, i: (rev(i), 0))
    last_rows = lambda w: pl.BlockSpec((tm, w), lambda j, i: (jnp.where(j == N_FFN_BLK - 1, rev(i), rev(0)), 0))
    const = lambda shape: pl.BlockSpec(shape, lambda j, i: (0,) * len(shape))
    pair = lambda *s: pl.BlockSpec((2, None) + s, lambda j, i: (0, j) + (0,) * len(s))
    upb = pl.BlockSpec((2, None, tm, FFN_BLK), lambda j, i: (0, j, rev(i), 0))
    body, more_specs, more = _ordered_behind(body, 9, after)
    return pl.pallas_call(
        body, name="ffn_bwd", grid=(N_FFN_BLK, n_t),
        in_specs=[last_rows(D_MODEL), rows(D_MODEL), last_rows(D_MODEL), const((1, D_MODEL)), upb, upb,
                  pair(FFN_BLK, D_MODEL), pair(3, 1, FFN_BLK),
                  pl.BlockSpec((None, FFN_BLK, D_MODEL), lambda j, i: (j, 0, 0))] + more_specs,
        out_specs=[upb, last_rows(D_MODEL), last_rows(D_MODEL),
                   const((2, N_FFN_BLK, 1, FFN_BLK)), const((2, N_FFN_BLK, 3, 1, FFN_BLK)), const((1, D_MODEL))],
        out_shape=[jax.ShapeDtypeStruct((2, N_FFN_BLK, SEQ, FFN_BLK), BF16), jax.ShapeDtypeStruct((SEQ, D_MODEL), F32),
                   jax.ShapeDtypeStruct((SEQ, D_MODEL), BF16), jax.ShapeDtypeStruct((2, N_FFN_BLK, 1, FFN_BLK), F32),
                   jax.ShapeDtypeStruct((2, N_FFN_BLK, 3, 1, FFN_BLK), F32), jax.ShapeDtypeStruct((1, D_MODEL), F32)],
        scratch_shapes=[pltpu.VMEM((SEQ, D_MODEL), F32), pltpu.VMEM((2, SUBLANES, FFN_BLK), F32)],
        compiler_params=_params(("arbitrary", "arbitrary")),
    )(dh2, dh2b, h1, g2, up, pre, w_up, fcw, w_down, *more)


def _grad_tn(a_list, b, out_rows, name, after=None):
    n = len(a_list)
    ncol = b.shape[1]

    def body(*refs):
        a_refs, b_ref, o_ref = refs[:n], refs[n], refs[n + 1]
        j = pl.program_id(0)
        for k in range(n):
            @pl.when(j == k)
            def _(k=k):
                o_ref[...] = _dot_tn(a_refs[k][...], b_ref[...]).astype(BF16)

    full = lambda shape: pl.BlockSpec(shape, lambda j: (0,) * len(shape))
    body, more_specs, more = _ordered_behind(body, n + 1, after)
    return pl.pallas_call(
        body, name=name, grid=(n,),
        in_specs=[full((SEQ, out_rows))] * n + [full((SEQ, ncol))] + more_specs,
        out_specs=pl.BlockSpec((None, out_rows, ncol), lambda j: (j, 0, 0)),
        out_shape=jax.ShapeDtypeStruct((n, out_rows, ncol), BF16),
        compiler_params=_params(("arbitrary",)),
    )(*a_list, b, *more)


def _grad_tn_blocked(a, b, name, per_step=2):
    nb, _, a_w = a.shape
    b_w = b.shape[-1]

    def body(a_ref, b_ref, o_ref):
        for p in range(per_step):
            o_ref[p] = _dot_tn(a_ref[p], b_ref[...]).astype(BF16)

    return pl.pallas_call(
        body, name=name, grid=(nb // per_step,),
        in_specs=[pl.BlockSpec((per_step, SEQ, a_w), lambda k: (k, 0, 0)), pl.BlockSpec((SEQ, b_w), lambda k: (0, 0))],
        out_specs=pl.BlockSpec((per_step, a_w, b_w), lambda k: (k, 0, 0)),
        out_shape=jax.ShapeDtypeStruct((nb, a_w, b_w), BF16),
        compiler_params=_params(("arbitrary",)),
    )(a, b)


def _out_bwd(dh1b, w_out, y_attn, gattn, after=None):
    tm = 1024
    n_t = SEQ // tm

    def body(dh_ref, wo_ref, y_ref, ga_ref, dycn_ref, dy_ref, dga_ref):
        @pl.when(pl.program_id(0) == 0)
        def _():
            dga_ref[...] = jnp.zeros_like(dga_ref)

        dycat = _dot_nt(dh_ref[...], wo_ref[...])
        dycn_ref[...] = dycat[:, :CONV_WIDTH]
        dy, dga = _rms_bwd(y_ref[...], ga_ref[...], dycat[:, CONV_WIDTH:])
        dy_ref[...] = dy
        dga_ref[...] += dga

    rows = lambda w: pl.BlockSpec((tm, w), lambda i: (i, 0))
    const = lambda shape: pl.BlockSpec(shape, lambda i: (0,) * len(shape))
    body, more_specs, more = _ordered_behind(body, 4, after)
    return pl.pallas_call(
        body, name="out_bwd", grid=(n_t,),
        in_specs=[rows(D_MODEL), const((D_MODEL, D_MODEL)), rows(ATTN_WIDTH), const((1, ATTN_WIDTH))] + more_specs,
        out_specs=[rows(CONV_WIDTH), rows(ATTN_WIDTH), const((1, ATTN_WIDTH))],
        out_shape=[jax.ShapeDtypeStruct((SEQ, CONV_WIDTH), F32), jax.ShapeDtypeStruct((SEQ, ATTN_WIDTH), F32),
                   jax.ShapeDtypeStruct((1, ATTN_WIDTH), F32)],
        compiler_params=_params(("arbitrary",)),
    )(dh1b, w_out, y_attn, gattn, *more)


def _attn_bwd(qn, kn, v, dy, probs, psinks, bkt, after=None):
    n_b = SEQ // BLK

    def body(q_ref, k_ref, v_ref, dy_ref, p_ref, ps_ref, bkt_ref,
             dq_ref, dk_ref, dv_ref, dtbl_ref, dsink_ref, dbias_ref, dsacc_ref):
        step = pl.program_id(0)

        @pl.when(step == 0)
        def _():
            dbias_ref[...] = jnp.zeros_like(dbias_ref)
            dsacc_ref[...] = jnp.zeros_like(dsacc_ref)
            dk_ref[...] = jnp.zeros_like(dk_ref)
            dv_ref[...] = jnp.zeros_like(dv_ref)

        lane = lax.broadcasted_iota(jnp.int32, (BLK, 128), 1)
        for blk in range(QUERY_BLOCKS_PER_STEP):
            i = QUERY_BLOCKS_PER_STEP * step + blk
            rows = slice(BLK * blk, BLK * (blk + 1))
            kb, prev, cur = _band_rows(k_ref, i)
            vb, _, _ = _band_rows(v_ref, i)
            upper, _ = _band_masks(i)
            q = q_ref[rows, :]
            dy = dy_ref[rows, :]
            psink = ps_ref[rows, :]
            dsink = jnp.zeros((BLK, 128), F32)
            dqs, dks, dvs = [], [], []
            for g in range(N_HEADS // GQA_GROUP):
                kv = slice(HEAD_DIM * g, HEAD_DIM * (g + 1))
                qg = _stack_heads(q, g)
                dog = _stack_heads(dy, g).astype(BF16)
                pb = p_ref[blk, g]
                pg = pb.astype(F32)
                dprobs = _fold(_dot_nt(dog, vb[:, kv]), upper)
                dvs.append(_dot_tn(_unfold(pb, upper), dog))
                dsum = jnp.sum(pg * dprobs, axis=-1, keepdims=True)
                dlogits = pg * (dprobs - dsum)
                for t in range(GQA_GROUP):
                    dsink = jnp.where(lane == GQA_GROUP * g + t, -psink * dsum[BLK * t:BLK * (t + 1)], dsink)
                dbias_ref[g] += dlogits
                ds = _unfold(dlogits * (HEAD_DIM ** -0.5), upper).astype(BF16)
                dqs.append(_dot(ds, kb[:, kv]))
                dks.append(_dot_tn(ds, qg))
            dsacc_ref[...] += dsink
            dq_ref[rows, :] = _unstack_heads(dqs)
            dkb = jnp.concatenate(dks, axis=-1)
            dvb = jnp.concatenate(dvs, axis=-1)
            dk_ref[pl.ds(prev, BLK), :] += dkb[:BLK]
            dk_ref[pl.ds(cur, BLK), :] += dkb[BLK:]
            dv_ref[pl.ds(prev, BLK), :] += dvb[:BLK]
            dv_ref[pl.ds(cur, BLK), :] += dvb[BLK:]

        @pl.when(step == n_b // QUERY_BLOCKS_PER_STEP - 1)
        def _():
            bkt = bkt_ref[...]
            row8 = lax.broadcasted_iota(jnp.int32, (N_HEADS, 128), 0)
            lane8 = lax.broadcasted_iota(jnp.int32, (N_HEADS, 128), 1)
            acc = jnp.zeros((N_HEADS, 128), F32)
            for h in range(N_HEADS):
                rows = slice(BLK * (h % GQA_GROUP), BLK * (h % GQA_GROUP + 1))
                dbh = dbias_ref[h // GQA_GROUP, rows, :]
                for b in range(NUM_BUCKETS):
                    acc = jnp.where((row8 == h) & (lane8 == b), jnp.sum(jnp.where(bkt == b, dbh, 0.0)), acc)
            dsink_ref[...] = jnp.sum(dsacc_ref[...], axis=0, keepdims=True)
            dtbl_ref[...] = acc

    const = lambda shape: pl.BlockSpec(shape, lambda i: (0,) * len(shape))
    rows = lambda w: pl.BlockSpec((QUERY_BLOCKS_PER_STEP * BLK, w), lambda i: (i, 0))
    n_g = N_HEADS // GQA_GROUP
    body, more_specs, more = _ordered_behind(body, 7, after)
    return pl.pallas_call(
        body, name="attn_bwd", grid=(n_b // QUERY_BLOCKS_PER_STEP,),
        in_specs=[rows(ATTN_WIDTH), const((SEQ, KV_WIDTH)), const((SEQ, KV_WIDTH)), rows(ATTN_WIDTH),
                  pl.BlockSpec((QUERY_BLOCKS_PER_STEP, n_g, GROUP_ROWS, BLK), lambda i: (i, 0, 0, 0)), rows(128),
                  const((BLK, BLK))] + more_specs,
        out_specs=[rows(ATTN_WIDTH), const((SEQ, KV_WIDTH)), const((SEQ, KV_WIDTH)), const((N_HEADS, 128)), const((1, 128))],
        out_shape=[jax.ShapeDtypeStruct((SEQ, ATTN_WIDTH), F32), jax.ShapeDtypeStruct((SEQ, KV_WIDTH), F32),
                   jax.ShapeDtypeStruct((SEQ, KV_WIDTH), F32), jax.ShapeDtypeStruct((N_HEADS, 128), F32),
                   jax.ShapeDtypeStruct((1, 128), F32)],
        scratch_shapes=[pltpu.VMEM((n_g, GROUP_ROWS, BLK), F32), pltpu.VMEM((BLK, 128), F32)],
        compiler_params=_params(("arbitrary",)),
    )(qn, kn, v, dy, probs, psinks, bkt, *more)


def _mix_in_bwd(x, dh1, proj, dycn, dqn, dkn, dv, w_in_t, conv_w, g1, gq, gk, gconv):
    tm = 512
    n_t = SEQ // tm
    halo_blocks = tm // SUBLANES

    def body(x_ref, dh1_ref, proj_ref, halo_ref, dycn_ref, dqn_ref, dkn_ref, dv_ref, w_ref, cw_ref,
             g1_ref, gq_ref, gk_ref, gc_ref,
             dx_ref, dproj_ref, dcw_ref, dgc_ref, dgq_ref, dgk_ref, dg1_ref, next_ref):
        i = pl.program_id(0)
        first_tile = i == n_t - 1

        @pl.when(i == 0)
        def _():
            for r in (dcw_ref, dgc_ref, dgq_ref, dgk_ref, dg1_ref, next_ref):
                r[...] = jnp.zeros_like(r)

        proj = proj_ref[...]
        hp = halo_ref[...]
        gate_b = proj[:, 0:CONV_WIDTH]
        gate_c = proj[:, CONV_WIDTH:2 * CONV_WIDTH]
        hc = proj[:, 2 * CONV_WIDTH:3 * CONV_WIDTH]
        a = gate_c * hc
        a_halo = jnp.where(first_tile, 0.0, hp[:, CONV_WIDTH:2 * CONV_WIDTH] * hp[:, 2 * CONV_WIDTH:3 * CONV_WIDTH])
        cw = _taps(cw_ref[...])
        cv, a2, a1 = _conv3(a, cw, a_halo)
        dyc, dgc = _rms_bwd(gate_b * cv, gc_ref[...], dycn_ref[...])
        dgc_ref[...] += dgc
        dcv = dyc * gate_b
        dcw_ref[...] += jnp.concatenate(
            [jnp.sum(dcv * a2, axis=0, keepdims=True), jnp.sum(dcv * a1, axis=0, keepdims=True),
             jnp.sum(dcv * a, axis=0, keepdims=True)], axis=0)
        da = _conv3_bwd_input(dcv, cw, next_ref[...])
        next_ref[...] = dcv[:SUBLANES]
        q0 = 3 * CONV_WIDTH
        k0 = q0 + ATTN_WIDTH
        dq, dgq = _head_norm_bwd(proj[:, q0:k0], gq_ref[...], dqn_ref[...], N_HEADS)
        dk, dgk = _head_norm_bwd(proj[:, k0:k0 + KV_WIDTH], gk_ref[...], dkn_ref[...], 2)
        dgq_ref[...] += dgq
        dgk_ref[...] += dgk
        dproj = jnp.concatenate([dyc * cv, da * hc, da * gate_c, dq, dk, dv_ref[...]], axis=-1).astype(BF16)
        dproj_ref[...] = dproj
        du1 = _dot(dproj, w_ref[...])
        xv = x_ref[...]
        dn, dg1 = _rms_bwd(xv, g1_ref[...], du1)
        dx_ref[...] = dh1_ref[...] + dn
        dg1_ref[...] += dg1

    rev = lambda i: n_t - 1 - i
    rows = lambda w: pl.BlockSpec((tm, w), lambda i: (rev(i), 0))
    const = lambda shape: pl.BlockSpec(shape, lambda i: (0,) * len(shape))
    halo = pl.BlockSpec((SUBLANES, IN_WIDTH), lambda i: (jnp.maximum(rev(i) * halo_blocks - 1, 0), 0))
    return pl.pallas_call(
        body, name="mix_in_bwd", grid=(n_t,),
        in_specs=[rows(D_MODEL), rows(D_MODEL), rows(IN_WIDTH), halo, rows(CONV_WIDTH), rows(ATTN_WIDTH), rows(KV_WIDTH),
                  rows(KV_WIDTH), const((IN_WIDTH, D_MODEL)), const((3, CONV_WIDTH)), const((1, D_MODEL)),
                  const((1, HEAD_DIM)), const((1, HEAD_DIM)), const((1, CONV_WIDTH))],
        out_specs=[rows(D_MODEL), rows(IN_WIDTH), const((3, CONV_WIDTH)), const((1, CONV_WIDTH)),
                   const((1, HEAD_DIM)), const((1, HEAD_DIM)), const((1, D_MODEL))],
        out_shape=[jax.ShapeDtypeStruct((SEQ, D_MODEL), F32), jax.ShapeDtypeStruct((SEQ, IN_WIDTH), BF16),
                   jax.ShapeDtypeStruct((3, CONV_WIDTH), F32),
                   jax.ShapeDtypeStruct((1, CONV_WIDTH), F32), jax.ShapeDtypeStruct((1, HEAD_DIM), F32),
                   jax.ShapeDtypeStruct((1, HEAD_DIM), F32), jax.ShapeDtypeStruct((1, D_MODEL), F32)],
        scratch_shapes=[pltpu.VMEM((SUBLANES, CONV_WIDTH), F32)],
        compiler_params=_params(("arbitrary",)),
    )(x, dh1, proj, proj, dycn, dqn, dkn, dv, w_in_t, conv_w, g1, gq, gk, gconv)


def _grad_w_in(dproj, u1, after=None):
    bw = 768

    def body(a_ref, b_ref, o_ref):
        o_ref[...] = _dot_tn(a_ref[...], b_ref[...]).astype(BF16)

    body, more_specs, more = _ordered_behind(body, 2, after)
    return pl.pallas_call(
        body, name="grad_w_in", grid=(IN_WIDTH // bw,),
        in_specs=[pl.BlockSpec((SEQ, bw), lambda k: (0, k)), pl.BlockSpec((SEQ, D_MODEL), lambda k: (0, 0))] + more_specs,
        out_specs=pl.BlockSpec((bw, D_MODEL), lambda k: (k, 0)),
        out_shape=jax.ShapeDtypeStruct((IN_WIDTH, D_MODEL), BF16),
        compiler_params=_params(("arbitrary",)),
    )(dproj, u1, *more)


def _adamw_math(w, g, m, v):
    m = ADAM_B1 * m + (1.0 - ADAM_B1) * g
    v = ADAM_B2 * v + (1.0 - ADAM_B2) * (g * g)
    m_hat = m / (1.0 - ADAM_B1 ** ADAM_STEP)
    v_hat = v / (1.0 - ADAM_B2 ** ADAM_STEP)
    return -ADAM_LR * (m_hat / (jnp.sqrt(v_hat) + ADAM_EPS) + ADAM_WD * w), m, v


_ROW_G1, _ROW_G2, _ROW_OUT_NORMS, _ROW_FFN_B, _ROW_GQ, _ROW_GK, _ROW_SINKS, _ROW_LOSS, _ROW_TABLE = 0, 1, 2, 3, 11, 12, 13, 14, 16
SMALL_ROWS, SMALL_COLS = 24, 1024
_SMALL_NAMES = ("norm_mix_g", "norm_ffn_g", "out_norm_conv_g", "out_norm_attn_g", "ffn_conv_b", "q_norm_g", "k_norm_g",
                "sinks", "rel_bias_table")


def _pack_small_grads(dg1, dg2, dgconv, dgattn, dfb, dgq, dgk, dsinks, dtbl_t, loss_acc):
    def body(dg1_ref, dg2_ref, dgc_ref, dga_ref, dfb_ref, dgq_ref, dgk_ref, ds_ref, dt_ref, loss_ref, o_ref, all_ref):
        o_ref[...] = jnp.zeros_like(o_ref)
        o_ref[_ROW_G1:_ROW_G1 + 1, :] = dg1_ref[...]
        o_ref[_ROW_G2:_ROW_G2 + 1, :] = dg2_ref[...]
        o_ref[_ROW_OUT_NORMS:_ROW_OUT_NORMS + 1, 0:CONV_WIDTH] = dgc_ref[...]
        o_ref[_ROW_OUT_NORMS:_ROW_OUT_NORMS + 1, CONV_WIDTH:] = dga_ref[...]
        for k in range(N_DEV):
            o_ref[_ROW_FFN_B + k:_ROW_FFN_B + k + 1, 0:FFN_BLK] = dfb_ref[k // N_FFN_BLK, k % N_FFN_BLK]
        o_ref[_ROW_GQ:_ROW_GQ + 1, 0:HEAD_DIM] = dgq_ref[...]
        o_ref[_ROW_GK:_ROW_GK + 1, 0:HEAD_DIM] = dgk_ref[...]
        o_ref[_ROW_SINKS:_ROW_SINKS + 1, 0:128] = ds_ref[...]
        o_ref[_ROW_LOSS:_ROW_LOSS + 1, 0:128] = loss_ref[0:1, :]
        o_ref[_ROW_TABLE:_ROW_TABLE + N_HEADS, 0:128] = dt_ref[...]
        for s in range(N_DEV):
            all_ref[s] = o_ref[...]

    return pl.pallas_call(
        body, name="pack_small_grads",
        out_shape=[jax.ShapeDtypeStruct((SMALL_ROWS, SMALL_COLS), F32),
                   jax.ShapeDtypeStruct((N_DEV, SMALL_ROWS, SMALL_COLS), F32)],
    )(dg1, dg2, dgconv, dgattn, dfb, dgq, dgk, dsinks, dtbl_t, loss_acc)


def _adamw_small(recv, params, after):
    names = _SMALL_NAMES
    n = len(names)

    def grad_of(g, name, k=None):
        if name == "norm_mix_g":
            return g[_ROW_G1:_ROW_G1 + 1, :]
        if name == "norm_ffn_g":
            return g[_ROW_G2:_ROW_G2 + 1, :]
        if name == "out_norm_conv_g":
            return g[_ROW_OUT_NORMS:_ROW_OUT_NORMS + 1, 0:CONV_WIDTH]
        if name == "out_norm_attn_g":
            return g[_ROW_OUT_NORMS:_ROW_OUT_NORMS + 1, CONV_WIDTH:]
        if name == "ffn_conv_b":
            return g[_ROW_FFN_B + k:_ROW_FFN_B + k + 1, 0:FFN_BLK]
        if name == "q_norm_g":
            return g[_ROW_GQ:_ROW_GQ + 1, 0:HEAD_DIM]
        if name == "k_norm_g":
            return g[_ROW_GK:_ROW_GK + 1, 0:HEAD_DIM]
        if name == "sinks":
            return g[_ROW_SINKS:_ROW_SINKS + 1, 0:N_HEADS]
        return g[_ROW_TABLE:_ROW_TABLE + N_HEADS, 0:NUM_BUCKETS]

    def body(r_ref, *refs):
        ins, outs, loss_ref = refs[:3 * n], refs[3 * n:7 * n], refs[7 * n]
        g = r_ref[0]
        for s in range(1, N_DEV):
            g = g + r_ref[s]
        loss_ref[...] = g[_ROW_LOSS:_ROW_LOSS + 1, 0:128]
        for i, name in enumerate(names):
            w_ref, m_ref, v_ref = ins[3 * i:3 * i + 3]
            o = outs[4 * i:4 * i + 4]
            cols = [slice(FFN_BLK * k, FFN_BLK * (k + 1)) for k in range(N_DEV)] if name == "ffn_conv_b" else [slice(None)]
            for k, cs in enumerate(cols):
                gk = grad_of(g, name, k)
                d, m2, v2 = _adamw_math(w_ref[:, cs], gk, m_ref[:, cs], v_ref[:, cs])
                o[0][:, cs], o[1][:, cs], o[2][:, cs], o[3][:, cs] = gk, d, m2, v2

    flat = [a for name in names for a in params[name]]
    body, more_specs, more = _ordered_behind(body, 1 + 3 * n, after)
    vmem = pl.BlockSpec(memory_space=pltpu.VMEM)
    out = pl.pallas_call(
        body, name="adamw_small",
        in_specs=[vmem] * (1 + 3 * n) + more_specs,
        out_shape=[jax.ShapeDtypeStruct(params[name][0].shape, F32) for name in names for _ in range(4)]
        + [jax.ShapeDtypeStruct((1, 128), F32)],
        compiler_params=pltpu.CompilerParams(vmem_limit_bytes=VMEM_LIMIT),
    )(recv, *flat, *more)
    return {name: tuple(out[4 * i:4 * i + 4]) for i, name in enumerate(names)}, out[4 * n]


def _adamw_direct(w, m, v, own, recv, me, name, row_blocks=1, after=None):
    rb = w.shape[0] // row_blocks
    cols = w.shape[1]

    def body(me_ref, w_ref, m_ref, v_ref, o_ref, r_ref, g_o, d_o, m_o, v_o):
        g = o_ref[...].astype(F32)
        for s in range(N_DEV - 1):
            g = g + r_ref[s].astype(F32)
        g_o[...] = g
        d_o[...], m_o[...], v_o[...] = _adamw_math(w_ref[...], g, m_ref[...], v_ref[...])

    blk = pl.BlockSpec((rb, cols), lambda i, me_ref: (i, 0))
    oblk = pl.BlockSpec((None, rb, cols), lambda i, me_ref: (me_ref[0], i, 0))
    rblk = pl.BlockSpec((N_DEV - 1, rb, cols), lambda i, me_ref: (0, i, 0))
    body, more_specs, more = _ordered_behind(body, 6, after)
    return pl.pallas_call(
        body, name=name,
        grid_spec=pltpu.PrefetchScalarGridSpec(num_scalar_prefetch=1, grid=(row_blocks,),
                                               in_specs=[blk, blk, blk, oblk, rblk] + more_specs, out_specs=[blk] * 4),
        out_shape=[jax.ShapeDtypeStruct(w.shape, F32)] * 4,
        compiler_params=_params(("arbitrary",)),
    )(me, w, m, v, own, recv, *more)


def _adamw(w, m, v, part, recv, chip, name, row_blocks=1, after=None):
    rb = w.shape[0] // row_blocks
    tail = w.shape[1:]
    zeros = (0,) * len(tail)

    def body(chip_ref, w_ref, m_ref, v_ref, p_ref, r_ref, g_o, d_o, m_o, v_o):
        g = p_ref[...].astype(F32)
        for s in range(3):
            g = g + r_ref[s].astype(F32)
        g_o[...] = g
        d_o[...], m_o[...], v_o[...] = _adamw_math(w_ref[...], g, m_ref[...], v_ref[...])

    blk = pl.BlockSpec((rb,) + tail, lambda i, chip_ref: (i,) + zeros)
    pblk = pl.BlockSpec((None, rb) + tail, lambda i, chip_ref: (chip_ref[0], i) + zeros)
    rblk = pl.BlockSpec((3, rb) + tail, lambda i, chip_ref: (0, i) + zeros)
    body, more_specs, more = _ordered_behind(body, 6, after)
    return pl.pallas_call(
        body, name=name,
        grid_spec=pltpu.PrefetchScalarGridSpec(num_scalar_prefetch=1, grid=(row_blocks,),
                                               in_specs=[blk, blk, blk, pblk, rblk] + more_specs, out_specs=[blk] * 4),
        out_shape=[jax.ShapeDtypeStruct(w.shape, F32)] * 4,
        compiler_params=_params(("arbitrary",)),
    )(chip, w, m, v, part, recv, *more)


def kernel(x, norm_mix_g, w_in, conv_w, q_norm_g, k_norm_g, rel_bias_table, sinks, out_norm_conv_g, out_norm_attn_g, w_out, norm_ffn_g, w_up, ffn_conv_w, ffn_conv_b, w_down, loss_target, m_norm_mix_g, m_w_in, m_conv_w, m_q_norm_g, m_k_norm_g, m_rel_bias_table, m_sinks, m_out_norm_conv_g, m_out_norm_attn_g, m_w_out, m_norm_ffn_g, m_w_up, m_ffn_conv_w, m_ffn_conv_b, m_w_down, v_norm_mix_g, v_w_in, v_conv_w, v_q_norm_g, v_k_norm_g, v_rel_bias_table, v_sinks, v_out_norm_conv_g, v_out_norm_attn_g, v_w_out, v_norm_ffn_g, v_w_up, v_ffn_conv_w, v_ffn_conv_b, v_w_down):
    p = dict(norm_mix_g=norm_mix_g, w_in=w_in, conv_w=conv_w, q_norm_g=q_norm_g, k_norm_g=k_norm_g,
             rel_bias_table=rel_bias_table, sinks=sinks, out_norm_conv_g=out_norm_conv_g, out_norm_attn_g=out_norm_attn_g,
             w_out=w_out, norm_ffn_g=norm_ffn_g, w_up=w_up, ffn_conv_w=ffn_conv_w, ffn_conv_b=ffn_conv_b, w_down=w_down)
    m = dict(norm_mix_g=m_norm_mix_g, w_in=m_w_in, conv_w=m_conv_w, q_norm_g=m_q_norm_g, k_norm_g=m_k_norm_g,
             rel_bias_table=m_rel_bias_table, sinks=m_sinks, out_norm_conv_g=m_out_norm_conv_g,
             out_norm_attn_g=m_out_norm_attn_g, w_out=m_w_out, norm_ffn_g=m_norm_ffn_g, w_up=m_w_up,
             ffn_conv_w=m_ffn_conv_w, ffn_conv_b=m_ffn_conv_b, w_down=m_w_down)
    v = dict(norm_mix_g=v_norm_mix_g, w_in=v_w_in, conv_w=v_conv_w, q_norm_g=v_q_norm_g, k_norm_g=v_k_norm_g,
             rel_bias_table=v_rel_bias_table, sinks=v_sinks, out_norm_conv_g=v_out_norm_conv_g,
             out_norm_attn_g=v_out_norm_attn_g, w_out=v_w_out, norm_ffn_g=v_norm_ffn_g, w_up=v_w_up,
             ffn_conv_w=v_ffn_conv_w, ffn_conv_b=v_ffn_conv_b, w_down=v_w_down)

    xs, tgt = x[0], loss_target[0]
    g1, g2, gq, gk, gconv, gattn = norm_mix_g, norm_ffn_g, q_norm_g, k_norm_g, out_norm_conv_g, out_norm_attn_g
    ix, iy, ic = _coords()
    core = ic.astype(jnp.int32).reshape(1)
    chip = (2 * ix + iy).astype(jnp.int32).reshape(1)
    me = _lin(ix, iy, ic).astype(jnp.int32).reshape(1)
    bkt = jnp.asarray(_bucket_map())
    tr = lambda a: a[0].T
    taps = lambda a: jnp.transpose(a, (1, 0, 2))
    tbl_t = rel_bias_table.T

    wi_l, cw_l = _place_shards(me, [tr(w_in), taps(conv_w)], [BF16, F32], "place_mixer_shards")
    mixer_stage2, mixer_stage3, token_a = _all_gather_tree([wi_l, cw_l], "mixer", None, _MIXER_GATHER)
    wo_l, wu_l, wd_l, fcw_l = _place_shards(me, [w_out[0], tr(w_up), w_down[0], taps(ffn_conv_w)],
                                            [BF16, BF16, BF16, F32], "place_ffn_shards", after=token_a)
    ffn_stage2, ffn_stage3, token_b = _all_gather_tree([wo_l, wu_l, wd_l, fcw_l], "ffn", token_a, _FFN_GATHER)
    wi_g, cw_g = mixer_stage3(mixer_stage2(token_b))
    w_in_t = wi_g.reshape(IN_WIDTH, D_MODEL)
    conv_w_f = jnp.transpose(cw_g[:, :, 0, :], (1, 0, 2)).reshape(3, CONV_WIDTH)

    proj, u1, ycn, qn, kn, vv = _mix_in_fwd(xs, g1, w_in_t, conv_w_f, gq, gk, gconv)
    token_b2 = ffn_stage2(ycn)
    y_attn, yan, probs, psinks = _attn_fwd(qn, kn, vv, tbl_t, sinks, bkt, gattn, after=token_b2)
    wo_g, wu_g, wd_g, fcw_g = ffn_stage3(yan)
    w_out_f = wo_g.reshape(D_MODEL, D_MODEL)
    w_down_f = wd_g.reshape(N_FFN_BLK, FFN_BLK, D_MODEL)
    w_up_f = wu_g.reshape(2, N_FFN_BLK, FFN_BLK, D_MODEL)
    fcw_f = fcw_g.reshape(2, N_FFN_BLK, 3, 1, FFN_BLK)
    fcb = ffn_conv_b.reshape(2, N_FFN_BLK, 1, FFN_BLK)
    h1, u2, up, pre, act, dh2, dh2b, loss_acc = _ffn_fwd(xs, ycn, yan, w_out_f, g2, w_up_f, fcw_f, fcb, w_down_f, tgt)

    dw_down = _grad_tn_blocked(act, dh2b, "grad_w_down").reshape(N_DEV, D_FF // N_DEV, D_MODEL)
    plan_d, slots_d = _scatter_plan(1)
    d_sem = _split_start("scatter_w_down_start", [dw_down], [lax.empty((N_DEV - 1,) + dw_down.shape[1:], BF16)],
                         plan_d, None, _ALL_FOR_W_DOWN)
    dup, dh1, dh1b, dfb, dfcw, dg2 = _ffn_bwd(dh2, dh2b, h1, g2, up, pre, w_up_f, fcw_f, w_down_f, after=d_sem[4])
    dw_up = _grad_tn_blocked(dup.reshape(N_DEV, SEQ, FFN_BLK), u2, "grad_w_up")
    dw_out = _grad_tn([ycn, yan], dh1b, CONV_WIDTH, "grad_w_out").reshape(N_DEV, D_MODEL // N_DEV, D_MODEL)
    out_bwd = {}

    def behind_ffn(token):
        out_bwd["r"] = _out_bwd(dh1b, w_out_f, y_attn, gattn, after=token)
        return out_bwd["r"][0]

    finish_ffn, token_ffn = _reduce_scatter_split(
        [dw_up, dw_out, dfcw.reshape(N_DEV, 3, 1, FFN_BLK)], "ffn", core, behind_ffn)
    dycn, dy_attn, dgattn = out_bwd["r"]
    dqn, dkn, dv, dtbl_t, dsinks = _attn_bwd(qn, kn, vv, dy_attn, probs, psinks, bkt, after=token_ffn)
    dx, dproj, dcw, dgconv, dgq, dgk, dg1 = _mix_in_bwd(xs, dh1, proj, dycn, dqn, dkn, dv, w_in_t, conv_w_f,
                                                         g1, gq, gk, gconv)
    packed, packed_all = _pack_small_grads(dg1, dg2, dgconv, dgattn, dfb, dgq, dgk, dsinks, dtbl_t, loss_acc)
    plan_s, slots_s = _broadcast_plan()
    s_sem, r_sem, src_s, land_s, token_s = _split_start("gather_small_start", [packed], [packed_all], plan_s, None,
                                                        _ALL_FOR_SMALL)
    dw_in_t = _grad_w_in(dproj, u1, after=token_s).reshape(N_DEV, IN_WIDTH // N_DEV, D_MODEL)
    dcw_b = jnp.transpose(dcw.reshape(3, N_DEV, 1, CONV_WIDTH // N_DEV), (1, 0, 2, 3))
    adam = {}
    ffn_got = {}

    def behind_mixer(token):
        ffn_got["r"] = finish_ffn(token)
        return ffn_got["r"][1][0]

    finish_mixer, token_mixer = _reduce_scatter_split([dw_in_t, dcw_b], "mixer", core, behind_mixer)
    (p_wu, p_wo, p_fcw), (r_wu, r_wo, r_fcw) = ffn_got["r"]
    (own_wd,), (r_wd,) = _split_wait("scatter_w_down_wait", d_sem[0], d_sem[1], d_sem[2], d_sem[3], plan_d, slots_d,
                                     token_mixer)
    adam["w_down"] = _adamw_direct(w_down[0], m_w_down[0], v_w_down[0], own_wd, r_wd, me, "adamw_w_down", row_blocks=2)
    adam_up = _adamw(tr(w_up), tr(m_w_up), tr(v_w_up), p_wu, r_wu, chip, "adamw_w_up", row_blocks=4,
                     after=adam["w_down"][0])
    adam["w_out"] = _adamw(w_out[0], m_w_out[0], v_w_out[0], p_wo, r_wo, chip, "adamw_w_out", after=adam_up[0])
    adam_fcw = _adamw(taps(ffn_conv_w), taps(m_ffn_conv_w), taps(v_ffn_conv_w), p_fcw, r_fcw, chip, "adamw_ffn_conv_w",
                      after=adam["w_out"][0])
    _, (r_small,) = _split_wait("gather_small_wait", s_sem, r_sem, src_s, land_s, plan_s, slots_s, adam_fcw[0])
    small_in = {k: (p[k], m[k], v[k]) for k in _SMALL_NAMES}
    small_in["rel_bias_table"] = (tbl_t, m_rel_bias_table.T, v_rel_bias_table.T)
    small_out, loss_row = _adamw_small(r_small, small_in, None)
    (p_wi, p_cw), (r_wi, r_cw) = finish_mixer(loss_row)
    adam_in = _adamw(tr(w_in), tr(m_w_in), tr(v_w_in), p_wi, r_wi, chip, "adamw_w_in")
    adam_cw = _adamw(taps(conv_w), taps(m_conv_w), taps(v_conv_w), p_cw, r_cw, chip, "adamw_conv_w")

    res = {k: tuple(a[None] for a in t) for k, t in adam.items()}
    res["w_up"] = tuple(a.T[None] for a in adam_up)
    res["w_in"] = tuple(a.T[None] for a in adam_in)
    res["ffn_conv_w"] = tuple(taps(a) for a in adam_fcw)
    res["conv_w"] = tuple(taps(a) for a in adam_cw)
    res.update(small_out)
    res["rel_bias_table"] = tuple(a.T for a in small_out["rel_bias_table"])
    loss = loss_row[0, 0]
    order = ("norm_mix_g", "w_in", "conv_w", "q_norm_g", "k_norm_g", "rel_bias_table", "sinks", "out_norm_conv_g",
             "out_norm_attn_g", "w_out", "norm_ffn_g", "w_up", "ffn_conv_w", "ffn_conv_b", "w_down")
    return (loss, dx[None], *[res[k][0] for k in order], *[res[k][1] for k in order],
            *[res[k][2] for k in order], *[res[k][3] for k in order])
```

```python
import math

import numpy as np
import jax
import jax.numpy as jnp
from jax import lax
from jax.experimental import pallas as pl
from jax.experimental.pallas import tpu as pltpu

F32 = jnp.float32
BF16 = jnp.bfloat16

SEQ = 2048
D_MODEL = 1024
CONV_WIDTH = 512
ATTN_WIDTH = 512
KV_WIDTH = 128
HEAD_DIM = 64
N_HEADS = 8
GQA_GROUP = 4
IN_WIDTH = 2304
D_FF = 2816
BLK = 128
NUM_BUCKETS = 32
EPS = 1e-6
NEG_INF = -1e30
ADAM_LR = 0.001
ADAM_B1 = 0.9
ADAM_B2 = 0.999
ADAM_EPS = 1e-08
ADAM_WD = 0.01
ADAM_STEP = 10

N_DEV = 8
FFN_BLK = 2 * D_FF // N_DEV
N_FFN_BLK = D_FF // FFN_BLK
SUBLANES = 8
VMEM_LIMIT = 56 * 1024 * 1024

_MESH = pl.DeviceIdType.MESH
_ANY = pl.BlockSpec(memory_space=pl.ANY)


def _params(sem):
    return pltpu.CompilerParams(dimension_semantics=sem, vmem_limit_bytes=VMEM_LIMIT)


def _ordered_behind(body, pos, after):
    if after is None:
        return body, [], []
    return (lambda *refs: body(*refs[:pos], *refs[pos + 1:])), [_ANY], [after]


def _dot(a, b):
    return jnp.dot(a, b, preferred_element_type=F32)


def _dot_nt(a, b):
    return lax.dot_general(a, b, (((1,), (1,)), ((), ())), preferred_element_type=F32)


def _dot_tn(a, b):
    return lax.dot_general(a, b, (((0,), (0,)), ((), ())), preferred_element_type=F32)


def _shift_down(x, s, halo):
    r = pltpu.roll(x, s, axis=0)
    hr = pltpu.roll(halo, s, axis=0)
    row = lax.broadcasted_iota(jnp.int32, halo.shape, 0)
    top = jnp.where(row < s, hr, r[:SUBLANES])
    return jnp.concatenate([top, r[SUBLANES:]], axis=0)


def _shift_up(x, s, halo):
    n = x.shape[0]
    r = pltpu.roll(x, n - s, axis=0)
    hr = pltpu.roll(halo, SUBLANES - s, axis=0)
    row = lax.broadcasted_iota(jnp.int32, halo.shape, 0)
    bot = jnp.where(row >= SUBLANES - s, hr, r[n - SUBLANES:])
    return jnp.concatenate([r[:n - SUBLANES], bot], axis=0)


def _taps(w):
    return (w[0], w[1], w[2]) if len(w.shape) == 3 else (w[0:1], w[1:2], w[2:3])


def _conv3(x, w, halo):
    x2 = _shift_down(x, 2, halo)
    x1 = _shift_down(x, 1, halo)
    return x2 * w[0] + x1 * w[1] + x * w[2], x2, x1


def _conv3_bwd_input(dy, w, halo_next):
    return dy * w[2] + _shift_up(dy, 1, halo_next) * w[1] + _shift_up(dy, 2, halo_next) * w[0]


def _rstd(x):
    return lax.rsqrt(jnp.mean(x * x, axis=-1, keepdims=True) + EPS)


def _rms_bwd(x, g, dy):
    r = _rstd(x)
    n = x * r
    dn = dy * g
    dx = r * (dn - n * jnp.mean(dn * n, axis=-1, keepdims=True))
    return dx, jnp.sum(dy * n, axis=0, keepdims=True)


def _head_mean(x):
    width = x.shape[-1]
    ri = lax.broadcasted_iota(jnp.int32, (width, width), 0) // HEAD_DIM
    ci = lax.broadcasted_iota(jnp.int32, (width, width), 1) // HEAD_DIM
    ones = jnp.where(ri == ci, 1.0, 0.0).astype(BF16)
    hi = x.astype(BF16)
    lo = (x - hi.astype(F32)).astype(BF16)
    return (_dot(hi, ones) + _dot(lo, ones)) * (1.0 / HEAD_DIM)


def _head_norm(x, g, heads):
    return x * lax.rsqrt(_head_mean(x * x) + EPS) * jnp.tile(g, (1, heads))


def _head_norm_bwd(x, g, dy, heads):
    r = lax.rsqrt(_head_mean(x * x) + EPS)
    n = x * r
    dn = dy * jnp.tile(g, (1, heads))
    dx = r * (dn - n * _head_mean(dn * n))
    per_lane = jnp.sum(dy * n, axis=0, keepdims=True)
    dg = per_lane[:, 0:HEAD_DIM]
    for h in range(1, heads):
        dg = dg + per_lane[:, HEAD_DIM * h:HEAD_DIM * (h + 1)]
    return dx, dg


def _bucket_map():
    q = np.arange(BLK)[:, None]
    j = np.arange(BLK)[None, :]
    n = np.where(j > q, q + BLK - j, q - j)
    nf = np.maximum(n, 1).astype(np.float32)
    max_exact = NUM_BUCKETS // 2
    large = max_exact + (np.log(nf / max_exact) / math.log(BLK / max_exact) * (NUM_BUCKETS - max_exact)).astype(np.int32)
    large = np.minimum(large, NUM_BUCKETS - 1)
    return np.where(n < max_exact, n, large).astype(np.int32)


def _coords():
    return lax.axis_index("x"), lax.axis_index("y"), lax.axis_index("c")


def _lin(px, py, pc):
    return 4 * px + 2 * py + pc


def _chips(x, y):
    return [(1 - x, y), (x, 1 - y), (1 - x, 1 - y)]


def _peers(x, y, c):
    return [(1 - x if r & 4 else x, 1 - y if r & 2 else y, 1 - c if r & 1 else c) for r in range(1, N_DEV)]


_HBM = pl.BlockSpec(memory_space=pltpu.HBM)
_SEM = pl.BlockSpec(memory_space=pltpu.SEMAPHORE)
_EFFECT = pltpu.SideEffectType.DATAFLOW_SIDE_EFFECTING


def _in_hbm(a):
    return pltpu.with_memory_space_constraint(a, pltpu.HBM)


_SIBLING = (1, lambda x, y, c: [(x, y, 1 - c)])
_SIBLING_AND_CHIPS = (2, lambda x, y, c: [(x, y, 1 - c)] + [(cx, cy, c) for cx, cy in _chips(x, y)])
_SIBLING_AND_NEIGHBOURS = (3, lambda x, y, c: [(x, y, 1 - c), (1 - x, y, c), (x, 1 - y, c)])
_ONWARD_AND_SIBLING = (4, lambda x, y, c: [(jnp.where(c == 1, x, 1 - x), jnp.where(c == 1, 1 - y, y), c), (x, y, 1 - c)])
_ALL_FOR_W_DOWN = (5, lambda x, y, c: _peers(x, y, c))
_CHIPS = (6, lambda x, y, c: [(cx, cy, c) for cx, cy in _chips(x, y)])
_ALL_FOR_SMALL = (7, lambda x, y, c: _peers(x, y, c))


def _split_start(name, srcs, lands, plan, after, handshake):
    ns, nl = len(srcs), len(lands)
    n_copies = len(plan(0, 0, 0))
    n_after = 0 if after is None else 1
    collective_id, peers_of = handshake

    def body(*refs):
        src_refs, land_refs = refs[:ns + nl], refs[ns:ns + nl]
        send_sems, recv_sems = refs[ns + nl + n_after], refs[ns + nl + n_after + 1]
        token = refs[-1]
        barrier = pltpu.get_barrier_semaphore()
        peers = peers_of(*_coords())
        for peer in peers:
            pl.semaphore_signal(barrier, inc=1, device_id=peer, device_id_type=_MESH)
        pl.semaphore_wait(barrier, len(peers))
        for k, (a, s_slot, l, d_slot, dev) in enumerate(plan(*_coords())):
            src = src_refs[a] if s_slot is None else src_refs[a].at[s_slot]
            pltpu.make_async_remote_copy(src_ref=src, dst_ref=land_refs[l].at[d_slot], send_sem=send_sems.at[k],
                                         recv_sem=recv_sems.at[k], device_id=dev, device_id_type=_MESH).start()
        token[...] = jnp.zeros_like(token)

    arrs = list(srcs) + list(lands)
    out = pl.pallas_call(
        body, name=name,
        out_shape=(pltpu.SemaphoreType.DMA((n_copies,)), pltpu.SemaphoreType.DMA((n_copies,)),
                   *[pltpu.HBM(a.shape, a.dtype) for a in arrs], jax.ShapeDtypeStruct((SUBLANES, 128), F32)),
        in_specs=[_HBM] * (ns + nl) + [_ANY] * n_after,
        out_specs=(_SEM, _SEM, *[_HBM] * (ns + nl), pl.BlockSpec(memory_space=pltpu.VMEM)),
        input_output_aliases={i: 2 + i for i in range(ns + nl)},
        compiler_params=pltpu.CompilerParams(has_side_effects=_EFFECT, collective_id=collective_id),
    )(*[_in_hbm(a) for a in arrs], *([] if after is None else [after]))
    return out[0], out[1], list(out[2:2 + ns]), list(out[2 + ns:2 + ns + nl]), out[-1]


def _split_wait(name, send_sems, recv_sems, srcs, lands, plan, recv_slots, after):
    ns, nl = len(srcs), len(lands)

    def body(*refs):
        src_refs, land_refs = refs[:ns + nl], refs[ns:ns + nl]
        send_sems, recv_sems = refs[ns + nl], refs[ns + nl + 1]
        coords = _coords()
        slots = recv_slots(*coords)
        for k, (a, s_slot, l, _, dev) in enumerate(plan(*coords)):
            src = src_refs[a] if s_slot is None else src_refs[a].at[s_slot]
            cp = pltpu.make_async_remote_copy(src_ref=src, dst_ref=land_refs[l].at[slots[k]], send_sem=send_sems.at[k],
                                              recv_sem=recv_sems.at[k], device_id=dev, device_id_type=_MESH)
            cp.wait_send()
            cp.wait_recv()

    arrs = list(srcs) + list(lands)
    out = pl.pallas_call(
        body, name=name,
        out_shape=tuple(pltpu.HBM(a.shape, a.dtype) for a in arrs),
        in_specs=[_HBM] * (ns + nl) + [_SEM, _SEM, _ANY],
        out_specs=tuple([_HBM] * (ns + nl)),
        input_output_aliases={i: i for i in range(ns + nl)},
        compiler_params=pltpu.CompilerParams(has_side_effects=_EFFECT),
    )(*arrs, send_sems, recv_sems, after)
    return list(out[:ns]), list(out[ns:])


def _gather_plan_ici(n):
    def plan(x, y, c):
        me = _lin(x, y, c)
        out = []
        for a in range(n):
            out.append((a, me, a, me, (x, y, 1 - c)))
            out += [(a, me, a, me, (cx, cy, c)) for cx, cy in _chips(x, y)]
        return out

    def recv_slots(x, y, c):
        out = []
        for _ in range(n):
            out.append(_lin(x, y, 1 - c))
            out += [_lin(cx, cy, c) for cx, cy in _chips(x, y)]
        return out

    return plan, recv_slots


def _gather_plan_d2d(n):
    def plan(x, y, c):
        return [(a, _lin(cx, cy, c), a, _lin(cx, cy, c), (x, y, 1 - c)) for a in range(n) for cx, cy in _chips(x, y)]

    def recv_slots(x, y, c):
        return [_lin(cx, cy, 1 - c) for _ in range(n) for cx, cy in _chips(x, y)]

    return plan, recv_slots


def _all_gather_split(lands, tag, after):
    n = len(lands)
    plan1, slots1 = _gather_plan_ici(n)
    s1, r1, _, lands, token = _split_start(f"gather_{tag}_ici_start", [], lands, plan1, after, _SIBLING_AND_CHIPS)

    def finish(after):
        _, got = _split_wait(f"gather_{tag}_ici_wait", s1, r1, [], lands, plan1, slots1, after)
        plan2, slots2 = _gather_plan_d2d(n)
        s2, r2, _, got, token2 = _split_start(f"gather_{tag}_d2d_start", [], got, plan2, None, _SIBLING)
        return _split_wait(f"gather_{tag}_d2d_wait", s2, r2, [], got, plan2, slots2, token2)[1]

    return finish, token


def _all_gather_tree(lands, tag, after):
    n = len(lands)

    def plan1(x, y, c):
        me = _lin(x, y, c)
        return [(a, me, a, me, dev) for a in range(n) for dev in ((x, y, 1 - c), (1 - x, y, c), (x, 1 - y, c))]

    def slots1(x, y, c):
        return [s for _ in range(n) for s in (_lin(x, y, 1 - c), _lin(1 - x, y, c), _lin(x, 1 - y, c))]

    def plan2(x, y, c):
        from_x, from_y = _lin(1 - x, y, c), _lin(x, 1 - y, c)
        north = c == 1
        passed = jnp.where(north, from_x, from_y)
        onward = (jnp.where(north, x, 1 - x), jnp.where(north, 1 - y, y), c)
        sib = (x, y, 1 - c)
        return [cp for a in range(n) for cp in ((a, passed, a, passed, onward), (a, from_x, a, from_x, sib),
                                                (a, from_y, a, from_y, sib))]

    def slots2(x, y, c):
        return [s for _ in range(n) for s in (_lin(1 - x, 1 - y, c), _lin(1 - x, y, 1 - c), _lin(x, 1 - y, 1 - c))]

    def plan3(x, y, c):
        diag = _lin(1 - x, 1 - y, c)
        return [(a, diag, a, diag, (x, y, 1 - c)) for a in range(n)]

    def slots3(x, y, c):
        return [_lin(1 - x, 1 - y, 1 - c)] * n

    s1, r1, _, lands, token = _split_start(f"gather_{tag}_1_start", [], lands, plan1, after, _SIBLING_AND_NEIGHBOURS)
    state = {}

    def stage2(after):
        _, got = _split_wait(f"gather_{tag}_1_wait", s1, r1, [], lands, plan1, slots1, after)
        state["s"], state["r"], _, state["lands"], token2 = _split_start(f"gather_{tag}_2_start", [], got, plan2, None,
                                                                         _ONWARD_AND_SIBLING)
        return token2

    def stage3(after):
        _, got = _split_wait(f"gather_{tag}_2_wait", state["s"], state["r"], [], state["lands"], plan2, slots2, after)
        s3, r3, _, got, token3 = _split_start(f"gather_{tag}_3_start", [], got, plan3, None, _SIBLING)
        return _split_wait(f"gather_{tag}_3_wait", s3, r3, [], got, plan3, slots3, token3)[1]

    return stage2, stage3, token


_CHIP_LIST = ((0, 0), (0, 1), (1, 0), (1, 1))


def _reduce_plan_d2d(n):
    def plan(x, y, c):
        return [(a, _lin(qx, qy, 1 - c), a, q, (x, y, 1 - c)) for a in range(n) for q, (qx, qy) in enumerate(_CHIP_LIST)]

    def recv_slots(x, y, c):
        return [q for _ in range(n) for q in range(4)]

    return plan, recv_slots


def _reduce_plan_ici(n):
    def plan(x, y, c):
        return [(a, 2 * cx + cy, a, j, (cx, cy, c)) for a in range(n) for j, (cx, cy) in enumerate(_chips(x, y))]

    def recv_slots(x, y, c):
        return [j for _ in range(n) for j in range(3)]

    return plan, recv_slots


def _scatter_plan(n):
    def plan(x, y, c):
        return [(a, _lin(*peer), a, r, peer) for a in range(n) for r, peer in enumerate(_peers(x, y, c))]

    def recv_slots(x, y, c):
        return [r for _ in range(n) for r in range(N_DEV - 1)]

    return plan, recv_slots


def _broadcast_plan():
    def plan(x, y, c):
        return [(0, None, 0, _lin(x, y, c), peer) for peer in _peers(x, y, c)]

    def recv_slots(x, y, c):
        return [_lin(*peer) for peer in _peers(x, y, c)]

    return plan, recv_slots


def _chip_partial(grads, recvd, core, name):
    n = len(grads)

    def body(c_ref, *refs):
        for a in range(n):
            g_ref, r_ref, o_ref = refs[a], refs[n + a], refs[2 * n + a]
            o_ref[...] = (g_ref[...].astype(F32) + r_ref[...].astype(F32)).astype(o_ref.dtype)

    def blk(a, own):
        zeros = (0,) * (a.ndim - 1)
        return pl.BlockSpec((None,) + a.shape[1:],
                            (lambda q, c_ref: (2 * q + c_ref[0],) + zeros) if own else (lambda q, c_ref: (q,) + zeros))

    return pl.pallas_call(
        body, name=name,
        grid_spec=pltpu.PrefetchScalarGridSpec(
            num_scalar_prefetch=1, grid=(4,),
            in_specs=[blk(a, True) for a in grads] + [blk(a, False) for a in recvd],
            out_specs=[blk(a, False) for a in recvd]),
        out_shape=[jax.ShapeDtypeStruct(a.shape, a.dtype) for a in recvd],
        compiler_params=_params(("arbitrary",)),
    )(core, *grads, *recvd)


def _reduce_scatter_split(grads, tag, core, behind):
    n = len(grads)
    plan1, slots1 = _reduce_plan_d2d(n)
    lands1 = [lax.empty((4,) + a.shape[1:], a.dtype) for a in grads]
    s1, r1, srcs1, lands1, token1 = _split_start(f"reduce_{tag}_d2d_start", grads, lands1, plan1, None, _SIBLING)
    own, got = _split_wait(f"reduce_{tag}_d2d_wait", s1, r1, srcs1, lands1, plan1, slots1, behind(token1))
    parts = _chip_partial(own, got, core, f"reduce_{tag}_partial")
    plan2, slots2 = _reduce_plan_ici(n)
    lands2 = [lax.empty((3,) + a.shape[1:], a.dtype) for a in grads]
    s2, r2, srcs2, lands2, token2 = _split_start(f"reduce_{tag}_ici_start", parts, lands2, plan2, None, _CHIPS)

    def finish(after):
        return _split_wait(f"reduce_{tag}_ici_wait", s2, r2, srcs2, lands2, plan2, slots2, after)

    return finish, token2


def _place_shards(me, shards, dtypes, name, after=None):
    n = len(shards)

    def body(me_ref, *refs):
        for a in range(n):
            refs[n + a][...] = refs[a][...].astype(dtypes[a])

    full = lambda s: pl.BlockSpec(s.shape, lambda i, me_ref: (0,) * s.ndim)
    slot = lambda s: pl.BlockSpec((None,) + s.shape, lambda i, me_ref: (me_ref[0],) + (0,) * s.ndim)
    body, more_specs, more = _ordered_behind(body, 1 + n, after)
    return pl.pallas_call(
        body, name=name,
        grid_spec=pltpu.PrefetchScalarGridSpec(num_scalar_prefetch=1, grid=(1,),
                                               in_specs=[full(s) for s in shards] + more_specs,
                                               out_specs=[slot(s) for s in shards]),
        out_shape=[jax.ShapeDtypeStruct((N_DEV,) + s.shape, d) for s, d in zip(shards, dtypes)],
        compiler_params=_params(("arbitrary",)),
    )(me, *shards, *more)


def _mix_in_fwd(x, g1, w_in_t, conv_w, gq, gk, gconv):
    tm = 512
    n_t = SEQ // tm

    def body(x_ref, g1_ref, w_ref, cw_ref, gq_ref, gk_ref, gc_ref,
             proj_ref, u1_ref, ycn_ref, qn_ref, kn_ref, v_ref, halo_ref):
        @pl.when(pl.program_id(0) == 0)
        def _():
            halo_ref[...] = jnp.zeros_like(halo_ref)

        xv = x_ref[...]
        u = (xv * _rstd(xv) * g1_ref[...]).astype(BF16)
        u1_ref[...] = u
        proj = _dot_nt(u, w_ref[...])
        proj_ref[...] = proj
        gate_b = proj[:, 0:CONV_WIDTH]
        a = proj[:, CONV_WIDTH:2 * CONV_WIDTH] * proj[:, 2 * CONV_WIDTH:3 * CONV_WIDTH]
        cv, _, _ = _conv3(a, _taps(cw_ref[...]), halo_ref[...])
        halo_ref[...] = a[tm - SUBLANES:]
        yc = gate_b * cv
        ycn_ref[...] = (yc * _rstd(yc) * gc_ref[...]).astype(BF16)
        q0 = 3 * CONV_WIDTH
        qn_ref[...] = _head_norm(proj[:, q0:q0 + ATTN_WIDTH], gq_ref[...], N_HEADS).astype(BF16)
        k0 = q0 + ATTN_WIDTH
        kn_ref[...] = _head_norm(proj[:, k0:k0 + KV_WIDTH], gk_ref[...], 2).astype(BF16)
        v_ref[...] = proj[:, k0 + KV_WIDTH:k0 + 2 * KV_WIDTH].astype(BF16)

    const = lambda shape: pl.BlockSpec(shape, lambda i: (0,) * len(shape))
    rows = lambda w: pl.BlockSpec((tm, w), lambda i: (i, 0))
    return pl.pallas_call(
        body, name="mix_in_fwd", grid=(n_t,),
        in_specs=[rows(D_MODEL), const((1, D_MODEL)), const((IN_WIDTH, D_MODEL)), const((3, CONV_WIDTH)),
                  const((1, HEAD_DIM)), const((1, HEAD_DIM)), const((1, CONV_WIDTH))],
        out_specs=[rows(IN_WIDTH), rows(D_MODEL), rows(CONV_WIDTH), rows(ATTN_WIDTH), rows(KV_WIDTH), rows(KV_WIDTH)],
        out_shape=[jax.ShapeDtypeStruct((SEQ, IN_WIDTH), F32), jax.ShapeDtypeStruct((SEQ, D_MODEL), BF16),
                   jax.ShapeDtypeStruct((SEQ, CONV_WIDTH), BF16),
                   jax.ShapeDtypeStruct((SEQ, ATTN_WIDTH), BF16), jax.ShapeDtypeStruct((SEQ, KV_WIDTH), BF16),
                   jax.ShapeDtypeStruct((SEQ, KV_WIDTH), BF16)],
        scratch_shapes=[pltpu.VMEM((SUBLANES, CONV_WIDTH), F32)],
        compiler_params=_params(("arbitrary",)),
    )(x, g1, w_in_t, conv_w, gq, gk, gconv)


GROUP_ROWS = GQA_GROUP * BLK
QUERY_BLOCKS_PER_STEP = 2


def _band_bias(tbl_ref, bkt, bias_ref):
    for h in range(N_HEADS):
        acc = jnp.zeros(bkt.shape, F32)
        for b in range(NUM_BUCKETS):
            acc = jnp.where(bkt == b, tbl_ref[h, b], acc)
        bias_ref[h // GQA_GROUP, BLK * (h % GQA_GROUP):BLK * (h % GQA_GROUP + 1), :] = acc


def _band_masks(i):
    qi = lax.broadcasted_iota(jnp.int32, (GROUP_ROWS, BLK), 0) & (BLK - 1)
    ji = lax.broadcasted_iota(jnp.int32, (GROUP_ROWS, BLK), 1)
    upper = ji > qi
    return upper, upper & (i == 0)


def _stack_heads(x, g):
    return jnp.concatenate([x[:, HEAD_DIM * h:HEAD_DIM * (h + 1)] for h in range(GQA_GROUP * g, GQA_GROUP * (g + 1))], axis=0)


def _unstack_heads(groups):
    return jnp.concatenate([p[BLK * t:BLK * (t + 1)] for p in groups for t in range(GQA_GROUP)], axis=-1)


def _per_head_rows(vals):
    row = lax.broadcasted_iota(jnp.int32, (GROUP_ROWS, 1), 0)
    col = jnp.full((GROUP_ROWS, 1), vals[GQA_GROUP - 1], F32)
    for t in range(GQA_GROUP - 2, -1, -1):
        col = jnp.where(row < BLK * (t + 1), vals[t], col)
    return col


def _band_rows(ref, i):
    prev = pl.multiple_of(jnp.maximum(i - 1, 0) * BLK, BLK)
    cur = pl.multiple_of(i * BLK, BLK)
    return jnp.concatenate([ref[pl.ds(prev, BLK), :], ref[pl.ds(cur, BLK), :]], axis=0), prev, cur


def _fold(band, upper):
    return jnp.where(upper, band[:, :BLK], band[:, BLK:])


def _unfold(tile, upper):
    return jnp.concatenate([jnp.where(upper, tile, 0.0), jnp.where(upper, 0.0, tile)], axis=1)


def _head_probs(qh, kh, bias, upper, dead, sink):
    logits = _fold(_dot_nt(qh, kh), upper) * (HEAD_DIM ** -0.5) + bias
    logits = jnp.where(dead, NEG_INF, logits)
    m = jnp.maximum(jnp.max(logits, axis=-1, keepdims=True), sink)
    p = jnp.exp(logits - m)
    es = jnp.exp(sink - m)
    den = jnp.sum(p, axis=-1, keepdims=True) + es
    return p / den, es / den


def _attn_fwd(qn, kn, v, tbl, sinks, bkt, gattn, after=None):
    n_b = SEQ // BLK

    def body(q_ref, k_ref, v_ref, tbl_ref, sink_ref, bkt_ref, ga_ref, y_ref, yn_ref, p_ref, ps_ref, bias_ref):
        step = pl.program_id(0)

        @pl.when(step == 0)
        def _():
            _band_bias(tbl_ref, bkt_ref[...], bias_ref)

        lane = lax.broadcasted_iota(jnp.int32, (BLK, 128), 1)
        for b in range(QUERY_BLOCKS_PER_STEP):
            i = QUERY_BLOCKS_PER_STEP * step + b
            rows = slice(BLK * b, BLK * (b + 1))
            kb, _, _ = _band_rows(k_ref, i)
            vb, _, _ = _band_rows(v_ref, i)
            upper, dead = _band_masks(i)
            q = q_ref[rows, :]
            outs = []
            psinks = jnp.zeros((BLK, 128), F32)
            for g in range(N_HEADS // GQA_GROUP):
                kv = slice(HEAD_DIM * g, HEAD_DIM * (g + 1))
                sink = _per_head_rows([sink_ref[0, GQA_GROUP * g + t] for t in range(GQA_GROUP)])
                probs, psink = _head_probs(_stack_heads(q, g), kb[:, kv], bias_ref[g], upper, dead, sink)
                p_ref[b, g] = probs.astype(BF16)
                for t in range(GQA_GROUP):
                    psinks = jnp.where(lane == GQA_GROUP * g + t, psink[BLK * t:BLK * (t + 1)], psinks)
                outs.append(_dot(_unfold(probs, upper).astype(BF16), vb[:, kv]))
            ps_ref[rows, :] = psinks
            y = _unstack_heads(outs)
            y_ref[rows, :] = y
            yn_ref[rows, :] = (y * _rstd(y) * ga_ref[...]).astype(BF16)

    const = lambda shape: pl.BlockSpec(shape, lambda i: (0,) * len(shape))
    rows = lambda w: pl.BlockSpec((QUERY_BLOCKS_PER_STEP * BLK, w), lambda i: (i, 0))
    smem = pl.BlockSpec(memory_space=pltpu.SMEM)
    body, more_specs, more = _ordered_behind(body, 7, after)
    return pl.pallas_call(
        body, name="attn_fwd", grid=(n_b // QUERY_BLOCKS_PER_STEP,),
        in_specs=[rows(ATTN_WIDTH), const((SEQ, KV_WIDTH)), const((SEQ, KV_WIDTH)), smem, smem,
                  const((BLK, BLK)), const((1, ATTN_WIDTH))] + more_specs,
        out_specs=[rows(ATTN_WIDTH), rows(ATTN_WIDTH),
                   pl.BlockSpec((QUERY_BLOCKS_PER_STEP, N_HEADS // GQA_GROUP, GROUP_ROWS, BLK), lambda i: (i, 0, 0, 0)),
                   rows(128)],
        out_shape=[jax.ShapeDtypeStruct((SEQ, ATTN_WIDTH), F32), jax.ShapeDtypeStruct((SEQ, ATTN_WIDTH), BF16),
                   jax.ShapeDtypeStruct((n_b, N_HEADS // GQA_GROUP, GROUP_ROWS, BLK), BF16),
                   jax.ShapeDtypeStruct((SEQ, 128), F32)],
        scratch_shapes=[pltpu.VMEM((N_HEADS // GQA_GROUP, GROUP_ROWS, BLK), F32)],
        compiler_params=_params(("arbitrary",)),
    )(qn, kn, v, tbl, sinks, bkt, gattn, *more)


def _ffn_block(i, step):
    return jnp.where(i % 2 == 0, step, N_FFN_BLK - 1 - step)


def _ffn_fwd(x, ycn, yan, w_out, g2, w_up, fcw, fcb, w_down, tgt):
    tm = 512
    n_t = SEQ // tm

    def body(x_ref, ycn_ref, yan_ref, wo_ref, g2_ref, wu_ref, cw_ref, b_ref, wd_ref, tgt_ref,
             h1_ref, u2_ref, up_ref, pre_ref, act_ref, dh2_ref, dh2b_ref, loss_ref, acc_ref, halo_ref):
        i, step = pl.program_id(0), pl.program_id(1)
        j = _ffn_block(i, step)

        @pl.when((i == 0) & (step == 0))
        def _():
            loss_ref[...] = jnp.zeros_like(loss_ref)

        @pl.when(step == 0)
        def _():
            h1 = x_ref[...] + _dot(ycn_ref[...], wo_ref[0:CONV_WIDTH, :]) + _dot(yan_ref[...], wo_ref[CONV_WIDTH:, :])
            h1_ref[...] = h1
            u2_ref[...] = (h1 * _rstd(h1) * g2_ref[...]).astype(BF16)
            acc_ref[...] = jnp.zeros_like(acc_ref)

        u2 = u2_ref[...]
        pre = []
        for s in range(2):
            up = _dot_nt(u2, wu_ref[s])
            up_ref[s] = up.astype(BF16)
            halo = jnp.where(i == 0, 0.0, halo_ref[s, j])
            pre.append(_conv3(up, _taps(cw_ref.at[s]), halo)[0] + b_ref[s])
            pre_ref[s] = pre[s].astype(BF16)
            halo_ref[s, j] = up[tm - SUBLANES:]
        g, val = pre
        act = (g * jax.nn.sigmoid(g) * val).astype(BF16)
        act_ref[...] = act
        acc_ref[...] += _dot(act, wd_ref[...])

        @pl.when(step == N_FFN_BLK - 1)
        def _():
            err = h1_ref[...] + acc_ref[...] - tgt_ref[...]
            loss_ref[...] += 0.5 * jnp.sum(err * err) / D_MODEL
            dh2 = err / D_MODEL
            dh2_ref[...] = dh2
            dh2b_ref[...] = dh2.astype(BF16)

    rows = lambda w: pl.BlockSpec((tm, w), lambda i, step: (i, 0))
    const = lambda shape: pl.BlockSpec(shape, lambda i, step: (0,) * len(shape))
    pair = lambda *s: pl.BlockSpec((2, None) + s, lambda i, step: (0, _ffn_block(i, step)) + (0,) * len(s))
    upb = pl.BlockSpec((2, None, tm, FFN_BLK), lambda i, step: (0, _ffn_block(i, step), i, 0))
    return pl.pallas_call(
        body, name="ffn_fwd", grid=(n_t, N_FFN_BLK),
        in_specs=[rows(D_MODEL), rows(CONV_WIDTH), rows(ATTN_WIDTH), const((D_MODEL, D_MODEL)), const((1, D_MODEL)),
                  pair(FFN_BLK, D_MODEL), pair(3, 1, FFN_BLK), pair(1, FFN_BLK),
                  pl.BlockSpec((None, FFN_BLK, D_MODEL), lambda i, step: (_ffn_block(i, step), 0, 0)), rows(D_MODEL)],
        out_specs=[rows(D_MODEL), rows(D_MODEL), upb, upb,
                   pl.BlockSpec((None, tm, FFN_BLK), lambda i, step: (_ffn_block(i, step), i, 0)),
                   rows(D_MODEL), rows(D_MODEL), const((SUBLANES, 128))],
        out_shape=[jax.ShapeDtypeStruct((SEQ, D_MODEL), F32), jax.ShapeDtypeStruct((SEQ, D_MODEL), BF16),
                   jax.ShapeDtypeStruct((2, N_FFN_BLK, SEQ, FFN_BLK), BF16),
                   jax.ShapeDtypeStruct((2, N_FFN_BLK, SEQ, FFN_BLK), BF16),
                   jax.ShapeDtypeStruct((N_FFN_BLK, SEQ, FFN_BLK), BF16),
                   jax.ShapeDtypeStruct((SEQ, D_MODEL), F32), jax.ShapeDtypeStruct((SEQ, D_MODEL), BF16),
                   jax.ShapeDtypeStruct((SUBLANES, 128), F32)],
        scratch_shapes=[pltpu.VMEM((tm, D_MODEL), F32), pltpu.VMEM((2, N_FFN_BLK, SUBLANES, FFN_BLK), F32)],
        compiler_params=_params(("arbitrary", "arbitrary")),
    )(x, ycn, yan, w_out, g2, w_up, fcw, fcb, w_down, tgt)


def _ffn_bwd(dh2, dh2b, h1, g2, up, pre, w_up, fcw, w_down, after=None):
    tm = 512
    units = ((288, 224), (0, 288))
    n_t = SEQ // tm

    def body(dh2_ref, dh2b_ref, h1_ref, g2_ref, up_ref, pre_ref, wu_ref, cw_ref, wd_ref,
             dup_ref, dh1_ref, dh1b_ref, dfb_ref, dfcw_ref, dg2_ref, acc_ref, next_ref):
        j, i = pl.program_id(0), pl.program_id(1)
        tile = pl.ds(pl.multiple_of((n_t - 1 - i) * tm, tm), tm)

        @pl.when((j == 0) & (i == 0))
        def _():
            dfb_ref[...] = jnp.zeros_like(dfb_ref)
            dfcw_ref[...] = jnp.zeros_like(dfcw_ref)
            dg2_ref[...] = jnp.zeros_like(dg2_ref)

        @pl.when(j == 0)
        def _():
            acc_ref[tile, :] = jnp.zeros((tm, D_MODEL), F32)

        nxt = [jnp.where(i == 0, 0.0, next_ref[s]) for s in range(2)]
        sums = [[0.0] * 4 for _ in range(2)]
        for r0, rn in units:
            rows = slice(r0, r0 + rn)
            g, val = pre_ref[0, rows, :].astype(F32), pre_ref[1, rows, :].astype(F32)
            sg = jax.nn.sigmoid(g)
            silu = g * sg
            dact = _dot_nt(dh2b_ref[rows, :], wd_ref[...])
            dpre = (dact * val * (sg * (1.0 + g * (1.0 - sg))), dact * silu)
            dups = []
            for s in range(2):
                d = dpre[s]
                u = up_ref[s, rows, :].astype(F32)
                w = _taps(cw_ref.at[s])
                d1 = _shift_up(d, 1, nxt[s])
                d2 = _shift_up(d, 2, nxt[s])
                nxt[s] = d[:SUBLANES]
                for t, term in enumerate((d, d2 * u, d1 * u, d * u)):
                    sums[s][t] = sums[s][t] + jnp.sum(term, axis=0, keepdims=True)
                dups.append((d * w[2] + d1 * w[1] + d2 * w[0]).astype(BF16))
                dup_ref[s, rows, :] = dups[s]
            acc_rows = pl.ds(pl.multiple_of((n_t - 1 - i) * tm + r0, SUBLANES), rn)
            acc_ref[acc_rows, :] += _dot(dups[0], wu_ref[0]) + _dot(dups[1], wu_ref[1])
        for s in range(2):
            next_ref[s] = nxt[s]
            dfb_ref[s, j] += sums[s][0]
            for t in range(3):
                dfcw_ref[s, j, t] += sums[s][1 + t]

        @pl.when(j == N_FFN_BLK - 1)
        def _():
            dn, dgain = _rms_bwd(h1_ref[...], g2_ref[...], acc_ref[tile, :])
            dh1 = dh2_ref[...] + dn
            dh1_ref[...] = dh1
            dh1b_ref[...] = dh1.astype(BF16)
            dg2_ref[...] += dgain

    rev = lambda i: n_t - 1 - i
    rows = lambda w: pl.BlockSpec((tm, w), lambda j, i: (rev(i), 0))
    last_rows = lambda w: pl.BlockSpec((tm, w), lambda j, i: (jnp.where(j == N_FFN_BLK - 1, rev(i), rev(0)), 0))
    const = lambda shape: pl.BlockSpec(shape, lambda j, i: (0,) * len(shape))
    pair = lambda *s: pl.BlockSpec((2, None) + s, lambda j, i: (0, j) + (0,) * len(s))
    upb = pl.BlockSpec((2, None, tm, FFN_BLK), lambda j, i: (0, j, rev(i), 0))
    body, more_specs, more = _ordered_behind(body, 9, after)
    return pl.pallas_call(
        body, name="ffn_bwd", grid=(N_FFN_BLK, n_t),
        in_specs=[last_rows(D_MODEL), rows(D_MODEL), last_rows(D_MODEL), const((1, D_MODEL)), upb, upb,
                  pair(FFN_BLK, D_MODEL), pair(3, 1, FFN_BLK),
                  pl.BlockSpec((None, FFN_BLK, D_MODEL), lambda j, i: (j, 0, 0))] + more_specs,
        out_specs=[upb, last_rows(D_MODEL), last_rows(D_MODEL),
                   const((2, N_FFN_BLK, 1, FFN_BLK)), const((2, N_FFN_BLK, 3, 1, FFN_BLK)), const((1, D_MODEL))],
        out_shape=[jax.ShapeDtypeStruct((2, N_FFN_BLK, SEQ, FFN_BLK), BF16), jax.ShapeDtypeStruct((SEQ, D_MODEL), F32),
                   jax.ShapeDtypeStruct((SEQ, D_MODEL), BF16), jax.ShapeDtypeStruct((2, N_FFN_BLK, 1, FFN_BLK), F32),
                   jax.ShapeDtypeStruct((2, N_FFN_BLK, 3, 1, FFN_BLK), F32), jax.ShapeDtypeStruct((1, D_MODEL), F32)],
        scratch_shapes=[pltpu.VMEM((SEQ, D_MODEL), F32), pltpu.VMEM((2, SUBLANES, FFN_BLK), F32)],
        compiler_params=_params(("arbitrary", "arbitrary")),
    )(dh2, dh2b, h1, g2, up, pre, w_up, fcw, w_down, *more)


def _grad_tn(a_list, b, out_rows, name, after=None):
    n = len(a_list)
    ncol = b.shape[1]

    def body(*refs):
        a_refs, b_ref, o_ref = refs[:n], refs[n], refs[n + 1]
        j = pl.program_id(0)
        for k in range(n):
            @pl.when(j == k)
            def _(k=k):
                o_ref[...] = _dot_tn(a_refs[k][...], b_ref[...]).astype(BF16)

    full = lambda shape: pl.BlockSpec(shape, lambda j: (0,) * len(shape))
    body, more_specs, more = _ordered_behind(body, n + 1, after)
    return pl.pallas_call(
        body, name=name, grid=(n,),
        in_specs=[full((SEQ, out_rows))] * n + [full((SEQ, ncol))] + more_specs,
        out_specs=pl.BlockSpec((None, out_rows, ncol), lambda j: (j, 0, 0)),
        out_shape=jax.ShapeDtypeStruct((n, out_rows, ncol), BF16),
        compiler_params=_params(("arbitrary",)),
    )(*a_list, b, *more)


def _grad_tn_blocked(a, b, name, per_step=2):
    nb, _, a_w = a.shape
    b_w = b.shape[-1]

    def body(a_ref, b_ref, o_ref):
        for p in range(per_step):
            o_ref[p] = _dot_tn(a_ref[p], b_ref[...]).astype(BF16)

    return pl.pallas_call(
        body, name=name, grid=(nb // per_step,),
        in_specs=[pl.BlockSpec((per_step, SEQ, a_w), lambda k: (k, 0, 0)), pl.BlockSpec((SEQ, b_w), lambda k: (0, 0))],
        out_specs=pl.BlockSpec((per_step, a_w, b_w), lambda k: (k, 0, 0)),
        out_shape=jax.ShapeDtypeStruct((nb, a_w, b_w), BF16),
        compiler_params=_params(("arbitrary",)),
    )(a, b)


def _out_bwd(dh1b, w_out, y_attn, gattn, after=None):
    tm = 1024
    n_t = SEQ // tm

    def body(dh_ref, wo_ref, y_ref, ga_ref, dycn_ref, dy_ref, dga_ref):
        @pl.when(pl.program_id(0) == 0)
        def _():
            dga_ref[...] = jnp.zeros_like(dga_ref)

        dycat = _dot_nt(dh_ref[...], wo_ref[...])
        dycn_ref[...] = dycat[:, :CONV_WIDTH]
        dy, dga = _rms_bwd(y_ref[...], ga_ref[...], dycat[:, CONV_WIDTH:])
        dy_ref[...] = dy
        dga_ref[...] += dga

    rows = lambda w: pl.BlockSpec((tm, w), lambda i: (i, 0))
    const = lambda shape: pl.BlockSpec(shape, lambda i: (0,) * len(shape))
    body, more_specs, more = _ordered_behind(body, 4, after)
    return pl.pallas_call(
        body, name="out_bwd", grid=(n_t,),
        in_specs=[rows(D_MODEL), const((D_MODEL, D_MODEL)), rows(ATTN_WIDTH), const((1, ATTN_WIDTH))] + more_specs,
        out_specs=[rows(CONV_WIDTH), rows(ATTN_WIDTH), const((1, ATTN_WIDTH))],
        out_shape=[jax.ShapeDtypeStruct((SEQ, CONV_WIDTH), F32), jax.ShapeDtypeStruct((SEQ, ATTN_WIDTH), F32),
                   jax.ShapeDtypeStruct((1, ATTN_WIDTH), F32)],
        compiler_params=_params(("arbitrary",)),
    )(dh1b, w_out, y_attn, gattn, *more)


def _attn_bwd(qn, kn, v, dy, probs, psinks, bkt, after=None):
    n_b = SEQ // BLK

    def body(q_ref, k_ref, v_ref, dy_ref, p_ref, ps_ref, bkt_ref,
             dq_ref, dk_ref, dv_ref, dtbl_ref, dsink_ref, dbias_ref, dsacc_ref):
        step = pl.program_id(0)

        @pl.when(step == 0)
        def _():
            dbias_ref[...] = jnp.zeros_like(dbias_ref)
            dsacc_ref[...] = jnp.zeros_like(dsacc_ref)
            dk_ref[...] = jnp.zeros_like(dk_ref)
            dv_ref[...] = jnp.zeros_like(dv_ref)

        lane = lax.broadcasted_iota(jnp.int32, (BLK, 128), 1)
        for blk in range(QUERY_BLOCKS_PER_STEP):
            i = QUERY_BLOCKS_PER_STEP * step + blk
            rows = slice(BLK * blk, BLK * (blk + 1))
            kb, prev, cur = _band_rows(k_ref, i)
            vb, _, _ = _band_rows(v_ref, i)
            upper, _ = _band_masks(i)
            q = q_ref[rows, :]
            dy = dy_ref[rows, :]
            psink = ps_ref[rows, :]
            dsink = jnp.zeros((BLK, 128), F32)
            dqs, dks, dvs = [], [], []
            for g in range(N_HEADS // GQA_GROUP):
                kv = slice(HEAD_DIM * g, HEAD_DIM * (g + 1))
                qg = _stack_heads(q, g)
                dog = _stack_heads(dy, g).astype(BF16)
                pb = p_ref[blk, g]
                pg = pb.astype(F32)
                dprobs = _fold(_dot_nt(dog, vb[:, kv]), upper)
                dvs.append(_dot_tn(_unfold(pb, upper), dog))
                dsum = jnp.sum(pg * dprobs, axis=-1, keepdims=True)
                dlogits = pg * (dprobs - dsum)
                for t in range(GQA_GROUP):
                    dsink = jnp.where(lane == GQA_GROUP * g + t, -psink * dsum[BLK * t:BLK * (t + 1)], dsink)
                dbias_ref[g] += dlogits
                ds = _unfold(dlogits * (HEAD_DIM ** -0.5), upper).astype(BF16)
                dqs.append(_dot(ds, kb[:, kv]))
                dks.append(_dot_tn(ds, qg))
            dsacc_ref[...] += dsink
            dq_ref[rows, :] = _unstack_heads(dqs)
            dkb = jnp.concatenate(dks, axis=-1)
            dvb = jnp.concatenate(dvs, axis=-1)
            dk_ref[pl.ds(prev, BLK), :] += dkb[:BLK]
            dk_ref[pl.ds(cur, BLK), :] += dkb[BLK:]
            dv_ref[pl.ds(prev, BLK), :] += dvb[:BLK]
            dv_ref[pl.ds(cur, BLK), :] += dvb[BLK:]

        @pl.when(step == n_b // QUERY_BLOCKS_PER_STEP - 1)
        def _():
            bkt = bkt_ref[...]
            row8 = lax.broadcasted_iota(jnp.int32, (N_HEADS, 128), 0)
            lane8 = lax.broadcasted_iota(jnp.int32, (N_HEADS, 128), 1)
            acc = jnp.zeros((N_HEADS, 128), F32)
            for h in range(N_HEADS):
                rows = slice(BLK * (h % GQA_GROUP), BLK * (h % GQA_GROUP + 1))
                dbh = dbias_ref[h // GQA_GROUP, rows, :]
                for b in range(NUM_BUCKETS):
                    acc = jnp.where((row8 == h) & (lane8 == b), jnp.sum(jnp.where(bkt == b, dbh, 0.0)), acc)
            dsink_ref[...] = jnp.sum(dsacc_ref[...], axis=0, keepdims=True)
            dtbl_ref[...] = acc

    const = lambda shape: pl.BlockSpec(shape, lambda i: (0,) * len(shape))
    rows = lambda w: pl.BlockSpec((QUERY_BLOCKS_PER_STEP * BLK, w), lambda i: (i, 0))
    n_g = N_HEADS // GQA_GROUP
    body, more_specs, more = _ordered_behind(body, 7, after)
    return pl.pallas_call(
        body, name="attn_bwd", grid=(n_b // QUERY_BLOCKS_PER_STEP,),
        in_specs=[rows(ATTN_WIDTH), const((SEQ, KV_WIDTH)), const((SEQ, KV_WIDTH)), rows(ATTN_WIDTH),
                  pl.BlockSpec((QUERY_BLOCKS_PER_STEP, n_g, GROUP_ROWS, BLK), lambda i: (i, 0, 0, 0)), rows(128),
                  const((BLK, BLK))] + more_specs,
        out_specs=[rows(ATTN_WIDTH), const((SEQ, KV_WIDTH)), const((SEQ, KV_WIDTH)), const((N_HEADS, 128)), const((1, 128))],
        out_shape=[jax.ShapeDtypeStruct((SEQ, ATTN_WIDTH), F32), jax.ShapeDtypeStruct((SEQ, KV_WIDTH), F32),
                   jax.ShapeDtypeStruct((SEQ, KV_WIDTH), F32), jax.ShapeDtypeStruct((N_HEADS, 128), F32),
                   jax.ShapeDtypeStruct((1, 128), F32)],
        scratch_shapes=[pltpu.VMEM((n_g, GROUP_ROWS, BLK), F32), pltpu.VMEM((BLK, 128), F32)],
        compiler_params=_params(("arbitrary",)),
    )(qn, kn, v, dy, probs, psinks, bkt, *more)


def _mix_in_bwd(x, dh1, proj, dycn, dqn, dkn, dv, w_in_t, conv_w, g1, gq, gk, gconv):
    tm = 512
    n_t = SEQ // tm
    halo_blocks = tm // SUBLANES

    def body(x_ref, dh1_ref, proj_ref, halo_ref, dycn_ref, dqn_ref, dkn_ref, dv_ref, w_ref, cw_ref,
             g1_ref, gq_ref, gk_ref, gc_ref,
             dx_ref, dproj_ref, dcw_ref, dgc_ref, dgq_ref, dgk_ref, dg1_ref, next_ref):
        i = pl.program_id(0)
        first_tile = i == n_t - 1

        @pl.when(i == 0)
        def _():
            for r in (dcw_ref, dgc_ref, dgq_ref, dgk_ref, dg1_ref, next_ref):
                r[...] = jnp.zeros_like(r)

        proj = proj_ref[...]
        hp = halo_ref[...]
        gate_b = proj[:, 0:CONV_WIDTH]
        gate_c = proj[:, CONV_WIDTH:2 * CONV_WIDTH]
        hc = proj[:, 2 * CONV_WIDTH:3 * CONV_WIDTH]
        a = gate_c * hc
        a_halo = jnp.where(first_tile, 0.0, hp[:, CONV_WIDTH:2 * CONV_WIDTH] * hp[:, 2 * CONV_WIDTH:3 * CONV_WIDTH])
        cw = _taps(cw_ref[...])
        cv, a2, a1 = _conv3(a, cw, a_halo)
        dyc, dgc = _rms_bwd(gate_b * cv, gc_ref[...], dycn_ref[...])
        dgc_ref[...] += dgc
        dcv = dyc * gate_b
        dcw_ref[...] += jnp.concatenate(
            [jnp.sum(dcv * a2, axis=0, keepdims=True), jnp.sum(dcv * a1, axis=0, keepdims=True),
             jnp.sum(dcv * a, axis=0, keepdims=True)], axis=0)
        da = _conv3_bwd_input(dcv, cw, next_ref[...])
        next_ref[...] = dcv[:SUBLANES]
        q0 = 3 * CONV_WIDTH
        k0 = q0 + ATTN_WIDTH
        dq, dgq = _head_norm_bwd(proj[:, q0:k0], gq_ref[...], dqn_ref[...], N_HEADS)
        dk, dgk = _head_norm_bwd(proj[:, k0:k0 + KV_WIDTH], gk_ref[...], dkn_ref[...], 2)
        dgq_ref[...] += dgq
        dgk_ref[...] += dgk
        dproj = jnp.concatenate([dyc * cv, da * hc, da * gate_c, dq, dk, dv_ref[...]], axis=-1).astype(BF16)
        dproj_ref[...] = dproj
        du1 = _dot(dproj, w_ref[...])
        xv = x_ref[...]
        dn, dg1 = _rms_bwd(xv, g1_ref[...], du1)
        dx_ref[...] = dh1_ref[...] + dn
        dg1_ref[...] += dg1

    rev = lambda i: n_t - 1 - i
    rows = lambda w: pl.BlockSpec((tm, w), lambda i: (rev(i), 0))
    const = lambda shape: pl.BlockSpec(shape, lambda i: (0,) * len(shape))
    halo = pl.BlockSpec((SUBLANES, IN_WIDTH), lambda i: (jnp.maximum(rev(i) * halo_blocks - 1, 0), 0))
    return pl.pallas_call(
        body, name="mix_in_bwd", grid=(n_t,),
        in_specs=[rows(D_MODEL), rows(D_MODEL), rows(IN_WIDTH), halo, rows(CONV_WIDTH), rows(ATTN_WIDTH), rows(KV_WIDTH),
                  rows(KV_WIDTH), const((IN_WIDTH, D_MODEL)), const((3, CONV_WIDTH)), const((1, D_MODEL)),
                  const((1, HEAD_DIM)), const((1, HEAD_DIM)), const((1, CONV_WIDTH))],
        out_specs=[rows(D_MODEL), rows(IN_WIDTH), const((3, CONV_WIDTH)), const((1, CONV_WIDTH)),
                   const((1, HEAD_DIM)), const((1, HEAD_DIM)), const((1, D_MODEL))],
        out_shape=[jax.ShapeDtypeStruct((SEQ, D_MODEL), F32), jax.ShapeDtypeStruct((SEQ, IN_WIDTH), BF16),
                   jax.ShapeDtypeStruct((3, CONV_WIDTH), F32),
                   jax.ShapeDtypeStruct((1, CONV_WIDTH), F32), jax.ShapeDtypeStruct((1, HEAD_DIM), F32),
                   jax.ShapeDtypeStruct((1, HEAD_DIM), F32), jax.ShapeDtypeStruct((1, D_MODEL), F32)],
        scratch_shapes=[pltpu.VMEM((SUBLANES, CONV_WIDTH), F32)],
        compiler_params=_params(("arbitrary",)),
    )(x, dh1, proj, proj, dycn, dqn, dkn, dv, w_in_t, conv_w, g1, gq, gk, gconv)


def _grad_w_in(dproj, u1, after=None):
    bw = 768

    def body(a_ref, b_ref, o_ref):
        o_ref[...] = _dot_tn(a_ref[...], b_ref[...]).astype(BF16)

    body, more_specs, more = _ordered_behind(body, 2, after)
    return pl.pallas_call(
        body, name="grad_w_in", grid=(IN_WIDTH // bw,),
        in_specs=[pl.BlockSpec((SEQ, bw), lambda k: (0, k)), pl.BlockSpec((SEQ, D_MODEL), lambda k: (0, 0))] + more_specs,
        out_specs=pl.BlockSpec((bw, D_MODEL), lambda k: (k, 0)),
        out_shape=jax.ShapeDtypeStruct((IN_WIDTH, D_MODEL), BF16),
        compiler_params=_params(("arbitrary",)),
    )(dproj, u1, *more)


def _adamw_math(w, g, m, v):
    m = ADAM_B1 * m + (1.0 - ADAM_B1) * g
    v = ADAM_B2 * v + (1.0 - ADAM_B2) * (g * g)
    m_hat = m / (1.0 - ADAM_B1 ** ADAM_STEP)
    v_hat = v / (1.0 - ADAM_B2 ** ADAM_STEP)
    return -ADAM_LR * (m_hat / (jnp.sqrt(v_hat) + ADAM_EPS) + ADAM_WD * w), m, v


_ROW_G1, _ROW_G2, _ROW_OUT_NORMS, _ROW_FFN_B, _ROW_GQ, _ROW_GK, _ROW_SINKS, _ROW_LOSS, _ROW_TABLE = 0, 1, 2, 3, 11, 12, 13, 14, 16
SMALL_ROWS, SMALL_COLS = 24, 1024
_SMALL_NAMES = ("norm_mix_g", "norm_ffn_g", "out_norm_conv_g", "out_norm_attn_g", "ffn_conv_b", "q_norm_g", "k_norm_g",
                "sinks", "rel_bias_table")


def _pack_small_grads(dg1, dg2, dgconv, dgattn, dfb, dgq, dgk, dsinks, dtbl_t, loss_acc, after=None):
    def body(dg1_ref, dg2_ref, dgc_ref, dga_ref, dfb_ref, dgq_ref, dgk_ref, ds_ref, dt_ref, loss_ref, o_ref, all_ref):
        o_ref[...] = jnp.zeros_like(o_ref)
        o_ref[_ROW_G1:_ROW_G1 + 1, :] = dg1_ref[...]
        o_ref[_ROW_G2:_ROW_G2 + 1, :] = dg2_ref[...]
        o_ref[_ROW_OUT_NORMS:_ROW_OUT_NORMS + 1, 0:CONV_WIDTH] = dgc_ref[...]
        o_ref[_ROW_OUT_NORMS:_ROW_OUT_NORMS + 1, CONV_WIDTH:] = dga_ref[...]
        for k in range(N_DEV):
            o_ref[_ROW_FFN_B + k:_ROW_FFN_B + k + 1, 0:FFN_BLK] = dfb_ref[k // N_FFN_BLK, k % N_FFN_BLK]
        o_ref[_ROW_GQ:_ROW_GQ + 1, 0:HEAD_DIM] = dgq_ref[...]
        o_ref[_ROW_GK:_ROW_GK + 1, 0:HEAD_DIM] = dgk_ref[...]
        o_ref[_ROW_SINKS:_ROW_SINKS + 1, 0:128] = ds_ref[...]
        o_ref[_ROW_LOSS:_ROW_LOSS + 1, 0:128] = loss_ref[0:1, :]
        o_ref[_ROW_TABLE:_ROW_TABLE + N_HEADS, 0:128] = dt_ref[...]
        for s in range(N_DEV):
            all_ref[s] = o_ref[...]

    body, more_specs, more = _ordered_behind(body, 10, after)
    return pl.pallas_call(
        body, name="pack_small_grads",
        in_specs=[pl.BlockSpec(memory_space=pltpu.VMEM)] * 10 + more_specs,
        out_shape=[jax.ShapeDtypeStruct((SMALL_ROWS, SMALL_COLS), F32),
                   jax.ShapeDtypeStruct((N_DEV, SMALL_ROWS, SMALL_COLS), F32)],
    )(dg1, dg2, dgconv, dgattn, dfb, dgq, dgk, dsinks, dtbl_t, loss_acc, *more)


def _adamw_small(recv, params, after):
    names = _SMALL_NAMES
    n = len(names)

    def grad_of(g, name, k=None):
        if name == "norm_mix_g":
            return g[_ROW_G1:_ROW_G1 + 1, :]
        if name == "norm_ffn_g":
            return g[_ROW_G2:_ROW_G2 + 1, :]
        if name == "out_norm_conv_g":
            return g[_ROW_OUT_NORMS:_ROW_OUT_NORMS + 1, 0:CONV_WIDTH]
        if name == "out_norm_attn_g":
            return g[_ROW_OUT_NORMS:_ROW_OUT_NORMS + 1, CONV_WIDTH:]
        if name == "ffn_conv_b":
            return g[_ROW_FFN_B + k:_ROW_FFN_B + k + 1, 0:FFN_BLK]
        if name == "q_norm_g":
            return g[_ROW_GQ:_ROW_GQ + 1, 0:HEAD_DIM]
        if name == "k_norm_g":
            return g[_ROW_GK:_ROW_GK + 1, 0:HEAD_DIM]
        if name == "sinks":
            return g[_ROW_SINKS:_ROW_SINKS + 1, 0:N_HEADS]
        return g[_ROW_TABLE:_ROW_TABLE + N_HEADS, 0:NUM_BUCKETS]

    def body(r_ref, *refs):
        ins, outs, loss_ref = refs[:3 * n], refs[3 * n:7 * n], refs[7 * n]
        g = r_ref[0]
        for s in range(1, N_DEV):
            g = g + r_ref[s]
        loss_ref[...] = g[_ROW_LOSS:_ROW_LOSS + 1, 0:128]
        for i, name in enumerate(names):
            w_ref, m_ref, v_ref = ins[3 * i:3 * i + 3]
            o = outs[4 * i:4 * i + 4]
            cols = [slice(FFN_BLK * k, FFN_BLK * (k + 1)) for k in range(N_DEV)] if name == "ffn_conv_b" else [slice(None)]
            for k, cs in enumerate(cols):
                gk = grad_of(g, name, k)
                d, m2, v2 = _adamw_math(w_ref[:, cs], gk, m_ref[:, cs], v_ref[:, cs])
                o[0][:, cs], o[1][:, cs], o[2][:, cs], o[3][:, cs] = gk, d, m2, v2

    flat = [a for name in names for a in params[name]]
    body, more_specs, more = _ordered_behind(body, 1 + 3 * n, after)
    vmem = pl.BlockSpec(memory_space=pltpu.VMEM)
    out = pl.pallas_call(
        body, name="adamw_small",
        in_specs=[vmem] * (1 + 3 * n) + more_specs,
        out_shape=[jax.ShapeDtypeStruct(params[name][0].shape, F32) for name in names for _ in range(4)]
        + [jax.ShapeDtypeStruct((1, 128), F32)],
        compiler_params=pltpu.CompilerParams(vmem_limit_bytes=VMEM_LIMIT),
    )(recv, *flat, *more)
    return {name: tuple(out[4 * i:4 * i + 4]) for i, name in enumerate(names)}, out[4 * n]


def _adamw_direct(w, m, v, own, recv, me, name, row_blocks=1, after=None):
    rb = w.shape[0] // row_blocks
    cols = w.shape[1]

    def body(me_ref, w_ref, m_ref, v_ref, o_ref, r_ref, g_o, d_o, m_o, v_o):
        g = o_ref[...].astype(F32)
        for s in range(N_DEV - 1):
            g = g + r_ref[s].astype(F32)
        g_o[...] = g
        d_o[...], m_o[...], v_o[...] = _adamw_math(w_ref[...], g, m_ref[...], v_ref[...])

    blk = pl.BlockSpec((rb, cols), lambda i, me_ref: (i, 0))
    oblk = pl.BlockSpec((None, rb, cols), lambda i, me_ref: (me_ref[0], i, 0))
    rblk = pl.BlockSpec((N_DEV - 1, rb, cols), lambda i, me_ref: (0, i, 0))
    body, more_specs, more = _ordered_behind(body, 6, after)
    return pl.pallas_call(
        body, name=name,
        grid_spec=pltpu.PrefetchScalarGridSpec(num_scalar_prefetch=1, grid=(row_blocks,),
                                               in_specs=[blk, blk, blk, oblk, rblk] + more_specs, out_specs=[blk] * 4),
        out_shape=[jax.ShapeDtypeStruct(w.shape, F32)] * 4,
        compiler_params=_params(("arbitrary",)),
    )(me, w, m, v, own, recv, *more)


def _adamw(w, m, v, part, recv, chip, name, row_blocks=1, after=None):
    rb = w.shape[0] // row_blocks
    tail = w.shape[1:]
    zeros = (0,) * len(tail)

    def body(chip_ref, w_ref, m_ref, v_ref, p_ref, r_ref, g_o, d_o, m_o, v_o):
        g = p_ref[...].astype(F32)
        for s in range(3):
            g = g + r_ref[s].astype(F32)
        g_o[...] = g
        d_o[...], m_o[...], v_o[...] = _adamw_math(w_ref[...], g, m_ref[...], v_ref[...])

    blk = pl.BlockSpec((rb,) + tail, lambda i, chip_ref: (i,) + zeros)
    pblk = pl.BlockSpec((None, rb) + tail, lambda i, chip_ref: (chip_ref[0], i) + zeros)
    rblk = pl.BlockSpec((3, rb) + tail, lambda i, chip_ref: (0, i) + zeros)
    body, more_specs, more = _ordered_behind(body, 6, after)
    return pl.pallas_call(
        body, name=name,
        grid_spec=pltpu.PrefetchScalarGridSpec(num_scalar_prefetch=1, grid=(row_blocks,),
                                               in_specs=[blk, blk, blk, pblk, rblk] + more_specs, out_specs=[blk] * 4),
        out_shape=[jax.ShapeDtypeStruct(w.shape, F32)] * 4,
        compiler_params=_params(("arbitrary",)),
    )(chip, w, m, v, part, recv, *more)


def kernel(x, norm_mix_g, w_in, conv_w, q_norm_g, k_norm_g, rel_bias_table, sinks, out_norm_conv_g, out_norm_attn_g, w_out, norm_ffn_g, w_up, ffn_conv_w, ffn_conv_b, w_down, loss_target, m_norm_mix_g, m_w_in, m_conv_w, m_q_norm_g, m_k_norm_g, m_rel_bias_table, m_sinks, m_out_norm_conv_g, m_out_norm_attn_g, m_w_out, m_norm_ffn_g, m_w_up, m_ffn_conv_w, m_ffn_conv_b, m_w_down, v_norm_mix_g, v_w_in, v_conv_w, v_q_norm_g, v_k_norm_g, v_rel_bias_table, v_sinks, v_out_norm_conv_g, v_out_norm_attn_g, v_w_out, v_norm_ffn_g, v_w_up, v_ffn_conv_w, v_ffn_conv_b, v_w_down):
    p = dict(norm_mix_g=norm_mix_g, w_in=w_in, conv_w=conv_w, q_norm_g=q_norm_g, k_norm_g=k_norm_g,
             rel_bias_table=rel_bias_table, sinks=sinks, out_norm_conv_g=out_norm_conv_g, out_norm_attn_g=out_norm_attn_g,
             w_out=w_out, norm_ffn_g=norm_ffn_g, w_up=w_up, ffn_conv_w=ffn_conv_w, ffn_conv_b=ffn_conv_b, w_down=w_down)
    m = dict(norm_mix_g=m_norm_mix_g, w_in=m_w_in, conv_w=m_conv_w, q_norm_g=m_q_norm_g, k_norm_g=m_k_norm_g,
             rel_bias_table=m_rel_bias_table, sinks=m_sinks, out_norm_conv_g=m_out_norm_conv_g,
             out_norm_attn_g=m_out_norm_attn_g, w_out=m_w_out, norm_ffn_g=m_norm_ffn_g, w_up=m_w_up,
             ffn_conv_w=m_ffn_conv_w, ffn_conv_b=m_ffn_conv_b, w_down=m_w_down)
    v = dict(norm_mix_g=v_norm_mix_g, w_in=v_w_in, conv_w=v_conv_w, q_norm_g=v_q_norm_g, k_norm_g=v_k_norm_g,
             rel_bias_table=v_rel_bias_table, sinks=v_sinks, out_norm_conv_g=v_out_norm_conv_g,
             out_norm_attn_g=v_out_norm_attn_g, w_out=v_w_out, norm_ffn_g=v_norm_ffn_g, w_up=v_w_up,
             ffn_conv_w=v_ffn_conv_w, ffn_conv_b=v_ffn_conv_b, w_down=v_w_down)

    xs, tgt = x[0], loss_target[0]
    g1, g2, gq, gk, gconv, gattn = norm_mix_g, norm_ffn_g, q_norm_g, k_norm_g, out_norm_conv_g, out_norm_attn_g
    ix, iy, ic = _coords()
    core = ic.astype(jnp.int32).reshape(1)
    chip = (2 * ix + iy).astype(jnp.int32).reshape(1)
    me = _lin(ix, iy, ic).astype(jnp.int32).reshape(1)
    bkt = jnp.asarray(_bucket_map())
    tr = lambda a: a[0].T
    taps = lambda a: jnp.transpose(a, (1, 0, 2))
    tbl_t = rel_bias_table.T

    wi_l, cw_l = _place_shards(me, [tr(w_in), taps(conv_w)], [BF16, F32], "place_mixer_shards")
    finish_a, token_a = _all_gather_split([wi_l, cw_l], "mixer", None)
    wo_l, wu_l, wd_l, fcw_l = _place_shards(me, [w_out[0], tr(w_up), w_down[0], taps(ffn_conv_w)],
                                            [BF16, BF16, BF16, F32], "place_ffn_shards", after=token_a)
    ffn_stage2, ffn_stage3, token_b = _all_gather_tree([wo_l, wu_l, wd_l, fcw_l], "ffn", token_a)
    wi_g, cw_g = finish_a(token_b)
    w_in_t = wi_g.reshape(IN_WIDTH, D_MODEL)
    conv_w_f = jnp.transpose(cw_g[:, :, 0, :], (1, 0, 2)).reshape(3, CONV_WIDTH)

    proj, u1, ycn, qn, kn, vv = _mix_in_fwd(xs, g1, w_in_t, conv_w_f, gq, gk, gconv)
    token_b2 = ffn_stage2(ycn)
    y_attn, yan, probs, psinks = _attn_fwd(qn, kn, vv, tbl_t, sinks, bkt, gattn, after=token_b2)
    wo_g, wu_g, wd_g, fcw_g = ffn_stage3(yan)
    w_out_f = wo_g.reshape(D_MODEL, D_MODEL)
    w_down_f = wd_g.reshape(N_FFN_BLK, FFN_BLK, D_MODEL)
    w_up_f = wu_g.reshape(2, N_FFN_BLK, FFN_BLK, D_MODEL)
    fcw_f = fcw_g.reshape(2, N_FFN_BLK, 3, 1, FFN_BLK)
    fcb = ffn_conv_b.reshape(2, N_FFN_BLK, 1, FFN_BLK)
    h1, u2, up, pre, act, dh2, dh2b, loss_acc = _ffn_fwd(xs, ycn, yan, w_out_f, g2, w_up_f, fcw_f, fcb, w_down_f, tgt)

    dw_down = _grad_tn_blocked(act, dh2b, "grad_w_down").reshape(N_DEV, D_FF // N_DEV, D_MODEL)
    plan_d, slots_d = _scatter_plan(1)
    d_sem = _split_start("scatter_w_down_start", [dw_down], [lax.empty((N_DEV - 1,) + dw_down.shape[1:], BF16)],
                         plan_d, None, _ALL_FOR_W_DOWN)
    dup, dh1, dh1b, dfb, dfcw, dg2 = _ffn_bwd(dh2, dh2b, h1, g2, up, pre, w_up_f, fcw_f, w_down_f, after=d_sem[4])
    dw_up = _grad_tn_blocked(dup.reshape(N_DEV, SEQ, FFN_BLK), u2, "grad_w_up")
    dw_out = _grad_tn([ycn, yan], dh1b, CONV_WIDTH, "grad_w_out").reshape(N_DEV, D_MODEL // N_DEV, D_MODEL)
    out_bwd = {}

    def behind_ffn(token):
        out_bwd["r"] = _out_bwd(dh1b, w_out_f, y_attn, gattn, after=token)
        return out_bwd["r"][0]

    finish_ffn, token_ffn = _reduce_scatter_split(
        [dw_up, dw_out, dfcw.reshape(N_DEV, 3, 1, FFN_BLK)], "ffn", core, behind_ffn)
    dycn, dy_attn, dgattn = out_bwd["r"]
    dqn, dkn, dv, dtbl_t, dsinks = _attn_bwd(qn, kn, vv, dy_attn, probs, psinks, bkt, after=token_ffn)
    dx, dproj, dcw, dgconv, dgq, dgk, dg1 = _mix_in_bwd(xs, dh1, proj, dycn, dqn, dkn, dv, w_in_t, conv_w_f,
                                                         g1, gq, gk, gconv)
    dw_in_t = _grad_w_in(dproj, u1).reshape(N_DEV, IN_WIDTH // N_DEV, D_MODEL)
    dcw_b = jnp.transpose(dcw.reshape(3, N_DEV, 1, CONV_WIDTH // N_DEV), (1, 0, 2, 3))
    plan_s, slots_s = _broadcast_plan()
    adam = {}
    ffn_got = {}
    small = {}

    def behind_mixer(token):
        packed, packed_all = _pack_small_grads(dg1, dg2, dgconv, dgattn, dfb, dgq, dgk, dsinks, dtbl_t, loss_acc,
                                               after=token)
        small["r"] = _split_start("gather_small_start", [packed], [packed_all], plan_s, None, _ALL_FOR_SMALL)
        ffn_got["r"] = finish_ffn(small["r"][4])
        return ffn_got["r"][1][0]

    finish_mixer, token_mixer = _reduce_scatter_split([dw_in_t, dcw_b], "mixer", core, behind_mixer)
    s_sem, r_sem, src_s, land_s, _ = small["r"]
    (p_wu, p_wo, p_fcw), (r_wu, r_wo, r_fcw) = ffn_got["r"]
    (own_wd,), (r_wd,) = _split_wait("scatter_w_down_wait", d_sem[0], d_sem[1], d_sem[2], d_sem[3], plan_d, slots_d,
                                     token_mixer)
    adam["w_down"] = _adamw_direct(w_down[0], m_w_down[0], v_w_down[0], own_wd, r_wd, me, "adamw_w_down", row_blocks=2)
    adam_up = _adamw(tr(w_up), tr(m_w_up), tr(v_w_up), p_wu, r_wu, chip, "adamw_w_up", row_blocks=4,
                     after=adam["w_down"][0])
    adam["w_out"] = _adamw(w_out[0], m_w_out[0], v_w_out[0], p_wo, r_wo, chip, "adamw_w_out", after=adam_up[0])
    adam_fcw = _adamw(taps(ffn_conv_w), taps(m_ffn_conv_w), taps(v_ffn_conv_w), p_fcw, r_fcw, chip, "adamw_ffn_conv_w",
                      after=adam["w_out"][0])
    _, (r_small,) = _split_wait("gather_small_wait", s_sem, r_sem, src_s, land_s, plan_s, slots_s, adam_fcw[0])
    small_in = {k: (p[k], m[k], v[k]) for k in _SMALL_NAMES}
    small_in["rel_bias_table"] = (tbl_t, m_rel_bias_table.T, v_rel_bias_table.T)
    small_out, loss_row = _adamw_small(r_small, small_in, None)
    (p_wi, p_cw), (r_wi, r_cw) = finish_mixer(loss_row)
    adam_in = _adamw(tr(w_in), tr(m_w_in), tr(v_w_in), p_wi, r_wi, chip, "adamw_w_in")
    adam_cw = _adamw(taps(conv_w), taps(m_conv_w), taps(v_conv_w), p_cw, r_cw, chip, "adamw_conv_w")

    res = {k: tuple(a[None] for a in t) for k, t in adam.items()}
    res["w_up"] = tuple(a.T[None] for a in adam_up)
    res["w_in"] = tuple(a.T[None] for a in adam_in)
    res["ffn_conv_w"] = tuple(taps(a) for a in adam_fcw)
    res["conv_w"] = tuple(taps(a) for a in adam_cw)
    res.update(small_out)
    res["rel_bias_table"] = tuple(a.T for a in small_out["rel_bias_table"])
    loss = loss_row[0, 0]
    order = ("norm_mix_g", "w_in", "conv_w", "q_norm_g", "k_norm_g", "rel_bias_table", "sinks", "out_norm_conv_g",
             "out_norm_attn_g", "w_out", "norm_ffn_g", "w_up", "ffn_conv_w", "ffn_conv_b", "w_down")
    return (loss, dx[None], *[res[k][0] for k in order], *[res[k][1] for k in order],
            *[res[k][2] for k in order], *[res[k][3] for k in order])
```

```python
import math

import numpy as np
import jax
import jax.numpy as jnp
from jax import lax
from jax.experimental import pallas as pl
from jax.experimental.pallas import tpu as pltpu

F32 = jnp.float32
BF16 = jnp.bfloat16

SEQ = 2048
D_MODEL = 1024
CONV_WIDTH = 512
ATTN_WIDTH = 512
KV_WIDTH = 128
HEAD_DIM = 64
N_HEADS = 8
GQA_GROUP = 4
IN_WIDTH = 2304
D_FF = 2816
BLK = 128
NUM_BUCKETS = 32
EPS = 1e-6
NEG_INF = -1e30
ADAM_LR = 0.001
ADAM_B1 = 0.9
ADAM_B2 = 0.999
ADAM_EPS = 1e-08
ADAM_WD = 0.01
ADAM_STEP = 10

N_DEV = 8
FFN_BLK = 2 * D_FF // N_DEV
N_FFN_BLK = D_FF // FFN_BLK
SUBLANES = 8
VMEM_LIMIT = 56 * 1024 * 1024

_MESH = pl.DeviceIdType.MESH
_ANY = pl.BlockSpec(memory_space=pl.ANY)


def _params(sem):
    return pltpu.CompilerParams(dimension_semantics=sem, vmem_limit_bytes=VMEM_LIMIT)


def _ordered_behind(body, pos, after):
    if after is None:
        return body, [], []
    return (lambda *refs: body(*refs[:pos], *refs[pos + 1:])), [_ANY], [after]


def _dot(a, b):
    return jnp.dot(a, b, preferred_element_type=F32)


def _dot_nt(a, b):
    return lax.dot_general(a, b, (((1,), (1,)), ((), ())), preferred_element_type=F32)


def _dot_tn(a, b):
    return lax.dot_general(a, b, (((0,), (0,)), ((), ())), preferred_element_type=F32)


def _shift_down(x, s, halo):
    r = pltpu.roll(x, s, axis=0)
    hr = pltpu.roll(halo, s, axis=0)
    row = lax.broadcasted_iota(jnp.int32, halo.shape, 0)
    top = jnp.where(row < s, hr, r[:SUBLANES])
    return jnp.concatenate([top, r[SUBLANES:]], axis=0)


def _shift_up(x, s, halo):
    n = x.shape[0]
    r = pltpu.roll(x, n - s, axis=0)
    hr = pltpu.roll(halo, SUBLANES - s, axis=0)
    row = lax.broadcasted_iota(jnp.int32, halo.shape, 0)
    bot = jnp.where(row >= SUBLANES - s, hr, r[n - SUBLANES:])
    return jnp.concatenate([r[:n - SUBLANES], bot], axis=0)


def _taps(w):
    return (w[0], w[1], w[2]) if len(w.shape) == 3 else (w[0:1], w[1:2], w[2:3])


def _conv3(x, w, halo):
    x2 = _shift_down(x, 2, halo)
    x1 = _shift_down(x, 1, halo)
    return x2 * w[0] + x1 * w[1] + x * w[2], x2, x1


def _conv3_bwd_input(dy, w, halo_next):
    return dy * w[2] + _shift_up(dy, 1, halo_next) * w[1] + _shift_up(dy, 2, halo_next) * w[0]


def _rstd(x):
    return lax.rsqrt(jnp.mean(x * x, axis=-1, keepdims=True) + EPS)


def _rms_bwd(x, g, dy):
    r = _rstd(x)
    n = x * r
    dn = dy * g
    dx = r * (dn - n * jnp.mean(dn * n, axis=-1, keepdims=True))
    return dx, jnp.sum(dy * n, axis=0, keepdims=True)


def _head_mean(x):
    width = x.shape[-1]
    ri = lax.broadcasted_iota(jnp.int32, (width, width), 0) // HEAD_DIM
    ci = lax.broadcasted_iota(jnp.int32, (width, width), 1) // HEAD_DIM
    ones = jnp.where(ri == ci, 1.0, 0.0).astype(BF16)
    hi = x.astype(BF16)
    lo = (x - hi.astype(F32)).astype(BF16)
    return (_dot(hi, ones) + _dot(lo, ones)) * (1.0 / HEAD_DIM)


def _head_norm(x, g, heads):
    return x * lax.rsqrt(_head_mean(x * x) + EPS) * jnp.tile(g, (1, heads))


def _head_norm_bwd(x, g, dy, heads):
    r = lax.rsqrt(_head_mean(x * x) + EPS)
    n = x * r
    dn = dy * jnp.tile(g, (1, heads))
    dx = r * (dn - n * _head_mean(dn * n))
    per_lane = jnp.sum(dy * n, axis=0, keepdims=True)
    dg = per_lane[:, 0:HEAD_DIM]
    for h in range(1, heads):
        dg = dg + per_lane[:, HEAD_DIM * h:HEAD_DIM * (h + 1)]
    return dx, dg


def _bucket_map():
    q = np.arange(BLK)[:, None]
    j = np.arange(BLK)[None, :]
    n = np.where(j > q, q + BLK - j, q - j)
    nf = np.maximum(n, 1).astype(np.float32)
    max_exact = NUM_BUCKETS // 2
    large = max_exact + (np.log(nf / max_exact) / math.log(BLK / max_exact) * (NUM_BUCKETS - max_exact)).astype(np.int32)
    large = np.minimum(large, NUM_BUCKETS - 1)
    return np.where(n < max_exact, n, large).astype(np.int32)


def _coords():
    return lax.axis_index("x"), lax.axis_index("y"), lax.axis_index("c")


def _lin(px, py, pc):
    return 4 * px + 2 * py + pc


def _chips(x, y):
    return [(1 - x, y), (x, 1 - y), (1 - x, 1 - y)]


def _peers(x, y, c):
    return [(1 - x if r & 4 else x, 1 - y if r & 2 else y, 1 - c if r & 1 else c) for r in range(1, N_DEV)]


_HBM = pl.BlockSpec(memory_space=pltpu.HBM)
_SEM = pl.BlockSpec(memory_space=pltpu.SEMAPHORE)
_EFFECT = pltpu.SideEffectType.DATAFLOW_SIDE_EFFECTING


def _in_hbm(a):
    return pltpu.with_memory_space_constraint(a, pltpu.HBM)


_SIBLING = (1, lambda x, y, c: [(x, y, 1 - c)])
_SIBLING_AND_CHIPS = (2, lambda x, y, c: [(x, y, 1 - c)] + [(cx, cy, c) for cx, cy in _chips(x, y)])
_SIBLING_AND_NEIGHBOURS = (3, lambda x, y, c: [(x, y, 1 - c), (1 - x, y, c), (x, 1 - y, c)])
_ONWARD_AND_SIBLING = (4, lambda x, y, c: [(jnp.where(c == 1, x, 1 - x), jnp.where(c == 1, 1 - y, y), c), (x, y, 1 - c)])
_ALL_FOR_W_DOWN = (5, lambda x, y, c: _peers(x, y, c))
_CHIPS = (6, lambda x, y, c: [(cx, cy, c) for cx, cy in _chips(x, y)])
_ALL_FOR_SMALL = (7, lambda x, y, c: _peers(x, y, c))


def _split_start(name, srcs, lands, plan, after, handshake):
    ns, nl = len(srcs), len(lands)
    n_copies = len(plan(0, 0, 0))
    n_after = 0 if after is None else 1
    collective_id, peers_of = handshake

    def body(*refs):
        src_refs, land_refs = refs[:ns + nl], refs[ns:ns + nl]
        send_sems, recv_sems = refs[ns + nl + n_after], refs[ns + nl + n_after + 1]
        token = refs[-1]
        barrier = pltpu.get_barrier_semaphore()
        peers = peers_of(*_coords())
        for peer in peers:
            pl.semaphore_signal(barrier, inc=1, device_id=peer, device_id_type=_MESH)
        pl.semaphore_wait(barrier, len(peers))
        for k, (a, s_slot, l, d_slot, dev) in enumerate(plan(*_coords())):
            src = src_refs[a] if s_slot is None else src_refs[a].at[s_slot]
            pltpu.make_async_remote_copy(src_ref=src, dst_ref=land_refs[l].at[d_slot], send_sem=send_sems.at[k],
                                         recv_sem=recv_sems.at[k], device_id=dev, device_id_type=_MESH).start()
        token[...] = jnp.zeros_like(token)

    arrs = list(srcs) + list(lands)
    out = pl.pallas_call(
        body, name=name,
        out_shape=(pltpu.SemaphoreType.DMA((n_copies,)), pltpu.SemaphoreType.DMA((n_copies,)),
                   *[pltpu.HBM(a.shape, a.dtype) for a in arrs], jax.ShapeDtypeStruct((SUBLANES, 128), F32)),
        in_specs=[_HBM] * (ns + nl) + [_ANY] * n_after,
        out_specs=(_SEM, _SEM, *[_HBM] * (ns + nl), pl.BlockSpec(memory_space=pltpu.VMEM)),
        input_output_aliases={i: 2 + i for i in range(ns + nl)},
        compiler_params=pltpu.CompilerParams(has_side_effects=_EFFECT, collective_id=collective_id),
    )(*[_in_hbm(a) for a in arrs], *([] if after is None else [after]))
    return out[0], out[1], list(out[2:2 + ns]), list(out[2 + ns:2 + ns + nl]), out[-1]


def _split_wait(name, send_sems, recv_sems, srcs, lands, plan, recv_slots, after):
    ns, nl = len(srcs), len(lands)

    def body(*refs):
        src_refs, land_refs = refs[:ns + nl], refs[ns:ns + nl]
        send_sems, recv_sems = refs[ns + nl], refs[ns + nl + 1]
        coords = _coords()
        slots = recv_slots(*coords)
        for k, (a, s_slot, l, _, dev) in enumerate(plan(*coords)):
            src = src_refs[a] if s_slot is None else src_refs[a].at[s_slot]
            cp = pltpu.make_async_remote_copy(src_ref=src, dst_ref=land_refs[l].at[slots[k]], send_sem=send_sems.at[k],
                                              recv_sem=recv_sems.at[k], device_id=dev, device_id_type=_MESH)
            cp.wait_send()
            cp.wait_recv()

    arrs = list(srcs) + list(lands)
    out = pl.pallas_call(
        body, name=name,
        out_shape=tuple(pltpu.HBM(a.shape, a.dtype) for a in arrs),
        in_specs=[_HBM] * (ns + nl) + [_SEM, _SEM, _ANY],
        out_specs=tuple([_HBM] * (ns + nl)),
        input_output_aliases={i: i for i in range(ns + nl)},
        compiler_params=pltpu.CompilerParams(has_side_effects=_EFFECT),
    )(*arrs, send_sems, recv_sems, after)
    return list(out[:ns]), list(out[ns:])


def _gather_plan_ici(n):
    def plan(x, y, c):
        me = _lin(x, y, c)
        out = []
        for a in range(n):
            out.append((a, me, a, me, (x, y, 1 - c)))
            out += [(a, me, a, me, (cx, cy, c)) for cx, cy in _chips(x, y)]
        return out

    def recv_slots(x, y, c):
        out = []
        for _ in range(n):
            out.append(_lin(x, y, 1 - c))
            out += [_lin(cx, cy, c) for cx, cy in _chips(x, y)]
        return out

    return plan, recv_slots


def _gather_plan_d2d(n):
    def plan(x, y, c):
        return [(a, _lin(cx, cy, c), a, _lin(cx, cy, c), (x, y, 1 - c)) for a in range(n) for cx, cy in _chips(x, y)]

    def recv_slots(x, y, c):
        return [_lin(cx, cy, 1 - c) for _ in range(n) for cx, cy in _chips(x, y)]

    return plan, recv_slots


def _all_gather_split(lands, tag, after):
    n = len(lands)
    plan1, slots1 = _gather_plan_ici(n)
    s1, r1, _, lands, token = _split_start(f"gather_{tag}_ici_start", [], lands, plan1, after, _SIBLING_AND_CHIPS)

    def finish(after):
        _, got = _split_wait(f"gather_{tag}_ici_wait", s1, r1, [], lands, plan1, slots1, after)
        plan2, slots2 = _gather_plan_d2d(n)
        s2, r2, _, got, token2 = _split_start(f"gather_{tag}_d2d_start", [], got, plan2, None, _SIBLING)
        return _split_wait(f"gather_{tag}_d2d_wait", s2, r2, [], got, plan2, slots2, token2)[1]

    return finish, token


def _all_gather_tree(lands, tag, after):
    n = len(lands)

    def plan1(x, y, c):
        me = _lin(x, y, c)
        return [(a, me, a, me, dev) for a in range(n) for dev in ((x, y, 1 - c), (1 - x, y, c), (x, 1 - y, c))]

    def slots1(x, y, c):
        return [s for _ in range(n) for s in (_lin(x, y, 1 - c), _lin(1 - x, y, c), _lin(x, 1 - y, c))]

    def plan2(x, y, c):
        from_x, from_y = _lin(1 - x, y, c), _lin(x, 1 - y, c)
        north = c == 1
        passed = jnp.where(north, from_x, from_y)
        onward = (jnp.where(north, x, 1 - x), jnp.where(north, 1 - y, y), c)
        sib = (x, y, 1 - c)
        return [cp for a in range(n) for cp in ((a, passed, a, passed, onward), (a, from_x, a, from_x, sib),
                                                (a, from_y, a, from_y, sib))]

    def slots2(x, y, c):
        return [s for _ in range(n) for s in (_lin(1 - x, 1 - y, c), _lin(1 - x, y, 1 - c), _lin(x, 1 - y, 1 - c))]

    def plan3(x, y, c):
        diag = _lin(1 - x, 1 - y, c)
        return [(a, diag, a, diag, (x, y, 1 - c)) for a in range(n)]

    def slots3(x, y, c):
        return [_lin(1 - x, 1 - y, 1 - c)] * n

    s1, r1, _, lands, token = _split_start(f"gather_{tag}_1_start", [], lands, plan1, after, _SIBLING_AND_NEIGHBOURS)
    state = {}

    def stage2(after):
        _, got = _split_wait(f"gather_{tag}_1_wait", s1, r1, [], lands, plan1, slots1, after)
        state["s"], state["r"], _, state["lands"], token2 = _split_start(f"gather_{tag}_2_start", [], got, plan2, None,
                                                                         _ONWARD_AND_SIBLING)
        return token2

    def stage3(after):
        _, got = _split_wait(f"gather_{tag}_2_wait", state["s"], state["r"], [], state["lands"], plan2, slots2, after)
        s3, r3, _, got, token3 = _split_start(f"gather_{tag}_3_start", [], got, plan3, None, _SIBLING)
        return _split_wait(f"gather_{tag}_3_wait", s3, r3, [], got, plan3, slots3, token3)[1]

    return stage2, stage3, token


_CHIP_LIST = ((0, 0), (0, 1), (1, 0), (1, 1))


def _reduce_plan_d2d(n):
    def plan(x, y, c):
        return [(a, _lin(qx, qy, 1 - c), a, q, (x, y, 1 - c)) for a in range(n) for q, (qx, qy) in enumerate(_CHIP_LIST)]

    def recv_slots(x, y, c):
        return [q for _ in range(n) for q in range(4)]

    return plan, recv_slots


def _reduce_plan_ici(n):
    def plan(x, y, c):
        return [(a, 2 * cx + cy, a, j, (cx, cy, c)) for a in range(n) for j, (cx, cy) in enumerate(_chips(x, y))]

    def recv_slots(x, y, c):
        return [j for _ in range(n) for j in range(3)]

    return plan, recv_slots


def _scatter_plan(n):
    def plan(x, y, c):
        return [(a, _lin(*peer), a, r, peer) for a in range(n) for r, peer in enumerate(_peers(x, y, c))]

    def recv_slots(x, y, c):
        return [r for _ in range(n) for r in range(N_DEV - 1)]

    return plan, recv_slots


def _broadcast_plan():
    def plan(x, y, c):
        return [(0, None, 0, _lin(x, y, c), peer) for peer in _peers(x, y, c)]

    def recv_slots(x, y, c):
        return [_lin(*peer) for peer in _peers(x, y, c)]

    return plan, recv_slots


def _chip_partial(grads, recvd, core, name):
    n = len(grads)

    def body(c_ref, *refs):
        for a in range(n):
            g_ref, r_ref, o_ref = refs[a], refs[n + a], refs[2 * n + a]
            o_ref[...] = (g_ref[...].astype(F32) + r_ref[...].astype(F32)).astype(o_ref.dtype)

    def blk(a, own):
        zeros = (0,) * (a.ndim - 1)
        return pl.BlockSpec((None,) + a.shape[1:],
                            (lambda q, c_ref: (2 * q + c_ref[0],) + zeros) if own else (lambda q, c_ref: (q,) + zeros))

    return pl.pallas_call(
        body, name=name,
        grid_spec=pltpu.PrefetchScalarGridSpec(
            num_scalar_prefetch=1, grid=(4,),
            in_specs=[blk(a, True) for a in grads] + [blk(a, False) for a in recvd],
            out_specs=[blk(a, False) for a in recvd]),
        out_shape=[jax.ShapeDtypeStruct(a.shape, a.dtype) for a in recvd],
        compiler_params=_params(("arbitrary",)),
    )(core, *grads, *recvd)


def _reduce_scatter_split(grads, tag, core, behind):
    n = len(grads)
    plan1, slots1 = _reduce_plan_d2d(n)
    lands1 = [lax.empty((4,) + a.shape[1:], a.dtype) for a in grads]
    s1, r1, srcs1, lands1, token1 = _split_start(f"reduce_{tag}_d2d_start", grads, lands1, plan1, None, _SIBLING)
    own, got = _split_wait(f"reduce_{tag}_d2d_wait", s1, r1, srcs1, lands1, plan1, slots1, behind(token1))
    parts = _chip_partial(own, got, core, f"reduce_{tag}_partial")
    plan2, slots2 = _reduce_plan_ici(n)
    lands2 = [lax.empty((3,) + a.shape[1:], a.dtype) for a in grads]
    s2, r2, srcs2, lands2, token2 = _split_start(f"reduce_{tag}_ici_start", parts, lands2, plan2, None, _CHIPS)

    def finish(after):
        return _split_wait(f"reduce_{tag}_ici_wait", s2, r2, srcs2, lands2, plan2, slots2, after)

    return finish, token2


def _place_shards(me, shards, dtypes, name, after=None):
    n = len(shards)

    def body(me_ref, *refs):
        for a in range(n):
            refs[n + a][...] = refs[a][...].astype(dtypes[a])

    full = lambda s: pl.BlockSpec(s.shape, lambda i, me_ref: (0,) * s.ndim)
    slot = lambda s: pl.BlockSpec((None,) + s.shape, lambda i, me_ref: (me_ref[0],) + (0,) * s.ndim)
    body, more_specs, more = _ordered_behind(body, 1 + n, after)
    return pl.pallas_call(
        body, name=name,
        grid_spec=pltpu.PrefetchScalarGridSpec(num_scalar_prefetch=1, grid=(1,),
                                               in_specs=[full(s) for s in shards] + more_specs,
                                               out_specs=[slot(s) for s in shards]),
        out_shape=[jax.ShapeDtypeStruct((N_DEV,) + s.shape, d) for s, d in zip(shards, dtypes)],
        compiler_params=_params(("arbitrary",)),
    )(me, *shards, *more)


def _mix_in_fwd(x, g1, w_in_t, conv_w, gq, gk, gconv):
    tm = 512
    n_t = SEQ // tm

    def body(x_ref, g1_ref, w_ref, cw_ref, gq_ref, gk_ref, gc_ref,
             proj_ref, u1_ref, ycn_ref, qn_ref, kn_ref, v_ref, halo_ref):
        @pl.when(pl.program_id(0) == 0)
        def _():
            halo_ref[...] = jnp.zeros_like(halo_ref)

        xv = x_ref[...]
        u = (xv * _rstd(xv) * g1_ref[...]).astype(BF16)
        u1_ref[...] = u
        proj = _dot_nt(u, w_ref[...])
        proj_ref[...] = proj
        gate_b = proj[:, 0:CONV_WIDTH]
        a = proj[:, CONV_WIDTH:2 * CONV_WIDTH] * proj[:, 2 * CONV_WIDTH:3 * CONV_WIDTH]
        cv, _, _ = _conv3(a, _taps(cw_ref[...]), halo_ref[...])
        halo_ref[...] = a[tm - SUBLANES:]
        yc = gate_b * cv
        ycn_ref[...] = (yc * _rstd(yc) * gc_ref[...]).astype(BF16)
        q0 = 3 * CONV_WIDTH
        qn_ref[...] = _head_norm(proj[:, q0:q0 + ATTN_WIDTH], gq_ref[...], N_HEADS).astype(BF16)
        k0 = q0 + ATTN_WIDTH
        kn_ref[...] = _head_norm(proj[:, k0:k0 + KV_WIDTH], gk_ref[...], 2).astype(BF16)
        v_ref[...] = proj[:, k0 + KV_WIDTH:k0 + 2 * KV_WIDTH].astype(BF16)

    const = lambda shape: pl.BlockSpec(shape, lambda i: (0,) * len(shape))
    rows = lambda w: pl.BlockSpec((tm, w), lambda i: (i, 0))
    return pl.pallas_call(
        body, name="mix_in_fwd", grid=(n_t,),
        in_specs=[rows(D_MODEL), const((1, D_MODEL)), const((IN_WIDTH, D_MODEL)), const((3, CONV_WIDTH)),
                  const((1, HEAD_DIM)), const((1, HEAD_DIM)), const((1, CONV_WIDTH))],
        out_specs=[rows(IN_WIDTH), rows(D_MODEL), rows(CONV_WIDTH), rows(ATTN_WIDTH), rows(KV_WIDTH), rows(KV_WIDTH)],
        out_shape=[jax.ShapeDtypeStruct((SEQ, IN_WIDTH), F32), jax.ShapeDtypeStruct((SEQ, D_MODEL), BF16),
                   jax.ShapeDtypeStruct((SEQ, CONV_WIDTH), BF16),
                   jax.ShapeDtypeStruct((SEQ, ATTN_WIDTH), BF16), jax.ShapeDtypeStruct((SEQ, KV_WIDTH), BF16),
                   jax.ShapeDtypeStruct((SEQ, KV_WIDTH), BF16)],
        scratch_shapes=[pltpu.VMEM((SUBLANES, CONV_WIDTH), F32)],
        compiler_params=_params(("arbitrary",)),
    )(x, g1, w_in_t, conv_w, gq, gk, gconv)


GROUP_ROWS = GQA_GROUP * BLK
WEIGHT_RING = 3
QUERY_BLOCKS_PER_STEP = 2


def _band_bias(tbl_ref, bkt, bias_ref):
    for h in range(N_HEADS):
        acc = jnp.zeros(bkt.shape, F32)
        for b in range(NUM_BUCKETS):
            acc = jnp.where(bkt == b, tbl_ref[h, b], acc)
        bias_ref[h // GQA_GROUP, BLK * (h % GQA_GROUP):BLK * (h % GQA_GROUP + 1), :] = acc


def _band_masks(i):
    qi = lax.broadcasted_iota(jnp.int32, (GROUP_ROWS, BLK), 0) & (BLK - 1)
    ji = lax.broadcasted_iota(jnp.int32, (GROUP_ROWS, BLK), 1)
    upper = ji > qi
    return upper, upper & (i == 0)


def _stack_heads(x, g):
    return jnp.concatenate([x[:, HEAD_DIM * h:HEAD_DIM * (h + 1)] for h in range(GQA_GROUP * g, GQA_GROUP * (g + 1))], axis=0)


def _unstack_heads(groups):
    return jnp.concatenate([p[BLK * t:BLK * (t + 1)] for p in groups for t in range(GQA_GROUP)], axis=-1)


def _per_head_rows(vals):
    row = lax.broadcasted_iota(jnp.int32, (GROUP_ROWS, 1), 0)
    col = jnp.full((GROUP_ROWS, 1), vals[GQA_GROUP - 1], F32)
    for t in range(GQA_GROUP - 2, -1, -1):
        col = jnp.where(row < BLK * (t + 1), vals[t], col)
    return col


def _band_rows(ref, i):
    prev = pl.multiple_of(jnp.maximum(i - 1, 0) * BLK, BLK)
    cur = pl.multiple_of(i * BLK, BLK)
    return jnp.concatenate([ref[pl.ds(prev, BLK), :], ref[pl.ds(cur, BLK), :]], axis=0), prev, cur


def _fold(band, upper):
    return jnp.where(upper, band[:, :BLK], band[:, BLK:])


def _unfold(tile, upper):
    return jnp.concatenate([jnp.where(upper, tile, 0.0), jnp.where(upper, 0.0, tile)], axis=1)


def _head_probs(qh, kh, bias, upper, dead, sink):
    logits = _fold(_dot_nt(qh, kh), upper) * (HEAD_DIM ** -0.5) + bias
    logits = jnp.where(dead, NEG_INF, logits)
    m = jnp.maximum(jnp.max(logits, axis=-1, keepdims=True), sink)
    p = jnp.exp(logits - m)
    es = jnp.exp(sink - m)
    den = jnp.sum(p, axis=-1, keepdims=True) + es
    return p / den, es / den


def _attn_fwd(qn, kn, v, tbl, sinks, bkt, gattn, after=None):
    n_b = SEQ // BLK

    def body(q_ref, k_ref, v_ref, tbl_ref, sink_ref, bkt_ref, ga_ref, y_ref, yn_ref, p_ref, ps_ref, bias_ref):
        step = pl.program_id(0)

        @pl.when(step == 0)
        def _():
            _band_bias(tbl_ref, bkt_ref[...], bias_ref)

        lane = lax.broadcasted_iota(jnp.int32, (BLK, 128), 1)
        for b in range(QUERY_BLOCKS_PER_STEP):
            i = QUERY_BLOCKS_PER_STEP * step + b
            rows = slice(BLK * b, BLK * (b + 1))
            kb, _, _ = _band_rows(k_ref, i)
            vb, _, _ = _band_rows(v_ref, i)
            upper, dead = _band_masks(i)
            q = q_ref[rows, :]
            outs = []
            psinks = jnp.zeros((BLK, 128), F32)
            for g in range(N_HEADS // GQA_GROUP):
                kv = slice(HEAD_DIM * g, HEAD_DIM * (g + 1))
                sink = _per_head_rows([sink_ref[0, GQA_GROUP * g + t] for t in range(GQA_GROUP)])
                probs, psink = _head_probs(_stack_heads(q, g), kb[:, kv], bias_ref[g], upper, dead, sink)
                p_ref[b, g] = probs.astype(BF16)
                for t in range(GQA_GROUP):
                    psinks = jnp.where(lane == GQA_GROUP * g + t, psink[BLK * t:BLK * (t + 1)], psinks)
                outs.append(_dot(_unfold(probs, upper).astype(BF16), vb[:, kv]))
            ps_ref[rows, :] = psinks
            y = _unstack_heads(outs)
            y_ref[rows, :] = y
            yn_ref[rows, :] = (y * _rstd(y) * ga_ref[...]).astype(BF16)

    const = lambda shape: pl.BlockSpec(shape, lambda i: (0,) * len(shape))
    rows = lambda w: pl.BlockSpec((QUERY_BLOCKS_PER_STEP * BLK, w), lambda i: (i, 0))
    smem = pl.BlockSpec(memory_space=pltpu.SMEM)
    body, more_specs, more = _ordered_behind(body, 7, after)
    return pl.pallas_call(
        body, name="attn_fwd", grid=(n_b // QUERY_BLOCKS_PER_STEP,),
        in_specs=[rows(ATTN_WIDTH), const((SEQ, KV_WIDTH)), const((SEQ, KV_WIDTH)), smem, smem,
                  const((BLK, BLK)), const((1, ATTN_WIDTH))] + more_specs,
        out_specs=[rows(ATTN_WIDTH), rows(ATTN_WIDTH),
                   pl.BlockSpec((QUERY_BLOCKS_PER_STEP, N_HEADS // GQA_GROUP, GROUP_ROWS, BLK), lambda i: (i, 0, 0, 0)),
                   rows(128)],
        out_shape=[jax.ShapeDtypeStruct((SEQ, ATTN_WIDTH), F32), jax.ShapeDtypeStruct((SEQ, ATTN_WIDTH), BF16),
                   jax.ShapeDtypeStruct((n_b, N_HEADS // GQA_GROUP, GROUP_ROWS, BLK), BF16),
                   jax.ShapeDtypeStruct((SEQ, 128), F32)],
        scratch_shapes=[pltpu.VMEM((N_HEADS // GQA_GROUP, GROUP_ROWS, BLK), F32)],
        compiler_params=_params(("arbitrary",)),
    )(qn, kn, v, tbl, sinks, bkt, gattn, *more)


def _ffn_block(i, step):
    return jnp.where(i % 2 == 0, step, N_FFN_BLK - 1 - step)


def _ffn_fwd(x, ycn, yan, w_out, g2, w_up, fcw, fcb, w_down, tgt):
    tm = 512
    n_t = SEQ // tm

    def body(x_ref, ycn_ref, yan_ref, wo_ref, g2_ref, wu_hbm, cw_ref, b_ref, wd_hbm, tgt_ref,
             h1_ref, u2_ref, up_ref, pre_ref, act_ref, dh2_ref, dh2b_ref, loss_ref, acc_ref, halo_ref,
             wu_buf, wd_buf, w_sem):
        i, step = pl.program_id(0), pl.program_id(1)
        j = _ffn_block(i, step)

        now = i * N_FFN_BLK + step

        def weight_copies(t):
            block = _ffn_block(t // N_FFN_BLK, t % N_FFN_BLK)
            slot = t % WEIGHT_RING
            return (pltpu.make_async_copy(wu_hbm.at[:, block], wu_buf.at[slot], w_sem.at[slot, 0]),
                    pltpu.make_async_copy(wd_hbm.at[block], wd_buf.at[slot], w_sem.at[slot, 1]))

        @pl.when(now == 0)
        def _():
            for t in range(WEIGHT_RING - 1):
                for cp in weight_copies(t):
                    cp.start()

        @pl.when(now + WEIGHT_RING - 1 < n_t * N_FFN_BLK)
        def _():
            for cp in weight_copies(now + WEIGHT_RING - 1):
                cp.start()

        for cp in weight_copies(now):
            cp.wait()
        wu_ref, wd_ref = wu_buf.at[now % WEIGHT_RING], wd_buf.at[now % WEIGHT_RING]

        @pl.when((i == 0) & (step == 0))
        def _():
            loss_ref[...] = jnp.zeros_like(loss_ref)

        @pl.when(step == 0)
        def _():
            h1 = x_ref[...] + _dot(ycn_ref[...], wo_ref[0:CONV_WIDTH, :]) + _dot(yan_ref[...], wo_ref[CONV_WIDTH:, :])
            h1_ref[...] = h1
            u2_ref[...] = (h1 * _rstd(h1) * g2_ref[...]).astype(BF16)
            acc_ref[...] = jnp.zeros_like(acc_ref)

        u2 = u2_ref[...]
        pre = []
        for s in range(2):
            up = _dot_nt(u2, wu_ref[s])
            up_ref[s] = up.astype(BF16)
            halo = jnp.where(i == 0, 0.0, halo_ref[s, j])
            pre.append(_conv3(up, _taps(cw_ref.at[s]), halo)[0] + b_ref[s])
            pre_ref[s] = pre[s].astype(BF16)
            halo_ref[s, j] = up[tm - SUBLANES:]
        g, val = pre
        act = (g * jax.nn.sigmoid(g) * val).astype(BF16)
        act_ref[...] = act
        acc_ref[...] += _dot(act, wd_ref[...])

        @pl.when(step == N_FFN_BLK - 1)
        def _():
            err = h1_ref[...] + acc_ref[...] - tgt_ref[...]
            loss_ref[...] += 0.5 * jnp.sum(err * err) / D_MODEL
            dh2 = err / D_MODEL
            dh2_ref[...] = dh2
            dh2b_ref[...] = dh2.astype(BF16)

    rows = lambda w: pl.BlockSpec((tm, w), lambda i, step: (i, 0))
    const = lambda shape: pl.BlockSpec(shape, lambda i, step: (0,) * len(shape))
    pair = lambda *s: pl.BlockSpec((2, None) + s, lambda i, step: (0, _ffn_block(i, step)) + (0,) * len(s))
    upb = pl.BlockSpec((2, None, tm, FFN_BLK), lambda i, step: (0, _ffn_block(i, step), i, 0))
    return pl.pallas_call(
        body, name="ffn_fwd", grid=(n_t, N_FFN_BLK),
        in_specs=[rows(D_MODEL), rows(CONV_WIDTH), rows(ATTN_WIDTH), const((D_MODEL, D_MODEL)), const((1, D_MODEL)),
                  _ANY, pair(3, 1, FFN_BLK), pair(1, FFN_BLK), _ANY, rows(D_MODEL)],
        out_specs=[rows(D_MODEL), rows(D_MODEL), upb, upb,
                   pl.BlockSpec((None, tm, FFN_BLK), lambda i, step: (_ffn_block(i, step), i, 0)),
                   rows(D_MODEL), rows(D_MODEL), const((SUBLANES, 128))],
        out_shape=[jax.ShapeDtypeStruct((SEQ, D_MODEL), F32), jax.ShapeDtypeStruct((SEQ, D_MODEL), BF16),
                   jax.ShapeDtypeStruct((2, N_FFN_BLK, SEQ, FFN_BLK), BF16),
                   jax.ShapeDtypeStruct((2, N_FFN_BLK, SEQ, FFN_BLK), BF16),
                   jax.ShapeDtypeStruct((N_FFN_BLK, SEQ, FFN_BLK), BF16),
                   jax.ShapeDtypeStruct((SEQ, D_MODEL), F32), jax.ShapeDtypeStruct((SEQ, D_MODEL), BF16),
                   jax.ShapeDtypeStruct((SUBLANES, 128), F32)],
        scratch_shapes=[pltpu.VMEM((tm, D_MODEL), F32), pltpu.VMEM((2, N_FFN_BLK, SUBLANES, FFN_BLK), F32),
                        pltpu.VMEM((WEIGHT_RING, 2, FFN_BLK, D_MODEL), BF16), pltpu.VMEM((WEIGHT_RING, FFN_BLK, D_MODEL), BF16),
                        pltpu.SemaphoreType.DMA((WEIGHT_RING, 2))],
        compiler_params=_params(("arbitrary", "arbitrary")),
    )(x, ycn, yan, w_out, g2, w_up, fcw, fcb, w_down, tgt)


def _ffn_bwd(dh2, dh2b, h1, g2, up, pre, w_up, fcw, w_down, after=None):
    tm = 512
    units = ((288, 224), (0, 288))
    n_t = SEQ // tm

    def body(dh2_ref, dh2b_ref, h1_ref, g2_ref, up_ref, pre_ref, wu_ref, cw_ref, wd_ref,
             dup_ref, dh1_ref, dh1b_ref, dfb_ref, dfcw_ref, dg2_ref, acc_ref, next_ref):
        j, i = pl.program_id(0), pl.program_id(1)
        tile = pl.ds(pl.multiple_of((n_t - 1 - i) * tm, tm), tm)

        @pl.when((j == 0) & (i == 0))
        def _():
            dfb_ref[...] = jnp.zeros_like(dfb_ref)
            dfcw_ref[...] = jnp.zeros_like(dfcw_ref)
            dg2_ref[...] = jnp.zeros_like(dg2_ref)

        @pl.when(j == 0)
        def _():
            acc_ref[tile, :] = jnp.zeros((tm, D_MODEL), F32)

        nxt = [jnp.where(i == 0, 0.0, next_ref[s]) for s in range(2)]
        sums = [[0.0] * 4 for _ in range(2)]
        for r0, rn in units:
            rows = slice(r0, r0 + rn)
            g, val = pre_ref[0, rows, :].astype(F32), pre_ref[1, rows, :].astype(F32)
            sg = jax.nn.sigmoid(g)
            silu = g * sg
            dact = _dot_nt(dh2b_ref[rows, :], wd_ref[...])
            dpre = (dact * val * (sg * (1.0 + g * (1.0 - sg))), dact * silu)
            dups = []
            for s in range(2):
                d = dpre[s]
                u = up_ref[s, rows, :].astype(F32)
                w = _taps(cw_ref.at[s])
                d1 = _shift_up(d, 1, nxt[s])
                d2 = _shift_up(d, 2, nxt[s])
                nxt[s] = d[:SUBLANES]
                for t, term in enumerate((d, d2 * u, d1 * u, d * u)):
                    sums[s][t] = sums[s][t] + jnp.sum(term, axis=0, keepdims=True)
                dups.append((d * w[2] + d1 * w[1] + d2 * w[0]).astype(BF16))
                dup_ref[s, rows, :] = dups[s]
            acc_rows = pl.ds(pl.multiple_of((n_t - 1 - i) * tm + r0, SUBLANES), rn)
            acc_ref[acc_rows, :] += _dot(dups[0], wu_ref[0]) + _dot(dups[1], wu_ref[1])
        for s in range(2):
            next_ref[s] = nxt[s]
            dfb_ref[s, j] += sums[s][0]
            for t in range(3):
                dfcw_ref[s, j, t] += sums[s][1 + t]

        @pl.when(j == N_FFN_BLK - 1)
        def _():
            dn, dgain = _rms_bwd(h1_ref[...], g2_ref[...], acc_ref[tile, :])
            dh1 = dh2_ref[...] + dn
            dh1_ref[...] = dh1
            dh1b_ref[...] = dh1.astype(BF16)
            dg2_ref[...] += dgain

    rev = lambda i: n_t - 1 - i
    rows = lambda w: pl.BlockSpec((tm, w), lambda j, i: (rev(i), 0))
    last_rows = lambda w: pl.BlockSpec((tm, w), lambda j, i: (jnp.where(j == N_FFN_BLK - 1, rev(i), rev(0)), 0))
    const = lambda shape: pl.BlockSpec(shape, lambda j, i: (0,) * len(shape))
    pair = lambda *s: pl.BlockSpec((2, None) + s, lambda j, i: (0, j) + (0,) * len(s))
    upb = pl.BlockSpec((2, None, tm, FFN_BLK), lambda j, i: (0, j, rev(i), 0))
    body, more_specs, more = _ordered_behind(body, 9, after)
    return pl.pallas_call(
        body, name="ffn_bwd", grid=(N_FFN_BLK, n_t),
        in_specs=[last_rows(D_MODEL), rows(D_MODEL), last_rows(D_MODEL), const((1, D_MODEL)), upb, upb,
                  pair(FFN_BLK, D_MODEL), pair(3, 1, FFN_BLK),
                  pl.BlockSpec((None, FFN_BLK, D_MODEL), lambda j, i: (j, 0, 0))] + more_specs,
        out_specs=[upb, last_rows(D_MODEL), last_rows(D_MODEL),
                   const((2, N_FFN_BLK, 1, FFN_BLK)), const((2, N_FFN_BLK, 3, 1, FFN_BLK)), const((1, D_MODEL))],
        out_shape=[jax.ShapeDtypeStruct((2, N_FFN_BLK, SEQ, FFN_BLK), BF16), jax.ShapeDtypeStruct((SEQ, D_MODEL), F32),
                   jax.ShapeDtypeStruct((SEQ, D_MODEL), BF16), jax.ShapeDtypeStruct((2, N_FFN_BLK, 1, FFN_BLK), F32),
                   jax.ShapeDtypeStruct((2, N_FFN_BLK, 3, 1, FFN_BLK), F32), jax.ShapeDtypeStruct((1, D_MODEL), F32)],
        scratch_shapes=[pltpu.VMEM((SEQ, D_MODEL), F32), pltpu.VMEM((2, SUBLANES, FFN_BLK), F32)],
        compiler_params=_params(("arbitrary", "arbitrary")),
    )(dh2, dh2b, h1, g2, up, pre, w_up, fcw, w_down, *more)


def _grad_tn(a_list, b, out_rows, name, after=None):
    n = len(a_list)
    ncol = b.shape[1]

    def body(*refs):
        a_refs, b_ref, o_ref = refs[:n], refs[n], refs[n + 1]
        j = pl.program_id(0)
        for k in range(n):
            @pl.when(j == k)
            def _(k=k):
                o_ref[...] = _dot_tn(a_refs[k][...], b_ref[...]).astype(BF16)

    full = lambda shape: pl.BlockSpec(shape, lambda j: (0,) * len(shape))
    body, more_specs, more = _ordered_behind(body, n + 1, after)
    return pl.pallas_call(
        body, name=name, grid=(n,),
        in_specs=[full((SEQ, out_rows))] * n + [full((SEQ, ncol))] + more_specs,
        out_specs=pl.BlockSpec((None, out_rows, ncol), lambda j: (j, 0, 0)),
        out_shape=jax.ShapeDtypeStruct((n, out_rows, ncol), BF16),
        compiler_params=_params(("arbitrary",)),
    )(*a_list, b, *more)


def _grad_tn_blocked(a, b, name, per_step=2):
    nb, _, a_w = a.shape
    b_w = b.shape[-1]

    def body(a_ref, b_ref, o_ref):
        for p in range(per_step):
            o_ref[p] = _dot_tn(a_ref[p], b_ref[...]).astype(BF16)

    return pl.pallas_call(
        body, name=name, grid=(nb // per_step,),
        in_specs=[pl.BlockSpec((per_step, SEQ, a_w), lambda k: (k, 0, 0)), pl.BlockSpec((SEQ, b_w), lambda k: (0, 0))],
        out_specs=pl.BlockSpec((per_step, a_w, b_w), lambda k: (k, 0, 0)),
        out_shape=jax.ShapeDtypeStruct((nb, a_w, b_w), BF16),
        compiler_params=_params(("arbitrary",)),
    )(a, b)


def _out_bwd(dh1b, w_out, y_attn, gattn, after=None):
    tm = 1024
    n_t = SEQ // tm

    def body(dh_ref, wo_ref, y_ref, ga_ref, dycn_ref, dy_ref, dga_ref):
        @pl.when(pl.program_id(0) == 0)
        def _():
            dga_ref[...] = jnp.zeros_like(dga_ref)

        dycat = _dot_nt(dh_ref[...], wo_ref[...])
        dycn_ref[...] = dycat[:, :CONV_WIDTH]
        dy, dga = _rms_bwd(y_ref[...], ga_ref[...], dycat[:, CONV_WIDTH:])
        dy_ref[...] = dy
        dga_ref[...] += dga

    rows = lambda w: pl.BlockSpec((tm, w), lambda i: (i, 0))
    const = lambda shape: pl.BlockSpec(shape, lambda i: (0,) * len(shape))
    body, more_specs, more = _ordered_behind(body, 4, after)
    return pl.pallas_call(
        body, name="out_bwd", grid=(n_t,),
        in_specs=[rows(D_MODEL), const((D_MODEL, D_MODEL)), rows(ATTN_WIDTH), const((1, ATTN_WIDTH))] + more_specs,
        out_specs=[rows(CONV_WIDTH), rows(ATTN_WIDTH), const((1, ATTN_WIDTH))],
        out_shape=[jax.ShapeDtypeStruct((SEQ, CONV_WIDTH), F32), jax.ShapeDtypeStruct((SEQ, ATTN_WIDTH), F32),
                   jax.ShapeDtypeStruct((1, ATTN_WIDTH), F32)],
        compiler_params=_params(("arbitrary",)),
    )(dh1b, w_out, y_attn, gattn, *more)


def _attn_bwd(qn, kn, v, dy, probs, psinks, bkt, after=None):
    n_b = SEQ // BLK

    def body(q_ref, k_ref, v_ref, dy_ref, p_ref, ps_ref, bkt_ref,
             dq_ref, dk_ref, dv_ref, dtbl_ref, dsink_ref, dbias_ref, dsacc_ref):
        step = pl.program_id(0)

        @pl.when(step == 0)
        def _():
            dbias_ref[...] = jnp.zeros_like(dbias_ref)
            dsacc_ref[...] = jnp.zeros_like(dsacc_ref)
            dk_ref[...] = jnp.zeros_like(dk_ref)
            dv_ref[...] = jnp.zeros_like(dv_ref)

        lane = lax.broadcasted_iota(jnp.int32, (BLK, 128), 1)
        for blk in range(QUERY_BLOCKS_PER_STEP):
            i = QUERY_BLOCKS_PER_STEP * step + blk
            rows = slice(BLK * blk, BLK * (blk + 1))
            kb, prev, cur = _band_rows(k_ref, i)
            vb, _, _ = _band_rows(v_ref, i)
            upper, _ = _band_masks(i)
            q = q_ref[rows, :]
            dy = dy_ref[rows, :]
            psink = ps_ref[rows, :]
            dsink = jnp.zeros((BLK, 128), F32)
            dqs, dks, dvs = [], [], []
            for g in range(N_HEADS // GQA_GROUP):
                kv = slice(HEAD_DIM * g, HEAD_DIM * (g + 1))
                qg = _stack_heads(q, g)
                dog = _stack_heads(dy, g).astype(BF16)
                pb = p_ref[blk, g]
                pg = pb.astype(F32)
                dprobs = _fold(_dot_nt(dog, vb[:, kv]), upper)
                dvs.append(_dot_tn(_unfold(pb, upper), dog))
                dsum = jnp.sum(pg * dprobs, axis=-1, keepdims=True)
                dlogits = pg * (dprobs - dsum)
                for t in range(GQA_GROUP):
                    dsink = jnp.where(lane == GQA_GROUP * g + t, -psink * dsum[BLK * t:BLK * (t + 1)], dsink)
                dbias_ref[g] += dlogits
                ds = _unfold(dlogits * (HEAD_DIM ** -0.5), upper).astype(BF16)
                dqs.append(_dot(ds, kb[:, kv]))
                dks.append(_dot_tn(ds, qg))
            dsacc_ref[...] += dsink
            dq_ref[rows, :] = _unstack_heads(dqs)
            dkb = jnp.concatenate(dks, axis=-1)
            dvb = jnp.concatenate(dvs, axis=-1)
            dk_ref[pl.ds(prev, BLK), :] += dkb[:BLK]
            dk_ref[pl.ds(cur, BLK), :] += dkb[BLK:]
            dv_ref[pl.ds(prev, BLK), :] += dvb[:BLK]
            dv_ref[pl.ds(cur, BLK), :] += dvb[BLK:]

        @pl.when(step == n_b // QUERY_BLOCKS_PER_STEP - 1)
        def _():
            bkt = bkt_ref[...]
            row8 = lax.broadcasted_iota(jnp.int32, (N_HEADS, 128), 0)
            lane8 = lax.broadcasted_iota(jnp.int32, (N_HEADS, 128), 1)
            acc = jnp.zeros((N_HEADS, 128), F32)
            for h in range(N_HEADS):
                rows = slice(BLK * (h % GQA_GROUP), BLK * (h % GQA_GROUP + 1))
                dbh = dbias_ref[h // GQA_GROUP, rows, :]
                for b in range(NUM_BUCKETS):
                    acc = jnp.where((row8 == h) & (lane8 == b), jnp.sum(jnp.where(bkt == b, dbh, 0.0)), acc)
            dsink_ref[...] = jnp.sum(dsacc_ref[...], axis=0, keepdims=True)
            dtbl_ref[...] = acc

    const = lambda shape: pl.BlockSpec(shape, lambda i: (0,) * len(shape))
    rows = lambda w: pl.BlockSpec((QUERY_BLOCKS_PER_STEP * BLK, w), lambda i: (i, 0))
    n_g = N_HEADS // GQA_GROUP
    body, more_specs, more = _ordered_behind(body, 7, after)
    return pl.pallas_call(
        body, name="attn_bwd", grid=(n_b // QUERY_BLOCKS_PER_STEP,),
        in_specs=[rows(ATTN_WIDTH), const((SEQ, KV_WIDTH)), const((SEQ, KV_WIDTH)), rows(ATTN_WIDTH),
                  pl.BlockSpec((QUERY_BLOCKS_PER_STEP, n_g, GROUP_ROWS, BLK), lambda i: (i, 0, 0, 0)), rows(128),
                  const((BLK, BLK))] + more_specs,
        out_specs=[rows(ATTN_WIDTH), const((SEQ, KV_WIDTH)), const((SEQ, KV_WIDTH)), const((N_HEADS, 128)), const((1, 128))],
        out_shape=[jax.ShapeDtypeStruct((SEQ, ATTN_WIDTH), F32), jax.ShapeDtypeStruct((SEQ, KV_WIDTH), F32),
                   jax.ShapeDtypeStruct((SEQ, KV_WIDTH), F32), jax.ShapeDtypeStruct((N_HEADS, 128), F32),
                   jax.ShapeDtypeStruct((1, 128), F32)],
        scratch_shapes=[pltpu.VMEM((n_g, GROUP_ROWS, BLK), F32), pltpu.VMEM((BLK, 128), F32)],
        compiler_params=_params(("arbitrary",)),
    )(qn, kn, v, dy, probs, psinks, bkt, *more)


def _mix_in_bwd(x, dh1, proj, dycn, dqn, dkn, dv, w_in_t, conv_w, g1, gq, gk, gconv):
    tm = 512
    n_t = SEQ // tm
    halo_blocks = tm // SUBLANES

    def body(x_ref, dh1_ref, proj_ref, halo_ref, dycn_ref, dqn_ref, dkn_ref, dv_ref, w_ref, cw_ref,
             g1_ref, gq_ref, gk_ref, gc_ref,
             dx_ref, dproj_ref, dcw_ref, dgc_ref, dgq_ref, dgk_ref, dg1_ref, next_ref):
        i = pl.program_id(0)
        first_tile = i == n_t - 1

        @pl.when(i == 0)
        def _():
            for r in (dcw_ref, dgc_ref, dgq_ref, dgk_ref, dg1_ref, next_ref):
                r[...] = jnp.zeros_like(r)

        proj = proj_ref[...]
        hp = halo_ref[...]
        gate_b = proj[:, 0:CONV_WIDTH]
        gate_c = proj[:, CONV_WIDTH:2 * CONV_WIDTH]
        hc = proj[:, 2 * CONV_WIDTH:3 * CONV_WIDTH]
        a = gate_c * hc
        a_halo = jnp.where(first_tile, 0.0, hp[:, CONV_WIDTH:2 * CONV_WIDTH] * hp[:, 2 * CONV_WIDTH:3 * CONV_WIDTH])
        cw = _taps(cw_ref[...])
        cv, a2, a1 = _conv3(a, cw, a_halo)
        dyc, dgc = _rms_bwd(gate_b * cv, gc_ref[...], dycn_ref[...])
        dgc_ref[...] += dgc
        dcv = dyc * gate_b
        dcw_ref[...] += jnp.concatenate(
            [jnp.sum(dcv * a2, axis=0, keepdims=True), jnp.sum(dcv * a1, axis=0, keepdims=True),
             jnp.sum(dcv * a, axis=0, keepdims=True)], axis=0)
        da = _conv3_bwd_input(dcv, cw, next_ref[...])
        next_ref[...] = dcv[:SUBLANES]
        q0 = 3 * CONV_WIDTH
        k0 = q0 + ATTN_WIDTH
        dq, dgq = _head_norm_bwd(proj[:, q0:k0], gq_ref[...], dqn_ref[...], N_HEADS)
        dk, dgk = _head_norm_bwd(proj[:, k0:k0 + KV_WIDTH], gk_ref[...], dkn_ref[...], 2)
        dgq_ref[...] += dgq
        dgk_ref[...] += dgk
        dproj = jnp.concatenate([dyc * cv, da * hc, da * gate_c, dq, dk, dv_ref[...]], axis=-1).astype(BF16)
        dproj_ref[...] = dproj
        du1 = _dot(dproj, w_ref[...])
        xv = x_ref[...]
        dn, dg1 = _rms_bwd(xv, g1_ref[...], du1)
        dx_ref[...] = dh1_ref[...] + dn
        dg1_ref[...] += dg1

    rev = lambda i: n_t - 1 - i
    rows = lambda w: pl.BlockSpec((tm, w), lambda i: (rev(i), 0))
    const = lambda shape: pl.BlockSpec(shape, lambda i: (0,) * len(shape))
    halo = pl.BlockSpec((SUBLANES, IN_WIDTH), lambda i: (jnp.maximum(rev(i) * halo_blocks - 1, 0), 0))
    return pl.pallas_call(
        body, name="mix_in_bwd", grid=(n_t,),
        in_specs=[rows(D_MODEL), rows(D_MODEL), rows(IN_WIDTH), halo, rows(CONV_WIDTH), rows(ATTN_WIDTH), rows(KV_WIDTH),
                  rows(KV_WIDTH), const((IN_WIDTH, D_MODEL)), const((3, CONV_WIDTH)), const((1, D_MODEL)),
                  const((1, HEAD_DIM)), const((1, HEAD_DIM)), const((1, CONV_WIDTH))],
        out_specs=[rows(D_MODEL), rows(IN_WIDTH), const((3, CONV_WIDTH)), const((1, CONV_WIDTH)),
                   const((1, HEAD_DIM)), const((1, HEAD_DIM)), const((1, D_MODEL))],
        out_shape=[jax.ShapeDtypeStruct((SEQ, D_MODEL), F32), jax.ShapeDtypeStruct((SEQ, IN_WIDTH), BF16),
                   jax.ShapeDtypeStruct((3, CONV_WIDTH), F32),
                   jax.ShapeDtypeStruct((1, CONV_WIDTH), F32), jax.ShapeDtypeStruct((1, HEAD_DIM), F32),
                   jax.ShapeDtypeStruct((1, HEAD_DIM), F32), jax.ShapeDtypeStruct((1, D_MODEL), F32)],
        scratch_shapes=[pltpu.VMEM((SUBLANES, CONV_WIDTH), F32)],
        compiler_params=_params(("arbitrary",)),
    )(x, dh1, proj, proj, dycn, dqn, dkn, dv, w_in_t, conv_w, g1, gq, gk, gconv)


def _grad_w_in(dproj, u1, after=None):
    bw = 768

    def body(a_ref, b_ref, o_ref):
        o_ref[...] = _dot_tn(a_ref[...], b_ref[...]).astype(BF16)

    body, more_specs, more = _ordered_behind(body, 2, after)
    return pl.pallas_call(
        body, name="grad_w_in", grid=(IN_WIDTH // bw,),
        in_specs=[pl.BlockSpec((SEQ, bw), lambda k: (0, k)), pl.BlockSpec((SEQ, D_MODEL), lambda k: (0, 0))] + more_specs,
        out_specs=pl.BlockSpec((bw, D_MODEL), lambda k: (k, 0)),
        out_shape=jax.ShapeDtypeStruct((IN_WIDTH, D_MODEL), BF16),
        compiler_params=_params(("arbitrary",)),
    )(dproj, u1, *more)


def _adamw_math(w, g, m, v):
    m = ADAM_B1 * m + (1.0 - ADAM_B1) * g
    v = ADAM_B2 * v + (1.0 - ADAM_B2) * (g * g)
    m_hat = m / (1.0 - ADAM_B1 ** ADAM_STEP)
    v_hat = v / (1.0 - ADAM_B2 ** ADAM_STEP)
    return -ADAM_LR * (m_hat / (jnp.sqrt(v_hat) + ADAM_EPS) + ADAM_WD * w), m, v


_ROW_G1, _ROW_G2, _ROW_OUT_NORMS, _ROW_FFN_B, _ROW_GQ, _ROW_GK, _ROW_SINKS, _ROW_LOSS, _ROW_TABLE = 0, 1, 2, 3, 11, 12, 13, 14, 16
SMALL_ROWS, SMALL_COLS = 24, 1024
_SMALL_NAMES = ("norm_mix_g", "norm_ffn_g", "out_norm_conv_g", "out_norm_attn_g", "ffn_conv_b", "q_norm_g", "k_norm_g",
                "sinks", "rel_bias_table")


def _pack_small_grads(dg1, dg2, dgconv, dgattn, dfb, dgq, dgk, dsinks, dtbl_t, loss_acc, after=None):
    def body(dg1_ref, dg2_ref, dgc_ref, dga_ref, dfb_ref, dgq_ref, dgk_ref, ds_ref, dt_ref, loss_ref, o_ref, all_ref):
        o_ref[...] = jnp.zeros_like(o_ref)
        o_ref[_ROW_G1:_ROW_G1 + 1, :] = dg1_ref[...]
        o_ref[_ROW_G2:_ROW_G2 + 1, :] = dg2_ref[...]
        o_ref[_ROW_OUT_NORMS:_ROW_OUT_NORMS + 1, 0:CONV_WIDTH] = dgc_ref[...]
        o_ref[_ROW_OUT_NORMS:_ROW_OUT_NORMS + 1, CONV_WIDTH:] = dga_ref[...]
        for k in range(N_DEV):
            o_ref[_ROW_FFN_B + k:_ROW_FFN_B + k + 1, 0:FFN_BLK] = dfb_ref[k // N_FFN_BLK, k % N_FFN_BLK]
        o_ref[_ROW_GQ:_ROW_GQ + 1, 0:HEAD_DIM] = dgq_ref[...]
        o_ref[_ROW_GK:_ROW_GK + 1, 0:HEAD_DIM] = dgk_ref[...]
        o_ref[_ROW_SINKS:_ROW_SINKS + 1, 0:128] = ds_ref[...]
        o_ref[_ROW_LOSS:_ROW_LOSS + 1, 0:128] = loss_ref[0:1, :]
        o_ref[_ROW_TABLE:_ROW_TABLE + N_HEADS, 0:128] = dt_ref[...]
        for s in range(N_DEV):
            all_ref[s] = o_ref[...]

    body, more_specs, more = _ordered_behind(body, 10, after)
    return pl.pallas_call(
        body, name="pack_small_grads",
        in_specs=[pl.BlockSpec(memory_space=pltpu.VMEM)] * 10 + more_specs,
        out_shape=[jax.ShapeDtypeStruct((SMALL_ROWS, SMALL_COLS), F32),
                   jax.ShapeDtypeStruct((N_DEV, SMALL_ROWS, SMALL_COLS), F32)],
    )(dg1, dg2, dgconv, dgattn, dfb, dgq, dgk, dsinks, dtbl_t, loss_acc, *more)


def _adamw_small(recv, params, after):
    names = _SMALL_NAMES
    n = len(names)

    def grad_of(g, name, k=None):
        if name == "norm_mix_g":
            return g[_ROW_G1:_ROW_G1 + 1, :]
        if name == "norm_ffn_g":
            return g[_ROW_G2:_ROW_G2 + 1, :]
        if name == "out_norm_conv_g":
            return g[_ROW_OUT_NORMS:_ROW_OUT_NORMS + 1, 0:CONV_WIDTH]
        if name == "out_norm_attn_g":
            return g[_ROW_OUT_NORMS:_ROW_OUT_NORMS + 1, CONV_WIDTH:]
        if name == "ffn_conv_b":
            return g[_ROW_FFN_B + k:_ROW_FFN_B + k + 1, 0:FFN_BLK]
        if name == "q_norm_g":
            return g[_ROW_GQ:_ROW_GQ + 1, 0:HEAD_DIM]
        if name == "k_norm_g":
            return g[_ROW_GK:_ROW_GK + 1, 0:HEAD_DIM]
        if name == "sinks":
            return g[_ROW_SINKS:_ROW_SINKS + 1, 0:N_HEADS]
        return g[_ROW_TABLE:_ROW_TABLE + N_HEADS, 0:NUM_BUCKETS]

    def body(r_ref, *refs):
        ins, outs, loss_ref = refs[:3 * n], refs[3 * n:7 * n], refs[7 * n]
        g = r_ref[0]
        for s in range(1, N_DEV):
            g = g + r_ref[s]
        loss_ref[...] = g[_ROW_LOSS:_ROW_LOSS + 1, 0:128]
        for i, name in enumerate(names):
            w_ref, m_ref, v_ref = ins[3 * i:3 * i + 3]
            o = outs[4 * i:4 * i + 4]
            cols = [slice(FFN_BLK * k, FFN_BLK * (k + 1)) for k in range(N_DEV)] if name == "ffn_conv_b" else [slice(None)]
            for k, cs in enumerate(cols):
                gk = grad_of(g, name, k)
                d, m2, v2 = _adamw_math(w_ref[:, cs], gk, m_ref[:, cs], v_ref[:, cs])
                o[0][:, cs], o[1][:, cs], o[2][:, cs], o[3][:, cs] = gk, d, m2, v2

    flat = [a for name in names for a in params[name]]
    body, more_specs, more = _ordered_behind(body, 1 + 3 * n, after)
    vmem = pl.BlockSpec(memory_space=pltpu.VMEM)
    out = pl.pallas_call(
        body, name="adamw_small",
        in_specs=[vmem] * (1 + 3 * n) + more_specs,
        out_shape=[jax.ShapeDtypeStruct(params[name][0].shape, F32) for name in names for _ in range(4)]
        + [jax.ShapeDtypeStruct((1, 128), F32)],
        compiler_params=pltpu.CompilerParams(vmem_limit_bytes=VMEM_LIMIT),
    )(recv, *flat, *more)
    return {name: tuple(out[4 * i:4 * i + 4]) for i, name in enumerate(names)}, out[4 * n]


def _adamw_direct(w, m, v, own, recv, me, name, row_blocks=1, after=None):
    rb = w.shape[0] // row_blocks
    cols = w.shape[1]

    def body(me_ref, w_ref, m_ref, v_ref, o_ref, r_ref, g_o, d_o, m_o, v_o):
        g = o_ref[...].astype(F32)
        for s in range(N_DEV - 1):
            g = g + r_ref[s].astype(F32)
        g_o[...] = g
        d_o[...], m_o[...], v_o[...] = _adamw_math(w_ref[...], g, m_ref[...], v_ref[...])

    blk = pl.BlockSpec((rb, cols), lambda i, me_ref: (i, 0))
    oblk = pl.BlockSpec((None, rb, cols), lambda i, me_ref: (me_ref[0], i, 0))
    rblk = pl.BlockSpec((N_DEV - 1, rb, cols), lambda i, me_ref: (0, i, 0))
    body, more_specs, more = _ordered_behind(body, 6, after)
    return pl.pallas_call(
        body, name=name,
        grid_spec=pltpu.PrefetchScalarGridSpec(num_scalar_prefetch=1, grid=(row_blocks,),
                                               in_specs=[blk, blk, blk, oblk, rblk] + more_specs, out_specs=[blk] * 4),
        out_shape=[jax.ShapeDtypeStruct(w.shape, F32)] * 4,
        compiler_params=_params(("arbitrary",)),
    )(me, w, m, v, own, recv, *more)


def _adamw(w, m, v, part, recv, chip, name, row_blocks=1, after=None):
    rb = w.shape[0] // row_blocks
    tail = w.shape[1:]
    zeros = (0,) * len(tail)

    def body(chip_ref, w_ref, m_ref, v_ref, p_ref, r_ref, g_o, d_o, m_o, v_o):
        g = p_ref[...].astype(F32)
        for s in range(3):
            g = g + r_ref[s].astype(F32)
        g_o[...] = g
        d_o[...], m_o[...], v_o[...] = _adamw_math(w_ref[...], g, m_ref[...], v_ref[...])

    blk = pl.BlockSpec((rb,) + tail, lambda i, chip_ref: (i,) + zeros)
    pblk = pl.BlockSpec((None, rb) + tail, lambda i, chip_ref: (chip_ref[0], i) + zeros)
    rblk = pl.BlockSpec((3, rb) + tail, lambda i, chip_ref: (0, i) + zeros)
    body, more_specs, more = _ordered_behind(body, 6, after)
    return pl.pallas_call(
        body, name=name,
        grid_spec=pltpu.PrefetchScalarGridSpec(num_scalar_prefetch=1, grid=(row_blocks,),
                                               in_specs=[blk, blk, blk, pblk, rblk] + more_specs, out_specs=[blk] * 4),
        out_shape=[jax.ShapeDtypeStruct(w.shape, F32)] * 4,
        compiler_params=_params(("arbitrary",)),
    )(chip, w, m, v, part, recv, *more)


def kernel(x, norm_mix_g, w_in, conv_w, q_norm_g, k_norm_g, rel_bias_table, sinks, out_norm_conv_g, out_norm_attn_g, w_out, norm_ffn_g, w_up, ffn_conv_w, ffn_conv_b, w_down, loss_target, m_norm_mix_g, m_w_in, m_conv_w, m_q_norm_g, m_k_norm_g, m_rel_bias_table, m_sinks, m_out_norm_conv_g, m_out_norm_attn_g, m_w_out, m_norm_ffn_g, m_w_up, m_ffn_conv_w, m_ffn_conv_b, m_w_down, v_norm_mix_g, v_w_in, v_conv_w, v_q_norm_g, v_k_norm_g, v_rel_bias_table, v_sinks, v_out_norm_conv_g, v_out_norm_attn_g, v_w_out, v_norm_ffn_g, v_w_up, v_ffn_conv_w, v_ffn_conv_b, v_w_down):
    p = dict(norm_mix_g=norm_mix_g, w_in=w_in, conv_w=conv_w, q_norm_g=q_norm_g, k_norm_g=k_norm_g,
             rel_bias_table=rel_bias_table, sinks=sinks, out_norm_conv_g=out_norm_conv_g, out_norm_attn_g=out_norm_attn_g,
             w_out=w_out, norm_ffn_g=norm_ffn_g, w_up=w_up, ffn_conv_w=ffn_conv_w, ffn_conv_b=ffn_conv_b, w_down=w_down)
    m = dict(norm_mix_g=m_norm_mix_g, w_in=m_w_in, conv_w=m_conv_w, q_norm_g=m_q_norm_g, k_norm_g=m_k_norm_g,
             rel_bias_table=m_rel_bias_table, sinks=m_sinks, out_norm_conv_g=m_out_norm_conv_g,
             out_norm_attn_g=m_out_norm_attn_g, w_out=m_w_out, norm_ffn_g=m_norm_ffn_g, w_up=m_w_up,
             ffn_conv_w=m_ffn_conv_w, ffn_conv_b=m_ffn_conv_b, w_down=m_w_down)
    v = dict(norm_mix_g=v_norm_mix_g, w_in=v_w_in, conv_w=v_conv_w, q_norm_g=v_q_norm_g, k_norm_g=v_k_norm_g,
             rel_bias_table=v_rel_bias_table, sinks=v_sinks, out_norm_conv_g=v_out_norm_conv_g,
             out_norm_attn_g=v_out_norm_attn_g, w_out=v_w_out, norm_ffn_g=v_norm_ffn_g, w_up=v_w_up,
             ffn_conv_w=v_ffn_conv_w, ffn_conv_b=v_ffn_conv_b, w_down=v_w_down)

    xs, tgt = x[0], loss_target[0]
    g1, g2, gq, gk, gconv, gattn = norm_mix_g, norm_ffn_g, q_norm_g, k_norm_g, out_norm_conv_g, out_norm_attn_g
    ix, iy, ic = _coords()
    core = ic.astype(jnp.int32).reshape(1)
    chip = (2 * ix + iy).astype(jnp.int32).reshape(1)
    me = _lin(ix, iy, ic).astype(jnp.int32).reshape(1)
    bkt = jnp.asarray(_bucket_map())
    tr = lambda a: a[0].T
    taps = lambda a: jnp.transpose(a, (1, 0, 2))
    tbl_t = rel_bias_table.T

    wi_l, cw_l = _place_shards(me, [tr(w_in), taps(conv_w)], [BF16, F32], "place_mixer_shards")
    finish_a, token_a = _all_gather_split([wi_l, cw_l], "mixer", None)
    wo_l, wu_l, wd_l, fcw_l = _place_shards(me, [w_out[0], tr(w_up), w_down[0], taps(ffn_conv_w)],
                                            [BF16, BF16, BF16, F32], "place_ffn_shards", after=token_a)
    ffn_stage2, ffn_stage3, token_b = _all_gather_tree([wo_l, wu_l, wd_l, fcw_l], "ffn", token_a)
    wi_g, cw_g = finish_a(token_b)
    w_in_t = wi_g.reshape(IN_WIDTH, D_MODEL)
    conv_w_f = jnp.transpose(cw_g[:, :, 0, :], (1, 0, 2)).reshape(3, CONV_WIDTH)

    proj, u1, ycn, qn, kn, vv = _mix_in_fwd(xs, g1, w_in_t, conv_w_f, gq, gk, gconv)
    token_b2 = ffn_stage2(ycn)
    y_attn, yan, probs, psinks = _attn_fwd(qn, kn, vv, tbl_t, sinks, bkt, gattn, after=token_b2)
    wo_g, wu_g, wd_g, fcw_g = ffn_stage3(yan)
    w_out_f = wo_g.reshape(D_MODEL, D_MODEL)
    w_down_f = wd_g.reshape(N_FFN_BLK, FFN_BLK, D_MODEL)
    w_up_f = wu_g.reshape(2, N_FFN_BLK, FFN_BLK, D_MODEL)
    fcw_f = fcw_g.reshape(2, N_FFN_BLK, 3, 1, FFN_BLK)
    fcb = ffn_conv_b.reshape(2, N_FFN_BLK, 1, FFN_BLK)
    h1, u2, up, pre, act, dh2, dh2b, loss_acc = _ffn_fwd(xs, ycn, yan, w_out_f, g2, w_up_f, fcw_f, fcb, w_down_f, tgt)

    dw_down = _grad_tn_blocked(act, dh2b, "grad_w_down").reshape(N_DEV, D_FF // N_DEV, D_MODEL)
    plan_d, slots_d = _scatter_plan(1)
    d_sem = _split_start("scatter_w_down_start", [dw_down], [lax.empty((N_DEV - 1,) + dw_down.shape[1:], BF16)],
                         plan_d, None, _ALL_FOR_W_DOWN)
    dup, dh1, dh1b, dfb, dfcw, dg2 = _ffn_bwd(dh2, dh2b, h1, g2, up, pre, w_up_f, fcw_f, w_down_f, after=d_sem[4])
    dw_up = _grad_tn_blocked(dup.reshape(N_DEV, SEQ, FFN_BLK), u2, "grad_w_up")
    dw_out = _grad_tn([ycn, yan], dh1b, CONV_WIDTH, "grad_w_out").reshape(N_DEV, D_MODEL // N_DEV, D_MODEL)
    out_bwd = {}

    def behind_ffn(token):
        out_bwd["r"] = _out_bwd(dh1b, w_out_f, y_attn, gattn, after=token)
        return out_bwd["r"][0]

    finish_ffn, token_ffn = _reduce_scatter_split(
        [dw_up, dw_out, dfcw.reshape(N_DEV, 3, 1, FFN_BLK)], "ffn", core, behind_ffn)
    dycn, dy_attn, dgattn = out_bwd["r"]
    dqn, dkn, dv, dtbl_t, dsinks = _attn_bwd(qn, kn, vv, dy_attn, probs, psinks, bkt, after=token_ffn)
    dx, dproj, dcw, dgconv, dgq, dgk, dg1 = _mix_in_bwd(xs, dh1, proj, dycn, dqn, dkn, dv, w_in_t, conv_w_f,
                                                         g1, gq, gk, gconv)
    dw_in_t = _grad_w_in(dproj, u1).reshape(N_DEV, IN_WIDTH // N_DEV, D_MODEL)
    dcw_b = jnp.transpose(dcw.reshape(3, N_DEV, 1, CONV_WIDTH // N_DEV), (1, 0, 2, 3))
    plan_s, slots_s = _broadcast_plan()
    adam = {}
    ffn_got = {}
    small = {}

    def behind_mixer(token):
        packed, packed_all = _pack_small_grads(dg1, dg2, dgconv, dgattn, dfb, dgq, dgk, dsinks, dtbl_t, loss_acc,
                                               after=token)
        small["r"] = _split_start("gather_small_start", [packed], [packed_all], plan_s, None, _ALL_FOR_SMALL)
        ffn_got["r"] = finish_ffn(small["r"][4])
        return ffn_got["r"][1][0]

    finish_mixer, token_mixer = _reduce_scatter_split([dw_in_t, dcw_b], "mixer", core, behind_mixer)
    s_sem, r_sem, src_s, land_s, _ = small["r"]
    (p_wu, p_wo, p_fcw), (r_wu, r_wo, r_fcw) = ffn_got["r"]
    (own_wd,), (r_wd,) = _split_wait("scatter_w_down_wait", d_sem[0], d_sem[1], d_sem[2], d_sem[3], plan_d, slots_d,
                                     token_mixer)
    adam["w_down"] = _adamw_direct(w_down[0], m_w_down[0], v_w_down[0], own_wd, r_wd, me, "adamw_w_down", row_blocks=2)
    adam_up = _adamw(tr(w_up), tr(m_w_up), tr(v_w_up), p_wu, r_wu, chip, "adamw_w_up", row_blocks=4,
                     after=adam["w_down"][0])
    adam["w_out"] = _adamw(w_out[0], m_w_out[0], v_w_out[0], p_wo, r_wo, chip, "adamw_w_out", after=adam_up[0])
    adam_fcw = _adamw(taps(ffn_conv_w), taps(m_ffn_conv_w), taps(v_ffn_conv_w), p_fcw, r_fcw, chip, "adamw_ffn_conv_w",
                      after=adam["w_out"][0])
    _, (r_small,) = _split_wait("gather_small_wait", s_sem, r_sem, src_s, land_s, plan_s, slots_s, adam_fcw[0])
    small_in = {k: (p[k], m[k], v[k]) for k in _SMALL_NAMES}
    small_in["rel_bias_table"] = (tbl_t, m_rel_bias_table.T, v_rel_bias_table.T)
    small_out, loss_row = _adamw_small(r_small, small_in, None)
    (p_wi, p_cw), (r_wi, r_cw) = finish_mixer(loss_row)
    adam_in = _adamw(tr(w_in), tr(m_w_in), tr(v_w_in), p_wi, r_wi, chip, "adamw_w_in")
    adam_cw = _adamw(taps(conv_w), taps(m_conv_w), taps(v_conv_w), p_cw, r_cw, chip, "adamw_conv_w")

    res = {k: tuple(a[None] for a in t) for k, t in adam.items()}
    res["w_up"] = tuple(a.T[None] for a in adam_up)
    res["w_in"] = tuple(a.T[None] for a in adam_in)
    res["ffn_conv_w"] = tuple(taps(a) for a in adam_fcw)
    res["conv_w"] = tuple(taps(a) for a in adam_cw)
    res.update(small_out)
    res["rel_bias_table"] = tuple(a.T for a in small_out["rel_bias_table"])
    loss = loss_row[0, 0]
    order = ("norm_mix_g", "w_in", "conv_w", "q_norm_g", "k_norm_g", "rel_bias_table", "sinks", "out_norm_conv_g",
             "out_norm_attn_g", "w_out", "norm_ffn_g", "w_up", "ffn_conv_w", "ffn_conv_b", "w_down")
    return (loss, dx[None], *[res[k][0] for k in order], *[res[k][1] for k in order],
            *[res[k][2] for k in order], *[res[k][3] for k in order])
```

```python
import math

import numpy as np
import jax
import jax.numpy as jnp
from jax import lax
from jax.experimental import pallas as pl
from jax.experimental.pallas import tpu as pltpu

F32 = jnp.float32
BF16 = jnp.bfloat16

SEQ = 2048
D_MODEL = 1024
CONV_WIDTH = 512
ATTN_WIDTH = 512
KV_WIDTH = 128
HEAD_DIM = 64
N_HEADS = 8
GQA_GROUP = 4
IN_WIDTH = 2304
D_FF = 2816
BLK = 128
NUM_BUCKETS = 32
EPS = 1e-6
NEG_INF = -1e30
ADAM_LR = 0.001
ADAM_B1 = 0.9
ADAM_B2 = 0.999
ADAM_EPS = 1e-08
ADAM_WD = 0.01
ADAM_STEP = 10

N_DEV = 8
FFN_BLK = 2 * D_FF // N_DEV
N_FFN_BLK = D_FF // FFN_BLK
SUBLANES = 8
VMEM_LIMIT = 56 * 1024 * 1024

_MESH = pl.DeviceIdType.MESH
_ANY = pl.BlockSpec(memory_space=pl.ANY)


def _params(sem):
    return pltpu.CompilerParams(dimension_semantics=sem, vmem_limit_bytes=VMEM_LIMIT)


def _ordered_behind(body, pos, after):
    if after is None:
        return body, [], []
    return (lambda *refs: body(*refs[:pos], *refs[pos + 1:])), [_ANY], [after]


def _dot(a, b):
    return jnp.dot(a, b, preferred_element_type=F32)


def _dot_nt(a, b):
    return lax.dot_general(a, b, (((1,), (1,)), ((), ())), preferred_element_type=F32)


def _dot_tn(a, b):
    return lax.dot_general(a, b, (((0,), (0,)), ((), ())), preferred_element_type=F32)


def _shift_down(x, s, halo):
    r = pltpu.roll(x, s, axis=0)
    hr = pltpu.roll(halo, s, axis=0)
    row = lax.broadcasted_iota(jnp.int32, halo.shape, 0)
    top = jnp.where(row < s, hr, r[:SUBLANES])
    return jnp.concatenate([top, r[SUBLANES:]], axis=0)


def _shift_up(x, s, halo):
    n = x.shape[0]
    r = pltpu.roll(x, n - s, axis=0)
    hr = pltpu.roll(halo, SUBLANES - s, axis=0)
    row = lax.broadcasted_iota(jnp.int32, halo.shape, 0)
    bot = jnp.where(row >= SUBLANES - s, hr, r[n - SUBLANES:])
    return jnp.concatenate([r[:n - SUBLANES], bot], axis=0)


def _taps(w):
    return (w[0], w[1], w[2]) if len(w.shape) == 3 else (w[0:1], w[1:2], w[2:3])


def _conv3(x, w, halo):
    x2 = _shift_down(x, 2, halo)
    x1 = _shift_down(x, 1, halo)
    return x2 * w[0] + x1 * w[1] + x * w[2], x2, x1


def _conv3_bwd_input(dy, w, halo_next):
    return dy * w[2] + _shift_up(dy, 1, halo_next) * w[1] + _shift_up(dy, 2, halo_next) * w[0]


def _rstd(x):
    return lax.rsqrt(jnp.mean(x * x, axis=-1, keepdims=True) + EPS)


def _rms_bwd(x, g, dy):
    r = _rstd(x)
    n = x * r
    dn = dy * g
    dx = r * (dn - n * jnp.mean(dn * n, axis=-1, keepdims=True))
    return dx, jnp.sum(dy * n, axis=0, keepdims=True)


def _head_mean(x):
    width = x.shape[-1]
    ri = lax.broadcasted_iota(jnp.int32, (width, width), 0) // HEAD_DIM
    ci = lax.broadcasted_iota(jnp.int32, (width, width), 1) // HEAD_DIM
    ones = jnp.where(ri == ci, 1.0, 0.0).astype(BF16)
    hi = x.astype(BF16)
    lo = (x - hi.astype(F32)).astype(BF16)
    return (_dot(hi, ones) + _dot(lo, ones)) * (1.0 / HEAD_DIM)


def _head_norm(x, g, heads):
    return x * lax.rsqrt(_head_mean(x * x) + EPS) * jnp.tile(g, (1, heads))


def _head_norm_bwd(x, g, dy, heads):
    r = lax.rsqrt(_head_mean(x * x) + EPS)
    n = x * r
    dn = dy * jnp.tile(g, (1, heads))
    dx = r * (dn - n * _head_mean(dn * n))
    per_lane = jnp.sum(dy * n, axis=0, keepdims=True)
    dg = per_lane[:, 0:HEAD_DIM]
    for h in range(1, heads):
        dg = dg + per_lane[:, HEAD_DIM * h:HEAD_DIM * (h + 1)]
    return dx, dg


def _bucket_map():
    q = np.arange(BLK)[:, None]
    j = np.arange(BLK)[None, :]
    n = np.where(j > q, q + BLK - j, q - j)
    nf = np.maximum(n, 1).astype(np.float32)
    max_exact = NUM_BUCKETS // 2
    large = max_exact + (np.log(nf / max_exact) / math.log(BLK / max_exact) * (NUM_BUCKETS - max_exact)).astype(np.int32)
    large = np.minimum(large, NUM_BUCKETS - 1)
    return np.where(n < max_exact, n, large).astype(np.int32)


def _coords():
    return lax.axis_index("x"), lax.axis_index("y"), lax.axis_index("c")


def _lin(px, py, pc):
    return 4 * px + 2 * py + pc


def _chips(x, y):
    return [(1 - x, y), (x, 1 - y), (1 - x, 1 - y)]


def _peers(x, y, c):
    return [(1 - x if r & 4 else x, 1 - y if r & 2 else y, 1 - c if r & 1 else c) for r in range(1, N_DEV)]


_HBM = pl.BlockSpec(memory_space=pltpu.HBM)
_SEM = pl.BlockSpec(memory_space=pltpu.SEMAPHORE)
_EFFECT = pltpu.SideEffectType.DATAFLOW_SIDE_EFFECTING


def _in_hbm(a):
    return pltpu.with_memory_space_constraint(a, pltpu.HBM)


_SIBLING = (1, lambda x, y, c: [(x, y, 1 - c)])
_SIBLING_AND_CHIPS = (2, lambda x, y, c: [(x, y, 1 - c)] + [(cx, cy, c) for cx, cy in _chips(x, y)])
_SIBLING_AND_NEIGHBOURS = (3, lambda x, y, c: [(x, y, 1 - c), (1 - x, y, c), (x, 1 - y, c)])
_ONWARD_AND_SIBLING = (4, lambda x, y, c: [(jnp.where(c == 1, x, 1 - x), jnp.where(c == 1, 1 - y, y), c), (x, y, 1 - c)])
_ALL_FOR_W_DOWN = (5, lambda x, y, c: _peers(x, y, c))
_CHIPS = (6, lambda x, y, c: [(cx, cy, c) for cx, cy in _chips(x, y)])
_ALL_FOR_SMALL = (7, lambda x, y, c: _peers(x, y, c))


def _split_start(name, srcs, lands, plan, after, handshake):
    ns, nl = len(srcs), len(lands)
    n_copies = len(plan(0, 0, 0))
    n_after = 0 if after is None else 1
    collective_id, peers_of = handshake

    def body(*refs):
        src_refs, land_refs = refs[:ns + nl], refs[ns:ns + nl]
        send_sems, recv_sems = refs[ns + nl + n_after], refs[ns + nl + n_after + 1]
        token = refs[-1]
        barrier = pltpu.get_barrier_semaphore()
        peers = peers_of(*_coords())
        for peer in peers:
            pl.semaphore_signal(barrier, inc=1, device_id=peer, device_id_type=_MESH)
        pl.semaphore_wait(barrier, len(peers))
        for k, (a, s_slot, l, d_slot, dev) in enumerate(plan(*_coords())):
            src = src_refs[a] if s_slot is None else src_refs[a].at[s_slot]
            pltpu.make_async_remote_copy(src_ref=src, dst_ref=land_refs[l].at[d_slot], send_sem=send_sems.at[k],
                                         recv_sem=recv_sems.at[k], device_id=dev, device_id_type=_MESH).start()
        token[...] = jnp.zeros_like(token)

    arrs = list(srcs) + list(lands)
    out = pl.pallas_call(
        body, name=name,
        out_shape=(pltpu.SemaphoreType.DMA((n_copies,)), pltpu.SemaphoreType.DMA((n_copies,)),
                   *[pltpu.HBM(a.shape, a.dtype) for a in arrs], jax.ShapeDtypeStruct((SUBLANES, 128), F32)),
        in_specs=[_HBM] * (ns + nl) + [_ANY] * n_after,
        out_specs=(_SEM, _SEM, *[_HBM] * (ns + nl), pl.BlockSpec(memory_space=pltpu.VMEM)),
        input_output_aliases={i: 2 + i for i in range(ns + nl)},
        compiler_params=pltpu.CompilerParams(has_side_effects=_EFFECT, collective_id=collective_id),
    )(*[_in_hbm(a) for a in arrs], *([] if after is None else [after]))
    return out[0], out[1], list(out[2:2 + ns]), list(out[2 + ns:2 + ns + nl]), out[-1]


def _split_wait(name, send_sems, recv_sems, srcs, lands, plan, recv_slots, after):
    ns, nl = len(srcs), len(lands)

    def body(*refs):
        src_refs, land_refs = refs[:ns + nl], refs[ns:ns + nl]
        send_sems, recv_sems = refs[ns + nl], refs[ns + nl + 1]
        coords = _coords()
        slots = recv_slots(*coords)
        for k, (a, s_slot, l, _, dev) in enumerate(plan(*coords)):
            src = src_refs[a] if s_slot is None else src_refs[a].at[s_slot]
            cp = pltpu.make_async_remote_copy(src_ref=src, dst_ref=land_refs[l].at[slots[k]], send_sem=send_sems.at[k],
                                              recv_sem=recv_sems.at[k], device_id=dev, device_id_type=_MESH)
            cp.wait_send()
            cp.wait_recv()

    arrs = list(srcs) + list(lands)
    out = pl.pallas_call(
        body, name=name,
        out_shape=tuple(pltpu.HBM(a.shape, a.dtype) for a in arrs),
        in_specs=[_HBM] * (ns + nl) + [_SEM, _SEM, _ANY],
        out_specs=tuple([_HBM] * (ns + nl)),
        input_output_aliases={i: i for i in range(ns + nl)},
        compiler_params=pltpu.CompilerParams(has_side_effects=_EFFECT),
    )(*arrs, send_sems, recv_sems, after)
    return list(out[:ns]), list(out[ns:])


def _gather_plan_ici(n):
    def plan(x, y, c):
        me = _lin(x, y, c)
        out = []
        for a in range(n):
            out.append((a, me, a, me, (x, y, 1 - c)))
            out += [(a, me, a, me, (cx, cy, c)) for cx, cy in _chips(x, y)]
        return out

    def recv_slots(x, y, c):
        out = []
        for _ in range(n):
            out.append(_lin(x, y, 1 - c))
            out += [_lin(cx, cy, c) for cx, cy in _chips(x, y)]
        return out

    return plan, recv_slots


def _gather_plan_d2d(n):
    def plan(x, y, c):
        return [(a, _lin(cx, cy, c), a, _lin(cx, cy, c), (x, y, 1 - c)) for a in range(n) for cx, cy in _chips(x, y)]

    def recv_slots(x, y, c):
        return [_lin(cx, cy, 1 - c) for _ in range(n) for cx, cy in _chips(x, y)]

    return plan, recv_slots


def _all_gather_split(lands, tag, after):
    n = len(lands)
    plan1, slots1 = _gather_plan_ici(n)
    s1, r1, _, lands, token = _split_start(f"gather_{tag}_ici_start", [], lands, plan1, after, _SIBLING_AND_CHIPS)

    def finish(after):
        _, got = _split_wait(f"gather_{tag}_ici_wait", s1, r1, [], lands, plan1, slots1, after)
        plan2, slots2 = _gather_plan_d2d(n)
        s2, r2, _, got, token2 = _split_start(f"gather_{tag}_d2d_start", [], got, plan2, None, _SIBLING)
        return _split_wait(f"gather_{tag}_d2d_wait", s2, r2, [], got, plan2, slots2, token2)[1]

    return finish, token


def _all_gather_tree(lands, tag, after):
    n = len(lands)

    def plan1(x, y, c):
        me = _lin(x, y, c)
        return [(a, me, a, me, dev) for a in range(n) for dev in ((x, y, 1 - c), (1 - x, y, c), (x, 1 - y, c))]

    def slots1(x, y, c):
        return [s for _ in range(n) for s in (_lin(x, y, 1 - c), _lin(1 - x, y, c), _lin(x, 1 - y, c))]

    def plan2(x, y, c):
        from_x, from_y = _lin(1 - x, y, c), _lin(x, 1 - y, c)
        north = c == 1
        passed = jnp.where(north, from_x, from_y)
        onward = (jnp.where(north, x, 1 - x), jnp.where(north, 1 - y, y), c)
        sib = (x, y, 1 - c)
        return [cp for a in range(n) for cp in ((a, passed, a, passed, onward), (a, from_x, a, from_x, sib),
                                                (a, from_y, a, from_y, sib))]

    def slots2(x, y, c):
        return [s for _ in range(n) for s in (_lin(1 - x, 1 - y, c), _lin(1 - x, y, 1 - c), _lin(x, 1 - y, 1 - c))]

    def plan3(x, y, c):
        diag = _lin(1 - x, 1 - y, c)
        return [(a, diag, a, diag, (x, y, 1 - c)) for a in range(n)]

    def slots3(x, y, c):
        return [_lin(1 - x, 1 - y, 1 - c)] * n

    s1, r1, _, lands, token = _split_start(f"gather_{tag}_1_start", [], lands, plan1, after, _SIBLING_AND_NEIGHBOURS)
    state = {}

    def stage2(after):
        _, got = _split_wait(f"gather_{tag}_1_wait", s1, r1, [], lands, plan1, slots1, after)
        state["s"], state["r"], _, state["lands"], token2 = _split_start(f"gather_{tag}_2_start", [], got, plan2, None,
                                                                         _ONWARD_AND_SIBLING)
        return token2

    def stage3(after):
        _, got = _split_wait(f"gather_{tag}_2_wait", state["s"], state["r"], [], state["lands"], plan2, slots2, after)
        s3, r3, _, got, token3 = _split_start(f"gather_{tag}_3_start", [], got, plan3, None, _SIBLING)
        return _split_wait(f"gather_{tag}_3_wait", s3, r3, [], got, plan3, slots3, token3)[1]

    return stage2, stage3, token


_CHIP_LIST = ((0, 0), (0, 1), (1, 0), (1, 1))


def _reduce_plan_d2d(n):
    def plan(x, y, c):
        return [(a, _lin(qx, qy, 1 - c), a, q, (x, y, 1 - c)) for a in range(n) for q, (qx, qy) in enumerate(_CHIP_LIST)]

    def recv_slots(x, y, c):
        return [q for _ in range(n) for q in range(4)]

    return plan, recv_slots


def _reduce_plan_ici(n):
    def plan(x, y, c):
        return [(a, 2 * cx + cy, a, j, (cx, cy, c)) for a in range(n) for j, (cx, cy) in enumerate(_chips(x, y))]

    def recv_slots(x, y, c):
        return [j for _ in range(n) for j in range(3)]

    return plan, recv_slots


def _scatter_plan(n):
    def plan(x, y, c):
        return [(a, _lin(*peer), a, r, peer) for a in range(n) for r, peer in enumerate(_peers(x, y, c))]

    def recv_slots(x, y, c):
        return [r for _ in range(n) for r in range(N_DEV - 1)]

    return plan, recv_slots


def _broadcast_plan():
    def plan(x, y, c):
        return [(0, None, 0, _lin(x, y, c), peer) for peer in _peers(x, y, c)]

    def recv_slots(x, y, c):
        return [_lin(*peer) for peer in _peers(x, y, c)]

    return plan, recv_slots


def _chip_partial(grads, recvd, name):
    n = len(grads)
    x, y, c = _coords()
    ids = jnp.stack([c] + [2 * cx + cy for cx, cy in _chips(x, y)]).astype(jnp.int32)

    def body(ids_ref, *refs):
        for a in range(n):
            g_ref, r_ref, o_ref = refs[a], refs[n + a], refs[2 * n + a]
            o_ref[...] = (g_ref[...].astype(F32) + r_ref[...].astype(F32)).astype(o_ref.dtype)

    def blk(a, own):
        zeros = (0,) * (a.ndim - 1)
        return pl.BlockSpec((None,) + a.shape[1:],
                            (lambda q, ids_ref: (2 * ids_ref[1 + q] + ids_ref[0],) + zeros) if own
                            else (lambda q, ids_ref: (ids_ref[1 + q],) + zeros))

    return pl.pallas_call(
        body, name=name,
        grid_spec=pltpu.PrefetchScalarGridSpec(
            num_scalar_prefetch=1, grid=(3,),
            in_specs=[blk(a, True) for a in grads] + [blk(a, False) for a in recvd],
            out_specs=[blk(a, False) for a in recvd]),
        out_shape=[jax.ShapeDtypeStruct(a.shape, a.dtype) for a in recvd],
        compiler_params=_params(("arbitrary",)),
    )(ids, *grads, *recvd)


def _reduce_scatter_split(grads, tag, behind):
    n = len(grads)
    plan1, slots1 = _reduce_plan_d2d(n)
    lands1 = [lax.empty((4,) + a.shape[1:], a.dtype) for a in grads]
    s1, r1, srcs1, lands1, token1 = _split_start(f"reduce_{tag}_d2d_start", grads, lands1, plan1, None, _SIBLING)
    own, got = _split_wait(f"reduce_{tag}_d2d_wait", s1, r1, srcs1, lands1, plan1, slots1, behind(token1))
    parts = _chip_partial(own, got, f"reduce_{tag}_partial")
    plan2, slots2 = _reduce_plan_ici(n)
    lands2 = [lax.empty((3,) + a.shape[1:], a.dtype) for a in grads]
    s2, r2, srcs2, lands2, token2 = _split_start(f"reduce_{tag}_ici_start", parts, lands2, plan2, None, _CHIPS)

    def finish(after):
        return list(zip(own, got)), _split_wait(f"reduce_{tag}_ici_wait", s2, r2, srcs2, lands2, plan2, slots2, after)[1]

    return finish, token2


def _place_shards(me, shards, dtypes, name, after=None):
    n = len(shards)

    def body(me_ref, *refs):
        for a in range(n):
            refs[n + a][...] = refs[a][...].astype(dtypes[a])

    full = lambda s: pl.BlockSpec(s.shape, lambda i, me_ref: (0,) * s.ndim)
    slot = lambda s: pl.BlockSpec((None,) + s.shape, lambda i, me_ref: (me_ref[0],) + (0,) * s.ndim)
    body, more_specs, more = _ordered_behind(body, 1 + n, after)
    return pl.pallas_call(
        body, name=name,
        grid_spec=pltpu.PrefetchScalarGridSpec(num_scalar_prefetch=1, grid=(1,),
                                               in_specs=[full(s) for s in shards] + more_specs,
                                               out_specs=[slot(s) for s in shards]),
        out_shape=[jax.ShapeDtypeStruct((N_DEV,) + s.shape, d) for s, d in zip(shards, dtypes)],
        compiler_params=_params(("arbitrary",)),
    )(me, *shards, *more)


def _mix_in_fwd(x, g1, w_in_t, conv_w, gq, gk, gconv):
    tm = 512
    n_t = SEQ // tm

    def body(x_ref, g1_ref, w_ref, cw_ref, gq_ref, gk_ref, gc_ref,
             proj_ref, u1_ref, ycn_ref, qn_ref, kn_ref, v_ref, halo_ref):
        @pl.when(pl.program_id(0) == 0)
        def _():
            halo_ref[...] = jnp.zeros_like(halo_ref)

        xv = x_ref[...]
        u = (xv * _rstd(xv) * g1_ref[...]).astype(BF16)
        u1_ref[...] = u
        proj = _dot_nt(u, w_ref[...])
        proj_ref[...] = proj
        gate_b = proj[:, 0:CONV_WIDTH]
        a = proj[:, CONV_WIDTH:2 * CONV_WIDTH] * proj[:, 2 * CONV_WIDTH:3 * CONV_WIDTH]
        cv, _, _ = _conv3(a, _taps(cw_ref[...]), halo_ref[...])
        halo_ref[...] = a[tm - SUBLANES:]
        yc = gate_b * cv
        ycn_ref[...] = (yc * _rstd(yc) * gc_ref[...]).astype(BF16)
        q0 = 3 * CONV_WIDTH
        qn_ref[...] = _head_norm(proj[:, q0:q0 + ATTN_WIDTH], gq_ref[...], N_HEADS).astype(BF16)
        k0 = q0 + ATTN_WIDTH
        kn_ref[...] = _head_norm(proj[:, k0:k0 + KV_WIDTH], gk_ref[...], 2).astype(BF16)
        v_ref[...] = proj[:, k0 + KV_WIDTH:k0 + 2 * KV_WIDTH].astype(BF16)

    const = lambda shape: pl.BlockSpec(shape, lambda i: (0,) * len(shape))
    rows = lambda w: pl.BlockSpec((tm, w), lambda i: (i, 0))
    return pl.pallas_call(
        body, name="mix_in_fwd", grid=(n_t,),
        in_specs=[rows(D_MODEL), const((1, D_MODEL)), const((IN_WIDTH, D_MODEL)), const((3, CONV_WIDTH)),
                  const((1, HEAD_DIM)), const((1, HEAD_DIM)), const((1, CONV_WIDTH))],
        out_specs=[rows(IN_WIDTH), rows(D_MODEL), rows(CONV_WIDTH), rows(ATTN_WIDTH), rows(KV_WIDTH), rows(KV_WIDTH)],
        out_shape=[jax.ShapeDtypeStruct((SEQ, IN_WIDTH), F32), jax.ShapeDtypeStruct((SEQ, D_MODEL), BF16),
                   jax.ShapeDtypeStruct((SEQ, CONV_WIDTH), BF16),
                   jax.ShapeDtypeStruct((SEQ, ATTN_WIDTH), BF16), jax.ShapeDtypeStruct((SEQ, KV_WIDTH), BF16),
                   jax.ShapeDtypeStruct((SEQ, KV_WIDTH), BF16)],
        scratch_shapes=[pltpu.VMEM((SUBLANES, CONV_WIDTH), F32)],
        compiler_params=_params(("arbitrary",)),
    )(x, g1, w_in_t, conv_w, gq, gk, gconv)


GROUP_ROWS = GQA_GROUP * BLK
QUERY_BLOCKS_PER_STEP = 2


def _band_bias(tbl_ref, bkt, bias_ref):
    for h in range(N_HEADS):
        acc = jnp.zeros(bkt.shape, F32)
        for b in range(NUM_BUCKETS):
            acc = jnp.where(bkt == b, tbl_ref[h, b], acc)
        bias_ref[h // GQA_GROUP, BLK * (h % GQA_GROUP):BLK * (h % GQA_GROUP + 1), :] = acc


def _band_masks(i):
    qi = lax.broadcasted_iota(jnp.int32, (GROUP_ROWS, BLK), 0) & (BLK - 1)
    ji = lax.broadcasted_iota(jnp.int32, (GROUP_ROWS, BLK), 1)
    upper = ji > qi
    return upper, upper & (i == 0)


def _stack_heads(x, g):
    return jnp.concatenate([x[:, HEAD_DIM * h:HEAD_DIM * (h + 1)] for h in range(GQA_GROUP * g, GQA_GROUP * (g + 1))], axis=0)


def _unstack_heads(groups):
    return jnp.concatenate([p[BLK * t:BLK * (t + 1)] for p in groups for t in range(GQA_GROUP)], axis=-1)


def _per_head_rows(vals):
    row = lax.broadcasted_iota(jnp.int32, (GROUP_ROWS, 1), 0)
    col = jnp.full((GROUP_ROWS, 1), vals[GQA_GROUP - 1], F32)
    for t in range(GQA_GROUP - 2, -1, -1):
        col = jnp.where(row < BLK * (t + 1), vals[t], col)
    return col


def _band_rows(ref, i):
    prev = pl.multiple_of(jnp.maximum(i - 1, 0) * BLK, BLK)
    cur = pl.multiple_of(i * BLK, BLK)
    return jnp.concatenate([ref[pl.ds(prev, BLK), :], ref[pl.ds(cur, BLK), :]], axis=0), prev, cur


def _fold(band, upper):
    return jnp.where(upper, band[:, :BLK], band[:, BLK:])


def _unfold(tile, upper):
    return jnp.concatenate([jnp.where(upper, tile, 0.0), jnp.where(upper, 0.0, tile)], axis=1)


def _head_probs(qh, kh, bias, upper, dead, sink):
    logits = _fold(_dot_nt(qh, kh), upper) * (HEAD_DIM ** -0.5) + bias
    logits = jnp.where(dead, NEG_INF, logits)
    m = jnp.maximum(jnp.max(logits, axis=-1, keepdims=True), sink)
    p = jnp.exp(logits - m)
    es = jnp.exp(sink - m)
    den = jnp.sum(p, axis=-1, keepdims=True) + es
    return p / den, es / den


def _attn_fwd(qn, kn, v, tbl, sinks, bkt, gattn, after=None):
    n_b = SEQ // BLK

    def body(q_ref, k_ref, v_ref, tbl_ref, sink_ref, bkt_ref, ga_ref, y_ref, yn_ref, p_ref, ps_ref, bias_ref):
        step = pl.program_id(0)

        @pl.when(step == 0)
        def _():
            _band_bias(tbl_ref, bkt_ref[...], bias_ref)

        lane = lax.broadcasted_iota(jnp.int32, (BLK, 128), 1)
        for b in range(QUERY_BLOCKS_PER_STEP):
            i = QUERY_BLOCKS_PER_STEP * step + b
            rows = slice(BLK * b, BLK * (b + 1))
            kb, _, _ = _band_rows(k_ref, i)
            vb, _, _ = _band_rows(v_ref, i)
            upper, dead = _band_masks(i)
            q = q_ref[rows, :]
            outs = []
            psinks = jnp.zeros((BLK, 128), F32)
            for g in range(N_HEADS // GQA_GROUP):
                kv = slice(HEAD_DIM * g, HEAD_DIM * (g + 1))
                sink = _per_head_rows([sink_ref[0, GQA_GROUP * g + t] for t in range(GQA_GROUP)])
                probs, psink = _head_probs(_stack_heads(q, g), kb[:, kv], bias_ref[g], upper, dead, sink)
                p_ref[b, g] = probs.astype(BF16)
                for t in range(GQA_GROUP):
                    psinks = jnp.where(lane == GQA_GROUP * g + t, psink[BLK * t:BLK * (t + 1)], psinks)
                outs.append(_dot(_unfold(probs, upper).astype(BF16), vb[:, kv]))
            ps_ref[rows, :] = psinks
            y = _unstack_heads(outs)
            y_ref[rows, :] = y
            yn_ref[rows, :] = (y * _rstd(y) * ga_ref[...]).astype(BF16)

    const = lambda shape: pl.BlockSpec(shape, lambda i: (0,) * len(shape))
    rows = lambda w: pl.BlockSpec((QUERY_BLOCKS_PER_STEP * BLK, w), lambda i: (i, 0))
    smem = pl.BlockSpec(memory_space=pltpu.SMEM)
    body, more_specs, more = _ordered_behind(body, 7, after)
    return pl.pallas_call(
        body, name="attn_fwd", grid=(n_b // QUERY_BLOCKS_PER_STEP,),
        in_specs=[rows(ATTN_WIDTH), const((SEQ, KV_WIDTH)), const((SEQ, KV_WIDTH)), smem, smem,
                  const((BLK, BLK)), const((1, ATTN_WIDTH))] + more_specs,
        out_specs=[rows(ATTN_WIDTH), rows(ATTN_WIDTH),
                   pl.BlockSpec((QUERY_BLOCKS_PER_STEP, N_HEADS // GQA_GROUP, GROUP_ROWS, BLK), lambda i: (i, 0, 0, 0)),
                   rows(128)],
        out_shape=[jax.ShapeDtypeStruct((SEQ, ATTN_WIDTH), F32), jax.ShapeDtypeStruct((SEQ, ATTN_WIDTH), BF16),
                   jax.ShapeDtypeStruct((n_b, N_HEADS // GQA_GROUP, GROUP_ROWS, BLK), BF16),
                   jax.ShapeDtypeStruct((SEQ, 128), F32)],
        scratch_shapes=[pltpu.VMEM((N_HEADS // GQA_GROUP, GROUP_ROWS, BLK), F32)],
        compiler_params=_params(("arbitrary",)),
    )(qn, kn, v, tbl, sinks, bkt, gattn, *more)


def _ffn_block(i, step):
    return jnp.where(i % 2 == 0, step, N_FFN_BLK - 1 - step)


def _ffn_fwd(x, ycn, yan, w_out, g2, w_up, fcw, fcb, w_down, tgt):
    tm = 512
    n_t = SEQ // tm

    def body(x_ref, ycn_ref, yan_ref, wo_ref, g2_ref, wu_ref, cw_ref, b_ref, wd_ref, tgt_ref,
             h1_ref, u2_ref, up_ref, pre_ref, act_ref, dh2_ref, dh2b_ref, loss_ref, acc_ref, halo_ref):
        i, step = pl.program_id(0), pl.program_id(1)
        j = _ffn_block(i, step)

        @pl.when((i == 0) & (step == 0))
        def _():
            loss_ref[...] = jnp.zeros_like(loss_ref)

        @pl.when(step == 0)
        def _():
            h1 = x_ref[...] + _dot(ycn_ref[...], wo_ref[0:CONV_WIDTH, :]) + _dot(yan_ref[...], wo_ref[CONV_WIDTH:, :])
            h1_ref[...] = h1
            u2_ref[...] = (h1 * _rstd(h1) * g2_ref[...]).astype(BF16)
            acc_ref[...] = jnp.zeros_like(acc_ref)

        u2 = u2_ref[...]
        pre = []
        for s in range(2):
            up = _dot_nt(u2, wu_ref[s])
            up_ref[s] = up.astype(BF16)
            halo = jnp.where(i == 0, 0.0, halo_ref[s, j])
            pre.append(_conv3(up, _taps(cw_ref.at[s]), halo)[0] + b_ref[s])
            pre_ref[s] = pre[s].astype(BF16)
            halo_ref[s, j] = up[tm - SUBLANES:]
        g, val = pre
        act = (g * jax.nn.sigmoid(g) * val).astype(BF16)
        act_ref[...] = act
        acc_ref[...] += _dot(act, wd_ref[...])

        @pl.when(step == N_FFN_BLK - 1)
        def _():
            err = h1_ref[...] + acc_ref[...] - tgt_ref[...]
            loss_ref[...] += 0.5 * jnp.sum(err * err) / D_MODEL
            dh2 = err / D_MODEL
            dh2_ref[...] = dh2
            dh2b_ref[...] = dh2.astype(BF16)

    rows = lambda w: pl.BlockSpec((tm, w), lambda i, step: (i, 0))
    const = lambda shape: pl.BlockSpec(shape, lambda i, step: (0,) * len(shape))
    pair = lambda *s: pl.BlockSpec((2, None) + s, lambda i, step: (0, _ffn_block(i, step)) + (0,) * len(s))
    upb = pl.BlockSpec((2, None, tm, FFN_BLK), lambda i, step: (0, _ffn_block(i, step), i, 0))
    return pl.pallas_call(
        body, name="ffn_fwd", grid=(n_t, N_FFN_BLK),
        in_specs=[rows(D_MODEL), rows(CONV_WIDTH), rows(ATTN_WIDTH), const((D_MODEL, D_MODEL)), const((1, D_MODEL)),
                  pair(FFN_BLK, D_MODEL), pair(3, 1, FFN_BLK), pair(1, FFN_BLK),
                  pl.BlockSpec((None, FFN_BLK, D_MODEL), lambda i, step: (_ffn_block(i, step), 0, 0)), rows(D_MODEL)],
        out_specs=[rows(D_MODEL), rows(D_MODEL), upb, upb,
                   pl.BlockSpec((None, tm, FFN_BLK), lambda i, step: (_ffn_block(i, step), i, 0)),
                   rows(D_MODEL), rows(D_MODEL), const((SUBLANES, 128))],
        out_shape=[jax.ShapeDtypeStruct((SEQ, D_MODEL), F32), jax.ShapeDtypeStruct((SEQ, D_MODEL), BF16),
                   jax.ShapeDtypeStruct((2, N_FFN_BLK, SEQ, FFN_BLK), BF16),
                   jax.ShapeDtypeStruct((2, N_FFN_BLK, SEQ, FFN_BLK), BF16),
                   jax.ShapeDtypeStruct((N_FFN_BLK, SEQ, FFN_BLK), BF16),
                   jax.ShapeDtypeStruct((SEQ, D_MODEL), F32), jax.ShapeDtypeStruct((SEQ, D_MODEL), BF16),
                   jax.ShapeDtypeStruct((SUBLANES, 128), F32)],
        scratch_shapes=[pltpu.VMEM((tm, D_MODEL), F32), pltpu.VMEM((2, N_FFN_BLK, SUBLANES, FFN_BLK), F32)],
        compiler_params=_params(("arbitrary", "arbitrary")),
    )(x, ycn, yan, w_out, g2, w_up, fcw, fcb, w_down, tgt)


def _ffn_bwd(dh2, dh2b, h1, g2, up, pre, w_up, fcw, w_down, after=None):
    tm = 512
    units = ((288, 224), (0, 288))
    n_t = SEQ // tm

    def body(dh2_ref, dh2b_ref, h1_ref, g2_ref, up_ref, pre_ref, wu_ref, cw_ref, wd_ref,
             dup_ref, dh1_ref, dh1b_ref, dfb_ref, dfcw_ref, dg2_ref, acc_ref, next_ref):
        j, i = pl.program_id(0), pl.program_id(1)
        tile = pl.ds(pl.multiple_of((n_t - 1 - i) * tm, tm), tm)

        @pl.when((j == 0) & (i == 0))
        def _():
            dfb_ref[...] = jnp.zeros_like(dfb_ref)
            dfcw_ref[...] = jnp.zeros_like(dfcw_ref)
            dg2_ref[...] = jnp.zeros_like(dg2_ref)

        @pl.when(j == 0)
        def _():
            acc_ref[tile, :] = jnp.zeros((tm, D_MODEL), F32)

        nxt = [jnp.where(i == 0, 0.0, next_ref[s]) for s in range(2)]
        sums = [[0.0] * 4 for _ in range(2)]
        for r0, rn in units:
            rows = slice(r0, r0 + rn)
            g, val = pre_ref[0, rows, :].astype(F32), pre_ref[1, rows, :].astype(F32)
            sg = jax.nn.sigmoid(g)
            silu = g * sg
            dact = _dot_nt(dh2b_ref[rows, :], wd_ref[...])
            dpre = (dact * val * (sg * (1.0 + g * (1.0 - sg))), dact * silu)
            dups = []
            for s in range(2):
                d = dpre[s]
                u = up_ref[s, rows, :].astype(F32)
                w = _taps(cw_ref.at[s])
                d1 = _shift_up(d, 1, nxt[s])
                d2 = _shift_up(d, 2, nxt[s])
                nxt[s] = d[:SUBLANES]
                for t, term in enumerate((d, d2 * u, d1 * u, d * u)):
                    sums[s][t] = sums[s][t] + jnp.sum(term, axis=0, keepdims=True)
                dups.append((d * w[2] + d1 * w[1] + d2 * w[0]).astype(BF16))
                dup_ref[s, rows, :] = dups[s]
            acc_rows = pl.ds(pl.multiple_of((n_t - 1 - i) * tm + r0, SUBLANES), rn)
            acc_ref[acc_rows, :] += _dot(dups[0], wu_ref[0]) + _dot(dups[1], wu_ref[1])
        for s in range(2):
            next_ref[s] = nxt[s]
            dfb_ref[s, j] += sums[s][0]
            for t in range(3):
                dfcw_ref[s, j, t] += sums[s][1 + t]

        @pl.when(j == N_FFN_BLK - 1)
        def _():
            dn, dgain = _rms_bwd(h1_ref[...], g2_ref[...], acc_ref[tile, :])
            dh1 = dh2_ref[...] + dn
            dh1_ref[...] = dh1
            dh1b_ref[...] = dh1.astype(BF16)
            dg2_ref[...] += dgain

    rev = lambda i: n_t - 1 - i
    rows = lambda w: pl.BlockSpec((tm, w), lambda j, i: (rev(i), 0))
    last_rows = lambda w: pl.BlockSpec((tm, w), lambda j, i: (jnp.where(j == N_FFN_BLK - 1, rev(i), rev(0)), 0))
    const = lambda shape: pl.BlockSpec(shape, lambda j, i: (0,) * len(shape))
    pair = lambda *s: pl.BlockSpec((2, None) + s, lambda j, i: (0, j) + (0,) * len(s))
    upb = pl.BlockSpec((2, None, tm, FFN_BLK), lambda j, i: (0, j, rev(i), 0))
    body, more_specs, more = _ordered_behind(body, 9, after)
    return pl.pallas_call(
        body, name="ffn_bwd", grid=(N_FFN_BLK, n_t),
        in_specs=[last_rows(D_MODEL), rows(D_MODEL), last_rows(D_MODEL), const((1, D_MODEL)), upb, upb,
                  pair(FFN_BLK, D_MODEL), pair(3, 1, FFN_BLK),
                  pl.BlockSpec((None, FFN_BLK, D_MODEL), lambda j, i: (j, 0, 0))] + more_specs,
        out_specs=[upb, last_rows(D_MODEL), last_rows(D_MODEL),
                   const((2, N_FFN_BLK, 1, FFN_BLK)), const((2, N_FFN_BLK, 3, 1, FFN_BLK)), const((1, D_MODEL))],
        out_shape=[jax.ShapeDtypeStruct((2, N_FFN_BLK, SEQ, FFN_BLK), BF16), jax.ShapeDtypeStruct((SEQ, D_MODEL), F32),
                   jax.ShapeDtypeStruct((SEQ, D_MODEL), BF16), jax.ShapeDtypeStruct((2, N_FFN_BLK, 1, FFN_BLK), F32),
                   jax.ShapeDtypeStruct((2, N_FFN_BLK, 3, 1, FFN_BLK), F32), jax.ShapeDtypeStruct((1, D_MODEL), F32)],
        scratch_shapes=[pltpu.VMEM((SEQ, D_MODEL), F32), pltpu.VMEM((2, SUBLANES, FFN_BLK), F32)],
        compiler_params=_params(("arbitrary", "arbitrary")),
    )(dh2, dh2b, h1, g2, up, pre, w_up, fcw, w_down, *more)


def _grad_tn(a_list, b, out_rows, name, after=None):
    n = len(a_list)
    ncol = b.shape[1]

    def body(*refs):
        a_refs, b_ref, o_ref = refs[:n], refs[n], refs[n + 1]
        j = pl.program_id(0)
        for k in range(n):
            @pl.when(j == k)
            def _(k=k):
                o_ref[...] = _dot_tn(a_refs[k][...], b_ref[...]).astype(BF16)

    full = lambda shape: pl.BlockSpec(shape, lambda j: (0,) * len(shape))
    body, more_specs, more = _ordered_behind(body, n + 1, after)
    return pl.pallas_call(
        body, name=name, grid=(n,),
        in_specs=[full((SEQ, out_rows))] * n + [full((SEQ, ncol))] + more_specs,
        out_specs=pl.BlockSpec((None, out_rows, ncol), lambda j: (j, 0, 0)),
        out_shape=jax.ShapeDtypeStruct((n, out_rows, ncol), BF16),
        compiler_params=_params(("arbitrary",)),
    )(*a_list, b, *more)


def _grad_tn_blocked(a, b, name, per_step=2):
    nb, _, a_w = a.shape
    b_w = b.shape[-1]

    def body(a_ref, b_ref, o_ref):
        for p in range(per_step):
            o_ref[p] = _dot_tn(a_ref[p], b_ref[...]).astype(BF16)

    return pl.pallas_call(
        body, name=name, grid=(nb // per_step,),
        in_specs=[pl.BlockSpec((per_step, SEQ, a_w), lambda k: (k, 0, 0)), pl.BlockSpec((SEQ, b_w), lambda k: (0, 0))],
        out_specs=pl.BlockSpec((per_step, a_w, b_w), lambda k: (k, 0, 0)),
        out_shape=jax.ShapeDtypeStruct((nb, a_w, b_w), BF16),
        compiler_params=_params(("arbitrary",)),
    )(a, b)


def _out_bwd(dh1b, w_out, y_attn, gattn, after=None):
    tm = 1024
    n_t = SEQ // tm

    def body(dh_ref, wo_ref, y_ref, ga_ref, dycn_ref, dy_ref, dga_ref):
        @pl.when(pl.program_id(0) == 0)
        def _():
            dga_ref[...] = jnp.zeros_like(dga_ref)

        dycat = _dot_nt(dh_ref[...], wo_ref[...])
        dycn_ref[...] = dycat[:, :CONV_WIDTH]
        dy, dga = _rms_bwd(y_ref[...], ga_ref[...], dycat[:, CONV_WIDTH:])
        dy_ref[...] = dy
        dga_ref[...] += dga

    rows = lambda w: pl.BlockSpec((tm, w), lambda i: (i, 0))
    const = lambda shape: pl.BlockSpec(shape, lambda i: (0,) * len(shape))
    body, more_specs, more = _ordered_behind(body, 4, after)
    return pl.pallas_call(
        body, name="out_bwd", grid=(n_t,),
        in_specs=[rows(D_MODEL), const((D_MODEL, D_MODEL)), rows(ATTN_WIDTH), const((1, ATTN_WIDTH))] + more_specs,
        out_specs=[rows(CONV_WIDTH), rows(ATTN_WIDTH), const((1, ATTN_WIDTH))],
        out_shape=[jax.ShapeDtypeStruct((SEQ, CONV_WIDTH), F32), jax.ShapeDtypeStruct((SEQ, ATTN_WIDTH), F32),
                   jax.ShapeDtypeStruct((1, ATTN_WIDTH), F32)],
        compiler_params=_params(("arbitrary",)),
    )(dh1b, w_out, y_attn, gattn, *more)


def _attn_bwd(qn, kn, v, dy, probs, psinks, bkt, after=None):
    n_b = SEQ // BLK

    def body(q_ref, k_ref, v_ref, dy_ref, p_ref, ps_ref, bkt_ref,
             dq_ref, dk_ref, dv_ref, dtbl_ref, dsink_ref, dbias_ref, dsacc_ref):
        step = pl.program_id(0)

        @pl.when(step == 0)
        def _():
            dbias_ref[...] = jnp.zeros_like(dbias_ref)
            dsacc_ref[...] = jnp.zeros_like(dsacc_ref)
            dk_ref[...] = jnp.zeros_like(dk_ref)
            dv_ref[...] = jnp.zeros_like(dv_ref)

        lane = lax.broadcasted_iota(jnp.int32, (BLK, 128), 1)
        for blk in range(QUERY_BLOCKS_PER_STEP):
            i = QUERY_BLOCKS_PER_STEP * step + blk
            rows = slice(BLK * blk, BLK * (blk + 1))
            kb, prev, cur = _band_rows(k_ref, i)
            vb, _, _ = _band_rows(v_ref, i)
            upper, _ = _band_masks(i)
            q = q_ref[rows, :]
            dy = dy_ref[rows, :]
            psink = ps_ref[rows, :]
            dsink = jnp.zeros((BLK, 128), F32)
            dqs, dks, dvs = [], [], []
            for g in range(N_HEADS // GQA_GROUP):
                kv = slice(HEAD_DIM * g, HEAD_DIM * (g + 1))
                qg = _stack_heads(q, g)
                dog = _stack_heads(dy, g).astype(BF16)
                pb = p_ref[blk, g]
                pg = pb.astype(F32)
                dprobs = _fold(_dot_nt(dog, vb[:, kv]), upper)
                dvs.append(_dot_tn(_unfold(pb, upper), dog))
                dsum = jnp.sum(pg * dprobs, axis=-1, keepdims=True)
                dlogits = pg * (dprobs - dsum)
                for t in range(GQA_GROUP):
                    dsink = jnp.where(lane == GQA_GROUP * g + t, -psink * dsum[BLK * t:BLK * (t + 1)], dsink)
                dbias_ref[g] += dlogits
                ds = _unfold(dlogits * (HEAD_DIM ** -0.5), upper).astype(BF16)
                dqs.append(_dot(ds, kb[:, kv]))
                dks.append(_dot_tn(ds, qg))
            dsacc_ref[...] += dsink
            dq_ref[rows, :] = _unstack_heads(dqs)
            dkb = jnp.concatenate(dks, axis=-1)
            dvb = jnp.concatenate(dvs, axis=-1)
            dk_ref[pl.ds(prev, BLK), :] += dkb[:BLK]
            dk_ref[pl.ds(cur, BLK), :] += dkb[BLK:]
            dv_ref[pl.ds(prev, BLK), :] += dvb[:BLK]
            dv_ref[pl.ds(cur, BLK), :] += dvb[BLK:]

        @pl.when(step == n_b // QUERY_BLOCKS_PER_STEP - 1)
        def _():
            bkt = bkt_ref[...]
            row8 = lax.broadcasted_iota(jnp.int32, (N_HEADS, 128), 0)
            lane8 = lax.broadcasted_iota(jnp.int32, (N_HEADS, 128), 1)
            acc = jnp.zeros((N_HEADS, 128), F32)
            for h in range(N_HEADS):
                rows = slice(BLK * (h % GQA_GROUP), BLK * (h % GQA_GROUP + 1))
                dbh = dbias_ref[h // GQA_GROUP, rows, :]
                for b in range(NUM_BUCKETS):
                    acc = jnp.where((row8 == h) & (lane8 == b), jnp.sum(jnp.where(bkt == b, dbh, 0.0)), acc)
            dsink_ref[...] = jnp.sum(dsacc_ref[...], axis=0, keepdims=True)
            dtbl_ref[...] = acc

    const = lambda shape: pl.BlockSpec(shape, lambda i: (0,) * len(shape))
    rows = lambda w: pl.BlockSpec((QUERY_BLOCKS_PER_STEP * BLK, w), lambda i: (i, 0))
    n_g = N_HEADS // GQA_GROUP
    body, more_specs, more = _ordered_behind(body, 7, after)
    return pl.pallas_call(
        body, name="attn_bwd", grid=(n_b // QUERY_BLOCKS_PER_STEP,),
        in_specs=[rows(ATTN_WIDTH), const((SEQ, KV_WIDTH)), const((SEQ, KV_WIDTH)), rows(ATTN_WIDTH),
                  pl.BlockSpec((QUERY_BLOCKS_PER_STEP, n_g, GROUP_ROWS, BLK), lambda i: (i, 0, 0, 0)), rows(128),
                  const((BLK, BLK))] + more_specs,
        out_specs=[rows(ATTN_WIDTH), const((SEQ, KV_WIDTH)), const((SEQ, KV_WIDTH)), const((N_HEADS, 128)), const((1, 128))],
        out_shape=[jax.ShapeDtypeStruct((SEQ, ATTN_WIDTH), F32), jax.ShapeDtypeStruct((SEQ, KV_WIDTH), F32),
                   jax.ShapeDtypeStruct((SEQ, KV_WIDTH), F32), jax.ShapeDtypeStruct((N_HEADS, 128), F32),
                   jax.ShapeDtypeStruct((1, 128), F32)],
        scratch_shapes=[pltpu.VMEM((n_g, GROUP_ROWS, BLK), F32), pltpu.VMEM((BLK, 128), F32)],
        compiler_params=_params(("arbitrary",)),
    )(qn, kn, v, dy, probs, psinks, bkt, *more)


def _mix_in_bwd(x, dh1, proj, dycn, dqn, dkn, dv, w_in_t, conv_w, g1, gq, gk, gconv):
    tm = 512
    n_t = SEQ // tm
    halo_blocks = tm // SUBLANES

    def body(x_ref, dh1_ref, proj_ref, halo_ref, dycn_ref, dqn_ref, dkn_ref, dv_ref, w_ref, cw_ref,
             g1_ref, gq_ref, gk_ref, gc_ref,
             dx_ref, dproj_ref, dcw_ref, dgc_ref, dgq_ref, dgk_ref, dg1_ref, next_ref):
        i = pl.program_id(0)
        first_tile = i == n_t - 1

        @pl.when(i == 0)
        def _():
            for r in (dcw_ref, dgc_ref, dgq_ref, dgk_ref, dg1_ref, next_ref):
                r[...] = jnp.zeros_like(r)

        proj = proj_ref[...]
        hp = halo_ref[...]
        gate_b = proj[:, 0:CONV_WIDTH]
        gate_c = proj[:, CONV_WIDTH:2 * CONV_WIDTH]
        hc = proj[:, 2 * CONV_WIDTH:3 * CONV_WIDTH]
        a = gate_c * hc
        a_halo = jnp.where(first_tile, 0.0, hp[:, CONV_WIDTH:2 * CONV_WIDTH] * hp[:, 2 * CONV_WIDTH:3 * CONV_WIDTH])
        cw = _taps(cw_ref[...])
        cv, a2, a1 = _conv3(a, cw, a_halo)
        dyc, dgc = _rms_bwd(gate_b * cv, gc_ref[...], dycn_ref[...])
        dgc_ref[...] += dgc
        dcv = dyc * gate_b
        dcw_ref[...] += jnp.concatenate(
            [jnp.sum(dcv * a2, axis=0, keepdims=True), jnp.sum(dcv * a1, axis=0, keepdims=True),
             jnp.sum(dcv * a, axis=0, keepdims=True)], axis=0)
        da = _conv3_bwd_input(dcv, cw, next_ref[...])
        next_ref[...] = dcv[:SUBLANES]
        q0 = 3 * CONV_WIDTH
        k0 = q0 + ATTN_WIDTH
        dq, dgq = _head_norm_bwd(proj[:, q0:k0], gq_ref[...], dqn_ref[...], N_HEADS)
        dk, dgk = _head_norm_bwd(proj[:, k0:k0 + KV_WIDTH], gk_ref[...], dkn_ref[...], 2)
        dgq_ref[...] += dgq
        dgk_ref[...] += dgk
        dproj = jnp.concatenate([dyc * cv, da * hc, da * gate_c, dq, dk, dv_ref[...]], axis=-1).astype(BF16)
        dproj_ref[...] = dproj
        du1 = _dot(dproj, w_ref[...])
        xv = x_ref[...]
        dn, dg1 = _rms_bwd(xv, g1_ref[...], du1)
        dx_ref[...] = dh1_ref[...] + dn
        dg1_ref[...] += dg1

    rev = lambda i: n_t - 1 - i
    rows = lambda w: pl.BlockSpec((tm, w), lambda i: (rev(i), 0))
    const = lambda shape: pl.BlockSpec(shape, lambda i: (0,) * len(shape))
    halo = pl.BlockSpec((SUBLANES, IN_WIDTH), lambda i: (jnp.maximum(rev(i) * halo_blocks - 1, 0), 0))
    return pl.pallas_call(
        body, name="mix_in_bwd", grid=(n_t,),
        in_specs=[rows(D_MODEL), rows(D_MODEL), rows(IN_WIDTH), halo, rows(CONV_WIDTH), rows(ATTN_WIDTH), rows(KV_WIDTH),
                  rows(KV_WIDTH), const((IN_WIDTH, D_MODEL)), const((3, CONV_WIDTH)), const((1, D_MODEL)),
                  const((1, HEAD_DIM)), const((1, HEAD_DIM)), const((1, CONV_WIDTH))],
        out_specs=[rows(D_MODEL), rows(IN_WIDTH), const((3, CONV_WIDTH)), const((1, CONV_WIDTH)),
                   const((1, HEAD_DIM)), const((1, HEAD_DIM)), const((1, D_MODEL))],
        out_shape=[jax.ShapeDtypeStruct((SEQ, D_MODEL), F32), jax.ShapeDtypeStruct((SEQ, IN_WIDTH), BF16),
                   jax.ShapeDtypeStruct((3, CONV_WIDTH), F32),
                   jax.ShapeDtypeStruct((1, CONV_WIDTH), F32), jax.ShapeDtypeStruct((1, HEAD_DIM), F32),
                   jax.ShapeDtypeStruct((1, HEAD_DIM), F32), jax.ShapeDtypeStruct((1, D_MODEL), F32)],
        scratch_shapes=[pltpu.VMEM((SUBLANES, CONV_WIDTH), F32)],
        compiler_params=_params(("arbitrary",)),
    )(x, dh1, proj, proj, dycn, dqn, dkn, dv, w_in_t, conv_w, g1, gq, gk, gconv)


def _grad_w_in(dproj, u1, after=None):
    bw = 768

    def body(a_ref, b_ref, o_ref):
        o_ref[...] = _dot_tn(a_ref[...], b_ref[...]).astype(BF16)

    body, more_specs, more = _ordered_behind(body, 2, after)
    return pl.pallas_call(
        body, name="grad_w_in", grid=(IN_WIDTH // bw,),
        in_specs=[pl.BlockSpec((SEQ, bw), lambda k: (0, k)), pl.BlockSpec((SEQ, D_MODEL), lambda k: (0, 0))] + more_specs,
        out_specs=pl.BlockSpec((bw, D_MODEL), lambda k: (k, 0)),
        out_shape=jax.ShapeDtypeStruct((IN_WIDTH, D_MODEL), BF16),
        compiler_params=_params(("arbitrary",)),
    )(dproj, u1, *more)


def _adamw_math(w, g, m, v):
    m = ADAM_B1 * m + (1.0 - ADAM_B1) * g
    v = ADAM_B2 * v + (1.0 - ADAM_B2) * (g * g)
    m_hat = m / (1.0 - ADAM_B1 ** ADAM_STEP)
    v_hat = v / (1.0 - ADAM_B2 ** ADAM_STEP)
    return -ADAM_LR * (m_hat / (jnp.sqrt(v_hat) + ADAM_EPS) + ADAM_WD * w), m, v


_ROW_G1, _ROW_G2, _ROW_OUT_NORMS, _ROW_FFN_B, _ROW_GQ, _ROW_GK, _ROW_SINKS, _ROW_LOSS, _ROW_TABLE = 0, 1, 2, 3, 11, 12, 13, 14, 16
SMALL_ROWS, SMALL_COLS = 24, 1024
_SMALL_NAMES = ("norm_mix_g", "norm_ffn_g", "out_norm_conv_g", "out_norm_attn_g", "ffn_conv_b", "q_norm_g", "k_norm_g",
                "sinks", "rel_bias_table")


def _pack_small_grads(dg1, dg2, dgconv, dgattn, dfb, dgq, dgk, dsinks, dtbl_t, loss_acc, after=None):
    def body(dg1_ref, dg2_ref, dgc_ref, dga_ref, dfb_ref, dgq_ref, dgk_ref, ds_ref, dt_ref, loss_ref, o_ref, all_ref):
        o_ref[...] = jnp.zeros_like(o_ref)
        o_ref[_ROW_G1:_ROW_G1 + 1, :] = dg1_ref[...]
        o_ref[_ROW_G2:_ROW_G2 + 1, :] = dg2_ref[...]
        o_ref[_ROW_OUT_NORMS:_ROW_OUT_NORMS + 1, 0:CONV_WIDTH] = dgc_ref[...]
        o_ref[_ROW_OUT_NORMS:_ROW_OUT_NORMS + 1, CONV_WIDTH:] = dga_ref[...]
        for k in range(N_DEV):
            o_ref[_ROW_FFN_B + k:_ROW_FFN_B + k + 1, 0:FFN_BLK] = dfb_ref[k // N_FFN_BLK, k % N_FFN_BLK]
        o_ref[_ROW_GQ:_ROW_GQ + 1, 0:HEAD_DIM] = dgq_ref[...]
        o_ref[_ROW_GK:_ROW_GK + 1, 0:HEAD_DIM] = dgk_ref[...]
        o_ref[_ROW_SINKS:_ROW_SINKS + 1, 0:128] = ds_ref[...]
        o_ref[_ROW_LOSS:_ROW_LOSS + 1, 0:128] = loss_ref[0:1, :]
        o_ref[_ROW_TABLE:_ROW_TABLE + N_HEADS, 0:128] = dt_ref[...]
        for s in range(N_DEV):
            all_ref[s] = o_ref[...]

    body, more_specs, more = _ordered_behind(body, 10, after)
    return pl.pallas_call(
        body, name="pack_small_grads",
        in_specs=[pl.BlockSpec(memory_space=pltpu.VMEM)] * 10 + more_specs,
        out_shape=[jax.ShapeDtypeStruct((SMALL_ROWS, SMALL_COLS), F32),
                   jax.ShapeDtypeStruct((N_DEV, SMALL_ROWS, SMALL_COLS), F32)],
    )(dg1, dg2, dgconv, dgattn, dfb, dgq, dgk, dsinks, dtbl_t, loss_acc, *more)


def _adamw_small(recv, params, after):
    names = _SMALL_NAMES
    n = len(names)

    def grad_of(g, name, k=None):
        if name == "norm_mix_g":
            return g[_ROW_G1:_ROW_G1 + 1, :]
        if name == "norm_ffn_g":
            return g[_ROW_G2:_ROW_G2 + 1, :]
        if name == "out_norm_conv_g":
            return g[_ROW_OUT_NORMS:_ROW_OUT_NORMS + 1, 0:CONV_WIDTH]
        if name == "out_norm_attn_g":
            return g[_ROW_OUT_NORMS:_ROW_OUT_NORMS + 1, CONV_WIDTH:]
        if name == "ffn_conv_b":
            return g[_ROW_FFN_B + k:_ROW_FFN_B + k + 1, 0:FFN_BLK]
        if name == "q_norm_g":
            return g[_ROW_GQ:_ROW_GQ + 1, 0:HEAD_DIM]
        if name == "k_norm_g":
            return g[_ROW_GK:_ROW_GK + 1, 0:HEAD_DIM]
        if name == "sinks":
            return g[_ROW_SINKS:_ROW_SINKS + 1, 0:N_HEADS]
        return g[_ROW_TABLE:_ROW_TABLE + N_HEADS, 0:NUM_BUCKETS]

    def body(r_ref, *refs):
        ins, outs, loss_ref = refs[:3 * n], refs[3 * n:7 * n], refs[7 * n]
        g = r_ref[0]
        for s in range(1, N_DEV):
            g = g + r_ref[s]
        loss_ref[...] = g[_ROW_LOSS:_ROW_LOSS + 1, 0:128]
        for i, name in enumerate(names):
            w_ref, m_ref, v_ref = ins[3 * i:3 * i + 3]
            o = outs[4 * i:4 * i + 4]
            cols = [slice(FFN_BLK * k, FFN_BLK * (k + 1)) for k in range(N_DEV)] if name == "ffn_conv_b" else [slice(None)]
            for k, cs in enumerate(cols):
                gk = grad_of(g, name, k)
                d, m2, v2 = _adamw_math(w_ref[:, cs], gk, m_ref[:, cs], v_ref[:, cs])
                o[0][:, cs], o[1][:, cs], o[2][:, cs], o[3][:, cs] = gk, d, m2, v2

    flat = [a for name in names for a in params[name]]
    body, more_specs, more = _ordered_behind(body, 1 + 3 * n, after)
    vmem = pl.BlockSpec(memory_space=pltpu.VMEM)
    out = pl.pallas_call(
        body, name="adamw_small",
        in_specs=[vmem] * (1 + 3 * n) + more_specs,
        out_shape=[jax.ShapeDtypeStruct(params[name][0].shape, F32) for name in names for _ in range(4)]
        + [jax.ShapeDtypeStruct((1, 128), F32)],
        compiler_params=pltpu.CompilerParams(vmem_limit_bytes=VMEM_LIMIT),
    )(recv, *flat, *more)
    return {name: tuple(out[4 * i:4 * i + 4]) for i, name in enumerate(names)}, out[4 * n]


def _adamw_direct(w, m, v, own, recv, me, name, row_blocks=1, after=None):
    rb = w.shape[0] // row_blocks
    cols = w.shape[1]

    def body(me_ref, w_ref, m_ref, v_ref, o_ref, r_ref, g_o, d_o, m_o, v_o):
        g = o_ref[...].astype(F32)
        for s in range(N_DEV - 1):
            g = g + r_ref[s].astype(F32)
        g_o[...] = g
        d_o[...], m_o[...], v_o[...] = _adamw_math(w_ref[...], g, m_ref[...], v_ref[...])

    blk = pl.BlockSpec((rb, cols), lambda i, me_ref: (i, 0))
    oblk = pl.BlockSpec((None, rb, cols), lambda i, me_ref: (me_ref[0], i, 0))
    rblk = pl.BlockSpec((N_DEV - 1, rb, cols), lambda i, me_ref: (0, i, 0))
    body, more_specs, more = _ordered_behind(body, 6, after)
    return pl.pallas_call(
        body, name=name,
        grid_spec=pltpu.PrefetchScalarGridSpec(num_scalar_prefetch=1, grid=(row_blocks,),
                                               in_specs=[blk, blk, blk, oblk, rblk] + more_specs, out_specs=[blk] * 4),
        out_shape=[jax.ShapeDtypeStruct(w.shape, F32)] * 4,
        compiler_params=_params(("arbitrary",)),
    )(me, w, m, v, own, recv, *more)


def _adamw(w, m, v, halves, recv, me_chip, name, row_blocks=1, after=None):
    rb = w.shape[0] // row_blocks
    tail = w.shape[1:]
    zeros = (0,) * len(tail)
    own, got = halves

    def body(ids_ref, w_ref, m_ref, v_ref, o_ref, s_ref, r_ref, g_o, d_o, m_o, v_o):
        g = o_ref[...].astype(F32) + s_ref[...].astype(F32)
        for s in range(3):
            g = g + r_ref[s].astype(F32)
        g_o[...] = g
        d_o[...], m_o[...], v_o[...] = _adamw_math(w_ref[...], g, m_ref[...], v_ref[...])

    blk = pl.BlockSpec((rb,) + tail, lambda i, ids_ref: (i,) + zeros)
    oblk = pl.BlockSpec((None, rb) + tail, lambda i, ids_ref: (ids_ref[0], i) + zeros)
    sblk = pl.BlockSpec((None, rb) + tail, lambda i, ids_ref: (ids_ref[1], i) + zeros)
    rblk = pl.BlockSpec((3, rb) + tail, lambda i, ids_ref: (0, i) + zeros)
    body, more_specs, more = _ordered_behind(body, 7, after)
    return pl.pallas_call(
        body, name=name,
        grid_spec=pltpu.PrefetchScalarGridSpec(num_scalar_prefetch=1, grid=(row_blocks,),
                                               in_specs=[blk, blk, blk, oblk, sblk, rblk] + more_specs,
                                               out_specs=[blk] * 4),
        out_shape=[jax.ShapeDtypeStruct(w.shape, F32)] * 4,
        compiler_params=_params(("arbitrary",)),
    )(me_chip, w, m, v, own, got, recv, *more)


def kernel(x, norm_mix_g, w_in, conv_w, q_norm_g, k_norm_g, rel_bias_table, sinks, out_norm_conv_g, out_norm_attn_g, w_out, norm_ffn_g, w_up, ffn_conv_w, ffn_conv_b, w_down, loss_target, m_norm_mix_g, m_w_in, m_conv_w, m_q_norm_g, m_k_norm_g, m_rel_bias_table, m_sinks, m_out_norm_conv_g, m_out_norm_attn_g, m_w_out, m_norm_ffn_g, m_w_up, m_ffn_conv_w, m_ffn_conv_b, m_w_down, v_norm_mix_g, v_w_in, v_conv_w, v_q_norm_g, v_k_norm_g, v_rel_bias_table, v_sinks, v_out_norm_conv_g, v_out_norm_attn_g, v_w_out, v_norm_ffn_g, v_w_up, v_ffn_conv_w, v_ffn_conv_b, v_w_down):
    p = dict(norm_mix_g=norm_mix_g, w_in=w_in, conv_w=conv_w, q_norm_g=q_norm_g, k_norm_g=k_norm_g,
             rel_bias_table=rel_bias_table, sinks=sinks, out_norm_conv_g=out_norm_conv_g, out_norm_attn_g=out_norm_attn_g,
             w_out=w_out, norm_ffn_g=norm_ffn_g, w_up=w_up, ffn_conv_w=ffn_conv_w, ffn_conv_b=ffn_conv_b, w_down=w_down)
    m = dict(norm_mix_g=m_norm_mix_g, w_in=m_w_in, conv_w=m_conv_w, q_norm_g=m_q_norm_g, k_norm_g=m_k_norm_g,
             rel_bias_table=m_rel_bias_table, sinks=m_sinks, out_norm_conv_g=m_out_norm_conv_g,
             out_norm_attn_g=m_out_norm_attn_g, w_out=m_w_out, norm_ffn_g=m_norm_ffn_g, w_up=m_w_up,
             ffn_conv_w=m_ffn_conv_w, ffn_conv_b=m_ffn_conv_b, w_down=m_w_down)
    v = dict(norm_mix_g=v_norm_mix_g, w_in=v_w_in, conv_w=v_conv_w, q_norm_g=v_q_norm_g, k_norm_g=v_k_norm_g,
             rel_bias_table=v_rel_bias_table, sinks=v_sinks, out_norm_conv_g=v_out_norm_conv_g,
             out_norm_attn_g=v_out_norm_attn_g, w_out=v_w_out, norm_ffn_g=v_norm_ffn_g, w_up=v_w_up,
             ffn_conv_w=v_ffn_conv_w, ffn_conv_b=v_ffn_conv_b, w_down=v_w_down)

    xs, tgt = x[0], loss_target[0]
    g1, g2, gq, gk, gconv, gattn = norm_mix_g, norm_ffn_g, q_norm_g, k_norm_g, out_norm_conv_g, out_norm_attn_g
    ix, iy, ic = _coords()
    me_chip = jnp.stack([_lin(ix, iy, ic), 2 * ix + iy]).astype(jnp.int32)
    me = _lin(ix, iy, ic).astype(jnp.int32).reshape(1)
    bkt = jnp.asarray(_bucket_map())
    tr = lambda a: a[0].T
    taps = lambda a: jnp.transpose(a, (1, 0, 2))
    tbl_t = rel_bias_table.T

    wi_l, cw_l = _place_shards(me, [tr(w_in), taps(conv_w)], [BF16, F32], "place_mixer_shards")
    finish_a, token_a = _all_gather_split([wi_l, cw_l], "mixer", None)
    wo_l, wu_l, wd_l, fcw_l = _place_shards(me, [w_out[0], tr(w_up), w_down[0], taps(ffn_conv_w)],
                                            [BF16, BF16, BF16, F32], "place_ffn_shards", after=token_a)
    ffn_stage2, ffn_stage3, token_b = _all_gather_tree([wo_l, wu_l, wd_l, fcw_l], "ffn", token_a)
    wi_g, cw_g = finish_a(token_b)
    w_in_t = wi_g.reshape(IN_WIDTH, D_MODEL)
    conv_w_f = jnp.transpose(cw_g[:, :, 0, :], (1, 0, 2)).reshape(3, CONV_WIDTH)

    proj, u1, ycn, qn, kn, vv = _mix_in_fwd(xs, g1, w_in_t, conv_w_f, gq, gk, gconv)
    token_b2 = ffn_stage2(ycn)
    y_attn, yan, probs, psinks = _attn_fwd(qn, kn, vv, tbl_t, sinks, bkt, gattn, after=token_b2)
    wo_g, wu_g, wd_g, fcw_g = ffn_stage3(yan)
    w_out_f = wo_g.reshape(D_MODEL, D_MODEL)
    w_down_f = wd_g.reshape(N_FFN_BLK, FFN_BLK, D_MODEL)
    w_up_f = wu_g.reshape(2, N_FFN_BLK, FFN_BLK, D_MODEL)
    fcw_f = fcw_g.reshape(2, N_FFN_BLK, 3, 1, FFN_BLK)
    fcb = ffn_conv_b.reshape(2, N_FFN_BLK, 1, FFN_BLK)
    h1, u2, up, pre, act, dh2, dh2b, loss_acc = _ffn_fwd(xs, ycn, yan, w_out_f, g2, w_up_f, fcw_f, fcb, w_down_f, tgt)

    dw_down = _grad_tn_blocked(act, dh2b, "grad_w_down").reshape(N_DEV, D_FF // N_DEV, D_MODEL)
    plan_d, slots_d = _scatter_plan(1)
    d_sem = _split_start("scatter_w_down_start", [dw_down], [lax.empty((N_DEV - 1,) + dw_down.shape[1:], BF16)],
                         plan_d, None, _ALL_FOR_W_DOWN)
    dup, dh1, dh1b, dfb, dfcw, dg2 = _ffn_bwd(dh2, dh2b, h1, g2, up, pre, w_up_f, fcw_f, w_down_f, after=d_sem[4])
    dw_up = _grad_tn_blocked(dup.reshape(N_DEV, SEQ, FFN_BLK), u2, "grad_w_up")
    dw_out = _grad_tn([ycn, yan], dh1b, CONV_WIDTH, "grad_w_out").reshape(N_DEV, D_MODEL // N_DEV, D_MODEL)
    out_bwd = {}

    def behind_ffn(token):
        out_bwd["r"] = _out_bwd(dh1b, w_out_f, y_attn, gattn, after=token)
        return out_bwd["r"][0]

    finish_ffn, token_ffn = _reduce_scatter_split(
        [dw_up, dw_out, dfcw.reshape(N_DEV, 3, 1, FFN_BLK)], "ffn", behind_ffn)
    dycn, dy_attn, dgattn = out_bwd["r"]
    dqn, dkn, dv, dtbl_t, dsinks = _attn_bwd(qn, kn, vv, dy_attn, probs, psinks, bkt, after=token_ffn)
    dx, dproj, dcw, dgconv, dgq, dgk, dg1 = _mix_in_bwd(xs, dh1, proj, dycn, dqn, dkn, dv, w_in_t, conv_w_f,
                                                         g1, gq, gk, gconv)
    dw_in_t = _grad_w_in(dproj, u1).reshape(N_DEV, IN_WIDTH // N_DEV, D_MODEL)
    dcw_b = jnp.transpose(dcw.reshape(3, N_DEV, 1, CONV_WIDTH // N_DEV), (1, 0, 2, 3))
    plan_s, slots_s = _broadcast_plan()
    adam = {}
    ffn_got = {}
    small = {}

    def behind_mixer(token):
        packed, packed_all = _pack_small_grads(dg1, dg2, dgconv, dgattn, dfb, dgq, dgk, dsinks, dtbl_t, loss_acc,
                                               after=token)
        small["r"] = _split_start("gather_small_start", [packed], [packed_all], plan_s, None, _ALL_FOR_SMALL)
        ffn_got["r"] = finish_ffn(small["r"][4])
        return ffn_got["r"][1][0]

    finish_mixer, token_mixer = _reduce_scatter_split([dw_in_t, dcw_b], "mixer", behind_mixer)
    s_sem, r_sem, src_s, land_s, _ = small["r"]
    (p_wu, p_wo, p_fcw), (r_wu, r_wo, r_fcw) = ffn_got["r"]
    (own_wd,), (r_wd,) = _split_wait("scatter_w_down_wait", d_sem[0], d_sem[1], d_sem[2], d_sem[3], plan_d, slots_d,
                                     token_mixer)
    adam["w_down"] = _adamw_direct(w_down[0], m_w_down[0], v_w_down[0], own_wd, r_wd, me, "adamw_w_down", row_blocks=2)
    adam_up = _adamw(tr(w_up), tr(m_w_up), tr(v_w_up), p_wu, r_wu, me_chip, "adamw_w_up", row_blocks=4,
                     after=adam["w_down"][0])
    adam["w_out"] = _adamw(w_out[0], m_w_out[0], v_w_out[0], p_wo, r_wo, me_chip, "adamw_w_out", after=adam_up[0])
    adam_fcw = _adamw(taps(ffn_conv_w), taps(m_ffn_conv_w), taps(v_ffn_conv_w), p_fcw, r_fcw, me_chip, "adamw_ffn_conv_w",
                      after=adam["w_out"][0])
    _, (r_small,) = _split_wait("gather_small_wait", s_sem, r_sem, src_s, land_s, plan_s, slots_s, adam_fcw[0])
    small_in = {k: (p[k], m[k], v[k]) for k in _SMALL_NAMES}
    small_in["rel_bias_table"] = (tbl_t, m_rel_bias_table.T, v_rel_bias_table.T)
    small_out, loss_row = _adamw_small(r_small, small_in, None)
    (p_wi, p_cw), (r_wi, r_cw) = finish_mixer(loss_row)
    adam_in = _adamw(tr(w_in), tr(m_w_in), tr(v_w_in), p_wi, r_wi, me_chip, "adamw_w_in")
    adam_cw = _adamw(taps(conv_w), taps(m_conv_w), taps(v_conv_w), p_cw, r_cw, me_chip, "adamw_conv_w")

    res = {k: tuple(a[None] for a in t) for k, t in adam.items()}
    res["w_up"] = tuple(a.T[None] for a in adam_up)
    res["w_in"] = tuple(a.T[None] for a in adam_in)
    res["ffn_conv_w"] = tuple(taps(a) for a in adam_fcw)
    res["conv_w"] = tuple(taps(a) for a in adam_cw)
    res.update(small_out)
    res["rel_bias_table"] = tuple(a.T for a in small_out["rel_bias_table"])
    loss = loss_row[0, 0]
    order = ("norm_mix_g", "w_in", "conv_w", "q_norm_g", "k_norm_g", "rel_bias_table", "sinks", "out_norm_conv_g",
             "out_norm_attn_g", "w_out", "norm_ffn_g", "w_up", "ffn_conv_w", "ffn_conv_b", "w_down")
    return (loss, dx[None], *[res[k][0] for k in order], *[res[k][1] for k in order],
            *[res[k][2] for k in order], *[res[k][3] for k in order])
```

```python
import math

import numpy as np
import jax
import jax.numpy as jnp
from jax import lax
from jax.experimental import pallas as pl
from jax.experimental.pallas import tpu as pltpu

F32 = jnp.float32
BF16 = jnp.bfloat16

SEQ = 2048
D_MODEL = 1024
CONV_WIDTH = 512
ATTN_WIDTH = 512
KV_WIDTH = 128
HEAD_DIM = 64
N_HEADS = 8
GQA_GROUP = 4
IN_WIDTH = 2304
D_FF = 2816
BLK = 128
NUM_BUCKETS = 32
EPS = 1e-6
NEG_INF = -1e30
ADAM_LR = 0.001
ADAM_B1 = 0.9
ADAM_B2 = 0.999
ADAM_EPS = 1e-08
ADAM_WD = 0.01
ADAM_STEP = 10

N_DEV = 8
FFN_BLK = 2 * D_FF // N_DEV
N_FFN_BLK = D_FF // FFN_BLK
SUBLANES = 8
VMEM_LIMIT = 56 * 1024 * 1024

_MESH = pl.DeviceIdType.MESH
_ANY = pl.BlockSpec(memory_space=pl.ANY)


def _params(sem):
    return pltpu.CompilerParams(dimension_semantics=sem, vmem_limit_bytes=VMEM_LIMIT)


def _ordered_behind(body, pos, after):
    if after is None:
        return body, [], []
    return (lambda *refs: body(*refs[:pos], *refs[pos + 1:])), [_ANY], [after]


def _dot(a, b):
    return jnp.dot(a, b, preferred_element_type=F32)


def _dot_nt(a, b):
    return lax.dot_general(a, b, (((1,), (1,)), ((), ())), preferred_element_type=F32)


def _dot_tn(a, b):
    return lax.dot_general(a, b, (((0,), (0,)), ((), ())), preferred_element_type=F32)


def _shift_down(x, s, halo):
    r = pltpu.roll(x, s, axis=0)
    hr = pltpu.roll(halo, s, axis=0)
    row = lax.broadcasted_iota(jnp.int32, halo.shape, 0)
    top = jnp.where(row < s, hr, r[:SUBLANES])
    return jnp.concatenate([top, r[SUBLANES:]], axis=0)


def _shift_up(x, s, halo):
    n = x.shape[0]
    r = pltpu.roll(x, n - s, axis=0)
    hr = pltpu.roll(halo, SUBLANES - s, axis=0)
    row = lax.broadcasted_iota(jnp.int32, halo.shape, 0)
    bot = jnp.where(row >= SUBLANES - s, hr, r[n - SUBLANES:])
    return jnp.concatenate([r[:n - SUBLANES], bot], axis=0)


def _taps(w):
    return (w[0], w[1], w[2]) if len(w.shape) == 3 else (w[0:1], w[1:2], w[2:3])


def _gathered_taps(ref):
    return tuple(jnp.concatenate([ref[r, k] for r in range(N_DEV)], axis=-1) for k in range(3))


def _conv3(x, w, halo):
    x2 = _shift_down(x, 2, halo)
    x1 = _shift_down(x, 1, halo)
    return x2 * w[0] + x1 * w[1] + x * w[2], x2, x1


def _conv3_bwd_input(dy, w, halo_next):
    return dy * w[2] + _shift_up(dy, 1, halo_next) * w[1] + _shift_up(dy, 2, halo_next) * w[0]


def _rstd(x):
    return lax.rsqrt(jnp.mean(x * x, axis=-1, keepdims=True) + EPS)


def _rms_bwd(x, g, dy):
    r = _rstd(x)
    n = x * r
    dn = dy * g
    dx = r * (dn - n * jnp.mean(dn * n, axis=-1, keepdims=True))
    return dx, jnp.sum(dy * n, axis=0, keepdims=True)


def _head_mean(x):
    width = x.shape[-1]
    ri = lax.broadcasted_iota(jnp.int32, (width, width), 0) // HEAD_DIM
    ci = lax.broadcasted_iota(jnp.int32, (width, width), 1) // HEAD_DIM
    ones = jnp.where(ri == ci, 1.0, 0.0).astype(BF16)
    hi = x.astype(BF16)
    lo = (x - hi.astype(F32)).astype(BF16)
    return (_dot(hi, ones) + _dot(lo, ones)) * (1.0 / HEAD_DIM)


def _head_norm(x, g, heads):
    return x * lax.rsqrt(_head_mean(x * x) + EPS) * jnp.tile(g, (1, heads))


def _head_norm_bwd(x, g, dy, heads):
    r = lax.rsqrt(_head_mean(x * x) + EPS)
    n = x * r
    dn = dy * jnp.tile(g, (1, heads))
    dx = r * (dn - n * _head_mean(dn * n))
    per_lane = jnp.sum(dy * n, axis=0, keepdims=True)
    dg = per_lane[:, 0:HEAD_DIM]
    for h in range(1, heads):
        dg = dg + per_lane[:, HEAD_DIM * h:HEAD_DIM * (h + 1)]
    return dx, dg


def _bucket_map():
    q = np.arange(BLK)[:, None]
    j = np.arange(BLK)[None, :]
    n = np.where(j > q, q + BLK - j, q - j)
    nf = np.maximum(n, 1).astype(np.float32)
    max_exact = NUM_BUCKETS // 2
    large = max_exact + (np.log(nf / max_exact) / math.log(BLK / max_exact) * (NUM_BUCKETS - max_exact)).astype(np.int32)
    large = np.minimum(large, NUM_BUCKETS - 1)
    return np.where(n < max_exact, n, large).astype(np.int32)


def _coords():
    return lax.axis_index("x"), lax.axis_index("y"), lax.axis_index("c")


def _lin(px, py, pc):
    return 4 * px + 2 * py + pc


def _chips(x, y):
    return [(1 - x, y), (x, 1 - y), (1 - x, 1 - y)]


def _peers(x, y, c):
    return [(1 - x if r & 4 else x, 1 - y if r & 2 else y, 1 - c if r & 1 else c) for r in range(1, N_DEV)]


_HBM = pl.BlockSpec(memory_space=pltpu.HBM)
_SEM = pl.BlockSpec(memory_space=pltpu.SEMAPHORE)
_EFFECT = pltpu.SideEffectType.DATAFLOW_SIDE_EFFECTING


def _in_hbm(a):
    return pltpu.with_memory_space_constraint(a, pltpu.HBM)


_SIBLING = (1, lambda x, y, c: [(x, y, 1 - c)])
_SIBLING_AND_CHIPS = (2, lambda x, y, c: [(x, y, 1 - c)] + [(cx, cy, c) for cx, cy in _chips(x, y)])
_SIBLING_AND_NEIGHBOURS = (3, lambda x, y, c: [(x, y, 1 - c), (1 - x, y, c), (x, 1 - y, c)])
_ONWARD_AND_SIBLING = (4, lambda x, y, c: [(jnp.where(c == 1, x, 1 - x), jnp.where(c == 1, 1 - y, y), c), (x, y, 1 - c)])
_ALL_FOR_W_DOWN = (5, lambda x, y, c: _peers(x, y, c))
_CHIPS = (6, lambda x, y, c: [(cx, cy, c) for cx, cy in _chips(x, y)])
_ALL_FOR_SMALL = (7, lambda x, y, c: _peers(x, y, c))


def _split_start(name, srcs, lands, plan, after, handshake):
    ns, nl = len(srcs), len(lands)
    n_copies = len(plan(0, 0, 0))
    n_after = 0 if after is None else 1
    collective_id, peers_of = handshake

    def body(*refs):
        src_refs, land_refs = refs[:ns + nl], refs[ns:ns + nl]
        send_sems, recv_sems = refs[ns + nl + n_after], refs[ns + nl + n_after + 1]
        token = refs[-1]
        barrier = pltpu.get_barrier_semaphore()
        peers = peers_of(*_coords())
        for peer in peers:
            pl.semaphore_signal(barrier, inc=1, device_id=peer, device_id_type=_MESH)
        pl.semaphore_wait(barrier, len(peers))
        for k, (a, s_slot, l, d_slot, dev) in enumerate(plan(*_coords())):
            src = src_refs[a] if s_slot is None else src_refs[a].at[s_slot]
            pltpu.make_async_remote_copy(src_ref=src, dst_ref=land_refs[l].at[d_slot], send_sem=send_sems.at[k],
                                         recv_sem=recv_sems.at[k], device_id=dev, device_id_type=_MESH).start()
        token[...] = jnp.zeros_like(token)

    arrs = list(srcs) + list(lands)
    out = pl.pallas_call(
        body, name=name,
        out_shape=(pltpu.SemaphoreType.DMA((n_copies,)), pltpu.SemaphoreType.DMA((n_copies,)),
                   *[pltpu.HBM(a.shape, a.dtype) for a in arrs], jax.ShapeDtypeStruct((SUBLANES, 128), F32)),
        in_specs=[_HBM] * (ns + nl) + [_ANY] * n_after,
        out_specs=(_SEM, _SEM, *[_HBM] * (ns + nl), pl.BlockSpec(memory_space=pltpu.VMEM)),
        input_output_aliases={i: 2 + i for i in range(ns + nl)},
        compiler_params=pltpu.CompilerParams(has_side_effects=_EFFECT, collective_id=collective_id),
    )(*[_in_hbm(a) for a in arrs], *([] if after is None else [after]))
    return out[0], out[1], list(out[2:2 + ns]), list(out[2 + ns:2 + ns + nl]), out[-1]


def _split_wait(name, send_sems, recv_sems, srcs, lands, plan, recv_slots, after):
    ns, nl = len(srcs), len(lands)

    def body(*refs):
        src_refs, land_refs = refs[:ns + nl], refs[ns:ns + nl]
        send_sems, recv_sems = refs[ns + nl], refs[ns + nl + 1]
        coords = _coords()
        slots = recv_slots(*coords)
        for k, (a, s_slot, l, _, dev) in enumerate(plan(*coords)):
            src = src_refs[a] if s_slot is None else src_refs[a].at[s_slot]
            cp = pltpu.make_async_remote_copy(src_ref=src, dst_ref=land_refs[l].at[slots[k]], send_sem=send_sems.at[k],
                                              recv_sem=recv_sems.at[k], device_id=dev, device_id_type=_MESH)
            cp.wait_send()
            cp.wait_recv()

    arrs = list(srcs) + list(lands)
    out = pl.pallas_call(
        body, name=name,
        out_shape=tuple(pltpu.HBM(a.shape, a.dtype) for a in arrs),
        in_specs=[_HBM] * (ns + nl) + [_SEM, _SEM, _ANY],
        out_specs=tuple([_HBM] * (ns + nl)),
        input_output_aliases={i: i for i in range(ns + nl)},
        compiler_params=pltpu.CompilerParams(has_side_effects=_EFFECT),
    )(*arrs, send_sems, recv_sems, after)
    return list(out[:ns]), list(out[ns:])


def _gather_plan_ici(n):
    def plan(x, y, c):
        me = _lin(x, y, c)
        out = []
        for a in range(n):
            out.append((a, me, a, me, (x, y, 1 - c)))
            out += [(a, me, a, me, (cx, cy, c)) for cx, cy in _chips(x, y)]
        return out

    def recv_slots(x, y, c):
        out = []
        for _ in range(n):
            out.append(_lin(x, y, 1 - c))
            out += [_lin(cx, cy, c) for cx, cy in _chips(x, y)]
        return out

    return plan, recv_slots


def _gather_plan_d2d(n):
    def plan(x, y, c):
        return [(a, _lin(cx, cy, c), a, _lin(cx, cy, c), (x, y, 1 - c)) for a in range(n) for cx, cy in _chips(x, y)]

    def recv_slots(x, y, c):
        return [_lin(cx, cy, 1 - c) for _ in range(n) for cx, cy in _chips(x, y)]

    return plan, recv_slots


def _all_gather_split(lands, tag, after):
    n = len(lands)
    plan1, slots1 = _gather_plan_ici(n)
    s1, r1, _, lands, token = _split_start(f"gather_{tag}_ici_start", [], lands, plan1, after, _SIBLING_AND_CHIPS)

    def finish(after):
        _, got = _split_wait(f"gather_{tag}_ici_wait", s1, r1, [], lands, plan1, slots1, after)
        plan2, slots2 = _gather_plan_d2d(n)
        s2, r2, _, got, token2 = _split_start(f"gather_{tag}_d2d_start", [], got, plan2, None, _SIBLING)
        return _split_wait(f"gather_{tag}_d2d_wait", s2, r2, [], got, plan2, slots2, token2)[1]

    return finish, token


def _all_gather_tree(lands, tag, after):
    n = len(lands)

    def plan1(x, y, c):
        me = _lin(x, y, c)
        return [(a, me, a, me, dev) for a in range(n) for dev in ((x, y, 1 - c), (1 - x, y, c), (x, 1 - y, c))]

    def slots1(x, y, c):
        return [s for _ in range(n) for s in (_lin(x, y, 1 - c), _lin(1 - x, y, c), _lin(x, 1 - y, c))]

    def plan2(x, y, c):
        from_x, from_y = _lin(1 - x, y, c), _lin(x, 1 - y, c)
        north = c == 1
        passed = jnp.where(north, from_x, from_y)
        onward = (jnp.where(north, x, 1 - x), jnp.where(north, 1 - y, y), c)
        sib = (x, y, 1 - c)
        return [cp for a in range(n) for cp in ((a, passed, a, passed, onward), (a, from_x, a, from_x, sib),
                                                (a, from_y, a, from_y, sib))]

    def slots2(x, y, c):
        return [s for _ in range(n) for s in (_lin(1 - x, 1 - y, c), _lin(1 - x, y, 1 - c), _lin(x, 1 - y, 1 - c))]

    def plan3(x, y, c):
        diag = _lin(1 - x, 1 - y, c)
        return [(a, diag, a, diag, (x, y, 1 - c)) for a in range(n)]

    def slots3(x, y, c):
        return [_lin(1 - x, 1 - y, 1 - c)] * n

    s1, r1, _, lands, token = _split_start(f"gather_{tag}_1_start", [], lands, plan1, after, _SIBLING_AND_NEIGHBOURS)
    state = {}

    def stage2(after):
        _, got = _split_wait(f"gather_{tag}_1_wait", s1, r1, [], lands, plan1, slots1, after)
        state["s"], state["r"], _, state["lands"], token2 = _split_start(f"gather_{tag}_2_start", [], got, plan2, None,
                                                                         _ONWARD_AND_SIBLING)
        return token2

    def stage3(after):
        _, got = _split_wait(f"gather_{tag}_2_wait", state["s"], state["r"], [], state["lands"], plan2, slots2, after)
        s3, r3, _, got, token3 = _split_start(f"gather_{tag}_3_start", [], got, plan3, None, _SIBLING)
        return _split_wait(f"gather_{tag}_3_wait", s3, r3, [], got, plan3, slots3, token3)[1]

    return stage2, stage3, token


_CHIP_LIST = ((0, 0), (0, 1), (1, 0), (1, 1))


def _reduce_plan_d2d(n):
    def plan(x, y, c):
        return [(a, _lin(qx, qy, 1 - c), a, q, (x, y, 1 - c)) for a in range(n) for q, (qx, qy) in enumerate(_CHIP_LIST)]

    def recv_slots(x, y, c):
        return [q for _ in range(n) for q in range(4)]

    return plan, recv_slots


def _reduce_plan_ici(n):
    def plan(x, y, c):
        return [(a, 2 * cx + cy, a, j, (cx, cy, c)) for a in range(n) for j, (cx, cy) in enumerate(_chips(x, y))]

    def recv_slots(x, y, c):
        return [j for _ in range(n) for j in range(3)]

    return plan, recv_slots


def _scatter_plan(n):
    def plan(x, y, c):
        return [(a, _lin(*peer), a, r, peer) for a in range(n) for r, peer in enumerate(_peers(x, y, c))]

    def recv_slots(x, y, c):
        return [r for _ in range(n) for r in range(N_DEV - 1)]

    return plan, recv_slots


def _broadcast_plan():
    def plan(x, y, c):
        return [(0, None, 0, _lin(x, y, c), peer) for peer in _peers(x, y, c)]

    def recv_slots(x, y, c):
        return [_lin(*peer) for peer in _peers(x, y, c)]

    return plan, recv_slots


def _chip_partial(grads, recvd, core, name):
    n = len(grads)

    def body(c_ref, *refs):
        for a in range(n):
            g_ref, r_ref, o_ref = refs[a], refs[n + a], refs[2 * n + a]
            o_ref[...] = (g_ref[...].astype(F32) + r_ref[...].astype(F32)).astype(o_ref.dtype)

    def blk(a, own):
        zeros = (0,) * (a.ndim - 1)
        return pl.BlockSpec((None,) + a.shape[1:],
                            (lambda q, c_ref: (2 * q + c_ref[0],) + zeros) if own else (lambda q, c_ref: (q,) + zeros))

    return pl.pallas_call(
        body, name=name,
        grid_spec=pltpu.PrefetchScalarGridSpec(
            num_scalar_prefetch=1, grid=(4,),
            in_specs=[blk(a, True) for a in grads] + [blk(a, False) for a in recvd],
            out_specs=[blk(a, False) for a in recvd]),
        out_shape=[jax.ShapeDtypeStruct(a.shape, a.dtype) for a in recvd],
        compiler_params=_params(("arbitrary",)),
    )(core, *grads, *recvd)


def _reduce_scatter_split(grads, tag, core, behind):
    n = len(grads)
    plan1, slots1 = _reduce_plan_d2d(n)
    lands1 = [lax.empty((4,) + a.shape[1:], a.dtype) for a in grads]
    s1, r1, srcs1, lands1, token1 = _split_start(f"reduce_{tag}_d2d_start", grads, lands1, plan1, None, _SIBLING)
    own, got = _split_wait(f"reduce_{tag}_d2d_wait", s1, r1, srcs1, lands1, plan1, slots1, behind(token1))
    parts = _chip_partial(own, got, core, f"reduce_{tag}_partial")
    plan2, slots2 = _reduce_plan_ici(n)
    lands2 = [lax.empty((3,) + a.shape[1:], a.dtype) for a in grads]
    s2, r2, srcs2, lands2, token2 = _split_start(f"reduce_{tag}_ici_start", parts, lands2, plan2, None, _CHIPS)

    def finish(after):
        return _split_wait(f"reduce_{tag}_ici_wait", s2, r2, srcs2, lands2, plan2, slots2, after)

    return finish, token2


def _place_shards(me, shards, dtypes, name, after=None):
    n = len(shards)

    def body(me_ref, *refs):
        for a in range(n):
            refs[n + a][...] = refs[a][...].astype(dtypes[a])

    full = lambda s: pl.BlockSpec(s.shape, lambda i, me_ref: (0,) * s.ndim)
    slot = lambda s: pl.BlockSpec((None,) + s.shape, lambda i, me_ref: (me_ref[0],) + (0,) * s.ndim)
    body, more_specs, more = _ordered_behind(body, 1 + n, after)
    return pl.pallas_call(
        body, name=name,
        grid_spec=pltpu.PrefetchScalarGridSpec(num_scalar_prefetch=1, grid=(1,),
                                               in_specs=[full(s) for s in shards] + more_specs,
                                               out_specs=[slot(s) for s in shards]),
        out_shape=[jax.ShapeDtypeStruct((N_DEV,) + s.shape, d) for s, d in zip(shards, dtypes)],
        compiler_params=_params(("arbitrary",)),
    )(me, *shards, *more)


def _mix_in_fwd(x, g1, w_in_t, conv_w, gq, gk, gconv):
    tm = 512
    n_t = SEQ // tm

    def body(x_ref, g1_ref, w_ref, cw_ref, gq_ref, gk_ref, gc_ref,
             proj_ref, u1_ref, ycn_ref, qn_ref, kn_ref, v_ref, halo_ref):
        @pl.when(pl.program_id(0) == 0)
        def _():
            halo_ref[...] = jnp.zeros_like(halo_ref)

        xv = x_ref[...]
        u = (xv * _rstd(xv) * g1_ref[...]).astype(BF16)
        u1_ref[...] = u
        proj = _dot_nt(u, w_ref[...])
        proj_ref[...] = proj
        gate_b = proj[:, 0:CONV_WIDTH]
        a = proj[:, CONV_WIDTH:2 * CONV_WIDTH] * proj[:, 2 * CONV_WIDTH:3 * CONV_WIDTH]
        cv, _, _ = _conv3(a, _gathered_taps(cw_ref), halo_ref[...])
        halo_ref[...] = a[tm - SUBLANES:]
        yc = gate_b * cv
        ycn_ref[...] = (yc * _rstd(yc) * gc_ref[...]).astype(BF16)
        q0 = 3 * CONV_WIDTH
        qn_ref[...] = _head_norm(proj[:, q0:q0 + ATTN_WIDTH], gq_ref[...], N_HEADS).astype(BF16)
        k0 = q0 + ATTN_WIDTH
        kn_ref[...] = _head_norm(proj[:, k0:k0 + KV_WIDTH], gk_ref[...], 2).astype(BF16)
        v_ref[...] = proj[:, k0 + KV_WIDTH:k0 + 2 * KV_WIDTH].astype(BF16)

    const = lambda shape: pl.BlockSpec(shape, lambda i: (0,) * len(shape))
    rows = lambda w: pl.BlockSpec((tm, w), lambda i: (i, 0))
    return pl.pallas_call(
        body, name="mix_in_fwd", grid=(n_t,),
        in_specs=[rows(D_MODEL), const((1, D_MODEL)), const((IN_WIDTH, D_MODEL)), const((N_DEV, 3, 1, CONV_WIDTH // N_DEV)),
                  const((1, HEAD_DIM)), const((1, HEAD_DIM)), const((1, CONV_WIDTH))],
        out_specs=[rows(IN_WIDTH), rows(D_MODEL), rows(CONV_WIDTH), rows(ATTN_WIDTH), rows(KV_WIDTH), rows(KV_WIDTH)],
        out_shape=[jax.ShapeDtypeStruct((SEQ, IN_WIDTH), F32), jax.ShapeDtypeStruct((SEQ, D_MODEL), BF16),
                   jax.ShapeDtypeStruct((SEQ, CONV_WIDTH), BF16),
                   jax.ShapeDtypeStruct((SEQ, ATTN_WIDTH), BF16), jax.ShapeDtypeStruct((SEQ, KV_WIDTH), BF16),
                   jax.ShapeDtypeStruct((SEQ, KV_WIDTH), BF16)],
        scratch_shapes=[pltpu.VMEM((SUBLANES, CONV_WIDTH), F32)],
        compiler_params=_params(("arbitrary",)),
    )(x, g1, w_in_t, conv_w, gq, gk, gconv)


GROUP_ROWS = GQA_GROUP * BLK
QUERY_BLOCKS_PER_STEP = 2


def _band_bias(tbl_ref, bkt, bias_ref):
    for h in range(N_HEADS):
        acc = jnp.zeros(bkt.shape, F32)
        for b in range(NUM_BUCKETS):
            acc = jnp.where(bkt == b, tbl_ref[h, b], acc)
        bias_ref[h // GQA_GROUP, BLK * (h % GQA_GROUP):BLK * (h % GQA_GROUP + 1), :] = acc


def _band_masks(i):
    qi = lax.broadcasted_iota(jnp.int32, (GROUP_ROWS, BLK), 0) & (BLK - 1)
    ji = lax.broadcasted_iota(jnp.int32, (GROUP_ROWS, BLK), 1)
    upper = ji > qi
    return upper, upper & (i == 0)


def _stack_heads(x, g):
    return jnp.concatenate([x[:, HEAD_DIM * h:HEAD_DIM * (h + 1)] for h in range(GQA_GROUP * g, GQA_GROUP * (g + 1))], axis=0)


def _unstack_heads(groups):
    return jnp.concatenate([p[BLK * t:BLK * (t + 1)] for p in groups for t in range(GQA_GROUP)], axis=-1)


def _per_head_rows(vals):
    row = lax.broadcasted_iota(jnp.int32, (GROUP_ROWS, 1), 0)
    col = jnp.full((GROUP_ROWS, 1), vals[GQA_GROUP - 1], F32)
    for t in range(GQA_GROUP - 2, -1, -1):
        col = jnp.where(row < BLK * (t + 1), vals[t], col)
    return col


def _band_rows(ref, i):
    prev = pl.multiple_of(jnp.maximum(i - 1, 0) * BLK, BLK)
    cur = pl.multiple_of(i * BLK, BLK)
    return jnp.concatenate([ref[pl.ds(prev, BLK), :], ref[pl.ds(cur, BLK), :]], axis=0), prev, cur


def _fold(band, upper):
    return jnp.where(upper, band[:, :BLK], band[:, BLK:])


def _unfold(tile, upper):
    return jnp.concatenate([jnp.where(upper, tile, 0.0), jnp.where(upper, 0.0, tile)], axis=1)


def _head_probs(qh, kh, bias, upper, dead, sink):
    logits = _fold(_dot_nt(qh, kh), upper) * (HEAD_DIM ** -0.5) + bias
    logits = jnp.where(dead, NEG_INF, logits)
    m = jnp.maximum(jnp.max(logits, axis=-1, keepdims=True), sink)
    p = jnp.exp(logits - m)
    es = jnp.exp(sink - m)
    den = jnp.sum(p, axis=-1, keepdims=True) + es
    return p / den, es / den


def _attn_fwd(qn, kn, v, tbl, sinks, bkt, gattn, after=None):
    n_b = SEQ // BLK

    def body(q_ref, k_ref, v_ref, tbl_ref, sink_ref, bkt_ref, ga_ref, y_ref, yn_ref, p_ref, ps_ref, bias_ref):
        step = pl.program_id(0)

        @pl.when(step == 0)
        def _():
            _band_bias(tbl_ref, bkt_ref[...], bias_ref)

        lane = lax.broadcasted_iota(jnp.int32, (BLK, 128), 1)
        for b in range(QUERY_BLOCKS_PER_STEP):
            i = QUERY_BLOCKS_PER_STEP * step + b
            rows = slice(BLK * b, BLK * (b + 1))
            kb, _, _ = _band_rows(k_ref, i)
            vb, _, _ = _band_rows(v_ref, i)
            upper, dead = _band_masks(i)
            q = q_ref[rows, :]
            outs = []
            psinks = jnp.zeros((BLK, 128), F32)
            for g in range(N_HEADS // GQA_GROUP):
                kv = slice(HEAD_DIM * g, HEAD_DIM * (g + 1))
                sink = _per_head_rows([sink_ref[0, GQA_GROUP * g + t] for t in range(GQA_GROUP)])
                probs, psink = _head_probs(_stack_heads(q, g), kb[:, kv], bias_ref[g], upper, dead, sink)
                p_ref[b, g] = probs.astype(BF16)
                for t in range(GQA_GROUP):
                    psinks = jnp.where(lane == GQA_GROUP * g + t, psink[BLK * t:BLK * (t + 1)], psinks)
                outs.append(_dot(_unfold(probs, upper).astype(BF16), vb[:, kv]))
            ps_ref[rows, :] = psinks
            y = _unstack_heads(outs)
            y_ref[rows, :] = y
            yn_ref[rows, :] = (y * _rstd(y) * ga_ref[...]).astype(BF16)

    const = lambda shape: pl.BlockSpec(shape, lambda i: (0,) * len(shape))
    rows = lambda w: pl.BlockSpec((QUERY_BLOCKS_PER_STEP * BLK, w), lambda i: (i, 0))
    smem = pl.BlockSpec(memory_space=pltpu.SMEM)
    body, more_specs, more = _ordered_behind(body, 7, after)
    return pl.pallas_call(
        body, name="attn_fwd", grid=(n_b // QUERY_BLOCKS_PER_STEP,),
        in_specs=[rows(ATTN_WIDTH), const((SEQ, KV_WIDTH)), const((SEQ, KV_WIDTH)), smem, smem,
                  const((BLK, BLK)), const((1, ATTN_WIDTH))] + more_specs,
        out_specs=[rows(ATTN_WIDTH), rows(ATTN_WIDTH),
                   pl.BlockSpec((QUERY_BLOCKS_PER_STEP, N_HEADS // GQA_GROUP, GROUP_ROWS, BLK), lambda i: (i, 0, 0, 0)),
                   rows(128)],
        out_shape=[jax.ShapeDtypeStruct((SEQ, ATTN_WIDTH), F32), jax.ShapeDtypeStruct((SEQ, ATTN_WIDTH), BF16),
                   jax.ShapeDtypeStruct((n_b, N_HEADS // GQA_GROUP, GROUP_ROWS, BLK), BF16),
                   jax.ShapeDtypeStruct((SEQ, 128), F32)],
        scratch_shapes=[pltpu.VMEM((N_HEADS // GQA_GROUP, GROUP_ROWS, BLK), F32)],
        compiler_params=_params(("arbitrary",)),
    )(qn, kn, v, tbl, sinks, bkt, gattn, *more)


def _ffn_block(i, step):
    return jnp.where(i % 2 == 0, step, N_FFN_BLK - 1 - step)


def _ffn_fwd(x, ycn, yan, w_out, g2, w_up, fcw, fcb, w_down, tgt):
    tm = 512
    n_t = SEQ // tm

    def body(x_ref, ycn_ref, yan_ref, wo_ref, g2_ref, wu_ref, cw_ref, b_ref, wd_ref, tgt_ref,
             h1_ref, u2_ref, up_ref, pre_ref, act_ref, dh2_ref, dh2b_ref, loss_ref, acc_ref, halo_ref):
        i, step = pl.program_id(0), pl.program_id(1)
        j = _ffn_block(i, step)

        @pl.when((i == 0) & (step == 0))
        def _():
            loss_ref[...] = jnp.zeros_like(loss_ref)

        @pl.when(step == 0)
        def _():
            h1 = x_ref[...] + _dot(ycn_ref[...], wo_ref[0:CONV_WIDTH, :]) + _dot(yan_ref[...], wo_ref[CONV_WIDTH:, :])
            h1_ref[...] = h1
            u2_ref[...] = (h1 * _rstd(h1) * g2_ref[...]).astype(BF16)
            acc_ref[...] = jnp.zeros_like(acc_ref)

        u2 = u2_ref[...]
        pre = []
        for s in range(2):
            up = _dot_nt(u2, wu_ref[s])
            up_ref[s] = up.astype(BF16)
            halo = jnp.where(i == 0, 0.0, halo_ref[s, j])
            pre.append(_conv3(up, _taps(cw_ref.at[s]), halo)[0] + b_ref[s])
            pre_ref[s] = pre[s].astype(BF16)
            halo_ref[s, j] = up[tm - SUBLANES:]
        g, val = pre
        act = (g * jax.nn.sigmoid(g) * val).astype(BF16)
        act_ref[...] = act
        acc_ref[...] += _dot(act, wd_ref[...])

        @pl.when(step == N_FFN_BLK - 1)
        def _():
            err = h1_ref[...] + acc_ref[...] - tgt_ref[...]
            loss_ref[...] += 0.5 * jnp.sum(err * err) / D_MODEL
            dh2 = err / D_MODEL
            dh2_ref[...] = dh2
            dh2b_ref[...] = dh2.astype(BF16)

    rows = lambda w: pl.BlockSpec((tm, w), lambda i, step: (i, 0))
    const = lambda shape: pl.BlockSpec(shape, lambda i, step: (0,) * len(shape))
    pair = lambda *s: pl.BlockSpec((2, None) + s, lambda i, step: (0, _ffn_block(i, step)) + (0,) * len(s))
    upb = pl.BlockSpec((2, None, tm, FFN_BLK), lambda i, step: (0, _ffn_block(i, step), i, 0))
    return pl.pallas_call(
        body, name="ffn_fwd", grid=(n_t, N_FFN_BLK),
        in_specs=[rows(D_MODEL), rows(CONV_WIDTH), rows(ATTN_WIDTH), const((D_MODEL, D_MODEL)), const((1, D_MODEL)),
                  pair(FFN_BLK, D_MODEL), pair(3, 1, FFN_BLK), pair(1, FFN_BLK),
                  pl.BlockSpec((None, FFN_BLK, D_MODEL), lambda i, step: (_ffn_block(i, step), 0, 0)), rows(D_MODEL)],
        out_specs=[rows(D_MODEL), rows(D_MODEL), upb, upb,
                   pl.BlockSpec((None, tm, FFN_BLK), lambda i, step: (_ffn_block(i, step), i, 0)),
                   rows(D_MODEL), rows(D_MODEL), const((SUBLANES, 128))],
        out_shape=[jax.ShapeDtypeStruct((SEQ, D_MODEL), F32), jax.ShapeDtypeStruct((SEQ, D_MODEL), BF16),
                   jax.ShapeDtypeStruct((2, N_FFN_BLK, SEQ, FFN_BLK), BF16),
                   jax.ShapeDtypeStruct((2, N_FFN_BLK, SEQ, FFN_BLK), BF16),
                   jax.ShapeDtypeStruct((N_FFN_BLK, SEQ, FFN_BLK), BF16),
                   jax.ShapeDtypeStruct((SEQ, D_MODEL), F32), jax.ShapeDtypeStruct((SEQ, D_MODEL), BF16),
                   jax.ShapeDtypeStruct((SUBLANES, 128), F32)],
        scratch_shapes=[pltpu.VMEM((tm, D_MODEL), F32), pltpu.VMEM((2, N_FFN_BLK, SUBLANES, FFN_BLK), F32)],
        compiler_params=_params(("arbitrary", "arbitrary")),
    )(x, ycn, yan, w_out, g2, w_up, fcw, fcb, w_down, tgt)


def _ffn_bwd(dh2, dh2b, h1, g2, up, pre, w_up, fcw, w_down, after=None):
    tm = 512
    units = ((288, 224), (0, 288))
    n_t = SEQ // tm

    def body(dh2_ref, dh2b_ref, h1_ref, g2_ref, up_ref, pre_ref, wu_ref, cw_ref, wd_ref,
             dup_ref, dh1_ref, dh1b_ref, dfb_ref, dfcw_ref, dg2_ref, acc_ref, next_ref):
        j, i = pl.program_id(0), pl.program_id(1)
        tile = pl.ds(pl.multiple_of((n_t - 1 - i) * tm, tm), tm)

        @pl.when((j == 0) & (i == 0))
        def _():
            dfb_ref[...] = jnp.zeros_like(dfb_ref)
            dfcw_ref[...] = jnp.zeros_like(dfcw_ref)
            dg2_ref[...] = jnp.zeros_like(dg2_ref)

        @pl.when(j == 0)
        def _():
            acc_ref[tile, :] = jnp.zeros((tm, D_MODEL), F32)

        nxt = [jnp.where(i == 0, 0.0, next_ref[s]) for s in range(2)]
        sums = [[0.0] * 4 for _ in range(2)]
        for r0, rn in units:
            rows = slice(r0, r0 + rn)
            g, val = pre_ref[0, rows, :].astype(F32), pre_ref[1, rows, :].astype(F32)
            sg = jax.nn.sigmoid(g)
            silu = g * sg
            dact = _dot_nt(dh2b_ref[rows, :], wd_ref[...])
            dpre = (dact * val * (sg * (1.0 + g * (1.0 - sg))), dact * silu)
            dups = []
            for s in range(2):
                d = dpre[s]
                u = up_ref[s, rows, :].astype(F32)
                w = _taps(cw_ref.at[s])
                d1 = _shift_up(d, 1, nxt[s])
                d2 = _shift_up(d, 2, nxt[s])
                nxt[s] = d[:SUBLANES]
                for t, term in enumerate((d, d2 * u, d1 * u, d * u)):
                    sums[s][t] = sums[s][t] + jnp.sum(term, axis=0, keepdims=True)
                dups.append((d * w[2] + d1 * w[1] + d2 * w[0]).astype(BF16))
                dup_ref[s, rows, :] = dups[s]
            acc_rows = pl.ds(pl.multiple_of((n_t - 1 - i) * tm + r0, SUBLANES), rn)
            acc_ref[acc_rows, :] += _dot(dups[0], wu_ref[0]) + _dot(dups[1], wu_ref[1])
        for s in range(2):
            next_ref[s] = nxt[s]
            dfb_ref[s, j] += sums[s][0]
            for t in range(3):
                dfcw_ref[s, j, t] += sums[s][1 + t]

        @pl.when(j == N_FFN_BLK - 1)
        def _():
            dn, dgain = _rms_bwd(h1_ref[...], g2_ref[...], acc_ref[tile, :])
            dh1 = dh2_ref[...] + dn
            dh1_ref[...] = dh1
            dh1b_ref[...] = dh1.astype(BF16)
            dg2_ref[...] += dgain

    rev = lambda i: n_t - 1 - i
    rows = lambda w: pl.BlockSpec((tm, w), lambda j, i: (rev(i), 0))
    last_rows = lambda w: pl.BlockSpec((tm, w), lambda j, i: (jnp.where(j == N_FFN_BLK - 1, rev(i), rev(0)), 0))
    const = lambda shape: pl.BlockSpec(shape, lambda j, i: (0,) * len(shape))
    pair = lambda *s: pl.BlockSpec((2, None) + s, lambda j, i: (0, j) + (0,) * len(s))
    upb = pl.BlockSpec((2, None, tm, FFN_BLK), lambda j, i: (0, j, rev(i), 0))
    body, more_specs, more = _ordered_behind(body, 9, after)
    return pl.pallas_call(
        body, name="ffn_bwd", grid=(N_FFN_BLK, n_t),
        in_specs=[last_rows(D_MODEL), rows(D_MODEL), last_rows(D_MODEL), const((1, D_MODEL)), upb, upb,
                  pair(FFN_BLK, D_MODEL), pair(3, 1, FFN_BLK),
                  pl.BlockSpec((None, FFN_BLK, D_MODEL), lambda j, i: (j, 0, 0))] + more_specs,
        out_specs=[upb, last_rows(D_MODEL), last_rows(D_MODEL),
                   const((2, N_FFN_BLK, 1, FFN_BLK)), const((2, N_FFN_BLK, 3, 1, FFN_BLK)), const((1, D_MODEL))],
        out_shape=[jax.ShapeDtypeStruct((2, N_FFN_BLK, SEQ, FFN_BLK), BF16), jax.ShapeDtypeStruct((SEQ, D_MODEL), F32),
                   jax.ShapeDtypeStruct((SEQ, D_MODEL), BF16), jax.ShapeDtypeStruct((2, N_FFN_BLK, 1, FFN_BLK), F32),
                   jax.ShapeDtypeStruct((2, N_FFN_BLK, 3, 1, FFN_BLK), F32), jax.ShapeDtypeStruct((1, D_MODEL), F32)],
        scratch_shapes=[pltpu.VMEM((SEQ, D_MODEL), F32), pltpu.VMEM((2, SUBLANES, FFN_BLK), F32)],
        compiler_params=_params(("arbitrary", "arbitrary")),
    )(dh2, dh2b, h1, g2, up, pre, w_up, fcw, w_down, *more)


def _grad_tn(a_list, b, out_rows, name, after=None):
    n = len(a_list)
    ncol = b.shape[1]

    def body(*refs):
        a_refs, b_ref, o_ref = refs[:n], refs[n], refs[n + 1]
        j = pl.program_id(0)
        for k in range(n):
            @pl.when(j == k)
            def _(k=k):
                o_ref[...] = _dot_tn(a_refs[k][...], b_ref[...]).astype(BF16)

    full = lambda shape: pl.BlockSpec(shape, lambda j: (0,) * len(shape))
    body, more_specs, more = _ordered_behind(body, n + 1, after)
    return pl.pallas_call(
        body, name=name, grid=(n,),
        in_specs=[full((SEQ, out_rows))] * n + [full((SEQ, ncol))] + more_specs,
        out_specs=pl.BlockSpec((None, out_rows, ncol), lambda j: (j, 0, 0)),
        out_shape=jax.ShapeDtypeStruct((n, out_rows, ncol), BF16),
        compiler_params=_params(("arbitrary",)),
    )(*a_list, b, *more)


def _grad_tn_blocked(a, b, name, per_step=2):
    nb, _, a_w = a.shape
    b_w = b.shape[-1]

    def body(a_ref, b_ref, o_ref):
        for p in range(per_step):
            o_ref[p] = _dot_tn(a_ref[p], b_ref[...]).astype(BF16)

    return pl.pallas_call(
        body, name=name, grid=(nb // per_step,),
        in_specs=[pl.BlockSpec((per_step, SEQ, a_w), lambda k: (k, 0, 0)), pl.BlockSpec((SEQ, b_w), lambda k: (0, 0))],
        out_specs=pl.BlockSpec((per_step, a_w, b_w), lambda k: (k, 0, 0)),
        out_shape=jax.ShapeDtypeStruct((nb, a_w, b_w), BF16),
        compiler_params=_params(("arbitrary",)),
    )(a, b)


def _out_bwd(dh1b, w_out, y_attn, gattn, after=None):
    tm = 1024
    n_t = SEQ // tm

    def body(dh_ref, wo_ref, y_ref, ga_ref, dycn_ref, dy_ref, dga_ref):
        @pl.when(pl.program_id(0) == 0)
        def _():
            dga_ref[...] = jnp.zeros_like(dga_ref)

        dycat = _dot_nt(dh_ref[...], wo_ref[...])
        dycn_ref[...] = dycat[:, :CONV_WIDTH]
        dy, dga = _rms_bwd(y_ref[...], ga_ref[...], dycat[:, CONV_WIDTH:])
        dy_ref[...] = dy
        dga_ref[...] += dga

    rows = lambda w: pl.BlockSpec((tm, w), lambda i: (i, 0))
    const = lambda shape: pl.BlockSpec(shape, lambda i: (0,) * len(shape))
    body, more_specs, more = _ordered_behind(body, 4, after)
    return pl.pallas_call(
        body, name="out_bwd", grid=(n_t,),
        in_specs=[rows(D_MODEL), const((D_MODEL, D_MODEL)), rows(ATTN_WIDTH), const((1, ATTN_WIDTH))] + more_specs,
        out_specs=[rows(CONV_WIDTH), rows(ATTN_WIDTH), const((1, ATTN_WIDTH))],
        out_shape=[jax.ShapeDtypeStruct((SEQ, CONV_WIDTH), F32), jax.ShapeDtypeStruct((SEQ, ATTN_WIDTH), F32),
                   jax.ShapeDtypeStruct((1, ATTN_WIDTH), F32)],
        compiler_params=_params(("arbitrary",)),
    )(dh1b, w_out, y_attn, gattn, *more)


def _attn_bwd(qn, kn, v, dy, probs, psinks, bkt, after=None):
    n_b = SEQ // BLK

    def body(q_ref, k_ref, v_ref, dy_ref, p_ref, ps_ref, bkt_ref,
             dq_ref, dk_ref, dv_ref, dtbl_ref, dsink_ref, dbias_ref, dsacc_ref):
        step = pl.program_id(0)

        @pl.when(step == 0)
        def _():
            dbias_ref[...] = jnp.zeros_like(dbias_ref)
            dsacc_ref[...] = jnp.zeros_like(dsacc_ref)
            dk_ref[...] = jnp.zeros_like(dk_ref)
            dv_ref[...] = jnp.zeros_like(dv_ref)

        lane = lax.broadcasted_iota(jnp.int32, (BLK, 128), 1)
        for blk in range(QUERY_BLOCKS_PER_STEP):
            i = QUERY_BLOCKS_PER_STEP * step + blk
            rows = slice(BLK * blk, BLK * (blk + 1))
            kb, prev, cur = _band_rows(k_ref, i)
            vb, _, _ = _band_rows(v_ref, i)
            upper, _ = _band_masks(i)
            q = q_ref[rows, :]
            dy = dy_ref[rows, :]
            psink = ps_ref[rows, :]
            dsink = jnp.zeros((BLK, 128), F32)
            dqs, dks, dvs = [], [], []
            for g in range(N_HEADS // GQA_GROUP):
                kv = slice(HEAD_DIM * g, HEAD_DIM * (g + 1))
                qg = _stack_heads(q, g)
                dog = _stack_heads(dy, g).astype(BF16)
                pb = p_ref[blk, g]
                pg = pb.astype(F32)
                dprobs = _fold(_dot_nt(dog, vb[:, kv]), upper)
                dvs.append(_dot_tn(_unfold(pb, upper), dog))
                dsum = jnp.sum(pg * dprobs, axis=-1, keepdims=True)
                dlogits = pg * (dprobs - dsum)
                for t in range(GQA_GROUP):
                    dsink = jnp.where(lane == GQA_GROUP * g + t, -psink * dsum[BLK * t:BLK * (t + 1)], dsink)
                dbias_ref[g] += dlogits
                ds = _unfold(dlogits * (HEAD_DIM ** -0.5), upper).astype(BF16)
                dqs.append(_dot(ds, kb[:, kv]))
                dks.append(_dot_tn(ds, qg))
            dsacc_ref[...] += dsink
            dq_ref[rows, :] = _unstack_heads(dqs)
            dkb = jnp.concatenate(dks, axis=-1)
            dvb = jnp.concatenate(dvs, axis=-1)
            dk_ref[pl.ds(prev, BLK), :] += dkb[:BLK]
            dk_ref[pl.ds(cur, BLK), :] += dkb[BLK:]
            dv_ref[pl.ds(prev, BLK), :] += dvb[:BLK]
            dv_ref[pl.ds(cur, BLK), :] += dvb[BLK:]

        @pl.when(step == n_b // QUERY_BLOCKS_PER_STEP - 1)
        def _():
            bkt = bkt_ref[...]
            row8 = lax.broadcasted_iota(jnp.int32, (N_HEADS, 128), 0)
            lane8 = lax.broadcasted_iota(jnp.int32, (N_HEADS, 128), 1)
            acc = jnp.zeros((N_HEADS, 128), F32)
            for h in range(N_HEADS):
                rows = slice(BLK * (h % GQA_GROUP), BLK * (h % GQA_GROUP + 1))
                dbh = dbias_ref[h // GQA_GROUP, rows, :]
                for b in range(NUM_BUCKETS):
                    acc = jnp.where((row8 == h) & (lane8 == b), jnp.sum(jnp.where(bkt == b, dbh, 0.0)), acc)
            dsink_ref[...] = jnp.sum(dsacc_ref[...], axis=0, keepdims=True)
            dtbl_ref[...] = acc

    const = lambda shape: pl.BlockSpec(shape, lambda i: (0,) * len(shape))
    rows = lambda w: pl.BlockSpec((QUERY_BLOCKS_PER_STEP * BLK, w), lambda i: (i, 0))
    n_g = N_HEADS // GQA_GROUP
    body, more_specs, more = _ordered_behind(body, 7, after)
    return pl.pallas_call(
        body, name="attn_bwd", grid=(n_b // QUERY_BLOCKS_PER_STEP,),
        in_specs=[rows(ATTN_WIDTH), const((SEQ, KV_WIDTH)), const((SEQ, KV_WIDTH)), rows(ATTN_WIDTH),
                  pl.BlockSpec((QUERY_BLOCKS_PER_STEP, n_g, GROUP_ROWS, BLK), lambda i: (i, 0, 0, 0)), rows(128),
                  const((BLK, BLK))] + more_specs,
        out_specs=[rows(ATTN_WIDTH), const((SEQ, KV_WIDTH)), const((SEQ, KV_WIDTH)), const((N_HEADS, 128)), const((1, 128))],
        out_shape=[jax.ShapeDtypeStruct((SEQ, ATTN_WIDTH), F32), jax.ShapeDtypeStruct((SEQ, KV_WIDTH), F32),
                   jax.ShapeDtypeStruct((SEQ, KV_WIDTH), F32), jax.ShapeDtypeStruct((N_HEADS, 128), F32),
                   jax.ShapeDtypeStruct((1, 128), F32)],
        scratch_shapes=[pltpu.VMEM((n_g, GROUP_ROWS, BLK), F32), pltpu.VMEM((BLK, 128), F32)],
        compiler_params=_params(("arbitrary",)),
    )(qn, kn, v, dy, probs, psinks, bkt, *more)


def _mix_in_bwd(x, dh1, proj, dycn, dqn, dkn, dv, w_in_t, conv_w, g1, gq, gk, gconv):
    tm = 512
    n_t = SEQ // tm
    halo_blocks = tm // SUBLANES

    def body(x_ref, dh1_ref, proj_ref, halo_ref, dycn_ref, dqn_ref, dkn_ref, dv_ref, w_ref, cw_ref,
             g1_ref, gq_ref, gk_ref, gc_ref,
             dx_ref, dproj_ref, dcw_ref, dgc_ref, dgq_ref, dgk_ref, dg1_ref, next_ref):
        i = pl.program_id(0)
        first_tile = i == n_t - 1

        @pl.when(i == 0)
        def _():
            for r in (dcw_ref, dgc_ref, dgq_ref, dgk_ref, dg1_ref, next_ref):
                r[...] = jnp.zeros_like(r)

        proj = proj_ref[...]
        hp = halo_ref[...]
        gate_b = proj[:, 0:CONV_WIDTH]
        gate_c = proj[:, CONV_WIDTH:2 * CONV_WIDTH]
        hc = proj[:, 2 * CONV_WIDTH:3 * CONV_WIDTH]
        a = gate_c * hc
        a_halo = jnp.where(first_tile, 0.0, hp[:, CONV_WIDTH:2 * CONV_WIDTH] * hp[:, 2 * CONV_WIDTH:3 * CONV_WIDTH])
        cw = _gathered_taps(cw_ref)
        cv, a2, a1 = _conv3(a, cw, a_halo)
        dyc, dgc = _rms_bwd(gate_b * cv, gc_ref[...], dycn_ref[...])
        dgc_ref[...] += dgc
        dcv = dyc * gate_b
        dcw_ref[...] += jnp.concatenate(
            [jnp.sum(dcv * a2, axis=0, keepdims=True), jnp.sum(dcv * a1, axis=0, keepdims=True),
             jnp.sum(dcv * a, axis=0, keepdims=True)], axis=0)
        da = _conv3_bwd_input(dcv, cw, next_ref[...])
        next_ref[...] = dcv[:SUBLANES]
        q0 = 3 * CONV_WIDTH
        k0 = q0 + ATTN_WIDTH
        dq, dgq = _head_norm_bwd(proj[:, q0:k0], gq_ref[...], dqn_ref[...], N_HEADS)
        dk, dgk = _head_norm_bwd(proj[:, k0:k0 + KV_WIDTH], gk_ref[...], dkn_ref[...], 2)
        dgq_ref[...] += dgq
        dgk_ref[...] += dgk
        dproj = jnp.concatenate([dyc * cv, da * hc, da * gate_c, dq, dk, dv_ref[...]], axis=-1).astype(BF16)
        dproj_ref[...] = dproj
        du1 = _dot(dproj, w_ref[...])
        xv = x_ref[...]
        dn, dg1 = _rms_bwd(xv, g1_ref[...], du1)
        dx_ref[...] = dh1_ref[...] + dn
        dg1_ref[...] += dg1

    rev = lambda i: n_t - 1 - i
    rows = lambda w: pl.BlockSpec((tm, w), lambda i: (rev(i), 0))
    const = lambda shape: pl.BlockSpec(shape, lambda i: (0,) * len(shape))
    halo = pl.BlockSpec((SUBLANES, IN_WIDTH), lambda i: (jnp.maximum(rev(i) * halo_blocks - 1, 0), 0))
    return pl.pallas_call(
        body, name="mix_in_bwd", grid=(n_t,),
        in_specs=[rows(D_MODEL), rows(D_MODEL), rows(IN_WIDTH), halo, rows(CONV_WIDTH), rows(ATTN_WIDTH), rows(KV_WIDTH),
                  rows(KV_WIDTH), const((IN_WIDTH, D_MODEL)), const((N_DEV, 3, 1, CONV_WIDTH // N_DEV)), const((1, D_MODEL)),
                  const((1, HEAD_DIM)), const((1, HEAD_DIM)), const((1, CONV_WIDTH))],
        out_specs=[rows(D_MODEL), rows(IN_WIDTH), const((3, CONV_WIDTH)), const((1, CONV_WIDTH)),
                   const((1, HEAD_DIM)), const((1, HEAD_DIM)), const((1, D_MODEL))],
        out_shape=[jax.ShapeDtypeStruct((SEQ, D_MODEL), F32), jax.ShapeDtypeStruct((SEQ, IN_WIDTH), BF16),
                   jax.ShapeDtypeStruct((3, CONV_WIDTH), F32),
                   jax.ShapeDtypeStruct((1, CONV_WIDTH), F32), jax.ShapeDtypeStruct((1, HEAD_DIM), F32),
                   jax.ShapeDtypeStruct((1, HEAD_DIM), F32), jax.ShapeDtypeStruct((1, D_MODEL), F32)],
        scratch_shapes=[pltpu.VMEM((SUBLANES, CONV_WIDTH), F32)],
        compiler_params=_params(("arbitrary",)),
    )(x, dh1, proj, proj, dycn, dqn, dkn, dv, w_in_t, conv_w, g1, gq, gk, gconv)


def _grad_w_in(dproj, u1, after=None):
    bw = 768

    def body(a_ref, b_ref, o_ref):
        o_ref[...] = _dot_tn(a_ref[...], b_ref[...]).astype(BF16)

    body, more_specs, more = _ordered_behind(body, 2, after)
    return pl.pallas_call(
        body, name="grad_w_in", grid=(IN_WIDTH // bw,),
        in_specs=[pl.BlockSpec((SEQ, bw), lambda k: (0, k)), pl.BlockSpec((SEQ, D_MODEL), lambda k: (0, 0))] + more_specs,
        out_specs=pl.BlockSpec((bw, D_MODEL), lambda k: (k, 0)),
        out_shape=jax.ShapeDtypeStruct((IN_WIDTH, D_MODEL), BF16),
        compiler_params=_params(("arbitrary",)),
    )(dproj, u1, *more)


def _adamw_math(w, g, m, v):
    m = ADAM_B1 * m + (1.0 - ADAM_B1) * g
    v = ADAM_B2 * v + (1.0 - ADAM_B2) * (g * g)
    m_hat = m / (1.0 - ADAM_B1 ** ADAM_STEP)
    v_hat = v / (1.0 - ADAM_B2 ** ADAM_STEP)
    return -ADAM_LR * (m_hat / (jnp.sqrt(v_hat) + ADAM_EPS) + ADAM_WD * w), m, v


_ROW_G1, _ROW_G2, _ROW_OUT_NORMS, _ROW_FFN_B, _ROW_GQ, _ROW_GK, _ROW_SINKS, _ROW_LOSS, _ROW_TABLE = 0, 1, 2, 3, 11, 12, 13, 14, 16
SMALL_ROWS, SMALL_COLS = 24, 1024
_SMALL_NAMES = ("norm_mix_g", "norm_ffn_g", "out_norm_conv_g", "out_norm_attn_g", "ffn_conv_b", "q_norm_g", "k_norm_g",
                "sinks", "rel_bias_table")


def _pack_small_grads(dg1, dg2, dgconv, dgattn, dfb, dgq, dgk, dsinks, dtbl_t, loss_acc, after=None):
    def body(dg1_ref, dg2_ref, dgc_ref, dga_ref, dfb_ref, dgq_ref, dgk_ref, ds_ref, dt_ref, loss_ref, o_ref, all_ref):
        o_ref[...] = jnp.zeros_like(o_ref)
        o_ref[_ROW_G1:_ROW_G1 + 1, :] = dg1_ref[...]
        o_ref[_ROW_G2:_ROW_G2 + 1, :] = dg2_ref[...]
        o_ref[_ROW_OUT_NORMS:_ROW_OUT_NORMS + 1, 0:CONV_WIDTH] = dgc_ref[...]
        o_ref[_ROW_OUT_NORMS:_ROW_OUT_NORMS + 1, CONV_WIDTH:] = dga_ref[...]
        for k in range(N_DEV):
            o_ref[_ROW_FFN_B + k:_ROW_FFN_B + k + 1, 0:FFN_BLK] = dfb_ref[k // N_FFN_BLK, k % N_FFN_BLK]
        o_ref[_ROW_GQ:_ROW_GQ + 1, 0:HEAD_DIM] = dgq_ref[...]
        o_ref[_ROW_GK:_ROW_GK + 1, 0:HEAD_DIM] = dgk_ref[...]
        o_ref[_ROW_SINKS:_ROW_SINKS + 1, 0:128] = ds_ref[...]
        o_ref[_ROW_LOSS:_ROW_LOSS + 1, 0:128] = loss_ref[0:1, :]
        o_ref[_ROW_TABLE:_ROW_TABLE + N_HEADS, 0:128] = dt_ref[...]
        for s in range(N_DEV):
            all_ref[s] = o_ref[...]

    body, more_specs, more = _ordered_behind(body, 10, after)
    return pl.pallas_call(
        body, name="pack_small_grads",
        in_specs=[pl.BlockSpec(memory_space=pltpu.VMEM)] * 10 + more_specs,
        out_shape=[jax.ShapeDtypeStruct((SMALL_ROWS, SMALL_COLS), F32),
                   jax.ShapeDtypeStruct((N_DEV, SMALL_ROWS, SMALL_COLS), F32)],
    )(dg1, dg2, dgconv, dgattn, dfb, dgq, dgk, dsinks, dtbl_t, loss_acc, *more)


def _adamw_small(recv, params, after):
    names = _SMALL_NAMES
    n = len(names)

    def grad_of(g, name, k=None):
        if name == "norm_mix_g":
            return g[_ROW_G1:_ROW_G1 + 1, :]
        if name == "norm_ffn_g":
            return g[_ROW_G2:_ROW_G2 + 1, :]
        if name == "out_norm_conv_g":
            return g[_ROW_OUT_NORMS:_ROW_OUT_NORMS + 1, 0:CONV_WIDTH]
        if name == "out_norm_attn_g":
            return g[_ROW_OUT_NORMS:_ROW_OUT_NORMS + 1, CONV_WIDTH:]
        if name == "ffn_conv_b":
            return g[_ROW_FFN_B + k:_ROW_FFN_B + k + 1, 0:FFN_BLK]
        if name == "q_norm_g":
            return g[_ROW_GQ:_ROW_GQ + 1, 0:HEAD_DIM]
        if name == "k_norm_g":
            return g[_ROW_GK:_ROW_GK + 1, 0:HEAD_DIM]
        if name == "sinks":
            return g[_ROW_SINKS:_ROW_SINKS + 1, 0:N_HEADS]
        return g[_ROW_TABLE:_ROW_TABLE + N_HEADS, 0:NUM_BUCKETS]

    def body(r_ref, *refs):
        ins, outs, loss_ref = refs[:3 * n], refs[3 * n:7 * n], refs[7 * n]
        g = r_ref[0]
        for s in range(1, N_DEV):
            g = g + r_ref[s]
        loss_ref[...] = g[_ROW_LOSS:_ROW_LOSS + 1, 0:128]
        for i, name in enumerate(names):
            w_ref, m_ref, v_ref = ins[3 * i:3 * i + 3]
            o = outs[4 * i:4 * i + 4]
            cols = [slice(FFN_BLK * k, FFN_BLK * (k + 1)) for k in range(N_DEV)] if name == "ffn_conv_b" else [slice(None)]
            for k, cs in enumerate(cols):
                gk = grad_of(g, name, k)
                d, m2, v2 = _adamw_math(w_ref[:, cs], gk, m_ref[:, cs], v_ref[:, cs])
                o[0][:, cs], o[1][:, cs], o[2][:, cs], o[3][:, cs] = gk, d, m2, v2

    flat = [a for name in names for a in params[name]]
    body, more_specs, more = _ordered_behind(body, 1 + 3 * n, after)
    vmem = pl.BlockSpec(memory_space=pltpu.VMEM)
    out = pl.pallas_call(
        body, name="adamw_small",
        in_specs=[vmem] * (1 + 3 * n) + more_specs,
        out_shape=[jax.ShapeDtypeStruct(params[name][0].shape, F32) for name in names for _ in range(4)]
        + [jax.ShapeDtypeStruct((1, 128), F32)],
        compiler_params=pltpu.CompilerParams(vmem_limit_bytes=VMEM_LIMIT),
    )(recv, *flat, *more)
    return {name: tuple(out[4 * i:4 * i + 4]) for i, name in enumerate(names)}, out[4 * n]


def _adamw_direct(w, m, v, own, recv, me, name, row_blocks=1, after=None):
    rb = w.shape[0] // row_blocks
    cols = w.shape[1]

    def body(me_ref, w_ref, m_ref, v_ref, o_ref, r_ref, g_o, d_o, m_o, v_o):
        g = o_ref[...].astype(F32)
        for s in range(N_DEV - 1):
            g = g + r_ref[s].astype(F32)
        g_o[...] = g
        d_o[...], m_o[...], v_o[...] = _adamw_math(w_ref[...], g, m_ref[...], v_ref[...])

    blk = pl.BlockSpec((rb, cols), lambda i, me_ref: (i, 0))
    oblk = pl.BlockSpec((None, rb, cols), lambda i, me_ref: (me_ref[0], i, 0))
    rblk = pl.BlockSpec((N_DEV - 1, rb, cols), lambda i, me_ref: (0, i, 0))
    body, more_specs, more = _ordered_behind(body, 6, after)
    return pl.pallas_call(
        body, name=name,
        grid_spec=pltpu.PrefetchScalarGridSpec(num_scalar_prefetch=1, grid=(row_blocks,),
                                               in_specs=[blk, blk, blk, oblk, rblk] + more_specs, out_specs=[blk] * 4),
        out_shape=[jax.ShapeDtypeStruct(w.shape, F32)] * 4,
        compiler_params=_params(("arbitrary",)),
    )(me, w, m, v, own, recv, *more)


def _adamw(w, m, v, part, recv, chip, name, row_blocks=1, after=None):
    rb = w.shape[0] // row_blocks
    tail = w.shape[1:]
    zeros = (0,) * len(tail)

    def body(chip_ref, w_ref, m_ref, v_ref, p_ref, r_ref, g_o, d_o, m_o, v_o):
        g = p_ref[...].astype(F32)
        for s in range(3):
            g = g + r_ref[s].astype(F32)
        g_o[...] = g
        d_o[...], m_o[...], v_o[...] = _adamw_math(w_ref[...], g, m_ref[...], v_ref[...])

    blk = pl.BlockSpec((rb,) + tail, lambda i, chip_ref: (i,) + zeros)
    pblk = pl.BlockSpec((None, rb) + tail, lambda i, chip_ref: (chip_ref[0], i) + zeros)
    rblk = pl.BlockSpec((3, rb) + tail, lambda i, chip_ref: (0, i) + zeros)
    body, more_specs, more = _ordered_behind(body, 6, after)
    return pl.pallas_call(
        body, name=name,
        grid_spec=pltpu.PrefetchScalarGridSpec(num_scalar_prefetch=1, grid=(row_blocks,),
                                               in_specs=[blk, blk, blk, pblk, rblk] + more_specs, out_specs=[blk] * 4),
        out_shape=[jax.ShapeDtypeStruct(w.shape, F32)] * 4,
        compiler_params=_params(("arbitrary",)),
    )(chip, w, m, v, part, recv, *more)


def kernel(x, norm_mix_g, w_in, conv_w, q_norm_g, k_norm_g, rel_bias_table, sinks, out_norm_conv_g, out_norm_attn_g, w_out, norm_ffn_g, w_up, ffn_conv_w, ffn_conv_b, w_down, loss_target, m_norm_mix_g, m_w_in, m_conv_w, m_q_norm_g, m_k_norm_g, m_rel_bias_table, m_sinks, m_out_norm_conv_g, m_out_norm_attn_g, m_w_out, m_norm_ffn_g, m_w_up, m_ffn_conv_w, m_ffn_conv_b, m_w_down, v_norm_mix_g, v_w_in, v_conv_w, v_q_norm_g, v_k_norm_g, v_rel_bias_table, v_sinks, v_out_norm_conv_g, v_out_norm_attn_g, v_w_out, v_norm_ffn_g, v_w_up, v_ffn_conv_w, v_ffn_conv_b, v_w_down):
    p = dict(norm_mix_g=norm_mix_g, w_in=w_in, conv_w=conv_w, q_norm_g=q_norm_g, k_norm_g=k_norm_g,
             rel_bias_table=rel_bias_table, sinks=sinks, out_norm_conv_g=out_norm_conv_g, out_norm_attn_g=out_norm_attn_g,
             w_out=w_out, norm_ffn_g=norm_ffn_g, w_up=w_up, ffn_conv_w=ffn_conv_w, ffn_conv_b=ffn_conv_b, w_down=w_down)
    m = dict(norm_mix_g=m_norm_mix_g, w_in=m_w_in, conv_w=m_conv_w, q_norm_g=m_q_norm_g, k_norm_g=m_k_norm_g,
             rel_bias_table=m_rel_bias_table, sinks=m_sinks, out_norm_conv_g=m_out_norm_conv_g,
             out_norm_attn_g=m_out_norm_attn_g, w_out=m_w_out, norm_ffn_g=m_norm_ffn_g, w_up=m_w_up,
             ffn_conv_w=m_ffn_conv_w, ffn_conv_b=m_ffn_conv_b, w_down=m_w_down)
    v = dict(norm_mix_g=v_norm_mix_g, w_in=v_w_in, conv_w=v_conv_w, q_norm_g=v_q_norm_g, k_norm_g=v_k_norm_g,
             rel_bias_table=v_rel_bias_table, sinks=v_sinks, out_norm_conv_g=v_out_norm_conv_g,
             out_norm_attn_g=v_out_norm_attn_g, w_out=v_w_out, norm_ffn_g=v_norm_ffn_g, w_up=v_w_up,
             ffn_conv_w=v_ffn_conv_w, ffn_conv_b=v_ffn_conv_b, w_down=v_w_down)

    xs, tgt = x[0], loss_target[0]
    g1, g2, gq, gk, gconv, gattn = norm_mix_g, norm_ffn_g, q_norm_g, k_norm_g, out_norm_conv_g, out_norm_attn_g
    ix, iy, ic = _coords()
    core = ic.astype(jnp.int32).reshape(1)
    chip = (2 * ix + iy).astype(jnp.int32).reshape(1)
    me = _lin(ix, iy, ic).astype(jnp.int32).reshape(1)
    bkt = jnp.asarray(_bucket_map())
    tr = lambda a: a[0].T
    taps = lambda a: jnp.transpose(a, (1, 0, 2))
    tbl_t = rel_bias_table.T

    wi_l, cw_l = _place_shards(me, [tr(w_in), taps(conv_w)], [BF16, F32], "place_mixer_shards")
    finish_a, token_a = _all_gather_split([wi_l, cw_l], "mixer", None)
    wo_l, wu_l, wd_l, fcw_l = _place_shards(me, [w_out[0], tr(w_up), w_down[0], taps(ffn_conv_w)],
                                            [BF16, BF16, BF16, F32], "place_ffn_shards", after=token_a)
    ffn_stage2, ffn_stage3, token_b = _all_gather_tree([wo_l, wu_l, wd_l, fcw_l], "ffn", token_a)
    wi_g, cw_g = finish_a(token_b)
    w_in_t = wi_g.reshape(IN_WIDTH, D_MODEL)

    proj, u1, ycn, qn, kn, vv = _mix_in_fwd(xs, g1, w_in_t, cw_g, gq, gk, gconv)
    token_b2 = ffn_stage2(ycn)
    y_attn, yan, probs, psinks = _attn_fwd(qn, kn, vv, tbl_t, sinks, bkt, gattn, after=token_b2)
    wo_g, wu_g, wd_g, fcw_g = ffn_stage3(yan)
    w_out_f = wo_g.reshape(D_MODEL, D_MODEL)
    w_down_f = wd_g.reshape(N_FFN_BLK, FFN_BLK, D_MODEL)
    w_up_f = wu_g.reshape(2, N_FFN_BLK, FFN_BLK, D_MODEL)
    fcw_f = fcw_g.reshape(2, N_FFN_BLK, 3, 1, FFN_BLK)
    fcb = ffn_conv_b.reshape(2, N_FFN_BLK, 1, FFN_BLK)
    h1, u2, up, pre, act, dh2, dh2b, loss_acc = _ffn_fwd(xs, ycn, yan, w_out_f, g2, w_up_f, fcw_f, fcb, w_down_f, tgt)

    dw_down = _grad_tn_blocked(act, dh2b, "grad_w_down").reshape(N_DEV, D_FF // N_DEV, D_MODEL)
    plan_d, slots_d = _scatter_plan(1)
    d_sem = _split_start("scatter_w_down_start", [dw_down], [lax.empty((N_DEV - 1,) + dw_down.shape[1:], BF16)],
                         plan_d, None, _ALL_FOR_W_DOWN)
    dup, dh1, dh1b, dfb, dfcw, dg2 = _ffn_bwd(dh2, dh2b, h1, g2, up, pre, w_up_f, fcw_f, w_down_f, after=d_sem[4])
    dw_up = _grad_tn_blocked(dup.reshape(N_DEV, SEQ, FFN_BLK), u2, "grad_w_up")
    dw_out = _grad_tn([ycn, yan], dh1b, CONV_WIDTH, "grad_w_out").reshape(N_DEV, D_MODEL // N_DEV, D_MODEL)
    out_bwd = {}

    def behind_ffn(token):
        out_bwd["r"] = _out_bwd(dh1b, w_out_f, y_attn, gattn, after=token)
        return out_bwd["r"][0]

    finish_ffn, token_ffn = _reduce_scatter_split(
        [dw_up, dw_out, dfcw.reshape(N_DEV, 3, 1, FFN_BLK)], "ffn", core, behind_ffn)
    dycn, dy_attn, dgattn = out_bwd["r"]
    dqn, dkn, dv, dtbl_t, dsinks = _attn_bwd(qn, kn, vv, dy_attn, probs, psinks, bkt, after=token_ffn)
    dx, dproj, dcw, dgconv, dgq, dgk, dg1 = _mix_in_bwd(xs, dh1, proj, dycn, dqn, dkn, dv, w_in_t, cw_g,
                                                         g1, gq, gk, gconv)
    dw_in_t = _grad_w_in(dproj, u1).reshape(N_DEV, IN_WIDTH // N_DEV, D_MODEL)
    dcw_b = jnp.transpose(dcw.reshape(3, N_DEV, 1, CONV_WIDTH // N_DEV), (1, 0, 2, 3))
    plan_s, slots_s = _broadcast_plan()
    adam = {}
    ffn_got = {}
    small = {}

    def behind_mixer(token):
        packed, packed_all = _pack_small_grads(dg1, dg2, dgconv, dgattn, dfb, dgq, dgk, dsinks, dtbl_t, loss_acc,
                                               after=token)
        small["r"] = _split_start("gather_small_start", [packed], [packed_all], plan_s, None, _ALL_FOR_SMALL)
        ffn_got["r"] = finish_ffn(small["r"][4])
        return ffn_got["r"][1][0]

    finish_mixer, token_mixer = _reduce_scatter_split([dw_in_t, dcw_b], "mixer", core, behind_mixer)
    s_sem, r_sem, src_s, land_s, _ = small["r"]
    (p_wu, p_wo, p_fcw), (r_wu, r_wo, r_fcw) = ffn_got["r"]
    (own_wd,), (r_wd,) = _split_wait("scatter_w_down_wait", d_sem[0], d_sem[1], d_sem[2], d_sem[3], plan_d, slots_d,
                                     token_mixer)
    adam["w_down"] = _adamw_direct(w_down[0], m_w_down[0], v_w_down[0], own_wd, r_wd, me, "adamw_w_down", row_blocks=2)
    adam_up = _adamw(tr(w_up), tr(m_w_up), tr(v_w_up), p_wu, r_wu, chip, "adamw_w_up", row_blocks=4,
                     after=adam["w_down"][0])
    adam["w_out"] = _adamw(w_out[0], m_w_out[0], v_w_out[0], p_wo, r_wo, chip, "adamw_w_out", after=adam_up[0])
    adam_fcw = _adamw(taps(ffn_conv_w), taps(m_ffn_conv_w), taps(v_ffn_conv_w), p_fcw, r_fcw, chip, "adamw_ffn_conv_w",
                      after=adam["w_out"][0])
    _, (r_small,) = _split_wait("gather_small_wait", s_sem, r_sem, src_s, land_s, plan_s, slots_s, adam_fcw[0])
    small_in = {k: (p[k], m[k], v[k]) for k in _SMALL_NAMES}
    small_in["rel_bias_table"] = (tbl_t, m_rel_bias_table.T, v_rel_bias_table.T)
    small_out, loss_row = _adamw_small(r_small, small_in, None)
    (p_wi, p_cw), (r_wi, r_cw) = finish_mixer(loss_row)
    adam_in = _adamw(tr(w_in), tr(m_w_in), tr(v_w_in), p_wi, r_wi, chip, "adamw_w_in")
    adam_cw = _adamw(taps(conv_w), taps(m_conv_w), taps(v_conv_w), p_cw, r_cw, chip, "adamw_conv_w")

    res = {k: tuple(a[None] for a in t) for k, t in adam.items()}
    res["w_up"] = tuple(a.T[None] for a in adam_up)
    res["w_in"] = tuple(a.T[None] for a in adam_in)
    res["ffn_conv_w"] = tuple(taps(a) for a in adam_fcw)
    res["conv_w"] = tuple(taps(a) for a in adam_cw)
    res.update(small_out)
    res["rel_bias_table"] = tuple(a.T for a in small_out["rel_bias_table"])
    loss = loss_row[0, 0]
    order = ("norm_mix_g", "w_in", "conv_w", "q_norm_g", "k_norm_g", "rel_bias_table", "sinks", "out_norm_conv_g",
             "out_norm_attn_g", "w_out", "norm_ffn_g", "w_up", "ffn_conv_w", "ffn_conv_b", "w_down")
    return (loss, dx[None], *[res[k][0] for k in order], *[res[k][1] for k in order],
            *[res[k][2] for k in order], *[res[k][3] for k in order])
```

```python
import math

import numpy as np
import jax
import jax.numpy as jnp
from jax import lax
from jax.experimental import pallas as pl
from jax.experimental.pallas import tpu as pltpu

F32 = jnp.float32
BF16 = jnp.bfloat16

SEQ = 2048
D_MODEL = 1024
CONV_WIDTH = 512
ATTN_WIDTH = 512
KV_WIDTH = 128
HEAD_DIM = 64
N_HEADS = 8
GQA_GROUP = 4
IN_WIDTH = 2304
D_FF = 2816
BLK = 128
NUM_BUCKETS = 32
EPS = 1e-6
NEG_INF = -1e30
ADAM_LR = 0.001
ADAM_B1 = 0.9
ADAM_B2 = 0.999
ADAM_EPS = 1e-08
ADAM_WD = 0.01
ADAM_STEP = 10

N_DEV = 8
FFN_BLK = 2 * D_FF // N_DEV
N_FFN_BLK = D_FF // FFN_BLK
SUBLANES = 8
VMEM_LIMIT = 56 * 1024 * 1024

_MESH = pl.DeviceIdType.MESH
_ANY = pl.BlockSpec(memory_space=pl.ANY)


def _params(sem):
    return pltpu.CompilerParams(dimension_semantics=sem, vmem_limit_bytes=VMEM_LIMIT)


def _ordered_behind(body, pos, after):
    if after is None:
        return body, [], []
    return (lambda *refs: body(*refs[:pos], *refs[pos + 1:])), [_ANY], [after]


def _dot(a, b):
    return jnp.dot(a, b, preferred_element_type=F32)


def _dot_nt(a, b):
    return lax.dot_general(a, b, (((1,), (1,)), ((), ())), preferred_element_type=F32)


def _dot_tn(a, b):
    return lax.dot_general(a, b, (((0,), (0,)), ((), ())), preferred_element_type=F32)


def _shift_down(x, s, halo):
    r = pltpu.roll(x, s, axis=0)
    hr = pltpu.roll(halo, s, axis=0)
    row = lax.broadcasted_iota(jnp.int32, halo.shape, 0)
    top = jnp.where(row < s, hr, r[:SUBLANES])
    return jnp.concatenate([top, r[SUBLANES:]], axis=0)


def _shift_up(x, s, halo):
    n = x.shape[0]
    r = pltpu.roll(x, n - s, axis=0)
    hr = pltpu.roll(halo, SUBLANES - s, axis=0)
    row = lax.broadcasted_iota(jnp.int32, halo.shape, 0)
    bot = jnp.where(row >= SUBLANES - s, hr, r[n - SUBLANES:])
    return jnp.concatenate([r[:n - SUBLANES], bot], axis=0)


def _taps(w):
    return (w[0], w[1], w[2]) if len(w.shape) == 3 else (w[0:1], w[1:2], w[2:3])


def _gathered_taps(ref):
    return tuple(jnp.concatenate([ref[r, k] for r in range(N_DEV)], axis=-1) for k in range(3))


def _conv3(x, w, halo):
    x2 = _shift_down(x, 2, halo)
    x1 = _shift_down(x, 1, halo)
    return x2 * w[0] + x1 * w[1] + x * w[2], x2, x1


def _conv3_bwd_input(dy, w, halo_next):
    return dy * w[2] + _shift_up(dy, 1, halo_next) * w[1] + _shift_up(dy, 2, halo_next) * w[0]


def _rstd(x):
    return lax.rsqrt(jnp.mean(x * x, axis=-1, keepdims=True) + EPS)


def _rms_bwd(x, g, dy):
    r = _rstd(x)
    n = x * r
    dn = dy * g
    dx = r * (dn - n * jnp.mean(dn * n, axis=-1, keepdims=True))
    return dx, jnp.sum(dy * n, axis=0, keepdims=True)


def _head_mean(x):
    width = x.shape[-1]
    ri = lax.broadcasted_iota(jnp.int32, (width, width), 0) // HEAD_DIM
    ci = lax.broadcasted_iota(jnp.int32, (width, width), 1) // HEAD_DIM
    ones = jnp.where(ri == ci, 1.0, 0.0).astype(BF16)
    hi = x.astype(BF16)
    lo = (x - hi.astype(F32)).astype(BF16)
    return (_dot(hi, ones) + _dot(lo, ones)) * (1.0 / HEAD_DIM)


def _head_norm(x, g, heads):
    return x * lax.rsqrt(_head_mean(x * x) + EPS) * jnp.tile(g, (1, heads))


def _head_norm_bwd(x, g, dy, heads):
    r = lax.rsqrt(_head_mean(x * x) + EPS)
    n = x * r
    dn = dy * jnp.tile(g, (1, heads))
    dx = r * (dn - n * _head_mean(dn * n))
    per_lane = jnp.sum(dy * n, axis=0, keepdims=True)
    dg = per_lane[:, 0:HEAD_DIM]
    for h in range(1, heads):
        dg = dg + per_lane[:, HEAD_DIM * h:HEAD_DIM * (h + 1)]
    return dx, dg


def _bucket_map():
    q = np.arange(BLK)[:, None]
    j = np.arange(BLK)[None, :]
    n = np.where(j > q, q + BLK - j, q - j)
    nf = np.maximum(n, 1).astype(np.float32)
    max_exact = NUM_BUCKETS // 2
    large = max_exact + (np.log(nf / max_exact) / math.log(BLK / max_exact) * (NUM_BUCKETS - max_exact)).astype(np.int32)
    large = np.minimum(large, NUM_BUCKETS - 1)
    return np.where(n < max_exact, n, large).astype(np.int32)


def _coords():
    return lax.axis_index("x"), lax.axis_index("y"), lax.axis_index("c")


def _lin(px, py, pc):
    return 4 * px + 2 * py + pc


def _chips(x, y):
    return [(1 - x, y), (x, 1 - y), (1 - x, 1 - y)]


def _peers(x, y, c):
    return [(1 - x if r & 4 else x, 1 - y if r & 2 else y, 1 - c if r & 1 else c) for r in range(1, N_DEV)]


_HBM = pl.BlockSpec(memory_space=pltpu.HBM)
_SEM = pl.BlockSpec(memory_space=pltpu.SEMAPHORE)
_EFFECT = pltpu.SideEffectType.DATAFLOW_SIDE_EFFECTING


def _in_hbm(a):
    return pltpu.with_memory_space_constraint(a, pltpu.HBM)


_SIBLING = (1, lambda x, y, c: [(x, y, 1 - c)])
_SIBLING_AND_CHIPS = (2, lambda x, y, c: [(x, y, 1 - c)] + [(cx, cy, c) for cx, cy in _chips(x, y)])
_SIBLING_AND_NEIGHBOURS = (3, lambda x, y, c: [(x, y, 1 - c), (1 - x, y, c), (x, 1 - y, c)])
_ONWARD_AND_SIBLING = (4, lambda x, y, c: [(jnp.where(c == 1, x, 1 - x), jnp.where(c == 1, 1 - y, y), c), (x, y, 1 - c)])
_ALL_FOR_W_DOWN = (5, lambda x, y, c: _peers(x, y, c))
_CHIPS = (6, lambda x, y, c: [(cx, cy, c) for cx, cy in _chips(x, y)])
_ALL_FOR_SMALL = (7, lambda x, y, c: _peers(x, y, c))


def _split_start(name, srcs, lands, plan, after, handshake):
    ns, nl = len(srcs), len(lands)
    n_copies = len(plan(0, 0, 0))
    n_after = 0 if after is None else 1
    collective_id, peers_of = handshake

    def body(*refs):
        src_refs, land_refs = refs[:ns + nl], refs[ns:ns + nl]
        send_sems, recv_sems = refs[ns + nl + n_after], refs[ns + nl + n_after + 1]
        token = refs[-1]
        barrier = pltpu.get_barrier_semaphore()
        peers = peers_of(*_coords())
        for peer in peers:
            pl.semaphore_signal(barrier, inc=1, device_id=peer, device_id_type=_MESH)
        pl.semaphore_wait(barrier, len(peers))
        for k, (a, s_slot, l, d_slot, dev) in enumerate(plan(*_coords())):
            src = src_refs[a] if s_slot is None else src_refs[a].at[s_slot]
            pltpu.make_async_remote_copy(src_ref=src, dst_ref=land_refs[l].at[d_slot], send_sem=send_sems.at[k],
                                         recv_sem=recv_sems.at[k], device_id=dev, device_id_type=_MESH).start()
        token[...] = jnp.zeros_like(token)

    arrs = list(srcs) + list(lands)
    out = pl.pallas_call(
        body, name=name,
        out_shape=(pltpu.SemaphoreType.DMA((n_copies,)), pltpu.SemaphoreType.DMA((n_copies,)),
                   *[pltpu.HBM(a.shape, a.dtype) for a in arrs], jax.ShapeDtypeStruct((SUBLANES, 128), F32)),
        in_specs=[_HBM] * (ns + nl) + [_ANY] * n_after,
        out_specs=(_SEM, _SEM, *[_HBM] * (ns + nl), pl.BlockSpec(memory_space=pltpu.VMEM)),
        input_output_aliases={i: 2 + i for i in range(ns + nl)},
        compiler_params=pltpu.CompilerParams(has_side_effects=_EFFECT, collective_id=collective_id),
    )(*[_in_hbm(a) for a in arrs], *([] if after is None else [after]))
    return out[0], out[1], list(out[2:2 + ns]), list(out[2 + ns:2 + ns + nl]), out[-1]


def _split_wait(name, send_sems, recv_sems, srcs, lands, plan, recv_slots, after):
    ns, nl = len(srcs), len(lands)

    def body(*refs):
        src_refs, land_refs = refs[:ns + nl], refs[ns:ns + nl]
        send_sems, recv_sems = refs[ns + nl], refs[ns + nl + 1]
        coords = _coords()
        slots = recv_slots(*coords)
        for k, (a, s_slot, l, _, dev) in enumerate(plan(*coords)):
            src = src_refs[a] if s_slot is None else src_refs[a].at[s_slot]
            cp = pltpu.make_async_remote_copy(src_ref=src, dst_ref=land_refs[l].at[slots[k]], send_sem=send_sems.at[k],
                                              recv_sem=recv_sems.at[k], device_id=dev, device_id_type=_MESH)
            cp.wait_send()
            cp.wait_recv()

    arrs = list(srcs) + list(lands)
    out = pl.pallas_call(
        body, name=name,
        out_shape=tuple(pltpu.HBM(a.shape, a.dtype) for a in arrs),
        in_specs=[_HBM] * (ns + nl) + [_SEM, _SEM, _ANY],
        out_specs=tuple([_HBM] * (ns + nl)),
        input_output_aliases={i: i for i in range(ns + nl)},
        compiler_params=pltpu.CompilerParams(has_side_effects=_EFFECT),
    )(*arrs, send_sems, recv_sems, after)
    return list(out[:ns]), list(out[ns:])


def _gather_plan_ici(n):
    def plan(x, y, c):
        me = _lin(x, y, c)
        out = []
        for a in range(n):
            out.append((a, me, a, me, (x, y, 1 - c)))
            out += [(a, me, a, me, (cx, cy, c)) for cx, cy in _chips(x, y)]
        return out

    def recv_slots(x, y, c):
        out = []
        for _ in range(n):
            out.append(_lin(x, y, 1 - c))
            out += [_lin(cx, cy, c) for cx, cy in _chips(x, y)]
        return out

    return plan, recv_slots


def _gather_plan_d2d(n):
    def plan(x, y, c):
        return [(a, _lin(cx, cy, c), a, _lin(cx, cy, c), (x, y, 1 - c)) for a in range(n) for cx, cy in _chips(x, y)]

    def recv_slots(x, y, c):
        return [_lin(cx, cy, 1 - c) for _ in range(n) for cx, cy in _chips(x, y)]

    return plan, recv_slots


def _all_gather_split(lands, tag, after):
    n = len(lands)
    plan1, slots1 = _gather_plan_ici(n)
    s1, r1, _, lands, token = _split_start(f"gather_{tag}_ici_start", [], lands, plan1, after, _SIBLING_AND_CHIPS)

    def finish(after):
        _, got = _split_wait(f"gather_{tag}_ici_wait", s1, r1, [], lands, plan1, slots1, after)
        plan2, slots2 = _gather_plan_d2d(n)
        s2, r2, _, got, token2 = _split_start(f"gather_{tag}_d2d_start", [], got, plan2, None, _SIBLING)
        return _split_wait(f"gather_{tag}_d2d_wait", s2, r2, [], got, plan2, slots2, token2)[1]

    return finish, token


def _all_gather_tree(lands, tag, after):
    n = len(lands)

    def plan1(x, y, c):
        me = _lin(x, y, c)
        return [(a, me, a, me, dev) for a in range(n) for dev in ((x, y, 1 - c), (1 - x, y, c), (x, 1 - y, c))]

    def slots1(x, y, c):
        return [s for _ in range(n) for s in (_lin(x, y, 1 - c), _lin(1 - x, y, c), _lin(x, 1 - y, c))]

    def plan2(x, y, c):
        from_x, from_y = _lin(1 - x, y, c), _lin(x, 1 - y, c)
        north = c == 1
        passed = jnp.where(north, from_x, from_y)
        onward = (jnp.where(north, x, 1 - x), jnp.where(north, 1 - y, y), c)
        sib = (x, y, 1 - c)
        return [cp for a in range(n) for cp in ((a, passed, a, passed, onward), (a, from_x, a, from_x, sib),
                                                (a, from_y, a, from_y, sib))]

    def slots2(x, y, c):
        return [s for _ in range(n) for s in (_lin(1 - x, 1 - y, c), _lin(1 - x, y, 1 - c), _lin(x, 1 - y, 1 - c))]

    def plan3(x, y, c):
        diag = _lin(1 - x, 1 - y, c)
        return [(a, diag, a, diag, (x, y, 1 - c)) for a in range(n)]

    def slots3(x, y, c):
        return [_lin(1 - x, 1 - y, 1 - c)] * n

    s1, r1, _, lands, token = _split_start(f"gather_{tag}_1_start", [], lands, plan1, after, _SIBLING_AND_NEIGHBOURS)
    state = {}

    def stage2(after):
        _, got = _split_wait(f"gather_{tag}_1_wait", s1, r1, [], lands, plan1, slots1, after)
        state["s"], state["r"], _, state["lands"], token2 = _split_start(f"gather_{tag}_2_start", [], got, plan2, None,
                                                                         _ONWARD_AND_SIBLING)
        return token2

    def stage3(after):
        _, got = _split_wait(f"gather_{tag}_2_wait", state["s"], state["r"], [], state["lands"], plan2, slots2, after)
        s3, r3, _, got, token3 = _split_start(f"gather_{tag}_3_start", [], got, plan3, None, _SIBLING)
        return _split_wait(f"gather_{tag}_3_wait", s3, r3, [], got, plan3, slots3, token3)[1]

    return stage2, stage3, token


_CHIP_LIST = ((0, 0), (0, 1), (1, 0), (1, 1))


def _reduce_plan_d2d(n):
    def plan(x, y, c):
        return [(a, _lin(qx, qy, 1 - c), a, q, (x, y, 1 - c)) for a in range(n) for q, (qx, qy) in enumerate(_CHIP_LIST)]

    def recv_slots(x, y, c):
        return [q for _ in range(n) for q in range(4)]

    return plan, recv_slots


def _reduce_plan_ici(n):
    def plan(x, y, c):
        return [(a, 2 * cx + cy, a, j, (cx, cy, c)) for a in range(n) for j, (cx, cy) in enumerate(_chips(x, y))]

    def recv_slots(x, y, c):
        return [j for _ in range(n) for j in range(3)]

    return plan, recv_slots


def _scatter_plan(n):
    def plan(x, y, c):
        return [(a, _lin(*peer), a, r, peer) for a in range(n) for r, peer in enumerate(_peers(x, y, c))]

    def recv_slots(x, y, c):
        return [r for _ in range(n) for r in range(N_DEV - 1)]

    return plan, recv_slots


def _broadcast_plan():
    def plan(x, y, c):
        return [(0, None, 0, _lin(x, y, c), peer) for peer in _peers(x, y, c)]

    def recv_slots(x, y, c):
        return [_lin(*peer) for peer in _peers(x, y, c)]

    return plan, recv_slots


def _chip_partial(grads, recvd, core, name):
    n = len(grads)

    def body(c_ref, *refs):
        for a in range(n):
            g_ref, r_ref, o_ref = refs[a], refs[n + a], refs[2 * n + a]
            o_ref[...] = (g_ref[...].astype(F32) + r_ref[...].astype(F32)).astype(o_ref.dtype)

    def blk(a, own):
        zeros = (0,) * (a.ndim - 1)
        return pl.BlockSpec((None,) + a.shape[1:],
                            (lambda q, c_ref: (2 * q + c_ref[0],) + zeros) if own else (lambda q, c_ref: (q,) + zeros))

    return pl.pallas_call(
        body, name=name,
        grid_spec=pltpu.PrefetchScalarGridSpec(
            num_scalar_prefetch=1, grid=(4,),
            in_specs=[blk(a, True) for a in grads] + [blk(a, False) for a in recvd],
            out_specs=[blk(a, False) for a in recvd]),
        out_shape=[jax.ShapeDtypeStruct(a.shape, a.dtype) for a in recvd],
        compiler_params=_params(("arbitrary",)),
    )(core, *grads, *recvd)


def _reduce_scatter_split(grads, tag, core, behind):
    n = len(grads)
    plan1, slots1 = _reduce_plan_d2d(n)
    lands1 = [lax.empty((4,) + a.shape[1:], a.dtype) for a in grads]
    s1, r1, srcs1, lands1, token1 = _split_start(f"reduce_{tag}_d2d_start", grads, lands1, plan1, None, _SIBLING)
    own, got = _split_wait(f"reduce_{tag}_d2d_wait", s1, r1, srcs1, lands1, plan1, slots1, behind(token1))
    parts = _chip_partial(own, got, core, f"reduce_{tag}_partial")
    plan2, slots2 = _reduce_plan_ici(n)
    lands2 = [lax.empty((3,) + a.shape[1:], a.dtype) for a in grads]
    s2, r2, srcs2, lands2, token2 = _split_start(f"reduce_{tag}_ici_start", parts, lands2, plan2, None, _CHIPS)

    def finish(after):
        return _split_wait(f"reduce_{tag}_ici_wait", s2, r2, srcs2, lands2, plan2, slots2, after)

    return finish, token2


def _place_shards(me, shards, dtypes, name, after=None):
    n = len(shards)

    def body(me_ref, *refs):
        for a in range(n):
            refs[n + a][...] = refs[a][...].astype(dtypes[a])

    full = lambda s: pl.BlockSpec(s.shape, lambda i, me_ref: (0,) * s.ndim)
    slot = lambda s: pl.BlockSpec((None,) + s.shape, lambda i, me_ref: (me_ref[0],) + (0,) * s.ndim)
    body, more_specs, more = _ordered_behind(body, 1 + n, after)
    return pl.pallas_call(
        body, name=name,
        grid_spec=pltpu.PrefetchScalarGridSpec(num_scalar_prefetch=1, grid=(1,),
                                               in_specs=[full(s) for s in shards] + more_specs,
                                               out_specs=[slot(s) for s in shards]),
        out_shape=[jax.ShapeDtypeStruct((N_DEV,) + s.shape, d) for s, d in zip(shards, dtypes)],
        compiler_params=_params(("arbitrary",)),
    )(me, *shards, *more)


def _mix_in_fwd(x, g1, w_in_t, conv_w, gq, gk, gconv):
    tm = 512
    n_t = SEQ // tm

    def body(x_ref, g1_ref, w_ref, cw_ref, gq_ref, gk_ref, gc_ref,
             proj_ref, u1_ref, ycn_ref, qn_ref, kn_ref, v_ref, halo_ref):
        @pl.when(pl.program_id(0) == 0)
        def _():
            halo_ref[...] = jnp.zeros_like(halo_ref)

        xv = x_ref[...]
        u = (xv * _rstd(xv) * g1_ref[...]).astype(BF16)
        u1_ref[...] = u
        proj = _dot_nt(u, w_ref[...])
        proj_ref[...] = proj
        gate_b = proj[:, 0:CONV_WIDTH]
        a = proj[:, CONV_WIDTH:2 * CONV_WIDTH] * proj[:, 2 * CONV_WIDTH:3 * CONV_WIDTH]
        cv, _, _ = _conv3(a, _gathered_taps(cw_ref), halo_ref[...])
        halo_ref[...] = a[tm - SUBLANES:]
        yc = gate_b * cv
        ycn_ref[...] = (yc * _rstd(yc) * gc_ref[...]).astype(BF16)
        q0 = 3 * CONV_WIDTH
        qn_ref[...] = _head_norm(proj[:, q0:q0 + ATTN_WIDTH], gq_ref[...], N_HEADS).astype(BF16)
        k0 = q0 + ATTN_WIDTH
        kn_ref[...] = _head_norm(proj[:, k0:k0 + KV_WIDTH], gk_ref[...], 2).astype(BF16)
        v_ref[...] = proj[:, k0 + KV_WIDTH:k0 + 2 * KV_WIDTH].astype(BF16)

    const = lambda shape: pl.BlockSpec(shape, lambda i: (0,) * len(shape))
    rows = lambda w: pl.BlockSpec((tm, w), lambda i: (i, 0))
    return pl.pallas_call(
        body, name="mix_in_fwd", grid=(n_t,),
        in_specs=[rows(D_MODEL), const((1, D_MODEL)), const((IN_WIDTH, D_MODEL)), const((N_DEV, 3, 1, CONV_WIDTH // N_DEV)),
                  const((1, HEAD_DIM)), const((1, HEAD_DIM)), const((1, CONV_WIDTH))],
        out_specs=[rows(IN_WIDTH), rows(D_MODEL), rows(CONV_WIDTH), rows(ATTN_WIDTH), rows(KV_WIDTH), rows(KV_WIDTH)],
        out_shape=[jax.ShapeDtypeStruct((SEQ, IN_WIDTH), F32), jax.ShapeDtypeStruct((SEQ, D_MODEL), BF16),
                   jax.ShapeDtypeStruct((SEQ, CONV_WIDTH), BF16),
                   jax.ShapeDtypeStruct((SEQ, ATTN_WIDTH), BF16), jax.ShapeDtypeStruct((SEQ, KV_WIDTH), BF16),
                   jax.ShapeDtypeStruct((SEQ, KV_WIDTH), BF16)],
        scratch_shapes=[pltpu.VMEM((SUBLANES, CONV_WIDTH), F32)],
        compiler_params=_params(("arbitrary",)),
    )(x, g1, w_in_t, conv_w, gq, gk, gconv)


GROUP_ROWS = GQA_GROUP * BLK
QUERY_BLOCKS_PER_STEP = 2


def _band_bias(tbl_ref, bkt, bias_ref):
    for h in range(N_HEADS):
        acc = jnp.zeros(bkt.shape, F32)
        for b in range(NUM_BUCKETS):
            acc = jnp.where(bkt == b, tbl_ref[h, b], acc)
        bias_ref[h // GQA_GROUP, BLK * (h % GQA_GROUP):BLK * (h % GQA_GROUP + 1), :] = acc


def _band_masks(i):
    qi = lax.broadcasted_iota(jnp.int32, (GROUP_ROWS, BLK), 0) & (BLK - 1)
    ji = lax.broadcasted_iota(jnp.int32, (GROUP_ROWS, BLK), 1)
    upper = ji > qi
    return upper, upper & (i == 0)


def _stack_heads(x, g):
    return jnp.concatenate([x[:, HEAD_DIM * h:HEAD_DIM * (h + 1)] for h in range(GQA_GROUP * g, GQA_GROUP * (g + 1))], axis=0)


def _unstack_heads(groups):
    return jnp.concatenate([p[BLK * t:BLK * (t + 1)] for p in groups for t in range(GQA_GROUP)], axis=-1)


def _per_head_rows(vals):
    row = lax.broadcasted_iota(jnp.int32, (GROUP_ROWS, 1), 0)
    col = jnp.full((GROUP_ROWS, 1), vals[GQA_GROUP - 1], F32)
    for t in range(GQA_GROUP - 2, -1, -1):
        col = jnp.where(row < BLK * (t + 1), vals[t], col)
    return col


def _band_rows(ref, i):
    prev = pl.multiple_of(jnp.maximum(i - 1, 0) * BLK, BLK)
    cur = pl.multiple_of(i * BLK, BLK)
    return jnp.concatenate([ref[pl.ds(prev, BLK), :], ref[pl.ds(cur, BLK), :]], axis=0), prev, cur


def _fold(band, upper):
    return jnp.where(upper, band[:, :BLK], band[:, BLK:])


def _unfold(tile, upper):
    return jnp.concatenate([jnp.where(upper, tile, 0.0), jnp.where(upper, 0.0, tile)], axis=1)


def _head_probs(qh, kh, bias, upper, dead, sink):
    logits = _fold(_dot_nt(qh, kh), upper) * (HEAD_DIM ** -0.5) + bias
    logits = jnp.where(dead, NEG_INF, logits)
    m = jnp.maximum(jnp.max(logits, axis=-1, keepdims=True), sink)
    p = jnp.exp(logits - m)
    es = jnp.exp(sink - m)
    den = jnp.sum(p, axis=-1, keepdims=True) + es
    return p / den, es / den


def _attn_fwd(qn, kn, v, tbl, sinks, bkt, gattn, after=None):
    n_b = SEQ // BLK

    def body(q_ref, k_ref, v_ref, tbl_ref, sink_ref, bkt_ref, ga_ref, y_ref, yn_ref, p_ref, ps_ref, bias_ref):
        step = pl.program_id(0)

        @pl.when(step == 0)
        def _():
            _band_bias(tbl_ref, bkt_ref[...], bias_ref)

        lane = lax.broadcasted_iota(jnp.int32, (BLK, 128), 1)
        for b in range(QUERY_BLOCKS_PER_STEP):
            i = QUERY_BLOCKS_PER_STEP * step + b
            rows = slice(BLK * b, BLK * (b + 1))
            kb, _, _ = _band_rows(k_ref, i)
            vb, _, _ = _band_rows(v_ref, i)
            upper, dead = _band_masks(i)
            q = q_ref[rows, :]
            outs = []
            psinks = jnp.zeros((BLK, 128), F32)
            for g in range(N_HEADS // GQA_GROUP):
                kv = slice(HEAD_DIM * g, HEAD_DIM * (g + 1))
                sink = _per_head_rows([sink_ref[0, GQA_GROUP * g + t] for t in range(GQA_GROUP)])
                probs, psink = _head_probs(_stack_heads(q, g), kb[:, kv], bias_ref[g], upper, dead, sink)
                p_ref[b, g] = probs.astype(BF16)
                for t in range(GQA_GROUP):
                    psinks = jnp.where(lane == GQA_GROUP * g + t, psink[BLK * t:BLK * (t + 1)], psinks)
                outs.append(_dot(_unfold(probs, upper).astype(BF16), vb[:, kv]))
            ps_ref[rows, :] = psinks
            y = _unstack_heads(outs)
            y_ref[rows, :] = y
            yn_ref[rows, :] = (y * _rstd(y) * ga_ref[...]).astype(BF16)

    const = lambda shape: pl.BlockSpec(shape, lambda i: (0,) * len(shape))
    rows = lambda w: pl.BlockSpec((QUERY_BLOCKS_PER_STEP * BLK, w), lambda i: (i, 0))
    smem = pl.BlockSpec(memory_space=pltpu.SMEM)
    body, more_specs, more = _ordered_behind(body, 7, after)
    return pl.pallas_call(
        body, name="attn_fwd", grid=(n_b // QUERY_BLOCKS_PER_STEP,),
        in_specs=[rows(ATTN_WIDTH), const((SEQ, KV_WIDTH)), const((SEQ, KV_WIDTH)), smem, smem,
                  const((BLK, BLK)), const((1, ATTN_WIDTH))] + more_specs,
        out_specs=[rows(ATTN_WIDTH), rows(ATTN_WIDTH),
                   pl.BlockSpec((QUERY_BLOCKS_PER_STEP, N_HEADS // GQA_GROUP, GROUP_ROWS, BLK), lambda i: (i, 0, 0, 0)),
                   rows(128)],
        out_shape=[jax.ShapeDtypeStruct((SEQ, ATTN_WIDTH), F32), jax.ShapeDtypeStruct((SEQ, ATTN_WIDTH), BF16),
                   jax.ShapeDtypeStruct((n_b, N_HEADS // GQA_GROUP, GROUP_ROWS, BLK), BF16),
                   jax.ShapeDtypeStruct((SEQ, 128), F32)],
        scratch_shapes=[pltpu.VMEM((N_HEADS // GQA_GROUP, GROUP_ROWS, BLK), F32)],
        compiler_params=_params(("arbitrary",)),
    )(qn, kn, v, tbl, sinks, bkt, gattn, *more)


def _ffn_block(i, step):
    return jnp.where(i % 2 == 0, step, N_FFN_BLK - 1 - step)


def _ffn_fwd(x, ycn, yan, w_out, g2, w_up, fcw, fcb, w_down, tgt):
    tm = 512
    n_t = SEQ // tm

    def body(x_ref, ycn_ref, yan_ref, wo_ref, g2_ref, wu_ref, cw_ref, b_ref, wd_ref, tgt_ref,
             h1_ref, u2_ref, up_ref, pre_ref, act_ref, dh2_ref, dh2b_ref, loss_ref, acc_ref, halo_ref):
        i, step = pl.program_id(0), pl.program_id(1)
        j = _ffn_block(i, step)

        @pl.when((i == 0) & (step == 0))
        def _():
            loss_ref[...] = jnp.zeros_like(loss_ref)

        @pl.when(step == 0)
        def _():
            h1 = x_ref[...] + _dot(ycn_ref[...], wo_ref[0:CONV_WIDTH, :]) + _dot(yan_ref[...], wo_ref[CONV_WIDTH:, :])
            h1_ref[...] = h1
            u2_ref[...] = (h1 * _rstd(h1) * g2_ref[...]).astype(BF16)
            acc_ref[...] = jnp.zeros_like(acc_ref)

        u2 = u2_ref[...]
        pre = []
        for s in range(2):
            up = _dot_nt(u2, wu_ref[s])
            up_ref[s] = up.astype(BF16)
            halo = jnp.where(i == 0, 0.0, halo_ref[s, j])
            pre.append(_conv3(up, _taps(cw_ref.at[s]), halo)[0] + b_ref[s])
            pre_ref[s] = pre[s].astype(BF16)
            halo_ref[s, j] = up[tm - SUBLANES:]
        g, val = pre
        act = (g * jax.nn.sigmoid(g) * val).astype(BF16)
        act_ref[...] = act
        acc_ref[...] += _dot(act, wd_ref[...])

        @pl.when(step == N_FFN_BLK - 1)
        def _():
            err = h1_ref[...] + acc_ref[...] - tgt_ref[...]
            loss_ref[...] += 0.5 * jnp.sum(err * err) / D_MODEL
            dh2 = err / D_MODEL
            dh2_ref[...] = dh2
            dh2b_ref[...] = dh2.astype(BF16)

    rows = lambda w: pl.BlockSpec((tm, w), lambda i, step: (i, 0))
    const = lambda shape: pl.BlockSpec(shape, lambda i, step: (0,) * len(shape))
    pair = lambda *s: pl.BlockSpec((2, None) + s, lambda i, step: (0, _ffn_block(i, step)) + (0,) * len(s))
    upb = pl.BlockSpec((2, None, tm, FFN_BLK), lambda i, step: (0, _ffn_block(i, step), i, 0))
    return pl.pallas_call(
        body, name="ffn_fwd", grid=(n_t, N_FFN_BLK),
        in_specs=[rows(D_MODEL), rows(CONV_WIDTH), rows(ATTN_WIDTH), const((D_MODEL, D_MODEL)), const((1, D_MODEL)),
                  pair(FFN_BLK, D_MODEL), pair(3, 1, FFN_BLK), pair(1, FFN_BLK),
                  pl.BlockSpec((None, FFN_BLK, D_MODEL), lambda i, step: (_ffn_block(i, step), 0, 0)), rows(D_MODEL)],
        out_specs=[rows(D_MODEL), rows(D_MODEL), upb, upb,
                   pl.BlockSpec((None, tm, FFN_BLK), lambda i, step: (_ffn_block(i, step), i, 0)),
                   rows(D_MODEL), rows(D_MODEL), const((SUBLANES, 128))],
        out_shape=[jax.ShapeDtypeStruct((SEQ, D_MODEL), F32), jax.ShapeDtypeStruct((SEQ, D_MODEL), BF16),
                   jax.ShapeDtypeStruct((2, N_FFN_BLK, SEQ, FFN_BLK), BF16),
                   jax.ShapeDtypeStruct((2, N_FFN_BLK, SEQ, FFN_BLK), BF16),
                   jax.ShapeDtypeStruct((N_FFN_BLK, SEQ, FFN_BLK), BF16),
                   jax.ShapeDtypeStruct((SEQ, D_MODEL), F32), jax.ShapeDtypeStruct((SEQ, D_MODEL), BF16),
                   jax.ShapeDtypeStruct((SUBLANES, 128), F32)],
        scratch_shapes=[pltpu.VMEM((tm, D_MODEL), F32), pltpu.VMEM((2, N_FFN_BLK, SUBLANES, FFN_BLK), F32)],
        compiler_params=_params(("arbitrary", "arbitrary")),
    )(x, ycn, yan, w_out, g2, w_up, fcw, fcb, w_down, tgt)


def _ffn_bwd(dh2, dh2b, h1, g2, up, pre, w_up, fcw, w_down, after=None):
    tm = 512
    units = ((288, 224), (0, 288))
    n_t = SEQ // tm

    def body(dh2_ref, dh2b_ref, h1_ref, g2_ref, up_ref, pre_ref, wu_ref, cw_ref, wd_ref,
             dup_ref, dh1_ref, dh1b_ref, dfb_ref, dfcw_ref, dg2_ref, acc_ref, next_ref):
        j, i = pl.program_id(0), pl.program_id(1)
        tile = pl.ds(pl.multiple_of((n_t - 1 - i) * tm, tm), tm)

        @pl.when((j == 0) & (i == 0))
        def _():
            dfb_ref[...] = jnp.zeros_like(dfb_ref)
            dfcw_ref[...] = jnp.zeros_like(dfcw_ref)
            dg2_ref[...] = jnp.zeros_like(dg2_ref)

        @pl.when(j == 0)
        def _():
            acc_ref[tile, :] = jnp.zeros((tm, D_MODEL), F32)

        nxt = [jnp.where(i == 0, 0.0, next_ref[s]) for s in range(2)]
        sums = [[0.0] * 4 for _ in range(2)]
        for r0, rn in units:
            rows = slice(r0, r0 + rn)
            g, val = pre_ref[0, rows, :].astype(F32), pre_ref[1, rows, :].astype(F32)
            sg = jax.nn.sigmoid(g)
            silu = g * sg
            dact = _dot_nt(dh2b_ref[rows, :], wd_ref[...])
            dpre = (dact * val * (sg * (1.0 + g * (1.0 - sg))), dact * silu)
            dups = []
            for s in range(2):
                d = dpre[s]
                u = up_ref[s, rows, :].astype(F32)
                w = _taps(cw_ref.at[s])
                d1 = _shift_up(d, 1, nxt[s])
                d2 = _shift_up(d, 2, nxt[s])
                nxt[s] = d[:SUBLANES]
                for t, term in enumerate((d, d2 * u, d1 * u, d * u)):
                    sums[s][t] = sums[s][t] + jnp.sum(term, axis=0, keepdims=True)
                dups.append((d * w[2] + d1 * w[1] + d2 * w[0]).astype(BF16))
                dup_ref[s, rows, :] = dups[s]
            acc_rows = pl.ds(pl.multiple_of((n_t - 1 - i) * tm + r0, SUBLANES), rn)
            acc_ref[acc_rows, :] += _dot(dups[0], wu_ref[0]) + _dot(dups[1], wu_ref[1])
        for s in range(2):
            next_ref[s] = nxt[s]
            dfb_ref[s, j] += sums[s][0]
            for t in range(3):
                dfcw_ref[s, j, t] += sums[s][1 + t]

        @pl.when(j == N_FFN_BLK - 1)
        def _():
            dn, dgain = _rms_bwd(h1_ref[...], g2_ref[...], acc_ref[tile, :])
            dh1 = dh2_ref[...] + dn
            dh1_ref[...] = dh1
            dh1b_ref[...] = dh1.astype(BF16)
            dg2_ref[...] += dgain

    rev = lambda i: n_t - 1 - i
    rows = lambda w: pl.BlockSpec((tm, w), lambda j, i: (rev(i), 0))
    last_rows = lambda w: pl.BlockSpec((tm, w), lambda j, i: (jnp.where(j == N_FFN_BLK - 1, rev(i), rev(0)), 0))
    const = lambda shape: pl.BlockSpec(shape, lambda j, i: (0,) * len(shape))
    pair = lambda *s: pl.BlockSpec((2, None) + s, lambda j, i: (0, j) + (0,) * len(s))
    upb = pl.BlockSpec((2, None, tm, FFN_BLK), lambda j, i: (0, j, rev(i), 0))
    body, more_specs, more = _ordered_behind(body, 9, after)
    return pl.pallas_call(
        body, name="ffn_bwd", grid=(N_FFN_BLK, n_t),
        in_specs=[last_rows(D_MODEL), rows(D_MODEL), last_rows(D_MODEL), const((1, D_MODEL)), upb, upb,
                  pair(FFN_BLK, D_MODEL), pair(3, 1, FFN_BLK),
                  pl.BlockSpec((None, FFN_BLK, D_MODEL), lambda j, i: (j, 0, 0))] + more_specs,
        out_specs=[upb, last_rows(D_MODEL), last_rows(D_MODEL),
                   const((2, N_FFN_BLK, 1, FFN_BLK)), const((2, N_FFN_BLK, 3, 1, FFN_BLK)), const((1, D_MODEL))],
        out_shape=[jax.ShapeDtypeStruct((2, N_FFN_BLK, SEQ, FFN_BLK), BF16), jax.ShapeDtypeStruct((SEQ, D_MODEL), F32),
                   jax.ShapeDtypeStruct((SEQ, D_MODEL), BF16), jax.ShapeDtypeStruct((2, N_FFN_BLK, 1, FFN_BLK), F32),
                   jax.ShapeDtypeStruct((2, N_FFN_BLK, 3, 1, FFN_BLK), F32), jax.ShapeDtypeStruct((1, D_MODEL), F32)],
        scratch_shapes=[pltpu.VMEM((SEQ, D_MODEL), F32), pltpu.VMEM((2, SUBLANES, FFN_BLK), F32)],
        compiler_params=_params(("arbitrary", "arbitrary")),
    )(dh2, dh2b, h1, g2, up, pre, w_up, fcw, w_down, *more)


def _grad_tn(a_list, b, out_rows, name, after=None):
    n = len(a_list)
    ncol = b.shape[1]

    def body(*refs):
        a_refs, b_ref, o_ref = refs[:n], refs[n], refs[n + 1]
        j = pl.program_id(0)
        for k in range(n):
            @pl.when(j == k)
            def _(k=k):
                o_ref[...] = _dot_tn(a_refs[k][...], b_ref[...]).astype(BF16)

    full = lambda shape: pl.BlockSpec(shape, lambda j: (0,) * len(shape))
    body, more_specs, more = _ordered_behind(body, n + 1, after)
    return pl.pallas_call(
        body, name=name, grid=(n,),
        in_specs=[full((SEQ, out_rows))] * n + [full((SEQ, ncol))] + more_specs,
        out_specs=pl.BlockSpec((None, out_rows, ncol), lambda j: (j, 0, 0)),
        out_shape=jax.ShapeDtypeStruct((n, out_rows, ncol), BF16),
        compiler_params=_params(("arbitrary",)),
    )(*a_list, b, *more)


def _grad_tn_blocked(a, b, name, per_step=2):
    nb, _, a_w = a.shape
    b_w = b.shape[-1]

    def body(a_ref, b_ref, o_ref):
        for p in range(per_step):
            o_ref[p] = _dot_tn(a_ref[p], b_ref[...]).astype(BF16)

    return pl.pallas_call(
        body, name=name, grid=(nb // per_step,),
        in_specs=[pl.BlockSpec((per_step, SEQ, a_w), lambda k: (k, 0, 0)), pl.BlockSpec((SEQ, b_w), lambda k: (0, 0))],
        out_specs=pl.BlockSpec((per_step, a_w, b_w), lambda k: (k, 0, 0)),
        out_shape=jax.ShapeDtypeStruct((nb, a_w, b_w), BF16),
        compiler_params=_params(("arbitrary",)),
    )(a, b)


def _out_bwd(dh1b, w_out, y_attn, gattn, after=None):
    tm = 1024
    n_t = SEQ // tm

    def body(dh_ref, wo_ref, y_ref, ga_ref, dycn_ref, dy_ref, dga_ref):
        @pl.when(pl.program_id(0) == 0)
        def _():
            dga_ref[...] = jnp.zeros_like(dga_ref)

        dycat = _dot_nt(dh_ref[...], wo_ref[...])
        dycn_ref[...] = dycat[:, :CONV_WIDTH]
        dy, dga = _rms_bwd(y_ref[...], ga_ref[...], dycat[:, CONV_WIDTH:])
        dy_ref[...] = dy
        dga_ref[...] += dga

    rows = lambda w: pl.BlockSpec((tm, w), lambda i: (i, 0))
    const = lambda shape: pl.BlockSpec(shape, lambda i: (0,) * len(shape))
    body, more_specs, more = _ordered_behind(body, 4, after)
    return pl.pallas_call(
        body, name="out_bwd", grid=(n_t,),
        in_specs=[rows(D_MODEL), const((D_MODEL, D_MODEL)), rows(ATTN_WIDTH), const((1, ATTN_WIDTH))] + more_specs,
        out_specs=[rows(CONV_WIDTH), rows(ATTN_WIDTH), const((1, ATTN_WIDTH))],
        out_shape=[jax.ShapeDtypeStruct((SEQ, CONV_WIDTH), F32), jax.ShapeDtypeStruct((SEQ, ATTN_WIDTH), F32),
                   jax.ShapeDtypeStruct((1, ATTN_WIDTH), F32)],
        compiler_params=_params(("arbitrary",)),
    )(dh1b, w_out, y_attn, gattn, *more)


def _attn_bwd(qn, kn, v, dy, probs, psinks, bkt, after=None):
    n_b = SEQ // BLK

    def body(q_ref, k_ref, v_ref, dy_ref, p_ref, ps_ref, bkt_ref,
             dq_ref, dk_ref, dv_ref, dtbl_ref, dsink_ref, dbias_ref, dsacc_ref):
        step = pl.program_id(0)

        @pl.when(step == 0)
        def _():
            dbias_ref[...] = jnp.zeros_like(dbias_ref)
            dsacc_ref[...] = jnp.zeros_like(dsacc_ref)
            dk_ref[...] = jnp.zeros_like(dk_ref)
            dv_ref[...] = jnp.zeros_like(dv_ref)

        lane = lax.broadcasted_iota(jnp.int32, (BLK, 128), 1)
        for blk in range(QUERY_BLOCKS_PER_STEP):
            i = QUERY_BLOCKS_PER_STEP * step + blk
            rows = slice(BLK * blk, BLK * (blk + 1))
            kb, prev, cur = _band_rows(k_ref, i)
            vb, _, _ = _band_rows(v_ref, i)
            upper, _ = _band_masks(i)
            q = q_ref[rows, :]
            dy = dy_ref[rows, :]
            psink = ps_ref[rows, :]
            dsink = jnp.zeros((BLK, 128), F32)
            dqs, dks, dvs = [], [], []
            for g in range(N_HEADS // GQA_GROUP):
                kv = slice(HEAD_DIM * g, HEAD_DIM * (g + 1))
                qg = _stack_heads(q, g)
                dog = _stack_heads(dy, g).astype(BF16)
                pb = p_ref[blk, g]
                pg = pb.astype(F32)
                dprobs = _fold(_dot_nt(dog, vb[:, kv]), upper)
                dvs.append(_dot_tn(_unfold(pb, upper), dog))
                dsum = jnp.sum(pg * dprobs, axis=-1, keepdims=True)
                dlogits = pg * (dprobs - dsum)
                for t in range(GQA_GROUP):
                    dsink = jnp.where(lane == GQA_GROUP * g + t, -psink * dsum[BLK * t:BLK * (t + 1)], dsink)
                dbias_ref[g] += dlogits
                ds = _unfold(dlogits * (HEAD_DIM ** -0.5), upper).astype(BF16)
                dqs.append(_dot(ds, kb[:, kv]))
                dks.append(_dot_tn(ds, qg))
            dsacc_ref[...] += dsink
            dq_ref[rows, :] = _unstack_heads(dqs)
            dkb = jnp.concatenate(dks, axis=-1)
            dvb = jnp.concatenate(dvs, axis=-1)
            dk_ref[pl.ds(prev, BLK), :] += dkb[:BLK]
            dk_ref[pl.ds(cur, BLK), :] += dkb[BLK:]
            dv_ref[pl.ds(prev, BLK), :] += dvb[:BLK]
            dv_ref[pl.ds(cur, BLK), :] += dvb[BLK:]

        @pl.when(step == n_b // QUERY_BLOCKS_PER_STEP - 1)
        def _():
            bkt = bkt_ref[...]
            row8 = lax.broadcasted_iota(jnp.int32, (N_HEADS, 128), 0)
            lane8 = lax.broadcasted_iota(jnp.int32, (N_HEADS, 128), 1)
            acc = jnp.zeros((N_HEADS, 128), F32)
            for h in range(N_HEADS):
                rows = slice(BLK * (h % GQA_GROUP), BLK * (h % GQA_GROUP + 1))
                dbh = dbias_ref[h // GQA_GROUP, rows, :]
                for b in range(NUM_BUCKETS):
                    acc = jnp.where((row8 == h) & (lane8 == b), jnp.sum(jnp.where(bkt == b, dbh, 0.0)), acc)
            dsink_ref[...] = jnp.sum(dsacc_ref[...], axis=0, keepdims=True)
            dtbl_ref[...] = acc

    const = lambda shape: pl.BlockSpec(shape, lambda i: (0,) * len(shape))
    rows = lambda w: pl.BlockSpec((QUERY_BLOCKS_PER_STEP * BLK, w), lambda i: (i, 0))
    n_g = N_HEADS // GQA_GROUP
    body, more_specs, more = _ordered_behind(body, 7, after)
    return pl.pallas_call(
        body, name="attn_bwd", grid=(n_b // QUERY_BLOCKS_PER_STEP,),
        in_specs=[rows(ATTN_WIDTH), const((SEQ, KV_WIDTH)), const((SEQ, KV_WIDTH)), rows(ATTN_WIDTH),
                  pl.BlockSpec((QUERY_BLOCKS_PER_STEP, n_g, GROUP_ROWS, BLK), lambda i: (i, 0, 0, 0)), rows(128),
                  const((BLK, BLK))] + more_specs,
        out_specs=[rows(ATTN_WIDTH), const((SEQ, KV_WIDTH)), const((SEQ, KV_WIDTH)), const((N_HEADS, 128)), const((1, 128))],
        out_shape=[jax.ShapeDtypeStruct((SEQ, ATTN_WIDTH), F32), jax.ShapeDtypeStruct((SEQ, KV_WIDTH), F32),
                   jax.ShapeDtypeStruct((SEQ, KV_WIDTH), F32), jax.ShapeDtypeStruct((N_HEADS, 128), F32),
                   jax.ShapeDtypeStruct((1, 128), F32)],
        scratch_shapes=[pltpu.VMEM((n_g, GROUP_ROWS, BLK), F32), pltpu.VMEM((BLK, 128), F32)],
        compiler_params=_params(("arbitrary",)),
    )(qn, kn, v, dy, probs, psinks, bkt, *more)


def _mix_in_bwd(x, dh1, proj, dycn, dqn, dkn, dv, w_in_t, conv_w, g1, gq, gk, gconv):
    tm = 512
    n_t = SEQ // tm
    halo_blocks = tm // SUBLANES

    def body(x_ref, dh1_ref, proj_ref, halo_ref, dycn_ref, dqn_ref, dkn_ref, dv_ref, w_ref, cw_ref,
             g1_ref, gq_ref, gk_ref, gc_ref,
             dx_ref, dproj_ref, dcw_ref, dgc_ref, dgq_ref, dgk_ref, dg1_ref, dcw_shards_ref, next_ref):
        i = pl.program_id(0)
        first_tile = i == n_t - 1

        @pl.when(i == 0)
        def _():
            for r in (dcw_ref, dgc_ref, dgq_ref, dgk_ref, dg1_ref, next_ref):
                r[...] = jnp.zeros_like(r)

        proj = proj_ref[...]
        hp = halo_ref[...]
        gate_b = proj[:, 0:CONV_WIDTH]
        gate_c = proj[:, CONV_WIDTH:2 * CONV_WIDTH]
        hc = proj[:, 2 * CONV_WIDTH:3 * CONV_WIDTH]
        a = gate_c * hc
        a_halo = jnp.where(first_tile, 0.0, hp[:, CONV_WIDTH:2 * CONV_WIDTH] * hp[:, 2 * CONV_WIDTH:3 * CONV_WIDTH])
        cw = _gathered_taps(cw_ref)
        cv, a2, a1 = _conv3(a, cw, a_halo)
        dyc, dgc = _rms_bwd(gate_b * cv, gc_ref[...], dycn_ref[...])
        dgc_ref[...] += dgc
        dcv = dyc * gate_b
        dcw_ref[...] += jnp.concatenate(
            [jnp.sum(dcv * a2, axis=0, keepdims=True), jnp.sum(dcv * a1, axis=0, keepdims=True),
             jnp.sum(dcv * a, axis=0, keepdims=True)], axis=0)
        da = _conv3_bwd_input(dcv, cw, next_ref[...])
        next_ref[...] = dcv[:SUBLANES]
        q0 = 3 * CONV_WIDTH
        k0 = q0 + ATTN_WIDTH
        dq, dgq = _head_norm_bwd(proj[:, q0:k0], gq_ref[...], dqn_ref[...], N_HEADS)
        dk, dgk = _head_norm_bwd(proj[:, k0:k0 + KV_WIDTH], gk_ref[...], dkn_ref[...], 2)
        dgq_ref[...] += dgq
        dgk_ref[...] += dgk
        dproj = jnp.concatenate([dyc * cv, da * hc, da * gate_c, dq, dk, dv_ref[...]], axis=-1).astype(BF16)
        dproj_ref[...] = dproj
        du1 = _dot(dproj, w_ref[...])
        xv = x_ref[...]
        dn, dg1 = _rms_bwd(xv, g1_ref[...], du1)
        dx_ref[...] = dh1_ref[...] + dn
        dg1_ref[...] += dg1

        @pl.when(i == n_t - 1)
        def _():
            shard = CONV_WIDTH // N_DEV
            for r in range(N_DEV):
                for k in range(3):
                    dcw_shards_ref[r, k] = dcw_ref[k:k + 1, shard * r:shard * (r + 1)]

    rev = lambda i: n_t - 1 - i
    rows = lambda w: pl.BlockSpec((tm, w), lambda i: (rev(i), 0))
    const = lambda shape: pl.BlockSpec(shape, lambda i: (0,) * len(shape))
    halo = pl.BlockSpec((SUBLANES, IN_WIDTH), lambda i: (jnp.maximum(rev(i) * halo_blocks - 1, 0), 0))
    return pl.pallas_call(
        body, name="mix_in_bwd", grid=(n_t,),
        in_specs=[rows(D_MODEL), rows(D_MODEL), rows(IN_WIDTH), halo, rows(CONV_WIDTH), rows(ATTN_WIDTH), rows(KV_WIDTH),
                  rows(KV_WIDTH), const((IN_WIDTH, D_MODEL)), const((N_DEV, 3, 1, CONV_WIDTH // N_DEV)), const((1, D_MODEL)),
                  const((1, HEAD_DIM)), const((1, HEAD_DIM)), const((1, CONV_WIDTH))],
        out_specs=[rows(D_MODEL), rows(IN_WIDTH), const((3, CONV_WIDTH)), const((1, CONV_WIDTH)),
                   const((1, HEAD_DIM)), const((1, HEAD_DIM)), const((1, D_MODEL)),
                   const((N_DEV, 3, 1, CONV_WIDTH // N_DEV))],
        out_shape=[jax.ShapeDtypeStruct((SEQ, D_MODEL), F32), jax.ShapeDtypeStruct((SEQ, IN_WIDTH), BF16),
                   jax.ShapeDtypeStruct((3, CONV_WIDTH), F32),
                   jax.ShapeDtypeStruct((1, CONV_WIDTH), F32), jax.ShapeDtypeStruct((1, HEAD_DIM), F32),
                   jax.ShapeDtypeStruct((1, HEAD_DIM), F32), jax.ShapeDtypeStruct((1, D_MODEL), F32),
                   jax.ShapeDtypeStruct((N_DEV, 3, 1, CONV_WIDTH // N_DEV), F32)],
        scratch_shapes=[pltpu.VMEM((SUBLANES, CONV_WIDTH), F32)],
        compiler_params=_params(("arbitrary",)),
    )(x, dh1, proj, proj, dycn, dqn, dkn, dv, w_in_t, conv_w, g1, gq, gk, gconv)


def _grad_w_in(dproj, u1, after=None):
    bw = 768

    def body(a_ref, b_ref, o_ref):
        o_ref[...] = _dot_tn(a_ref[...], b_ref[...]).astype(BF16)

    body, more_specs, more = _ordered_behind(body, 2, after)
    return pl.pallas_call(
        body, name="grad_w_in", grid=(IN_WIDTH // bw,),
        in_specs=[pl.BlockSpec((SEQ, bw), lambda k: (0, k)), pl.BlockSpec((SEQ, D_MODEL), lambda k: (0, 0))] + more_specs,
        out_specs=pl.BlockSpec((bw, D_MODEL), lambda k: (k, 0)),
        out_shape=jax.ShapeDtypeStruct((IN_WIDTH, D_MODEL), BF16),
        compiler_params=_params(("arbitrary",)),
    )(dproj, u1, *more)


def _adamw_math(w, g, m, v):
    m = ADAM_B1 * m + (1.0 - ADAM_B1) * g
    v = ADAM_B2 * v + (1.0 - ADAM_B2) * (g * g)
    m_hat = m / (1.0 - ADAM_B1 ** ADAM_STEP)
    v_hat = v / (1.0 - ADAM_B2 ** ADAM_STEP)
    return -ADAM_LR * (m_hat / (jnp.sqrt(v_hat) + ADAM_EPS) + ADAM_WD * w), m, v


_ROW_G1, _ROW_G2, _ROW_OUT_NORMS, _ROW_FFN_B, _ROW_GQ, _ROW_GK, _ROW_SINKS, _ROW_LOSS, _ROW_TABLE = 0, 1, 2, 3, 11, 12, 13, 14, 16
SMALL_ROWS, SMALL_COLS = 24, 1024
_SMALL_NAMES = ("norm_mix_g", "norm_ffn_g", "out_norm_conv_g", "out_norm_attn_g", "ffn_conv_b", "q_norm_g", "k_norm_g",
                "sinks", "rel_bias_table")


def _pack_small_grads(dg1, dg2, dgconv, dgattn, dfb, dgq, dgk, dsinks, dtbl_t, loss_acc, after=None):
    def body(dg1_ref, dg2_ref, dgc_ref, dga_ref, dfb_ref, dgq_ref, dgk_ref, ds_ref, dt_ref, loss_ref, o_ref, all_ref):
        o_ref[...] = jnp.zeros_like(o_ref)
        o_ref[_ROW_G1:_ROW_G1 + 1, :] = dg1_ref[...]
        o_ref[_ROW_G2:_ROW_G2 + 1, :] = dg2_ref[...]
        o_ref[_ROW_OUT_NORMS:_ROW_OUT_NORMS + 1, 0:CONV_WIDTH] = dgc_ref[...]
        o_ref[_ROW_OUT_NORMS:_ROW_OUT_NORMS + 1, CONV_WIDTH:] = dga_ref[...]
        for k in range(N_DEV):
            o_ref[_ROW_FFN_B + k:_ROW_FFN_B + k + 1, 0:FFN_BLK] = dfb_ref[k // N_FFN_BLK, k % N_FFN_BLK]
        o_ref[_ROW_GQ:_ROW_GQ + 1, 0:HEAD_DIM] = dgq_ref[...]
        o_ref[_ROW_GK:_ROW_GK + 1, 0:HEAD_DIM] = dgk_ref[...]
        o_ref[_ROW_SINKS:_ROW_SINKS + 1, 0:128] = ds_ref[...]
        o_ref[_ROW_LOSS:_ROW_LOSS + 1, 0:128] = loss_ref[0:1, :]
        o_ref[_ROW_TABLE:_ROW_TABLE + N_HEADS, 0:128] = dt_ref[...]
        for s in range(N_DEV):
            all_ref[s] = o_ref[...]

    body, more_specs, more = _ordered_behind(body, 10, after)
    return pl.pallas_call(
        body, name="pack_small_grads",
        in_specs=[pl.BlockSpec(memory_space=pltpu.VMEM)] * 10 + more_specs,
        out_shape=[jax.ShapeDtypeStruct((SMALL_ROWS, SMALL_COLS), F32),
                   jax.ShapeDtypeStruct((N_DEV, SMALL_ROWS, SMALL_COLS), F32)],
    )(dg1, dg2, dgconv, dgattn, dfb, dgq, dgk, dsinks, dtbl_t, loss_acc, *more)


def _adamw_small(recv, params, after):
    names = _SMALL_NAMES
    n = len(names)

    def grad_of(g, name, k=None):
        if name == "norm_mix_g":
            return g[_ROW_G1:_ROW_G1 + 1, :]
        if name == "norm_ffn_g":
            return g[_ROW_G2:_ROW_G2 + 1, :]
        if name == "out_norm_conv_g":
            return g[_ROW_OUT_NORMS:_ROW_OUT_NORMS + 1, 0:CONV_WIDTH]
        if name == "out_norm_attn_g":
            return g[_ROW_OUT_NORMS:_ROW_OUT_NORMS + 1, CONV_WIDTH:]
        if name == "ffn_conv_b":
            return g[_ROW_FFN_B + k:_ROW_FFN_B + k + 1, 0:FFN_BLK]
        if name == "q_norm_g":
            return g[_ROW_GQ:_ROW_GQ + 1, 0:HEAD_DIM]
        if name == "k_norm_g":
            return g[_ROW_GK:_ROW_GK + 1, 0:HEAD_DIM]
        if name == "sinks":
            return g[_ROW_SINKS:_ROW_SINKS + 1, 0:N_HEADS]
        return g[_ROW_TABLE:_ROW_TABLE + N_HEADS, 0:NUM_BUCKETS]

    def body(r_ref, *refs):
        ins, outs, loss_ref = refs[:3 * n], refs[3 * n:7 * n], refs[7 * n]
        g = r_ref[0]
        for s in range(1, N_DEV):
            g = g + r_ref[s]
        loss_ref[...] = g[_ROW_LOSS:_ROW_LOSS + 1, 0:128]
        for i, name in enumerate(names):
            w_ref, m_ref, v_ref = ins[3 * i:3 * i + 3]
            o = outs[4 * i:4 * i + 4]
            cols = [slice(FFN_BLK * k, FFN_BLK * (k + 1)) for k in range(N_DEV)] if name == "ffn_conv_b" else [slice(None)]
            for k, cs in enumerate(cols):
                gk = grad_of(g, name, k)
                d, m2, v2 = _adamw_math(w_ref[:, cs], gk, m_ref[:, cs], v_ref[:, cs])
                o[0][:, cs], o[1][:, cs], o[2][:, cs], o[3][:, cs] = gk, d, m2, v2

    flat = [a for name in names for a in params[name]]
    body, more_specs, more = _ordered_behind(body, 1 + 3 * n, after)
    vmem = pl.BlockSpec(memory_space=pltpu.VMEM)
    out = pl.pallas_call(
        body, name="adamw_small",
        in_specs=[vmem] * (1 + 3 * n) + more_specs,
        out_shape=[jax.ShapeDtypeStruct(params[name][0].shape, F32) for name in names for _ in range(4)]
        + [jax.ShapeDtypeStruct((1, 128), F32)],
        compiler_params=pltpu.CompilerParams(vmem_limit_bytes=VMEM_LIMIT),
    )(recv, *flat, *more)
    return {name: tuple(out[4 * i:4 * i + 4]) for i, name in enumerate(names)}, out[4 * n]


def _adamw_direct(w, m, v, own, recv, me, name, row_blocks=1, after=None):
    rb = w.shape[0] // row_blocks
    cols = w.shape[1]

    def body(me_ref, w_ref, m_ref, v_ref, o_ref, r_ref, g_o, d_o, m_o, v_o):
        g = o_ref[...].astype(F32)
        for s in range(N_DEV - 1):
            g = g + r_ref[s].astype(F32)
        g_o[...] = g
        d_o[...], m_o[...], v_o[...] = _adamw_math(w_ref[...], g, m_ref[...], v_ref[...])

    blk = pl.BlockSpec((rb, cols), lambda i, me_ref: (i, 0))
    oblk = pl.BlockSpec((None, rb, cols), lambda i, me_ref: (me_ref[0], i, 0))
    rblk = pl.BlockSpec((N_DEV - 1, rb, cols), lambda i, me_ref: (0, i, 0))
    body, more_specs, more = _ordered_behind(body, 6, after)
    return pl.pallas_call(
        body, name=name,
        grid_spec=pltpu.PrefetchScalarGridSpec(num_scalar_prefetch=1, grid=(row_blocks,),
                                               in_specs=[blk, blk, blk, oblk, rblk] + more_specs, out_specs=[blk] * 4),
        out_shape=[jax.ShapeDtypeStruct(w.shape, F32)] * 4,
        compiler_params=_params(("arbitrary",)),
    )(me, w, m, v, own, recv, *more)


def _adamw(w, m, v, part, recv, chip, name, row_blocks=1, after=None):
    rb = w.shape[0] // row_blocks
    tail = w.shape[1:]
    zeros = (0,) * len(tail)

    def body(chip_ref, w_ref, m_ref, v_ref, p_ref, r_ref, g_o, d_o, m_o, v_o):
        g = p_ref[...].astype(F32)
        for s in range(3):
            g = g + r_ref[s].astype(F32)
        g_o[...] = g
        d_o[...], m_o[...], v_o[...] = _adamw_math(w_ref[...], g, m_ref[...], v_ref[...])

    blk = pl.BlockSpec((rb,) + tail, lambda i, chip_ref: (i,) + zeros)
    pblk = pl.BlockSpec((None, rb) + tail, lambda i, chip_ref: (chip_ref[0], i) + zeros)
    rblk = pl.BlockSpec((3, rb) + tail, lambda i, chip_ref: (0, i) + zeros)
    body, more_specs, more = _ordered_behind(body, 6, after)
    return pl.pallas_call(
        body, name=name,
        grid_spec=pltpu.PrefetchScalarGridSpec(num_scalar_prefetch=1, grid=(row_blocks,),
                                               in_specs=[blk, blk, blk, pblk, rblk] + more_specs, out_specs=[blk] * 4),
        out_shape=[jax.ShapeDtypeStruct(w.shape, F32)] * 4,
        compiler_params=_params(("arbitrary",)),
    )(chip, w, m, v, part, recv, *more)


def kernel(x, norm_mix_g, w_in, conv_w, q_norm_g, k_norm_g, rel_bias_table, sinks, out_norm_conv_g, out_norm_attn_g, w_out, norm_ffn_g, w_up, ffn_conv_w, ffn_conv_b, w_down, loss_target, m_norm_mix_g, m_w_in, m_conv_w, m_q_norm_g, m_k_norm_g, m_rel_bias_table, m_sinks, m_out_norm_conv_g, m_out_norm_attn_g, m_w_out, m_norm_ffn_g, m_w_up, m_ffn_conv_w, m_ffn_conv_b, m_w_down, v_norm_mix_g, v_w_in, v_conv_w, v_q_norm_g, v_k_norm_g, v_rel_bias_table, v_sinks, v_out_norm_conv_g, v_out_norm_attn_g, v_w_out, v_norm_ffn_g, v_w_up, v_ffn_conv_w, v_ffn_conv_b, v_w_down):
    p = dict(norm_mix_g=norm_mix_g, w_in=w_in, conv_w=conv_w, q_norm_g=q_norm_g, k_norm_g=k_norm_g,
             rel_bias_table=rel_bias_table, sinks=sinks, out_norm_conv_g=out_norm_conv_g, out_norm_attn_g=out_norm_attn_g,
             w_out=w_out, norm_ffn_g=norm_ffn_g, w_up=w_up, ffn_conv_w=ffn_conv_w, ffn_conv_b=ffn_conv_b, w_down=w_down)
    m = dict(norm_mix_g=m_norm_mix_g, w_in=m_w_in, conv_w=m_conv_w, q_norm_g=m_q_norm_g, k_norm_g=m_k_norm_g,
             rel_bias_table=m_rel_bias_table, sinks=m_sinks, out_norm_conv_g=m_out_norm_conv_g,
             out_norm_attn_g=m_out_norm_attn_g, w_out=m_w_out, norm_ffn_g=m_norm_ffn_g, w_up=m_w_up,
             ffn_conv_w=m_ffn_conv_w, ffn_conv_b=m_ffn_conv_b, w_down=m_w_down)
    v = dict(norm_mix_g=v_norm_mix_g, w_in=v_w_in, conv_w=v_conv_w, q_norm_g=v_q_norm_g, k_norm_g=v_k_norm_g,
             rel_bias_table=v_rel_bias_table, sinks=v_sinks, out_norm_conv_g=v_out_norm_conv_g,
             out_norm_attn_g=v_out_norm_attn_g, w_out=v_w_out, norm_ffn_g=v_norm_ffn_g, w_up=v_w_up,
             ffn_conv_w=v_ffn_conv_w, ffn_conv_b=v_ffn_conv_b, w_down=v_w_down)

    xs, tgt = x[0], loss_target[0]
    g1, g2, gq, gk, gconv, gattn = norm_mix_g, norm_ffn_g, q_norm_g, k_norm_g, out_norm_conv_g, out_norm_attn_g
    ix, iy, ic = _coords()
    core = ic.astype(jnp.int32).reshape(1)
    chip = (2 * ix + iy).astype(jnp.int32).reshape(1)
    me = _lin(ix, iy, ic).astype(jnp.int32).reshape(1)
    bkt = jnp.asarray(_bucket_map())
    tr = lambda a: a[0].T
    taps = lambda a: jnp.transpose(a, (1, 0, 2))
    tbl_t = rel_bias_table.T

    wi_l, cw_l = _place_shards(me, [tr(w_in), taps(conv_w)], [BF16, F32], "place_mixer_shards")
    finish_a, token_a = _all_gather_split([wi_l, cw_l], "mixer", None)
    wo_l, wu_l, wd_l, fcw_l = _place_shards(me, [w_out[0], tr(w_up), w_down[0], taps(ffn_conv_w)],
                                            [BF16, BF16, BF16, F32], "place_ffn_shards", after=token_a)
    ffn_stage2, ffn_stage3, token_b = _all_gather_tree([wo_l, wu_l, wd_l, fcw_l], "ffn", token_a)
    wi_g, cw_g = finish_a(token_b)
    w_in_t = wi_g.reshape(IN_WIDTH, D_MODEL)

    proj, u1, ycn, qn, kn, vv = _mix_in_fwd(xs, g1, w_in_t, cw_g, gq, gk, gconv)
    token_b2 = ffn_stage2(ycn)
    y_attn, yan, probs, psinks = _attn_fwd(qn, kn, vv, tbl_t, sinks, bkt, gattn, after=token_b2)
    wo_g, wu_g, wd_g, fcw_g = ffn_stage3(yan)
    w_out_f = wo_g.reshape(D_MODEL, D_MODEL)
    w_down_f = wd_g.reshape(N_FFN_BLK, FFN_BLK, D_MODEL)
    w_up_f = wu_g.reshape(2, N_FFN_BLK, FFN_BLK, D_MODEL)
    fcw_f = fcw_g.reshape(2, N_FFN_BLK, 3, 1, FFN_BLK)
    fcb = ffn_conv_b.reshape(2, N_FFN_BLK, 1, FFN_BLK)
    h1, u2, up, pre, act, dh2, dh2b, loss_acc = _ffn_fwd(xs, ycn, yan, w_out_f, g2, w_up_f, fcw_f, fcb, w_down_f, tgt)

    dw_down = _grad_tn_blocked(act, dh2b, "grad_w_down").reshape(N_DEV, D_FF // N_DEV, D_MODEL)
    plan_d, slots_d = _scatter_plan(1)
    d_sem = _split_start("scatter_w_down_start", [dw_down], [lax.empty((N_DEV - 1,) + dw_down.shape[1:], BF16)],
                         plan_d, None, _ALL_FOR_W_DOWN)
    dup, dh1, dh1b, dfb, dfcw, dg2 = _ffn_bwd(dh2, dh2b, h1, g2, up, pre, w_up_f, fcw_f, w_down_f, after=d_sem[4])
    dw_up = _grad_tn_blocked(dup.reshape(N_DEV, SEQ, FFN_BLK), u2, "grad_w_up")
    dw_out = _grad_tn([ycn, yan], dh1b, CONV_WIDTH, "grad_w_out").reshape(N_DEV, D_MODEL // N_DEV, D_MODEL)
    out_bwd = {}

    def behind_ffn(token):
        out_bwd["r"] = _out_bwd(dh1b, w_out_f, y_attn, gattn, after=token)
        return out_bwd["r"][0]

    finish_ffn, token_ffn = _reduce_scatter_split(
        [dw_up, dw_out, dfcw.reshape(N_DEV, 3, 1, FFN_BLK)], "ffn", core, behind_ffn)
    dycn, dy_attn, dgattn = out_bwd["r"]
    dqn, dkn, dv, dtbl_t, dsinks = _attn_bwd(qn, kn, vv, dy_attn, probs, psinks, bkt, after=token_ffn)
    dx, dproj, _, dgconv, dgq, dgk, dg1, dcw_b = _mix_in_bwd(xs, dh1, proj, dycn, dqn, dkn, dv, w_in_t, cw_g,
                                                         g1, gq, gk, gconv)
    dw_in_t = _grad_w_in(dproj, u1).reshape(N_DEV, IN_WIDTH // N_DEV, D_MODEL)
    plan_s, slots_s = _broadcast_plan()
    adam = {}
    ffn_got = {}
    small = {}

    def behind_mixer(token):
        packed, packed_all = _pack_small_grads(dg1, dg2, dgconv, dgattn, dfb, dgq, dgk, dsinks, dtbl_t, loss_acc,
                                               after=token)
        small["r"] = _split_start("gather_small_start", [packed], [packed_all], plan_s, None, _ALL_FOR_SMALL)
        ffn_got["r"] = finish_ffn(small["r"][4])
        return ffn_got["r"][1][0]

    finish_mixer, token_mixer = _reduce_scatter_split([dw_in_t, dcw_b], "mixer", core, behind_mixer)
    s_sem, r_sem, src_s, land_s, _ = small["r"]
    (p_wu, p_wo, p_fcw), (r_wu, r_wo, r_fcw) = ffn_got["r"]
    (own_wd,), (r_wd,) = _split_wait("scatter_w_down_wait", d_sem[0], d_sem[1], d_sem[2], d_sem[3], plan_d, slots_d,
                                     token_mixer)
    adam["w_down"] = _adamw_direct(w_down[0], m_w_down[0], v_w_down[0], own_wd, r_wd, me, "adamw_w_down", row_blocks=2)
    adam_up = _adamw(tr(w_up), tr(m_w_up), tr(v_w_up), p_wu, r_wu, chip, "adamw_w_up", row_blocks=4,
                     after=adam["w_down"][0])
    adam["w_out"] = _adamw(w_out[0], m_w_out[0], v_w_out[0], p_wo, r_wo, chip, "adamw_w_out", after=adam_up[0])
    adam_fcw = _adamw(taps(ffn_conv_w), taps(m_ffn_conv_w), taps(v_ffn_conv_w), p_fcw, r_fcw, chip, "adamw_ffn_conv_w",
                      after=adam["w_out"][0])
    _, (r_small,) = _split_wait("gather_small_wait", s_sem, r_sem, src_s, land_s, plan_s, slots_s, adam_fcw[0])
    small_in = {k: (p[k], m[k], v[k]) for k in _SMALL_NAMES}
    small_in["rel_bias_table"] = (tbl_t, m_rel_bias_table.T, v_rel_bias_table.T)
    small_out, loss_row = _adamw_small(r_small, small_in, None)
    (p_wi, p_cw), (r_wi, r_cw) = finish_mixer(loss_row)
    adam_in = _adamw(tr(w_in), tr(m_w_in), tr(v_w_in), p_wi, r_wi, chip, "adamw_w_in")
    adam_cw = _adamw(taps(conv_w), taps(m_conv_w), taps(v_conv_w), p_cw, r_cw, chip, "adamw_conv_w")

    res = {k: tuple(a[None] for a in t) for k, t in adam.items()}
    res["w_up"] = tuple(a.T[None] for a in adam_up)
    res["w_in"] = tuple(a.T[None] for a in adam_in)
    res["ffn_conv_w"] = tuple(taps(a) for a in adam_fcw)
    res["conv_w"] = tuple(taps(a) for a in adam_cw)
    res.update(small_out)
    res["rel_bias_table"] = tuple(a.T for a in small_out["rel_bias_table"])
    loss = loss_row[0, 0]
    order = ("norm_mix_g", "w_in", "conv_w", "q_norm_g", "k_norm_g", "rel_bias_table", "sinks", "out_norm_conv_g",
             "out_norm_attn_g", "w_out", "norm_ffn_g", "w_up", "ffn_conv_w", "ffn_conv_b", "w_down")
    return (loss, dx[None], *[res[k][0] for k in order], *[res[k][1] for k in order],
            *[res[k][2] for k in order], *[res[k][3] for k in order])
```
